```python
import math
import jax, jax.numpy as jnp
from jax import lax
import numpy as np

D_MODEL = 1024
BATCH = 8
SEQ = 4096
DEPTH = 1

N_Q_HEADS = 8
N_KV_HEADS = 2
HEAD_DIM = 64
Q_GROUP = N_Q_HEADS // N_KV_HEADS
ATTN_WIDTH = N_Q_HEADS * HEAD_DIM
KV_WIDTH = N_KV_HEADS * HEAD_DIM
WINDOW = 128
BLOCK = 128
N_BUCKETS = 32
MAX_DISTANCE = 128
NEG_INF = -1e30
SSM_WIDTH = D_MODEL // 2
SSM_GROUP = 16
SSM_GROUPS = SSM_WIDTH // SSM_GROUP
SSM_STATE = 64
DT_MIN = 1e-3
DT_MAX = 1e-1
N_BRANCHES = 2
D_FF = 4 * D_MODEL
IN_WIDTH = ATTN_WIDTH + 2 * KV_WIDTH + SSM_WIDTH + N_BRANCHES * D_MODEL
SPLITS = (ATTN_WIDTH, ATTN_WIDTH + KV_WIDTH, ATTN_WIDTH + 2 * KV_WIDTH,
          ATTN_WIDTH + 2 * KV_WIDTH + SSM_WIDTH, ATTN_WIDTH + 2 * KV_WIDTH + SSM_WIDTH + D_MODEL)
RMS_EPS = 1e-6

kernel_name = "hybrid_swa_sink_s5_gated_block"


def rmsnorm(x, g):
    xf = x.astype(jnp.float32)
    y = xf * lax.rsqrt(jnp.mean(xf * xf, axis=-1, keepdims=True) + RMS_EPS)
    return (y * g.astype(jnp.float32)).astype(x.dtype)


def t5_causal_bucket(dist):
    max_exact = N_BUCKETS // 2
    d = jnp.maximum(dist, 0)
    df = jnp.maximum(d, 1).astype(jnp.float32)
    large = max_exact + (jnp.log(df / max_exact) / math.log(MAX_DISTANCE / max_exact)
                         * (N_BUCKETS - max_exact)).astype(jnp.int32)
    large = jnp.minimum(large, N_BUCKETS - 1)
    return jnp.where(d < max_exact, d, large)


def sliding_window_attention(q, k, v, sinks, rel_bias):
    B, L = q.shape[0], q.shape[1]
    nb = L // BLOCK
    qb = q.reshape(B, nb, BLOCK, N_KV_HEADS, Q_GROUP, HEAD_DIM)

    def band(t):
        tb = t.reshape(B, nb, BLOCK, N_KV_HEADS, HEAD_DIM)
        prev = jnp.pad(tb, ((0, 0), (1, 0), (0, 0), (0, 0), (0, 0)))[:, :-1]
        return jnp.concatenate([prev, tb], axis=2)

    kb, vb = band(k), band(v)
    logits = jnp.einsum('bnqkgd,bnskd->bnkgqs', qb, kb,
                        preferred_element_type=jnp.float32) * (HEAD_DIM ** -0.5)
    qi = jnp.arange(BLOCK)[:, None]
    kj = jnp.arange(2 * BLOCK)[None, :]
    dist = qi + BLOCK - kj
    band_ok = (dist >= 0) & (dist < WINDOW)
    key_pos = jnp.arange(nb)[:, None] * BLOCK - BLOCK + jnp.arange(2 * BLOCK)[None, :]
    mask = band_ok[None] & (key_pos >= 0)[:, None, :]
    bias = rel_bias.astype(jnp.float32)[t5_causal_bucket(dist)]
    bias = jnp.transpose(bias, (2, 0, 1)).reshape(N_KV_HEADS, Q_GROUP, BLOCK, 2 * BLOCK)
    logits = jnp.where(mask[None, :, None, None], logits + bias[None, None], NEG_INF)
    s = sinks.astype(jnp.float32).reshape(1, 1, N_KV_HEADS, Q_GROUP, 1, 1)
    m = jnp.maximum(jnp.max(logits, axis=-1, keepdims=True), s)
    p = jnp.exp(logits - m)
    denom = jnp.sum(p, axis=-1, keepdims=True) + jnp.exp(s - m)
    probs = (p / denom).astype(v.dtype)
    out = jnp.einsum('bnkgqs,bnskd->bnqkgd', probs, vb)
    return out.reshape(B, L, ATTN_WIDTH)


def s5_ssm(u, lam_re, lam_im, log_dt, b_re, b_im, c_re, c_im, d_skip):
    B, L = u.shape[0], u.shape[1]
    uf = u.astype(jnp.float32).reshape(B, L, SSM_GROUPS, SSM_GROUP)
    dt = jnp.exp(log_dt.astype(jnp.float32))[:, None]
    lr = lam_re.astype(jnp.float32)
    li = lam_im.astype(jnp.float32)
    mag = jnp.exp(lr * dt)
    ab_re = mag * jnp.cos(li * dt)
    ab_im = mag * jnp.sin(li * dt)
    nr = ab_re - 1.0
    den = lr * lr + li * li
    f_re = (nr * lr + ab_im * li) / den
    f_im = (ab_im * lr - nr * li) / den
    br = b_re.astype(jnp.float32)
    bi = b_im.astype(jnp.float32)
    bb_re = f_re[..., None] * br - f_im[..., None] * bi
    bb_im = f_re[..., None] * bi + f_im[..., None] * br
    bu_re = jnp.einsum('blgc,gpc->blgp', uf, bb_re)
    bu_im = jnp.einsum('blgc,gpc->blgp', uf, bb_im)
    a_re = jnp.broadcast_to(ab_re, bu_re.shape)
    a_im = jnp.broadcast_to(ab_im, bu_im.shape)

    def combine(e1, e2):
        a1r, a1i, b1r, b1i = e1
        a2r, a2i, b2r, b2i = e2
        return (a2r * a1r - a2i * a1i,
                a2r * a1i + a2i * a1r,
                a2r * b1r - a2i * b1i + b2r,
                a2r * b1i + a2i * b1r + b2i)

    _, _, h_re, h_im = lax.associative_scan(combine, (a_re, a_im, bu_re, bu_im), axis=1)
    y = (jnp.einsum('blgp,gcp->blgc', h_re, c_re.astype(jnp.float32))
         - jnp.einsum('blgp,gcp->blgc', h_im, c_im.astype(jnp.float32)))
    y = y + d_skip.astype(jnp.float32).reshape(SSM_GROUPS, SSM_GROUP) * uf
    return y.reshape(B, L, SSM_WIDTH).astype(u.dtype)


def _fwd_setup_inputs(seed: int = 0) -> dict:
    key = jax.random.key(seed)
    ks = jax.random.split(key, 24)
    f32 = jnp.float32
    nrm = lambda k, shape, scale: jax.random.normal(k, shape, f32) * scale
    n_idx = jnp.arange(SSM_STATE, dtype=f32)
    return {
        "x": jax.random.normal(ks[0], (BATCH, SEQ, D_MODEL), f32),
        "norm_mix_pre": 1.0 + nrm(ks[1], (DEPTH, D_MODEL), 0.05),
        "norm_mix_post": 1.0 + nrm(ks[2], (DEPTH, D_MODEL), 0.05),
        "norm_mlp_pre": 1.0 + nrm(ks[3], (DEPTH, D_MODEL), 0.05),
        "norm_mlp_post": 1.0 + nrm(ks[4], (DEPTH, D_MODEL), 0.05),
        "w_in": nrm(ks[5], (DEPTH, D_MODEL, IN_WIDTH), D_MODEL ** -0.5),
        "rel_bias": nrm(ks[6], (N_BUCKETS, N_Q_HEADS), 0.5),
        "sinks": nrm(ks[7], (DEPTH, N_Q_HEADS), 0.5),
        "lam_re": -0.5 + nrm(ks[8], (DEPTH, SSM_GROUPS, SSM_STATE), 0.02),
        "lam_im": math.pi * n_idx + nrm(ks[9], (DEPTH, SSM_GROUPS, SSM_STATE), 0.02),
        "log_dt": jax.random.uniform(ks[10], (DEPTH, SSM_GROUPS), f32,
                                     math.log(DT_MIN), math.log(DT_MAX)),
        "b_re": nrm(ks[11], (DEPTH, SSM_GROUPS, SSM_STATE, SSM_GROUP), SSM_GROUP ** -0.5),
        "b_im": nrm(ks[12], (DEPTH, SSM_GROUPS, SSM_STATE, SSM_GROUP), SSM_GROUP ** -0.5),
        "c_re": nrm(ks[13], (DEPTH, SSM_GROUPS, SSM_GROUP, SSM_STATE), SSM_STATE ** -0.5),
        "c_im": nrm(ks[14], (DEPTH, SSM_GROUPS, SSM_GROUP, SSM_STATE), SSM_STATE ** -0.5),
        "d_skip": nrm(ks[15], (DEPTH, SSM_WIDTH), 1.0),
        "w_glu": nrm(ks[16], (DEPTH, SSM_WIDTH, SSM_WIDTH), SSM_WIDTH ** -0.5),
        "w_attn_branch": nrm(ks[17], (DEPTH, ATTN_WIDTH, D_MODEL), ATTN_WIDTH ** -0.5),
        "w_ssm_branch": nrm(ks[18], (DEPTH, SSM_WIDTH, D_MODEL), SSM_WIDTH ** -0.5),
        "w_out": nrm(ks[19], (DEPTH, D_MODEL, D_MODEL), D_MODEL ** -0.5),
        "w_ff_in": nrm(ks[20], (DEPTH, D_MODEL, D_FF), D_MODEL ** -0.5),
        "w_ff_out": nrm(ks[21], (DEPTH, D_FF, D_MODEL), D_FF ** -0.5),
    }


def _fwd_reference(x, norm_mix_pre, norm_mix_post, norm_mlp_pre, norm_mlp_post, w_in, rel_bias,
              sinks, lam_re, lam_im, log_dt, b_re, b_im, c_re, c_im, d_skip, w_glu,
              w_attn_branch, w_ssm_branch, w_out, w_ff_in, w_ff_out):
    B, L = x.shape[0], x.shape[1]
    for l in range(DEPTH):
        h = rmsnorm(x, norm_mix_pre[l])
        proj = h @ w_in[l]
        q, k, v, u, g_attn, g_ssm = jnp.split(proj, SPLITS, axis=-1)
        q = q.reshape(B, L, N_Q_HEADS, HEAD_DIM)
        k = k.reshape(B, L, N_KV_HEADS, HEAD_DIM)
        v = v.reshape(B, L, N_KV_HEADS, HEAD_DIM)
        y_attn = sliding_window_attention(q, k, v, sinks[l], rel_bias) @ w_attn_branch[l]
        z = jax.nn.gelu(s5_ssm(u, lam_re[l], lam_im[l], log_dt[l], b_re[l], b_im[l],
                               c_re[l], c_im[l], d_skip[l]))
        z = z * jax.nn.sigmoid(z @ w_glu[l])
        y_ssm = z @ w_ssm_branch[l]
        merged = jax.nn.sigmoid(g_attn) * y_attn + jax.nn.sigmoid(g_ssm) * y_ssm
        x = x + rmsnorm(merged @ w_out[l], norm_mix_post[l])
        h = rmsnorm(x, norm_mlp_pre[l])
        f = jnp.square(jax.nn.relu(h @ w_ff_in[l])) @ w_ff_out[l]
        x = x + rmsnorm(f, norm_mlp_post[l])
    return x


import jax as _jax
import jax.numpy as _jnp

TWIN_FORMAT = 'train_step'
FWD_PARAMS = ['x', 'norm_mix_pre', 'norm_mix_post', 'norm_mlp_pre', 'norm_mlp_post', 'w_in', 'rel_bias', 'sinks', 'lam_re', 'lam_im', 'log_dt', 'b_re', 'b_im', 'c_re', 'c_im', 'd_skip', 'w_glu', 'w_attn_branch', 'w_ssm_branch', 'w_out', 'w_ff_in', 'w_ff_out']
TWIN_WEIGHTS = ['norm_mix_pre', 'norm_mix_post', 'norm_mlp_pre', 'norm_mlp_post', 'w_in', 'rel_bias', 'sinks', 'lam_re', 'lam_im', 'log_dt', 'b_re', 'b_im', 'c_re', 'c_im', 'd_skip', 'w_glu', 'w_attn_branch', 'w_ssm_branch', 'w_out', 'w_ff_in', 'w_ff_out']
TWIN_DIFF_INPUT = 'x'
TWIN_INPUTS = ['x', 'norm_mix_pre', 'norm_mix_post', 'norm_mlp_pre', 'norm_mlp_post', 'w_in', 'rel_bias', 'sinks', 'lam_re', 'lam_im', 'log_dt', 'b_re', 'b_im', 'c_re', 'c_im', 'd_skip', 'w_glu', 'w_attn_branch', 'w_ssm_branch', 'w_out', 'w_ff_in', 'w_ff_out', 'loss_target', 'm_norm_mix_pre', 'm_norm_mix_post', 'm_norm_mlp_pre', 'm_norm_mlp_post', 'm_w_in', 'm_rel_bias', 'm_sinks', 'm_lam_re', 'm_lam_im', 'm_log_dt', 'm_b_re', 'm_b_im', 'm_c_re', 'm_c_im', 'm_d_skip', 'm_w_glu', 'm_w_attn_branch', 'm_w_ssm_branch', 'm_w_out', 'm_w_ff_in', 'm_w_ff_out', 'v_norm_mix_pre', 'v_norm_mix_post', 'v_norm_mlp_pre', 'v_norm_mlp_post', 'v_w_in', 'v_rel_bias', 'v_sinks', 'v_lam_re', 'v_lam_im', 'v_log_dt', 'v_b_re', 'v_b_im', 'v_c_re', 'v_c_im', 'v_d_skip', 'v_w_glu', 'v_w_attn_branch', 'v_w_ssm_branch', 'v_w_out', 'v_w_ff_in', 'v_w_ff_out']
TWIN_OUTPUTS = ['loss', 'grad_x', 'grad_norm_mix_pre', 'grad_norm_mix_post', 'grad_norm_mlp_pre', 'grad_norm_mlp_post', 'grad_w_in', 'grad_rel_bias', 'grad_sinks', 'grad_lam_re', 'grad_lam_im', 'grad_log_dt', 'grad_b_re', 'grad_b_im', 'grad_c_re', 'grad_c_im', 'grad_d_skip', 'grad_w_glu', 'grad_w_attn_branch', 'grad_w_ssm_branch', 'grad_w_out', 'grad_w_ff_in', 'grad_w_ff_out', 'delta_norm_mix_pre', 'delta_norm_mix_post', 'delta_norm_mlp_pre', 'delta_norm_mlp_post', 'delta_w_in', 'delta_rel_bias', 'delta_sinks', 'delta_lam_re', 'delta_lam_im', 'delta_log_dt', 'delta_b_re', 'delta_b_im', 'delta_c_re', 'delta_c_im', 'delta_d_skip', 'delta_w_glu', 'delta_w_attn_branch', 'delta_w_ssm_branch', 'delta_w_out', 'delta_w_ff_in', 'delta_w_ff_out', 'new_m_norm_mix_pre', 'new_m_norm_mix_post', 'new_m_norm_mlp_pre', 'new_m_norm_mlp_post', 'new_m_w_in', 'new_m_rel_bias', 'new_m_sinks', 'new_m_lam_re', 'new_m_lam_im', 'new_m_log_dt', 'new_m_b_re', 'new_m_b_im', 'new_m_c_re', 'new_m_c_im', 'new_m_d_skip', 'new_m_w_glu', 'new_m_w_attn_branch', 'new_m_w_ssm_branch', 'new_m_w_out', 'new_m_w_ff_in', 'new_m_w_ff_out', 'new_v_norm_mix_pre', 'new_v_norm_mix_post', 'new_v_norm_mlp_pre', 'new_v_norm_mlp_post', 'new_v_w_in', 'new_v_rel_bias', 'new_v_sinks', 'new_v_lam_re', 'new_v_lam_im', 'new_v_log_dt', 'new_v_b_re', 'new_v_b_im', 'new_v_c_re', 'new_v_c_im', 'new_v_d_skip', 'new_v_w_glu', 'new_v_w_attn_branch', 'new_v_w_ssm_branch', 'new_v_w_out', 'new_v_w_ff_in', 'new_v_w_ff_out']
TWIN_LEAF_KINDS = {'loss': 'loss', 'grad_x': 'grad_x', 'grad_norm_mix_pre': 'grad_w', 'grad_norm_mix_post': 'grad_w', 'grad_norm_mlp_pre': 'grad_w', 'grad_norm_mlp_post': 'grad_w', 'grad_w_in': 'grad_w', 'grad_rel_bias': 'grad_w', 'grad_sinks': 'grad_w', 'grad_lam_re': 'grad_w', 'grad_lam_im': 'grad_w', 'grad_log_dt': 'grad_w', 'grad_b_re': 'grad_w', 'grad_b_im': 'grad_w', 'grad_c_re': 'grad_w', 'grad_c_im': 'grad_w', 'grad_d_skip': 'grad_w', 'grad_w_glu': 'grad_w', 'grad_w_attn_branch': 'grad_w', 'grad_w_ssm_branch': 'grad_w', 'grad_w_out': 'grad_w', 'grad_w_ff_in': 'grad_w', 'grad_w_ff_out': 'grad_w', 'delta_norm_mix_pre': 'delta_w', 'delta_norm_mix_post': 'delta_w', 'delta_norm_mlp_pre': 'delta_w', 'delta_norm_mlp_post': 'delta_w', 'delta_w_in': 'delta_w', 'delta_rel_bias': 'delta_w', 'delta_sinks': 'delta_w', 'delta_lam_re': 'delta_w', 'delta_lam_im': 'delta_w', 'delta_log_dt': 'delta_w', 'delta_b_re': 'delta_w', 'delta_b_im': 'delta_w', 'delta_c_re': 'delta_w', 'delta_c_im': 'delta_w', 'delta_d_skip': 'delta_w', 'delta_w_glu': 'delta_w', 'delta_w_attn_branch': 'delta_w', 'delta_w_ssm_branch': 'delta_w', 'delta_w_out': 'delta_w', 'delta_w_ff_in': 'delta_w', 'delta_w_ff_out': 'delta_w', 'new_m_norm_mix_pre': 'new_m', 'new_m_norm_mix_post': 'new_m', 'new_m_norm_mlp_pre': 'new_m', 'new_m_norm_mlp_post': 'new_m', 'new_m_w_in': 'new_m', 'new_m_rel_bias': 'new_m', 'new_m_sinks': 'new_m', 'new_m_lam_re': 'new_m', 'new_m_lam_im': 'new_m', 'new_m_log_dt': 'new_m', 'new_m_b_re': 'new_m', 'new_m_b_im': 'new_m', 'new_m_c_re': 'new_m', 'new_m_c_im': 'new_m', 'new_m_d_skip': 'new_m', 'new_m_w_glu': 'new_m', 'new_m_w_attn_branch': 'new_m', 'new_m_w_ssm_branch': 'new_m', 'new_m_w_out': 'new_m', 'new_m_w_ff_in': 'new_m', 'new_m_w_ff_out': 'new_m', 'new_v_norm_mix_pre': 'new_v', 'new_v_norm_mix_post': 'new_v', 'new_v_norm_mlp_pre': 'new_v', 'new_v_norm_mlp_post': 'new_v', 'new_v_w_in': 'new_v', 'new_v_rel_bias': 'new_v', 'new_v_sinks': 'new_v', 'new_v_lam_re': 'new_v', 'new_v_lam_im': 'new_v', 'new_v_log_dt': 'new_v', 'new_v_b_re': 'new_v', 'new_v_b_im': 'new_v', 'new_v_c_re': 'new_v', 'new_v_c_im': 'new_v', 'new_v_d_skip': 'new_v', 'new_v_w_glu': 'new_v', 'new_v_w_attn_branch': 'new_v', 'new_v_w_ssm_branch': 'new_v', 'new_v_w_out': 'new_v', 'new_v_w_ff_in': 'new_v', 'new_v_w_ff_out': 'new_v'}


def _forward(args):
    return _fwd_reference(*[args[k] for k in FWD_PARAMS])


def _output_shape():
    def fwd():
        inp = _fwd_setup_inputs(0)
        return _fwd_reference(*[inp[k] for k in FWD_PARAMS])
    out = _jax.eval_shape(fwd)
    return out.shape, out.dtype

N_MICROBATCH = 1
ADAM_LR = 0.001
ADAM_B1 = 0.9
ADAM_B2 = 0.999
ADAM_EPS = 1e-08
ADAM_WD = 0.01
ADAM_STEP = 10
PER_EXAMPLE_BATCH_AXIS = {'x': 0, 'loss_target': 0}
SHARED_INPUTS = []
_WEIGHT_DTYPES = {'norm_mix_pre': _jnp.float32, 'norm_mix_post': _jnp.float32, 'norm_mlp_pre': _jnp.float32, 'norm_mlp_post': _jnp.float32, 'w_in': _jnp.float32, 'rel_bias': _jnp.float32, 'sinks': _jnp.float32, 'lam_re': _jnp.float32, 'lam_im': _jnp.float32, 'log_dt': _jnp.float32, 'b_re': _jnp.float32, 'b_im': _jnp.float32, 'c_re': _jnp.float32, 'c_im': _jnp.float32, 'd_skip': _jnp.float32, 'w_glu': _jnp.float32, 'w_attn_branch': _jnp.float32, 'w_ssm_branch': _jnp.float32, 'w_out': _jnp.float32, 'w_ff_in': _jnp.float32, 'w_ff_out': _jnp.float32}
MOMENT_SCALE = {'norm_mix_pre': 8.839586e-01, 'norm_mix_post': 3.333066e+01, 'norm_mlp_pre': 2.461069e+00, 'norm_mlp_post': 3.345840e+01, 'w_in': 4.601333e-01, 'rel_bias': 3.278973e-01, 'sinks': 1.031311e-01, 'lam_re': 5.776085e-02, 'lam_im': 6.596472e-02, 'log_dt': 3.723461e+01, 'b_re': 3.175326e-02, 'b_im': 3.229392e-02, 'c_re': 7.440954e-02, 'c_im': 6.514676e-02, 'd_skip': 9.134133e+00, 'w_glu': 1.344013e+00, 'w_attn_branch': 3.299959e-01, 'w_ssm_branch': 6.155624e+00, 'w_out': 6.209109e+00, 'w_ff_in': 1.284637e+00, 'w_ff_out': 5.464358e+00}


def _to_microbatches(a, axis):
    t = _jnp.moveaxis(a, axis, 0)
    t = t.reshape((N_MICROBATCH, t.shape[0] // N_MICROBATCH) + t.shape[1:])
    return _jnp.moveaxis(t, 1, axis + 1)


def setup_inputs(seed: int = 0) -> dict:
    inp = _fwd_setup_inputs(seed)
    key = _jax.random.fold_in(_jax.random.key(seed), 7919)
    shape, _ = _output_shape()
    out = dict(inp)
    out["loss_target"] = _jax.random.normal(_jax.random.fold_in(key, 0), shape, _jnp.float32)
    for i, name in enumerate(TWIN_WEIGHTS):
        w = inp[name].astype(_jnp.float32)
        if MOMENT_SCALE is None:
            s = _jnp.sqrt(_jnp.mean(_jnp.square(w)) + 1e-30)
        else:
            s = MOMENT_SCALE[name]
        km, kv = _jax.random.split(_jax.random.fold_in(key, i + 1))
        out[name] = w
        out["m_" + name] = s * _jax.random.normal(km, w.shape, _jnp.float32)
        out["v_" + name] = (s * s) * _jax.random.uniform(kv, w.shape, _jnp.float32, 0.5, 1.5)
    if N_MICROBATCH > 1:
        for name, axis in PER_EXAMPLE_BATCH_AXIS.items():
            out[name] = _to_microbatches(out[name], axis)
    return {'x': out['x'], 'norm_mix_pre': out['norm_mix_pre'], 'norm_mix_post': out['norm_mix_post'], 'norm_mlp_pre': out['norm_mlp_pre'], 'norm_mlp_post': out['norm_mlp_post'], 'w_in': out['w_in'], 'rel_bias': out['rel_bias'], 'sinks': out['sinks'], 'lam_re': out['lam_re'], 'lam_im': out['lam_im'], 'log_dt': out['log_dt'], 'b_re': out['b_re'], 'b_im': out['b_im'], 'c_re': out['c_re'], 'c_im': out['c_im'], 'd_skip': out['d_skip'], 'w_glu': out['w_glu'], 'w_attn_branch': out['w_attn_branch'], 'w_ssm_branch': out['w_ssm_branch'], 'w_out': out['w_out'], 'w_ff_in': out['w_ff_in'], 'w_ff_out': out['w_ff_out'], 'loss_target': out['loss_target'], 'm_norm_mix_pre': out['m_norm_mix_pre'], 'm_norm_mix_post': out['m_norm_mix_post'], 'm_norm_mlp_pre': out['m_norm_mlp_pre'], 'm_norm_mlp_post': out['m_norm_mlp_post'], 'm_w_in': out['m_w_in'], 'm_rel_bias': out['m_rel_bias'], 'm_sinks': out['m_sinks'], 'm_lam_re': out['m_lam_re'], 'm_lam_im': out['m_lam_im'], 'm_log_dt': out['m_log_dt'], 'm_b_re': out['m_b_re'], 'm_b_im': out['m_b_im'], 'm_c_re': out['m_c_re'], 'm_c_im': out['m_c_im'], 'm_d_skip': out['m_d_skip'], 'm_w_glu': out['m_w_glu'], 'm_w_attn_branch': out['m_w_attn_branch'], 'm_w_ssm_branch': out['m_w_ssm_branch'], 'm_w_out': out['m_w_out'], 'm_w_ff_in': out['m_w_ff_in'], 'm_w_ff_out': out['m_w_ff_out'], 'v_norm_mix_pre': out['v_norm_mix_pre'], 'v_norm_mix_post': out['v_norm_mix_post'], 'v_norm_mlp_pre': out['v_norm_mlp_pre'], 'v_norm_mlp_post': out['v_norm_mlp_post'], 'v_w_in': out['v_w_in'], 'v_rel_bias': out['v_rel_bias'], 'v_sinks': out['v_sinks'], 'v_lam_re': out['v_lam_re'], 'v_lam_im': out['v_lam_im'], 'v_log_dt': out['v_log_dt'], 'v_b_re': out['v_b_re'], 'v_b_im': out['v_b_im'], 'v_c_re': out['v_c_re'], 'v_c_im': out['v_c_im'], 'v_d_skip': out['v_d_skip'], 'v_w_glu': out['v_w_glu'], 'v_w_attn_branch': out['v_w_attn_branch'], 'v_w_ssm_branch': out['v_w_ssm_branch'], 'v_w_out': out['v_w_out'], 'v_w_ff_in': out['v_w_ff_in'], 'v_w_ff_out': out['v_w_ff_out']}


def _loss(weights, diff, rest, loss_target):
    with _jax.named_scope("forward"):
        args = {**rest, TWIN_DIFF_INPUT: diff, **{k: w.astype(_WEIGHT_DTYPES[k]) for k, w in weights.items()}}
        y = _forward(args)
    with _jax.named_scope("loss_head"):
        err = _jnp.square(y.astype(_jnp.float32) - loss_target)
        return 0.5 * _jnp.sum(_jnp.mean(err, axis=-1)) if err.ndim else 0.5 * err


def _adamw(w, g, m, v):
    m = ADAM_B1 * m + (1.0 - ADAM_B1) * g
    v = ADAM_B2 * v + (1.0 - ADAM_B2) * _jnp.square(g)
    m_hat = m / (1.0 - ADAM_B1 ** ADAM_STEP)
    v_hat = v / (1.0 - ADAM_B2 ** ADAM_STEP)
    delta = -ADAM_LR * (m_hat / (_jnp.sqrt(v_hat) + ADAM_EPS) + ADAM_WD * w)
    return delta, m, v


def reference(x, norm_mix_pre, norm_mix_post, norm_mlp_pre, norm_mlp_post, w_in, rel_bias, sinks, lam_re, lam_im, log_dt, b_re, b_im, c_re, c_im, d_skip, w_glu, w_attn_branch, w_ssm_branch, w_out, w_ff_in, w_ff_out, loss_target, m_norm_mix_pre, m_norm_mix_post, m_norm_mlp_pre, m_norm_mlp_post, m_w_in, m_rel_bias, m_sinks, m_lam_re, m_lam_im, m_log_dt, m_b_re, m_b_im, m_c_re, m_c_im, m_d_skip, m_w_glu, m_w_attn_branch, m_w_ssm_branch, m_w_out, m_w_ff_in, m_w_ff_out, v_norm_mix_pre, v_norm_mix_post, v_norm_mlp_pre, v_norm_mlp_post, v_w_in, v_rel_bias, v_sinks, v_lam_re, v_lam_im, v_log_dt, v_b_re, v_b_im, v_c_re, v_c_im, v_d_skip, v_w_glu, v_w_attn_branch, v_w_ssm_branch, v_w_out, v_w_ff_in, v_w_ff_out):
    given = dict(x=x, norm_mix_pre=norm_mix_pre, norm_mix_post=norm_mix_post, norm_mlp_pre=norm_mlp_pre, norm_mlp_post=norm_mlp_post, w_in=w_in, rel_bias=rel_bias, sinks=sinks, lam_re=lam_re, lam_im=lam_im, log_dt=log_dt, b_re=b_re, b_im=b_im, c_re=c_re, c_im=c_im, d_skip=d_skip, w_glu=w_glu, w_attn_branch=w_attn_branch, w_ssm_branch=w_ssm_branch, w_out=w_out, w_ff_in=w_ff_in, w_ff_out=w_ff_out, loss_target=loss_target, m_norm_mix_pre=m_norm_mix_pre, m_norm_mix_post=m_norm_mix_post, m_norm_mlp_pre=m_norm_mlp_pre, m_norm_mlp_post=m_norm_mlp_post, m_w_in=m_w_in, m_rel_bias=m_rel_bias, m_sinks=m_sinks, m_lam_re=m_lam_re, m_lam_im=m_lam_im, m_log_dt=m_log_dt, m_b_re=m_b_re, m_b_im=m_b_im, m_c_re=m_c_re, m_c_im=m_c_im, m_d_skip=m_d_skip, m_w_glu=m_w_glu, m_w_attn_branch=m_w_attn_branch, m_w_ssm_branch=m_w_ssm_branch, m_w_out=m_w_out, m_w_ff_in=m_w_ff_in, m_w_ff_out=m_w_ff_out, v_norm_mix_pre=v_norm_mix_pre, v_norm_mix_post=v_norm_mix_post, v_norm_mlp_pre=v_norm_mlp_pre, v_norm_mlp_post=v_norm_mlp_post, v_w_in=v_w_in, v_rel_bias=v_rel_bias, v_sinks=v_sinks, v_lam_re=v_lam_re, v_lam_im=v_lam_im, v_log_dt=v_log_dt, v_b_re=v_b_re, v_b_im=v_b_im, v_c_re=v_c_re, v_c_im=v_c_im, v_d_skip=v_d_skip, v_w_glu=v_w_glu, v_w_attn_branch=v_w_attn_branch, v_w_ssm_branch=v_w_ssm_branch, v_w_out=v_w_out, v_w_ff_in=v_w_ff_in, v_w_ff_out=v_w_ff_out)
    weights = {n: given[n] for n in TWIN_WEIGHTS}
    shared = {n: given[n] for n in SHARED_INPUTS}
    per_example = {n: given[n] for n in ['x']}
    grad_fn = _jax.value_and_grad(_loss, argnums=(0, 1))

    def one_microbatch(ex, loss_target):
        ex = dict(ex)
        diff = ex.pop(TWIN_DIFF_INPUT)
        return grad_fn(weights, diff, {**shared, **ex}, loss_target)

    if N_MICROBATCH == 1:
        loss, (grad_w, grad_x) = one_microbatch(per_example, given["loss_target"])
    else:
        def body(carry, xs):
            loss_sum, grad_sum = carry
            l_k, (gw_k, gx_k) = one_microbatch(xs[0], xs[1])
            with _jax.named_scope("update"):
                return (loss_sum + l_k, _jax.tree.map(_jnp.add, grad_sum, gw_k)), gx_k

        init = (_jnp.zeros((), _jnp.float32), _jax.tree.map(_jnp.zeros_like, weights))
        (loss, grad_w), grad_x = _jax.lax.scan(body, init, (per_example, given["loss_target"]))
    with _jax.named_scope("update"):
        delta_w, new_m, new_v = {}, {}, {}
        for n in TWIN_WEIGHTS:
            delta_w[n], new_m[n], new_v[n] = _adamw(weights[n], grad_w[n], given["m_" + n], given["v_" + n])
    return (loss, grad_x, *[grad_w[n] for n in TWIN_WEIGHTS], *[delta_w[n] for n in TWIN_WEIGHTS],
            *[new_m[n] for n in TWIN_WEIGHTS], *[new_v[n] for n in TWIN_WEIGHTS])
```

```python
import functools
import math

import jax
import jax.numpy as jnp
import numpy as np
from jax import lax
from jax.experimental import pallas as pl
from jax.experimental.pallas import tpu as pltpu

F32 = jnp.float32
BF16 = jnp.bfloat16

D_MODEL = 1024
N_HEADS = 8
HEAD_DIM = 64
ATTN_W = 512
KV_W = 128
BLOCK = 128
N_BUCKETS = 32
SSM_W = 512
N_GROUPS = 32
N_STATE = 64
GROUP_CH = 16
STATES = N_GROUPS * N_STATE
D_FF = 4096
IN_W = 3328
SPLITS = (0, 512, 640, 768, 1280, 2304, 3328)
RMS_EPS = 1e-6
NEG_INF = -1e30
SUBLANES = 8
LANES = 128
SSM_LANE_BLOCK = 512
N_SSM_BLOCKS = STATES // SSM_LANE_BLOCK
VMEM_BIG = 52 * 1024 * 1024
VMEM_MID = 40 * 1024 * 1024

ADAM_LR = 0.001
ADAM_B1 = 0.9
ADAM_B2 = 0.999
ADAM_EPS = 1e-08
ADAM_WD = 0.01
ADAM_STEP = 10

N_DEV = 8


def _dot(a, b):
    return jnp.dot(a, b, preferred_element_type=F32)


def _dot_nt(a, b):
    return lax.dot_general(a, b, (((1,), (1,)), ((), ())), preferred_element_type=F32)


def _dot_tn(a, b):
    return lax.dot_general(a, b, (((0,), (0,)), ((), ())), preferred_element_type=F32)


def _rms_scale(x):
    return lax.rsqrt(jnp.mean(x * x, axis=-1, keepdims=True) + RMS_EPS)


def _rms_bwd(dy, x, r, g):
    t = dy * g
    dx = r * t - x * (r * r * r) * jnp.mean(t * x, axis=-1, keepdims=True)
    dg = jnp.sum(dy * x * r, axis=0, keepdims=True)
    return dx, dg


def _const_spec(shape):
    nd = len(shape)
    return pl.BlockSpec(shape, lambda *_: (0,) * nd, pipeline_mode=pl.Buffered(1))


def _params(sem, vmem=None):
    return pltpu.CompilerParams(dimension_semantics=sem, vmem_limit_bytes=vmem)


def _in_proj_fwd(x, g1, w_in, tile):
    T = x.shape[0]

    def body(x_ref, g_ref, w_ref, q_ref, k_ref, v_ref, u_ref, ga_ref, gs_ref):
        xv = x_ref[...]
        h = (xv * _rms_scale(xv) * g_ref[...]).astype(BF16)
        outs = (q_ref, k_ref, v_ref, u_ref, ga_ref, gs_ref)
        for p, o_ref in enumerate(outs):
            o_ref[...] = _dot(h, w_ref[:, SPLITS[p]:SPLITS[p + 1]]).astype(o_ref.dtype)

    widths = [SPLITS[p + 1] - SPLITS[p] for p in range(6)]
    dtypes = [BF16, BF16, BF16, F32, F32, F32]
    return pl.pallas_call(
        body, name="in_proj_fwd", grid=(T // tile,),
        in_specs=[pl.BlockSpec((tile, D_MODEL), lambda i: (i, 0)), _const_spec((1, D_MODEL)), _const_spec((D_MODEL, IN_W))],
        out_specs=[pl.BlockSpec((tile, w), lambda i: (i, 0)) for w in widths],
        out_shape=[jax.ShapeDtypeStruct((T, w), dt) for w, dt in zip(widths, dtypes)],
        compiler_params=_params(("arbitrary",), VMEM_MID),
    )(x, g1, w_in)


def _in_proj_bwd(x, g1, w_in, dx1, dparts, tile):
    T = x.shape[0]
    widths = [SPLITS[p + 1] - SPLITS[p] for p in range(6)]
    n_steps = T // tile

    def body(x_ref, g_ref, w_ref, dx1_ref, dq, dk, dv, du, dga, dgs, gx_ref, dw_hbm, dg_ref, acc_ref):
        i = pl.program_id(0)
        xv = x_ref[...]
        r = _rms_scale(xv)
        g = g_ref[...]
        h = (xv * r * g).astype(BF16)
        dh = jnp.zeros((tile, D_MODEL), F32)
        for p, d_ref in enumerate((dq, dk, dv, du, dga, dgs)):
            dp = d_ref[...]
            cols = slice(SPLITS[p], SPLITS[p + 1])
            dh = dh + _dot_nt(dp, w_ref[:, cols])
            contrib = _dot_tn(h, dp)

            @pl.when(i == 0)
            def _():
                acc_ref[:, cols] = contrib

            @pl.when(i > 0)
            def _():
                acc_ref[:, cols] += contrib

        dxn, dg = _rms_bwd(dh, xv, r, g)
        gx_ref[...] = dx1_ref[...] + dxn

        @pl.when(i == 0)
        def _():
            dg_ref[...] = dg

        @pl.when(i > 0)
        def _():
            dg_ref[...] += dg

        @pl.when(i == n_steps - 1)
        def _():
            pltpu.sync_copy(acc_ref, dw_hbm)

    tok = lambda w: pl.BlockSpec((tile, w), lambda i: (i, 0))
    return pl.pallas_call(
        body, name="in_proj_bwd", grid=(n_steps,),
        in_specs=[tok(D_MODEL), _const_spec((1, D_MODEL)), _const_spec((D_MODEL, IN_W)), tok(D_MODEL)] + [tok(w) for w in widths],
        out_specs=[tok(D_MODEL), pl.BlockSpec(memory_space=pl.ANY), pl.BlockSpec((1, D_MODEL), lambda i: (0, 0))],
        out_shape=[jax.ShapeDtypeStruct((T, D_MODEL), F32), jax.ShapeDtypeStruct((D_MODEL, IN_W), F32),
                   jax.ShapeDtypeStruct((1, D_MODEL), F32)],
        scratch_shapes=[pltpu.VMEM((D_MODEL, IN_W), F32)],
        compiler_params=_params(("arbitrary",), VMEM_BIG),
    )(x, g1, w_in, dx1, *dparts)


def _bucket_table():
    qi = np.arange(BLOCK)[:, None]
    kj = np.arange(2 * BLOCK)[None, :]
    dist = qi + BLOCK - kj
    max_exact = N_BUCKETS // 2
    d = np.maximum(dist, 0)
    df = np.maximum(d, 1).astype(np.float32)
    large = max_exact + (np.log(df / np.float32(max_exact)) / np.float32(math.log(BLOCK / max_exact))
                         * np.float32(N_BUCKETS - max_exact)).astype(np.int32)
    large = np.minimum(large, N_BUCKETS - 1)
    bucket = np.where(d < max_exact, d, large)
    return np.where((dist >= 0) & (dist < BLOCK), bucket, -1).astype(np.int32)


def _build_bias(bucket_ref, rb_ref, bias_ref):
    bk = bucket_ref[...]
    for h in range(N_HEADS):
        def add(b, acc, h=h):
            return acc + jnp.where(bk == b, rb_ref[b, h], 0.0)
        bias_ref[h] = lax.fori_loop(0, N_BUCKETS, add, jnp.zeros((BLOCK, 2 * BLOCK), F32))


def _kv_variants(prev_ref, cur_ref):
    cat = jnp.concatenate([prev_ref[...], cur_ref[...]], axis=0)
    lo = lax.broadcasted_iota(jnp.int32, cat.shape, 1) < HEAD_DIM
    zero = jnp.zeros_like(cat)
    head0_lo = jnp.where(lo, cat, zero)
    head1_hi = jnp.where(lo, zero, cat)
    return ((head0_lo, pltpu.roll(head0_lo, HEAD_DIM, 1)), (pltpu.roll(head1_hi, HEAD_DIM, 1), head1_hi))


def _merge_kv_grads(g):
    lo = lax.broadcasted_iota(jnp.int32, g[0][0].shape, 1) < HEAD_DIM
    return jnp.where(lo, g[0][0] + pltpu.roll(g[0][1], HEAD_DIM, 1), g[1][1] + pltpu.roll(g[1][0], HEAD_DIM, 1))


def _attn_probs(qm, kvar, bias_h, valid, sink):
    s = _dot_nt(qm, kvar) * (HEAD_DIM ** -0.5) + bias_h
    s = jnp.where(valid, s, NEG_INF)
    m = jnp.maximum(jnp.max(s, axis=-1, keepdims=True), sink)
    p = jnp.exp(s - m)
    e_sink = jnp.exp(sink - m)
    inv = 1.0 / (jnp.sum(p, axis=-1, keepdims=True) + e_sink)
    return p * inv, e_sink * inv


def _attn_valid(bucket_ref, n):
    col = lax.broadcasted_iota(jnp.int32, (BLOCK, 2 * BLOCK), 1)
    return (bucket_ref[...] >= 0) & ((n > 0) | (col >= BLOCK))


def _attn_fwd(q, k, v, bucket, rel_bias, sinks):
    T = q.shape[0]
    nb = T // BLOCK

    def body(q_ref, kc_ref, kp_ref, vc_ref, vp_ref, bucket_ref, rb_ref, sk_ref, o_ref, bias_ref):
        n = pl.program_id(0)

        @pl.when(n == 0)
        def _():
            _build_bias(bucket_ref, rb_ref, bias_ref)

        valid = _attn_valid(bucket_ref, n)
        kvar = _kv_variants(kp_ref, kc_ref)
        vvar = _kv_variants(vp_ref, vc_ref)
        for m in range(N_HEADS // 2):
            qm = q_ref[:, m * LANES:(m + 1) * LANES]
            acc = jnp.zeros((BLOCK, LANES), F32)
            for e in range(2):
                h = 2 * m + e
                pr, _ = _attn_probs(qm, kvar[m // 2][e], bias_ref[h], valid, sk_ref[0, h])
                acc = acc + _dot(pr.astype(BF16), vvar[m // 2][e])
            o_ref[:, m * LANES:(m + 1) * LANES] = acc.astype(o_ref.dtype)

    cur = lambda w: pl.BlockSpec((BLOCK, w), lambda n: (n, 0))
    prev = lambda w: pl.BlockSpec((BLOCK, w), lambda n: (jnp.maximum(n - 1, 0), 0))
    smem = pl.BlockSpec(memory_space=pltpu.SMEM)
    return pl.pallas_call(
        body, name="attn_fwd", grid=(nb,),
        in_specs=[cur(ATTN_W), cur(KV_W), prev(KV_W), cur(KV_W), prev(KV_W), _const_spec((BLOCK, 2 * BLOCK)), smem, smem],
        out_specs=cur(ATTN_W),
        out_shape=jax.ShapeDtypeStruct((T, ATTN_W), BF16),
        scratch_shapes=[pltpu.VMEM((N_HEADS, BLOCK, 2 * BLOCK), F32)],
        compiler_params=_params(("arbitrary",)),
    )(q, k, k, v, v, bucket, rel_bias, sinks)


def _attn_bwd(q, k, v, datt, bucket, rel_bias, sinks):
    T = q.shape[0]
    nb = T // BLOCK

    def body(q_ref, do_ref, kc_ref, kp_ref, vc_ref, vp_ref, bucket_ref, rb_ref, sk_ref,
             dq_ref, dk_ref, dv_ref, small_ref, bias_ref, ds_sum_ref, dsink_ref, kcarry_ref, vcarry_ref):
        n = pl.program_id(0)

        @pl.when(n == 0)
        def _():
            _build_bias(bucket_ref, rb_ref, bias_ref)
            ds_sum_ref[...] = jnp.zeros_like(ds_sum_ref)
            dsink_ref[...] = jnp.zeros_like(dsink_ref)
            kcarry_ref[...] = jnp.zeros_like(kcarry_ref)
            vcarry_ref[...] = jnp.zeros_like(vcarry_ref)

        @pl.when(n < nb)
        def _():
            valid = _attn_valid(bucket_ref, n)
            kvar = _kv_variants(kp_ref, kc_ref)
            vvar = _kv_variants(vp_ref, vc_ref)
            zeros = lambda: [[jnp.zeros((2 * BLOCK, LANES), F32) for _ in range(2)] for _ in range(2)]
            dk_var, dv_var = zeros(), zeros()
            lane = lax.broadcasted_iota(jnp.int32, (1, LANES), 1)
            dsink = jnp.zeros((1, LANES), F32)
            for m in range(N_HEADS // 2):
                qm = q_ref[:, m * LANES:(m + 1) * LANES]
                dom = do_ref[:, m * LANES:(m + 1) * LANES]
                dqm = jnp.zeros((BLOCK, LANES), F32)
                for e in range(2):
                    h = 2 * m + e
                    kvh = m // 2
                    pr, p_sink = _attn_probs(qm, kvar[kvh][e], bias_ref[h], valid, sk_ref[0, h])
                    dp = _dot_nt(dom, vvar[kvh][e])
                    dsum = jnp.sum(pr * dp, axis=-1, keepdims=True)
                    ds = pr * (dp - dsum)
                    ds_sum_ref[h] += ds
                    dsink = dsink + jnp.where(lane == h, -jnp.sum(p_sink * dsum), 0.0)
                    dsc = (ds * (HEAD_DIM ** -0.5)).astype(BF16)
                    dqm = dqm + _dot(dsc, kvar[kvh][e])
                    dk_var[kvh][e] = dk_var[kvh][e] + _dot_tn(dsc, qm)
                    dv_var[kvh][e] = dv_var[kvh][e] + _dot_tn(pr.astype(BF16), dom)
                dq_ref[:, m * LANES:(m + 1) * LANES] = dqm.astype(dq_ref.dtype)
            dsink_ref[...] += dsink
            dk_cat = _merge_kv_grads(dk_var)
            dv_cat = _merge_kv_grads(dv_var)

            @pl.when(n > 0)
            def _():
                dk_ref[...] = (kcarry_ref[...] + dk_cat[:BLOCK]).astype(dk_ref.dtype)
                dv_ref[...] = (vcarry_ref[...] + dv_cat[:BLOCK]).astype(dv_ref.dtype)

            kcarry_ref[...] = dk_cat[BLOCK:]
            vcarry_ref[...] = dv_cat[BLOCK:]

        @pl.when(n == nb)
        def _():
            dk_ref[...] = kcarry_ref[...].astype(dk_ref.dtype)
            dv_ref[...] = vcarry_ref[...].astype(dv_ref.dtype)
            bk = bucket_ref[...]
            row = lax.broadcasted_iota(jnp.int32, (40, LANES), 0)
            col = lax.broadcasted_iota(jnp.int32, (40, LANES), 1)
            out = jnp.where(row == N_BUCKETS, jnp.broadcast_to(dsink_ref[...], (40, LANES)), 0.0)
            for h in range(N_HEADS):
                dsh = ds_sum_ref[h]

                def add(b, acc, h=h, dsh=dsh):
                    val = jnp.sum(jnp.where(bk == b, dsh, 0.0))
                    return acc + jnp.where((row == b) & (col == h), val, 0.0)
                out = lax.fori_loop(0, N_BUCKETS, add, out)
            small_ref[...] = out

    last = nb - 1
    cur = lambda w: pl.BlockSpec((BLOCK, w), lambda n: (jnp.minimum(n, last), 0))
    prev = lambda w: pl.BlockSpec((BLOCK, w), lambda n: (jnp.clip(n - 1, 0, last), 0))
    smem = pl.BlockSpec(memory_space=pltpu.SMEM)
    return pl.pallas_call(
        body, name="attn_bwd", grid=(nb + 1,),
        in_specs=[cur(ATTN_W), cur(ATTN_W), cur(KV_W), prev(KV_W), cur(KV_W), prev(KV_W),
                  _const_spec((BLOCK, 2 * BLOCK)), smem, smem],
        out_specs=[cur(ATTN_W), prev(KV_W), prev(KV_W), pl.BlockSpec((40, LANES), lambda n: (0, 0))],
        out_shape=[jax.ShapeDtypeStruct((T, ATTN_W), BF16), jax.ShapeDtypeStruct((T, KV_W), BF16),
                   jax.ShapeDtypeStruct((T, KV_W), BF16), jax.ShapeDtypeStruct((40, LANES), F32)],
        scratch_shapes=[pltpu.VMEM((N_HEADS, BLOCK, 2 * BLOCK), F32), pltpu.VMEM((N_HEADS, BLOCK, 2 * BLOCK), F32),
                        pltpu.VMEM((1, LANES), F32), pltpu.VMEM((BLOCK, KV_W), F32), pltpu.VMEM((BLOCK, KV_W), F32)],
        compiler_params=_params(("arbitrary",)),
    )(q, datt, k, k, v, v, bucket, rel_bias, sinks)


def _cmul(ar, ai, br, bi):
    return ar * br - ai * bi, ar * bi + ai * br


def _ssm_discretize(lr, li, ldt):
    dt = jnp.exp(ldt)
    mag = jnp.exp(lr * dt)
    ab_re = mag * jnp.cos(li * dt)
    ab_im = mag * jnp.sin(li * dt)
    nr = ab_re - 1.0
    den = lr * lr + li * li
    f_re = (nr * lr + ab_im * li) / den
    f_im = (ab_im * lr - nr * li) / den
    return ab_re, ab_im, f_re, f_im


def _ssm_prep(lam_re, lam_im, ldt_rep, bd_re, bd_im):
    def body(lr_ref, li_ref, ldt_ref, bdr_ref, bdi_ref, ar_ref, ai_ref, br_ref, bi_ref):
        ab_re, ab_im, f_re, f_im = _ssm_discretize(lr_ref[...], li_ref[...], ldt_ref[...])
        ar_ref[...] = ab_re
        ai_ref[...] = ab_im
        bdr, bdi = bdr_ref[0], bdi_ref[0]
        br_ref[0] = (bdr * f_re - bdi * f_im).astype(BF16)
        bi_ref[0] = (bdi * f_re + bdr * f_im).astype(BF16)

    row = pl.BlockSpec((1, SSM_LANE_BLOCK), lambda j: (0, j))
    mat = pl.BlockSpec((1, LANES, SSM_LANE_BLOCK), lambda j: (j, 0, 0))
    return pl.pallas_call(
        body, name="ssm_prep", grid=(N_SSM_BLOCKS,),
        in_specs=[row, row, row, mat, mat], out_specs=[row, row, mat, mat],
        out_shape=[jax.ShapeDtypeStruct((1, STATES), F32)] * 2 + [jax.ShapeDtypeStruct((N_SSM_BLOCKS, LANES, SSM_LANE_BLOCK), BF16)] * 2,
        compiler_params=_params(("arbitrary",)),
    )(lam_re, lam_im, ldt_rep, bd_re, bd_im)


def _ssm_prep_bwd(lam_re, lam_im, ldt_rep, bd_re, bd_im, dbr, dbi, da_re, da_im):
    def body(lr_ref, li_ref, ldt_ref, bdr_ref, bdi_ref, dbr_ref, dbi_ref, dar_ref, dai_ref,
             dbdr_ref, dbdi_ref, dlr_ref, dli_ref, dldt_ref):
        lr, li, ldt = lr_ref[...], li_ref[...], ldt_ref[...]
        (_, _, f_re, f_im), vjp = jax.vjp(_ssm_discretize, lr, li, ldt)
        bdr, bdi, gbr, gbi = bdr_ref[0], bdi_ref[0], dbr_ref[0], dbi_ref[0]
        dbdr_ref[0] = gbr * f_re + gbi * f_im
        dbdi_ref[0] = gbi * f_re - gbr * f_im
        df_re = jnp.sum(gbr * bdr + gbi * bdi, axis=0, keepdims=True)
        df_im = jnp.sum(gbi * bdr - gbr * bdi, axis=0, keepdims=True)
        dlr, dli, dldt = vjp((dar_ref[...], dai_ref[...], df_re, df_im))
        dlr_ref[...] = dlr
        dli_ref[...] = dli
        dldt_ref[...] = dldt

    row = pl.BlockSpec((1, SSM_LANE_BLOCK), lambda j: (0, j))
    mat = pl.BlockSpec((1, LANES, SSM_LANE_BLOCK), lambda j: (j, 0, 0))
    mat_shape = jax.ShapeDtypeStruct((N_SSM_BLOCKS, LANES, SSM_LANE_BLOCK), F32)
    row_shape = jax.ShapeDtypeStruct((1, STATES), F32)
    return pl.pallas_call(
        body, name="ssm_prep_bwd", grid=(N_SSM_BLOCKS,),
        in_specs=[row, row, row, mat, mat, mat, mat, row, row], out_specs=[mat, mat, row, row, row],
        out_shape=[mat_shape, mat_shape, row_shape, row_shape, row_shape],
        compiler_params=_params(("arbitrary",)),
    )(lam_re, lam_im, ldt_rep, bd_re, bd_im, dbr, dbi, da_re, da_im)


def _group_sum(x):
    def body(x_ref, o_ref):
        o_ref[...] = jnp.sum(x_ref[...], axis=1, keepdims=True)
    return pl.pallas_call(body, name="ssm_group_sum", out_shape=jax.ShapeDtypeStruct((N_GROUPS, 1), F32))(x)


def _power_table(ar, ai, p_re_ref, p_im_ref, steps):
    shape = (SUBLANES, SSM_LANE_BLOCK)
    p_re_ref[0:SUBLANES] = jnp.broadcast_to(ar, shape)
    p_im_ref[0:SUBLANES] = jnp.broadcast_to(ai, shape)
    m = 1
    while m < steps:
        rows = m * SUBLANES
        top_re = p_re_ref[rows - SUBLANES:rows]
        top_im = p_im_ref[rows - SUBLANES:rows]
        cur_re = p_re_ref[0:rows].reshape(m, SUBLANES, SSM_LANE_BLOCK)
        cur_im = p_im_ref[0:rows].reshape(m, SUBLANES, SSM_LANE_BLOCK)
        nxt_re, nxt_im = _cmul(cur_re, cur_im, top_re[None], top_im[None])
        p_re_ref[rows:2 * rows] = nxt_re.reshape(rows, SSM_LANE_BLOCK)
        p_im_ref[rows:2 * rows] = nxt_im.reshape(rows, SSM_LANE_BLOCK)
        m *= 2


def _to_segments(src_ref, dst_ref, steps):
    for s in range(SUBLANES):
        dst_ref[pl.ds(s, steps, stride=SUBLANES), :] = src_ref[s * steps:(s + 1) * steps, :]


def _from_segments(src_ref, dst_ref, steps):
    for s in range(SUBLANES):
        dst_ref[s * steps:(s + 1) * steps, :] = src_ref[pl.ds(s, steps, stride=SUBLANES), :]


def _segment_carries(e_re, e_im, an_re, an_im, c_re, c_im, reverse):
    order = range(SUBLANES - 1, -1, -1) if reverse else range(SUBLANES)
    ins_re, ins_im = [None] * SUBLANES, [None] * SUBLANES
    for s in order:
        ins_re[s], ins_im[s] = c_re, c_im
        pr, pi = _cmul(an_re, an_im, c_re, c_im)
        c_re = e_re[s:s + 1] + pr
        c_im = e_im[s:s + 1] + pi
    return jnp.concatenate(ins_re, axis=0), jnp.concatenate(ins_im, axis=0), c_re, c_im


def _ssm_fwd(u, a_re, a_im, b_re, b_im, c_re, c_im, d_skip, chunk):
    T = u.shape[0]
    nc = T // chunk
    steps = chunk // SUBLANES
    blk = SSM_LANE_BLOCK

    def body(u_ref, ar_ref, ai_ref, br_ref, bi_ref, cr_ref, ci_ref, dk_ref,
             y_ref, hr_ref, hi_ref, inr_ref, ini_ref, useg_ref, yseg_ref, pr_ref, pi_ref, carry_ref):
        c = pl.program_id(1)
        ar, ai = ar_ref[...], ai_ref[...]

        @pl.when(c == 0)
        def _():
            _power_table(ar, ai, pr_ref, pi_ref, steps)
            carry_ref[...] = jnp.zeros_like(carry_ref)

        _to_segments(u_ref, useg_ref, steps)
        ub = useg_ref[...].astype(BF16)
        hr_ref[...] = _dot(ub, br_ref[0])
        hi_ref[...] = _dot(ub, bi_ref[0])
        ar8 = jnp.broadcast_to(ar, (SUBLANES, blk))
        ai8 = jnp.broadcast_to(ai, (SUBLANES, blk))

        def scan(t, prev):
            rows = pl.ds(pl.multiple_of(t * SUBLANES, SUBLANES), SUBLANES)
            pr, pi = _cmul(ar8, ai8, prev[0], prev[1])
            nr = pr + hr_ref[rows, :]
            ni = pi + hi_ref[rows, :]
            hr_ref[rows, :] = nr
            hi_ref[rows, :] = ni
            return nr, ni

        lax.fori_loop(1, steps, scan, (hr_ref[0:SUBLANES, :], hi_ref[0:SUBLANES, :]), unroll=4)

        top = slice(chunk - SUBLANES, chunk)
        in_re, in_im, out_re, out_im = _segment_carries(
            hr_ref[top, :], hi_ref[top, :], pr_ref[top, :][0:1], pi_ref[top, :][0:1],
            carry_ref[0:1, :], carry_ref[1:2, :], reverse=False)
        carry_ref[0:1, :] = out_re
        carry_ref[1:2, :] = out_im
        inr_ref[...] = in_re
        ini_ref[...] = in_im

        def fix(t, _):
            rows = pl.ds(pl.multiple_of(t * SUBLANES, SUBLANES), SUBLANES)
            fr, fi = _cmul(pr_ref[rows, :], pi_ref[rows, :], in_re, in_im)
            hr_ref[rows, :] += fr
            hi_ref[rows, :] += fi
            return 0

        lax.fori_loop(0, steps, fix, 0, unroll=4)

        yseg_ref[...] = _dot(hr_ref[...].astype(BF16), cr_ref[0]) - _dot(hi_ref[...].astype(BF16), ci_ref[0])
        _from_segments(yseg_ref, y_ref, steps)
        y_ref[...] += dk_ref[...] * u_ref[...]

    row = pl.BlockSpec((1, blk), lambda j, c: (0, j))
    b_mat = pl.BlockSpec((1, LANES, blk), lambda j, c: (j, 0, 0))
    c_mat = pl.BlockSpec((1, blk, LANES), lambda j, c: (j, 0, 0))
    tok = pl.BlockSpec((chunk, LANES), lambda j, c: (c, j))
    state = pl.BlockSpec((chunk, blk), lambda j, c: (c, j))
    enter = pl.BlockSpec((SUBLANES, blk), lambda j, c: (c, j))
    return pl.pallas_call(
        body, name="ssm_fwd", grid=(N_SSM_BLOCKS, nc),
        in_specs=[tok, row, row, b_mat, b_mat, c_mat, c_mat, pl.BlockSpec((1, LANES), lambda j, c: (0, j))],
        out_specs=[tok, state, state, enter, enter],
        out_shape=[jax.ShapeDtypeStruct((T, SSM_W), F32), jax.ShapeDtypeStruct((T, STATES), F32),
                   jax.ShapeDtypeStruct((T, STATES), F32), jax.ShapeDtypeStruct((nc * SUBLANES, STATES), F32),
                   jax.ShapeDtypeStruct((nc * SUBLANES, STATES), F32)],
        scratch_shapes=[pltpu.VMEM((chunk, LANES), F32), pltpu.VMEM((chunk, LANES), F32),
                        pltpu.VMEM((chunk, blk), F32), pltpu.VMEM((chunk, blk), F32), pltpu.VMEM((SUBLANES, blk), F32)],
        compiler_params=_params(("arbitrary", "arbitrary"), VMEM_MID),
    )(u, a_re, a_im, b_re, b_im, c_re, c_im, d_skip)


def _ssm_bwd(dy, u, h_re, h_im, in_re, in_im, a_re, a_im, b_re, b_im, c_re, c_im, d_skip, chunk):
    T = u.shape[0]
    nc = T // chunk
    steps = chunk // SUBLANES
    blk = SSM_LANE_BLOCK

    def body(dy_ref, u_ref, hr_ref, hi_ref, inr_ref, ini_ref, ar_ref, ai_ref, br_ref, bi_ref, cr_ref, ci_ref, dk_ref,
             du_ref, dbr_ref, dbi_ref, dcr_ref, dci_ref, dar_ref, dai_ref, ddk_ref,
             dyseg_ref, useg_ref, duseg_ref, gr_ref, gi_ref, pr_ref, pi_ref, carry_ref, accr_ref, acci_ref):
        c = pl.program_id(1)
        ar, ai = ar_ref[...], ai_ref[...]

        @pl.when(c == 0)
        def _():
            _power_table(ar, ai, pr_ref, pi_ref, steps)
            carry_ref[...] = jnp.zeros_like(carry_ref)
            accr_ref[...] = jnp.zeros_like(accr_ref)
            acci_ref[...] = jnp.zeros_like(acci_ref)

        _to_segments(dy_ref, dyseg_ref, steps)
        _to_segments(u_ref, useg_ref, steps)
        dyb = dyseg_ref[...].astype(BF16)
        ub = useg_ref[...].astype(BF16)
        gr_ref[...] = _dot_nt(dyb, cr_ref[0])
        gi_ref[...] = -_dot_nt(dyb, ci_ref[0])
        dcr = _dot_tn(hr_ref[...].astype(BF16), dyb)
        dci = -_dot_tn(hi_ref[...].astype(BF16), dyb)
        ddk = jnp.sum(dy_ref[...] * u_ref[...], axis=0, keepdims=True)

        ar8 = jnp.broadcast_to(ar, (SUBLANES, blk))
        ai8 = jnp.broadcast_to(-ai, (SUBLANES, blk))

        def scan(k, nxt):
            t = steps - 2 - k
            rows = pl.ds(pl.multiple_of(t * SUBLANES, SUBLANES), SUBLANES)
            pr, pi = _cmul(ar8, ai8, nxt[0], nxt[1])
            nr = pr + gr_ref[rows, :]
            ni = pi + gi_ref[rows, :]
            gr_ref[rows, :] = nr
            gi_ref[rows, :] = ni
            return nr, ni

        top = slice(chunk - SUBLANES, chunk)
        lax.fori_loop(0, steps - 1, scan, (gr_ref[top, :], gi_ref[top, :]), unroll=4)

        gin_re, gin_im, out_re, out_im = _segment_carries(
            gr_ref[0:SUBLANES, :], gi_ref[0:SUBLANES, :], pr_ref[top, :][0:1], -pi_ref[top, :][0:1],
            carry_ref[0:1, :], carry_ref[1:2, :], reverse=True)
        carry_ref[0:1, :] = out_re
        carry_ref[1:2, :] = out_im

        def fix(t, acc):
            rows = pl.ds(pl.multiple_of(t * SUBLANES, SUBLANES), SUBLANES)
            prow = pl.ds(pl.multiple_of((steps - 1 - t) * SUBLANES, SUBLANES), SUBLANES)
            fr, fi = _cmul(pr_ref[prow, :], -pi_ref[prow, :], gin_re, gin_im)
            g_re = gr_ref[rows, :] + fr
            g_im = gi_ref[rows, :] + fi
            gr_ref[rows, :] = g_re
            gi_ref[rows, :] = g_im
            before = pl.ds(pl.multiple_of(jnp.maximum(t - 1, 0) * SUBLANES, SUBLANES), SUBLANES)
            first = t == 0
            hp_re = jnp.where(first, inr_ref[...], hr_ref[before, :])
            hp_im = jnp.where(first, ini_ref[...], hi_ref[before, :])
            return acc[0] + g_re * hp_re + g_im * hp_im, acc[1] + g_im * hp_re - g_re * hp_im

        acc_re, acc_im = lax.fori_loop(0, steps, fix, (accr_ref[...], acci_ref[...]), unroll=2)
        accr_ref[...] = acc_re
        acci_ref[...] = acc_im

        gbr = gr_ref[...].astype(BF16)
        gbi = gi_ref[...].astype(BF16)
        duseg_ref[...] = _dot_nt(gbr, br_ref[0]) + _dot_nt(gbi, bi_ref[0])
        _from_segments(duseg_ref, du_ref, steps)
        du_ref[...] += dk_ref[...] * dy_ref[...]
        dbr = _dot_tn(ub, gbr)
        dbi = _dot_tn(ub, gbi)

        @pl.when(c == 0)
        def _():
            dbr_ref[0] = dbr
            dbi_ref[0] = dbi
            dcr_ref[0] = dcr
            dci_ref[0] = dci
            ddk_ref[...] = ddk

        @pl.when(c > 0)
        def _():
            dbr_ref[0] += dbr
            dbi_ref[0] += dbi
            dcr_ref[0] += dcr
            dci_ref[0] += dci
            ddk_ref[...] += ddk

        @pl.when(c == nc - 1)
        def _():
            dar_ref[...] = jnp.sum(acc_re, axis=0, keepdims=True)
            dai_ref[...] = jnp.sum(acc_im, axis=0, keepdims=True)

    rev = lambda c: nc - 1 - c
    row = pl.BlockSpec((1, blk), lambda j, c: (0, j))
    b_mat = pl.BlockSpec((1, LANES, blk), lambda j, c: (j, 0, 0))
    c_mat = pl.BlockSpec((1, blk, LANES), lambda j, c: (j, 0, 0))
    tok = pl.BlockSpec((chunk, LANES), lambda j, c: (rev(c), j))
    state = pl.BlockSpec((chunk, blk), lambda j, c: (rev(c), j))
    enter = pl.BlockSpec((SUBLANES, blk), lambda j, c: (rev(c), j))
    chan = pl.BlockSpec((1, LANES), lambda j, c: (0, j))
    f32 = lambda *s: jax.ShapeDtypeStruct(s, F32)
    return pl.pallas_call(
        body, name="ssm_bwd", grid=(N_SSM_BLOCKS, nc),
        in_specs=[tok, tok, state, state, enter, enter, row, row, b_mat, b_mat, c_mat, c_mat, chan],
        out_specs=[tok, b_mat, b_mat, c_mat, c_mat, row, row, chan],
        out_shape=[f32(T, SSM_W), f32(N_SSM_BLOCKS, LANES, blk), f32(N_SSM_BLOCKS, LANES, blk),
                   f32(N_SSM_BLOCKS, blk, LANES), f32(N_SSM_BLOCKS, blk, LANES), f32(1, STATES), f32(1, STATES), f32(1, SSM_W)],
        scratch_shapes=[pltpu.VMEM((chunk, LANES), F32), pltpu.VMEM((chunk, LANES), F32), pltpu.VMEM((chunk, LANES), F32),
                        pltpu.VMEM((chunk, blk), F32), pltpu.VMEM((chunk, blk), F32),
                        pltpu.VMEM((chunk, blk), F32), pltpu.VMEM((chunk, blk), F32),
                        pltpu.VMEM((SUBLANES, blk), F32), pltpu.VMEM((SUBLANES, blk), F32), pltpu.VMEM((SUBLANES, blk), F32)],
        compiler_params=_params(("arbitrary", "arbitrary"), VMEM_BIG),
    )(dy, u, h_re, h_im, in_re, in_im, a_re, a_im, b_re, b_im, c_re, c_im, d_skip)


def _merge_forward(y, att, ga, gs, w_glu, w_ssm, w_attn):
    z = jax.nn.gelu(y)
    zb = z.astype(BF16)
    gl = jax.nn.sigmoid(_dot(zb, w_glu))
    z2b = (z * gl).astype(BF16)
    y_ssm = _dot(z2b, w_ssm)
    y_attn = _dot(att, w_attn)
    sa = jax.nn.sigmoid(ga)
    ss = jax.nn.sigmoid(gs)
    merged = (sa * y_attn + ss * y_ssm).astype(BF16)
    return z, zb, gl, z2b, y_ssm, y_attn, sa, ss, merged


def _merge_fwd(x, y, att, ga, gs, g2, g3, w_glu, w_ssm, w_attn, w_out, tile):
    T = x.shape[0]

    def body(x_ref, y_ref, att_ref, ga_ref, gs_ref, g2_ref, g3_ref, wg_ref, ws_ref, wa_ref, wo_ref, x1_ref, o_ref, h2_ref):
        merged = _merge_forward(y_ref[...], att_ref[...], ga_ref[...], gs_ref[...], wg_ref[...], ws_ref[...], wa_ref[...])[-1]
        o = _dot(merged, wo_ref[...])
        x1 = x_ref[...] + o * _rms_scale(o) * g2_ref[...]
        o_ref[...] = o
        x1_ref[...] = x1
        h2_ref[...] = (x1 * _rms_scale(x1) * g3_ref[...]).astype(BF16)

    tok = lambda w: pl.BlockSpec((tile, w), lambda i: (i, 0))
    vec = _const_spec((1, D_MODEL))
    return pl.pallas_call(
        body, name="merge_fwd", grid=(T // tile,),
        in_specs=[tok(D_MODEL), tok(SSM_W), tok(ATTN_W), tok(D_MODEL), tok(D_MODEL), vec, vec,
                  _const_spec((SSM_W, SSM_W)), _const_spec((SSM_W, D_MODEL)), _const_spec((ATTN_W, D_MODEL)),
                  _const_spec((D_MODEL, D_MODEL))],
        out_specs=[tok(D_MODEL), tok(D_MODEL), tok(D_MODEL)],
        out_shape=[jax.ShapeDtypeStruct((T, D_MODEL), F32), jax.ShapeDtypeStruct((T, D_MODEL), F32),
                   jax.ShapeDtypeStruct((T, D_MODEL), BF16)],
        compiler_params=_params(("arbitrary",), VMEM_MID),
    )(x, y, att, ga, gs, g2, g3, w_glu, w_ssm, w_attn, w_out)


def _merge_bwd(dh2, dx2, x1, o, y, att, ga, gs, g2, g3, w_glu, w_ssm, w_attn, w_out, tile):
    T = x1.shape[0]
    n_steps = T // tile

    def body(dh2_ref, dx2_ref, x1_ref, o_ref, y_ref, att_ref, ga_ref, gs_ref, g2_ref, g3_ref, wg_ref, ws_ref, wa_ref, wo_ref,
             dx1_ref, dga_ref, dgs_ref, datt_ref, dy_ref, dwg_hbm, dws_hbm, dwa_hbm, dwo_hbm, dg2_ref, dg3_ref,
             awg_ref, aws_ref, awa_ref, awo_ref):
        i = pl.program_id(0)
        x1v, ov = x1_ref[...], o_ref[...]
        dxn, dg3 = _rms_bwd(dh2_ref[...], x1v, _rms_scale(x1v), g3_ref[...])
        dx1 = dx2_ref[...] + dxn
        dx1_ref[...] = dx1
        do, dg2 = _rms_bwd(dx1, ov, _rms_scale(ov), g2_ref[...])
        dob = do.astype(BF16)

        yv = y_ref[...]
        att = att_ref[...]
        z, zb, gl, z2b, y_ssm, y_attn, sa, ss, merged = _merge_forward(
            yv, att, ga_ref[...], gs_ref[...], wg_ref[...], ws_ref[...], wa_ref[...])
        dmerged = _dot_nt(dob, wo_ref[...])
        dya = (dmerged * sa).astype(BF16)
        dys = (dmerged * ss).astype(BF16)
        dga_ref[...] = (dmerged * y_attn * sa * (1.0 - sa)).astype(BF16)
        dgs_ref[...] = (dmerged * y_ssm * ss * (1.0 - ss)).astype(BF16)
        datt_ref[...] = _dot_nt(dya, wa_ref[...]).astype(BF16)
        dz2 = _dot_nt(dys, ws_ref[...])
        dpre = (dz2 * z * gl * (1.0 - gl)).astype(BF16)
        dz = dz2 * gl + _dot_nt(dpre, wg_ref[...])
        _, gelu_vjp = jax.vjp(jax.nn.gelu, yv)
        dy_ref[...] = gelu_vjp(dz)[0]

        grads = ((awo_ref, _dot_tn(merged, dob)), (awa_ref, _dot_tn(att, dya)),
                 (aws_ref, _dot_tn(z2b, dys)), (awg_ref, _dot_tn(zb, dpre)), (dg2_ref, dg2), (dg3_ref, dg3))

        @pl.when(i == 0)
        def _():
            for ref, val in grads:
                ref[...] = val

        @pl.when(i > 0)
        def _():
            for ref, val in grads:
                ref[...] += val

        @pl.when(i == n_steps - 1)
        def _():
            pltpu.sync_copy(awg_ref, dwg_hbm)
            pltpu.sync_copy(aws_ref, dws_hbm)
            pltpu.sync_copy(awa_ref, dwa_hbm)
            pltpu.sync_copy(awo_ref, dwo_hbm)

    tok = lambda w: pl.BlockSpec((tile, w), lambda i: (i, 0))
    vec = _const_spec((1, D_MODEL))
    any_ = pl.BlockSpec(memory_space=pl.ANY)
    vec_out = pl.BlockSpec((1, D_MODEL), lambda i: (0, 0))
    f32 = lambda *s: jax.ShapeDtypeStruct(s, F32)
    bf = lambda *s: jax.ShapeDtypeStruct(s, BF16)
    return pl.pallas_call(
        body, name="merge_bwd", grid=(n_steps,),
        in_specs=[tok(D_MODEL), tok(D_MODEL), tok(D_MODEL), tok(D_MODEL), tok(SSM_W), tok(ATTN_W), tok(D_MODEL), tok(D_MODEL),
                  vec, vec, _const_spec((SSM_W, SSM_W)), _const_spec((SSM_W, D_MODEL)), _const_spec((ATTN_W, D_MODEL)),
                  _const_spec((D_MODEL, D_MODEL))],
        out_specs=[tok(D_MODEL), tok(D_MODEL), tok(D_MODEL), tok(ATTN_W), tok(SSM_W), any_, any_, any_, any_, vec_out, vec_out],
        out_shape=[f32(T, D_MODEL), bf(T, D_MODEL), bf(T, D_MODEL), bf(T, ATTN_W), f32(T, SSM_W),
                   f32(SSM_W, SSM_W), f32(SSM_W, D_MODEL), f32(ATTN_W, D_MODEL), f32(D_MODEL, D_MODEL),
                   f32(1, D_MODEL), f32(1, D_MODEL)],
        scratch_shapes=[pltpu.VMEM((SSM_W, SSM_W), F32), pltpu.VMEM((SSM_W, D_MODEL), F32),
                        pltpu.VMEM((ATTN_W, D_MODEL), F32), pltpu.VMEM((D_MODEL, D_MODEL), F32)],
        compiler_params=_params(("arbitrary",), VMEM_BIG),
    )(dh2, dx2, x1, o, y, att, ga, gs, g2, g3, w_glu, w_ssm, w_attn, w_out)


def _mlp_fwd(h2, x1, target, g4, w_ff_in, w_ff_out, tile, ff_chunk):
    T = h2.shape[0]
    n_i, n_k = T // tile, D_FF // ff_chunk

    def body(h2_ref, x1_ref, tg_ref, g4_ref, wi_ref, wo_ref, a_ref, dfo_ref, dx2_ref, loss_ref, dg4_ref, acc_ref):
        i, k = pl.program_id(0), pl.program_id(1)
        a = _dot(h2_ref[...], wi_ref[...])
        a_ref[...] = a.astype(BF16)
        ra = jnp.maximum(a, 0.0)
        part = _dot((ra * ra).astype(BF16), wo_ref[...])

        @pl.when(k == 0)
        def _():
            acc_ref[...] = part

        @pl.when(k > 0)
        def _():
            acc_ref[...] += part

        @pl.when(k == n_k - 1)
        def _():
            f = acc_ref[...]
            r = _rms_scale(f)
            g = g4_ref[...]
            err = x1_ref[...] + f * r * g - tg_ref[...]
            dx2 = err * (1.0 / D_MODEL)
            dx2_ref[...] = dx2
            dfo, dg = _rms_bwd(dx2, f, r, g)
            dfo_ref[...] = dfo.astype(BF16)
            row = lax.broadcasted_iota(jnp.int32, (8, LANES), 0)
            col = lax.broadcasted_iota(jnp.int32, (8, LANES), 1)
            loss = jnp.where((row == 0) & (col == 0), (0.5 / D_MODEL) * jnp.sum(err * err), 0.0)

            @pl.when(i == 0)
            def _():
                loss_ref[...] = loss
                dg4_ref[...] = dg

            @pl.when(i > 0)
            def _():
                loss_ref[...] += loss
                dg4_ref[...] += dg

    tok = pl.BlockSpec((tile, D_MODEL), lambda i, k: (i, 0))
    return pl.pallas_call(
        body, name="mlp_fwd", grid=(n_i, n_k),
        in_specs=[tok, tok, tok, pl.BlockSpec((1, D_MODEL), lambda i, k: (0, 0)),
                  pl.BlockSpec((D_MODEL, ff_chunk), lambda i, k: (0, k)), pl.BlockSpec((ff_chunk, D_MODEL), lambda i, k: (k, 0))],
        out_specs=[pl.BlockSpec((tile, ff_chunk), lambda i, k: (i, k)), tok, tok,
                   pl.BlockSpec((8, LANES), lambda i, k: (0, 0)), pl.BlockSpec((1, D_MODEL), lambda i, k: (0, 0))],
        out_shape=[jax.ShapeDtypeStruct((T, D_FF), BF16), jax.ShapeDtypeStruct((T, D_MODEL), BF16),
                   jax.ShapeDtypeStruct((T, D_MODEL), F32), jax.ShapeDtypeStruct((8, LANES), F32),
                   jax.ShapeDtypeStruct((1, D_MODEL), F32)],
        scratch_shapes=[pltpu.VMEM((tile, D_MODEL), F32)],
        compiler_params=_params(("arbitrary", "arbitrary"), VMEM_MID),
    )(h2, x1, target, g4, w_ff_in, w_ff_out)


def _mlp_bwd(dfo, a, h2, w_ff_in, w_ff_out, tile, ff_chunk):
    T = h2.shape[0]
    n_i, n_k = T // tile, D_FF // ff_chunk

    def body(dfo_ref, a_ref, h2_ref, wi_ref, wo_ref, dwi_ref, dwo_ref, dh2_ref, acc_ref):
        k, i = pl.program_id(0), pl.program_id(1)
        dfo = dfo_ref[...]
        ra = jnp.maximum(a_ref[...].astype(F32), 0.0)
        drr = _dot_nt(dfo, wo_ref[...])
        da = (drr * (2.0 * ra)).astype(BF16)
        dwo = _dot_tn((ra * ra).astype(BF16), dfo)
        dwi = _dot_tn(h2_ref[...], da)
        part = _dot_nt(da, wi_ref[...])
        rows = pl.ds(pl.multiple_of(i * tile, tile), tile)

        @pl.when(i == 0)
        def _():
            dwi_ref[...] = dwi
            dwo_ref[...] = dwo

        @pl.when(i > 0)
        def _():
            dwi_ref[...] += dwi
            dwo_ref[...] += dwo

        @pl.when(k == 0)
        def _():
            acc_ref[rows, :] = part

        @pl.when((k > 0) & (k < n_k - 1))
        def _():
            acc_ref[rows, :] += part

        @pl.when(k == n_k - 1)
        def _():
            dh2_ref[...] = acc_ref[rows, :] + part

    return pl.pallas_call(
        body, name="mlp_bwd", grid=(n_k, n_i),
        in_specs=[pl.BlockSpec((tile, D_MODEL), lambda k, i: (i, 0)), pl.BlockSpec((tile, ff_chunk), lambda k, i: (i, k)),
                  pl.BlockSpec((tile, D_MODEL), lambda k, i: (i, 0)),
                  pl.BlockSpec((D_MODEL, ff_chunk), lambda k, i: (0, k)), pl.BlockSpec((ff_chunk, D_MODEL), lambda k, i: (k, 0))],
        out_specs=[pl.BlockSpec((D_MODEL, ff_chunk), lambda k, i: (0, k)), pl.BlockSpec((ff_chunk, D_MODEL), lambda k, i: (k, 0)),
                   pl.BlockSpec((tile, D_MODEL), lambda k, i: (jnp.where(k == n_k - 1, i, 0), 0))],
        out_shape=[jax.ShapeDtypeStruct((D_MODEL, D_FF), F32), jax.ShapeDtypeStruct((D_FF, D_MODEL), F32),
                   jax.ShapeDtypeStruct((T, D_MODEL), F32)],
        scratch_shapes=[pltpu.VMEM((T, D_MODEL), F32)],
        compiler_params=_params(("arbitrary", "arbitrary"), VMEM_BIG),
    )(dfo, a, h2, w_ff_in, w_ff_out)


def _block_diag_in(b):
    bt = jnp.transpose(b, (0, 2, 1)).reshape(N_SSM_BLOCKS, 8, GROUP_CH, N_STATE)
    eye = jnp.eye(8, dtype=b.dtype)
    return jnp.einsum("jacp,ab->jacbp", bt, eye).reshape(N_SSM_BLOCKS, LANES, SSM_LANE_BLOCK)


def _block_diag_in_grad(g):
    g = g.reshape(N_SSM_BLOCKS, 8, GROUP_CH, 8, N_STATE)
    d = jnp.diagonal(g, axis1=1, axis2=3)
    return jnp.transpose(d, (0, 3, 2, 1)).reshape(N_GROUPS, N_STATE, GROUP_CH)


def _block_diag_out(c):
    ct = c.reshape(N_SSM_BLOCKS, 8, GROUP_CH, N_STATE)
    eye = jnp.eye(8, dtype=c.dtype)
    return jnp.einsum("jacp,ab->japbc", ct, eye).reshape(N_SSM_BLOCKS, SSM_LANE_BLOCK, LANES)


def _block_diag_out_grad(g):
    g = g.reshape(N_SSM_BLOCKS, 8, N_STATE, 8, GROUP_CH)
    d = jnp.diagonal(g, axis1=1, axis2=3)
    return jnp.transpose(d, (0, 3, 2, 1)).reshape(N_GROUPS, GROUP_CH, N_STATE)


def _tiles(T):
    return dict(proj=min(512, T), proj_bwd=min(256, T), merge=min(512, T), merge_bwd=min(256, T),
                mlp=min(512, T), ff_fwd=1024, ff_bwd=512, ssm_chunk=min(1024, T))


def _local_step(x, target, small, big):
    T = x.shape[0]
    t = _tiles(T)
    g1, g2, g3, g4 = (small[n].reshape(1, D_MODEL) for n in ("norm_mix_pre", "norm_mix_post", "norm_mlp_pre", "norm_mlp_post"))
    bucket = jnp.asarray(_bucket_table())
    rel_bias, sinks = small["rel_bias"], small["sinks"].reshape(1, N_HEADS)
    lam_re = small["lam_re"].reshape(1, STATES)
    lam_im = small["lam_im"].reshape(1, STATES)
    ldt_rep = jnp.repeat(small["log_dt"].reshape(N_GROUPS), N_STATE).reshape(1, STATES)
    bd_re, bd_im = _block_diag_in(small["b_re"]), _block_diag_in(small["b_im"])
    cm_re, cm_im = _block_diag_out(small["c_re"]).astype(BF16), _block_diag_out(small["c_im"]).astype(BF16)
    d_skip = small["d_skip"].reshape(1, SSM_W)

    q, k, v, u, ga, gs = _in_proj_fwd(x, g1, big["w_in"], t["proj"])
    att = _attn_fwd(q, k, v, bucket, rel_bias, sinks)
    a_re, a_im, bm_re, bm_im = _ssm_prep(lam_re, lam_im, ldt_rep, bd_re, bd_im)
    y, h_re, h_im, in_re, in_im = _ssm_fwd(u, a_re, a_im, bm_re, bm_im, cm_re, cm_im, d_skip, t["ssm_chunk"])
    x1, o, h2 = _merge_fwd(x, y, att, ga, gs, g2, g3, big["w_glu"], big["w_ssm_branch"], big["w_attn_branch"],
                           big["w_out"], t["merge"])
    a, dfo, dx2, loss_blk, dg4 = _mlp_fwd(h2, x1, target, g4, big["w_ff_in"], big["w_ff_out"], t["mlp"], t["ff_fwd"])

    dw_ff_in, dw_ff_out, dh2 = _mlp_bwd(dfo, a, h2, big["w_ff_in"], big["w_ff_out"], t["mlp"], t["ff_bwd"])
    dx1, dga, dgs, datt, dy, dw_glu, dw_ssm, dw_attn, dw_out, dg2, dg3 = _merge_bwd(
        dh2, dx2, x1, o, y, att, ga, gs, g2, g3, big["w_glu"], big["w_ssm_branch"], big["w_attn_branch"], big["w_out"],
        t["merge_bwd"])
    du, dbm_re, dbm_im, dcm_re, dcm_im, da_re, da_im, dd_skip = _ssm_bwd(
        dy, u, h_re, h_im, in_re, in_im, a_re, a_im, bm_re, bm_im, cm_re, cm_im, d_skip, t["ssm_chunk"])
    dbd_re, dbd_im, dlam_re, dlam_im, dldt_rep = _ssm_prep_bwd(lam_re, lam_im, ldt_rep, bd_re, bd_im, dbm_re, dbm_im, da_re, da_im)
    dlog_dt = _group_sum(dldt_rep.reshape(N_GROUPS, N_STATE))
    dq, dk, dv, attn_small = _attn_bwd(q, k, v, datt, bucket, rel_bias, sinks)
    grad_x, dw_in, dg1 = _in_proj_bwd(x, g1, big["w_in"], dx1, (dq, dk, dv, du.astype(BF16), dga, dgs), t["proj_bwd"])

    big_grads = dict(w_in=dw_in, w_glu=dw_glu, w_attn_branch=dw_attn, w_ssm_branch=dw_ssm, w_out=dw_out,
                     w_ff_in=dw_ff_in, w_ff_out=dw_ff_out)
    small_grads = dict(
        norm_mix_pre=dg1, norm_mix_post=dg2, norm_mlp_pre=dg3, norm_mlp_post=dg4,
        rel_bias=attn_small[:N_BUCKETS, :N_HEADS], sinks=attn_small[N_BUCKETS:N_BUCKETS + 1, :N_HEADS],
        lam_re=dlam_re.reshape(N_GROUPS, N_STATE), lam_im=dlam_im.reshape(N_GROUPS, N_STATE),
        log_dt=dlog_dt.reshape(1, N_GROUPS),
        b_re=_block_diag_in_grad(dbd_re), b_im=_block_diag_in_grad(dbd_im),
        c_re=_block_diag_out_grad(dcm_re), c_im=_block_diag_out_grad(dcm_im),
        d_skip=dd_skip)
    return loss_blk[0, 0], grad_x, big_grads, small_grads


MESH_IDS = pl.DeviceIdType.MESH
HBM_SPEC = pl.BlockSpec(memory_space=pl.ANY)


def _mesh_position():
    x, y, c = lax.axis_index("x"), lax.axis_index("y"), lax.axis_index("c")
    other_chips = [(1 - x, y), (x, 1 - y), (1 - x, 1 - y)]
    return x, y, c, other_chips


def _all_gather(arrays, name):
    n = len(arrays)

    def body(*refs):
        ins, outs = refs[:n], refs[n:2 * n]
        send_sems, recv_sems, local_sems = refs[2 * n:]
        x, y, c, chips = _mesh_position()
        me, sibling = (x, y, c), (x, y, 1 - c)

        def slot(a, px, py, pc):
            return outs[a].at[4 * px + 2 * py + pc]

        def copy(a, k, block, to, src=None):
            dst = slot(a, *block)
            return pltpu.make_async_remote_copy(
                src_ref=dst if src is None else src, dst_ref=dst, send_sem=send_sems.at[7 * a + k],
                recv_sem=recv_sems.at[7 * a + k], device_id=to, device_id_type=MESH_IDS)

        mine = [pltpu.make_async_copy(ins[a], slot(a, *me), local_sems.at[a]) for a in range(n)]
        for cp in mine:
            cp.start()
        started = []
        for a in range(n):
            started.append(copy(a, 0, me, sibling, src=ins[a]))
            started += [copy(a, 1 + j, me, (*chip, c), src=ins[a]) for j, chip in enumerate(chips)]
        for cp in started:
            cp.start()
        for a in range(n):
            for j, chip in enumerate(chips):
                copy(a, 1 + j, (*chip, c), me).wait_recv()
                passed = copy(a, 4 + j, (*chip, c), sibling)
                passed.start()
                started.append(passed)
        for a in range(n):
            copy(a, 0, sibling, me).wait_recv()
            for j, chip in enumerate(chips):
                copy(a, 4 + j, (*chip, 1 - c), me).wait_recv()
        for cp in started:
            cp.wait_send()
        for cp in mine:
            cp.wait()

    return pl.pallas_call(
        body, name=name,
        in_specs=[HBM_SPEC] * n, out_specs=[HBM_SPEC] * n,
        out_shape=[jax.ShapeDtypeStruct((N_DEV,) + a.shape, a.dtype) for a in arrays],
        scratch_shapes=[pltpu.SemaphoreType.DMA((7 * n,)), pltpu.SemaphoreType.DMA((7 * n,)), pltpu.SemaphoreType.DMA((n,))],
    )(*arrays)


def _exchange_sibling(grads, name):
    n = len(grads)

    def body(*refs):
        ins, outs = refs[:n], refs[n:2 * n]
        send_sems, recv_sems = refs[2 * n:]
        x, y, c, _ = _mesh_position()
        copies = []
        for a in range(n):
            for ch in range(4):
                copies.append(pltpu.make_async_remote_copy(
                    src_ref=ins[a].at[2 * ch + (1 - c)], dst_ref=outs[a].at[ch], send_sem=send_sems.at[4 * a + ch],
                    recv_sem=recv_sems.at[4 * a + ch], device_id=(x, y, 1 - c), device_id_type=MESH_IDS))
        for cp in copies:
            cp.start()
        for cp in copies:
            cp.wait()

    return pl.pallas_call(
        body, name=name, in_specs=[HBM_SPEC] * n, out_specs=[HBM_SPEC] * n,
        out_shape=[jax.ShapeDtypeStruct((4,) + g.shape[1:], g.dtype) for g in grads],
        scratch_shapes=[pltpu.SemaphoreType.DMA((4 * n,)), pltpu.SemaphoreType.DMA((4 * n,))],
    )(*grads)


def _exchange_chips(sums, name):
    n = len(sums)

    def body(*refs):
        ins, outs = refs[:n], refs[n:2 * n]
        send_sems, recv_sems = refs[2 * n:]
        x, y, c, chips = _mesh_position()
        copies = []
        for a in range(n):
            for j, (px, py) in enumerate(chips):
                copies.append(pltpu.make_async_remote_copy(
                    src_ref=ins[a].at[2 * px + py], dst_ref=outs[a].at[j], send_sem=send_sems.at[3 * a + j],
                    recv_sem=recv_sems.at[3 * a + j], device_id=(px, py, c), device_id_type=MESH_IDS))
        for cp in copies:
            cp.start()
        for cp in copies:
            cp.wait()

    return pl.pallas_call(
        body, name=name, in_specs=[HBM_SPEC] * n, out_specs=[HBM_SPEC] * n,
        out_shape=[jax.ShapeDtypeStruct((3,) + s.shape[1:], s.dtype) for s in sums],
        scratch_shapes=[pltpu.SemaphoreType.DMA((3 * n,)), pltpu.SemaphoreType.DMA((3 * n,))],
    )(*sums)


def _row_tile(rows, cols):
    t = max(8, min(rows, (1 << 18) // cols // 8 * 8))
    while rows % t:
        t -= 8
    return t


def _add_sibling(grads8, recv, core, name):
    _, R, C = grads8.shape
    tr = _row_tile(R, C)
    g4 = grads8.reshape(4, 2, R, C)

    def body(core_ref, g_ref, r_ref, o_ref):
        o_ref[...] = g_ref[0] + r_ref[...]

    return pl.pallas_call(
        body, name=name,
        grid_spec=pltpu.PrefetchScalarGridSpec(
            num_scalar_prefetch=1, grid=(4, R // tr),
            in_specs=[pl.BlockSpec((1, 1, tr, C), lambda ch, r, core_ref: (ch, core_ref[0], r, 0)),
                      pl.BlockSpec((1, tr, C), lambda ch, r, core_ref: (ch, r, 0))],
            out_specs=pl.BlockSpec((1, tr, C), lambda ch, r, core_ref: (ch, r, 0))),
        out_shape=jax.ShapeDtypeStruct((4, R, C), F32),
        compiler_params=_params(("arbitrary", "arbitrary")),
    )(core, g4, recv)


def _adam_math(w, g, m, v):
    m = ADAM_B1 * m + (1.0 - ADAM_B1) * g
    v = ADAM_B2 * v + (1.0 - ADAM_B2) * jnp.square(g)
    m_hat = m / (1.0 - ADAM_B1 ** ADAM_STEP)
    v_hat = v / (1.0 - ADAM_B2 ** ADAM_STEP)
    delta = -ADAM_LR * (m_hat / (jnp.sqrt(v_hat) + ADAM_EPS) + ADAM_WD * w)
    return delta, m, v


def _adam_big(w, m, v, chip_sums, recv, chip, name):
    R, C = w.shape
    tr = _row_tile(R, C)

    def body(chip_ref, w_ref, m_ref, v_ref, s_ref, r_ref, g_ref, d_ref, nm_ref, nv_ref):
        g = s_ref[0] + r_ref[0] + r_ref[1] + r_ref[2]
        g_ref[...] = g
        d_ref[...], nm_ref[...], nv_ref[...] = _adam_math(w_ref[...], g, m_ref[...], v_ref[...])

    blk = pl.BlockSpec((tr, C), lambda r, chip_ref: (r, 0))
    return pl.pallas_call(
        body, name=name,
        grid_spec=pltpu.PrefetchScalarGridSpec(
            num_scalar_prefetch=1, grid=(R // tr,),
            in_specs=[blk, blk, blk, pl.BlockSpec((1, tr, C), lambda r, chip_ref: (chip_ref[0], r, 0)),
                      pl.BlockSpec((3, tr, C), lambda r, chip_ref: (0, r, 0))],
            out_specs=[blk] * 4),
        out_shape=[jax.ShapeDtypeStruct((R, C), F32)] * 4,
        compiler_params=_params(("arbitrary",)),
    )(chip, w, m, v, chip_sums, recv)


def _adam_small(w, m, v, partials):
    R = w.shape[0]

    def body(w_ref, m_ref, v_ref, p_ref, g_ref, d_ref, nm_ref, nv_ref):
        g = p_ref[0]
        for d in range(1, N_DEV):
            g = g + p_ref[d]
        g_ref[...] = g
        d_ref[...], nm_ref[...], nv_ref[...] = _adam_math(w_ref[...], g, m_ref[...], v_ref[...])

    return pl.pallas_call(body, name="adam_small", out_shape=[jax.ShapeDtypeStruct((R, LANES), F32)] * 4,
                          compiler_params=_params(None, VMEM_MID))(w, m, v, partials)


PACK_QUANTUM = SUBLANES * LANES


def _pack(named, names):
    parts = []
    for nme in names:
        flat = named[nme].reshape(-1)
        parts.append(jnp.pad(flat, (0, -flat.size % PACK_QUANTUM)))
    return jnp.concatenate(parts).reshape(-1, LANES)


def _unpack(packed, shapes, names):
    flat = packed.reshape(-1)
    out, pos = {}, 0
    for nme in names:
        size = math.prod(shapes[nme])
        out[nme] = flat[pos:pos + size].reshape(shapes[nme])
        pos += size + (-size % PACK_QUANTUM)
    return out


BIG = ("w_in", "w_glu", "w_attn_branch", "w_ssm_branch", "w_out", "w_ff_in", "w_ff_out")
COLUMN_SHARDED = ("w_in", "w_attn_branch", "w_ssm_branch", "w_ff_in")
SMALL = ("norm_mix_pre", "norm_mix_post", "norm_mlp_pre", "norm_mlp_post", "rel_bias", "sinks", "lam_re", "lam_im",
         "log_dt", "b_re", "b_im", "c_re", "c_im", "d_skip")
ALL_WEIGHTS = ("norm_mix_pre", "norm_mix_post", "norm_mlp_pre", "norm_mlp_post", "w_in", "rel_bias", "sinks", "lam_re",
               "lam_im", "log_dt", "b_re", "b_im", "c_re", "c_im", "d_skip", "w_glu", "w_attn_branch", "w_ssm_branch",
               "w_out", "w_ff_in", "w_ff_out")


def _full_from_gathered(name, gathered):
    _, r, c = gathered.shape
    if name in COLUMN_SHARDED:
        return jnp.transpose(gathered, (1, 0, 2)).reshape(r, N_DEV * c)
    return gathered.reshape(N_DEV * r, c)


def _blocks_from_full(name, full):
    r, c = full.shape
    if name in COLUMN_SHARDED:
        return jnp.transpose(full.reshape(r, N_DEV, c // N_DEV), (1, 0, 2))
    return full.reshape(N_DEV, r // N_DEV, c)


def kernel(x, norm_mix_pre, norm_mix_post, norm_mlp_pre, norm_mlp_post, w_in, rel_bias, sinks, lam_re, lam_im, log_dt, b_re, b_im, c_re, c_im, d_skip, w_glu, w_attn_branch, w_ssm_branch, w_out, w_ff_in, w_ff_out, loss_target, m_norm_mix_pre, m_norm_mix_post, m_norm_mlp_pre, m_norm_mlp_post, m_w_in, m_rel_bias, m_sinks, m_lam_re, m_lam_im, m_log_dt, m_b_re, m_b_im, m_c_re, m_c_im, m_d_skip, m_w_glu, m_w_attn_branch, m_w_ssm_branch, m_w_out, m_w_ff_in, m_w_ff_out, v_norm_mix_pre, v_norm_mix_post, v_norm_mlp_pre, v_norm_mlp_post, v_w_in, v_rel_bias, v_sinks, v_lam_re, v_lam_im, v_log_dt, v_b_re, v_b_im, v_c_re, v_c_im, v_d_skip, v_w_glu, v_w_attn_branch, v_w_ssm_branch, v_w_out, v_w_ff_in, v_w_ff_out):
    args = dict(locals())
    w = {n: args[n] for n in ALL_WEIGHTS}
    m = {n: args["m_" + n] for n in ALL_WEIGHTS}
    v = {n: args["v_" + n] for n in ALL_WEIGHTS}
    core = lax.axis_index("c").astype(jnp.int32).reshape(1)
    chip = (2 * lax.axis_index("x") + lax.axis_index("y")).astype(jnp.int32).reshape(1)

    shards = [w[n][0].astype(BF16) for n in BIG]
    gathered = _all_gather(shards, "gather_weights")
    big = {n: _full_from_gathered(n, g) for n, g in zip(BIG, gathered)}

    small = {n: (w[n] if n == "rel_bias" else w[n][0]) for n in SMALL}
    loss_part, grad_x, big_grads, small_grads = _local_step(x[0], loss_target[0], small, big)
    loss = lax.psum(loss_part, ("x", "y", "c"))

    blocks = [_blocks_from_full(n, big_grads[n]) for n in BIG]
    from_sibling = _exchange_sibling(blocks, "reduce_sibling")
    chip_sums = [_add_sibling(b, r, core, "add_sibling_" + n) for n, b, r in zip(BIG, blocks, from_sibling)]
    from_chips = _exchange_chips(chip_sums, "reduce_chips")

    grads, deltas, new_m, new_v = {}, {}, {}, {}
    for n, s, r in zip(BIG, chip_sums, from_chips):
        outs = _adam_big(w[n][0], m[n][0], v[n][0], s, r, chip, "adam_" + n)
        grads[n], deltas[n], new_m[n], new_v[n] = (o[None] for o in outs)

    shapes = {n: w[n].shape for n in SMALL}
    packed_partial = _pack({n: small_grads[n].reshape(shapes[n]) for n in SMALL}, SMALL)
    (partials,) = _all_gather([packed_partial], "gather_small_grads")
    packed = _adam_small(_pack(w, SMALL), _pack(m, SMALL), _pack(v, SMALL), partials)
    for store, p in zip((grads, deltas, new_m, new_v), packed):
        store.update(_unpack(p, shapes, SMALL))

    return (loss, grad_x[None], *[grads[n] for n in ALL_WEIGHTS], *[deltas[n] for n in ALL_WEIGHTS],
            *[new_m[n] for n in ALL_WEIGHTS], *[new_v[n] for n in ALL_WEIGHTS])
```

```python
import functools
import math

import jax
import jax.numpy as jnp
import numpy as np
from jax import lax
from jax.experimental import pallas as pl
from jax.experimental.pallas import tpu as pltpu

F32 = jnp.float32
BF16 = jnp.bfloat16

D_MODEL = 1024
N_HEADS = 8
HEAD_DIM = 64
ATTN_W = 512
KV_W = 128
BLOCK = 128
N_BUCKETS = 32
SSM_W = 512
N_GROUPS = 32
N_STATE = 64
GROUP_CH = 16
STATES = N_GROUPS * N_STATE
D_FF = 4096
IN_W = 3328
SPLITS = (0, 512, 640, 768, 1280, 2304, 3328)
RMS_EPS = 1e-6
NEG_INF = -1e30
SUBLANES = 8
LANES = 128
SSM_LANE_BLOCK = 512
N_SSM_BLOCKS = STATES // SSM_LANE_BLOCK
VMEM_BIG = 52 * 1024 * 1024
VMEM_MID = 40 * 1024 * 1024

ADAM_LR = 0.001
ADAM_B1 = 0.9
ADAM_B2 = 0.999
ADAM_EPS = 1e-08
ADAM_WD = 0.01
ADAM_STEP = 10

N_DEV = 8


def _dot(a, b):
    return jnp.dot(a, b, preferred_element_type=F32)


def _dot_nt(a, b):
    return lax.dot_general(a, b, (((1,), (1,)), ((), ())), preferred_element_type=F32)


def _dot_tn(a, b):
    return lax.dot_general(a, b, (((0,), (0,)), ((), ())), preferred_element_type=F32)


def _rms_scale(x):
    return lax.rsqrt(jnp.mean(x * x, axis=-1, keepdims=True) + RMS_EPS)


def _rms_bwd(dy, x, r, g):
    t = dy * g
    dx = r * t - x * (r * r * r) * jnp.mean(t * x, axis=-1, keepdims=True)
    dg = jnp.sum(dy * x * r, axis=0, keepdims=True)
    return dx, dg


def _const_spec(shape):
    nd = len(shape)
    return pl.BlockSpec(shape, lambda *_: (0,) * nd, pipeline_mode=pl.Buffered(1))


def _params(sem, vmem=None):
    return pltpu.CompilerParams(dimension_semantics=sem, vmem_limit_bytes=vmem)


MESH_IDS = pl.DeviceIdType.MESH
HBM_SPEC = pl.BlockSpec(memory_space=pl.ANY)


class _Carry:
    def __init__(self, inputs, out_shapes, sems, start, finish):
        self.inputs, self.out_shapes, self.sems, self.start, self.finish = list(inputs), list(out_shapes), list(sems), start, finish


def _join(a, b):
    na_in, na_out, na_sem = len(a.inputs), len(a.out_shapes), len(a.sems)

    def start(ins, outs, sems):
        a.start(ins[:na_in], outs[:na_out], sems[:na_sem])
        b.start(ins[na_in:], outs[na_out:], sems[na_sem:])

    def finish(ins, outs, sems):
        a.finish(ins[:na_in], outs[:na_out], sems[:na_sem])
        b.finish(ins[na_in:], outs[na_out:], sems[na_sem:])

    return _Carry(a.inputs + b.inputs, a.out_shapes + b.out_shapes, a.sems + b.sems, start, finish)


def _hosted_call(body, carry, edge, *, name, grid, in_specs, out_specs, out_shape, scratch_shapes, compiler_params, inputs):
    n_in, n_out = len(in_specs), len(out_specs)
    if carry is None:
        outs = pl.pallas_call(body, name=name, grid=grid, in_specs=in_specs, out_specs=out_specs, out_shape=out_shape,
                              scratch_shapes=scratch_shapes, compiler_params=compiler_params)(*inputs)
        return list(outs), []
    c_in, c_out, c_sem = len(carry.inputs), len(carry.out_shapes), len(carry.sems)

    def wrapped(*refs):
        ins, refs = refs[:n_in], refs[n_in:]
        cins, refs = refs[:c_in], refs[c_in:]
        outs, refs = refs[:n_out], refs[n_out:]
        couts, refs = refs[:c_out], refs[c_out:]
        scratch, csems = refs[:len(refs) - c_sem], refs[len(refs) - c_sem:]
        first, last = edge()

        @pl.when(first)
        def _():
            carry.start(cins, couts, csems)

        body(*ins, *outs, *scratch)

        @pl.when(last)
        def _():
            carry.finish(cins, couts, csems)

    outs = pl.pallas_call(
        wrapped, name=name, grid=grid, in_specs=list(in_specs) + [HBM_SPEC] * c_in,
        out_specs=list(out_specs) + [HBM_SPEC] * c_out, out_shape=list(out_shape) + carry.out_shapes,
        scratch_shapes=list(scratch_shapes) + carry.sems, compiler_params=compiler_params)(*inputs, *carry.inputs)
    return list(outs[:n_out]), list(outs[n_out:])


def _edge_1d(n_steps):
    return lambda: (pl.program_id(0) == 0, pl.program_id(0) == n_steps - 1)


def _edge_2d(n0, n1):
    return lambda: ((pl.program_id(0) == 0) & (pl.program_id(1) == 0),
                    (pl.program_id(0) == n0 - 1) & (pl.program_id(1) == n1 - 1))


def _run_carry(carry, name):
    c_in, c_out = len(carry.inputs), len(carry.out_shapes)

    def body(*refs):
        ins, outs, sems = refs[:c_in], refs[c_in:c_in + c_out], refs[c_in + c_out:]
        carry.start(ins, outs, sems)
        carry.finish(ins, outs, sems)

    return pl.pallas_call(body, name=name, in_specs=[HBM_SPEC] * c_in, out_specs=[HBM_SPEC] * c_out,
                          out_shape=carry.out_shapes, scratch_shapes=carry.sems)(*carry.inputs)


def _in_proj_fwd(x, g1, w_in, tile, carry=None):
    T = x.shape[0]

    def body(x_ref, g_ref, w_ref, q_ref, k_ref, v_ref, u_ref, ga_ref, gs_ref):
        xv = x_ref[...]
        h = (xv * _rms_scale(xv) * g_ref[...]).astype(BF16)
        outs = (q_ref, k_ref, v_ref, u_ref, ga_ref, gs_ref)
        for p, o_ref in enumerate(outs):
            o_ref[...] = _dot(h, w_ref[:, SPLITS[p]:SPLITS[p + 1]]).astype(o_ref.dtype)

    widths = [SPLITS[p + 1] - SPLITS[p] for p in range(6)]
    dtypes = [BF16, BF16, BF16, F32, F32, F32]
    return _hosted_call(
        body, carry, _edge_1d(T // tile), name="in_proj_fwd", grid=(T // tile,),
        in_specs=[pl.BlockSpec((tile, D_MODEL), lambda i: (i, 0)), _const_spec((1, D_MODEL)), _const_spec((D_MODEL, IN_W))],
        out_specs=[pl.BlockSpec((tile, w), lambda i: (i, 0)) for w in widths],
        out_shape=[jax.ShapeDtypeStruct((T, w), dt) for w, dt in zip(widths, dtypes)],
        scratch_shapes=[], compiler_params=_params(("arbitrary",), VMEM_MID), inputs=(x, g1, w_in))


def _in_proj_bwd(x, g1, w_in, dx1, dparts, tile, carry=None):
    T = x.shape[0]
    widths = [SPLITS[p + 1] - SPLITS[p] for p in range(6)]
    n_steps = T // tile

    def body(x_ref, g_ref, w_ref, dx1_ref, dq, dk, dv, du, dga, dgs, gx_ref, dw_hbm, dg_ref, acc_ref):
        i = pl.program_id(0)
        xv = x_ref[...]
        r = _rms_scale(xv)
        g = g_ref[...]
        h = (xv * r * g).astype(BF16)
        dh = jnp.zeros((tile, D_MODEL), F32)
        for p, d_ref in enumerate((dq, dk, dv, du, dga, dgs)):
            dp = d_ref[...]
            cols = slice(SPLITS[p], SPLITS[p + 1])
            dh = dh + _dot_nt(dp, w_ref[:, cols])
            contrib = _dot_tn(h, dp)

            @pl.when(i == 0)
            def _():
                acc_ref[:, cols] = contrib

            @pl.when(i > 0)
            def _():
                acc_ref[:, cols] += contrib

        dxn, dg = _rms_bwd(dh, xv, r, g)
        gx_ref[...] = dx1_ref[...] + dxn

        @pl.when(i == 0)
        def _():
            dg_ref[...] = dg

        @pl.when(i > 0)
        def _():
            dg_ref[...] += dg

        @pl.when(i == n_steps - 1)
        def _():
            pltpu.sync_copy(acc_ref, dw_hbm)

    tok = lambda w: pl.BlockSpec((tile, w), lambda i: (i, 0))
    return _hosted_call(
        body, carry, _edge_1d(n_steps), name="in_proj_bwd", grid=(n_steps,),
        in_specs=[tok(D_MODEL), _const_spec((1, D_MODEL)), _const_spec((D_MODEL, IN_W)), tok(D_MODEL)] + [tok(w) for w in widths],
        out_specs=[tok(D_MODEL), HBM_SPEC, pl.BlockSpec((1, D_MODEL), lambda i: (0, 0))],
        out_shape=[jax.ShapeDtypeStruct((T, D_MODEL), F32), jax.ShapeDtypeStruct((D_MODEL, IN_W), F32),
                   jax.ShapeDtypeStruct((1, D_MODEL), F32)],
        scratch_shapes=[pltpu.VMEM((D_MODEL, IN_W), F32)],
        compiler_params=_params(("arbitrary",), VMEM_BIG), inputs=(x, g1, w_in, dx1, *dparts))


def _bucket_table():
    qi = np.arange(BLOCK)[:, None]
    kj = np.arange(2 * BLOCK)[None, :]
    dist = qi + BLOCK - kj
    max_exact = N_BUCKETS // 2
    d = np.maximum(dist, 0)
    df = np.maximum(d, 1).astype(np.float32)
    large = max_exact + (np.log(df / np.float32(max_exact)) / np.float32(math.log(BLOCK / max_exact))
                         * np.float32(N_BUCKETS - max_exact)).astype(np.int32)
    large = np.minimum(large, N_BUCKETS - 1)
    bucket = np.where(d < max_exact, d, large)
    return np.where((dist >= 0) & (dist < BLOCK), bucket, -1).astype(np.int32)


def _build_bias(bucket_ref, rb_ref, bias_ref):
    bk = bucket_ref[...]
    for h in range(N_HEADS):
        def add(b, acc, h=h):
            return acc + jnp.where(bk == b, rb_ref[b, h], 0.0)
        bias_ref[h] = lax.fori_loop(0, N_BUCKETS, add, jnp.zeros((BLOCK, 2 * BLOCK), F32))


def _kv_variants(prev_ref, cur_ref):
    cat = jnp.concatenate([prev_ref[...], cur_ref[...]], axis=0)
    lo = lax.broadcasted_iota(jnp.int32, cat.shape, 1) < HEAD_DIM
    zero = jnp.zeros_like(cat)
    head0_lo = jnp.where(lo, cat, zero)
    head1_hi = jnp.where(lo, zero, cat)
    return ((head0_lo, pltpu.roll(head0_lo, HEAD_DIM, 1)), (pltpu.roll(head1_hi, HEAD_DIM, 1), head1_hi))


def _merge_kv_grads(g):
    lo = lax.broadcasted_iota(jnp.int32, g[0][0].shape, 1) < HEAD_DIM
    return jnp.where(lo, g[0][0] + pltpu.roll(g[0][1], HEAD_DIM, 1), g[1][1] + pltpu.roll(g[1][0], HEAD_DIM, 1))


def _attn_probs(qm, kvar, bias_h, valid, sink):
    s = _dot_nt(qm, kvar) * (HEAD_DIM ** -0.5) + bias_h
    s = jnp.where(valid, s, NEG_INF)
    m = jnp.maximum(jnp.max(s, axis=-1, keepdims=True), sink)
    p = jnp.exp(s - m)
    e_sink = jnp.exp(sink - m)
    inv = 1.0 / (jnp.sum(p, axis=-1, keepdims=True) + e_sink)
    return p * inv, e_sink * inv


def _attn_valid(bucket_ref, n):
    col = lax.broadcasted_iota(jnp.int32, (BLOCK, 2 * BLOCK), 1)
    return (bucket_ref[...] >= 0) & ((n > 0) | (col >= BLOCK))


def _attn_fwd(q, k, v, bucket, rel_bias, sinks, carry=None):
    T = q.shape[0]
    nb = T // BLOCK

    def body(q_ref, kc_ref, kp_ref, vc_ref, vp_ref, bucket_ref, rb_ref, sk_ref, o_ref, bias_ref):
        n = pl.program_id(0)

        @pl.when(n == 0)
        def _():
            _build_bias(bucket_ref, rb_ref, bias_ref)

        valid = _attn_valid(bucket_ref, n)
        kvar = _kv_variants(kp_ref, kc_ref)
        vvar = _kv_variants(vp_ref, vc_ref)
        for m in range(N_HEADS // 2):
            qm = q_ref[:, m * LANES:(m + 1) * LANES]
            acc = jnp.zeros((BLOCK, LANES), F32)
            for e in range(2):
                h = 2 * m + e
                pr, _ = _attn_probs(qm, kvar[m // 2][e], bias_ref[h], valid, sk_ref[0, h])
                acc = acc + _dot(pr.astype(BF16), vvar[m // 2][e])
            o_ref[:, m * LANES:(m + 1) * LANES] = acc.astype(o_ref.dtype)

    cur = lambda w: pl.BlockSpec((BLOCK, w), lambda n: (n, 0))
    prev = lambda w: pl.BlockSpec((BLOCK, w), lambda n: (jnp.maximum(n - 1, 0), 0))
    smem = pl.BlockSpec(memory_space=pltpu.SMEM)
    return _hosted_call(
        body, carry, _edge_1d(nb), name="attn_fwd", grid=(nb,),
        in_specs=[cur(ATTN_W), cur(KV_W), prev(KV_W), cur(KV_W), prev(KV_W), _const_spec((BLOCK, 2 * BLOCK)), smem, smem],
        out_specs=[cur(ATTN_W)],
        out_shape=[jax.ShapeDtypeStruct((T, ATTN_W), BF16)],
        scratch_shapes=[pltpu.VMEM((N_HEADS, BLOCK, 2 * BLOCK), F32)],
        compiler_params=_params(("arbitrary",)), inputs=(q, k, k, v, v, bucket, rel_bias, sinks))


ATTN_SMALL_ROWS = N_BUCKETS + SUBLANES


def _attn_bwd(q, k, v, datt, bucket, rel_bias, sinks, carry=None):
    T = q.shape[0]
    nb = T // BLOCK

    def body(q_ref, do_ref, kc_ref, kp_ref, vc_ref, vp_ref, bucket_ref, rb_ref, sk_ref,
             dq_ref, dk_ref, dv_ref, small_ref, bias_ref, ds_sum_ref, dsink_ref, kcarry_ref, vcarry_ref):
        n = pl.program_id(0)

        @pl.when(n == 0)
        def _():
            _build_bias(bucket_ref, rb_ref, bias_ref)
            ds_sum_ref[...] = jnp.zeros_like(ds_sum_ref)
            dsink_ref[...] = jnp.zeros_like(dsink_ref)
            kcarry_ref[...] = jnp.zeros_like(kcarry_ref)
            vcarry_ref[...] = jnp.zeros_like(vcarry_ref)

        @pl.when(n < nb)
        def _():
            valid = _attn_valid(bucket_ref, n)
            kvar = _kv_variants(kp_ref, kc_ref)
            vvar = _kv_variants(vp_ref, vc_ref)
            zeros = lambda: [[jnp.zeros((2 * BLOCK, LANES), F32) for _ in range(2)] for _ in range(2)]
            dk_var, dv_var = zeros(), zeros()
            lane = lax.broadcasted_iota(jnp.int32, (1, LANES), 1)
            dsink = jnp.zeros((1, LANES), F32)
            for m in range(N_HEADS // 2):
                qm = q_ref[:, m * LANES:(m + 1) * LANES]
                dom = do_ref[:, m * LANES:(m + 1) * LANES]
                dqm = jnp.zeros((BLOCK, LANES), F32)
                for e in range(2):
                    h = 2 * m + e
                    kvh = m // 2
                    pr, p_sink = _attn_probs(qm, kvar[kvh][e], bias_ref[h], valid, sk_ref[0, h])
                    dp = _dot_nt(dom, vvar[kvh][e])
                    dsum = jnp.sum(pr * dp, axis=-1, keepdims=True)
                    ds = pr * (dp - dsum)
                    ds_sum_ref[h] += ds
                    dsink = dsink + jnp.where(lane == h, -jnp.sum(p_sink * dsum), 0.0)
                    dsc = (ds * (HEAD_DIM ** -0.5)).astype(BF16)
                    dqm = dqm + _dot(dsc, kvar[kvh][e])
                    dk_var[kvh][e] = dk_var[kvh][e] + _dot_tn(dsc, qm)
                    dv_var[kvh][e] = dv_var[kvh][e] + _dot_tn(pr.astype(BF16), dom)
                dq_ref[:, m * LANES:(m + 1) * LANES] = dqm.astype(dq_ref.dtype)
            dsink_ref[...] += dsink
            dk_cat = _merge_kv_grads(dk_var)
            dv_cat = _merge_kv_grads(dv_var)

            @pl.when(n > 0)
            def _():
                dk_ref[...] = (kcarry_ref[...] + dk_cat[:BLOCK]).astype(dk_ref.dtype)
                dv_ref[...] = (vcarry_ref[...] + dv_cat[:BLOCK]).astype(dv_ref.dtype)

            kcarry_ref[...] = dk_cat[BLOCK:]
            vcarry_ref[...] = dv_cat[BLOCK:]

        @pl.when(n == nb)
        def _():
            dk_ref[...] = kcarry_ref[...].astype(dk_ref.dtype)
            dv_ref[...] = vcarry_ref[...].astype(dv_ref.dtype)
            bk = bucket_ref[...]
            row = lax.broadcasted_iota(jnp.int32, (ATTN_SMALL_ROWS, LANES), 0)
            col = lax.broadcasted_iota(jnp.int32, (ATTN_SMALL_ROWS, LANES), 1)
            out = jnp.where(row == N_BUCKETS, jnp.broadcast_to(dsink_ref[...], (ATTN_SMALL_ROWS, LANES)), 0.0)
            for h in range(N_HEADS):
                dsh = ds_sum_ref[h]

                def add(b, acc, h=h, dsh=dsh):
                    val = jnp.sum(jnp.where(bk == b, dsh, 0.0))
                    return acc + jnp.where((row == b) & (col == h), val, 0.0)
                out = lax.fori_loop(0, N_BUCKETS, add, out)
            small_ref[...] = out

    last = nb - 1
    cur = lambda w: pl.BlockSpec((BLOCK, w), lambda n: (jnp.minimum(n, last), 0))
    prev = lambda w: pl.BlockSpec((BLOCK, w), lambda n: (jnp.clip(n - 1, 0, last), 0))
    smem = pl.BlockSpec(memory_space=pltpu.SMEM)
    return _hosted_call(
        body, carry, _edge_1d(nb + 1), name="attn_bwd", grid=(nb + 1,),
        in_specs=[cur(ATTN_W), cur(ATTN_W), cur(KV_W), prev(KV_W), cur(KV_W), prev(KV_W),
                  _const_spec((BLOCK, 2 * BLOCK)), smem, smem],
        out_specs=[cur(ATTN_W), prev(KV_W), prev(KV_W), pl.BlockSpec((ATTN_SMALL_ROWS, LANES), lambda n: (0, 0))],
        out_shape=[jax.ShapeDtypeStruct((T, ATTN_W), BF16), jax.ShapeDtypeStruct((T, KV_W), BF16),
                   jax.ShapeDtypeStruct((T, KV_W), BF16), jax.ShapeDtypeStruct((ATTN_SMALL_ROWS, LANES), F32)],
        scratch_shapes=[pltpu.VMEM((N_HEADS, BLOCK, 2 * BLOCK), F32), pltpu.VMEM((N_HEADS, BLOCK, 2 * BLOCK), F32),
                        pltpu.VMEM((1, LANES), F32), pltpu.VMEM((BLOCK, KV_W), F32), pltpu.VMEM((BLOCK, KV_W), F32)],
        compiler_params=_params(("arbitrary",)), inputs=(q, datt, k, k, v, v, bucket, rel_bias, sinks))


def _cmul(ar, ai, br, bi):
    return ar * br - ai * bi, ar * bi + ai * br


def _ssm_discretize(lr, li, ldt):
    dt = jnp.exp(ldt)
    mag = jnp.exp(lr * dt)
    ab_re = mag * jnp.cos(li * dt)
    ab_im = mag * jnp.sin(li * dt)
    nr = ab_re - 1.0
    den = lr * lr + li * li
    f_re = (nr * lr + ab_im * li) / den
    f_im = (ab_im * lr - nr * li) / den
    return ab_re, ab_im, f_re, f_im


def _ssm_prep(lam_re, lam_im, ldt_rep, bd_re, bd_im):
    def body(lr_ref, li_ref, ldt_ref, bdr_ref, bdi_ref, ar_ref, ai_ref, br_ref, bi_ref):
        ab_re, ab_im, f_re, f_im = _ssm_discretize(lr_ref[...], li_ref[...], ldt_ref[...])
        ar_ref[...] = ab_re
        ai_ref[...] = ab_im
        bdr, bdi = bdr_ref[0], bdi_ref[0]
        br_ref[0] = (bdr * f_re - bdi * f_im).astype(BF16)
        bi_ref[0] = (bdi * f_re + bdr * f_im).astype(BF16)

    row = pl.BlockSpec((1, SSM_LANE_BLOCK), lambda j: (0, j))
    mat = pl.BlockSpec((1, LANES, SSM_LANE_BLOCK), lambda j: (j, 0, 0))
    return pl.pallas_call(
        body, name="ssm_prep", grid=(N_SSM_BLOCKS,),
        in_specs=[row, row, row, mat, mat], out_specs=[row, row, mat, mat],
        out_shape=[jax.ShapeDtypeStruct((1, STATES), F32)] * 2 + [jax.ShapeDtypeStruct((N_SSM_BLOCKS, LANES, SSM_LANE_BLOCK), BF16)] * 2,
        compiler_params=_params(("arbitrary",)),
    )(lam_re, lam_im, ldt_rep, bd_re, bd_im)


def _ssm_prep_bwd(lam_re, lam_im, ldt_rep, bd_re, bd_im, dbr, dbi, da_re, da_im):
    def body(lr_ref, li_ref, ldt_ref, bdr_ref, bdi_ref, dbr_ref, dbi_ref, dar_ref, dai_ref,
             dbdr_ref, dbdi_ref, dlr_ref, dli_ref, dldt_ref):
        lr, li, ldt = lr_ref[...], li_ref[...], ldt_ref[...]
        (_, _, f_re, f_im), vjp = jax.vjp(_ssm_discretize, lr, li, ldt)
        bdr, bdi, gbr, gbi = bdr_ref[0], bdi_ref[0], dbr_ref[0], dbi_ref[0]
        dbdr_ref[0] = gbr * f_re + gbi * f_im
        dbdi_ref[0] = gbi * f_re - gbr * f_im
        df_re = jnp.sum(gbr * bdr + gbi * bdi, axis=0, keepdims=True)
        df_im = jnp.sum(gbi * bdr - gbr * bdi, axis=0, keepdims=True)
        dlr, dli, dldt = vjp((dar_ref[...], dai_ref[...], df_re, df_im))
        dlr_ref[...] = dlr
        dli_ref[...] = dli
        dldt_ref[...] = dldt

    row = pl.BlockSpec((1, SSM_LANE_BLOCK), lambda j: (0, j))
    mat = pl.BlockSpec((1, LANES, SSM_LANE_BLOCK), lambda j: (j, 0, 0))
    mat_shape = jax.ShapeDtypeStruct((N_SSM_BLOCKS, LANES, SSM_LANE_BLOCK), F32)
    row_shape = jax.ShapeDtypeStruct((1, STATES), F32)
    return pl.pallas_call(
        body, name="ssm_prep_bwd", grid=(N_SSM_BLOCKS,),
        in_specs=[row, row, row, mat, mat, mat, mat, row, row], out_specs=[mat, mat, row, row, row],
        out_shape=[mat_shape, mat_shape, row_shape, row_shape, row_shape],
        compiler_params=_params(("arbitrary",)),
    )(lam_re, lam_im, ldt_rep, bd_re, bd_im, dbr, dbi, da_re, da_im)


def _group_sum(x):
    def body(x_ref, o_ref):
        o_ref[...] = jnp.sum(x_ref[...], axis=1, keepdims=True)
    return pl.pallas_call(body, name="ssm_group_sum", out_shape=jax.ShapeDtypeStruct((N_GROUPS, 1), F32))(x)


def _power_table(ar, ai, p_re_ref, p_im_ref, steps):
    shape = (SUBLANES, SSM_LANE_BLOCK)
    p_re_ref[0:SUBLANES] = jnp.broadcast_to(ar, shape)
    p_im_ref[0:SUBLANES] = jnp.broadcast_to(ai, shape)
    m = 1
    while m < steps:
        rows = m * SUBLANES
        top_re = p_re_ref[rows - SUBLANES:rows]
        top_im = p_im_ref[rows - SUBLANES:rows]
        cur_re = p_re_ref[0:rows].reshape(m, SUBLANES, SSM_LANE_BLOCK)
        cur_im = p_im_ref[0:rows].reshape(m, SUBLANES, SSM_LANE_BLOCK)
        nxt_re, nxt_im = _cmul(cur_re, cur_im, top_re[None], top_im[None])
        p_re_ref[rows:2 * rows] = nxt_re.reshape(rows, SSM_LANE_BLOCK)
        p_im_ref[rows:2 * rows] = nxt_im.reshape(rows, SSM_LANE_BLOCK)
        m *= 2


def _to_segments(src_ref, dst_ref, steps):
    for s in range(SUBLANES):
        dst_ref[pl.ds(s, steps, stride=SUBLANES), :] = src_ref[s * steps:(s + 1) * steps, :]


def _from_segments(src_ref, dst_ref, steps):
    for s in range(SUBLANES):
        dst_ref[s * steps:(s + 1) * steps, :] = src_ref[pl.ds(s, steps, stride=SUBLANES), :]


def _segment_carries(e_re, e_im, an_re, an_im, c_re, c_im, reverse):
    order = range(SUBLANES - 1, -1, -1) if reverse else range(SUBLANES)
    ins_re, ins_im = [None] * SUBLANES, [None] * SUBLANES
    for s in order:
        ins_re[s], ins_im[s] = c_re, c_im
        pr, pi = _cmul(an_re, an_im, c_re, c_im)
        c_re = e_re[s:s + 1] + pr
        c_im = e_im[s:s + 1] + pi
    return jnp.concatenate(ins_re, axis=0), jnp.concatenate(ins_im, axis=0), c_re, c_im


def _ssm_fwd(u, a_re, a_im, b_re, b_im, c_re, c_im, d_skip, chunk, carry=None):
    T = u.shape[0]
    nc = T // chunk
    steps = chunk // SUBLANES
    blk = SSM_LANE_BLOCK

    def body(u_ref, ar_ref, ai_ref, br_ref, bi_ref, cr_ref, ci_ref, dk_ref,
             y_ref, hr_ref, hi_ref, inr_ref, ini_ref, useg_ref, yseg_ref, pr_ref, pi_ref, carry_ref):
        c = pl.program_id(1)
        ar, ai = ar_ref[...], ai_ref[...]

        @pl.when(c == 0)
        def _():
            _power_table(ar, ai, pr_ref, pi_ref, steps)
            carry_ref[...] = jnp.zeros_like(carry_ref)

        _to_segments(u_ref, useg_ref, steps)
        ub = useg_ref[...].astype(BF16)
        hr_ref[...] = _dot(ub, br_ref[0])
        hi_ref[...] = _dot(ub, bi_ref[0])
        ar8 = jnp.broadcast_to(ar, (SUBLANES, blk))
        ai8 = jnp.broadcast_to(ai, (SUBLANES, blk))

        def scan(t, prev):
            rows = pl.ds(pl.multiple_of(t * SUBLANES, SUBLANES), SUBLANES)
            pr, pi = _cmul(ar8, ai8, prev[0], prev[1])
            nr = pr + hr_ref[rows, :]
            ni = pi + hi_ref[rows, :]
            hr_ref[rows, :] = nr
            hi_ref[rows, :] = ni
            return nr, ni

        lax.fori_loop(1, steps, scan, (hr_ref[0:SUBLANES, :], hi_ref[0:SUBLANES, :]), unroll=4)

        top = slice(chunk - SUBLANES, chunk)
        in_re, in_im, out_re, out_im = _segment_carries(
            hr_ref[top, :], hi_ref[top, :], pr_ref[top, :][0:1], pi_ref[top, :][0:1],
            carry_ref[0:1, :], carry_ref[1:2, :], reverse=False)
        carry_ref[0:1, :] = out_re
        carry_ref[1:2, :] = out_im
        inr_ref[...] = in_re
        ini_ref[...] = in_im

        def fix(t, _):
            rows = pl.ds(pl.multiple_of(t * SUBLANES, SUBLANES), SUBLANES)
            fr, fi = _cmul(pr_ref[rows, :], pi_ref[rows, :], in_re, in_im)
            hr_ref[rows, :] += fr
            hi_ref[rows, :] += fi
            return 0

        lax.fori_loop(0, steps, fix, 0, unroll=4)

        yseg_ref[...] = _dot(hr_ref[...].astype(BF16), cr_ref[0]) - _dot(hi_ref[...].astype(BF16), ci_ref[0])
        _from_segments(yseg_ref, y_ref, steps)
        y_ref[...] += dk_ref[...] * u_ref[...]

    row = pl.BlockSpec((1, blk), lambda j, c: (0, j))
    b_mat = pl.BlockSpec((1, LANES, blk), lambda j, c: (j, 0, 0))
    c_mat = pl.BlockSpec((1, blk, LANES), lambda j, c: (j, 0, 0))
    tok = pl.BlockSpec((chunk, LANES), lambda j, c: (c, j))
    state = pl.BlockSpec((chunk, blk), lambda j, c: (c, j))
    enter = pl.BlockSpec((SUBLANES, blk), lambda j, c: (c, j))
    return _hosted_call(
        body, carry, _edge_2d(N_SSM_BLOCKS, nc), name="ssm_fwd", grid=(N_SSM_BLOCKS, nc),
        in_specs=[tok, row, row, b_mat, b_mat, c_mat, c_mat, pl.BlockSpec((1, LANES), lambda j, c: (0, j))],
        out_specs=[tok, state, state, enter, enter],
        out_shape=[jax.ShapeDtypeStruct((T, SSM_W), F32), jax.ShapeDtypeStruct((T, STATES), F32),
                   jax.ShapeDtypeStruct((T, STATES), F32), jax.ShapeDtypeStruct((nc * SUBLANES, STATES), F32),
                   jax.ShapeDtypeStruct((nc * SUBLANES, STATES), F32)],
        scratch_shapes=[pltpu.VMEM((chunk, LANES), F32), pltpu.VMEM((chunk, LANES), F32),
                        pltpu.VMEM((chunk, blk), F32), pltpu.VMEM((chunk, blk), F32), pltpu.VMEM((SUBLANES, blk), F32)],
        compiler_params=_params(("arbitrary", "arbitrary"), VMEM_MID),
        inputs=(u, a_re, a_im, b_re, b_im, c_re, c_im, d_skip))


def _ssm_bwd(dy, u, h_re, h_im, in_re, in_im, a_re, a_im, b_re, b_im, c_re, c_im, d_skip, chunk, carry=None):
    T = u.shape[0]
    nc = T // chunk
    steps = chunk // SUBLANES
    blk = SSM_LANE_BLOCK

    def body(dy_ref, u_ref, hr_ref, hi_ref, inr_ref, ini_ref, ar_ref, ai_ref, br_ref, bi_ref, cr_ref, ci_ref, dk_ref,
             du_ref, dbr_ref, dbi_ref, dcr_ref, dci_ref, dar_ref, dai_ref, ddk_ref,
             dyseg_ref, useg_ref, duseg_ref, gr_ref, gi_ref, pr_ref, pi_ref, carry_ref, accr_ref, acci_ref):
        c = pl.program_id(1)
        ar, ai = ar_ref[...], ai_ref[...]

        @pl.when(c == 0)
        def _():
            _power_table(ar, ai, pr_ref, pi_ref, steps)
            carry_ref[...] = jnp.zeros_like(carry_ref)
            accr_ref[...] = jnp.zeros_like(accr_ref)
            acci_ref[...] = jnp.zeros_like(acci_ref)

        _to_segments(dy_ref, dyseg_ref, steps)
        _to_segments(u_ref, useg_ref, steps)
        dyb = dyseg_ref[...].astype(BF16)
        ub = useg_ref[...].astype(BF16)
        gr_ref[...] = _dot_nt(dyb, cr_ref[0])
        gi_ref[...] = -_dot_nt(dyb, ci_ref[0])
        dcr = _dot_tn(hr_ref[...].astype(BF16), dyb)
        dci = -_dot_tn(hi_ref[...].astype(BF16), dyb)
        ddk = jnp.sum(dy_ref[...] * u_ref[...], axis=0, keepdims=True)

        ar8 = jnp.broadcast_to(ar, (SUBLANES, blk))
        ai8 = jnp.broadcast_to(-ai, (SUBLANES, blk))

        def scan(k, nxt):
            t = steps - 2 - k
            rows = pl.ds(pl.multiple_of(t * SUBLANES, SUBLANES), SUBLANES)
            pr, pi = _cmul(ar8, ai8, nxt[0], nxt[1])
            nr = pr + gr_ref[rows, :]
            ni = pi + gi_ref[rows, :]
            gr_ref[rows, :] = nr
            gi_ref[rows, :] = ni
            return nr, ni

        top = slice(chunk - SUBLANES, chunk)
        lax.fori_loop(0, steps - 1, scan, (gr_ref[top, :], gi_ref[top, :]), unroll=4)

        gin_re, gin_im, out_re, out_im = _segment_carries(
            gr_ref[0:SUBLANES, :], gi_ref[0:SUBLANES, :], pr_ref[top, :][0:1], -pi_ref[top, :][0:1],
            carry_ref[0:1, :], carry_ref[1:2, :], reverse=True)
        carry_ref[0:1, :] = out_re
        carry_ref[1:2, :] = out_im

        def fix(t, acc):
            rows = pl.ds(pl.multiple_of(t * SUBLANES, SUBLANES), SUBLANES)
            prow = pl.ds(pl.multiple_of((steps - 1 - t) * SUBLANES, SUBLANES), SUBLANES)
            fr, fi = _cmul(pr_ref[prow, :], -pi_ref[prow, :], gin_re, gin_im)
            g_re = gr_ref[rows, :] + fr
            g_im = gi_ref[rows, :] + fi
            gr_ref[rows, :] = g_re
            gi_ref[rows, :] = g_im
            before = pl.ds(pl.multiple_of(jnp.maximum(t - 1, 0) * SUBLANES, SUBLANES), SUBLANES)
            first = t == 0
            hp_re = jnp.where(first, inr_ref[...], hr_ref[before, :])
            hp_im = jnp.where(first, ini_ref[...], hi_ref[before, :])
            return acc[0] + g_re * hp_re + g_im * hp_im, acc[1] + g_im * hp_re - g_re * hp_im

        acc_re, acc_im = lax.fori_loop(0, steps, fix, (accr_ref[...], acci_ref[...]), unroll=2)
        accr_ref[...] = acc_re
        acci_ref[...] = acc_im

        gbr = gr_ref[...].astype(BF16)
        gbi = gi_ref[...].astype(BF16)
        duseg_ref[...] = _dot_nt(gbr, br_ref[0]) + _dot_nt(gbi, bi_ref[0])
        _from_segments(duseg_ref, du_ref, steps)
        du_ref[...] += dk_ref[...] * dy_ref[...]
        dbr = _dot_tn(ub, gbr)
        dbi = _dot_tn(ub, gbi)

        @pl.when(c == 0)
        def _():
            dbr_ref[0] = dbr
            dbi_ref[0] = dbi
            dcr_ref[0] = dcr
            dci_ref[0] = dci
            ddk_ref[...] = ddk

        @pl.when(c > 0)
        def _():
            dbr_ref[0] += dbr
            dbi_ref[0] += dbi
            dcr_ref[0] += dcr
            dci_ref[0] += dci
            ddk_ref[...] += ddk

        @pl.when(c == nc - 1)
        def _():
            dar_ref[...] = jnp.sum(acc_re, axis=0, keepdims=True)
            dai_ref[...] = jnp.sum(acc_im, axis=0, keepdims=True)

    rev = lambda c: nc - 1 - c
    row = pl.BlockSpec((1, blk), lambda j, c: (0, j))
    b_mat = pl.BlockSpec((1, LANES, blk), lambda j, c: (j, 0, 0))
    c_mat = pl.BlockSpec((1, blk, LANES), lambda j, c: (j, 0, 0))
    tok = pl.BlockSpec((chunk, LANES), lambda j, c: (rev(c), j))
    state = pl.BlockSpec((chunk, blk), lambda j, c: (rev(c), j))
    enter = pl.BlockSpec((SUBLANES, blk), lambda j, c: (rev(c), j))
    chan = pl.BlockSpec((1, LANES), lambda j, c: (0, j))
    f32 = lambda *s: jax.ShapeDtypeStruct(s, F32)
    return _hosted_call(
        body, carry, _edge_2d(N_SSM_BLOCKS, nc), name="ssm_bwd", grid=(N_SSM_BLOCKS, nc),
        in_specs=[tok, tok, state, state, enter, enter, row, row, b_mat, b_mat, c_mat, c_mat, chan],
        out_specs=[tok, b_mat, b_mat, c_mat, c_mat, row, row, chan],
        out_shape=[f32(T, SSM_W), f32(N_SSM_BLOCKS, LANES, blk), f32(N_SSM_BLOCKS, LANES, blk),
                   f32(N_SSM_BLOCKS, blk, LANES), f32(N_SSM_BLOCKS, blk, LANES), f32(1, STATES), f32(1, STATES), f32(1, SSM_W)],
        scratch_shapes=[pltpu.VMEM((chunk, LANES), F32), pltpu.VMEM((chunk, LANES), F32), pltpu.VMEM((chunk, LANES), F32),
                        pltpu.VMEM((chunk, blk), F32), pltpu.VMEM((chunk, blk), F32),
                        pltpu.VMEM((chunk, blk), F32), pltpu.VMEM((chunk, blk), F32),
                        pltpu.VMEM((SUBLANES, blk), F32), pltpu.VMEM((SUBLANES, blk), F32), pltpu.VMEM((SUBLANES, blk), F32)],
        compiler_params=_params(("arbitrary", "arbitrary"), VMEM_BIG),
        inputs=(dy, u, h_re, h_im, in_re, in_im, a_re, a_im, b_re, b_im, c_re, c_im, d_skip))


def _merge_forward(y, att, ga, gs, w_glu, w_ssm, w_attn):
    z = jax.nn.gelu(y)
    zb = z.astype(BF16)
    gl = jax.nn.sigmoid(_dot(zb, w_glu))
    z2b = (z * gl).astype(BF16)
    y_ssm = _dot(z2b, w_ssm)
    y_attn = _dot(att, w_attn)
    sa = jax.nn.sigmoid(ga)
    ss = jax.nn.sigmoid(gs)
    merged = (sa * y_attn + ss * y_ssm).astype(BF16)
    return z, zb, gl, z2b, y_ssm, y_attn, sa, ss, merged


def _merge_fwd(x, y, att, ga, gs, g2, g3, w_glu, w_ssm, w_attn, w_out, tile):
    T = x.shape[0]

    def body(x_ref, y_ref, att_ref, ga_ref, gs_ref, g2_ref, g3_ref, wg_ref, ws_ref, wa_ref, wo_ref, x1_ref, o_ref, h2_ref):
        merged = _merge_forward(y_ref[...], att_ref[...], ga_ref[...], gs_ref[...], wg_ref[...], ws_ref[...], wa_ref[...])[-1]
        o = _dot(merged, wo_ref[...])
        x1 = x_ref[...] + o * _rms_scale(o) * g2_ref[...]
        o_ref[...] = o
        x1_ref[...] = x1
        h2_ref[...] = (x1 * _rms_scale(x1) * g3_ref[...]).astype(BF16)

    tok = lambda w: pl.BlockSpec((tile, w), lambda i: (i, 0))
    vec = _const_spec((1, D_MODEL))
    return pl.pallas_call(
        body, name="merge_fwd", grid=(T // tile,),
        in_specs=[tok(D_MODEL), tok(SSM_W), tok(ATTN_W), tok(D_MODEL), tok(D_MODEL), vec, vec,
                  _const_spec((SSM_W, SSM_W)), _const_spec((SSM_W, D_MODEL)), _const_spec((ATTN_W, D_MODEL)),
                  _const_spec((D_MODEL, D_MODEL))],
        out_specs=[tok(D_MODEL), tok(D_MODEL), tok(D_MODEL)],
        out_shape=[jax.ShapeDtypeStruct((T, D_MODEL), F32), jax.ShapeDtypeStruct((T, D_MODEL), F32),
                   jax.ShapeDtypeStruct((T, D_MODEL), BF16)],
        compiler_params=_params(("arbitrary",), VMEM_MID),
    )(x, y, att, ga, gs, g2, g3, w_glu, w_ssm, w_attn, w_out)


def _merge_bwd(dh2, dx2, x1, o, y, att, ga, gs, g2, g3, w_glu, w_ssm, w_attn, w_out, tile, carry=None):
    T = x1.shape[0]
    n_steps = T // tile

    def body(dh2_ref, dx2_ref, x1_ref, o_ref, y_ref, att_ref, ga_ref, gs_ref, g2_ref, g3_ref, wg_ref, ws_ref, wa_ref, wo_ref,
             dx1_ref, dga_ref, dgs_ref, datt_ref, dy_ref, dwg_hbm, dws_hbm, dwa_hbm, dwo_hbm, dg2_ref, dg3_ref,
             awg_ref, aws_ref, awa_ref, awo_ref):
        i = pl.program_id(0)
        x1v, ov = x1_ref[...], o_ref[...]
        dxn, dg3 = _rms_bwd(dh2_ref[...], x1v, _rms_scale(x1v), g3_ref[...])
        dx1 = dx2_ref[...] + dxn
        dx1_ref[...] = dx1
        do, dg2 = _rms_bwd(dx1, ov, _rms_scale(ov), g2_ref[...])
        dob = do.astype(BF16)

        yv = y_ref[...]
        att = att_ref[...]
        z, zb, gl, z2b, y_ssm, y_attn, sa, ss, merged = _merge_forward(
            yv, att, ga_ref[...], gs_ref[...], wg_ref[...], ws_ref[...], wa_ref[...])
        dmerged = _dot_nt(dob, wo_ref[...])
        dya = (dmerged * sa).astype(BF16)
        dys = (dmerged * ss).astype(BF16)
        dga_ref[...] = (dmerged * y_attn * sa * (1.0 - sa)).astype(BF16)
        dgs_ref[...] = (dmerged * y_ssm * ss * (1.0 - ss)).astype(BF16)
        datt_ref[...] = _dot_nt(dya, wa_ref[...]).astype(BF16)
        dz2 = _dot_nt(dys, ws_ref[...])
        dpre = (dz2 * z * gl * (1.0 - gl)).astype(BF16)
        dz = dz2 * gl + _dot_nt(dpre, wg_ref[...])
        _, gelu_vjp = jax.vjp(jax.nn.gelu, yv)
        dy_ref[...] = gelu_vjp(dz)[0]

        grads = ((awo_ref, _dot_tn(merged, dob)), (awa_ref, _dot_tn(att, dya)),
                 (aws_ref, _dot_tn(z2b, dys)), (awg_ref, _dot_tn(zb, dpre)), (dg2_ref, dg2), (dg3_ref, dg3))

        @pl.when(i == 0)
        def _():
            for ref, val in grads:
                ref[...] = val

        @pl.when(i > 0)
        def _():
            for ref, val in grads:
                ref[...] += val

        @pl.when(i == n_steps - 1)
        def _():
            pltpu.sync_copy(awg_ref, dwg_hbm)
            pltpu.sync_copy(aws_ref, dws_hbm)
            pltpu.sync_copy(awa_ref, dwa_hbm)
            pltpu.sync_copy(awo_ref, dwo_hbm)

    tok = lambda w: pl.BlockSpec((tile, w), lambda i: (i, 0))
    vec = _const_spec((1, D_MODEL))
    any_ = pl.BlockSpec(memory_space=pl.ANY)
    vec_out = pl.BlockSpec((1, D_MODEL), lambda i: (0, 0))
    f32 = lambda *s: jax.ShapeDtypeStruct(s, F32)
    bf = lambda *s: jax.ShapeDtypeStruct(s, BF16)
    return _hosted_call(
        body, carry, _edge_1d(n_steps), name="merge_bwd", grid=(n_steps,),
        in_specs=[tok(D_MODEL), tok(D_MODEL), tok(D_MODEL), tok(D_MODEL), tok(SSM_W), tok(ATTN_W), tok(D_MODEL), tok(D_MODEL),
                  vec, vec, _const_spec((SSM_W, SSM_W)), _const_spec((SSM_W, D_MODEL)), _const_spec((ATTN_W, D_MODEL)),
                  _const_spec((D_MODEL, D_MODEL))],
        out_specs=[tok(D_MODEL), tok(D_MODEL), tok(D_MODEL), tok(ATTN_W), tok(SSM_W), any_, any_, any_, any_, vec_out, vec_out],
        out_shape=[f32(T, D_MODEL), bf(T, D_MODEL), bf(T, D_MODEL), bf(T, ATTN_W), f32(T, SSM_W),
                   f32(SSM_W, SSM_W), f32(SSM_W, D_MODEL), f32(ATTN_W, D_MODEL), f32(D_MODEL, D_MODEL),
                   f32(1, D_MODEL), f32(1, D_MODEL)],
        scratch_shapes=[pltpu.VMEM((SSM_W, SSM_W), F32), pltpu.VMEM((SSM_W, D_MODEL), F32),
                        pltpu.VMEM((ATTN_W, D_MODEL), F32), pltpu.VMEM((D_MODEL, D_MODEL), F32)],
        compiler_params=_params(("arbitrary",), VMEM_BIG),
        inputs=(dh2, dx2, x1, o, y, att, ga, gs, g2, g3, w_glu, w_ssm, w_attn, w_out))


FF_SHARD = D_FF // N_DEV


def _mlp_fwd(h2, x1, target, g4, w_ff_in, w_ff_out, tile):
    T = h2.shape[0]
    per_step = 2
    ff_chunk = per_step * FF_SHARD
    n_i, n_k = T // tile, N_DEV // per_step

    def body(h2_ref, x1_ref, tg_ref, g4_ref, wi_ref, wo_ref, a_ref, dfo_ref, dx2_ref, loss_ref, dg4_ref, acc_ref):
        i, k = pl.program_id(0), pl.program_id(1)
        h2v = h2_ref[...]
        part = jnp.zeros((tile, D_MODEL), F32)
        for s in range(per_step):
            a = _dot(h2v, wi_ref[s])
            a_ref[:, s * FF_SHARD:(s + 1) * FF_SHARD] = a.astype(BF16)
            ra = jnp.maximum(a, 0.0)
            part = part + _dot((ra * ra).astype(BF16), wo_ref[s])

        @pl.when(k == 0)
        def _():
            acc_ref[...] = part

        @pl.when(k > 0)
        def _():
            acc_ref[...] += part

        @pl.when(k == n_k - 1)
        def _():
            f = acc_ref[...]
            r = _rms_scale(f)
            g = g4_ref[...]
            err = x1_ref[...] + f * r * g - tg_ref[...]
            dx2 = err * (1.0 / D_MODEL)
            dx2_ref[...] = dx2
            dfo, dg = _rms_bwd(dx2, f, r, g)
            dfo_ref[...] = dfo.astype(BF16)
            row = lax.broadcasted_iota(jnp.int32, (8, LANES), 0)
            col = lax.broadcasted_iota(jnp.int32, (8, LANES), 1)
            loss = jnp.where((row == 0) & (col == 0), (0.5 / D_MODEL) * jnp.sum(err * err), 0.0)

            @pl.when(i == 0)
            def _():
                loss_ref[...] = loss
                dg4_ref[...] = dg

            @pl.when(i > 0)
            def _():
                loss_ref[...] += loss
                dg4_ref[...] += dg

    tok = pl.BlockSpec((tile, D_MODEL), lambda i, k: (i, 0))
    return pl.pallas_call(
        body, name="mlp_fwd", grid=(n_i, n_k),
        in_specs=[tok, tok, tok, pl.BlockSpec((1, D_MODEL), lambda i, k: (0, 0)),
                  pl.BlockSpec((per_step, D_MODEL, FF_SHARD), lambda i, k: (k, 0, 0)),
                  pl.BlockSpec((per_step, FF_SHARD, D_MODEL), lambda i, k: (k, 0, 0))],
        out_specs=[pl.BlockSpec((tile, ff_chunk), lambda i, k: (i, k)), tok, tok,
                   pl.BlockSpec((8, LANES), lambda i, k: (0, 0)), pl.BlockSpec((1, D_MODEL), lambda i, k: (0, 0))],
        out_shape=[jax.ShapeDtypeStruct((T, D_FF), BF16), jax.ShapeDtypeStruct((T, D_MODEL), BF16),
                   jax.ShapeDtypeStruct((T, D_MODEL), F32), jax.ShapeDtypeStruct((8, LANES), F32),
                   jax.ShapeDtypeStruct((1, D_MODEL), F32)],
        scratch_shapes=[pltpu.VMEM((tile, D_MODEL), F32)],
        compiler_params=_params(("arbitrary", "arbitrary"), VMEM_BIG),
    )(h2, x1, target, g4, w_ff_in, w_ff_out)


def _mlp_bwd(dfo, a, h2, w_ff_in, w_ff_out, tile):
    T = h2.shape[0]
    ff_chunk = FF_SHARD
    n_i, n_k = T // tile, N_DEV

    def body(dfo_ref, a_ref, h2_ref, wi_ref, wo_ref, dwi_ref, dwo_ref, dh2_ref, acc_ref):
        k, i = pl.program_id(0), pl.program_id(1)
        dfo = dfo_ref[...]
        ra = jnp.maximum(a_ref[...].astype(F32), 0.0)
        drr = _dot_nt(dfo, wo_ref[0])
        da = (drr * (2.0 * ra)).astype(BF16)
        dwo = _dot_tn((ra * ra).astype(BF16), dfo)
        dwi = _dot_tn(h2_ref[...], da)
        part = _dot_nt(da, wi_ref[0])
        rows = pl.ds(pl.multiple_of(i * tile, tile), tile)

        @pl.when(i == 0)
        def _():
            dwi_ref[0] = dwi
            dwo_ref[0] = dwo

        @pl.when(i > 0)
        def _():
            dwi_ref[0] += dwi
            dwo_ref[0] += dwo

        @pl.when(k == 0)
        def _():
            acc_ref[rows, :] = part

        @pl.when((k > 0) & (k < n_k - 1))
        def _():
            acc_ref[rows, :] += part

        @pl.when(k == n_k - 1)
        def _():
            dh2_ref[...] = acc_ref[rows, :] + part

    return pl.pallas_call(
        body, name="mlp_bwd", grid=(n_k, n_i),
        in_specs=[pl.BlockSpec((tile, D_MODEL), lambda k, i: (i, 0)), pl.BlockSpec((tile, ff_chunk), lambda k, i: (i, k)),
                  pl.BlockSpec((tile, D_MODEL), lambda k, i: (i, 0)),
                  pl.BlockSpec((1, D_MODEL, ff_chunk), lambda k, i: (k, 0, 0)),
                  pl.BlockSpec((1, ff_chunk, D_MODEL), lambda k, i: (k, 0, 0))],
        out_specs=[pl.BlockSpec((1, D_MODEL, ff_chunk), lambda k, i: (k, 0, 0)),
                   pl.BlockSpec((1, ff_chunk, D_MODEL), lambda k, i: (k, 0, 0)),
                   pl.BlockSpec((tile, D_MODEL), lambda k, i: (jnp.where(k == n_k - 1, i, 0), 0))],
        out_shape=[jax.ShapeDtypeStruct((N_DEV, D_MODEL, ff_chunk), F32), jax.ShapeDtypeStruct((N_DEV, ff_chunk, D_MODEL), F32),
                   jax.ShapeDtypeStruct((T, D_MODEL), F32)],
        scratch_shapes=[pltpu.VMEM((T, D_MODEL), F32)],
        compiler_params=_params(("arbitrary", "arbitrary"), VMEM_BIG),
    )(dfo, a, h2, w_ff_in, w_ff_out)


def _block_diag_in(b):
    bt = jnp.transpose(b, (0, 2, 1)).reshape(N_SSM_BLOCKS, 8, GROUP_CH, N_STATE)
    eye = jnp.eye(8, dtype=b.dtype)
    return jnp.einsum("jacp,ab->jacbp", bt, eye).reshape(N_SSM_BLOCKS, LANES, SSM_LANE_BLOCK)


def _block_diag_in_grad(g):
    g = g.reshape(N_SSM_BLOCKS, 8, GROUP_CH, 8, N_STATE)
    d = jnp.diagonal(g, axis1=1, axis2=3)
    return jnp.transpose(d, (0, 3, 2, 1)).reshape(N_GROUPS, N_STATE, GROUP_CH)


def _block_diag_out(c):
    ct = c.reshape(N_SSM_BLOCKS, 8, GROUP_CH, N_STATE)
    eye = jnp.eye(8, dtype=c.dtype)
    return jnp.einsum("jacp,ab->japbc", ct, eye).reshape(N_SSM_BLOCKS, SSM_LANE_BLOCK, LANES)


def _block_diag_out_grad(g):
    g = g.reshape(N_SSM_BLOCKS, 8, N_STATE, 8, GROUP_CH)
    d = jnp.diagonal(g, axis1=1, axis2=3)
    return jnp.transpose(d, (0, 3, 2, 1)).reshape(N_GROUPS, GROUP_CH, N_STATE)


def _tiles(T):
    return dict(proj=min(512, T), proj_bwd=min(256, T), merge=min(512, T), merge_bwd=min(256, T),
                mlp_fwd=min(512, T), mlp_bwd=min(512, T), ssm_chunk=min(1024, T))


def _mesh_position():
    x, y, c = lax.axis_index("x"), lax.axis_index("y"), lax.axis_index("c")
    other_chips = [(1 - x, y), (x, 1 - y), (1 - x, 1 - y)]
    return x, y, c, other_chips


def _gather_carry(arrays):
    n = len(arrays)

    def copies(ins, outs, sems):
        send_sems, recv_sems, local_sems = sems
        x, y, c, chips = _mesh_position()
        me, sibling = (x, y, c), (x, y, 1 - c)

        def copy(a, k, block, to, src=None):
            px, py, pc = block
            dst = outs[a].at[4 * px + 2 * py + pc]
            return pltpu.make_async_remote_copy(
                src_ref=dst if src is None else src, dst_ref=dst, send_sem=send_sems.at[7 * a + k],
                recv_sem=recv_sems.at[7 * a + k], device_id=to, device_id_type=MESH_IDS)

        mine = [pltpu.make_async_copy(ins[a], outs[a].at[4 * x + 2 * y + c], local_sems.at[a]) for a in range(n)]
        first = []
        for a in range(n):
            first.append(copy(a, 0, me, sibling, src=ins[a]))
            first += [copy(a, 1 + j, me, (*chip, c), src=ins[a]) for j, chip in enumerate(chips)]
        return copy, mine, first, me, sibling, chips, c

    def start(ins, outs, sems):
        _, mine, first, *_ = copies(ins, outs, sems)
        for cp in mine + first:
            cp.start()

    def finish(ins, outs, sems):
        copy, mine, first, me, sibling, chips, c = copies(ins, outs, sems)
        passed = []
        for a in range(n):
            for j, chip in enumerate(chips):
                copy(a, 1 + j, (*chip, c), me).wait_recv()
                passed.append(copy(a, 4 + j, (*chip, c), sibling))
                passed[-1].start()
        for a in range(n):
            copy(a, 0, sibling, me).wait_recv()
            for j, chip in enumerate(chips):
                copy(a, 4 + j, (*chip, 1 - c), me).wait_recv()
        for cp in first + passed:
            cp.wait_send()
        for cp in mine:
            cp.wait()

    return _Carry(arrays, [jax.ShapeDtypeStruct((N_DEV,) + a.shape, a.dtype) for a in arrays],
                  [pltpu.SemaphoreType.DMA((7 * n,)), pltpu.SemaphoreType.DMA((7 * n,)), pltpu.SemaphoreType.DMA((n,))],
                  start, finish)


def _pairwise_carry(arrays, n_slots, make_copies):
    n = len(arrays)

    def start(ins, outs, sems):
        for cp in make_copies(ins, outs, sems):
            cp.start()

    def finish(ins, outs, sems):
        for cp in make_copies(ins, outs, sems):
            cp.wait()

    return _Carry(arrays, [jax.ShapeDtypeStruct((n_slots,) + a.shape[1:], a.dtype) for a in arrays],
                  [pltpu.SemaphoreType.DMA((n_slots * n,)), pltpu.SemaphoreType.DMA((n_slots * n,))], start, finish)


def _sibling_carry(grads):
    def make_copies(ins, outs, sems):
        x, y, c, _ = _mesh_position()
        return [pltpu.make_async_remote_copy(
            src_ref=ins[a].at[2 * ch + (1 - c)], dst_ref=outs[a].at[ch], send_sem=sems[0].at[4 * a + ch],
            recv_sem=sems[1].at[4 * a + ch], device_id=(x, y, 1 - c), device_id_type=MESH_IDS)
            for a in range(len(grads)) for ch in range(4)]

    return _pairwise_carry(grads, 4, make_copies)


def _chips_carry(sums):
    def make_copies(ins, outs, sems):
        x, y, c, chips = _mesh_position()
        return [pltpu.make_async_remote_copy(
            src_ref=ins[a].at[2 * px + py], dst_ref=outs[a].at[j], send_sem=sems[0].at[3 * a + j],
            recv_sem=sems[1].at[3 * a + j], device_id=(px, py, c), device_id_type=MESH_IDS)
            for a in range(len(sums)) for j, (px, py) in enumerate(chips)]

    return _pairwise_carry(sums, 3, make_copies)


def _row_tile(rows, cols):
    t = max(8, min(rows, (1 << 18) // cols // 8 * 8))
    while rows % t:
        t -= 8
    return t


def _add_sibling(grads8, recv, core, name):
    _, R, C = grads8.shape
    tr = _row_tile(R, C)
    g4 = grads8.reshape(4, 2, R, C)

    def body(core_ref, g_ref, r_ref, o_ref, ob_ref):
        s = g_ref[0] + r_ref[...]
        o_ref[...] = s
        ob_ref[...] = s.astype(BF16)

    out = pl.BlockSpec((1, tr, C), lambda ch, r, core_ref: (ch, r, 0))
    return pl.pallas_call(
        body, name=name,
        grid_spec=pltpu.PrefetchScalarGridSpec(
            num_scalar_prefetch=1, grid=(4, R // tr),
            in_specs=[pl.BlockSpec((1, 1, tr, C), lambda ch, r, core_ref: (ch, core_ref[0], r, 0)),
                      pl.BlockSpec((1, tr, C), lambda ch, r, core_ref: (ch, r, 0))],
            out_specs=[out, out]),
        out_shape=[jax.ShapeDtypeStruct((4, R, C), F32), jax.ShapeDtypeStruct((4, R, C), BF16)],
        compiler_params=_params(("arbitrary", "arbitrary")),
    )(core, g4, recv)


def _adam_math(w, g, m, v):
    m = ADAM_B1 * m + (1.0 - ADAM_B1) * g
    v = ADAM_B2 * v + (1.0 - ADAM_B2) * jnp.square(g)
    m_hat = m / (1.0 - ADAM_B1 ** ADAM_STEP)
    v_hat = v / (1.0 - ADAM_B2 ** ADAM_STEP)
    delta = -ADAM_LR * (m_hat / (jnp.sqrt(v_hat) + ADAM_EPS) + ADAM_WD * w)
    return delta, m, v


def _adam_big(w, m, v, chip_sums, recv, chip, name):
    R, C = w.shape
    tr = _row_tile(R, C)

    def body(chip_ref, w_ref, m_ref, v_ref, s_ref, r_ref, g_ref, d_ref, nm_ref, nv_ref):
        g = s_ref[0] + r_ref[0].astype(F32) + r_ref[1].astype(F32) + r_ref[2].astype(F32)
        g_ref[...] = g
        d_ref[...], nm_ref[...], nv_ref[...] = _adam_math(w_ref[...], g, m_ref[...], v_ref[...])

    blk = pl.BlockSpec((tr, C), lambda r, chip_ref: (r, 0))
    return pl.pallas_call(
        body, name=name,
        grid_spec=pltpu.PrefetchScalarGridSpec(
            num_scalar_prefetch=1, grid=(R // tr,),
            in_specs=[blk, blk, blk, pl.BlockSpec((1, tr, C), lambda r, chip_ref: (chip_ref[0], r, 0)),
                      pl.BlockSpec((3, tr, C), lambda r, chip_ref: (0, r, 0))],
            out_specs=[blk] * 4),
        out_shape=[jax.ShapeDtypeStruct((R, C), F32)] * 4,
        compiler_params=_params(("arbitrary",)),
    )(chip, w, m, v, chip_sums, recv)


def _adam_small(w, m, v, partials):
    R = w.shape[0]

    def body(w_ref, m_ref, v_ref, p_ref, g_ref, d_ref, nm_ref, nv_ref):
        g = p_ref[0]
        for d in range(1, N_DEV):
            g = g + p_ref[d]
        g_ref[...] = g
        d_ref[...], nm_ref[...], nv_ref[...] = _adam_math(w_ref[...], g, m_ref[...], v_ref[...])

    return pl.pallas_call(body, name="adam_small", out_shape=[jax.ShapeDtypeStruct((R, LANES), F32)] * 4,
                          compiler_params=_params(None, VMEM_MID))(w, m, v, partials)


PACK_QUANTUM = SUBLANES * LANES


def _pack(named, names):
    parts = []
    for nme in names:
        flat = named[nme].reshape(-1)
        parts.append(jnp.pad(flat, (0, -flat.size % PACK_QUANTUM)))
    return jnp.concatenate(parts).reshape(-1, LANES)


def _unpack(packed, shapes, names):
    flat = packed.reshape(-1)
    out, pos = {}, 0
    for nme in names:
        size = math.prod(shapes[nme])
        out[nme] = flat[pos:pos + size].reshape(shapes[nme])
        pos += size + (-size % PACK_QUANTUM)
    return out


BIG = ("w_in", "w_glu", "w_attn_branch", "w_ssm_branch", "w_out", "w_ff_in", "w_ff_out")
COLUMN_SHARDED = ("w_in", "w_attn_branch", "w_ssm_branch", "w_ff_in")
SMALL = ("norm_mix_pre", "norm_mix_post", "norm_mlp_pre", "norm_mlp_post", "rel_bias", "sinks", "lam_re", "lam_im",
         "log_dt", "b_re", "b_im", "c_re", "c_im", "d_skip")
ALL_WEIGHTS = ("norm_mix_pre", "norm_mix_post", "norm_mlp_pre", "norm_mlp_post", "w_in", "rel_bias", "sinks", "lam_re",
               "lam_im", "log_dt", "b_re", "b_im", "c_re", "c_im", "d_skip", "w_glu", "w_attn_branch", "w_ssm_branch",
               "w_out", "w_ff_in", "w_ff_out")


def _full_from_gathered(name, gathered):
    _, r, c = gathered.shape
    if name in COLUMN_SHARDED:
        return jnp.transpose(gathered, (1, 0, 2)).reshape(r, N_DEV * c)
    return gathered.reshape(N_DEV * r, c)


def _blocks_from_full(name, full):
    r, c = full.shape
    if name in COLUMN_SHARDED:
        return jnp.transpose(full.reshape(r, N_DEV, c // N_DEV), (1, 0, 2))
    return full.reshape(N_DEV, r // N_DEV, c)


def kernel(x, norm_mix_pre, norm_mix_post, norm_mlp_pre, norm_mlp_post, w_in, rel_bias, sinks, lam_re, lam_im, log_dt, b_re, b_im, c_re, c_im, d_skip, w_glu, w_attn_branch, w_ssm_branch, w_out, w_ff_in, w_ff_out, loss_target, m_norm_mix_pre, m_norm_mix_post, m_norm_mlp_pre, m_norm_mlp_post, m_w_in, m_rel_bias, m_sinks, m_lam_re, m_lam_im, m_log_dt, m_b_re, m_b_im, m_c_re, m_c_im, m_d_skip, m_w_glu, m_w_attn_branch, m_w_ssm_branch, m_w_out, m_w_ff_in, m_w_ff_out, v_norm_mix_pre, v_norm_mix_post, v_norm_mlp_pre, v_norm_mlp_post, v_w_in, v_rel_bias, v_sinks, v_lam_re, v_lam_im, v_log_dt, v_b_re, v_b_im, v_c_re, v_c_im, v_d_skip, v_w_glu, v_w_attn_branch, v_w_ssm_branch, v_w_out, v_w_ff_in, v_w_ff_out):
    args = dict(locals())
    w = {n: args[n] for n in ALL_WEIGHTS}
    m = {n: args["m_" + n] for n in ALL_WEIGHTS}
    v = {n: args["v_" + n] for n in ALL_WEIGHTS}
    core = lax.axis_index("c").astype(jnp.int32).reshape(1)
    chip = (2 * lax.axis_index("x") + lax.axis_index("y")).astype(jnp.int32).reshape(1)
    xs, target = x[0], loss_target[0]
    t = _tiles(xs.shape[0])
    shard = {n: w[n][0].astype(BF16) for n in BIG}
    small = {n: (w[n] if n == "rel_bias" else w[n][0]) for n in SMALL}
    g1, g2, g3, g4 = (small[n].reshape(1, D_MODEL) for n in ("norm_mix_pre", "norm_mix_post", "norm_mlp_pre", "norm_mlp_post"))
    bucket = jnp.asarray(_bucket_table())
    rel_b, sink = small["rel_bias"], small["sinks"].reshape(1, N_HEADS)
    lam_r, lam_i = small["lam_re"].reshape(1, STATES), small["lam_im"].reshape(1, STATES)
    ldt_rep = jnp.repeat(small["log_dt"].reshape(N_GROUPS), N_STATE).reshape(1, STATES)
    bd_re, bd_im = _block_diag_in(small["b_re"]), _block_diag_in(small["b_im"])
    cm_re, cm_im = _block_diag_out(small["c_re"]).astype(BF16), _block_diag_out(small["c_im"]).astype(BF16)
    dsk = small["d_skip"].reshape(1, SSM_W)

    (g_in,) = _run_carry(_gather_carry([shard["w_in"]]), "gather_w_in")
    wf_in = _full_from_gathered("w_in", g_in)
    merge_names = ("w_glu", "w_attn_branch", "w_ssm_branch", "w_out")
    (q, k, vv, u, ga, gs), gathered = _in_proj_fwd(xs, g1, wf_in, t["proj"], _gather_carry([shard[n] for n in merge_names]))
    wf = {n: _full_from_gathered(n, g) for n, g in zip(merge_names, gathered)}
    (att,), (wf_ff_in,) = _attn_fwd(q, k, vv, bucket, rel_b, sink, _gather_carry([shard["w_ff_in"]]))
    a_re, a_im, bm_re, bm_im = _ssm_prep(lam_r, lam_i, ldt_rep, bd_re, bd_im)
    (y, h_re, h_im, in_re, in_im), (wf_ff_out,) = _ssm_fwd(
        u, a_re, a_im, bm_re, bm_im, cm_re, cm_im, dsk, t["ssm_chunk"], _gather_carry([shard["w_ff_out"]]))
    x1, o, h2 = _merge_fwd(xs, y, att, ga, gs, g2, g3, wf["w_glu"], wf["w_ssm_branch"], wf["w_attn_branch"], wf["w_out"],
                           t["merge"])
    a, dfo, dx2, loss_blk, dg4 = _mlp_fwd(h2, x1, target, g4, wf_ff_in, wf_ff_out, t["mlp_fwd"])
    loss = lax.psum(loss_blk[0, 0], ("x", "y", "c"))

    def add_sibling(names, blocks, received):
        pairs = [_add_sibling(b, r, core, "add_sibling_" + n) for n, b, r in zip(names, blocks, received)]
        return [p[0] for p in pairs], [p[1] for p in pairs]

    ff_names = ("w_ff_in", "w_ff_out")
    dw_ff_in, dw_ff_out, dh2 = _mlp_bwd(dfo, a, h2, wf_ff_in, wf_ff_out, t["mlp_bwd"])
    ff_blocks = [dw_ff_in, dw_ff_out]
    (dx1, dga, dgs, datt, dy, dw_glu, dw_ssm, dw_attn, dw_out, dg2, dg3), ff_recv = _merge_bwd(
        dh2, dx2, x1, o, y, att, ga, gs, g2, g3, wf["w_glu"], wf["w_ssm_branch"], wf["w_attn_branch"], wf["w_out"],
        t["merge_bwd"], _sibling_carry(ff_blocks))
    ff_sums, ff_sums_bf = add_sibling(ff_names, ff_blocks, ff_recv)
    (du, dbm_re, dbm_im, dcm_re, dcm_im, da_re, da_im, dd_skip), ff_from_chips = _ssm_bwd(
        dy, u, h_re, h_im, in_re, in_im, a_re, a_im, bm_re, bm_im, cm_re, cm_im, dsk, t["ssm_chunk"], _chips_carry(ff_sums_bf))
    dbd_re, dbd_im, dlam_re, dlam_im, dldt_rep = _ssm_prep_bwd(lam_r, lam_i, ldt_rep, bd_re, bd_im, dbm_re, dbm_im, da_re, da_im)
    dlog_dt = _group_sum(dldt_rep.reshape(N_GROUPS, N_STATE))
    merge_blocks = [_blocks_from_full(n, g) for n, g in zip(merge_names, (dw_glu, dw_attn, dw_ssm, dw_out))]
    (dq, dk, dv, attn_small), merge_recv = _attn_bwd(q, k, vv, datt, bucket, rel_b, sink, _sibling_carry(merge_blocks))
    merge_sums, merge_sums_bf = add_sibling(merge_names, merge_blocks, merge_recv)
    (grad_x, dw_in, dg1), merge_from_chips = _in_proj_bwd(
        xs, g1, wf_in, dx1, (dq, dk, dv, du.astype(BF16), dga, dgs), t["proj_bwd"], _chips_carry(merge_sums_bf))

    small_grads = dict(
        norm_mix_pre=dg1, norm_mix_post=dg2, norm_mlp_pre=dg3, norm_mlp_post=dg4,
        rel_bias=attn_small[:N_BUCKETS, :N_HEADS], sinks=attn_small[N_BUCKETS:N_BUCKETS + 1, :N_HEADS],
        lam_re=dlam_re, lam_im=dlam_im, log_dt=dlog_dt,
        b_re=_block_diag_in_grad(dbd_re), b_im=_block_diag_in_grad(dbd_im),
        c_re=_block_diag_out_grad(dcm_re), c_im=_block_diag_out_grad(dcm_im), d_skip=dd_skip)
    shapes = {n: w[n].shape for n in SMALL}
    packed_partial = _pack({n: small_grads[n].reshape(shapes[n]) for n in SMALL}, SMALL)

    in_blocks = [_blocks_from_full("w_in", dw_in)]
    in_recv = _run_carry(_sibling_carry(in_blocks), "reduce_sibling_w_in")
    in_sums, in_sums_bf = add_sibling(("w_in",), in_blocks, in_recv)
    in_from_chips, partials = _run_carry(_join(_chips_carry(in_sums_bf), _gather_carry([packed_partial])), "reduce_chips_w_in")

    grads, deltas, new_m, new_v = {}, {}, {}, {}
    sums = dict(zip(ff_names + merge_names + ("w_in",), ff_sums + merge_sums + in_sums))
    received = dict(zip(ff_names + merge_names + ("w_in",), ff_from_chips + merge_from_chips + [in_from_chips]))
    for n in BIG:
        outs = _adam_big(w[n][0], m[n][0], v[n][0], sums[n], received[n], chip, "adam_" + n)
        grads[n], deltas[n], new_m[n], new_v[n] = (o[None] for o in outs)

    packed = _adam_small(_pack(w, SMALL), _pack(m, SMALL), _pack(v, SMALL), partials)
    for store, p in zip((grads, deltas, new_m, new_v), packed):
        store.update(_unpack(p, shapes, SMALL))

    return (loss, grad_x[None], *[grads[n] for n in ALL_WEIGHTS], *[deltas[n] for n in ALL_WEIGHTS],
            *[new_m[n] for n in ALL_WEIGHTS], *[new_v[n] for n in ALL_WEIGHTS])
```

```python
import functools
import math

import jax
import jax.numpy as jnp
import numpy as np
from jax import lax
from jax.experimental import pallas as pl
from jax.experimental.pallas import tpu as pltpu

F32 = jnp.float32
BF16 = jnp.bfloat16

D_MODEL = 1024
N_HEADS = 8
HEAD_DIM = 64
ATTN_W = 512
KV_W = 128
BLOCK = 128
N_BUCKETS = 32
SSM_W = 512
N_GROUPS = 32
N_STATE = 64
GROUP_CH = 16
STATES = N_GROUPS * N_STATE
D_FF = 4096
IN_W = 3328
SPLITS = (0, 512, 640, 768, 1280, 2304, 3328)
RMS_EPS = 1e-6
NEG_INF = -1e30
SUBLANES = 8
LANES = 128
SSM_LANE_BLOCK = 512
N_SSM_BLOCKS = STATES // SSM_LANE_BLOCK
VMEM_BIG = 52 * 1024 * 1024
VMEM_MID = 40 * 1024 * 1024

ADAM_LR = 0.001
ADAM_B1 = 0.9
ADAM_B2 = 0.999
ADAM_EPS = 1e-08
ADAM_WD = 0.01
ADAM_STEP = 10

N_DEV = 8


def _dot(a, b):
    return jnp.dot(a, b, preferred_element_type=F32)


def _dot_nt(a, b):
    return lax.dot_general(a, b, (((1,), (1,)), ((), ())), preferred_element_type=F32)


def _dot_tn(a, b):
    return lax.dot_general(a, b, (((0,), (0,)), ((), ())), preferred_element_type=F32)


def _rms_scale(x):
    return lax.rsqrt(jnp.mean(x * x, axis=-1, keepdims=True) + RMS_EPS)


def _rms_bwd(dy, x, r, g):
    t = dy * g
    dx = r * t - x * (r * r * r) * jnp.mean(t * x, axis=-1, keepdims=True)
    dg = jnp.sum(dy * x * r, axis=0, keepdims=True)
    return dx, dg


def _const_spec(shape):
    nd = len(shape)
    return pl.BlockSpec(shape, lambda *_: (0,) * nd, pipeline_mode=pl.Buffered(1))


def _params(sem, vmem=None):
    return pltpu.CompilerParams(dimension_semantics=sem, vmem_limit_bytes=vmem)


MESH_IDS = pl.DeviceIdType.MESH
HBM_SPEC = pl.BlockSpec(memory_space=pl.ANY)


class _Carry:
    def __init__(self, inputs, out_shapes, sems, start, finish):
        self.inputs, self.out_shapes, self.sems, self.start, self.finish = list(inputs), list(out_shapes), list(sems), start, finish


def _join(a, b):
    na_in, na_out, na_sem = len(a.inputs), len(a.out_shapes), len(a.sems)

    def start(ins, outs, sems):
        a.start(ins[:na_in], outs[:na_out], sems[:na_sem])
        b.start(ins[na_in:], outs[na_out:], sems[na_sem:])

    def finish(ins, outs, sems):
        a.finish(ins[:na_in], outs[:na_out], sems[:na_sem])
        b.finish(ins[na_in:], outs[na_out:], sems[na_sem:])

    return _Carry(a.inputs + b.inputs, a.out_shapes + b.out_shapes, a.sems + b.sems, start, finish)


def _hosted_call(body, carry, edge, *, name, grid, in_specs, out_specs, out_shape, scratch_shapes, compiler_params, inputs):
    n_in, n_out = len(in_specs), len(out_specs)
    if carry is None:
        outs = pl.pallas_call(body, name=name, grid=grid, in_specs=in_specs, out_specs=out_specs, out_shape=out_shape,
                              scratch_shapes=scratch_shapes, compiler_params=compiler_params)(*inputs)
        return list(outs), []
    c_in, c_out, c_sem = len(carry.inputs), len(carry.out_shapes), len(carry.sems)

    def wrapped(*refs):
        ins, refs = refs[:n_in], refs[n_in:]
        cins, refs = refs[:c_in], refs[c_in:]
        outs, refs = refs[:n_out], refs[n_out:]
        couts, refs = refs[:c_out], refs[c_out:]
        scratch, csems = refs[:len(refs) - c_sem], refs[len(refs) - c_sem:]
        first, last = edge()

        @pl.when(first)
        def _():
            carry.start(cins, couts, csems)

        body(*ins, *outs, *scratch)

        @pl.when(last)
        def _():
            carry.finish(cins, couts, csems)

    outs = pl.pallas_call(
        wrapped, name=name, grid=grid, in_specs=list(in_specs) + [HBM_SPEC] * c_in,
        out_specs=list(out_specs) + [HBM_SPEC] * c_out, out_shape=list(out_shape) + carry.out_shapes,
        scratch_shapes=list(scratch_shapes) + carry.sems, compiler_params=compiler_params)(*inputs, *carry.inputs)
    return list(outs[:n_out]), list(outs[n_out:])


def _edge_1d(n_steps):
    return lambda: (pl.program_id(0) == 0, pl.program_id(0) == n_steps - 1)


def _edge_2d(n0, n1):
    return lambda: ((pl.program_id(0) == 0) & (pl.program_id(1) == 0),
                    (pl.program_id(0) == n0 - 1) & (pl.program_id(1) == n1 - 1))


def _run_carry(carry, name):
    c_in, c_out = len(carry.inputs), len(carry.out_shapes)

    def body(*refs):
        ins, outs, sems = refs[:c_in], refs[c_in:c_in + c_out], refs[c_in + c_out:]
        carry.start(ins, outs, sems)
        carry.finish(ins, outs, sems)

    return pl.pallas_call(body, name=name, in_specs=[HBM_SPEC] * c_in, out_specs=[HBM_SPEC] * c_out,
                          out_shape=carry.out_shapes, scratch_shapes=carry.sems)(*carry.inputs)


def _in_proj_fwd(x, g1, w_in, tile, carry=None):
    T = x.shape[0]

    def body(x_ref, g_ref, w_ref, q_ref, k_ref, v_ref, u_ref, ga_ref, gs_ref):
        xv = x_ref[...]
        h = (xv * _rms_scale(xv) * g_ref[...]).astype(BF16)
        outs = (q_ref, k_ref, v_ref, u_ref, ga_ref, gs_ref)
        for p, o_ref in enumerate(outs):
            o_ref[...] = _dot(h, w_ref[:, SPLITS[p]:SPLITS[p + 1]]).astype(o_ref.dtype)

    widths = [SPLITS[p + 1] - SPLITS[p] for p in range(6)]
    dtypes = [BF16, BF16, BF16, F32, F32, F32]
    return _hosted_call(
        body, carry, _edge_1d(T // tile), name="in_proj_fwd", grid=(T // tile,),
        in_specs=[pl.BlockSpec((tile, D_MODEL), lambda i: (i, 0)), _const_spec((1, D_MODEL)), _const_spec((D_MODEL, IN_W))],
        out_specs=[pl.BlockSpec((tile, w), lambda i: (i, 0)) for w in widths],
        out_shape=[jax.ShapeDtypeStruct((T, w), dt) for w, dt in zip(widths, dtypes)],
        scratch_shapes=[], compiler_params=_params(("arbitrary",), VMEM_MID), inputs=(x, g1, w_in))


def _in_proj_bwd(x, g1, w_in, dx1, dparts, tile, carry=None):
    T = x.shape[0]
    widths = [SPLITS[p + 1] - SPLITS[p] for p in range(6)]
    n_steps = T // tile

    def body(x_ref, g_ref, w_ref, dx1_ref, dq, dk, dv, du, dga, dgs, gx_ref, dw_hbm, dg_ref, acc_ref):
        i = pl.program_id(0)
        xv = x_ref[...]
        r = _rms_scale(xv)
        g = g_ref[...]
        h = (xv * r * g).astype(BF16)
        dh = jnp.zeros((tile, D_MODEL), F32)
        for p, d_ref in enumerate((dq, dk, dv, du, dga, dgs)):
            dp = d_ref[...]
            cols = slice(SPLITS[p], SPLITS[p + 1])
            dh = dh + _dot_nt(dp, w_ref[:, cols])
            contrib = _dot_tn(h, dp)

            @pl.when(i == 0)
            def _():
                acc_ref[:, cols] = contrib

            @pl.when(i > 0)
            def _():
                acc_ref[:, cols] += contrib

        dxn, dg = _rms_bwd(dh, xv, r, g)
        gx_ref[...] = dx1_ref[...] + dxn

        @pl.when(i == 0)
        def _():
            dg_ref[...] = dg

        @pl.when(i > 0)
        def _():
            dg_ref[...] += dg

        @pl.when(i == n_steps - 1)
        def _():
            pltpu.sync_copy(acc_ref, dw_hbm)

    tok = lambda w: pl.BlockSpec((tile, w), lambda i: (i, 0))
    return _hosted_call(
        body, carry, _edge_1d(n_steps), name="in_proj_bwd", grid=(n_steps,),
        in_specs=[tok(D_MODEL), _const_spec((1, D_MODEL)), _const_spec((D_MODEL, IN_W)), tok(D_MODEL)] + [tok(w) for w in widths],
        out_specs=[tok(D_MODEL), HBM_SPEC, pl.BlockSpec((1, D_MODEL), lambda i: (0, 0))],
        out_shape=[jax.ShapeDtypeStruct((T, D_MODEL), F32), jax.ShapeDtypeStruct((D_MODEL, IN_W), F32),
                   jax.ShapeDtypeStruct((1, D_MODEL), F32)],
        scratch_shapes=[pltpu.VMEM((D_MODEL, IN_W), F32)],
        compiler_params=_params(("arbitrary",), VMEM_BIG), inputs=(x, g1, w_in, dx1, *dparts))


def _bucket_table():
    qi = np.arange(BLOCK)[:, None]
    kj = np.arange(2 * BLOCK)[None, :]
    dist = qi + BLOCK - kj
    max_exact = N_BUCKETS // 2
    d = np.maximum(dist, 0)
    df = np.maximum(d, 1).astype(np.float32)
    large = max_exact + (np.log(df / np.float32(max_exact)) / np.float32(math.log(BLOCK / max_exact))
                         * np.float32(N_BUCKETS - max_exact)).astype(np.int32)
    large = np.minimum(large, N_BUCKETS - 1)
    bucket = np.where(d < max_exact, d, large)
    return np.where((dist >= 0) & (dist < BLOCK), bucket, -1).astype(np.int32)


def _build_bias(bucket_ref, rb_ref, bias_ref):
    bk = bucket_ref[...]
    for h in range(N_HEADS):
        def add(b, acc, h=h):
            return acc + jnp.where(bk == b, rb_ref[b, h], 0.0)
        bias_ref[h] = lax.fori_loop(0, N_BUCKETS, add, jnp.zeros((BLOCK, 2 * BLOCK), F32))


def _kv_variants(prev_ref, cur_ref):
    cat = jnp.concatenate([prev_ref[...], cur_ref[...]], axis=0)
    lo = lax.broadcasted_iota(jnp.int32, cat.shape, 1) < HEAD_DIM
    zero = jnp.zeros_like(cat)
    head0_lo = jnp.where(lo, cat, zero)
    head1_hi = jnp.where(lo, zero, cat)
    return ((head0_lo, pltpu.roll(head0_lo, HEAD_DIM, 1)), (pltpu.roll(head1_hi, HEAD_DIM, 1), head1_hi))


def _merge_kv_grads(g):
    lo = lax.broadcasted_iota(jnp.int32, g[0][0].shape, 1) < HEAD_DIM
    return jnp.where(lo, g[0][0] + pltpu.roll(g[0][1], HEAD_DIM, 1), g[1][1] + pltpu.roll(g[1][0], HEAD_DIM, 1))


def _head_lanes(h):
    return slice((h // 2) * LANES, (h // 2 + 1) * LANES)


def _attn_probs(q_ref, kvar, bias_ref, sk_ref, valid, s_ref):
    for h in range(N_HEADS):
        s_ref[h] = _dot_nt(q_ref[:, _head_lanes(h)], kvar[h // 4][h % 2])
    head = lax.broadcasted_iota(jnp.int32, (N_HEADS, 1, 1), 0)
    sink = jnp.zeros((N_HEADS, 1, 1), F32)
    for h in range(N_HEADS):
        sink = jnp.where(head == h, sk_ref[0, h], sink)
    s = jnp.where(valid[None], s_ref[...] * (HEAD_DIM ** -0.5) + bias_ref[...], NEG_INF)
    m = jnp.maximum(jnp.max(s, axis=-1, keepdims=True), sink)
    p = jnp.exp(s - m)
    e_sink = jnp.exp(sink - m)
    inv = 1.0 / (jnp.sum(p, axis=-1, keepdims=True) + e_sink)
    return p * inv, e_sink * inv


def _attn_valid(bucket_ref, n):
    col = lax.broadcasted_iota(jnp.int32, (BLOCK, 2 * BLOCK), 1)
    return (bucket_ref[...] >= 0) & ((n > 0) | (col >= BLOCK))


def _attn_fwd(q, k, v, bucket, rel_bias, sinks, carry=None):
    T = q.shape[0]
    nb = T // BLOCK

    def body(q_ref, kc_ref, kp_ref, vc_ref, vp_ref, bucket_ref, rb_ref, sk_ref, o_ref, bias_ref, s_ref, p_ref):
        n = pl.program_id(0)

        @pl.when(n == 0)
        def _():
            _build_bias(bucket_ref, rb_ref, bias_ref)

        kvar = _kv_variants(kp_ref, kc_ref)
        vvar = _kv_variants(vp_ref, vc_ref)
        pr, _ = _attn_probs(q_ref, kvar, bias_ref, sk_ref, _attn_valid(bucket_ref, n), s_ref)
        p_ref[...] = pr.astype(BF16)
        for m in range(N_HEADS // 2):
            acc = _dot(p_ref[2 * m], vvar[m // 2][0]) + _dot(p_ref[2 * m + 1], vvar[m // 2][1])
            o_ref[:, m * LANES:(m + 1) * LANES] = acc.astype(o_ref.dtype)

    cur = lambda w: pl.BlockSpec((BLOCK, w), lambda n: (n, 0))
    prev = lambda w: pl.BlockSpec((BLOCK, w), lambda n: (jnp.maximum(n - 1, 0), 0))
    smem = pl.BlockSpec(memory_space=pltpu.SMEM)
    return _hosted_call(
        body, carry, _edge_1d(nb), name="attn_fwd", grid=(nb,),
        in_specs=[cur(ATTN_W), cur(KV_W), prev(KV_W), cur(KV_W), prev(KV_W), _const_spec((BLOCK, 2 * BLOCK)), smem, smem],
        out_specs=[cur(ATTN_W)],
        out_shape=[jax.ShapeDtypeStruct((T, ATTN_W), BF16)],
        scratch_shapes=[pltpu.VMEM((N_HEADS, BLOCK, 2 * BLOCK), F32), pltpu.VMEM((N_HEADS, BLOCK, 2 * BLOCK), F32),
                        pltpu.VMEM((N_HEADS, BLOCK, 2 * BLOCK), BF16)],
        compiler_params=_params(("arbitrary",)), inputs=(q, k, k, v, v, bucket, rel_bias, sinks))


ATTN_SMALL_ROWS = N_BUCKETS + SUBLANES


def _attn_bwd(q, k, v, datt, bucket, rel_bias, sinks, carry=None):
    T = q.shape[0]
    nb = T // BLOCK

    def body(q_ref, do_ref, kc_ref, kp_ref, vc_ref, vp_ref, bucket_ref, rb_ref, sk_ref,
             dq_ref, dk_ref, dv_ref, small_ref, bias_ref, ds_sum_ref, dsink_ref, kcarry_ref, vcarry_ref,
             s_ref, dp_ref, p_ref, dsc_ref):
        n = pl.program_id(0)

        @pl.when(n == 0)
        def _():
            _build_bias(bucket_ref, rb_ref, bias_ref)
            ds_sum_ref[...] = jnp.zeros_like(ds_sum_ref)
            dsink_ref[...] = jnp.zeros_like(dsink_ref)
            kcarry_ref[...] = jnp.zeros_like(kcarry_ref)
            vcarry_ref[...] = jnp.zeros_like(vcarry_ref)

        @pl.when(n < nb)
        def _():
            kvar = _kv_variants(kp_ref, kc_ref)
            vvar = _kv_variants(vp_ref, vc_ref)
            pr, p_sink = _attn_probs(q_ref, kvar, bias_ref, sk_ref, _attn_valid(bucket_ref, n), s_ref)
            for h in range(N_HEADS):
                dp_ref[h] = _dot_nt(do_ref[:, _head_lanes(h)], vvar[h // 4][h % 2])
            dp = dp_ref[...]
            dsum = jnp.sum(pr * dp, axis=-1, keepdims=True)
            ds = pr * (dp - dsum)
            ds_sum_ref[...] += ds
            dsink_ref[...] -= jnp.sum(p_sink * dsum, axis=1, keepdims=True)
            dsc_ref[...] = (ds * (HEAD_DIM ** -0.5)).astype(BF16)
            p_ref[...] = pr.astype(BF16)
            for m in range(N_HEADS // 2):
                dqm = _dot(dsc_ref[2 * m], kvar[m // 2][0]) + _dot(dsc_ref[2 * m + 1], kvar[m // 2][1])
                dq_ref[:, m * LANES:(m + 1) * LANES] = dqm.astype(dq_ref.dtype)
            dk_var = [[None, None], [None, None]]
            dv_var = [[None, None], [None, None]]
            for kvh in range(2):
                for e in range(2):
                    heads = [h for h in range(N_HEADS) if h // 4 == kvh and h % 2 == e]
                    dk_var[kvh][e] = sum(_dot_tn(dsc_ref[h], q_ref[:, _head_lanes(h)]) for h in heads)
                    dv_var[kvh][e] = sum(_dot_tn(p_ref[h], do_ref[:, _head_lanes(h)]) for h in heads)
            dk_cat = _merge_kv_grads(dk_var)
            dv_cat = _merge_kv_grads(dv_var)

            @pl.when(n > 0)
            def _():
                dk_ref[...] = (kcarry_ref[...] + dk_cat[:BLOCK]).astype(dk_ref.dtype)
                dv_ref[...] = (vcarry_ref[...] + dv_cat[:BLOCK]).astype(dv_ref.dtype)

            kcarry_ref[...] = dk_cat[BLOCK:]
            vcarry_ref[...] = dv_cat[BLOCK:]

        @pl.when(n == nb)
        def _():
            dk_ref[...] = kcarry_ref[...].astype(dk_ref.dtype)
            dv_ref[...] = vcarry_ref[...].astype(dv_ref.dtype)
            bk = bucket_ref[...]
            row = lax.broadcasted_iota(jnp.int32, (N_HEADS, ATTN_SMALL_ROWS, LANES), 1)

            def add(b, acc):
                masked = jnp.where((bk == b)[None], ds_sum_ref[...], 0.0)
                val = jnp.sum(jnp.sum(masked, axis=1, keepdims=True), axis=2, keepdims=True)
                return acc + jnp.where(row == b, val, 0.0)

            small_ref[...] = lax.fori_loop(0, N_BUCKETS, add, jnp.where(row == N_BUCKETS, dsink_ref[...], 0.0))

    last = nb - 1
    cur = lambda w: pl.BlockSpec((BLOCK, w), lambda n: (jnp.minimum(n, last), 0))
    prev = lambda w: pl.BlockSpec((BLOCK, w), lambda n: (jnp.clip(n - 1, 0, last), 0))
    smem = pl.BlockSpec(memory_space=pltpu.SMEM)
    return _hosted_call(
        body, carry, _edge_1d(nb + 1), name="attn_bwd", grid=(nb + 1,),
        in_specs=[cur(ATTN_W), cur(ATTN_W), cur(KV_W), prev(KV_W), cur(KV_W), prev(KV_W),
                  _const_spec((BLOCK, 2 * BLOCK)), smem, smem],
        out_specs=[cur(ATTN_W), prev(KV_W), prev(KV_W),
                   pl.BlockSpec((N_HEADS, ATTN_SMALL_ROWS, LANES), lambda n: (0, 0, 0))],
        out_shape=[jax.ShapeDtypeStruct((T, ATTN_W), BF16), jax.ShapeDtypeStruct((T, KV_W), BF16),
                   jax.ShapeDtypeStruct((T, KV_W), BF16), jax.ShapeDtypeStruct((N_HEADS, ATTN_SMALL_ROWS, LANES), F32)],
        scratch_shapes=[pltpu.VMEM((N_HEADS, BLOCK, 2 * BLOCK), F32), pltpu.VMEM((N_HEADS, BLOCK, 2 * BLOCK), F32),
                        pltpu.VMEM((N_HEADS, 1, 1), F32), pltpu.VMEM((BLOCK, KV_W), F32), pltpu.VMEM((BLOCK, KV_W), F32),
                        pltpu.VMEM((N_HEADS, BLOCK, 2 * BLOCK), F32), pltpu.VMEM((N_HEADS, BLOCK, 2 * BLOCK), F32),
                        pltpu.VMEM((N_HEADS, BLOCK, 2 * BLOCK), BF16), pltpu.VMEM((N_HEADS, BLOCK, 2 * BLOCK), BF16)],
        compiler_params=_params(("arbitrary",)), inputs=(q, datt, k, k, v, v, bucket, rel_bias, sinks))


def _cmul(ar, ai, br, bi):
    return ar * br - ai * bi, ar * bi + ai * br


def _ssm_discretize(lr, li, ldt):
    dt = jnp.exp(ldt)
    mag = jnp.exp(lr * dt)
    ab_re = mag * jnp.cos(li * dt)
    ab_im = mag * jnp.sin(li * dt)
    nr = ab_re - 1.0
    den = lr * lr + li * li
    f_re = (nr * lr + ab_im * li) / den
    f_im = (ab_im * lr - nr * li) / den
    return ab_re, ab_im, f_re, f_im


def _ssm_prep(lam_re, lam_im, ldt_rep, bd_re, bd_im):
    def body(lr_ref, li_ref, ldt_ref, bdr_ref, bdi_ref, ar_ref, ai_ref, br_ref, bi_ref):
        ab_re, ab_im, f_re, f_im = _ssm_discretize(lr_ref[...], li_ref[...], ldt_ref[...])
        ar_ref[...] = ab_re
        ai_ref[...] = ab_im
        bdr, bdi = bdr_ref[0], bdi_ref[0]
        br_ref[0] = (bdr * f_re - bdi * f_im).astype(BF16)
        bi_ref[0] = (bdi * f_re + bdr * f_im).astype(BF16)

    row = pl.BlockSpec((1, SSM_LANE_BLOCK), lambda j: (0, j))
    mat = pl.BlockSpec((1, LANES, SSM_LANE_BLOCK), lambda j: (j, 0, 0))
    return pl.pallas_call(
        body, name="ssm_prep", grid=(N_SSM_BLOCKS,),
        in_specs=[row, row, row, mat, mat], out_specs=[row, row, mat, mat],
        out_shape=[jax.ShapeDtypeStruct((1, STATES), F32)] * 2 + [jax.ShapeDtypeStruct((N_SSM_BLOCKS, LANES, SSM_LANE_BLOCK), BF16)] * 2,
        compiler_params=_params(("arbitrary",)),
    )(lam_re, lam_im, ldt_rep, bd_re, bd_im)


def _ssm_prep_bwd(lam_re, lam_im, ldt_rep, bd_re, bd_im, dbr, dbi, da_re, da_im):
    def body(lr_ref, li_ref, ldt_ref, bdr_ref, bdi_ref, dbr_ref, dbi_ref, dar_ref, dai_ref,
             dbdr_ref, dbdi_ref, dlr_ref, dli_ref, dldt_ref):
        lr, li, ldt = lr_ref[...], li_ref[...], ldt_ref[...]
        (_, _, f_re, f_im), vjp = jax.vjp(_ssm_discretize, lr, li, ldt)
        bdr, bdi, gbr, gbi = bdr_ref[0], bdi_ref[0], dbr_ref[0], dbi_ref[0]
        dbdr_ref[0] = gbr * f_re + gbi * f_im
        dbdi_ref[0] = gbi * f_re - gbr * f_im
        df_re = jnp.sum(gbr * bdr + gbi * bdi, axis=0, keepdims=True)
        df_im = jnp.sum(gbi * bdr - gbr * bdi, axis=0, keepdims=True)
        dlr, dli, dldt = vjp((dar_ref[...], dai_ref[...], df_re, df_im))
        dlr_ref[...] = dlr
        dli_ref[...] = dli
        dldt_ref[...] = dldt

    row = pl.BlockSpec((1, SSM_LANE_BLOCK), lambda j: (0, j))
    mat = pl.BlockSpec((1, LANES, SSM_LANE_BLOCK), lambda j: (j, 0, 0))
    mat_shape = jax.ShapeDtypeStruct((N_SSM_BLOCKS, LANES, SSM_LANE_BLOCK), F32)
    row_shape = jax.ShapeDtypeStruct((1, STATES), F32)
    return pl.pallas_call(
        body, name="ssm_prep_bwd", grid=(N_SSM_BLOCKS,),
        in_specs=[row, row, row, mat, mat, mat, mat, row, row], out_specs=[mat, mat, row, row, row],
        out_shape=[mat_shape, mat_shape, row_shape, row_shape, row_shape],
        compiler_params=_params(("arbitrary",)),
    )(lam_re, lam_im, ldt_rep, bd_re, bd_im, dbr, dbi, da_re, da_im)


def _group_sum(x):
    def body(x_ref, o_ref):
        o_ref[...] = jnp.sum(x_ref[...], axis=1, keepdims=True)
    return pl.pallas_call(body, name="ssm_group_sum", out_shape=jax.ShapeDtypeStruct((N_GROUPS, 1), F32))(x)


def _power_table(ar, ai, p_re_ref, p_im_ref, steps):
    shape = (SUBLANES, SSM_LANE_BLOCK)
    p_re_ref[0:SUBLANES] = jnp.broadcast_to(ar, shape)
    p_im_ref[0:SUBLANES] = jnp.broadcast_to(ai, shape)
    m = 1
    while m < steps:
        rows = m * SUBLANES
        top_re = p_re_ref[rows - SUBLANES:rows]
        top_im = p_im_ref[rows - SUBLANES:rows]
        cur_re = p_re_ref[0:rows].reshape(m, SUBLANES, SSM_LANE_BLOCK)
        cur_im = p_im_ref[0:rows].reshape(m, SUBLANES, SSM_LANE_BLOCK)
        nxt_re, nxt_im = _cmul(cur_re, cur_im, top_re[None], top_im[None])
        p_re_ref[rows:2 * rows] = nxt_re.reshape(rows, SSM_LANE_BLOCK)
        p_im_ref[rows:2 * rows] = nxt_im.reshape(rows, SSM_LANE_BLOCK)
        m *= 2


def _to_segments(src_ref, dst_ref, steps):
    for s in range(SUBLANES):
        dst_ref[pl.ds(s, steps, stride=SUBLANES), :] = src_ref[s * steps:(s + 1) * steps, :]


def _from_segments(src_ref, dst_ref, steps):
    for s in range(SUBLANES):
        dst_ref[s * steps:(s + 1) * steps, :] = src_ref[pl.ds(s, steps, stride=SUBLANES), :]


def _segment_carries(e_re, e_im, an_re, an_im, c_re, c_im, reverse):
    order = range(SUBLANES - 1, -1, -1) if reverse else range(SUBLANES)
    ins_re, ins_im = [None] * SUBLANES, [None] * SUBLANES
    for s in order:
        ins_re[s], ins_im[s] = c_re, c_im
        pr, pi = _cmul(an_re, an_im, c_re, c_im)
        c_re = e_re[s:s + 1] + pr
        c_im = e_im[s:s + 1] + pi
    return jnp.concatenate(ins_re, axis=0), jnp.concatenate(ins_im, axis=0), c_re, c_im


def _ssm_fwd(u, a_re, a_im, b_re, b_im, c_re, c_im, d_skip, chunk, carry=None):
    T = u.shape[0]
    nc = T // chunk
    steps = chunk // SUBLANES
    blk = SSM_LANE_BLOCK

    def body(u_ref, ar_ref, ai_ref, br_ref, bi_ref, cr_ref, ci_ref, dk_ref,
             y_ref, hr_ref, hi_ref, inr_ref, ini_ref, useg_ref, yseg_ref, pr_ref, pi_ref, carry_ref):
        c = pl.program_id(1)
        ar, ai = ar_ref[...], ai_ref[...]

        @pl.when(c == 0)
        def _():
            _power_table(ar, ai, pr_ref, pi_ref, steps)
            carry_ref[...] = jnp.zeros_like(carry_ref)

        _to_segments(u_ref, useg_ref, steps)
        ub = useg_ref[...].astype(BF16)
        hr_ref[...] = _dot(ub, br_ref[0])
        hi_ref[...] = _dot(ub, bi_ref[0])
        ar8 = jnp.broadcast_to(ar, (SUBLANES, blk))
        ai8 = jnp.broadcast_to(ai, (SUBLANES, blk))

        def scan(t, prev):
            rows = pl.ds(pl.multiple_of(t * SUBLANES, SUBLANES), SUBLANES)
            pr, pi = _cmul(ar8, ai8, prev[0], prev[1])
            nr = pr + hr_ref[rows, :]
            ni = pi + hi_ref[rows, :]
            hr_ref[rows, :] = nr
            hi_ref[rows, :] = ni
            return nr, ni

        lax.fori_loop(1, steps, scan, (hr_ref[0:SUBLANES, :], hi_ref[0:SUBLANES, :]), unroll=4)

        top = slice(chunk - SUBLANES, chunk)
        in_re, in_im, out_re, out_im = _segment_carries(
            hr_ref[top, :], hi_ref[top, :], pr_ref[top, :][0:1], pi_ref[top, :][0:1],
            carry_ref[0:1, :], carry_ref[1:2, :], reverse=False)
        carry_ref[0:1, :] = out_re
        carry_ref[1:2, :] = out_im
        inr_ref[...] = in_re
        ini_ref[...] = in_im

        def fix(t, _):
            rows = pl.ds(pl.multiple_of(t * SUBLANES, SUBLANES), SUBLANES)
            fr, fi = _cmul(pr_ref[rows, :], pi_ref[rows, :], in_re, in_im)
            hr_ref[rows, :] += fr
            hi_ref[rows, :] += fi
            return 0

        lax.fori_loop(0, steps, fix, 0, unroll=4)

        yseg_ref[...] = _dot(hr_ref[...].astype(BF16), cr_ref[0]) - _dot(hi_ref[...].astype(BF16), ci_ref[0])
        _from_segments(yseg_ref, y_ref, steps)
        y_ref[...] += dk_ref[...] * u_ref[...]

    row = pl.BlockSpec((1, blk), lambda j, c: (0, j))
    b_mat = pl.BlockSpec((1, LANES, blk), lambda j, c: (j, 0, 0))
    c_mat = pl.BlockSpec((1, blk, LANES), lambda j, c: (j, 0, 0))
    tok = pl.BlockSpec((chunk, LANES), lambda j, c: (c, j))
    state = pl.BlockSpec((chunk, blk), lambda j, c: (c, j))
    enter = pl.BlockSpec((SUBLANES, blk), lambda j, c: (c, j))
    return _hosted_call(
        body, carry, _edge_2d(N_SSM_BLOCKS, nc), name="ssm_fwd", grid=(N_SSM_BLOCKS, nc),
        in_specs=[tok, row, row, b_mat, b_mat, c_mat, c_mat, pl.BlockSpec((1, LANES), lambda j, c: (0, j))],
        out_specs=[tok, state, state, enter, enter],
        out_shape=[jax.ShapeDtypeStruct((T, SSM_W), F32), jax.ShapeDtypeStruct((T, STATES), F32),
                   jax.ShapeDtypeStruct((T, STATES), F32), jax.ShapeDtypeStruct((nc * SUBLANES, STATES), F32),
                   jax.ShapeDtypeStruct((nc * SUBLANES, STATES), F32)],
        scratch_shapes=[pltpu.VMEM((chunk, LANES), F32), pltpu.VMEM((chunk, LANES), F32),
                        pltpu.VMEM((chunk, blk), F32), pltpu.VMEM((chunk, blk), F32), pltpu.VMEM((SUBLANES, blk), F32)],
        compiler_params=_params(("arbitrary", "arbitrary"), VMEM_MID),
        inputs=(u, a_re, a_im, b_re, b_im, c_re, c_im, d_skip))


def _ssm_bwd(dy, u, h_re, h_im, in_re, in_im, a_re, a_im, b_re, b_im, c_re, c_im, d_skip, chunk, carry=None):
    T = u.shape[0]
    nc = T // chunk
    steps = chunk // SUBLANES
    blk = SSM_LANE_BLOCK

    def body(dy_ref, u_ref, hr_ref, hi_ref, inr_ref, ini_ref, ar_ref, ai_ref, br_ref, bi_ref, cr_ref, ci_ref, dk_ref,
             du_ref, dbr_ref, dbi_ref, dcr_ref, dci_ref, dar_ref, dai_ref, ddk_ref,
             dyseg_ref, useg_ref, duseg_ref, gr_ref, gi_ref, pr_ref, pi_ref, carry_ref, accr_ref, acci_ref):
        c = pl.program_id(1)
        ar, ai = ar_ref[...], ai_ref[...]

        @pl.when(c == 0)
        def _():
            _power_table(ar, ai, pr_ref, pi_ref, steps)
            carry_ref[...] = jnp.zeros_like(carry_ref)
            accr_ref[...] = jnp.zeros_like(accr_ref)
            acci_ref[...] = jnp.zeros_like(acci_ref)

        _to_segments(dy_ref, dyseg_ref, steps)
        _to_segments(u_ref, useg_ref, steps)
        dyb = dyseg_ref[...].astype(BF16)
        ub = useg_ref[...].astype(BF16)
        gr_ref[...] = _dot_nt(dyb, cr_ref[0])
        gi_ref[...] = -_dot_nt(dyb, ci_ref[0])
        dcr = _dot_tn(hr_ref[...].astype(BF16), dyb)
        dci = -_dot_tn(hi_ref[...].astype(BF16), dyb)
        ddk = jnp.sum(dy_ref[...] * u_ref[...], axis=0, keepdims=True)

        ar8 = jnp.broadcast_to(ar, (SUBLANES, blk))
        ai8 = jnp.broadcast_to(-ai, (SUBLANES, blk))

        def scan(k, nxt):
            t = steps - 2 - k
            rows = pl.ds(pl.multiple_of(t * SUBLANES, SUBLANES), SUBLANES)
            pr, pi = _cmul(ar8, ai8, nxt[0], nxt[1])
            nr = pr + gr_ref[rows, :]
            ni = pi + gi_ref[rows, :]
            gr_ref[rows, :] = nr
            gi_ref[rows, :] = ni
            return nr, ni

        top = slice(chunk - SUBLANES, chunk)
        lax.fori_loop(0, steps - 1, scan, (gr_ref[top, :], gi_ref[top, :]), unroll=4)

        gin_re, gin_im, out_re, out_im = _segment_carries(
            gr_ref[0:SUBLANES, :], gi_ref[0:SUBLANES, :], pr_ref[top, :][0:1], -pi_ref[top, :][0:1],
            carry_ref[0:1, :], carry_ref[1:2, :], reverse=True)
        carry_ref[0:1, :] = out_re
        carry_ref[1:2, :] = out_im

        def fix(t, acc):
            rows = pl.ds(pl.multiple_of(t * SUBLANES, SUBLANES), SUBLANES)
            prow = pl.ds(pl.multiple_of((steps - 1 - t) * SUBLANES, SUBLANES), SUBLANES)
            fr, fi = _cmul(pr_ref[prow, :], -pi_ref[prow, :], gin_re, gin_im)
            g_re = gr_ref[rows, :] + fr
            g_im = gi_ref[rows, :] + fi
            gr_ref[rows, :] = g_re
            gi_ref[rows, :] = g_im
            before = pl.ds(pl.multiple_of(jnp.maximum(t - 1, 0) * SUBLANES, SUBLANES), SUBLANES)
            first = t == 0
            hp_re = jnp.where(first, inr_ref[...], hr_ref[before, :])
            hp_im = jnp.where(first, ini_ref[...], hi_ref[before, :])
            return acc[0] + g_re * hp_re + g_im * hp_im, acc[1] + g_im * hp_re - g_re * hp_im

        acc_re, acc_im = lax.fori_loop(0, steps, fix, (accr_ref[...], acci_ref[...]), unroll=2)
        accr_ref[...] = acc_re
        acci_ref[...] = acc_im

        gbr = gr_ref[...].astype(BF16)
        gbi = gi_ref[...].astype(BF16)
        duseg_ref[...] = _dot_nt(gbr, br_ref[0]) + _dot_nt(gbi, bi_ref[0])
        _from_segments(duseg_ref, du_ref, steps)
        du_ref[...] += dk_ref[...] * dy_ref[...]
        dbr = _dot_tn(ub, gbr)
        dbi = _dot_tn(ub, gbi)

        @pl.when(c == 0)
        def _():
            dbr_ref[0] = dbr
            dbi_ref[0] = dbi
            dcr_ref[0] = dcr
            dci_ref[0] = dci
            ddk_ref[...] = ddk

        @pl.when(c > 0)
        def _():
            dbr_ref[0] += dbr
            dbi_ref[0] += dbi
            dcr_ref[0] += dcr
            dci_ref[0] += dci
            ddk_ref[...] += ddk

        @pl.when(c == nc - 1)
        def _():
            dar_ref[...] = jnp.sum(acc_re, axis=0, keepdims=True)
            dai_ref[...] = jnp.sum(acc_im, axis=0, keepdims=True)

    rev = lambda c: nc - 1 - c
    row = pl.BlockSpec((1, blk), lambda j, c: (0, j))
    b_mat = pl.BlockSpec((1, LANES, blk), lambda j, c: (j, 0, 0))
    c_mat = pl.BlockSpec((1, blk, LANES), lambda j, c: (j, 0, 0))
    tok = pl.BlockSpec((chunk, LANES), lambda j, c: (rev(c), j))
    state = pl.BlockSpec((chunk, blk), lambda j, c: (rev(c), j))
    enter = pl.BlockSpec((SUBLANES, blk), lambda j, c: (rev(c), j))
    chan = pl.BlockSpec((1, LANES), lambda j, c: (0, j))
    f32 = lambda *s: jax.ShapeDtypeStruct(s, F32)
    return _hosted_call(
        body, carry, _edge_2d(N_SSM_BLOCKS, nc), name="ssm_bwd", grid=(N_SSM_BLOCKS, nc),
        in_specs=[tok, tok, state, state, enter, enter, row, row, b_mat, b_mat, c_mat, c_mat, chan],
        out_specs=[tok, b_mat, b_mat, c_mat, c_mat, row, row, chan],
        out_shape=[f32(T, SSM_W), f32(N_SSM_BLOCKS, LANES, blk), f32(N_SSM_BLOCKS, LANES, blk),
                   f32(N_SSM_BLOCKS, blk, LANES), f32(N_SSM_BLOCKS, blk, LANES), f32(1, STATES), f32(1, STATES), f32(1, SSM_W)],
        scratch_shapes=[pltpu.VMEM((chunk, LANES), F32), pltpu.VMEM((chunk, LANES), F32), pltpu.VMEM((chunk, LANES), F32),
                        pltpu.VMEM((chunk, blk), F32), pltpu.VMEM((chunk, blk), F32),
                        pltpu.VMEM((chunk, blk), F32), pltpu.VMEM((chunk, blk), F32),
                        pltpu.VMEM((SUBLANES, blk), F32), pltpu.VMEM((SUBLANES, blk), F32), pltpu.VMEM((SUBLANES, blk), F32)],
        compiler_params=_params(("arbitrary", "arbitrary"), VMEM_BIG),
        inputs=(dy, u, h_re, h_im, in_re, in_im, a_re, a_im, b_re, b_im, c_re, c_im, d_skip))


def _merge_forward(y, att, ga, gs, w_glu, w_ssm, w_attn):
    z = jax.nn.gelu(y)
    zb = z.astype(BF16)
    gl = jax.nn.sigmoid(_dot(zb, w_glu))
    z2b = (z * gl).astype(BF16)
    y_ssm = _dot(z2b, w_ssm)
    y_attn = _dot(att, w_attn)
    sa = jax.nn.sigmoid(ga)
    ss = jax.nn.sigmoid(gs)
    merged = (sa * y_attn + ss * y_ssm).astype(BF16)
    return z, zb, gl, z2b, y_ssm, y_attn, sa, ss, merged


def _merge_fwd(x, y, att, ga, gs, g2, g3, w_glu, w_ssm, w_attn, w_out, tile):
    T = x.shape[0]

    def body(x_ref, y_ref, att_ref, ga_ref, gs_ref, g2_ref, g3_ref, wg_ref, ws_ref, wa_ref, wo_ref, x1_ref, o_ref, h2_ref):
        merged = _merge_forward(y_ref[...], att_ref[...], ga_ref[...], gs_ref[...], wg_ref[...], ws_ref[...], wa_ref[...])[-1]
        o = _dot(merged, wo_ref[...])
        x1 = x_ref[...] + o * _rms_scale(o) * g2_ref[...]
        o_ref[...] = o
        x1_ref[...] = x1
        h2_ref[...] = (x1 * _rms_scale(x1) * g3_ref[...]).astype(BF16)

    tok = lambda w: pl.BlockSpec((tile, w), lambda i: (i, 0))
    vec = _const_spec((1, D_MODEL))
    return pl.pallas_call(
        body, name="merge_fwd", grid=(T // tile,),
        in_specs=[tok(D_MODEL), tok(SSM_W), tok(ATTN_W), tok(D_MODEL), tok(D_MODEL), vec, vec,
                  _const_spec((SSM_W, SSM_W)), _const_spec((SSM_W, D_MODEL)), _const_spec((ATTN_W, D_MODEL)),
                  _const_spec((D_MODEL, D_MODEL))],
        out_specs=[tok(D_MODEL), tok(D_MODEL), tok(D_MODEL)],
        out_shape=[jax.ShapeDtypeStruct((T, D_MODEL), F32), jax.ShapeDtypeStruct((T, D_MODEL), F32),
                   jax.ShapeDtypeStruct((T, D_MODEL), BF16)],
        compiler_params=_params(("arbitrary",), VMEM_MID),
    )(x, y, att, ga, gs, g2, g3, w_glu, w_ssm, w_attn, w_out)


def _merge_bwd(dh2, dx2, x1, o, y, att, ga, gs, g2, g3, w_glu, w_ssm, w_attn, w_out, tile, carry=None):
    T = x1.shape[0]
    n_steps = T // tile

    def body(dh2_ref, dx2_ref, x1_ref, o_ref, y_ref, att_ref, ga_ref, gs_ref, g2_ref, g3_ref, wg_ref, ws_ref, wa_ref, wo_ref,
             dx1_ref, dga_ref, dgs_ref, datt_ref, dy_ref, dwg_hbm, dws_hbm, dwa_hbm, dwo_hbm, dg2_ref, dg3_ref,
             awg_ref, aws_ref, awa_ref, awo_ref):
        i = pl.program_id(0)
        x1v, ov = x1_ref[...], o_ref[...]
        dxn, dg3 = _rms_bwd(dh2_ref[...], x1v, _rms_scale(x1v), g3_ref[...])
        dx1 = dx2_ref[...] + dxn
        dx1_ref[...] = dx1
        do, dg2 = _rms_bwd(dx1, ov, _rms_scale(ov), g2_ref[...])
        dob = do.astype(BF16)

        yv = y_ref[...]
        att = att_ref[...]
        z, zb, gl, z2b, y_ssm, y_attn, sa, ss, merged = _merge_forward(
            yv, att, ga_ref[...], gs_ref[...], wg_ref[...], ws_ref[...], wa_ref[...])
        dmerged = _dot_nt(dob, wo_ref[...])
        dya = (dmerged * sa).astype(BF16)
        dys = (dmerged * ss).astype(BF16)
        dga_ref[...] = (dmerged * y_attn * sa * (1.0 - sa)).astype(BF16)
        dgs_ref[...] = (dmerged * y_ssm * ss * (1.0 - ss)).astype(BF16)
        datt_ref[...] = _dot_nt(dya, wa_ref[...]).astype(BF16)
        dz2 = _dot_nt(dys, ws_ref[...])
        dpre = (dz2 * z * gl * (1.0 - gl)).astype(BF16)
        dz = dz2 * gl + _dot_nt(dpre, wg_ref[...])
        _, gelu_vjp = jax.vjp(jax.nn.gelu, yv)
        dy_ref[...] = gelu_vjp(dz)[0]

        grads = ((awo_ref, _dot_tn(merged, dob)), (awa_ref, _dot_tn(att, dya)),
                 (aws_ref, _dot_tn(z2b, dys)), (awg_ref, _dot_tn(zb, dpre)), (dg2_ref, dg2), (dg3_ref, dg3))

        @pl.when(i == 0)
        def _():
            for ref, val in grads:
                ref[...] = val

        @pl.when(i > 0)
        def _():
            for ref, val in grads:
                ref[...] += val

        @pl.when(i == n_steps - 1)
        def _():
            pltpu.sync_copy(awg_ref, dwg_hbm)
            pltpu.sync_copy(aws_ref, dws_hbm)
            pltpu.sync_copy(awa_ref, dwa_hbm)
            pltpu.sync_copy(awo_ref, dwo_hbm)

    tok = lambda w: pl.BlockSpec((tile, w), lambda i: (i, 0))
    vec = _const_spec((1, D_MODEL))
    any_ = pl.BlockSpec(memory_space=pl.ANY)
    vec_out = pl.BlockSpec((1, D_MODEL), lambda i: (0, 0))
    f32 = lambda *s: jax.ShapeDtypeStruct(s, F32)
    bf = lambda *s: jax.ShapeDtypeStruct(s, BF16)
    return _hosted_call(
        body, carry, _edge_1d(n_steps), name="merge_bwd", grid=(n_steps,),
        in_specs=[tok(D_MODEL), tok(D_MODEL), tok(D_MODEL), tok(D_MODEL), tok(SSM_W), tok(ATTN_W), tok(D_MODEL), tok(D_MODEL),
                  vec, vec, _const_spec((SSM_W, SSM_W)), _const_spec((SSM_W, D_MODEL)), _const_spec((ATTN_W, D_MODEL)),
                  _const_spec((D_MODEL, D_MODEL))],
        out_specs=[tok(D_MODEL), tok(D_MODEL), tok(D_MODEL), tok(ATTN_W), tok(SSM_W), any_, any_, any_, any_, vec_out, vec_out],
        out_shape=[f32(T, D_MODEL), bf(T, D_MODEL), bf(T, D_MODEL), bf(T, ATTN_W), f32(T, SSM_W),
                   f32(SSM_W, SSM_W), f32(SSM_W, D_MODEL), f32(ATTN_W, D_MODEL), f32(D_MODEL, D_MODEL),
                   f32(1, D_MODEL), f32(1, D_MODEL)],
        scratch_shapes=[pltpu.VMEM((SSM_W, SSM_W), F32), pltpu.VMEM((SSM_W, D_MODEL), F32),
                        pltpu.VMEM((ATTN_W, D_MODEL), F32), pltpu.VMEM((D_MODEL, D_MODEL), F32)],
        compiler_params=_params(("arbitrary",), VMEM_BIG),
        inputs=(dh2, dx2, x1, o, y, att, ga, gs, g2, g3, w_glu, w_ssm, w_attn, w_out))


FF_SHARD = D_FF // N_DEV


def _mlp_fwd(h2, x1, target, g4, w_ff_in, w_ff_out, tile):
    T = h2.shape[0]
    per_step = 2
    ff_chunk = per_step * FF_SHARD
    n_i, n_k = T // tile, N_DEV // per_step

    def body(h2_ref, x1_ref, tg_ref, g4_ref, wi_ref, wo_ref, a_ref, dfo_ref, dx2_ref, loss_ref, dg4_ref, acc_ref):
        i, k = pl.program_id(0), pl.program_id(1)
        h2v = h2_ref[...]
        part = jnp.zeros((tile, D_MODEL), F32)
        for s in range(per_step):
            a = _dot(h2v, wi_ref[s])
            a_ref[:, s * FF_SHARD:(s + 1) * FF_SHARD] = a.astype(BF16)
            ra = jnp.maximum(a, 0.0)
            part = part + _dot((ra * ra).astype(BF16), wo_ref[s])

        @pl.when(k == 0)
        def _():
            acc_ref[...] = part

        @pl.when(k > 0)
        def _():
            acc_ref[...] += part

        @pl.when(k == n_k - 1)
        def _():
            f = acc_ref[...]
            r = _rms_scale(f)
            g = g4_ref[...]
            err = x1_ref[...] + f * r * g - tg_ref[...]
            dx2 = err * (1.0 / D_MODEL)
            dx2_ref[...] = dx2
            dfo, dg = _rms_bwd(dx2, f, r, g)
            dfo_ref[...] = dfo.astype(BF16)
            row = lax.broadcasted_iota(jnp.int32, (8, LANES), 0)
            col = lax.broadcasted_iota(jnp.int32, (8, LANES), 1)
            loss = jnp.where((row == 0) & (col == 0), (0.5 / D_MODEL) * jnp.sum(err * err), 0.0)

            @pl.when(i == 0)
            def _():
                loss_ref[...] = loss
                dg4_ref[...] = dg

            @pl.when(i > 0)
            def _():
                loss_ref[...] += loss
                dg4_ref[...] += dg

    tok = pl.BlockSpec((tile, D_MODEL), lambda i, k: (i, 0))
    return pl.pallas_call(
        body, name="mlp_fwd", grid=(n_i, n_k),
        in_specs=[tok, tok, tok, pl.BlockSpec((1, D_MODEL), lambda i, k: (0, 0)),
                  pl.BlockSpec((per_step, D_MODEL, FF_SHARD), lambda i, k: (k, 0, 0)),
                  pl.BlockSpec((per_step, FF_SHARD, D_MODEL), lambda i, k: (k, 0, 0))],
        out_specs=[pl.BlockSpec((tile, ff_chunk), lambda i, k: (i, k)), tok, tok,
                   pl.BlockSpec((8, LANES), lambda i, k: (0, 0)), pl.BlockSpec((1, D_MODEL), lambda i, k: (0, 0))],
        out_shape=[jax.ShapeDtypeStruct((T, D_FF), BF16), jax.ShapeDtypeStruct((T, D_MODEL), BF16),
                   jax.ShapeDtypeStruct((T, D_MODEL), F32), jax.ShapeDtypeStruct((8, LANES), F32),
                   jax.ShapeDtypeStruct((1, D_MODEL), F32)],
        scratch_shapes=[pltpu.VMEM((tile, D_MODEL), F32)],
        compiler_params=_params(("arbitrary", "arbitrary"), VMEM_BIG),
    )(h2, x1, target, g4, w_ff_in, w_ff_out)


def _mlp_bwd(dfo, a, h2, w_ff_in, w_ff_out, tile):
    T = h2.shape[0]
    ff_chunk = FF_SHARD
    n_i, n_k = T // tile, N_DEV

    def body(dfo_ref, a_ref, h2_ref, wi_ref, wo_ref, dwi_ref, dwo_ref, dh2_ref, acc_ref):
        k, i = pl.program_id(0), pl.program_id(1)
        dfo = dfo_ref[...]
        ra = jnp.maximum(a_ref[...].astype(F32), 0.0)
        drr = _dot_nt(dfo, wo_ref[0])
        da = (drr * (2.0 * ra)).astype(BF16)
        dwo = _dot_tn((ra * ra).astype(BF16), dfo)
        dwi = _dot_tn(h2_ref[...], da)
        part = _dot_nt(da, wi_ref[0])
        rows = pl.ds(pl.multiple_of(i * tile, tile), tile)

        @pl.when(i == 0)
        def _():
            dwi_ref[0] = dwi
            dwo_ref[0] = dwo

        @pl.when(i > 0)
        def _():
            dwi_ref[0] += dwi
            dwo_ref[0] += dwo

        @pl.when(k == 0)
        def _():
            acc_ref[rows, :] = part

        @pl.when((k > 0) & (k < n_k - 1))
        def _():
            acc_ref[rows, :] += part

        @pl.when(k == n_k - 1)
        def _():
            dh2_ref[...] = acc_ref[rows, :] + part

    return pl.pallas_call(
        body, name="mlp_bwd", grid=(n_k, n_i),
        in_specs=[pl.BlockSpec((tile, D_MODEL), lambda k, i: (i, 0)), pl.BlockSpec((tile, ff_chunk), lambda k, i: (i, k)),
                  pl.BlockSpec((tile, D_MODEL), lambda k, i: (i, 0)),
                  pl.BlockSpec((1, D_MODEL, ff_chunk), lambda k, i: (k, 0, 0)),
                  pl.BlockSpec((1, ff_chunk, D_MODEL), lambda k, i: (k, 0, 0))],
        out_specs=[pl.BlockSpec((1, D_MODEL, ff_chunk), lambda k, i: (k, 0, 0)),
                   pl.BlockSpec((1, ff_chunk, D_MODEL), lambda k, i: (k, 0, 0)),
                   pl.BlockSpec((tile, D_MODEL), lambda k, i: (jnp.where(k == n_k - 1, i, 0), 0))],
        out_shape=[jax.ShapeDtypeStruct((N_DEV, D_MODEL, ff_chunk), F32), jax.ShapeDtypeStruct((N_DEV, ff_chunk, D_MODEL), F32),
                   jax.ShapeDtypeStruct((T, D_MODEL), F32)],
        scratch_shapes=[pltpu.VMEM((T, D_MODEL), F32)],
        compiler_params=_params(("arbitrary", "arbitrary"), VMEM_BIG),
    )(dfo, a, h2, w_ff_in, w_ff_out)


def _block_diag_in(b):
    bt = jnp.transpose(b, (0, 2, 1)).reshape(N_SSM_BLOCKS, 8, GROUP_CH, N_STATE)
    eye = jnp.eye(8, dtype=b.dtype)
    return jnp.einsum("jacp,ab->jacbp", bt, eye).reshape(N_SSM_BLOCKS, LANES, SSM_LANE_BLOCK)


def _block_diag_in_grad(g):
    g = g.reshape(N_SSM_BLOCKS, 8, GROUP_CH, 8, N_STATE)
    d = jnp.diagonal(g, axis1=1, axis2=3)
    return jnp.transpose(d, (0, 3, 2, 1)).reshape(N_GROUPS, N_STATE, GROUP_CH)


def _block_diag_out(c):
    ct = c.reshape(N_SSM_BLOCKS, 8, GROUP_CH, N_STATE)
    eye = jnp.eye(8, dtype=c.dtype)
    return jnp.einsum("jacp,ab->japbc", ct, eye).reshape(N_SSM_BLOCKS, SSM_LANE_BLOCK, LANES)


def _block_diag_out_grad(g):
    g = g.reshape(N_SSM_BLOCKS, 8, N_STATE, 8, GROUP_CH)
    d = jnp.diagonal(g, axis1=1, axis2=3)
    return jnp.transpose(d, (0, 3, 2, 1)).reshape(N_GROUPS, GROUP_CH, N_STATE)


def _tiles(T):
    return dict(proj=min(512, T), proj_bwd=min(256, T), merge=min(512, T), merge_bwd=min(256, T),
                mlp_fwd=min(512, T), mlp_bwd=min(512, T), ssm_chunk=min(1024, T))


def _mesh_position():
    x, y, c = lax.axis_index("x"), lax.axis_index("y"), lax.axis_index("c")
    other_chips = [(1 - x, y), (x, 1 - y), (1 - x, 1 - y)]
    return x, y, c, other_chips


def _gather_carry(arrays):
    n = len(arrays)

    def copies(ins, outs, sems):
        send_sems, recv_sems, local_sems = sems
        x, y, c, chips = _mesh_position()
        me, sibling = (x, y, c), (x, y, 1 - c)

        def copy(a, k, block, to, src=None):
            px, py, pc = block
            dst = outs[a].at[4 * px + 2 * py + pc]
            return pltpu.make_async_remote_copy(
                src_ref=dst if src is None else src, dst_ref=dst, send_sem=send_sems.at[7 * a + k],
                recv_sem=recv_sems.at[7 * a + k], device_id=to, device_id_type=MESH_IDS)

        mine = [pltpu.make_async_copy(ins[a], outs[a].at[4 * x + 2 * y + c], local_sems.at[a]) for a in range(n)]
        first = []
        for a in range(n):
            first.append(copy(a, 0, me, sibling, src=ins[a]))
            first += [copy(a, 1 + j, me, (*chip, c), src=ins[a]) for j, chip in enumerate(chips)]
        return copy, mine, first, me, sibling, chips, c

    def start(ins, outs, sems):
        _, mine, first, *_ = copies(ins, outs, sems)
        for cp in mine + first:
            cp.start()

    def finish(ins, outs, sems):
        copy, mine, first, me, sibling, chips, c = copies(ins, outs, sems)
        passed = []
        for a in range(n):
            for j, chip in enumerate(chips):
                copy(a, 1 + j, (*chip, c), me).wait_recv()
                passed.append(copy(a, 4 + j, (*chip, c), sibling))
                passed[-1].start()
        for a in range(n):
            copy(a, 0, sibling, me).wait_recv()
            for j, chip in enumerate(chips):
                copy(a, 4 + j, (*chip, 1 - c), me).wait_recv()
        for cp in first + passed:
            cp.wait_send()
        for cp in mine:
            cp.wait()

    return _Carry(arrays, [jax.ShapeDtypeStruct((N_DEV,) + a.shape, a.dtype) for a in arrays],
                  [pltpu.SemaphoreType.DMA((7 * n,)), pltpu.SemaphoreType.DMA((7 * n,)), pltpu.SemaphoreType.DMA((n,))],
                  start, finish)


def _pairwise_carry(arrays, n_slots, make_copies):
    n = len(arrays)

    def start(ins, outs, sems):
        for cp in make_copies(ins, outs, sems):
            cp.start()

    def finish(ins, outs, sems):
        for cp in make_copies(ins, outs, sems):
            cp.wait()

    return _Carry(arrays, [jax.ShapeDtypeStruct((n_slots,) + a.shape[1:], a.dtype) for a in arrays],
                  [pltpu.SemaphoreType.DMA((n_slots * n,)), pltpu.SemaphoreType.DMA((n_slots * n,))], start, finish)


def _sibling_carry(grads):
    def make_copies(ins, outs, sems):
        x, y, c, _ = _mesh_position()
        return [pltpu.make_async_remote_copy(
            src_ref=ins[a].at[2 * ch + (1 - c)], dst_ref=outs[a].at[ch], send_sem=sems[0].at[4 * a + ch],
            recv_sem=sems[1].at[4 * a + ch], device_id=(x, y, 1 - c), device_id_type=MESH_IDS)
            for a in range(len(grads)) for ch in range(4)]

    return _pairwise_carry(grads, 4, make_copies)


def _chips_carry(sums):
    def make_copies(ins, outs, sems):
        x, y, c, chips = _mesh_position()
        return [pltpu.make_async_remote_copy(
            src_ref=ins[a].at[2 * px + py], dst_ref=outs[a].at[j], send_sem=sems[0].at[3 * a + j],
            recv_sem=sems[1].at[3 * a + j], device_id=(px, py, c), device_id_type=MESH_IDS)
            for a in range(len(sums)) for j, (px, py) in enumerate(chips)]

    return _pairwise_carry(sums, 3, make_copies)


def _row_tile(rows, cols):
    t = max(8, min(rows, (1 << 18) // cols // 8 * 8))
    while rows % t:
        t -= 8
    return t


def _add_sibling(grads8, recv, core, name):
    _, R, C = grads8.shape
    tr = _row_tile(R, C)
    g4 = grads8.reshape(4, 2, R, C)

    def body(core_ref, g_ref, r_ref, o_ref, ob_ref):
        s = g_ref[0] + r_ref[...]
        o_ref[...] = s
        ob_ref[...] = s.astype(BF16)

    out = pl.BlockSpec((1, tr, C), lambda ch, r, core_ref: (ch, r, 0))
    return pl.pallas_call(
        body, name=name,
        grid_spec=pltpu.PrefetchScalarGridSpec(
            num_scalar_prefetch=1, grid=(4, R // tr),
            in_specs=[pl.BlockSpec((1, 1, tr, C), lambda ch, r, core_ref: (ch, core_ref[0], r, 0)),
                      pl.BlockSpec((1, tr, C), lambda ch, r, core_ref: (ch, r, 0))],
            out_specs=[out, out]),
        out_shape=[jax.ShapeDtypeStruct((4, R, C), F32), jax.ShapeDtypeStruct((4, R, C), BF16)],
        compiler_params=_params(("arbitrary", "arbitrary")),
    )(core, g4, recv)


def _adam_math(w, g, m, v):
    m = ADAM_B1 * m + (1.0 - ADAM_B1) * g
    v = ADAM_B2 * v + (1.0 - ADAM_B2) * jnp.square(g)
    m_hat = m / (1.0 - ADAM_B1 ** ADAM_STEP)
    v_hat = v / (1.0 - ADAM_B2 ** ADAM_STEP)
    delta = -ADAM_LR * (m_hat / (jnp.sqrt(v_hat) + ADAM_EPS) + ADAM_WD * w)
    return delta, m, v


def _adam_big(w, m, v, chip_sums, recv, chip, name):
    R, C = w.shape
    tr = _row_tile(R, C)

    def body(chip_ref, w_ref, m_ref, v_ref, s_ref, r_ref, g_ref, d_ref, nm_ref, nv_ref):
        g = s_ref[0] + r_ref[0].astype(F32) + r_ref[1].astype(F32) + r_ref[2].astype(F32)
        g_ref[...] = g
        d_ref[...], nm_ref[...], nv_ref[...] = _adam_math(w_ref[...], g, m_ref[...], v_ref[...])

    blk = pl.BlockSpec((tr, C), lambda r, chip_ref: (r, 0))
    return pl.pallas_call(
        body, name=name,
        grid_spec=pltpu.PrefetchScalarGridSpec(
            num_scalar_prefetch=1, grid=(R // tr,),
            in_specs=[blk, blk, blk, pl.BlockSpec((1, tr, C), lambda r, chip_ref: (chip_ref[0], r, 0)),
                      pl.BlockSpec((3, tr, C), lambda r, chip_ref: (0, r, 0))],
            out_specs=[blk] * 4),
        out_shape=[jax.ShapeDtypeStruct((R, C), F32)] * 4,
        compiler_params=_params(("arbitrary",)),
    )(chip, w, m, v, chip_sums, recv)


def _sum_partials(partials, name):
    def body(p_ref, g_ref):
        g = p_ref[0]
        for d in range(1, N_DEV):
            g = g + p_ref[d]
        g_ref[...] = g

    return pl.pallas_call(body, name=name, out_shape=jax.ShapeDtypeStruct(partials.shape[1:], F32))(partials)


def _adam_small(ws, ms, vs, gs):
    n = len(ws)

    def body(*refs):
        w_refs, m_refs, v_refs, g_refs = (refs[i * n:(i + 1) * n] for i in range(4))
        d_refs, nm_refs, nv_refs = (refs[(4 + i) * n:(5 + i) * n] for i in range(3))
        for j in range(n):
            d_refs[j][...], nm_refs[j][...], nv_refs[j][...] = _adam_math(
                w_refs[j][...], g_refs[j][...], m_refs[j][...], v_refs[j][...])

    outs = pl.pallas_call(body, name="adam_small", out_shape=[jax.ShapeDtypeStruct(w.shape, F32) for w in ws] * 3,
                          compiler_params=_params(None, VMEM_MID))(*ws, *ms, *vs, *gs)
    return outs[:n], outs[n:2 * n], outs[2 * n:]


PACK_QUANTUM = SUBLANES * LANES


def _pack(named, names):
    parts = []
    for nme in names:
        flat = named[nme].reshape(-1)
        parts.append(jnp.pad(flat, (0, -flat.size % PACK_QUANTUM)))
    return jnp.concatenate(parts).reshape(-1, LANES)


def _unpack(packed, shapes, names):
    flat = packed.reshape(-1)
    out, pos = {}, 0
    for nme in names:
        size = math.prod(shapes[nme])
        out[nme] = flat[pos:pos + size].reshape(shapes[nme])
        pos += size + (-size % PACK_QUANTUM)
    return out


BIG = ("w_in", "w_glu", "w_attn_branch", "w_ssm_branch", "w_out", "w_ff_in", "w_ff_out")
COLUMN_SHARDED = ("w_in", "w_attn_branch", "w_ssm_branch", "w_ff_in")
SMALL = ("norm_mix_pre", "norm_mix_post", "norm_mlp_pre", "norm_mlp_post", "rel_bias", "sinks", "lam_re", "lam_im",
         "log_dt", "b_re", "b_im", "c_re", "c_im", "d_skip")
SMALL_BEFORE_IN_PROJ = SMALL[1:]
ALL_WEIGHTS = ("norm_mix_pre", "norm_mix_post", "norm_mlp_pre", "norm_mlp_post", "w_in", "rel_bias", "sinks", "lam_re",
               "lam_im", "log_dt", "b_re", "b_im", "c_re", "c_im", "d_skip", "w_glu", "w_attn_branch", "w_ssm_branch",
               "w_out", "w_ff_in", "w_ff_out")


COLUMN_TILE_ROWS = 256


def _assemble_columns(blocks):
    _, R, C = blocks.shape
    tr = min(COLUMN_TILE_ROWS, R)

    def body(b_ref, o_ref):
        for d in range(N_DEV):
            o_ref[:, C * d:C * (d + 1)] = b_ref[d]

    return pl.pallas_call(
        body, name="assemble_columns", grid=(R // tr,),
        in_specs=[pl.BlockSpec((N_DEV, tr, C), lambda r: (0, r, 0))], out_specs=pl.BlockSpec((tr, N_DEV * C), lambda r: (r, 0)),
        out_shape=jax.ShapeDtypeStruct((R, N_DEV * C), blocks.dtype), compiler_params=_params(("arbitrary",)))(blocks)


def _split_columns(full):
    R, C = full.shape[0], full.shape[1] // N_DEV
    tr = min(COLUMN_TILE_ROWS, R)

    def body(f_ref, o_ref):
        for d in range(N_DEV):
            o_ref[d] = f_ref[:, C * d:C * (d + 1)]

    return pl.pallas_call(
        body, name="split_columns", grid=(R // tr,),
        in_specs=[pl.BlockSpec((tr, N_DEV * C), lambda r: (r, 0))], out_specs=pl.BlockSpec((N_DEV, tr, C), lambda r: (0, r, 0)),
        out_shape=jax.ShapeDtypeStruct((N_DEV, R, C), full.dtype), compiler_params=_params(("arbitrary",)))(full)


def _full_from_gathered(name, gathered):
    _, r, c = gathered.shape
    if name in COLUMN_SHARDED:
        return jnp.transpose(gathered, (1, 0, 2)).reshape(r, N_DEV * c)
    return gathered.reshape(N_DEV * r, c)


def _blocks_from_full(name, full):
    r, c = full.shape
    if name in COLUMN_SHARDED:
        return jnp.transpose(full.reshape(r, N_DEV, c // N_DEV), (1, 0, 2))
    return full.reshape(N_DEV, r // N_DEV, c)


def kernel(x, norm_mix_pre, norm_mix_post, norm_mlp_pre, norm_mlp_post, w_in, rel_bias, sinks, lam_re, lam_im, log_dt, b_re, b_im, c_re, c_im, d_skip, w_glu, w_attn_branch, w_ssm_branch, w_out, w_ff_in, w_ff_out, loss_target, m_norm_mix_pre, m_norm_mix_post, m_norm_mlp_pre, m_norm_mlp_post, m_w_in, m_rel_bias, m_sinks, m_lam_re, m_lam_im, m_log_dt, m_b_re, m_b_im, m_c_re, m_c_im, m_d_skip, m_w_glu, m_w_attn_branch, m_w_ssm_branch, m_w_out, m_w_ff_in, m_w_ff_out, v_norm_mix_pre, v_norm_mix_post, v_norm_mlp_pre, v_norm_mlp_post, v_w_in, v_rel_bias, v_sinks, v_lam_re, v_lam_im, v_log_dt, v_b_re, v_b_im, v_c_re, v_c_im, v_d_skip, v_w_glu, v_w_attn_branch, v_w_ssm_branch, v_w_out, v_w_ff_in, v_w_ff_out):
    args = dict(locals())
    w = {n: args[n] for n in ALL_WEIGHTS}
    m = {n: args["m_" + n] for n in ALL_WEIGHTS}
    v = {n: args["v_" + n] for n in ALL_WEIGHTS}
    core = lax.axis_index("c").astype(jnp.int32).reshape(1)
    chip = (2 * lax.axis_index("x") + lax.axis_index("y")).astype(jnp.int32).reshape(1)
    xs, target = x[0], loss_target[0]
    t = _tiles(xs.shape[0])
    shard = {n: w[n][0].astype(BF16) for n in BIG}
    small = {n: (w[n] if n == "rel_bias" else w[n][0]) for n in SMALL}
    g1, g2, g3, g4 = (small[n].reshape(1, D_MODEL) for n in ("norm_mix_pre", "norm_mix_post", "norm_mlp_pre", "norm_mlp_post"))
    bucket = jnp.asarray(_bucket_table())
    rel_b, sink = small["rel_bias"], small["sinks"].reshape(1, N_HEADS)
    lam_r, lam_i = small["lam_re"].reshape(1, STATES), small["lam_im"].reshape(1, STATES)
    ldt_rep = jnp.repeat(small["log_dt"].reshape(N_GROUPS), N_STATE).reshape(1, STATES)
    bd_re, bd_im = _block_diag_in(small["b_re"]), _block_diag_in(small["b_im"])
    cm_re, cm_im = _block_diag_out(small["c_re"]).astype(BF16), _block_diag_out(small["c_im"]).astype(BF16)
    dsk = small["d_skip"].reshape(1, SSM_W)

    (g_in,) = _run_carry(_gather_carry([shard["w_in"]]), "gather_w_in")
    wf_in = _assemble_columns(g_in)
    merge_names = ("w_glu", "w_attn_branch", "w_ssm_branch", "w_out")
    (q, k, vv, u, ga, gs), gathered = _in_proj_fwd(xs, g1, wf_in, t["proj"], _gather_carry([shard[n] for n in merge_names]))
    wf = {n: _full_from_gathered(n, g) for n, g in zip(merge_names, gathered)}
    (att,), (wf_ff_in,) = _attn_fwd(q, k, vv, bucket, rel_b, sink, _gather_carry([shard["w_ff_in"]]))
    a_re, a_im, bm_re, bm_im = _ssm_prep(lam_r, lam_i, ldt_rep, bd_re, bd_im)
    (y, h_re, h_im, in_re, in_im), (wf_ff_out,) = _ssm_fwd(
        u, a_re, a_im, bm_re, bm_im, cm_re, cm_im, dsk, t["ssm_chunk"], _gather_carry([shard["w_ff_out"]]))
    x1, o, h2 = _merge_fwd(xs, y, att, ga, gs, g2, g3, wf["w_glu"], wf["w_ssm_branch"], wf["w_attn_branch"], wf["w_out"],
                           t["merge"])
    a, dfo, dx2, loss_blk, dg4 = _mlp_fwd(h2, x1, target, g4, wf_ff_in, wf_ff_out, t["mlp_fwd"])
    loss = lax.psum(loss_blk[0, 0], ("x", "y", "c"))

    def add_sibling(names, blocks, received):
        pairs = [_add_sibling(b, r, core, "add_sibling_" + n) for n, b, r in zip(names, blocks, received)]
        return [p[0] for p in pairs], [p[1] for p in pairs]

    ff_names = ("w_ff_in", "w_ff_out")
    dw_ff_in, dw_ff_out, dh2 = _mlp_bwd(dfo, a, h2, wf_ff_in, wf_ff_out, t["mlp_bwd"])
    ff_blocks = [dw_ff_in, dw_ff_out]
    (dx1, dga, dgs, datt, dy, dw_glu, dw_ssm, dw_attn, dw_out, dg2, dg3), ff_recv = _merge_bwd(
        dh2, dx2, x1, o, y, att, ga, gs, g2, g3, wf["w_glu"], wf["w_ssm_branch"], wf["w_attn_branch"], wf["w_out"],
        t["merge_bwd"], _sibling_carry(ff_blocks))
    ff_sums, ff_sums_bf = add_sibling(ff_names, ff_blocks, ff_recv)
    (du, dbm_re, dbm_im, dcm_re, dcm_im, da_re, da_im, dd_skip), ff_from_chips = _ssm_bwd(
        dy, u, h_re, h_im, in_re, in_im, a_re, a_im, bm_re, bm_im, cm_re, cm_im, dsk, t["ssm_chunk"], _chips_carry(ff_sums_bf))
    dbd_re, dbd_im, dlam_re, dlam_im, dldt_rep = _ssm_prep_bwd(lam_r, lam_i, ldt_rep, bd_re, bd_im, dbm_re, dbm_im, da_re, da_im)
    dlog_dt = _group_sum(dldt_rep.reshape(N_GROUPS, N_STATE))
    merge_blocks = [_blocks_from_full(n, g) for n, g in zip(merge_names, (dw_glu, dw_attn, dw_ssm, dw_out))]
    (dq, dk, dv, attn_small), merge_recv = _attn_bwd(q, k, vv, datt, bucket, rel_b, sink, _sibling_carry(merge_blocks))
    merge_sums, merge_sums_bf = add_sibling(merge_names, merge_blocks, merge_recv)
    small_grads = dict(
        norm_mix_post=dg2, norm_mlp_pre=dg3, norm_mlp_post=dg4,
        rel_bias=attn_small[:, :N_BUCKETS, 0].T, sinks=attn_small[:, N_BUCKETS, 0],
        lam_re=dlam_re, lam_im=dlam_im, log_dt=dlog_dt,
        b_re=_block_diag_in_grad(dbd_re), b_im=_block_diag_in_grad(dbd_im),
        c_re=_block_diag_out_grad(dcm_re), c_im=_block_diag_out_grad(dcm_im), d_skip=dd_skip)
    shapes = {n: w[n].shape for n in SMALL}
    packed_partial = _pack({n: small_grads[n].reshape(shapes[n]) for n in SMALL_BEFORE_IN_PROJ}, SMALL_BEFORE_IN_PROJ)
    (grad_x, dw_in, dg1), carried = _in_proj_bwd(
        xs, g1, wf_in, dx1, (dq, dk, dv, du.astype(BF16), dga, dgs), t["proj_bwd"],
        _join(_chips_carry(merge_sums_bf), _gather_carry([packed_partial])))
    merge_from_chips, partials = carried[:-1], carried[-1]

    in_blocks = [_split_columns(dw_in)]
    in_recv = _run_carry(_sibling_carry(in_blocks), "reduce_sibling_w_in")
    in_sums, in_sums_bf = add_sibling(("w_in",), in_blocks, in_recv)
    in_from_chips, dg1_partials = _run_carry(_join(_chips_carry(in_sums_bf), _gather_carry([dg1])), "reduce_chips_w_in")

    grads, deltas, new_m, new_v = {}, {}, {}, {}
    sums = dict(zip(ff_names + merge_names + ("w_in",), ff_sums + merge_sums + in_sums))
    received = dict(zip(ff_names + merge_names + ("w_in",), ff_from_chips + merge_from_chips + [in_from_chips]))
    for n in BIG:
        outs = _adam_big(w[n][0], m[n][0], v[n][0], sums[n], received[n], chip, "adam_" + n)
        grads[n], deltas[n], new_m[n], new_v[n] = (o[None] for o in outs)

    grads.update(_unpack(_sum_partials(partials, "sum_small_grads"), shapes, SMALL_BEFORE_IN_PROJ))
    grads["norm_mix_pre"] = _sum_partials(dg1_partials, "sum_norm_grad")
    small_out = _adam_small(*[[d[n] for n in SMALL] for d in (w, m, v, grads)])
    for store, vals in zip((deltas, new_m, new_v), small_out):
        store.update(zip(SMALL, vals))

    return (loss, grad_x[None], *[grads[n] for n in ALL_WEIGHTS], *[deltas[n] for n in ALL_WEIGHTS],
            *[new_m[n] for n in ALL_WEIGHTS], *[new_v[n] for n in ALL_WEIGHTS])
```

```python
import functools
import math

import jax
import jax.numpy as jnp
import numpy as np
from jax import lax
from jax.experimental import pallas as pl
from jax.experimental.pallas import tpu as pltpu

F32 = jnp.float32
BF16 = jnp.bfloat16

D_MODEL = 1024
N_HEADS = 8
HEAD_DIM = 64
ATTN_W = 512
KV_W = 128
BLOCK = 128
N_BUCKETS = 32
SSM_W = 512
N_GROUPS = 32
N_STATE = 64
GROUP_CH = 16
STATES = N_GROUPS * N_STATE
D_FF = 4096
IN_W = 3328
SPLITS = (0, 512, 640, 768, 1280, 2304, 3328)
RMS_EPS = 1e-6
NEG_INF = -1e30
SUBLANES = 8
LANES = 128
SSM_LANE_BLOCK = 512
N_SSM_BLOCKS = STATES // SSM_LANE_BLOCK
VMEM_BIG = 52 * 1024 * 1024
VMEM_MID = 40 * 1024 * 1024

ADAM_LR = 0.001
ADAM_B1 = 0.9
ADAM_B2 = 0.999
ADAM_EPS = 1e-08
ADAM_WD = 0.01
ADAM_STEP = 10

N_DEV = 8


def _dot(a, b):
    return jnp.dot(a, b, preferred_element_type=F32)


def _dot_nt(a, b):
    return lax.dot_general(a, b, (((1,), (1,)), ((), ())), preferred_element_type=F32)


def _dot_tn(a, b):
    return lax.dot_general(a, b, (((0,), (0,)), ((), ())), preferred_element_type=F32)


def _rms_scale(x):
    return lax.rsqrt(jnp.mean(x * x, axis=-1, keepdims=True) + RMS_EPS)


def _rms_bwd(dy, x, r, g):
    t = dy * g
    dx = r * t - x * (r * r * r) * jnp.mean(t * x, axis=-1, keepdims=True)
    dg = jnp.sum(dy * x * r, axis=0, keepdims=True)
    return dx, dg


def _const_spec(shape):
    nd = len(shape)
    return pl.BlockSpec(shape, lambda *_: (0,) * nd, pipeline_mode=pl.Buffered(1))


def _whole(shape):
    nd = len(shape)
    return pl.BlockSpec(shape, lambda *_: (0,) * nd)


def _params(sem, vmem=None):
    return pltpu.CompilerParams(dimension_semantics=sem, vmem_limit_bytes=vmem)


MESH_IDS = pl.DeviceIdType.MESH
HBM_SPEC = pl.BlockSpec(memory_space=pl.ANY)


class _Carry:
    def __init__(self, inputs, out_shapes, sems, start, finish):
        self.inputs, self.out_shapes, self.sems, self.start, self.finish = list(inputs), list(out_shapes), list(sems), start, finish


def _join(a, b):
    na_in, na_out, na_sem = len(a.inputs), len(a.out_shapes), len(a.sems)

    def start(ins, outs, sems):
        a.start(ins[:na_in], outs[:na_out], sems[:na_sem])
        b.start(ins[na_in:], outs[na_out:], sems[na_sem:])

    def finish(ins, outs, sems):
        a.finish(ins[:na_in], outs[:na_out], sems[:na_sem])
        b.finish(ins[na_in:], outs[na_out:], sems[na_sem:])

    return _Carry(a.inputs + b.inputs, a.out_shapes + b.out_shapes, a.sems + b.sems, start, finish)


def _hosted_call(body, carry, edge, *, name, grid, in_specs, out_specs, out_shape, scratch_shapes, compiler_params, inputs):
    n_in, n_out = len(in_specs), len(out_specs)
    if carry is None:
        outs = pl.pallas_call(body, name=name, grid=grid, in_specs=in_specs, out_specs=out_specs, out_shape=out_shape,
                              scratch_shapes=scratch_shapes, compiler_params=compiler_params)(*inputs)
        return list(outs), []
    c_in, c_out, c_sem = len(carry.inputs), len(carry.out_shapes), len(carry.sems)

    def wrapped(*refs):
        ins, refs = refs[:n_in], refs[n_in:]
        cins, refs = refs[:c_in], refs[c_in:]
        outs, refs = refs[:n_out], refs[n_out:]
        couts, refs = refs[:c_out], refs[c_out:]
        scratch, csems = refs[:len(refs) - c_sem], refs[len(refs) - c_sem:]
        first, last = edge()

        @pl.when(first)
        def _():
            carry.start(cins, couts, csems)

        body(*ins, *outs, *scratch)

        @pl.when(last)
        def _():
            carry.finish(cins, couts, csems)

    outs = pl.pallas_call(
        wrapped, name=name, grid=grid, in_specs=list(in_specs) + [HBM_SPEC] * c_in,
        out_specs=list(out_specs) + [HBM_SPEC] * c_out, out_shape=list(out_shape) + carry.out_shapes,
        scratch_shapes=list(scratch_shapes) + carry.sems, compiler_params=compiler_params)(*inputs, *carry.inputs)
    return list(outs[:n_out]), list(outs[n_out:])


def _edge_1d(n_steps):
    return lambda: (pl.program_id(0) == 0, pl.program_id(0) == n_steps - 1)


def _edge_2d(n0, n1):
    return lambda: ((pl.program_id(0) == 0) & (pl.program_id(1) == 0),
                    (pl.program_id(0) == n0 - 1) & (pl.program_id(1) == n1 - 1))


def _run_carry(carry, name):
    c_in, c_out = len(carry.inputs), len(carry.out_shapes)

    def body(*refs):
        ins, outs, sems = refs[:c_in], refs[c_in:c_in + c_out], refs[c_in + c_out:]
        carry.start(ins, outs, sems)
        carry.finish(ins, outs, sems)

    return pl.pallas_call(body, name=name, in_specs=[HBM_SPEC] * c_in, out_specs=[HBM_SPEC] * c_out,
                          out_shape=carry.out_shapes, scratch_shapes=carry.sems)(*carry.inputs)


def _in_proj_fwd(x, g1, w_in_t, tile, carry=None):
    T = x.shape[0]

    def body(x_ref, g_ref, w_ref, q_ref, k_ref, v_ref, u_ref, ga_ref, gs_ref):
        xv = x_ref[...]
        h = (xv * _rms_scale(xv) * g_ref[...]).astype(BF16)
        outs = (q_ref, k_ref, v_ref, u_ref, ga_ref, gs_ref)
        for p, o_ref in enumerate(outs):
            o_ref[...] = _dot_nt(h, w_ref[SPLITS[p]:SPLITS[p + 1], :]).astype(o_ref.dtype)

    widths = [SPLITS[p + 1] - SPLITS[p] for p in range(6)]
    dtypes = [BF16, BF16, BF16, F32, F32, F32]
    return _hosted_call(
        body, carry, _edge_1d(T // tile), name="in_proj_fwd", grid=(T // tile,),
        in_specs=[pl.BlockSpec((tile, D_MODEL), lambda i: (i, 0)), _const_spec((1, D_MODEL)), _const_spec((IN_W, D_MODEL))],
        out_specs=[pl.BlockSpec((tile, w), lambda i: (i, 0)) for w in widths],
        out_shape=[jax.ShapeDtypeStruct((T, w), dt) for w, dt in zip(widths, dtypes)],
        scratch_shapes=[], compiler_params=_params(("arbitrary",), VMEM_MID), inputs=(x, g1, w_in_t))


def _in_proj_bwd(x, g1, w_in_t, dx1, dparts, tile, carry=None):
    T = x.shape[0]
    widths = [SPLITS[p + 1] - SPLITS[p] for p in range(6)]
    n_steps = T // tile

    def body(x_ref, g_ref, w_ref, dx1_ref, dq, dk, dv, du, dga, dgs, gx_ref, dw_hbm, dg_ref, acc_ref):
        i = pl.program_id(0)
        xv = x_ref[...]
        r = _rms_scale(xv)
        g = g_ref[...]
        h = (xv * r * g).astype(BF16)
        dh = jnp.zeros((tile, D_MODEL), F32)
        for p, d_ref in enumerate((dq, dk, dv, du, dga, dgs)):
            dp = d_ref[...]
            rows = slice(SPLITS[p], SPLITS[p + 1])
            dh = dh + _dot(dp, w_ref[rows, :])
            contrib = _dot_tn(dp, h)

            @pl.when(i == 0)
            def _():
                acc_ref[rows, :] = contrib

            @pl.when(i > 0)
            def _():
                acc_ref[rows, :] += contrib

        dxn, dg = _rms_bwd(dh, xv, r, g)
        gx_ref[...] = dx1_ref[...] + dxn

        @pl.when(i == 0)
        def _():
            dg_ref[...] = dg

        @pl.when(i > 0)
        def _():
            dg_ref[...] += dg

        @pl.when(i == n_steps - 1)
        def _():
            pltpu.sync_copy(acc_ref, dw_hbm)

    tok = lambda w: pl.BlockSpec((tile, w), lambda i: (i, 0))
    return _hosted_call(
        body, carry, _edge_1d(n_steps), name="in_proj_bwd", grid=(n_steps,),
        in_specs=[tok(D_MODEL), _const_spec((1, D_MODEL)), _const_spec((IN_W, D_MODEL)), tok(D_MODEL)] + [tok(w) for w in widths],
        out_specs=[tok(D_MODEL), HBM_SPEC, pl.BlockSpec((1, D_MODEL), lambda i: (0, 0))],
        out_shape=[jax.ShapeDtypeStruct((T, D_MODEL), F32), jax.ShapeDtypeStruct((IN_W, D_MODEL), F32),
                   jax.ShapeDtypeStruct((1, D_MODEL), F32)],
        scratch_shapes=[pltpu.VMEM((IN_W, D_MODEL), F32)],
        compiler_params=_params(("arbitrary",), VMEM_BIG), inputs=(x, g1, w_in_t, dx1, *dparts))


def _bucket_table():
    qi = np.arange(BLOCK)[:, None]
    kj = np.arange(2 * BLOCK)[None, :]
    dist = qi + BLOCK - kj
    max_exact = N_BUCKETS // 2
    d = np.maximum(dist, 0)
    df = np.maximum(d, 1).astype(np.float32)
    large = max_exact + (np.log(df / np.float32(max_exact)) / np.float32(math.log(BLOCK / max_exact))
                         * np.float32(N_BUCKETS - max_exact)).astype(np.int32)
    large = np.minimum(large, N_BUCKETS - 1)
    bucket = np.where(d < max_exact, d, large)
    return np.where((dist >= 0) & (dist < BLOCK), bucket, -1).astype(np.int32)


def _build_bias(bucket_ref, rb_ref, bias_ref):
    bk = bucket_ref[...]
    for h in range(N_HEADS):
        def add(b, acc, h=h):
            return acc + jnp.where(bk == b, rb_ref[b, h], 0.0)
        bias_ref[h] = lax.fori_loop(0, N_BUCKETS, add, jnp.zeros((BLOCK, 2 * BLOCK), F32))


def _kv_variants(prev_ref, cur_ref):
    cat = jnp.concatenate([prev_ref[...], cur_ref[...]], axis=0)
    lo = lax.broadcasted_iota(jnp.int32, cat.shape, 1) < HEAD_DIM
    zero = jnp.zeros_like(cat)
    head0_lo = jnp.where(lo, cat, zero)
    head1_hi = jnp.where(lo, zero, cat)
    return ((head0_lo, pltpu.roll(head0_lo, HEAD_DIM, 1)), (pltpu.roll(head1_hi, HEAD_DIM, 1), head1_hi))


def _merge_kv_grads(g):
    lo = lax.broadcasted_iota(jnp.int32, g[0][0].shape, 1) < HEAD_DIM
    return jnp.where(lo, g[0][0] + pltpu.roll(g[0][1], HEAD_DIM, 1), g[1][1] + pltpu.roll(g[1][0], HEAD_DIM, 1))


def _head_lanes(h):
    return slice((h // 2) * LANES, (h // 2 + 1) * LANES)


def _attn_probs(q_ref, kvar, bias_ref, sk_ref, valid, s_ref):
    for h in range(N_HEADS):
        s_ref[h] = _dot_nt(q_ref[:, _head_lanes(h)], kvar[h // 4][h % 2])
    head = lax.broadcasted_iota(jnp.int32, (N_HEADS, 1, 1), 0)
    sink = jnp.zeros((N_HEADS, 1, 1), F32)
    for h in range(N_HEADS):
        sink = jnp.where(head == h, sk_ref[0, h], sink)
    s = jnp.where(valid[None], s_ref[...] * (HEAD_DIM ** -0.5) + bias_ref[...], NEG_INF)
    m = jnp.maximum(jnp.max(s, axis=-1, keepdims=True), sink)
    p = jnp.exp(s - m)
    e_sink = jnp.exp(sink - m)
    inv = 1.0 / (jnp.sum(p, axis=-1, keepdims=True) + e_sink)
    return p * inv, e_sink * inv


def _attn_valid(bucket_ref, n):
    col = lax.broadcasted_iota(jnp.int32, (BLOCK, 2 * BLOCK), 1)
    return (bucket_ref[...] >= 0) & ((n > 0) | (col >= BLOCK))


def _attn_fwd(q, k, v, bucket, rel_bias, sinks, carry=None):
    T = q.shape[0]
    nb = T // BLOCK

    def body(q_ref, kc_ref, kp_ref, vc_ref, vp_ref, bucket_ref, rb_ref, sk_ref, o_ref, bias_ref, s_ref, p_ref):
        n = pl.program_id(0)

        @pl.when(n == 0)
        def _():
            _build_bias(bucket_ref, rb_ref, bias_ref)

        kvar = _kv_variants(kp_ref, kc_ref)
        vvar = _kv_variants(vp_ref, vc_ref)
        pr, _ = _attn_probs(q_ref, kvar, bias_ref, sk_ref, _attn_valid(bucket_ref, n), s_ref)
        p_ref[...] = pr.astype(BF16)
        for m in range(N_HEADS // 2):
            acc = _dot(p_ref[2 * m], vvar[m // 2][0]) + _dot(p_ref[2 * m + 1], vvar[m // 2][1])
            o_ref[:, m * LANES:(m + 1) * LANES] = acc.astype(o_ref.dtype)

    cur = lambda w: pl.BlockSpec((BLOCK, w), lambda n: (n, 0))
    prev = lambda w: pl.BlockSpec((BLOCK, w), lambda n: (jnp.maximum(n - 1, 0), 0))
    smem = pl.BlockSpec(memory_space=pltpu.SMEM)
    return _hosted_call(
        body, carry, _edge_1d(nb), name="attn_fwd", grid=(nb,),
        in_specs=[cur(ATTN_W), cur(KV_W), prev(KV_W), cur(KV_W), prev(KV_W), _const_spec((BLOCK, 2 * BLOCK)), smem, smem],
        out_specs=[cur(ATTN_W)],
        out_shape=[jax.ShapeDtypeStruct((T, ATTN_W), BF16)],
        scratch_shapes=[pltpu.VMEM((N_HEADS, BLOCK, 2 * BLOCK), F32), pltpu.VMEM((N_HEADS, BLOCK, 2 * BLOCK), F32),
                        pltpu.VMEM((N_HEADS, BLOCK, 2 * BLOCK), BF16)],
        compiler_params=_params(("arbitrary",)), inputs=(q, k, k, v, v, bucket, rel_bias, sinks))


ATTN_SMALL_ROWS = N_BUCKETS + SUBLANES


def _attn_bwd(q, k, v, datt, bucket, rel_bias, sinks, carry=None):
    T = q.shape[0]
    nb = T // BLOCK

    def body(q_ref, do_ref, kc_ref, kp_ref, vc_ref, vp_ref, bucket_ref, rb_ref, sk_ref,
             dq_ref, dk_ref, dv_ref, small_ref, bias_ref, ds_sum_ref, dsink_ref, kcarry_ref, vcarry_ref,
             s_ref, dp_ref, p_ref, dsc_ref):
        n = pl.program_id(0)

        @pl.when(n == 0)
        def _():
            _build_bias(bucket_ref, rb_ref, bias_ref)
            ds_sum_ref[...] = jnp.zeros_like(ds_sum_ref)
            dsink_ref[...] = jnp.zeros_like(dsink_ref)
            kcarry_ref[...] = jnp.zeros_like(kcarry_ref)
            vcarry_ref[...] = jnp.zeros_like(vcarry_ref)

        @pl.when(n < nb)
        def _():
            kvar = _kv_variants(kp_ref, kc_ref)
            vvar = _kv_variants(vp_ref, vc_ref)
            pr, p_sink = _attn_probs(q_ref, kvar, bias_ref, sk_ref, _attn_valid(bucket_ref, n), s_ref)
            for h in range(N_HEADS):
                dp_ref[h] = _dot_nt(do_ref[:, _head_lanes(h)], vvar[h // 4][h % 2])
            dp = dp_ref[...]
            dsum = jnp.sum(pr * dp, axis=-1, keepdims=True)
            ds = pr * (dp - dsum)
            ds_sum_ref[...] += ds
            dsink_ref[...] -= jnp.sum(p_sink * dsum, axis=1, keepdims=True)
            dsc_ref[...] = (ds * (HEAD_DIM ** -0.5)).astype(BF16)
            p_ref[...] = pr.astype(BF16)
            for m in range(N_HEADS // 2):
                dqm = _dot(dsc_ref[2 * m], kvar[m // 2][0]) + _dot(dsc_ref[2 * m + 1], kvar[m // 2][1])
                dq_ref[:, m * LANES:(m + 1) * LANES] = dqm.astype(dq_ref.dtype)
            dk_var = [[None, None], [None, None]]
            dv_var = [[None, None], [None, None]]
            for kvh in range(2):
                for e in range(2):
                    heads = [h for h in range(N_HEADS) if h // 4 == kvh and h % 2 == e]
                    dk_var[kvh][e] = sum(_dot_tn(dsc_ref[h], q_ref[:, _head_lanes(h)]) for h in heads)
                    dv_var[kvh][e] = sum(_dot_tn(p_ref[h], do_ref[:, _head_lanes(h)]) for h in heads)
            dk_cat = _merge_kv_grads(dk_var)
            dv_cat = _merge_kv_grads(dv_var)

            @pl.when(n > 0)
            def _():
                dk_ref[...] = (kcarry_ref[...] + dk_cat[:BLOCK]).astype(dk_ref.dtype)
                dv_ref[...] = (vcarry_ref[...] + dv_cat[:BLOCK]).astype(dv_ref.dtype)

            kcarry_ref[...] = dk_cat[BLOCK:]
            vcarry_ref[...] = dv_cat[BLOCK:]

        @pl.when(n == nb)
        def _():
            dk_ref[...] = kcarry_ref[...].astype(dk_ref.dtype)
            dv_ref[...] = vcarry_ref[...].astype(dv_ref.dtype)
            bk = bucket_ref[...]
            row = lax.broadcasted_iota(jnp.int32, (N_HEADS, ATTN_SMALL_ROWS, LANES), 1)

            def add(b, acc):
                masked = jnp.where((bk == b)[None], ds_sum_ref[...], 0.0)
                val = jnp.sum(jnp.sum(masked, axis=1, keepdims=True), axis=2, keepdims=True)
                return acc + jnp.where(row == b, val, 0.0)

            small_ref[...] = lax.fori_loop(0, N_BUCKETS, add, jnp.where(row == N_BUCKETS, dsink_ref[...], 0.0))

    last = nb - 1
    cur = lambda w: pl.BlockSpec((BLOCK, w), lambda n: (jnp.minimum(n, last), 0))
    prev = lambda w: pl.BlockSpec((BLOCK, w), lambda n: (jnp.clip(n - 1, 0, last), 0))
    smem = pl.BlockSpec(memory_space=pltpu.SMEM)
    return _hosted_call(
        body, carry, _edge_1d(nb + 1), name="attn_bwd", grid=(nb + 1,),
        in_specs=[cur(ATTN_W), cur(ATTN_W), cur(KV_W), prev(KV_W), cur(KV_W), prev(KV_W),
                  _const_spec((BLOCK, 2 * BLOCK)), smem, smem],
        out_specs=[cur(ATTN_W), prev(KV_W), prev(KV_W),
                   pl.BlockSpec((N_HEADS, ATTN_SMALL_ROWS, LANES), lambda n: (0, 0, 0))],
        out_shape=[jax.ShapeDtypeStruct((T, ATTN_W), BF16), jax.ShapeDtypeStruct((T, KV_W), BF16),
                   jax.ShapeDtypeStruct((T, KV_W), BF16), jax.ShapeDtypeStruct((N_HEADS, ATTN_SMALL_ROWS, LANES), F32)],
        scratch_shapes=[pltpu.VMEM((N_HEADS, BLOCK, 2 * BLOCK), F32), pltpu.VMEM((N_HEADS, BLOCK, 2 * BLOCK), F32),
                        pltpu.VMEM((N_HEADS, 1, 1), F32), pltpu.VMEM((BLOCK, KV_W), F32), pltpu.VMEM((BLOCK, KV_W), F32),
                        pltpu.VMEM((N_HEADS, BLOCK, 2 * BLOCK), F32), pltpu.VMEM((N_HEADS, BLOCK, 2 * BLOCK), F32),
                        pltpu.VMEM((N_HEADS, BLOCK, 2 * BLOCK), BF16), pltpu.VMEM((N_HEADS, BLOCK, 2 * BLOCK), BF16)],
        compiler_params=_params(("arbitrary",)), inputs=(q, datt, k, k, v, v, bucket, rel_bias, sinks))


def _cmul(ar, ai, br, bi):
    return ar * br - ai * bi, ar * bi + ai * br


def _ssm_discretize(lr, li, ldt):
    dt = jnp.exp(ldt)
    mag = jnp.exp(lr * dt)
    ab_re = mag * jnp.cos(li * dt)
    ab_im = mag * jnp.sin(li * dt)
    nr = ab_re - 1.0
    den = lr * lr + li * li
    f_re = (nr * lr + ab_im * li) / den
    f_im = (ab_im * lr - nr * li) / den
    return ab_re, ab_im, f_re, f_im


def _ssm_prep(lam_re, lam_im, ldt_rep, bd_re, bd_im):
    def body(lr_ref, li_ref, ldt_ref, bdr_ref, bdi_ref, ar_ref, ai_ref, br_ref, bi_ref):
        ab_re, ab_im, f_re, f_im = _ssm_discretize(lr_ref[...], li_ref[...], ldt_ref[...])
        ar_ref[...] = ab_re
        ai_ref[...] = ab_im
        bdr, bdi = bdr_ref[0], bdi_ref[0]
        br_ref[0] = (bdr * f_re - bdi * f_im).astype(BF16)
        bi_ref[0] = (bdi * f_re + bdr * f_im).astype(BF16)

    row = pl.BlockSpec((1, SSM_LANE_BLOCK), lambda j: (0, j))
    mat = pl.BlockSpec((1, LANES, SSM_LANE_BLOCK), lambda j: (j, 0, 0))
    return pl.pallas_call(
        body, name="ssm_prep", grid=(N_SSM_BLOCKS,),
        in_specs=[row, row, row, mat, mat], out_specs=[row, row, mat, mat],
        out_shape=[jax.ShapeDtypeStruct((1, STATES), F32)] * 2 + [jax.ShapeDtypeStruct((N_SSM_BLOCKS, LANES, SSM_LANE_BLOCK), BF16)] * 2,
        compiler_params=_params(("arbitrary",)),
    )(lam_re, lam_im, ldt_rep, bd_re, bd_im)


def _ssm_prep_bwd(lam_re, lam_im, ldt_rep, bd_re, bd_im, dbr, dbi, da_re, da_im):
    def body(lr_ref, li_ref, ldt_ref, bdr_ref, bdi_ref, dbr_ref, dbi_ref, dar_ref, dai_ref,
             dbdr_ref, dbdi_ref, dlr_ref, dli_ref, dldt_ref):
        lr, li, ldt = lr_ref[...], li_ref[...], ldt_ref[...]
        (_, _, f_re, f_im), vjp = jax.vjp(_ssm_discretize, lr, li, ldt)
        bdr, bdi, gbr, gbi = bdr_ref[0], bdi_ref[0], dbr_ref[0], dbi_ref[0]
        dbdr_ref[0] = gbr * f_re + gbi * f_im
        dbdi_ref[0] = gbi * f_re - gbr * f_im
        df_re = jnp.sum(gbr * bdr + gbi * bdi, axis=0, keepdims=True)
        df_im = jnp.sum(gbi * bdr - gbr * bdi, axis=0, keepdims=True)
        dlr, dli, dldt = vjp((dar_ref[...], dai_ref[...], df_re, df_im))
        dlr_ref[...] = dlr
        dli_ref[...] = dli
        dldt_ref[...] = dldt

    row = pl.BlockSpec((1, SSM_LANE_BLOCK), lambda j: (0, j))
    mat = pl.BlockSpec((1, LANES, SSM_LANE_BLOCK), lambda j: (j, 0, 0))
    mat_shape = jax.ShapeDtypeStruct((N_SSM_BLOCKS, LANES, SSM_LANE_BLOCK), F32)
    row_shape = jax.ShapeDtypeStruct((1, STATES), F32)
    return pl.pallas_call(
        body, name="ssm_prep_bwd", grid=(N_SSM_BLOCKS,),
        in_specs=[row, row, row, mat, mat, mat, mat, row, row], out_specs=[mat, mat, row, row, row],
        out_shape=[mat_shape, mat_shape, row_shape, row_shape, row_shape],
        compiler_params=_params(("arbitrary",)),
    )(lam_re, lam_im, ldt_rep, bd_re, bd_im, dbr, dbi, da_re, da_im)


def _group_sum(x):
    def body(x_ref, o_ref):
        o_ref[...] = jnp.sum(x_ref[...], axis=1, keepdims=True)
    return pl.pallas_call(body, name="ssm_group_sum", grid=(1,), in_specs=[_whole(x.shape)], out_specs=_whole((N_GROUPS, 1)),
                          out_shape=jax.ShapeDtypeStruct((N_GROUPS, 1), F32))(x)


def _power_table(ar, ai, p_re_ref, p_im_ref, steps):
    shape = (SUBLANES, SSM_LANE_BLOCK)
    p_re_ref[0:SUBLANES] = jnp.broadcast_to(ar, shape)
    p_im_ref[0:SUBLANES] = jnp.broadcast_to(ai, shape)
    m = 1
    while m < steps:
        rows = m * SUBLANES
        top_re = p_re_ref[rows - SUBLANES:rows]
        top_im = p_im_ref[rows - SUBLANES:rows]
        cur_re = p_re_ref[0:rows].reshape(m, SUBLANES, SSM_LANE_BLOCK)
        cur_im = p_im_ref[0:rows].reshape(m, SUBLANES, SSM_LANE_BLOCK)
        nxt_re, nxt_im = _cmul(cur_re, cur_im, top_re[None], top_im[None])
        p_re_ref[rows:2 * rows] = nxt_re.reshape(rows, SSM_LANE_BLOCK)
        p_im_ref[rows:2 * rows] = nxt_im.reshape(rows, SSM_LANE_BLOCK)
        m *= 2


def _to_segments(src_ref, dst_ref, steps):
    for s in range(SUBLANES):
        dst_ref[pl.ds(s, steps, stride=SUBLANES), :] = src_ref[s * steps:(s + 1) * steps, :]


def _from_segments(src_ref, dst_ref, steps):
    for s in range(SUBLANES):
        dst_ref[s * steps:(s + 1) * steps, :] = src_ref[pl.ds(s, steps, stride=SUBLANES), :]


def _segment_carries(e_re, e_im, an_re, an_im, c_re, c_im, reverse):
    order = range(SUBLANES - 1, -1, -1) if reverse else range(SUBLANES)
    ins_re, ins_im = [None] * SUBLANES, [None] * SUBLANES
    for s in order:
        ins_re[s], ins_im[s] = c_re, c_im
        pr, pi = _cmul(an_re, an_im, c_re, c_im)
        c_re = e_re[s:s + 1] + pr
        c_im = e_im[s:s + 1] + pi
    return jnp.concatenate(ins_re, axis=0), jnp.concatenate(ins_im, axis=0), c_re, c_im


def _ssm_fwd(u, a_re, a_im, b_re, b_im, c_re, c_im, d_skip, chunk, carry=None):
    T = u.shape[0]
    nc = T // chunk
    steps = chunk // SUBLANES
    blk = SSM_LANE_BLOCK

    def body(u_ref, ar_ref, ai_ref, br_ref, bi_ref, cr_ref, ci_ref, dk_ref,
             y_ref, hr_ref, hi_ref, inr_ref, ini_ref, useg_ref, yseg_ref, pr_ref, pi_ref, carry_ref):
        c = pl.program_id(1)
        ar, ai = ar_ref[...], ai_ref[...]

        @pl.when(c == 0)
        def _():
            _power_table(ar, ai, pr_ref, pi_ref, steps)
            carry_ref[...] = jnp.zeros_like(carry_ref)

        _to_segments(u_ref, useg_ref, steps)
        ub = useg_ref[...].astype(BF16)
        hr_ref[...] = _dot(ub, br_ref[0])
        hi_ref[...] = _dot(ub, bi_ref[0])
        ar8 = jnp.broadcast_to(ar, (SUBLANES, blk))
        ai8 = jnp.broadcast_to(ai, (SUBLANES, blk))

        def scan(t, prev):
            rows = pl.ds(pl.multiple_of(t * SUBLANES, SUBLANES), SUBLANES)
            pr, pi = _cmul(ar8, ai8, prev[0], prev[1])
            nr = pr + hr_ref[rows, :]
            ni = pi + hi_ref[rows, :]
            hr_ref[rows, :] = nr
            hi_ref[rows, :] = ni
            return nr, ni

        lax.fori_loop(1, steps, scan, (hr_ref[0:SUBLANES, :], hi_ref[0:SUBLANES, :]), unroll=4)

        top = slice(chunk - SUBLANES, chunk)
        in_re, in_im, out_re, out_im = _segment_carries(
            hr_ref[top, :], hi_ref[top, :], pr_ref[top, :][0:1], pi_ref[top, :][0:1],
            carry_ref[0:1, :], carry_ref[1:2, :], reverse=False)
        carry_ref[0:1, :] = out_re
        carry_ref[1:2, :] = out_im
        inr_ref[...] = in_re
        ini_ref[...] = in_im

        def fix(t, _):
            rows = pl.ds(pl.multiple_of(t * SUBLANES, SUBLANES), SUBLANES)
            fr, fi = _cmul(pr_ref[rows, :], pi_ref[rows, :], in_re, in_im)
            hr_ref[rows, :] += fr
            hi_ref[rows, :] += fi
            return 0

        lax.fori_loop(0, steps, fix, 0, unroll=4)

        yseg_ref[...] = _dot(hr_ref[...].astype(BF16), cr_ref[0]) - _dot(hi_ref[...].astype(BF16), ci_ref[0])
        _from_segments(yseg_ref, y_ref, steps)
        y_ref[...] += dk_ref[...] * u_ref[...]

    row = pl.BlockSpec((1, blk), lambda j, c: (0, j))
    b_mat = pl.BlockSpec((1, LANES, blk), lambda j, c: (j, 0, 0))
    c_mat = pl.BlockSpec((1, blk, LANES), lambda j, c: (j, 0, 0))
    tok = pl.BlockSpec((chunk, LANES), lambda j, c: (c, j))
    state = pl.BlockSpec((chunk, blk), lambda j, c: (c, j))
    enter = pl.BlockSpec((SUBLANES, blk), lambda j, c: (c, j))
    return _hosted_call(
        body, carry, _edge_2d(N_SSM_BLOCKS, nc), name="ssm_fwd", grid=(N_SSM_BLOCKS, nc),
        in_specs=[tok, row, row, b_mat, b_mat, c_mat, c_mat, pl.BlockSpec((1, LANES), lambda j, c: (0, j))],
        out_specs=[tok, state, state, enter, enter],
        out_shape=[jax.ShapeDtypeStruct((T, SSM_W), F32), jax.ShapeDtypeStruct((T, STATES), F32),
                   jax.ShapeDtypeStruct((T, STATES), F32), jax.ShapeDtypeStruct((nc * SUBLANES, STATES), F32),
                   jax.ShapeDtypeStruct((nc * SUBLANES, STATES), F32)],
        scratch_shapes=[pltpu.VMEM((chunk, LANES), F32), pltpu.VMEM((chunk, LANES), F32),
                        pltpu.VMEM((chunk, blk), F32), pltpu.VMEM((chunk, blk), F32), pltpu.VMEM((SUBLANES, blk), F32)],
        compiler_params=_params(("arbitrary", "arbitrary"), VMEM_MID),
        inputs=(u, a_re, a_im, b_re, b_im, c_re, c_im, d_skip))


def _ssm_bwd(dy, u, h_re, h_im, in_re, in_im, a_re, a_im, b_re, b_im, c_re, c_im, d_skip, chunk, carry=None):
    T = u.shape[0]
    nc = T // chunk
    steps = chunk // SUBLANES
    blk = SSM_LANE_BLOCK

    def body(dy_ref, u_ref, hr_ref, hi_ref, inr_ref, ini_ref, ar_ref, ai_ref, br_ref, bi_ref, cr_ref, ci_ref, dk_ref,
             du_ref, dbr_ref, dbi_ref, dcr_ref, dci_ref, dar_ref, dai_ref, ddk_ref,
             dyseg_ref, useg_ref, duseg_ref, gr_ref, gi_ref, pr_ref, pi_ref, carry_ref, accr_ref, acci_ref):
        c = pl.program_id(1)
        ar, ai = ar_ref[...], ai_ref[...]

        @pl.when(c == 0)
        def _():
            _power_table(ar, ai, pr_ref, pi_ref, steps)
            carry_ref[...] = jnp.zeros_like(carry_ref)
            accr_ref[...] = jnp.zeros_like(accr_ref)
            acci_ref[...] = jnp.zeros_like(acci_ref)

        _to_segments(dy_ref, dyseg_ref, steps)
        _to_segments(u_ref, useg_ref, steps)
        dyb = dyseg_ref[...].astype(BF16)
        ub = useg_ref[...].astype(BF16)
        gr_ref[...] = _dot_nt(dyb, cr_ref[0])
        gi_ref[...] = -_dot_nt(dyb, ci_ref[0])
        dcr = _dot_tn(hr_ref[...].astype(BF16), dyb)
        dci = -_dot_tn(hi_ref[...].astype(BF16), dyb)
        ddk = jnp.sum(dy_ref[...] * u_ref[...], axis=0, keepdims=True)

        ar8 = jnp.broadcast_to(ar, (SUBLANES, blk))
        ai8 = jnp.broadcast_to(-ai, (SUBLANES, blk))

        def scan(k, nxt):
            t = steps - 2 - k
            rows = pl.ds(pl.multiple_of(t * SUBLANES, SUBLANES), SUBLANES)
            pr, pi = _cmul(ar8, ai8, nxt[0], nxt[1])
            nr = pr + gr_ref[rows, :]
            ni = pi + gi_ref[rows, :]
            gr_ref[rows, :] = nr
            gi_ref[rows, :] = ni
            return nr, ni

        top = slice(chunk - SUBLANES, chunk)
        lax.fori_loop(0, steps - 1, scan, (gr_ref[top, :], gi_ref[top, :]), unroll=4)

        gin_re, gin_im, out_re, out_im = _segment_carries(
            gr_ref[0:SUBLANES, :], gi_ref[0:SUBLANES, :], pr_ref[top, :][0:1], -pi_ref[top, :][0:1],
            carry_ref[0:1, :], carry_ref[1:2, :], reverse=True)
        carry_ref[0:1, :] = out_re
        carry_ref[1:2, :] = out_im

        def fix(t, acc):
            rows = pl.ds(pl.multiple_of(t * SUBLANES, SUBLANES), SUBLANES)
            prow = pl.ds(pl.multiple_of((steps - 1 - t) * SUBLANES, SUBLANES), SUBLANES)
            fr, fi = _cmul(pr_ref[prow, :], -pi_ref[prow, :], gin_re, gin_im)
            g_re = gr_ref[rows, :] + fr
            g_im = gi_ref[rows, :] + fi
            gr_ref[rows, :] = g_re
            gi_ref[rows, :] = g_im
            before = pl.ds(pl.multiple_of(jnp.maximum(t - 1, 0) * SUBLANES, SUBLANES), SUBLANES)
            first = t == 0
            hp_re = jnp.where(first, inr_ref[...], hr_ref[before, :])
            hp_im = jnp.where(first, ini_ref[...], hi_ref[before, :])
            return acc[0] + g_re * hp_re + g_im * hp_im, acc[1] + g_im * hp_re - g_re * hp_im

        acc_re, acc_im = lax.fori_loop(0, steps, fix, (accr_ref[...], acci_ref[...]), unroll=2)
        accr_ref[...] = acc_re
        acci_ref[...] = acc_im

        gbr = gr_ref[...].astype(BF16)
        gbi = gi_ref[...].astype(BF16)
        duseg_ref[...] = _dot_nt(gbr, br_ref[0]) + _dot_nt(gbi, bi_ref[0])
        _from_segments(duseg_ref, dyseg_ref, steps)
        du_ref[...] = (dyseg_ref[...] + dk_ref[...] * dy_ref[...]).astype(BF16)
        dbr = _dot_tn(ub, gbr)
        dbi = _dot_tn(ub, gbi)

        @pl.when(c == 0)
        def _():
            dbr_ref[0] = dbr
            dbi_ref[0] = dbi
            dcr_ref[0] = dcr
            dci_ref[0] = dci
            ddk_ref[...] = ddk

        @pl.when(c > 0)
        def _():
            dbr_ref[0] += dbr
            dbi_ref[0] += dbi
            dcr_ref[0] += dcr
            dci_ref[0] += dci
            ddk_ref[...] += ddk

        @pl.when(c == nc - 1)
        def _():
            dar_ref[...] = jnp.sum(acc_re, axis=0, keepdims=True)
            dai_ref[...] = jnp.sum(acc_im, axis=0, keepdims=True)

    rev = lambda c: nc - 1 - c
    row = pl.BlockSpec((1, blk), lambda j, c: (0, j))
    b_mat = pl.BlockSpec((1, LANES, blk), lambda j, c: (j, 0, 0))
    c_mat = pl.BlockSpec((1, blk, LANES), lambda j, c: (j, 0, 0))
    tok = pl.BlockSpec((chunk, LANES), lambda j, c: (rev(c), j))
    state = pl.BlockSpec((chunk, blk), lambda j, c: (rev(c), j))
    enter = pl.BlockSpec((SUBLANES, blk), lambda j, c: (rev(c), j))
    chan = pl.BlockSpec((1, LANES), lambda j, c: (0, j))
    f32 = lambda *s: jax.ShapeDtypeStruct(s, F32)
    return _hosted_call(
        body, carry, _edge_2d(N_SSM_BLOCKS, nc), name="ssm_bwd", grid=(N_SSM_BLOCKS, nc),
        in_specs=[tok, tok, state, state, enter, enter, row, row, b_mat, b_mat, c_mat, c_mat, chan],
        out_specs=[tok, b_mat, b_mat, c_mat, c_mat, row, row, chan],
        out_shape=[jax.ShapeDtypeStruct((T, SSM_W), BF16), f32(N_SSM_BLOCKS, LANES, blk), f32(N_SSM_BLOCKS, LANES, blk),
                   f32(N_SSM_BLOCKS, blk, LANES), f32(N_SSM_BLOCKS, blk, LANES), f32(1, STATES), f32(1, STATES), f32(1, SSM_W)],
        scratch_shapes=[pltpu.VMEM((chunk, LANES), F32), pltpu.VMEM((chunk, LANES), F32), pltpu.VMEM((chunk, LANES), F32),
                        pltpu.VMEM((chunk, blk), F32), pltpu.VMEM((chunk, blk), F32),
                        pltpu.VMEM((chunk, blk), F32), pltpu.VMEM((chunk, blk), F32),
                        pltpu.VMEM((SUBLANES, blk), F32), pltpu.VMEM((SUBLANES, blk), F32), pltpu.VMEM((SUBLANES, blk), F32)],
        compiler_params=_params(("arbitrary", "arbitrary"), VMEM_BIG),
        inputs=(dy, u, h_re, h_im, in_re, in_im, a_re, a_im, b_re, b_im, c_re, c_im, d_skip))


def _merge_forward(y, att, ga, gs, w_glu, w_ssm, w_attn):
    z = jax.nn.gelu(y)
    zb = z.astype(BF16)
    gl = jax.nn.sigmoid(_dot(zb, w_glu))
    z2b = (z * gl).astype(BF16)
    y_ssm = _dot(z2b, w_ssm)
    y_attn = _dot(att, w_attn)
    sa = jax.nn.sigmoid(ga)
    ss = jax.nn.sigmoid(gs)
    merged = (sa * y_attn + ss * y_ssm).astype(BF16)
    return z, zb, gl, z2b, y_ssm, y_attn, sa, ss, merged


def _merge_fwd(x, y, att, ga, gs, g2, g3, w_glu, w_ssm, w_attn, w_out, tile):
    T = x.shape[0]

    def body(x_ref, y_ref, att_ref, ga_ref, gs_ref, g2_ref, g3_ref, wg_ref, ws_ref, wa_ref, wo_ref, x1_ref, o_ref, h2_ref):
        merged = _merge_forward(y_ref[...], att_ref[...], ga_ref[...], gs_ref[...], wg_ref[...], ws_ref[...], wa_ref[...])[-1]
        o = _dot(merged, wo_ref[...])
        x1 = x_ref[...] + o * _rms_scale(o) * g2_ref[...]
        o_ref[...] = o
        x1_ref[...] = x1
        h2_ref[...] = (x1 * _rms_scale(x1) * g3_ref[...]).astype(BF16)

    tok = lambda w: pl.BlockSpec((tile, w), lambda i: (i, 0))
    vec = _const_spec((1, D_MODEL))
    return pl.pallas_call(
        body, name="merge_fwd", grid=(T // tile,),
        in_specs=[tok(D_MODEL), tok(SSM_W), tok(ATTN_W), tok(D_MODEL), tok(D_MODEL), vec, vec,
                  _const_spec((SSM_W, SSM_W)), _const_spec((SSM_W, D_MODEL)), _const_spec((ATTN_W, D_MODEL)),
                  _const_spec((D_MODEL, D_MODEL))],
        out_specs=[tok(D_MODEL), tok(D_MODEL), tok(D_MODEL)],
        out_shape=[jax.ShapeDtypeStruct((T, D_MODEL), F32), jax.ShapeDtypeStruct((T, D_MODEL), F32),
                   jax.ShapeDtypeStruct((T, D_MODEL), BF16)],
        compiler_params=_params(("arbitrary",), VMEM_MID),
    )(x, y, att, ga, gs, g2, g3, w_glu, w_ssm, w_attn, w_out)


def _merge_bwd(dh2, dx2, x1, o, y, att, ga, gs, g2, g3, w_glu, w_ssm, w_attn, w_out, tile, carry=None):
    T = x1.shape[0]
    n_steps = T // tile

    def body(dh2_ref, dx2_ref, x1_ref, o_ref, y_ref, att_ref, ga_ref, gs_ref, g2_ref, g3_ref, wg_ref, ws_ref, wa_ref, wo_ref,
             dx1_ref, dga_ref, dgs_ref, datt_ref, dy_ref, dwg_hbm, dws_hbm, dwa_hbm, dwo_hbm, dg2_ref, dg3_ref,
             awg_ref, aws_ref, awa_ref, awo_ref):
        i = pl.program_id(0)
        x1v, ov = x1_ref[...], o_ref[...]
        dxn, dg3 = _rms_bwd(dh2_ref[...], x1v, _rms_scale(x1v), g3_ref[...])
        dx1 = dx2_ref[...] + dxn
        dx1_ref[...] = dx1
        do, dg2 = _rms_bwd(dx1, ov, _rms_scale(ov), g2_ref[...])
        dob = do.astype(BF16)

        yv = y_ref[...]
        att = att_ref[...]
        z, zb, gl, z2b, y_ssm, y_attn, sa, ss, merged = _merge_forward(
            yv, att, ga_ref[...], gs_ref[...], wg_ref[...], ws_ref[...], wa_ref[...])
        dmerged = _dot_nt(dob, wo_ref[...])
        dya = (dmerged * sa).astype(BF16)
        dys = (dmerged * ss).astype(BF16)
        dga_ref[...] = (dmerged * y_attn * sa * (1.0 - sa)).astype(BF16)
        dgs_ref[...] = (dmerged * y_ssm * ss * (1.0 - ss)).astype(BF16)
        datt_ref[...] = _dot_nt(dya, wa_ref[...]).astype(BF16)
        dz2 = _dot_nt(dys, ws_ref[...])
        dpre = (dz2 * z * gl * (1.0 - gl)).astype(BF16)
        dz = dz2 * gl + _dot_nt(dpre, wg_ref[...])
        _, gelu_vjp = jax.vjp(jax.nn.gelu, yv)
        dy_ref[...] = gelu_vjp(dz)[0]

        grads = ((awo_ref, _dot_tn(merged, dob)), (awa_ref, _dot_tn(att, dya)),
                 (aws_ref, _dot_tn(z2b, dys)), (awg_ref, _dot_tn(zb, dpre)), (dg2_ref, dg2), (dg3_ref, dg3))

        @pl.when(i == 0)
        def _():
            for ref, val in grads:
                ref[...] = val

        @pl.when(i > 0)
        def _():
            for ref, val in grads:
                ref[...] += val

        @pl.when(i == n_steps - 1)
        def _():
            pltpu.sync_copy(awg_ref, dwg_hbm)
            pltpu.sync_copy(aws_ref, dws_hbm)
            pltpu.sync_copy(awa_ref, dwa_hbm)
            pltpu.sync_copy(awo_ref, dwo_hbm)

    tok = lambda w: pl.BlockSpec((tile, w), lambda i: (i, 0))
    vec = _const_spec((1, D_MODEL))
    any_ = pl.BlockSpec(memory_space=pl.ANY)
    vec_out = pl.BlockSpec((1, D_MODEL), lambda i: (0, 0))
    f32 = lambda *s: jax.ShapeDtypeStruct(s, F32)
    bf = lambda *s: jax.ShapeDtypeStruct(s, BF16)
    return _hosted_call(
        body, carry, _edge_1d(n_steps), name="merge_bwd", grid=(n_steps,),
        in_specs=[tok(D_MODEL), tok(D_MODEL), tok(D_MODEL), tok(D_MODEL), tok(SSM_W), tok(ATTN_W), tok(D_MODEL), tok(D_MODEL),
                  vec, vec, _const_spec((SSM_W, SSM_W)), _const_spec((SSM_W, D_MODEL)), _const_spec((ATTN_W, D_MODEL)),
                  _const_spec((D_MODEL, D_MODEL))],
        out_specs=[tok(D_MODEL), tok(D_MODEL), tok(D_MODEL), tok(ATTN_W), tok(SSM_W), any_, any_, any_, any_, vec_out, vec_out],
        out_shape=[f32(T, D_MODEL), bf(T, D_MODEL), bf(T, D_MODEL), bf(T, ATTN_W), f32(T, SSM_W),
                   f32(SSM_W, SSM_W), f32(SSM_W, D_MODEL), f32(ATTN_W, D_MODEL), f32(D_MODEL, D_MODEL),
                   f32(1, D_MODEL), f32(1, D_MODEL)],
        scratch_shapes=[pltpu.VMEM((SSM_W, SSM_W), F32), pltpu.VMEM((SSM_W, D_MODEL), F32),
                        pltpu.VMEM((ATTN_W, D_MODEL), F32), pltpu.VMEM((D_MODEL, D_MODEL), F32)],
        compiler_params=_params(("arbitrary",), VMEM_BIG),
        inputs=(dh2, dx2, x1, o, y, att, ga, gs, g2, g3, w_glu, w_ssm, w_attn, w_out))


FF_SHARD = D_FF // N_DEV


def _mlp_fwd(h2, x1, target, g4, w_ff_in, w_ff_out, tile):
    T = h2.shape[0]
    per_step = 2
    ff_chunk = per_step * FF_SHARD
    n_i, n_k = T // tile, N_DEV // per_step

    def body(h2_ref, x1_ref, tg_ref, g4_ref, wi_ref, wo_ref, a_ref, dfo_ref, dx2_ref, loss_ref, dg4_ref, acc_ref):
        i, k = pl.program_id(0), pl.program_id(1)
        h2v = h2_ref[...]
        part = jnp.zeros((tile, D_MODEL), F32)
        for s in range(per_step):
            a = _dot(h2v, wi_ref[s])
            a_ref[:, s * FF_SHARD:(s + 1) * FF_SHARD] = a.astype(BF16)
            ra = jnp.maximum(a, 0.0)
            part = part + _dot((ra * ra).astype(BF16), wo_ref[s])

        @pl.when(k == 0)
        def _():
            acc_ref[...] = part

        @pl.when(k > 0)
        def _():
            acc_ref[...] += part

        @pl.when(k == n_k - 1)
        def _():
            f = acc_ref[...]
            r = _rms_scale(f)
            g = g4_ref[...]
            err = x1_ref[...] + f * r * g - tg_ref[...]
            dx2 = err * (1.0 / D_MODEL)
            dx2_ref[...] = dx2
            dfo, dg = _rms_bwd(dx2, f, r, g)
            dfo_ref[...] = dfo.astype(BF16)
            row = lax.broadcasted_iota(jnp.int32, (8, LANES), 0)
            col = lax.broadcasted_iota(jnp.int32, (8, LANES), 1)
            loss = jnp.where((row == 0) & (col == 0), (0.5 / D_MODEL) * jnp.sum(err * err), 0.0)

            @pl.when(i == 0)
            def _():
                loss_ref[...] = loss
                dg4_ref[...] = dg

            @pl.when(i > 0)
            def _():
                loss_ref[...] += loss
                dg4_ref[...] += dg

    tok = pl.BlockSpec((tile, D_MODEL), lambda i, k: (i, 0))
    return pl.pallas_call(
        body, name="mlp_fwd", grid=(n_i, n_k),
        in_specs=[tok, tok, tok, pl.BlockSpec((1, D_MODEL), lambda i, k: (0, 0)),
                  pl.BlockSpec((per_step, D_MODEL, FF_SHARD), lambda i, k: (k, 0, 0)),
                  pl.BlockSpec((per_step, FF_SHARD, D_MODEL), lambda i, k: (k, 0, 0))],
        out_specs=[pl.BlockSpec((tile, ff_chunk), lambda i, k: (i, k)), tok, tok,
                   pl.BlockSpec((8, LANES), lambda i, k: (0, 0)), pl.BlockSpec((1, D_MODEL), lambda i, k: (0, 0))],
        out_shape=[jax.ShapeDtypeStruct((T, D_FF), BF16), jax.ShapeDtypeStruct((T, D_MODEL), BF16),
                   jax.ShapeDtypeStruct((T, D_MODEL), F32), jax.ShapeDtypeStruct((8, LANES), F32),
                   jax.ShapeDtypeStruct((1, D_MODEL), F32)],
        scratch_shapes=[pltpu.VMEM((tile, D_MODEL), F32)],
        compiler_params=_params(("arbitrary", "arbitrary"), VMEM_BIG),
    )(h2, x1, target, g4, w_ff_in, w_ff_out)


def _mlp_bwd(dfo, a, h2, w_ff_in, w_ff_out, tile):
    T = h2.shape[0]
    ff_chunk = FF_SHARD
    n_i, n_k = T // tile, N_DEV

    def body(dfo_ref, a_ref, h2_ref, wi_ref, wo_ref, dwi_ref, dwo_ref, dh2_ref, acc_ref):
        k, i = pl.program_id(0), pl.program_id(1)
        dfo = dfo_ref[...]
        ra = jnp.maximum(a_ref[...].astype(F32), 0.0)
        drr = _dot_nt(dfo, wo_ref[0])
        da = (drr * (2.0 * ra)).astype(BF16)
        dwo = _dot_tn((ra * ra).astype(BF16), dfo)
        dwi = _dot_tn(h2_ref[...], da)
        part = _dot_nt(da, wi_ref[0])
        rows = pl.ds(pl.multiple_of(i * tile, tile), tile)

        @pl.when(i == 0)
        def _():
            dwi_ref[0] = dwi
            dwo_ref[0] = dwo

        @pl.when(i > 0)
        def _():
            dwi_ref[0] += dwi
            dwo_ref[0] += dwo

        @pl.when(k == 0)
        def _():
            acc_ref[rows, :] = part

        @pl.when((k > 0) & (k < n_k - 1))
        def _():
            acc_ref[rows, :] += part

        @pl.when(k == n_k - 1)
        def _():
            dh2_ref[...] = acc_ref[rows, :] + part

    return pl.pallas_call(
        body, name="mlp_bwd", grid=(n_k, n_i),
        in_specs=[pl.BlockSpec((tile, D_MODEL), lambda k, i: (i, 0)), pl.BlockSpec((tile, ff_chunk), lambda k, i: (i, k)),
                  pl.BlockSpec((tile, D_MODEL), lambda k, i: (i, 0)),
                  pl.BlockSpec((1, D_MODEL, ff_chunk), lambda k, i: (k, 0, 0)),
                  pl.BlockSpec((1, ff_chunk, D_MODEL), lambda k, i: (k, 0, 0))],
        out_specs=[pl.BlockSpec((1, D_MODEL, ff_chunk), lambda k, i: (k, 0, 0)),
                   pl.BlockSpec((1, ff_chunk, D_MODEL), lambda k, i: (k, 0, 0)),
                   pl.BlockSpec((tile, D_MODEL), lambda k, i: (jnp.where(k == n_k - 1, i, 0), 0))],
        out_shape=[jax.ShapeDtypeStruct((N_DEV, D_MODEL, ff_chunk), F32), jax.ShapeDtypeStruct((N_DEV, ff_chunk, D_MODEL), F32),
                   jax.ShapeDtypeStruct((T, D_MODEL), F32)],
        scratch_shapes=[pltpu.VMEM((T, D_MODEL), F32)],
        compiler_params=_params(("arbitrary", "arbitrary"), VMEM_BIG),
    )(dfo, a, h2, w_ff_in, w_ff_out)


def _block_diag_in(b):
    bt = jnp.transpose(b, (0, 2, 1)).reshape(N_SSM_BLOCKS, 8, GROUP_CH, N_STATE)
    eye = jnp.eye(8, dtype=b.dtype)
    return jnp.einsum("jacp,ab->jacbp", bt, eye).reshape(N_SSM_BLOCKS, LANES, SSM_LANE_BLOCK)


def _block_diag_in_grad(g):
    g = g.reshape(N_SSM_BLOCKS, 8, GROUP_CH, 8, N_STATE)
    d = jnp.diagonal(g, axis1=1, axis2=3)
    return jnp.transpose(d, (0, 3, 2, 1)).reshape(N_GROUPS, N_STATE, GROUP_CH)


def _block_diag_out(c):
    ct = c.reshape(N_SSM_BLOCKS, 8, GROUP_CH, N_STATE)
    eye = jnp.eye(8, dtype=c.dtype)
    return jnp.einsum("jacp,ab->japbc", ct, eye).reshape(N_SSM_BLOCKS, SSM_LANE_BLOCK, LANES)


def _block_diag_out_grad(g):
    g = g.reshape(N_SSM_BLOCKS, 8, N_STATE, 8, GROUP_CH)
    d = jnp.diagonal(g, axis1=1, axis2=3)
    return jnp.transpose(d, (0, 3, 2, 1)).reshape(N_GROUPS, GROUP_CH, N_STATE)


def _tiles(T):
    return dict(proj=min(512, T), proj_bwd=min(256, T), merge=min(512, T), merge_bwd=min(256, T),
                mlp_fwd=min(512, T), mlp_bwd=min(512, T), ssm_chunk=min(1024, T))


def _mesh_position():
    x, y, c = lax.axis_index("x"), lax.axis_index("y"), lax.axis_index("c")
    other_chips = [(1 - x, y), (x, 1 - y), (1 - x, 1 - y)]
    return x, y, c, other_chips


def _gather_carry(arrays):
    n = len(arrays)

    def copies(ins, outs, sems):
        send_sems, recv_sems, local_sems = sems
        x, y, c, chips = _mesh_position()
        me, sibling = (x, y, c), (x, y, 1 - c)

        def copy(a, k, block, to, src=None):
            px, py, pc = block
            dst = outs[a].at[4 * px + 2 * py + pc]
            return pltpu.make_async_remote_copy(
                src_ref=dst if src is None else src, dst_ref=dst, send_sem=send_sems.at[7 * a + k],
                recv_sem=recv_sems.at[7 * a + k], device_id=to, device_id_type=MESH_IDS)

        mine = [pltpu.make_async_copy(ins[a], outs[a].at[4 * x + 2 * y + c], local_sems.at[a]) for a in range(n)]
        first = []
        for a in range(n):
            first.append(copy(a, 0, me, sibling, src=ins[a]))
            first += [copy(a, 1 + j, me, (*chip, c), src=ins[a]) for j, chip in enumerate(chips)]
        return copy, mine, first, me, sibling, chips, c

    def start(ins, outs, sems):
        _, mine, first, *_ = copies(ins, outs, sems)
        for cp in mine + first:
            cp.start()

    def finish(ins, outs, sems):
        copy, mine, first, me, sibling, chips, c = copies(ins, outs, sems)
        passed = []
        for a in range(n):
            for j, chip in enumerate(chips):
                copy(a, 1 + j, (*chip, c), me).wait_recv()
                passed.append(copy(a, 4 + j, (*chip, c), sibling))
                passed[-1].start()
        for a in range(n):
            copy(a, 0, sibling, me).wait_recv()
            for j, chip in enumerate(chips):
                copy(a, 4 + j, (*chip, 1 - c), me).wait_recv()
        for cp in first + passed:
            cp.wait_send()
        for cp in mine:
            cp.wait()

    return _Carry(arrays, [jax.ShapeDtypeStruct((N_DEV,) + a.shape, a.dtype) for a in arrays],
                  [pltpu.SemaphoreType.DMA((7 * n,)), pltpu.SemaphoreType.DMA((7 * n,)), pltpu.SemaphoreType.DMA((n,))],
                  start, finish)


def _pairwise_carry(arrays, n_slots, make_copies):
    n = len(arrays)

    def start(ins, outs, sems):
        for cp in make_copies(ins, outs, sems):
            cp.start()

    def finish(ins, outs, sems):
        for cp in make_copies(ins, outs, sems):
            cp.wait()

    return _Carry(arrays, [jax.ShapeDtypeStruct((n_slots,) + a.shape[1:], a.dtype) for a in arrays],
                  [pltpu.SemaphoreType.DMA((n_slots * n,)), pltpu.SemaphoreType.DMA((n_slots * n,))], start, finish)


def _sibling_carry(grads):
    def make_copies(ins, outs, sems):
        x, y, c, _ = _mesh_position()
        return [pltpu.make_async_remote_copy(
            src_ref=ins[a].at[2 * ch + (1 - c)], dst_ref=outs[a].at[ch], send_sem=sems[0].at[4 * a + ch],
            recv_sem=sems[1].at[4 * a + ch], device_id=(x, y, 1 - c), device_id_type=MESH_IDS)
            for a in range(len(grads)) for ch in range(4)]

    return _pairwise_carry(grads, 4, make_copies)


def _chips_carry(sums):
    def make_copies(ins, outs, sems):
        x, y, c, chips = _mesh_position()
        return [pltpu.make_async_remote_copy(
            src_ref=ins[a].at[2 * px + py], dst_ref=outs[a].at[j], send_sem=sems[0].at[3 * a + j],
            recv_sem=sems[1].at[3 * a + j], device_id=(px, py, c), device_id_type=MESH_IDS)
            for a in range(len(sums)) for j, (px, py) in enumerate(chips)]

    return _pairwise_carry(sums, 3, make_copies)


def _row_tile(rows, cols):
    t = max(8, min(rows, (1 << 18) // cols // 8 * 8))
    while rows % t:
        t -= 8
    return t


def _add_sibling(grads8, recv, core, name):
    _, R, C = grads8.shape
    tr = _row_tile(R, C)
    g4 = grads8.reshape(4, 2, R, C)

    def body(core_ref, g_ref, r_ref, o_ref, ob_ref):
        s = g_ref[0] + r_ref[...]
        o_ref[...] = s
        ob_ref[...] = s.astype(BF16)

    out = pl.BlockSpec((1, tr, C), lambda ch, r, core_ref: (ch, r, 0))
    return pl.pallas_call(
        body, name=name,
        grid_spec=pltpu.PrefetchScalarGridSpec(
            num_scalar_prefetch=1, grid=(4, R // tr),
            in_specs=[pl.BlockSpec((1, 1, tr, C), lambda ch, r, core_ref: (ch, core_ref[0], r, 0)),
                      pl.BlockSpec((1, tr, C), lambda ch, r, core_ref: (ch, r, 0))],
            out_specs=[out, out]),
        out_shape=[jax.ShapeDtypeStruct((4, R, C), F32), jax.ShapeDtypeStruct((4, R, C), BF16)],
        compiler_params=_params(("arbitrary", "arbitrary")),
    )(core, g4, recv)


def _adam_math(w, g, m, v):
    m = ADAM_B1 * m + (1.0 - ADAM_B1) * g
    v = ADAM_B2 * v + (1.0 - ADAM_B2) * jnp.square(g)
    m_hat = m / (1.0 - ADAM_B1 ** ADAM_STEP)
    v_hat = v / (1.0 - ADAM_B2 ** ADAM_STEP)
    delta = -ADAM_LR * (m_hat / (jnp.sqrt(v_hat) + ADAM_EPS) + ADAM_WD * w)
    return delta, m, v


def _adam_big(w, m, v, chip_sums, recv, chip, name):
    R, C = w.shape
    tr = _row_tile(R, C)

    def body(chip_ref, w_ref, m_ref, v_ref, s_ref, r_ref, g_ref, d_ref, nm_ref, nv_ref):
        g = s_ref[0] + r_ref[0].astype(F32) + r_ref[1].astype(F32) + r_ref[2].astype(F32)
        g_ref[...] = g
        d_ref[...], nm_ref[...], nv_ref[...] = _adam_math(w_ref[...], g, m_ref[...], v_ref[...])

    blk = pl.BlockSpec((tr, C), lambda r, chip_ref: (r, 0))
    return pl.pallas_call(
        body, name=name,
        grid_spec=pltpu.PrefetchScalarGridSpec(
            num_scalar_prefetch=1, grid=(R // tr,),
            in_specs=[blk, blk, blk, pl.BlockSpec((1, tr, C), lambda r, chip_ref: (chip_ref[0], r, 0)),
                      pl.BlockSpec((3, tr, C), lambda r, chip_ref: (0, r, 0))],
            out_specs=[blk] * 4),
        out_shape=[jax.ShapeDtypeStruct((R, C), F32)] * 4,
        compiler_params=_params(("arbitrary",)),
    )(chip, w, m, v, chip_sums, recv)


def _sum_partials(partials, name):
    def body(p_ref, g_ref):
        g = p_ref[0]
        for d in range(1, N_DEV):
            g = g + p_ref[d]
        g_ref[...] = g

    return pl.pallas_call(body, name=name, grid=(1,), in_specs=[_whole(partials.shape)], out_specs=_whole(partials.shape[1:]),
                          out_shape=jax.ShapeDtypeStruct(partials.shape[1:], F32))(partials)


def _adam_small(ws, ms, vs, gs):
    n = len(ws)

    def body(*refs):
        w_refs, m_refs, v_refs, g_refs = (refs[i * n:(i + 1) * n] for i in range(4))
        d_refs, nm_refs, nv_refs = (refs[(4 + i) * n:(5 + i) * n] for i in range(3))
        for j in range(n):
            d_refs[j][...], nm_refs[j][...], nv_refs[j][...] = _adam_math(
                w_refs[j][...], g_refs[j][...], m_refs[j][...], v_refs[j][...])

    specs = [_whole(w.shape) for w in ws]
    outs = pl.pallas_call(body, name="adam_small", grid=(1,), in_specs=specs * 4, out_specs=specs * 3,
                          out_shape=[jax.ShapeDtypeStruct(w.shape, F32) for w in ws] * 3,
                          compiler_params=_params(("arbitrary",), VMEM_MID))(*ws, *ms, *vs, *gs)
    return outs[:n], outs[n:2 * n], outs[2 * n:]


PACK_QUANTUM = SUBLANES * LANES


def _pack(named, names):
    parts = []
    for nme in names:
        flat = named[nme].reshape(-1)
        parts.append(jnp.pad(flat, (0, -flat.size % PACK_QUANTUM)))
    return jnp.concatenate(parts).reshape(-1, LANES)


def _unpack(packed, shapes, names):
    flat = packed.reshape(-1)
    out, pos = {}, 0
    for nme in names:
        size = math.prod(shapes[nme])
        out[nme] = flat[pos:pos + size].reshape(shapes[nme])
        pos += size + (-size % PACK_QUANTUM)
    return out


BIG = ("w_in", "w_glu", "w_attn_branch", "w_ssm_branch", "w_out", "w_ff_in", "w_ff_out")
COLUMN_SHARDED = ("w_in", "w_attn_branch", "w_ssm_branch", "w_ff_in")
SMALL = ("norm_mix_pre", "norm_mix_post", "norm_mlp_pre", "norm_mlp_post", "rel_bias", "sinks", "lam_re", "lam_im",
         "log_dt", "b_re", "b_im", "c_re", "c_im", "d_skip")
SMALL_LATE = ("norm_mix_pre", "rel_bias", "sinks", "loss")
SMALL_BEFORE_ATTN_BWD = tuple(n for n in SMALL if n not in SMALL_LATE)
ALL_WEIGHTS = ("norm_mix_pre", "norm_mix_post", "norm_mlp_pre", "norm_mlp_post", "w_in", "rel_bias", "sinks", "lam_re",
               "lam_im", "log_dt", "b_re", "b_im", "c_re", "c_im", "d_skip", "w_glu", "w_attn_branch", "w_ssm_branch",
               "w_out", "w_ff_in", "w_ff_out")


def _full_from_gathered(name, gathered):
    _, r, c = gathered.shape
    if name in COLUMN_SHARDED:
        return jnp.transpose(gathered, (1, 0, 2)).reshape(r, N_DEV * c)
    return gathered.reshape(N_DEV * r, c)


def _blocks_from_full(name, full):
    r, c = full.shape
    if name in COLUMN_SHARDED:
        return jnp.transpose(full.reshape(r, N_DEV, c // N_DEV), (1, 0, 2))
    return full.reshape(N_DEV, r // N_DEV, c)


def kernel(x, norm_mix_pre, norm_mix_post, norm_mlp_pre, norm_mlp_post, w_in, rel_bias, sinks, lam_re, lam_im, log_dt, b_re, b_im, c_re, c_im, d_skip, w_glu, w_attn_branch, w_ssm_branch, w_out, w_ff_in, w_ff_out, loss_target, m_norm_mix_pre, m_norm_mix_post, m_norm_mlp_pre, m_norm_mlp_post, m_w_in, m_rel_bias, m_sinks, m_lam_re, m_lam_im, m_log_dt, m_b_re, m_b_im, m_c_re, m_c_im, m_d_skip, m_w_glu, m_w_attn_branch, m_w_ssm_branch, m_w_out, m_w_ff_in, m_w_ff_out, v_norm_mix_pre, v_norm_mix_post, v_norm_mlp_pre, v_norm_mlp_post, v_w_in, v_rel_bias, v_sinks, v_lam_re, v_lam_im, v_log_dt, v_b_re, v_b_im, v_c_re, v_c_im, v_d_skip, v_w_glu, v_w_attn_branch, v_w_ssm_branch, v_w_out, v_w_ff_in, v_w_ff_out):
    args = dict(locals())
    w = {n: args[n] for n in ALL_WEIGHTS}
    m = {n: args["m_" + n] for n in ALL_WEIGHTS}
    v = {n: args["v_" + n] for n in ALL_WEIGHTS}
    core = lax.axis_index("c").astype(jnp.int32).reshape(1)
    chip = (2 * lax.axis_index("x") + lax.axis_index("y")).astype(jnp.int32).reshape(1)
    xs, target = x[0], loss_target[0]
    t = _tiles(xs.shape[0])
    local = lambda d, n: d[n][0].T if n == "w_in" else d[n][0]
    shard = {n: local(w, n).astype(BF16) for n in BIG}
    small = {n: (w[n] if n == "rel_bias" else w[n][0]) for n in SMALL}
    g1, g2, g3, g4 = (small[n].reshape(1, D_MODEL) for n in ("norm_mix_pre", "norm_mix_post", "norm_mlp_pre", "norm_mlp_post"))
    bucket = jnp.asarray(_bucket_table())
    rel_b, sink = small["rel_bias"], small["sinks"].reshape(1, N_HEADS)
    lam_r, lam_i = small["lam_re"].reshape(1, STATES), small["lam_im"].reshape(1, STATES)
    ldt_rep = jnp.repeat(small["log_dt"].reshape(N_GROUPS), N_STATE).reshape(1, STATES)
    bd_re, bd_im = _block_diag_in(small["b_re"]), _block_diag_in(small["b_im"])
    cm_re, cm_im = _block_diag_out(small["c_re"]).astype(BF16), _block_diag_out(small["c_im"]).astype(BF16)
    dsk = small["d_skip"].reshape(1, SSM_W)

    (g_in,) = _run_carry(_gather_carry([shard["w_in"]]), "gather_w_in")
    wf_in = g_in.reshape(IN_W, D_MODEL)
    merge_names = ("w_glu", "w_attn_branch", "w_ssm_branch", "w_out")
    (q, k, vv, u, ga, gs), gathered = _in_proj_fwd(xs, g1, wf_in, t["proj"], _gather_carry([shard[n] for n in merge_names]))
    wf = {n: _full_from_gathered(n, g) for n, g in zip(merge_names, gathered)}
    (att,), (wf_ff_in,) = _attn_fwd(q, k, vv, bucket, rel_b, sink, _gather_carry([shard["w_ff_in"]]))
    a_re, a_im, bm_re, bm_im = _ssm_prep(lam_r, lam_i, ldt_rep, bd_re, bd_im)
    (y, h_re, h_im, in_re, in_im), (wf_ff_out,) = _ssm_fwd(
        u, a_re, a_im, bm_re, bm_im, cm_re, cm_im, dsk, t["ssm_chunk"], _gather_carry([shard["w_ff_out"]]))
    x1, o, h2 = _merge_fwd(xs, y, att, ga, gs, g2, g3, wf["w_glu"], wf["w_ssm_branch"], wf["w_attn_branch"], wf["w_out"],
                           t["merge"])
    a, dfo, dx2, loss_blk, dg4 = _mlp_fwd(h2, x1, target, g4, wf_ff_in, wf_ff_out, t["mlp_fwd"])

    def add_sibling(names, blocks, received):
        pairs = [_add_sibling(b, r, core, "add_sibling_" + n) for n, b, r in zip(names, blocks, received)]
        return [p[0] for p in pairs], [p[1] for p in pairs]

    ff_names = ("w_ff_in", "w_ff_out")
    dw_ff_in, dw_ff_out, dh2 = _mlp_bwd(dfo, a, h2, wf_ff_in, wf_ff_out, t["mlp_bwd"])
    ff_blocks = [dw_ff_in, dw_ff_out]
    (dx1, dga, dgs, datt, dy, dw_glu, dw_ssm, dw_attn, dw_out, dg2, dg3), ff_recv = _merge_bwd(
        dh2, dx2, x1, o, y, att, ga, gs, g2, g3, wf["w_glu"], wf["w_ssm_branch"], wf["w_attn_branch"], wf["w_out"],
        t["merge_bwd"], _sibling_carry(ff_blocks))
    ff_sums, ff_sums_bf = add_sibling(ff_names, ff_blocks, ff_recv)
    merge_blocks = [_blocks_from_full(n, g) for n, g in zip(merge_names, (dw_glu, dw_attn, dw_ssm, dw_out))]
    (du, dbm_re, dbm_im, dcm_re, dcm_im, da_re, da_im, dd_skip), carried = _ssm_bwd(
        dy, u, h_re, h_im, in_re, in_im, a_re, a_im, bm_re, bm_im, cm_re, cm_im, dsk, t["ssm_chunk"],
        _join(_chips_carry(ff_sums_bf), _sibling_carry(merge_blocks)))
    ff_from_chips, merge_recv = carried[:2], carried[2:]
    merge_sums, merge_sums_bf = add_sibling(merge_names, merge_blocks, merge_recv)
    dbd_re, dbd_im, dlam_re, dlam_im, dldt_rep = _ssm_prep_bwd(lam_r, lam_i, ldt_rep, bd_re, bd_im, dbm_re, dbm_im, da_re, da_im)
    dlog_dt = _group_sum(dldt_rep.reshape(N_GROUPS, N_STATE))
    shapes = {n: w[n].shape for n in SMALL}
    shapes["loss"] = (1,)
    small_grads = dict(
        norm_mix_post=dg2, norm_mlp_pre=dg3, norm_mlp_post=dg4, lam_re=dlam_re, lam_im=dlam_im, log_dt=dlog_dt,
        b_re=_block_diag_in_grad(dbd_re), b_im=_block_diag_in_grad(dbd_im),
        c_re=_block_diag_out_grad(dcm_re), c_im=_block_diag_out_grad(dcm_im), d_skip=dd_skip)
    packed_early = _pack({n: small_grads[n].reshape(shapes[n]) for n in SMALL_BEFORE_ATTN_BWD}, SMALL_BEFORE_ATTN_BWD)
    (dq, dk, dv, attn_small), carried = _attn_bwd(
        q, k, vv, datt, bucket, rel_b, sink, _join(_chips_carry(merge_sums_bf), _gather_carry([packed_early])))
    merge_from_chips, partials_early = carried[:-1], carried[-1]
    grad_x, dw_in_t, dg1 = _in_proj_bwd(xs, g1, wf_in, dx1, (dq, dk, dv, du, dga, dgs), t["proj_bwd"])[0]

    late = dict(norm_mix_pre=dg1, rel_bias=attn_small[:, :N_BUCKETS, 0].T, sinks=attn_small[:, N_BUCKETS, 0],
                loss=loss_blk[0:1, 0])
    packed_late = _pack({n: late[n].reshape(shapes[n]) for n in SMALL_LATE}, SMALL_LATE)
    in_blocks = [dw_in_t.reshape(N_DEV, IN_W // N_DEV, D_MODEL)]
    in_recv = _run_carry(_sibling_carry(in_blocks), "reduce_sibling_w_in")
    in_sums, in_sums_bf = add_sibling(("w_in",), in_blocks, in_recv)
    in_from_chips, partials_late = _run_carry(_join(_chips_carry(in_sums_bf), _gather_carry([packed_late])), "reduce_chips_w_in")

    grads, deltas, new_m, new_v = {}, {}, {}, {}
    sums = dict(zip(ff_names + merge_names + ("w_in",), ff_sums + merge_sums + in_sums))
    received = dict(zip(ff_names + merge_names + ("w_in",), ff_from_chips + merge_from_chips + [in_from_chips]))
    for n in BIG:
        outs = _adam_big(local(w, n), local(m, n), local(v, n), sums[n], received[n], chip, "adam_" + n)
        grads[n], deltas[n], new_m[n], new_v[n] = ((o.T if n == "w_in" else o)[None] for o in outs)

    grads.update(_unpack(_sum_partials(partials_early, "sum_small_grads"), shapes, SMALL_BEFORE_ATTN_BWD))
    grads.update(_unpack(_sum_partials(partials_late, "sum_late_grads"), shapes, SMALL_LATE))
    loss = grads.pop("loss").reshape(())
    small_out = _adam_small(*[[d[n] for n in SMALL] for d in (w, m, v, grads)])
    for store, vals in zip((deltas, new_m, new_v), small_out):
        store.update(zip(SMALL, vals))

    return (loss, grad_x[None], *[grads[n] for n in ALL_WEIGHTS], *[deltas[n] for n in ALL_WEIGHTS],
            *[new_m[n] for n in ALL_WEIGHTS], *[new_v[n] for n in ALL_WEIGHTS])
```

```python
import functools
import math

import jax
import jax.numpy as jnp
import numpy as np
from jax import lax
from jax.experimental import pallas as pl
from jax.experimental.pallas import tpu as pltpu

F32 = jnp.float32
BF16 = jnp.bfloat16

D_MODEL = 1024
N_HEADS = 8
HEAD_DIM = 64
ATTN_W = 512
KV_W = 128
BLOCK = 128
N_BUCKETS = 32
SSM_W = 512
N_GROUPS = 32
N_STATE = 64
GROUP_CH = 16
STATES = N_GROUPS * N_STATE
D_FF = 4096
IN_W = 3328
SPLITS = (0, 512, 640, 768, 1280, 2304, 3328)
RMS_EPS = 1e-6
NEG_INF = -1e30
SUBLANES = 8
LANES = 128
SSM_LANE_BLOCK = 512
N_SSM_BLOCKS = STATES // SSM_LANE_BLOCK
VMEM_BIG = 52 * 1024 * 1024
VMEM_MID = 40 * 1024 * 1024

ADAM_LR = 0.001
ADAM_B1 = 0.9
ADAM_B2 = 0.999
ADAM_EPS = 1e-08
ADAM_WD = 0.01
ADAM_STEP = 10

N_DEV = 8


def _dot(a, b):
    return jnp.dot(a, b, preferred_element_type=F32)


def _dot_nt(a, b):
    return lax.dot_general(a, b, (((1,), (1,)), ((), ())), preferred_element_type=F32)


def _dot_tn(a, b):
    return lax.dot_general(a, b, (((0,), (0,)), ((), ())), preferred_element_type=F32)


def _rms_scale(x):
    return lax.rsqrt(jnp.mean(x * x, axis=-1, keepdims=True) + RMS_EPS)


def _rms_bwd(dy, x, r, g):
    t = dy * g
    dx = r * t - x * (r * r * r) * jnp.mean(t * x, axis=-1, keepdims=True)
    dg = jnp.sum(dy * x * r, axis=0, keepdims=True)
    return dx, dg


def _const_spec(shape):
    nd = len(shape)
    return pl.BlockSpec(shape, lambda *_: (0,) * nd, pipeline_mode=pl.Buffered(1))


def _in_hbm(*arrays):
    return tuple(pltpu.with_memory_space_constraint(a, pltpu.HBM) for a in arrays)


def _whole(shape):
    nd = len(shape)
    return pl.BlockSpec(shape, lambda *_: (0,) * nd)


def _params(sem, vmem=None):
    return pltpu.CompilerParams(dimension_semantics=sem, vmem_limit_bytes=vmem)


MESH_IDS = pl.DeviceIdType.MESH
HBM_SPEC = pl.BlockSpec(memory_space=pl.ANY)


class _Carry:
    def __init__(self, inputs, out_shapes, sems, start, finish):
        self.inputs, self.out_shapes, self.sems, self.start, self.finish = list(inputs), list(out_shapes), list(sems), start, finish


def _join(a, b):
    na_in, na_out, na_sem = len(a.inputs), len(a.out_shapes), len(a.sems)

    def start(ins, outs, sems):
        a.start(ins[:na_in], outs[:na_out], sems[:na_sem])
        b.start(ins[na_in:], outs[na_out:], sems[na_sem:])

    def finish(ins, outs, sems):
        a.finish(ins[:na_in], outs[:na_out], sems[:na_sem])
        b.finish(ins[na_in:], outs[na_out:], sems[na_sem:])

    return _Carry(a.inputs + b.inputs, a.out_shapes + b.out_shapes, a.sems + b.sems, start, finish)


def _hosted_call(body, carry, edge, *, name, grid, in_specs, out_specs, out_shape, scratch_shapes, compiler_params, inputs):
    n_in, n_out = len(in_specs), len(out_specs)
    inputs = [a if s.memory_space == pltpu.SMEM else _in_hbm(a)[0] for a, s in zip(inputs, in_specs)]
    if carry is None:
        outs = pl.pallas_call(body, name=name, grid=grid, in_specs=in_specs, out_specs=out_specs, out_shape=out_shape,
                              scratch_shapes=scratch_shapes, compiler_params=compiler_params)(*inputs)
        return list(outs), []
    c_in, c_out, c_sem = len(carry.inputs), len(carry.out_shapes), len(carry.sems)

    def wrapped(*refs):
        ins, refs = refs[:n_in], refs[n_in:]
        cins, refs = refs[:c_in], refs[c_in:]
        outs, refs = refs[:n_out], refs[n_out:]
        couts, refs = refs[:c_out], refs[c_out:]
        scratch, csems = refs[:len(refs) - c_sem], refs[len(refs) - c_sem:]
        first, last = edge()

        @pl.when(first)
        def _():
            carry.start(cins, couts, csems)

        body(*ins, *outs, *scratch)

        @pl.when(last)
        def _():
            carry.finish(cins, couts, csems)

    outs = pl.pallas_call(
        wrapped, name=name, grid=grid, in_specs=list(in_specs) + [HBM_SPEC] * c_in,
        out_specs=list(out_specs) + [HBM_SPEC] * c_out, out_shape=list(out_shape) + carry.out_shapes,
        scratch_shapes=list(scratch_shapes) + carry.sems, compiler_params=compiler_params)(*inputs, *_in_hbm(*carry.inputs))
    return list(outs[:n_out]), list(outs[n_out:])


def _edge_1d(n_steps):
    return lambda: (pl.program_id(0) == 0, pl.program_id(0) == n_steps - 1)


def _edge_2d(n0, n1):
    return lambda: ((pl.program_id(0) == 0) & (pl.program_id(1) == 0),
                    (pl.program_id(0) == n0 - 1) & (pl.program_id(1) == n1 - 1))


def _run_carry(carry, name):
    c_in, c_out = len(carry.inputs), len(carry.out_shapes)

    def body(*refs):
        ins, outs, sems = refs[:c_in], refs[c_in:c_in + c_out], refs[c_in + c_out:]
        carry.start(ins, outs, sems)
        carry.finish(ins, outs, sems)

    return pl.pallas_call(body, name=name, in_specs=[HBM_SPEC] * c_in, out_specs=[HBM_SPEC] * c_out,
                          out_shape=carry.out_shapes, scratch_shapes=carry.sems)(*_in_hbm(*carry.inputs))


def _in_proj_fwd(x, g1, w_in_t, tile, carry=None):
    T = x.shape[0]

    def body(x_ref, g_ref, w_ref, q_ref, k_ref, v_ref, u_ref, ga_ref, gs_ref):
        xv = x_ref[...]
        h = (xv * _rms_scale(xv) * g_ref[...]).astype(BF16)
        outs = (q_ref, k_ref, v_ref, u_ref, ga_ref, gs_ref)
        for p, o_ref in enumerate(outs):
            o_ref[...] = _dot_nt(h, w_ref[SPLITS[p]:SPLITS[p + 1], :]).astype(o_ref.dtype)

    widths = [SPLITS[p + 1] - SPLITS[p] for p in range(6)]
    dtypes = [BF16, BF16, BF16, F32, F32, F32]
    return _hosted_call(
        body, carry, _edge_1d(T // tile), name="in_proj_fwd", grid=(T // tile,),
        in_specs=[pl.BlockSpec((tile, D_MODEL), lambda i: (i, 0)), _const_spec((1, D_MODEL)), _const_spec((IN_W, D_MODEL))],
        out_specs=[pl.BlockSpec((tile, w), lambda i: (i, 0)) for w in widths],
        out_shape=[jax.ShapeDtypeStruct((T, w), dt) for w, dt in zip(widths, dtypes)],
        scratch_shapes=[], compiler_params=_params(("arbitrary",), VMEM_MID), inputs=(x, g1, w_in_t))


def _in_proj_bwd(x, g1, w_in_t, dx1, dparts, tile, carry=None):
    T = x.shape[0]
    widths = [SPLITS[p + 1] - SPLITS[p] for p in range(6)]
    n_steps = T // tile

    def body(x_ref, g_ref, w_ref, dx1_ref, dq, dk, dv, du, dga, dgs, gx_ref, dw_hbm, dg_ref, acc_ref):
        i = pl.program_id(0)
        xv = x_ref[...]
        r = _rms_scale(xv)
        g = g_ref[...]
        h = (xv * r * g).astype(BF16)
        dh = jnp.zeros((tile, D_MODEL), F32)
        for p, d_ref in enumerate((dq, dk, dv, du, dga, dgs)):
            dp = d_ref[...]
            rows = slice(SPLITS[p], SPLITS[p + 1])
            dh = dh + _dot(dp, w_ref[rows, :])
            contrib = _dot_tn(dp, h)

            @pl.when(i == 0)
            def _():
                acc_ref[rows, :] = contrib

            @pl.when(i > 0)
            def _():
                acc_ref[rows, :] += contrib

        dxn, dg = _rms_bwd(dh, xv, r, g)
        gx_ref[...] = dx1_ref[...] + dxn

        @pl.when(i == 0)
        def _():
            dg_ref[...] = dg

        @pl.when(i > 0)
        def _():
            dg_ref[...] += dg

        @pl.when(i == n_steps - 1)
        def _():
            pltpu.sync_copy(acc_ref, dw_hbm)

    tok = lambda w: pl.BlockSpec((tile, w), lambda i: (i, 0))
    return _hosted_call(
        body, carry, _edge_1d(n_steps), name="in_proj_bwd", grid=(n_steps,),
        in_specs=[tok(D_MODEL), _const_spec((1, D_MODEL)), _const_spec((IN_W, D_MODEL)), tok(D_MODEL)] + [tok(w) for w in widths],
        out_specs=[tok(D_MODEL), HBM_SPEC, pl.BlockSpec((1, D_MODEL), lambda i: (0, 0))],
        out_shape=[jax.ShapeDtypeStruct((T, D_MODEL), F32), jax.ShapeDtypeStruct((IN_W, D_MODEL), F32),
                   jax.ShapeDtypeStruct((1, D_MODEL), F32)],
        scratch_shapes=[pltpu.VMEM((IN_W, D_MODEL), F32)],
        compiler_params=_params(("arbitrary",), VMEM_BIG), inputs=(x, g1, w_in_t, dx1, *dparts))


def _bucket_table():
    qi = np.arange(BLOCK)[:, None]
    kj = np.arange(2 * BLOCK)[None, :]
    dist = qi + BLOCK - kj
    max_exact = N_BUCKETS // 2
    d = np.maximum(dist, 0)
    df = np.maximum(d, 1).astype(np.float32)
    large = max_exact + (np.log(df / np.float32(max_exact)) / np.float32(math.log(BLOCK / max_exact))
                         * np.float32(N_BUCKETS - max_exact)).astype(np.int32)
    large = np.minimum(large, N_BUCKETS - 1)
    bucket = np.where(d < max_exact, d, large)
    return np.where((dist >= 0) & (dist < BLOCK), bucket, -1).astype(np.int32)


def _build_bias(bucket_ref, rb_ref, bias_ref):
    bk = bucket_ref[...]
    for h in range(N_HEADS):
        def add(b, acc, h=h):
            return acc + jnp.where(bk == b, rb_ref[h, b], 0.0)
        bias_ref[h] = lax.fori_loop(0, N_BUCKETS, add, jnp.zeros((BLOCK, 2 * BLOCK), F32))


def _kv_variants(prev_ref, cur_ref):
    cat = jnp.concatenate([prev_ref[...], cur_ref[...]], axis=0)
    lo = lax.broadcasted_iota(jnp.int32, cat.shape, 1) < HEAD_DIM
    zero = jnp.zeros_like(cat)
    head0_lo = jnp.where(lo, cat, zero)
    head1_hi = jnp.where(lo, zero, cat)
    return ((head0_lo, pltpu.roll(head0_lo, HEAD_DIM, 1)), (pltpu.roll(head1_hi, HEAD_DIM, 1), head1_hi))


def _merge_kv_grads(g):
    lo = lax.broadcasted_iota(jnp.int32, g[0][0].shape, 1) < HEAD_DIM
    return jnp.where(lo, g[0][0] + pltpu.roll(g[0][1], HEAD_DIM, 1), g[1][1] + pltpu.roll(g[1][0], HEAD_DIM, 1))


def _head_lanes(h):
    return slice((h // 2) * LANES, (h // 2 + 1) * LANES)


def _attn_probs(q_ref, kvar, bias_ref, sk_ref, valid, s_ref):
    for h in range(N_HEADS):
        s_ref[h] = _dot_nt(q_ref[:, _head_lanes(h)], kvar[h // 4][h % 2])
    head = lax.broadcasted_iota(jnp.int32, (N_HEADS, 1, 1), 0)
    sink = jnp.zeros((N_HEADS, 1, 1), F32)
    for h in range(N_HEADS):
        sink = jnp.where(head == h, sk_ref[0, h], sink)
    s = jnp.where(valid[None], s_ref[...] * (HEAD_DIM ** -0.5) + bias_ref[...], NEG_INF)
    m = jnp.maximum(jnp.max(s, axis=-1, keepdims=True), sink)
    p = jnp.exp(s - m)
    e_sink = jnp.exp(sink - m)
    inv = 1.0 / (jnp.sum(p, axis=-1, keepdims=True) + e_sink)
    return p * inv, e_sink * inv


def _attn_valid(bucket_ref, n):
    col = lax.broadcasted_iota(jnp.int32, (BLOCK, 2 * BLOCK), 1)
    return (bucket_ref[...] >= 0) & ((n > 0) | (col >= BLOCK))


def _attn_fwd(q, k, v, bucket, rel_bias, sinks, carry=None):
    T = q.shape[0]
    nb = T // BLOCK

    def body(q_ref, kc_ref, kp_ref, vc_ref, vp_ref, bucket_ref, rb_ref, sk_ref, o_ref, bias_ref, s_ref, p_ref):
        n = pl.program_id(0)

        @pl.when(n == 0)
        def _():
            _build_bias(bucket_ref, rb_ref, bias_ref)

        kvar = _kv_variants(kp_ref, kc_ref)
        vvar = _kv_variants(vp_ref, vc_ref)
        pr, _ = _attn_probs(q_ref, kvar, bias_ref, sk_ref, _attn_valid(bucket_ref, n), s_ref)
        p_ref[...] = pr.astype(BF16)
        for m in range(N_HEADS // 2):
            acc = _dot(p_ref[2 * m], vvar[m // 2][0]) + _dot(p_ref[2 * m + 1], vvar[m // 2][1])
            o_ref[:, m * LANES:(m + 1) * LANES] = acc.astype(o_ref.dtype)

    cur = lambda w: pl.BlockSpec((BLOCK, w), lambda n: (n, 0))
    prev = lambda w: pl.BlockSpec((BLOCK, w), lambda n: (jnp.maximum(n - 1, 0), 0))
    smem = pl.BlockSpec(memory_space=pltpu.SMEM)
    return _hosted_call(
        body, carry, _edge_1d(nb), name="attn_fwd", grid=(nb,),
        in_specs=[cur(ATTN_W), cur(KV_W), prev(KV_W), cur(KV_W), prev(KV_W), _const_spec((BLOCK, 2 * BLOCK)), smem, smem],
        out_specs=[cur(ATTN_W)],
        out_shape=[jax.ShapeDtypeStruct((T, ATTN_W), BF16)],
        scratch_shapes=[pltpu.VMEM((N_HEADS, BLOCK, 2 * BLOCK), F32), pltpu.VMEM((N_HEADS, BLOCK, 2 * BLOCK), F32),
                        pltpu.VMEM((N_HEADS, BLOCK, 2 * BLOCK), BF16)],
        compiler_params=_params(("arbitrary",)), inputs=(q, k, k, v, v, bucket, rel_bias, sinks))


ATTN_SMALL_ROWS = N_BUCKETS + SUBLANES


def _attn_bwd(q, k, v, datt, bucket, rel_bias, sinks, carry=None):
    T = q.shape[0]
    nb = T // BLOCK

    def body(q_ref, do_ref, kc_ref, kp_ref, vc_ref, vp_ref, bucket_ref, rb_ref, sk_ref,
             dq_ref, dk_ref, dv_ref, small_ref, bias_ref, ds_sum_ref, dsink_ref, kcarry_ref, vcarry_ref,
             s_ref, dp_ref, p_ref, dsc_ref):
        n = pl.program_id(0)

        @pl.when(n == 0)
        def _():
            _build_bias(bucket_ref, rb_ref, bias_ref)
            ds_sum_ref[...] = jnp.zeros_like(ds_sum_ref)
            dsink_ref[...] = jnp.zeros_like(dsink_ref)
            kcarry_ref[...] = jnp.zeros_like(kcarry_ref)
            vcarry_ref[...] = jnp.zeros_like(vcarry_ref)

        @pl.when(n < nb)
        def _():
            kvar = _kv_variants(kp_ref, kc_ref)
            vvar = _kv_variants(vp_ref, vc_ref)
            pr, p_sink = _attn_probs(q_ref, kvar, bias_ref, sk_ref, _attn_valid(bucket_ref, n), s_ref)
            for h in range(N_HEADS):
                dp_ref[h] = _dot_nt(do_ref[:, _head_lanes(h)], vvar[h // 4][h % 2])
            dp = dp_ref[...]
            dsum = jnp.sum(pr * dp, axis=-1, keepdims=True)
            ds = pr * (dp - dsum)
            ds_sum_ref[...] += ds
            dsink_ref[...] -= jnp.sum(p_sink * dsum, axis=1, keepdims=True)
            dsc_ref[...] = (ds * (HEAD_DIM ** -0.5)).astype(BF16)
            p_ref[...] = pr.astype(BF16)
            for m in range(N_HEADS // 2):
                dqm = _dot(dsc_ref[2 * m], kvar[m // 2][0]) + _dot(dsc_ref[2 * m + 1], kvar[m // 2][1])
                dq_ref[:, m * LANES:(m + 1) * LANES] = dqm.astype(dq_ref.dtype)
            dk_var = [[None, None], [None, None]]
            dv_var = [[None, None], [None, None]]
            for kvh in range(2):
                for e in range(2):
                    heads = [h for h in range(N_HEADS) if h // 4 == kvh and h % 2 == e]
                    dk_var[kvh][e] = sum(_dot_tn(dsc_ref[h], q_ref[:, _head_lanes(h)]) for h in heads)
                    dv_var[kvh][e] = sum(_dot_tn(p_ref[h], do_ref[:, _head_lanes(h)]) for h in heads)
            dk_cat = _merge_kv_grads(dk_var)
            dv_cat = _merge_kv_grads(dv_var)

            @pl.when(n > 0)
            def _():
                dk_ref[...] = (kcarry_ref[...] + dk_cat[:BLOCK]).astype(dk_ref.dtype)
                dv_ref[...] = (vcarry_ref[...] + dv_cat[:BLOCK]).astype(dv_ref.dtype)

            kcarry_ref[...] = dk_cat[BLOCK:]
            vcarry_ref[...] = dv_cat[BLOCK:]

        @pl.when(n == nb)
        def _():
            dk_ref[...] = kcarry_ref[...].astype(dk_ref.dtype)
            dv_ref[...] = vcarry_ref[...].astype(dv_ref.dtype)
            bk = bucket_ref[...]
            row = lax.broadcasted_iota(jnp.int32, (N_HEADS, ATTN_SMALL_ROWS, LANES), 1)

            def add(b, acc):
                masked = jnp.where((bk == b)[None], ds_sum_ref[...], 0.0)
                val = jnp.sum(jnp.sum(masked, axis=1, keepdims=True), axis=2, keepdims=True)
                return acc + jnp.where(row == b, val, 0.0)

            small_ref[...] = lax.fori_loop(0, N_BUCKETS, add, jnp.where(row == N_BUCKETS, dsink_ref[...], 0.0))

    last = nb - 1
    cur = lambda w: pl.BlockSpec((BLOCK, w), lambda n: (jnp.minimum(n, last), 0))
    prev = lambda w: pl.BlockSpec((BLOCK, w), lambda n: (jnp.clip(n - 1, 0, last), 0))
    smem = pl.BlockSpec(memory_space=pltpu.SMEM)
    return _hosted_call(
        body, carry, _edge_1d(nb + 1), name="attn_bwd", grid=(nb + 1,),
        in_specs=[cur(ATTN_W), cur(ATTN_W), cur(KV_W), prev(KV_W), cur(KV_W), prev(KV_W),
                  _const_spec((BLOCK, 2 * BLOCK)), smem, smem],
        out_specs=[cur(ATTN_W), prev(KV_W), prev(KV_W),
                   pl.BlockSpec((N_HEADS, ATTN_SMALL_ROWS, LANES), lambda n: (0, 0, 0))],
        out_shape=[jax.ShapeDtypeStruct((T, ATTN_W), BF16), jax.ShapeDtypeStruct((T, KV_W), BF16),
                   jax.ShapeDtypeStruct((T, KV_W), BF16), jax.ShapeDtypeStruct((N_HEADS, ATTN_SMALL_ROWS, LANES), F32)],
        scratch_shapes=[pltpu.VMEM((N_HEADS, BLOCK, 2 * BLOCK), F32), pltpu.VMEM((N_HEADS, BLOCK, 2 * BLOCK), F32),
                        pltpu.VMEM((N_HEADS, 1, 1), F32), pltpu.VMEM((BLOCK, KV_W), F32), pltpu.VMEM((BLOCK, KV_W), F32),
                        pltpu.VMEM((N_HEADS, BLOCK, 2 * BLOCK), F32), pltpu.VMEM((N_HEADS, BLOCK, 2 * BLOCK), F32),
                        pltpu.VMEM((N_HEADS, BLOCK, 2 * BLOCK), BF16), pltpu.VMEM((N_HEADS, BLOCK, 2 * BLOCK), BF16)],
        compiler_params=_params(("arbitrary",)), inputs=(q, datt, k, k, v, v, bucket, rel_bias, sinks))


def _cmul(ar, ai, br, bi):
    return ar * br - ai * bi, ar * bi + ai * br


def _ssm_discretize(lr, li, ldt):
    dt = jnp.exp(ldt)
    mag = jnp.exp(lr * dt)
    ab_re = mag * jnp.cos(li * dt)
    ab_im = mag * jnp.sin(li * dt)
    nr = ab_re - 1.0
    den = lr * lr + li * li
    f_re = (nr * lr + ab_im * li) / den
    f_im = (ab_im * lr - nr * li) / den
    return ab_re, ab_im, f_re, f_im


def _ssm_prep(lam_re, lam_im, ldt_rep, bd_re, bd_im):
    def body(lr_ref, li_ref, ldt_ref, bdr_ref, bdi_ref, ar_ref, ai_ref, br_ref, bi_ref):
        ab_re, ab_im, f_re, f_im = _ssm_discretize(lr_ref[...], li_ref[...], ldt_ref[...])
        ar_ref[...] = ab_re
        ai_ref[...] = ab_im
        bdr, bdi = bdr_ref[0], bdi_ref[0]
        br_ref[0] = (bdr * f_re - bdi * f_im).astype(BF16)
        bi_ref[0] = (bdi * f_re + bdr * f_im).astype(BF16)

    row = pl.BlockSpec((1, SSM_LANE_BLOCK), lambda j: (0, j))
    mat = pl.BlockSpec((1, LANES, SSM_LANE_BLOCK), lambda j: (j, 0, 0))
    return pl.pallas_call(
        body, name="ssm_prep", grid=(N_SSM_BLOCKS,),
        in_specs=[row, row, row, mat, mat], out_specs=[row, row, mat, mat],
        out_shape=[jax.ShapeDtypeStruct((1, STATES), F32)] * 2 + [jax.ShapeDtypeStruct((N_SSM_BLOCKS, LANES, SSM_LANE_BLOCK), BF16)] * 2,
        compiler_params=_params(("arbitrary",)),
    )(*_in_hbm(lam_re, lam_im, ldt_rep, bd_re, bd_im))


def _ssm_prep_bwd(lam_re, lam_im, ldt_rep, bd_re, bd_im, dbr, dbi, da_re, da_im):
    def body(lr_ref, li_ref, ldt_ref, bdr_ref, bdi_ref, dbr_ref, dbi_ref, dar_ref, dai_ref,
             dbdr_ref, dbdi_ref, dlr_ref, dli_ref, dldt_ref):
        lr, li, ldt = lr_ref[...], li_ref[...], ldt_ref[...]
        (_, _, f_re, f_im), vjp = jax.vjp(_ssm_discretize, lr, li, ldt)
        bdr, bdi, gbr, gbi = bdr_ref[0], bdi_ref[0], dbr_ref[0], dbi_ref[0]
        dbdr_ref[0] = gbr * f_re + gbi * f_im
        dbdi_ref[0] = gbi * f_re - gbr * f_im
        df_re = jnp.sum(gbr * bdr + gbi * bdi, axis=0, keepdims=True)
        df_im = jnp.sum(gbi * bdr - gbr * bdi, axis=0, keepdims=True)
        dlr, dli, dldt = vjp((dar_ref[...], dai_ref[...], df_re, df_im))
        dlr_ref[...] = dlr
        dli_ref[...] = dli
        dldt_ref[...] = dldt

    row = pl.BlockSpec((1, SSM_LANE_BLOCK), lambda j: (0, j))
    mat = pl.BlockSpec((1, LANES, SSM_LANE_BLOCK), lambda j: (j, 0, 0))
    mat_shape = jax.ShapeDtypeStruct((N_SSM_BLOCKS, LANES, SSM_LANE_BLOCK), F32)
    row_shape = jax.ShapeDtypeStruct((1, STATES), F32)
    return pl.pallas_call(
        body, name="ssm_prep_bwd", grid=(N_SSM_BLOCKS,),
        in_specs=[row, row, row, mat, mat, mat, mat, row, row], out_specs=[mat, mat, row, row, row],
        out_shape=[mat_shape, mat_shape, row_shape, row_shape, row_shape],
        compiler_params=_params(("arbitrary",)),
    )(*_in_hbm(lam_re, lam_im, ldt_rep, bd_re, bd_im, dbr, dbi, da_re, da_im))


def _group_sum(x):
    def body(x_ref, o_ref):
        o_ref[...] = jnp.sum(x_ref[...], axis=1, keepdims=True)
    return pl.pallas_call(body, name="ssm_group_sum", grid=(1,), in_specs=[_whole(x.shape)], out_specs=_whole((N_GROUPS, 1)),
                          out_shape=jax.ShapeDtypeStruct((N_GROUPS, 1), F32))(*_in_hbm(x))


def _power_table(ar, ai, p_re_ref, p_im_ref, steps):
    shape = (SUBLANES, SSM_LANE_BLOCK)
    p_re_ref[0:SUBLANES] = jnp.broadcast_to(ar, shape)
    p_im_ref[0:SUBLANES] = jnp.broadcast_to(ai, shape)
    m = 1
    while m < steps:
        rows = m * SUBLANES
        top_re = p_re_ref[rows - SUBLANES:rows]
        top_im = p_im_ref[rows - SUBLANES:rows]
        cur_re = p_re_ref[0:rows].reshape(m, SUBLANES, SSM_LANE_BLOCK)
        cur_im = p_im_ref[0:rows].reshape(m, SUBLANES, SSM_LANE_BLOCK)
        nxt_re, nxt_im = _cmul(cur_re, cur_im, top_re[None], top_im[None])
        p_re_ref[rows:2 * rows] = nxt_re.reshape(rows, SSM_LANE_BLOCK)
        p_im_ref[rows:2 * rows] = nxt_im.reshape(rows, SSM_LANE_BLOCK)
        m *= 2


def _to_segments(src_ref, dst_ref, steps):
    for s in range(SUBLANES):
        dst_ref[pl.ds(s, steps, stride=SUBLANES), :] = src_ref[s * steps:(s + 1) * steps, :]


def _from_segments(src_ref, dst_ref, steps):
    for s in range(SUBLANES):
        dst_ref[s * steps:(s + 1) * steps, :] = src_ref[pl.ds(s, steps, stride=SUBLANES), :]


def _segment_carries(e_re, e_im, an_re, an_im, c_re, c_im, reverse):
    order = range(SUBLANES - 1, -1, -1) if reverse else range(SUBLANES)
    ins_re, ins_im = [None] * SUBLANES, [None] * SUBLANES
    for s in order:
        ins_re[s], ins_im[s] = c_re, c_im
        pr, pi = _cmul(an_re, an_im, c_re, c_im)
        c_re = e_re[s:s + 1] + pr
        c_im = e_im[s:s + 1] + pi
    return jnp.concatenate(ins_re, axis=0), jnp.concatenate(ins_im, axis=0), c_re, c_im


def _ssm_fwd(u, a_re, a_im, b_re, b_im, c_re, c_im, d_skip, chunk, carry=None):
    T = u.shape[0]
    nc = T // chunk
    steps = chunk // SUBLANES
    blk = SSM_LANE_BLOCK

    def body(u_ref, ar_ref, ai_ref, br_ref, bi_ref, cr_ref, ci_ref, dk_ref,
             y_ref, hr_ref, hi_ref, inr_ref, ini_ref, useg_ref, yseg_ref, pr_ref, pi_ref, carry_ref):
        c = pl.program_id(1)
        ar, ai = ar_ref[...], ai_ref[...]

        @pl.when(c == 0)
        def _():
            _power_table(ar, ai, pr_ref, pi_ref, steps)
            carry_ref[...] = jnp.zeros_like(carry_ref)

        _to_segments(u_ref, useg_ref, steps)
        ub = useg_ref[...].astype(BF16)
        hr_ref[...] = _dot(ub, br_ref[0])
        hi_ref[...] = _dot(ub, bi_ref[0])
        ar8 = jnp.broadcast_to(ar, (SUBLANES, blk))
        ai8 = jnp.broadcast_to(ai, (SUBLANES, blk))

        def scan(t, prev):
            rows = pl.ds(pl.multiple_of(t * SUBLANES, SUBLANES), SUBLANES)
            pr, pi = _cmul(ar8, ai8, prev[0], prev[1])
            nr = pr + hr_ref[rows, :]
            ni = pi + hi_ref[rows, :]
            hr_ref[rows, :] = nr
            hi_ref[rows, :] = ni
            return nr, ni

        lax.fori_loop(1, steps, scan, (hr_ref[0:SUBLANES, :], hi_ref[0:SUBLANES, :]), unroll=4)

        top = slice(chunk - SUBLANES, chunk)
        in_re, in_im, out_re, out_im = _segment_carries(
            hr_ref[top, :], hi_ref[top, :], pr_ref[top, :][0:1], pi_ref[top, :][0:1],
            carry_ref[0:1, :], carry_ref[1:2, :], reverse=False)
        carry_ref[0:1, :] = out_re
        carry_ref[1:2, :] = out_im
        inr_ref[...] = in_re
        ini_ref[...] = in_im

        def fix(t, _):
            rows = pl.ds(pl.multiple_of(t * SUBLANES, SUBLANES), SUBLANES)
            fr, fi = _cmul(pr_ref[rows, :], pi_ref[rows, :], in_re, in_im)
            hr_ref[rows, :] += fr
            hi_ref[rows, :] += fi
            return 0

        lax.fori_loop(0, steps, fix, 0, unroll=4)

        yseg_ref[...] = _dot(hr_ref[...].astype(BF16), cr_ref[0]) - _dot(hi_ref[...].astype(BF16), ci_ref[0])
        _from_segments(yseg_ref, y_ref, steps)
        y_ref[...] += dk_ref[...] * u_ref[...]

    row = pl.BlockSpec((1, blk), lambda j, c: (0, j))
    b_mat = pl.BlockSpec((1, LANES, blk), lambda j, c: (j, 0, 0))
    c_mat = pl.BlockSpec((1, blk, LANES), lambda j, c: (j, 0, 0))
    tok = pl.BlockSpec((chunk, LANES), lambda j, c: (c, j))
    state = pl.BlockSpec((chunk, blk), lambda j, c: (c, j))
    enter = pl.BlockSpec((SUBLANES, blk), lambda j, c: (c, j))
    return _hosted_call(
        body, carry, _edge_2d(N_SSM_BLOCKS, nc), name="ssm_fwd", grid=(N_SSM_BLOCKS, nc),
        in_specs=[tok, row, row, b_mat, b_mat, c_mat, c_mat, pl.BlockSpec((1, LANES), lambda j, c: (0, j))],
        out_specs=[tok, state, state, enter, enter],
        out_shape=[jax.ShapeDtypeStruct((T, SSM_W), F32), jax.ShapeDtypeStruct((T, STATES), F32),
                   jax.ShapeDtypeStruct((T, STATES), F32), jax.ShapeDtypeStruct((nc * SUBLANES, STATES), F32),
                   jax.ShapeDtypeStruct((nc * SUBLANES, STATES), F32)],
        scratch_shapes=[pltpu.VMEM((chunk, LANES), F32), pltpu.VMEM((chunk, LANES), F32),
                        pltpu.VMEM((chunk, blk), F32), pltpu.VMEM((chunk, blk), F32), pltpu.VMEM((SUBLANES, blk), F32)],
        compiler_params=_params(("arbitrary", "arbitrary"), VMEM_MID),
        inputs=(u, a_re, a_im, b_re, b_im, c_re, c_im, d_skip))


def _ssm_bwd(dy, u, h_re, h_im, in_re, in_im, a_re, a_im, b_re, b_im, c_re, c_im, d_skip, chunk, carry=None):
    T = u.shape[0]
    nc = T // chunk
    steps = chunk // SUBLANES
    blk = SSM_LANE_BLOCK

    def body(dy_ref, u_ref, hr_ref, hi_ref, inr_ref, ini_ref, ar_ref, ai_ref, br_ref, bi_ref, cr_ref, ci_ref, dk_ref,
             du_ref, dbr_ref, dbi_ref, dcr_ref, dci_ref, dar_ref, dai_ref, ddk_ref,
             dyseg_ref, useg_ref, duseg_ref, gr_ref, gi_ref, pr_ref, pi_ref, carry_ref, accr_ref, acci_ref):
        c = pl.program_id(1)
        ar, ai = ar_ref[...], ai_ref[...]

        @pl.when(c == 0)
        def _():
            _power_table(ar, ai, pr_ref, pi_ref, steps)
            carry_ref[...] = jnp.zeros_like(carry_ref)
            accr_ref[...] = jnp.zeros_like(accr_ref)
            acci_ref[...] = jnp.zeros_like(acci_ref)

        _to_segments(dy_ref, dyseg_ref, steps)
        _to_segments(u_ref, useg_ref, steps)
        dyb = dyseg_ref[...].astype(BF16)
        ub = useg_ref[...].astype(BF16)
        gr_ref[...] = _dot_nt(dyb, cr_ref[0])
        gi_ref[...] = -_dot_nt(dyb, ci_ref[0])
        dcr = _dot_tn(hr_ref[...].astype(BF16), dyb)
        dci = -_dot_tn(hi_ref[...].astype(BF16), dyb)
        ddk = jnp.sum(dy_ref[...] * u_ref[...], axis=0, keepdims=True)

        ar8 = jnp.broadcast_to(ar, (SUBLANES, blk))
        ai8 = jnp.broadcast_to(-ai, (SUBLANES, blk))

        def scan(k, nxt):
            t = steps - 2 - k
            rows = pl.ds(pl.multiple_of(t * SUBLANES, SUBLANES), SUBLANES)
            pr, pi = _cmul(ar8, ai8, nxt[0], nxt[1])
            nr = pr + gr_ref[rows, :]
            ni = pi + gi_ref[rows, :]
            gr_ref[rows, :] = nr
            gi_ref[rows, :] = ni
            return nr, ni

        top = slice(chunk - SUBLANES, chunk)
        lax.fori_loop(0, steps - 1, scan, (gr_ref[top, :], gi_ref[top, :]), unroll=4)

        gin_re, gin_im, out_re, out_im = _segment_carries(
            gr_ref[0:SUBLANES, :], gi_ref[0:SUBLANES, :], pr_ref[top, :][0:1], -pi_ref[top, :][0:1],
            carry_ref[0:1, :], carry_ref[1:2, :], reverse=True)
        carry_ref[0:1, :] = out_re
        carry_ref[1:2, :] = out_im

        def fix(t, acc):
            rows = pl.ds(pl.multiple_of(t * SUBLANES, SUBLANES), SUBLANES)
            prow = pl.ds(pl.multiple_of((steps - 1 - t) * SUBLANES, SUBLANES), SUBLANES)
            fr, fi = _cmul(pr_ref[prow, :], -pi_ref[prow, :], gin_re, gin_im)
            g_re = gr_ref[rows, :] + fr
            g_im = gi_ref[rows, :] + fi
            gr_ref[rows, :] = g_re
            gi_ref[rows, :] = g_im
            before = pl.ds(pl.multiple_of(jnp.maximum(t - 1, 0) * SUBLANES, SUBLANES), SUBLANES)
            first = t == 0
            hp_re = jnp.where(first, inr_ref[...], hr_ref[before, :])
            hp_im = jnp.where(first, ini_ref[...], hi_ref[before, :])
            return acc[0] + g_re * hp_re + g_im * hp_im, acc[1] + g_im * hp_re - g_re * hp_im

        acc_re, acc_im = lax.fori_loop(0, steps, fix, (accr_ref[...], acci_ref[...]), unroll=2)
        accr_ref[...] = acc_re
        acci_ref[...] = acc_im

        gbr = gr_ref[...].astype(BF16)
        gbi = gi_ref[...].astype(BF16)
        duseg_ref[...] = _dot_nt(gbr, br_ref[0]) + _dot_nt(gbi, bi_ref[0])
        _from_segments(duseg_ref, dyseg_ref, steps)
        du_ref[...] = (dyseg_ref[...] + dk_ref[...] * dy_ref[...]).astype(BF16)
        dbr = _dot_tn(ub, gbr)
        dbi = _dot_tn(ub, gbi)

        @pl.when(c == 0)
        def _():
            dbr_ref[0] = dbr
            dbi_ref[0] = dbi
            dcr_ref[0] = dcr
            dci_ref[0] = dci
            ddk_ref[...] = ddk

        @pl.when(c > 0)
        def _():
            dbr_ref[0] += dbr
            dbi_ref[0] += dbi
            dcr_ref[0] += dcr
            dci_ref[0] += dci
            ddk_ref[...] += ddk

        @pl.when(c == nc - 1)
        def _():
            dar_ref[...] = jnp.sum(acc_re, axis=0, keepdims=True)
            dai_ref[...] = jnp.sum(acc_im, axis=0, keepdims=True)

    rev = lambda c: nc - 1 - c
    row = pl.BlockSpec((1, blk), lambda j, c: (0, j))
    b_mat = pl.BlockSpec((1, LANES, blk), lambda j, c: (j, 0, 0))
    c_mat = pl.BlockSpec((1, blk, LANES), lambda j, c: (j, 0, 0))
    tok = pl.BlockSpec((chunk, LANES), lambda j, c: (rev(c), j))
    state = pl.BlockSpec((chunk, blk), lambda j, c: (rev(c), j))
    enter = pl.BlockSpec((SUBLANES, blk), lambda j, c: (rev(c), j))
    chan = pl.BlockSpec((1, LANES), lambda j, c: (0, j))
    f32 = lambda *s: jax.ShapeDtypeStruct(s, F32)
    return _hosted_call(
        body, carry, _edge_2d(N_SSM_BLOCKS, nc), name="ssm_bwd", grid=(N_SSM_BLOCKS, nc),
        in_specs=[tok, tok, state, state, enter, enter, row, row, b_mat, b_mat, c_mat, c_mat, chan],
        out_specs=[tok, b_mat, b_mat, c_mat, c_mat, row, row, chan],
        out_shape=[jax.ShapeDtypeStruct((T, SSM_W), BF16), f32(N_SSM_BLOCKS, LANES, blk), f32(N_SSM_BLOCKS, LANES, blk),
                   f32(N_SSM_BLOCKS, blk, LANES), f32(N_SSM_BLOCKS, blk, LANES), f32(1, STATES), f32(1, STATES), f32(1, SSM_W)],
        scratch_shapes=[pltpu.VMEM((chunk, LANES), F32), pltpu.VMEM((chunk, LANES), F32), pltpu.VMEM((chunk, LANES), F32),
                        pltpu.VMEM((chunk, blk), F32), pltpu.VMEM((chunk, blk), F32),
                        pltpu.VMEM((chunk, blk), F32), pltpu.VMEM((chunk, blk), F32),
                        pltpu.VMEM((SUBLANES, blk), F32), pltpu.VMEM((SUBLANES, blk), F32), pltpu.VMEM((SUBLANES, blk), F32)],
        compiler_params=_params(("arbitrary", "arbitrary"), VMEM_BIG),
        inputs=(dy, u, h_re, h_im, in_re, in_im, a_re, a_im, b_re, b_im, c_re, c_im, d_skip))


def _merge_forward(y, att, ga, gs, w_glu, w_ssm, w_attn):
    z = jax.nn.gelu(y)
    zb = z.astype(BF16)
    gl = jax.nn.sigmoid(_dot(zb, w_glu))
    z2b = (z * gl).astype(BF16)
    y_ssm = _dot(z2b, w_ssm)
    y_attn = _dot(att, w_attn)
    sa = jax.nn.sigmoid(ga)
    ss = jax.nn.sigmoid(gs)
    merged = (sa * y_attn + ss * y_ssm).astype(BF16)
    return z, zb, gl, z2b, y_ssm, y_attn, sa, ss, merged


def _merge_fwd(x, y, att, ga, gs, g2, g3, w_glu, w_ssm, w_attn, w_out, tile):
    T = x.shape[0]

    def body(x_ref, y_ref, att_ref, ga_ref, gs_ref, g2_ref, g3_ref, wg_ref, ws_ref, wa_ref, wo_ref, x1_ref, o_ref, h2_ref):
        merged = _merge_forward(y_ref[...], att_ref[...], ga_ref[...], gs_ref[...], wg_ref[...], ws_ref[...], wa_ref[...])[-1]
        o = _dot(merged, wo_ref[...])
        x1 = x_ref[...] + o * _rms_scale(o) * g2_ref[...]
        o_ref[...] = o
        x1_ref[...] = x1
        h2_ref[...] = (x1 * _rms_scale(x1) * g3_ref[...]).astype(BF16)

    tok = lambda w: pl.BlockSpec((tile, w), lambda i: (i, 0))
    vec = _const_spec((1, D_MODEL))
    return pl.pallas_call(
        body, name="merge_fwd", grid=(T // tile,),
        in_specs=[tok(D_MODEL), tok(SSM_W), tok(ATTN_W), tok(D_MODEL), tok(D_MODEL), vec, vec,
                  _const_spec((SSM_W, SSM_W)), _const_spec((SSM_W, D_MODEL)), _const_spec((ATTN_W, D_MODEL)),
                  _const_spec((D_MODEL, D_MODEL))],
        out_specs=[tok(D_MODEL), tok(D_MODEL), tok(D_MODEL)],
        out_shape=[jax.ShapeDtypeStruct((T, D_MODEL), F32), jax.ShapeDtypeStruct((T, D_MODEL), F32),
                   jax.ShapeDtypeStruct((T, D_MODEL), BF16)],
        compiler_params=_params(("arbitrary",), VMEM_MID),
    )(*_in_hbm(x, y, att, ga, gs, g2, g3, w_glu, w_ssm, w_attn, w_out))


def _merge_bwd(dh2, dx2, x1, o, y, att, ga, gs, g2, g3, w_glu, w_ssm, w_attn, w_out, tile, carry=None):
    T = x1.shape[0]
    n_steps = T // tile

    def body(dh2_ref, dx2_ref, x1_ref, o_ref, y_ref, att_ref, ga_ref, gs_ref, g2_ref, g3_ref, wg_ref, ws_ref, wa_ref, wo_ref,
             dx1_ref, dga_ref, dgs_ref, datt_ref, dy_ref, dwg_hbm, dws_hbm, dwa_hbm, dwo_hbm, dg2_ref, dg3_ref,
             awg_ref, aws_ref, awa_ref, awo_ref):
        i = pl.program_id(0)
        x1v, ov = x1_ref[...], o_ref[...]
        dxn, dg3 = _rms_bwd(dh2_ref[...], x1v, _rms_scale(x1v), g3_ref[...])
        dx1 = dx2_ref[...] + dxn
        dx1_ref[...] = dx1
        do, dg2 = _rms_bwd(dx1, ov, _rms_scale(ov), g2_ref[...])
        dob = do.astype(BF16)

        yv = y_ref[...]
        att = att_ref[...]
        z, zb, gl, z2b, y_ssm, y_attn, sa, ss, merged = _merge_forward(
            yv, att, ga_ref[...], gs_ref[...], wg_ref[...], ws_ref[...], wa_ref[...])
        dmerged = _dot_nt(dob, wo_ref[...])
        dya = (dmerged * sa).astype(BF16)
        dys = (dmerged * ss).astype(BF16)
        dga_ref[...] = (dmerged * y_attn * sa * (1.0 - sa)).astype(BF16)
        dgs_ref[...] = (dmerged * y_ssm * ss * (1.0 - ss)).astype(BF16)
        datt_ref[...] = _dot_nt(dya, wa_ref[...]).astype(BF16)
        dz2 = _dot_nt(dys, ws_ref[...])
        dpre = (dz2 * z * gl * (1.0 - gl)).astype(BF16)
        dz = dz2 * gl + _dot_nt(dpre, wg_ref[...])
        _, gelu_vjp = jax.vjp(jax.nn.gelu, yv)
        dy_ref[...] = gelu_vjp(dz)[0]

        grads = ((awo_ref, _dot_tn(merged, dob)), (awa_ref, _dot_tn(att, dya)),
                 (aws_ref, _dot_tn(z2b, dys)), (awg_ref, _dot_tn(zb, dpre)), (dg2_ref, dg2), (dg3_ref, dg3))

        @pl.when(i == 0)
        def _():
            for ref, val in grads:
                ref[...] = val

        @pl.when(i > 0)
        def _():
            for ref, val in grads:
                ref[...] += val

        @pl.when(i == n_steps - 1)
        def _():
            pltpu.sync_copy(awg_ref, dwg_hbm)
            pltpu.sync_copy(aws_ref, dws_hbm)
            pltpu.sync_copy(awa_ref, dwa_hbm)
            pltpu.sync_copy(awo_ref, dwo_hbm)

    tok = lambda w: pl.BlockSpec((tile, w), lambda i: (i, 0))
    vec = _const_spec((1, D_MODEL))
    any_ = pl.BlockSpec(memory_space=pl.ANY)
    vec_out = pl.BlockSpec((1, D_MODEL), lambda i: (0, 0))
    f32 = lambda *s: jax.ShapeDtypeStruct(s, F32)
    bf = lambda *s: jax.ShapeDtypeStruct(s, BF16)
    return _hosted_call(
        body, carry, _edge_1d(n_steps), name="merge_bwd", grid=(n_steps,),
        in_specs=[tok(D_MODEL), tok(D_MODEL), tok(D_MODEL), tok(D_MODEL), tok(SSM_W), tok(ATTN_W), tok(D_MODEL), tok(D_MODEL),
                  vec, vec, _const_spec((SSM_W, SSM_W)), _const_spec((SSM_W, D_MODEL)), _const_spec((ATTN_W, D_MODEL)),
                  _const_spec((D_MODEL, D_MODEL))],
        out_specs=[tok(D_MODEL), tok(D_MODEL), tok(D_MODEL), tok(ATTN_W), tok(SSM_W), any_, any_, any_, any_, vec_out, vec_out],
        out_shape=[f32(T, D_MODEL), bf(T, D_MODEL), bf(T, D_MODEL), bf(T, ATTN_W), f32(T, SSM_W),
                   f32(SSM_W, SSM_W), f32(SSM_W, D_MODEL), f32(ATTN_W, D_MODEL), f32(D_MODEL, D_MODEL),
                   f32(1, D_MODEL), f32(1, D_MODEL)],
        scratch_shapes=[pltpu.VMEM((SSM_W, SSM_W), F32), pltpu.VMEM((SSM_W, D_MODEL), F32),
                        pltpu.VMEM((ATTN_W, D_MODEL), F32), pltpu.VMEM((D_MODEL, D_MODEL), F32)],
        compiler_params=_params(("arbitrary",), VMEM_BIG),
        inputs=(dh2, dx2, x1, o, y, att, ga, gs, g2, g3, w_glu, w_ssm, w_attn, w_out))


FF_SHARD = D_FF // N_DEV


def _mlp_fwd(h2, x1, target, g4, w_ff_in, w_ff_out, tile):
    T = h2.shape[0]
    per_step = 2
    ff_chunk = per_step * FF_SHARD
    n_i, n_k = T // tile, N_DEV // per_step

    def body(h2_ref, x1_ref, tg_ref, g4_ref, wi_ref, wo_ref, a_ref, dfo_ref, dx2_ref, loss_ref, dg4_ref, acc_ref):
        i, k = pl.program_id(0), pl.program_id(1)
        h2v = h2_ref[...]
        part = jnp.zeros((tile, D_MODEL), F32)
        for s in range(per_step):
            a = _dot(h2v, wi_ref[s])
            a_ref[:, s * FF_SHARD:(s + 1) * FF_SHARD] = a.astype(BF16)
            ra = jnp.maximum(a, 0.0)
            part = part + _dot((ra * ra).astype(BF16), wo_ref[s])

        @pl.when(k == 0)
        def _():
            acc_ref[...] = part

        @pl.when(k > 0)
        def _():
            acc_ref[...] += part

        @pl.when(k == n_k - 1)
        def _():
            f = acc_ref[...]
            r = _rms_scale(f)
            g = g4_ref[...]
            err = x1_ref[...] + f * r * g - tg_ref[...]
            dx2 = err * (1.0 / D_MODEL)
            dx2_ref[...] = dx2
            dfo, dg = _rms_bwd(dx2, f, r, g)
            dfo_ref[...] = dfo.astype(BF16)
            row = lax.broadcasted_iota(jnp.int32, (8, LANES), 0)
            col = lax.broadcasted_iota(jnp.int32, (8, LANES), 1)
            loss = jnp.where((row == 0) & (col == 0), (0.5 / D_MODEL) * jnp.sum(err * err), 0.0)

            @pl.when(i == 0)
            def _():
                loss_ref[...] = loss
                dg4_ref[...] = dg

            @pl.when(i > 0)
            def _():
                loss_ref[...] += loss
                dg4_ref[...] += dg

    tok = pl.BlockSpec((tile, D_MODEL), lambda i, k: (i, 0))
    return pl.pallas_call(
        body, name="mlp_fwd", grid=(n_i, n_k),
        in_specs=[tok, tok, tok, pl.BlockSpec((1, D_MODEL), lambda i, k: (0, 0)),
                  pl.BlockSpec((per_step, D_MODEL, FF_SHARD), lambda i, k: (k, 0, 0)),
                  pl.BlockSpec((per_step, FF_SHARD, D_MODEL), lambda i, k: (k, 0, 0))],
        out_specs=[pl.BlockSpec((tile, ff_chunk), lambda i, k: (i, k)), tok, tok,
                   pl.BlockSpec((8, LANES), lambda i, k: (0, 0)), pl.BlockSpec((1, D_MODEL), lambda i, k: (0, 0))],
        out_shape=[jax.ShapeDtypeStruct((T, D_FF), BF16), jax.ShapeDtypeStruct((T, D_MODEL), BF16),
                   jax.ShapeDtypeStruct((T, D_MODEL), F32), jax.ShapeDtypeStruct((8, LANES), F32),
                   jax.ShapeDtypeStruct((1, D_MODEL), F32)],
        scratch_shapes=[pltpu.VMEM((tile, D_MODEL), F32)],
        compiler_params=_params(("arbitrary", "arbitrary"), VMEM_BIG),
    )(*_in_hbm(h2, x1, target, g4, w_ff_in, w_ff_out))


def _mlp_bwd(dfo, a, h2, w_ff_in, w_ff_out, tile):
    T = h2.shape[0]
    ff_chunk = FF_SHARD
    n_i, n_k = T // tile, N_DEV

    def body(dfo_ref, a_ref, h2_ref, wi_ref, wo_ref, dwi_ref, dwo_ref, dh2_ref, acc_ref):
        k, i = pl.program_id(0), pl.program_id(1)
        dfo = dfo_ref[...]
        ra = jnp.maximum(a_ref[...].astype(F32), 0.0)
        drr = _dot_nt(dfo, wo_ref[0])
        da = (drr * (2.0 * ra)).astype(BF16)
        dwo = _dot_tn((ra * ra).astype(BF16), dfo)
        dwi = _dot_tn(h2_ref[...], da)
        part = _dot_nt(da, wi_ref[0])
        rows = pl.ds(pl.multiple_of(i * tile, tile), tile)

        @pl.when(i == 0)
        def _():
            dwi_ref[0] = dwi
            dwo_ref[0] = dwo

        @pl.when(i > 0)
        def _():
            dwi_ref[0] += dwi
            dwo_ref[0] += dwo

        @pl.when(k == 0)
        def _():
            acc_ref[rows, :] = part

        @pl.when((k > 0) & (k < n_k - 1))
        def _():
            acc_ref[rows, :] += part

        @pl.when(k == n_k - 1)
        def _():
            dh2_ref[...] = acc_ref[rows, :] + part

    return pl.pallas_call(
        body, name="mlp_bwd", grid=(n_k, n_i),
        in_specs=[pl.BlockSpec((tile, D_MODEL), lambda k, i: (i, 0)), pl.BlockSpec((tile, ff_chunk), lambda k, i: (i, k)),
                  pl.BlockSpec((tile, D_MODEL), lambda k, i: (i, 0)),
                  pl.BlockSpec((1, D_MODEL, ff_chunk), lambda k, i: (k, 0, 0)),
                  pl.BlockSpec((1, ff_chunk, D_MODEL), lambda k, i: (k, 0, 0))],
        out_specs=[pl.BlockSpec((1, D_MODEL, ff_chunk), lambda k, i: (k, 0, 0)),
                   pl.BlockSpec((1, ff_chunk, D_MODEL), lambda k, i: (k, 0, 0)),
                   pl.BlockSpec((tile, D_MODEL), lambda k, i: (jnp.where(k == n_k - 1, i, 0), 0))],
        out_shape=[jax.ShapeDtypeStruct((N_DEV, D_MODEL, ff_chunk), F32), jax.ShapeDtypeStruct((N_DEV, ff_chunk, D_MODEL), F32),
                   jax.ShapeDtypeStruct((T, D_MODEL), F32)],
        scratch_shapes=[pltpu.VMEM((T, D_MODEL), F32)],
        compiler_params=_params(("arbitrary", "arbitrary"), VMEM_BIG),
    )(*_in_hbm(dfo, a, h2, w_ff_in, w_ff_out))


def _block_diag_in(b):
    bt = b.reshape(N_SSM_BLOCKS, 8, GROUP_CH, N_STATE)
    eye = jnp.eye(8, dtype=b.dtype)
    return jnp.einsum("jacp,ab->jacbp", bt, eye).reshape(N_SSM_BLOCKS, LANES, SSM_LANE_BLOCK)


def _block_diag_in_grad(g):
    g = g.reshape(N_SSM_BLOCKS, 8, GROUP_CH, 8, N_STATE)
    d = jnp.diagonal(g, axis1=1, axis2=3)
    return jnp.transpose(d, (0, 3, 1, 2)).reshape(N_GROUPS, GROUP_CH, N_STATE)


def _block_diag_out(c):
    ct = c.reshape(N_SSM_BLOCKS, 8, GROUP_CH, N_STATE)
    eye = jnp.eye(8, dtype=c.dtype)
    return jnp.einsum("jacp,ab->japbc", ct, eye).reshape(N_SSM_BLOCKS, SSM_LANE_BLOCK, LANES)


def _block_diag_out_grad(g):
    g = g.reshape(N_SSM_BLOCKS, 8, N_STATE, 8, GROUP_CH)
    d = jnp.diagonal(g, axis1=1, axis2=3)
    return jnp.transpose(d, (0, 3, 2, 1)).reshape(N_GROUPS, GROUP_CH, N_STATE)


def _tiles(T):
    return dict(proj=min(512, T), proj_bwd=min(256, T), merge=min(512, T), merge_bwd=min(256, T),
                mlp_fwd=min(512, T), mlp_bwd=min(512, T), ssm_chunk=min(1024, T))


def _mesh_position():
    x, y, c = lax.axis_index("x"), lax.axis_index("y"), lax.axis_index("c")
    other_chips = [(1 - x, y), (x, 1 - y), (1 - x, 1 - y)]
    return x, y, c, other_chips


def _gather_carry(arrays):
    n = len(arrays)

    def copies(ins, outs, sems):
        send_sems, recv_sems, local_sems = sems
        x, y, c, chips = _mesh_position()
        me, sibling = (x, y, c), (x, y, 1 - c)

        def copy(a, k, block, to, src=None):
            px, py, pc = block
            dst = outs[a].at[4 * px + 2 * py + pc]
            return pltpu.make_async_remote_copy(
                src_ref=dst if src is None else src, dst_ref=dst, send_sem=send_sems.at[7 * a + k],
                recv_sem=recv_sems.at[7 * a + k], device_id=to, device_id_type=MESH_IDS)

        mine = [pltpu.make_async_copy(ins[a], outs[a].at[4 * x + 2 * y + c], local_sems.at[a]) for a in range(n)]
        first = []
        for a in range(n):
            first.append(copy(a, 0, me, sibling, src=ins[a]))
            first += [copy(a, 1 + j, me, (*chip, c), src=ins[a]) for j, chip in enumerate(chips)]
        return copy, mine, first, me, sibling, chips, c

    def start(ins, outs, sems):
        _, mine, first, *_ = copies(ins, outs, sems)
        for cp in mine + first:
            cp.start()

    def finish(ins, outs, sems):
        copy, mine, first, me, sibling, chips, c = copies(ins, outs, sems)
        passed = []
        for a in range(n):
            for j, chip in enumerate(chips):
                copy(a, 1 + j, (*chip, c), me).wait_recv()
                passed.append(copy(a, 4 + j, (*chip, c), sibling))
                passed[-1].start()
        for a in range(n):
            copy(a, 0, sibling, me).wait_recv()
            for j, chip in enumerate(chips):
                copy(a, 4 + j, (*chip, 1 - c), me).wait_recv()
        for cp in first + passed:
            cp.wait_send()
        for cp in mine:
            cp.wait()

    return _Carry(arrays, [jax.ShapeDtypeStruct((N_DEV,) + a.shape, a.dtype) for a in arrays],
                  [pltpu.SemaphoreType.DMA((7 * n,)), pltpu.SemaphoreType.DMA((7 * n,)), pltpu.SemaphoreType.DMA((n,))],
                  start, finish)


def _pairwise_carry(arrays, n_slots, make_copies):
    n = len(arrays)

    def start(ins, outs, sems):
        for cp in make_copies(ins, outs, sems):
            cp.start()

    def finish(ins, outs, sems):
        for cp in make_copies(ins, outs, sems):
            cp.wait()

    return _Carry(arrays, [jax.ShapeDtypeStruct((n_slots,) + a.shape[1:], a.dtype) for a in arrays],
                  [pltpu.SemaphoreType.DMA((n_slots * n,)), pltpu.SemaphoreType.DMA((n_slots * n,))], start, finish)


def _sibling_carry(grads):
    def make_copies(ins, outs, sems):
        x, y, c, _ = _mesh_position()
        return [pltpu.make_async_remote_copy(
            src_ref=ins[a].at[2 * ch + (1 - c)], dst_ref=outs[a].at[ch], send_sem=sems[0].at[4 * a + ch],
            recv_sem=sems[1].at[4 * a + ch], device_id=(x, y, 1 - c), device_id_type=MESH_IDS)
            for a in range(len(grads)) for ch in range(4)]

    return _pairwise_carry(grads, 4, make_copies)


def _chips_carry(sums):
    def make_copies(ins, outs, sems):
        x, y, c, chips = _mesh_position()
        return [pltpu.make_async_remote_copy(
            src_ref=ins[a].at[2 * px + py], dst_ref=outs[a].at[j], send_sem=sems[0].at[3 * a + j],
            recv_sem=sems[1].at[3 * a + j], device_id=(px, py, c), device_id_type=MESH_IDS)
            for a in range(len(sums)) for j, (px, py) in enumerate(chips)]

    return _pairwise_carry(sums, 3, make_copies)


def _row_tile(rows, cols):
    t = max(8, min(rows, (1 << 18) // cols // 8 * 8))
    while rows % t:
        t -= 8
    return t


def _add_sibling(grads8, recv, core, name):
    _, R, C = grads8.shape
    tr = _row_tile(R, C)
    g4 = grads8.reshape(4, 2, R, C)

    def body(core_ref, g_ref, r_ref, o_ref, ob_ref):
        s = g_ref[0] + r_ref[...]
        o_ref[...] = s
        ob_ref[...] = s.astype(BF16)

    out = pl.BlockSpec((1, tr, C), lambda ch, r, core_ref: (ch, r, 0))
    return pl.pallas_call(
        body, name=name,
        grid_spec=pltpu.PrefetchScalarGridSpec(
            num_scalar_prefetch=1, grid=(4, R // tr),
            in_specs=[pl.BlockSpec((1, 1, tr, C), lambda ch, r, core_ref: (ch, core_ref[0], r, 0)),
                      pl.BlockSpec((1, tr, C), lambda ch, r, core_ref: (ch, r, 0))],
            out_specs=[out, out]),
        out_shape=[jax.ShapeDtypeStruct((4, R, C), F32), jax.ShapeDtypeStruct((4, R, C), BF16)],
        compiler_params=_params(("arbitrary", "arbitrary")),
    )(core, *_in_hbm(g4, recv))


def _adam_math(w, g, m, v):
    m = ADAM_B1 * m + (1.0 - ADAM_B1) * g
    v = ADAM_B2 * v + (1.0 - ADAM_B2) * jnp.square(g)
    m_hat = m / (1.0 - ADAM_B1 ** ADAM_STEP)
    v_hat = v / (1.0 - ADAM_B2 ** ADAM_STEP)
    delta = -ADAM_LR * (m_hat / (jnp.sqrt(v_hat) + ADAM_EPS) + ADAM_WD * w)
    return delta, m, v


def _adam_big(w, m, v, chip_sums, recv, chip, name):
    R, C = w.shape
    tr = _row_tile(R, C)

    def body(chip_ref, w_ref, m_ref, v_ref, s_ref, r_ref, g_ref, d_ref, nm_ref, nv_ref):
        g = s_ref[0] + r_ref[0].astype(F32) + r_ref[1].astype(F32) + r_ref[2].astype(F32)
        g_ref[...] = g
        d_ref[...], nm_ref[...], nv_ref[...] = _adam_math(w_ref[...], g, m_ref[...], v_ref[...])

    blk = pl.BlockSpec((tr, C), lambda r, chip_ref: (r, 0))
    return pl.pallas_call(
        body, name=name,
        grid_spec=pltpu.PrefetchScalarGridSpec(
            num_scalar_prefetch=1, grid=(R // tr,),
            in_specs=[blk, blk, blk, pl.BlockSpec((1, tr, C), lambda r, chip_ref: (chip_ref[0], r, 0)),
                      pl.BlockSpec((3, tr, C), lambda r, chip_ref: (0, r, 0))],
            out_specs=[blk] * 4),
        out_shape=[jax.ShapeDtypeStruct((R, C), F32)] * 4,
        compiler_params=_params(("arbitrary",)),
    )(chip, *_in_hbm(w, m, v, chip_sums, recv))


def _sum_partials(partials, name):
    def body(p_ref, g_ref):
        g = p_ref[0]
        for d in range(1, N_DEV):
            g = g + p_ref[d]
        g_ref[...] = g

    return pl.pallas_call(body, name=name, grid=(1,), in_specs=[_whole(partials.shape)], out_specs=_whole(partials.shape[1:]),
                          out_shape=jax.ShapeDtypeStruct(partials.shape[1:], F32))(*_in_hbm(partials))


def _adam_small(ws, ms, vs, gs):
    n = len(ws)

    def body(*refs):
        w_refs, m_refs, v_refs, g_refs = (refs[i * n:(i + 1) * n] for i in range(4))
        d_refs, nm_refs, nv_refs = (refs[(4 + i) * n:(5 + i) * n] for i in range(3))
        for j in range(n):
            d_refs[j][...], nm_refs[j][...], nv_refs[j][...] = _adam_math(
                w_refs[j][...], g_refs[j][...], m_refs[j][...], v_refs[j][...])

    specs = [_whole(w.shape) for w in ws]
    outs = pl.pallas_call(body, name="adam_small", grid=(1,), in_specs=specs * 4, out_specs=specs * 3,
                          out_shape=[jax.ShapeDtypeStruct(w.shape, F32) for w in ws] * 3,
                          compiler_params=_params(("arbitrary",), VMEM_MID))(*_in_hbm(*ws, *ms, *vs, *gs))
    return outs[:n], outs[n:2 * n], outs[2 * n:]


PACK_QUANTUM = SUBLANES * LANES


def _pack(named, names):
    parts = []
    for nme in names:
        flat = named[nme].reshape(-1)
        parts.append(jnp.pad(flat, (0, -flat.size % PACK_QUANTUM)))
    return jnp.concatenate(parts).reshape(-1, LANES)


def _unpack(packed, shapes, names):
    flat = packed.reshape(-1)
    out, pos = {}, 0
    for nme in names:
        size = math.prod(shapes[nme])
        out[nme] = flat[pos:pos + size].reshape(shapes[nme])
        pos += size + (-size % PACK_QUANTUM)
    return out


BIG = ("w_in", "w_glu", "w_attn_branch", "w_ssm_branch", "w_out", "w_ff_in", "w_ff_out")
COLUMN_SHARDED = ("w_in", "w_attn_branch", "w_ssm_branch", "w_ff_in")
SMALL = ("norm_mix_pre", "norm_mix_post", "norm_mlp_pre", "norm_mlp_post", "rel_bias", "sinks", "lam_re", "lam_im",
         "log_dt", "b_re", "b_im", "c_re", "c_im", "d_skip")
SWAPPED_SMALL = ("rel_bias", "b_re", "b_im")
SMALL_LATE = ("norm_mix_pre", "rel_bias", "sinks", "loss")
SMALL_BEFORE_ATTN_BWD = tuple(n for n in SMALL if n not in SMALL_LATE)
ALL_WEIGHTS = ("norm_mix_pre", "norm_mix_post", "norm_mlp_pre", "norm_mlp_post", "w_in", "rel_bias", "sinks", "lam_re",
               "lam_im", "log_dt", "b_re", "b_im", "c_re", "c_im", "d_skip", "w_glu", "w_attn_branch", "w_ssm_branch",
               "w_out", "w_ff_in", "w_ff_out")


def _full_from_gathered(name, gathered):
    _, r, c = gathered.shape
    if name in COLUMN_SHARDED:
        return jnp.transpose(gathered, (1, 0, 2)).reshape(r, N_DEV * c)
    return gathered.reshape(N_DEV * r, c)


def _blocks_from_full(name, full):
    r, c = full.shape
    if name in COLUMN_SHARDED:
        return jnp.transpose(full.reshape(r, N_DEV, c // N_DEV), (1, 0, 2))
    return full.reshape(N_DEV, r // N_DEV, c)


def kernel(x, norm_mix_pre, norm_mix_post, norm_mlp_pre, norm_mlp_post, w_in, rel_bias, sinks, lam_re, lam_im, log_dt, b_re, b_im, c_re, c_im, d_skip, w_glu, w_attn_branch, w_ssm_branch, w_out, w_ff_in, w_ff_out, loss_target, m_norm_mix_pre, m_norm_mix_post, m_norm_mlp_pre, m_norm_mlp_post, m_w_in, m_rel_bias, m_sinks, m_lam_re, m_lam_im, m_log_dt, m_b_re, m_b_im, m_c_re, m_c_im, m_d_skip, m_w_glu, m_w_attn_branch, m_w_ssm_branch, m_w_out, m_w_ff_in, m_w_ff_out, v_norm_mix_pre, v_norm_mix_post, v_norm_mlp_pre, v_norm_mlp_post, v_w_in, v_rel_bias, v_sinks, v_lam_re, v_lam_im, v_log_dt, v_b_re, v_b_im, v_c_re, v_c_im, v_d_skip, v_w_glu, v_w_attn_branch, v_w_ssm_branch, v_w_out, v_w_ff_in, v_w_ff_out):
    args = dict(locals())
    w = {n: args[n] for n in ALL_WEIGHTS}
    m = {n: args["m_" + n] for n in ALL_WEIGHTS}
    v = {n: args["v_" + n] for n in ALL_WEIGHTS}
    core = lax.axis_index("c").astype(jnp.int32).reshape(1)
    chip = (2 * lax.axis_index("x") + lax.axis_index("y")).astype(jnp.int32).reshape(1)
    xs, target = x[0], loss_target[0]
    t = _tiles(xs.shape[0])
    local = lambda d, n: d[n][0].T if n == "w_in" else d[n][0]
    shard = {n: local(w, n).astype(BF16) for n in BIG}
    view = lambda n, a: jnp.swapaxes(a, -1, -2) if n in SWAPPED_SMALL else a
    small = {n: (view(n, w[n]) if n == "rel_bias" else view(n, w[n])[0]) for n in SMALL}
    g1, g2, g3, g4 = (small[n].reshape(1, D_MODEL) for n in ("norm_mix_pre", "norm_mix_post", "norm_mlp_pre", "norm_mlp_post"))
    bucket = jnp.asarray(_bucket_table())
    rel_b, sink = small["rel_bias"], small["sinks"].reshape(1, N_HEADS)
    lam_r, lam_i = small["lam_re"].reshape(1, STATES), small["lam_im"].reshape(1, STATES)
    ldt_rep = jnp.repeat(small["log_dt"].reshape(N_GROUPS), N_STATE).reshape(1, STATES)
    bd_re, bd_im = _block_diag_in(small["b_re"]), _block_diag_in(small["b_im"])
    cm_re, cm_im = _block_diag_out(small["c_re"]).astype(BF16), _block_diag_out(small["c_im"]).astype(BF16)
    dsk = small["d_skip"].reshape(1, SSM_W)

    (g_in,) = _run_carry(_gather_carry([shard["w_in"]]), "gather_w_in")
    wf_in = g_in.reshape(IN_W, D_MODEL)
    merge_names = ("w_glu", "w_attn_branch", "w_ssm_branch", "w_out")
    (q, k, vv, u, ga, gs), gathered = _in_proj_fwd(xs, g1, wf_in, t["proj"], _gather_carry([shard[n] for n in merge_names]))
    wf = {n: _full_from_gathered(n, g) for n, g in zip(merge_names, gathered)}
    (att,), (wf_ff_in,) = _attn_fwd(q, k, vv, bucket, rel_b, sink, _gather_carry([shard["w_ff_in"]]))
    a_re, a_im, bm_re, bm_im = _ssm_prep(lam_r, lam_i, ldt_rep, bd_re, bd_im)
    (y, h_re, h_im, in_re, in_im), (wf_ff_out,) = _ssm_fwd(
        u, a_re, a_im, bm_re, bm_im, cm_re, cm_im, dsk, t["ssm_chunk"], _gather_carry([shard["w_ff_out"]]))
    x1, o, h2 = _merge_fwd(xs, y, att, ga, gs, g2, g3, wf["w_glu"], wf["w_ssm_branch"], wf["w_attn_branch"], wf["w_out"],
                           t["merge"])
    a, dfo, dx2, loss_blk, dg4 = _mlp_fwd(h2, x1, target, g4, wf_ff_in, wf_ff_out, t["mlp_fwd"])

    def add_sibling(names, blocks, received):
        pairs = [_add_sibling(b, r, core, "add_sibling_" + n) for n, b, r in zip(names, blocks, received)]
        return [p[0] for p in pairs], [p[1] for p in pairs]

    ff_names = ("w_ff_in", "w_ff_out")
    dw_ff_in, dw_ff_out, dh2 = _mlp_bwd(dfo, a, h2, wf_ff_in, wf_ff_out, t["mlp_bwd"])
    ff_blocks = [dw_ff_in, dw_ff_out]
    (dx1, dga, dgs, datt, dy, dw_glu, dw_ssm, dw_attn, dw_out, dg2, dg3), ff_recv = _merge_bwd(
        dh2, dx2, x1, o, y, att, ga, gs, g2, g3, wf["w_glu"], wf["w_ssm_branch"], wf["w_attn_branch"], wf["w_out"],
        t["merge_bwd"], _sibling_carry(ff_blocks))
    ff_sums, ff_sums_bf = add_sibling(ff_names, ff_blocks, ff_recv)
    merge_blocks = [_blocks_from_full(n, g) for n, g in zip(merge_names, (dw_glu, dw_attn, dw_ssm, dw_out))]
    (du, dbm_re, dbm_im, dcm_re, dcm_im, da_re, da_im, dd_skip), carried = _ssm_bwd(
        dy, u, h_re, h_im, in_re, in_im, a_re, a_im, bm_re, bm_im, cm_re, cm_im, dsk, t["ssm_chunk"],
        _join(_chips_carry(ff_sums_bf), _sibling_carry(merge_blocks)))
    ff_from_chips, merge_recv = carried[:2], carried[2:]
    merge_sums, merge_sums_bf = add_sibling(merge_names, merge_blocks, merge_recv)
    dbd_re, dbd_im, dlam_re, dlam_im, dldt_rep = _ssm_prep_bwd(lam_r, lam_i, ldt_rep, bd_re, bd_im, dbm_re, dbm_im, da_re, da_im)
    dlog_dt = _group_sum(dldt_rep.reshape(N_GROUPS, N_STATE))
    shapes = {n: view(n, w[n]).shape for n in SMALL}
    shapes["loss"] = (1,)
    small_grads = dict(
        norm_mix_post=dg2, norm_mlp_pre=dg3, norm_mlp_post=dg4, lam_re=dlam_re, lam_im=dlam_im, log_dt=dlog_dt,
        b_re=_block_diag_in_grad(dbd_re), b_im=_block_diag_in_grad(dbd_im),
        c_re=_block_diag_out_grad(dcm_re), c_im=_block_diag_out_grad(dcm_im), d_skip=dd_skip)
    packed_early = _pack({n: small_grads[n].reshape(shapes[n]) for n in SMALL_BEFORE_ATTN_BWD}, SMALL_BEFORE_ATTN_BWD)
    (dq, dk, dv, attn_small), carried = _attn_bwd(
        q, k, vv, datt, bucket, rel_b, sink, _join(_chips_carry(merge_sums_bf), _gather_carry([packed_early])))
    merge_from_chips, partials_early = carried[:-1], carried[-1]
    grad_x, dw_in_t, dg1 = _in_proj_bwd(xs, g1, wf_in, dx1, (dq, dk, dv, du, dga, dgs), t["proj_bwd"])[0]

    late = dict(norm_mix_pre=dg1, rel_bias=attn_small[:, :N_BUCKETS, 0], sinks=attn_small[:, N_BUCKETS, 0],
                loss=loss_blk[0:1, 0])
    packed_late = _pack({n: late[n].reshape(shapes[n]) for n in SMALL_LATE}, SMALL_LATE)
    in_blocks = [dw_in_t.reshape(N_DEV, IN_W // N_DEV, D_MODEL)]
    in_recv = _run_carry(_sibling_carry(in_blocks), "reduce_sibling_w_in")
    in_sums, in_sums_bf = add_sibling(("w_in",), in_blocks, in_recv)
    in_from_chips, partials_late = _run_carry(_join(_chips_carry(in_sums_bf), _gather_carry([packed_late])), "reduce_chips_w_in")

    grads, deltas, new_m, new_v = {}, {}, {}, {}
    sums = dict(zip(ff_names + merge_names + ("w_in",), ff_sums + merge_sums + in_sums))
    received = dict(zip(ff_names + merge_names + ("w_in",), ff_from_chips + merge_from_chips + [in_from_chips]))
    for n in BIG:
        outs = _adam_big(local(w, n), local(m, n), local(v, n), sums[n], received[n], chip, "adam_" + n)
        grads[n], deltas[n], new_m[n], new_v[n] = ((o.T if n == "w_in" else o)[None] for o in outs)

    grads.update(_unpack(_sum_partials(partials_early, "sum_small_grads"), shapes, SMALL_BEFORE_ATTN_BWD))
    grads.update(_unpack(_sum_partials(partials_late, "sum_late_grads"), shapes, SMALL_LATE))
    loss = grads.pop("loss").reshape(())
    small_out = _adam_small(*[[view(n, d[n]) for n in SMALL] for d in (w, m, v)], [grads[n] for n in SMALL])
    for store, vals in zip((deltas, new_m, new_v), small_out):
        store.update(zip(SMALL, vals))
    for store in (grads, deltas, new_m, new_v):
        store.update({n: view(n, store[n]) for n in SWAPPED_SMALL})

    return (loss, grad_x[None], *[grads[n] for n in ALL_WEIGHTS], *[deltas[n] for n in ALL_WEIGHTS],
            *[new_m[n] for n in ALL_WEIGHTS], *[new_v[n] for n in ALL_WEIGHTS])
```

```python
import functools
import math

import jax
import jax.numpy as jnp
import numpy as np
from jax import lax
from jax.experimental import pallas as pl
from jax.experimental.pallas import tpu as pltpu

F32 = jnp.float32
BF16 = jnp.bfloat16

D_MODEL = 1024
N_HEADS = 8
HEAD_DIM = 64
ATTN_W = 512
KV_W = 128
BLOCK = 128
N_BUCKETS = 32
SSM_W = 512
N_GROUPS = 32
N_STATE = 64
GROUP_CH = 16
STATES = N_GROUPS * N_STATE
D_FF = 4096
IN_W = 3328
SPLITS = (0, 512, 640, 768, 1280, 2304, 3328)
RMS_EPS = 1e-6
NEG_INF = -1e30
SUBLANES = 8
LANES = 128
SSM_LANE_BLOCK = 512
N_SSM_BLOCKS = STATES // SSM_LANE_BLOCK
VMEM_BIG = 52 * 1024 * 1024
VMEM_MID = 40 * 1024 * 1024
VMEM_MAX = 60 * 1024 * 1024

ADAM_LR = 0.001
ADAM_B1 = 0.9
ADAM_B2 = 0.999
ADAM_EPS = 1e-08
ADAM_WD = 0.01
ADAM_STEP = 10

N_DEV = 8


def _dot(a, b):
    return jnp.dot(a, b, preferred_element_type=F32)


def _dot_nt(a, b):
    return lax.dot_general(a, b, (((1,), (1,)), ((), ())), preferred_element_type=F32)


def _dot_tn(a, b):
    return lax.dot_general(a, b, (((0,), (0,)), ((), ())), preferred_element_type=F32)


def _rms_scale(x):
    return lax.rsqrt(jnp.mean(x * x, axis=-1, keepdims=True) + RMS_EPS)


def _rms_bwd(dy, x, r, g):
    t = dy * g
    dx = r * t - x * (r * r * r) * jnp.mean(t * x, axis=-1, keepdims=True)
    dg = jnp.sum(dy * x * r, axis=0, keepdims=True)
    return dx, dg


def _const_spec(shape):
    nd = len(shape)
    return pl.BlockSpec(shape, lambda *_: (0,) * nd, pipeline_mode=pl.Buffered(1))


def _in_hbm(*arrays):
    return tuple(pltpu.with_memory_space_constraint(a, pltpu.HBM) for a in arrays)


def _whole(shape):
    nd = len(shape)
    return pl.BlockSpec(shape, lambda *_: (0,) * nd)


def _params(sem, vmem=None):
    return pltpu.CompilerParams(dimension_semantics=sem, vmem_limit_bytes=vmem)


MESH_IDS = pl.DeviceIdType.MESH
HBM_SPEC = pl.BlockSpec(memory_space=pl.ANY)


class _Carry:
    def __init__(self, inputs, out_shapes, sems, start, finish):
        self.inputs, self.out_shapes, self.sems, self.start, self.finish = list(inputs), list(out_shapes), list(sems), start, finish


def _join(a, b):
    na_in, na_out, na_sem = len(a.inputs), len(a.out_shapes), len(a.sems)

    def start(ins, outs, sems):
        a.start(ins[:na_in], outs[:na_out], sems[:na_sem])
        b.start(ins[na_in:], outs[na_out:], sems[na_sem:])

    def finish(ins, outs, sems):
        a.finish(ins[:na_in], outs[:na_out], sems[:na_sem])
        b.finish(ins[na_in:], outs[na_out:], sems[na_sem:])

    return _Carry(a.inputs + b.inputs, a.out_shapes + b.out_shapes, a.sems + b.sems, start, finish)


def _hosted_call(body, carry, edge, *, name, grid, in_specs, out_specs, out_shape, scratch_shapes, compiler_params, inputs):
    n_in, n_out = len(in_specs), len(out_specs)
    inputs = [a if s.memory_space == pltpu.SMEM else _in_hbm(a)[0] for a, s in zip(inputs, in_specs)]
    if carry is None:
        outs = pl.pallas_call(body, name=name, grid=grid, in_specs=in_specs, out_specs=out_specs, out_shape=out_shape,
                              scratch_shapes=scratch_shapes, compiler_params=compiler_params)(*inputs)
        return list(outs), []
    c_in, c_out, c_sem = len(carry.inputs), len(carry.out_shapes), len(carry.sems)

    def wrapped(*refs):
        ins, refs = refs[:n_in], refs[n_in:]
        cins, refs = refs[:c_in], refs[c_in:]
        outs, refs = refs[:n_out], refs[n_out:]
        couts, refs = refs[:c_out], refs[c_out:]
        scratch, csems = refs[:len(refs) - c_sem], refs[len(refs) - c_sem:]
        first, last = edge()

        @pl.when(first)
        def _():
            carry.start(cins, couts, csems)

        body(*ins, *outs, *scratch)

        @pl.when(last)
        def _():
            carry.finish(cins, couts, csems)

    outs = pl.pallas_call(
        wrapped, name=name, grid=grid, in_specs=list(in_specs) + [HBM_SPEC] * c_in,
        out_specs=list(out_specs) + [HBM_SPEC] * c_out, out_shape=list(out_shape) + carry.out_shapes,
        scratch_shapes=list(scratch_shapes) + carry.sems, compiler_params=compiler_params)(*inputs, *_in_hbm(*carry.inputs))
    return list(outs[:n_out]), list(outs[n_out:])


def _edge_1d(n_steps):
    return lambda: (pl.program_id(0) == 0, pl.program_id(0) == n_steps - 1)


def _edge_2d(n0, n1):
    return lambda: ((pl.program_id(0) == 0) & (pl.program_id(1) == 0),
                    (pl.program_id(0) == n0 - 1) & (pl.program_id(1) == n1 - 1))


def _run_carry(carry, name):
    c_in, c_out = len(carry.inputs), len(carry.out_shapes)

    def body(*refs):
        ins, outs, sems = refs[:c_in], refs[c_in:c_in + c_out], refs[c_in + c_out:]
        carry.start(ins, outs, sems)
        carry.finish(ins, outs, sems)

    return pl.pallas_call(body, name=name, in_specs=[HBM_SPEC] * c_in, out_specs=[HBM_SPEC] * c_out,
                          out_shape=carry.out_shapes, scratch_shapes=carry.sems)(*_in_hbm(*carry.inputs))


def _in_proj_fwd(x, g1, w_in_t, tile, carry=None):
    T = x.shape[0]

    def body(x_ref, g_ref, w_ref, q_ref, k_ref, v_ref, u_ref, ga_ref, gs_ref):
        xv = x_ref[...]
        h = (xv * _rms_scale(xv) * g_ref[...]).astype(BF16)
        outs = (q_ref, k_ref, v_ref, u_ref, ga_ref, gs_ref)
        for p, o_ref in enumerate(outs):
            o_ref[...] = _dot_nt(h, w_ref[SPLITS[p]:SPLITS[p + 1], :]).astype(o_ref.dtype)

    widths = [SPLITS[p + 1] - SPLITS[p] for p in range(6)]
    dtypes = [BF16, BF16, BF16, F32, F32, F32]
    return _hosted_call(
        body, carry, _edge_1d(T // tile), name="in_proj_fwd", grid=(T // tile,),
        in_specs=[pl.BlockSpec((tile, D_MODEL), lambda i: (i, 0)), _const_spec((1, D_MODEL)), _const_spec((IN_W, D_MODEL))],
        out_specs=[pl.BlockSpec((tile, w), lambda i: (i, 0)) for w in widths],
        out_shape=[jax.ShapeDtypeStruct((T, w), dt) for w, dt in zip(widths, dtypes)],
        scratch_shapes=[], compiler_params=_params(("arbitrary",), VMEM_MID), inputs=(x, g1, w_in_t))


def _in_proj_bwd(x, g1, w_in_t, dx1, dparts, tile, carry=None):
    T = x.shape[0]
    widths = [SPLITS[p + 1] - SPLITS[p] for p in range(6)]
    n_steps = T // tile

    def body(x_ref, g_ref, w_ref, dx1_ref, dq, dk, dv, du, dga, dgs, gx_ref, dw_hbm, dg_ref, acc_ref):
        i = pl.program_id(0)
        xv = x_ref[...]
        r = _rms_scale(xv)
        g = g_ref[...]
        h = (xv * r * g).astype(BF16)
        dh = jnp.zeros((tile, D_MODEL), F32)
        for p, d_ref in enumerate((dq, dk, dv, du, dga, dgs)):
            dp = d_ref[...]
            rows = slice(SPLITS[p], SPLITS[p + 1])
            dh = dh + _dot(dp, w_ref[rows, :])
            contrib = _dot_tn(dp, h)

            @pl.when(i == 0)
            def _():
                acc_ref[rows, :] = contrib

            @pl.when(i > 0)
            def _():
                acc_ref[rows, :] += contrib

        dxn, dg = _rms_bwd(dh, xv, r, g)
        gx_ref[...] = dx1_ref[...] + dxn

        @pl.when(i == 0)
        def _():
            dg_ref[...] = dg

        @pl.when(i > 0)
        def _():
            dg_ref[...] += dg

        @pl.when(i == n_steps - 1)
        def _():
            pltpu.sync_copy(acc_ref, dw_hbm)

    tok = lambda w: pl.BlockSpec((tile, w), lambda i: (i, 0))
    return _hosted_call(
        body, carry, _edge_1d(n_steps), name="in_proj_bwd", grid=(n_steps,),
        in_specs=[tok(D_MODEL), _const_spec((1, D_MODEL)), _const_spec((IN_W, D_MODEL)), tok(D_MODEL)] + [tok(w) for w in widths],
        out_specs=[tok(D_MODEL), HBM_SPEC, pl.BlockSpec((1, D_MODEL), lambda i: (0, 0))],
        out_shape=[jax.ShapeDtypeStruct((T, D_MODEL), F32), jax.ShapeDtypeStruct((IN_W, D_MODEL), F32),
                   jax.ShapeDtypeStruct((1, D_MODEL), F32)],
        scratch_shapes=[pltpu.VMEM((IN_W, D_MODEL), F32)],
        compiler_params=_params(("arbitrary",), VMEM_BIG), inputs=(x, g1, w_in_t, dx1, *dparts))


def _bucket_table():
    qi = np.arange(BLOCK)[:, None]
    kj = np.arange(2 * BLOCK)[None, :]
    dist = qi + BLOCK - kj
    max_exact = N_BUCKETS // 2
    d = np.maximum(dist, 0)
    df = np.maximum(d, 1).astype(np.float32)
    large = max_exact + (np.log(df / np.float32(max_exact)) / np.float32(math.log(BLOCK / max_exact))
                         * np.float32(N_BUCKETS - max_exact)).astype(np.int32)
    large = np.minimum(large, N_BUCKETS - 1)
    bucket = np.where(d < max_exact, d, large)
    return np.where((dist >= 0) & (dist < BLOCK), bucket, -1).astype(np.int32)


def _build_bias(bucket_ref, rb_ref, bias_ref):
    bk = bucket_ref[...]
    for h in range(N_HEADS):
        def add(b, acc, h=h):
            return acc + jnp.where(bk == b, rb_ref[h, b], 0.0)
        bias_ref[h] = lax.fori_loop(0, N_BUCKETS, add, jnp.zeros((BLOCK, 2 * BLOCK), F32))


def _kv_variants(prev_ref, cur_ref):
    cat = jnp.concatenate([prev_ref[...], cur_ref[...]], axis=0)
    lo = lax.broadcasted_iota(jnp.int32, cat.shape, 1) < HEAD_DIM
    zero = jnp.zeros_like(cat)
    head0_lo = jnp.where(lo, cat, zero)
    head1_hi = jnp.where(lo, zero, cat)
    return ((head0_lo, pltpu.roll(head0_lo, HEAD_DIM, 1)), (pltpu.roll(head1_hi, HEAD_DIM, 1), head1_hi))


def _merge_kv_grads(g):
    lo = lax.broadcasted_iota(jnp.int32, g[0][0].shape, 1) < HEAD_DIM
    return jnp.where(lo, g[0][0] + pltpu.roll(g[0][1], HEAD_DIM, 1), g[1][1] + pltpu.roll(g[1][0], HEAD_DIM, 1))


def _head_lanes(h):
    return slice((h // 2) * LANES, (h // 2 + 1) * LANES)


def _attn_probs(q_ref, kvar, bias_ref, sk_ref, valid, s_ref):
    for h in range(N_HEADS):
        s_ref[h] = _dot_nt(q_ref[:, _head_lanes(h)], kvar[h // 4][h % 2])
    head = lax.broadcasted_iota(jnp.int32, (N_HEADS, 1, 1), 0)
    sink = jnp.zeros((N_HEADS, 1, 1), F32)
    for h in range(N_HEADS):
        sink = jnp.where(head == h, sk_ref[0, h], sink)
    s = jnp.where(valid[None], s_ref[...] * (HEAD_DIM ** -0.5) + bias_ref[...], NEG_INF)
    m = jnp.maximum(jnp.max(s, axis=-1, keepdims=True), sink)
    p = jnp.exp(s - m)
    e_sink = jnp.exp(sink - m)
    inv = 1.0 / (jnp.sum(p, axis=-1, keepdims=True) + e_sink)
    return p * inv, e_sink * inv


def _attn_valid(bucket_ref, n):
    col = lax.broadcasted_iota(jnp.int32, (BLOCK, 2 * BLOCK), 1)
    return (bucket_ref[...] >= 0) & ((n > 0) | (col >= BLOCK))


def _attn_fwd(q, k, v, bucket, rel_bias, sinks, carry=None):
    T = q.shape[0]
    nb = T // BLOCK

    def body(q_ref, kc_ref, kp_ref, vc_ref, vp_ref, bucket_ref, rb_ref, sk_ref, o_ref, bias_ref, s_ref, p_ref):
        n = pl.program_id(0)

        @pl.when(n == 0)
        def _():
            _build_bias(bucket_ref, rb_ref, bias_ref)

        kvar = _kv_variants(kp_ref, kc_ref)
        vvar = _kv_variants(vp_ref, vc_ref)
        pr, _ = _attn_probs(q_ref, kvar, bias_ref, sk_ref, _attn_valid(bucket_ref, n), s_ref)
        p_ref[...] = pr.astype(BF16)
        for m in range(N_HEADS // 2):
            acc = _dot(p_ref[2 * m], vvar[m // 2][0]) + _dot(p_ref[2 * m + 1], vvar[m // 2][1])
            o_ref[:, m * LANES:(m + 1) * LANES] = acc.astype(o_ref.dtype)

    cur = lambda w: pl.BlockSpec((BLOCK, w), lambda n: (n, 0))
    prev = lambda w: pl.BlockSpec((BLOCK, w), lambda n: (jnp.maximum(n - 1, 0), 0))
    smem = pl.BlockSpec(memory_space=pltpu.SMEM)
    return _hosted_call(
        body, carry, _edge_1d(nb), name="attn_fwd", grid=(nb,),
        in_specs=[cur(ATTN_W), cur(KV_W), prev(KV_W), cur(KV_W), prev(KV_W), _const_spec((BLOCK, 2 * BLOCK)), smem, smem],
        out_specs=[cur(ATTN_W)],
        out_shape=[jax.ShapeDtypeStruct((T, ATTN_W), BF16)],
        scratch_shapes=[pltpu.VMEM((N_HEADS, BLOCK, 2 * BLOCK), F32), pltpu.VMEM((N_HEADS, BLOCK, 2 * BLOCK), F32),
                        pltpu.VMEM((N_HEADS, BLOCK, 2 * BLOCK), BF16)],
        compiler_params=_params(("arbitrary",)), inputs=(q, k, k, v, v, bucket, rel_bias, sinks))


ATTN_SMALL_ROWS = N_BUCKETS + SUBLANES


def _attn_bwd(q, k, v, datt, bucket, rel_bias, sinks, carry=None):
    T = q.shape[0]
    nb = T // BLOCK

    def body(q_ref, do_ref, kc_ref, kp_ref, vc_ref, vp_ref, bucket_ref, rb_ref, sk_ref,
             dq_ref, dk_ref, dv_ref, small_ref, bias_ref, ds_sum_ref, dsink_ref, kcarry_ref, vcarry_ref,
             s_ref, dp_ref, p_ref, dsc_ref):
        n = pl.program_id(0)

        @pl.when(n == 0)
        def _():
            _build_bias(bucket_ref, rb_ref, bias_ref)
            ds_sum_ref[...] = jnp.zeros_like(ds_sum_ref)
            dsink_ref[...] = jnp.zeros_like(dsink_ref)
            kcarry_ref[...] = jnp.zeros_like(kcarry_ref)
            vcarry_ref[...] = jnp.zeros_like(vcarry_ref)

        @pl.when(n < nb)
        def _():
            kvar = _kv_variants(kp_ref, kc_ref)
            vvar = _kv_variants(vp_ref, vc_ref)
            pr, p_sink = _attn_probs(q_ref, kvar, bias_ref, sk_ref, _attn_valid(bucket_ref, n), s_ref)
            for h in range(N_HEADS):
                dp_ref[h] = _dot_nt(do_ref[:, _head_lanes(h)], vvar[h // 4][h % 2])
            dp = dp_ref[...]
            dsum = jnp.sum(pr * dp, axis=-1, keepdims=True)
            ds = pr * (dp - dsum)
            ds_sum_ref[...] += ds
            dsink_ref[...] -= jnp.sum(p_sink * dsum, axis=1, keepdims=True)
            dsc_ref[...] = (ds * (HEAD_DIM ** -0.5)).astype(BF16)
            p_ref[...] = pr.astype(BF16)
            for m in range(N_HEADS // 2):
                dqm = _dot(dsc_ref[2 * m], kvar[m // 2][0]) + _dot(dsc_ref[2 * m + 1], kvar[m // 2][1])
                dq_ref[:, m * LANES:(m + 1) * LANES] = dqm.astype(dq_ref.dtype)
            dk_var = [[None, None], [None, None]]
            dv_var = [[None, None], [None, None]]
            for kvh in range(2):
                for e in range(2):
                    heads = [h for h in range(N_HEADS) if h // 4 == kvh and h % 2 == e]
                    dk_var[kvh][e] = sum(_dot_tn(dsc_ref[h], q_ref[:, _head_lanes(h)]) for h in heads)
                    dv_var[kvh][e] = sum(_dot_tn(p_ref[h], do_ref[:, _head_lanes(h)]) for h in heads)
            dk_cat = _merge_kv_grads(dk_var)
            dv_cat = _merge_kv_grads(dv_var)

            @pl.when(n > 0)
            def _():
                dk_ref[...] = (kcarry_ref[...] + dk_cat[:BLOCK]).astype(dk_ref.dtype)
                dv_ref[...] = (vcarry_ref[...] + dv_cat[:BLOCK]).astype(dv_ref.dtype)

            kcarry_ref[...] = dk_cat[BLOCK:]
            vcarry_ref[...] = dv_cat[BLOCK:]

        @pl.when(n == nb)
        def _():
            dk_ref[...] = kcarry_ref[...].astype(dk_ref.dtype)
            dv_ref[...] = vcarry_ref[...].astype(dv_ref.dtype)
            bk = bucket_ref[...]
            row = lax.broadcasted_iota(jnp.int32, (N_HEADS, ATTN_SMALL_ROWS, LANES), 1)

            def add(b, acc):
                masked = jnp.where((bk == b)[None], ds_sum_ref[...], 0.0)
                val = jnp.sum(jnp.sum(masked, axis=1, keepdims=True), axis=2, keepdims=True)
                return acc + jnp.where(row == b, val, 0.0)

            small_ref[...] = lax.fori_loop(0, N_BUCKETS, add, jnp.where(row == N_BUCKETS, dsink_ref[...], 0.0))

    last = nb - 1
    cur = lambda w: pl.BlockSpec((BLOCK, w), lambda n: (jnp.minimum(n, last), 0))
    prev = lambda w: pl.BlockSpec((BLOCK, w), lambda n: (jnp.clip(n - 1, 0, last), 0))
    smem = pl.BlockSpec(memory_space=pltpu.SMEM)
    return _hosted_call(
        body, carry, _edge_1d(nb + 1), name="attn_bwd", grid=(nb + 1,),
        in_specs=[cur(ATTN_W), cur(ATTN_W), cur(KV_W), prev(KV_W), cur(KV_W), prev(KV_W),
                  _const_spec((BLOCK, 2 * BLOCK)), smem, smem],
        out_specs=[cur(ATTN_W), prev(KV_W), prev(KV_W),
                   pl.BlockSpec((N_HEADS, ATTN_SMALL_ROWS, LANES), lambda n: (0, 0, 0))],
        out_shape=[jax.ShapeDtypeStruct((T, ATTN_W), BF16), jax.ShapeDtypeStruct((T, KV_W), BF16),
                   jax.ShapeDtypeStruct((T, KV_W), BF16), jax.ShapeDtypeStruct((N_HEADS, ATTN_SMALL_ROWS, LANES), F32)],
        scratch_shapes=[pltpu.VMEM((N_HEADS, BLOCK, 2 * BLOCK), F32), pltpu.VMEM((N_HEADS, BLOCK, 2 * BLOCK), F32),
                        pltpu.VMEM((N_HEADS, 1, 1), F32), pltpu.VMEM((BLOCK, KV_W), F32), pltpu.VMEM((BLOCK, KV_W), F32),
                        pltpu.VMEM((N_HEADS, BLOCK, 2 * BLOCK), F32), pltpu.VMEM((N_HEADS, BLOCK, 2 * BLOCK), F32),
                        pltpu.VMEM((N_HEADS, BLOCK, 2 * BLOCK), BF16), pltpu.VMEM((N_HEADS, BLOCK, 2 * BLOCK), BF16)],
        compiler_params=_params(("arbitrary",)), inputs=(q, datt, k, k, v, v, bucket, rel_bias, sinks))


SCAN_UNROLL = 4


def _cmul(ar, ai, br, bi):
    return ar * br - ai * bi, ar * bi + ai * br


def _cmul_conj(ar, ai, br, bi):
    return ar * br + ai * bi, ar * bi - ai * br


def _ssm_discretize(lr, li, ldt):
    dt = jnp.exp(ldt)
    mag = jnp.exp(lr * dt)
    ab_re = mag * jnp.cos(li * dt)
    ab_im = mag * jnp.sin(li * dt)
    nr = ab_re - 1.0
    den = lr * lr + li * li
    f_re = (nr * lr + ab_im * li) / den
    f_im = (ab_im * lr - nr * li) / den
    return ab_re, ab_im, f_re, f_im


def _ssm_prep(lam_re, lam_im, ldt_rep, bd_re, bd_im):
    def body(lr_ref, li_ref, ldt_ref, bdr_ref, bdi_ref, ar_ref, ai_ref, br_ref, bi_ref):
        ab_re, ab_im, f_re, f_im = _ssm_discretize(lr_ref[...], li_ref[...], ldt_ref[...])
        ar_ref[...] = ab_re
        ai_ref[...] = ab_im
        bdr, bdi = bdr_ref[0], bdi_ref[0]
        br_ref[0] = (bdr * f_re - bdi * f_im).astype(BF16)
        bi_ref[0] = (bdi * f_re + bdr * f_im).astype(BF16)

    row = pl.BlockSpec((1, SSM_LANE_BLOCK), lambda j: (0, j))
    mat = pl.BlockSpec((1, LANES, SSM_LANE_BLOCK), lambda j: (j, 0, 0))
    return pl.pallas_call(
        body, name="ssm_prep", grid=(N_SSM_BLOCKS,),
        in_specs=[row, row, row, mat, mat], out_specs=[row, row, mat, mat],
        out_shape=[jax.ShapeDtypeStruct((1, STATES), F32)] * 2 + [jax.ShapeDtypeStruct((N_SSM_BLOCKS, LANES, SSM_LANE_BLOCK), BF16)] * 2,
        compiler_params=_params(("arbitrary",)),
    )(*_in_hbm(lam_re, lam_im, ldt_rep, bd_re, bd_im))


def _ssm_prep_bwd(lam_re, lam_im, ldt_rep, bd_re, bd_im, dbr, dbi, da_re, da_im):
    def body(lr_ref, li_ref, ldt_ref, bdr_ref, bdi_ref, dbr_ref, dbi_ref, dar_ref, dai_ref,
             dbdr_ref, dbdi_ref, dlr_ref, dli_ref, dldt_ref):
        lr, li, ldt = lr_ref[...], li_ref[...], ldt_ref[...]
        (_, _, f_re, f_im), vjp = jax.vjp(_ssm_discretize, lr, li, ldt)
        bdr, bdi, gbr, gbi = bdr_ref[0], bdi_ref[0], dbr_ref[0], dbi_ref[0]
        dbdr_ref[0] = gbr * f_re + gbi * f_im
        dbdi_ref[0] = gbi * f_re - gbr * f_im
        df_re = jnp.sum(gbr * bdr + gbi * bdi, axis=0, keepdims=True)
        df_im = jnp.sum(gbi * bdr - gbr * bdi, axis=0, keepdims=True)
        dlr, dli, dldt = vjp((dar_ref[...], dai_ref[...], df_re, df_im))
        dlr_ref[...] = dlr
        dli_ref[...] = dli
        dldt_ref[...] = dldt

    row = pl.BlockSpec((1, SSM_LANE_BLOCK), lambda j: (0, j))
    mat = pl.BlockSpec((1, LANES, SSM_LANE_BLOCK), lambda j: (j, 0, 0))
    mat_shape = jax.ShapeDtypeStruct((N_SSM_BLOCKS, LANES, SSM_LANE_BLOCK), F32)
    row_shape = jax.ShapeDtypeStruct((1, STATES), F32)
    return pl.pallas_call(
        body, name="ssm_prep_bwd", grid=(N_SSM_BLOCKS,),
        in_specs=[row, row, row, mat, mat, mat, mat, row, row], out_specs=[mat, mat, row, row, row],
        out_shape=[mat_shape, mat_shape, row_shape, row_shape, row_shape],
        compiler_params=_params(("arbitrary",)),
    )(*_in_hbm(lam_re, lam_im, ldt_rep, bd_re, bd_im, dbr, dbi, da_re, da_im))


def _group_sum(x):
    def body(x_ref, o_ref):
        o_ref[...] = jnp.sum(x_ref[...], axis=1, keepdims=True)
    return pl.pallas_call(body, name="ssm_group_sum", grid=(1,), in_specs=[_whole(x.shape)], out_specs=_whole((N_GROUPS, 1)),
                          out_shape=jax.ShapeDtypeStruct((N_GROUPS, 1), F32))(*_in_hbm(x))


def _power_table(ar, ai, p_re_ref, p_im_ref, steps):
    shape = (SUBLANES, SSM_LANE_BLOCK)
    p_re_ref[0:SUBLANES] = jnp.broadcast_to(ar, shape)
    p_im_ref[0:SUBLANES] = jnp.broadcast_to(ai, shape)
    m = 1
    while m < steps:
        rows = m * SUBLANES
        top_re = p_re_ref[rows - SUBLANES:rows]
        top_im = p_im_ref[rows - SUBLANES:rows]
        cur_re = p_re_ref[0:rows].reshape(m, SUBLANES, SSM_LANE_BLOCK)
        cur_im = p_im_ref[0:rows].reshape(m, SUBLANES, SSM_LANE_BLOCK)
        nxt_re, nxt_im = _cmul(cur_re, cur_im, top_re[None], top_im[None])
        p_re_ref[rows:2 * rows] = nxt_re.reshape(rows, SSM_LANE_BLOCK)
        p_im_ref[rows:2 * rows] = nxt_im.reshape(rows, SSM_LANE_BLOCK)
        m *= 2


def _to_segments(src_ref, dst_ref, steps):
    for s in range(SUBLANES):
        dst_ref[pl.ds(s, steps, stride=SUBLANES), :] = src_ref[s * steps:(s + 1) * steps, :]


def _from_segments(src_ref, dst_ref, steps):
    for s in range(SUBLANES):
        dst_ref[s * steps:(s + 1) * steps, :] = src_ref[pl.ds(s, steps, stride=SUBLANES), :]


def _segment_carries(e_re, e_im, an_re, an_im, c_re, c_im, reverse):
    order = range(SUBLANES - 1, -1, -1) if reverse else range(SUBLANES)
    ins_re, ins_im = [None] * SUBLANES, [None] * SUBLANES
    for s in order:
        ins_re[s], ins_im[s] = c_re, c_im
        pr, pi = _cmul(an_re, an_im, c_re, c_im)
        c_re = e_re[s:s + 1] + pr
        c_im = e_im[s:s + 1] + pi
    return jnp.concatenate(ins_re, axis=0), jnp.concatenate(ins_im, axis=0), c_re, c_im


def _ssm_fwd(u, a_re, a_im, b_re, b_im, c_re, c_im, d_skip, chunk, carry=None):
    T = u.shape[0]
    nc = T // chunk
    steps = chunk // SUBLANES
    blk = SSM_LANE_BLOCK

    def body(u_ref, ar_ref, ai_ref, br_ref, bi_ref, cr_ref, ci_ref, dk_ref,
             y_ref, hr_ref, hi_ref, inr_ref, ini_ref, useg_ref, yseg_ref, pr_ref, pi_ref, carry_ref):
        c = pl.program_id(1)
        ar, ai = ar_ref[...], ai_ref[...]

        @pl.when(c == 0)
        def _():
            _power_table(ar, ai, pr_ref, pi_ref, steps)
            carry_ref[...] = jnp.zeros_like(carry_ref)

        _to_segments(u_ref, useg_ref, steps)
        ub = useg_ref[...].astype(BF16)
        hr_ref[...] = _dot(ub, br_ref[0])
        hi_ref[...] = _dot(ub, bi_ref[0])
        first = slice(0, SUBLANES)

        def scan(t4, prev):
            for j in range(SCAN_UNROLL):
                rows = pl.ds(pl.multiple_of((t4 * SCAN_UNROLL + j) * SUBLANES, SUBLANES), SUBLANES)
                pr, pi = _cmul(pr_ref[first, :], pi_ref[first, :], prev[0], prev[1])
                prev = (pr + hr_ref[rows, :], pi + hi_ref[rows, :])
                hr_ref[rows, :] = prev[0]
                hi_ref[rows, :] = prev[1]
            return prev

        zero = jnp.zeros((SUBLANES, blk), F32)
        lax.fori_loop(0, steps // SCAN_UNROLL, scan, (zero, zero))

        top = slice(chunk - SUBLANES, chunk)
        in_re, in_im, out_re, out_im = _segment_carries(
            hr_ref[top, :], hi_ref[top, :], pr_ref[top, :][0:1], pi_ref[top, :][0:1],
            carry_ref[0:1, :], carry_ref[1:2, :], reverse=False)
        carry_ref[0:1, :] = out_re
        carry_ref[1:2, :] = out_im
        inr_ref[...] = in_re
        ini_ref[...] = in_im

        def fix(t4, _):
            for j in range(SCAN_UNROLL):
                rows = pl.ds(pl.multiple_of((t4 * SCAN_UNROLL + j) * SUBLANES, SUBLANES), SUBLANES)
                fr, fi = _cmul(pr_ref[rows, :], pi_ref[rows, :], in_re, in_im)
                hr_ref[rows, :] += fr
                hi_ref[rows, :] += fi
            return 0

        lax.fori_loop(0, steps // SCAN_UNROLL, fix, 0)

        yseg_ref[...] = _dot(hr_ref[...].astype(BF16), cr_ref[0]) - _dot(hi_ref[...].astype(BF16), ci_ref[0])
        _from_segments(yseg_ref, y_ref, steps)
        y_ref[...] += dk_ref[...] * u_ref[...]

    row = pl.BlockSpec((1, blk), lambda j, c: (0, j))
    b_mat = pl.BlockSpec((1, LANES, blk), lambda j, c: (j, 0, 0))
    c_mat = pl.BlockSpec((1, blk, LANES), lambda j, c: (j, 0, 0))
    tok = pl.BlockSpec((chunk, LANES), lambda j, c: (c, j))
    state = pl.BlockSpec((chunk, blk), lambda j, c: (c, j))
    enter = pl.BlockSpec((SUBLANES, blk), lambda j, c: (c, j))
    return _hosted_call(
        body, carry, _edge_2d(N_SSM_BLOCKS, nc), name="ssm_fwd", grid=(N_SSM_BLOCKS, nc),
        in_specs=[tok, row, row, b_mat, b_mat, c_mat, c_mat, pl.BlockSpec((1, LANES), lambda j, c: (0, j))],
        out_specs=[tok, state, state, enter, enter],
        out_shape=[jax.ShapeDtypeStruct((T, SSM_W), F32), jax.ShapeDtypeStruct((T, STATES), F32),
                   jax.ShapeDtypeStruct((T, STATES), F32), jax.ShapeDtypeStruct((nc * SUBLANES, STATES), F32),
                   jax.ShapeDtypeStruct((nc * SUBLANES, STATES), F32)],
        scratch_shapes=[pltpu.VMEM((chunk, LANES), F32), pltpu.VMEM((chunk, LANES), F32),
                        pltpu.VMEM((chunk, blk), F32), pltpu.VMEM((chunk, blk), F32), pltpu.VMEM((SUBLANES, blk), F32)],
        compiler_params=_params(("arbitrary", "arbitrary"), VMEM_MID),
        inputs=(u, a_re, a_im, b_re, b_im, c_re, c_im, d_skip))


def _ssm_bwd(dy, u, h_re, h_im, in_re, in_im, a_re, a_im, b_re, b_im, c_re, c_im, d_skip, chunk, carry=None):
    T = u.shape[0]
    nc = T // chunk
    steps = chunk // SUBLANES
    blk = SSM_LANE_BLOCK

    def body(dy_ref, u_ref, hr_ref, hi_ref, inr_ref, ini_ref, ar_ref, ai_ref, br_ref, bi_ref, cr_ref, ci_ref, dk_ref,
             du_ref, dbr_ref, dbi_ref, dcr_ref, dci_ref, dar_ref, dai_ref, ddk_ref,
             dyseg_ref, useg_ref, duseg_ref, gr_ref, gi_ref, pr_ref, pi_ref, carry_ref, accr_ref, acci_ref):
        c = pl.program_id(1)
        ar, ai = ar_ref[...], ai_ref[...]

        @pl.when(c == 0)
        def _():
            _power_table(ar, ai, pr_ref, pi_ref, steps)
            carry_ref[...] = jnp.zeros_like(carry_ref)
            accr_ref[...] = jnp.zeros_like(accr_ref)
            acci_ref[...] = jnp.zeros_like(acci_ref)

        _to_segments(dy_ref, dyseg_ref, steps)
        _to_segments(u_ref, useg_ref, steps)
        dyb = dyseg_ref[...].astype(BF16)
        ub = useg_ref[...].astype(BF16)
        gr_ref[...] = _dot_nt(dyb, cr_ref[0])
        gi_ref[...] = -_dot_nt(dyb, ci_ref[0])
        dcr = _dot_tn(hr_ref[...].astype(BF16), dyb)
        dci = -_dot_tn(hi_ref[...].astype(BF16), dyb)
        ddk = jnp.sum(dy_ref[...] * u_ref[...], axis=0, keepdims=True)

        first = slice(0, SUBLANES)

        def scan(k4, nxt):
            for j in range(SCAN_UNROLL):
                t = steps - 1 - (k4 * SCAN_UNROLL + j)
                rows = pl.ds(pl.multiple_of(t * SUBLANES, SUBLANES), SUBLANES)
                pr, pi = _cmul_conj(pr_ref[first, :], pi_ref[first, :], nxt[0], nxt[1])
                nxt = (pr + gr_ref[rows, :], pi + gi_ref[rows, :])
                gr_ref[rows, :] = nxt[0]
                gi_ref[rows, :] = nxt[1]
            return nxt

        top = slice(chunk - SUBLANES, chunk)
        zero = jnp.zeros((SUBLANES, blk), F32)
        lax.fori_loop(0, steps // SCAN_UNROLL, scan, (zero, zero))

        gin_re, gin_im, out_re, out_im = _segment_carries(
            gr_ref[0:SUBLANES, :], gi_ref[0:SUBLANES, :], pr_ref[top, :][0:1], -pi_ref[top, :][0:1],
            carry_ref[0:1, :], carry_ref[1:2, :], reverse=True)
        carry_ref[0:1, :] = out_re
        carry_ref[1:2, :] = out_im

        def fix_row(rows, prow, hp_re, hp_im, acc):
            fr, fi = _cmul_conj(pr_ref[prow, :], pi_ref[prow, :], gin_re, gin_im)
            g_re = gr_ref[rows, :] + fr
            g_im = gi_ref[rows, :] + fi
            gr_ref[rows, :] = g_re
            gi_ref[rows, :] = g_im
            return acc[0] + g_re * hp_re + g_im * hp_im, acc[1] + g_im * hp_re - g_re * hp_im

        def fix_at(t, acc):
            aligned = (lambda r: r * SUBLANES) if isinstance(t, int) else (lambda r: pl.multiple_of(r * SUBLANES, SUBLANES))
            rows, before, prow = (pl.ds(aligned(r), SUBLANES) for r in (t, t - 1, steps - 1 - t))
            return fix_row(rows, prow, hr_ref[before, :], hi_ref[before, :], acc)

        def fix(t4, acc):
            for j in range(SCAN_UNROLL):
                acc = fix_at(t4 * SCAN_UNROLL + j, acc)
            return acc

        acc = fix_row(first, top, inr_ref[...], ini_ref[...], (accr_ref[...], acci_ref[...]))
        for t in range(1, SCAN_UNROLL):
            acc = fix_at(t, acc)
        acc_re, acc_im = lax.fori_loop(1, steps // SCAN_UNROLL, fix, acc)
        accr_ref[...] = acc_re
        acci_ref[...] = acc_im

        gbr = gr_ref[...].astype(BF16)
        gbi = gi_ref[...].astype(BF16)
        duseg_ref[...] = _dot_nt(gbr, br_ref[0]) + _dot_nt(gbi, bi_ref[0])
        _from_segments(duseg_ref, dyseg_ref, steps)
        du_ref[...] = (dyseg_ref[...] + dk_ref[...] * dy_ref[...]).astype(BF16)
        dbr = _dot_tn(ub, gbr)
        dbi = _dot_tn(ub, gbi)

        @pl.when(c == 0)
        def _():
            dbr_ref[0] = dbr
            dbi_ref[0] = dbi
            dcr_ref[0] = dcr
            dci_ref[0] = dci
            ddk_ref[...] = ddk

        @pl.when(c > 0)
        def _():
            dbr_ref[0] += dbr
            dbi_ref[0] += dbi
            dcr_ref[0] += dcr
            dci_ref[0] += dci
            ddk_ref[...] += ddk

        @pl.when(c == nc - 1)
        def _():
            dar_ref[...] = jnp.sum(acc_re, axis=0, keepdims=True)
            dai_ref[...] = jnp.sum(acc_im, axis=0, keepdims=True)

    rev = lambda c: nc - 1 - c
    row = pl.BlockSpec((1, blk), lambda j, c: (0, j))
    b_mat = pl.BlockSpec((1, LANES, blk), lambda j, c: (j, 0, 0))
    c_mat = pl.BlockSpec((1, blk, LANES), lambda j, c: (j, 0, 0))
    tok = pl.BlockSpec((chunk, LANES), lambda j, c: (rev(c), j))
    state = pl.BlockSpec((chunk, blk), lambda j, c: (rev(c), j))
    enter = pl.BlockSpec((SUBLANES, blk), lambda j, c: (rev(c), j))
    chan = pl.BlockSpec((1, LANES), lambda j, c: (0, j))
    f32 = lambda *s: jax.ShapeDtypeStruct(s, F32)
    return _hosted_call(
        body, carry, _edge_2d(N_SSM_BLOCKS, nc), name="ssm_bwd", grid=(N_SSM_BLOCKS, nc),
        in_specs=[tok, tok, state, state, enter, enter, row, row, b_mat, b_mat, c_mat, c_mat, chan],
        out_specs=[tok, b_mat, b_mat, c_mat, c_mat, row, row, chan],
        out_shape=[jax.ShapeDtypeStruct((T, SSM_W), BF16), f32(N_SSM_BLOCKS, LANES, blk), f32(N_SSM_BLOCKS, LANES, blk),
                   f32(N_SSM_BLOCKS, blk, LANES), f32(N_SSM_BLOCKS, blk, LANES), f32(1, STATES), f32(1, STATES), f32(1, SSM_W)],
        scratch_shapes=[pltpu.VMEM((chunk, LANES), F32), pltpu.VMEM((chunk, LANES), F32), pltpu.VMEM((chunk, LANES), F32),
                        pltpu.VMEM((chunk, blk), F32), pltpu.VMEM((chunk, blk), F32),
                        pltpu.VMEM((chunk, blk), F32), pltpu.VMEM((chunk, blk), F32),
                        pltpu.VMEM((SUBLANES, blk), F32), pltpu.VMEM((SUBLANES, blk), F32), pltpu.VMEM((SUBLANES, blk), F32)],
        compiler_params=_params(("arbitrary", "arbitrary"), VMEM_BIG),
        inputs=(dy, u, h_re, h_im, in_re, in_im, a_re, a_im, b_re, b_im, c_re, c_im, d_skip))


def _merge_forward(y, att, ga, gs, w_glu, w_ssm, w_attn):
    z = jax.nn.gelu(y)
    zb = z.astype(BF16)
    gl = jax.nn.sigmoid(_dot(zb, w_glu))
    z2b = (z * gl).astype(BF16)
    y_ssm = _dot(z2b, w_ssm)
    y_attn = _dot(att, w_attn)
    sa = jax.nn.sigmoid(ga)
    ss = jax.nn.sigmoid(gs)
    merged = (sa * y_attn + ss * y_ssm).astype(BF16)
    return z, zb, gl, z2b, y_ssm, y_attn, sa, ss, merged


def _merge_fwd(x, y, att, ga, gs, g2, g3, w_glu, w_ssm, w_attn, w_out, tile):
    T = x.shape[0]

    def body(x_ref, y_ref, att_ref, ga_ref, gs_ref, g2_ref, g3_ref, wg_ref, ws_ref, wa_ref, wo_ref, x1_ref, o_ref, h2_ref):
        merged = _merge_forward(y_ref[...], att_ref[...], ga_ref[...], gs_ref[...], wg_ref[...], ws_ref[...], wa_ref[...])[-1]
        o = _dot(merged, wo_ref[...])
        x1 = x_ref[...] + o * _rms_scale(o) * g2_ref[...]
        o_ref[...] = o
        x1_ref[...] = x1
        h2_ref[...] = (x1 * _rms_scale(x1) * g3_ref[...]).astype(BF16)

    tok = lambda w: pl.BlockSpec((tile, w), lambda i: (i, 0))
    vec = _const_spec((1, D_MODEL))
    return pl.pallas_call(
        body, name="merge_fwd", grid=(T // tile,),
        in_specs=[tok(D_MODEL), tok(SSM_W), tok(ATTN_W), tok(D_MODEL), tok(D_MODEL), vec, vec,
                  _const_spec((SSM_W, SSM_W)), _const_spec((SSM_W, D_MODEL)), _const_spec((ATTN_W, D_MODEL)),
                  _const_spec((D_MODEL, D_MODEL))],
        out_specs=[tok(D_MODEL), tok(D_MODEL), tok(D_MODEL)],
        out_shape=[jax.ShapeDtypeStruct((T, D_MODEL), F32), jax.ShapeDtypeStruct((T, D_MODEL), F32),
                   jax.ShapeDtypeStruct((T, D_MODEL), BF16)],
        compiler_params=_params(("arbitrary",), VMEM_MID),
    )(*_in_hbm(x, y, att, ga, gs, g2, g3, w_glu, w_ssm, w_attn, w_out))


def _merge_bwd(dh2, dx2, x1, o, y, att, ga, gs, g2, g3, w_glu, w_ssm, w_attn, w_out, tile, carry=None):
    T = x1.shape[0]
    n_steps = T // tile

    def body(dh2_ref, dx2_ref, x1_ref, o_ref, y_ref, att_ref, ga_ref, gs_ref, g2_ref, g3_ref, wg_ref, ws_ref, wa_ref, wo_ref,
             dx1_ref, dga_ref, dgs_ref, datt_ref, dy_ref, dwg_hbm, dws_hbm, dwa_hbm, dwo_hbm, dg2_ref, dg3_ref,
             awg_ref, aws_ref, awa_ref, awo_ref):
        i = pl.program_id(0)
        x1v, ov = x1_ref[...], o_ref[...]
        dxn, dg3 = _rms_bwd(dh2_ref[...], x1v, _rms_scale(x1v), g3_ref[...])
        dx1 = dx2_ref[...] + dxn
        dx1_ref[...] = dx1
        do, dg2 = _rms_bwd(dx1, ov, _rms_scale(ov), g2_ref[...])
        dob = do.astype(BF16)

        yv = y_ref[...]
        att = att_ref[...]
        z, zb, gl, z2b, y_ssm, y_attn, sa, ss, merged = _merge_forward(
            yv, att, ga_ref[...], gs_ref[...], wg_ref[...], ws_ref[...], wa_ref[...])
        dmerged = _dot_nt(dob, wo_ref[...])
        dya = (dmerged * sa).astype(BF16)
        dys = (dmerged * ss).astype(BF16)
        dga_ref[...] = (dmerged * y_attn * sa * (1.0 - sa)).astype(BF16)
        dgs_ref[...] = (dmerged * y_ssm * ss * (1.0 - ss)).astype(BF16)
        datt_ref[...] = _dot_nt(dya, wa_ref[...]).astype(BF16)
        dz2 = _dot_nt(dys, ws_ref[...])
        dpre = (dz2 * z * gl * (1.0 - gl)).astype(BF16)
        dz = dz2 * gl + _dot_nt(dpre, wg_ref[...])
        _, gelu_vjp = jax.vjp(jax.nn.gelu, yv)
        dy_ref[...] = gelu_vjp(dz)[0]

        grads = ((awo_ref, _dot_tn(merged, dob)), (awa_ref, _dot_tn(att, dya)),
                 (aws_ref, _dot_tn(z2b, dys)), (awg_ref, _dot_tn(zb, dpre)), (dg2_ref, dg2), (dg3_ref, dg3))

        @pl.when(i == 0)
        def _():
            for ref, val in grads:
                ref[...] = val

        @pl.when(i > 0)
        def _():
            for ref, val in grads:
                ref[...] += val

        @pl.when(i == n_steps - 1)
        def _():
            pltpu.sync_copy(awg_ref, dwg_hbm)
            pltpu.sync_copy(aws_ref, dws_hbm)
            pltpu.sync_copy(awa_ref, dwa_hbm)
            pltpu.sync_copy(awo_ref, dwo_hbm)

    tok = lambda w: pl.BlockSpec((tile, w), lambda i: (i, 0))
    vec = _const_spec((1, D_MODEL))
    any_ = pl.BlockSpec(memory_space=pl.ANY)
    vec_out = pl.BlockSpec((1, D_MODEL), lambda i: (0, 0))
    f32 = lambda *s: jax.ShapeDtypeStruct(s, F32)
    bf = lambda *s: jax.ShapeDtypeStruct(s, BF16)
    return _hosted_call(
        body, carry, _edge_1d(n_steps), name="merge_bwd", grid=(n_steps,),
        in_specs=[tok(D_MODEL), tok(D_MODEL), tok(D_MODEL), tok(D_MODEL), tok(SSM_W), tok(ATTN_W), tok(D_MODEL), tok(D_MODEL),
                  vec, vec, _const_spec((SSM_W, SSM_W)), _const_spec((SSM_W, D_MODEL)), _const_spec((ATTN_W, D_MODEL)),
                  _const_spec((D_MODEL, D_MODEL))],
        out_specs=[tok(D_MODEL), tok(D_MODEL), tok(D_MODEL), tok(ATTN_W), tok(SSM_W), any_, any_, any_, any_, vec_out, vec_out],
        out_shape=[f32(T, D_MODEL), bf(T, D_MODEL), bf(T, D_MODEL), bf(T, ATTN_W), f32(T, SSM_W),
                   f32(SSM_W, SSM_W), f32(SSM_W, D_MODEL), f32(ATTN_W, D_MODEL), f32(D_MODEL, D_MODEL),
                   f32(1, D_MODEL), f32(1, D_MODEL)],
        scratch_shapes=[pltpu.VMEM((SSM_W, SSM_W), F32), pltpu.VMEM((SSM_W, D_MODEL), F32),
                        pltpu.VMEM((ATTN_W, D_MODEL), F32), pltpu.VMEM((D_MODEL, D_MODEL), F32)],
        compiler_params=_params(("arbitrary",), VMEM_BIG),
        inputs=(dh2, dx2, x1, o, y, att, ga, gs, g2, g3, w_glu, w_ssm, w_attn, w_out))


FF_SHARD = D_FF // N_DEV


def _mlp_fwd(h2, x1, target, g4, w_ff_in, w_ff_out, tile):
    T = h2.shape[0]
    per_step = 2
    ff_chunk = per_step * FF_SHARD
    n_i, n_k = T // tile, N_DEV // per_step

    def body(h2_ref, x1_ref, tg_ref, g4_ref, wi_ref, wo_ref, a_ref, dfo_ref, dx2_ref, loss_ref, dg4_ref, acc_ref):
        i, k = pl.program_id(0), pl.program_id(1)
        h2v = h2_ref[...]
        part = jnp.zeros((tile, D_MODEL), F32)
        for s in range(per_step):
            a = _dot_nt(h2v, wi_ref[s])
            a_ref[:, s * FF_SHARD:(s + 1) * FF_SHARD] = a.astype(BF16)
            ra = jnp.maximum(a, 0.0)
            part = part + _dot((ra * ra).astype(BF16), wo_ref[s])

        @pl.when(k == 0)
        def _():
            acc_ref[...] = part

        @pl.when(k > 0)
        def _():
            acc_ref[...] += part

        @pl.when(k == n_k - 1)
        def _():
            f = acc_ref[...]
            r = _rms_scale(f)
            g = g4_ref[...]
            err = x1_ref[...] + f * r * g - tg_ref[...]
            dx2 = err * (1.0 / D_MODEL)
            dx2_ref[...] = dx2
            dfo, dg = _rms_bwd(dx2, f, r, g)
            dfo_ref[...] = dfo.astype(BF16)
            row = lax.broadcasted_iota(jnp.int32, (8, LANES), 0)
            col = lax.broadcasted_iota(jnp.int32, (8, LANES), 1)
            loss = jnp.where((row == 0) & (col == 0), (0.5 / D_MODEL) * jnp.sum(err * err), 0.0)

            @pl.when(i == 0)
            def _():
                loss_ref[...] = loss
                dg4_ref[...] = dg

            @pl.when(i > 0)
            def _():
                loss_ref[...] += loss
                dg4_ref[...] += dg

    tok = pl.BlockSpec((tile, D_MODEL), lambda i, k: (i, 0))
    return pl.pallas_call(
        body, name="mlp_fwd", grid=(n_i, n_k),
        in_specs=[tok, tok, tok, pl.BlockSpec((1, D_MODEL), lambda i, k: (0, 0)),
                  pl.BlockSpec((per_step, FF_SHARD, D_MODEL), lambda i, k: (k, 0, 0)),
                  pl.BlockSpec((per_step, FF_SHARD, D_MODEL), lambda i, k: (k, 0, 0))],
        out_specs=[pl.BlockSpec((tile, ff_chunk), lambda i, k: (i, k)), tok, tok,
                   pl.BlockSpec((8, LANES), lambda i, k: (0, 0)), pl.BlockSpec((1, D_MODEL), lambda i, k: (0, 0))],
        out_shape=[jax.ShapeDtypeStruct((T, D_FF), BF16), jax.ShapeDtypeStruct((T, D_MODEL), BF16),
                   jax.ShapeDtypeStruct((T, D_MODEL), F32), jax.ShapeDtypeStruct((8, LANES), F32),
                   jax.ShapeDtypeStruct((1, D_MODEL), F32)],
        scratch_shapes=[pltpu.VMEM((tile, D_MODEL), F32)],
        compiler_params=_params(("arbitrary", "arbitrary"), VMEM_BIG),
    )(*_in_hbm(h2, x1, target, g4, w_ff_in, w_ff_out))


def _mlp_weight_grads(dfo, a, h2, w_ff_out, row_chunk):
    T = h2.shape[0]

    def body(dfo_ref, h2_ref, a_ref, wo_ref, dwi_ref, dwo_ref, da_ref, rr_ref):
        def rows(r, _):
            sl = pl.ds(pl.multiple_of(r * row_chunk, row_chunk), row_chunk)
            ra = jnp.maximum(a_ref[sl, :].astype(F32), 0.0)
            da_ref[sl, :] = (_dot_nt(dfo_ref[sl, :], wo_ref[0]) * (2.0 * ra)).astype(BF16)
            rr_ref[sl, :] = (ra * ra).astype(BF16)
            return 0

        lax.fori_loop(0, T // row_chunk, rows, 0)
        dwo_ref[0] = _dot_tn(rr_ref[...], dfo_ref[...])
        dwi_ref[0] = _dot_tn(h2_ref[...], da_ref[...])

    once = lambda shape, index: pl.BlockSpec(shape, index, pipeline_mode=pl.Buffered(1))
    return pl.pallas_call(
        body, name="mlp_weight_grads", grid=(N_DEV,),
        in_specs=[_const_spec((T, D_MODEL)), _const_spec((T, D_MODEL)), once((T, FF_SHARD), lambda k: (0, k)),
                  pl.BlockSpec((1, FF_SHARD, D_MODEL), lambda k: (k, 0, 0))],
        out_specs=[pl.BlockSpec((1, D_MODEL, FF_SHARD), lambda k: (k, 0, 0)),
                   pl.BlockSpec((1, FF_SHARD, D_MODEL), lambda k: (k, 0, 0)), pl.BlockSpec((T, FF_SHARD), lambda k: (0, k))],
        out_shape=[jax.ShapeDtypeStruct((N_DEV, D_MODEL, FF_SHARD), F32), jax.ShapeDtypeStruct((N_DEV, FF_SHARD, D_MODEL), F32),
                   jax.ShapeDtypeStruct((T, D_FF), BF16)],
        scratch_shapes=[pltpu.VMEM((T, FF_SHARD), BF16)],
        compiler_params=_params(("arbitrary",), VMEM_MAX),
    )(*_in_hbm(dfo, h2, a, w_ff_out))


def _mlp_input_grad(da, w_ff_in_t, tile):
    T = da.shape[0]

    def body(da_ref, w_ref, o_ref):
        o_ref[...] = _dot(da_ref[...], w_ref[...])

    return pl.pallas_call(
        body, name="mlp_input_grad", grid=(T // tile,),
        in_specs=[pl.BlockSpec((tile, D_FF), lambda i: (i, 0)), _const_spec((D_FF, D_MODEL))],
        out_specs=pl.BlockSpec((tile, D_MODEL), lambda i: (i, 0)),
        out_shape=jax.ShapeDtypeStruct((T, D_MODEL), F32),
        compiler_params=_params(("arbitrary",), VMEM_MID),
    )(*_in_hbm(da, w_ff_in_t))


def _block_diag_in(b):
    bt = b.reshape(N_SSM_BLOCKS, 8, GROUP_CH, N_STATE)
    eye = jnp.eye(8, dtype=b.dtype)
    return jnp.einsum("jacp,ab->jacbp", bt, eye).reshape(N_SSM_BLOCKS, LANES, SSM_LANE_BLOCK)


def _block_diag_in_grad(g):
    g = g.reshape(N_SSM_BLOCKS, 8, GROUP_CH, 8, N_STATE)
    d = jnp.diagonal(g, axis1=1, axis2=3)
    return jnp.transpose(d, (0, 3, 1, 2)).reshape(N_GROUPS, GROUP_CH, N_STATE)


def _block_diag_out(c):
    ct = c.reshape(N_SSM_BLOCKS, 8, GROUP_CH, N_STATE)
    eye = jnp.eye(8, dtype=c.dtype)
    return jnp.einsum("jacp,ab->japbc", ct, eye).reshape(N_SSM_BLOCKS, SSM_LANE_BLOCK, LANES)


def _block_diag_out_grad(g):
    g = g.reshape(N_SSM_BLOCKS, 8, N_STATE, 8, GROUP_CH)
    d = jnp.diagonal(g, axis1=1, axis2=3)
    return jnp.transpose(d, (0, 3, 2, 1)).reshape(N_GROUPS, GROUP_CH, N_STATE)


def _tiles(T):
    return dict(proj=min(512, T), proj_bwd=min(512, T), merge=min(512, T), merge_bwd=min(256, T),
                mlp_fwd=min(512, T), mlp_bwd=min(512, T), ssm_chunk=min(1024, T))


def _mesh_position():
    x, y, c = lax.axis_index("x"), lax.axis_index("y"), lax.axis_index("c")
    other_chips = [(1 - x, y), (x, 1 - y), (1 - x, 1 - y)]
    return x, y, c, other_chips


def _gather_carry(arrays):
    n = len(arrays)

    def copies(ins, outs, sems):
        send_sems, recv_sems, local_sems = sems
        x, y, c, chips = _mesh_position()
        me, sibling = (x, y, c), (x, y, 1 - c)

        def copy(a, k, block, to, src=None):
            px, py, pc = block
            dst = outs[a].at[4 * px + 2 * py + pc]
            return pltpu.make_async_remote_copy(
                src_ref=dst if src is None else src, dst_ref=dst, send_sem=send_sems.at[7 * a + k],
                recv_sem=recv_sems.at[7 * a + k], device_id=to, device_id_type=MESH_IDS)

        mine = [pltpu.make_async_copy(ins[a], outs[a].at[4 * x + 2 * y + c], local_sems.at[a]) for a in range(n)]
        first = []
        for a in range(n):
            first.append(copy(a, 0, me, sibling, src=ins[a]))
            first += [copy(a, 1 + j, me, (*chip, c), src=ins[a]) for j, chip in enumerate(chips)]
        return copy, mine, first, me, sibling, chips, c

    def start(ins, outs, sems):
        _, mine, first, *_ = copies(ins, outs, sems)
        for cp in mine + first:
            cp.start()

    def finish(ins, outs, sems):
        copy, mine, first, me, sibling, chips, c = copies(ins, outs, sems)
        passed = []
        for a in range(n):
            for j, chip in enumerate(chips):
                copy(a, 1 + j, (*chip, c), me).wait_recv()
                passed.append(copy(a, 4 + j, (*chip, c), sibling))
                passed[-1].start()
        for a in range(n):
            copy(a, 0, sibling, me).wait_recv()
            for j, chip in enumerate(chips):
                copy(a, 4 + j, (*chip, 1 - c), me).wait_recv()
        for cp in first + passed:
            cp.wait_send()
        for cp in mine:
            cp.wait()

    return _Carry(arrays, [jax.ShapeDtypeStruct((N_DEV,) + a.shape, a.dtype) for a in arrays],
                  [pltpu.SemaphoreType.DMA((7 * n,)), pltpu.SemaphoreType.DMA((7 * n,)), pltpu.SemaphoreType.DMA((n,))],
                  start, finish)


def _pairwise_carry(arrays, n_slots, make_copies):
    n = len(arrays)

    def start(ins, outs, sems):
        for cp in make_copies(ins, outs, sems):
            cp.start()

    def finish(ins, outs, sems):
        for cp in make_copies(ins, outs, sems):
            cp.wait()

    return _Carry(arrays, [jax.ShapeDtypeStruct((n_slots,) + a.shape[1:], a.dtype) for a in arrays],
                  [pltpu.SemaphoreType.DMA((n_slots * n,)), pltpu.SemaphoreType.DMA((n_slots * n,))], start, finish)


def _sibling_carry(grads):
    def make_copies(ins, outs, sems):
        x, y, c, _ = _mesh_position()
        return [pltpu.make_async_remote_copy(
            src_ref=ins[a].at[2 * ch + (1 - c)], dst_ref=outs[a].at[ch], send_sem=sems[0].at[4 * a + ch],
            recv_sem=sems[1].at[4 * a + ch], device_id=(x, y, 1 - c), device_id_type=MESH_IDS)
            for a in range(len(grads)) for ch in range(4)]

    return _pairwise_carry(grads, 4, make_copies)


def _chips_carry(sums):
    def make_copies(ins, outs, sems):
        x, y, c, chips = _mesh_position()
        return [pltpu.make_async_remote_copy(
            src_ref=ins[a].at[2 * px + py], dst_ref=outs[a].at[j], send_sem=sems[0].at[3 * a + j],
            recv_sem=sems[1].at[3 * a + j], device_id=(px, py, c), device_id_type=MESH_IDS)
            for a in range(len(sums)) for j, (px, py) in enumerate(chips)]

    return _pairwise_carry(sums, 3, make_copies)


def _row_tile(rows, cols):
    t = max(8, min(rows, (1 << 18) // cols // 8 * 8))
    while rows % t:
        t -= 8
    return t


def _add_sibling(grads8, recv, core, name):
    _, R, C = grads8.shape
    tr = _row_tile(R, C)
    g4 = grads8.reshape(4, 2, R, C)

    def body(core_ref, g_ref, r_ref, o_ref, ob_ref):
        s = g_ref[0] + r_ref[...]
        o_ref[...] = s
        ob_ref[...] = s.astype(BF16)

    out = pl.BlockSpec((1, tr, C), lambda ch, r, core_ref: (ch, r, 0))
    return pl.pallas_call(
        body, name=name,
        grid_spec=pltpu.PrefetchScalarGridSpec(
            num_scalar_prefetch=1, grid=(4, R // tr),
            in_specs=[pl.BlockSpec((1, 1, tr, C), lambda ch, r, core_ref: (ch, core_ref[0], r, 0)),
                      pl.BlockSpec((1, tr, C), lambda ch, r, core_ref: (ch, r, 0))],
            out_specs=[out, out]),
        out_shape=[jax.ShapeDtypeStruct((4, R, C), F32), jax.ShapeDtypeStruct((4, R, C), BF16)],
        compiler_params=_params(("arbitrary", "arbitrary")),
    )(core, *_in_hbm(g4, recv))


def _adam_math(w, g, m, v):
    m = ADAM_B1 * m + (1.0 - ADAM_B1) * g
    v = ADAM_B2 * v + (1.0 - ADAM_B2) * jnp.square(g)
    m_hat = m / (1.0 - ADAM_B1 ** ADAM_STEP)
    v_hat = v / (1.0 - ADAM_B2 ** ADAM_STEP)
    delta = -ADAM_LR * (m_hat / (jnp.sqrt(v_hat) + ADAM_EPS) + ADAM_WD * w)
    return delta, m, v


def _adam_big(w, m, v, chip_sums, recv, chip, name):
    R, C = w.shape
    tr = _row_tile(R, C)

    def body(chip_ref, w_ref, m_ref, v_ref, s_ref, r_ref, g_ref, d_ref, nm_ref, nv_ref):
        g = s_ref[0] + r_ref[0].astype(F32) + r_ref[1].astype(F32) + r_ref[2].astype(F32)
        g_ref[...] = g
        d_ref[...], nm_ref[...], nv_ref[...] = _adam_math(w_ref[...], g, m_ref[...], v_ref[...])

    blk = pl.BlockSpec((tr, C), lambda r, chip_ref: (r, 0))
    return pl.pallas_call(
        body, name=name,
        grid_spec=pltpu.PrefetchScalarGridSpec(
            num_scalar_prefetch=1, grid=(R // tr,),
            in_specs=[blk, blk, blk, pl.BlockSpec((1, tr, C), lambda r, chip_ref: (chip_ref[0], r, 0)),
                      pl.BlockSpec((3, tr, C), lambda r, chip_ref: (0, r, 0))],
            out_specs=[blk] * 4),
        out_shape=[jax.ShapeDtypeStruct((R, C), F32)] * 4,
        compiler_params=_params(("arbitrary",)),
    )(chip, *_in_hbm(w, m, v, chip_sums, recv))


def _sum_partials(partials, name):
    def body(p_ref, g_ref):
        g = p_ref[0]
        for d in range(1, N_DEV):
            g = g + p_ref[d]
        g_ref[...] = g

    return pl.pallas_call(body, name=name, grid=(1,), in_specs=[_whole(partials.shape)], out_specs=_whole(partials.shape[1:]),
                          out_shape=jax.ShapeDtypeStruct(partials.shape[1:], F32))(*_in_hbm(partials))


def _adam_small(ws, ms, vs, gs):
    n = len(ws)

    def body(*refs):
        w_refs, m_refs, v_refs, g_refs = (refs[i * n:(i + 1) * n] for i in range(4))
        d_refs, nm_refs, nv_refs = (refs[(4 + i) * n:(5 + i) * n] for i in range(3))
        for j in range(n):
            d_refs[j][...], nm_refs[j][...], nv_refs[j][...] = _adam_math(
                w_refs[j][...], g_refs[j][...], m_refs[j][...], v_refs[j][...])

    specs = [_whole(w.shape) for w in ws]
    outs = pl.pallas_call(body, name="adam_small", grid=(1,), in_specs=specs * 4, out_specs=specs * 3,
                          out_shape=[jax.ShapeDtypeStruct(w.shape, F32) for w in ws] * 3,
                          compiler_params=_params(("arbitrary",), VMEM_MID))(*_in_hbm(*ws, *ms, *vs, *gs))
    return outs[:n], outs[n:2 * n], outs[2 * n:]


PACK_QUANTUM = SUBLANES * LANES


def _pack(named, names):
    parts = []
    for nme in names:
        flat = named[nme].reshape(-1)
        parts.append(jnp.pad(flat, (0, -flat.size % PACK_QUANTUM)))
    return jnp.concatenate(parts).reshape(-1, LANES)


def _unpack(packed, shapes, names):
    flat = packed.reshape(-1)
    out, pos = {}, 0
    for nme in names:
        size = math.prod(shapes[nme])
        out[nme] = flat[pos:pos + size].reshape(shapes[nme])
        pos += size + (-size % PACK_QUANTUM)
    return out


BIG = ("w_in", "w_glu", "w_attn_branch", "w_ssm_branch", "w_out", "w_ff_in", "w_ff_out")
COLUMN_SHARDED = ("w_in", "w_attn_branch", "w_ssm_branch", "w_ff_in")
SMALL = ("norm_mix_pre", "norm_mix_post", "norm_mlp_pre", "norm_mlp_post", "rel_bias", "sinks", "lam_re", "lam_im",
         "log_dt", "b_re", "b_im", "c_re", "c_im", "d_skip")
SWAPPED_SMALL = ("rel_bias", "b_re", "b_im")
SMALL_LATE = ("norm_mix_pre", "rel_bias", "sinks", "loss")
SMALL_BEFORE_ATTN_BWD = tuple(n for n in SMALL if n not in SMALL_LATE)
ALL_WEIGHTS = ("norm_mix_pre", "norm_mix_post", "norm_mlp_pre", "norm_mlp_post", "w_in", "rel_bias", "sinks", "lam_re",
               "lam_im", "log_dt", "b_re", "b_im", "c_re", "c_im", "d_skip", "w_glu", "w_attn_branch", "w_ssm_branch",
               "w_out", "w_ff_in", "w_ff_out")


def _full_from_gathered(name, gathered):
    _, r, c = gathered.shape
    if name in COLUMN_SHARDED:
        return jnp.transpose(gathered, (1, 0, 2)).reshape(r, N_DEV * c)
    return gathered.reshape(N_DEV * r, c)


def _blocks_from_full(name, full):
    r, c = full.shape
    if name in COLUMN_SHARDED:
        return jnp.transpose(full.reshape(r, N_DEV, c // N_DEV), (1, 0, 2))
    return full.reshape(N_DEV, r // N_DEV, c)


def kernel(x, norm_mix_pre, norm_mix_post, norm_mlp_pre, norm_mlp_post, w_in, rel_bias, sinks, lam_re, lam_im, log_dt, b_re, b_im, c_re, c_im, d_skip, w_glu, w_attn_branch, w_ssm_branch, w_out, w_ff_in, w_ff_out, loss_target, m_norm_mix_pre, m_norm_mix_post, m_norm_mlp_pre, m_norm_mlp_post, m_w_in, m_rel_bias, m_sinks, m_lam_re, m_lam_im, m_log_dt, m_b_re, m_b_im, m_c_re, m_c_im, m_d_skip, m_w_glu, m_w_attn_branch, m_w_ssm_branch, m_w_out, m_w_ff_in, m_w_ff_out, v_norm_mix_pre, v_norm_mix_post, v_norm_mlp_pre, v_norm_mlp_post, v_w_in, v_rel_bias, v_sinks, v_lam_re, v_lam_im, v_log_dt, v_b_re, v_b_im, v_c_re, v_c_im, v_d_skip, v_w_glu, v_w_attn_branch, v_w_ssm_branch, v_w_out, v_w_ff_in, v_w_ff_out):
    args = dict(locals())
    w = {n: args[n] for n in ALL_WEIGHTS}
    m = {n: args["m_" + n] for n in ALL_WEIGHTS}
    v = {n: args["v_" + n] for n in ALL_WEIGHTS}
    core = lax.axis_index("c").astype(jnp.int32).reshape(1)
    chip = (2 * lax.axis_index("x") + lax.axis_index("y")).astype(jnp.int32).reshape(1)
    xs, target = x[0], loss_target[0]
    t = _tiles(xs.shape[0])
    local = lambda d, n: d[n][0].T if n == "w_in" else d[n][0]
    shard = {n: local(w, n).astype(BF16) for n in BIG}
    shard["w_ff_in"] = shard["w_ff_in"].T
    view = lambda n, a: jnp.swapaxes(a, -1, -2) if n in SWAPPED_SMALL else a
    small = {n: (view(n, w[n]) if n == "rel_bias" else view(n, w[n])[0]) for n in SMALL}
    g1, g2, g3, g4 = (small[n].reshape(1, D_MODEL) for n in ("norm_mix_pre", "norm_mix_post", "norm_mlp_pre", "norm_mlp_post"))
    bucket = jnp.asarray(_bucket_table())
    rel_b, sink = small["rel_bias"], small["sinks"].reshape(1, N_HEADS)
    lam_r, lam_i = small["lam_re"].reshape(1, STATES), small["lam_im"].reshape(1, STATES)
    ldt_rep = jnp.repeat(small["log_dt"].reshape(N_GROUPS), N_STATE).reshape(1, STATES)
    bd_re, bd_im = _block_diag_in(small["b_re"]), _block_diag_in(small["b_im"])
    cm_re, cm_im = _block_diag_out(small["c_re"]).astype(BF16), _block_diag_out(small["c_im"]).astype(BF16)
    dsk = small["d_skip"].reshape(1, SSM_W)

    (g_in,) = _run_carry(_gather_carry([shard["w_in"]]), "gather_w_in")
    wf_in = g_in.reshape(IN_W, D_MODEL)
    merge_names = ("w_glu", "w_attn_branch", "w_ssm_branch", "w_out")
    (q, k, vv, u, ga, gs), gathered = _in_proj_fwd(xs, g1, wf_in, t["proj"], _gather_carry([shard[n] for n in merge_names]))
    wf = {n: _full_from_gathered(n, g) for n, g in zip(merge_names, gathered)}
    (att,), (wf_ff_in,) = _attn_fwd(q, k, vv, bucket, rel_b, sink, _gather_carry([shard["w_ff_in"]]))
    a_re, a_im, bm_re, bm_im = _ssm_prep(lam_r, lam_i, ldt_rep, bd_re, bd_im)
    (y, h_re, h_im, in_re, in_im), (wf_ff_out,) = _ssm_fwd(
        u, a_re, a_im, bm_re, bm_im, cm_re, cm_im, dsk, t["ssm_chunk"], _gather_carry([shard["w_ff_out"]]))
    x1, o, h2 = _merge_fwd(xs, y, att, ga, gs, g2, g3, wf["w_glu"], wf["w_ssm_branch"], wf["w_attn_branch"], wf["w_out"],
                           t["merge"])
    a, dfo, dx2, loss_blk, dg4 = _mlp_fwd(h2, x1, target, g4, wf_ff_in, wf_ff_out, t["mlp_fwd"])

    def add_sibling(names, blocks, received):
        pairs = [_add_sibling(b, r, core, "add_sibling_" + n) for n, b, r in zip(names, blocks, received)]
        return [p[0] for p in pairs], [p[1] for p in pairs]

    ff_names = ("w_ff_in", "w_ff_out")
    dw_ff_in, dw_ff_out, da = _mlp_weight_grads(dfo, a, h2, wf_ff_out, t["mlp_bwd"])
    dh2 = _mlp_input_grad(da, wf_ff_in.reshape(D_FF, D_MODEL), t["mlp_bwd"])
    ff_blocks = [dw_ff_in, dw_ff_out]
    (dx1, dga, dgs, datt, dy, dw_glu, dw_ssm, dw_attn, dw_out, dg2, dg3), ff_recv = _merge_bwd(
        dh2, dx2, x1, o, y, att, ga, gs, g2, g3, wf["w_glu"], wf["w_ssm_branch"], wf["w_attn_branch"], wf["w_out"],
        t["merge_bwd"], _sibling_carry(ff_blocks))
    ff_sums, ff_sums_bf = add_sibling(ff_names, ff_blocks, ff_recv)
    merge_blocks = [_blocks_from_full(n, g) for n, g in zip(merge_names, (dw_glu, dw_attn, dw_ssm, dw_out))]
    (du, dbm_re, dbm_im, dcm_re, dcm_im, da_re, da_im, dd_skip), carried = _ssm_bwd(
        dy, u, h_re, h_im, in_re, in_im, a_re, a_im, bm_re, bm_im, cm_re, cm_im, dsk, t["ssm_chunk"],
        _join(_chips_carry(ff_sums_bf), _sibling_carry(merge_blocks)))
    ff_from_chips, merge_recv = carried[:2], carried[2:]
    merge_sums, merge_sums_bf = add_sibling(merge_names, merge_blocks, merge_recv)
    dbd_re, dbd_im, dlam_re, dlam_im, dldt_rep = _ssm_prep_bwd(lam_r, lam_i, ldt_rep, bd_re, bd_im, dbm_re, dbm_im, da_re, da_im)
    dlog_dt = _group_sum(dldt_rep.reshape(N_GROUPS, N_STATE))
    shapes = {n: view(n, w[n]).shape for n in SMALL}
    shapes["loss"] = (1,)
    small_grads = dict(
        norm_mix_post=dg2, norm_mlp_pre=dg3, norm_mlp_post=dg4, lam_re=dlam_re, lam_im=dlam_im, log_dt=dlog_dt,
        b_re=_block_diag_in_grad(dbd_re), b_im=_block_diag_in_grad(dbd_im),
        c_re=_block_diag_out_grad(dcm_re), c_im=_block_diag_out_grad(dcm_im), d_skip=dd_skip)
    packed_early = _pack({n: small_grads[n].reshape(shapes[n]) for n in SMALL_BEFORE_ATTN_BWD}, SMALL_BEFORE_ATTN_BWD)
    (dq, dk, dv, attn_small), carried = _attn_bwd(
        q, k, vv, datt, bucket, rel_b, sink, _join(_chips_carry(merge_sums_bf), _gather_carry([packed_early])))
    merge_from_chips, partials_early = carried[:-1], carried[-1]
    grad_x, dw_in_t, dg1 = _in_proj_bwd(xs, g1, wf_in, dx1, (dq, dk, dv, du, dga, dgs), t["proj_bwd"])[0]

    late = dict(norm_mix_pre=dg1, rel_bias=attn_small[:, :N_BUCKETS, 0], sinks=attn_small[:, N_BUCKETS, 0],
                loss=loss_blk[0:1, 0])
    packed_late = _pack({n: late[n].reshape(shapes[n]) for n in SMALL_LATE}, SMALL_LATE)
    in_blocks = [dw_in_t.reshape(N_DEV, IN_W // N_DEV, D_MODEL)]
    in_recv = _run_carry(_sibling_carry(in_blocks), "reduce_sibling_w_in")
    in_sums, in_sums_bf = add_sibling(("w_in",), in_blocks, in_recv)
    in_from_chips, partials_late = _run_carry(_join(_chips_carry(in_sums_bf), _gather_carry([packed_late])), "reduce_chips_w_in")

    grads, deltas, new_m, new_v = {}, {}, {}, {}
    sums = dict(zip(ff_names + merge_names + ("w_in",), ff_sums + merge_sums + in_sums))
    received = dict(zip(ff_names + merge_names + ("w_in",), ff_from_chips + merge_from_chips + [in_from_chips]))
    for n in BIG:
        outs = _adam_big(local(w, n), local(m, n), local(v, n), sums[n], received[n], chip, "adam_" + n)
        grads[n], deltas[n], new_m[n], new_v[n] = ((o.T if n == "w_in" else o)[None] for o in outs)

    grads.update(_unpack(_sum_partials(partials_early, "sum_small_grads"), shapes, SMALL_BEFORE_ATTN_BWD))
    grads.update(_unpack(_sum_partials(partials_late, "sum_late_grads"), shapes, SMALL_LATE))
    loss = grads.pop("loss").reshape(())
    small_out = _adam_small(*[[view(n, d[n]) for n in SMALL] for d in (w, m, v)], [grads[n] for n in SMALL])
    for store, vals in zip((deltas, new_m, new_v), small_out):
        store.update(zip(SMALL, vals))
    for store in (grads, deltas, new_m, new_v):
        store.update({n: view(n, store[n]) for n in SWAPPED_SMALL})

    return (loss, grad_x[None], *[grads[n] for n in ALL_WEIGHTS], *[deltas[n] for n in ALL_WEIGHTS],
            *[new_m[n] for n in ALL_WEIGHTS], *[new_v[n] for n in ALL_WEIGHTS])
```

```python
import functools
import math

import jax
import jax.numpy as jnp
import numpy as np
from jax import lax
from jax.experimental import pallas as pl
from jax.experimental.pallas import tpu as pltpu

F32 = jnp.float32
BF16 = jnp.bfloat16

D_MODEL = 1024
N_HEADS = 8
HEAD_DIM = 64
ATTN_W = 512
KV_W = 128
BLOCK = 128
N_BUCKETS = 32
SSM_W = 512
N_GROUPS = 32
N_STATE = 64
GROUP_CH = 16
STATES = N_GROUPS * N_STATE
D_FF = 4096
IN_W = 3328
SPLITS = (0, 512, 640, 768, 1280, 2304, 3328)
RMS_EPS = 1e-6
NEG_INF = -1e30
SUBLANES = 8
LANES = 128
SSM_LANE_BLOCK = 512
N_SSM_BLOCKS = STATES // SSM_LANE_BLOCK
VMEM_BIG = 52 * 1024 * 1024
VMEM_MID = 40 * 1024 * 1024
VMEM_MAX = 60 * 1024 * 1024

ADAM_LR = 0.001
ADAM_B1 = 0.9
ADAM_B2 = 0.999
ADAM_EPS = 1e-08
ADAM_WD = 0.01
ADAM_STEP = 10

N_DEV = 8


def _dot(a, b):
    return jnp.dot(a, b, preferred_element_type=F32)


def _dot_nt(a, b):
    return lax.dot_general(a, b, (((1,), (1,)), ((), ())), preferred_element_type=F32)


def _dot_tn(a, b):
    return lax.dot_general(a, b, (((0,), (0,)), ((), ())), preferred_element_type=F32)


def _rms_scale(x):
    return lax.rsqrt(jnp.mean(x * x, axis=-1, keepdims=True) + RMS_EPS)


def _rms_bwd(dy, x, r, g):
    t = dy * g
    dx = r * t - x * (r * r * r) * jnp.mean(t * x, axis=-1, keepdims=True)
    dg = jnp.sum(dy * x * r, axis=0, keepdims=True)
    return dx, dg


def _const_spec(shape):
    nd = len(shape)
    return pl.BlockSpec(shape, lambda *_: (0,) * nd, pipeline_mode=pl.Buffered(1))


def _in_hbm(*arrays):
    return tuple(pltpu.with_memory_space_constraint(a, pltpu.HBM) for a in arrays)


def _whole(shape):
    nd = len(shape)
    return pl.BlockSpec(shape, lambda *_: (0,) * nd)


def _params(sem, vmem=None):
    return pltpu.CompilerParams(dimension_semantics=sem, vmem_limit_bytes=vmem)


MESH_IDS = pl.DeviceIdType.MESH
HBM_SPEC = pl.BlockSpec(memory_space=pl.ANY)


class _Carry:
    def __init__(self, inputs, out_shapes, sems, start, finish):
        self.inputs, self.out_shapes, self.sems, self.start, self.finish = list(inputs), list(out_shapes), list(sems), start, finish


def _join(a, b):
    na_in, na_out, na_sem = len(a.inputs), len(a.out_shapes), len(a.sems)

    def start(ins, outs, sems):
        a.start(ins[:na_in], outs[:na_out], sems[:na_sem])
        b.start(ins[na_in:], outs[na_out:], sems[na_sem:])

    def finish(ins, outs, sems):
        a.finish(ins[:na_in], outs[:na_out], sems[:na_sem])
        b.finish(ins[na_in:], outs[na_out:], sems[na_sem:])

    return _Carry(a.inputs + b.inputs, a.out_shapes + b.out_shapes, a.sems + b.sems, start, finish)


def _hosted_call(body, carry, edge, *, name, grid, in_specs, out_specs, out_shape, scratch_shapes, compiler_params, inputs):
    n_in, n_out = len(in_specs), len(out_specs)
    inputs = [a if s.memory_space == pltpu.SMEM else _in_hbm(a)[0] for a, s in zip(inputs, in_specs)]
    if carry is None:
        outs = pl.pallas_call(body, name=name, grid=grid, in_specs=in_specs, out_specs=out_specs, out_shape=out_shape,
                              scratch_shapes=scratch_shapes, compiler_params=compiler_params)(*inputs)
        return list(outs), []
    c_in, c_out, c_sem = len(carry.inputs), len(carry.out_shapes), len(carry.sems)

    def wrapped(*refs):
        ins, refs = refs[:n_in], refs[n_in:]
        cins, refs = refs[:c_in], refs[c_in:]
        outs, refs = refs[:n_out], refs[n_out:]
        couts, refs = refs[:c_out], refs[c_out:]
        scratch, csems = refs[:len(refs) - c_sem], refs[len(refs) - c_sem:]
        first, last = edge()

        @pl.when(first)
        def _():
            carry.start(cins, couts, csems)

        body(*ins, *outs, *scratch)

        @pl.when(last)
        def _():
            carry.finish(cins, couts, csems)

    outs = pl.pallas_call(
        wrapped, name=name, grid=grid, in_specs=list(in_specs) + [HBM_SPEC] * c_in,
        out_specs=list(out_specs) + [HBM_SPEC] * c_out, out_shape=list(out_shape) + carry.out_shapes,
        scratch_shapes=list(scratch_shapes) + carry.sems, compiler_params=compiler_params)(*inputs, *_in_hbm(*carry.inputs))
    return list(outs[:n_out]), list(outs[n_out:])


def _edge_1d(n_steps):
    return lambda: (pl.program_id(0) == 0, pl.program_id(0) == n_steps - 1)


def _edge_2d(n0, n1):
    return lambda: ((pl.program_id(0) == 0) & (pl.program_id(1) == 0),
                    (pl.program_id(0) == n0 - 1) & (pl.program_id(1) == n1 - 1))


def _run_carry(carry, name):
    c_in, c_out = len(carry.inputs), len(carry.out_shapes)

    def body(*refs):
        ins, outs, sems = refs[:c_in], refs[c_in:c_in + c_out], refs[c_in + c_out:]
        carry.start(ins, outs, sems)
        carry.finish(ins, outs, sems)

    return pl.pallas_call(body, name=name, in_specs=[HBM_SPEC] * c_in, out_specs=[HBM_SPEC] * c_out,
                          out_shape=carry.out_shapes, scratch_shapes=carry.sems)(*_in_hbm(*carry.inputs))


def _in_proj_fwd(x, g1, w_in_t, tile, carry=None):
    T = x.shape[0]

    def body(x_ref, g_ref, w_ref, q_ref, k_ref, v_ref, u_ref, ga_ref, gs_ref):
        xv = x_ref[...]
        h = (xv * _rms_scale(xv) * g_ref[...]).astype(BF16)
        outs = (q_ref, k_ref, v_ref, u_ref, ga_ref, gs_ref)
        for p, o_ref in enumerate(outs):
            o_ref[...] = _dot_nt(h, w_ref[SPLITS[p]:SPLITS[p + 1], :]).astype(o_ref.dtype)

    widths = [SPLITS[p + 1] - SPLITS[p] for p in range(6)]
    dtypes = [BF16, BF16, BF16, F32, F32, F32]
    return _hosted_call(
        body, carry, _edge_1d(T // tile), name="in_proj_fwd", grid=(T // tile,),
        in_specs=[pl.BlockSpec((tile, D_MODEL), lambda i: (i, 0)), _const_spec((1, D_MODEL)), _const_spec((IN_W, D_MODEL))],
        out_specs=[pl.BlockSpec((tile, w), lambda i: (i, 0)) for w in widths],
        out_shape=[jax.ShapeDtypeStruct((T, w), dt) for w, dt in zip(widths, dtypes)],
        scratch_shapes=[], compiler_params=_params(("arbitrary",), VMEM_MID), inputs=(x, g1, w_in_t))


def _in_proj_bwd(x, g1, w_in_t, dx1, dparts, tile, carry=None):
    T = x.shape[0]
    widths = [SPLITS[p + 1] - SPLITS[p] for p in range(6)]
    n_steps = T // tile

    def body(x_ref, g_ref, w_ref, dx1_ref, dq, dk, dv, du, dga, dgs, gx_ref, dw_hbm, dg_ref, acc_ref):
        i = pl.program_id(0)
        xv = x_ref[...]
        r = _rms_scale(xv)
        g = g_ref[...]
        h = (xv * r * g).astype(BF16)
        dh = jnp.zeros((tile, D_MODEL), F32)
        for p, d_ref in enumerate((dq, dk, dv, du, dga, dgs)):
            dp = d_ref[...]
            rows = slice(SPLITS[p], SPLITS[p + 1])
            dh = dh + _dot(dp, w_ref[rows, :])
            contrib = _dot_tn(dp, h)

            @pl.when(i == 0)
            def _():
                acc_ref[rows, :] = contrib

            @pl.when(i > 0)
            def _():
                acc_ref[rows, :] += contrib

        dxn, dg = _rms_bwd(dh, xv, r, g)
        gx_ref[...] = dx1_ref[...] + dxn

        @pl.when(i == 0)
        def _():
            dg_ref[...] = dg

        @pl.when(i > 0)
        def _():
            dg_ref[...] += dg

        @pl.when(i == n_steps - 1)
        def _():
            pltpu.sync_copy(acc_ref, dw_hbm)

    tok = lambda w: pl.BlockSpec((tile, w), lambda i: (i, 0))
    return _hosted_call(
        body, carry, _edge_1d(n_steps), name="in_proj_bwd", grid=(n_steps,),
        in_specs=[tok(D_MODEL), _const_spec((1, D_MODEL)), _const_spec((IN_W, D_MODEL)), tok(D_MODEL)] + [tok(w) for w in widths],
        out_specs=[tok(D_MODEL), HBM_SPEC, pl.BlockSpec((1, D_MODEL), lambda i: (0, 0))],
        out_shape=[jax.ShapeDtypeStruct((T, D_MODEL), F32), jax.ShapeDtypeStruct((IN_W, D_MODEL), F32),
                   jax.ShapeDtypeStruct((1, D_MODEL), F32)],
        scratch_shapes=[pltpu.VMEM((IN_W, D_MODEL), F32)],
        compiler_params=_params(("arbitrary",), VMEM_BIG), inputs=(x, g1, w_in_t, dx1, *dparts))


def _bucket_table():
    qi = np.arange(BLOCK)[:, None]
    kj = np.arange(2 * BLOCK)[None, :]
    dist = qi + BLOCK - kj
    max_exact = N_BUCKETS // 2
    d = np.maximum(dist, 0)
    df = np.maximum(d, 1).astype(np.float32)
    large = max_exact + (np.log(df / np.float32(max_exact)) / np.float32(math.log(BLOCK / max_exact))
                         * np.float32(N_BUCKETS - max_exact)).astype(np.int32)
    large = np.minimum(large, N_BUCKETS - 1)
    bucket = np.where(d < max_exact, d, large)
    return np.where((dist >= 0) & (dist < BLOCK), bucket, -1).astype(np.int32)


def _build_bias(bucket_ref, rb_ref, bias_ref):
    bk = bucket_ref[...]
    for h in range(N_HEADS):
        def add(b, acc, h=h):
            return acc + jnp.where(bk == b, rb_ref[h, b], 0.0)
        bias_ref[h] = lax.fori_loop(0, N_BUCKETS, add, jnp.zeros((BLOCK, 2 * BLOCK), F32))


def _kv_variants(prev_ref, cur_ref):
    cat = jnp.concatenate([prev_ref[...], cur_ref[...]], axis=0)
    lo = lax.broadcasted_iota(jnp.int32, cat.shape, 1) < HEAD_DIM
    zero = jnp.zeros_like(cat)
    head0_lo = jnp.where(lo, cat, zero)
    head1_hi = jnp.where(lo, zero, cat)
    return ((head0_lo, pltpu.roll(head0_lo, HEAD_DIM, 1)), (pltpu.roll(head1_hi, HEAD_DIM, 1), head1_hi))


def _merge_kv_grads(g):
    lo = lax.broadcasted_iota(jnp.int32, g[0][0].shape, 1) < HEAD_DIM
    return jnp.where(lo, g[0][0] + pltpu.roll(g[0][1], HEAD_DIM, 1), g[1][1] + pltpu.roll(g[1][0], HEAD_DIM, 1))


def _head_lanes(h):
    return slice((h // 2) * LANES, (h // 2 + 1) * LANES)


def _attn_probs(q_ref, kvar, bias_ref, sk_ref, valid, s_ref):
    for h in range(N_HEADS):
        s_ref[h] = _dot_nt(q_ref[:, _head_lanes(h)], kvar[h // 4][h % 2])
    head = lax.broadcasted_iota(jnp.int32, (N_HEADS, 1, 1), 0)
    sink = jnp.zeros((N_HEADS, 1, 1), F32)
    for h in range(N_HEADS):
        sink = jnp.where(head == h, sk_ref[0, h], sink)
    s = jnp.where(valid[None], s_ref[...] * (HEAD_DIM ** -0.5) + bias_ref[...], NEG_INF)
    m = jnp.maximum(jnp.max(s, axis=-1, keepdims=True), sink)
    p = jnp.exp(s - m)
    e_sink = jnp.exp(sink - m)
    inv = 1.0 / (jnp.sum(p, axis=-1, keepdims=True) + e_sink)
    return p * inv, e_sink * inv


def _attn_valid(bucket_ref, n):
    col = lax.broadcasted_iota(jnp.int32, (BLOCK, 2 * BLOCK), 1)
    return (bucket_ref[...] >= 0) & ((n > 0) | (col >= BLOCK))


def _attn_fwd(q, k, v, bucket, rel_bias, sinks, carry=None):
    T = q.shape[0]
    nb = T // BLOCK

    def body(q_ref, kc_ref, kp_ref, vc_ref, vp_ref, bucket_ref, rb_ref, sk_ref, o_ref, bias_ref, s_ref, p_ref):
        n = pl.program_id(0)

        @pl.when(n == 0)
        def _():
            _build_bias(bucket_ref, rb_ref, bias_ref)

        kvar = _kv_variants(kp_ref, kc_ref)
        vvar = _kv_variants(vp_ref, vc_ref)
        pr, _ = _attn_probs(q_ref, kvar, bias_ref, sk_ref, _attn_valid(bucket_ref, n), s_ref)
        p_ref[...] = pr.astype(BF16)
        for m in range(N_HEADS // 2):
            acc = _dot(p_ref[2 * m], vvar[m // 2][0]) + _dot(p_ref[2 * m + 1], vvar[m // 2][1])
            o_ref[:, m * LANES:(m + 1) * LANES] = acc.astype(o_ref.dtype)

    cur = lambda w: pl.BlockSpec((BLOCK, w), lambda n: (n, 0))
    prev = lambda w: pl.BlockSpec((BLOCK, w), lambda n: (jnp.maximum(n - 1, 0), 0))
    smem = pl.BlockSpec(memory_space=pltpu.SMEM)
    return _hosted_call(
        body, carry, _edge_1d(nb), name="attn_fwd", grid=(nb,),
        in_specs=[cur(ATTN_W), cur(KV_W), prev(KV_W), cur(KV_W), prev(KV_W), _const_spec((BLOCK, 2 * BLOCK)), smem, smem],
        out_specs=[cur(ATTN_W)],
        out_shape=[jax.ShapeDtypeStruct((T, ATTN_W), BF16)],
        scratch_shapes=[pltpu.VMEM((N_HEADS, BLOCK, 2 * BLOCK), F32), pltpu.VMEM((N_HEADS, BLOCK, 2 * BLOCK), F32),
                        pltpu.VMEM((N_HEADS, BLOCK, 2 * BLOCK), BF16)],
        compiler_params=_params(("arbitrary",)), inputs=(q, k, k, v, v, bucket, rel_bias, sinks))


ATTN_SMALL_ROWS = N_BUCKETS + SUBLANES


def _attn_bwd(q, k, v, datt, bucket, rel_bias, sinks, carry=None):
    T = q.shape[0]
    nb = T // BLOCK

    def body(q_ref, do_ref, kc_ref, kp_ref, vc_ref, vp_ref, bucket_ref, rb_ref, sk_ref,
             dq_ref, dk_ref, dv_ref, small_ref, bias_ref, ds_sum_ref, dsink_ref, kcarry_ref, vcarry_ref,
             s_ref, dp_ref, p_ref, dsc_ref):
        n = pl.program_id(0)

        @pl.when(n == 0)
        def _():
            _build_bias(bucket_ref, rb_ref, bias_ref)
            ds_sum_ref[...] = jnp.zeros_like(ds_sum_ref)
            dsink_ref[...] = jnp.zeros_like(dsink_ref)
            kcarry_ref[...] = jnp.zeros_like(kcarry_ref)
            vcarry_ref[...] = jnp.zeros_like(vcarry_ref)

        @pl.when(n < nb)
        def _():
            kvar = _kv_variants(kp_ref, kc_ref)
            vvar = _kv_variants(vp_ref, vc_ref)
            pr, p_sink = _attn_probs(q_ref, kvar, bias_ref, sk_ref, _attn_valid(bucket_ref, n), s_ref)
            for h in range(N_HEADS):
                dp_ref[h] = _dot_nt(do_ref[:, _head_lanes(h)], vvar[h // 4][h % 2])
            dp = dp_ref[...]
            dsum = jnp.sum(pr * dp, axis=-1, keepdims=True)
            ds = pr * (dp - dsum)
            ds_sum_ref[...] += ds
            dsink_ref[...] -= jnp.sum(p_sink * dsum, axis=1, keepdims=True)
            dsc_ref[...] = (ds * (HEAD_DIM ** -0.5)).astype(BF16)
            p_ref[...] = pr.astype(BF16)
            for m in range(N_HEADS // 2):
                dqm = _dot(dsc_ref[2 * m], kvar[m // 2][0]) + _dot(dsc_ref[2 * m + 1], kvar[m // 2][1])
                dq_ref[:, m * LANES:(m + 1) * LANES] = dqm.astype(dq_ref.dtype)
            dk_var = [[None, None], [None, None]]
            dv_var = [[None, None], [None, None]]
            for kvh in range(2):
                for e in range(2):
                    heads = [h for h in range(N_HEADS) if h // 4 == kvh and h % 2 == e]
                    dk_var[kvh][e] = sum(_dot_tn(dsc_ref[h], q_ref[:, _head_lanes(h)]) for h in heads)
                    dv_var[kvh][e] = sum(_dot_tn(p_ref[h], do_ref[:, _head_lanes(h)]) for h in heads)
            dk_cat = _merge_kv_grads(dk_var)
            dv_cat = _merge_kv_grads(dv_var)

            @pl.when(n > 0)
            def _():
                dk_ref[...] = (kcarry_ref[...] + dk_cat[:BLOCK]).astype(dk_ref.dtype)
                dv_ref[...] = (vcarry_ref[...] + dv_cat[:BLOCK]).astype(dv_ref.dtype)

            kcarry_ref[...] = dk_cat[BLOCK:]
            vcarry_ref[...] = dv_cat[BLOCK:]

        @pl.when(n == nb)
        def _():
            dk_ref[...] = kcarry_ref[...].astype(dk_ref.dtype)
            dv_ref[...] = vcarry_ref[...].astype(dv_ref.dtype)
            bk = bucket_ref[...]
            row = lax.broadcasted_iota(jnp.int32, (N_HEADS, ATTN_SMALL_ROWS, LANES), 1)

            def add(b, acc):
                masked = jnp.where((bk == b)[None], ds_sum_ref[...], 0.0)
                val = jnp.sum(jnp.sum(masked, axis=1, keepdims=True), axis=2, keepdims=True)
                return acc + jnp.where(row == b, val, 0.0)

            small_ref[...] = lax.fori_loop(0, N_BUCKETS, add, jnp.where(row == N_BUCKETS, dsink_ref[...], 0.0))

    last = nb - 1
    cur = lambda w: pl.BlockSpec((BLOCK, w), lambda n: (jnp.minimum(n, last), 0))
    prev = lambda w: pl.BlockSpec((BLOCK, w), lambda n: (jnp.clip(n - 1, 0, last), 0))
    smem = pl.BlockSpec(memory_space=pltpu.SMEM)
    return _hosted_call(
        body, carry, _edge_1d(nb + 1), name="attn_bwd", grid=(nb + 1,),
        in_specs=[cur(ATTN_W), cur(ATTN_W), cur(KV_W), prev(KV_W), cur(KV_W), prev(KV_W),
                  _const_spec((BLOCK, 2 * BLOCK)), smem, smem],
        out_specs=[cur(ATTN_W), prev(KV_W), prev(KV_W),
                   pl.BlockSpec((N_HEADS, ATTN_SMALL_ROWS, LANES), lambda n: (0, 0, 0))],
        out_shape=[jax.ShapeDtypeStruct((T, ATTN_W), BF16), jax.ShapeDtypeStruct((T, KV_W), BF16),
                   jax.ShapeDtypeStruct((T, KV_W), BF16), jax.ShapeDtypeStruct((N_HEADS, ATTN_SMALL_ROWS, LANES), F32)],
        scratch_shapes=[pltpu.VMEM((N_HEADS, BLOCK, 2 * BLOCK), F32), pltpu.VMEM((N_HEADS, BLOCK, 2 * BLOCK), F32),
                        pltpu.VMEM((N_HEADS, 1, 1), F32), pltpu.VMEM((BLOCK, KV_W), F32), pltpu.VMEM((BLOCK, KV_W), F32),
                        pltpu.VMEM((N_HEADS, BLOCK, 2 * BLOCK), F32), pltpu.VMEM((N_HEADS, BLOCK, 2 * BLOCK), F32),
                        pltpu.VMEM((N_HEADS, BLOCK, 2 * BLOCK), BF16), pltpu.VMEM((N_HEADS, BLOCK, 2 * BLOCK), BF16)],
        compiler_params=_params(("arbitrary",)), inputs=(q, datt, k, k, v, v, bucket, rel_bias, sinks))


SCAN_UNROLL = 4


def _cmul(ar, ai, br, bi):
    return ar * br - ai * bi, ar * bi + ai * br


def _cmul_conj(ar, ai, br, bi):
    return ar * br + ai * bi, ar * bi - ai * br


def _ssm_discretize(lr, li, ldt):
    dt = jnp.exp(ldt)
    mag = jnp.exp(lr * dt)
    ab_re = mag * jnp.cos(li * dt)
    ab_im = mag * jnp.sin(li * dt)
    nr = ab_re - 1.0
    den = lr * lr + li * li
    f_re = (nr * lr + ab_im * li) / den
    f_im = (ab_im * lr - nr * li) / den
    return ab_re, ab_im, f_re, f_im


def _ssm_prep(lam_re, lam_im, ldt_rep, bd_re, bd_im):
    def body(lr_ref, li_ref, ldt_ref, bdr_ref, bdi_ref, ar_ref, ai_ref, br_ref, bi_ref):
        ab_re, ab_im, f_re, f_im = _ssm_discretize(lr_ref[...], li_ref[...], ldt_ref[...])
        ar_ref[...] = ab_re
        ai_ref[...] = ab_im
        bdr, bdi = bdr_ref[0], bdi_ref[0]
        br_ref[0] = (bdr * f_re - bdi * f_im).astype(BF16)
        bi_ref[0] = (bdi * f_re + bdr * f_im).astype(BF16)

    row = pl.BlockSpec((1, SSM_LANE_BLOCK), lambda j: (0, j))
    mat = pl.BlockSpec((1, LANES, SSM_LANE_BLOCK), lambda j: (j, 0, 0))
    return pl.pallas_call(
        body, name="ssm_prep", grid=(N_SSM_BLOCKS,),
        in_specs=[row, row, row, mat, mat], out_specs=[row, row, mat, mat],
        out_shape=[jax.ShapeDtypeStruct((1, STATES), F32)] * 2 + [jax.ShapeDtypeStruct((N_SSM_BLOCKS, LANES, SSM_LANE_BLOCK), BF16)] * 2,
        compiler_params=_params(("arbitrary",)),
    )(*_in_hbm(lam_re, lam_im, ldt_rep, bd_re, bd_im))


def _ssm_prep_bwd(lam_re, lam_im, ldt_rep, bd_re, bd_im, dbr, dbi, da_re, da_im):
    def body(lr_ref, li_ref, ldt_ref, bdr_ref, bdi_ref, dbr_ref, dbi_ref, dar_ref, dai_ref,
             dbdr_ref, dbdi_ref, dlr_ref, dli_ref, dldt_ref):
        lr, li, ldt = lr_ref[...], li_ref[...], ldt_ref[...]
        (_, _, f_re, f_im), vjp = jax.vjp(_ssm_discretize, lr, li, ldt)
        bdr, bdi, gbr, gbi = bdr_ref[0], bdi_ref[0], dbr_ref[0], dbi_ref[0]
        dbdr_ref[0] = gbr * f_re + gbi * f_im
        dbdi_ref[0] = gbi * f_re - gbr * f_im
        df_re = jnp.sum(gbr * bdr + gbi * bdi, axis=0, keepdims=True)
        df_im = jnp.sum(gbi * bdr - gbr * bdi, axis=0, keepdims=True)
        dlr, dli, dldt = vjp((dar_ref[...], dai_ref[...], df_re, df_im))
        dlr_ref[...] = dlr
        dli_ref[...] = dli
        dldt_ref[...] = dldt

    row = pl.BlockSpec((1, SSM_LANE_BLOCK), lambda j: (0, j))
    mat = pl.BlockSpec((1, LANES, SSM_LANE_BLOCK), lambda j: (j, 0, 0))
    mat_shape = jax.ShapeDtypeStruct((N_SSM_BLOCKS, LANES, SSM_LANE_BLOCK), F32)
    row_shape = jax.ShapeDtypeStruct((1, STATES), F32)
    return pl.pallas_call(
        body, name="ssm_prep_bwd", grid=(N_SSM_BLOCKS,),
        in_specs=[row, row, row, mat, mat, mat, mat, row, row], out_specs=[mat, mat, row, row, row],
        out_shape=[mat_shape, mat_shape, row_shape, row_shape, row_shape],
        compiler_params=_params(("arbitrary",)),
    )(*_in_hbm(lam_re, lam_im, ldt_rep, bd_re, bd_im, dbr, dbi, da_re, da_im))


def _group_sum(x):
    def body(x_ref, o_ref):
        o_ref[...] = jnp.sum(x_ref[...], axis=1, keepdims=True)
    return pl.pallas_call(body, name="ssm_group_sum", grid=(1,), in_specs=[_whole(x.shape)], out_specs=_whole((N_GROUPS, 1)),
                          out_shape=jax.ShapeDtypeStruct((N_GROUPS, 1), F32))(*_in_hbm(x))


def _power_table(ar, ai, p_re_ref, p_im_ref, steps):
    shape = (SUBLANES, SSM_LANE_BLOCK)
    p_re_ref[0:SUBLANES] = jnp.broadcast_to(ar, shape)
    p_im_ref[0:SUBLANES] = jnp.broadcast_to(ai, shape)
    m = 1
    while m < steps:
        rows = m * SUBLANES
        top_re = p_re_ref[rows - SUBLANES:rows]
        top_im = p_im_ref[rows - SUBLANES:rows]
        cur_re = p_re_ref[0:rows].reshape(m, SUBLANES, SSM_LANE_BLOCK)
        cur_im = p_im_ref[0:rows].reshape(m, SUBLANES, SSM_LANE_BLOCK)
        nxt_re, nxt_im = _cmul(cur_re, cur_im, top_re[None], top_im[None])
        p_re_ref[rows:2 * rows] = nxt_re.reshape(rows, SSM_LANE_BLOCK)
        p_im_ref[rows:2 * rows] = nxt_im.reshape(rows, SSM_LANE_BLOCK)
        m *= 2


def _to_segments(src_ref, dst_ref, steps):
    for s in range(SUBLANES):
        dst_ref[pl.ds(s, steps, stride=SUBLANES), :] = src_ref[s * steps:(s + 1) * steps, :]


def _from_segments(src_ref, dst_ref, steps):
    for s in range(SUBLANES):
        dst_ref[s * steps:(s + 1) * steps, :] = src_ref[pl.ds(s, steps, stride=SUBLANES), :]


def _segment_carries(e_re, e_im, an_re, an_im, c_re, c_im, reverse):
    order = range(SUBLANES - 1, -1, -1) if reverse else range(SUBLANES)
    ins_re, ins_im = [None] * SUBLANES, [None] * SUBLANES
    for s in order:
        ins_re[s], ins_im[s] = c_re, c_im
        pr, pi = _cmul(an_re, an_im, c_re, c_im)
        c_re = e_re[s:s + 1] + pr
        c_im = e_im[s:s + 1] + pi
    return jnp.concatenate(ins_re, axis=0), jnp.concatenate(ins_im, axis=0), c_re, c_im


def _ssm_fwd(u, a_re, a_im, b_re, b_im, c_re, c_im, d_skip, chunk, carry=None):
    T = u.shape[0]
    nc = T // chunk
    steps = chunk // SUBLANES
    blk = SSM_LANE_BLOCK

    def body(u_ref, ar_ref, ai_ref, br_ref, bi_ref, cr_ref, ci_ref, dk_ref,
             y_ref, hr_ref, hi_ref, inr_ref, ini_ref, useg_ref, yseg_ref, pr_ref, pi_ref, carry_ref):
        c = pl.program_id(1)
        ar, ai = ar_ref[...], ai_ref[...]

        @pl.when(c == 0)
        def _():
            _power_table(ar, ai, pr_ref, pi_ref, steps)
            carry_ref[...] = jnp.zeros_like(carry_ref)

        _to_segments(u_ref, useg_ref, steps)
        ub = useg_ref[...].astype(BF16)
        hr_ref[...] = _dot(ub, br_ref[0])
        hi_ref[...] = _dot(ub, bi_ref[0])
        first = slice(0, SUBLANES)

        def scan(t4, prev):
            for j in range(SCAN_UNROLL):
                rows = pl.ds(pl.multiple_of((t4 * SCAN_UNROLL + j) * SUBLANES, SUBLANES), SUBLANES)
                pr, pi = _cmul(pr_ref[first, :], pi_ref[first, :], prev[0], prev[1])
                prev = (pr + hr_ref[rows, :], pi + hi_ref[rows, :])
                hr_ref[rows, :] = prev[0]
                hi_ref[rows, :] = prev[1]
            return prev

        zero = jnp.zeros((SUBLANES, blk), F32)
        lax.fori_loop(0, steps // SCAN_UNROLL, scan, (zero, zero))

        top = slice(chunk - SUBLANES, chunk)
        in_re, in_im, out_re, out_im = _segment_carries(
            hr_ref[top, :], hi_ref[top, :], pr_ref[top, :][0:1], pi_ref[top, :][0:1],
            carry_ref[0:1, :], carry_ref[1:2, :], reverse=False)
        carry_ref[0:1, :] = out_re
        carry_ref[1:2, :] = out_im
        inr_ref[...] = in_re
        ini_ref[...] = in_im

        def fix(t4, _):
            for j in range(SCAN_UNROLL):
                rows = pl.ds(pl.multiple_of((t4 * SCAN_UNROLL + j) * SUBLANES, SUBLANES), SUBLANES)
                fr, fi = _cmul(pr_ref[rows, :], pi_ref[rows, :], in_re, in_im)
                hr_ref[rows, :] += fr
                hi_ref[rows, :] += fi
            return 0

        lax.fori_loop(0, steps // SCAN_UNROLL, fix, 0)

        yseg_ref[...] = _dot(hr_ref[...].astype(BF16), cr_ref[0]) - _dot(hi_ref[...].astype(BF16), ci_ref[0])
        _from_segments(yseg_ref, y_ref, steps)
        y_ref[...] += dk_ref[...] * u_ref[...]

    row = pl.BlockSpec((1, blk), lambda j, c: (0, j))
    b_mat = pl.BlockSpec((1, LANES, blk), lambda j, c: (j, 0, 0))
    c_mat = pl.BlockSpec((1, blk, LANES), lambda j, c: (j, 0, 0))
    tok = pl.BlockSpec((chunk, LANES), lambda j, c: (c, j))
    state = pl.BlockSpec((chunk, blk), lambda j, c: (c, j))
    enter = pl.BlockSpec((SUBLANES, blk), lambda j, c: (c, j))
    return _hosted_call(
        body, carry, _edge_2d(N_SSM_BLOCKS, nc), name="ssm_fwd", grid=(N_SSM_BLOCKS, nc),
        in_specs=[tok, row, row, b_mat, b_mat, c_mat, c_mat, pl.BlockSpec((1, LANES), lambda j, c: (0, j))],
        out_specs=[tok, state, state, enter, enter],
        out_shape=[jax.ShapeDtypeStruct((T, SSM_W), F32), jax.ShapeDtypeStruct((T, STATES), F32),
                   jax.ShapeDtypeStruct((T, STATES), F32), jax.ShapeDtypeStruct((nc * SUBLANES, STATES), F32),
                   jax.ShapeDtypeStruct((nc * SUBLANES, STATES), F32)],
        scratch_shapes=[pltpu.VMEM((chunk, LANES), F32), pltpu.VMEM((chunk, LANES), F32),
                        pltpu.VMEM((chunk, blk), F32), pltpu.VMEM((chunk, blk), F32), pltpu.VMEM((SUBLANES, blk), F32)],
        compiler_params=_params(("arbitrary", "arbitrary"), VMEM_MID),
        inputs=(u, a_re, a_im, b_re, b_im, c_re, c_im, d_skip))


def _ssm_bwd(dy, u, h_re, h_im, in_re, in_im, a_re, a_im, b_re, b_im, c_re, c_im, d_skip, chunk, carry=None):
    T = u.shape[0]
    nc = T // chunk
    steps = chunk // SUBLANES
    blk = SSM_LANE_BLOCK

    def body(dy_ref, u_ref, hr_ref, hi_ref, inr_ref, ini_ref, ar_ref, ai_ref, br_ref, bi_ref, cr_ref, ci_ref, dk_ref,
             du_ref, dbr_ref, dbi_ref, dcr_ref, dci_ref, dar_ref, dai_ref, ddk_ref,
             dyseg_ref, useg_ref, duseg_ref, gr_ref, gi_ref, pr_ref, pi_ref, carry_ref, accr_ref, acci_ref):
        c = pl.program_id(1)
        ar, ai = ar_ref[...], ai_ref[...]

        @pl.when(c == 0)
        def _():
            _power_table(ar, ai, pr_ref, pi_ref, steps)
            carry_ref[...] = jnp.zeros_like(carry_ref)
            accr_ref[...] = jnp.zeros_like(accr_ref)
            acci_ref[...] = jnp.zeros_like(acci_ref)

        _to_segments(dy_ref, dyseg_ref, steps)
        _to_segments(u_ref, useg_ref, steps)
        dyb = dyseg_ref[...].astype(BF16)
        ub = useg_ref[...].astype(BF16)
        gr_ref[...] = _dot_nt(dyb, cr_ref[0])
        gi_ref[...] = -_dot_nt(dyb, ci_ref[0])
        dcr = _dot_tn(hr_ref[...].astype(BF16), dyb)
        dci = -_dot_tn(hi_ref[...].astype(BF16), dyb)
        ddk = jnp.sum(dy_ref[...] * u_ref[...], axis=0, keepdims=True)

        first = slice(0, SUBLANES)

        def scan(k4, nxt):
            for j in range(SCAN_UNROLL):
                t = steps - 1 - (k4 * SCAN_UNROLL + j)
                rows = pl.ds(pl.multiple_of(t * SUBLANES, SUBLANES), SUBLANES)
                pr, pi = _cmul_conj(pr_ref[first, :], pi_ref[first, :], nxt[0], nxt[1])
                nxt = (pr + gr_ref[rows, :], pi + gi_ref[rows, :])
                gr_ref[rows, :] = nxt[0]
                gi_ref[rows, :] = nxt[1]
            return nxt

        top = slice(chunk - SUBLANES, chunk)
        zero = jnp.zeros((SUBLANES, blk), F32)
        lax.fori_loop(0, steps // SCAN_UNROLL, scan, (zero, zero))

        gin_re, gin_im, out_re, out_im = _segment_carries(
            gr_ref[0:SUBLANES, :], gi_ref[0:SUBLANES, :], pr_ref[top, :][0:1], -pi_ref[top, :][0:1],
            carry_ref[0:1, :], carry_ref[1:2, :], reverse=True)
        carry_ref[0:1, :] = out_re
        carry_ref[1:2, :] = out_im

        def fix_row(rows, prow, hp_re, hp_im, acc):
            fr, fi = _cmul_conj(pr_ref[prow, :], pi_ref[prow, :], gin_re, gin_im)
            g_re = gr_ref[rows, :] + fr
            g_im = gi_ref[rows, :] + fi
            gr_ref[rows, :] = g_re
            gi_ref[rows, :] = g_im
            return acc[0] + g_re * hp_re + g_im * hp_im, acc[1] + g_im * hp_re - g_re * hp_im

        def fix_at(t, acc):
            aligned = (lambda r: r * SUBLANES) if isinstance(t, int) else (lambda r: pl.multiple_of(r * SUBLANES, SUBLANES))
            rows, before, prow = (pl.ds(aligned(r), SUBLANES) for r in (t, t - 1, steps - 1 - t))
            return fix_row(rows, prow, hr_ref[before, :], hi_ref[before, :], acc)

        def fix(t4, acc):
            for j in range(SCAN_UNROLL):
                acc = fix_at(t4 * SCAN_UNROLL + j, acc)
            return acc

        acc = fix_row(first, top, inr_ref[...], ini_ref[...], (accr_ref[...], acci_ref[...]))
        for t in range(1, SCAN_UNROLL):
            acc = fix_at(t, acc)
        acc_re, acc_im = lax.fori_loop(1, steps // SCAN_UNROLL, fix, acc)
        accr_ref[...] = acc_re
        acci_ref[...] = acc_im

        gbr = gr_ref[...].astype(BF16)
        gbi = gi_ref[...].astype(BF16)
        duseg_ref[...] = _dot_nt(gbr, br_ref[0]) + _dot_nt(gbi, bi_ref[0])
        _from_segments(duseg_ref, dyseg_ref, steps)
        du_ref[...] = (dyseg_ref[...] + dk_ref[...] * dy_ref[...]).astype(BF16)
        dbr = _dot_tn(ub, gbr)
        dbi = _dot_tn(ub, gbi)

        @pl.when(c == 0)
        def _():
            dbr_ref[0] = dbr
            dbi_ref[0] = dbi
            dcr_ref[0] = dcr
            dci_ref[0] = dci
            ddk_ref[...] = ddk

        @pl.when(c > 0)
        def _():
            dbr_ref[0] += dbr
            dbi_ref[0] += dbi
            dcr_ref[0] += dcr
            dci_ref[0] += dci
            ddk_ref[...] += ddk

        @pl.when(c == nc - 1)
        def _():
            dar_ref[...] = jnp.sum(acc_re, axis=0, keepdims=True)
            dai_ref[...] = jnp.sum(acc_im, axis=0, keepdims=True)

    rev = lambda c: nc - 1 - c
    row = pl.BlockSpec((1, blk), lambda j, c: (0, j))
    b_mat = pl.BlockSpec((1, LANES, blk), lambda j, c: (j, 0, 0))
    c_mat = pl.BlockSpec((1, blk, LANES), lambda j, c: (j, 0, 0))
    tok = pl.BlockSpec((chunk, LANES), lambda j, c: (rev(c), j))
    state = pl.BlockSpec((chunk, blk), lambda j, c: (rev(c), j))
    enter = pl.BlockSpec((SUBLANES, blk), lambda j, c: (rev(c), j))
    chan = pl.BlockSpec((1, LANES), lambda j, c: (0, j))
    f32 = lambda *s: jax.ShapeDtypeStruct(s, F32)
    return _hosted_call(
        body, carry, _edge_2d(N_SSM_BLOCKS, nc), name="ssm_bwd", grid=(N_SSM_BLOCKS, nc),
        in_specs=[tok, tok, state, state, enter, enter, row, row, b_mat, b_mat, c_mat, c_mat, chan],
        out_specs=[tok, b_mat, b_mat, c_mat, c_mat, row, row, chan],
        out_shape=[jax.ShapeDtypeStruct((T, SSM_W), BF16), f32(N_SSM_BLOCKS, LANES, blk), f32(N_SSM_BLOCKS, LANES, blk),
                   f32(N_SSM_BLOCKS, blk, LANES), f32(N_SSM_BLOCKS, blk, LANES), f32(1, STATES), f32(1, STATES), f32(1, SSM_W)],
        scratch_shapes=[pltpu.VMEM((chunk, LANES), F32), pltpu.VMEM((chunk, LANES), F32), pltpu.VMEM((chunk, LANES), F32),
                        pltpu.VMEM((chunk, blk), F32), pltpu.VMEM((chunk, blk), F32),
                        pltpu.VMEM((chunk, blk), F32), pltpu.VMEM((chunk, blk), F32),
                        pltpu.VMEM((SUBLANES, blk), F32), pltpu.VMEM((SUBLANES, blk), F32), pltpu.VMEM((SUBLANES, blk), F32)],
        compiler_params=_params(("arbitrary", "arbitrary"), VMEM_BIG),
        inputs=(dy, u, h_re, h_im, in_re, in_im, a_re, a_im, b_re, b_im, c_re, c_im, d_skip))


def _merge_forward(y, att, ga, gs, w_glu, w_ssm, w_attn):
    z = jax.nn.gelu(y)
    zb = z.astype(BF16)
    gl = jax.nn.sigmoid(_dot(zb, w_glu))
    z2b = (z * gl).astype(BF16)
    y_ssm = _dot(z2b, w_ssm)
    y_attn = _dot(att, w_attn)
    sa = jax.nn.sigmoid(ga)
    ss = jax.nn.sigmoid(gs)
    merged = (sa * y_attn + ss * y_ssm).astype(BF16)
    return z, zb, gl, z2b, y_ssm, y_attn, sa, ss, merged


def _merge_fwd(x, y, att, ga, gs, g2, g3, w_glu, w_ssm, w_attn, w_out, tile):
    T = x.shape[0]

    def body(x_ref, y_ref, att_ref, ga_ref, gs_ref, g2_ref, g3_ref, wg_ref, ws_ref, wa_ref, wo_ref, x1_ref, o_ref, h2_ref):
        merged = _merge_forward(y_ref[...], att_ref[...], ga_ref[...], gs_ref[...], wg_ref[...], ws_ref[...], wa_ref[...])[-1]
        o = _dot(merged, wo_ref[...])
        x1 = x_ref[...] + o * _rms_scale(o) * g2_ref[...]
        o_ref[...] = o
        x1_ref[...] = x1
        h2_ref[...] = (x1 * _rms_scale(x1) * g3_ref[...]).astype(BF16)

    tok = lambda w: pl.BlockSpec((tile, w), lambda i: (i, 0))
    vec = _const_spec((1, D_MODEL))
    return pl.pallas_call(
        body, name="merge_fwd", grid=(T // tile,),
        in_specs=[tok(D_MODEL), tok(SSM_W), tok(ATTN_W), tok(D_MODEL), tok(D_MODEL), vec, vec,
                  _const_spec((SSM_W, SSM_W)), _const_spec((SSM_W, D_MODEL)), _const_spec((ATTN_W, D_MODEL)),
                  _const_spec((D_MODEL, D_MODEL))],
        out_specs=[tok(D_MODEL), tok(D_MODEL), tok(D_MODEL)],
        out_shape=[jax.ShapeDtypeStruct((T, D_MODEL), F32), jax.ShapeDtypeStruct((T, D_MODEL), F32),
                   jax.ShapeDtypeStruct((T, D_MODEL), BF16)],
        compiler_params=_params(("arbitrary",), VMEM_MID),
    )(*_in_hbm(x, y, att, ga, gs, g2, g3, w_glu, w_ssm, w_attn, w_out))


def _merge_bwd(dh2, dx2, x1, o, y, att, ga, gs, g2, g3, w_glu, w_ssm, w_attn, w_out, tile, carry=None):
    T = x1.shape[0]
    n_steps = T // tile

    def body(dh2_ref, dx2_ref, x1_ref, o_ref, y_ref, att_ref, ga_ref, gs_ref, g2_ref, g3_ref, wg_ref, ws_ref, wa_ref, wo_ref,
             dx1_ref, dga_ref, dgs_ref, datt_ref, dy_ref, dwg_hbm, dws_hbm, dwa_hbm, dwo_hbm, dg2_ref, dg3_ref,
             awg_ref, aws_ref, awa_ref, awo_ref):
        i = pl.program_id(0)
        x1v, ov = x1_ref[...], o_ref[...]
        dxn, dg3 = _rms_bwd(dh2_ref[...], x1v, _rms_scale(x1v), g3_ref[...])
        dx1 = dx2_ref[...] + dxn
        dx1_ref[...] = dx1
        do, dg2 = _rms_bwd(dx1, ov, _rms_scale(ov), g2_ref[...])
        dob = do.astype(BF16)

        yv = y_ref[...]
        att = att_ref[...]
        z, zb, gl, z2b, y_ssm, y_attn, sa, ss, merged = _merge_forward(
            yv, att, ga_ref[...], gs_ref[...], wg_ref[...], ws_ref[...], wa_ref[...])
        dmerged = _dot_nt(dob, wo_ref[...])
        dya = (dmerged * sa).astype(BF16)
        dys = (dmerged * ss).astype(BF16)
        dga_ref[...] = (dmerged * y_attn * sa * (1.0 - sa)).astype(BF16)
        dgs_ref[...] = (dmerged * y_ssm * ss * (1.0 - ss)).astype(BF16)
        datt_ref[...] = _dot_nt(dya, wa_ref[...]).astype(BF16)
        dz2 = _dot_nt(dys, ws_ref[...])
        dpre = (dz2 * z * gl * (1.0 - gl)).astype(BF16)
        dz = dz2 * gl + _dot_nt(dpre, wg_ref[...])
        _, gelu_vjp = jax.vjp(jax.nn.gelu, yv)
        dy_ref[...] = gelu_vjp(dz)[0]

        grads = ((awo_ref, _dot_tn(merged, dob)), (awa_ref, _dot_tn(att, dya)),
                 (aws_ref, _dot_tn(z2b, dys)), (awg_ref, _dot_tn(zb, dpre)), (dg2_ref, dg2), (dg3_ref, dg3))

        @pl.when(i == 0)
        def _():
            for ref, val in grads:
                ref[...] = val

        @pl.when(i > 0)
        def _():
            for ref, val in grads:
                ref[...] += val

        @pl.when(i == n_steps - 1)
        def _():
            pltpu.sync_copy(awg_ref, dwg_hbm)
            pltpu.sync_copy(aws_ref, dws_hbm)
            pltpu.sync_copy(awa_ref, dwa_hbm)
            pltpu.sync_copy(awo_ref, dwo_hbm)

    tok = lambda w: pl.BlockSpec((tile, w), lambda i: (i, 0))
    vec = _const_spec((1, D_MODEL))
    any_ = pl.BlockSpec(memory_space=pl.ANY)
    vec_out = pl.BlockSpec((1, D_MODEL), lambda i: (0, 0))
    f32 = lambda *s: jax.ShapeDtypeStruct(s, F32)
    bf = lambda *s: jax.ShapeDtypeStruct(s, BF16)
    return _hosted_call(
        body, carry, _edge_1d(n_steps), name="merge_bwd", grid=(n_steps,),
        in_specs=[tok(D_MODEL), tok(D_MODEL), tok(D_MODEL), tok(D_MODEL), tok(SSM_W), tok(ATTN_W), tok(D_MODEL), tok(D_MODEL),
                  vec, vec, _const_spec((SSM_W, SSM_W)), _const_spec((SSM_W, D_MODEL)), _const_spec((ATTN_W, D_MODEL)),
                  _const_spec((D_MODEL, D_MODEL))],
        out_specs=[tok(D_MODEL), tok(D_MODEL), tok(D_MODEL), tok(ATTN_W), tok(SSM_W), any_, any_, any_, any_, vec_out, vec_out],
        out_shape=[f32(T, D_MODEL), bf(T, D_MODEL), bf(T, D_MODEL), bf(T, ATTN_W), f32(T, SSM_W),
                   f32(SSM_W, SSM_W), f32(SSM_W, D_MODEL), f32(ATTN_W, D_MODEL), f32(D_MODEL, D_MODEL),
                   f32(1, D_MODEL), f32(1, D_MODEL)],
        scratch_shapes=[pltpu.VMEM((SSM_W, SSM_W), F32), pltpu.VMEM((SSM_W, D_MODEL), F32),
                        pltpu.VMEM((ATTN_W, D_MODEL), F32), pltpu.VMEM((D_MODEL, D_MODEL), F32)],
        compiler_params=_params(("arbitrary",), VMEM_BIG),
        inputs=(dh2, dx2, x1, o, y, att, ga, gs, g2, g3, w_glu, w_ssm, w_attn, w_out))


FF_SHARD = D_FF // N_DEV


def _mlp_fwd(h2, x1, target, g4, w_ff_in, w_ff_out, tile):
    T = h2.shape[0]
    col_chunk = 2 * FF_SHARD

    def body(h2_ref, x1_ref, tg_ref, g4_ref, wi_ref, wo_ref, a_ref, dfo_ref, dx2_ref, loss_ref, dg4_ref, rr_ref):
        i = pl.program_id(0)
        h2v = h2_ref[...]
        for c in range(D_FF // col_chunk):
            cols = slice(c * col_chunk, (c + 1) * col_chunk)
            a = _dot_nt(h2v, wi_ref[cols, :])
            a_ref[:, cols] = a.astype(BF16)
            ra = jnp.maximum(a, 0.0)
            rr_ref[:, cols] = (ra * ra).astype(BF16)
        f = _dot(rr_ref[...], wo_ref[...])
        r = _rms_scale(f)
        g = g4_ref[...]
        err = x1_ref[...] + f * r * g - tg_ref[...]
        dx2 = err * (1.0 / D_MODEL)
        dx2_ref[...] = dx2
        dfo, dg = _rms_bwd(dx2, f, r, g)
        dfo_ref[...] = dfo.astype(BF16)
        row = lax.broadcasted_iota(jnp.int32, (8, LANES), 0)
        col = lax.broadcasted_iota(jnp.int32, (8, LANES), 1)
        loss = jnp.where((row == 0) & (col == 0), (0.5 / D_MODEL) * jnp.sum(err * err), 0.0)

        @pl.when(i == 0)
        def _():
            loss_ref[...] = loss
            dg4_ref[...] = dg

        @pl.when(i > 0)
        def _():
            loss_ref[...] += loss
            dg4_ref[...] += dg

    tok = pl.BlockSpec((tile, D_MODEL), lambda i: (i, 0))
    return pl.pallas_call(
        body, name="mlp_fwd", grid=(T // tile,),
        in_specs=[tok, tok, tok, _const_spec((1, D_MODEL)), _const_spec((D_FF, D_MODEL)), _const_spec((D_FF, D_MODEL))],
        out_specs=[pl.BlockSpec((tile, D_FF), lambda i: (i, 0)), tok, tok,
                   pl.BlockSpec((8, LANES), lambda i: (0, 0)), pl.BlockSpec((1, D_MODEL), lambda i: (0, 0))],
        out_shape=[jax.ShapeDtypeStruct((T, D_FF), BF16), jax.ShapeDtypeStruct((T, D_MODEL), BF16),
                   jax.ShapeDtypeStruct((T, D_MODEL), F32), jax.ShapeDtypeStruct((8, LANES), F32),
                   jax.ShapeDtypeStruct((1, D_MODEL), F32)],
        scratch_shapes=[pltpu.VMEM((tile, D_FF), BF16)],
        compiler_params=_params(("arbitrary",), VMEM_MAX),
    )(*_in_hbm(h2, x1, target, g4, w_ff_in.reshape(D_FF, D_MODEL), w_ff_out.reshape(D_FF, D_MODEL)))


def _mlp_weight_grads(dfo, a, h2, w_ff_out, row_chunk):
    T = h2.shape[0]

    def body(dfo_ref, h2_ref, a_ref, wo_ref, dwi_ref, dwo_ref, da_ref, rr_ref):
        def rows(r, _):
            sl = pl.ds(pl.multiple_of(r * row_chunk, row_chunk), row_chunk)
            ra = jnp.maximum(a_ref[sl, :].astype(F32), 0.0)
            da_ref[sl, :] = (_dot_nt(dfo_ref[sl, :], wo_ref[0]) * (2.0 * ra)).astype(BF16)
            rr_ref[sl, :] = (ra * ra).astype(BF16)
            return 0

        lax.fori_loop(0, T // row_chunk, rows, 0)
        dwo_ref[0] = _dot_tn(rr_ref[...], dfo_ref[...])
        dwi_ref[0] = _dot_tn(h2_ref[...], da_ref[...])

    return pl.pallas_call(
        body, name="mlp_weight_grads", grid=(N_DEV,),
        in_specs=[_const_spec((T, D_MODEL)), _const_spec((T, D_MODEL)), pl.BlockSpec((T, FF_SHARD), lambda k: (0, k)),
                  pl.BlockSpec((1, FF_SHARD, D_MODEL), lambda k: (k, 0, 0))],
        out_specs=[pl.BlockSpec((1, D_MODEL, FF_SHARD), lambda k: (k, 0, 0)),
                   pl.BlockSpec((1, FF_SHARD, D_MODEL), lambda k: (k, 0, 0)), pl.BlockSpec((T, FF_SHARD), lambda k: (0, k))],
        out_shape=[jax.ShapeDtypeStruct((N_DEV, D_MODEL, FF_SHARD), F32), jax.ShapeDtypeStruct((N_DEV, FF_SHARD, D_MODEL), F32),
                   jax.ShapeDtypeStruct((T, D_FF), BF16)],
        scratch_shapes=[pltpu.VMEM((T, FF_SHARD), BF16)],
        compiler_params=_params(("arbitrary",), VMEM_MAX),
    )(*_in_hbm(dfo, h2, a, w_ff_out))


def _mlp_input_grad(da, w_ff_in_t, tile):
    T = da.shape[0]

    def body(da_ref, w_ref, o_ref):
        o_ref[...] = _dot(da_ref[...], w_ref[...])

    return pl.pallas_call(
        body, name="mlp_input_grad", grid=(T // tile,),
        in_specs=[pl.BlockSpec((tile, D_FF), lambda i: (i, 0)), _const_spec((D_FF, D_MODEL))],
        out_specs=pl.BlockSpec((tile, D_MODEL), lambda i: (i, 0)),
        out_shape=jax.ShapeDtypeStruct((T, D_MODEL), F32),
        compiler_params=_params(("arbitrary",), VMEM_MID),
    )(*_in_hbm(da, w_ff_in_t))


def _block_diag_in(b):
    bt = b.reshape(N_SSM_BLOCKS, 8, GROUP_CH, N_STATE)
    eye = jnp.eye(8, dtype=b.dtype)
    return jnp.einsum("jacp,ab->jacbp", bt, eye).reshape(N_SSM_BLOCKS, LANES, SSM_LANE_BLOCK)


def _block_diag_in_grad(g):
    g = g.reshape(N_SSM_BLOCKS, 8, GROUP_CH, 8, N_STATE)
    d = jnp.diagonal(g, axis1=1, axis2=3)
    return jnp.transpose(d, (0, 3, 1, 2)).reshape(N_GROUPS, GROUP_CH, N_STATE)


def _block_diag_out(c):
    ct = c.reshape(N_SSM_BLOCKS, 8, GROUP_CH, N_STATE)
    eye = jnp.eye(8, dtype=c.dtype)
    return jnp.einsum("jacp,ab->japbc", ct, eye).reshape(N_SSM_BLOCKS, SSM_LANE_BLOCK, LANES)


def _block_diag_out_grad(g):
    g = g.reshape(N_SSM_BLOCKS, 8, N_STATE, 8, GROUP_CH)
    d = jnp.diagonal(g, axis1=1, axis2=3)
    return jnp.transpose(d, (0, 3, 2, 1)).reshape(N_GROUPS, GROUP_CH, N_STATE)


def _tiles(T):
    return dict(proj=min(512, T), proj_bwd=min(512, T), merge=min(512, T), merge_bwd=min(256, T),
                mlp_fwd=min(512, T), mlp_bwd=min(512, T), ssm_chunk=min(1024, T))


def _mesh_position():
    x, y, c = lax.axis_index("x"), lax.axis_index("y"), lax.axis_index("c")
    other_chips = [(1 - x, y), (x, 1 - y), (1 - x, 1 - y)]
    return x, y, c, other_chips


def _gather_carry(arrays):
    n = len(arrays)

    def copies(ins, outs, sems):
        send_sems, recv_sems, local_sems = sems
        x, y, c, chips = _mesh_position()
        me, sibling = (x, y, c), (x, y, 1 - c)

        def copy(a, k, block, to, src=None):
            px, py, pc = block
            dst = outs[a].at[4 * px + 2 * py + pc]
            return pltpu.make_async_remote_copy(
                src_ref=dst if src is None else src, dst_ref=dst, send_sem=send_sems.at[7 * a + k],
                recv_sem=recv_sems.at[7 * a + k], device_id=to, device_id_type=MESH_IDS)

        mine = [pltpu.make_async_copy(ins[a], outs[a].at[4 * x + 2 * y + c], local_sems.at[a]) for a in range(n)]
        first = []
        for a in range(n):
            first.append(copy(a, 0, me, sibling, src=ins[a]))
            first += [copy(a, 1 + j, me, (*chip, c), src=ins[a]) for j, chip in enumerate(chips)]
        return copy, mine, first, me, sibling, chips, c

    def start(ins, outs, sems):
        _, mine, first, *_ = copies(ins, outs, sems)
        for cp in mine + first:
            cp.start()

    def finish(ins, outs, sems):
        copy, mine, first, me, sibling, chips, c = copies(ins, outs, sems)
        passed = []
        for a in range(n):
            for j, chip in enumerate(chips):
                copy(a, 1 + j, (*chip, c), me).wait_recv()
                passed.append(copy(a, 4 + j, (*chip, c), sibling))
                passed[-1].start()
        for a in range(n):
            copy(a, 0, sibling, me).wait_recv()
            for j, chip in enumerate(chips):
                copy(a, 4 + j, (*chip, 1 - c), me).wait_recv()
        for cp in first + passed:
            cp.wait_send()
        for cp in mine:
            cp.wait()

    return _Carry(arrays, [jax.ShapeDtypeStruct((N_DEV,) + a.shape, a.dtype) for a in arrays],
                  [pltpu.SemaphoreType.DMA((7 * n,)), pltpu.SemaphoreType.DMA((7 * n,)), pltpu.SemaphoreType.DMA((n,))],
                  start, finish)


def _pairwise_carry(arrays, n_slots, make_copies):
    n = len(arrays)

    def start(ins, outs, sems):
        for cp in make_copies(ins, outs, sems):
            cp.start()

    def finish(ins, outs, sems):
        for cp in make_copies(ins, outs, sems):
            cp.wait()

    return _Carry(arrays, [jax.ShapeDtypeStruct((n_slots,) + a.shape[1:], a.dtype) for a in arrays],
                  [pltpu.SemaphoreType.DMA((n_slots * n,)), pltpu.SemaphoreType.DMA((n_slots * n,))], start, finish)


def _sibling_carry(grads):
    def make_copies(ins, outs, sems):
        x, y, c, _ = _mesh_position()
        return [pltpu.make_async_remote_copy(
            src_ref=ins[a].at[2 * ch + (1 - c)], dst_ref=outs[a].at[ch], send_sem=sems[0].at[4 * a + ch],
            recv_sem=sems[1].at[4 * a + ch], device_id=(x, y, 1 - c), device_id_type=MESH_IDS)
            for a in range(len(grads)) for ch in range(4)]

    return _pairwise_carry(grads, 4, make_copies)


def _chips_carry(sums):
    def make_copies(ins, outs, sems):
        x, y, c, chips = _mesh_position()
        return [pltpu.make_async_remote_copy(
            src_ref=ins[a].at[2 * px + py], dst_ref=outs[a].at[j], send_sem=sems[0].at[3 * a + j],
            recv_sem=sems[1].at[3 * a + j], device_id=(px, py, c), device_id_type=MESH_IDS)
            for a in range(len(sums)) for j, (px, py) in enumerate(chips)]

    return _pairwise_carry(sums, 3, make_copies)


def _row_tile(rows, cols):
    t = max(8, min(rows, (1 << 18) // cols // 8 * 8))
    while rows % t:
        t -= 8
    return t


def _add_sibling(grads8, recv, core, name):
    _, R, C = grads8.shape
    tr = _row_tile(R, C)
    g4 = grads8.reshape(4, 2, R, C)

    def body(core_ref, g_ref, r_ref, o_ref, ob_ref):
        s = g_ref[0] + r_ref[...]
        o_ref[...] = s
        ob_ref[...] = s.astype(BF16)

    out = pl.BlockSpec((1, tr, C), lambda ch, r, core_ref: (ch, r, 0))
    return pl.pallas_call(
        body, name=name,
        grid_spec=pltpu.PrefetchScalarGridSpec(
            num_scalar_prefetch=1, grid=(4, R // tr),
            in_specs=[pl.BlockSpec((1, 1, tr, C), lambda ch, r, core_ref: (ch, core_ref[0], r, 0)),
                      pl.BlockSpec((1, tr, C), lambda ch, r, core_ref: (ch, r, 0))],
            out_specs=[out, out]),
        out_shape=[jax.ShapeDtypeStruct((4, R, C), F32), jax.ShapeDtypeStruct((4, R, C), BF16)],
        compiler_params=_params(("arbitrary", "arbitrary")),
    )(core, *_in_hbm(g4, recv))


def _adam_math(w, g, m, v):
    m = ADAM_B1 * m + (1.0 - ADAM_B1) * g
    v = ADAM_B2 * v + (1.0 - ADAM_B2) * jnp.square(g)
    m_hat = m / (1.0 - ADAM_B1 ** ADAM_STEP)
    v_hat = v / (1.0 - ADAM_B2 ** ADAM_STEP)
    delta = -ADAM_LR * (m_hat / (jnp.sqrt(v_hat) + ADAM_EPS) + ADAM_WD * w)
    return delta, m, v


def _adam_big(w, m, v, chip_sums, recv, chip, name):
    R, C = w.shape
    tr = _row_tile(R, C)

    def body(chip_ref, w_ref, m_ref, v_ref, s_ref, r_ref, g_ref, d_ref, nm_ref, nv_ref):
        g = s_ref[0] + r_ref[0].astype(F32) + r_ref[1].astype(F32) + r_ref[2].astype(F32)
        g_ref[...] = g
        d_ref[...], nm_ref[...], nv_ref[...] = _adam_math(w_ref[...], g, m_ref[...], v_ref[...])

    blk = pl.BlockSpec((tr, C), lambda r, chip_ref: (r, 0))
    return pl.pallas_call(
        body, name=name,
        grid_spec=pltpu.PrefetchScalarGridSpec(
            num_scalar_prefetch=1, grid=(R // tr,),
            in_specs=[blk, blk, blk, pl.BlockSpec((1, tr, C), lambda r, chip_ref: (chip_ref[0], r, 0)),
                      pl.BlockSpec((3, tr, C), lambda r, chip_ref: (0, r, 0))],
            out_specs=[blk] * 4),
        out_shape=[jax.ShapeDtypeStruct((R, C), F32)] * 4,
        compiler_params=_params(("arbitrary",)),
    )(chip, *_in_hbm(w, m, v, chip_sums, recv))


def _sum_partials(partials, name):
    def body(p_ref, g_ref):
        g = p_ref[0]
        for d in range(1, N_DEV):
            g = g + p_ref[d]
        g_ref[...] = g

    return pl.pallas_call(body, name=name, grid=(1,), in_specs=[_whole(partials.shape)], out_specs=_whole(partials.shape[1:]),
                          out_shape=jax.ShapeDtypeStruct(partials.shape[1:], F32))(*_in_hbm(partials))


def _adam_small(ws, ms, vs, gs):
    n = len(ws)

    def body(*refs):
        w_refs, m_refs, v_refs, g_refs = (refs[i * n:(i + 1) * n] for i in range(4))
        d_refs, nm_refs, nv_refs = (refs[(4 + i) * n:(5 + i) * n] for i in range(3))
        for j in range(n):
            d_refs[j][...], nm_refs[j][...], nv_refs[j][...] = _adam_math(
                w_refs[j][...], g_refs[j][...], m_refs[j][...], v_refs[j][...])

    specs = [_whole(w.shape) for w in ws]
    outs = pl.pallas_call(body, name="adam_small", grid=(1,), in_specs=specs * 4, out_specs=specs * 3,
                          out_shape=[jax.ShapeDtypeStruct(w.shape, F32) for w in ws] * 3,
                          compiler_params=_params(("arbitrary",), VMEM_MID))(*_in_hbm(*ws, *ms, *vs, *gs))
    return outs[:n], outs[n:2 * n], outs[2 * n:]


PACK_QUANTUM = SUBLANES * LANES


def _pack(named, names):
    parts = []
    for nme in names:
        flat = named[nme].reshape(-1)
        parts.append(jnp.pad(flat, (0, -flat.size % PACK_QUANTUM)))
    return jnp.concatenate(parts).reshape(-1, LANES)


def _unpack(packed, shapes, names):
    flat = packed.reshape(-1)
    out, pos = {}, 0
    for nme in names:
        size = math.prod(shapes[nme])
        out[nme] = flat[pos:pos + size].reshape(shapes[nme])
        pos += size + (-size % PACK_QUANTUM)
    return out


BIG = ("w_in", "w_glu", "w_attn_branch", "w_ssm_branch", "w_out", "w_ff_in", "w_ff_out")
COLUMN_SHARDED = ("w_in", "w_attn_branch", "w_ssm_branch", "w_ff_in")
SMALL = ("norm_mix_pre", "norm_mix_post", "norm_mlp_pre", "norm_mlp_post", "rel_bias", "sinks", "lam_re", "lam_im",
         "log_dt", "b_re", "b_im", "c_re", "c_im", "d_skip")
SWAPPED_SMALL = ("rel_bias", "b_re", "b_im")
SMALL_LATE = ("norm_mix_pre", "rel_bias", "sinks", "loss")
SMALL_BEFORE_ATTN_BWD = tuple(n for n in SMALL if n not in SMALL_LATE)
ALL_WEIGHTS = ("norm_mix_pre", "norm_mix_post", "norm_mlp_pre", "norm_mlp_post", "w_in", "rel_bias", "sinks", "lam_re",
               "lam_im", "log_dt", "b_re", "b_im", "c_re", "c_im", "d_skip", "w_glu", "w_attn_branch", "w_ssm_branch",
               "w_out", "w_ff_in", "w_ff_out")


def _full_from_gathered(name, gathered):
    _, r, c = gathered.shape
    if name in COLUMN_SHARDED:
        return jnp.transpose(gathered, (1, 0, 2)).reshape(r, N_DEV * c)
    return gathered.reshape(N_DEV * r, c)


def _blocks_from_full(name, full):
    r, c = full.shape
    if name in COLUMN_SHARDED:
        return jnp.transpose(full.reshape(r, N_DEV, c // N_DEV), (1, 0, 2))
    return full.reshape(N_DEV, r // N_DEV, c)


def kernel(x, norm_mix_pre, norm_mix_post, norm_mlp_pre, norm_mlp_post, w_in, rel_bias, sinks, lam_re, lam_im, log_dt, b_re, b_im, c_re, c_im, d_skip, w_glu, w_attn_branch, w_ssm_branch, w_out, w_ff_in, w_ff_out, loss_target, m_norm_mix_pre, m_norm_mix_post, m_norm_mlp_pre, m_norm_mlp_post, m_w_in, m_rel_bias, m_sinks, m_lam_re, m_lam_im, m_log_dt, m_b_re, m_b_im, m_c_re, m_c_im, m_d_skip, m_w_glu, m_w_attn_branch, m_w_ssm_branch, m_w_out, m_w_ff_in, m_w_ff_out, v_norm_mix_pre, v_norm_mix_post, v_norm_mlp_pre, v_norm_mlp_post, v_w_in, v_rel_bias, v_sinks, v_lam_re, v_lam_im, v_log_dt, v_b_re, v_b_im, v_c_re, v_c_im, v_d_skip, v_w_glu, v_w_attn_branch, v_w_ssm_branch, v_w_out, v_w_ff_in, v_w_ff_out):
    args = dict(locals())
    w = {n: args[n] for n in ALL_WEIGHTS}
    m = {n: args["m_" + n] for n in ALL_WEIGHTS}
    v = {n: args["v_" + n] for n in ALL_WEIGHTS}
    core = lax.axis_index("c").astype(jnp.int32).reshape(1)
    chip = (2 * lax.axis_index("x") + lax.axis_index("y")).astype(jnp.int32).reshape(1)
    xs, target = x[0], loss_target[0]
    t = _tiles(xs.shape[0])
    local = lambda d, n: d[n][0].T if n == "w_in" else d[n][0]
    shard = {n: local(w, n).astype(BF16) for n in BIG}
    shard["w_ff_in"] = shard["w_ff_in"].T
    view = lambda n, a: jnp.swapaxes(a, -1, -2) if n in SWAPPED_SMALL else a
    small = {n: (view(n, w[n]) if n == "rel_bias" else view(n, w[n])[0]) for n in SMALL}
    g1, g2, g3, g4 = (small[n].reshape(1, D_MODEL) for n in ("norm_mix_pre", "norm_mix_post", "norm_mlp_pre", "norm_mlp_post"))
    bucket = jnp.asarray(_bucket_table())
    rel_b, sink = small["rel_bias"], small["sinks"].reshape(1, N_HEADS)
    lam_r, lam_i = small["lam_re"].reshape(1, STATES), small["lam_im"].reshape(1, STATES)
    ldt_rep = jnp.repeat(small["log_dt"].reshape(N_GROUPS), N_STATE).reshape(1, STATES)
    bd_re, bd_im = _block_diag_in(small["b_re"]), _block_diag_in(small["b_im"])
    cm_re, cm_im = _block_diag_out(small["c_re"]).astype(BF16), _block_diag_out(small["c_im"]).astype(BF16)
    dsk = small["d_skip"].reshape(1, SSM_W)

    (g_in,) = _run_carry(_gather_carry([shard["w_in"]]), "gather_w_in")
    wf_in = g_in.reshape(IN_W, D_MODEL)
    merge_names = ("w_glu", "w_attn_branch", "w_ssm_branch", "w_out")
    (q, k, vv, u, ga, gs), gathered = _in_proj_fwd(xs, g1, wf_in, t["proj"], _gather_carry([shard[n] for n in merge_names]))
    wf = {n: _full_from_gathered(n, g) for n, g in zip(merge_names, gathered)}
    (att,), (wf_ff_in,) = _attn_fwd(q, k, vv, bucket, rel_b, sink, _gather_carry([shard["w_ff_in"]]))
    a_re, a_im, bm_re, bm_im = _ssm_prep(lam_r, lam_i, ldt_rep, bd_re, bd_im)
    (y, h_re, h_im, in_re, in_im), (wf_ff_out,) = _ssm_fwd(
        u, a_re, a_im, bm_re, bm_im, cm_re, cm_im, dsk, t["ssm_chunk"], _gather_carry([shard["w_ff_out"]]))
    x1, o, h2 = _merge_fwd(xs, y, att, ga, gs, g2, g3, wf["w_glu"], wf["w_ssm_branch"], wf["w_attn_branch"], wf["w_out"],
                           t["merge"])
    a, dfo, dx2, loss_blk, dg4 = _mlp_fwd(h2, x1, target, g4, wf_ff_in, wf_ff_out, t["mlp_fwd"])

    def add_sibling(names, blocks, received):
        pairs = [_add_sibling(b, r, core, "add_sibling_" + n) for n, b, r in zip(names, blocks, received)]
        return [p[0] for p in pairs], [p[1] for p in pairs]

    ff_names = ("w_ff_in", "w_ff_out")
    dw_ff_in, dw_ff_out, da = _mlp_weight_grads(dfo, a, h2, wf_ff_out, t["mlp_bwd"])
    dh2 = _mlp_input_grad(da, wf_ff_in.reshape(D_FF, D_MODEL), t["mlp_bwd"])
    ff_blocks = [dw_ff_in, dw_ff_out]
    (dx1, dga, dgs, datt, dy, dw_glu, dw_ssm, dw_attn, dw_out, dg2, dg3), ff_recv = _merge_bwd(
        dh2, dx2, x1, o, y, att, ga, gs, g2, g3, wf["w_glu"], wf["w_ssm_branch"], wf["w_attn_branch"], wf["w_out"],
        t["merge_bwd"], _sibling_carry(ff_blocks))
    ff_sums, ff_sums_bf = add_sibling(ff_names, ff_blocks, ff_recv)
    merge_blocks = [_blocks_from_full(n, g) for n, g in zip(merge_names, (dw_glu, dw_attn, dw_ssm, dw_out))]
    (du, dbm_re, dbm_im, dcm_re, dcm_im, da_re, da_im, dd_skip), carried = _ssm_bwd(
        dy, u, h_re, h_im, in_re, in_im, a_re, a_im, bm_re, bm_im, cm_re, cm_im, dsk, t["ssm_chunk"],
        _join(_chips_carry(ff_sums_bf), _sibling_carry(merge_blocks)))
    ff_from_chips, merge_recv = carried[:2], carried[2:]
    merge_sums, merge_sums_bf = add_sibling(merge_names, merge_blocks, merge_recv)
    dbd_re, dbd_im, dlam_re, dlam_im, dldt_rep = _ssm_prep_bwd(lam_r, lam_i, ldt_rep, bd_re, bd_im, dbm_re, dbm_im, da_re, da_im)
    dlog_dt = _group_sum(dldt_rep.reshape(N_GROUPS, N_STATE))
    shapes = {n: view(n, w[n]).shape for n in SMALL}
    shapes["loss"] = (1,)
    small_grads = dict(
        norm_mix_post=dg2, norm_mlp_pre=dg3, norm_mlp_post=dg4, lam_re=dlam_re, lam_im=dlam_im, log_dt=dlog_dt,
        b_re=_block_diag_in_grad(dbd_re), b_im=_block_diag_in_grad(dbd_im),
        c_re=_block_diag_out_grad(dcm_re), c_im=_block_diag_out_grad(dcm_im), d_skip=dd_skip)
    packed_early = _pack({n: small_grads[n].reshape(shapes[n]) for n in SMALL_BEFORE_ATTN_BWD}, SMALL_BEFORE_ATTN_BWD)
    (dq, dk, dv, attn_small), carried = _attn_bwd(
        q, k, vv, datt, bucket, rel_b, sink, _join(_chips_carry(merge_sums_bf), _gather_carry([packed_early])))
    merge_from_chips, partials_early = carried[:-1], carried[-1]
    grad_x, dw_in_t, dg1 = _in_proj_bwd(xs, g1, wf_in, dx1, (dq, dk, dv, du, dga, dgs), t["proj_bwd"])[0]

    late = dict(norm_mix_pre=dg1, rel_bias=attn_small[:, :N_BUCKETS, 0], sinks=attn_small[:, N_BUCKETS, 0],
                loss=loss_blk[0:1, 0])
    packed_late = _pack({n: late[n].reshape(shapes[n]) for n in SMALL_LATE}, SMALL_LATE)
    in_blocks = [dw_in_t.reshape(N_DEV, IN_W // N_DEV, D_MODEL)]
    in_recv = _run_carry(_sibling_carry(in_blocks), "reduce_sibling_w_in")
    in_sums, in_sums_bf = add_sibling(("w_in",), in_blocks, in_recv)
    in_from_chips, partials_late = _run_carry(_join(_chips_carry(in_sums_bf), _gather_carry([packed_late])), "reduce_chips_w_in")

    grads, deltas, new_m, new_v = {}, {}, {}, {}
    sums = dict(zip(ff_names + merge_names + ("w_in",), ff_sums + merge_sums + in_sums))
    received = dict(zip(ff_names + merge_names + ("w_in",), ff_from_chips + merge_from_chips + [in_from_chips]))
    for n in BIG:
        outs = _adam_big(local(w, n), local(m, n), local(v, n), sums[n], received[n], chip, "adam_" + n)
        grads[n], deltas[n], new_m[n], new_v[n] = ((o.T if n == "w_in" else o)[None] for o in outs)

    grads.update(_unpack(_sum_partials(partials_early, "sum_small_grads"), shapes, SMALL_BEFORE_ATTN_BWD))
    grads.update(_unpack(_sum_partials(partials_late, "sum_late_grads"), shapes, SMALL_LATE))
    loss = grads.pop("loss").reshape(())
    small_out = _adam_small(*[[view(n, d[n]) for n in SMALL] for d in (w, m, v)], [grads[n] for n in SMALL])
    for store, vals in zip((deltas, new_m, new_v), small_out):
        store.update(zip(SMALL, vals))
    for store in (grads, deltas, new_m, new_v):
        store.update({n: view(n, store[n]) for n in SWAPPED_SMALL})

    return (loss, grad_x[None], *[grads[n] for n in ALL_WEIGHTS], *[deltas[n] for n in ALL_WEIGHTS],
            *[new_m[n] for n in ALL_WEIGHTS], *[new_v[n] for n in ALL_WEIGHTS])
```

```python
import functools
import math

import jax
import jax.numpy as jnp
import numpy as np
from jax import lax
from jax.experimental import pallas as pl
from jax.experimental.pallas import tpu as pltpu

F32 = jnp.float32
BF16 = jnp.bfloat16

D_MODEL = 1024
N_HEADS = 8
HEAD_DIM = 64
ATTN_W = 512
KV_W = 128
BLOCK = 128
N_BUCKETS = 32
SSM_W = 512
N_GROUPS = 32
N_STATE = 64
GROUP_CH = 16
STATES = N_GROUPS * N_STATE
D_FF = 4096
IN_W = 3328
SPLITS = (0, 512, 640, 768, 1280, 2304, 3328)
RMS_EPS = 1e-6
NEG_INF = -1e30
SUBLANES = 8
LANES = 128
SSM_LANE_BLOCK = 512
N_SSM_BLOCKS = STATES // SSM_LANE_BLOCK
VMEM_BIG = 52 * 1024 * 1024
VMEM_MID = 40 * 1024 * 1024
VMEM_MAX = 60 * 1024 * 1024

ADAM_LR = 0.001
ADAM_B1 = 0.9
ADAM_B2 = 0.999
ADAM_EPS = 1e-08
ADAM_WD = 0.01
ADAM_STEP = 10

N_DEV = 8


def _dot(a, b):
    return jnp.dot(a, b, preferred_element_type=F32)


def _dot_nt(a, b):
    return lax.dot_general(a, b, (((1,), (1,)), ((), ())), preferred_element_type=F32)


def _dot_tn(a, b):
    return lax.dot_general(a, b, (((0,), (0,)), ((), ())), preferred_element_type=F32)


def _rms_scale(x):
    return lax.rsqrt(jnp.mean(x * x, axis=-1, keepdims=True) + RMS_EPS)


def _rms_bwd(dy, x, r, g):
    t = dy * g
    dx = r * t - x * (r * r * r) * jnp.mean(t * x, axis=-1, keepdims=True)
    dg = jnp.sum(dy * x * r, axis=0, keepdims=True)
    return dx, dg


def _const_spec(shape):
    nd = len(shape)
    return pl.BlockSpec(shape, lambda *_: (0,) * nd, pipeline_mode=pl.Buffered(1))


def _in_hbm(*arrays):
    return tuple(pltpu.with_memory_space_constraint(a, pltpu.HBM) for a in arrays)


def _whole(shape):
    nd = len(shape)
    return pl.BlockSpec(shape, lambda *_: (0,) * nd)


def _params(sem, vmem=None):
    return pltpu.CompilerParams(dimension_semantics=sem, vmem_limit_bytes=vmem)


MESH_IDS = pl.DeviceIdType.MESH
HBM_SPEC = pl.BlockSpec(memory_space=pl.ANY)


class _Carry:
    def __init__(self, inputs, out_shapes, sems, start, finish):
        self.inputs, self.out_shapes, self.sems, self.start, self.finish = list(inputs), list(out_shapes), list(sems), start, finish


def _join(a, b):
    na_in, na_out, na_sem = len(a.inputs), len(a.out_shapes), len(a.sems)

    def start(ins, outs, sems):
        a.start(ins[:na_in], outs[:na_out], sems[:na_sem])
        b.start(ins[na_in:], outs[na_out:], sems[na_sem:])

    def finish(ins, outs, sems):
        a.finish(ins[:na_in], outs[:na_out], sems[:na_sem])
        b.finish(ins[na_in:], outs[na_out:], sems[na_sem:])

    return _Carry(a.inputs + b.inputs, a.out_shapes + b.out_shapes, a.sems + b.sems, start, finish)


def _hosted_call(body, carry, edge, *, name, grid, in_specs, out_specs, out_shape, scratch_shapes, compiler_params, inputs):
    n_in, n_out = len(in_specs), len(out_specs)
    inputs = [a if s.memory_space == pltpu.SMEM else _in_hbm(a)[0] for a, s in zip(inputs, in_specs)]
    if carry is None:
        outs = pl.pallas_call(body, name=name, grid=grid, in_specs=in_specs, out_specs=out_specs, out_shape=out_shape,
                              scratch_shapes=scratch_shapes, compiler_params=compiler_params)(*inputs)
        return list(outs), []
    c_in, c_out, c_sem = len(carry.inputs), len(carry.out_shapes), len(carry.sems)

    def wrapped(*refs):
        ins, refs = refs[:n_in], refs[n_in:]
        cins, refs = refs[:c_in], refs[c_in:]
        outs, refs = refs[:n_out], refs[n_out:]
        couts, refs = refs[:c_out], refs[c_out:]
        scratch, csems = refs[:len(refs) - c_sem], refs[len(refs) - c_sem:]
        first, last = edge()

        @pl.when(first)
        def _():
            carry.start(cins, couts, csems)

        body(*ins, *outs, *scratch)

        @pl.when(last)
        def _():
            carry.finish(cins, couts, csems)

    outs = pl.pallas_call(
        wrapped, name=name, grid=grid, in_specs=list(in_specs) + [HBM_SPEC] * c_in,
        out_specs=list(out_specs) + [HBM_SPEC] * c_out, out_shape=list(out_shape) + carry.out_shapes,
        scratch_shapes=list(scratch_shapes) + carry.sems, compiler_params=compiler_params)(*inputs, *_in_hbm(*carry.inputs))
    return list(outs[:n_out]), list(outs[n_out:])


def _edge_1d(n_steps):
    return lambda: (pl.program_id(0) == 0, pl.program_id(0) == n_steps - 1)


def _edge_2d(n0, n1):
    return lambda: ((pl.program_id(0) == 0) & (pl.program_id(1) == 0),
                    (pl.program_id(0) == n0 - 1) & (pl.program_id(1) == n1 - 1))


def _run_carry(carry, name):
    c_in, c_out = len(carry.inputs), len(carry.out_shapes)

    def body(*refs):
        ins, outs, sems = refs[:c_in], refs[c_in:c_in + c_out], refs[c_in + c_out:]
        carry.start(ins, outs, sems)
        carry.finish(ins, outs, sems)

    return pl.pallas_call(body, name=name, in_specs=[HBM_SPEC] * c_in, out_specs=[HBM_SPEC] * c_out,
                          out_shape=carry.out_shapes, scratch_shapes=carry.sems)(*_in_hbm(*carry.inputs))


def _in_proj_fwd(x, g1, w_in_t, tile, carry=None):
    T = x.shape[0]

    def body(x_ref, g_ref, w_ref, q_ref, k_ref, v_ref, u_ref, ga_ref, gs_ref, h_ref):
        xv = x_ref[...]
        h = (xv * _rms_scale(xv) * g_ref[...]).astype(BF16)
        h_ref[...] = h
        outs = (q_ref, k_ref, v_ref, u_ref, ga_ref, gs_ref)
        for p, o_ref in enumerate(outs):
            o_ref[...] = _dot_nt(h, w_ref[SPLITS[p]:SPLITS[p + 1], :]).astype(o_ref.dtype)

    widths = [SPLITS[p + 1] - SPLITS[p] for p in range(6)] + [D_MODEL]
    dtypes = [BF16, BF16, BF16, F32, F32, F32, BF16]
    return _hosted_call(
        body, carry, _edge_1d(T // tile), name="in_proj_fwd", grid=(T // tile,),
        in_specs=[pl.BlockSpec((tile, D_MODEL), lambda i: (i, 0)), _const_spec((1, D_MODEL)), _const_spec((IN_W, D_MODEL))],
        out_specs=[pl.BlockSpec((tile, w), lambda i: (i, 0)) for w in widths],
        out_shape=[jax.ShapeDtypeStruct((T, w), dt) for w, dt in zip(widths, dtypes)],
        scratch_shapes=[], compiler_params=_params(("arbitrary",), VMEM_MID), inputs=(x, g1, w_in_t))


PROJ_PARTS = (512, 256, 512, 2048)
PROJ_GRAD_BLOCK = 256


def _in_proj_weight_grad(h, dparts):
    T = h.shape[0]
    blocks = [wd // PROJ_GRAD_BLOCK for wd in PROJ_PARTS]
    starts = [sum(blocks[:p]) for p in range(len(blocks))]

    def body(h_ref, *refs):
        part_refs, o_ref = refs[:-1], refs[-1]
        j = pl.program_id(0)
        for p_ref, start, count in zip(part_refs, starts, blocks):
            @pl.when((j >= start) & (j < start + count))
            def _(p_ref=p_ref):
                o_ref[...] = _dot_tn(p_ref[...], h_ref[...])

    def part_spec(start, count):
        return pl.BlockSpec((T, PROJ_GRAD_BLOCK), lambda j: (0, jnp.clip(j - start, 0, count - 1)))

    return pl.pallas_call(
        body, name="in_proj_weight_grad", grid=(sum(blocks),),
        in_specs=[_const_spec((T, D_MODEL))] + [part_spec(s, c) for s, c in zip(starts, blocks)],
        out_specs=pl.BlockSpec((PROJ_GRAD_BLOCK, D_MODEL), lambda j: (j, 0)),
        out_shape=jax.ShapeDtypeStruct((IN_W, D_MODEL), F32),
        compiler_params=_params(("arbitrary",), VMEM_MID),
    )(*_in_hbm(h, *dparts))


def _in_proj_input_grad(x, g1, w_in_t, dx1, dparts, tile, first_tile, n_tiles, name, carry=None):
    offsets = [sum(PROJ_PARTS[:p]) for p in range(len(PROJ_PARTS))]

    def body(x_ref, g_ref, w_ref, dx1_ref, *refs):
        part_refs, (gx_ref, dg_ref) = refs[:len(PROJ_PARTS)], refs[len(PROJ_PARTS):]
        i = pl.program_id(0)
        xv = x_ref[...]
        r = _rms_scale(xv)
        g = g_ref[...]
        dh = sum(_dot(p_ref[...], w_ref[off:off + wd, :]) for p_ref, off, wd in zip(part_refs, offsets, PROJ_PARTS))
        dxn, dg = _rms_bwd(dh, xv, r, g)
        gx_ref[...] = dx1_ref[...] + dxn

        @pl.when(i == 0)
        def _():
            dg_ref[...] = dg

        @pl.when(i > 0)
        def _():
            dg_ref[...] += dg

    tok = lambda wd: pl.BlockSpec((tile, wd), lambda i: (i + first_tile, 0))
    return _hosted_call(
        body, carry, _edge_1d(n_tiles), name=name, grid=(n_tiles,),
        in_specs=[tok(D_MODEL), _const_spec((1, D_MODEL)), _const_spec((IN_W, D_MODEL)), tok(D_MODEL)] + [tok(wd) for wd in PROJ_PARTS],
        out_specs=[pl.BlockSpec((tile, D_MODEL), lambda i: (i, 0)), pl.BlockSpec((1, D_MODEL), lambda i: (0, 0))],
        out_shape=[jax.ShapeDtypeStruct((n_tiles * tile, D_MODEL), F32), jax.ShapeDtypeStruct((1, D_MODEL), F32)],
        scratch_shapes=[], compiler_params=_params(("arbitrary",), VMEM_MID), inputs=(x, g1, w_in_t, dx1, *dparts))


def _bucket_table():
    qi = np.arange(BLOCK)[:, None]
    kj = np.arange(2 * BLOCK)[None, :]
    dist = qi + BLOCK - kj
    max_exact = N_BUCKETS // 2
    d = np.maximum(dist, 0)
    df = np.maximum(d, 1).astype(np.float32)
    large = max_exact + (np.log(df / np.float32(max_exact)) / np.float32(math.log(BLOCK / max_exact))
                         * np.float32(N_BUCKETS - max_exact)).astype(np.int32)
    large = np.minimum(large, N_BUCKETS - 1)
    bucket = np.where(d < max_exact, d, large)
    return np.where((dist >= 0) & (dist < BLOCK), bucket, -1).astype(np.int32)


def _build_bias(bucket_ref, rb_ref, bias_ref):
    bk = bucket_ref[...]
    for h in range(N_HEADS):
        def add(b, acc, h=h):
            return acc + jnp.where(bk == b, rb_ref[h, b], 0.0)
        bias_ref[h] = lax.fori_loop(0, N_BUCKETS, add, jnp.zeros((BLOCK, 2 * BLOCK), F32))


def _kv_variants(prev_ref, cur_ref):
    cat = jnp.concatenate([prev_ref[...], cur_ref[...]], axis=0)
    lo = lax.broadcasted_iota(jnp.int32, cat.shape, 1) < HEAD_DIM
    zero = jnp.zeros_like(cat)
    head0_lo = jnp.where(lo, cat, zero)
    head1_hi = jnp.where(lo, zero, cat)
    return ((head0_lo, pltpu.roll(head0_lo, HEAD_DIM, 1)), (pltpu.roll(head1_hi, HEAD_DIM, 1), head1_hi))


def _merge_kv_grads(g):
    lo = lax.broadcasted_iota(jnp.int32, g[0][0].shape, 1) < HEAD_DIM
    return jnp.where(lo, g[0][0] + pltpu.roll(g[0][1], HEAD_DIM, 1), g[1][1] + pltpu.roll(g[1][0], HEAD_DIM, 1))


def _head_lanes(h):
    return slice((h // 2) * LANES, (h // 2 + 1) * LANES)


def _attn_probs(q_ref, kvar, bias_ref, sk_ref, valid, s_ref):
    for h in range(N_HEADS):
        s_ref[h] = _dot_nt(q_ref[:, _head_lanes(h)], kvar[h // 4][h % 2])
    head = lax.broadcasted_iota(jnp.int32, (N_HEADS, 1, 1), 0)
    sink = jnp.zeros((N_HEADS, 1, 1), F32)
    for h in range(N_HEADS):
        sink = jnp.where(head == h, sk_ref[0, h], sink)
    s = jnp.where(valid[None], s_ref[...] * (HEAD_DIM ** -0.5) + bias_ref[...], NEG_INF)
    m = jnp.maximum(jnp.max(s, axis=-1, keepdims=True), sink)
    p = jnp.exp(s - m)
    e_sink = jnp.exp(sink - m)
    inv = 1.0 / (jnp.sum(p, axis=-1, keepdims=True) + e_sink)
    return p * inv, e_sink * inv


def _attn_valid(bucket_ref, n):
    col = lax.broadcasted_iota(jnp.int32, (BLOCK, 2 * BLOCK), 1)
    return (bucket_ref[...] >= 0) & ((n > 0) | (col >= BLOCK))


def _attn_fwd(q, k, v, bucket, rel_bias, sinks, carry=None):
    T = q.shape[0]
    nb = T // BLOCK

    def body(q_ref, kc_ref, kp_ref, vc_ref, vp_ref, bucket_ref, rb_ref, sk_ref, o_ref, bias_ref, s_ref, p_ref):
        n = pl.program_id(0)

        @pl.when(n == 0)
        def _():
            _build_bias(bucket_ref, rb_ref, bias_ref)

        kvar = _kv_variants(kp_ref, kc_ref)
        vvar = _kv_variants(vp_ref, vc_ref)
        pr, _ = _attn_probs(q_ref, kvar, bias_ref, sk_ref, _attn_valid(bucket_ref, n), s_ref)
        p_ref[...] = pr.astype(BF16)
        for m in range(N_HEADS // 2):
            acc = _dot(p_ref[2 * m], vvar[m // 2][0]) + _dot(p_ref[2 * m + 1], vvar[m // 2][1])
            o_ref[:, m * LANES:(m + 1) * LANES] = acc.astype(o_ref.dtype)

    cur = lambda w: pl.BlockSpec((BLOCK, w), lambda n: (n, 0))
    prev = lambda w: pl.BlockSpec((BLOCK, w), lambda n: (jnp.maximum(n - 1, 0), 0))
    smem = pl.BlockSpec(memory_space=pltpu.SMEM)
    return _hosted_call(
        body, carry, _edge_1d(nb), name="attn_fwd", grid=(nb,),
        in_specs=[cur(ATTN_W), cur(KV_W), prev(KV_W), cur(KV_W), prev(KV_W), _const_spec((BLOCK, 2 * BLOCK)), smem, smem],
        out_specs=[cur(ATTN_W)],
        out_shape=[jax.ShapeDtypeStruct((T, ATTN_W), BF16)],
        scratch_shapes=[pltpu.VMEM((N_HEADS, BLOCK, 2 * BLOCK), F32), pltpu.VMEM((N_HEADS, BLOCK, 2 * BLOCK), F32),
                        pltpu.VMEM((N_HEADS, BLOCK, 2 * BLOCK), BF16)],
        compiler_params=_params(("arbitrary",)), inputs=(q, k, k, v, v, bucket, rel_bias, sinks))


ATTN_SMALL_ROWS = N_BUCKETS + SUBLANES


def _attn_bwd(q, k, v, datt, bucket, rel_bias, sinks, carry=None):
    T = q.shape[0]
    nb = T // BLOCK

    def body(q_ref, do_ref, kc_ref, kp_ref, vc_ref, vp_ref, bucket_ref, rb_ref, sk_ref,
             dq_ref, dkv_ref, small_ref, bias_ref, ds_sum_ref, dsink_ref, kcarry_ref, vcarry_ref,
             s_ref, dp_ref, p_ref, dsc_ref):
        n = pl.program_id(0)

        @pl.when(n == 0)
        def _():
            _build_bias(bucket_ref, rb_ref, bias_ref)
            ds_sum_ref[...] = jnp.zeros_like(ds_sum_ref)
            dsink_ref[...] = jnp.zeros_like(dsink_ref)
            kcarry_ref[...] = jnp.zeros_like(kcarry_ref)
            vcarry_ref[...] = jnp.zeros_like(vcarry_ref)

        @pl.when(n < nb)
        def _():
            kvar = _kv_variants(kp_ref, kc_ref)
            vvar = _kv_variants(vp_ref, vc_ref)
            pr, p_sink = _attn_probs(q_ref, kvar, bias_ref, sk_ref, _attn_valid(bucket_ref, n), s_ref)
            for h in range(N_HEADS):
                dp_ref[h] = _dot_nt(do_ref[:, _head_lanes(h)], vvar[h // 4][h % 2])
            dp = dp_ref[...]
            dsum = jnp.sum(pr * dp, axis=-1, keepdims=True)
            ds = pr * (dp - dsum)
            ds_sum_ref[...] += ds
            dsink_ref[...] -= jnp.sum(p_sink * dsum, axis=1, keepdims=True)
            dsc_ref[...] = (ds * (HEAD_DIM ** -0.5)).astype(BF16)
            p_ref[...] = pr.astype(BF16)
            for m in range(N_HEADS // 2):
                dqm = _dot(dsc_ref[2 * m], kvar[m // 2][0]) + _dot(dsc_ref[2 * m + 1], kvar[m // 2][1])
                dq_ref[:, m * LANES:(m + 1) * LANES] = dqm.astype(dq_ref.dtype)
            dk_var = [[None, None], [None, None]]
            dv_var = [[None, None], [None, None]]
            for kvh in range(2):
                for e in range(2):
                    heads = [h for h in range(N_HEADS) if h // 4 == kvh and h % 2 == e]
                    dk_var[kvh][e] = sum(_dot_tn(dsc_ref[h], q_ref[:, _head_lanes(h)]) for h in heads)
                    dv_var[kvh][e] = sum(_dot_tn(p_ref[h], do_ref[:, _head_lanes(h)]) for h in heads)
            dk_cat = _merge_kv_grads(dk_var)
            dv_cat = _merge_kv_grads(dv_var)

            @pl.when(n > 0)
            def _():
                dkv_ref[:, :KV_W] = (kcarry_ref[...] + dk_cat[:BLOCK]).astype(BF16)
                dkv_ref[:, KV_W:] = (vcarry_ref[...] + dv_cat[:BLOCK]).astype(BF16)

            kcarry_ref[...] = dk_cat[BLOCK:]
            vcarry_ref[...] = dv_cat[BLOCK:]

        @pl.when(n == nb)
        def _():
            dkv_ref[:, :KV_W] = kcarry_ref[...].astype(BF16)
            dkv_ref[:, KV_W:] = vcarry_ref[...].astype(BF16)
            bk = bucket_ref[...]
            row = lax.broadcasted_iota(jnp.int32, (N_HEADS, ATTN_SMALL_ROWS, LANES), 1)

            def add(b, acc):
                masked = jnp.where((bk == b)[None], ds_sum_ref[...], 0.0)
                val = jnp.sum(jnp.sum(masked, axis=1, keepdims=True), axis=2, keepdims=True)
                return acc + jnp.where(row == b, val, 0.0)

            small_ref[...] = lax.fori_loop(0, N_BUCKETS, add, jnp.where(row == N_BUCKETS, dsink_ref[...], 0.0))

    last = nb - 1
    cur = lambda w: pl.BlockSpec((BLOCK, w), lambda n: (jnp.minimum(n, last), 0))
    prev = lambda w: pl.BlockSpec((BLOCK, w), lambda n: (jnp.clip(n - 1, 0, last), 0))
    smem = pl.BlockSpec(memory_space=pltpu.SMEM)
    return _hosted_call(
        body, carry, _edge_1d(nb + 1), name="attn_bwd", grid=(nb + 1,),
        in_specs=[cur(ATTN_W), cur(ATTN_W), cur(KV_W), prev(KV_W), cur(KV_W), prev(KV_W),
                  _const_spec((BLOCK, 2 * BLOCK)), smem, smem],
        out_specs=[cur(ATTN_W), prev(2 * KV_W), pl.BlockSpec((N_HEADS, ATTN_SMALL_ROWS, LANES), lambda n: (0, 0, 0))],
        out_shape=[jax.ShapeDtypeStruct((T, ATTN_W), BF16), jax.ShapeDtypeStruct((T, 2 * KV_W), BF16),
                   jax.ShapeDtypeStruct((N_HEADS, ATTN_SMALL_ROWS, LANES), F32)],
        scratch_shapes=[pltpu.VMEM((N_HEADS, BLOCK, 2 * BLOCK), F32), pltpu.VMEM((N_HEADS, BLOCK, 2 * BLOCK), F32),
                        pltpu.VMEM((N_HEADS, 1, 1), F32), pltpu.VMEM((BLOCK, KV_W), F32), pltpu.VMEM((BLOCK, KV_W), F32),
                        pltpu.VMEM((N_HEADS, BLOCK, 2 * BLOCK), F32), pltpu.VMEM((N_HEADS, BLOCK, 2 * BLOCK), F32),
                        pltpu.VMEM((N_HEADS, BLOCK, 2 * BLOCK), BF16), pltpu.VMEM((N_HEADS, BLOCK, 2 * BLOCK), BF16)],
        compiler_params=_params(("arbitrary",)), inputs=(q, datt, k, k, v, v, bucket, rel_bias, sinks))


SCAN_UNROLL = 4


def _cmul(ar, ai, br, bi):
    return ar * br - ai * bi, ar * bi + ai * br


def _cmul_conj(ar, ai, br, bi):
    return ar * br + ai * bi, ar * bi - ai * br


def _ssm_discretize(lr, li, ldt):
    dt = jnp.exp(ldt)
    mag = jnp.exp(lr * dt)
    ab_re = mag * jnp.cos(li * dt)
    ab_im = mag * jnp.sin(li * dt)
    nr = ab_re - 1.0
    den = lr * lr + li * li
    f_re = (nr * lr + ab_im * li) / den
    f_im = (ab_im * lr - nr * li) / den
    return ab_re, ab_im, f_re, f_im


def _ssm_prep(lam_re, lam_im, ldt_rep, bd_re, bd_im):
    def body(lr_ref, li_ref, ldt_ref, bdr_ref, bdi_ref, ar_ref, ai_ref, br_ref, bi_ref):
        ab_re, ab_im, f_re, f_im = _ssm_discretize(lr_ref[...], li_ref[...], ldt_ref[...])
        ar_ref[...] = ab_re
        ai_ref[...] = ab_im
        bdr, bdi = bdr_ref[0], bdi_ref[0]
        br_ref[0] = (bdr * f_re - bdi * f_im).astype(BF16)
        bi_ref[0] = (bdi * f_re + bdr * f_im).astype(BF16)

    row = pl.BlockSpec((1, SSM_LANE_BLOCK), lambda j: (0, j))
    mat = pl.BlockSpec((1, LANES, SSM_LANE_BLOCK), lambda j: (j, 0, 0))
    return pl.pallas_call(
        body, name="ssm_prep", grid=(N_SSM_BLOCKS,),
        in_specs=[row, row, row, mat, mat], out_specs=[row, row, mat, mat],
        out_shape=[jax.ShapeDtypeStruct((1, STATES), F32)] * 2 + [jax.ShapeDtypeStruct((N_SSM_BLOCKS, LANES, SSM_LANE_BLOCK), BF16)] * 2,
        compiler_params=_params(("arbitrary",)),
    )(*_in_hbm(lam_re, lam_im, ldt_rep, bd_re, bd_im))


def _ssm_prep_bwd(lam_re, lam_im, ldt_rep, bd_re, bd_im, dbr, dbi, da_re, da_im):
    def body(lr_ref, li_ref, ldt_ref, bdr_ref, bdi_ref, dbr_ref, dbi_ref, dar_ref, dai_ref,
             dbdr_ref, dbdi_ref, dlr_ref, dli_ref, dldt_ref):
        lr, li, ldt = lr_ref[...], li_ref[...], ldt_ref[...]
        (_, _, f_re, f_im), vjp = jax.vjp(_ssm_discretize, lr, li, ldt)
        bdr, bdi, gbr, gbi = bdr_ref[0], bdi_ref[0], dbr_ref[0], dbi_ref[0]
        dbdr_ref[0] = gbr * f_re + gbi * f_im
        dbdi_ref[0] = gbi * f_re - gbr * f_im
        df_re = jnp.sum(gbr * bdr + gbi * bdi, axis=0, keepdims=True)
        df_im = jnp.sum(gbi * bdr - gbr * bdi, axis=0, keepdims=True)
        dlr, dli, dldt = vjp((dar_ref[...], dai_ref[...], df_re, df_im))
        dlr_ref[...] = dlr
        dli_ref[...] = dli
        dldt_ref[...] = dldt

    row = pl.BlockSpec((1, SSM_LANE_BLOCK), lambda j: (0, j))
    mat = pl.BlockSpec((1, LANES, SSM_LANE_BLOCK), lambda j: (j, 0, 0))
    mat_shape = jax.ShapeDtypeStruct((N_SSM_BLOCKS, LANES, SSM_LANE_BLOCK), F32)
    row_shape = jax.ShapeDtypeStruct((1, STATES), F32)
    return pl.pallas_call(
        body, name="ssm_prep_bwd", grid=(N_SSM_BLOCKS,),
        in_specs=[row, row, row, mat, mat, mat, mat, row, row], out_specs=[mat, mat, row, row, row],
        out_shape=[mat_shape, mat_shape, row_shape, row_shape, row_shape],
        compiler_params=_params(("arbitrary",)),
    )(*_in_hbm(lam_re, lam_im, ldt_rep, bd_re, bd_im, dbr, dbi, da_re, da_im))


def _group_sum(x):
    def body(x_ref, o_ref):
        o_ref[...] = jnp.sum(x_ref[...], axis=1, keepdims=True)
    return pl.pallas_call(body, name="ssm_group_sum", grid=(1,), in_specs=[_whole(x.shape)], out_specs=_whole((N_GROUPS, 1)),
                          out_shape=jax.ShapeDtypeStruct((N_GROUPS, 1), F32))(*_in_hbm(x))


def _power_table(ar, ai, p_re_ref, p_im_ref, steps):
    shape = (SUBLANES, SSM_LANE_BLOCK)
    p_re_ref[0:SUBLANES] = jnp.broadcast_to(ar, shape)
    p_im_ref[0:SUBLANES] = jnp.broadcast_to(ai, shape)
    m = 1
    while m < steps:
        rows = m * SUBLANES
        top_re = p_re_ref[rows - SUBLANES:rows]
        top_im = p_im_ref[rows - SUBLANES:rows]
        cur_re = p_re_ref[0:rows].reshape(m, SUBLANES, SSM_LANE_BLOCK)
        cur_im = p_im_ref[0:rows].reshape(m, SUBLANES, SSM_LANE_BLOCK)
        nxt_re, nxt_im = _cmul(cur_re, cur_im, top_re[None], top_im[None])
        p_re_ref[rows:2 * rows] = nxt_re.reshape(rows, SSM_LANE_BLOCK)
        p_im_ref[rows:2 * rows] = nxt_im.reshape(rows, SSM_LANE_BLOCK)
        m *= 2


def _to_segments(src_ref, dst_ref, steps):
    for s in range(SUBLANES):
        dst_ref[pl.ds(s, steps, stride=SUBLANES), :] = src_ref[s * steps:(s + 1) * steps, :]


def _from_segments(src_ref, dst_ref, steps):
    for s in range(SUBLANES):
        dst_ref[s * steps:(s + 1) * steps, :] = src_ref[pl.ds(s, steps, stride=SUBLANES), :]


def _segment_carries(e_re, e_im, an_re, an_im, c_re, c_im, reverse):
    order = range(SUBLANES - 1, -1, -1) if reverse else range(SUBLANES)
    ins_re, ins_im = [None] * SUBLANES, [None] * SUBLANES
    for s in order:
        ins_re[s], ins_im[s] = c_re, c_im
        pr, pi = _cmul(an_re, an_im, c_re, c_im)
        c_re = e_re[s:s + 1] + pr
        c_im = e_im[s:s + 1] + pi
    return jnp.concatenate(ins_re, axis=0), jnp.concatenate(ins_im, axis=0), c_re, c_im


def _ssm_fwd(u, a_re, a_im, b_re, b_im, c_re, c_im, d_skip, chunk, carry=None):
    T = u.shape[0]
    nc = T // chunk
    steps = chunk // SUBLANES
    blk = SSM_LANE_BLOCK

    def body(u_ref, ar_ref, ai_ref, br_ref, bi_ref, cr_ref, ci_ref, dk_ref,
             y_ref, hr_ref, hi_ref, inr_ref, ini_ref, useg_ref, yseg_ref, pr_ref, pi_ref, carry_ref):
        c = pl.program_id(1)
        ar, ai = ar_ref[...], ai_ref[...]

        @pl.when(c == 0)
        def _():
            _power_table(ar, ai, pr_ref, pi_ref, steps)
            carry_ref[...] = jnp.zeros_like(carry_ref)

        _to_segments(u_ref, useg_ref, steps)
        ub = useg_ref[...].astype(BF16)
        hr_ref[...] = _dot(ub, br_ref[0])
        hi_ref[...] = _dot(ub, bi_ref[0])
        first = slice(0, SUBLANES)

        def scan(t4, prev):
            for j in range(SCAN_UNROLL):
                rows = pl.ds(pl.multiple_of((t4 * SCAN_UNROLL + j) * SUBLANES, SUBLANES), SUBLANES)
                pr, pi = _cmul(pr_ref[first, :], pi_ref[first, :], prev[0], prev[1])
                prev = (pr + hr_ref[rows, :], pi + hi_ref[rows, :])
                hr_ref[rows, :] = prev[0]
                hi_ref[rows, :] = prev[1]
            return prev

        zero = jnp.zeros((SUBLANES, blk), F32)
        lax.fori_loop(0, steps // SCAN_UNROLL, scan, (zero, zero))

        top = slice(chunk - SUBLANES, chunk)
        in_re, in_im, out_re, out_im = _segment_carries(
            hr_ref[top, :], hi_ref[top, :], pr_ref[top, :][0:1], pi_ref[top, :][0:1],
            carry_ref[0:1, :], carry_ref[1:2, :], reverse=False)
        carry_ref[0:1, :] = out_re
        carry_ref[1:2, :] = out_im
        inr_ref[...] = in_re
        ini_ref[...] = in_im

        def fix(t4, _):
            for j in range(SCAN_UNROLL):
                rows = pl.ds(pl.multiple_of((t4 * SCAN_UNROLL + j) * SUBLANES, SUBLANES), SUBLANES)
                fr, fi = _cmul(pr_ref[rows, :], pi_ref[rows, :], in_re, in_im)
                hr_ref[rows, :] += fr
                hi_ref[rows, :] += fi
            return 0

        lax.fori_loop(0, steps // SCAN_UNROLL, fix, 0)

        yseg_ref[...] = _dot(hr_ref[...].astype(BF16), cr_ref[0]) - _dot(hi_ref[...].astype(BF16), ci_ref[0])
        _from_segments(yseg_ref, y_ref, steps)
        y_ref[...] += dk_ref[...] * u_ref[...]

    row = pl.BlockSpec((1, blk), lambda j, c: (0, j))
    b_mat = pl.BlockSpec((1, LANES, blk), lambda j, c: (j, 0, 0))
    c_mat = pl.BlockSpec((1, blk, LANES), lambda j, c: (j, 0, 0))
    tok = pl.BlockSpec((chunk, LANES), lambda j, c: (c, j))
    state = pl.BlockSpec((chunk, blk), lambda j, c: (c, j))
    enter = pl.BlockSpec((SUBLANES, blk), lambda j, c: (c, j))
    return _hosted_call(
        body, carry, _edge_2d(N_SSM_BLOCKS, nc), name="ssm_fwd", grid=(N_SSM_BLOCKS, nc),
        in_specs=[tok, row, row, b_mat, b_mat, c_mat, c_mat, pl.BlockSpec((1, LANES), lambda j, c: (0, j))],
        out_specs=[tok, state, state, enter, enter],
        out_shape=[jax.ShapeDtypeStruct((T, SSM_W), F32), jax.ShapeDtypeStruct((T, STATES), F32),
                   jax.ShapeDtypeStruct((T, STATES), F32), jax.ShapeDtypeStruct((nc * SUBLANES, STATES), F32),
                   jax.ShapeDtypeStruct((nc * SUBLANES, STATES), F32)],
        scratch_shapes=[pltpu.VMEM((chunk, LANES), F32), pltpu.VMEM((chunk, LANES), F32),
                        pltpu.VMEM((chunk, blk), F32), pltpu.VMEM((chunk, blk), F32), pltpu.VMEM((SUBLANES, blk), F32)],
        compiler_params=_params(("arbitrary", "arbitrary"), VMEM_MID),
        inputs=(u, a_re, a_im, b_re, b_im, c_re, c_im, d_skip))


def _ssm_bwd(dy, u, h_re, h_im, in_re, in_im, a_re, a_im, b_re, b_im, c_re, c_im, d_skip, chunk, carry=None):
    T = u.shape[0]
    nc = T // chunk
    steps = chunk // SUBLANES
    blk = SSM_LANE_BLOCK

    def body(dy_ref, u_ref, hr_ref, hi_ref, inr_ref, ini_ref, ar_ref, ai_ref, br_ref, bi_ref, cr_ref, ci_ref, dk_ref,
             du_ref, dbr_ref, dbi_ref, dcr_ref, dci_ref, dar_ref, dai_ref, ddk_ref,
             dyseg_ref, useg_ref, duseg_ref, gr_ref, gi_ref, pr_ref, pi_ref, carry_ref, accr_ref, acci_ref):
        c = pl.program_id(1)
        ar, ai = ar_ref[...], ai_ref[...]

        @pl.when(c == 0)
        def _():
            _power_table(ar, ai, pr_ref, pi_ref, steps)
            carry_ref[...] = jnp.zeros_like(carry_ref)
            accr_ref[...] = jnp.zeros_like(accr_ref)
            acci_ref[...] = jnp.zeros_like(acci_ref)

        _to_segments(dy_ref, dyseg_ref, steps)
        _to_segments(u_ref, useg_ref, steps)
        dyb = dyseg_ref[...].astype(BF16)
        ub = useg_ref[...].astype(BF16)
        gr_ref[...] = _dot_nt(dyb, cr_ref[0])
        gi_ref[...] = -_dot_nt(dyb, ci_ref[0])
        dcr = _dot_tn(hr_ref[...].astype(BF16), dyb)
        dci = -_dot_tn(hi_ref[...].astype(BF16), dyb)
        ddk = jnp.sum(dy_ref[...] * u_ref[...], axis=0, keepdims=True)

        first = slice(0, SUBLANES)

        def scan(k4, nxt):
            for j in range(SCAN_UNROLL):
                t = steps - 1 - (k4 * SCAN_UNROLL + j)
                rows = pl.ds(pl.multiple_of(t * SUBLANES, SUBLANES), SUBLANES)
                pr, pi = _cmul_conj(pr_ref[first, :], pi_ref[first, :], nxt[0], nxt[1])
                nxt = (pr + gr_ref[rows, :], pi + gi_ref[rows, :])
                gr_ref[rows, :] = nxt[0]
                gi_ref[rows, :] = nxt[1]
            return nxt

        top = slice(chunk - SUBLANES, chunk)
        zero = jnp.zeros((SUBLANES, blk), F32)
        lax.fori_loop(0, steps // SCAN_UNROLL, scan, (zero, zero))

        gin_re, gin_im, out_re, out_im = _segment_carries(
            gr_ref[0:SUBLANES, :], gi_ref[0:SUBLANES, :], pr_ref[top, :][0:1], -pi_ref[top, :][0:1],
            carry_ref[0:1, :], carry_ref[1:2, :], reverse=True)
        carry_ref[0:1, :] = out_re
        carry_ref[1:2, :] = out_im

        def fix_row(rows, prow, hp_re, hp_im, acc):
            fr, fi = _cmul_conj(pr_ref[prow, :], pi_ref[prow, :], gin_re, gin_im)
            g_re = gr_ref[rows, :] + fr
            g_im = gi_ref[rows, :] + fi
            gr_ref[rows, :] = g_re
            gi_ref[rows, :] = g_im
            return acc[0] + g_re * hp_re + g_im * hp_im, acc[1] + g_im * hp_re - g_re * hp_im

        def fix_at(t, acc):
            aligned = (lambda r: r * SUBLANES) if isinstance(t, int) else (lambda r: pl.multiple_of(r * SUBLANES, SUBLANES))
            rows, before, prow = (pl.ds(aligned(r), SUBLANES) for r in (t, t - 1, steps - 1 - t))
            return fix_row(rows, prow, hr_ref[before, :], hi_ref[before, :], acc)

        def fix(t4, acc):
            for j in range(SCAN_UNROLL):
                acc = fix_at(t4 * SCAN_UNROLL + j, acc)
            return acc

        acc = fix_row(first, top, inr_ref[...], ini_ref[...], (accr_ref[...], acci_ref[...]))
        for t in range(1, SCAN_UNROLL):
            acc = fix_at(t, acc)
        acc_re, acc_im = lax.fori_loop(1, steps // SCAN_UNROLL, fix, acc)
        accr_ref[...] = acc_re
        acci_ref[...] = acc_im

        gbr = gr_ref[...].astype(BF16)
        gbi = gi_ref[...].astype(BF16)
        duseg_ref[...] = _dot_nt(gbr, br_ref[0]) + _dot_nt(gbi, bi_ref[0])
        _from_segments(duseg_ref, dyseg_ref, steps)
        du_ref[...] = (dyseg_ref[...] + dk_ref[...] * dy_ref[...]).astype(BF16)
        dbr = _dot_tn(ub, gbr)
        dbi = _dot_tn(ub, gbi)

        @pl.when(c == 0)
        def _():
            dbr_ref[0] = dbr
            dbi_ref[0] = dbi
            dcr_ref[0] = dcr
            dci_ref[0] = dci
            ddk_ref[...] = ddk

        @pl.when(c > 0)
        def _():
            dbr_ref[0] += dbr
            dbi_ref[0] += dbi
            dcr_ref[0] += dcr
            dci_ref[0] += dci
            ddk_ref[...] += ddk

        @pl.when(c == nc - 1)
        def _():
            dar_ref[...] = jnp.sum(acc_re, axis=0, keepdims=True)
            dai_ref[...] = jnp.sum(acc_im, axis=0, keepdims=True)

    rev = lambda c: nc - 1 - c
    row = pl.BlockSpec((1, blk), lambda j, c: (0, j))
    b_mat = pl.BlockSpec((1, LANES, blk), lambda j, c: (j, 0, 0))
    c_mat = pl.BlockSpec((1, blk, LANES), lambda j, c: (j, 0, 0))
    tok = pl.BlockSpec((chunk, LANES), lambda j, c: (rev(c), j))
    state = pl.BlockSpec((chunk, blk), lambda j, c: (rev(c), j))
    enter = pl.BlockSpec((SUBLANES, blk), lambda j, c: (rev(c), j))
    chan = pl.BlockSpec((1, LANES), lambda j, c: (0, j))
    f32 = lambda *s: jax.ShapeDtypeStruct(s, F32)
    return _hosted_call(
        body, carry, _edge_2d(N_SSM_BLOCKS, nc), name="ssm_bwd", grid=(N_SSM_BLOCKS, nc),
        in_specs=[tok, tok, state, state, enter, enter, row, row, b_mat, b_mat, c_mat, c_mat, chan],
        out_specs=[tok, b_mat, b_mat, c_mat, c_mat, row, row, chan],
        out_shape=[jax.ShapeDtypeStruct((T, SSM_W), BF16), f32(N_SSM_BLOCKS, LANES, blk), f32(N_SSM_BLOCKS, LANES, blk),
                   f32(N_SSM_BLOCKS, blk, LANES), f32(N_SSM_BLOCKS, blk, LANES), f32(1, STATES), f32(1, STATES), f32(1, SSM_W)],
        scratch_shapes=[pltpu.VMEM((chunk, LANES), F32), pltpu.VMEM((chunk, LANES), F32), pltpu.VMEM((chunk, LANES), F32),
                        pltpu.VMEM((chunk, blk), F32), pltpu.VMEM((chunk, blk), F32),
                        pltpu.VMEM((chunk, blk), F32), pltpu.VMEM((chunk, blk), F32),
                        pltpu.VMEM((SUBLANES, blk), F32), pltpu.VMEM((SUBLANES, blk), F32), pltpu.VMEM((SUBLANES, blk), F32)],
        compiler_params=_params(("arbitrary", "arbitrary"), VMEM_BIG),
        inputs=(dy, u, h_re, h_im, in_re, in_im, a_re, a_im, b_re, b_im, c_re, c_im, d_skip))


def _merge_forward(y, att, ga, gs, w_glu, w_ssm, w_attn):
    z = jax.nn.gelu(y)
    zb = z.astype(BF16)
    gl = jax.nn.sigmoid(_dot(zb, w_glu))
    z2b = (z * gl).astype(BF16)
    y_ssm = _dot(z2b, w_ssm)
    y_attn = _dot(att, w_attn)
    sa = jax.nn.sigmoid(ga)
    ss = jax.nn.sigmoid(gs)
    merged = (sa * y_attn + ss * y_ssm).astype(BF16)
    return z, zb, gl, z2b, y_ssm, y_attn, sa, ss, merged


def _merge_fwd(x, y, att, ga, gs, g2, g3, w_glu, w_ssm, w_attn, w_out, tile):
    T = x.shape[0]

    def body(x_ref, y_ref, att_ref, ga_ref, gs_ref, g2_ref, g3_ref, wg_ref, ws_ref, wa_ref, wo_ref, x1_ref, o_ref, h2_ref):
        merged = _merge_forward(y_ref[...], att_ref[...], ga_ref[...], gs_ref[...], wg_ref[...], ws_ref[...], wa_ref[...])[-1]
        o = _dot(merged, wo_ref[...])
        x1 = x_ref[...] + o * _rms_scale(o) * g2_ref[...]
        o_ref[...] = o
        x1_ref[...] = x1
        h2_ref[...] = (x1 * _rms_scale(x1) * g3_ref[...]).astype(BF16)

    tok = lambda w: pl.BlockSpec((tile, w), lambda i: (i, 0))
    vec = _const_spec((1, D_MODEL))
    return pl.pallas_call(
        body, name="merge_fwd", grid=(T // tile,),
        in_specs=[tok(D_MODEL), tok(SSM_W), tok(ATTN_W), tok(D_MODEL), tok(D_MODEL), vec, vec,
                  _const_spec((SSM_W, SSM_W)), _const_spec((SSM_W, D_MODEL)), _const_spec((ATTN_W, D_MODEL)),
                  _const_spec((D_MODEL, D_MODEL))],
        out_specs=[tok(D_MODEL), tok(D_MODEL), tok(D_MODEL)],
        out_shape=[jax.ShapeDtypeStruct((T, D_MODEL), F32), jax.ShapeDtypeStruct((T, D_MODEL), F32),
                   jax.ShapeDtypeStruct((T, D_MODEL), BF16)],
        compiler_params=_params(("arbitrary",), VMEM_MID),
    )(*_in_hbm(x, y, att, ga, gs, g2, g3, w_glu, w_ssm, w_attn, w_out))


def _merge_bwd(dh2, dx2, x1, o, y, att, ga, gs, g2, g3, w_glu, w_ssm, w_attn, w_out, tile, carry=None):
    T = x1.shape[0]
    n_steps = T // tile

    def body(dh2_ref, dx2_ref, x1_ref, o_ref, y_ref, att_ref, ga_ref, gs_ref, g2_ref, g3_ref, wg_ref, ws_ref, wa_ref, wo_ref,
             dx1_ref, dgates_ref, datt_ref, dy_ref, dwg_hbm, dws_hbm, dwa_hbm, dwo_hbm, dg2_ref, dg3_ref,
             awg_ref, aws_ref, awa_ref, awo_ref):
        i = pl.program_id(0)
        x1v, ov = x1_ref[...], o_ref[...]
        dxn, dg3 = _rms_bwd(dh2_ref[...], x1v, _rms_scale(x1v), g3_ref[...])
        dx1 = dx2_ref[...] + dxn
        dx1_ref[...] = dx1
        do, dg2 = _rms_bwd(dx1, ov, _rms_scale(ov), g2_ref[...])
        dob = do.astype(BF16)

        yv = y_ref[...]
        att = att_ref[...]
        z, zb, gl, z2b, y_ssm, y_attn, sa, ss, merged = _merge_forward(
            yv, att, ga_ref[...], gs_ref[...], wg_ref[...], ws_ref[...], wa_ref[...])
        dmerged = _dot_nt(dob, wo_ref[...])
        dya = (dmerged * sa).astype(BF16)
        dys = (dmerged * ss).astype(BF16)
        dgates_ref[:, :D_MODEL] = (dmerged * y_attn * sa * (1.0 - sa)).astype(BF16)
        dgates_ref[:, D_MODEL:] = (dmerged * y_ssm * ss * (1.0 - ss)).astype(BF16)
        datt_ref[...] = _dot_nt(dya, wa_ref[...]).astype(BF16)
        dz2 = _dot_nt(dys, ws_ref[...])
        dpre = (dz2 * z * gl * (1.0 - gl)).astype(BF16)
        dz = dz2 * gl + _dot_nt(dpre, wg_ref[...])
        _, gelu_vjp = jax.vjp(jax.nn.gelu, yv)
        dy_ref[...] = gelu_vjp(dz)[0]

        grads = ((awo_ref, _dot_tn(merged, dob)), (awa_ref, _dot_tn(att, dya)),
                 (aws_ref, _dot_tn(z2b, dys)), (awg_ref, _dot_tn(zb, dpre)), (dg2_ref, dg2), (dg3_ref, dg3))

        @pl.when(i == 0)
        def _():
            for ref, val in grads:
                ref[...] = val

        @pl.when(i > 0)
        def _():
            for ref, val in grads:
                ref[...] += val

        @pl.when(i == n_steps - 1)
        def _():
            pltpu.sync_copy(awg_ref, dwg_hbm)
            pltpu.sync_copy(aws_ref, dws_hbm)
            pltpu.sync_copy(awa_ref, dwa_hbm)
            pltpu.sync_copy(awo_ref, dwo_hbm)

    tok = lambda w: pl.BlockSpec((tile, w), lambda i: (i, 0))
    vec = _const_spec((1, D_MODEL))
    any_ = pl.BlockSpec(memory_space=pl.ANY)
    vec_out = pl.BlockSpec((1, D_MODEL), lambda i: (0, 0))
    f32 = lambda *s: jax.ShapeDtypeStruct(s, F32)
    bf = lambda *s: jax.ShapeDtypeStruct(s, BF16)
    return _hosted_call(
        body, carry, _edge_1d(n_steps), name="merge_bwd", grid=(n_steps,),
        in_specs=[tok(D_MODEL), tok(D_MODEL), tok(D_MODEL), tok(D_MODEL), tok(SSM_W), tok(ATTN_W), tok(D_MODEL), tok(D_MODEL),
                  vec, vec, _const_spec((SSM_W, SSM_W)), _const_spec((SSM_W, D_MODEL)), _const_spec((ATTN_W, D_MODEL)),
                  _const_spec((D_MODEL, D_MODEL))],
        out_specs=[tok(D_MODEL), tok(2 * D_MODEL), tok(ATTN_W), tok(SSM_W), any_, any_, any_, any_, vec_out, vec_out],
        out_shape=[f32(T, D_MODEL), bf(T, 2 * D_MODEL), bf(T, ATTN_W), f32(T, SSM_W),
                   f32(SSM_W, SSM_W), f32(SSM_W, D_MODEL), f32(ATTN_W, D_MODEL), f32(D_MODEL, D_MODEL),
                   f32(1, D_MODEL), f32(1, D_MODEL)],
        scratch_shapes=[pltpu.VMEM((SSM_W, SSM_W), F32), pltpu.VMEM((SSM_W, D_MODEL), F32),
                        pltpu.VMEM((ATTN_W, D_MODEL), F32), pltpu.VMEM((D_MODEL, D_MODEL), F32)],
        compiler_params=_params(("arbitrary",), VMEM_BIG),
        inputs=(dh2, dx2, x1, o, y, att, ga, gs, g2, g3, w_glu, w_ssm, w_attn, w_out))


FF_SHARD = D_FF // N_DEV


def _mlp_fwd(h2, x1, target, g4, w_ff_in, w_ff_out, tile):
    T = h2.shape[0]
    col_chunk = 2 * FF_SHARD

    def body(h2_ref, x1_ref, tg_ref, g4_ref, wi_ref, wo_ref, a_ref, dfo_ref, dx2_ref, loss_ref, dg4_ref, rr_ref):
        i = pl.program_id(0)
        h2v = h2_ref[...]
        for c in range(D_FF // col_chunk):
            cols = slice(c * col_chunk, (c + 1) * col_chunk)
            a = _dot_nt(h2v, wi_ref[cols, :])
            a_ref[:, cols] = a.astype(BF16)
            ra = jnp.maximum(a, 0.0)
            rr_ref[:, cols] = (ra * ra).astype(BF16)
        f = _dot(rr_ref[...], wo_ref[...])
        r = _rms_scale(f)
        g = g4_ref[...]
        err = x1_ref[...] + f * r * g - tg_ref[...]
        dx2 = err * (1.0 / D_MODEL)
        dx2_ref[...] = dx2
        dfo, dg = _rms_bwd(dx2, f, r, g)
        dfo_ref[...] = dfo.astype(BF16)
        row = lax.broadcasted_iota(jnp.int32, (8, LANES), 0)
        col = lax.broadcasted_iota(jnp.int32, (8, LANES), 1)
        loss = jnp.where((row == 0) & (col == 0), (0.5 / D_MODEL) * jnp.sum(err * err), 0.0)

        @pl.when(i == 0)
        def _():
            loss_ref[...] = loss
            dg4_ref[...] = dg

        @pl.when(i > 0)
        def _():
            loss_ref[...] += loss
            dg4_ref[...] += dg

    tok = pl.BlockSpec((tile, D_MODEL), lambda i: (i, 0))
    return pl.pallas_call(
        body, name="mlp_fwd", grid=(T // tile,),
        in_specs=[tok, tok, tok, _const_spec((1, D_MODEL)), _const_spec((D_FF, D_MODEL)), _const_spec((D_FF, D_MODEL))],
        out_specs=[pl.BlockSpec((tile, D_FF), lambda i: (i, 0)), tok, tok,
                   pl.BlockSpec((8, LANES), lambda i: (0, 0)), pl.BlockSpec((1, D_MODEL), lambda i: (0, 0))],
        out_shape=[jax.ShapeDtypeStruct((T, D_FF), BF16), jax.ShapeDtypeStruct((T, D_MODEL), BF16),
                   jax.ShapeDtypeStruct((T, D_MODEL), F32), jax.ShapeDtypeStruct((8, LANES), F32),
                   jax.ShapeDtypeStruct((1, D_MODEL), F32)],
        scratch_shapes=[pltpu.VMEM((tile, D_FF), BF16)],
        compiler_params=_params(("arbitrary",), VMEM_MAX),
    )(*_in_hbm(h2, x1, target, g4, w_ff_in.reshape(D_FF, D_MODEL), w_ff_out.reshape(D_FF, D_MODEL)))


def _mlp_weight_grads(dfo, a, h2, w_ff_out, row_chunk):
    T = h2.shape[0]

    def body(dfo_ref, h2_ref, a_ref, wo_ref, dwi_ref, dwo_ref, da_ref, rr_ref):
        def rows(r, _):
            sl = pl.ds(pl.multiple_of(r * row_chunk, row_chunk), row_chunk)
            ra = jnp.maximum(a_ref[sl, :].astype(F32), 0.0)
            da_ref[sl, :] = (_dot_nt(dfo_ref[sl, :], wo_ref[0]) * (2.0 * ra)).astype(BF16)
            rr_ref[sl, :] = (ra * ra).astype(BF16)
            return 0

        lax.fori_loop(0, T // row_chunk, rows, 0)
        dwo_ref[0] = _dot_tn(rr_ref[...], dfo_ref[...])
        dwi_ref[0] = _dot_tn(h2_ref[...], da_ref[...])

    return pl.pallas_call(
        body, name="mlp_weight_grads", grid=(N_DEV,),
        in_specs=[_const_spec((T, D_MODEL)), _const_spec((T, D_MODEL)), pl.BlockSpec((T, FF_SHARD), lambda k: (0, k)),
                  pl.BlockSpec((1, FF_SHARD, D_MODEL), lambda k: (k, 0, 0))],
        out_specs=[pl.BlockSpec((1, D_MODEL, FF_SHARD), lambda k: (k, 0, 0)),
                   pl.BlockSpec((1, FF_SHARD, D_MODEL), lambda k: (k, 0, 0)), pl.BlockSpec((T, FF_SHARD), lambda k: (0, k))],
        out_shape=[jax.ShapeDtypeStruct((N_DEV, D_MODEL, FF_SHARD), F32), jax.ShapeDtypeStruct((N_DEV, FF_SHARD, D_MODEL), F32),
                   jax.ShapeDtypeStruct((T, D_FF), BF16)],
        scratch_shapes=[pltpu.VMEM((T, FF_SHARD), BF16)],
        compiler_params=_params(("arbitrary",), VMEM_MAX),
    )(*_in_hbm(dfo, h2, a, w_ff_out))


def _mlp_input_grad(da, w_ff_in_t, tile):
    T = da.shape[0]

    def body(da_ref, w_ref, o_ref):
        o_ref[...] = _dot(da_ref[...], w_ref[...])

    return pl.pallas_call(
        body, name="mlp_input_grad", grid=(T // tile,),
        in_specs=[pl.BlockSpec((tile, D_FF), lambda i: (i, 0)), _const_spec((D_FF, D_MODEL))],
        out_specs=pl.BlockSpec((tile, D_MODEL), lambda i: (i, 0)),
        out_shape=jax.ShapeDtypeStruct((T, D_MODEL), F32),
        compiler_params=_params(("arbitrary",), VMEM_MID),
    )(*_in_hbm(da, w_ff_in_t))


def _block_diag_in(b):
    bt = b.reshape(N_SSM_BLOCKS, 8, GROUP_CH, N_STATE)
    eye = jnp.eye(8, dtype=b.dtype)
    return jnp.einsum("jacp,ab->jacbp", bt, eye).reshape(N_SSM_BLOCKS, LANES, SSM_LANE_BLOCK)


def _block_diag_in_grad(g):
    g = g.reshape(N_SSM_BLOCKS, 8, GROUP_CH, 8, N_STATE)
    d = jnp.diagonal(g, axis1=1, axis2=3)
    return jnp.transpose(d, (0, 3, 1, 2)).reshape(N_GROUPS, GROUP_CH, N_STATE)


def _block_diag_out(c):
    ct = c.reshape(N_SSM_BLOCKS, 8, GROUP_CH, N_STATE)
    eye = jnp.eye(8, dtype=c.dtype)
    return jnp.einsum("jacp,ab->japbc", ct, eye).reshape(N_SSM_BLOCKS, SSM_LANE_BLOCK, LANES)


def _block_diag_out_grad(g):
    g = g.reshape(N_SSM_BLOCKS, 8, N_STATE, 8, GROUP_CH)
    d = jnp.diagonal(g, axis1=1, axis2=3)
    return jnp.transpose(d, (0, 3, 2, 1)).reshape(N_GROUPS, GROUP_CH, N_STATE)


def _tiles(T):
    return dict(proj=min(512, T), proj_bwd=min(512, T // 2), merge=min(512, T), merge_bwd=min(256, T),
                mlp_fwd=min(512, T), mlp_bwd=min(512, T), ssm_chunk=min(1024, T))


def _mesh_position():
    x, y, c = lax.axis_index("x"), lax.axis_index("y"), lax.axis_index("c")
    other_chips = [(1 - x, y), (x, 1 - y), (1 - x, 1 - y)]
    return x, y, c, other_chips


def _gather_carry(arrays):
    n = len(arrays)

    def copies(ins, outs, sems):
        send_sems, recv_sems, local_sems = sems
        x, y, c, chips = _mesh_position()
        me, sibling = (x, y, c), (x, y, 1 - c)

        def copy(a, k, block, to, src=None):
            px, py, pc = block
            dst = outs[a].at[4 * px + 2 * py + pc]
            return pltpu.make_async_remote_copy(
                src_ref=dst if src is None else src, dst_ref=dst, send_sem=send_sems.at[7 * a + k],
                recv_sem=recv_sems.at[7 * a + k], device_id=to, device_id_type=MESH_IDS)

        mine = [pltpu.make_async_copy(ins[a], outs[a].at[4 * x + 2 * y + c], local_sems.at[a]) for a in range(n)]
        first = []
        for a in range(n):
            first.append(copy(a, 0, me, sibling, src=ins[a]))
            first += [copy(a, 1 + j, me, (*chip, c), src=ins[a]) for j, chip in enumerate(chips)]
        return copy, mine, first, me, sibling, chips, c

    def start(ins, outs, sems):
        _, mine, first, *_ = copies(ins, outs, sems)
        for cp in mine + first:
            cp.start()

    def finish(ins, outs, sems):
        copy, mine, first, me, sibling, chips, c = copies(ins, outs, sems)
        passed = []
        for a in range(n):
            for j, chip in enumerate(chips):
                copy(a, 1 + j, (*chip, c), me).wait_recv()
                passed.append(copy(a, 4 + j, (*chip, c), sibling))
                passed[-1].start()
        for a in range(n):
            copy(a, 0, sibling, me).wait_recv()
            for j, chip in enumerate(chips):
                copy(a, 4 + j, (*chip, 1 - c), me).wait_recv()
        for cp in first + passed:
            cp.wait_send()
        for cp in mine:
            cp.wait()

    return _Carry(arrays, [jax.ShapeDtypeStruct((N_DEV,) + a.shape, a.dtype) for a in arrays],
                  [pltpu.SemaphoreType.DMA((7 * n,)), pltpu.SemaphoreType.DMA((7 * n,)), pltpu.SemaphoreType.DMA((n,))],
                  start, finish)


def _pairwise_carry(arrays, n_slots, make_copies):
    n = len(arrays)

    def start(ins, outs, sems):
        for cp in make_copies(ins, outs, sems):
            cp.start()

    def finish(ins, outs, sems):
        for cp in make_copies(ins, outs, sems):
            cp.wait()

    return _Carry(arrays, [jax.ShapeDtypeStruct((n_slots,) + a.shape[1:], a.dtype) for a in arrays],
                  [pltpu.SemaphoreType.DMA((n_slots * n,)), pltpu.SemaphoreType.DMA((n_slots * n,))], start, finish)


def _sibling_carry(grads):
    def make_copies(ins, outs, sems):
        x, y, c, _ = _mesh_position()
        return [pltpu.make_async_remote_copy(
            src_ref=ins[a].at[2 * ch + (1 - c)], dst_ref=outs[a].at[ch], send_sem=sems[0].at[4 * a + ch],
            recv_sem=sems[1].at[4 * a + ch], device_id=(x, y, 1 - c), device_id_type=MESH_IDS)
            for a in range(len(grads)) for ch in range(4)]

    return _pairwise_carry(grads, 4, make_copies)


def _chips_carry(sums):
    def make_copies(ins, outs, sems):
        x, y, c, chips = _mesh_position()
        return [pltpu.make_async_remote_copy(
            src_ref=ins[a].at[2 * px + py], dst_ref=outs[a].at[j], send_sem=sems[0].at[3 * a + j],
            recv_sem=sems[1].at[3 * a + j], device_id=(px, py, c), device_id_type=MESH_IDS)
            for a in range(len(sums)) for j, (px, py) in enumerate(chips)]

    return _pairwise_carry(sums, 3, make_copies)


def _row_tile(rows, cols):
    t = max(8, min(rows, (1 << 18) // cols // 8 * 8))
    while rows % t:
        t -= 8
    return t


def _add_sibling(grads8, recv, core, name):
    _, R, C = grads8.shape
    tr = _row_tile(R, C)
    g4 = grads8.reshape(4, 2, R, C)

    def body(core_ref, g_ref, r_ref, o_ref, ob_ref):
        s = g_ref[0] + r_ref[...]
        o_ref[...] = s
        ob_ref[...] = s.astype(BF16)

    out = pl.BlockSpec((1, tr, C), lambda ch, r, core_ref: (ch, r, 0))
    return pl.pallas_call(
        body, name=name,
        grid_spec=pltpu.PrefetchScalarGridSpec(
            num_scalar_prefetch=1, grid=(4, R // tr),
            in_specs=[pl.BlockSpec((1, 1, tr, C), lambda ch, r, core_ref: (ch, core_ref[0], r, 0)),
                      pl.BlockSpec((1, tr, C), lambda ch, r, core_ref: (ch, r, 0))],
            out_specs=[out, out]),
        out_shape=[jax.ShapeDtypeStruct((4, R, C), F32), jax.ShapeDtypeStruct((4, R, C), BF16)],
        compiler_params=_params(("arbitrary", "arbitrary")),
    )(core, *_in_hbm(g4, recv))


def _adam_math(w, g, m, v):
    m = ADAM_B1 * m + (1.0 - ADAM_B1) * g
    v = ADAM_B2 * v + (1.0 - ADAM_B2) * jnp.square(g)
    m_hat = m / (1.0 - ADAM_B1 ** ADAM_STEP)
    v_hat = v / (1.0 - ADAM_B2 ** ADAM_STEP)
    delta = -ADAM_LR * (m_hat / (jnp.sqrt(v_hat) + ADAM_EPS) + ADAM_WD * w)
    return delta, m, v


def _adam_big(w, m, v, chip_sums, recv, chip, name):
    R, C = w.shape
    tr = _row_tile(R, C)

    def body(chip_ref, w_ref, m_ref, v_ref, s_ref, r_ref, g_ref, d_ref, nm_ref, nv_ref):
        g = s_ref[0] + r_ref[0].astype(F32) + r_ref[1].astype(F32) + r_ref[2].astype(F32)
        g_ref[...] = g
        d_ref[...], nm_ref[...], nv_ref[...] = _adam_math(w_ref[...], g, m_ref[...], v_ref[...])

    blk = pl.BlockSpec((tr, C), lambda r, chip_ref: (r, 0))
    return pl.pallas_call(
        body, name=name,
        grid_spec=pltpu.PrefetchScalarGridSpec(
            num_scalar_prefetch=1, grid=(R // tr,),
            in_specs=[blk, blk, blk, pl.BlockSpec((1, tr, C), lambda r, chip_ref: (chip_ref[0], r, 0)),
                      pl.BlockSpec((3, tr, C), lambda r, chip_ref: (0, r, 0))],
            out_specs=[blk] * 4),
        out_shape=[jax.ShapeDtypeStruct((R, C), F32)] * 4,
        compiler_params=_params(("arbitrary",)),
    )(chip, *_in_hbm(w, m, v, chip_sums, recv))


def _sum_partials(partials, name):
    def body(p_ref, g_ref):
        g = p_ref[0]
        for d in range(1, partials.shape[0]):
            g = g + p_ref[d]
        g_ref[...] = g

    return pl.pallas_call(body, name=name, grid=(1,), in_specs=[_whole(partials.shape)], out_specs=_whole(partials.shape[1:]),
                          out_shape=jax.ShapeDtypeStruct(partials.shape[1:], F32))(*_in_hbm(partials))


def _adam_small(ws, ms, vs, gs):
    n = len(ws)

    def body(*refs):
        w_refs, m_refs, v_refs, g_refs = (refs[i * n:(i + 1) * n] for i in range(4))
        d_refs, nm_refs, nv_refs = (refs[(4 + i) * n:(5 + i) * n] for i in range(3))
        for j in range(n):
            d_refs[j][...], nm_refs[j][...], nv_refs[j][...] = _adam_math(
                w_refs[j][...], g_refs[j][...], m_refs[j][...], v_refs[j][...])

    specs = [_whole(w.shape) for w in ws]
    outs = pl.pallas_call(body, name="adam_small", grid=(1,), in_specs=specs * 4, out_specs=specs * 3,
                          out_shape=[jax.ShapeDtypeStruct(w.shape, F32) for w in ws] * 3,
                          compiler_params=_params(("arbitrary",), VMEM_MID))(*_in_hbm(*ws, *ms, *vs, *gs))
    return outs[:n], outs[n:2 * n], outs[2 * n:]


PACK_QUANTUM = SUBLANES * LANES


def _pack(named, names):
    parts = []
    for nme in names:
        flat = named[nme].reshape(-1)
        parts.append(jnp.pad(flat, (0, -flat.size % PACK_QUANTUM)))
    return jnp.concatenate(parts).reshape(-1, LANES)


def _unpack(packed, shapes, names):
    flat = packed.reshape(-1)
    out, pos = {}, 0
    for nme in names:
        size = math.prod(shapes[nme])
        out[nme] = flat[pos:pos + size].reshape(shapes[nme])
        pos += size + (-size % PACK_QUANTUM)
    return out


BIG = ("w_in", "w_glu", "w_attn_branch", "w_ssm_branch", "w_out", "w_ff_in", "w_ff_out")
COLUMN_SHARDED = ("w_in", "w_attn_branch", "w_ssm_branch", "w_ff_in")
SMALL = ("norm_mix_pre", "norm_mix_post", "norm_mlp_pre", "norm_mlp_post", "rel_bias", "sinks", "lam_re", "lam_im",
         "log_dt", "b_re", "b_im", "c_re", "c_im", "d_skip")
SWAPPED_SMALL = ("rel_bias", "b_re", "b_im")
SMALL_LATE = ("rel_bias", "sinks", "loss")
SMALL_BEFORE_ATTN_BWD = tuple(n for n in SMALL if n not in SMALL_LATE + ("norm_mix_pre",))
ALL_WEIGHTS = ("norm_mix_pre", "norm_mix_post", "norm_mlp_pre", "norm_mlp_post", "w_in", "rel_bias", "sinks", "lam_re",
               "lam_im", "log_dt", "b_re", "b_im", "c_re", "c_im", "d_skip", "w_glu", "w_attn_branch", "w_ssm_branch",
               "w_out", "w_ff_in", "w_ff_out")


def _full_from_gathered(name, gathered):
    _, r, c = gathered.shape
    if name in COLUMN_SHARDED:
        return jnp.transpose(gathered, (1, 0, 2)).reshape(r, N_DEV * c)
    return gathered.reshape(N_DEV * r, c)


def _blocks_from_full(name, full):
    r, c = full.shape
    if name in COLUMN_SHARDED:
        return jnp.transpose(full.reshape(r, N_DEV, c // N_DEV), (1, 0, 2))
    return full.reshape(N_DEV, r // N_DEV, c)


def kernel(x, norm_mix_pre, norm_mix_post, norm_mlp_pre, norm_mlp_post, w_in, rel_bias, sinks, lam_re, lam_im, log_dt, b_re, b_im, c_re, c_im, d_skip, w_glu, w_attn_branch, w_ssm_branch, w_out, w_ff_in, w_ff_out, loss_target, m_norm_mix_pre, m_norm_mix_post, m_norm_mlp_pre, m_norm_mlp_post, m_w_in, m_rel_bias, m_sinks, m_lam_re, m_lam_im, m_log_dt, m_b_re, m_b_im, m_c_re, m_c_im, m_d_skip, m_w_glu, m_w_attn_branch, m_w_ssm_branch, m_w_out, m_w_ff_in, m_w_ff_out, v_norm_mix_pre, v_norm_mix_post, v_norm_mlp_pre, v_norm_mlp_post, v_w_in, v_rel_bias, v_sinks, v_lam_re, v_lam_im, v_log_dt, v_b_re, v_b_im, v_c_re, v_c_im, v_d_skip, v_w_glu, v_w_attn_branch, v_w_ssm_branch, v_w_out, v_w_ff_in, v_w_ff_out):
    args = dict(locals())
    w = {n: args[n] for n in ALL_WEIGHTS}
    m = {n: args["m_" + n] for n in ALL_WEIGHTS}
    v = {n: args["v_" + n] for n in ALL_WEIGHTS}
    core = lax.axis_index("c").astype(jnp.int32).reshape(1)
    chip = (2 * lax.axis_index("x") + lax.axis_index("y")).astype(jnp.int32).reshape(1)
    xs, target = x[0], loss_target[0]
    t = _tiles(xs.shape[0])
    local = lambda d, n: d[n][0].T if n == "w_in" else d[n][0]
    shard = {n: local(w, n).astype(BF16) for n in BIG}
    shard["w_ff_in"] = shard["w_ff_in"].T
    view = lambda n, a: jnp.swapaxes(a, -1, -2) if n in SWAPPED_SMALL else a
    small = {n: (view(n, w[n]) if n == "rel_bias" else view(n, w[n])[0]) for n in SMALL}
    g1, g2, g3, g4 = (small[n].reshape(1, D_MODEL) for n in ("norm_mix_pre", "norm_mix_post", "norm_mlp_pre", "norm_mlp_post"))
    bucket = jnp.asarray(_bucket_table())
    rel_b, sink = small["rel_bias"], small["sinks"].reshape(1, N_HEADS)
    lam_r, lam_i = small["lam_re"].reshape(1, STATES), small["lam_im"].reshape(1, STATES)
    ldt_rep = jnp.repeat(small["log_dt"].reshape(N_GROUPS), N_STATE).reshape(1, STATES)
    bd_re, bd_im = _block_diag_in(small["b_re"]), _block_diag_in(small["b_im"])
    cm_re, cm_im = _block_diag_out(small["c_re"]).astype(BF16), _block_diag_out(small["c_im"]).astype(BF16)
    dsk = small["d_skip"].reshape(1, SSM_W)

    (g_in,) = _run_carry(_gather_carry([shard["w_in"]]), "gather_w_in")
    wf_in = g_in.reshape(IN_W, D_MODEL)
    merge_names = ("w_glu", "w_attn_branch", "w_ssm_branch", "w_out")
    (q, k, vv, u, ga, gs, h), gathered = _in_proj_fwd(xs, g1, wf_in, t["proj"], _gather_carry([shard[n] for n in merge_names]))
    wf = {n: _full_from_gathered(n, g) for n, g in zip(merge_names, gathered)}
    (att,), (wf_ff_in,) = _attn_fwd(q, k, vv, bucket, rel_b, sink, _gather_carry([shard["w_ff_in"]]))
    a_re, a_im, bm_re, bm_im = _ssm_prep(lam_r, lam_i, ldt_rep, bd_re, bd_im)
    (y, h_re, h_im, in_re, in_im), (wf_ff_out,) = _ssm_fwd(
        u, a_re, a_im, bm_re, bm_im, cm_re, cm_im, dsk, t["ssm_chunk"], _gather_carry([shard["w_ff_out"]]))
    x1, o, h2 = _merge_fwd(xs, y, att, ga, gs, g2, g3, wf["w_glu"], wf["w_ssm_branch"], wf["w_attn_branch"], wf["w_out"],
                           t["merge"])
    a, dfo, dx2, loss_blk, dg4 = _mlp_fwd(h2, x1, target, g4, wf_ff_in, wf_ff_out, t["mlp_fwd"])

    def add_sibling(names, blocks, received):
        pairs = [_add_sibling(b, r, core, "add_sibling_" + n) for n, b, r in zip(names, blocks, received)]
        return [p[0] for p in pairs], [p[1] for p in pairs]

    ff_names = ("w_ff_in", "w_ff_out")
    dw_ff_in, dw_ff_out, da = _mlp_weight_grads(dfo, a, h2, wf_ff_out, t["mlp_bwd"])
    dh2 = _mlp_input_grad(da, wf_ff_in.reshape(D_FF, D_MODEL), t["mlp_bwd"])
    ff_blocks = [dw_ff_in, dw_ff_out]
    (dx1, dgates, datt, dy, dw_glu, dw_ssm, dw_attn, dw_out, dg2, dg3), ff_recv = _merge_bwd(
        dh2, dx2, x1, o, y, att, ga, gs, g2, g3, wf["w_glu"], wf["w_ssm_branch"], wf["w_attn_branch"], wf["w_out"],
        t["merge_bwd"], _sibling_carry(ff_blocks))
    ff_sums, ff_sums_bf = add_sibling(ff_names, ff_blocks, ff_recv)
    merge_blocks = [_blocks_from_full(n, g) for n, g in zip(merge_names, (dw_glu, dw_attn, dw_ssm, dw_out))]
    (du, dbm_re, dbm_im, dcm_re, dcm_im, da_re, da_im, dd_skip), carried = _ssm_bwd(
        dy, u, h_re, h_im, in_re, in_im, a_re, a_im, bm_re, bm_im, cm_re, cm_im, dsk, t["ssm_chunk"],
        _join(_chips_carry(ff_sums_bf), _sibling_carry(merge_blocks)))
    ff_from_chips, merge_recv = carried[:2], carried[2:]
    merge_sums, merge_sums_bf = add_sibling(merge_names, merge_blocks, merge_recv)
    dbd_re, dbd_im, dlam_re, dlam_im, dldt_rep = _ssm_prep_bwd(lam_r, lam_i, ldt_rep, bd_re, bd_im, dbm_re, dbm_im, da_re, da_im)
    dlog_dt = _group_sum(dldt_rep.reshape(N_GROUPS, N_STATE))
    shapes = {n: view(n, w[n]).shape for n in SMALL}
    shapes["loss"] = (1,)
    small_grads = dict(
        norm_mix_post=dg2, norm_mlp_pre=dg3, norm_mlp_post=dg4, lam_re=dlam_re, lam_im=dlam_im, log_dt=dlog_dt,
        b_re=_block_diag_in_grad(dbd_re), b_im=_block_diag_in_grad(dbd_im),
        c_re=_block_diag_out_grad(dcm_re), c_im=_block_diag_out_grad(dcm_im), d_skip=dd_skip)
    packed_early = _pack({n: small_grads[n].reshape(shapes[n]) for n in SMALL_BEFORE_ATTN_BWD}, SMALL_BEFORE_ATTN_BWD)
    (dq, dkv, attn_small), carried = _attn_bwd(
        q, k, vv, datt, bucket, rel_b, sink, _join(_chips_carry(merge_sums_bf), _gather_carry([packed_early])))
    merge_from_chips, partials_early = carried[:-1], carried[-1]

    dparts = (dq, dkv, du, dgates)
    dw_in_t = _in_proj_weight_grad(h, dparts)
    in_blocks = [dw_in_t.reshape(N_DEV, IN_W // N_DEV, D_MODEL)]
    n_tiles = xs.shape[0] // t["proj_bwd"]
    (gx_a, dg1_a), in_recv = _in_proj_input_grad(
        xs, g1, wf_in, dx1, dparts, t["proj_bwd"], 0, n_tiles // 2, "in_proj_input_grad_a", _sibling_carry(in_blocks))
    in_sums, in_sums_bf = add_sibling(("w_in",), in_blocks, in_recv)
    late = dict(rel_bias=attn_small[:, :N_BUCKETS, 0], sinks=attn_small[:, N_BUCKETS, 0], loss=loss_blk[0:1, 0])
    packed_late = _pack({n: late[n].reshape(shapes[n]) for n in SMALL_LATE}, SMALL_LATE)
    (gx_b, dg1_b), (in_from_chips, partials_late) = _in_proj_input_grad(
        xs, g1, wf_in, dx1, dparts, t["proj_bwd"], n_tiles // 2, n_tiles - n_tiles // 2, "in_proj_input_grad_b",
        _join(_chips_carry(in_sums_bf), _gather_carry([packed_late])))
    grad_x = jnp.concatenate([gx_a, gx_b], axis=0)
    (dg1_partials,) = _run_carry(_gather_carry([jnp.concatenate([dg1_a, dg1_b], axis=0)]), "gather_norm_grad")

    grads, deltas, new_m, new_v = {}, {}, {}, {}
    sums = dict(zip(ff_names + merge_names + ("w_in",), ff_sums + merge_sums + in_sums))
    received = dict(zip(ff_names + merge_names + ("w_in",), ff_from_chips + merge_from_chips + [in_from_chips]))
    for n in BIG:
        outs = _adam_big(local(w, n), local(m, n), local(v, n), sums[n], received[n], chip, "adam_" + n)
        grads[n], deltas[n], new_m[n], new_v[n] = ((o.T if n == "w_in" else o)[None] for o in outs)

    grads.update(_unpack(_sum_partials(partials_early, "sum_small_grads"), shapes, SMALL_BEFORE_ATTN_BWD))
    grads.update(_unpack(_sum_partials(partials_late, "sum_late_grads"), shapes, SMALL_LATE))
    grads["norm_mix_pre"] = _sum_partials(dg1_partials.reshape(2 * N_DEV, 1, D_MODEL), "sum_norm_grad")
    loss = grads.pop("loss").reshape(())
    small_out = _adam_small(*[[view(n, d[n]) for n in SMALL] for d in (w, m, v)], [grads[n] for n in SMALL])
    for store, vals in zip((deltas, new_m, new_v), small_out):
        store.update(zip(SMALL, vals))
    for store in (grads, deltas, new_m, new_v):
        store.update({n: view(n, store[n]) for n in SWAPPED_SMALL})

    return (loss, grad_x[None], *[grads[n] for n in ALL_WEIGHTS], *[deltas[n] for n in ALL_WEIGHTS],
            *[new_m[n] for n in ALL_WEIGHTS], *[new_v[n] for n in ALL_WEIGHTS])
```

```python
import functools
import math

import jax
import jax.numpy as jnp
import numpy as np
from jax import lax
from jax.experimental import pallas as pl
from jax.experimental.pallas import tpu as pltpu

F32 = jnp.float32
BF16 = jnp.bfloat16

D_MODEL = 1024
N_HEADS = 8
HEAD_DIM = 64
ATTN_W = 512
KV_W = 128
BLOCK = 128
N_BUCKETS = 32
SSM_W = 512
N_GROUPS = 32
N_STATE = 64
GROUP_CH = 16
STATES = N_GROUPS * N_STATE
D_FF = 4096
IN_W = 3328
SPLITS = (0, 512, 640, 768, 1280, 2304, 3328)
RMS_EPS = 1e-6
NEG_INF = -1e30
SUBLANES = 8
LANES = 128
SSM_LANE_BLOCK = 512
N_SSM_BLOCKS = STATES // SSM_LANE_BLOCK
VMEM_BIG = 52 * 1024 * 1024
VMEM_MID = 40 * 1024 * 1024
VMEM_MAX = 60 * 1024 * 1024

ADAM_LR = 0.001
ADAM_B1 = 0.9
ADAM_B2 = 0.999
ADAM_EPS = 1e-08
ADAM_WD = 0.01
ADAM_STEP = 10

N_DEV = 8


def _dot(a, b):
    return jnp.dot(a, b, preferred_element_type=F32)


def _dot_nt(a, b):
    return lax.dot_general(a, b, (((1,), (1,)), ((), ())), preferred_element_type=F32)


def _dot_tn(a, b):
    return lax.dot_general(a, b, (((0,), (0,)), ((), ())), preferred_element_type=F32)


def _rms_scale(x):
    return lax.rsqrt(jnp.mean(x * x, axis=-1, keepdims=True) + RMS_EPS)


def _rms_bwd(dy, x, r, g):
    t = dy * g
    dx = r * t - x * (r * r * r) * jnp.mean(t * x, axis=-1, keepdims=True)
    dg = jnp.sum(dy * x * r, axis=0, keepdims=True)
    return dx, dg


def _const_spec(shape):
    nd = len(shape)
    return pl.BlockSpec(shape, lambda *_: (0,) * nd, pipeline_mode=pl.Buffered(1))


def _in_hbm(*arrays):
    return tuple(pltpu.with_memory_space_constraint(a, pltpu.HBM) for a in arrays)


def _hbm_out(shapes):
    if isinstance(shapes, (list, tuple)):
        return [_hbm_out(s) for s in shapes]
    return shapes if isinstance(shapes, pl.MemoryRef) else pltpu.HBM(shapes.shape, shapes.dtype)


def _whole(shape):
    nd = len(shape)
    return pl.BlockSpec(shape, lambda *_: (0,) * nd)


def _params(sem, vmem=None):
    return pltpu.CompilerParams(dimension_semantics=sem, vmem_limit_bytes=vmem)


MESH_IDS = pl.DeviceIdType.MESH
HBM_SPEC = pl.BlockSpec(memory_space=pl.ANY)


class _Carry:
    def __init__(self, inputs, out_shapes, sems, start, finish):
        self.inputs, self.out_shapes, self.sems, self.start, self.finish = list(inputs), list(out_shapes), list(sems), start, finish


def _join(a, b):
    na_in, na_out, na_sem = len(a.inputs), len(a.out_shapes), len(a.sems)

    def start(ins, outs, sems):
        a.start(ins[:na_in], outs[:na_out], sems[:na_sem])
        b.start(ins[na_in:], outs[na_out:], sems[na_sem:])

    def finish(ins, outs, sems):
        a.finish(ins[:na_in], outs[:na_out], sems[:na_sem])
        b.finish(ins[na_in:], outs[na_out:], sems[na_sem:])

    return _Carry(a.inputs + b.inputs, a.out_shapes + b.out_shapes, a.sems + b.sems, start, finish)


def _hosted_call(body, carry, edge, *, name, grid, in_specs, out_specs, out_shape, scratch_shapes, compiler_params, inputs):
    n_in, n_out = len(in_specs), len(out_specs)
    inputs = [a if s.memory_space == pltpu.SMEM else _in_hbm(a)[0] for a, s in zip(inputs, in_specs)]
    out_shape = _hbm_out(list(out_shape))
    if carry is None:
        outs = pl.pallas_call(body, name=name, grid=grid, in_specs=in_specs, out_specs=out_specs, out_shape=out_shape,
                              scratch_shapes=scratch_shapes, compiler_params=compiler_params)(*inputs)
        return list(outs), []
    c_in, c_out, c_sem = len(carry.inputs), len(carry.out_shapes), len(carry.sems)

    def wrapped(*refs):
        ins, refs = refs[:n_in], refs[n_in:]
        cins, refs = refs[:c_in], refs[c_in:]
        outs, refs = refs[:n_out], refs[n_out:]
        couts, refs = refs[:c_out], refs[c_out:]
        scratch, csems = refs[:len(refs) - c_sem], refs[len(refs) - c_sem:]
        first, last = edge()

        @pl.when(first)
        def _():
            carry.start(cins, couts, csems)

        body(*ins, *outs, *scratch)

        @pl.when(last)
        def _():
            carry.finish(cins, couts, csems)

    outs = pl.pallas_call(
        wrapped, name=name, grid=grid, in_specs=list(in_specs) + [HBM_SPEC] * c_in,
        out_specs=list(out_specs) + [HBM_SPEC] * c_out, out_shape=out_shape + _hbm_out(carry.out_shapes),
        scratch_shapes=list(scratch_shapes) + carry.sems, compiler_params=compiler_params)(*inputs, *_in_hbm(*carry.inputs))
    return list(outs[:n_out]), list(outs[n_out:])


def _edge_1d(n_steps):
    return lambda: (pl.program_id(0) == 0, pl.program_id(0) == n_steps - 1)


def _edge_2d(n0, n1):
    return lambda: ((pl.program_id(0) == 0) & (pl.program_id(1) == 0),
                    (pl.program_id(0) == n0 - 1) & (pl.program_id(1) == n1 - 1))


def _run_carry(carry, name):
    c_in, c_out = len(carry.inputs), len(carry.out_shapes)

    def body(*refs):
        ins, outs, sems = refs[:c_in], refs[c_in:c_in + c_out], refs[c_in + c_out:]
        carry.start(ins, outs, sems)
        carry.finish(ins, outs, sems)

    return pl.pallas_call(body, name=name, in_specs=[HBM_SPEC] * c_in, out_specs=[HBM_SPEC] * c_out,
                          out_shape=_hbm_out(carry.out_shapes), scratch_shapes=carry.sems)(*_in_hbm(*carry.inputs))


def _in_proj_fwd(x, g1, w_in_t, tile, carry=None):
    T = x.shape[0]

    def body(x_ref, g_ref, w_ref, q_ref, k_ref, v_ref, u_ref, ga_ref, gs_ref, h_ref):
        xv = x_ref[...]
        h = (xv * _rms_scale(xv) * g_ref[...]).astype(BF16)
        h_ref[...] = h
        outs = (q_ref, k_ref, v_ref, u_ref, ga_ref, gs_ref)
        for p, o_ref in enumerate(outs):
            o_ref[...] = _dot_nt(h, w_ref[SPLITS[p]:SPLITS[p + 1], :]).astype(o_ref.dtype)

    widths = [SPLITS[p + 1] - SPLITS[p] for p in range(6)] + [D_MODEL]
    dtypes = [BF16, BF16, BF16, F32, F32, F32, BF16]
    return _hosted_call(
        body, carry, _edge_1d(T // tile), name="in_proj_fwd", grid=(T // tile,),
        in_specs=[pl.BlockSpec((tile, D_MODEL), lambda i: (i, 0)), _const_spec((1, D_MODEL)), _const_spec((IN_W, D_MODEL))],
        out_specs=[pl.BlockSpec((tile, w), lambda i: (i, 0)) for w in widths],
        out_shape=[jax.ShapeDtypeStruct((T, w), dt) for w, dt in zip(widths, dtypes)],
        scratch_shapes=[], compiler_params=_params(("arbitrary",), VMEM_MID), inputs=(x, g1, w_in_t))


PROJ_PARTS = (512, 256, 512, 2048)
PROJ_GRAD_BLOCK = 256


def _in_proj_weight_grad(h, dparts):
    T = h.shape[0]
    blocks = [wd // PROJ_GRAD_BLOCK for wd in PROJ_PARTS]
    starts = [sum(blocks[:p]) for p in range(len(blocks))]

    def body(h_ref, *refs):
        part_refs, o_ref = refs[:-1], refs[-1]
        j = pl.program_id(0)
        for p_ref, start, count in zip(part_refs, starts, blocks):
            @pl.when((j >= start) & (j < start + count))
            def _(p_ref=p_ref):
                o_ref[...] = _dot_tn(p_ref[...], h_ref[...])

    def part_spec(start, count):
        return pl.BlockSpec((T, PROJ_GRAD_BLOCK), lambda j: (0, jnp.clip(j - start, 0, count - 1)))

    return pl.pallas_call(
        body, name="in_proj_weight_grad", grid=(sum(blocks),),
        in_specs=[_const_spec((T, D_MODEL))] + [part_spec(s, c) for s, c in zip(starts, blocks)],
        out_specs=pl.BlockSpec((PROJ_GRAD_BLOCK, D_MODEL), lambda j: (j, 0)),
        out_shape=_hbm_out(jax.ShapeDtypeStruct((IN_W, D_MODEL), F32)),
        compiler_params=_params(("arbitrary",), VMEM_MID),
    )(*_in_hbm(h, *dparts))


def _in_proj_input_grad(x, g1, w_in_t, dx1, dparts, tile, first_tile, n_tiles, name, carry=None):
    offsets = [sum(PROJ_PARTS[:p]) for p in range(len(PROJ_PARTS))]

    def body(x_ref, g_ref, w_ref, dx1_ref, *refs):
        part_refs, (gx_ref, dg_ref) = refs[:len(PROJ_PARTS)], refs[len(PROJ_PARTS):]
        i = pl.program_id(0)
        xv = x_ref[...]
        r = _rms_scale(xv)
        g = g_ref[...]
        dh = sum(_dot(p_ref[...], w_ref[off:off + wd, :]) for p_ref, off, wd in zip(part_refs, offsets, PROJ_PARTS))
        dxn, dg = _rms_bwd(dh, xv, r, g)
        gx_ref[...] = dx1_ref[...] + dxn

        @pl.when(i == 0)
        def _():
            dg_ref[...] = dg

        @pl.when(i > 0)
        def _():
            dg_ref[...] += dg

    tok = lambda wd: pl.BlockSpec((tile, wd), lambda i: (i + first_tile, 0))
    return _hosted_call(
        body, carry, _edge_1d(n_tiles), name=name, grid=(n_tiles,),
        in_specs=[tok(D_MODEL), _const_spec((1, D_MODEL)), _const_spec((IN_W, D_MODEL)), tok(D_MODEL)] + [tok(wd) for wd in PROJ_PARTS],
        out_specs=[pl.BlockSpec((tile, D_MODEL), lambda i: (i, 0)), pl.BlockSpec((1, D_MODEL), lambda i: (0, 0))],
        out_shape=[jax.ShapeDtypeStruct((n_tiles * tile, D_MODEL), F32), jax.ShapeDtypeStruct((1, D_MODEL), F32)],
        scratch_shapes=[], compiler_params=_params(("arbitrary",), VMEM_MID), inputs=(x, g1, w_in_t, dx1, *dparts))


def _bucket_table():
    qi = np.arange(BLOCK)[:, None]
    kj = np.arange(2 * BLOCK)[None, :]
    dist = qi + BLOCK - kj
    max_exact = N_BUCKETS // 2
    d = np.maximum(dist, 0)
    df = np.maximum(d, 1).astype(np.float32)
    large = max_exact + (np.log(df / np.float32(max_exact)) / np.float32(math.log(BLOCK / max_exact))
                         * np.float32(N_BUCKETS - max_exact)).astype(np.int32)
    large = np.minimum(large, N_BUCKETS - 1)
    bucket = np.where(d < max_exact, d, large)
    return np.where((dist >= 0) & (dist < BLOCK), bucket, -1).astype(np.int32)


def _build_bias(bucket_ref, rb_ref, bias_ref):
    bk = bucket_ref[...]
    for h in range(N_HEADS):
        def add(b, acc, h=h):
            return acc + jnp.where(bk == b, rb_ref[h, b], 0.0)
        bias_ref[h] = lax.fori_loop(0, N_BUCKETS, add, jnp.zeros((BLOCK, 2 * BLOCK), F32))


def _kv_variants(prev_ref, cur_ref):
    cat = jnp.concatenate([prev_ref[...], cur_ref[...]], axis=0)
    lo = lax.broadcasted_iota(jnp.int32, cat.shape, 1) < HEAD_DIM
    zero = jnp.zeros_like(cat)
    head0_lo = jnp.where(lo, cat, zero)
    head1_hi = jnp.where(lo, zero, cat)
    return ((head0_lo, pltpu.roll(head0_lo, HEAD_DIM, 1)), (pltpu.roll(head1_hi, HEAD_DIM, 1), head1_hi))


def _merge_kv_grads(g):
    lo = lax.broadcasted_iota(jnp.int32, g[0][0].shape, 1) < HEAD_DIM
    return jnp.where(lo, g[0][0] + pltpu.roll(g[0][1], HEAD_DIM, 1), g[1][1] + pltpu.roll(g[1][0], HEAD_DIM, 1))


def _head_lanes(h):
    return slice((h // 2) * LANES, (h // 2 + 1) * LANES)


def _attn_probs(q_ref, kvar, bias_ref, sk_ref, valid, s_ref):
    for h in range(N_HEADS):
        s_ref[h] = _dot_nt(q_ref[:, _head_lanes(h)], kvar[h // 4][h % 2])
    head = lax.broadcasted_iota(jnp.int32, (N_HEADS, 1, 1), 0)
    sink = jnp.zeros((N_HEADS, 1, 1), F32)
    for h in range(N_HEADS):
        sink = jnp.where(head == h, sk_ref[0, h], sink)
    s = jnp.where(valid[None], s_ref[...] * (HEAD_DIM ** -0.5) + bias_ref[...], NEG_INF)
    m = jnp.maximum(jnp.max(s, axis=-1, keepdims=True), sink)
    p = jnp.exp(s - m)
    e_sink = jnp.exp(sink - m)
    inv = 1.0 / (jnp.sum(p, axis=-1, keepdims=True) + e_sink)
    return p * inv, e_sink * inv


def _attn_valid(bucket_ref, n):
    col = lax.broadcasted_iota(jnp.int32, (BLOCK, 2 * BLOCK), 1)
    return (bucket_ref[...] >= 0) & ((n > 0) | (col >= BLOCK))


def _attn_fwd(q, k, v, bucket, rel_bias, sinks, carry=None):
    T = q.shape[0]
    nb = T // BLOCK

    def body(q_ref, kc_ref, kp_ref, vc_ref, vp_ref, bucket_ref, rb_ref, sk_ref, o_ref, bias_ref, s_ref, p_ref):
        n = pl.program_id(0)

        @pl.when(n == 0)
        def _():
            _build_bias(bucket_ref, rb_ref, bias_ref)

        kvar = _kv_variants(kp_ref, kc_ref)
        vvar = _kv_variants(vp_ref, vc_ref)
        pr, _ = _attn_probs(q_ref, kvar, bias_ref, sk_ref, _attn_valid(bucket_ref, n), s_ref)
        p_ref[...] = pr.astype(BF16)
        for m in range(N_HEADS // 2):
            acc = _dot(p_ref[2 * m], vvar[m // 2][0]) + _dot(p_ref[2 * m + 1], vvar[m // 2][1])
            o_ref[:, m * LANES:(m + 1) * LANES] = acc.astype(o_ref.dtype)

    cur = lambda w: pl.BlockSpec((BLOCK, w), lambda n: (n, 0))
    prev = lambda w: pl.BlockSpec((BLOCK, w), lambda n: (jnp.maximum(n - 1, 0), 0))
    smem = pl.BlockSpec(memory_space=pltpu.SMEM)
    return _hosted_call(
        body, carry, _edge_1d(nb), name="attn_fwd", grid=(nb,),
        in_specs=[cur(ATTN_W), cur(KV_W), prev(KV_W), cur(KV_W), prev(KV_W), _const_spec((BLOCK, 2 * BLOCK)), smem, smem],
        out_specs=[cur(ATTN_W)],
        out_shape=[jax.ShapeDtypeStruct((T, ATTN_W), BF16)],
        scratch_shapes=[pltpu.VMEM((N_HEADS, BLOCK, 2 * BLOCK), F32), pltpu.VMEM((N_HEADS, BLOCK, 2 * BLOCK), F32),
                        pltpu.VMEM((N_HEADS, BLOCK, 2 * BLOCK), BF16)],
        compiler_params=_params(("arbitrary",)), inputs=(q, k, k, v, v, bucket, rel_bias, sinks))


ATTN_SMALL_ROWS = N_BUCKETS + SUBLANES


def _attn_bwd(q, k, v, datt, bucket, rel_bias, sinks, carry=None):
    T = q.shape[0]
    nb = T // BLOCK

    def body(q_ref, do_ref, kc_ref, kp_ref, vc_ref, vp_ref, bucket_ref, rb_ref, sk_ref,
             dq_ref, dkv_ref, small_ref, bias_ref, ds_sum_ref, dsink_ref, kcarry_ref, vcarry_ref,
             s_ref, dp_ref, p_ref, dsc_ref):
        n = pl.program_id(0)

        @pl.when(n == 0)
        def _():
            _build_bias(bucket_ref, rb_ref, bias_ref)
            ds_sum_ref[...] = jnp.zeros_like(ds_sum_ref)
            dsink_ref[...] = jnp.zeros_like(dsink_ref)
            kcarry_ref[...] = jnp.zeros_like(kcarry_ref)
            vcarry_ref[...] = jnp.zeros_like(vcarry_ref)

        @pl.when(n < nb)
        def _():
            kvar = _kv_variants(kp_ref, kc_ref)
            vvar = _kv_variants(vp_ref, vc_ref)
            pr, p_sink = _attn_probs(q_ref, kvar, bias_ref, sk_ref, _attn_valid(bucket_ref, n), s_ref)
            for h in range(N_HEADS):
                dp_ref[h] = _dot_nt(do_ref[:, _head_lanes(h)], vvar[h // 4][h % 2])
            dp = dp_ref[...]
            dsum = jnp.sum(pr * dp, axis=-1, keepdims=True)
            ds = pr * (dp - dsum)
            ds_sum_ref[...] += ds
            dsink_ref[...] -= jnp.sum(p_sink * dsum, axis=1, keepdims=True)
            dsc_ref[...] = (ds * (HEAD_DIM ** -0.5)).astype(BF16)
            p_ref[...] = pr.astype(BF16)
            for m in range(N_HEADS // 2):
                dqm = _dot(dsc_ref[2 * m], kvar[m // 2][0]) + _dot(dsc_ref[2 * m + 1], kvar[m // 2][1])
                dq_ref[:, m * LANES:(m + 1) * LANES] = dqm.astype(dq_ref.dtype)
            dk_var = [[None, None], [None, None]]
            dv_var = [[None, None], [None, None]]
            for kvh in range(2):
                for e in range(2):
                    heads = [h for h in range(N_HEADS) if h // 4 == kvh and h % 2 == e]
                    dk_var[kvh][e] = sum(_dot_tn(dsc_ref[h], q_ref[:, _head_lanes(h)]) for h in heads)
                    dv_var[kvh][e] = sum(_dot_tn(p_ref[h], do_ref[:, _head_lanes(h)]) for h in heads)
            dk_cat = _merge_kv_grads(dk_var)
            dv_cat = _merge_kv_grads(dv_var)

            @pl.when(n > 0)
            def _():
                dkv_ref[:, :KV_W] = (kcarry_ref[...] + dk_cat[:BLOCK]).astype(BF16)
                dkv_ref[:, KV_W:] = (vcarry_ref[...] + dv_cat[:BLOCK]).astype(BF16)

            kcarry_ref[...] = dk_cat[BLOCK:]
            vcarry_ref[...] = dv_cat[BLOCK:]

        @pl.when(n == nb)
        def _():
            dkv_ref[:, :KV_W] = kcarry_ref[...].astype(BF16)
            dkv_ref[:, KV_W:] = vcarry_ref[...].astype(BF16)
            bk = bucket_ref[...]
            row = lax.broadcasted_iota(jnp.int32, (N_HEADS, ATTN_SMALL_ROWS, LANES), 1)

            def add(b, acc):
                masked = jnp.where((bk == b)[None], ds_sum_ref[...], 0.0)
                val = jnp.sum(jnp.sum(masked, axis=1, keepdims=True), axis=2, keepdims=True)
                return acc + jnp.where(row == b, val, 0.0)

            small_ref[...] = lax.fori_loop(0, N_BUCKETS, add, jnp.where(row == N_BUCKETS, dsink_ref[...], 0.0))

    last = nb - 1
    cur = lambda w: pl.BlockSpec((BLOCK, w), lambda n: (jnp.minimum(n, last), 0))
    prev = lambda w: pl.BlockSpec((BLOCK, w), lambda n: (jnp.clip(n - 1, 0, last), 0))
    smem = pl.BlockSpec(memory_space=pltpu.SMEM)
    return _hosted_call(
        body, carry, _edge_1d(nb + 1), name="attn_bwd", grid=(nb + 1,),
        in_specs=[cur(ATTN_W), cur(ATTN_W), cur(KV_W), prev(KV_W), cur(KV_W), prev(KV_W),
                  _const_spec((BLOCK, 2 * BLOCK)), smem, smem],
        out_specs=[cur(ATTN_W), prev(2 * KV_W), pl.BlockSpec((N_HEADS, ATTN_SMALL_ROWS, LANES), lambda n: (0, 0, 0))],
        out_shape=[jax.ShapeDtypeStruct((T, ATTN_W), BF16), jax.ShapeDtypeStruct((T, 2 * KV_W), BF16),
                   jax.ShapeDtypeStruct((N_HEADS, ATTN_SMALL_ROWS, LANES), F32)],
        scratch_shapes=[pltpu.VMEM((N_HEADS, BLOCK, 2 * BLOCK), F32), pltpu.VMEM((N_HEADS, BLOCK, 2 * BLOCK), F32),
                        pltpu.VMEM((N_HEADS, 1, 1), F32), pltpu.VMEM((BLOCK, KV_W), F32), pltpu.VMEM((BLOCK, KV_W), F32),
                        pltpu.VMEM((N_HEADS, BLOCK, 2 * BLOCK), F32), pltpu.VMEM((N_HEADS, BLOCK, 2 * BLOCK), F32),
                        pltpu.VMEM((N_HEADS, BLOCK, 2 * BLOCK), BF16), pltpu.VMEM((N_HEADS, BLOCK, 2 * BLOCK), BF16)],
        compiler_params=_params(("arbitrary",)), inputs=(q, datt, k, k, v, v, bucket, rel_bias, sinks))


SCAN_UNROLL = 4


def _cmul(ar, ai, br, bi):
    return ar * br - ai * bi, ar * bi + ai * br


def _cmul_conj(ar, ai, br, bi):
    return ar * br + ai * bi, ar * bi - ai * br


def _ssm_discretize(lr, li, ldt):
    dt = jnp.exp(ldt)
    mag = jnp.exp(lr * dt)
    ab_re = mag * jnp.cos(li * dt)
    ab_im = mag * jnp.sin(li * dt)
    nr = ab_re - 1.0
    den = lr * lr + li * li
    f_re = (nr * lr + ab_im * li) / den
    f_im = (ab_im * lr - nr * li) / den
    return ab_re, ab_im, f_re, f_im


def _ssm_prep(lam_re, lam_im, ldt_rep, bd_re, bd_im):
    def body(lr_ref, li_ref, ldt_ref, bdr_ref, bdi_ref, ar_ref, ai_ref, br_ref, bi_ref):
        ab_re, ab_im, f_re, f_im = _ssm_discretize(lr_ref[...], li_ref[...], ldt_ref[...])
        ar_ref[...] = ab_re
        ai_ref[...] = ab_im
        bdr, bdi = bdr_ref[0], bdi_ref[0]
        br_ref[0] = (bdr * f_re - bdi * f_im).astype(BF16)
        bi_ref[0] = (bdi * f_re + bdr * f_im).astype(BF16)

    row = pl.BlockSpec((1, SSM_LANE_BLOCK), lambda j: (0, j))
    mat = pl.BlockSpec((1, LANES, SSM_LANE_BLOCK), lambda j: (j, 0, 0))
    return pl.pallas_call(
        body, name="ssm_prep", grid=(N_SSM_BLOCKS,),
        in_specs=[row, row, row, mat, mat], out_specs=[row, row, mat, mat],
        out_shape=[jax.ShapeDtypeStruct((1, STATES), F32)] * 2 + [jax.ShapeDtypeStruct((N_SSM_BLOCKS, LANES, SSM_LANE_BLOCK), BF16)] * 2,
        compiler_params=_params(("arbitrary",)),
    )(*_in_hbm(lam_re, lam_im, ldt_rep, bd_re, bd_im))


def _ssm_prep_bwd(lam_re, lam_im, ldt_rep, bd_re, bd_im, dbr, dbi, da_re, da_im):
    def body(lr_ref, li_ref, ldt_ref, bdr_ref, bdi_ref, dbr_ref, dbi_ref, dar_ref, dai_ref,
             dbdr_ref, dbdi_ref, dlr_ref, dli_ref, dldt_ref):
        lr, li, ldt = lr_ref[...], li_ref[...], ldt_ref[...]
        (_, _, f_re, f_im), vjp = jax.vjp(_ssm_discretize, lr, li, ldt)
        bdr, bdi, gbr, gbi = bdr_ref[0], bdi_ref[0], dbr_ref[0], dbi_ref[0]
        dbdr_ref[0] = gbr * f_re + gbi * f_im
        dbdi_ref[0] = gbi * f_re - gbr * f_im
        df_re = jnp.sum(gbr * bdr + gbi * bdi, axis=0, keepdims=True)
        df_im = jnp.sum(gbi * bdr - gbr * bdi, axis=0, keepdims=True)
        dlr, dli, dldt = vjp((dar_ref[...], dai_ref[...], df_re, df_im))
        dlr_ref[...] = dlr
        dli_ref[...] = dli
        dldt_ref[...] = dldt

    row = pl.BlockSpec((1, SSM_LANE_BLOCK), lambda j: (0, j))
    mat = pl.BlockSpec((1, LANES, SSM_LANE_BLOCK), lambda j: (j, 0, 0))
    mat_shape = jax.ShapeDtypeStruct((N_SSM_BLOCKS, LANES, SSM_LANE_BLOCK), F32)
    row_shape = jax.ShapeDtypeStruct((1, STATES), F32)
    return pl.pallas_call(
        body, name="ssm_prep_bwd", grid=(N_SSM_BLOCKS,),
        in_specs=[row, row, row, mat, mat, mat, mat, row, row], out_specs=[mat, mat, row, row, row],
        out_shape=[mat_shape, mat_shape, row_shape, row_shape, row_shape],
        compiler_params=_params(("arbitrary",)),
    )(*_in_hbm(lam_re, lam_im, ldt_rep, bd_re, bd_im, dbr, dbi, da_re, da_im))


def _group_sum(x):
    def body(x_ref, o_ref):
        o_ref[...] = jnp.sum(x_ref[...], axis=1, keepdims=True)
    return pl.pallas_call(body, name="ssm_group_sum", grid=(1,), in_specs=[_whole(x.shape)], out_specs=_whole((N_GROUPS, 1)),
                          out_shape=jax.ShapeDtypeStruct((N_GROUPS, 1), F32))(*_in_hbm(x))


def _power_table(ar, ai, p_re_ref, p_im_ref, steps):
    shape = (SUBLANES, SSM_LANE_BLOCK)
    p_re_ref[0:SUBLANES] = jnp.broadcast_to(ar, shape)
    p_im_ref[0:SUBLANES] = jnp.broadcast_to(ai, shape)
    m = 1
    while m < steps:
        rows = m * SUBLANES
        top_re = p_re_ref[rows - SUBLANES:rows]
        top_im = p_im_ref[rows - SUBLANES:rows]
        cur_re = p_re_ref[0:rows].reshape(m, SUBLANES, SSM_LANE_BLOCK)
        cur_im = p_im_ref[0:rows].reshape(m, SUBLANES, SSM_LANE_BLOCK)
        nxt_re, nxt_im = _cmul(cur_re, cur_im, top_re[None], top_im[None])
        p_re_ref[rows:2 * rows] = nxt_re.reshape(rows, SSM_LANE_BLOCK)
        p_im_ref[rows:2 * rows] = nxt_im.reshape(rows, SSM_LANE_BLOCK)
        m *= 2


def _to_segments(src_ref, dst_ref, steps):
    for s in range(SUBLANES):
        dst_ref[pl.ds(s, steps, stride=SUBLANES), :] = src_ref[s * steps:(s + 1) * steps, :]


def _from_segments(src_ref, dst_ref, steps):
    for s in range(SUBLANES):
        dst_ref[s * steps:(s + 1) * steps, :] = src_ref[pl.ds(s, steps, stride=SUBLANES), :]


def _segment_carries(e_re, e_im, an_re, an_im, c_re, c_im, reverse):
    order = range(SUBLANES - 1, -1, -1) if reverse else range(SUBLANES)
    ins_re, ins_im = [None] * SUBLANES, [None] * SUBLANES
    for s in order:
        ins_re[s], ins_im[s] = c_re, c_im
        pr, pi = _cmul(an_re, an_im, c_re, c_im)
        c_re = e_re[s:s + 1] + pr
        c_im = e_im[s:s + 1] + pi
    return jnp.concatenate(ins_re, axis=0), jnp.concatenate(ins_im, axis=0), c_re, c_im


def _ssm_fwd(u, a_re, a_im, b_re, b_im, c_re, c_im, d_skip, chunk, carry=None):
    T = u.shape[0]
    nc = T // chunk
    steps = chunk // SUBLANES
    blk = SSM_LANE_BLOCK

    def body(u_ref, ar_ref, ai_ref, br_ref, bi_ref, cr_ref, ci_ref, dk_ref,
             y_ref, hr_ref, hi_ref, inr_ref, ini_ref, useg_ref, yseg_ref, pr_ref, pi_ref, carry_ref):
        c = pl.program_id(1)
        ar, ai = ar_ref[...], ai_ref[...]

        @pl.when(c == 0)
        def _():
            _power_table(ar, ai, pr_ref, pi_ref, steps)
            carry_ref[...] = jnp.zeros_like(carry_ref)

        _to_segments(u_ref, useg_ref, steps)
        ub = useg_ref[...].astype(BF16)
        hr_ref[...] = _dot(ub, br_ref[0])
        hi_ref[...] = _dot(ub, bi_ref[0])
        first = slice(0, SUBLANES)

        def scan(t4, prev):
            for j in range(SCAN_UNROLL):
                rows = pl.ds(pl.multiple_of((t4 * SCAN_UNROLL + j) * SUBLANES, SUBLANES), SUBLANES)
                pr, pi = _cmul(pr_ref[first, :], pi_ref[first, :], prev[0], prev[1])
                prev = (pr + hr_ref[rows, :], pi + hi_ref[rows, :])
                hr_ref[rows, :] = prev[0]
                hi_ref[rows, :] = prev[1]
            return prev

        zero = jnp.zeros((SUBLANES, blk), F32)
        lax.fori_loop(0, steps // SCAN_UNROLL, scan, (zero, zero))

        top = slice(chunk - SUBLANES, chunk)
        in_re, in_im, out_re, out_im = _segment_carries(
            hr_ref[top, :], hi_ref[top, :], pr_ref[top, :][0:1], pi_ref[top, :][0:1],
            carry_ref[0:1, :], carry_ref[1:2, :], reverse=False)
        carry_ref[0:1, :] = out_re
        carry_ref[1:2, :] = out_im
        inr_ref[...] = in_re
        ini_ref[...] = in_im

        def fix(t4, _):
            for j in range(SCAN_UNROLL):
                rows = pl.ds(pl.multiple_of((t4 * SCAN_UNROLL + j) * SUBLANES, SUBLANES), SUBLANES)
                fr, fi = _cmul(pr_ref[rows, :], pi_ref[rows, :], in_re, in_im)
                hr_ref[rows, :] += fr
                hi_ref[rows, :] += fi
            return 0

        lax.fori_loop(0, steps // SCAN_UNROLL, fix, 0)

        yseg_ref[...] = _dot(hr_ref[...].astype(BF16), cr_ref[0]) - _dot(hi_ref[...].astype(BF16), ci_ref[0])
        _from_segments(yseg_ref, y_ref, steps)
        y_ref[...] += dk_ref[...] * u_ref[...]

    row = pl.BlockSpec((1, blk), lambda j, c: (0, j))
    b_mat = pl.BlockSpec((1, LANES, blk), lambda j, c: (j, 0, 0))
    c_mat = pl.BlockSpec((1, blk, LANES), lambda j, c: (j, 0, 0))
    tok = pl.BlockSpec((chunk, LANES), lambda j, c: (c, j))
    state = pl.BlockSpec((chunk, blk), lambda j, c: (c, j))
    enter = pl.BlockSpec((SUBLANES, blk), lambda j, c: (c, j))
    return _hosted_call(
        body, carry, _edge_2d(N_SSM_BLOCKS, nc), name="ssm_fwd", grid=(N_SSM_BLOCKS, nc),
        in_specs=[tok, row, row, b_mat, b_mat, c_mat, c_mat, pl.BlockSpec((1, LANES), lambda j, c: (0, j))],
        out_specs=[tok, state, state, enter, enter],
        out_shape=[jax.ShapeDtypeStruct((T, SSM_W), F32), jax.ShapeDtypeStruct((T, STATES), F32),
                   jax.ShapeDtypeStruct((T, STATES), F32), jax.ShapeDtypeStruct((nc * SUBLANES, STATES), F32),
                   jax.ShapeDtypeStruct((nc * SUBLANES, STATES), F32)],
        scratch_shapes=[pltpu.VMEM((chunk, LANES), F32), pltpu.VMEM((chunk, LANES), F32),
                        pltpu.VMEM((chunk, blk), F32), pltpu.VMEM((chunk, blk), F32), pltpu.VMEM((SUBLANES, blk), F32)],
        compiler_params=_params(("arbitrary", "arbitrary"), VMEM_MID),
        inputs=(u, a_re, a_im, b_re, b_im, c_re, c_im, d_skip))


def _ssm_bwd(dy, u, h_re, h_im, in_re, in_im, a_re, a_im, b_re, b_im, c_re, c_im, d_skip, chunk, carry=None):
    T = u.shape[0]
    nc = T // chunk
    steps = chunk // SUBLANES
    blk = SSM_LANE_BLOCK

    def body(dy_ref, u_ref, hr_ref, hi_ref, inr_ref, ini_ref, ar_ref, ai_ref, br_ref, bi_ref, cr_ref, ci_ref, dk_ref,
             du_ref, dbr_ref, dbi_ref, dcr_ref, dci_ref, dar_ref, dai_ref, ddk_ref,
             dyseg_ref, useg_ref, duseg_ref, gr_ref, gi_ref, pr_ref, pi_ref, carry_ref, accr_ref, acci_ref):
        c = pl.program_id(1)
        ar, ai = ar_ref[...], ai_ref[...]

        @pl.when(c == 0)
        def _():
            _power_table(ar, ai, pr_ref, pi_ref, steps)
            carry_ref[...] = jnp.zeros_like(carry_ref)
            accr_ref[...] = jnp.zeros_like(accr_ref)
            acci_ref[...] = jnp.zeros_like(acci_ref)

        _to_segments(dy_ref, dyseg_ref, steps)
        _to_segments(u_ref, useg_ref, steps)
        dyb = dyseg_ref[...].astype(BF16)
        ub = useg_ref[...].astype(BF16)
        gr_ref[...] = _dot_nt(dyb, cr_ref[0])
        gi_ref[...] = -_dot_nt(dyb, ci_ref[0])
        dcr = _dot_tn(hr_ref[...].astype(BF16), dyb)
        dci = -_dot_tn(hi_ref[...].astype(BF16), dyb)
        ddk = jnp.sum(dy_ref[...] * u_ref[...], axis=0, keepdims=True)

        first = slice(0, SUBLANES)

        def scan(k4, nxt):
            for j in range(SCAN_UNROLL):
                t = steps - 1 - (k4 * SCAN_UNROLL + j)
                rows = pl.ds(pl.multiple_of(t * SUBLANES, SUBLANES), SUBLANES)
                pr, pi = _cmul_conj(pr_ref[first, :], pi_ref[first, :], nxt[0], nxt[1])
                nxt = (pr + gr_ref[rows, :], pi + gi_ref[rows, :])
                gr_ref[rows, :] = nxt[0]
                gi_ref[rows, :] = nxt[1]
            return nxt

        top = slice(chunk - SUBLANES, chunk)
        zero = jnp.zeros((SUBLANES, blk), F32)
        lax.fori_loop(0, steps // SCAN_UNROLL, scan, (zero, zero))

        gin_re, gin_im, out_re, out_im = _segment_carries(
            gr_ref[0:SUBLANES, :], gi_ref[0:SUBLANES, :], pr_ref[top, :][0:1], -pi_ref[top, :][0:1],
            carry_ref[0:1, :], carry_ref[1:2, :], reverse=True)
        carry_ref[0:1, :] = out_re
        carry_ref[1:2, :] = out_im

        def fix_row(rows, prow, hp_re, hp_im, acc):
            fr, fi = _cmul_conj(pr_ref[prow, :], pi_ref[prow, :], gin_re, gin_im)
            g_re = gr_ref[rows, :] + fr
            g_im = gi_ref[rows, :] + fi
            gr_ref[rows, :] = g_re
            gi_ref[rows, :] = g_im
            return acc[0] + g_re * hp_re + g_im * hp_im, acc[1] + g_im * hp_re - g_re * hp_im

        def fix_at(t, acc):
            aligned = (lambda r: r * SUBLANES) if isinstance(t, int) else (lambda r: pl.multiple_of(r * SUBLANES, SUBLANES))
            rows, before, prow = (pl.ds(aligned(r), SUBLANES) for r in (t, t - 1, steps - 1 - t))
            return fix_row(rows, prow, hr_ref[before, :], hi_ref[before, :], acc)

        def fix(t4, acc):
            for j in range(SCAN_UNROLL):
                acc = fix_at(t4 * SCAN_UNROLL + j, acc)
            return acc

        acc = fix_row(first, top, inr_ref[...], ini_ref[...], (accr_ref[...], acci_ref[...]))
        for t in range(1, SCAN_UNROLL):
            acc = fix_at(t, acc)
        acc_re, acc_im = lax.fori_loop(1, steps // SCAN_UNROLL, fix, acc)
        accr_ref[...] = acc_re
        acci_ref[...] = acc_im

        gbr = gr_ref[...].astype(BF16)
        gbi = gi_ref[...].astype(BF16)
        duseg_ref[...] = _dot_nt(gbr, br_ref[0]) + _dot_nt(gbi, bi_ref[0])
        _from_segments(duseg_ref, dyseg_ref, steps)
        du_ref[...] = (dyseg_ref[...] + dk_ref[...] * dy_ref[...]).astype(BF16)
        dbr = _dot_tn(ub, gbr)
        dbi = _dot_tn(ub, gbi)

        @pl.when(c == 0)
        def _():
            dbr_ref[0] = dbr
            dbi_ref[0] = dbi
            dcr_ref[0] = dcr
            dci_ref[0] = dci
            ddk_ref[...] = ddk

        @pl.when(c > 0)
        def _():
            dbr_ref[0] += dbr
            dbi_ref[0] += dbi
            dcr_ref[0] += dcr
            dci_ref[0] += dci
            ddk_ref[...] += ddk

        @pl.when(c == nc - 1)
        def _():
            dar_ref[...] = jnp.sum(acc_re, axis=0, keepdims=True)
            dai_ref[...] = jnp.sum(acc_im, axis=0, keepdims=True)

    rev = lambda c: nc - 1 - c
    row = pl.BlockSpec((1, blk), lambda j, c: (0, j))
    b_mat = pl.BlockSpec((1, LANES, blk), lambda j, c: (j, 0, 0))
    c_mat = pl.BlockSpec((1, blk, LANES), lambda j, c: (j, 0, 0))
    tok = pl.BlockSpec((chunk, LANES), lambda j, c: (rev(c), j))
    state = pl.BlockSpec((chunk, blk), lambda j, c: (rev(c), j))
    enter = pl.BlockSpec((SUBLANES, blk), lambda j, c: (rev(c), j))
    chan = pl.BlockSpec((1, LANES), lambda j, c: (0, j))
    f32 = lambda *s: jax.ShapeDtypeStruct(s, F32)
    return _hosted_call(
        body, carry, _edge_2d(N_SSM_BLOCKS, nc), name="ssm_bwd", grid=(N_SSM_BLOCKS, nc),
        in_specs=[tok, tok, state, state, enter, enter, row, row, b_mat, b_mat, c_mat, c_mat, chan],
        out_specs=[tok, b_mat, b_mat, c_mat, c_mat, row, row, chan],
        out_shape=[jax.ShapeDtypeStruct((T, SSM_W), BF16), f32(N_SSM_BLOCKS, LANES, blk), f32(N_SSM_BLOCKS, LANES, blk),
                   f32(N_SSM_BLOCKS, blk, LANES), f32(N_SSM_BLOCKS, blk, LANES), f32(1, STATES), f32(1, STATES), f32(1, SSM_W)],
        scratch_shapes=[pltpu.VMEM((chunk, LANES), F32), pltpu.VMEM((chunk, LANES), F32), pltpu.VMEM((chunk, LANES), F32),
                        pltpu.VMEM((chunk, blk), F32), pltpu.VMEM((chunk, blk), F32),
                        pltpu.VMEM((chunk, blk), F32), pltpu.VMEM((chunk, blk), F32),
                        pltpu.VMEM((SUBLANES, blk), F32), pltpu.VMEM((SUBLANES, blk), F32), pltpu.VMEM((SUBLANES, blk), F32)],
        compiler_params=_params(("arbitrary", "arbitrary"), VMEM_BIG),
        inputs=(dy, u, h_re, h_im, in_re, in_im, a_re, a_im, b_re, b_im, c_re, c_im, d_skip))


def _merge_forward(y, att, ga, gs, w_glu, w_ssm, w_attn):
    z = jax.nn.gelu(y)
    zb = z.astype(BF16)
    gl = jax.nn.sigmoid(_dot(zb, w_glu))
    z2b = (z * gl).astype(BF16)
    y_ssm = _dot(z2b, w_ssm)
    y_attn = _dot(att, w_attn)
    sa = jax.nn.sigmoid(ga)
    ss = jax.nn.sigmoid(gs)
    merged = (sa * y_attn + ss * y_ssm).astype(BF16)
    return z, zb, gl, z2b, y_ssm, y_attn, sa, ss, merged


def _merge_fwd(x, y, att, ga, gs, g2, g3, w_glu, w_ssm, w_attn, w_out, tile):
    T = x.shape[0]

    def body(x_ref, y_ref, att_ref, ga_ref, gs_ref, g2_ref, g3_ref, wg_ref, ws_ref, wa_ref, wo_ref, x1_ref, o_ref, h2_ref):
        merged = _merge_forward(y_ref[...], att_ref[...], ga_ref[...], gs_ref[...], wg_ref[...], ws_ref[...], wa_ref[...])[-1]
        o = _dot(merged, wo_ref[...])
        x1 = x_ref[...] + o * _rms_scale(o) * g2_ref[...]
        o_ref[...] = o
        x1_ref[...] = x1
        h2_ref[...] = (x1 * _rms_scale(x1) * g3_ref[...]).astype(BF16)

    tok = lambda w: pl.BlockSpec((tile, w), lambda i: (i, 0))
    vec = _const_spec((1, D_MODEL))
    return pl.pallas_call(
        body, name="merge_fwd", grid=(T // tile,),
        in_specs=[tok(D_MODEL), tok(SSM_W), tok(ATTN_W), tok(D_MODEL), tok(D_MODEL), vec, vec,
                  _const_spec((SSM_W, SSM_W)), _const_spec((SSM_W, D_MODEL)), _const_spec((ATTN_W, D_MODEL)),
                  _const_spec((D_MODEL, D_MODEL))],
        out_specs=[tok(D_MODEL), tok(D_MODEL), tok(D_MODEL)],
        out_shape=_hbm_out([jax.ShapeDtypeStruct((T, D_MODEL), F32), jax.ShapeDtypeStruct((T, D_MODEL), F32),
                            jax.ShapeDtypeStruct((T, D_MODEL), BF16)]),
        compiler_params=_params(("arbitrary",), VMEM_MID),
    )(*_in_hbm(x, y, att, ga, gs, g2, g3, w_glu, w_ssm, w_attn, w_out))


def _merge_bwd(dh2, dx2, x1, o, y, att, ga, gs, g2, g3, w_glu, w_ssm, w_attn, w_out, tile, carry=None):
    T = x1.shape[0]
    n_steps = T // tile

    def body(dh2_ref, dx2_ref, x1_ref, o_ref, y_ref, att_ref, ga_ref, gs_ref, g2_ref, g3_ref, wg_ref, ws_ref, wa_ref, wo_ref,
             dx1_ref, dgates_ref, datt_ref, dy_ref, dwg_hbm, dws_hbm, dwa_hbm, dwo_hbm, dg2_ref, dg3_ref,
             awg_ref, aws_ref, awa_ref, awo_ref):
        i = pl.program_id(0)
        x1v, ov = x1_ref[...], o_ref[...]
        dxn, dg3 = _rms_bwd(dh2_ref[...], x1v, _rms_scale(x1v), g3_ref[...])
        dx1 = dx2_ref[...] + dxn
        dx1_ref[...] = dx1
        do, dg2 = _rms_bwd(dx1, ov, _rms_scale(ov), g2_ref[...])
        dob = do.astype(BF16)

        yv = y_ref[...]
        att = att_ref[...]
        z, zb, gl, z2b, y_ssm, y_attn, sa, ss, merged = _merge_forward(
            yv, att, ga_ref[...], gs_ref[...], wg_ref[...], ws_ref[...], wa_ref[...])
        dmerged = _dot_nt(dob, wo_ref[...])
        dya = (dmerged * sa).astype(BF16)
        dys = (dmerged * ss).astype(BF16)
        dgates_ref[:, :D_MODEL] = (dmerged * y_attn * sa * (1.0 - sa)).astype(BF16)
        dgates_ref[:, D_MODEL:] = (dmerged * y_ssm * ss * (1.0 - ss)).astype(BF16)
        datt_ref[...] = _dot_nt(dya, wa_ref[...]).astype(BF16)
        dz2 = _dot_nt(dys, ws_ref[...])
        dpre = (dz2 * z * gl * (1.0 - gl)).astype(BF16)
        dz = dz2 * gl + _dot_nt(dpre, wg_ref[...])
        _, gelu_vjp = jax.vjp(jax.nn.gelu, yv)
        dy_ref[...] = gelu_vjp(dz)[0]

        grads = ((awo_ref, _dot_tn(merged, dob)), (awa_ref, _dot_tn(att, dya)),
                 (aws_ref, _dot_tn(z2b, dys)), (awg_ref, _dot_tn(zb, dpre)), (dg2_ref, dg2), (dg3_ref, dg3))

        @pl.when(i == 0)
        def _():
            for ref, val in grads:
                ref[...] = val

        @pl.when(i > 0)
        def _():
            for ref, val in grads:
                ref[...] += val

        @pl.when(i == n_steps - 1)
        def _():
            pltpu.sync_copy(awg_ref, dwg_hbm)
            pltpu.sync_copy(aws_ref, dws_hbm)
            pltpu.sync_copy(awa_ref, dwa_hbm)
            pltpu.sync_copy(awo_ref, dwo_hbm)

    tok = lambda w: pl.BlockSpec((tile, w), lambda i: (i, 0))
    vec = _const_spec((1, D_MODEL))
    any_ = pl.BlockSpec(memory_space=pl.ANY)
    vec_out = pl.BlockSpec((1, D_MODEL), lambda i: (0, 0))
    f32 = lambda *s: jax.ShapeDtypeStruct(s, F32)
    bf = lambda *s: jax.ShapeDtypeStruct(s, BF16)
    return _hosted_call(
        body, carry, _edge_1d(n_steps), name="merge_bwd", grid=(n_steps,),
        in_specs=[tok(D_MODEL), tok(D_MODEL), tok(D_MODEL), tok(D_MODEL), tok(SSM_W), tok(ATTN_W), tok(D_MODEL), tok(D_MODEL),
                  vec, vec, _const_spec((SSM_W, SSM_W)), _const_spec((SSM_W, D_MODEL)), _const_spec((ATTN_W, D_MODEL)),
                  _const_spec((D_MODEL, D_MODEL))],
        out_specs=[tok(D_MODEL), tok(2 * D_MODEL), tok(ATTN_W), tok(SSM_W), any_, any_, any_, any_, vec_out, vec_out],
        out_shape=[f32(T, D_MODEL), bf(T, 2 * D_MODEL), bf(T, ATTN_W), f32(T, SSM_W),
                   f32(SSM_W, SSM_W), f32(SSM_W, D_MODEL), f32(ATTN_W, D_MODEL), f32(D_MODEL, D_MODEL),
                   f32(1, D_MODEL), f32(1, D_MODEL)],
        scratch_shapes=[pltpu.VMEM((SSM_W, SSM_W), F32), pltpu.VMEM((SSM_W, D_MODEL), F32),
                        pltpu.VMEM((ATTN_W, D_MODEL), F32), pltpu.VMEM((D_MODEL, D_MODEL), F32)],
        compiler_params=_params(("arbitrary",), VMEM_BIG),
        inputs=(dh2, dx2, x1, o, y, att, ga, gs, g2, g3, w_glu, w_ssm, w_attn, w_out))


FF_SHARD = D_FF // N_DEV


def _mlp_fwd(h2, x1, target, g4, w_ff_in, w_ff_out, tile):
    T = h2.shape[0]
    col_chunk = 2 * FF_SHARD

    def body(h2_ref, x1_ref, tg_ref, g4_ref, wi_ref, wo_ref, a_ref, dfo_ref, dx2_ref, loss_ref, dg4_ref, rr_ref):
        i = pl.program_id(0)
        h2v = h2_ref[...]
        for c in range(D_FF // col_chunk):
            cols = slice(c * col_chunk, (c + 1) * col_chunk)
            a = _dot_nt(h2v, wi_ref[cols, :])
            a_ref[:, cols] = a.astype(BF16)
            ra = jnp.maximum(a, 0.0)
            rr_ref[:, cols] = (ra * ra).astype(BF16)
        f = _dot(rr_ref[...], wo_ref[...])
        r = _rms_scale(f)
        g = g4_ref[...]
        err = x1_ref[...] + f * r * g - tg_ref[...]
        dx2 = err * (1.0 / D_MODEL)
        dx2_ref[...] = dx2
        dfo, dg = _rms_bwd(dx2, f, r, g)
        dfo_ref[...] = dfo.astype(BF16)
        row = lax.broadcasted_iota(jnp.int32, (8, LANES), 0)
        col = lax.broadcasted_iota(jnp.int32, (8, LANES), 1)
        loss = jnp.where((row == 0) & (col == 0), (0.5 / D_MODEL) * jnp.sum(err * err), 0.0)

        @pl.when(i == 0)
        def _():
            loss_ref[...] = loss
            dg4_ref[...] = dg

        @pl.when(i > 0)
        def _():
            loss_ref[...] += loss
            dg4_ref[...] += dg

    tok = pl.BlockSpec((tile, D_MODEL), lambda i: (i, 0))
    return pl.pallas_call(
        body, name="mlp_fwd", grid=(T // tile,),
        in_specs=[tok, tok, tok, _const_spec((1, D_MODEL)), _const_spec((D_FF, D_MODEL)), _const_spec((D_FF, D_MODEL))],
        out_specs=[pl.BlockSpec((tile, D_FF), lambda i: (i, 0)), tok, tok,
                   pl.BlockSpec((8, LANES), lambda i: (0, 0)), pl.BlockSpec((1, D_MODEL), lambda i: (0, 0))],
        out_shape=_hbm_out([jax.ShapeDtypeStruct((T, D_FF), BF16), jax.ShapeDtypeStruct((T, D_MODEL), BF16),
                            jax.ShapeDtypeStruct((T, D_MODEL), F32), jax.ShapeDtypeStruct((8, LANES), F32),
                            jax.ShapeDtypeStruct((1, D_MODEL), F32)]),
        scratch_shapes=[pltpu.VMEM((tile, D_FF), BF16)],
        compiler_params=_params(("arbitrary",), VMEM_MAX),
    )(*_in_hbm(h2, x1, target, g4, w_ff_in.reshape(D_FF, D_MODEL), w_ff_out.reshape(D_FF, D_MODEL)))


def _mlp_weight_grads(dfo, a, h2, w_ff_out, row_chunk):
    T = h2.shape[0]

    def body(dfo_ref, h2_ref, a_ref, wo_ref, dwi_ref, dwo_ref, da_ref, rr_ref):
        def rows(r, _):
            sl = pl.ds(pl.multiple_of(r * row_chunk, row_chunk), row_chunk)
            ra = jnp.maximum(a_ref[sl, :].astype(F32), 0.0)
            da_ref[sl, :] = (_dot_nt(dfo_ref[sl, :], wo_ref[0]) * (2.0 * ra)).astype(BF16)
            rr_ref[sl, :] = (ra * ra).astype(BF16)
            return 0

        lax.fori_loop(0, T // row_chunk, rows, 0)
        dwo_ref[0] = _dot_tn(rr_ref[...], dfo_ref[...])
        dwi_ref[0] = _dot_tn(h2_ref[...], da_ref[...])

    return pl.pallas_call(
        body, name="mlp_weight_grads", grid=(N_DEV,),
        in_specs=[_const_spec((T, D_MODEL)), _const_spec((T, D_MODEL)), pl.BlockSpec((T, FF_SHARD), lambda k: (0, k)),
                  pl.BlockSpec((1, FF_SHARD, D_MODEL), lambda k: (k, 0, 0))],
        out_specs=[pl.BlockSpec((1, D_MODEL, FF_SHARD), lambda k: (k, 0, 0)),
                   pl.BlockSpec((1, FF_SHARD, D_MODEL), lambda k: (k, 0, 0)), pl.BlockSpec((T, FF_SHARD), lambda k: (0, k))],
        out_shape=_hbm_out([jax.ShapeDtypeStruct((N_DEV, D_MODEL, FF_SHARD), F32),
                            jax.ShapeDtypeStruct((N_DEV, FF_SHARD, D_MODEL), F32), jax.ShapeDtypeStruct((T, D_FF), BF16)]),
        scratch_shapes=[pltpu.VMEM((T, FF_SHARD), BF16)],
        compiler_params=_params(("arbitrary",), VMEM_MAX),
    )(*_in_hbm(dfo, h2, a, w_ff_out))


def _mlp_input_grad(da, w_ff_in_t, tile):
    T = da.shape[0]

    def body(da_ref, w_ref, o_ref):
        o_ref[...] = _dot(da_ref[...], w_ref[...])

    return pl.pallas_call(
        body, name="mlp_input_grad", grid=(T // tile,),
        in_specs=[pl.BlockSpec((tile, D_FF), lambda i: (i, 0)), _const_spec((D_FF, D_MODEL))],
        out_specs=pl.BlockSpec((tile, D_MODEL), lambda i: (i, 0)),
        out_shape=_hbm_out(jax.ShapeDtypeStruct((T, D_MODEL), F32)),
        compiler_params=_params(("arbitrary",), VMEM_MID),
    )(*_in_hbm(da, w_ff_in_t))


def _block_diag_in(b):
    bt = b.reshape(N_SSM_BLOCKS, 8, GROUP_CH, N_STATE)
    eye = jnp.eye(8, dtype=b.dtype)
    return jnp.einsum("jacp,ab->jacbp", bt, eye).reshape(N_SSM_BLOCKS, LANES, SSM_LANE_BLOCK)


def _block_diag_in_grad(g):
    g = g.reshape(N_SSM_BLOCKS, 8, GROUP_CH, 8, N_STATE)
    d = jnp.diagonal(g, axis1=1, axis2=3)
    return jnp.transpose(d, (0, 3, 1, 2)).reshape(N_GROUPS, GROUP_CH, N_STATE)


def _block_diag_out(c):
    ct = c.reshape(N_SSM_BLOCKS, 8, GROUP_CH, N_STATE)
    eye = jnp.eye(8, dtype=c.dtype)
    return jnp.einsum("jacp,ab->japbc", ct, eye).reshape(N_SSM_BLOCKS, SSM_LANE_BLOCK, LANES)


def _block_diag_out_grad(g):
    g = g.reshape(N_SSM_BLOCKS, 8, N_STATE, 8, GROUP_CH)
    d = jnp.diagonal(g, axis1=1, axis2=3)
    return jnp.transpose(d, (0, 3, 2, 1)).reshape(N_GROUPS, GROUP_CH, N_STATE)


def _tiles(T):
    return dict(proj=min(512, T), proj_bwd=min(512, T // 2), merge=min(512, T), merge_bwd=min(256, T),
                mlp_fwd=min(512, T), mlp_bwd=min(512, T), ssm_chunk=min(1024, T))


def _mesh_position():
    x, y, c = lax.axis_index("x"), lax.axis_index("y"), lax.axis_index("c")
    other_chips = [(1 - x, y), (x, 1 - y), (1 - x, 1 - y)]
    return x, y, c, other_chips


def _gather_carry(arrays):
    n = len(arrays)

    def copies(ins, outs, sems):
        send_sems, recv_sems, local_sems = sems
        x, y, c, chips = _mesh_position()
        me, sibling = (x, y, c), (x, y, 1 - c)

        def copy(a, k, block, to, src=None):
            px, py, pc = block
            dst = outs[a].at[4 * px + 2 * py + pc]
            return pltpu.make_async_remote_copy(
                src_ref=dst if src is None else src, dst_ref=dst, send_sem=send_sems.at[7 * a + k],
                recv_sem=recv_sems.at[7 * a + k], device_id=to, device_id_type=MESH_IDS)

        mine = [pltpu.make_async_copy(ins[a], outs[a].at[4 * x + 2 * y + c], local_sems.at[a]) for a in range(n)]
        first = []
        for a in range(n):
            first.append(copy(a, 0, me, sibling, src=ins[a]))
            first += [copy(a, 1 + j, me, (*chip, c), src=ins[a]) for j, chip in enumerate(chips)]
        return copy, mine, first, me, sibling, chips, c

    def start(ins, outs, sems):
        _, mine, first, *_ = copies(ins, outs, sems)
        for cp in mine + first:
            cp.start()

    def finish(ins, outs, sems):
        copy, mine, first, me, sibling, chips, c = copies(ins, outs, sems)
        passed = []
        for a in range(n):
            for j, chip in enumerate(chips):
                copy(a, 1 + j, (*chip, c), me).wait_recv()
                passed.append(copy(a, 4 + j, (*chip, c), sibling))
                passed[-1].start()
        for a in range(n):
            copy(a, 0, sibling, me).wait_recv()
            for j, chip in enumerate(chips):
                copy(a, 4 + j, (*chip, 1 - c), me).wait_recv()
        for cp in first + passed:
            cp.wait_send()
        for cp in mine:
            cp.wait()

    return _Carry(arrays, [jax.ShapeDtypeStruct((N_DEV,) + a.shape, a.dtype) for a in arrays],
                  [pltpu.SemaphoreType.DMA((7 * n,)), pltpu.SemaphoreType.DMA((7 * n,)), pltpu.SemaphoreType.DMA((n,))],
                  start, finish)


def _pairwise_carry(arrays, n_slots, make_copies):
    n = len(arrays)

    def start(ins, outs, sems):
        for cp in make_copies(ins, outs, sems):
            cp.start()

    def finish(ins, outs, sems):
        for cp in make_copies(ins, outs, sems):
            cp.wait()

    return _Carry(arrays, [jax.ShapeDtypeStruct((n_slots,) + a.shape[1:], a.dtype) for a in arrays],
                  [pltpu.SemaphoreType.DMA((n_slots * n,)), pltpu.SemaphoreType.DMA((n_slots * n,))], start, finish)


def _sibling_carry(grads):
    def make_copies(ins, outs, sems):
        x, y, c, _ = _mesh_position()
        return [pltpu.make_async_remote_copy(
            src_ref=ins[a].at[2 * ch + (1 - c)], dst_ref=outs[a].at[ch], send_sem=sems[0].at[4 * a + ch],
            recv_sem=sems[1].at[4 * a + ch], device_id=(x, y, 1 - c), device_id_type=MESH_IDS)
            for a in range(len(grads)) for ch in range(4)]

    return _pairwise_carry(grads, 4, make_copies)


def _chips_carry(sums):
    def make_copies(ins, outs, sems):
        x, y, c, chips = _mesh_position()
        return [pltpu.make_async_remote_copy(
            src_ref=ins[a].at[2 * px + py], dst_ref=outs[a].at[j], send_sem=sems[0].at[3 * a + j],
            recv_sem=sems[1].at[3 * a + j], device_id=(px, py, c), device_id_type=MESH_IDS)
            for a in range(len(sums)) for j, (px, py) in enumerate(chips)]

    return _pairwise_carry(sums, 3, make_copies)


def _row_tile(rows, cols):
    t = max(8, min(rows, (1 << 18) // cols // 8 * 8))
    while rows % t:
        t -= 8
    return t


def _add_sibling(grads8, recv, core, name):
    _, R, C = grads8.shape
    tr = _row_tile(R, C)
    g4 = grads8.reshape(4, 2, R, C)

    def body(core_ref, g_ref, r_ref, o_ref, ob_ref):
        s = g_ref[0] + r_ref[...]
        o_ref[...] = s
        ob_ref[...] = s.astype(BF16)

    out = pl.BlockSpec((1, tr, C), lambda ch, r, core_ref: (ch, r, 0))
    return pl.pallas_call(
        body, name=name,
        grid_spec=pltpu.PrefetchScalarGridSpec(
            num_scalar_prefetch=1, grid=(4, R // tr),
            in_specs=[pl.BlockSpec((1, 1, tr, C), lambda ch, r, core_ref: (ch, core_ref[0], r, 0)),
                      pl.BlockSpec((1, tr, C), lambda ch, r, core_ref: (ch, r, 0))],
            out_specs=[out, out]),
        out_shape=_hbm_out([jax.ShapeDtypeStruct((4, R, C), F32), jax.ShapeDtypeStruct((4, R, C), BF16)]),
        compiler_params=_params(("arbitrary", "arbitrary")),
    )(core, *_in_hbm(g4, recv))


def _adam_math(w, g, m, v):
    m = ADAM_B1 * m + (1.0 - ADAM_B1) * g
    v = ADAM_B2 * v + (1.0 - ADAM_B2) * jnp.square(g)
    m_hat = m / (1.0 - ADAM_B1 ** ADAM_STEP)
    v_hat = v / (1.0 - ADAM_B2 ** ADAM_STEP)
    delta = -ADAM_LR * (m_hat / (jnp.sqrt(v_hat) + ADAM_EPS) + ADAM_WD * w)
    return delta, m, v


def _adam_big(w, m, v, chip_sums, recv, chip, name):
    R, C = w.shape
    tr = _row_tile(R, C)

    def body(chip_ref, w_ref, m_ref, v_ref, s_ref, r_ref, g_ref, d_ref, nm_ref, nv_ref):
        g = s_ref[0] + r_ref[0].astype(F32) + r_ref[1].astype(F32) + r_ref[2].astype(F32)
        g_ref[...] = g
        d_ref[...], nm_ref[...], nv_ref[...] = _adam_math(w_ref[...], g, m_ref[...], v_ref[...])

    blk = pl.BlockSpec((tr, C), lambda r, chip_ref: (r, 0))
    return pl.pallas_call(
        body, name=name,
        grid_spec=pltpu.PrefetchScalarGridSpec(
            num_scalar_prefetch=1, grid=(R // tr,),
            in_specs=[blk, blk, blk, pl.BlockSpec((1, tr, C), lambda r, chip_ref: (chip_ref[0], r, 0)),
                      pl.BlockSpec((3, tr, C), lambda r, chip_ref: (0, r, 0))],
            out_specs=[blk] * 4),
        out_shape=[jax.ShapeDtypeStruct((R, C), F32)] * 4,
        compiler_params=_params(("arbitrary",)),
    )(chip, *_in_hbm(w, m, v, chip_sums, recv))


def _sum_partials(partials, name):
    def body(p_ref, g_ref):
        g = p_ref[0]
        for d in range(1, partials.shape[0]):
            g = g + p_ref[d]
        g_ref[...] = g

    return pl.pallas_call(body, name=name, grid=(1,), in_specs=[_whole(partials.shape)], out_specs=_whole(partials.shape[1:]),
                          out_shape=jax.ShapeDtypeStruct(partials.shape[1:], F32))(*_in_hbm(partials))


def _adam_small(ws, ms, vs, gs):
    n = len(ws)

    def body(*refs):
        w_refs, m_refs, v_refs, g_refs = (refs[i * n:(i + 1) * n] for i in range(4))
        d_refs, nm_refs, nv_refs = (refs[(4 + i) * n:(5 + i) * n] for i in range(3))
        for j in range(n):
            d_refs[j][...], nm_refs[j][...], nv_refs[j][...] = _adam_math(
                w_refs[j][...], g_refs[j][...], m_refs[j][...], v_refs[j][...])

    specs = [_whole(w.shape) for w in ws]
    outs = pl.pallas_call(body, name="adam_small", grid=(1,), in_specs=specs * 4, out_specs=specs * 3,
                          out_shape=[jax.ShapeDtypeStruct(w.shape, F32) for w in ws] * 3,
                          compiler_params=_params(("arbitrary",), VMEM_MID))(*_in_hbm(*ws, *ms, *vs, *gs))
    return outs[:n], outs[n:2 * n], outs[2 * n:]


PACK_QUANTUM = SUBLANES * LANES


def _pack(named, names):
    parts = []
    for nme in names:
        flat = named[nme].reshape(-1)
        parts.append(jnp.pad(flat, (0, -flat.size % PACK_QUANTUM)))
    return jnp.concatenate(parts).reshape(-1, LANES)


def _unpack(packed, shapes, names):
    flat = packed.reshape(-1)
    out, pos = {}, 0
    for nme in names:
        size = math.prod(shapes[nme])
        out[nme] = flat[pos:pos + size].reshape(shapes[nme])
        pos += size + (-size % PACK_QUANTUM)
    return out


BIG = ("w_in", "w_glu", "w_attn_branch", "w_ssm_branch", "w_out", "w_ff_in", "w_ff_out")
COLUMN_SHARDED = ("w_in", "w_attn_branch", "w_ssm_branch", "w_ff_in")
SMALL = ("norm_mix_pre", "norm_mix_post", "norm_mlp_pre", "norm_mlp_post", "rel_bias", "sinks", "lam_re", "lam_im",
         "log_dt", "b_re", "b_im", "c_re", "c_im", "d_skip")
SWAPPED_SMALL = ("rel_bias", "b_re", "b_im")
SMALL_LATE = ("rel_bias", "sinks", "loss")
SMALL_BEFORE_ATTN_BWD = tuple(n for n in SMALL if n not in SMALL_LATE + ("norm_mix_pre",))
ALL_WEIGHTS = ("norm_mix_pre", "norm_mix_post", "norm_mlp_pre", "norm_mlp_post", "w_in", "rel_bias", "sinks", "lam_re",
               "lam_im", "log_dt", "b_re", "b_im", "c_re", "c_im", "d_skip", "w_glu", "w_attn_branch", "w_ssm_branch",
               "w_out", "w_ff_in", "w_ff_out")


def _full_from_gathered(name, gathered):
    _, r, c = gathered.shape
    if name in COLUMN_SHARDED:
        return jnp.transpose(gathered, (1, 0, 2)).reshape(r, N_DEV * c)
    return gathered.reshape(N_DEV * r, c)


def _blocks_from_full(name, full):
    r, c = full.shape
    if name in COLUMN_SHARDED:
        return jnp.transpose(full.reshape(r, N_DEV, c // N_DEV), (1, 0, 2))
    return full.reshape(N_DEV, r // N_DEV, c)


def kernel(x, norm_mix_pre, norm_mix_post, norm_mlp_pre, norm_mlp_post, w_in, rel_bias, sinks, lam_re, lam_im, log_dt, b_re, b_im, c_re, c_im, d_skip, w_glu, w_attn_branch, w_ssm_branch, w_out, w_ff_in, w_ff_out, loss_target, m_norm_mix_pre, m_norm_mix_post, m_norm_mlp_pre, m_norm_mlp_post, m_w_in, m_rel_bias, m_sinks, m_lam_re, m_lam_im, m_log_dt, m_b_re, m_b_im, m_c_re, m_c_im, m_d_skip, m_w_glu, m_w_attn_branch, m_w_ssm_branch, m_w_out, m_w_ff_in, m_w_ff_out, v_norm_mix_pre, v_norm_mix_post, v_norm_mlp_pre, v_norm_mlp_post, v_w_in, v_rel_bias, v_sinks, v_lam_re, v_lam_im, v_log_dt, v_b_re, v_b_im, v_c_re, v_c_im, v_d_skip, v_w_glu, v_w_attn_branch, v_w_ssm_branch, v_w_out, v_w_ff_in, v_w_ff_out):
    args = dict(locals())
    w = {n: args[n] for n in ALL_WEIGHTS}
    m = {n: args["m_" + n] for n in ALL_WEIGHTS}
    v = {n: args["v_" + n] for n in ALL_WEIGHTS}
    core = lax.axis_index("c").astype(jnp.int32).reshape(1)
    chip = (2 * lax.axis_index("x") + lax.axis_index("y")).astype(jnp.int32).reshape(1)
    xs, target = x[0], loss_target[0]
    t = _tiles(xs.shape[0])
    local = lambda d, n: d[n][0].T if n == "w_in" else d[n][0]
    shard = {n: local(w, n).astype(BF16) for n in BIG}
    shard["w_ff_in"] = shard["w_ff_in"].T
    view = lambda n, a: jnp.swapaxes(a, -1, -2) if n in SWAPPED_SMALL else a
    small = {n: (view(n, w[n]) if n == "rel_bias" else view(n, w[n])[0]) for n in SMALL}
    g1, g2, g3, g4 = (small[n].reshape(1, D_MODEL) for n in ("norm_mix_pre", "norm_mix_post", "norm_mlp_pre", "norm_mlp_post"))
    bucket = jnp.asarray(_bucket_table())
    rel_b, sink = small["rel_bias"], small["sinks"].reshape(1, N_HEADS)
    lam_r, lam_i = small["lam_re"].reshape(1, STATES), small["lam_im"].reshape(1, STATES)
    ldt_rep = jnp.repeat(small["log_dt"].reshape(N_GROUPS), N_STATE).reshape(1, STATES)
    bd_re, bd_im = _block_diag_in(small["b_re"]), _block_diag_in(small["b_im"])
    cm_re, cm_im = _block_diag_out(small["c_re"]).astype(BF16), _block_diag_out(small["c_im"]).astype(BF16)
    dsk = small["d_skip"].reshape(1, SSM_W)

    (g_in,) = _run_carry(_gather_carry([shard["w_in"]]), "gather_w_in")
    wf_in = g_in.reshape(IN_W, D_MODEL)
    merge_names = ("w_glu", "w_attn_branch", "w_ssm_branch", "w_out")
    (q, k, vv, u, ga, gs, h), gathered = _in_proj_fwd(xs, g1, wf_in, t["proj"], _gather_carry([shard[n] for n in merge_names]))
    wf = {n: _full_from_gathered(n, g) for n, g in zip(merge_names, gathered)}
    (att,), (wf_ff_in,) = _attn_fwd(q, k, vv, bucket, rel_b, sink, _gather_carry([shard["w_ff_in"]]))
    a_re, a_im, bm_re, bm_im = _ssm_prep(lam_r, lam_i, ldt_rep, bd_re, bd_im)
    (y, h_re, h_im, in_re, in_im), (wf_ff_out,) = _ssm_fwd(
        u, a_re, a_im, bm_re, bm_im, cm_re, cm_im, dsk, t["ssm_chunk"], _gather_carry([shard["w_ff_out"]]))
    x1, o, h2 = _merge_fwd(xs, y, att, ga, gs, g2, g3, wf["w_glu"], wf["w_ssm_branch"], wf["w_attn_branch"], wf["w_out"],
                           t["merge"])
    a, dfo, dx2, loss_blk, dg4 = _mlp_fwd(h2, x1, target, g4, wf_ff_in, wf_ff_out, t["mlp_fwd"])

    def add_sibling(names, blocks, received):
        pairs = [_add_sibling(b, r, core, "add_sibling_" + n) for n, b, r in zip(names, blocks, received)]
        return [p[0] for p in pairs], [p[1] for p in pairs]

    ff_names = ("w_ff_in", "w_ff_out")
    dw_ff_in, dw_ff_out, da = _mlp_weight_grads(dfo, a, h2, wf_ff_out, t["mlp_bwd"])
    dh2 = _mlp_input_grad(da, wf_ff_in.reshape(D_FF, D_MODEL), t["mlp_bwd"])
    ff_blocks = [dw_ff_in, dw_ff_out]
    (dx1, dgates, datt, dy, dw_glu, dw_ssm, dw_attn, dw_out, dg2, dg3), ff_recv = _merge_bwd(
        dh2, dx2, x1, o, y, att, ga, gs, g2, g3, wf["w_glu"], wf["w_ssm_branch"], wf["w_attn_branch"], wf["w_out"],
        t["merge_bwd"], _sibling_carry(ff_blocks))
    ff_sums, ff_sums_bf = add_sibling(ff_names, ff_blocks, ff_recv)
    merge_blocks = [_blocks_from_full(n, g) for n, g in zip(merge_names, (dw_glu, dw_attn, dw_ssm, dw_out))]
    (du, dbm_re, dbm_im, dcm_re, dcm_im, da_re, da_im, dd_skip), carried = _ssm_bwd(
        dy, u, h_re, h_im, in_re, in_im, a_re, a_im, bm_re, bm_im, cm_re, cm_im, dsk, t["ssm_chunk"],
        _join(_chips_carry(ff_sums_bf), _sibling_carry(merge_blocks)))
    ff_from_chips, merge_recv = carried[:2], carried[2:]
    merge_sums, merge_sums_bf = add_sibling(merge_names, merge_blocks, merge_recv)
    dbd_re, dbd_im, dlam_re, dlam_im, dldt_rep = _ssm_prep_bwd(lam_r, lam_i, ldt_rep, bd_re, bd_im, dbm_re, dbm_im, da_re, da_im)
    dlog_dt = _group_sum(dldt_rep.reshape(N_GROUPS, N_STATE))
    shapes = {n: view(n, w[n]).shape for n in SMALL}
    shapes["loss"] = (1,)
    small_grads = dict(
        norm_mix_post=dg2, norm_mlp_pre=dg3, norm_mlp_post=dg4, lam_re=dlam_re, lam_im=dlam_im, log_dt=dlog_dt,
        b_re=_block_diag_in_grad(dbd_re), b_im=_block_diag_in_grad(dbd_im),
        c_re=_block_diag_out_grad(dcm_re), c_im=_block_diag_out_grad(dcm_im), d_skip=dd_skip)
    packed_early = _pack({n: small_grads[n].reshape(shapes[n]) for n in SMALL_BEFORE_ATTN_BWD}, SMALL_BEFORE_ATTN_BWD)
    (dq, dkv, attn_small), carried = _attn_bwd(
        q, k, vv, datt, bucket, rel_b, sink, _join(_chips_carry(merge_sums_bf), _gather_carry([packed_early])))
    merge_from_chips, partials_early = carried[:-1], carried[-1]

    dparts = (dq, dkv, du, dgates)
    dw_in_t = _in_proj_weight_grad(h, dparts)
    in_blocks = [dw_in_t.reshape(N_DEV, IN_W // N_DEV, D_MODEL)]
    n_tiles = xs.shape[0] // t["proj_bwd"]
    first_part = max(1, (3 * n_tiles) // 8)
    (gx_a, dg1_a), in_recv = _in_proj_input_grad(
        xs, g1, wf_in, dx1, dparts, t["proj_bwd"], 0, first_part, "in_proj_input_grad_a", _sibling_carry(in_blocks))
    in_sums, in_sums_bf = add_sibling(("w_in",), in_blocks, in_recv)
    late = dict(rel_bias=attn_small[:, :N_BUCKETS, 0], sinks=attn_small[:, N_BUCKETS, 0], loss=loss_blk[0:1, 0])
    packed_late = _pack({n: late[n].reshape(shapes[n]) for n in SMALL_LATE}, SMALL_LATE)
    (gx_b, dg1_b), (in_from_chips, partials_late) = _in_proj_input_grad(
        xs, g1, wf_in, dx1, dparts, t["proj_bwd"], first_part, n_tiles - first_part, "in_proj_input_grad_b",
        _join(_chips_carry(in_sums_bf), _gather_carry([packed_late])))
    grad_x = jnp.concatenate([gx_a, gx_b], axis=0)
    (dg1_partials,) = _run_carry(_gather_carry([jnp.concatenate([dg1_a, dg1_b], axis=0)]), "gather_norm_grad")

    grads, deltas, new_m, new_v = {}, {}, {}, {}
    sums = dict(zip(ff_names + merge_names + ("w_in",), ff_sums + merge_sums + in_sums))
    received = dict(zip(ff_names + merge_names + ("w_in",), ff_from_chips + merge_from_chips + [in_from_chips]))
    for n in BIG:
        outs = _adam_big(local(w, n), local(m, n), local(v, n), sums[n], received[n], chip, "adam_" + n)
        grads[n], deltas[n], new_m[n], new_v[n] = ((o.T if n == "w_in" else o)[None] for o in outs)

    grads.update(_unpack(_sum_partials(partials_early, "sum_small_grads"), shapes, SMALL_BEFORE_ATTN_BWD))
    grads.update(_unpack(_sum_partials(partials_late, "sum_late_grads"), shapes, SMALL_LATE))
    grads["norm_mix_pre"] = _sum_partials(dg1_partials.reshape(2 * N_DEV, 1, D_MODEL), "sum_norm_grad")
    loss = grads.pop("loss").reshape(())
    small_out = _adam_small(*[[view(n, d[n]) for n in SMALL] for d in (w, m, v)], [grads[n] for n in SMALL])
    for store, vals in zip((deltas, new_m, new_v), small_out):
        store.update(zip(SMALL, vals))
    for store in (grads, deltas, new_m, new_v):
        store.update({n: view(n, store[n]) for n in SWAPPED_SMALL})

    return (loss, grad_x[None], *[grads[n] for n in ALL_WEIGHTS], *[deltas[n] for n in ALL_WEIGHTS],
            *[new_m[n] for n in ALL_WEIGHTS], *[new_v[n] for n in ALL_WEIGHTS])
```

```python
import functools
import math

import jax
import jax.numpy as jnp
import numpy as np
from jax import lax
from jax.experimental import pallas as pl
from jax.experimental.pallas import tpu as pltpu

F32 = jnp.float32
BF16 = jnp.bfloat16

D_MODEL = 1024
N_HEADS = 8
HEAD_DIM = 64
ATTN_W = 512
KV_W = 128
BLOCK = 128
N_BUCKETS = 32
SSM_W = 512
N_GROUPS = 32
N_STATE = 64
GROUP_CH = 16
STATES = N_GROUPS * N_STATE
D_FF = 4096
IN_W = 3328
SPLITS = (0, 512, 640, 768, 1280, 2304, 3328)
RMS_EPS = 1e-6
NEG_INF = -1e30
SUBLANES = 8
LANES = 128
SSM_LANE_BLOCK = 512
N_SSM_BLOCKS = STATES // SSM_LANE_BLOCK
VMEM_BIG = 52 * 1024 * 1024
VMEM_MID = 40 * 1024 * 1024
VMEM_MAX = 60 * 1024 * 1024

ADAM_LR = 0.001
ADAM_B1 = 0.9
ADAM_B2 = 0.999
ADAM_EPS = 1e-08
ADAM_WD = 0.01
ADAM_STEP = 10

N_DEV = 8


def _dot(a, b):
    return jnp.dot(a, b, preferred_element_type=F32)


def _dot_nt(a, b):
    return lax.dot_general(a, b, (((1,), (1,)), ((), ())), preferred_element_type=F32)


def _dot_tn(a, b):
    return lax.dot_general(a, b, (((0,), (0,)), ((), ())), preferred_element_type=F32)


def _rms_scale(x):
    return lax.rsqrt(jnp.mean(x * x, axis=-1, keepdims=True) + RMS_EPS)


def _rms_bwd(dy, x, r, g):
    t = dy * g
    dx = r * t - x * (r * r * r) * jnp.mean(t * x, axis=-1, keepdims=True)
    dg = jnp.sum(dy * x * r, axis=0, keepdims=True)
    return dx, dg


def _const_spec(shape):
    nd = len(shape)
    return pl.BlockSpec(shape, lambda *_: (0,) * nd, pipeline_mode=pl.Buffered(1))


def _in_hbm(*arrays):
    return tuple(pltpu.with_memory_space_constraint(a, pltpu.HBM) for a in arrays)


def _hbm_out(shapes):
    if isinstance(shapes, (list, tuple)):
        return [_hbm_out(s) for s in shapes]
    return shapes if isinstance(shapes, pl.MemoryRef) else pltpu.HBM(shapes.shape, shapes.dtype)


def _whole(shape):
    nd = len(shape)
    return pl.BlockSpec(shape, lambda *_: (0,) * nd)


def _params(sem, vmem=None):
    return pltpu.CompilerParams(dimension_semantics=sem, vmem_limit_bytes=vmem)


MESH_IDS = pl.DeviceIdType.MESH
HBM_SPEC = pl.BlockSpec(memory_space=pl.ANY)


class _Carry:
    def __init__(self, inputs, out_shapes, sems, start, finish):
        self.inputs, self.out_shapes, self.sems, self.start, self.finish = list(inputs), list(out_shapes), list(sems), start, finish


def _join(a, b):
    na_in, na_out, na_sem = len(a.inputs), len(a.out_shapes), len(a.sems)

    def start(ins, outs, sems):
        a.start(ins[:na_in], outs[:na_out], sems[:na_sem])
        b.start(ins[na_in:], outs[na_out:], sems[na_sem:])

    def finish(ins, outs, sems):
        a.finish(ins[:na_in], outs[:na_out], sems[:na_sem])
        b.finish(ins[na_in:], outs[na_out:], sems[na_sem:])

    return _Carry(a.inputs + b.inputs, a.out_shapes + b.out_shapes, a.sems + b.sems, start, finish)


def _hosted_call(body, carry, edge, *, name, grid, in_specs, out_specs, out_shape, scratch_shapes, compiler_params, inputs):
    n_in, n_out = len(in_specs), len(out_specs)
    inputs = [a if s.memory_space == pltpu.SMEM else _in_hbm(a)[0] for a, s in zip(inputs, in_specs)]
    out_shape = _hbm_out(list(out_shape))
    if carry is None:
        outs = pl.pallas_call(body, name=name, grid=grid, in_specs=in_specs, out_specs=out_specs, out_shape=out_shape,
                              scratch_shapes=scratch_shapes, compiler_params=compiler_params)(*inputs)
        return list(outs), []
    c_in, c_out, c_sem = len(carry.inputs), len(carry.out_shapes), len(carry.sems)

    def wrapped(*refs):
        ins, refs = refs[:n_in], refs[n_in:]
        cins, refs = refs[:c_in], refs[c_in:]
        outs, refs = refs[:n_out], refs[n_out:]
        couts, refs = refs[:c_out], refs[c_out:]
        scratch, csems = refs[:len(refs) - c_sem], refs[len(refs) - c_sem:]
        first, last = edge()

        @pl.when(first)
        def _():
            carry.start(cins, couts, csems)

        body(*ins, *outs, *scratch)

        @pl.when(last)
        def _():
            carry.finish(cins, couts, csems)

    outs = pl.pallas_call(
        wrapped, name=name, grid=grid, in_specs=list(in_specs) + [HBM_SPEC] * c_in,
        out_specs=list(out_specs) + [HBM_SPEC] * c_out, out_shape=out_shape + _hbm_out(carry.out_shapes),
        scratch_shapes=list(scratch_shapes) + carry.sems, compiler_params=compiler_params)(*inputs, *_in_hbm(*carry.inputs))
    return list(outs[:n_out]), list(outs[n_out:])


def _edge_1d(n_steps):
    return lambda: (pl.program_id(0) == 0, pl.program_id(0) == n_steps - 1)


def _edge_2d(n0, n1):
    return lambda: ((pl.program_id(0) == 0) & (pl.program_id(1) == 0),
                    (pl.program_id(0) == n0 - 1) & (pl.program_id(1) == n1 - 1))


def _run_carry(carry, name):
    c_in, c_out = len(carry.inputs), len(carry.out_shapes)

    def body(*refs):
        ins, outs, sems = refs[:c_in], refs[c_in:c_in + c_out], refs[c_in + c_out:]
        carry.start(ins, outs, sems)
        carry.finish(ins, outs, sems)

    return pl.pallas_call(body, name=name, in_specs=[HBM_SPEC] * c_in, out_specs=[HBM_SPEC] * c_out,
                          out_shape=_hbm_out(carry.out_shapes), scratch_shapes=carry.sems)(*_in_hbm(*carry.inputs))


def _in_proj_fwd(x, g1, w_in_t, tile, carry=None):
    T = x.shape[0]

    def body(x_ref, g_ref, w_ref, q_ref, k_ref, v_ref, u_ref, ga_ref, gs_ref, h_ref):
        xv = x_ref[...]
        h = (xv * _rms_scale(xv) * g_ref[...]).astype(BF16)
        h_ref[...] = h
        outs = (q_ref, k_ref, v_ref, u_ref, ga_ref, gs_ref)
        for p, o_ref in enumerate(outs):
            o_ref[...] = _dot_nt(h, w_ref[SPLITS[p]:SPLITS[p + 1], :]).astype(o_ref.dtype)

    widths = [SPLITS[p + 1] - SPLITS[p] for p in range(6)] + [D_MODEL]
    dtypes = [BF16, BF16, BF16, F32, F32, F32, BF16]
    return _hosted_call(
        body, carry, _edge_1d(T // tile), name="in_proj_fwd", grid=(T // tile,),
        in_specs=[pl.BlockSpec((tile, D_MODEL), lambda i: (i, 0)), _const_spec((1, D_MODEL)), _const_spec((IN_W, D_MODEL))],
        out_specs=[pl.BlockSpec((tile, w), lambda i: (i, 0)) for w in widths],
        out_shape=[jax.ShapeDtypeStruct((T, w), dt) for w, dt in zip(widths, dtypes)],
        scratch_shapes=[], compiler_params=_params(("arbitrary",), VMEM_MID), inputs=(x, g1, w_in_t))


PROJ_PARTS = (512, 256, 512, 2048)
PROJ_GRAD_BLOCK = 256


def _in_proj_weight_grad(h, dparts):
    T = h.shape[0]
    blocks = [wd // PROJ_GRAD_BLOCK for wd in PROJ_PARTS]
    starts = [sum(blocks[:p]) for p in range(len(blocks))]

    def body(h_ref, *refs):
        part_refs, o_ref = refs[:-1], refs[-1]
        j = pl.program_id(0)
        for p_ref, start, count in zip(part_refs, starts, blocks):
            @pl.when((j >= start) & (j < start + count))
            def _(p_ref=p_ref):
                o_ref[...] = _dot_tn(p_ref[...], h_ref[...])

    def part_spec(start, count):
        return pl.BlockSpec((T, PROJ_GRAD_BLOCK), lambda j: (0, jnp.clip(j - start, 0, count - 1)))

    return pl.pallas_call(
        body, name="in_proj_weight_grad", grid=(sum(blocks),),
        in_specs=[_const_spec((T, D_MODEL))] + [part_spec(s, c) for s, c in zip(starts, blocks)],
        out_specs=pl.BlockSpec((PROJ_GRAD_BLOCK, D_MODEL), lambda j: (j, 0)),
        out_shape=_hbm_out(jax.ShapeDtypeStruct((IN_W, D_MODEL), F32)),
        compiler_params=_params(("arbitrary",), VMEM_MID),
    )(*_in_hbm(h, *dparts))


def _in_proj_input_grad(x, g1, w_in_t, dx1, dparts, tile, first_tile, n_tiles, name, carry=None):
    offsets = [sum(PROJ_PARTS[:p]) for p in range(len(PROJ_PARTS))]

    def body(x_ref, g_ref, w_ref, dx1_ref, *refs):
        part_refs, (gx_ref, dg_ref) = refs[:len(PROJ_PARTS)], refs[len(PROJ_PARTS):]
        i = pl.program_id(0)
        xv = x_ref[...]
        r = _rms_scale(xv)
        g = g_ref[...]
        dh = sum(_dot(p_ref[...], w_ref[off:off + wd, :]) for p_ref, off, wd in zip(part_refs, offsets, PROJ_PARTS))
        dxn, dg = _rms_bwd(dh, xv, r, g)
        gx_ref[...] = dx1_ref[...] + dxn

        @pl.when(i == 0)
        def _():
            dg_ref[...] = dg

        @pl.when(i > 0)
        def _():
            dg_ref[...] += dg

    tok = lambda wd: pl.BlockSpec((tile, wd), lambda i: (i + first_tile, 0))
    return _hosted_call(
        body, carry, _edge_1d(n_tiles), name=name, grid=(n_tiles,),
        in_specs=[tok(D_MODEL), _const_spec((1, D_MODEL)), _const_spec((IN_W, D_MODEL)), tok(D_MODEL)] + [tok(wd) for wd in PROJ_PARTS],
        out_specs=[pl.BlockSpec((tile, D_MODEL), lambda i: (i, 0)), pl.BlockSpec((1, D_MODEL), lambda i: (0, 0))],
        out_shape=[jax.ShapeDtypeStruct((n_tiles * tile, D_MODEL), F32), jax.ShapeDtypeStruct((1, D_MODEL), F32)],
        scratch_shapes=[], compiler_params=_params(("arbitrary",), VMEM_MID), inputs=(x, g1, w_in_t, dx1, *dparts))


def _bucket_table():
    qi = np.arange(BLOCK)[:, None]
    kj = np.arange(2 * BLOCK)[None, :]
    dist = qi + BLOCK - kj
    max_exact = N_BUCKETS // 2
    d = np.maximum(dist, 0)
    df = np.maximum(d, 1).astype(np.float32)
    large = max_exact + (np.log(df / np.float32(max_exact)) / np.float32(math.log(BLOCK / max_exact))
                         * np.float32(N_BUCKETS - max_exact)).astype(np.int32)
    large = np.minimum(large, N_BUCKETS - 1)
    bucket = np.where(d < max_exact, d, large)
    return np.where((dist >= 0) & (dist < BLOCK), bucket, -1).astype(np.int32)


def _build_bias(bucket_ref, rb_ref, bias_ref):
    bk = bucket_ref[...]
    for h in range(N_HEADS):
        def add(b, acc, h=h):
            return acc + jnp.where(bk == b, rb_ref[h, b], 0.0)
        bias_ref[h] = lax.fori_loop(0, N_BUCKETS, add, jnp.zeros((BLOCK, 2 * BLOCK), F32))


def _kv_variants(prev_ref, cur_ref):
    cat = jnp.concatenate([prev_ref[...], cur_ref[...]], axis=0)
    lo = lax.broadcasted_iota(jnp.int32, cat.shape, 1) < HEAD_DIM
    zero = jnp.zeros_like(cat)
    head0_lo = jnp.where(lo, cat, zero)
    head1_hi = jnp.where(lo, zero, cat)
    return ((head0_lo, pltpu.roll(head0_lo, HEAD_DIM, 1)), (pltpu.roll(head1_hi, HEAD_DIM, 1), head1_hi))


def _merge_kv_grads(g):
    lo = lax.broadcasted_iota(jnp.int32, g[0][0].shape, 1) < HEAD_DIM
    return jnp.where(lo, g[0][0] + pltpu.roll(g[0][1], HEAD_DIM, 1), g[1][1] + pltpu.roll(g[1][0], HEAD_DIM, 1))


def _head_lanes(h):
    return slice((h // 2) * LANES, (h // 2 + 1) * LANES)


def _attn_probs(q_ref, kvar, bias_ref, sk_ref, valid, s_ref):
    for h in range(N_HEADS):
        s_ref[h] = _dot_nt(q_ref[:, _head_lanes(h)], kvar[h // 4][h % 2])
    head = lax.broadcasted_iota(jnp.int32, (N_HEADS, 1, 1), 0)
    sink = jnp.zeros((N_HEADS, 1, 1), F32)
    for h in range(N_HEADS):
        sink = jnp.where(head == h, sk_ref[0, h], sink)
    s = jnp.where(valid[None], s_ref[...] * (HEAD_DIM ** -0.5) + bias_ref[...], NEG_INF)
    m = jnp.maximum(jnp.max(s, axis=-1, keepdims=True), sink)
    p = jnp.exp(s - m)
    e_sink = jnp.exp(sink - m)
    inv = 1.0 / (jnp.sum(p, axis=-1, keepdims=True) + e_sink)
    return p * inv, e_sink * inv


def _attn_valid(bucket_ref, n):
    col = lax.broadcasted_iota(jnp.int32, (BLOCK, 2 * BLOCK), 1)
    return (bucket_ref[...] >= 0) & ((n > 0) | (col >= BLOCK))


def _attn_fwd(q, k, v, bucket, rel_bias, sinks, carry=None):
    T = q.shape[0]
    nb = T // BLOCK

    def body(q_ref, kc_ref, kp_ref, vc_ref, vp_ref, bucket_ref, rb_ref, sk_ref, o_ref, bias_ref, s_ref, p_ref):
        n = pl.program_id(0)

        @pl.when(n == 0)
        def _():
            _build_bias(bucket_ref, rb_ref, bias_ref)

        kvar = _kv_variants(kp_ref, kc_ref)
        vvar = _kv_variants(vp_ref, vc_ref)
        pr, _ = _attn_probs(q_ref, kvar, bias_ref, sk_ref, _attn_valid(bucket_ref, n), s_ref)
        p_ref[...] = pr.astype(BF16)
        for m in range(N_HEADS // 2):
            acc = _dot(p_ref[2 * m], vvar[m // 2][0]) + _dot(p_ref[2 * m + 1], vvar[m // 2][1])
            o_ref[:, m * LANES:(m + 1) * LANES] = acc.astype(o_ref.dtype)

    cur = lambda w: pl.BlockSpec((BLOCK, w), lambda n: (n, 0))
    prev = lambda w: pl.BlockSpec((BLOCK, w), lambda n: (jnp.maximum(n - 1, 0), 0))
    smem = pl.BlockSpec(memory_space=pltpu.SMEM)
    return _hosted_call(
        body, carry, _edge_1d(nb), name="attn_fwd", grid=(nb,),
        in_specs=[cur(ATTN_W), cur(KV_W), prev(KV_W), cur(KV_W), prev(KV_W), _const_spec((BLOCK, 2 * BLOCK)), smem, smem],
        out_specs=[cur(ATTN_W)],
        out_shape=[jax.ShapeDtypeStruct((T, ATTN_W), BF16)],
        scratch_shapes=[pltpu.VMEM((N_HEADS, BLOCK, 2 * BLOCK), F32), pltpu.VMEM((N_HEADS, BLOCK, 2 * BLOCK), F32),
                        pltpu.VMEM((N_HEADS, BLOCK, 2 * BLOCK), BF16)],
        compiler_params=_params(("arbitrary",)), inputs=(q, k, k, v, v, bucket, rel_bias, sinks))


ATTN_SMALL_ROWS = N_BUCKETS + SUBLANES


def _attn_bwd(q, k, v, datt, bucket, rel_bias, sinks, carry=None):
    T = q.shape[0]
    nb = T // BLOCK

    def body(q_ref, do_ref, kc_ref, kp_ref, vc_ref, vp_ref, bucket_ref, rb_ref, sk_ref,
             dq_ref, dkv_ref, small_ref, bias_ref, ds_sum_ref, dsink_ref, kcarry_ref, vcarry_ref,
             s_ref, dp_ref, p_ref, dsc_ref):
        n = pl.program_id(0)

        @pl.when(n == 0)
        def _():
            _build_bias(bucket_ref, rb_ref, bias_ref)
            ds_sum_ref[...] = jnp.zeros_like(ds_sum_ref)
            dsink_ref[...] = jnp.zeros_like(dsink_ref)
            kcarry_ref[...] = jnp.zeros_like(kcarry_ref)
            vcarry_ref[...] = jnp.zeros_like(vcarry_ref)

        @pl.when(n < nb)
        def _():
            kvar = _kv_variants(kp_ref, kc_ref)
            vvar = _kv_variants(vp_ref, vc_ref)
            pr, p_sink = _attn_probs(q_ref, kvar, bias_ref, sk_ref, _attn_valid(bucket_ref, n), s_ref)
            for h in range(N_HEADS):
                dp_ref[h] = _dot_nt(do_ref[:, _head_lanes(h)], vvar[h // 4][h % 2])
            dp = dp_ref[...]
            dsum = jnp.sum(pr * dp, axis=-1, keepdims=True)
            ds = pr * (dp - dsum)
            ds_sum_ref[...] += ds
            dsink_ref[...] -= jnp.sum(p_sink * dsum, axis=1, keepdims=True)
            dsc_ref[...] = (ds * (HEAD_DIM ** -0.5)).astype(BF16)
            p_ref[...] = pr.astype(BF16)
            for m in range(N_HEADS // 2):
                dqm = _dot(dsc_ref[2 * m], kvar[m // 2][0]) + _dot(dsc_ref[2 * m + 1], kvar[m // 2][1])
                dq_ref[:, m * LANES:(m + 1) * LANES] = dqm.astype(dq_ref.dtype)
            dk_var = [[None, None], [None, None]]
            dv_var = [[None, None], [None, None]]
            for kvh in range(2):
                for e in range(2):
                    heads = [h for h in range(N_HEADS) if h // 4 == kvh and h % 2 == e]
                    dk_var[kvh][e] = sum(_dot_tn(dsc_ref[h], q_ref[:, _head_lanes(h)]) for h in heads)
                    dv_var[kvh][e] = sum(_dot_tn(p_ref[h], do_ref[:, _head_lanes(h)]) for h in heads)
            dk_cat = _merge_kv_grads(dk_var)
            dv_cat = _merge_kv_grads(dv_var)

            @pl.when(n > 0)
            def _():
                dkv_ref[:, :KV_W] = (kcarry_ref[...] + dk_cat[:BLOCK]).astype(BF16)
                dkv_ref[:, KV_W:] = (vcarry_ref[...] + dv_cat[:BLOCK]).astype(BF16)

            kcarry_ref[...] = dk_cat[BLOCK:]
            vcarry_ref[...] = dv_cat[BLOCK:]

        @pl.when(n == nb)
        def _():
            dkv_ref[:, :KV_W] = kcarry_ref[...].astype(BF16)
            dkv_ref[:, KV_W:] = vcarry_ref[...].astype(BF16)
            bk = bucket_ref[...]
            row = lax.broadcasted_iota(jnp.int32, (N_HEADS, ATTN_SMALL_ROWS, LANES), 1)

            def add(b, acc):
                masked = jnp.where((bk == b)[None], ds_sum_ref[...], 0.0)
                val = jnp.sum(jnp.sum(masked, axis=1, keepdims=True), axis=2, keepdims=True)
                return acc + jnp.where(row == b, val, 0.0)

            small_ref[...] = lax.fori_loop(0, N_BUCKETS, add, jnp.where(row == N_BUCKETS, dsink_ref[...], 0.0))

    last = nb - 1
    cur = lambda w: pl.BlockSpec((BLOCK, w), lambda n: (jnp.minimum(n, last), 0))
    prev = lambda w: pl.BlockSpec((BLOCK, w), lambda n: (jnp.clip(n - 1, 0, last), 0))
    smem = pl.BlockSpec(memory_space=pltpu.SMEM)
    return _hosted_call(
        body, carry, _edge_1d(nb + 1), name="attn_bwd", grid=(nb + 1,),
        in_specs=[cur(ATTN_W), cur(ATTN_W), cur(KV_W), prev(KV_W), cur(KV_W), prev(KV_W),
                  _const_spec((BLOCK, 2 * BLOCK)), smem, smem],
        out_specs=[cur(ATTN_W), prev(2 * KV_W), pl.BlockSpec((N_HEADS, ATTN_SMALL_ROWS, LANES), lambda n: (0, 0, 0))],
        out_shape=[jax.ShapeDtypeStruct((T, ATTN_W), BF16), jax.ShapeDtypeStruct((T, 2 * KV_W), BF16),
                   jax.ShapeDtypeStruct((N_HEADS, ATTN_SMALL_ROWS, LANES), F32)],
        scratch_shapes=[pltpu.VMEM((N_HEADS, BLOCK, 2 * BLOCK), F32), pltpu.VMEM((N_HEADS, BLOCK, 2 * BLOCK), F32),
                        pltpu.VMEM((N_HEADS, 1, 1), F32), pltpu.VMEM((BLOCK, KV_W), F32), pltpu.VMEM((BLOCK, KV_W), F32),
                        pltpu.VMEM((N_HEADS, BLOCK, 2 * BLOCK), F32), pltpu.VMEM((N_HEADS, BLOCK, 2 * BLOCK), F32),
                        pltpu.VMEM((N_HEADS, BLOCK, 2 * BLOCK), BF16), pltpu.VMEM((N_HEADS, BLOCK, 2 * BLOCK), BF16)],
        compiler_params=_params(("arbitrary",)), inputs=(q, datt, k, k, v, v, bucket, rel_bias, sinks))


SCAN_UNROLL = 4


def _cmul(ar, ai, br, bi):
    return ar * br - ai * bi, ar * bi + ai * br


def _cmul_conj(ar, ai, br, bi):
    return ar * br + ai * bi, ar * bi - ai * br


def _ssm_discretize(lr, li, ldt):
    dt = jnp.exp(ldt)
    mag = jnp.exp(lr * dt)
    ab_re = mag * jnp.cos(li * dt)
    ab_im = mag * jnp.sin(li * dt)
    nr = ab_re - 1.0
    den = lr * lr + li * li
    f_re = (nr * lr + ab_im * li) / den
    f_im = (ab_im * lr - nr * li) / den
    return ab_re, ab_im, f_re, f_im


def _ssm_prep(lam_re, lam_im, ldt_rep, bd_re, bd_im):
    def body(lr_ref, li_ref, ldt_ref, bdr_ref, bdi_ref, ar_ref, ai_ref, br_ref, bi_ref):
        ab_re, ab_im, f_re, f_im = _ssm_discretize(lr_ref[...], li_ref[...], ldt_ref[...])
        ar_ref[...] = ab_re
        ai_ref[...] = ab_im
        bdr, bdi = bdr_ref[0], bdi_ref[0]
        br_ref[0] = (bdr * f_re - bdi * f_im).astype(BF16)
        bi_ref[0] = (bdi * f_re + bdr * f_im).astype(BF16)

    row = pl.BlockSpec((1, SSM_LANE_BLOCK), lambda j: (0, j))
    mat = pl.BlockSpec((1, LANES, SSM_LANE_BLOCK), lambda j: (j, 0, 0))
    return pl.pallas_call(
        body, name="ssm_prep", grid=(N_SSM_BLOCKS,),
        in_specs=[row, row, row, mat, mat], out_specs=[row, row, mat, mat],
        out_shape=[jax.ShapeDtypeStruct((1, STATES), F32)] * 2 + [jax.ShapeDtypeStruct((N_SSM_BLOCKS, LANES, SSM_LANE_BLOCK), BF16)] * 2,
        compiler_params=_params(("arbitrary",)),
    )(*_in_hbm(lam_re, lam_im, ldt_rep, bd_re, bd_im))


def _ssm_prep_bwd(lam_re, lam_im, ldt_rep, bd_re, bd_im, dbr, dbi, da_re, da_im):
    def body(lr_ref, li_ref, ldt_ref, bdr_ref, bdi_ref, dbr_ref, dbi_ref, dar_ref, dai_ref,
             dbdr_ref, dbdi_ref, dlr_ref, dli_ref, dldt_ref):
        lr, li, ldt = lr_ref[...], li_ref[...], ldt_ref[...]
        (_, _, f_re, f_im), vjp = jax.vjp(_ssm_discretize, lr, li, ldt)
        bdr, bdi, gbr, gbi = bdr_ref[0], bdi_ref[0], dbr_ref[0], dbi_ref[0]
        dbdr_ref[0] = gbr * f_re + gbi * f_im
        dbdi_ref[0] = gbi * f_re - gbr * f_im
        df_re = jnp.sum(gbr * bdr + gbi * bdi, axis=0, keepdims=True)
        df_im = jnp.sum(gbi * bdr - gbr * bdi, axis=0, keepdims=True)
        dlr, dli, dldt = vjp((dar_ref[...], dai_ref[...], df_re, df_im))
        dlr_ref[...] = dlr
        dli_ref[...] = dli
        dldt_ref[...] = dldt

    row = pl.BlockSpec((1, SSM_LANE_BLOCK), lambda j: (0, j))
    mat = pl.BlockSpec((1, LANES, SSM_LANE_BLOCK), lambda j: (j, 0, 0))
    mat_shape = jax.ShapeDtypeStruct((N_SSM_BLOCKS, LANES, SSM_LANE_BLOCK), F32)
    row_shape = jax.ShapeDtypeStruct((1, STATES), F32)
    return pl.pallas_call(
        body, name="ssm_prep_bwd", grid=(N_SSM_BLOCKS,),
        in_specs=[row, row, row, mat, mat, mat, mat, row, row], out_specs=[mat, mat, row, row, row],
        out_shape=[mat_shape, mat_shape, row_shape, row_shape, row_shape],
        compiler_params=_params(("arbitrary",)),
    )(*_in_hbm(lam_re, lam_im, ldt_rep, bd_re, bd_im, dbr, dbi, da_re, da_im))


def _group_sum(x):
    def body(x_ref, o_ref):
        o_ref[...] = jnp.sum(x_ref[...], axis=1, keepdims=True)
    return pl.pallas_call(body, name="ssm_group_sum", grid=(1,), in_specs=[_whole(x.shape)], out_specs=_whole((N_GROUPS, 1)),
                          out_shape=jax.ShapeDtypeStruct((N_GROUPS, 1), F32))(*_in_hbm(x))


def _power_table(ar, ai, p_re_ref, p_im_ref, steps):
    shape = (SUBLANES, SSM_LANE_BLOCK)
    p_re_ref[0:SUBLANES] = jnp.broadcast_to(ar, shape)
    p_im_ref[0:SUBLANES] = jnp.broadcast_to(ai, shape)
    m = 1
    while m < steps:
        rows = m * SUBLANES
        top_re = p_re_ref[rows - SUBLANES:rows]
        top_im = p_im_ref[rows - SUBLANES:rows]
        cur_re = p_re_ref[0:rows].reshape(m, SUBLANES, SSM_LANE_BLOCK)
        cur_im = p_im_ref[0:rows].reshape(m, SUBLANES, SSM_LANE_BLOCK)
        nxt_re, nxt_im = _cmul(cur_re, cur_im, top_re[None], top_im[None])
        p_re_ref[rows:2 * rows] = nxt_re.reshape(rows, SSM_LANE_BLOCK)
        p_im_ref[rows:2 * rows] = nxt_im.reshape(rows, SSM_LANE_BLOCK)
        m *= 2


def _to_segments(src_ref, dst_ref, steps):
    for s in range(SUBLANES):
        dst_ref[pl.ds(s, steps, stride=SUBLANES), :] = src_ref[s * steps:(s + 1) * steps, :]


def _from_segments(src_ref, dst_ref, steps):
    for s in range(SUBLANES):
        dst_ref[s * steps:(s + 1) * steps, :] = src_ref[pl.ds(s, steps, stride=SUBLANES), :]


def _segment_carries(e_re, e_im, an_re, an_im, c_re, c_im, reverse):
    order = range(SUBLANES - 1, -1, -1) if reverse else range(SUBLANES)
    ins_re, ins_im = [None] * SUBLANES, [None] * SUBLANES
    for s in order:
        ins_re[s], ins_im[s] = c_re, c_im
        pr, pi = _cmul(an_re, an_im, c_re, c_im)
        c_re = e_re[s:s + 1] + pr
        c_im = e_im[s:s + 1] + pi
    return jnp.concatenate(ins_re, axis=0), jnp.concatenate(ins_im, axis=0), c_re, c_im


def _ssm_fwd(u, a_re, a_im, b_re, b_im, c_re, c_im, d_skip, chunk, carry=None):
    T = u.shape[0]
    nc = T // chunk
    steps = chunk // SUBLANES
    blk = SSM_LANE_BLOCK

    def body(u_ref, ar_ref, ai_ref, br_ref, bi_ref, cr_ref, ci_ref, dk_ref,
             y_ref, hr_ref, hi_ref, inr_ref, ini_ref, useg_ref, yseg_ref, pr_ref, pi_ref, carry_ref):
        c = pl.program_id(1)
        ar, ai = ar_ref[...], ai_ref[...]

        @pl.when(c == 0)
        def _():
            _power_table(ar, ai, pr_ref, pi_ref, steps)
            carry_ref[...] = jnp.zeros_like(carry_ref)

        _to_segments(u_ref, useg_ref, steps)
        ub = useg_ref[...].astype(BF16)
        hr_ref[...] = _dot(ub, br_ref[0])
        hi_ref[...] = _dot(ub, bi_ref[0])
        first = slice(0, SUBLANES)

        def scan(t4, prev):
            for j in range(SCAN_UNROLL):
                rows = pl.ds(pl.multiple_of((t4 * SCAN_UNROLL + j) * SUBLANES, SUBLANES), SUBLANES)
                pr, pi = _cmul(pr_ref[first, :], pi_ref[first, :], prev[0], prev[1])
                prev = (pr + hr_ref[rows, :], pi + hi_ref[rows, :])
                hr_ref[rows, :] = prev[0]
                hi_ref[rows, :] = prev[1]
            return prev

        zero = jnp.zeros((SUBLANES, blk), F32)
        lax.fori_loop(0, steps // SCAN_UNROLL, scan, (zero, zero))

        top = slice(chunk - SUBLANES, chunk)
        in_re, in_im, out_re, out_im = _segment_carries(
            hr_ref[top, :], hi_ref[top, :], pr_ref[top, :][0:1], pi_ref[top, :][0:1],
            carry_ref[0:1, :], carry_ref[1:2, :], reverse=False)
        carry_ref[0:1, :] = out_re
        carry_ref[1:2, :] = out_im
        inr_ref[...] = in_re
        ini_ref[...] = in_im

        def fix(t4, _):
            for j in range(SCAN_UNROLL):
                rows = pl.ds(pl.multiple_of((t4 * SCAN_UNROLL + j) * SUBLANES, SUBLANES), SUBLANES)
                fr, fi = _cmul(pr_ref[rows, :], pi_ref[rows, :], in_re, in_im)
                hr_ref[rows, :] += fr
                hi_ref[rows, :] += fi
            return 0

        lax.fori_loop(0, steps // SCAN_UNROLL, fix, 0)

        yseg_ref[...] = _dot(hr_ref[...].astype(BF16), cr_ref[0]) - _dot(hi_ref[...].astype(BF16), ci_ref[0])
        _from_segments(yseg_ref, y_ref, steps)
        y_ref[...] += dk_ref[...] * u_ref[...]

    row = pl.BlockSpec((1, blk), lambda j, c: (0, j))
    b_mat = pl.BlockSpec((1, LANES, blk), lambda j, c: (j, 0, 0))
    c_mat = pl.BlockSpec((1, blk, LANES), lambda j, c: (j, 0, 0))
    tok = pl.BlockSpec((chunk, LANES), lambda j, c: (c, j))
    state = pl.BlockSpec((chunk, blk), lambda j, c: (c, j))
    enter = pl.BlockSpec((SUBLANES, blk), lambda j, c: (c, j))
    return _hosted_call(
        body, carry, _edge_2d(N_SSM_BLOCKS, nc), name="ssm_fwd", grid=(N_SSM_BLOCKS, nc),
        in_specs=[tok, row, row, b_mat, b_mat, c_mat, c_mat, pl.BlockSpec((1, LANES), lambda j, c: (0, j))],
        out_specs=[tok, state, state, enter, enter],
        out_shape=[jax.ShapeDtypeStruct((T, SSM_W), F32), jax.ShapeDtypeStruct((T, STATES), F32),
                   jax.ShapeDtypeStruct((T, STATES), F32), jax.ShapeDtypeStruct((nc * SUBLANES, STATES), F32),
                   jax.ShapeDtypeStruct((nc * SUBLANES, STATES), F32)],
        scratch_shapes=[pltpu.VMEM((chunk, LANES), F32), pltpu.VMEM((chunk, LANES), F32),
                        pltpu.VMEM((chunk, blk), F32), pltpu.VMEM((chunk, blk), F32), pltpu.VMEM((SUBLANES, blk), F32)],
        compiler_params=_params(("arbitrary", "arbitrary"), VMEM_MID),
        inputs=(u, a_re, a_im, b_re, b_im, c_re, c_im, d_skip))


def _ssm_bwd(dy, u, h_re, h_im, in_re, in_im, a_re, a_im, b_re, b_im, c_re, c_im, d_skip, chunk, carry=None):
    T = u.shape[0]
    nc = T // chunk
    steps = chunk // SUBLANES
    blk = SSM_LANE_BLOCK

    def body(dy_ref, u_ref, hr_ref, hi_ref, inr_ref, ini_ref, ar_ref, ai_ref, br_ref, bi_ref, cr_ref, ci_ref, dk_ref,
             du_ref, dbr_ref, dbi_ref, dcr_ref, dci_ref, dar_ref, dai_ref, ddk_ref,
             dyseg_ref, useg_ref, duseg_ref, gr_ref, gi_ref, pr_ref, pi_ref, carry_ref, accr_ref, acci_ref):
        c = pl.program_id(1)
        ar, ai = ar_ref[...], ai_ref[...]

        @pl.when(c == 0)
        def _():
            _power_table(ar, ai, pr_ref, pi_ref, steps)
            carry_ref[...] = jnp.zeros_like(carry_ref)
            accr_ref[...] = jnp.zeros_like(accr_ref)
            acci_ref[...] = jnp.zeros_like(acci_ref)

        _to_segments(dy_ref, dyseg_ref, steps)
        _to_segments(u_ref, useg_ref, steps)
        dyb = dyseg_ref[...].astype(BF16)
        ub = useg_ref[...].astype(BF16)
        gr_ref[...] = _dot_nt(dyb, cr_ref[0])
        gi_ref[...] = -_dot_nt(dyb, ci_ref[0])
        dcr = _dot_tn(hr_ref[...].astype(BF16), dyb)
        dci = -_dot_tn(hi_ref[...].astype(BF16), dyb)
        ddk = jnp.sum(dy_ref[...] * u_ref[...], axis=0, keepdims=True)

        first = slice(0, SUBLANES)

        def scan(k4, nxt):
            for j in range(SCAN_UNROLL):
                t = steps - 1 - (k4 * SCAN_UNROLL + j)
                rows = pl.ds(pl.multiple_of(t * SUBLANES, SUBLANES), SUBLANES)
                pr, pi = _cmul_conj(pr_ref[first, :], pi_ref[first, :], nxt[0], nxt[1])
                nxt = (pr + gr_ref[rows, :], pi + gi_ref[rows, :])
                gr_ref[rows, :] = nxt[0]
                gi_ref[rows, :] = nxt[1]
            return nxt

        top = slice(chunk - SUBLANES, chunk)
        zero = jnp.zeros((SUBLANES, blk), F32)
        lax.fori_loop(0, steps // SCAN_UNROLL, scan, (zero, zero))

        gin_re, gin_im, out_re, out_im = _segment_carries(
            gr_ref[0:SUBLANES, :], gi_ref[0:SUBLANES, :], pr_ref[top, :][0:1], -pi_ref[top, :][0:1],
            carry_ref[0:1, :], carry_ref[1:2, :], reverse=True)
        carry_ref[0:1, :] = out_re
        carry_ref[1:2, :] = out_im

        def fix_row(rows, prow, hp_re, hp_im, acc):
            fr, fi = _cmul_conj(pr_ref[prow, :], pi_ref[prow, :], gin_re, gin_im)
            g_re = gr_ref[rows, :] + fr
            g_im = gi_ref[rows, :] + fi
            gr_ref[rows, :] = g_re
            gi_ref[rows, :] = g_im
            return acc[0] + g_re * hp_re + g_im * hp_im, acc[1] + g_im * hp_re - g_re * hp_im

        def fix_at(t, acc):
            aligned = (lambda r: r * SUBLANES) if isinstance(t, int) else (lambda r: pl.multiple_of(r * SUBLANES, SUBLANES))
            rows, before, prow = (pl.ds(aligned(r), SUBLANES) for r in (t, t - 1, steps - 1 - t))
            return fix_row(rows, prow, hr_ref[before, :], hi_ref[before, :], acc)

        def fix(t4, acc):
            for j in range(SCAN_UNROLL):
                acc = fix_at(t4 * SCAN_UNROLL + j, acc)
            return acc

        acc = fix_row(first, top, inr_ref[...], ini_ref[...], (accr_ref[...], acci_ref[...]))
        for t in range(1, SCAN_UNROLL):
            acc = fix_at(t, acc)
        acc_re, acc_im = lax.fori_loop(1, steps // SCAN_UNROLL, fix, acc)
        accr_ref[...] = acc_re
        acci_ref[...] = acc_im

        gbr = gr_ref[...].astype(BF16)
        gbi = gi_ref[...].astype(BF16)
        duseg_ref[...] = _dot_nt(gbr, br_ref[0]) + _dot_nt(gbi, bi_ref[0])
        _from_segments(duseg_ref, dyseg_ref, steps)
        du_ref[...] = (dyseg_ref[...] + dk_ref[...] * dy_ref[...]).astype(BF16)
        dbr = _dot_tn(ub, gbr)
        dbi = _dot_tn(ub, gbi)

        @pl.when(c == 0)
        def _():
            dbr_ref[0] = dbr
            dbi_ref[0] = dbi
            dcr_ref[0] = dcr
            dci_ref[0] = dci
            ddk_ref[...] = ddk

        @pl.when(c > 0)
        def _():
            dbr_ref[0] += dbr
            dbi_ref[0] += dbi
            dcr_ref[0] += dcr
            dci_ref[0] += dci
            ddk_ref[...] += ddk

        @pl.when(c == nc - 1)
        def _():
            dar_ref[...] = jnp.sum(acc_re, axis=0, keepdims=True)
            dai_ref[...] = jnp.sum(acc_im, axis=0, keepdims=True)

    rev = lambda c: nc - 1 - c
    row = pl.BlockSpec((1, blk), lambda j, c: (0, j))
    b_mat = pl.BlockSpec((1, LANES, blk), lambda j, c: (j, 0, 0))
    c_mat = pl.BlockSpec((1, blk, LANES), lambda j, c: (j, 0, 0))
    tok = pl.BlockSpec((chunk, LANES), lambda j, c: (rev(c), j))
    state = pl.BlockSpec((chunk, blk), lambda j, c: (rev(c), j))
    enter = pl.BlockSpec((SUBLANES, blk), lambda j, c: (rev(c), j))
    chan = pl.BlockSpec((1, LANES), lambda j, c: (0, j))
    f32 = lambda *s: jax.ShapeDtypeStruct(s, F32)
    return _hosted_call(
        body, carry, _edge_2d(N_SSM_BLOCKS, nc), name="ssm_bwd", grid=(N_SSM_BLOCKS, nc),
        in_specs=[tok, tok, state, state, enter, enter, row, row, b_mat, b_mat, c_mat, c_mat, chan],
        out_specs=[tok, b_mat, b_mat, c_mat, c_mat, row, row, chan],
        out_shape=[jax.ShapeDtypeStruct((T, SSM_W), BF16), f32(N_SSM_BLOCKS, LANES, blk), f32(N_SSM_BLOCKS, LANES, blk),
                   f32(N_SSM_BLOCKS, blk, LANES), f32(N_SSM_BLOCKS, blk, LANES), f32(1, STATES), f32(1, STATES), f32(1, SSM_W)],
        scratch_shapes=[pltpu.VMEM((chunk, LANES), F32), pltpu.VMEM((chunk, LANES), F32), pltpu.VMEM((chunk, LANES), F32),
                        pltpu.VMEM((chunk, blk), F32), pltpu.VMEM((chunk, blk), F32),
                        pltpu.VMEM((chunk, blk), F32), pltpu.VMEM((chunk, blk), F32),
                        pltpu.VMEM((SUBLANES, blk), F32), pltpu.VMEM((SUBLANES, blk), F32), pltpu.VMEM((SUBLANES, blk), F32)],
        compiler_params=_params(("arbitrary", "arbitrary"), VMEM_BIG),
        inputs=(dy, u, h_re, h_im, in_re, in_im, a_re, a_im, b_re, b_im, c_re, c_im, d_skip))


def _merge_forward(y, att, ga, gs, w_glu, w_ssm, w_attn):
    z = jax.nn.gelu(y)
    zb = z.astype(BF16)
    gl = jax.nn.sigmoid(_dot(zb, w_glu))
    z2b = (z * gl).astype(BF16)
    y_ssm = _dot(z2b, w_ssm)
    y_attn = _dot(att, w_attn)
    sa = jax.nn.sigmoid(ga)
    ss = jax.nn.sigmoid(gs)
    merged = (sa * y_attn + ss * y_ssm).astype(BF16)
    return z, zb, gl, z2b, y_ssm, y_attn, sa, ss, merged


def _merge_fwd(x, y, att, ga, gs, g2, g3, w_glu, w_ssm, w_attn, w_out, tile):
    T = x.shape[0]

    def body(x_ref, y_ref, att_ref, ga_ref, gs_ref, g2_ref, g3_ref, wg_ref, ws_ref, wa_ref, wo_ref, x1_ref, o_ref, h2_ref):
        merged = _merge_forward(y_ref[...], att_ref[...], ga_ref[...], gs_ref[...], wg_ref[...], ws_ref[...], wa_ref[...])[-1]
        o = _dot(merged, wo_ref[...])
        x1 = x_ref[...] + o * _rms_scale(o) * g2_ref[...]
        o_ref[...] = o
        x1_ref[...] = x1
        h2_ref[...] = (x1 * _rms_scale(x1) * g3_ref[...]).astype(BF16)

    tok = lambda w: pl.BlockSpec((tile, w), lambda i: (i, 0))
    vec = _const_spec((1, D_MODEL))
    return pl.pallas_call(
        body, name="merge_fwd", grid=(T // tile,),
        in_specs=[tok(D_MODEL), tok(SSM_W), tok(ATTN_W), tok(D_MODEL), tok(D_MODEL), vec, vec,
                  _const_spec((SSM_W, SSM_W)), _const_spec((SSM_W, D_MODEL)), _const_spec((ATTN_W, D_MODEL)),
                  _const_spec((D_MODEL, D_MODEL))],
        out_specs=[tok(D_MODEL), tok(D_MODEL), tok(D_MODEL)],
        out_shape=_hbm_out([jax.ShapeDtypeStruct((T, D_MODEL), F32), jax.ShapeDtypeStruct((T, D_MODEL), F32),
                            jax.ShapeDtypeStruct((T, D_MODEL), BF16)]),
        compiler_params=_params(("arbitrary",), VMEM_MID),
    )(*_in_hbm(x, y, att, ga, gs, g2, g3, w_glu, w_ssm, w_attn, w_out))


def _merge_bwd(dh2, dx2, x1, o, y, att, ga, gs, g2, g3, w_glu, w_ssm, w_attn, w_out, tile, carry=None):
    T = x1.shape[0]
    n_steps = T // tile

    def body(dh2_ref, dx2_ref, x1_ref, o_ref, y_ref, att_ref, ga_ref, gs_ref, g2_ref, g3_ref, wg_ref, ws_ref, wa_ref, wo_ref,
             dx1_ref, dgates_ref, datt_ref, dy_ref, dwg_hbm, dws_hbm, dwa_hbm, dwo_hbm, dg2_ref, dg3_ref,
             awg_ref, aws_ref, awa_ref, awo_ref):
        i = pl.program_id(0)
        x1v, ov = x1_ref[...], o_ref[...]
        dxn, dg3 = _rms_bwd(dh2_ref[...], x1v, _rms_scale(x1v), g3_ref[...])
        dx1 = dx2_ref[...] + dxn
        dx1_ref[...] = dx1
        do, dg2 = _rms_bwd(dx1, ov, _rms_scale(ov), g2_ref[...])
        dob = do.astype(BF16)

        yv = y_ref[...]
        att = att_ref[...]
        z, zb, gl, z2b, y_ssm, y_attn, sa, ss, merged = _merge_forward(
            yv, att, ga_ref[...], gs_ref[...], wg_ref[...], ws_ref[...], wa_ref[...])
        dmerged = _dot_nt(dob, wo_ref[...])
        dya = (dmerged * sa).astype(BF16)
        dys = (dmerged * ss).astype(BF16)
        dgates_ref[:, :D_MODEL] = (dmerged * y_attn * sa * (1.0 - sa)).astype(BF16)
        dgates_ref[:, D_MODEL:] = (dmerged * y_ssm * ss * (1.0 - ss)).astype(BF16)
        datt_ref[...] = _dot_nt(dya, wa_ref[...]).astype(BF16)
        dz2 = _dot_nt(dys, ws_ref[...])
        dpre = (dz2 * z * gl * (1.0 - gl)).astype(BF16)
        dz = dz2 * gl + _dot_nt(dpre, wg_ref[...])
        _, gelu_vjp = jax.vjp(jax.nn.gelu, yv)
        dy_ref[...] = gelu_vjp(dz)[0]

        grads = ((awo_ref, _dot_tn(merged, dob)), (awa_ref, _dot_tn(att, dya)),
                 (aws_ref, _dot_tn(z2b, dys)), (awg_ref, _dot_tn(zb, dpre)), (dg2_ref, dg2), (dg3_ref, dg3))

        @pl.when(i == 0)
        def _():
            for ref, val in grads:
                ref[...] = val

        @pl.when(i > 0)
        def _():
            for ref, val in grads:
                ref[...] += val

        @pl.when(i == n_steps - 1)
        def _():
            pltpu.sync_copy(awg_ref, dwg_hbm)
            pltpu.sync_copy(aws_ref, dws_hbm)
            pltpu.sync_copy(awa_ref, dwa_hbm)
            pltpu.sync_copy(awo_ref, dwo_hbm)

    tok = lambda w: pl.BlockSpec((tile, w), lambda i: (i, 0))
    vec = _const_spec((1, D_MODEL))
    any_ = pl.BlockSpec(memory_space=pl.ANY)
    vec_out = pl.BlockSpec((1, D_MODEL), lambda i: (0, 0))
    f32 = lambda *s: jax.ShapeDtypeStruct(s, F32)
    bf = lambda *s: jax.ShapeDtypeStruct(s, BF16)
    return _hosted_call(
        body, carry, _edge_1d(n_steps), name="merge_bwd", grid=(n_steps,),
        in_specs=[tok(D_MODEL), tok(D_MODEL), tok(D_MODEL), tok(D_MODEL), tok(SSM_W), tok(ATTN_W), tok(D_MODEL), tok(D_MODEL),
                  vec, vec, _const_spec((SSM_W, SSM_W)), _const_spec((SSM_W, D_MODEL)), _const_spec((ATTN_W, D_MODEL)),
                  _const_spec((D_MODEL, D_MODEL))],
        out_specs=[tok(D_MODEL), tok(2 * D_MODEL), tok(ATTN_W), tok(SSM_W), any_, any_, any_, any_, vec_out, vec_out],
        out_shape=[f32(T, D_MODEL), bf(T, 2 * D_MODEL), bf(T, ATTN_W), f32(T, SSM_W),
                   f32(SSM_W, SSM_W), f32(SSM_W, D_MODEL), f32(ATTN_W, D_MODEL), f32(D_MODEL, D_MODEL),
                   f32(1, D_MODEL), f32(1, D_MODEL)],
        scratch_shapes=[pltpu.VMEM((SSM_W, SSM_W), F32), pltpu.VMEM((SSM_W, D_MODEL), F32),
                        pltpu.VMEM((ATTN_W, D_MODEL), F32), pltpu.VMEM((D_MODEL, D_MODEL), F32)],
        compiler_params=_params(("arbitrary",), VMEM_BIG),
        inputs=(dh2, dx2, x1, o, y, att, ga, gs, g2, g3, w_glu, w_ssm, w_attn, w_out))


FF_SHARD = D_FF // N_DEV


def _mlp_fwd(h2, x1, target, g4, w_ff_in, w_ff_out, tile):
    T = h2.shape[0]
    col_chunk = 2 * FF_SHARD

    def body(h2_ref, x1_ref, tg_ref, g4_ref, wi_ref, wo_ref, a_ref, dfo_ref, dx2_ref, loss_ref, dg4_ref, rr_ref):
        i = pl.program_id(0)
        h2v = h2_ref[...]
        for c in range(D_FF // col_chunk):
            cols = slice(c * col_chunk, (c + 1) * col_chunk)
            a = _dot_nt(h2v, wi_ref[cols, :])
            a_ref[:, cols] = a.astype(BF16)
            ra = jnp.maximum(a, 0.0)
            rr_ref[:, cols] = (ra * ra).astype(BF16)
        f = _dot(rr_ref[...], wo_ref[...])
        r = _rms_scale(f)
        g = g4_ref[...]
        err = x1_ref[...] + f * r * g - tg_ref[...]
        dx2 = err * (1.0 / D_MODEL)
        dx2_ref[...] = dx2
        dfo, dg = _rms_bwd(dx2, f, r, g)
        dfo_ref[...] = dfo.astype(BF16)
        row = lax.broadcasted_iota(jnp.int32, (8, LANES), 0)
        col = lax.broadcasted_iota(jnp.int32, (8, LANES), 1)
        loss = jnp.where((row == 0) & (col == 0), (0.5 / D_MODEL) * jnp.sum(err * err), 0.0)

        @pl.when(i == 0)
        def _():
            loss_ref[...] = loss
            dg4_ref[...] = dg

        @pl.when(i > 0)
        def _():
            loss_ref[...] += loss
            dg4_ref[...] += dg

    tok = pl.BlockSpec((tile, D_MODEL), lambda i: (i, 0))
    return pl.pallas_call(
        body, name="mlp_fwd", grid=(T // tile,),
        in_specs=[tok, tok, tok, _const_spec((1, D_MODEL)), _const_spec((D_FF, D_MODEL)), _const_spec((D_FF, D_MODEL))],
        out_specs=[pl.BlockSpec((tile, D_FF), lambda i: (i, 0)), tok, tok,
                   pl.BlockSpec((8, LANES), lambda i: (0, 0)), pl.BlockSpec((1, D_MODEL), lambda i: (0, 0))],
        out_shape=_hbm_out([jax.ShapeDtypeStruct((T, D_FF), BF16), jax.ShapeDtypeStruct((T, D_MODEL), BF16),
                            jax.ShapeDtypeStruct((T, D_MODEL), F32), jax.ShapeDtypeStruct((8, LANES), F32),
                            jax.ShapeDtypeStruct((1, D_MODEL), F32)]),
        scratch_shapes=[pltpu.VMEM((tile, D_FF), BF16)],
        compiler_params=_params(("arbitrary",), VMEM_MAX),
    )(*_in_hbm(h2, x1, target, g4, w_ff_in.reshape(D_FF, D_MODEL), w_ff_out.reshape(D_FF, D_MODEL)))


def _mlp_weight_grads(dfo, a, h2, w_ff_out, row_chunk):
    T = h2.shape[0]

    def body(dfo_ref, h2_ref, a_ref, wo_ref, dwi_ref, dwo_ref, da_ref, rr_ref):
        def rows(r, _):
            sl = pl.ds(pl.multiple_of(r * row_chunk, row_chunk), row_chunk)
            ra = jnp.maximum(a_ref[sl, :].astype(F32), 0.0)
            da_ref[sl, :] = (_dot_nt(dfo_ref[sl, :], wo_ref[0]) * (2.0 * ra)).astype(BF16)
            rr_ref[sl, :] = (ra * ra).astype(BF16)
            return 0

        lax.fori_loop(0, T // row_chunk, rows, 0)
        dwo_ref[0] = _dot_tn(rr_ref[...], dfo_ref[...])
        dwi_ref[0] = _dot_tn(h2_ref[...], da_ref[...])

    return pl.pallas_call(
        body, name="mlp_weight_grads", grid=(N_DEV,),
        in_specs=[_const_spec((T, D_MODEL)), _const_spec((T, D_MODEL)), pl.BlockSpec((T, FF_SHARD), lambda k: (0, k)),
                  pl.BlockSpec((1, FF_SHARD, D_MODEL), lambda k: (k, 0, 0))],
        out_specs=[pl.BlockSpec((1, D_MODEL, FF_SHARD), lambda k: (k, 0, 0)),
                   pl.BlockSpec((1, FF_SHARD, D_MODEL), lambda k: (k, 0, 0)), pl.BlockSpec((T, FF_SHARD), lambda k: (0, k))],
        out_shape=_hbm_out([jax.ShapeDtypeStruct((N_DEV, D_MODEL, FF_SHARD), F32),
                            jax.ShapeDtypeStruct((N_DEV, FF_SHARD, D_MODEL), F32), jax.ShapeDtypeStruct((T, D_FF), BF16)]),
        scratch_shapes=[pltpu.VMEM((T, FF_SHARD), BF16)],
        compiler_params=_params(("arbitrary",), VMEM_MAX),
    )(*_in_hbm(dfo, h2, a, w_ff_out))


def _mlp_input_grad(da, w_ff_in_t, tile):
    T = da.shape[0]

    def body(da_ref, w_ref, o_ref):
        o_ref[...] = _dot(da_ref[...], w_ref[...])

    return pl.pallas_call(
        body, name="mlp_input_grad", grid=(T // tile,),
        in_specs=[pl.BlockSpec((tile, D_FF), lambda i: (i, 0)), _const_spec((D_FF, D_MODEL))],
        out_specs=pl.BlockSpec((tile, D_MODEL), lambda i: (i, 0)),
        out_shape=_hbm_out(jax.ShapeDtypeStruct((T, D_MODEL), F32)),
        compiler_params=_params(("arbitrary",), VMEM_MID),
    )(*_in_hbm(da, w_ff_in_t))


def _block_diag_in(b):
    bt = b.reshape(N_SSM_BLOCKS, 8, GROUP_CH, N_STATE)
    eye = jnp.eye(8, dtype=b.dtype)
    return jnp.einsum("jacp,ab->jacbp", bt, eye).reshape(N_SSM_BLOCKS, LANES, SSM_LANE_BLOCK)


def _block_diag_in_grad(g):
    g = g.reshape(N_SSM_BLOCKS, 8, GROUP_CH, 8, N_STATE)
    d = jnp.diagonal(g, axis1=1, axis2=3)
    return jnp.transpose(d, (0, 3, 1, 2)).reshape(N_GROUPS, GROUP_CH, N_STATE)


def _block_diag_out(c):
    ct = c.reshape(N_SSM_BLOCKS, 8, GROUP_CH, N_STATE)
    eye = jnp.eye(8, dtype=c.dtype)
    return jnp.einsum("jacp,ab->japbc", ct, eye).reshape(N_SSM_BLOCKS, SSM_LANE_BLOCK, LANES)


def _block_diag_out_grad(g):
    g = g.reshape(N_SSM_BLOCKS, 8, N_STATE, 8, GROUP_CH)
    d = jnp.diagonal(g, axis1=1, axis2=3)
    return jnp.transpose(d, (0, 3, 2, 1)).reshape(N_GROUPS, GROUP_CH, N_STATE)


def _tiles(T):
    return dict(proj=min(512, T), proj_bwd=min(512, T // 2), merge=min(512, T), merge_bwd=min(256, T),
                mlp_fwd=min(512, T), mlp_bwd=min(512, T), ssm_chunk=min(1024, T))


def _mesh_position():
    x, y, c = lax.axis_index("x"), lax.axis_index("y"), lax.axis_index("c")
    other_chips = [(1 - x, y), (x, 1 - y), (1 - x, 1 - y)]
    return x, y, c, other_chips


def _gather_carry(arrays):
    n = len(arrays)

    def copies(ins, outs, sems):
        send_sems, recv_sems, local_sems = sems
        x, y, c, chips = _mesh_position()
        me, sibling = (x, y, c), (x, y, 1 - c)

        def copy(a, k, block, to, src=None):
            px, py, pc = block
            dst = outs[a].at[4 * px + 2 * py + pc]
            return pltpu.make_async_remote_copy(
                src_ref=dst if src is None else src, dst_ref=dst, send_sem=send_sems.at[7 * a + k],
                recv_sem=recv_sems.at[7 * a + k], device_id=to, device_id_type=MESH_IDS)

        mine = [pltpu.make_async_copy(ins[a], outs[a].at[4 * x + 2 * y + c], local_sems.at[a]) for a in range(n)]
        first = []
        for a in range(n):
            first.append(copy(a, 0, me, sibling, src=ins[a]))
            first += [copy(a, 1 + j, me, (*chip, c), src=ins[a]) for j, chip in enumerate(chips)]
        return copy, mine, first, me, sibling, chips, c

    def start(ins, outs, sems):
        _, mine, first, *_ = copies(ins, outs, sems)
        for cp in mine + first:
            cp.start()

    def finish(ins, outs, sems):
        copy, mine, first, me, sibling, chips, c = copies(ins, outs, sems)
        passed = []
        for a in range(n):
            for j, chip in enumerate(chips):
                copy(a, 1 + j, (*chip, c), me).wait_recv()
                passed.append(copy(a, 4 + j, (*chip, c), sibling))
                passed[-1].start()
        for a in range(n):
            copy(a, 0, sibling, me).wait_recv()
            for j, chip in enumerate(chips):
                copy(a, 4 + j, (*chip, 1 - c), me).wait_recv()
        for cp in first + passed:
            cp.wait_send()
        for cp in mine:
            cp.wait()

    return _Carry(arrays, [jax.ShapeDtypeStruct((N_DEV,) + a.shape, a.dtype) for a in arrays],
                  [pltpu.SemaphoreType.DMA((7 * n,)), pltpu.SemaphoreType.DMA((7 * n,)), pltpu.SemaphoreType.DMA((n,))],
                  start, finish)


def _pairwise_carry(arrays, n_slots, make_copies):
    n = len(arrays)

    def start(ins, outs, sems):
        for cp in make_copies(ins, outs, sems):
            cp.start()

    def finish(ins, outs, sems):
        for cp in make_copies(ins, outs, sems):
            cp.wait()

    return _Carry(arrays, [jax.ShapeDtypeStruct((n_slots,) + a.shape[1:], a.dtype) for a in arrays],
                  [pltpu.SemaphoreType.DMA((n_slots * n,)), pltpu.SemaphoreType.DMA((n_slots * n,))], start, finish)


def _sibling_carry(grads):
    def make_copies(ins, outs, sems):
        x, y, c, _ = _mesh_position()
        return [pltpu.make_async_remote_copy(
            src_ref=ins[a].at[2 * ch + (1 - c)], dst_ref=outs[a].at[ch], send_sem=sems[0].at[4 * a + ch],
            recv_sem=sems[1].at[4 * a + ch], device_id=(x, y, 1 - c), device_id_type=MESH_IDS)
            for a in range(len(grads)) for ch in range(4)]

    return _pairwise_carry(grads, 4, make_copies)


def _chips_carry(sums):
    def make_copies(ins, outs, sems):
        x, y, c, chips = _mesh_position()
        return [pltpu.make_async_remote_copy(
            src_ref=ins[a].at[2 * px + py], dst_ref=outs[a].at[j], send_sem=sems[0].at[3 * a + j],
            recv_sem=sems[1].at[3 * a + j], device_id=(px, py, c), device_id_type=MESH_IDS)
            for a in range(len(sums)) for j, (px, py) in enumerate(chips)]

    return _pairwise_carry(sums, 3, make_copies)


def _row_tile(rows, cols):
    t = max(8, min(rows, (1 << 18) // cols // 8 * 8))
    while rows % t:
        t -= 8
    return t


def _add_sibling(grads8, recv, core, name):
    _, R, C = grads8.shape
    tr = _row_tile(R, C)
    g4 = grads8.reshape(4, 2, R, C)

    def body(core_ref, g_ref, r_ref, o_ref, ob_ref):
        s = g_ref[0] + r_ref[...]
        o_ref[...] = s
        ob_ref[...] = s.astype(BF16)

    out = pl.BlockSpec((1, tr, C), lambda ch, r, core_ref: (ch, r, 0))
    return pl.pallas_call(
        body, name=name,
        grid_spec=pltpu.PrefetchScalarGridSpec(
            num_scalar_prefetch=1, grid=(4, R // tr),
            in_specs=[pl.BlockSpec((1, 1, tr, C), lambda ch, r, core_ref: (ch, core_ref[0], r, 0)),
                      pl.BlockSpec((1, tr, C), lambda ch, r, core_ref: (ch, r, 0))],
            out_specs=[out, out]),
        out_shape=_hbm_out([jax.ShapeDtypeStruct((4, R, C), F32), jax.ShapeDtypeStruct((4, R, C), BF16)]),
        compiler_params=_params(("arbitrary", "arbitrary")),
    )(core, *_in_hbm(g4, recv))


def _adam_math(w, g, m, v):
    m = ADAM_B1 * m + (1.0 - ADAM_B1) * g
    v = ADAM_B2 * v + (1.0 - ADAM_B2) * jnp.square(g)
    m_hat = m / (1.0 - ADAM_B1 ** ADAM_STEP)
    v_hat = v / (1.0 - ADAM_B2 ** ADAM_STEP)
    delta = -ADAM_LR * (m_hat / (jnp.sqrt(v_hat) + ADAM_EPS) + ADAM_WD * w)
    return delta, m, v


def _adam_big(w, m, v, chip_sums, recv, chip, name):
    R, C = w.shape
    tr = _row_tile(R, C)

    def body(chip_ref, w_ref, m_ref, v_ref, s_ref, r_ref, g_ref, d_ref, nm_ref, nv_ref):
        g = s_ref[0] + r_ref[0].astype(F32) + r_ref[1].astype(F32) + r_ref[2].astype(F32)
        g_ref[...] = g
        d_ref[...], nm_ref[...], nv_ref[...] = _adam_math(w_ref[...], g, m_ref[...], v_ref[...])

    blk = pl.BlockSpec((tr, C), lambda r, chip_ref: (r, 0))
    return pl.pallas_call(
        body, name=name,
        grid_spec=pltpu.PrefetchScalarGridSpec(
            num_scalar_prefetch=1, grid=(R // tr,),
            in_specs=[blk, blk, blk, pl.BlockSpec((1, tr, C), lambda r, chip_ref: (chip_ref[0], r, 0)),
                      pl.BlockSpec((3, tr, C), lambda r, chip_ref: (0, r, 0))],
            out_specs=[blk] * 4),
        out_shape=[jax.ShapeDtypeStruct((R, C), F32)] * 4,
        compiler_params=_params(("arbitrary",)),
    )(chip, *_in_hbm(w, m, v, chip_sums, recv))


def _sum_partials(partials, name):
    def body(p_ref, g_ref):
        g = p_ref[0]
        for d in range(1, partials.shape[0]):
            g = g + p_ref[d]
        g_ref[...] = g

    return pl.pallas_call(body, name=name, grid=(1,), in_specs=[_whole(partials.shape)], out_specs=_whole(partials.shape[1:]),
                          out_shape=jax.ShapeDtypeStruct(partials.shape[1:], F32))(*_in_hbm(partials))


def _adam_small(ws, ms, vs, gs):
    n = len(ws)

    def body(*refs):
        w_refs, m_refs, v_refs, g_refs = (refs[i * n:(i + 1) * n] for i in range(4))
        d_refs, nm_refs, nv_refs = (refs[(4 + i) * n:(5 + i) * n] for i in range(3))
        for j in range(n):
            d_refs[j][...], nm_refs[j][...], nv_refs[j][...] = _adam_math(
                w_refs[j][...], g_refs[j][...], m_refs[j][...], v_refs[j][...])

    specs = [_whole(w.shape) for w in ws]
    outs = pl.pallas_call(body, name="adam_small", grid=(1,), in_specs=specs * 4, out_specs=specs * 3,
                          out_shape=[jax.ShapeDtypeStruct(w.shape, F32) for w in ws] * 3,
                          compiler_params=_params(("arbitrary",), VMEM_MID))(*_in_hbm(*ws, *ms, *vs, *gs))
    return outs[:n], outs[n:2 * n], outs[2 * n:]


PACK_QUANTUM = SUBLANES * LANES


def _pack(named, names):
    parts = []
    for nme in names:
        flat = named[nme].reshape(-1)
        parts.append(jnp.pad(flat, (0, -flat.size % PACK_QUANTUM)))
    return jnp.concatenate(parts).reshape(-1, LANES)


def _unpack(packed, shapes, names):
    flat = packed.reshape(-1)
    out, pos = {}, 0
    for nme in names:
        size = math.prod(shapes[nme])
        out[nme] = flat[pos:pos + size].reshape(shapes[nme])
        pos += size + (-size % PACK_QUANTUM)
    return out


BIG = ("w_in", "w_glu", "w_attn_branch", "w_ssm_branch", "w_out", "w_ff_in", "w_ff_out")
COLUMN_SHARDED = ("w_in", "w_attn_branch", "w_ssm_branch", "w_ff_in")
SMALL = ("norm_mix_pre", "norm_mix_post", "norm_mlp_pre", "norm_mlp_post", "rel_bias", "sinks", "lam_re", "lam_im",
         "log_dt", "b_re", "b_im", "c_re", "c_im", "d_skip")
SWAPPED_SMALL = ("rel_bias", "b_re", "b_im")
SMALL_LATE = ("rel_bias", "sinks", "loss")
SMALL_BEFORE_ATTN_BWD = tuple(n for n in SMALL if n not in SMALL_LATE + ("norm_mix_pre",))
ALL_WEIGHTS = ("norm_mix_pre", "norm_mix_post", "norm_mlp_pre", "norm_mlp_post", "w_in", "rel_bias", "sinks", "lam_re",
               "lam_im", "log_dt", "b_re", "b_im", "c_re", "c_im", "d_skip", "w_glu", "w_attn_branch", "w_ssm_branch",
               "w_out", "w_ff_in", "w_ff_out")


def _full_from_gathered(name, gathered):
    _, r, c = gathered.shape
    if name in COLUMN_SHARDED:
        return jnp.transpose(gathered, (1, 0, 2)).reshape(r, N_DEV * c)
    return gathered.reshape(N_DEV * r, c)


def _blocks_from_full(name, full):
    r, c = full.shape
    if name in COLUMN_SHARDED:
        return jnp.transpose(full.reshape(r, N_DEV, c // N_DEV), (1, 0, 2))
    return full.reshape(N_DEV, r // N_DEV, c)


def kernel(x, norm_mix_pre, norm_mix_post, norm_mlp_pre, norm_mlp_post, w_in, rel_bias, sinks, lam_re, lam_im, log_dt, b_re, b_im, c_re, c_im, d_skip, w_glu, w_attn_branch, w_ssm_branch, w_out, w_ff_in, w_ff_out, loss_target, m_norm_mix_pre, m_norm_mix_post, m_norm_mlp_pre, m_norm_mlp_post, m_w_in, m_rel_bias, m_sinks, m_lam_re, m_lam_im, m_log_dt, m_b_re, m_b_im, m_c_re, m_c_im, m_d_skip, m_w_glu, m_w_attn_branch, m_w_ssm_branch, m_w_out, m_w_ff_in, m_w_ff_out, v_norm_mix_pre, v_norm_mix_post, v_norm_mlp_pre, v_norm_mlp_post, v_w_in, v_rel_bias, v_sinks, v_lam_re, v_lam_im, v_log_dt, v_b_re, v_b_im, v_c_re, v_c_im, v_d_skip, v_w_glu, v_w_attn_branch, v_w_ssm_branch, v_w_out, v_w_ff_in, v_w_ff_out):
    args = dict(locals())
    w = {n: args[n] for n in ALL_WEIGHTS}
    m = {n: args["m_" + n] for n in ALL_WEIGHTS}
    v = {n: args["v_" + n] for n in ALL_WEIGHTS}
    core = lax.axis_index("c").astype(jnp.int32).reshape(1)
    chip = (2 * lax.axis_index("x") + lax.axis_index("y")).astype(jnp.int32).reshape(1)
    xs, target = x[0], loss_target[0]
    t = _tiles(xs.shape[0])
    local = lambda d, n: d[n][0].T if n == "w_in" else d[n][0]
    shard = {n: local(w, n).astype(BF16) for n in BIG}
    shard["w_ff_in"] = shard["w_ff_in"].T
    view = lambda n, a: jnp.swapaxes(a, -1, -2) if n in SWAPPED_SMALL else a
    small = {n: (view(n, w[n]) if n == "rel_bias" else view(n, w[n])[0]) for n in SMALL}
    g1, g2, g3, g4 = (small[n].reshape(1, D_MODEL) for n in ("norm_mix_pre", "norm_mix_post", "norm_mlp_pre", "norm_mlp_post"))
    bucket = jnp.asarray(_bucket_table())
    rel_b, sink = small["rel_bias"], small["sinks"].reshape(1, N_HEADS)
    lam_r, lam_i = small["lam_re"].reshape(1, STATES), small["lam_im"].reshape(1, STATES)
    ldt_rep = jnp.repeat(small["log_dt"].reshape(N_GROUPS), N_STATE).reshape(1, STATES)
    bd_re, bd_im = _block_diag_in(small["b_re"]), _block_diag_in(small["b_im"])
    cm_re, cm_im = _block_diag_out(small["c_re"]).astype(BF16), _block_diag_out(small["c_im"]).astype(BF16)
    dsk = small["d_skip"].reshape(1, SSM_W)

    (g_in,) = _run_carry(_gather_carry([shard["w_in"]]), "gather_w_in")
    wf_in = g_in.reshape(IN_W, D_MODEL)
    merge_names = ("w_glu", "w_attn_branch", "w_ssm_branch", "w_out")
    (q, k, vv, u, ga, gs, h), (wf_ff_in,) = _in_proj_fwd(xs, g1, wf_in, t["proj"], _gather_carry([shard["w_ff_in"]]))
    (att,), gathered = _attn_fwd(q, k, vv, bucket, rel_b, sink, _gather_carry([shard[n] for n in merge_names]))
    wf = {n: _full_from_gathered(n, g) for n, g in zip(merge_names, gathered)}
    a_re, a_im, bm_re, bm_im = _ssm_prep(lam_r, lam_i, ldt_rep, bd_re, bd_im)
    (y, h_re, h_im, in_re, in_im), (wf_ff_out,) = _ssm_fwd(
        u, a_re, a_im, bm_re, bm_im, cm_re, cm_im, dsk, t["ssm_chunk"], _gather_carry([shard["w_ff_out"]]))
    x1, o, h2 = _merge_fwd(xs, y, att, ga, gs, g2, g3, wf["w_glu"], wf["w_ssm_branch"], wf["w_attn_branch"], wf["w_out"],
                           t["merge"])
    a, dfo, dx2, loss_blk, dg4 = _mlp_fwd(h2, x1, target, g4, wf_ff_in, wf_ff_out, t["mlp_fwd"])

    def add_sibling(names, blocks, received):
        pairs = [_add_sibling(b, r, core, "add_sibling_" + n) for n, b, r in zip(names, blocks, received)]
        return [p[0] for p in pairs], [p[1] for p in pairs]

    ff_names = ("w_ff_in", "w_ff_out")
    dw_ff_in, dw_ff_out, da = _mlp_weight_grads(dfo, a, h2, wf_ff_out, t["mlp_bwd"])
    dh2 = _mlp_input_grad(da, wf_ff_in.reshape(D_FF, D_MODEL), t["mlp_bwd"])
    ff_blocks = [dw_ff_in, dw_ff_out]
    (dx1, dgates, datt, dy, dw_glu, dw_ssm, dw_attn, dw_out, dg2, dg3), ff_recv = _merge_bwd(
        dh2, dx2, x1, o, y, att, ga, gs, g2, g3, wf["w_glu"], wf["w_ssm_branch"], wf["w_attn_branch"], wf["w_out"],
        t["merge_bwd"], _sibling_carry(ff_blocks))
    ff_sums, ff_sums_bf = add_sibling(ff_names, ff_blocks, ff_recv)
    merge_blocks = [_blocks_from_full(n, g) for n, g in zip(merge_names, (dw_glu, dw_attn, dw_ssm, dw_out))]
    (du, dbm_re, dbm_im, dcm_re, dcm_im, da_re, da_im, dd_skip), carried = _ssm_bwd(
        dy, u, h_re, h_im, in_re, in_im, a_re, a_im, bm_re, bm_im, cm_re, cm_im, dsk, t["ssm_chunk"],
        _join(_chips_carry(ff_sums_bf), _sibling_carry(merge_blocks)))
    ff_from_chips, merge_recv = carried[:2], carried[2:]
    merge_sums, merge_sums_bf = add_sibling(merge_names, merge_blocks, merge_recv)
    dbd_re, dbd_im, dlam_re, dlam_im, dldt_rep = _ssm_prep_bwd(lam_r, lam_i, ldt_rep, bd_re, bd_im, dbm_re, dbm_im, da_re, da_im)
    dlog_dt = _group_sum(dldt_rep.reshape(N_GROUPS, N_STATE))
    shapes = {n: view(n, w[n]).shape for n in SMALL}
    shapes["loss"] = (1,)
    small_grads = dict(
        norm_mix_post=dg2, norm_mlp_pre=dg3, norm_mlp_post=dg4, lam_re=dlam_re, lam_im=dlam_im, log_dt=dlog_dt,
        b_re=_block_diag_in_grad(dbd_re), b_im=_block_diag_in_grad(dbd_im),
        c_re=_block_diag_out_grad(dcm_re), c_im=_block_diag_out_grad(dcm_im), d_skip=dd_skip)
    packed_early = _pack({n: small_grads[n].reshape(shapes[n]) for n in SMALL_BEFORE_ATTN_BWD}, SMALL_BEFORE_ATTN_BWD)
    (dq, dkv, attn_small), carried = _attn_bwd(
        q, k, vv, datt, bucket, rel_b, sink, _join(_chips_carry(merge_sums_bf), _gather_carry([packed_early])))
    merge_from_chips, partials_early = carried[:-1], carried[-1]

    dparts = (dq, dkv, du, dgates)
    dw_in_t = _in_proj_weight_grad(h, dparts)
    in_blocks = [dw_in_t.reshape(N_DEV, IN_W // N_DEV, D_MODEL)]
    n_tiles = xs.shape[0] // t["proj_bwd"]
    first_part = max(1, (3 * n_tiles) // 8)
    (gx_a, dg1_a), in_recv = _in_proj_input_grad(
        xs, g1, wf_in, dx1, dparts, t["proj_bwd"], 0, first_part, "in_proj_input_grad_a", _sibling_carry(in_blocks))
    in_sums, in_sums_bf = add_sibling(("w_in",), in_blocks, in_recv)
    late = dict(rel_bias=attn_small[:, :N_BUCKETS, 0], sinks=attn_small[:, N_BUCKETS, 0], loss=loss_blk[0:1, 0])
    packed_late = _pack({n: late[n].reshape(shapes[n]) for n in SMALL_LATE}, SMALL_LATE)
    (gx_b, dg1_b), (in_from_chips, partials_late) = _in_proj_input_grad(
        xs, g1, wf_in, dx1, dparts, t["proj_bwd"], first_part, n_tiles - first_part, "in_proj_input_grad_b",
        _join(_chips_carry(in_sums_bf), _gather_carry([packed_late])))
    grad_x = jnp.concatenate([gx_a, gx_b], axis=0)
    (dg1_partials,) = _run_carry(_gather_carry([jnp.concatenate([dg1_a, dg1_b], axis=0)]), "gather_norm_grad")

    grads, deltas, new_m, new_v = {}, {}, {}, {}
    sums = dict(zip(ff_names + merge_names + ("w_in",), ff_sums + merge_sums + in_sums))
    received = dict(zip(ff_names + merge_names + ("w_in",), ff_from_chips + merge_from_chips + [in_from_chips]))
    for n in BIG:
        outs = _adam_big(local(w, n), local(m, n), local(v, n), sums[n], received[n], chip, "adam_" + n)
        grads[n], deltas[n], new_m[n], new_v[n] = ((o.T if n == "w_in" else o)[None] for o in outs)

    grads.update(_unpack(_sum_partials(partials_early, "sum_small_grads"), shapes, SMALL_BEFORE_ATTN_BWD))
    grads.update(_unpack(_sum_partials(partials_late, "sum_late_grads"), shapes, SMALL_LATE))
    grads["norm_mix_pre"] = _sum_partials(dg1_partials.reshape(2 * N_DEV, 1, D_MODEL), "sum_norm_grad")
    loss = grads.pop("loss").reshape(())
    small_out = _adam_small(*[[view(n, d[n]) for n in SMALL] for d in (w, m, v)], [grads[n] for n in SMALL])
    for store, vals in zip((deltas, new_m, new_v), small_out):
        store.update(zip(SMALL, vals))
    for store in (grads, deltas, new_m, new_v):
        store.update({n: view(n, store[n]) for n in SWAPPED_SMALL})

    return (loss, grad_x[None], *[grads[n] for n in ALL_WEIGHTS], *[deltas[n] for n in ALL_WEIGHTS],
            *[new_m[n] for n in ALL_WEIGHTS], *[new_v[n] for n in ALL_WEIGHTS])
```

```python
import functools
import math

import jax
import jax.numpy as jnp
import numpy as np
from jax import lax
from jax.experimental import pallas as pl
from jax.experimental.pallas import tpu as pltpu

F32 = jnp.float32
BF16 = jnp.bfloat16

D_MODEL = 1024
N_HEADS = 8
HEAD_DIM = 64
ATTN_W = 512
KV_W = 128
BLOCK = 128
N_BUCKETS = 32
SSM_W = 512
N_GROUPS = 32
N_STATE = 64
GROUP_CH = 16
STATES = N_GROUPS * N_STATE
D_FF = 4096
IN_W = 3328
SPLITS = (0, 512, 640, 768, 1280, 2304, 3328)
RMS_EPS = 1e-6
NEG_INF = -1e30
SUBLANES = 8
LANES = 128
SSM_LANE_BLOCK = 512
N_SSM_BLOCKS = STATES // SSM_LANE_BLOCK
VMEM_BIG = 52 * 1024 * 1024
VMEM_MID = 40 * 1024 * 1024
VMEM_MAX = 60 * 1024 * 1024

ADAM_LR = 0.001
ADAM_B1 = 0.9
ADAM_B2 = 0.999
ADAM_EPS = 1e-08
ADAM_WD = 0.01
ADAM_STEP = 10

N_DEV = 8


def _dot(a, b):
    return jnp.dot(a, b, preferred_element_type=F32)


def _dot_nt(a, b):
    return lax.dot_general(a, b, (((1,), (1,)), ((), ())), preferred_element_type=F32)


def _dot_tn(a, b):
    return lax.dot_general(a, b, (((0,), (0,)), ((), ())), preferred_element_type=F32)


def _rms_scale(x):
    return lax.rsqrt(jnp.mean(x * x, axis=-1, keepdims=True) + RMS_EPS)


def _rms_bwd(dy, x, r, g):
    t = dy * g
    dx = r * t - x * (r * r * r) * jnp.mean(t * x, axis=-1, keepdims=True)
    dg = jnp.sum(dy * x * r, axis=0, keepdims=True)
    return dx, dg


def _const_spec(shape):
    nd = len(shape)
    return pl.BlockSpec(shape, lambda *_: (0,) * nd, pipeline_mode=pl.Buffered(1))


def _in_hbm(*arrays):
    return tuple(pltpu.with_memory_space_constraint(a, pltpu.HBM) for a in arrays)


def _hbm_out(shapes):
    if isinstance(shapes, (list, tuple)):
        return [_hbm_out(s) for s in shapes]
    return shapes if isinstance(shapes, pl.MemoryRef) else pltpu.HBM(shapes.shape, shapes.dtype)


def _whole(shape):
    nd = len(shape)
    return pl.BlockSpec(shape, lambda *_: (0,) * nd)


def _params(sem, vmem=None):
    return pltpu.CompilerParams(dimension_semantics=sem, vmem_limit_bytes=vmem)


MESH_IDS = pl.DeviceIdType.MESH
HBM_SPEC = pl.BlockSpec(memory_space=pl.ANY)


class _Carry:
    def __init__(self, inputs, out_shapes, sems, start, finish):
        self.inputs, self.out_shapes, self.sems, self.start, self.finish = list(inputs), list(out_shapes), list(sems), start, finish


def _join(a, b):
    na_in, na_out, na_sem = len(a.inputs), len(a.out_shapes), len(a.sems)

    def start(ins, outs, sems):
        a.start(ins[:na_in], outs[:na_out], sems[:na_sem])
        b.start(ins[na_in:], outs[na_out:], sems[na_sem:])

    def finish(ins, outs, sems):
        a.finish(ins[:na_in], outs[:na_out], sems[:na_sem])
        b.finish(ins[na_in:], outs[na_out:], sems[na_sem:])

    return _Carry(a.inputs + b.inputs, a.out_shapes + b.out_shapes, a.sems + b.sems, start, finish)


def _hosted_call(body, carry, edge, *, name, grid, in_specs, out_specs, out_shape, scratch_shapes, compiler_params, inputs):
    n_in, n_out = len(in_specs), len(out_specs)
    inputs = [a if s.memory_space == pltpu.SMEM else _in_hbm(a)[0] for a, s in zip(inputs, in_specs)]
    out_shape = _hbm_out(list(out_shape))
    if carry is None:
        outs = pl.pallas_call(body, name=name, grid=grid, in_specs=in_specs, out_specs=out_specs, out_shape=out_shape,
                              scratch_shapes=scratch_shapes, compiler_params=compiler_params)(*inputs)
        return list(outs), []
    c_in, c_out, c_sem = len(carry.inputs), len(carry.out_shapes), len(carry.sems)

    def wrapped(*refs):
        ins, refs = refs[:n_in], refs[n_in:]
        cins, refs = refs[:c_in], refs[c_in:]
        outs, refs = refs[:n_out], refs[n_out:]
        couts, refs = refs[:c_out], refs[c_out:]
        scratch, csems = refs[:len(refs) - c_sem], refs[len(refs) - c_sem:]
        first, last = edge()

        @pl.when(first)
        def _():
            carry.start(cins, couts, csems)

        body(*ins, *outs, *scratch)

        @pl.when(last)
        def _():
            carry.finish(cins, couts, csems)

    outs = pl.pallas_call(
        wrapped, name=name, grid=grid, in_specs=list(in_specs) + [HBM_SPEC] * c_in,
        out_specs=list(out_specs) + [HBM_SPEC] * c_out, out_shape=out_shape + _hbm_out(carry.out_shapes),
        scratch_shapes=list(scratch_shapes) + carry.sems, compiler_params=compiler_params)(*inputs, *_in_hbm(*carry.inputs))
    return list(outs[:n_out]), list(outs[n_out:])


def _edge_1d(n_steps):
    return lambda: (pl.program_id(0) == 0, pl.program_id(0) == n_steps - 1)


def _edge_2d(n0, n1):
    return lambda: ((pl.program_id(0) == 0) & (pl.program_id(1) == 0),
                    (pl.program_id(0) == n0 - 1) & (pl.program_id(1) == n1 - 1))


def _run_carry(carry, name):
    c_in, c_out = len(carry.inputs), len(carry.out_shapes)

    def body(*refs):
        ins, outs, sems = refs[:c_in], refs[c_in:c_in + c_out], refs[c_in + c_out:]
        carry.start(ins, outs, sems)
        carry.finish(ins, outs, sems)

    return pl.pallas_call(body, name=name, in_specs=[HBM_SPEC] * c_in, out_specs=[HBM_SPEC] * c_out,
                          out_shape=_hbm_out(carry.out_shapes), scratch_shapes=carry.sems)(*_in_hbm(*carry.inputs))


def _in_proj_fwd(x, g1, w_in_t, tile, carry=None):
    T = x.shape[0]

    def body(x_ref, g_ref, w_ref, q_ref, k_ref, v_ref, u_ref, ga_ref, gs_ref, h_ref):
        xv = x_ref[...]
        h = (xv * _rms_scale(xv) * g_ref[...]).astype(BF16)
        h_ref[...] = h
        outs = (q_ref, k_ref, v_ref, u_ref, ga_ref, gs_ref)
        for p, o_ref in enumerate(outs):
            o_ref[...] = _dot_nt(h, w_ref[SPLITS[p]:SPLITS[p + 1], :]).astype(o_ref.dtype)

    widths = [SPLITS[p + 1] - SPLITS[p] for p in range(6)] + [D_MODEL]
    dtypes = [BF16, BF16, BF16, F32, F32, F32, BF16]
    return _hosted_call(
        body, carry, _edge_1d(T // tile), name="in_proj_fwd", grid=(T // tile,),
        in_specs=[pl.BlockSpec((tile, D_MODEL), lambda i: (i, 0)), _const_spec((1, D_MODEL)), _const_spec((IN_W, D_MODEL))],
        out_specs=[pl.BlockSpec((tile, w), lambda i: (i, 0)) for w in widths],
        out_shape=[jax.ShapeDtypeStruct((T, w), dt) for w, dt in zip(widths, dtypes)],
        scratch_shapes=[], compiler_params=_params(("arbitrary",), VMEM_MID), inputs=(x, g1, w_in_t))


PROJ_PARTS = (512, 256, 512, 2048)
PROJ_GRAD_BLOCK = 256


def _in_proj_weight_grad(h, dparts):
    T = h.shape[0]
    blocks = [wd // PROJ_GRAD_BLOCK for wd in PROJ_PARTS]
    starts = [sum(blocks[:p]) for p in range(len(blocks))]

    def body(h_ref, *refs):
        part_refs, o_ref = refs[:-1], refs[-1]
        j = pl.program_id(0)
        for p_ref, start, count in zip(part_refs, starts, blocks):
            @pl.when((j >= start) & (j < start + count))
            def _(p_ref=p_ref):
                o_ref[...] = _dot_tn(p_ref[...], h_ref[...])

    def part_spec(start, count):
        return pl.BlockSpec((T, PROJ_GRAD_BLOCK), lambda j: (0, jnp.clip(j - start, 0, count - 1)))

    return pl.pallas_call(
        body, name="in_proj_weight_grad", grid=(sum(blocks),),
        in_specs=[_const_spec((T, D_MODEL))] + [part_spec(s, c) for s, c in zip(starts, blocks)],
        out_specs=pl.BlockSpec((PROJ_GRAD_BLOCK, D_MODEL), lambda j: (j, 0)),
        out_shape=_hbm_out(jax.ShapeDtypeStruct((IN_W, D_MODEL), F32)),
        compiler_params=_params(("arbitrary",), VMEM_MID),
    )(*_in_hbm(h, *dparts))


def _in_proj_input_grad(x, g1, w_in_t, dx1, dparts, tile, first_tile, n_tiles, name, carry=None):
    offsets = [sum(PROJ_PARTS[:p]) for p in range(len(PROJ_PARTS))]

    def body(x_ref, g_ref, w_ref, dx1_ref, *refs):
        part_refs, (gx_ref, dg_ref) = refs[:len(PROJ_PARTS)], refs[len(PROJ_PARTS):]
        i = pl.program_id(0)
        xv = x_ref[...]
        r = _rms_scale(xv)
        g = g_ref[...]
        dh = sum(_dot(p_ref[...], w_ref[off:off + wd, :]) for p_ref, off, wd in zip(part_refs, offsets, PROJ_PARTS))
        dxn, dg = _rms_bwd(dh, xv, r, g)
        gx_ref[...] = dx1_ref[...] + dxn

        @pl.when(i == 0)
        def _():
            dg_ref[...] = dg

        @pl.when(i > 0)
        def _():
            dg_ref[...] += dg

    tok = lambda wd: pl.BlockSpec((tile, wd), lambda i: (i + first_tile, 0))
    return _hosted_call(
        body, carry, _edge_1d(n_tiles), name=name, grid=(n_tiles,),
        in_specs=[tok(D_MODEL), _const_spec((1, D_MODEL)), _const_spec((IN_W, D_MODEL)), tok(D_MODEL)] + [tok(wd) for wd in PROJ_PARTS],
        out_specs=[pl.BlockSpec((tile, D_MODEL), lambda i: (i, 0)), pl.BlockSpec((1, D_MODEL), lambda i: (0, 0))],
        out_shape=[jax.ShapeDtypeStruct((n_tiles * tile, D_MODEL), F32), jax.ShapeDtypeStruct((1, D_MODEL), F32)],
        scratch_shapes=[], compiler_params=_params(("arbitrary",), VMEM_MID), inputs=(x, g1, w_in_t, dx1, *dparts))


def _bucket_table():
    qi = np.arange(BLOCK)[:, None]
    kj = np.arange(2 * BLOCK)[None, :]
    dist = qi + BLOCK - kj
    max_exact = N_BUCKETS // 2
    d = np.maximum(dist, 0)
    df = np.maximum(d, 1).astype(np.float32)
    large = max_exact + (np.log(df / np.float32(max_exact)) / np.float32(math.log(BLOCK / max_exact))
                         * np.float32(N_BUCKETS - max_exact)).astype(np.int32)
    large = np.minimum(large, N_BUCKETS - 1)
    bucket = np.where(d < max_exact, d, large)
    return np.where((dist >= 0) & (dist < BLOCK), bucket, -1).astype(np.int32)


def _build_bias(bucket_ref, rb_ref, bias_ref):
    bk = bucket_ref[...]
    for h in range(N_HEADS):
        def add(b, acc, h=h):
            return acc + jnp.where(bk == b, rb_ref[h, b], 0.0)
        bias_ref[h] = lax.fori_loop(0, N_BUCKETS, add, jnp.zeros((BLOCK, 2 * BLOCK), F32))


def _kv_variants(prev_ref, cur_ref):
    cat = jnp.concatenate([prev_ref[...], cur_ref[...]], axis=0)
    lo = lax.broadcasted_iota(jnp.int32, cat.shape, 1) < HEAD_DIM
    zero = jnp.zeros_like(cat)
    head0_lo = jnp.where(lo, cat, zero)
    head1_hi = jnp.where(lo, zero, cat)
    return ((head0_lo, pltpu.roll(head0_lo, HEAD_DIM, 1)), (pltpu.roll(head1_hi, HEAD_DIM, 1), head1_hi))


def _merge_kv_grads(g):
    lo = lax.broadcasted_iota(jnp.int32, g[0][0].shape, 1) < HEAD_DIM
    return jnp.where(lo, g[0][0] + pltpu.roll(g[0][1], HEAD_DIM, 1), g[1][1] + pltpu.roll(g[1][0], HEAD_DIM, 1))


def _head_lanes(h):
    return slice((h // 2) * LANES, (h // 2 + 1) * LANES)


def _attn_probs(q_ref, kvar, bias_ref, sk_ref, valid, s_ref):
    for h in range(N_HEADS):
        s_ref[h] = _dot_nt(q_ref[:, _head_lanes(h)], kvar[h // 4][h % 2])
    head = lax.broadcasted_iota(jnp.int32, (N_HEADS, 1, 1), 0)
    sink = jnp.zeros((N_HEADS, 1, 1), F32)
    for h in range(N_HEADS):
        sink = jnp.where(head == h, sk_ref[0, h], sink)
    s = jnp.where(valid[None], s_ref[...] * (HEAD_DIM ** -0.5) + bias_ref[...], NEG_INF)
    m = jnp.maximum(jnp.max(s, axis=-1, keepdims=True), sink)
    p = jnp.exp(s - m)
    e_sink = jnp.exp(sink - m)
    inv = 1.0 / (jnp.sum(p, axis=-1, keepdims=True) + e_sink)
    return p * inv, e_sink * inv


def _attn_valid(bucket_ref, n):
    col = lax.broadcasted_iota(jnp.int32, (BLOCK, 2 * BLOCK), 1)
    return (bucket_ref[...] >= 0) & ((n > 0) | (col >= BLOCK))


def _attn_fwd(q, k, v, bucket, rel_bias, sinks, carry=None):
    T = q.shape[0]
    nb = T // BLOCK

    def body(q_ref, kc_ref, kp_ref, vc_ref, vp_ref, bucket_ref, rb_ref, sk_ref, o_ref, bias_ref, s_ref, p_ref):
        n = pl.program_id(0)

        @pl.when(n == 0)
        def _():
            _build_bias(bucket_ref, rb_ref, bias_ref)

        kvar = _kv_variants(kp_ref, kc_ref)
        vvar = _kv_variants(vp_ref, vc_ref)
        pr, _ = _attn_probs(q_ref, kvar, bias_ref, sk_ref, _attn_valid(bucket_ref, n), s_ref)
        p_ref[...] = pr.astype(BF16)
        for m in range(N_HEADS // 2):
            acc = _dot(p_ref[2 * m], vvar[m // 2][0]) + _dot(p_ref[2 * m + 1], vvar[m // 2][1])
            o_ref[:, m * LANES:(m + 1) * LANES] = acc.astype(o_ref.dtype)

    cur = lambda w: pl.BlockSpec((BLOCK, w), lambda n: (n, 0))
    prev = lambda w: pl.BlockSpec((BLOCK, w), lambda n: (jnp.maximum(n - 1, 0), 0))
    smem = pl.BlockSpec(memory_space=pltpu.SMEM)
    return _hosted_call(
        body, carry, _edge_1d(nb), name="attn_fwd", grid=(nb,),
        in_specs=[cur(ATTN_W), cur(KV_W), prev(KV_W), cur(KV_W), prev(KV_W), _const_spec((BLOCK, 2 * BLOCK)), smem, smem],
        out_specs=[cur(ATTN_W)],
        out_shape=[jax.ShapeDtypeStruct((T, ATTN_W), BF16)],
        scratch_shapes=[pltpu.VMEM((N_HEADS, BLOCK, 2 * BLOCK), F32), pltpu.VMEM((N_HEADS, BLOCK, 2 * BLOCK), F32),
                        pltpu.VMEM((N_HEADS, BLOCK, 2 * BLOCK), BF16)],
        compiler_params=_params(("arbitrary",)), inputs=(q, k, k, v, v, bucket, rel_bias, sinks))


ATTN_SMALL_ROWS = N_BUCKETS + SUBLANES


def _attn_bwd(q, k, v, datt, bucket, rel_bias, sinks, carry=None):
    T = q.shape[0]
    nb = T // BLOCK

    def body(q_ref, do_ref, kc_ref, kp_ref, vc_ref, vp_ref, bucket_ref, rb_ref, sk_ref,
             dq_ref, dkv_ref, small_ref, bias_ref, ds_sum_ref, dsink_ref, kcarry_ref, vcarry_ref,
             s_ref, dp_ref, p_ref, dsc_ref):
        n = pl.program_id(0)

        @pl.when(n == 0)
        def _():
            _build_bias(bucket_ref, rb_ref, bias_ref)
            ds_sum_ref[...] = jnp.zeros_like(ds_sum_ref)
            dsink_ref[...] = jnp.zeros_like(dsink_ref)
            kcarry_ref[...] = jnp.zeros_like(kcarry_ref)
            vcarry_ref[...] = jnp.zeros_like(vcarry_ref)

        @pl.when(n < nb)
        def _():
            kvar = _kv_variants(kp_ref, kc_ref)
            vvar = _kv_variants(vp_ref, vc_ref)
            pr, p_sink = _attn_probs(q_ref, kvar, bias_ref, sk_ref, _attn_valid(bucket_ref, n), s_ref)
            for h in range(N_HEADS):
                dp_ref[h] = _dot_nt(do_ref[:, _head_lanes(h)], vvar[h // 4][h % 2])
            dp = dp_ref[...]
            dsum = jnp.sum(pr * dp, axis=-1, keepdims=True)
            ds = pr * (dp - dsum)
            ds_sum_ref[...] += ds
            dsink_ref[...] -= jnp.sum(p_sink * dsum, axis=1, keepdims=True)
            dsc_ref[...] = (ds * (HEAD_DIM ** -0.5)).astype(BF16)
            p_ref[...] = pr.astype(BF16)
            for m in range(N_HEADS // 2):
                dqm = _dot(dsc_ref[2 * m], kvar[m // 2][0]) + _dot(dsc_ref[2 * m + 1], kvar[m // 2][1])
                dq_ref[:, m * LANES:(m + 1) * LANES] = dqm.astype(dq_ref.dtype)
            dk_var = [[None, None], [None, None]]
            dv_var = [[None, None], [None, None]]
            for kvh in range(2):
                for e in range(2):
                    heads = [h for h in range(N_HEADS) if h // 4 == kvh and h % 2 == e]
                    dk_var[kvh][e] = sum(_dot_tn(dsc_ref[h], q_ref[:, _head_lanes(h)]) for h in heads)
                    dv_var[kvh][e] = sum(_dot_tn(p_ref[h], do_ref[:, _head_lanes(h)]) for h in heads)
            dk_cat = _merge_kv_grads(dk_var)
            dv_cat = _merge_kv_grads(dv_var)

            @pl.when(n > 0)
            def _():
                dkv_ref[:, :KV_W] = (kcarry_ref[...] + dk_cat[:BLOCK]).astype(BF16)
                dkv_ref[:, KV_W:] = (vcarry_ref[...] + dv_cat[:BLOCK]).astype(BF16)

            kcarry_ref[...] = dk_cat[BLOCK:]
            vcarry_ref[...] = dv_cat[BLOCK:]

        @pl.when(n == nb)
        def _():
            dkv_ref[:, :KV_W] = kcarry_ref[...].astype(BF16)
            dkv_ref[:, KV_W:] = vcarry_ref[...].astype(BF16)
            bk = bucket_ref[...]
            row = lax.broadcasted_iota(jnp.int32, (N_HEADS, ATTN_SMALL_ROWS, LANES), 1)

            def add(b, acc):
                masked = jnp.where((bk == b)[None], ds_sum_ref[...], 0.0)
                val = jnp.sum(jnp.sum(masked, axis=1, keepdims=True), axis=2, keepdims=True)
                return acc + jnp.where(row == b, val, 0.0)

            small_ref[...] = lax.fori_loop(0, N_BUCKETS, add, jnp.where(row == N_BUCKETS, dsink_ref[...], 0.0))

    last = nb - 1
    cur = lambda w: pl.BlockSpec((BLOCK, w), lambda n: (jnp.minimum(n, last), 0))
    prev = lambda w: pl.BlockSpec((BLOCK, w), lambda n: (jnp.clip(n - 1, 0, last), 0))
    smem = pl.BlockSpec(memory_space=pltpu.SMEM)
    return _hosted_call(
        body, carry, _edge_1d(nb + 1), name="attn_bwd", grid=(nb + 1,),
        in_specs=[cur(ATTN_W), cur(ATTN_W), cur(KV_W), prev(KV_W), cur(KV_W), prev(KV_W),
                  _const_spec((BLOCK, 2 * BLOCK)), smem, smem],
        out_specs=[cur(ATTN_W), prev(2 * KV_W), pl.BlockSpec((N_HEADS, ATTN_SMALL_ROWS, LANES), lambda n: (0, 0, 0))],
        out_shape=[jax.ShapeDtypeStruct((T, ATTN_W), BF16), jax.ShapeDtypeStruct((T, 2 * KV_W), BF16),
                   jax.ShapeDtypeStruct((N_HEADS, ATTN_SMALL_ROWS, LANES), F32)],
        scratch_shapes=[pltpu.VMEM((N_HEADS, BLOCK, 2 * BLOCK), F32), pltpu.VMEM((N_HEADS, BLOCK, 2 * BLOCK), F32),
                        pltpu.VMEM((N_HEADS, 1, 1), F32), pltpu.VMEM((BLOCK, KV_W), F32), pltpu.VMEM((BLOCK, KV_W), F32),
                        pltpu.VMEM((N_HEADS, BLOCK, 2 * BLOCK), F32), pltpu.VMEM((N_HEADS, BLOCK, 2 * BLOCK), F32),
                        pltpu.VMEM((N_HEADS, BLOCK, 2 * BLOCK), BF16), pltpu.VMEM((N_HEADS, BLOCK, 2 * BLOCK), BF16)],
        compiler_params=_params(("arbitrary",)), inputs=(q, datt, k, k, v, v, bucket, rel_bias, sinks))


SCAN_UNROLL = 4


def _cmul(ar, ai, br, bi):
    return ar * br - ai * bi, ar * bi + ai * br


def _cmul_conj(ar, ai, br, bi):
    return ar * br + ai * bi, ar * bi - ai * br


def _ssm_discretize(lr, li, ldt):
    dt = jnp.exp(ldt)
    mag = jnp.exp(lr * dt)
    ab_re = mag * jnp.cos(li * dt)
    ab_im = mag * jnp.sin(li * dt)
    nr = ab_re - 1.0
    den = lr * lr + li * li
    f_re = (nr * lr + ab_im * li) / den
    f_im = (ab_im * lr - nr * li) / den
    return ab_re, ab_im, f_re, f_im


def _ssm_prep(lam_re, lam_im, ldt_rep, bd_re, bd_im):
    def body(lr_ref, li_ref, ldt_ref, bdr_ref, bdi_ref, ar_ref, ai_ref, br_ref, bi_ref):
        ab_re, ab_im, f_re, f_im = _ssm_discretize(lr_ref[...], li_ref[...], ldt_ref[...])
        ar_ref[...] = ab_re
        ai_ref[...] = ab_im
        bdr, bdi = bdr_ref[0], bdi_ref[0]
        br_ref[0] = (bdr * f_re - bdi * f_im).astype(BF16)
        bi_ref[0] = (bdi * f_re + bdr * f_im).astype(BF16)

    row = pl.BlockSpec((1, SSM_LANE_BLOCK), lambda j: (0, j))
    mat = pl.BlockSpec((1, LANES, SSM_LANE_BLOCK), lambda j: (j, 0, 0))
    return pl.pallas_call(
        body, name="ssm_prep", grid=(N_SSM_BLOCKS,),
        in_specs=[row, row, row, mat, mat], out_specs=[row, row, mat, mat],
        out_shape=[jax.ShapeDtypeStruct((1, STATES), F32)] * 2 + [jax.ShapeDtypeStruct((N_SSM_BLOCKS, LANES, SSM_LANE_BLOCK), BF16)] * 2,
        compiler_params=_params(("arbitrary",)),
    )(*_in_hbm(lam_re, lam_im, ldt_rep, bd_re, bd_im))


def _ssm_prep_bwd(lam_re, lam_im, ldt_rep, bd_re, bd_im, dbr, dbi, da_re, da_im):
    def body(lr_ref, li_ref, ldt_ref, bdr_ref, bdi_ref, dbr_ref, dbi_ref, dar_ref, dai_ref,
             dbdr_ref, dbdi_ref, dlr_ref, dli_ref, dldt_ref):
        lr, li, ldt = lr_ref[...], li_ref[...], ldt_ref[...]
        (_, _, f_re, f_im), vjp = jax.vjp(_ssm_discretize, lr, li, ldt)
        bdr, bdi, gbr, gbi = bdr_ref[0], bdi_ref[0], dbr_ref[0], dbi_ref[0]
        dbdr_ref[0] = gbr * f_re + gbi * f_im
        dbdi_ref[0] = gbi * f_re - gbr * f_im
        df_re = jnp.sum(gbr * bdr + gbi * bdi, axis=0, keepdims=True)
        df_im = jnp.sum(gbi * bdr - gbr * bdi, axis=0, keepdims=True)
        dlr, dli, dldt = vjp((dar_ref[...], dai_ref[...], df_re, df_im))
        dlr_ref[...] = dlr
        dli_ref[...] = dli
        dldt_ref[...] = dldt

    row = pl.BlockSpec((1, SSM_LANE_BLOCK), lambda j: (0, j))
    mat = pl.BlockSpec((1, LANES, SSM_LANE_BLOCK), lambda j: (j, 0, 0))
    mat_shape = jax.ShapeDtypeStruct((N_SSM_BLOCKS, LANES, SSM_LANE_BLOCK), F32)
    row_shape = jax.ShapeDtypeStruct((1, STATES), F32)
    return pl.pallas_call(
        body, name="ssm_prep_bwd", grid=(N_SSM_BLOCKS,),
        in_specs=[row, row, row, mat, mat, mat, mat, row, row], out_specs=[mat, mat, row, row, row],
        out_shape=[mat_shape, mat_shape, row_shape, row_shape, row_shape],
        compiler_params=_params(("arbitrary",)),
    )(*_in_hbm(lam_re, lam_im, ldt_rep, bd_re, bd_im, dbr, dbi, da_re, da_im))


def _group_sum(x):
    def body(x_ref, o_ref):
        o_ref[...] = jnp.sum(x_ref[...], axis=1, keepdims=True)
    return pl.pallas_call(body, name="ssm_group_sum", grid=(1,), in_specs=[_whole(x.shape)], out_specs=_whole((N_GROUPS, 1)),
                          out_shape=jax.ShapeDtypeStruct((N_GROUPS, 1), F32))(*_in_hbm(x))


def _power_table(ar, ai, p_re_ref, p_im_ref, steps):
    shape = (SUBLANES, SSM_LANE_BLOCK)
    p_re_ref[0:SUBLANES] = jnp.broadcast_to(ar, shape)
    p_im_ref[0:SUBLANES] = jnp.broadcast_to(ai, shape)
    m = 1
    while m < steps:
        rows = m * SUBLANES
        top_re = p_re_ref[rows - SUBLANES:rows]
        top_im = p_im_ref[rows - SUBLANES:rows]
        cur_re = p_re_ref[0:rows].reshape(m, SUBLANES, SSM_LANE_BLOCK)
        cur_im = p_im_ref[0:rows].reshape(m, SUBLANES, SSM_LANE_BLOCK)
        nxt_re, nxt_im = _cmul(cur_re, cur_im, top_re[None], top_im[None])
        p_re_ref[rows:2 * rows] = nxt_re.reshape(rows, SSM_LANE_BLOCK)
        p_im_ref[rows:2 * rows] = nxt_im.reshape(rows, SSM_LANE_BLOCK)
        m *= 2


def _to_segments(src_ref, dst_ref, steps):
    for s in range(SUBLANES):
        dst_ref[pl.ds(s, steps, stride=SUBLANES), :] = src_ref[s * steps:(s + 1) * steps, :]


def _from_segments(src_ref, dst_ref, steps):
    for s in range(SUBLANES):
        dst_ref[s * steps:(s + 1) * steps, :] = src_ref[pl.ds(s, steps, stride=SUBLANES), :]


def _segment_carries(e_re, e_im, an_re, an_im, c_re, c_im, reverse):
    order = range(SUBLANES - 1, -1, -1) if reverse else range(SUBLANES)
    ins_re, ins_im = [None] * SUBLANES, [None] * SUBLANES
    for s in order:
        ins_re[s], ins_im[s] = c_re, c_im
        pr, pi = _cmul(an_re, an_im, c_re, c_im)
        c_re = e_re[s:s + 1] + pr
        c_im = e_im[s:s + 1] + pi
    return jnp.concatenate(ins_re, axis=0), jnp.concatenate(ins_im, axis=0), c_re, c_im


def _ssm_fwd(u, a_re, a_im, b_re, b_im, c_re, c_im, d_skip, chunk, carry=None):
    T = u.shape[0]
    nc = T // chunk
    steps = chunk // SUBLANES
    blk = SSM_LANE_BLOCK

    def body(u_ref, ar_ref, ai_ref, br_ref, bi_ref, cr_ref, ci_ref, dk_ref,
             y_ref, hr_ref, hi_ref, inr_ref, ini_ref, useg_ref, yseg_ref, pr_ref, pi_ref, carry_ref):
        c = pl.program_id(1)
        ar, ai = ar_ref[...], ai_ref[...]

        @pl.when(c == 0)
        def _():
            _power_table(ar, ai, pr_ref, pi_ref, steps)
            carry_ref[...] = jnp.zeros_like(carry_ref)

        _to_segments(u_ref, useg_ref, steps)
        ub = useg_ref[...].astype(BF16)
        hr_ref[...] = _dot(ub, br_ref[0])
        hi_ref[...] = _dot(ub, bi_ref[0])
        first = slice(0, SUBLANES)

        def scan(t4, prev):
            for j in range(SCAN_UNROLL):
                rows = pl.ds(pl.multiple_of((t4 * SCAN_UNROLL + j) * SUBLANES, SUBLANES), SUBLANES)
                pr, pi = _cmul(pr_ref[first, :], pi_ref[first, :], prev[0], prev[1])
                prev = (pr + hr_ref[rows, :], pi + hi_ref[rows, :])
                hr_ref[rows, :] = prev[0]
                hi_ref[rows, :] = prev[1]
            return prev

        zero = jnp.zeros((SUBLANES, blk), F32)
        lax.fori_loop(0, steps // SCAN_UNROLL, scan, (zero, zero))

        top = slice(chunk - SUBLANES, chunk)
        in_re, in_im, out_re, out_im = _segment_carries(
            hr_ref[top, :], hi_ref[top, :], pr_ref[top, :][0:1], pi_ref[top, :][0:1],
            carry_ref[0:1, :], carry_ref[1:2, :], reverse=False)
        carry_ref[0:1, :] = out_re
        carry_ref[1:2, :] = out_im
        inr_ref[...] = in_re
        ini_ref[...] = in_im

        def fix(t4, _):
            for j in range(SCAN_UNROLL):
                rows = pl.ds(pl.multiple_of((t4 * SCAN_UNROLL + j) * SUBLANES, SUBLANES), SUBLANES)
                fr, fi = _cmul(pr_ref[rows, :], pi_ref[rows, :], in_re, in_im)
                hr_ref[rows, :] += fr
                hi_ref[rows, :] += fi
            return 0

        lax.fori_loop(0, steps // SCAN_UNROLL, fix, 0)

        yseg_ref[...] = _dot(hr_ref[...].astype(BF16), cr_ref[0]) - _dot(hi_ref[...].astype(BF16), ci_ref[0])
        _from_segments(yseg_ref, y_ref, steps)
        y_ref[...] += dk_ref[...] * u_ref[...]

    row = pl.BlockSpec((1, blk), lambda j, c: (0, j))
    b_mat = pl.BlockSpec((1, LANES, blk), lambda j, c: (j, 0, 0))
    c_mat = pl.BlockSpec((1, blk, LANES), lambda j, c: (j, 0, 0))
    tok = pl.BlockSpec((chunk, LANES), lambda j, c: (c, j))
    state = pl.BlockSpec((chunk, blk), lambda j, c: (c, j))
    enter = pl.BlockSpec((SUBLANES, blk), lambda j, c: (c, j))
    return _hosted_call(
        body, carry, _edge_2d(N_SSM_BLOCKS, nc), name="ssm_fwd", grid=(N_SSM_BLOCKS, nc),
        in_specs=[tok, row, row, b_mat, b_mat, c_mat, c_mat, pl.BlockSpec((1, LANES), lambda j, c: (0, j))],
        out_specs=[tok, state, state, enter, enter],
        out_shape=[jax.ShapeDtypeStruct((T, SSM_W), F32), jax.ShapeDtypeStruct((T, STATES), F32),
                   jax.ShapeDtypeStruct((T, STATES), F32), jax.ShapeDtypeStruct((nc * SUBLANES, STATES), F32),
                   jax.ShapeDtypeStruct((nc * SUBLANES, STATES), F32)],
        scratch_shapes=[pltpu.VMEM((chunk, LANES), F32), pltpu.VMEM((chunk, LANES), F32),
                        pltpu.VMEM((chunk, blk), F32), pltpu.VMEM((chunk, blk), F32), pltpu.VMEM((SUBLANES, blk), F32)],
        compiler_params=_params(("arbitrary", "arbitrary"), VMEM_MID),
        inputs=(u, a_re, a_im, b_re, b_im, c_re, c_im, d_skip))


def _ssm_bwd(dy, u, h_re, h_im, in_re, in_im, a_re, a_im, b_re, b_im, c_re, c_im, d_skip, chunk, carry=None):
    T = u.shape[0]
    nc = T // chunk
    steps = chunk // SUBLANES
    blk = SSM_LANE_BLOCK

    def body(dy_ref, u_ref, hr_ref, hi_ref, inr_ref, ini_ref, ar_ref, ai_ref, br_ref, bi_ref, cr_ref, ci_ref, dk_ref,
             du_ref, dbr_ref, dbi_ref, dcr_ref, dci_ref, dar_ref, dai_ref, ddk_ref,
             dyseg_ref, useg_ref, duseg_ref, gr_ref, gi_ref, pr_ref, pi_ref, carry_ref, accr_ref, acci_ref):
        c = pl.program_id(1)
        ar, ai = ar_ref[...], ai_ref[...]

        @pl.when(c == 0)
        def _():
            _power_table(ar, ai, pr_ref, pi_ref, steps)
            carry_ref[...] = jnp.zeros_like(carry_ref)
            accr_ref[...] = jnp.zeros_like(accr_ref)
            acci_ref[...] = jnp.zeros_like(acci_ref)

        _to_segments(dy_ref, dyseg_ref, steps)
        _to_segments(u_ref, useg_ref, steps)
        dyb = dyseg_ref[...].astype(BF16)
        ub = useg_ref[...].astype(BF16)
        gr_ref[...] = _dot_nt(dyb, cr_ref[0])
        gi_ref[...] = -_dot_nt(dyb, ci_ref[0])
        dcr = _dot_tn(hr_ref[...].astype(BF16), dyb)
        dci = -_dot_tn(hi_ref[...].astype(BF16), dyb)
        ddk = jnp.sum(dy_ref[...] * u_ref[...], axis=0, keepdims=True)

        first = slice(0, SUBLANES)

        def scan(k4, nxt):
            for j in range(SCAN_UNROLL):
                t = steps - 1 - (k4 * SCAN_UNROLL + j)
                rows = pl.ds(pl.multiple_of(t * SUBLANES, SUBLANES), SUBLANES)
                pr, pi = _cmul_conj(pr_ref[first, :], pi_ref[first, :], nxt[0], nxt[1])
                nxt = (pr + gr_ref[rows, :], pi + gi_ref[rows, :])
                gr_ref[rows, :] = nxt[0]
                gi_ref[rows, :] = nxt[1]
            return nxt

        top = slice(chunk - SUBLANES, chunk)
        zero = jnp.zeros((SUBLANES, blk), F32)
        lax.fori_loop(0, steps // SCAN_UNROLL, scan, (zero, zero))

        gin_re, gin_im, out_re, out_im = _segment_carries(
            gr_ref[0:SUBLANES, :], gi_ref[0:SUBLANES, :], pr_ref[top, :][0:1], -pi_ref[top, :][0:1],
            carry_ref[0:1, :], carry_ref[1:2, :], reverse=True)
        carry_ref[0:1, :] = out_re
        carry_ref[1:2, :] = out_im

        def fix_row(rows, prow, hp_re, hp_im, acc):
            fr, fi = _cmul_conj(pr_ref[prow, :], pi_ref[prow, :], gin_re, gin_im)
            g_re = gr_ref[rows, :] + fr
            g_im = gi_ref[rows, :] + fi
            gr_ref[rows, :] = g_re
            gi_ref[rows, :] = g_im
            return acc[0] + g_re * hp_re + g_im * hp_im, acc[1] + g_im * hp_re - g_re * hp_im

        def fix_at(t, acc):
            aligned = (lambda r: r * SUBLANES) if isinstance(t, int) else (lambda r: pl.multiple_of(r * SUBLANES, SUBLANES))
            rows, before, prow = (pl.ds(aligned(r), SUBLANES) for r in (t, t - 1, steps - 1 - t))
            return fix_row(rows, prow, hr_ref[before, :], hi_ref[before, :], acc)

        def fix(t4, acc):
            for j in range(SCAN_UNROLL):
                acc = fix_at(t4 * SCAN_UNROLL + j, acc)
            return acc

        acc = fix_row(first, top, inr_ref[...], ini_ref[...], (accr_ref[...], acci_ref[...]))
        for t in range(1, SCAN_UNROLL):
            acc = fix_at(t, acc)
        acc_re, acc_im = lax.fori_loop(1, steps // SCAN_UNROLL, fix, acc)
        accr_ref[...] = acc_re
        acci_ref[...] = acc_im

        gbr = gr_ref[...].astype(BF16)
        gbi = gi_ref[...].astype(BF16)
        duseg_ref[...] = _dot_nt(gbr, br_ref[0]) + _dot_nt(gbi, bi_ref[0])
        _from_segments(duseg_ref, dyseg_ref, steps)
        du_ref[...] = (dyseg_ref[...] + dk_ref[...] * dy_ref[...]).astype(BF16)
        dbr = _dot_tn(ub, gbr)
        dbi = _dot_tn(ub, gbi)

        @pl.when(c == 0)
        def _():
            dbr_ref[0] = dbr
            dbi_ref[0] = dbi
            dcr_ref[0] = dcr
            dci_ref[0] = dci
            ddk_ref[...] = ddk

        @pl.when(c > 0)
        def _():
            dbr_ref[0] += dbr
            dbi_ref[0] += dbi
            dcr_ref[0] += dcr
            dci_ref[0] += dci
            ddk_ref[...] += ddk

        @pl.when(c == nc - 1)
        def _():
            dar_ref[...] = jnp.sum(acc_re, axis=0, keepdims=True)
            dai_ref[...] = jnp.sum(acc_im, axis=0, keepdims=True)

    rev = lambda c: nc - 1 - c
    row = pl.BlockSpec((1, blk), lambda j, c: (0, j))
    b_mat = pl.BlockSpec((1, LANES, blk), lambda j, c: (j, 0, 0))
    c_mat = pl.BlockSpec((1, blk, LANES), lambda j, c: (j, 0, 0))
    tok = pl.BlockSpec((chunk, LANES), lambda j, c: (rev(c), j))
    state = pl.BlockSpec((chunk, blk), lambda j, c: (rev(c), j))
    enter = pl.BlockSpec((SUBLANES, blk), lambda j, c: (rev(c), j))
    chan = pl.BlockSpec((1, LANES), lambda j, c: (0, j))
    f32 = lambda *s: jax.ShapeDtypeStruct(s, F32)
    return _hosted_call(
        body, carry, _edge_2d(N_SSM_BLOCKS, nc), name="ssm_bwd", grid=(N_SSM_BLOCKS, nc),
        in_specs=[tok, tok, state, state, enter, enter, row, row, b_mat, b_mat, c_mat, c_mat, chan],
        out_specs=[tok, b_mat, b_mat, c_mat, c_mat, row, row, chan],
        out_shape=[jax.ShapeDtypeStruct((T, SSM_W), BF16), f32(N_SSM_BLOCKS, LANES, blk), f32(N_SSM_BLOCKS, LANES, blk),
                   f32(N_SSM_BLOCKS, blk, LANES), f32(N_SSM_BLOCKS, blk, LANES), f32(1, STATES), f32(1, STATES), f32(1, SSM_W)],
        scratch_shapes=[pltpu.VMEM((chunk, LANES), F32), pltpu.VMEM((chunk, LANES), F32), pltpu.VMEM((chunk, LANES), F32),
                        pltpu.VMEM((chunk, blk), F32), pltpu.VMEM((chunk, blk), F32),
                        pltpu.VMEM((chunk, blk), F32), pltpu.VMEM((chunk, blk), F32),
                        pltpu.VMEM((SUBLANES, blk), F32), pltpu.VMEM((SUBLANES, blk), F32), pltpu.VMEM((SUBLANES, blk), F32)],
        compiler_params=_params(("arbitrary", "arbitrary"), VMEM_BIG),
        inputs=(dy, u, h_re, h_im, in_re, in_im, a_re, a_im, b_re, b_im, c_re, c_im, d_skip))


def _merge_forward(y, att, ga, gs, w_glu, w_ssm, w_attn):
    z = jax.nn.gelu(y)
    zb = z.astype(BF16)
    gl = jax.nn.sigmoid(_dot(zb, w_glu))
    z2b = (z * gl).astype(BF16)
    y_ssm = _dot(z2b, w_ssm)
    y_attn = _dot(att, w_attn)
    sa = jax.nn.sigmoid(ga)
    ss = jax.nn.sigmoid(gs)
    merged = (sa * y_attn + ss * y_ssm).astype(BF16)
    return z, zb, gl, z2b, y_ssm, y_attn, sa, ss, merged


def _merge_fwd(x, y, att, ga, gs, g2, g3, w_glu, w_ssm, w_attn, w_out, tile):
    T = x.shape[0]

    def body(x_ref, y_ref, att_ref, ga_ref, gs_ref, g2_ref, g3_ref, wg_ref, ws_ref, wa_ref, wo_ref, x1_ref, o_ref, h2_ref):
        merged = _merge_forward(y_ref[...], att_ref[...], ga_ref[...], gs_ref[...], wg_ref[...], ws_ref[...], wa_ref[...])[-1]
        o = _dot(merged, wo_ref[...])
        x1 = x_ref[...] + o * _rms_scale(o) * g2_ref[...]
        o_ref[...] = o
        x1_ref[...] = x1
        h2_ref[...] = (x1 * _rms_scale(x1) * g3_ref[...]).astype(BF16)

    tok = lambda w: pl.BlockSpec((tile, w), lambda i: (i, 0))
    vec = _const_spec((1, D_MODEL))
    return pl.pallas_call(
        body, name="merge_fwd", grid=(T // tile,),
        in_specs=[tok(D_MODEL), tok(SSM_W), tok(ATTN_W), tok(D_MODEL), tok(D_MODEL), vec, vec,
                  _const_spec((SSM_W, SSM_W)), _const_spec((SSM_W, D_MODEL)), _const_spec((ATTN_W, D_MODEL)),
                  _const_spec((D_MODEL, D_MODEL))],
        out_specs=[tok(D_MODEL), tok(D_MODEL), tok(D_MODEL)],
        out_shape=_hbm_out([jax.ShapeDtypeStruct((T, D_MODEL), F32), jax.ShapeDtypeStruct((T, D_MODEL), F32),
                            jax.ShapeDtypeStruct((T, D_MODEL), BF16)]),
        compiler_params=_params(("arbitrary",), VMEM_MID),
    )(*_in_hbm(x, y, att, ga, gs, g2, g3, w_glu, w_ssm, w_attn, w_out))


def _merge_bwd(dh2, dx2, x1, o, y, att, ga, gs, g2, g3, w_glu, w_ssm, w_attn, w_out, tile, carry=None):
    T = x1.shape[0]
    n_steps = T // tile

    def body(dh2_ref, dx2_ref, x1_ref, o_ref, y_ref, att_ref, ga_ref, gs_ref, g2_ref, g3_ref, wg_ref, ws_ref, wa_ref, wo_ref,
             dx1_ref, dgates_ref, datt_ref, dy_ref, merged_ref, dob_ref, dya_ref, dys_ref, z2b_ref, zb_ref, dpre_ref,
             dg2_ref, dg3_ref):
        i = pl.program_id(0)
        x1v, ov = x1_ref[...], o_ref[...]
        dxn, dg3 = _rms_bwd(dh2_ref[...], x1v, _rms_scale(x1v), g3_ref[...])
        dx1 = dx2_ref[...] + dxn
        dx1_ref[...] = dx1
        do, dg2 = _rms_bwd(dx1, ov, _rms_scale(ov), g2_ref[...])
        dob = do.astype(BF16)

        yv = y_ref[...]
        att = att_ref[...]
        z, zb, gl, z2b, y_ssm, y_attn, sa, ss, merged = _merge_forward(
            yv, att, ga_ref[...], gs_ref[...], wg_ref[...], ws_ref[...], wa_ref[...])
        dmerged = _dot_nt(dob, wo_ref[...])
        dya = (dmerged * sa).astype(BF16)
        dys = (dmerged * ss).astype(BF16)
        dgates_ref[:, :D_MODEL] = (dmerged * y_attn * sa * (1.0 - sa)).astype(BF16)
        dgates_ref[:, D_MODEL:] = (dmerged * y_ssm * ss * (1.0 - ss)).astype(BF16)
        datt_ref[...] = _dot_nt(dya, wa_ref[...]).astype(BF16)
        dz2 = _dot_nt(dys, ws_ref[...])
        dpre = (dz2 * z * gl * (1.0 - gl)).astype(BF16)
        dz = dz2 * gl + _dot_nt(dpre, wg_ref[...])
        _, gelu_vjp = jax.vjp(jax.nn.gelu, yv)
        dy_ref[...] = gelu_vjp(dz)[0]

        for ref, val in ((merged_ref, merged), (dob_ref, dob), (dya_ref, dya), (dys_ref, dys), (z2b_ref, z2b),
                         (zb_ref, zb), (dpre_ref, dpre)):
            ref[...] = val

        @pl.when(i == 0)
        def _():
            dg2_ref[...] = dg2
            dg3_ref[...] = dg3

        @pl.when(i > 0)
        def _():
            dg2_ref[...] += dg2
            dg3_ref[...] += dg3

    tok = lambda w: pl.BlockSpec((tile, w), lambda i: (i, 0))
    vec = _const_spec((1, D_MODEL))
    vec_out = pl.BlockSpec((1, D_MODEL), lambda i: (0, 0))
    f32 = lambda *s: jax.ShapeDtypeStruct(s, F32)
    bf = lambda *s: jax.ShapeDtypeStruct(s, BF16)
    operand_widths = (D_MODEL, D_MODEL, D_MODEL, D_MODEL, SSM_W, SSM_W, SSM_W)
    return _hosted_call(
        body, carry, _edge_1d(n_steps), name="merge_bwd", grid=(n_steps,),
        in_specs=[tok(D_MODEL), tok(D_MODEL), tok(D_MODEL), tok(D_MODEL), tok(SSM_W), tok(ATTN_W), tok(D_MODEL), tok(D_MODEL),
                  vec, vec, _const_spec((SSM_W, SSM_W)), _const_spec((SSM_W, D_MODEL)), _const_spec((ATTN_W, D_MODEL)),
                  _const_spec((D_MODEL, D_MODEL))],
        out_specs=[tok(D_MODEL), tok(2 * D_MODEL), tok(ATTN_W), tok(SSM_W)] + [tok(wd) for wd in operand_widths]
        + [vec_out, vec_out],
        out_shape=[f32(T, D_MODEL), bf(T, 2 * D_MODEL), bf(T, ATTN_W), f32(T, SSM_W)] + [bf(T, wd) for wd in operand_widths]
        + [f32(1, D_MODEL), f32(1, D_MODEL)],
        scratch_shapes=[], compiler_params=_params(("arbitrary",), VMEM_BIG),
        inputs=(dh2, dx2, x1, o, y, att, ga, gs, g2, g3, w_glu, w_ssm, w_attn, w_out))


LONG_K_BLOCK = 256


def _weight_grad(lhs, rhs, name):
    T, M = lhs.shape
    N = rhs.shape[1]

    def body(l_ref, r_ref, o_ref):
        o_ref[...] = _dot_tn(l_ref[...], r_ref[...])

    return pl.pallas_call(
        body, name=name, grid=(M // LONG_K_BLOCK,),
        in_specs=[pl.BlockSpec((T, LONG_K_BLOCK), lambda j: (0, j)), _const_spec((T, N))],
        out_specs=pl.BlockSpec((LONG_K_BLOCK, N), lambda j: (j, 0)),
        out_shape=_hbm_out(jax.ShapeDtypeStruct((M, N), F32)),
        compiler_params=_params(("arbitrary",), VMEM_MID),
    )(*_in_hbm(lhs, rhs))


FF_SHARD = D_FF // N_DEV


def _mlp_fwd(h2, x1, target, g4, w_ff_in, w_ff_out, tile):
    T = h2.shape[0]
    col_chunk = 2 * FF_SHARD

    def body(h2_ref, x1_ref, tg_ref, g4_ref, wi_ref, wo_ref, a_ref, dfo_ref, dx2_ref, loss_ref, dg4_ref, rr_ref):
        i = pl.program_id(0)
        h2v = h2_ref[...]
        for c in range(D_FF // col_chunk):
            cols = slice(c * col_chunk, (c + 1) * col_chunk)
            a = _dot_nt(h2v, wi_ref[cols, :])
            a_ref[:, cols] = a.astype(BF16)
            ra = jnp.maximum(a, 0.0)
            rr_ref[:, cols] = (ra * ra).astype(BF16)
        f = _dot(rr_ref[...], wo_ref[...])
        r = _rms_scale(f)
        g = g4_ref[...]
        err = x1_ref[...] + f * r * g - tg_ref[...]
        dx2 = err * (1.0 / D_MODEL)
        dx2_ref[...] = dx2
        dfo, dg = _rms_bwd(dx2, f, r, g)
        dfo_ref[...] = dfo.astype(BF16)
        row = lax.broadcasted_iota(jnp.int32, (8, LANES), 0)
        col = lax.broadcasted_iota(jnp.int32, (8, LANES), 1)
        loss = jnp.where((row == 0) & (col == 0), (0.5 / D_MODEL) * jnp.sum(err * err), 0.0)

        @pl.when(i == 0)
        def _():
            loss_ref[...] = loss
            dg4_ref[...] = dg

        @pl.when(i > 0)
        def _():
            loss_ref[...] += loss
            dg4_ref[...] += dg

    tok = pl.BlockSpec((tile, D_MODEL), lambda i: (i, 0))
    return pl.pallas_call(
        body, name="mlp_fwd", grid=(T // tile,),
        in_specs=[tok, tok, tok, _const_spec((1, D_MODEL)), _const_spec((D_FF, D_MODEL)), _const_spec((D_FF, D_MODEL))],
        out_specs=[pl.BlockSpec((tile, D_FF), lambda i: (i, 0)), tok, tok,
                   pl.BlockSpec((8, LANES), lambda i: (0, 0)), pl.BlockSpec((1, D_MODEL), lambda i: (0, 0))],
        out_shape=_hbm_out([jax.ShapeDtypeStruct((T, D_FF), BF16), jax.ShapeDtypeStruct((T, D_MODEL), BF16),
                            jax.ShapeDtypeStruct((T, D_MODEL), F32), jax.ShapeDtypeStruct((8, LANES), F32),
                            jax.ShapeDtypeStruct((1, D_MODEL), F32)]),
        scratch_shapes=[pltpu.VMEM((tile, D_FF), BF16)],
        compiler_params=_params(("arbitrary",), VMEM_MAX),
    )(*_in_hbm(h2, x1, target, g4, w_ff_in.reshape(D_FF, D_MODEL), w_ff_out.reshape(D_FF, D_MODEL)))


def _mlp_weight_grads(dfo, a, h2, w_ff_out, row_chunk):
    T = h2.shape[0]

    def body(dfo_ref, h2_ref, a_ref, wo_ref, dwi_ref, dwo_ref, da_ref, rr_ref):
        def rows(r, _):
            sl = pl.ds(pl.multiple_of(r * row_chunk, row_chunk), row_chunk)
            ra = jnp.maximum(a_ref[sl, :].astype(F32), 0.0)
            da_ref[sl, :] = (_dot_nt(dfo_ref[sl, :], wo_ref[0]) * (2.0 * ra)).astype(BF16)
            rr_ref[sl, :] = (ra * ra).astype(BF16)
            return 0

        lax.fori_loop(0, T // row_chunk, rows, 0)
        dwo_ref[0] = _dot_tn(rr_ref[...], dfo_ref[...])
        dwi_ref[0] = _dot_tn(h2_ref[...], da_ref[...])

    return pl.pallas_call(
        body, name="mlp_weight_grads", grid=(N_DEV,),
        in_specs=[_const_spec((T, D_MODEL)), _const_spec((T, D_MODEL)), pl.BlockSpec((T, FF_SHARD), lambda k: (0, k)),
                  pl.BlockSpec((1, FF_SHARD, D_MODEL), lambda k: (k, 0, 0))],
        out_specs=[pl.BlockSpec((1, D_MODEL, FF_SHARD), lambda k: (k, 0, 0)),
                   pl.BlockSpec((1, FF_SHARD, D_MODEL), lambda k: (k, 0, 0)), pl.BlockSpec((T, FF_SHARD), lambda k: (0, k))],
        out_shape=_hbm_out([jax.ShapeDtypeStruct((N_DEV, D_MODEL, FF_SHARD), F32),
                            jax.ShapeDtypeStruct((N_DEV, FF_SHARD, D_MODEL), F32), jax.ShapeDtypeStruct((T, D_FF), BF16)]),
        scratch_shapes=[pltpu.VMEM((T, FF_SHARD), BF16)],
        compiler_params=_params(("arbitrary",), VMEM_MAX),
    )(*_in_hbm(dfo, h2, a, w_ff_out))


def _mlp_input_grad(da, w_ff_in_t, tile):
    T = da.shape[0]

    def body(da_ref, w_ref, o_ref):
        o_ref[...] = _dot(da_ref[...], w_ref[...])

    return pl.pallas_call(
        body, name="mlp_input_grad", grid=(T // tile,),
        in_specs=[pl.BlockSpec((tile, D_FF), lambda i: (i, 0)), _const_spec((D_FF, D_MODEL))],
        out_specs=pl.BlockSpec((tile, D_MODEL), lambda i: (i, 0)),
        out_shape=_hbm_out(jax.ShapeDtypeStruct((T, D_MODEL), F32)),
        compiler_params=_params(("arbitrary",), VMEM_MID),
    )(*_in_hbm(da, w_ff_in_t))


def _block_diag_in(b):
    bt = b.reshape(N_SSM_BLOCKS, 8, GROUP_CH, N_STATE)
    eye = jnp.eye(8, dtype=b.dtype)
    return jnp.einsum("jacp,ab->jacbp", bt, eye).reshape(N_SSM_BLOCKS, LANES, SSM_LANE_BLOCK)


def _block_diag_in_grad(g):
    g = g.reshape(N_SSM_BLOCKS, 8, GROUP_CH, 8, N_STATE)
    d = jnp.diagonal(g, axis1=1, axis2=3)
    return jnp.transpose(d, (0, 3, 1, 2)).reshape(N_GROUPS, GROUP_CH, N_STATE)


def _block_diag_out(c):
    ct = c.reshape(N_SSM_BLOCKS, 8, GROUP_CH, N_STATE)
    eye = jnp.eye(8, dtype=c.dtype)
    return jnp.einsum("jacp,ab->japbc", ct, eye).reshape(N_SSM_BLOCKS, SSM_LANE_BLOCK, LANES)


def _block_diag_out_grad(g):
    g = g.reshape(N_SSM_BLOCKS, 8, N_STATE, 8, GROUP_CH)
    d = jnp.diagonal(g, axis1=1, axis2=3)
    return jnp.transpose(d, (0, 3, 2, 1)).reshape(N_GROUPS, GROUP_CH, N_STATE)


def _tiles(T):
    return dict(proj=min(512, T), proj_bwd=min(512, T // 2), merge=min(512, T), merge_bwd=min(256, T),
                mlp_fwd=min(512, T), mlp_bwd=min(512, T), ssm_chunk=min(1024, T))


def _mesh_position():
    x, y, c = lax.axis_index("x"), lax.axis_index("y"), lax.axis_index("c")
    other_chips = [(1 - x, y), (x, 1 - y), (1 - x, 1 - y)]
    return x, y, c, other_chips


def _gather_carry(arrays):
    n = len(arrays)

    def copies(ins, outs, sems):
        send_sems, recv_sems, local_sems = sems
        x, y, c, chips = _mesh_position()
        me, sibling = (x, y, c), (x, y, 1 - c)

        def copy(a, k, block, to, src=None):
            px, py, pc = block
            dst = outs[a].at[4 * px + 2 * py + pc]
            return pltpu.make_async_remote_copy(
                src_ref=dst if src is None else src, dst_ref=dst, send_sem=send_sems.at[7 * a + k],
                recv_sem=recv_sems.at[7 * a + k], device_id=to, device_id_type=MESH_IDS)

        mine = [pltpu.make_async_copy(ins[a], outs[a].at[4 * x + 2 * y + c], local_sems.at[a]) for a in range(n)]
        first = []
        for a in range(n):
            first.append(copy(a, 0, me, sibling, src=ins[a]))
            first += [copy(a, 1 + j, me, (*chip, c), src=ins[a]) for j, chip in enumerate(chips)]
        return copy, mine, first, me, sibling, chips, c

    def start(ins, outs, sems):
        _, mine, first, *_ = copies(ins, outs, sems)
        for cp in mine + first:
            cp.start()

    def finish(ins, outs, sems):
        copy, mine, first, me, sibling, chips, c = copies(ins, outs, sems)
        passed = []
        for a in range(n):
            for j, chip in enumerate(chips):
                copy(a, 1 + j, (*chip, c), me).wait_recv()
                passed.append(copy(a, 4 + j, (*chip, c), sibling))
                passed[-1].start()
        for a in range(n):
            copy(a, 0, sibling, me).wait_recv()
            for j, chip in enumerate(chips):
                copy(a, 4 + j, (*chip, 1 - c), me).wait_recv()
        for cp in first + passed:
            cp.wait_send()
        for cp in mine:
            cp.wait()

    return _Carry(arrays, [jax.ShapeDtypeStruct((N_DEV,) + a.shape, a.dtype) for a in arrays],
                  [pltpu.SemaphoreType.DMA((7 * n,)), pltpu.SemaphoreType.DMA((7 * n,)), pltpu.SemaphoreType.DMA((n,))],
                  start, finish)


def _pairwise_carry(arrays, n_slots, make_copies):
    n = len(arrays)

    def start(ins, outs, sems):
        for cp in make_copies(ins, outs, sems):
            cp.start()

    def finish(ins, outs, sems):
        for cp in make_copies(ins, outs, sems):
            cp.wait()

    return _Carry(arrays, [jax.ShapeDtypeStruct((n_slots,) + a.shape[1:], a.dtype) for a in arrays],
                  [pltpu.SemaphoreType.DMA((n_slots * n,)), pltpu.SemaphoreType.DMA((n_slots * n,))], start, finish)


def _sibling_carry(grads):
    def make_copies(ins, outs, sems):
        x, y, c, _ = _mesh_position()
        return [pltpu.make_async_remote_copy(
            src_ref=ins[a].at[2 * ch + (1 - c)], dst_ref=outs[a].at[ch], send_sem=sems[0].at[4 * a + ch],
            recv_sem=sems[1].at[4 * a + ch], device_id=(x, y, 1 - c), device_id_type=MESH_IDS)
            for a in range(len(grads)) for ch in range(4)]

    return _pairwise_carry(grads, 4, make_copies)


def _chips_carry(sums):
    def make_copies(ins, outs, sems):
        x, y, c, chips = _mesh_position()
        return [pltpu.make_async_remote_copy(
            src_ref=ins[a].at[2 * px + py], dst_ref=outs[a].at[j], send_sem=sems[0].at[3 * a + j],
            recv_sem=sems[1].at[3 * a + j], device_id=(px, py, c), device_id_type=MESH_IDS)
            for a in range(len(sums)) for j, (px, py) in enumerate(chips)]

    return _pairwise_carry(sums, 3, make_copies)


def _row_tile(rows, cols):
    t = max(8, min(rows, (1 << 18) // cols // 8 * 8))
    while rows % t:
        t -= 8
    return t


def _add_sibling(grads8, recv, core, name):
    _, R, C = grads8.shape
    tr = _row_tile(R, C)
    g4 = grads8.reshape(4, 2, R, C)

    def body(core_ref, g_ref, r_ref, o_ref, ob_ref):
        s = g_ref[0] + r_ref[...]
        o_ref[...] = s
        ob_ref[...] = s.astype(BF16)

    out = pl.BlockSpec((1, tr, C), lambda ch, r, core_ref: (ch, r, 0))
    return pl.pallas_call(
        body, name=name,
        grid_spec=pltpu.PrefetchScalarGridSpec(
            num_scalar_prefetch=1, grid=(4, R // tr),
            in_specs=[pl.BlockSpec((1, 1, tr, C), lambda ch, r, core_ref: (ch, core_ref[0], r, 0)),
                      pl.BlockSpec((1, tr, C), lambda ch, r, core_ref: (ch, r, 0))],
            out_specs=[out, out]),
        out_shape=_hbm_out([jax.ShapeDtypeStruct((4, R, C), F32), jax.ShapeDtypeStruct((4, R, C), BF16)]),
        compiler_params=_params(("arbitrary", "arbitrary")),
    )(core, *_in_hbm(g4, recv))


def _adam_math(w, g, m, v):
    m = ADAM_B1 * m + (1.0 - ADAM_B1) * g
    v = ADAM_B2 * v + (1.0 - ADAM_B2) * jnp.square(g)
    m_hat = m / (1.0 - ADAM_B1 ** ADAM_STEP)
    v_hat = v / (1.0 - ADAM_B2 ** ADAM_STEP)
    delta = -ADAM_LR * (m_hat / (jnp.sqrt(v_hat) + ADAM_EPS) + ADAM_WD * w)
    return delta, m, v


def _adam_big(w, m, v, chip_sums, recv, chip, name):
    R, C = w.shape
    tr = _row_tile(R, C)

    def body(chip_ref, w_ref, m_ref, v_ref, s_ref, r_ref, g_ref, d_ref, nm_ref, nv_ref):
        g = s_ref[0] + r_ref[0].astype(F32) + r_ref[1].astype(F32) + r_ref[2].astype(F32)
        g_ref[...] = g
        d_ref[...], nm_ref[...], nv_ref[...] = _adam_math(w_ref[...], g, m_ref[...], v_ref[...])

    blk = pl.BlockSpec((tr, C), lambda r, chip_ref: (r, 0))
    return pl.pallas_call(
        body, name=name,
        grid_spec=pltpu.PrefetchScalarGridSpec(
            num_scalar_prefetch=1, grid=(R // tr,),
            in_specs=[blk, blk, blk, pl.BlockSpec((1, tr, C), lambda r, chip_ref: (chip_ref[0], r, 0)),
                      pl.BlockSpec((3, tr, C), lambda r, chip_ref: (0, r, 0))],
            out_specs=[blk] * 4),
        out_shape=[jax.ShapeDtypeStruct((R, C), F32)] * 4,
        compiler_params=_params(("arbitrary",)),
    )(chip, *_in_hbm(w, m, v, chip_sums, recv))


def _sum_partials(partials, name):
    def body(p_ref, g_ref):
        g = p_ref[0]
        for d in range(1, partials.shape[0]):
            g = g + p_ref[d]
        g_ref[...] = g

    return pl.pallas_call(body, name=name, grid=(1,), in_specs=[_whole(partials.shape)], out_specs=_whole(partials.shape[1:]),
                          out_shape=jax.ShapeDtypeStruct(partials.shape[1:], F32))(*_in_hbm(partials))


def _adam_small(ws, ms, vs, gs):
    n = len(ws)

    def body(*refs):
        w_refs, m_refs, v_refs, g_refs = (refs[i * n:(i + 1) * n] for i in range(4))
        d_refs, nm_refs, nv_refs = (refs[(4 + i) * n:(5 + i) * n] for i in range(3))
        for j in range(n):
            d_refs[j][...], nm_refs[j][...], nv_refs[j][...] = _adam_math(
                w_refs[j][...], g_refs[j][...], m_refs[j][...], v_refs[j][...])

    specs = [_whole(w.shape) for w in ws]
    outs = pl.pallas_call(body, name="adam_small", grid=(1,), in_specs=specs * 4, out_specs=specs * 3,
                          out_shape=[jax.ShapeDtypeStruct(w.shape, F32) for w in ws] * 3,
                          compiler_params=_params(("arbitrary",), VMEM_MID))(*_in_hbm(*ws, *ms, *vs, *gs))
    return outs[:n], outs[n:2 * n], outs[2 * n:]


PACK_QUANTUM = SUBLANES * LANES


def _pack(named, names):
    parts = []
    for nme in names:
        flat = named[nme].reshape(-1)
        parts.append(jnp.pad(flat, (0, -flat.size % PACK_QUANTUM)))
    return jnp.concatenate(parts).reshape(-1, LANES)


def _unpack(packed, shapes, names):
    flat = packed.reshape(-1)
    out, pos = {}, 0
    for nme in names:
        size = math.prod(shapes[nme])
        out[nme] = flat[pos:pos + size].reshape(shapes[nme])
        pos += size + (-size % PACK_QUANTUM)
    return out


BIG = ("w_in", "w_glu", "w_attn_branch", "w_ssm_branch", "w_out", "w_ff_in", "w_ff_out")
COLUMN_SHARDED = ("w_in", "w_attn_branch", "w_ssm_branch", "w_ff_in")
SMALL = ("norm_mix_pre", "norm_mix_post", "norm_mlp_pre", "norm_mlp_post", "rel_bias", "sinks", "lam_re", "lam_im",
         "log_dt", "b_re", "b_im", "c_re", "c_im", "d_skip")
SWAPPED_SMALL = ("rel_bias", "b_re", "b_im")
SMALL_LATE = ("rel_bias", "sinks", "loss")
SMALL_BEFORE_ATTN_BWD = tuple(n for n in SMALL if n not in SMALL_LATE + ("norm_mix_pre",))
ALL_WEIGHTS = ("norm_mix_pre", "norm_mix_post", "norm_mlp_pre", "norm_mlp_post", "w_in", "rel_bias", "sinks", "lam_re",
               "lam_im", "log_dt", "b_re", "b_im", "c_re", "c_im", "d_skip", "w_glu", "w_attn_branch", "w_ssm_branch",
               "w_out", "w_ff_in", "w_ff_out")


def _full_from_gathered(name, gathered):
    _, r, c = gathered.shape
    if name in COLUMN_SHARDED:
        return jnp.transpose(gathered, (1, 0, 2)).reshape(r, N_DEV * c)
    return gathered.reshape(N_DEV * r, c)


def _blocks_from_full(name, full):
    r, c = full.shape
    if name in COLUMN_SHARDED:
        return jnp.transpose(full.reshape(r, N_DEV, c // N_DEV), (1, 0, 2))
    return full.reshape(N_DEV, r // N_DEV, c)


def kernel(x, norm_mix_pre, norm_mix_post, norm_mlp_pre, norm_mlp_post, w_in, rel_bias, sinks, lam_re, lam_im, log_dt, b_re, b_im, c_re, c_im, d_skip, w_glu, w_attn_branch, w_ssm_branch, w_out, w_ff_in, w_ff_out, loss_target, m_norm_mix_pre, m_norm_mix_post, m_norm_mlp_pre, m_norm_mlp_post, m_w_in, m_rel_bias, m_sinks, m_lam_re, m_lam_im, m_log_dt, m_b_re, m_b_im, m_c_re, m_c_im, m_d_skip, m_w_glu, m_w_attn_branch, m_w_ssm_branch, m_w_out, m_w_ff_in, m_w_ff_out, v_norm_mix_pre, v_norm_mix_post, v_norm_mlp_pre, v_norm_mlp_post, v_w_in, v_rel_bias, v_sinks, v_lam_re, v_lam_im, v_log_dt, v_b_re, v_b_im, v_c_re, v_c_im, v_d_skip, v_w_glu, v_w_attn_branch, v_w_ssm_branch, v_w_out, v_w_ff_in, v_w_ff_out):
    args = dict(locals())
    w = {n: args[n] for n in ALL_WEIGHTS}
    m = {n: args["m_" + n] for n in ALL_WEIGHTS}
    v = {n: args["v_" + n] for n in ALL_WEIGHTS}
    core = lax.axis_index("c").astype(jnp.int32).reshape(1)
    chip = (2 * lax.axis_index("x") + lax.axis_index("y")).astype(jnp.int32).reshape(1)
    xs, target = x[0], loss_target[0]
    t = _tiles(xs.shape[0])
    local = lambda d, n: d[n][0].T if n == "w_in" else d[n][0]
    shard = {n: local(w, n).astype(BF16) for n in BIG}
    shard["w_ff_in"] = shard["w_ff_in"].T
    view = lambda n, a: jnp.swapaxes(a, -1, -2) if n in SWAPPED_SMALL else a
    small = {n: (view(n, w[n]) if n == "rel_bias" else view(n, w[n])[0]) for n in SMALL}
    g1, g2, g3, g4 = (small[n].reshape(1, D_MODEL) for n in ("norm_mix_pre", "norm_mix_post", "norm_mlp_pre", "norm_mlp_post"))
    bucket = jnp.asarray(_bucket_table())
    rel_b, sink = small["rel_bias"], small["sinks"].reshape(1, N_HEADS)
    lam_r, lam_i = small["lam_re"].reshape(1, STATES), small["lam_im"].reshape(1, STATES)
    ldt_rep = jnp.repeat(small["log_dt"].reshape(N_GROUPS), N_STATE).reshape(1, STATES)
    bd_re, bd_im = _block_diag_in(small["b_re"]), _block_diag_in(small["b_im"])
    cm_re, cm_im = _block_diag_out(small["c_re"]).astype(BF16), _block_diag_out(small["c_im"]).astype(BF16)
    dsk = small["d_skip"].reshape(1, SSM_W)

    (g_in,) = _run_carry(_gather_carry([shard["w_in"]]), "gather_w_in")
    wf_in = g_in.reshape(IN_W, D_MODEL)
    merge_names = ("w_glu", "w_attn_branch", "w_ssm_branch", "w_out")
    (q, k, vv, u, ga, gs, h), gathered = _in_proj_fwd(xs, g1, wf_in, t["proj"], _gather_carry([shard[n] for n in merge_names]))
    wf = {n: _full_from_gathered(n, g) for n, g in zip(merge_names, gathered)}
    (att,), (wf_ff_in,) = _attn_fwd(q, k, vv, bucket, rel_b, sink, _gather_carry([shard["w_ff_in"]]))
    a_re, a_im, bm_re, bm_im = _ssm_prep(lam_r, lam_i, ldt_rep, bd_re, bd_im)
    (y, h_re, h_im, in_re, in_im), (wf_ff_out,) = _ssm_fwd(
        u, a_re, a_im, bm_re, bm_im, cm_re, cm_im, dsk, t["ssm_chunk"], _gather_carry([shard["w_ff_out"]]))
    x1, o, h2 = _merge_fwd(xs, y, att, ga, gs, g2, g3, wf["w_glu"], wf["w_ssm_branch"], wf["w_attn_branch"], wf["w_out"],
                           t["merge"])
    a, dfo, dx2, loss_blk, dg4 = _mlp_fwd(h2, x1, target, g4, wf_ff_in, wf_ff_out, t["mlp_fwd"])

    def add_sibling(names, blocks, received):
        pairs = [_add_sibling(b, r, core, "add_sibling_" + n) for n, b, r in zip(names, blocks, received)]
        return [p[0] for p in pairs], [p[1] for p in pairs]

    ff_names = ("w_ff_in", "w_ff_out")
    dw_ff_in, dw_ff_out, da = _mlp_weight_grads(dfo, a, h2, wf_ff_out, t["mlp_bwd"])
    dh2 = _mlp_input_grad(da, wf_ff_in.reshape(D_FF, D_MODEL), t["mlp_bwd"])
    ff_blocks = [dw_ff_in, dw_ff_out]
    (dx1, dgates, datt, dy, merged, dob, dya, dys, z2b, zb, dpre, dg2, dg3), ff_recv = _merge_bwd(
        dh2, dx2, x1, o, y, att, ga, gs, g2, g3, wf["w_glu"], wf["w_ssm_branch"], wf["w_attn_branch"], wf["w_out"],
        t["merge_bwd"], _sibling_carry(ff_blocks))
    dw_out, dw_attn = _weight_grad(merged, dob, "w_out_grad"), _weight_grad(att, dya, "w_attn_grad")
    dw_ssm, dw_glu = _weight_grad(z2b, dys, "w_ssm_grad"), _weight_grad(zb, dpre, "w_glu_grad")
    ff_sums, ff_sums_bf = add_sibling(ff_names, ff_blocks, ff_recv)
    merge_blocks = [_blocks_from_full(n, g) for n, g in zip(merge_names, (dw_glu, dw_attn, dw_ssm, dw_out))]
    (du, dbm_re, dbm_im, dcm_re, dcm_im, da_re, da_im, dd_skip), carried = _ssm_bwd(
        dy, u, h_re, h_im, in_re, in_im, a_re, a_im, bm_re, bm_im, cm_re, cm_im, dsk, t["ssm_chunk"],
        _join(_chips_carry(ff_sums_bf), _sibling_carry(merge_blocks)))
    ff_from_chips, merge_recv = carried[:2], carried[2:]
    merge_sums, merge_sums_bf = add_sibling(merge_names, merge_blocks, merge_recv)
    dbd_re, dbd_im, dlam_re, dlam_im, dldt_rep = _ssm_prep_bwd(lam_r, lam_i, ldt_rep, bd_re, bd_im, dbm_re, dbm_im, da_re, da_im)
    dlog_dt = _group_sum(dldt_rep.reshape(N_GROUPS, N_STATE))
    shapes = {n: view(n, w[n]).shape for n in SMALL}
    shapes["loss"] = (1,)
    small_grads = dict(
        norm_mix_post=dg2, norm_mlp_pre=dg3, norm_mlp_post=dg4, lam_re=dlam_re, lam_im=dlam_im, log_dt=dlog_dt,
        b_re=_block_diag_in_grad(dbd_re), b_im=_block_diag_in_grad(dbd_im),
        c_re=_block_diag_out_grad(dcm_re), c_im=_block_diag_out_grad(dcm_im), d_skip=dd_skip)
    packed_early = _pack({n: small_grads[n].reshape(shapes[n]) for n in SMALL_BEFORE_ATTN_BWD}, SMALL_BEFORE_ATTN_BWD)
    (dq, dkv, attn_small), carried = _attn_bwd(
        q, k, vv, datt, bucket, rel_b, sink, _join(_chips_carry(merge_sums_bf), _gather_carry([packed_early])))
    merge_from_chips, partials_early = carried[:-1], carried[-1]

    dparts = (dq, dkv, du, dgates)
    dw_in_t = _in_proj_weight_grad(h, dparts)
    in_blocks = [dw_in_t.reshape(N_DEV, IN_W // N_DEV, D_MODEL)]
    n_tiles = xs.shape[0] // t["proj_bwd"]
    first_part = max(1, (3 * n_tiles) // 8)
    (gx_a, dg1_a), in_recv = _in_proj_input_grad(
        xs, g1, wf_in, dx1, dparts, t["proj_bwd"], 0, first_part, "in_proj_input_grad_a", _sibling_carry(in_blocks))
    in_sums, in_sums_bf = add_sibling(("w_in",), in_blocks, in_recv)
    late = dict(rel_bias=attn_small[:, :N_BUCKETS, 0], sinks=attn_small[:, N_BUCKETS, 0], loss=loss_blk[0:1, 0])
    packed_late = _pack({n: late[n].reshape(shapes[n]) for n in SMALL_LATE}, SMALL_LATE)
    (gx_b, dg1_b), (in_from_chips, partials_late) = _in_proj_input_grad(
        xs, g1, wf_in, dx1, dparts, t["proj_bwd"], first_part, n_tiles - first_part, "in_proj_input_grad_b",
        _join(_chips_carry(in_sums_bf), _gather_carry([packed_late])))
    grad_x = jnp.concatenate([gx_a, gx_b], axis=0)
    (dg1_partials,) = _run_carry(_gather_carry([jnp.concatenate([dg1_a, dg1_b], axis=0)]), "gather_norm_grad")

    grads, deltas, new_m, new_v = {}, {}, {}, {}
    sums = dict(zip(ff_names + merge_names + ("w_in",), ff_sums + merge_sums + in_sums))
    received = dict(zip(ff_names + merge_names + ("w_in",), ff_from_chips + merge_from_chips + [in_from_chips]))
    for n in BIG:
        outs = _adam_big(local(w, n), local(m, n), local(v, n), sums[n], received[n], chip, "adam_" + n)
        grads[n], deltas[n], new_m[n], new_v[n] = ((o.T if n == "w_in" else o)[None] for o in outs)

    grads.update(_unpack(_sum_partials(partials_early, "sum_small_grads"), shapes, SMALL_BEFORE_ATTN_BWD))
    grads.update(_unpack(_sum_partials(partials_late, "sum_late_grads"), shapes, SMALL_LATE))
    grads["norm_mix_pre"] = _sum_partials(dg1_partials.reshape(2 * N_DEV, 1, D_MODEL), "sum_norm_grad")
    loss = grads.pop("loss").reshape(())
    small_out = _adam_small(*[[view(n, d[n]) for n in SMALL] for d in (w, m, v)], [grads[n] for n in SMALL])
    for store, vals in zip((deltas, new_m, new_v), small_out):
        store.update(zip(SMALL, vals))
    for store in (grads, deltas, new_m, new_v):
        store.update({n: view(n, store[n]) for n in SWAPPED_SMALL})

    return (loss, grad_x[None], *[grads[n] for n in ALL_WEIGHTS], *[deltas[n] for n in ALL_WEIGHTS],
            *[new_m[n] for n in ALL_WEIGHTS], *[new_v[n] for n in ALL_WEIGHTS])
```

```python
import functools
import math

import jax
import jax.numpy as jnp
import numpy as np
from jax import lax
from jax.experimental import pallas as pl
from jax.experimental.pallas import tpu as pltpu

F32 = jnp.float32
BF16 = jnp.bfloat16

D_MODEL = 1024
N_HEADS = 8
HEAD_DIM = 64
ATTN_W = 512
KV_W = 128
BLOCK = 128
N_BUCKETS = 32
SSM_W = 512
N_GROUPS = 32
N_STATE = 64
GROUP_CH = 16
STATES = N_GROUPS * N_STATE
D_FF = 4096
IN_W = 3328
SPLITS = (0, 512, 640, 768, 1280, 2304, 3328)
RMS_EPS = 1e-6
NEG_INF = -1e30
SUBLANES = 8
LANES = 128
SSM_LANE_BLOCK = 512
N_SSM_BLOCKS = STATES // SSM_LANE_BLOCK
VMEM_BIG = 52 * 1024 * 1024
VMEM_MID = 40 * 1024 * 1024
VMEM_MAX = 60 * 1024 * 1024

ADAM_LR = 0.001
ADAM_B1 = 0.9
ADAM_B2 = 0.999
ADAM_EPS = 1e-08
ADAM_WD = 0.01
ADAM_STEP = 10

N_DEV = 8


def _dot(a, b):
    return jnp.dot(a, b, preferred_element_type=F32)


def _dot_nt(a, b):
    return lax.dot_general(a, b, (((1,), (1,)), ((), ())), preferred_element_type=F32)


def _dot_tn(a, b):
    return lax.dot_general(a, b, (((0,), (0,)), ((), ())), preferred_element_type=F32)


def _rms_scale(x):
    return lax.rsqrt(jnp.mean(x * x, axis=-1, keepdims=True) + RMS_EPS)


def _rms_bwd(dy, x, r, g):
    t = dy * g
    dx = r * t - x * (r * r * r) * jnp.mean(t * x, axis=-1, keepdims=True)
    dg = jnp.sum(dy * x * r, axis=0, keepdims=True)
    return dx, dg


def _const_spec(shape):
    nd = len(shape)
    return pl.BlockSpec(shape, lambda *_: (0,) * nd, pipeline_mode=pl.Buffered(1))


def _in_hbm(*arrays):
    return tuple(pltpu.with_memory_space_constraint(a, pltpu.HBM) for a in arrays)


def _hbm_out(shapes):
    if isinstance(shapes, (list, tuple)):
        return [_hbm_out(s) for s in shapes]
    return shapes if isinstance(shapes, pl.MemoryRef) else pltpu.HBM(shapes.shape, shapes.dtype)


def _whole(shape):
    nd = len(shape)
    return pl.BlockSpec(shape, lambda *_: (0,) * nd)


def _params(sem, vmem=None):
    return pltpu.CompilerParams(dimension_semantics=sem, vmem_limit_bytes=vmem)


MESH_IDS = pl.DeviceIdType.MESH
HBM_SPEC = pl.BlockSpec(memory_space=pl.ANY)


class _Carry:
    def __init__(self, inputs, out_shapes, sems, start, finish):
        self.inputs, self.out_shapes, self.sems, self.start, self.finish = list(inputs), list(out_shapes), list(sems), start, finish


def _join(a, b):
    na_in, na_out, na_sem = len(a.inputs), len(a.out_shapes), len(a.sems)

    def start(ins, outs, sems):
        a.start(ins[:na_in], outs[:na_out], sems[:na_sem])
        b.start(ins[na_in:], outs[na_out:], sems[na_sem:])

    def finish(ins, outs, sems):
        a.finish(ins[:na_in], outs[:na_out], sems[:na_sem])
        b.finish(ins[na_in:], outs[na_out:], sems[na_sem:])

    return _Carry(a.inputs + b.inputs, a.out_shapes + b.out_shapes, a.sems + b.sems, start, finish)


def _hosted_call(body, carry, edge, *, name, grid, in_specs, out_specs, out_shape, scratch_shapes, compiler_params, inputs):
    n_in, n_out = len(in_specs), len(out_specs)
    inputs = [a if s.memory_space == pltpu.SMEM else _in_hbm(a)[0] for a, s in zip(inputs, in_specs)]
    out_shape = _hbm_out(list(out_shape))
    if carry is None:
        outs = pl.pallas_call(body, name=name, grid=grid, in_specs=in_specs, out_specs=out_specs, out_shape=out_shape,
                              scratch_shapes=scratch_shapes, compiler_params=compiler_params)(*inputs)
        return list(outs), []
    c_in, c_out, c_sem = len(carry.inputs), len(carry.out_shapes), len(carry.sems)

    def wrapped(*refs):
        ins, refs = refs[:n_in], refs[n_in:]
        cins, refs = refs[:c_in], refs[c_in:]
        outs, refs = refs[:n_out], refs[n_out:]
        couts, refs = refs[:c_out], refs[c_out:]
        scratch, csems = refs[:len(refs) - c_sem], refs[len(refs) - c_sem:]
        first, last = edge()

        @pl.when(first)
        def _():
            carry.start(cins, couts, csems)

        body(*ins, *outs, *scratch)

        @pl.when(last)
        def _():
            carry.finish(cins, couts, csems)

    outs = pl.pallas_call(
        wrapped, name=name, grid=grid, in_specs=list(in_specs) + [HBM_SPEC] * c_in,
        out_specs=list(out_specs) + [HBM_SPEC] * c_out, out_shape=out_shape + _hbm_out(carry.out_shapes),
        scratch_shapes=list(scratch_shapes) + carry.sems, compiler_params=compiler_params)(*inputs, *_in_hbm(*carry.inputs))
    return list(outs[:n_out]), list(outs[n_out:])


def _edge_1d(n_steps):
    return lambda: (pl.program_id(0) == 0, pl.program_id(0) == n_steps - 1)


def _edge_2d(n0, n1):
    return lambda: ((pl.program_id(0) == 0) & (pl.program_id(1) == 0),
                    (pl.program_id(0) == n0 - 1) & (pl.program_id(1) == n1 - 1))


def _run_carry(carry, name):
    c_in, c_out = len(carry.inputs), len(carry.out_shapes)

    def body(*refs):
        ins, outs, sems = refs[:c_in], refs[c_in:c_in + c_out], refs[c_in + c_out:]
        carry.start(ins, outs, sems)
        carry.finish(ins, outs, sems)

    return pl.pallas_call(body, name=name, in_specs=[HBM_SPEC] * c_in, out_specs=[HBM_SPEC] * c_out,
                          out_shape=_hbm_out(carry.out_shapes), scratch_shapes=carry.sems)(*_in_hbm(*carry.inputs))


def _in_proj_fwd(x, g1, w_in_t, tile, carry=None):
    T = x.shape[0]

    def body(x_ref, g_ref, w_ref, q_ref, k_ref, v_ref, u_ref, ga_ref, gs_ref, h_ref):
        xv = x_ref[...]
        h = (xv * _rms_scale(xv) * g_ref[...]).astype(BF16)
        h_ref[...] = h
        outs = (q_ref, k_ref, v_ref, u_ref, ga_ref, gs_ref)
        for p, o_ref in enumerate(outs):
            o_ref[...] = _dot_nt(h, w_ref[SPLITS[p]:SPLITS[p + 1], :]).astype(o_ref.dtype)

    widths = [SPLITS[p + 1] - SPLITS[p] for p in range(6)] + [D_MODEL]
    dtypes = [BF16, BF16, BF16, F32, F32, F32, BF16]
    return _hosted_call(
        body, carry, _edge_1d(T // tile), name="in_proj_fwd", grid=(T // tile,),
        in_specs=[pl.BlockSpec((tile, D_MODEL), lambda i: (i, 0)), _const_spec((1, D_MODEL)), _const_spec((IN_W, D_MODEL))],
        out_specs=[pl.BlockSpec((tile, w), lambda i: (i, 0)) for w in widths],
        out_shape=[jax.ShapeDtypeStruct((T, w), dt) for w, dt in zip(widths, dtypes)],
        scratch_shapes=[], compiler_params=_params(("arbitrary",), VMEM_MID), inputs=(x, g1, w_in_t))


PROJ_PARTS = (512, 256, 512, 2048)
PROJ_GRAD_BLOCK = 256


def _in_proj_weight_grad(h, dparts):
    T = h.shape[0]
    blocks = [wd // PROJ_GRAD_BLOCK for wd in PROJ_PARTS]
    starts = [sum(blocks[:p]) for p in range(len(blocks))]

    def body(h_ref, *refs):
        part_refs, o_ref = refs[:-1], refs[-1]
        j = pl.program_id(0)
        for p_ref, start, count in zip(part_refs, starts, blocks):
            @pl.when((j >= start) & (j < start + count))
            def _(p_ref=p_ref):
                o_ref[...] = _dot_tn(p_ref[...], h_ref[...])

    def part_spec(start, count):
        return pl.BlockSpec((T, PROJ_GRAD_BLOCK), lambda j: (0, jnp.clip(j - start, 0, count - 1)))

    return pl.pallas_call(
        body, name="in_proj_weight_grad", grid=(sum(blocks),),
        in_specs=[_const_spec((T, D_MODEL))] + [part_spec(s, c) for s, c in zip(starts, blocks)],
        out_specs=pl.BlockSpec((PROJ_GRAD_BLOCK, D_MODEL), lambda j: (j, 0)),
        out_shape=_hbm_out(jax.ShapeDtypeStruct((IN_W, D_MODEL), F32)),
        compiler_params=_params(("arbitrary",), VMEM_MID),
    )(*_in_hbm(h, *dparts))


def _in_proj_input_grad(x, g1, w_in_t, dx1, dparts, tile, first_tile, n_tiles, name, carry=None):
    offsets = [sum(PROJ_PARTS[:p]) for p in range(len(PROJ_PARTS))]

    def body(x_ref, g_ref, w_ref, dx1_ref, *refs):
        part_refs, (gx_ref, dg_ref) = refs[:len(PROJ_PARTS)], refs[len(PROJ_PARTS):]
        i = pl.program_id(0)
        xv = x_ref[...]
        r = _rms_scale(xv)
        g = g_ref[...]
        dh = sum(_dot(p_ref[...], w_ref[off:off + wd, :]) for p_ref, off, wd in zip(part_refs, offsets, PROJ_PARTS))
        dxn, dg = _rms_bwd(dh, xv, r, g)
        gx_ref[...] = dx1_ref[...] + dxn

        @pl.when(i == 0)
        def _():
            dg_ref[...] = dg

        @pl.when(i > 0)
        def _():
            dg_ref[...] += dg

    tok = lambda wd: pl.BlockSpec((tile, wd), lambda i: (i + first_tile, 0))
    return _hosted_call(
        body, carry, _edge_1d(n_tiles), name=name, grid=(n_tiles,),
        in_specs=[tok(D_MODEL), _const_spec((1, D_MODEL)), _const_spec((IN_W, D_MODEL)), tok(D_MODEL)] + [tok(wd) for wd in PROJ_PARTS],
        out_specs=[pl.BlockSpec((tile, D_MODEL), lambda i: (i, 0)), pl.BlockSpec((1, D_MODEL), lambda i: (0, 0))],
        out_shape=[jax.ShapeDtypeStruct((n_tiles * tile, D_MODEL), F32), jax.ShapeDtypeStruct((1, D_MODEL), F32)],
        scratch_shapes=[], compiler_params=_params(("arbitrary",), VMEM_MID), inputs=(x, g1, w_in_t, dx1, *dparts))


def _bucket_table():
    qi = np.arange(BLOCK)[:, None]
    kj = np.arange(2 * BLOCK)[None, :]
    dist = qi + BLOCK - kj
    max_exact = N_BUCKETS // 2
    d = np.maximum(dist, 0)
    df = np.maximum(d, 1).astype(np.float32)
    large = max_exact + (np.log(df / np.float32(max_exact)) / np.float32(math.log(BLOCK / max_exact))
                         * np.float32(N_BUCKETS - max_exact)).astype(np.int32)
    large = np.minimum(large, N_BUCKETS - 1)
    bucket = np.where(d < max_exact, d, large)
    return np.where((dist >= 0) & (dist < BLOCK), bucket, -1).astype(np.int32)


def _build_bias(bucket_ref, rb_ref, bias_ref):
    bk = bucket_ref[...]
    for h in range(N_HEADS):
        def add(b, acc, h=h):
            return acc + jnp.where(bk == b, rb_ref[h, b], 0.0)
        bias_ref[h] = lax.fori_loop(0, N_BUCKETS, add, jnp.zeros((BLOCK, 2 * BLOCK), F32))


def _kv_variants(prev_ref, cur_ref):
    cat = jnp.concatenate([prev_ref[...], cur_ref[...]], axis=0)
    lo = lax.broadcasted_iota(jnp.int32, cat.shape, 1) < HEAD_DIM
    zero = jnp.zeros_like(cat)
    head0_lo = jnp.where(lo, cat, zero)
    head1_hi = jnp.where(lo, zero, cat)
    return ((head0_lo, pltpu.roll(head0_lo, HEAD_DIM, 1)), (pltpu.roll(head1_hi, HEAD_DIM, 1), head1_hi))


def _merge_kv_grads(g):
    lo = lax.broadcasted_iota(jnp.int32, g[0][0].shape, 1) < HEAD_DIM
    return jnp.where(lo, g[0][0] + pltpu.roll(g[0][1], HEAD_DIM, 1), g[1][1] + pltpu.roll(g[1][0], HEAD_DIM, 1))


def _head_lanes(h):
    return slice((h // 2) * LANES, (h // 2 + 1) * LANES)


def _attn_probs(q_ref, kvar, bias_ref, sk_ref, valid, s_ref):
    for h in range(N_HEADS):
        s_ref[h] = _dot_nt(q_ref[:, _head_lanes(h)], kvar[h // 4][h % 2])
    head = lax.broadcasted_iota(jnp.int32, (N_HEADS, 1, 1), 0)
    sink = jnp.zeros((N_HEADS, 1, 1), F32)
    for h in range(N_HEADS):
        sink = jnp.where(head == h, sk_ref[0, h], sink)
    s = jnp.where(valid[None], s_ref[...] * (HEAD_DIM ** -0.5) + bias_ref[...], NEG_INF)
    m = jnp.maximum(jnp.max(s, axis=-1, keepdims=True), sink)
    p = jnp.exp(s - m)
    e_sink = jnp.exp(sink - m)
    inv = 1.0 / (jnp.sum(p, axis=-1, keepdims=True) + e_sink)
    return p * inv, e_sink * inv


def _attn_valid(bucket_ref, n):
    col = lax.broadcasted_iota(jnp.int32, (BLOCK, 2 * BLOCK), 1)
    return (bucket_ref[...] >= 0) & ((n > 0) | (col >= BLOCK))


def _attn_fwd(q, k, v, bucket, rel_bias, sinks, carry=None):
    T = q.shape[0]
    nb = T // BLOCK

    def body(q_ref, kc_ref, kp_ref, vc_ref, vp_ref, bucket_ref, rb_ref, sk_ref, o_ref, bias_ref, s_ref, p_ref):
        n = pl.program_id(0)

        @pl.when(n == 0)
        def _():
            _build_bias(bucket_ref, rb_ref, bias_ref)

        kvar = _kv_variants(kp_ref, kc_ref)
        vvar = _kv_variants(vp_ref, vc_ref)
        pr, _ = _attn_probs(q_ref, kvar, bias_ref, sk_ref, _attn_valid(bucket_ref, n), s_ref)
        p_ref[...] = pr.astype(BF16)
        for m in range(N_HEADS // 2):
            acc = _dot(p_ref[2 * m], vvar[m // 2][0]) + _dot(p_ref[2 * m + 1], vvar[m // 2][1])
            o_ref[:, m * LANES:(m + 1) * LANES] = acc.astype(o_ref.dtype)

    cur = lambda w: pl.BlockSpec((BLOCK, w), lambda n: (n, 0))
    prev = lambda w: pl.BlockSpec((BLOCK, w), lambda n: (jnp.maximum(n - 1, 0), 0))
    smem = pl.BlockSpec(memory_space=pltpu.SMEM)
    return _hosted_call(
        body, carry, _edge_1d(nb), name="attn_fwd", grid=(nb,),
        in_specs=[cur(ATTN_W), cur(KV_W), prev(KV_W), cur(KV_W), prev(KV_W), _const_spec((BLOCK, 2 * BLOCK)), smem, smem],
        out_specs=[cur(ATTN_W)],
        out_shape=[jax.ShapeDtypeStruct((T, ATTN_W), BF16)],
        scratch_shapes=[pltpu.VMEM((N_HEADS, BLOCK, 2 * BLOCK), F32), pltpu.VMEM((N_HEADS, BLOCK, 2 * BLOCK), F32),
                        pltpu.VMEM((N_HEADS, BLOCK, 2 * BLOCK), BF16)],
        compiler_params=_params(("arbitrary",)), inputs=(q, k, k, v, v, bucket, rel_bias, sinks))


ATTN_SMALL_ROWS = N_BUCKETS + SUBLANES


def _attn_bwd(q, k, v, datt, bucket, rel_bias, sinks, carry=None):
    T = q.shape[0]
    nb = T // BLOCK

    def body(q_ref, do_ref, kc_ref, kp_ref, vc_ref, vp_ref, bucket_ref, rb_ref, sk_ref,
             dq_ref, dkv_ref, small_ref, bias_ref, ds_sum_ref, dsink_ref, kcarry_ref, vcarry_ref,
             s_ref, dp_ref, p_ref, dsc_ref):
        n = pl.program_id(0)

        @pl.when(n == 0)
        def _():
            _build_bias(bucket_ref, rb_ref, bias_ref)
            ds_sum_ref[...] = jnp.zeros_like(ds_sum_ref)
            dsink_ref[...] = jnp.zeros_like(dsink_ref)
            kcarry_ref[...] = jnp.zeros_like(kcarry_ref)
            vcarry_ref[...] = jnp.zeros_like(vcarry_ref)

        @pl.when(n < nb)
        def _():
            kvar = _kv_variants(kp_ref, kc_ref)
            vvar = _kv_variants(vp_ref, vc_ref)
            pr, p_sink = _attn_probs(q_ref, kvar, bias_ref, sk_ref, _attn_valid(bucket_ref, n), s_ref)
            for h in range(N_HEADS):
                dp_ref[h] = _dot_nt(do_ref[:, _head_lanes(h)], vvar[h // 4][h % 2])
            dp = dp_ref[...]
            dsum = jnp.sum(pr * dp, axis=-1, keepdims=True)
            ds = pr * (dp - dsum)
            ds_sum_ref[...] += ds
            dsink_ref[...] -= jnp.sum(p_sink * dsum, axis=1, keepdims=True)
            dsc_ref[...] = (ds * (HEAD_DIM ** -0.5)).astype(BF16)
            p_ref[...] = pr.astype(BF16)
            for m in range(N_HEADS // 2):
                dqm = _dot(dsc_ref[2 * m], kvar[m // 2][0]) + _dot(dsc_ref[2 * m + 1], kvar[m // 2][1])
                dq_ref[:, m * LANES:(m + 1) * LANES] = dqm.astype(dq_ref.dtype)
            dk_var = [[None, None], [None, None]]
            dv_var = [[None, None], [None, None]]
            for kvh in range(2):
                for e in range(2):
                    heads = [h for h in range(N_HEADS) if h // 4 == kvh and h % 2 == e]
                    dk_var[kvh][e] = sum(_dot_tn(dsc_ref[h], q_ref[:, _head_lanes(h)]) for h in heads)
                    dv_var[kvh][e] = sum(_dot_tn(p_ref[h], do_ref[:, _head_lanes(h)]) for h in heads)
            dk_cat = _merge_kv_grads(dk_var)
            dv_cat = _merge_kv_grads(dv_var)

            @pl.when(n > 0)
            def _():
                dkv_ref[:, :KV_W] = (kcarry_ref[...] + dk_cat[:BLOCK]).astype(BF16)
                dkv_ref[:, KV_W:] = (vcarry_ref[...] + dv_cat[:BLOCK]).astype(BF16)

            kcarry_ref[...] = dk_cat[BLOCK:]
            vcarry_ref[...] = dv_cat[BLOCK:]

        @pl.when(n == nb)
        def _():
            dkv_ref[:, :KV_W] = kcarry_ref[...].astype(BF16)
            dkv_ref[:, KV_W:] = vcarry_ref[...].astype(BF16)
            bk = bucket_ref[...]
            row = lax.broadcasted_iota(jnp.int32, (N_HEADS, ATTN_SMALL_ROWS, LANES), 1)

            def add(b, acc):
                masked = jnp.where((bk == b)[None], ds_sum_ref[...], 0.0)
                val = jnp.sum(jnp.sum(masked, axis=1, keepdims=True), axis=2, keepdims=True)
                return acc + jnp.where(row == b, val, 0.0)

            small_ref[...] = lax.fori_loop(0, N_BUCKETS, add, jnp.where(row == N_BUCKETS, dsink_ref[...], 0.0))

    last = nb - 1
    cur = lambda w: pl.BlockSpec((BLOCK, w), lambda n: (jnp.minimum(n, last), 0))
    prev = lambda w: pl.BlockSpec((BLOCK, w), lambda n: (jnp.clip(n - 1, 0, last), 0))
    smem = pl.BlockSpec(memory_space=pltpu.SMEM)
    return _hosted_call(
        body, carry, _edge_1d(nb + 1), name="attn_bwd", grid=(nb + 1,),
        in_specs=[cur(ATTN_W), cur(ATTN_W), cur(KV_W), prev(KV_W), cur(KV_W), prev(KV_W),
                  _const_spec((BLOCK, 2 * BLOCK)), smem, smem],
        out_specs=[cur(ATTN_W), prev(2 * KV_W), pl.BlockSpec((N_HEADS, ATTN_SMALL_ROWS, LANES), lambda n: (0, 0, 0))],
        out_shape=[jax.ShapeDtypeStruct((T, ATTN_W), BF16), jax.ShapeDtypeStruct((T, 2 * KV_W), BF16),
                   jax.ShapeDtypeStruct((N_HEADS, ATTN_SMALL_ROWS, LANES), F32)],
        scratch_shapes=[pltpu.VMEM((N_HEADS, BLOCK, 2 * BLOCK), F32), pltpu.VMEM((N_HEADS, BLOCK, 2 * BLOCK), F32),
                        pltpu.VMEM((N_HEADS, 1, 1), F32), pltpu.VMEM((BLOCK, KV_W), F32), pltpu.VMEM((BLOCK, KV_W), F32),
                        pltpu.VMEM((N_HEADS, BLOCK, 2 * BLOCK), F32), pltpu.VMEM((N_HEADS, BLOCK, 2 * BLOCK), F32),
                        pltpu.VMEM((N_HEADS, BLOCK, 2 * BLOCK), BF16), pltpu.VMEM((N_HEADS, BLOCK, 2 * BLOCK), BF16)],
        compiler_params=_params(("arbitrary",)), inputs=(q, datt, k, k, v, v, bucket, rel_bias, sinks))


SCAN_UNROLL = 4


def _cmul(ar, ai, br, bi):
    return ar * br - ai * bi, ar * bi + ai * br


def _cmul_conj(ar, ai, br, bi):
    return ar * br + ai * bi, ar * bi - ai * br


def _ssm_discretize(lr, li, ldt):
    dt = jnp.exp(ldt)
    mag = jnp.exp(lr * dt)
    ab_re = mag * jnp.cos(li * dt)
    ab_im = mag * jnp.sin(li * dt)
    nr = ab_re - 1.0
    den = lr * lr + li * li
    f_re = (nr * lr + ab_im * li) / den
    f_im = (ab_im * lr - nr * li) / den
    return ab_re, ab_im, f_re, f_im


def _ssm_prep(lam_re, lam_im, ldt_rep, bd_re, bd_im):
    def body(lr_ref, li_ref, ldt_ref, bdr_ref, bdi_ref, ar_ref, ai_ref, br_ref, bi_ref):
        ab_re, ab_im, f_re, f_im = _ssm_discretize(lr_ref[...], li_ref[...], ldt_ref[...])
        ar_ref[...] = ab_re
        ai_ref[...] = ab_im
        bdr, bdi = bdr_ref[0], bdi_ref[0]
        br_ref[0] = (bdr * f_re - bdi * f_im).astype(BF16)
        bi_ref[0] = (bdi * f_re + bdr * f_im).astype(BF16)

    row = pl.BlockSpec((1, SSM_LANE_BLOCK), lambda j: (0, j))
    mat = pl.BlockSpec((1, LANES, SSM_LANE_BLOCK), lambda j: (j, 0, 0))
    return pl.pallas_call(
        body, name="ssm_prep", grid=(N_SSM_BLOCKS,),
        in_specs=[row, row, row, mat, mat], out_specs=[row, row, mat, mat],
        out_shape=[jax.ShapeDtypeStruct((1, STATES), F32)] * 2 + [jax.ShapeDtypeStruct((N_SSM_BLOCKS, LANES, SSM_LANE_BLOCK), BF16)] * 2,
        compiler_params=_params(("arbitrary",)),
    )(*_in_hbm(lam_re, lam_im, ldt_rep, bd_re, bd_im))


def _ssm_prep_bwd(lam_re, lam_im, ldt_rep, bd_re, bd_im, dbr, dbi, da_re, da_im):
    def body(lr_ref, li_ref, ldt_ref, bdr_ref, bdi_ref, dbr_ref, dbi_ref, dar_ref, dai_ref,
             dbdr_ref, dbdi_ref, dlr_ref, dli_ref, dldt_ref):
        lr, li, ldt = lr_ref[...], li_ref[...], ldt_ref[...]
        (_, _, f_re, f_im), vjp = jax.vjp(_ssm_discretize, lr, li, ldt)
        bdr, bdi, gbr, gbi = bdr_ref[0], bdi_ref[0], dbr_ref[0], dbi_ref[0]
        dbdr_ref[0] = gbr * f_re + gbi * f_im
        dbdi_ref[0] = gbi * f_re - gbr * f_im
        df_re = jnp.sum(gbr * bdr + gbi * bdi, axis=0, keepdims=True)
        df_im = jnp.sum(gbi * bdr - gbr * bdi, axis=0, keepdims=True)
        dlr, dli, dldt = vjp((dar_ref[...], dai_ref[...], df_re, df_im))
        dlr_ref[...] = dlr
        dli_ref[...] = dli
        dldt_ref[...] = dldt

    row = pl.BlockSpec((1, SSM_LANE_BLOCK), lambda j: (0, j))
    mat = pl.BlockSpec((1, LANES, SSM_LANE_BLOCK), lambda j: (j, 0, 0))
    mat_shape = jax.ShapeDtypeStruct((N_SSM_BLOCKS, LANES, SSM_LANE_BLOCK), F32)
    row_shape = jax.ShapeDtypeStruct((1, STATES), F32)
    return pl.pallas_call(
        body, name="ssm_prep_bwd", grid=(N_SSM_BLOCKS,),
        in_specs=[row, row, row, mat, mat, mat, mat, row, row], out_specs=[mat, mat, row, row, row],
        out_shape=[mat_shape, mat_shape, row_shape, row_shape, row_shape],
        compiler_params=_params(("arbitrary",)),
    )(*_in_hbm(lam_re, lam_im, ldt_rep, bd_re, bd_im, dbr, dbi, da_re, da_im))


def _group_sum(x):
    def body(x_ref, o_ref):
        o_ref[...] = jnp.sum(x_ref[...], axis=1, keepdims=True)
    return pl.pallas_call(body, name="ssm_group_sum", grid=(1,), in_specs=[_whole(x.shape)], out_specs=_whole((N_GROUPS, 1)),
                          out_shape=jax.ShapeDtypeStruct((N_GROUPS, 1), F32))(*_in_hbm(x))


def _power_table(ar, ai, p_re_ref, p_im_ref, steps):
    shape = (SUBLANES, SSM_LANE_BLOCK)
    p_re_ref[0:SUBLANES] = jnp.broadcast_to(ar, shape)
    p_im_ref[0:SUBLANES] = jnp.broadcast_to(ai, shape)
    m = 1
    while m < steps:
        rows = m * SUBLANES
        top_re = p_re_ref[rows - SUBLANES:rows]
        top_im = p_im_ref[rows - SUBLANES:rows]
        cur_re = p_re_ref[0:rows].reshape(m, SUBLANES, SSM_LANE_BLOCK)
        cur_im = p_im_ref[0:rows].reshape(m, SUBLANES, SSM_LANE_BLOCK)
        nxt_re, nxt_im = _cmul(cur_re, cur_im, top_re[None], top_im[None])
        p_re_ref[rows:2 * rows] = nxt_re.reshape(rows, SSM_LANE_BLOCK)
        p_im_ref[rows:2 * rows] = nxt_im.reshape(rows, SSM_LANE_BLOCK)
        m *= 2


def _to_segments(src_ref, dst_ref, steps):
    for s in range(SUBLANES):
        dst_ref[pl.ds(s, steps, stride=SUBLANES), :] = src_ref[s * steps:(s + 1) * steps, :]


def _from_segments(src_ref, dst_ref, steps):
    for s in range(SUBLANES):
        dst_ref[s * steps:(s + 1) * steps, :] = src_ref[pl.ds(s, steps, stride=SUBLANES), :]


def _segment_carries(e_re, e_im, an_re, an_im, c_re, c_im, reverse):
    order = range(SUBLANES - 1, -1, -1) if reverse else range(SUBLANES)
    ins_re, ins_im = [None] * SUBLANES, [None] * SUBLANES
    for s in order:
        ins_re[s], ins_im[s] = c_re, c_im
        pr, pi = _cmul(an_re, an_im, c_re, c_im)
        c_re = e_re[s:s + 1] + pr
        c_im = e_im[s:s + 1] + pi
    return jnp.concatenate(ins_re, axis=0), jnp.concatenate(ins_im, axis=0), c_re, c_im


def _ssm_fwd(u, a_re, a_im, b_re, b_im, c_re, c_im, d_skip, chunk, carry=None):
    T = u.shape[0]
    nc = T // chunk
    steps = chunk // SUBLANES
    blk = SSM_LANE_BLOCK

    def body(u_ref, ar_ref, ai_ref, br_ref, bi_ref, cr_ref, ci_ref, dk_ref,
             y_ref, hr_ref, hi_ref, inr_ref, ini_ref, useg_ref, yseg_ref, pr_ref, pi_ref, carry_ref):
        c = pl.program_id(1)
        ar, ai = ar_ref[...], ai_ref[...]

        @pl.when(c == 0)
        def _():
            _power_table(ar, ai, pr_ref, pi_ref, steps)
            carry_ref[...] = jnp.zeros_like(carry_ref)

        _to_segments(u_ref, useg_ref, steps)
        ub = useg_ref[...].astype(BF16)
        hr_ref[...] = _dot(ub, br_ref[0])
        hi_ref[...] = _dot(ub, bi_ref[0])
        first = slice(0, SUBLANES)

        def scan(t4, prev):
            for j in range(SCAN_UNROLL):
                rows = pl.ds(pl.multiple_of((t4 * SCAN_UNROLL + j) * SUBLANES, SUBLANES), SUBLANES)
                pr, pi = _cmul(pr_ref[first, :], pi_ref[first, :], prev[0], prev[1])
                prev = (pr + hr_ref[rows, :], pi + hi_ref[rows, :])
                hr_ref[rows, :] = prev[0]
                hi_ref[rows, :] = prev[1]
            return prev

        zero = jnp.zeros((SUBLANES, blk), F32)
        lax.fori_loop(0, steps // SCAN_UNROLL, scan, (zero, zero))

        top = slice(chunk - SUBLANES, chunk)
        in_re, in_im, out_re, out_im = _segment_carries(
            hr_ref[top, :], hi_ref[top, :], pr_ref[top, :][0:1], pi_ref[top, :][0:1],
            carry_ref[0:1, :], carry_ref[1:2, :], reverse=False)
        carry_ref[0:1, :] = out_re
        carry_ref[1:2, :] = out_im
        inr_ref[...] = in_re
        ini_ref[...] = in_im

        def fix(t4, _):
            for j in range(SCAN_UNROLL):
                rows = pl.ds(pl.multiple_of((t4 * SCAN_UNROLL + j) * SUBLANES, SUBLANES), SUBLANES)
                fr, fi = _cmul(pr_ref[rows, :], pi_ref[rows, :], in_re, in_im)
                hr_ref[rows, :] += fr
                hi_ref[rows, :] += fi
            return 0

        lax.fori_loop(0, steps // SCAN_UNROLL, fix, 0)

        yseg_ref[...] = _dot(hr_ref[...].astype(BF16), cr_ref[0]) - _dot(hi_ref[...].astype(BF16), ci_ref[0])
        _from_segments(yseg_ref, y_ref, steps)
        y_ref[...] += dk_ref[...] * u_ref[...]

    row = pl.BlockSpec((1, blk), lambda j, c: (0, j))
    b_mat = pl.BlockSpec((1, LANES, blk), lambda j, c: (j, 0, 0))
    c_mat = pl.BlockSpec((1, blk, LANES), lambda j, c: (j, 0, 0))
    tok = pl.BlockSpec((chunk, LANES), lambda j, c: (c, j))
    state = pl.BlockSpec((chunk, blk), lambda j, c: (c, j))
    enter = pl.BlockSpec((SUBLANES, blk), lambda j, c: (c, j))
    return _hosted_call(
        body, carry, _edge_2d(N_SSM_BLOCKS, nc), name="ssm_fwd", grid=(N_SSM_BLOCKS, nc),
        in_specs=[tok, row, row, b_mat, b_mat, c_mat, c_mat, pl.BlockSpec((1, LANES), lambda j, c: (0, j))],
        out_specs=[tok, state, state, enter, enter],
        out_shape=[jax.ShapeDtypeStruct((T, SSM_W), F32), jax.ShapeDtypeStruct((T, STATES), F32),
                   jax.ShapeDtypeStruct((T, STATES), F32), jax.ShapeDtypeStruct((nc * SUBLANES, STATES), F32),
                   jax.ShapeDtypeStruct((nc * SUBLANES, STATES), F32)],
        scratch_shapes=[pltpu.VMEM((chunk, LANES), F32), pltpu.VMEM((chunk, LANES), F32),
                        pltpu.VMEM((chunk, blk), F32), pltpu.VMEM((chunk, blk), F32), pltpu.VMEM((SUBLANES, blk), F32)],
        compiler_params=_params(("arbitrary", "arbitrary"), VMEM_MID),
        inputs=(u, a_re, a_im, b_re, b_im, c_re, c_im, d_skip))


def _ssm_bwd(dy, u, h_re, h_im, in_re, in_im, a_re, a_im, b_re, b_im, c_re, c_im, d_skip, chunk, carry=None):
    T = u.shape[0]
    nc = T // chunk
    steps = chunk // SUBLANES
    blk = SSM_LANE_BLOCK

    def body(dy_ref, u_ref, hr_ref, hi_ref, inr_ref, ini_ref, ar_ref, ai_ref, br_ref, bi_ref, cr_ref, ci_ref, dk_ref,
             du_ref, dbr_ref, dbi_ref, dcr_ref, dci_ref, dar_ref, dai_ref, ddk_ref,
             dyseg_ref, useg_ref, duseg_ref, gr_ref, gi_ref, pr_ref, pi_ref, carry_ref, accr_ref, acci_ref):
        c = pl.program_id(1)
        ar, ai = ar_ref[...], ai_ref[...]

        @pl.when(c == 0)
        def _():
            _power_table(ar, ai, pr_ref, pi_ref, steps)
            carry_ref[...] = jnp.zeros_like(carry_ref)
            accr_ref[...] = jnp.zeros_like(accr_ref)
            acci_ref[...] = jnp.zeros_like(acci_ref)

        _to_segments(dy_ref, dyseg_ref, steps)
        _to_segments(u_ref, useg_ref, steps)
        dyb = dyseg_ref[...].astype(BF16)
        ub = useg_ref[...].astype(BF16)
        gr_ref[...] = _dot_nt(dyb, cr_ref[0])
        gi_ref[...] = -_dot_nt(dyb, ci_ref[0])
        dcr = _dot_tn(hr_ref[...].astype(BF16), dyb)
        dci = -_dot_tn(hi_ref[...].astype(BF16), dyb)
        ddk = jnp.sum(dy_ref[...] * u_ref[...], axis=0, keepdims=True)

        first = slice(0, SUBLANES)

        def scan(k4, nxt):
            for j in range(SCAN_UNROLL):
                t = steps - 1 - (k4 * SCAN_UNROLL + j)
                rows = pl.ds(pl.multiple_of(t * SUBLANES, SUBLANES), SUBLANES)
                pr, pi = _cmul_conj(pr_ref[first, :], pi_ref[first, :], nxt[0], nxt[1])
                nxt = (pr + gr_ref[rows, :], pi + gi_ref[rows, :])
                gr_ref[rows, :] = nxt[0]
                gi_ref[rows, :] = nxt[1]
            return nxt

        top = slice(chunk - SUBLANES, chunk)
        zero = jnp.zeros((SUBLANES, blk), F32)
        lax.fori_loop(0, steps // SCAN_UNROLL, scan, (zero, zero))

        gin_re, gin_im, out_re, out_im = _segment_carries(
            gr_ref[0:SUBLANES, :], gi_ref[0:SUBLANES, :], pr_ref[top, :][0:1], -pi_ref[top, :][0:1],
            carry_ref[0:1, :], carry_ref[1:2, :], reverse=True)
        carry_ref[0:1, :] = out_re
        carry_ref[1:2, :] = out_im

        def fix_row(rows, prow, hp_re, hp_im, acc):
            fr, fi = _cmul_conj(pr_ref[prow, :], pi_ref[prow, :], gin_re, gin_im)
            g_re = gr_ref[rows, :] + fr
            g_im = gi_ref[rows, :] + fi
            gr_ref[rows, :] = g_re
            gi_ref[rows, :] = g_im
            return acc[0] + g_re * hp_re + g_im * hp_im, acc[1] + g_im * hp_re - g_re * hp_im

        def fix_at(t, acc):
            aligned = (lambda r: r * SUBLANES) if isinstance(t, int) else (lambda r: pl.multiple_of(r * SUBLANES, SUBLANES))
            rows, before, prow = (pl.ds(aligned(r), SUBLANES) for r in (t, t - 1, steps - 1 - t))
            return fix_row(rows, prow, hr_ref[before, :], hi_ref[before, :], acc)

        def fix(t4, acc):
            for j in range(SCAN_UNROLL):
                acc = fix_at(t4 * SCAN_UNROLL + j, acc)
            return acc

        acc = fix_row(first, top, inr_ref[...], ini_ref[...], (accr_ref[...], acci_ref[...]))
        for t in range(1, SCAN_UNROLL):
            acc = fix_at(t, acc)
        acc_re, acc_im = lax.fori_loop(1, steps // SCAN_UNROLL, fix, acc)
        accr_ref[...] = acc_re
        acci_ref[...] = acc_im

        gbr = gr_ref[...].astype(BF16)
        gbi = gi_ref[...].astype(BF16)
        duseg_ref[...] = _dot_nt(gbr, br_ref[0]) + _dot_nt(gbi, bi_ref[0])
        _from_segments(duseg_ref, dyseg_ref, steps)
        du_ref[...] = (dyseg_ref[...] + dk_ref[...] * dy_ref[...]).astype(BF16)
        dbr = _dot_tn(ub, gbr)
        dbi = _dot_tn(ub, gbi)

        @pl.when(c == 0)
        def _():
            dbr_ref[0] = dbr
            dbi_ref[0] = dbi
            dcr_ref[0] = dcr
            dci_ref[0] = dci
            ddk_ref[...] = ddk

        @pl.when(c > 0)
        def _():
            dbr_ref[0] += dbr
            dbi_ref[0] += dbi
            dcr_ref[0] += dcr
            dci_ref[0] += dci
            ddk_ref[...] += ddk

        @pl.when(c == nc - 1)
        def _():
            dar_ref[...] = jnp.sum(acc_re, axis=0, keepdims=True)
            dai_ref[...] = jnp.sum(acc_im, axis=0, keepdims=True)

    rev = lambda c: nc - 1 - c
    row = pl.BlockSpec((1, blk), lambda j, c: (0, j))
    b_mat = pl.BlockSpec((1, LANES, blk), lambda j, c: (j, 0, 0))
    c_mat = pl.BlockSpec((1, blk, LANES), lambda j, c: (j, 0, 0))
    tok = pl.BlockSpec((chunk, LANES), lambda j, c: (rev(c), j))
    state = pl.BlockSpec((chunk, blk), lambda j, c: (rev(c), j))
    enter = pl.BlockSpec((SUBLANES, blk), lambda j, c: (rev(c), j))
    chan = pl.BlockSpec((1, LANES), lambda j, c: (0, j))
    f32 = lambda *s: jax.ShapeDtypeStruct(s, F32)
    return _hosted_call(
        body, carry, _edge_2d(N_SSM_BLOCKS, nc), name="ssm_bwd", grid=(N_SSM_BLOCKS, nc),
        in_specs=[tok, tok, state, state, enter, enter, row, row, b_mat, b_mat, c_mat, c_mat, chan],
        out_specs=[tok, b_mat, b_mat, c_mat, c_mat, row, row, chan],
        out_shape=[jax.ShapeDtypeStruct((T, SSM_W), BF16), f32(N_SSM_BLOCKS, LANES, blk), f32(N_SSM_BLOCKS, LANES, blk),
                   f32(N_SSM_BLOCKS, blk, LANES), f32(N_SSM_BLOCKS, blk, LANES), f32(1, STATES), f32(1, STATES), f32(1, SSM_W)],
        scratch_shapes=[pltpu.VMEM((chunk, LANES), F32), pltpu.VMEM((chunk, LANES), F32), pltpu.VMEM((chunk, LANES), F32),
                        pltpu.VMEM((chunk, blk), F32), pltpu.VMEM((chunk, blk), F32),
                        pltpu.VMEM((chunk, blk), F32), pltpu.VMEM((chunk, blk), F32),
                        pltpu.VMEM((SUBLANES, blk), F32), pltpu.VMEM((SUBLANES, blk), F32), pltpu.VMEM((SUBLANES, blk), F32)],
        compiler_params=_params(("arbitrary", "arbitrary"), VMEM_BIG),
        inputs=(dy, u, h_re, h_im, in_re, in_im, a_re, a_im, b_re, b_im, c_re, c_im, d_skip))


def _merge_forward(y, att, ga, gs, w_glu, w_ssm, w_attn):
    z = jax.nn.gelu(y)
    zb = z.astype(BF16)
    gl = jax.nn.sigmoid(_dot(zb, w_glu))
    z2b = (z * gl).astype(BF16)
    y_ssm = _dot(z2b, w_ssm)
    y_attn = _dot(att, w_attn)
    sa = jax.nn.sigmoid(ga)
    ss = jax.nn.sigmoid(gs)
    merged = (sa * y_attn + ss * y_ssm).astype(BF16)
    return z, zb, gl, z2b, y_ssm, y_attn, sa, ss, merged


def _merge_fwd(x, y, att, ga, gs, g2, g3, w_glu, w_ssm, w_attn, w_out, tile):
    T = x.shape[0]

    def body(x_ref, y_ref, att_ref, ga_ref, gs_ref, g2_ref, g3_ref, wg_ref, ws_ref, wa_ref, wo_ref, x1_ref, o_ref, h2_ref):
        merged = _merge_forward(y_ref[...], att_ref[...], ga_ref[...], gs_ref[...], wg_ref[...], ws_ref[...], wa_ref[...])[-1]
        o = _dot(merged, wo_ref[...])
        x1 = x_ref[...] + o * _rms_scale(o) * g2_ref[...]
        o_ref[...] = o
        x1_ref[...] = x1
        h2_ref[...] = (x1 * _rms_scale(x1) * g3_ref[...]).astype(BF16)

    tok = lambda w: pl.BlockSpec((tile, w), lambda i: (i, 0))
    vec = _const_spec((1, D_MODEL))
    return pl.pallas_call(
        body, name="merge_fwd", grid=(T // tile,),
        in_specs=[tok(D_MODEL), tok(SSM_W), tok(ATTN_W), tok(D_MODEL), tok(D_MODEL), vec, vec,
                  _const_spec((SSM_W, SSM_W)), _const_spec((SSM_W, D_MODEL)), _const_spec((ATTN_W, D_MODEL)),
                  _const_spec((D_MODEL, D_MODEL))],
        out_specs=[tok(D_MODEL), tok(D_MODEL), tok(D_MODEL)],
        out_shape=_hbm_out([jax.ShapeDtypeStruct((T, D_MODEL), F32), jax.ShapeDtypeStruct((T, D_MODEL), F32),
                            jax.ShapeDtypeStruct((T, D_MODEL), BF16)]),
        compiler_params=_params(("arbitrary",), VMEM_MID),
    )(*_in_hbm(x, y, att, ga, gs, g2, g3, w_glu, w_ssm, w_attn, w_out))


def _merge_bwd(dh2, dx2, x1, o, y, att, ga, gs, g2, g3, w_glu, w_ssm, w_attn, w_out, tile, carry=None):
    T = x1.shape[0]
    n_steps = T // tile

    def body(dh2_ref, dx2_ref, x1_ref, o_ref, y_ref, att_ref, ga_ref, gs_ref, g2_ref, g3_ref, wg_ref, ws_ref, wa_ref, wo_ref,
             dx1_ref, dgates_ref, datt_ref, dy_ref, dwg_hbm, dws_hbm, dwa_hbm, dwo_hbm, dg2_ref, dg3_ref,
             awg_ref, aws_ref, awa_ref, awo_ref):
        i = pl.program_id(0)
        x1v, ov = x1_ref[...], o_ref[...]
        dxn, dg3 = _rms_bwd(dh2_ref[...], x1v, _rms_scale(x1v), g3_ref[...])
        dx1 = dx2_ref[...] + dxn
        dx1_ref[...] = dx1
        do, dg2 = _rms_bwd(dx1, ov, _rms_scale(ov), g2_ref[...])
        dob = do.astype(BF16)

        yv = y_ref[...]
        att = att_ref[...]
        z, zb, gl, z2b, y_ssm, y_attn, sa, ss, merged = _merge_forward(
            yv, att, ga_ref[...], gs_ref[...], wg_ref[...], ws_ref[...], wa_ref[...])
        dmerged = _dot_nt(dob, wo_ref[...])
        dya = (dmerged * sa).astype(BF16)
        dys = (dmerged * ss).astype(BF16)
        dgates_ref[:, :D_MODEL] = (dmerged * y_attn * sa * (1.0 - sa)).astype(BF16)
        dgates_ref[:, D_MODEL:] = (dmerged * y_ssm * ss * (1.0 - ss)).astype(BF16)
        datt_ref[...] = _dot_nt(dya, wa_ref[...]).astype(BF16)
        dz2 = _dot_nt(dys, ws_ref[...])
        dpre = (dz2 * z * gl * (1.0 - gl)).astype(BF16)
        dz = dz2 * gl + _dot_nt(dpre, wg_ref[...])
        _, gelu_vjp = jax.vjp(jax.nn.gelu, yv)
        dy_ref[...] = gelu_vjp(dz)[0]

        grads = ((awo_ref, _dot_tn(merged, dob)), (awa_ref, _dot_tn(att, dya)),
                 (aws_ref, _dot_tn(z2b, dys)), (awg_ref, _dot_tn(zb, dpre)), (dg2_ref, dg2), (dg3_ref, dg3))

        @pl.when(i == 0)
        def _():
            for ref, val in grads:
                ref[...] = val

        @pl.when(i > 0)
        def _():
            for ref, val in grads:
                ref[...] += val

        @pl.when(i == n_steps - 1)
        def _():
            pltpu.sync_copy(awg_ref, dwg_hbm)
            pltpu.sync_copy(aws_ref, dws_hbm)
            pltpu.sync_copy(awa_ref, dwa_hbm)
            pltpu.sync_copy(awo_ref, dwo_hbm)

    tok = lambda w: pl.BlockSpec((tile, w), lambda i: (i, 0))
    vec = _const_spec((1, D_MODEL))
    any_ = pl.BlockSpec(memory_space=pl.ANY)
    vec_out = pl.BlockSpec((1, D_MODEL), lambda i: (0, 0))
    f32 = lambda *s: jax.ShapeDtypeStruct(s, F32)
    bf = lambda *s: jax.ShapeDtypeStruct(s, BF16)
    return _hosted_call(
        body, carry, _edge_1d(n_steps), name="merge_bwd", grid=(n_steps,),
        in_specs=[tok(D_MODEL), tok(D_MODEL), tok(D_MODEL), tok(D_MODEL), tok(SSM_W), tok(ATTN_W), tok(D_MODEL), tok(D_MODEL),
                  vec, vec, _const_spec((SSM_W, SSM_W)), _const_spec((SSM_W, D_MODEL)), _const_spec((ATTN_W, D_MODEL)),
                  _const_spec((D_MODEL, D_MODEL))],
        out_specs=[tok(D_MODEL), tok(2 * D_MODEL), tok(ATTN_W), tok(SSM_W), any_, any_, any_, any_, vec_out, vec_out],
        out_shape=[f32(T, D_MODEL), bf(T, 2 * D_MODEL), bf(T, ATTN_W), f32(T, SSM_W),
                   f32(SSM_W, SSM_W), f32(SSM_W, D_MODEL), f32(ATTN_W, D_MODEL), f32(D_MODEL, D_MODEL),
                   f32(1, D_MODEL), f32(1, D_MODEL)],
        scratch_shapes=[pltpu.VMEM((SSM_W, SSM_W), F32), pltpu.VMEM((SSM_W, D_MODEL), F32),
                        pltpu.VMEM((ATTN_W, D_MODEL), F32), pltpu.VMEM((D_MODEL, D_MODEL), F32)],
        compiler_params=_params(("arbitrary",), VMEM_BIG),
        inputs=(dh2, dx2, x1, o, y, att, ga, gs, g2, g3, w_glu, w_ssm, w_attn, w_out))


FF_SHARD = D_FF // N_DEV


def _mlp_fwd(h2, x1, target, g4, w_ff_in, w_ff_out, tile):
    T = h2.shape[0]
    col_chunk = 2 * FF_SHARD

    def body(h2_ref, x1_ref, tg_ref, g4_ref, wi_ref, wo_ref, a_ref, dfo_ref, dx2_ref, loss_ref, dg4_ref, rr_ref):
        i = pl.program_id(0)
        h2v = h2_ref[...]
        for c in range(D_FF // col_chunk):
            cols = slice(c * col_chunk, (c + 1) * col_chunk)
            a = _dot_nt(h2v, wi_ref[cols, :])
            a_ref[:, cols] = a.astype(BF16)
            ra = jnp.maximum(a, 0.0)
            rr_ref[:, cols] = (ra * ra).astype(BF16)
        f = _dot(rr_ref[...], wo_ref[...])
        r = _rms_scale(f)
        g = g4_ref[...]
        err = x1_ref[...] + f * r * g - tg_ref[...]
        dx2 = err * (1.0 / D_MODEL)
        dx2_ref[...] = dx2
        dfo, dg = _rms_bwd(dx2, f, r, g)
        dfo_ref[...] = dfo.astype(BF16)
        row = lax.broadcasted_iota(jnp.int32, (8, LANES), 0)
        col = lax.broadcasted_iota(jnp.int32, (8, LANES), 1)
        loss = jnp.where((row == 0) & (col == 0), (0.5 / D_MODEL) * jnp.sum(err * err), 0.0)

        @pl.when(i == 0)
        def _():
            loss_ref[...] = loss
            dg4_ref[...] = dg

        @pl.when(i > 0)
        def _():
            loss_ref[...] += loss
            dg4_ref[...] += dg

    tok = pl.BlockSpec((tile, D_MODEL), lambda i: (i, 0))
    return pl.pallas_call(
        body, name="mlp_fwd", grid=(T // tile,),
        in_specs=[tok, tok, tok, _const_spec((1, D_MODEL)), _const_spec((D_FF, D_MODEL)), _const_spec((D_FF, D_MODEL))],
        out_specs=[pl.BlockSpec((tile, D_FF), lambda i: (i, 0)), tok, tok,
                   pl.BlockSpec((8, LANES), lambda i: (0, 0)), pl.BlockSpec((1, D_MODEL), lambda i: (0, 0))],
        out_shape=_hbm_out([jax.ShapeDtypeStruct((T, D_FF), BF16), jax.ShapeDtypeStruct((T, D_MODEL), BF16),
                            jax.ShapeDtypeStruct((T, D_MODEL), F32), jax.ShapeDtypeStruct((8, LANES), F32),
                            jax.ShapeDtypeStruct((1, D_MODEL), F32)]),
        scratch_shapes=[pltpu.VMEM((tile, D_FF), BF16)],
        compiler_params=_params(("arbitrary",), VMEM_MAX),
    )(*_in_hbm(h2, x1, target, g4, w_ff_in.reshape(D_FF, D_MODEL), w_ff_out.reshape(D_FF, D_MODEL)))


def _mlp_weight_grads(dfo, a, h2, w_ff_out, row_chunk):
    T = h2.shape[0]

    def body(dfo_ref, h2_ref, a_ref, wo_ref, dwi_ref, dwo_ref, da_ref, rr_ref):
        def rows(r, _):
            sl = pl.ds(pl.multiple_of(r * row_chunk, row_chunk), row_chunk)
            ra = jnp.maximum(a_ref[sl, :].astype(F32), 0.0)
            da_ref[sl, :] = (_dot_nt(dfo_ref[sl, :], wo_ref[0]) * (2.0 * ra)).astype(BF16)
            rr_ref[sl, :] = (ra * ra).astype(BF16)
            return 0

        lax.fori_loop(0, T // row_chunk, rows, 0)
        dwo_ref[0] = _dot_tn(rr_ref[...], dfo_ref[...])
        dwi_ref[0] = _dot_tn(h2_ref[...], da_ref[...])

    return pl.pallas_call(
        body, name="mlp_weight_grads", grid=(N_DEV,),
        in_specs=[_const_spec((T, D_MODEL)), _const_spec((T, D_MODEL)), pl.BlockSpec((T, FF_SHARD), lambda k: (0, k)),
                  pl.BlockSpec((1, FF_SHARD, D_MODEL), lambda k: (k, 0, 0))],
        out_specs=[pl.BlockSpec((1, D_MODEL, FF_SHARD), lambda k: (k, 0, 0)),
                   pl.BlockSpec((1, FF_SHARD, D_MODEL), lambda k: (k, 0, 0)), pl.BlockSpec((T, FF_SHARD), lambda k: (0, k))],
        out_shape=_hbm_out([jax.ShapeDtypeStruct((N_DEV, D_MODEL, FF_SHARD), F32),
                            jax.ShapeDtypeStruct((N_DEV, FF_SHARD, D_MODEL), F32), jax.ShapeDtypeStruct((T, D_FF), BF16)]),
        scratch_shapes=[pltpu.VMEM((T, FF_SHARD), BF16)],
        compiler_params=_params(("arbitrary",), VMEM_MAX),
    )(*_in_hbm(dfo, h2, a, w_ff_out))


def _mlp_input_grad(da, w_ff_in_t, tile):
    T = da.shape[0]

    def body(da_ref, w_ref, o_ref):
        o_ref[...] = _dot(da_ref[...], w_ref[...])

    return pl.pallas_call(
        body, name="mlp_input_grad", grid=(T // tile,),
        in_specs=[pl.BlockSpec((tile, D_FF), lambda i: (i, 0)), _const_spec((D_FF, D_MODEL))],
        out_specs=pl.BlockSpec((tile, D_MODEL), lambda i: (i, 0)),
        out_shape=_hbm_out(jax.ShapeDtypeStruct((T, D_MODEL), F32)),
        compiler_params=_params(("arbitrary",), VMEM_MID),
    )(*_in_hbm(da, w_ff_in_t))


def _block_diag_in(b):
    bt = b.reshape(N_SSM_BLOCKS, 8, GROUP_CH, N_STATE)
    eye = jnp.eye(8, dtype=b.dtype)
    return jnp.einsum("jacp,ab->jacbp", bt, eye).reshape(N_SSM_BLOCKS, LANES, SSM_LANE_BLOCK)


def _block_diag_in_grad(g):
    g = g.reshape(N_SSM_BLOCKS, 8, GROUP_CH, 8, N_STATE)
    d = jnp.diagonal(g, axis1=1, axis2=3)
    return jnp.transpose(d, (0, 3, 1, 2)).reshape(N_GROUPS, GROUP_CH, N_STATE)


def _block_diag_out(c):
    ct = c.reshape(N_SSM_BLOCKS, 8, GROUP_CH, N_STATE)
    eye = jnp.eye(8, dtype=c.dtype)
    return jnp.einsum("jacp,ab->japbc", ct, eye).reshape(N_SSM_BLOCKS, SSM_LANE_BLOCK, LANES)


def _block_diag_out_grad(g):
    g = g.reshape(N_SSM_BLOCKS, 8, N_STATE, 8, GROUP_CH)
    d = jnp.diagonal(g, axis1=1, axis2=3)
    return jnp.transpose(d, (0, 3, 2, 1)).reshape(N_GROUPS, GROUP_CH, N_STATE)


def _tiles(T):
    return dict(proj=min(512, T), proj_bwd=min(512, T // 2), merge=min(512, T), merge_bwd=min(256, T),
                mlp_fwd=min(512, T), mlp_bwd=min(512, T), ssm_chunk=min(1024, T))


def _mesh_position():
    x, y, c = lax.axis_index("x"), lax.axis_index("y"), lax.axis_index("c")
    other_chips = [(1 - x, y), (x, 1 - y), (1 - x, 1 - y)]
    return x, y, c, other_chips


def _gather_carry(arrays):
    n = len(arrays)

    def copies(ins, outs, sems):
        send_sems, recv_sems, local_sems = sems
        x, y, c, chips = _mesh_position()
        me, sibling = (x, y, c), (x, y, 1 - c)

        def copy(a, k, block, to, src=None):
            px, py, pc = block
            dst = outs[a].at[4 * px + 2 * py + pc]
            return pltpu.make_async_remote_copy(
                src_ref=dst if src is None else src, dst_ref=dst, send_sem=send_sems.at[7 * a + k],
                recv_sem=recv_sems.at[7 * a + k], device_id=to, device_id_type=MESH_IDS)

        mine = [pltpu.make_async_copy(ins[a], outs[a].at[4 * x + 2 * y + c], local_sems.at[a]) for a in range(n)]
        first = []
        for a in range(n):
            first.append(copy(a, 0, me, sibling, src=ins[a]))
            first += [copy(a, 1 + j, me, (*chip, c), src=ins[a]) for j, chip in enumerate(chips)]
        return copy, mine, first, me, sibling, chips, c

    def start(ins, outs, sems):
        _, mine, first, *_ = copies(ins, outs, sems)
        for cp in mine + first:
            cp.start()

    def finish(ins, outs, sems):
        copy, mine, first, me, sibling, chips, c = copies(ins, outs, sems)
        passed = []
        for a in range(n):
            for j, chip in enumerate(chips):
                copy(a, 1 + j, (*chip, c), me).wait_recv()
                passed.append(copy(a, 4 + j, (*chip, c), sibling))
                passed[-1].start()
        for a in range(n):
            copy(a, 0, sibling, me).wait_recv()
            for j, chip in enumerate(chips):
                copy(a, 4 + j, (*chip, 1 - c), me).wait_recv()
        for cp in first + passed:
            cp.wait_send()
        for cp in mine:
            cp.wait()

    return _Carry(arrays, [jax.ShapeDtypeStruct((N_DEV,) + a.shape, a.dtype) for a in arrays],
                  [pltpu.SemaphoreType.DMA((7 * n,)), pltpu.SemaphoreType.DMA((7 * n,)), pltpu.SemaphoreType.DMA((n,))],
                  start, finish)


def _pairwise_carry(arrays, n_slots, make_copies):
    n = len(arrays)

    def start(ins, outs, sems):
        for cp in make_copies(ins, outs, sems):
            cp.start()

    def finish(ins, outs, sems):
        for cp in make_copies(ins, outs, sems):
            cp.wait()

    return _Carry(arrays, [jax.ShapeDtypeStruct((n_slots,) + a.shape[1:], a.dtype) for a in arrays],
                  [pltpu.SemaphoreType.DMA((n_slots * n,)), pltpu.SemaphoreType.DMA((n_slots * n,))], start, finish)


def _sibling_carry(grads):
    def make_copies(ins, outs, sems):
        x, y, c, _ = _mesh_position()
        return [pltpu.make_async_remote_copy(
            src_ref=ins[a].at[2 * ch + (1 - c)], dst_ref=outs[a].at[ch], send_sem=sems[0].at[4 * a + ch],
            recv_sem=sems[1].at[4 * a + ch], device_id=(x, y, 1 - c), device_id_type=MESH_IDS)
            for a in range(len(grads)) for ch in range(4)]

    return _pairwise_carry(grads, 4, make_copies)


def _chips_carry(sums):
    def make_copies(ins, outs, sems):
        x, y, c, chips = _mesh_position()
        return [pltpu.make_async_remote_copy(
            src_ref=ins[a].at[2 * px + py], dst_ref=outs[a].at[j], send_sem=sems[0].at[3 * a + j],
            recv_sem=sems[1].at[3 * a + j], device_id=(px, py, c), device_id_type=MESH_IDS)
            for a in range(len(sums)) for j, (px, py) in enumerate(chips)]

    return _pairwise_carry(sums, 3, make_copies)


def _add_sibling(grads8, recvs, core, row_tiles, name):
    k = len(grads8)
    g4 = [g.reshape(4, 2, *g.shape[1:]) for g in grads8]

    def body(core_ref, *refs):
        g_refs, r_refs, o_refs, ob_refs = (refs[j * k:(j + 1) * k] for j in range(4))
        for g_ref, r_ref, o_ref, ob_ref in zip(g_refs, r_refs, o_refs, ob_refs):
            s = g_ref[0] + r_ref[...]
            o_ref[...] = s
            ob_ref[...] = s.astype(BF16)

    def blocks(make):
        return [make(g.shape[1] // row_tiles, g.shape[2]) for g in grads8]

    slot = lambda tr, C: pl.BlockSpec((1, tr, C), lambda ch, r, core_ref: (ch, r, 0))
    outs = pl.pallas_call(
        body, name=name,
        grid_spec=pltpu.PrefetchScalarGridSpec(
            num_scalar_prefetch=1, grid=(4, row_tiles),
            in_specs=blocks(lambda tr, C: pl.BlockSpec((1, 1, tr, C), lambda ch, r, core_ref: (ch, core_ref[0], r, 0)))
            + blocks(slot), out_specs=blocks(slot) + blocks(slot)),
        out_shape=_hbm_out([jax.ShapeDtypeStruct((4,) + g.shape[1:], F32) for g in grads8]
                           + [jax.ShapeDtypeStruct((4,) + g.shape[1:], BF16) for g in grads8]),
        compiler_params=_params(("arbitrary", "arbitrary")),
    )(core, *_in_hbm(*g4, *recvs))
    return list(outs[:k]), list(outs[k:])


def _adam_math(w, g, m, v):
    m = ADAM_B1 * m + (1.0 - ADAM_B1) * g
    v = ADAM_B2 * v + (1.0 - ADAM_B2) * jnp.square(g)
    m_hat = m / (1.0 - ADAM_B1 ** ADAM_STEP)
    v_hat = v / (1.0 - ADAM_B2 ** ADAM_STEP)
    delta = -ADAM_LR * (m_hat / (jnp.sqrt(v_hat) + ADAM_EPS) + ADAM_WD * w)
    return delta, m, v


def _adam_big(ws, ms, vs, chip_sums, recvs, chip, row_tiles, name):
    k = len(ws)

    def body(chip_ref, *refs):
        w_refs, m_refs, v_refs, s_refs, r_refs, g_refs, d_refs, nm_refs, nv_refs = (refs[j * k:(j + 1) * k] for j in range(9))
        for a in range(k):
            r_ref = r_refs[a]
            g = s_refs[a][0] + r_ref[0].astype(F32) + r_ref[1].astype(F32) + r_ref[2].astype(F32)
            g_refs[a][...] = g
            d_refs[a][...], nm_refs[a][...], nv_refs[a][...] = _adam_math(w_refs[a][...], g, m_refs[a][...], v_refs[a][...])

    def blocks(make):
        return [make(w.shape[0] // row_tiles, w.shape[1]) for w in ws]

    blk = lambda tr, C: pl.BlockSpec((tr, C), lambda r, chip_ref: (r, 0))
    outs = pl.pallas_call(
        body, name=name,
        grid_spec=pltpu.PrefetchScalarGridSpec(
            num_scalar_prefetch=1, grid=(row_tiles,),
            in_specs=blocks(blk) * 3 + blocks(lambda tr, C: pl.BlockSpec((1, tr, C), lambda r, chip_ref: (chip_ref[0], r, 0)))
            + blocks(lambda tr, C: pl.BlockSpec((3, tr, C), lambda r, chip_ref: (0, r, 0))),
            out_specs=blocks(blk) * 4),
        out_shape=[jax.ShapeDtypeStruct(w.shape, F32) for w in ws] * 4,
        compiler_params=_params(("arbitrary",)),
    )(chip, *_in_hbm(*ws, *ms, *vs, *chip_sums, *recvs))
    return [list(outs[j * k:(j + 1) * k]) for j in range(4)]


def _sum_partials(partials, name):
    def body(p_ref, g_ref):
        g = p_ref[0]
        for d in range(1, partials.shape[0]):
            g = g + p_ref[d]
        g_ref[...] = g

    return pl.pallas_call(body, name=name, grid=(1,), in_specs=[_whole(partials.shape)], out_specs=_whole(partials.shape[1:]),
                          out_shape=jax.ShapeDtypeStruct(partials.shape[1:], F32))(*_in_hbm(partials))


def _adam_small(ws, ms, vs, gs):
    n = len(ws)

    def body(*refs):
        w_refs, m_refs, v_refs, g_refs = (refs[i * n:(i + 1) * n] for i in range(4))
        d_refs, nm_refs, nv_refs = (refs[(4 + i) * n:(5 + i) * n] for i in range(3))
        for j in range(n):
            d_refs[j][...], nm_refs[j][...], nv_refs[j][...] = _adam_math(
                w_refs[j][...], g_refs[j][...], m_refs[j][...], v_refs[j][...])

    specs = [_whole(w.shape) for w in ws]
    outs = pl.pallas_call(body, name="adam_small", grid=(1,), in_specs=specs * 4, out_specs=specs * 3,
                          out_shape=[jax.ShapeDtypeStruct(w.shape, F32) for w in ws] * 3,
                          compiler_params=_params(("arbitrary",), VMEM_MID))(*_in_hbm(*ws, *ms, *vs, *gs))
    return outs[:n], outs[n:2 * n], outs[2 * n:]


PACK_QUANTUM = SUBLANES * LANES


def _pack(named, names):
    parts = []
    for nme in names:
        flat = named[nme].reshape(-1)
        parts.append(jnp.pad(flat, (0, -flat.size % PACK_QUANTUM)))
    return jnp.concatenate(parts).reshape(-1, LANES)


def _unpack(packed, shapes, names):
    flat = packed.reshape(-1)
    out, pos = {}, 0
    for nme in names:
        size = math.prod(shapes[nme])
        out[nme] = flat[pos:pos + size].reshape(shapes[nme])
        pos += size + (-size % PACK_QUANTUM)
    return out


BIG = ("w_in", "w_glu", "w_attn_branch", "w_ssm_branch", "w_out", "w_ff_in", "w_ff_out")
COLUMN_SHARDED = ("w_in", "w_attn_branch", "w_ssm_branch", "w_ff_in")
SMALL = ("norm_mix_pre", "norm_mix_post", "norm_mlp_pre", "norm_mlp_post", "rel_bias", "sinks", "lam_re", "lam_im",
         "log_dt", "b_re", "b_im", "c_re", "c_im", "d_skip")
SWAPPED_SMALL = ("rel_bias", "b_re", "b_im")
SMALL_LATE = ("rel_bias", "sinks", "loss")
SMALL_BEFORE_ATTN_BWD = tuple(n for n in SMALL if n not in SMALL_LATE + ("norm_mix_pre",))
ALL_WEIGHTS = ("norm_mix_pre", "norm_mix_post", "norm_mlp_pre", "norm_mlp_post", "w_in", "rel_bias", "sinks", "lam_re",
               "lam_im", "log_dt", "b_re", "b_im", "c_re", "c_im", "d_skip", "w_glu", "w_attn_branch", "w_ssm_branch",
               "w_out", "w_ff_in", "w_ff_out")


def _full_from_gathered(name, gathered):
    _, r, c = gathered.shape
    if name in COLUMN_SHARDED:
        return jnp.transpose(gathered, (1, 0, 2)).reshape(r, N_DEV * c)
    return gathered.reshape(N_DEV * r, c)


def _blocks_from_full(name, full):
    r, c = full.shape
    if name in COLUMN_SHARDED:
        return jnp.transpose(full.reshape(r, N_DEV, c // N_DEV), (1, 0, 2))
    return full.reshape(N_DEV, r // N_DEV, c)


def kernel(x, norm_mix_pre, norm_mix_post, norm_mlp_pre, norm_mlp_post, w_in, rel_bias, sinks, lam_re, lam_im, log_dt, b_re, b_im, c_re, c_im, d_skip, w_glu, w_attn_branch, w_ssm_branch, w_out, w_ff_in, w_ff_out, loss_target, m_norm_mix_pre, m_norm_mix_post, m_norm_mlp_pre, m_norm_mlp_post, m_w_in, m_rel_bias, m_sinks, m_lam_re, m_lam_im, m_log_dt, m_b_re, m_b_im, m_c_re, m_c_im, m_d_skip, m_w_glu, m_w_attn_branch, m_w_ssm_branch, m_w_out, m_w_ff_in, m_w_ff_out, v_norm_mix_pre, v_norm_mix_post, v_norm_mlp_pre, v_norm_mlp_post, v_w_in, v_rel_bias, v_sinks, v_lam_re, v_lam_im, v_log_dt, v_b_re, v_b_im, v_c_re, v_c_im, v_d_skip, v_w_glu, v_w_attn_branch, v_w_ssm_branch, v_w_out, v_w_ff_in, v_w_ff_out):
    args = dict(locals())
    w = {n: args[n] for n in ALL_WEIGHTS}
    m = {n: args["m_" + n] for n in ALL_WEIGHTS}
    v = {n: args["v_" + n] for n in ALL_WEIGHTS}
    core = lax.axis_index("c").astype(jnp.int32).reshape(1)
    chip = (2 * lax.axis_index("x") + lax.axis_index("y")).astype(jnp.int32).reshape(1)
    xs, target = x[0], loss_target[0]
    t = _tiles(xs.shape[0])
    local = lambda d, n: d[n][0].T if n == "w_in" else d[n][0]
    shard = {n: local(w, n).astype(BF16) for n in BIG}
    shard["w_ff_in"] = shard["w_ff_in"].T
    view = lambda n, a: jnp.swapaxes(a, -1, -2) if n in SWAPPED_SMALL else a
    small = {n: (view(n, w[n]) if n == "rel_bias" else view(n, w[n])[0]) for n in SMALL}
    g1, g2, g3, g4 = (small[n].reshape(1, D_MODEL) for n in ("norm_mix_pre", "norm_mix_post", "norm_mlp_pre", "norm_mlp_post"))
    bucket = jnp.asarray(_bucket_table())
    rel_b, sink = small["rel_bias"], small["sinks"].reshape(1, N_HEADS)
    lam_r, lam_i = small["lam_re"].reshape(1, STATES), small["lam_im"].reshape(1, STATES)
    ldt_rep = jnp.repeat(small["log_dt"].reshape(N_GROUPS), N_STATE).reshape(1, STATES)
    bd_re, bd_im = _block_diag_in(small["b_re"]), _block_diag_in(small["b_im"])
    cm_re, cm_im = _block_diag_out(small["c_re"]).astype(BF16), _block_diag_out(small["c_im"]).astype(BF16)
    dsk = small["d_skip"].reshape(1, SSM_W)

    (g_in,) = _run_carry(_gather_carry([shard["w_in"]]), "gather_w_in")
    wf_in = g_in.reshape(IN_W, D_MODEL)
    merge_names = ("w_glu", "w_attn_branch", "w_ssm_branch", "w_out")
    (q, k, vv, u, ga, gs, h), gathered = _in_proj_fwd(xs, g1, wf_in, t["proj"], _gather_carry([shard[n] for n in merge_names]))
    wf = {n: _full_from_gathered(n, g) for n, g in zip(merge_names, gathered)}
    (att,), (wf_ff_in,) = _attn_fwd(q, k, vv, bucket, rel_b, sink, _gather_carry([shard["w_ff_in"]]))
    a_re, a_im, bm_re, bm_im = _ssm_prep(lam_r, lam_i, ldt_rep, bd_re, bd_im)
    (y, h_re, h_im, in_re, in_im), (wf_ff_out,) = _ssm_fwd(
        u, a_re, a_im, bm_re, bm_im, cm_re, cm_im, dsk, t["ssm_chunk"], _gather_carry([shard["w_ff_out"]]))
    x1, o, h2 = _merge_fwd(xs, y, att, ga, gs, g2, g3, wf["w_glu"], wf["w_ssm_branch"], wf["w_attn_branch"], wf["w_out"],
                           t["merge"])
    a, dfo, dx2, loss_blk, dg4 = _mlp_fwd(h2, x1, target, g4, wf_ff_in, wf_ff_out, t["mlp_fwd"])

    groups = {"ff": 4, "merge": 1, "w_in": 2}

    def add_sibling(group, blocks, received):
        return _add_sibling(blocks, received, core, groups[group], "add_sibling_" + group)

    ff_names = ("w_ff_in", "w_ff_out")
    dw_ff_in, dw_ff_out, da = _mlp_weight_grads(dfo, a, h2, wf_ff_out, t["mlp_bwd"])
    dh2 = _mlp_input_grad(da, wf_ff_in.reshape(D_FF, D_MODEL), t["mlp_bwd"])
    ff_blocks = [dw_ff_in, dw_ff_out]
    (dx1, dgates, datt, dy, dw_glu, dw_ssm, dw_attn, dw_out, dg2, dg3), ff_recv = _merge_bwd(
        dh2, dx2, x1, o, y, att, ga, gs, g2, g3, wf["w_glu"], wf["w_ssm_branch"], wf["w_attn_branch"], wf["w_out"],
        t["merge_bwd"], _sibling_carry(ff_blocks))
    ff_sums, ff_sums_bf = add_sibling("ff", ff_blocks, ff_recv)
    merge_blocks = [_blocks_from_full(n, g) for n, g in zip(merge_names, (dw_glu, dw_attn, dw_ssm, dw_out))]
    (du, dbm_re, dbm_im, dcm_re, dcm_im, da_re, da_im, dd_skip), carried = _ssm_bwd(
        dy, u, h_re, h_im, in_re, in_im, a_re, a_im, bm_re, bm_im, cm_re, cm_im, dsk, t["ssm_chunk"],
        _join(_chips_carry(ff_sums_bf), _sibling_carry(merge_blocks)))
    ff_from_chips, merge_recv = carried[:2], carried[2:]
    merge_sums, merge_sums_bf = add_sibling("merge", merge_blocks, merge_recv)
    dbd_re, dbd_im, dlam_re, dlam_im, dldt_rep = _ssm_prep_bwd(lam_r, lam_i, ldt_rep, bd_re, bd_im, dbm_re, dbm_im, da_re, da_im)
    dlog_dt = _group_sum(dldt_rep.reshape(N_GROUPS, N_STATE))
    shapes = {n: view(n, w[n]).shape for n in SMALL}
    shapes["loss"] = (1,)
    small_grads = dict(
        norm_mix_post=dg2, norm_mlp_pre=dg3, norm_mlp_post=dg4, lam_re=dlam_re, lam_im=dlam_im, log_dt=dlog_dt,
        b_re=_block_diag_in_grad(dbd_re), b_im=_block_diag_in_grad(dbd_im),
        c_re=_block_diag_out_grad(dcm_re), c_im=_block_diag_out_grad(dcm_im), d_skip=dd_skip)
    packed_early = _pack({n: small_grads[n].reshape(shapes[n]) for n in SMALL_BEFORE_ATTN_BWD}, SMALL_BEFORE_ATTN_BWD)
    (dq, dkv, attn_small), carried = _attn_bwd(
        q, k, vv, datt, bucket, rel_b, sink, _join(_chips_carry(merge_sums_bf), _gather_carry([packed_early])))
    merge_from_chips, partials_early = carried[:-1], carried[-1]

    dparts = (dq, dkv, du, dgates)
    dw_in_t = _in_proj_weight_grad(h, dparts)
    in_blocks = [dw_in_t.reshape(N_DEV, IN_W // N_DEV, D_MODEL)]
    n_tiles = xs.shape[0] // t["proj_bwd"]
    first_part = max(1, (3 * n_tiles) // 8)
    (gx_a, dg1_a), in_recv = _in_proj_input_grad(
        xs, g1, wf_in, dx1, dparts, t["proj_bwd"], 0, first_part, "in_proj_input_grad_a", _sibling_carry(in_blocks))
    in_sums, in_sums_bf = add_sibling("w_in", in_blocks, in_recv)
    late = dict(rel_bias=attn_small[:, :N_BUCKETS, 0], sinks=attn_small[:, N_BUCKETS, 0], loss=loss_blk[0:1, 0])
    packed_late = _pack({n: late[n].reshape(shapes[n]) for n in SMALL_LATE}, SMALL_LATE)
    (gx_b, dg1_b), (in_from_chips, partials_late) = _in_proj_input_grad(
        xs, g1, wf_in, dx1, dparts, t["proj_bwd"], first_part, n_tiles - first_part, "in_proj_input_grad_b",
        _join(_chips_carry(in_sums_bf), _gather_carry([packed_late])))
    grad_x = jnp.concatenate([gx_a, gx_b], axis=0)
    (dg1_partials,) = _run_carry(_gather_carry([jnp.concatenate([dg1_a, dg1_b], axis=0)]), "gather_norm_grad")

    grads, deltas, new_m, new_v = {}, {}, {}, {}
    for group, names, sums, received in (("ff", ff_names, ff_sums, ff_from_chips), ("merge", merge_names, merge_sums, merge_from_chips),
                                         ("w_in", ("w_in",), in_sums, [in_from_chips])):
        outs = _adam_big(*[[local(d, n) for n in names] for d in (w, m, v)], sums, received, chip, groups[group], "adam_" + group)
        for store, vals in zip((grads, deltas, new_m, new_v), outs):
            store.update({n: (o.T if n == "w_in" else o)[None] for n, o in zip(names, vals)})

    grads.update(_unpack(_sum_partials(partials_early, "sum_small_grads"), shapes, SMALL_BEFORE_ATTN_BWD))
    grads.update(_unpack(_sum_partials(partials_late, "sum_late_grads"), shapes, SMALL_LATE))
    grads["norm_mix_pre"] = _sum_partials(dg1_partials.reshape(2 * N_DEV, 1, D_MODEL), "sum_norm_grad")
    loss = grads.pop("loss").reshape(())
    small_out = _adam_small(*[[view(n, d[n]) for n in SMALL] for d in (w, m, v)], [grads[n] for n in SMALL])
    for store, vals in zip((deltas, new_m, new_v), small_out):
        store.update(zip(SMALL, vals))
    for store in (grads, deltas, new_m, new_v):
        store.update({n: view(n, store[n]) for n in SWAPPED_SMALL})

    return (loss, grad_x[None], *[grads[n] for n in ALL_WEIGHTS], *[deltas[n] for n in ALL_WEIGHTS],
            *[new_m[n] for n in ALL_WEIGHTS], *[new_v[n] for n in ALL_WEIGHTS])
```

```python
import math

import jax
import jax.numpy as jnp
import numpy as np
from jax import lax
from jax.experimental import pallas as pl
from jax.experimental.pallas import tpu as pltpu

F32 = jnp.float32
BF16 = jnp.bfloat16

D_MODEL = 1024
N_HEADS = 8
HEAD_DIM = 64
ATTN_W = 512
KV_W = 128
BLOCK = 128
N_BUCKETS = 32
SSM_W = 512
N_GROUPS = 32
N_STATE = 64
GROUP_CH = 16
STATES = N_GROUPS * N_STATE
D_FF = 4096
IN_W = 3328
SPLITS = (0, 512, 640, 768, 1280, 2304, 3328)
RMS_EPS = 1e-6
NEG_INF = -1e30
SUBLANES = 8
LANES = 128
SSM_LANE_BLOCK = 512
N_SSM_BLOCKS = STATES // SSM_LANE_BLOCK
GROUPS_PER_BLOCK = SSM_LANE_BLOCK // N_STATE
VMEM_BIG = 52 * 1024 * 1024
VMEM_MID = 40 * 1024 * 1024
VMEM_MAX = 60 * 1024 * 1024

ADAM_LR = 0.001
ADAM_B1 = 0.9
ADAM_B2 = 0.999
ADAM_EPS = 1e-08
ADAM_WD = 0.01
ADAM_STEP = 10

N_DEV = 8


def _dot(a, b):
    return jnp.dot(a, b, preferred_element_type=F32)


def _dot_nt(a, b):
    return lax.dot_general(a, b, (((1,), (1,)), ((), ())), preferred_element_type=F32)


def _dot_tn(a, b):
    return lax.dot_general(a, b, (((0,), (0,)), ((), ())), preferred_element_type=F32)


def _rms_scale(x):
    return lax.rsqrt(jnp.mean(x * x, axis=-1, keepdims=True) + RMS_EPS)


def _rms_bwd(dy, x, r, g):
    t = dy * g
    dx = r * t - x * (r * r * r) * jnp.mean(t * x, axis=-1, keepdims=True)
    dg = jnp.sum(dy * x * r, axis=0, keepdims=True)
    return dx, dg


def _const_spec(shape):
    nd = len(shape)
    return pl.BlockSpec(shape, lambda *_: (0,) * nd, pipeline_mode=pl.Buffered(1))


def _in_hbm(*arrays):
    return tuple(pltpu.with_memory_space_constraint(a, pltpu.HBM) for a in arrays)


def _hbm_out(shapes):
    if isinstance(shapes, (list, tuple)):
        return [_hbm_out(s) for s in shapes]
    return shapes if isinstance(shapes, pl.MemoryRef) else pltpu.HBM(shapes.shape, shapes.dtype)


def _whole(shape):
    nd = len(shape)
    return pl.BlockSpec(shape, lambda *_: (0,) * nd)


def _params(sem, vmem=None):
    return pltpu.CompilerParams(dimension_semantics=sem, vmem_limit_bytes=vmem)


MESH_IDS = pl.DeviceIdType.MESH
HBM_SPEC = pl.BlockSpec(memory_space=pl.ANY)


class _Carry:
    def __init__(self, inputs, out_shapes, sems, start, finish):
        self.inputs, self.out_shapes, self.sems, self.start, self.finish = list(inputs), list(out_shapes), list(sems), start, finish


def _join(a, b):
    na_in, na_out, na_sem = len(a.inputs), len(a.out_shapes), len(a.sems)

    def start(ins, outs, sems):
        a.start(ins[:na_in], outs[:na_out], sems[:na_sem])
        b.start(ins[na_in:], outs[na_out:], sems[na_sem:])

    def finish(ins, outs, sems):
        a.finish(ins[:na_in], outs[:na_out], sems[:na_sem])
        b.finish(ins[na_in:], outs[na_out:], sems[na_sem:])

    return _Carry(a.inputs + b.inputs, a.out_shapes + b.out_shapes, a.sems + b.sems, start, finish)


def _hosted_call(body, carry, edge, *, name, grid, in_specs, out_specs, out_shape, scratch_shapes, compiler_params, inputs):
    n_in, n_out = len(in_specs), len(out_specs)
    inputs = [a if s.memory_space == pltpu.SMEM else _in_hbm(a)[0] for a, s in zip(inputs, in_specs)]
    out_shape = _hbm_out(list(out_shape))
    if carry is None:
        outs = pl.pallas_call(body, name=name, grid=grid, in_specs=in_specs, out_specs=out_specs, out_shape=out_shape,
                              scratch_shapes=scratch_shapes, compiler_params=compiler_params)(*inputs)
        return list(outs), []
    c_in, c_out, c_sem = len(carry.inputs), len(carry.out_shapes), len(carry.sems)

    def wrapped(*refs):
        ins, refs = refs[:n_in], refs[n_in:]
        cins, refs = refs[:c_in], refs[c_in:]
        outs, refs = refs[:n_out], refs[n_out:]
        couts, refs = refs[:c_out], refs[c_out:]
        scratch, csems = refs[:len(refs) - c_sem], refs[len(refs) - c_sem:]
        first, last = edge()

        @pl.when(first)
        def _():
            carry.start(cins, couts, csems)

        body(*ins, *outs, *scratch)

        @pl.when(last)
        def _():
            carry.finish(cins, couts, csems)

    outs = pl.pallas_call(
        wrapped, name=name, grid=grid, in_specs=list(in_specs) + [HBM_SPEC] * c_in,
        out_specs=list(out_specs) + [HBM_SPEC] * c_out, out_shape=out_shape + _hbm_out(carry.out_shapes),
        scratch_shapes=list(scratch_shapes) + carry.sems, compiler_params=compiler_params)(*inputs, *_in_hbm(*carry.inputs))
    return list(outs[:n_out]), list(outs[n_out:])


def _edge_1d(n_steps):
    return lambda: (pl.program_id(0) == 0, pl.program_id(0) == n_steps - 1)


def _edge_2d(n0, n1):
    return lambda: ((pl.program_id(0) == 0) & (pl.program_id(1) == 0),
                    (pl.program_id(0) == n0 - 1) & (pl.program_id(1) == n1 - 1))


def _run_carry(carry, name):
    c_in, c_out = len(carry.inputs), len(carry.out_shapes)

    def body(*refs):
        ins, outs, sems = refs[:c_in], refs[c_in:c_in + c_out], refs[c_in + c_out:]
        carry.start(ins, outs, sems)
        carry.finish(ins, outs, sems)

    return pl.pallas_call(body, name=name, in_specs=[HBM_SPEC] * c_in, out_specs=[HBM_SPEC] * c_out,
                          out_shape=_hbm_out(carry.out_shapes), scratch_shapes=carry.sems)(*_in_hbm(*carry.inputs))


def _in_proj_fwd(x, g1, w_in_t, tile, carry=None):
    T = x.shape[0]

    def body(x_ref, g_ref, w_ref, q_ref, k_ref, v_ref, u_ref, ga_ref, gs_ref, h_ref):
        xv = x_ref[...]
        h = (xv * _rms_scale(xv) * g_ref[...]).astype(BF16)
        h_ref[...] = h
        outs = (q_ref, k_ref, v_ref, u_ref, ga_ref, gs_ref)
        for p, o_ref in enumerate(outs):
            o_ref[...] = _dot_nt(h, w_ref[SPLITS[p]:SPLITS[p + 1], :]).astype(o_ref.dtype)

    widths = [SPLITS[p + 1] - SPLITS[p] for p in range(6)] + [D_MODEL]
    dtypes = [BF16, BF16, BF16, F32, F32, F32, BF16]
    return _hosted_call(
        body, carry, _edge_1d(T // tile), name="in_proj_fwd", grid=(T // tile,),
        in_specs=[pl.BlockSpec((tile, D_MODEL), lambda i: (i, 0)), _const_spec((1, D_MODEL)), _const_spec((IN_W, D_MODEL))],
        out_specs=[pl.BlockSpec((tile, w), lambda i: (i, 0)) for w in widths],
        out_shape=[jax.ShapeDtypeStruct((T, w), dt) for w, dt in zip(widths, dtypes)],
        scratch_shapes=[], compiler_params=_params(("arbitrary",), VMEM_MID), inputs=(x, g1, w_in_t))


PROJ_PARTS = (512, 256, 512, 2048)
PROJ_GRAD_BLOCK = 256


def _in_proj_weight_grad(h, dparts):
    T = h.shape[0]
    blocks = [wd // PROJ_GRAD_BLOCK for wd in PROJ_PARTS]
    starts = [sum(blocks[:p]) for p in range(len(blocks))]

    def body(h_ref, *refs):
        part_refs, o_ref = refs[:-1], refs[-1]
        j = pl.program_id(0)
        for p_ref, start, count in zip(part_refs, starts, blocks):
            @pl.when((j >= start) & (j < start + count))
            def _(p_ref=p_ref):
                o_ref[...] = _dot_tn(p_ref[...], h_ref[...])

    def part_spec(start, count):
        return pl.BlockSpec((T, PROJ_GRAD_BLOCK), lambda j: (0, jnp.clip(j - start, 0, count - 1)))

    return pl.pallas_call(
        body, name="in_proj_weight_grad", grid=(sum(blocks),),
        in_specs=[_const_spec((T, D_MODEL))] + [part_spec(s, c) for s, c in zip(starts, blocks)],
        out_specs=pl.BlockSpec((PROJ_GRAD_BLOCK, D_MODEL), lambda j: (j, 0)),
        out_shape=_hbm_out(jax.ShapeDtypeStruct((IN_W, D_MODEL), F32)),
        compiler_params=_params(("arbitrary",), VMEM_MID),
    )(*_in_hbm(h, *dparts))


def _in_proj_input_grad(x, g1, w_in_t, dx1, dparts, tile, first_tile, n_tiles, name, carry=None):
    offsets = [sum(PROJ_PARTS[:p]) for p in range(len(PROJ_PARTS))]

    def body(x_ref, g_ref, w_ref, dx1_ref, *refs):
        part_refs, (gx_ref, dg_ref) = refs[:len(PROJ_PARTS)], refs[len(PROJ_PARTS):]
        i = pl.program_id(0)
        xv = x_ref[...]
        r = _rms_scale(xv)
        g = g_ref[...]
        dh = sum(_dot(p_ref[...], w_ref[off:off + wd, :]) for p_ref, off, wd in zip(part_refs, offsets, PROJ_PARTS))
        dxn, dg = _rms_bwd(dh, xv, r, g)
        gx_ref[...] = dx1_ref[...] + dxn

        @pl.when(i == 0)
        def _():
            dg_ref[...] = dg

        @pl.when(i > 0)
        def _():
            dg_ref[...] += dg

    tok = lambda wd: pl.BlockSpec((tile, wd), lambda i: (i + first_tile, 0))
    return _hosted_call(
        body, carry, _edge_1d(n_tiles), name=name, grid=(n_tiles,),
        in_specs=[tok(D_MODEL), _const_spec((1, D_MODEL)), _const_spec((IN_W, D_MODEL)), tok(D_MODEL)] + [tok(wd) for wd in PROJ_PARTS],
        out_specs=[pl.BlockSpec((tile, D_MODEL), lambda i: (i, 0)), pl.BlockSpec((1, D_MODEL), lambda i: (0, 0))],
        out_shape=[jax.ShapeDtypeStruct((n_tiles * tile, D_MODEL), F32), jax.ShapeDtypeStruct((1, D_MODEL), F32)],
        scratch_shapes=[], compiler_params=_params(("arbitrary",), VMEM_MID), inputs=(x, g1, w_in_t, dx1, *dparts))


def _bucket_table():
    qi = np.arange(BLOCK)[:, None]
    kj = np.arange(2 * BLOCK)[None, :]
    dist = qi + BLOCK - kj
    max_exact = N_BUCKETS // 2
    d = np.maximum(dist, 0)
    df = np.maximum(d, 1).astype(np.float32)
    large = max_exact + (np.log(df / np.float32(max_exact)) / np.float32(math.log(BLOCK / max_exact))
                         * np.float32(N_BUCKETS - max_exact)).astype(np.int32)
    large = np.minimum(large, N_BUCKETS - 1)
    bucket = np.where(d < max_exact, d, large)
    return np.where((dist >= 0) & (dist < BLOCK), bucket, -1).astype(np.int32)


def _build_bias(bucket_ref, rb_ref, bias_ref):
    bk = bucket_ref[...]
    for h in range(N_HEADS):
        def add(b, acc, h=h):
            return acc + jnp.where(bk == b, rb_ref[h, b], 0.0)
        bias_ref[h] = lax.fori_loop(0, N_BUCKETS, add, jnp.zeros((BLOCK, 2 * BLOCK), F32))


def _kv_variants(prev_ref, cur_ref):
    cat = jnp.concatenate([prev_ref[...], cur_ref[...]], axis=0)
    lo = lax.broadcasted_iota(jnp.int32, cat.shape, 1) < HEAD_DIM
    zero = jnp.zeros_like(cat)
    head0_lo = jnp.where(lo, cat, zero)
    head1_hi = jnp.where(lo, zero, cat)
    return ((head0_lo, pltpu.roll(head0_lo, HEAD_DIM, 1)), (pltpu.roll(head1_hi, HEAD_DIM, 1), head1_hi))


def _merge_kv_grads(g):
    lo = lax.broadcasted_iota(jnp.int32, g[0][0].shape, 1) < HEAD_DIM
    return jnp.where(lo, g[0][0] + pltpu.roll(g[0][1], HEAD_DIM, 1), g[1][1] + pltpu.roll(g[1][0], HEAD_DIM, 1))


def _head_lanes(h):
    return slice((h // 2) * LANES, (h // 2 + 1) * LANES)


def _attn_probs(q_ref, kvar, bias_ref, sk_ref, valid, s_ref):
    for h in range(N_HEADS):
        s_ref[h] = _dot_nt(q_ref[:, _head_lanes(h)], kvar[h // 4][h % 2])
    head = lax.broadcasted_iota(jnp.int32, (N_HEADS, 1, 1), 0)
    sink = jnp.zeros((N_HEADS, 1, 1), F32)
    for h in range(N_HEADS):
        sink = jnp.where(head == h, sk_ref[0, h], sink)
    s = jnp.where(valid[None], s_ref[...] * (HEAD_DIM ** -0.5) + bias_ref[...], NEG_INF)
    m = jnp.maximum(jnp.max(s, axis=-1, keepdims=True), sink)
    p = jnp.exp(s - m)
    e_sink = jnp.exp(sink - m)
    inv = 1.0 / (jnp.sum(p, axis=-1, keepdims=True) + e_sink)
    return p * inv, e_sink * inv


def _attn_valid(bucket_ref, n):
    col = lax.broadcasted_iota(jnp.int32, (BLOCK, 2 * BLOCK), 1)
    return (bucket_ref[...] >= 0) & ((n > 0) | (col >= BLOCK))


def _attn_fwd(q, k, v, bucket, rel_bias, sinks, carry=None):
    T = q.shape[0]
    nb = T // BLOCK

    def body(q_ref, kc_ref, kp_ref, vc_ref, vp_ref, bucket_ref, rb_ref, sk_ref, o_ref, bias_ref, s_ref, p_ref):
        n = pl.program_id(0)

        @pl.when(n == 0)
        def _():
            _build_bias(bucket_ref, rb_ref, bias_ref)

        kvar = _kv_variants(kp_ref, kc_ref)
        vvar = _kv_variants(vp_ref, vc_ref)
        pr, _ = _attn_probs(q_ref, kvar, bias_ref, sk_ref, _attn_valid(bucket_ref, n), s_ref)
        p_ref[...] = pr.astype(BF16)
        for m in range(N_HEADS // 2):
            acc = _dot(p_ref[2 * m], vvar[m // 2][0]) + _dot(p_ref[2 * m + 1], vvar[m // 2][1])
            o_ref[:, m * LANES:(m + 1) * LANES] = acc.astype(o_ref.dtype)

    cur = lambda w: pl.BlockSpec((BLOCK, w), lambda n: (n, 0))
    prev = lambda w: pl.BlockSpec((BLOCK, w), lambda n: (jnp.maximum(n - 1, 0), 0))
    smem = pl.BlockSpec(memory_space=pltpu.SMEM)
    return _hosted_call(
        body, carry, _edge_1d(nb), name="attn_fwd", grid=(nb,),
        in_specs=[cur(ATTN_W), cur(KV_W), prev(KV_W), cur(KV_W), prev(KV_W), _const_spec((BLOCK, 2 * BLOCK)), smem, smem],
        out_specs=[cur(ATTN_W)],
        out_shape=[jax.ShapeDtypeStruct((T, ATTN_W), BF16)],
        scratch_shapes=[pltpu.VMEM((N_HEADS, BLOCK, 2 * BLOCK), F32), pltpu.VMEM((N_HEADS, BLOCK, 2 * BLOCK), F32),
                        pltpu.VMEM((N_HEADS, BLOCK, 2 * BLOCK), BF16)],
        compiler_params=_params(("arbitrary",)), inputs=(q, k, k, v, v, bucket, rel_bias, sinks))


ATTN_SMALL_ROWS = N_BUCKETS + SUBLANES


def _attn_bwd(q, k, v, datt, bucket, rel_bias, sinks, carry=None):
    T = q.shape[0]
    nb = T // BLOCK

    def body(q_ref, do_ref, kc_ref, kp_ref, vc_ref, vp_ref, bucket_ref, rb_ref, sk_ref,
             dq_ref, dkv_ref, small_ref, bias_ref, ds_sum_ref, dsink_ref, kcarry_ref, vcarry_ref,
             s_ref, dp_ref, p_ref, dsc_ref):
        n = pl.program_id(0)

        @pl.when(n == 0)
        def _():
            _build_bias(bucket_ref, rb_ref, bias_ref)
            ds_sum_ref[...] = jnp.zeros_like(ds_sum_ref)
            dsink_ref[...] = jnp.zeros_like(dsink_ref)
            kcarry_ref[...] = jnp.zeros_like(kcarry_ref)
            vcarry_ref[...] = jnp.zeros_like(vcarry_ref)

        @pl.when(n < nb)
        def _():
            kvar = _kv_variants(kp_ref, kc_ref)
            vvar = _kv_variants(vp_ref, vc_ref)
            pr, p_sink = _attn_probs(q_ref, kvar, bias_ref, sk_ref, _attn_valid(bucket_ref, n), s_ref)
            for h in range(N_HEADS):
                dp_ref[h] = _dot_nt(do_ref[:, _head_lanes(h)], vvar[h // 4][h % 2])
            dp = dp_ref[...]
            dsum = jnp.sum(pr * dp, axis=-1, keepdims=True)
            ds = pr * (dp - dsum)
            ds_sum_ref[...] += ds
            dsink_ref[...] -= jnp.sum(p_sink * dsum, axis=1, keepdims=True)
            dsc_ref[...] = (ds * (HEAD_DIM ** -0.5)).astype(BF16)
            p_ref[...] = pr.astype(BF16)
            for m in range(N_HEADS // 2):
                dqm = _dot(dsc_ref[2 * m], kvar[m // 2][0]) + _dot(dsc_ref[2 * m + 1], kvar[m // 2][1])
                dq_ref[:, m * LANES:(m + 1) * LANES] = dqm.astype(dq_ref.dtype)
            dk_var = [[None, None], [None, None]]
            dv_var = [[None, None], [None, None]]
            for kvh in range(2):
                for e in range(2):
                    heads = [h for h in range(N_HEADS) if h // 4 == kvh and h % 2 == e]
                    dk_var[kvh][e] = sum(_dot_tn(dsc_ref[h], q_ref[:, _head_lanes(h)]) for h in heads)
                    dv_var[kvh][e] = sum(_dot_tn(p_ref[h], do_ref[:, _head_lanes(h)]) for h in heads)
            dk_cat = _merge_kv_grads(dk_var)
            dv_cat = _merge_kv_grads(dv_var)

            @pl.when(n > 0)
            def _():
                dkv_ref[:, :KV_W] = (kcarry_ref[...] + dk_cat[:BLOCK]).astype(BF16)
                dkv_ref[:, KV_W:] = (vcarry_ref[...] + dv_cat[:BLOCK]).astype(BF16)

            kcarry_ref[...] = dk_cat[BLOCK:]
            vcarry_ref[...] = dv_cat[BLOCK:]

        @pl.when(n == nb)
        def _():
            dkv_ref[:, :KV_W] = kcarry_ref[...].astype(BF16)
            dkv_ref[:, KV_W:] = vcarry_ref[...].astype(BF16)
            bk = bucket_ref[...]
            row = lax.broadcasted_iota(jnp.int32, (N_HEADS, ATTN_SMALL_ROWS, LANES), 1)

            def add(b, acc):
                masked = jnp.where((bk == b)[None], ds_sum_ref[...], 0.0)
                val = jnp.sum(jnp.sum(masked, axis=1, keepdims=True), axis=2, keepdims=True)
                return acc + jnp.where(row == b, val, 0.0)

            small_ref[...] = lax.fori_loop(0, N_BUCKETS, add, jnp.where(row == N_BUCKETS, dsink_ref[...], 0.0))

    last = nb - 1
    cur = lambda w: pl.BlockSpec((BLOCK, w), lambda n: (jnp.minimum(n, last), 0))
    prev = lambda w: pl.BlockSpec((BLOCK, w), lambda n: (jnp.clip(n - 1, 0, last), 0))
    smem = pl.BlockSpec(memory_space=pltpu.SMEM)
    return _hosted_call(
        body, carry, _edge_1d(nb + 1), name="attn_bwd", grid=(nb + 1,),
        in_specs=[cur(ATTN_W), cur(ATTN_W), cur(KV_W), prev(KV_W), cur(KV_W), prev(KV_W),
                  _const_spec((BLOCK, 2 * BLOCK)), smem, smem],
        out_specs=[cur(ATTN_W), prev(2 * KV_W), pl.BlockSpec((N_HEADS, ATTN_SMALL_ROWS, LANES), lambda n: (0, 0, 0))],
        out_shape=[jax.ShapeDtypeStruct((T, ATTN_W), BF16), jax.ShapeDtypeStruct((T, 2 * KV_W), BF16),
                   jax.ShapeDtypeStruct((N_HEADS, ATTN_SMALL_ROWS, LANES), F32)],
        scratch_shapes=[pltpu.VMEM((N_HEADS, BLOCK, 2 * BLOCK), F32), pltpu.VMEM((N_HEADS, BLOCK, 2 * BLOCK), F32),
                        pltpu.VMEM((N_HEADS, 1, 1), F32), pltpu.VMEM((BLOCK, KV_W), F32), pltpu.VMEM((BLOCK, KV_W), F32),
                        pltpu.VMEM((N_HEADS, BLOCK, 2 * BLOCK), F32), pltpu.VMEM((N_HEADS, BLOCK, 2 * BLOCK), F32),
                        pltpu.VMEM((N_HEADS, BLOCK, 2 * BLOCK), BF16), pltpu.VMEM((N_HEADS, BLOCK, 2 * BLOCK), BF16)],
        compiler_params=_params(("arbitrary",)), inputs=(q, datt, k, k, v, v, bucket, rel_bias, sinks))


SCAN_UNROLL = 4


def _cmul(ar, ai, br, bi):
    return ar * br - ai * bi, ar * bi + ai * br


def _cmul_conj(ar, ai, br, bi):
    return ar * br + ai * bi, ar * bi - ai * br


def _ssm_discretize(lr, li, ldt):
    dt = jnp.exp(ldt)
    mag = jnp.exp(lr * dt)
    ab_re = mag * jnp.cos(li * dt)
    ab_im = mag * jnp.sin(li * dt)
    nr = ab_re - 1.0
    den = lr * lr + li * li
    f_re = (nr * lr + ab_im * li) / den
    f_im = (ab_im * lr - nr * li) / den
    return ab_re, ab_im, f_re, f_im


def _ssm_prep(lam_re, lam_im, ldt_rep, bd_re, bd_im):
    def body(lr_ref, li_ref, ldt_ref, bdr_ref, bdi_ref, ar_ref, ai_ref, br_ref, bi_ref):
        ab_re, ab_im, f_re, f_im = _ssm_discretize(lr_ref[...], li_ref[...], ldt_ref[...])
        ar_ref[...] = ab_re
        ai_ref[...] = ab_im
        bdr, bdi = bdr_ref[0], bdi_ref[0]
        br_ref[0] = (bdr * f_re - bdi * f_im).astype(BF16)
        bi_ref[0] = (bdi * f_re + bdr * f_im).astype(BF16)

    row = pl.BlockSpec((1, SSM_LANE_BLOCK), lambda j: (0, j))
    mat = pl.BlockSpec((1, LANES, SSM_LANE_BLOCK), lambda j: (j, 0, 0))
    return pl.pallas_call(
        body, name="ssm_prep", grid=(N_SSM_BLOCKS,),
        in_specs=[row, row, row, mat, mat], out_specs=[row, row, mat, mat],
        out_shape=[jax.ShapeDtypeStruct((1, STATES), F32)] * 2 + [jax.ShapeDtypeStruct((N_SSM_BLOCKS, LANES, SSM_LANE_BLOCK), BF16)] * 2,
        compiler_params=_params(("arbitrary",)),
    )(*_in_hbm(lam_re, lam_im, ldt_rep, bd_re, bd_im))


def _ssm_prep_bwd(lam_re, lam_im, ldt_rep, bd_re, bd_im, dbr, dbi, da_re, da_im):
    def body(lr_ref, li_ref, ldt_ref, bdr_ref, bdi_ref, dbr_ref, dbi_ref, dar_ref, dai_ref,
             dbdr_ref, dbdi_ref, dlr_ref, dli_ref, dldt_ref):
        lr, li, ldt = lr_ref[...], li_ref[...], ldt_ref[...]
        (_, _, f_re, f_im), vjp = jax.vjp(_ssm_discretize, lr, li, ldt)
        bdr, bdi, gbr, gbi = bdr_ref[0], bdi_ref[0], dbr_ref[0], dbi_ref[0]
        dbdr_ref[0] = gbr * f_re + gbi * f_im
        dbdi_ref[0] = gbi * f_re - gbr * f_im
        df_re = jnp.sum(gbr * bdr + gbi * bdi, axis=0, keepdims=True)
        df_im = jnp.sum(gbi * bdr - gbr * bdi, axis=0, keepdims=True)
        dlr, dli, dldt = vjp((dar_ref[...], dai_ref[...], df_re, df_im))
        dlr_ref[...] = dlr
        dli_ref[...] = dli
        dldt_ref[...] = dldt

    row = pl.BlockSpec((1, SSM_LANE_BLOCK), lambda j: (0, j))
    mat = pl.BlockSpec((1, LANES, SSM_LANE_BLOCK), lambda j: (j, 0, 0))
    mat_shape = jax.ShapeDtypeStruct((N_SSM_BLOCKS, LANES, SSM_LANE_BLOCK), F32)
    row_shape = jax.ShapeDtypeStruct((1, STATES), F32)
    return pl.pallas_call(
        body, name="ssm_prep_bwd", grid=(N_SSM_BLOCKS,),
        in_specs=[row, row, row, mat, mat, mat, mat, row, row], out_specs=[mat, mat, row, row, row],
        out_shape=[mat_shape, mat_shape, row_shape, row_shape, row_shape],
        compiler_params=_params(("arbitrary",)),
    )(*_in_hbm(lam_re, lam_im, ldt_rep, bd_re, bd_im, dbr, dbi, da_re, da_im))


def _group_sum(x):
    def body(x_ref, o_ref):
        o_ref[...] = jnp.sum(x_ref[...], axis=1, keepdims=True)
    return pl.pallas_call(body, name="ssm_group_sum", grid=(1,), in_specs=[_whole(x.shape)], out_specs=_whole((N_GROUPS, 1)),
                          out_shape=jax.ShapeDtypeStruct((N_GROUPS, 1), F32))(*_in_hbm(x))


def _power_table(ar, ai, p_re_ref, p_im_ref, steps):
    shape = (SUBLANES, SSM_LANE_BLOCK)
    p_re_ref[0:SUBLANES] = jnp.broadcast_to(ar, shape)
    p_im_ref[0:SUBLANES] = jnp.broadcast_to(ai, shape)
    m = 1
    while m < steps:
        rows = m * SUBLANES
        top_re = p_re_ref[rows - SUBLANES:rows]
        top_im = p_im_ref[rows - SUBLANES:rows]
        cur_re = p_re_ref[0:rows].reshape(m, SUBLANES, SSM_LANE_BLOCK)
        cur_im = p_im_ref[0:rows].reshape(m, SUBLANES, SSM_LANE_BLOCK)
        nxt_re, nxt_im = _cmul(cur_re, cur_im, top_re[None], top_im[None])
        p_re_ref[rows:2 * rows] = nxt_re.reshape(rows, SSM_LANE_BLOCK)
        p_im_ref[rows:2 * rows] = nxt_im.reshape(rows, SSM_LANE_BLOCK)
        m *= 2


def _to_segments(src_ref, dst_ref, steps):
    for s in range(SUBLANES):
        dst_ref[pl.ds(s, steps, stride=SUBLANES), :] = src_ref[s * steps:(s + 1) * steps, :]


def _from_segments(src_ref, dst_ref, steps):
    for s in range(SUBLANES):
        dst_ref[s * steps:(s + 1) * steps, :] = src_ref[pl.ds(s, steps, stride=SUBLANES), :]


def _segment_carries(e_re, e_im, an_re, an_im, c_re, c_im, reverse):
    order = range(SUBLANES - 1, -1, -1) if reverse else range(SUBLANES)
    ins_re, ins_im = [None] * SUBLANES, [None] * SUBLANES
    for s in order:
        ins_re[s], ins_im[s] = c_re, c_im
        pr, pi = _cmul(an_re, an_im, c_re, c_im)
        c_re = e_re[s:s + 1] + pr
        c_im = e_im[s:s + 1] + pi
    return jnp.concatenate(ins_re, axis=0), jnp.concatenate(ins_im, axis=0), c_re, c_im


def _ssm_fwd(u, a_re, a_im, b_re, b_im, c_re, c_im, d_skip, chunk, carry=None):
    T = u.shape[0]
    nc = T // chunk
    steps = chunk // SUBLANES
    blk = SSM_LANE_BLOCK

    def body(u_ref, ar_ref, ai_ref, br_ref, bi_ref, cr_ref, ci_ref, dk_ref,
             y_ref, hr_ref, hi_ref, inr_ref, ini_ref, useg_ref, yseg_ref, pr_ref, pi_ref, carry_ref):
        c = pl.program_id(1)
        ar, ai = ar_ref[...], ai_ref[...]

        @pl.when(c == 0)
        def _():
            _power_table(ar, ai, pr_ref, pi_ref, steps)
            carry_ref[...] = jnp.zeros_like(carry_ref)

        _to_segments(u_ref, useg_ref, steps)
        ub = useg_ref[...].astype(BF16)
        hr_ref[...] = _dot(ub, br_ref[0])
        hi_ref[...] = _dot(ub, bi_ref[0])
        first = slice(0, SUBLANES)

        def scan(t4, prev):
            for j in range(SCAN_UNROLL):
                rows = pl.ds(pl.multiple_of((t4 * SCAN_UNROLL + j) * SUBLANES, SUBLANES), SUBLANES)
                pr, pi = _cmul(pr_ref[first, :], pi_ref[first, :], prev[0], prev[1])
                prev = (pr + hr_ref[rows, :], pi + hi_ref[rows, :])
                hr_ref[rows, :] = prev[0]
                hi_ref[rows, :] = prev[1]
            return prev

        zero = jnp.zeros((SUBLANES, blk), F32)
        lax.fori_loop(0, steps // SCAN_UNROLL, scan, (zero, zero))

        top = slice(chunk - SUBLANES, chunk)
        in_re, in_im, out_re, out_im = _segment_carries(
            hr_ref[top, :], hi_ref[top, :], pr_ref[top, :][0:1], pi_ref[top, :][0:1],
            carry_ref[0:1, :], carry_ref[1:2, :], reverse=False)
        carry_ref[0:1, :] = out_re
        carry_ref[1:2, :] = out_im
        inr_ref[...] = in_re
        ini_ref[...] = in_im

        def fix(t4, _):
            for j in range(SCAN_UNROLL):
                rows = pl.ds(pl.multiple_of((t4 * SCAN_UNROLL + j) * SUBLANES, SUBLANES), SUBLANES)
                fr, fi = _cmul(pr_ref[rows, :], pi_ref[rows, :], in_re, in_im)
                hr_ref[rows, :] += fr
                hi_ref[rows, :] += fi
            return 0

        lax.fori_loop(0, steps // SCAN_UNROLL, fix, 0)

        yseg_ref[...] = _dot(hr_ref[...].astype(BF16), cr_ref[0]) - _dot(hi_ref[...].astype(BF16), ci_ref[0])
        _from_segments(yseg_ref, y_ref, steps)
        y_ref[...] += dk_ref[...] * u_ref[...]

    row = pl.BlockSpec((1, blk), lambda j, c: (0, j))
    b_mat = pl.BlockSpec((1, LANES, blk), lambda j, c: (j, 0, 0))
    c_mat = pl.BlockSpec((1, blk, LANES), lambda j, c: (j, 0, 0))
    tok = pl.BlockSpec((chunk, LANES), lambda j, c: (c, j))
    state = pl.BlockSpec((chunk, blk), lambda j, c: (c, j))
    enter = pl.BlockSpec((SUBLANES, blk), lambda j, c: (c, j))
    return _hosted_call(
        body, carry, _edge_2d(N_SSM_BLOCKS, nc), name="ssm_fwd", grid=(N_SSM_BLOCKS, nc),
        in_specs=[tok, row, row, b_mat, b_mat, c_mat, c_mat, pl.BlockSpec((1, LANES), lambda j, c: (0, j))],
        out_specs=[tok, state, state, enter, enter],
        out_shape=[jax.ShapeDtypeStruct((T, SSM_W), F32), jax.ShapeDtypeStruct((T, STATES), F32),
                   jax.ShapeDtypeStruct((T, STATES), F32), jax.ShapeDtypeStruct((nc * SUBLANES, STATES), F32),
                   jax.ShapeDtypeStruct((nc * SUBLANES, STATES), F32)],
        scratch_shapes=[pltpu.VMEM((chunk, LANES), F32), pltpu.VMEM((chunk, LANES), F32),
                        pltpu.VMEM((chunk, blk), F32), pltpu.VMEM((chunk, blk), F32), pltpu.VMEM((SUBLANES, blk), F32)],
        compiler_params=_params(("arbitrary", "arbitrary"), VMEM_MID),
        inputs=(u, a_re, a_im, b_re, b_im, c_re, c_im, d_skip))


def _ssm_bwd(dy, u, h_re, h_im, in_re, in_im, a_re, a_im, b_re, b_im, c_re, c_im, d_skip, chunk, carry=None):
    T = u.shape[0]
    nc = T // chunk
    steps = chunk // SUBLANES
    blk = SSM_LANE_BLOCK

    def body(dy_ref, u_ref, hr_ref, hi_ref, inr_ref, ini_ref, ar_ref, ai_ref, br_ref, bi_ref, cr_ref, ci_ref, dk_ref,
             du_ref, dbr_ref, dbi_ref, dcr_ref, dci_ref, dar_ref, dai_ref, ddk_ref,
             dyseg_ref, useg_ref, duseg_ref, gr_ref, gi_ref, pr_ref, pi_ref, carry_ref, accr_ref, acci_ref):
        c = pl.program_id(1)
        ar, ai = ar_ref[...], ai_ref[...]

        @pl.when(c == 0)
        def _():
            _power_table(ar, ai, pr_ref, pi_ref, steps)
            carry_ref[...] = jnp.zeros_like(carry_ref)
            accr_ref[...] = jnp.zeros_like(accr_ref)
            acci_ref[...] = jnp.zeros_like(acci_ref)

        _to_segments(dy_ref, dyseg_ref, steps)
        _to_segments(u_ref, useg_ref, steps)
        dyb = dyseg_ref[...].astype(BF16)
        ub = useg_ref[...].astype(BF16)
        gr_ref[...] = _dot_nt(dyb, cr_ref[0])
        gi_ref[...] = -_dot_nt(dyb, ci_ref[0])
        dcr = _dot_tn(hr_ref[...].astype(BF16), dyb)
        dci = -_dot_tn(hi_ref[...].astype(BF16), dyb)
        ddk = jnp.sum(dy_ref[...] * u_ref[...], axis=0, keepdims=True)

        first = slice(0, SUBLANES)

        def scan(k4, nxt):
            for j in range(SCAN_UNROLL):
                t = steps - 1 - (k4 * SCAN_UNROLL + j)
                rows = pl.ds(pl.multiple_of(t * SUBLANES, SUBLANES), SUBLANES)
                pr, pi = _cmul_conj(pr_ref[first, :], pi_ref[first, :], nxt[0], nxt[1])
                nxt = (pr + gr_ref[rows, :], pi + gi_ref[rows, :])
                gr_ref[rows, :] = nxt[0]
                gi_ref[rows, :] = nxt[1]
            return nxt

        top = slice(chunk - SUBLANES, chunk)
        zero = jnp.zeros((SUBLANES, blk), F32)
        lax.fori_loop(0, steps // SCAN_UNROLL, scan, (zero, zero))

        gin_re, gin_im, out_re, out_im = _segment_carries(
            gr_ref[0:SUBLANES, :], gi_ref[0:SUBLANES, :], pr_ref[top, :][0:1], -pi_ref[top, :][0:1],
            carry_ref[0:1, :], carry_ref[1:2, :], reverse=True)
        carry_ref[0:1, :] = out_re
        carry_ref[1:2, :] = out_im

        def fix_row(rows, prow, hp_re, hp_im, acc):
            fr, fi = _cmul_conj(pr_ref[prow, :], pi_ref[prow, :], gin_re, gin_im)
            g_re = gr_ref[rows, :] + fr
            g_im = gi_ref[rows, :] + fi
            gr_ref[rows, :] = g_re
            gi_ref[rows, :] = g_im
            return acc[0] + g_re * hp_re + g_im * hp_im, acc[1] + g_im * hp_re - g_re * hp_im

        def fix_at(t, acc):
            aligned = (lambda r: r * SUBLANES) if isinstance(t, int) else (lambda r: pl.multiple_of(r * SUBLANES, SUBLANES))
            rows, before, prow = (pl.ds(aligned(r), SUBLANES) for r in (t, t - 1, steps - 1 - t))
            return fix_row(rows, prow, hr_ref[before, :], hi_ref[before, :], acc)

        def fix(t4, acc):
            for j in range(SCAN_UNROLL):
                acc = fix_at(t4 * SCAN_UNROLL + j, acc)
            return acc

        acc = fix_row(first, top, inr_ref[...], ini_ref[...], (accr_ref[...], acci_ref[...]))
        for t in range(1, SCAN_UNROLL):
            acc = fix_at(t, acc)
        acc_re, acc_im = lax.fori_loop(1, steps // SCAN_UNROLL, fix, acc)
        accr_ref[...] = acc_re
        acci_ref[...] = acc_im

        gbr = gr_ref[...].astype(BF16)
        gbi = gi_ref[...].astype(BF16)
        duseg_ref[...] = _dot_nt(gbr, br_ref[0]) + _dot_nt(gbi, bi_ref[0])
        _from_segments(duseg_ref, dyseg_ref, steps)
        du_ref[...] = (dyseg_ref[...] + dk_ref[...] * dy_ref[...]).astype(BF16)
        dbr = _dot_tn(ub, gbr)
        dbi = _dot_tn(ub, gbi)

        @pl.when(c == 0)
        def _():
            dbr_ref[0] = dbr
            dbi_ref[0] = dbi
            dcr_ref[0] = dcr
            dci_ref[0] = dci
            ddk_ref[...] = ddk

        @pl.when(c > 0)
        def _():
            dbr_ref[0] += dbr
            dbi_ref[0] += dbi
            dcr_ref[0] += dcr
            dci_ref[0] += dci
            ddk_ref[...] += ddk

        @pl.when(c == nc - 1)
        def _():
            dar_ref[...] = jnp.sum(acc_re, axis=0, keepdims=True)
            dai_ref[...] = jnp.sum(acc_im, axis=0, keepdims=True)

    rev = lambda c: nc - 1 - c
    row = pl.BlockSpec((1, blk), lambda j, c: (0, j))
    b_mat = pl.BlockSpec((1, LANES, blk), lambda j, c: (j, 0, 0))
    c_mat = pl.BlockSpec((1, blk, LANES), lambda j, c: (j, 0, 0))
    tok = pl.BlockSpec((chunk, LANES), lambda j, c: (rev(c), j))
    state = pl.BlockSpec((chunk, blk), lambda j, c: (rev(c), j))
    enter = pl.BlockSpec((SUBLANES, blk), lambda j, c: (rev(c), j))
    chan = pl.BlockSpec((1, LANES), lambda j, c: (0, j))
    f32 = lambda *s: jax.ShapeDtypeStruct(s, F32)
    return _hosted_call(
        body, carry, _edge_2d(N_SSM_BLOCKS, nc), name="ssm_bwd", grid=(N_SSM_BLOCKS, nc),
        in_specs=[tok, tok, state, state, enter, enter, row, row, b_mat, b_mat, c_mat, c_mat, chan],
        out_specs=[tok, b_mat, b_mat, c_mat, c_mat, row, row, chan],
        out_shape=[jax.ShapeDtypeStruct((T, SSM_W), BF16), f32(N_SSM_BLOCKS, LANES, blk), f32(N_SSM_BLOCKS, LANES, blk),
                   f32(N_SSM_BLOCKS, blk, LANES), f32(N_SSM_BLOCKS, blk, LANES), f32(1, STATES), f32(1, STATES), f32(1, SSM_W)],
        scratch_shapes=[pltpu.VMEM((chunk, LANES), F32), pltpu.VMEM((chunk, LANES), F32), pltpu.VMEM((chunk, LANES), F32),
                        pltpu.VMEM((chunk, blk), F32), pltpu.VMEM((chunk, blk), F32),
                        pltpu.VMEM((chunk, blk), F32), pltpu.VMEM((chunk, blk), F32),
                        pltpu.VMEM((SUBLANES, blk), F32), pltpu.VMEM((SUBLANES, blk), F32), pltpu.VMEM((SUBLANES, blk), F32)],
        compiler_params=_params(("arbitrary", "arbitrary"), VMEM_BIG),
        inputs=(dy, u, h_re, h_im, in_re, in_im, a_re, a_im, b_re, b_im, c_re, c_im, d_skip))


def _merge_forward(y, att, ga, gs, w_glu, w_ssm, w_attn):
    z = jax.nn.gelu(y)
    zb = z.astype(BF16)
    gl = jax.nn.sigmoid(_dot(zb, w_glu))
    z2b = (z * gl).astype(BF16)
    y_ssm = _dot(z2b, w_ssm)
    y_attn = _dot(att, w_attn)
    sa = jax.nn.sigmoid(ga)
    ss = jax.nn.sigmoid(gs)
    merged = (sa * y_attn + ss * y_ssm).astype(BF16)
    return z, zb, gl, z2b, y_ssm, y_attn, sa, ss, merged


def _merge_fwd(x, y, att, ga, gs, g2, g3, w_glu, w_ssm, w_attn, w_out, tile):
    T = x.shape[0]

    def body(x_ref, y_ref, att_ref, ga_ref, gs_ref, g2_ref, g3_ref, wg_ref, ws_ref, wa_ref, wo_ref, x1_ref, o_ref, h2_ref):
        merged = _merge_forward(y_ref[...], att_ref[...], ga_ref[...], gs_ref[...], wg_ref[...], ws_ref[...], wa_ref[...])[-1]
        o = _dot(merged, wo_ref[...])
        x1 = x_ref[...] + o * _rms_scale(o) * g2_ref[...]
        o_ref[...] = o
        x1_ref[...] = x1
        h2_ref[...] = (x1 * _rms_scale(x1) * g3_ref[...]).astype(BF16)

    tok = lambda w: pl.BlockSpec((tile, w), lambda i: (i, 0))
    vec = _const_spec((1, D_MODEL))
    return pl.pallas_call(
        body, name="merge_fwd", grid=(T // tile,),
        in_specs=[tok(D_MODEL), tok(SSM_W), tok(ATTN_W), tok(D_MODEL), tok(D_MODEL), vec, vec,
                  _const_spec((SSM_W, SSM_W)), _const_spec((SSM_W, D_MODEL)), _const_spec((ATTN_W, D_MODEL)),
                  _const_spec((D_MODEL, D_MODEL))],
        out_specs=[tok(D_MODEL), tok(D_MODEL), tok(D_MODEL)],
        out_shape=_hbm_out([jax.ShapeDtypeStruct((T, D_MODEL), F32), jax.ShapeDtypeStruct((T, D_MODEL), F32),
                            jax.ShapeDtypeStruct((T, D_MODEL), BF16)]),
        compiler_params=_params(("arbitrary",), VMEM_MID),
    )(*_in_hbm(x, y, att, ga, gs, g2, g3, w_glu, w_ssm, w_attn, w_out))


def _merge_bwd(dh2, dx2, x1, o, y, att, ga, gs, g2, g3, w_glu, w_ssm, w_attn, w_out, tile, carry=None):
    T = x1.shape[0]
    n_steps = T // tile

    group = min(2, n_steps)
    staged_widths = (D_MODEL, D_MODEL, ATTN_W, D_MODEL, SSM_W, D_MODEL, SSM_W, SSM_W)

    def body(dh2_ref, dx2_ref, x1_ref, o_ref, y_ref, att_ref, ga_ref, gs_ref, g2_ref, g3_ref, wg_ref, ws_ref, wa_ref, wo_ref,
             dx1_ref, dgates_ref, datt_ref, dy_ref, dwg_hbm, dws_hbm, dwa_hbm, dwo_hbm, dg2_ref, dg3_ref,
             awg_ref, aws_ref, awa_ref, awo_ref, *staged):
        i = pl.program_id(0)
        x1v, ov = x1_ref[...], o_ref[...]
        dxn, dg3 = _rms_bwd(dh2_ref[...], x1v, _rms_scale(x1v), g3_ref[...])
        dx1 = dx2_ref[...] + dxn
        dx1_ref[...] = dx1
        do, dg2 = _rms_bwd(dx1, ov, _rms_scale(ov), g2_ref[...])
        dob = do.astype(BF16)

        yv = y_ref[...]
        att = att_ref[...]
        z, zb, gl, z2b, y_ssm, y_attn, sa, ss, merged = _merge_forward(
            yv, att, ga_ref[...], gs_ref[...], wg_ref[...], ws_ref[...], wa_ref[...])
        dmerged = _dot_nt(dob, wo_ref[...])
        dya = (dmerged * sa).astype(BF16)
        dys = (dmerged * ss).astype(BF16)
        dgates_ref[:, :D_MODEL] = (dmerged * y_attn * sa * (1.0 - sa)).astype(BF16)
        dgates_ref[:, D_MODEL:] = (dmerged * y_ssm * ss * (1.0 - ss)).astype(BF16)
        datt_ref[...] = _dot_nt(dya, wa_ref[...]).astype(BF16)
        dz2 = _dot_nt(dys, ws_ref[...])
        dpre = (dz2 * z * gl * (1.0 - gl)).astype(BF16)
        dz = dz2 * gl + _dot_nt(dpre, wg_ref[...])
        _, gelu_vjp = jax.vjp(jax.nn.gelu, yv)
        dy_ref[...] = gelu_vjp(dz)[0]

        part = pl.ds(pl.multiple_of((i % group) * tile, tile), tile)
        for ref, val in zip(staged, (merged, dob, att, dya, z2b, dys, zb, dpre)):
            ref[part, :] = val

        @pl.when(i == 0)
        def _():
            dg2_ref[...] = dg2
            dg3_ref[...] = dg3

        @pl.when(i > 0)
        def _():
            dg2_ref[...] += dg2
            dg3_ref[...] += dg3

        def weight_grads():
            s_merged, s_dob, s_att, s_dya, s_z2b, s_dys, s_zb, s_dpre = (ref[...] for ref in staged)
            return ((awo_ref, _dot_tn(s_merged, s_dob)), (awa_ref, _dot_tn(s_att, s_dya)),
                    (aws_ref, _dot_tn(s_z2b, s_dys)), (awg_ref, _dot_tn(s_zb, s_dpre)))

        @pl.when(i == group - 1)
        def _():
            for ref, val in weight_grads():
                ref[...] = val

        @pl.when((i % group == group - 1) & (i > group - 1))
        def _():
            for ref, val in weight_grads():
                ref[...] += val

        @pl.when(i == n_steps - 1)
        def _():
            pltpu.sync_copy(awg_ref, dwg_hbm)
            pltpu.sync_copy(aws_ref, dws_hbm)
            pltpu.sync_copy(awa_ref, dwa_hbm)
            pltpu.sync_copy(awo_ref, dwo_hbm)

    tok = lambda w: pl.BlockSpec((tile, w), lambda i: (i, 0))
    vec = _const_spec((1, D_MODEL))
    any_ = pl.BlockSpec(memory_space=pl.ANY)
    vec_out = pl.BlockSpec((1, D_MODEL), lambda i: (0, 0))
    f32 = lambda *s: jax.ShapeDtypeStruct(s, F32)
    bf = lambda *s: jax.ShapeDtypeStruct(s, BF16)
    return _hosted_call(
        body, carry, _edge_1d(n_steps), name="merge_bwd", grid=(n_steps,),
        in_specs=[tok(D_MODEL), tok(D_MODEL), tok(D_MODEL), tok(D_MODEL), tok(SSM_W), tok(ATTN_W), tok(D_MODEL), tok(D_MODEL),
                  vec, vec, _const_spec((SSM_W, SSM_W)), _const_spec((SSM_W, D_MODEL)), _const_spec((ATTN_W, D_MODEL)),
                  _const_spec((D_MODEL, D_MODEL))],
        out_specs=[tok(D_MODEL), tok(2 * D_MODEL), tok(ATTN_W), tok(SSM_W), any_, any_, any_, any_, vec_out, vec_out],
        out_shape=[f32(T, D_MODEL), bf(T, 2 * D_MODEL), bf(T, ATTN_W), f32(T, SSM_W),
                   f32(SSM_W, SSM_W), f32(SSM_W, D_MODEL), f32(ATTN_W, D_MODEL), f32(D_MODEL, D_MODEL),
                   f32(1, D_MODEL), f32(1, D_MODEL)],
        scratch_shapes=[pltpu.VMEM((SSM_W, SSM_W), F32), pltpu.VMEM((SSM_W, D_MODEL), F32),
                        pltpu.VMEM((ATTN_W, D_MODEL), F32), pltpu.VMEM((D_MODEL, D_MODEL), F32)]
        + [pltpu.VMEM((group * tile, wd), BF16) for wd in staged_widths],
        compiler_params=_params(("arbitrary",), VMEM_BIG),
        inputs=(dh2, dx2, x1, o, y, att, ga, gs, g2, g3, w_glu, w_ssm, w_attn, w_out))


FF_SHARD = D_FF // N_DEV


def _mlp_fwd(h2, x1, target, g4, w_ff_in, w_ff_out, tile):
    T = h2.shape[0]
    col_chunk = 2 * FF_SHARD

    def body(h2_ref, x1_ref, tg_ref, g4_ref, wi_ref, wo_ref, a_ref, dfo_ref, dx2_ref, loss_ref, dg4_ref, rr_ref):
        i = pl.program_id(0)
        h2v = h2_ref[...]
        for c in range(D_FF // col_chunk):
            cols = slice(c * col_chunk, (c + 1) * col_chunk)
            a = _dot_nt(h2v, wi_ref[cols, :])
            a_ref[:, cols] = a.astype(BF16)
            ra = jnp.maximum(a, 0.0)
            rr_ref[:, cols] = (ra * ra).astype(BF16)
        f = _dot(rr_ref[...], wo_ref[...])
        r = _rms_scale(f)
        g = g4_ref[...]
        err = x1_ref[...] + f * r * g - tg_ref[...]
        dx2 = err * (1.0 / D_MODEL)
        dx2_ref[...] = dx2
        dfo, dg = _rms_bwd(dx2, f, r, g)
        dfo_ref[...] = dfo.astype(BF16)
        row = lax.broadcasted_iota(jnp.int32, (SUBLANES, LANES), 0)
        col = lax.broadcasted_iota(jnp.int32, (SUBLANES, LANES), 1)
        loss = jnp.where((row == 0) & (col == 0), (0.5 / D_MODEL) * jnp.sum(err * err), 0.0)

        @pl.when(i == 0)
        def _():
            loss_ref[...] = loss
            dg4_ref[...] = dg

        @pl.when(i > 0)
        def _():
            loss_ref[...] += loss
            dg4_ref[...] += dg

    tok = pl.BlockSpec((tile, D_MODEL), lambda i: (i, 0))
    return pl.pallas_call(
        body, name="mlp_fwd", grid=(T // tile,),
        in_specs=[tok, tok, tok, _const_spec((1, D_MODEL)), _const_spec((D_FF, D_MODEL)), _const_spec((D_FF, D_MODEL))],
        out_specs=[pl.BlockSpec((tile, D_FF), lambda i: (i, 0)), tok, tok,
                   pl.BlockSpec((SUBLANES, LANES), lambda i: (0, 0)), pl.BlockSpec((1, D_MODEL), lambda i: (0, 0))],
        out_shape=_hbm_out([jax.ShapeDtypeStruct((T, D_FF), BF16), jax.ShapeDtypeStruct((T, D_MODEL), BF16),
                            jax.ShapeDtypeStruct((T, D_MODEL), F32), jax.ShapeDtypeStruct((SUBLANES, LANES), F32),
                            jax.ShapeDtypeStruct((1, D_MODEL), F32)]),
        scratch_shapes=[pltpu.VMEM((tile, D_FF), BF16)],
        compiler_params=_params(("arbitrary",), VMEM_MAX),
    )(*_in_hbm(h2, x1, target, g4, w_ff_in.reshape(D_FF, D_MODEL), w_ff_out.reshape(D_FF, D_MODEL)))


def _mlp_weight_grads(dfo, a, h2, w_ff_out, row_chunk):
    T = h2.shape[0]

    def body(dfo_ref, h2_ref, a_ref, wo_ref, dwi_ref, dwo_ref, da_ref, rr_ref):
        def rows(r, _):
            sl = pl.ds(pl.multiple_of(r * row_chunk, row_chunk), row_chunk)
            ra = jnp.maximum(a_ref[sl, :].astype(F32), 0.0)
            da_ref[sl, :] = (_dot_nt(dfo_ref[sl, :], wo_ref[0]) * (2.0 * ra)).astype(BF16)
            rr_ref[sl, :] = (ra * ra).astype(BF16)
            return 0

        lax.fori_loop(0, T // row_chunk, rows, 0)
        dwo_ref[0] = _dot_tn(rr_ref[...], dfo_ref[...])
        dwi_ref[0] = _dot_tn(h2_ref[...], da_ref[...])

    return pl.pallas_call(
        body, name="mlp_weight_grads", grid=(N_DEV,),
        in_specs=[_const_spec((T, D_MODEL)), _const_spec((T, D_MODEL)), pl.BlockSpec((T, FF_SHARD), lambda k: (0, k)),
                  pl.BlockSpec((1, FF_SHARD, D_MODEL), lambda k: (k, 0, 0))],
        out_specs=[pl.BlockSpec((1, D_MODEL, FF_SHARD), lambda k: (k, 0, 0)),
                   pl.BlockSpec((1, FF_SHARD, D_MODEL), lambda k: (k, 0, 0)), pl.BlockSpec((T, FF_SHARD), lambda k: (0, k))],
        out_shape=_hbm_out([jax.ShapeDtypeStruct((N_DEV, D_MODEL, FF_SHARD), F32),
                            jax.ShapeDtypeStruct((N_DEV, FF_SHARD, D_MODEL), F32), jax.ShapeDtypeStruct((T, D_FF), BF16)]),
        scratch_shapes=[pltpu.VMEM((T, FF_SHARD), BF16)],
        compiler_params=_params(("arbitrary",), VMEM_MAX),
    )(*_in_hbm(dfo, h2, a, w_ff_out))


def _mlp_input_grad(da, w_ff_in_t, tile):
    T = da.shape[0]

    def body(da_ref, w_ref, o_ref):
        o_ref[...] = _dot(da_ref[...], w_ref[...])

    return pl.pallas_call(
        body, name="mlp_input_grad", grid=(T // tile,),
        in_specs=[pl.BlockSpec((tile, D_FF), lambda i: (i, 0)), _const_spec((D_FF, D_MODEL))],
        out_specs=pl.BlockSpec((tile, D_MODEL), lambda i: (i, 0)),
        out_shape=_hbm_out(jax.ShapeDtypeStruct((T, D_MODEL), F32)),
        compiler_params=_params(("arbitrary",), VMEM_MID),
    )(*_in_hbm(da, w_ff_in_t))


def _block_diag_in(b):
    bt = b.reshape(N_SSM_BLOCKS, GROUPS_PER_BLOCK, GROUP_CH, N_STATE)
    eye = jnp.eye(GROUPS_PER_BLOCK, dtype=b.dtype)
    return jnp.einsum("jacp,ab->jacbp", bt, eye).reshape(N_SSM_BLOCKS, LANES, SSM_LANE_BLOCK)


def _block_diag_in_grad(g):
    g = g.reshape(N_SSM_BLOCKS, GROUPS_PER_BLOCK, GROUP_CH, GROUPS_PER_BLOCK, N_STATE)
    d = jnp.diagonal(g, axis1=1, axis2=3)
    return jnp.transpose(d, (0, 3, 1, 2)).reshape(N_GROUPS, GROUP_CH, N_STATE)


def _block_diag_out(c):
    ct = c.reshape(N_SSM_BLOCKS, GROUPS_PER_BLOCK, GROUP_CH, N_STATE)
    eye = jnp.eye(GROUPS_PER_BLOCK, dtype=c.dtype)
    return jnp.einsum("jacp,ab->japbc", ct, eye).reshape(N_SSM_BLOCKS, SSM_LANE_BLOCK, LANES)


def _block_diag_out_grad(g):
    g = g.reshape(N_SSM_BLOCKS, GROUPS_PER_BLOCK, N_STATE, GROUPS_PER_BLOCK, GROUP_CH)
    d = jnp.diagonal(g, axis1=1, axis2=3)
    return jnp.transpose(d, (0, 3, 2, 1)).reshape(N_GROUPS, GROUP_CH, N_STATE)


def _tiles(T):
    return dict(proj=min(512, T), proj_bwd=min(512, T // 2), merge=min(512, T), merge_bwd=min(256, T),
                mlp_fwd=min(512, T), mlp_bwd=min(512, T), ssm_chunk=min(1024, T))


def _mesh_position():
    x, y, c = lax.axis_index("x"), lax.axis_index("y"), lax.axis_index("c")
    other_chips = [(1 - x, y), (x, 1 - y), (1 - x, 1 - y)]
    return x, y, c, other_chips


def _gather_carry(arrays):
    n = len(arrays)

    def copies(ins, outs, sems):
        send_sems, recv_sems, local_sems = sems
        x, y, c, chips = _mesh_position()
        me, sibling = (x, y, c), (x, y, 1 - c)

        def copy(a, k, block, to, src=None):
            px, py, pc = block
            dst = outs[a].at[4 * px + 2 * py + pc]
            return pltpu.make_async_remote_copy(
                src_ref=dst if src is None else src, dst_ref=dst, send_sem=send_sems.at[7 * a + k],
                recv_sem=recv_sems.at[7 * a + k], device_id=to, device_id_type=MESH_IDS)

        mine = [pltpu.make_async_copy(ins[a], outs[a].at[4 * x + 2 * y + c], local_sems.at[a]) for a in range(n)]
        first = []
        for a in range(n):
            first.append(copy(a, 0, me, sibling, src=ins[a]))
            first += [copy(a, 1 + j, me, (*chip, c), src=ins[a]) for j, chip in enumerate(chips)]
        return copy, mine, first, me, sibling, chips, c

    def start(ins, outs, sems):
        _, mine, first, *_ = copies(ins, outs, sems)
        for cp in mine + first:
            cp.start()

    def finish(ins, outs, sems):
        copy, mine, first, me, sibling, chips, c = copies(ins, outs, sems)
        passed = []
        for a in range(n):
            for j, chip in enumerate(chips):
                copy(a, 1 + j, (*chip, c), me).wait_recv()
                passed.append(copy(a, 4 + j, (*chip, c), sibling))
                passed[-1].start()
        for a in range(n):
            copy(a, 0, sibling, me).wait_recv()
            for j, chip in enumerate(chips):
                copy(a, 4 + j, (*chip, 1 - c), me).wait_recv()
        for cp in first + passed:
            cp.wait_send()
        for cp in mine:
            cp.wait()

    return _Carry(arrays, [jax.ShapeDtypeStruct((N_DEV,) + a.shape, a.dtype) for a in arrays],
                  [pltpu.SemaphoreType.DMA((7 * n,)), pltpu.SemaphoreType.DMA((7 * n,)), pltpu.SemaphoreType.DMA((n,))],
                  start, finish)


def _pairwise_carry(arrays, n_slots, make_copies):
    n = len(arrays)

    def start(ins, outs, sems):
        for cp in make_copies(ins, outs, sems):
            cp.start()

    def finish(ins, outs, sems):
        for cp in make_copies(ins, outs, sems):
            cp.wait()

    return _Carry(arrays, [jax.ShapeDtypeStruct((n_slots,) + a.shape[1:], a.dtype) for a in arrays],
                  [pltpu.SemaphoreType.DMA((n_slots * n,)), pltpu.SemaphoreType.DMA((n_slots * n,))], start, finish)


def _sibling_carry(grads):
    def make_copies(ins, outs, sems):
        x, y, c, _ = _mesh_position()
        return [pltpu.make_async_remote_copy(
            src_ref=ins[a].at[2 * ch + (1 - c)], dst_ref=outs[a].at[ch], send_sem=sems[0].at[4 * a + ch],
            recv_sem=sems[1].at[4 * a + ch], device_id=(x, y, 1 - c), device_id_type=MESH_IDS)
            for a in range(len(grads)) for ch in range(4)]

    return _pairwise_carry(grads, 4, make_copies)


def _chips_carry(sums):
    def make_copies(ins, outs, sems):
        x, y, c, chips = _mesh_position()
        return [pltpu.make_async_remote_copy(
            src_ref=ins[a].at[2 * px + py], dst_ref=outs[a].at[j], send_sem=sems[0].at[3 * a + j],
            recv_sem=sems[1].at[3 * a + j], device_id=(px, py, c), device_id_type=MESH_IDS)
            for a in range(len(sums)) for j, (px, py) in enumerate(chips)]

    return _pairwise_carry(sums, 3, make_copies)


def _add_sibling(grads8, recvs, core, row_tiles, name):
    k = len(grads8)
    g4 = [g.reshape(4, 2, *g.shape[1:]) for g in grads8]

    def body(core_ref, *refs):
        g_refs, r_refs, o_refs, ob_refs = (refs[j * k:(j + 1) * k] for j in range(4))
        for g_ref, r_ref, o_ref, ob_ref in zip(g_refs, r_refs, o_refs, ob_refs):
            s = g_ref[0] + r_ref[...]
            o_ref[...] = s
            ob_ref[...] = s.astype(BF16)

    def blocks(make):
        return [make(g.shape[1] // row_tiles, g.shape[2]) for g in grads8]

    slot = lambda tr, C: pl.BlockSpec((1, tr, C), lambda ch, r, core_ref: (ch, r, 0))
    outs = pl.pallas_call(
        body, name=name,
        grid_spec=pltpu.PrefetchScalarGridSpec(
            num_scalar_prefetch=1, grid=(4, row_tiles),
            in_specs=blocks(lambda tr, C: pl.BlockSpec((1, 1, tr, C), lambda ch, r, core_ref: (ch, core_ref[0], r, 0)))
            + blocks(slot), out_specs=blocks(slot) + blocks(slot)),
        out_shape=_hbm_out([jax.ShapeDtypeStruct((4,) + g.shape[1:], F32) for g in grads8]
                           + [jax.ShapeDtypeStruct((4,) + g.shape[1:], BF16) for g in grads8]),
        compiler_params=_params(("arbitrary", "arbitrary")),
    )(core, *_in_hbm(*g4, *recvs))
    return list(outs[:k]), list(outs[k:])


def _adam_math(w, g, m, v):
    m = ADAM_B1 * m + (1.0 - ADAM_B1) * g
    v = ADAM_B2 * v + (1.0 - ADAM_B2) * jnp.square(g)
    m_hat = m / (1.0 - ADAM_B1 ** ADAM_STEP)
    v_hat = v / (1.0 - ADAM_B2 ** ADAM_STEP)
    delta = -ADAM_LR * (m_hat / (jnp.sqrt(v_hat) + ADAM_EPS) + ADAM_WD * w)
    return delta, m, v


def _adam_big(ws, ms, vs, chip_sums, recvs, chip, row_tiles, name):
    k = len(ws)

    def body(chip_ref, *refs):
        w_refs, m_refs, v_refs, s_refs, r_refs, g_refs, d_refs, nm_refs, nv_refs = (refs[j * k:(j + 1) * k] for j in range(9))
        for a in range(k):
            r_ref = r_refs[a]
            g = s_refs[a][0] + r_ref[0].astype(F32) + r_ref[1].astype(F32) + r_ref[2].astype(F32)
            g_refs[a][...] = g
            d_refs[a][...], nm_refs[a][...], nv_refs[a][...] = _adam_math(w_refs[a][...], g, m_refs[a][...], v_refs[a][...])

    def blocks(make):
        return [make(w.shape[0] // row_tiles, w.shape[1]) for w in ws]

    blk = lambda tr, C: pl.BlockSpec((tr, C), lambda r, chip_ref: (r, 0))
    outs = pl.pallas_call(
        body, name=name,
        grid_spec=pltpu.PrefetchScalarGridSpec(
            num_scalar_prefetch=1, grid=(row_tiles,),
            in_specs=blocks(blk) * 3 + blocks(lambda tr, C: pl.BlockSpec((1, tr, C), lambda r, chip_ref: (chip_ref[0], r, 0)))
            + blocks(lambda tr, C: pl.BlockSpec((3, tr, C), lambda r, chip_ref: (0, r, 0))),
            out_specs=blocks(blk) * 4),
        out_shape=[jax.ShapeDtypeStruct(w.shape, F32) for w in ws] * 4,
        compiler_params=_params(("arbitrary",)),
    )(chip, *_in_hbm(*ws, *ms, *vs, *chip_sums, *recvs))
    return [list(outs[j * k:(j + 1) * k]) for j in range(4)]


def _sum_partials(partials, name):
    def body(p_ref, g_ref):
        g = p_ref[0]
        for d in range(1, partials.shape[0]):
            g = g + p_ref[d]
        g_ref[...] = g

    return pl.pallas_call(body, name=name, grid=(1,), in_specs=[_whole(partials.shape)], out_specs=_whole(partials.shape[1:]),
                          out_shape=jax.ShapeDtypeStruct(partials.shape[1:], F32))(*_in_hbm(partials))


def _adam_small(ws, ms, vs, gs):
    n = len(ws)

    def body(*refs):
        w_refs, m_refs, v_refs, g_refs = (refs[i * n:(i + 1) * n] for i in range(4))
        d_refs, nm_refs, nv_refs = (refs[(4 + i) * n:(5 + i) * n] for i in range(3))
        for j in range(n):
            d_refs[j][...], nm_refs[j][...], nv_refs[j][...] = _adam_math(
                w_refs[j][...], g_refs[j][...], m_refs[j][...], v_refs[j][...])

    specs = [_whole(w.shape) for w in ws]
    outs = pl.pallas_call(body, name="adam_small", grid=(1,), in_specs=specs * 4, out_specs=specs * 3,
                          out_shape=[jax.ShapeDtypeStruct(w.shape, F32) for w in ws] * 3,
                          compiler_params=_params(("arbitrary",), VMEM_MID))(*_in_hbm(*ws, *ms, *vs, *gs))
    return outs[:n], outs[n:2 * n], outs[2 * n:]


PACK_QUANTUM = SUBLANES * LANES


def _pack(named, names):
    parts = []
    for nme in names:
        flat = named[nme].reshape(-1)
        parts.append(jnp.pad(flat, (0, -flat.size % PACK_QUANTUM)))
    return jnp.concatenate(parts).reshape(-1, LANES)


def _unpack(packed, shapes, names):
    flat = packed.reshape(-1)
    out, pos = {}, 0
    for nme in names:
        size = math.prod(shapes[nme])
        out[nme] = flat[pos:pos + size].reshape(shapes[nme])
        pos += size + (-size % PACK_QUANTUM)
    return out


BIG = ("w_in", "w_glu", "w_attn_branch", "w_ssm_branch", "w_out", "w_ff_in", "w_ff_out")
COLUMN_SHARDED = ("w_in", "w_attn_branch", "w_ssm_branch", "w_ff_in")
SMALL = ("norm_mix_pre", "norm_mix_post", "norm_mlp_pre", "norm_mlp_post", "rel_bias", "sinks", "lam_re", "lam_im",
         "log_dt", "b_re", "b_im", "c_re", "c_im", "d_skip")
SWAPPED_SMALL = ("rel_bias", "b_re", "b_im")
SMALL_LATE = ("rel_bias", "sinks", "loss")
SMALL_BEFORE_ATTN_BWD = tuple(n for n in SMALL if n not in SMALL_LATE + ("norm_mix_pre",))
ALL_WEIGHTS = ("norm_mix_pre", "norm_mix_post", "norm_mlp_pre", "norm_mlp_post", "w_in", "rel_bias", "sinks", "lam_re",
               "lam_im", "log_dt", "b_re", "b_im", "c_re", "c_im", "d_skip", "w_glu", "w_attn_branch", "w_ssm_branch",
               "w_out", "w_ff_in", "w_ff_out")


def _full_from_gathered(name, gathered):
    _, r, c = gathered.shape
    if name in COLUMN_SHARDED:
        return jnp.transpose(gathered, (1, 0, 2)).reshape(r, N_DEV * c)
    return gathered.reshape(N_DEV * r, c)


def _blocks_from_full(name, full):
    r, c = full.shape
    if name in COLUMN_SHARDED:
        return jnp.transpose(full.reshape(r, N_DEV, c // N_DEV), (1, 0, 2))
    return full.reshape(N_DEV, r // N_DEV, c)


def kernel(x, norm_mix_pre, norm_mix_post, norm_mlp_pre, norm_mlp_post, w_in, rel_bias, sinks, lam_re, lam_im, log_dt, b_re, b_im, c_re, c_im, d_skip, w_glu, w_attn_branch, w_ssm_branch, w_out, w_ff_in, w_ff_out, loss_target, m_norm_mix_pre, m_norm_mix_post, m_norm_mlp_pre, m_norm_mlp_post, m_w_in, m_rel_bias, m_sinks, m_lam_re, m_lam_im, m_log_dt, m_b_re, m_b_im, m_c_re, m_c_im, m_d_skip, m_w_glu, m_w_attn_branch, m_w_ssm_branch, m_w_out, m_w_ff_in, m_w_ff_out, v_norm_mix_pre, v_norm_mix_post, v_norm_mlp_pre, v_norm_mlp_post, v_w_in, v_rel_bias, v_sinks, v_lam_re, v_lam_im, v_log_dt, v_b_re, v_b_im, v_c_re, v_c_im, v_d_skip, v_w_glu, v_w_attn_branch, v_w_ssm_branch, v_w_out, v_w_ff_in, v_w_ff_out):
    args = dict(locals())
    w = {n: args[n] for n in ALL_WEIGHTS}
    m = {n: args["m_" + n] for n in ALL_WEIGHTS}
    v = {n: args["v_" + n] for n in ALL_WEIGHTS}
    core = lax.axis_index("c").astype(jnp.int32).reshape(1)
    chip = (2 * lax.axis_index("x") + lax.axis_index("y")).astype(jnp.int32).reshape(1)
    xs, target = x[0], loss_target[0]
    t = _tiles(xs.shape[0])
    local = lambda d, n: d[n][0].T if n == "w_in" else d[n][0]
    shard = {n: local(w, n).astype(BF16) for n in BIG}
    shard["w_ff_in"] = shard["w_ff_in"].T
    view = lambda n, a: jnp.swapaxes(a, -1, -2) if n in SWAPPED_SMALL else a
    small = {n: (view(n, w[n]) if n == "rel_bias" else view(n, w[n])[0]) for n in SMALL}
    g1, g2, g3, g4 = (small[n].reshape(1, D_MODEL) for n in ("norm_mix_pre", "norm_mix_post", "norm_mlp_pre", "norm_mlp_post"))
    bucket = jnp.asarray(_bucket_table())
    rel_b, sink = small["rel_bias"], small["sinks"].reshape(1, N_HEADS)
    lam_r, lam_i = small["lam_re"].reshape(1, STATES), small["lam_im"].reshape(1, STATES)
    ldt_rep = jnp.repeat(small["log_dt"].reshape(N_GROUPS), N_STATE).reshape(1, STATES)
    bd_re, bd_im = _block_diag_in(small["b_re"]), _block_diag_in(small["b_im"])
    cm_re, cm_im = _block_diag_out(small["c_re"]).astype(BF16), _block_diag_out(small["c_im"]).astype(BF16)
    dsk = small["d_skip"].reshape(1, SSM_W)

    (g_in,) = _run_carry(_gather_carry([shard["w_in"]]), "gather_w_in")
    wf_in = g_in.reshape(IN_W, D_MODEL)
    merge_names = ("w_glu", "w_attn_branch", "w_ssm_branch", "w_out")
    (q, k, vv, u, ga, gs, h), gathered = _in_proj_fwd(xs, g1, wf_in, t["proj"], _gather_carry([shard[n] for n in merge_names]))
    wf = {n: _full_from_gathered(n, g) for n, g in zip(merge_names, gathered)}
    (att,), (wf_ff_in,) = _attn_fwd(q, k, vv, bucket, rel_b, sink, _gather_carry([shard["w_ff_in"]]))
    a_re, a_im, bm_re, bm_im = _ssm_prep(lam_r, lam_i, ldt_rep, bd_re, bd_im)
    (y, h_re, h_im, in_re, in_im), (wf_ff_out,) = _ssm_fwd(
        u, a_re, a_im, bm_re, bm_im, cm_re, cm_im, dsk, t["ssm_chunk"], _gather_carry([shard["w_ff_out"]]))
    x1, o, h2 = _merge_fwd(xs, y, att, ga, gs, g2, g3, wf["w_glu"], wf["w_ssm_branch"], wf["w_attn_branch"], wf["w_out"],
                           t["merge"])
    a, dfo, dx2, loss_blk, dg4 = _mlp_fwd(h2, x1, target, g4, wf_ff_in, wf_ff_out, t["mlp_fwd"])

    groups = {"ff": 4, "merge": 1, "w_in": 2}

    def add_sibling(group, blocks, received):
        return _add_sibling(blocks, received, core, groups[group], "add_sibling_" + group)

    ff_names = ("w_ff_in", "w_ff_out")
    dw_ff_in, dw_ff_out, da = _mlp_weight_grads(dfo, a, h2, wf_ff_out, t["mlp_bwd"])
    dh2 = _mlp_input_grad(da, wf_ff_in.reshape(D_FF, D_MODEL), t["mlp_bwd"])
    ff_blocks = [dw_ff_in, dw_ff_out]
    (dx1, dgates, datt, dy, dw_glu, dw_ssm, dw_attn, dw_out, dg2, dg3), ff_recv = _merge_bwd(
        dh2, dx2, x1, o, y, att, ga, gs, g2, g3, wf["w_glu"], wf["w_ssm_branch"], wf["w_attn_branch"], wf["w_out"],
        t["merge_bwd"], _sibling_carry(ff_blocks))
    ff_sums, ff_sums_bf = add_sibling("ff", ff_blocks, ff_recv)
    merge_blocks = [_blocks_from_full(n, g) for n, g in zip(merge_names, (dw_glu, dw_attn, dw_ssm, dw_out))]
    (du, dbm_re, dbm_im, dcm_re, dcm_im, da_re, da_im, dd_skip), carried = _ssm_bwd(
        dy, u, h_re, h_im, in_re, in_im, a_re, a_im, bm_re, bm_im, cm_re, cm_im, dsk, t["ssm_chunk"],
        _join(_chips_carry(ff_sums_bf), _sibling_carry(merge_blocks)))
    ff_from_chips, merge_recv = carried[:2], carried[2:]
    merge_sums, merge_sums_bf = add_sibling("merge", merge_blocks, merge_recv)
    dbd_re, dbd_im, dlam_re, dlam_im, dldt_rep = _ssm_prep_bwd(lam_r, lam_i, ldt_rep, bd_re, bd_im, dbm_re, dbm_im, da_re, da_im)
    dlog_dt = _group_sum(dldt_rep.reshape(N_GROUPS, N_STATE))
    shapes = {n: view(n, w[n]).shape for n in SMALL}
    shapes["loss"] = (1,)
    small_grads = dict(
        norm_mix_post=dg2, norm_mlp_pre=dg3, norm_mlp_post=dg4, lam_re=dlam_re, lam_im=dlam_im, log_dt=dlog_dt,
        b_re=_block_diag_in_grad(dbd_re), b_im=_block_diag_in_grad(dbd_im),
        c_re=_block_diag_out_grad(dcm_re), c_im=_block_diag_out_grad(dcm_im), d_skip=dd_skip)
    packed_early = _pack({n: small_grads[n].reshape(shapes[n]) for n in SMALL_BEFORE_ATTN_BWD}, SMALL_BEFORE_ATTN_BWD)
    (dq, dkv, attn_small), carried = _attn_bwd(
        q, k, vv, datt, bucket, rel_b, sink, _join(_chips_carry(merge_sums_bf), _gather_carry([packed_early])))
    merge_from_chips, partials_early = carried[:-1], carried[-1]

    dparts = (dq, dkv, du, dgates)
    dw_in_t = _in_proj_weight_grad(h, dparts)
    in_blocks = [dw_in_t.reshape(N_DEV, IN_W // N_DEV, D_MODEL)]
    n_tiles = xs.shape[0] // t["proj_bwd"]
    first_part = max(1, (3 * n_tiles) // 8)
    (gx_a, dg1_a), in_recv = _in_proj_input_grad(
        xs, g1, wf_in, dx1, dparts, t["proj_bwd"], 0, first_part, "in_proj_input_grad_a", _sibling_carry(in_blocks))
    in_sums, in_sums_bf = add_sibling("w_in", in_blocks, in_recv)
    late = dict(rel_bias=attn_small[:, :N_BUCKETS, 0], sinks=attn_small[:, N_BUCKETS, 0], loss=loss_blk[0:1, 0])
    packed_late = _pack({n: late[n].reshape(shapes[n]) for n in SMALL_LATE}, SMALL_LATE)
    (gx_b, dg1_b), (in_from_chips, partials_late) = _in_proj_input_grad(
        xs, g1, wf_in, dx1, dparts, t["proj_bwd"], first_part, n_tiles - first_part, "in_proj_input_grad_b",
        _join(_chips_carry(in_sums_bf), _gather_carry([packed_late])))
    grad_x = jnp.concatenate([gx_a, gx_b], axis=0)
    (dg1_partials,) = _run_carry(_gather_carry([jnp.concatenate([dg1_a, dg1_b], axis=0)]), "gather_norm_grad")

    grads, deltas, new_m, new_v = {}, {}, {}, {}
    for group, names, sums, received in (("ff", ff_names, ff_sums, ff_from_chips), ("merge", merge_names, merge_sums, merge_from_chips),
                                         ("w_in", ("w_in",), in_sums, [in_from_chips])):
        outs = _adam_big(*[[local(d, n) for n in names] for d in (w, m, v)], sums, received, chip, groups[group], "adam_" + group)
        for store, vals in zip((grads, deltas, new_m, new_v), outs):
            store.update({n: (o.T if n == "w_in" else o)[None] for n, o in zip(names, vals)})

    grads.update(_unpack(_sum_partials(partials_early, "sum_small_grads"), shapes, SMALL_BEFORE_ATTN_BWD))
    grads.update(_unpack(_sum_partials(partials_late, "sum_late_grads"), shapes, SMALL_LATE))
    grads["norm_mix_pre"] = _sum_partials(dg1_partials.reshape(2 * N_DEV, 1, D_MODEL), "sum_norm_grad")
    loss = grads.pop("loss").reshape(())
    small_out = _adam_small(*[[view(n, d[n]) for n in SMALL] for d in (w, m, v)], [grads[n] for n in SMALL])
    for store, vals in zip((deltas, new_m, new_v), small_out):
        store.update(zip(SMALL, vals))
    for store in (grads, deltas, new_m, new_v):
        store.update({n: view(n, store[n]) for n in SWAPPED_SMALL})

    return (loss, grad_x[None], *[grads[n] for n in ALL_WEIGHTS], *[deltas[n] for n in ALL_WEIGHTS],
            *[new_m[n] for n in ALL_WEIGHTS], *[new_v[n] for n in ALL_WEIGHTS])
```

```python
import math

import jax
import jax.numpy as jnp
import numpy as np
from jax import lax
from jax.experimental import pallas as pl
from jax.experimental.pallas import tpu as pltpu

F32 = jnp.float32
BF16 = jnp.bfloat16

D_MODEL = 1024
N_HEADS = 8
HEAD_DIM = 64
ATTN_W = 512
KV_W = 128
BLOCK = 128
N_BUCKETS = 32
SSM_W = 512
N_GROUPS = 32
N_STATE = 64
GROUP_CH = 16
STATES = N_GROUPS * N_STATE
D_FF = 4096
IN_W = 3328
SPLITS = (0, 512, 640, 768, 1280, 2304, 3328)
RMS_EPS = 1e-6
NEG_INF = -1e30
SUBLANES = 8
LANES = 128
SSM_LANE_BLOCK = 512
N_SSM_BLOCKS = STATES // SSM_LANE_BLOCK
GROUPS_PER_BLOCK = SSM_LANE_BLOCK // N_STATE
VMEM_BIG = 52 * 1024 * 1024
VMEM_MID = 40 * 1024 * 1024
VMEM_MAX = 60 * 1024 * 1024

ADAM_LR = 0.001
ADAM_B1 = 0.9
ADAM_B2 = 0.999
ADAM_EPS = 1e-08
ADAM_WD = 0.01
ADAM_STEP = 10

N_DEV = 8


def _dot(a, b):
    return jnp.dot(a, b, preferred_element_type=F32)


def _dot_nt(a, b):
    return lax.dot_general(a, b, (((1,), (1,)), ((), ())), preferred_element_type=F32)


def _dot_tn(a, b):
    return lax.dot_general(a, b, (((0,), (0,)), ((), ())), preferred_element_type=F32)


def _rms_scale(x):
    return lax.rsqrt(jnp.mean(x * x, axis=-1, keepdims=True) + RMS_EPS)


def _rms_bwd(dy, x, r, g):
    t = dy * g
    dx = r * t - x * (r * r * r) * jnp.mean(t * x, axis=-1, keepdims=True)
    dg = jnp.sum(dy * x * r, axis=0, keepdims=True)
    return dx, dg


def _const_spec(shape):
    nd = len(shape)
    return pl.BlockSpec(shape, lambda *_: (0,) * nd, pipeline_mode=pl.Buffered(1))


def _in_hbm(*arrays):
    return tuple(pltpu.with_memory_space_constraint(a, pltpu.HBM) for a in arrays)


def _hbm_out(shapes):
    if isinstance(shapes, (list, tuple)):
        return [_hbm_out(s) for s in shapes]
    return shapes if isinstance(shapes, pl.MemoryRef) else pltpu.HBM(shapes.shape, shapes.dtype)


def _whole(shape):
    nd = len(shape)
    return pl.BlockSpec(shape, lambda *_: (0,) * nd)


def _params(sem, vmem=None):
    return pltpu.CompilerParams(dimension_semantics=sem, vmem_limit_bytes=vmem)


MESH_IDS = pl.DeviceIdType.MESH
HBM_SPEC = pl.BlockSpec(memory_space=pl.ANY)


class _Carry:
    def __init__(self, inputs, out_shapes, sems, start, finish):
        self.inputs, self.out_shapes, self.sems, self.start, self.finish = list(inputs), list(out_shapes), list(sems), start, finish


def _join(a, b):
    na_in, na_out, na_sem = len(a.inputs), len(a.out_shapes), len(a.sems)

    def start(ins, outs, sems):
        a.start(ins[:na_in], outs[:na_out], sems[:na_sem])
        b.start(ins[na_in:], outs[na_out:], sems[na_sem:])

    def finish(ins, outs, sems):
        a.finish(ins[:na_in], outs[:na_out], sems[:na_sem])
        b.finish(ins[na_in:], outs[na_out:], sems[na_sem:])

    return _Carry(a.inputs + b.inputs, a.out_shapes + b.out_shapes, a.sems + b.sems, start, finish)


def _hosted_call(body, carry, edge, *, name, grid, in_specs, out_specs, out_shape, scratch_shapes, compiler_params, inputs):
    n_in, n_out = len(in_specs), len(out_specs)
    inputs = [a if s.memory_space == pltpu.SMEM else _in_hbm(a)[0] for a, s in zip(inputs, in_specs)]
    out_shape = _hbm_out(list(out_shape))
    if carry is None:
        outs = pl.pallas_call(body, name=name, grid=grid, in_specs=in_specs, out_specs=out_specs, out_shape=out_shape,
                              scratch_shapes=scratch_shapes, compiler_params=compiler_params)(*inputs)
        return list(outs), []
    c_in, c_out, c_sem = len(carry.inputs), len(carry.out_shapes), len(carry.sems)

    def wrapped(*refs):
        ins, refs = refs[:n_in], refs[n_in:]
        cins, refs = refs[:c_in], refs[c_in:]
        outs, refs = refs[:n_out], refs[n_out:]
        couts, refs = refs[:c_out], refs[c_out:]
        scratch, csems = refs[:len(refs) - c_sem], refs[len(refs) - c_sem:]
        first, last = edge()

        @pl.when(first)
        def _():
            carry.start(cins, couts, csems)

        body(*ins, *outs, *scratch)

        @pl.when(last)
        def _():
            carry.finish(cins, couts, csems)

    outs = pl.pallas_call(
        wrapped, name=name, grid=grid, in_specs=list(in_specs) + [HBM_SPEC] * c_in,
        out_specs=list(out_specs) + [HBM_SPEC] * c_out, out_shape=out_shape + _hbm_out(carry.out_shapes),
        scratch_shapes=list(scratch_shapes) + carry.sems, compiler_params=compiler_params)(*inputs, *_in_hbm(*carry.inputs))
    return list(outs[:n_out]), list(outs[n_out:])


def _edge_1d(n_steps):
    return lambda: (pl.program_id(0) == 0, pl.program_id(0) == n_steps - 1)


def _edge_2d(n0, n1):
    return lambda: ((pl.program_id(0) == 0) & (pl.program_id(1) == 0),
                    (pl.program_id(0) == n0 - 1) & (pl.program_id(1) == n1 - 1))


def _run_carry(carry, name):
    c_in, c_out = len(carry.inputs), len(carry.out_shapes)

    def body(*refs):
        ins, outs, sems = refs[:c_in], refs[c_in:c_in + c_out], refs[c_in + c_out:]
        carry.start(ins, outs, sems)
        carry.finish(ins, outs, sems)

    return pl.pallas_call(body, name=name, in_specs=[HBM_SPEC] * c_in, out_specs=[HBM_SPEC] * c_out,
                          out_shape=_hbm_out(carry.out_shapes), scratch_shapes=carry.sems)(*_in_hbm(*carry.inputs))


def _in_proj_fwd(x, g1, w_in_t, tile, carry=None):
    T = x.shape[0]

    def body(x_ref, g_ref, w_ref, q_ref, k_ref, v_ref, u_ref, ga_ref, gs_ref, h_ref):
        xv = x_ref[...]
        h = (xv * _rms_scale(xv) * g_ref[...]).astype(BF16)
        h_ref[...] = h
        outs = (q_ref, k_ref, v_ref, u_ref, ga_ref, gs_ref)
        for p, o_ref in enumerate(outs):
            o_ref[...] = _dot_nt(h, w_ref[SPLITS[p]:SPLITS[p + 1], :]).astype(o_ref.dtype)

    widths = [SPLITS[p + 1] - SPLITS[p] for p in range(6)] + [D_MODEL]
    dtypes = [BF16, BF16, BF16, F32, F32, F32, BF16]
    return _hosted_call(
        body, carry, _edge_1d(T // tile), name="in_proj_fwd", grid=(T // tile,),
        in_specs=[pl.BlockSpec((tile, D_MODEL), lambda i: (i, 0)), _const_spec((1, D_MODEL)), _const_spec((IN_W, D_MODEL))],
        out_specs=[pl.BlockSpec((tile, w), lambda i: (i, 0)) for w in widths],
        out_shape=[jax.ShapeDtypeStruct((T, w), dt) for w, dt in zip(widths, dtypes)],
        scratch_shapes=[], compiler_params=_params(("arbitrary",), VMEM_MID), inputs=(x, g1, w_in_t))


PROJ_PARTS = (512, 256, 512, 2048)
PROJ_GRAD_BLOCK = 256


def _in_proj_weight_grad(h, dparts):
    T = h.shape[0]
    blocks = [wd // PROJ_GRAD_BLOCK for wd in PROJ_PARTS]
    starts = [sum(blocks[:p]) for p in range(len(blocks))]

    def body(h_ref, *refs):
        part_refs, o_ref = refs[:-1], refs[-1]
        j = pl.program_id(0)
        for p_ref, start, count in zip(part_refs, starts, blocks):
            @pl.when((j >= start) & (j < start + count))
            def _(p_ref=p_ref):
                o_ref[...] = _dot_tn(p_ref[...], h_ref[...])

    def part_spec(start, count):
        return pl.BlockSpec((T, PROJ_GRAD_BLOCK), lambda j: (0, jnp.clip(j - start, 0, count - 1)))

    return pl.pallas_call(
        body, name="in_proj_weight_grad", grid=(sum(blocks),),
        in_specs=[_const_spec((T, D_MODEL))] + [part_spec(s, c) for s, c in zip(starts, blocks)],
        out_specs=pl.BlockSpec((PROJ_GRAD_BLOCK, D_MODEL), lambda j: (j, 0)),
        out_shape=_hbm_out(jax.ShapeDtypeStruct((IN_W, D_MODEL), F32)),
        compiler_params=_params(("arbitrary",), VMEM_MID),
    )(*_in_hbm(h, *dparts))


def _in_proj_input_grad(x, g1, w_in_t, dx1, dparts, tile, first_tile, n_tiles, name, carry=None):
    offsets = [sum(PROJ_PARTS[:p]) for p in range(len(PROJ_PARTS))]

    def body(x_ref, g_ref, w_ref, dx1_ref, *refs):
        part_refs, (gx_ref, dg_ref) = refs[:len(PROJ_PARTS)], refs[len(PROJ_PARTS):]
        i = pl.program_id(0)
        xv = x_ref[...]
        r = _rms_scale(xv)
        g = g_ref[...]
        dh = sum(_dot(p_ref[...], w_ref[off:off + wd, :]) for p_ref, off, wd in zip(part_refs, offsets, PROJ_PARTS))
        dxn, dg = _rms_bwd(dh, xv, r, g)
        gx_ref[...] = dx1_ref[...] + dxn

        @pl.when(i == 0)
        def _():
            dg_ref[...] = dg

        @pl.when(i > 0)
        def _():
            dg_ref[...] += dg

    tok = lambda wd: pl.BlockSpec((tile, wd), lambda i: (i + first_tile, 0))
    return _hosted_call(
        body, carry, _edge_1d(n_tiles), name=name, grid=(n_tiles,),
        in_specs=[tok(D_MODEL), _const_spec((1, D_MODEL)), _const_spec((IN_W, D_MODEL)), tok(D_MODEL)] + [tok(wd) for wd in PROJ_PARTS],
        out_specs=[pl.BlockSpec((tile, D_MODEL), lambda i: (i, 0)), pl.BlockSpec((1, D_MODEL), lambda i: (0, 0))],
        out_shape=[jax.ShapeDtypeStruct((n_tiles * tile, D_MODEL), F32), jax.ShapeDtypeStruct((1, D_MODEL), F32)],
        scratch_shapes=[], compiler_params=_params(("arbitrary",), VMEM_MID), inputs=(x, g1, w_in_t, dx1, *dparts))


def _bucket_table():
    qi = np.arange(BLOCK)[:, None]
    kj = np.arange(2 * BLOCK)[None, :]
    dist = qi + BLOCK - kj
    max_exact = N_BUCKETS // 2
    d = np.maximum(dist, 0)
    df = np.maximum(d, 1).astype(np.float32)
    large = max_exact + (np.log(df / np.float32(max_exact)) / np.float32(math.log(BLOCK / max_exact))
                         * np.float32(N_BUCKETS - max_exact)).astype(np.int32)
    large = np.minimum(large, N_BUCKETS - 1)
    bucket = np.where(d < max_exact, d, large)
    return np.where((dist >= 0) & (dist < BLOCK), bucket, -1).astype(np.int32)


def _build_bias(bucket_ref, rb_ref, bias_ref):
    bk = bucket_ref[...]
    for h in range(N_HEADS):
        def add(b, acc, h=h):
            return acc + jnp.where(bk == b, rb_ref[h, b], 0.0)
        bias_ref[h] = lax.fori_loop(0, N_BUCKETS, add, jnp.zeros((BLOCK, 2 * BLOCK), F32))


def _kv_variants(prev_ref, cur_ref):
    cat = jnp.concatenate([prev_ref[...], cur_ref[...]], axis=0)
    lo = lax.broadcasted_iota(jnp.int32, cat.shape, 1) < HEAD_DIM
    zero = jnp.zeros_like(cat)
    head0_lo = jnp.where(lo, cat, zero)
    head1_hi = jnp.where(lo, zero, cat)
    return ((head0_lo, pltpu.roll(head0_lo, HEAD_DIM, 1)), (pltpu.roll(head1_hi, HEAD_DIM, 1), head1_hi))


def _merge_kv_grads(g):
    lo = lax.broadcasted_iota(jnp.int32, g[0][0].shape, 1) < HEAD_DIM
    return jnp.where(lo, g[0][0] + pltpu.roll(g[0][1], HEAD_DIM, 1), g[1][1] + pltpu.roll(g[1][0], HEAD_DIM, 1))


def _head_lanes(h):
    return slice((h // 2) * LANES, (h // 2 + 1) * LANES)


def _attn_probs(q_ref, kvar, bias_ref, sk_ref, valid, s_ref):
    for h in range(N_HEADS):
        s_ref[h] = _dot_nt(q_ref[:, _head_lanes(h)], kvar[h // 4][h % 2])
    head = lax.broadcasted_iota(jnp.int32, (N_HEADS, 1, 1), 0)
    sink = jnp.zeros((N_HEADS, 1, 1), F32)
    for h in range(N_HEADS):
        sink = jnp.where(head == h, sk_ref[0, h], sink)
    s = jnp.where(valid[None], s_ref[...] * (HEAD_DIM ** -0.5) + bias_ref[...], NEG_INF)
    m = jnp.maximum(jnp.max(s, axis=-1, keepdims=True), sink)
    p = jnp.exp(s - m)
    e_sink = jnp.exp(sink - m)
    inv = 1.0 / (jnp.sum(p, axis=-1, keepdims=True) + e_sink)
    return p * inv, e_sink * inv


def _attn_valid(bucket_ref, n):
    col = lax.broadcasted_iota(jnp.int32, (BLOCK, 2 * BLOCK), 1)
    return (bucket_ref[...] >= 0) & ((n > 0) | (col >= BLOCK))


def _attn_fwd(q, k, v, bucket, rel_bias, sinks, carry=None):
    T = q.shape[0]
    nb = T // BLOCK

    def body(q_ref, kc_ref, kp_ref, vc_ref, vp_ref, bucket_ref, rb_ref, sk_ref, o_ref, bias_ref, s_ref, p_ref):
        n = pl.program_id(0)

        @pl.when(n == 0)
        def _():
            _build_bias(bucket_ref, rb_ref, bias_ref)

        kvar = _kv_variants(kp_ref, kc_ref)
        vvar = _kv_variants(vp_ref, vc_ref)
        pr, _ = _attn_probs(q_ref, kvar, bias_ref, sk_ref, _attn_valid(bucket_ref, n), s_ref)
        p_ref[...] = pr.astype(BF16)
        for m in range(N_HEADS // 2):
            acc = _dot(p_ref[2 * m], vvar[m // 2][0]) + _dot(p_ref[2 * m + 1], vvar[m // 2][1])
            o_ref[:, m * LANES:(m + 1) * LANES] = acc.astype(o_ref.dtype)

    cur = lambda w: pl.BlockSpec((BLOCK, w), lambda n: (n, 0))
    prev = lambda w: pl.BlockSpec((BLOCK, w), lambda n: (jnp.maximum(n - 1, 0), 0))
    smem = pl.BlockSpec(memory_space=pltpu.SMEM)
    return _hosted_call(
        body, carry, _edge_1d(nb), name="attn_fwd", grid=(nb,),
        in_specs=[cur(ATTN_W), cur(KV_W), prev(KV_W), cur(KV_W), prev(KV_W), _const_spec((BLOCK, 2 * BLOCK)), smem, smem],
        out_specs=[cur(ATTN_W)],
        out_shape=[jax.ShapeDtypeStruct((T, ATTN_W), BF16)],
        scratch_shapes=[pltpu.VMEM((N_HEADS, BLOCK, 2 * BLOCK), F32), pltpu.VMEM((N_HEADS, BLOCK, 2 * BLOCK), F32),
                        pltpu.VMEM((N_HEADS, BLOCK, 2 * BLOCK), BF16)],
        compiler_params=_params(("arbitrary",)), inputs=(q, k, k, v, v, bucket, rel_bias, sinks))


ATTN_SMALL_ROWS = N_BUCKETS + SUBLANES


def _attn_bwd(q, k, v, datt, bucket, rel_bias, sinks, carry=None):
    T = q.shape[0]
    nb = T // BLOCK

    def body(q_ref, do_ref, kc_ref, kp_ref, vc_ref, vp_ref, bucket_ref, rb_ref, sk_ref,
             dq_ref, dkv_ref, small_ref, bias_ref, ds_sum_ref, dsink_ref, kcarry_ref, vcarry_ref,
             s_ref, dp_ref, p_ref, dsc_ref):
        n = pl.program_id(0)

        @pl.when(n == 0)
        def _():
            _build_bias(bucket_ref, rb_ref, bias_ref)
            ds_sum_ref[...] = jnp.zeros_like(ds_sum_ref)
            dsink_ref[...] = jnp.zeros_like(dsink_ref)
            kcarry_ref[...] = jnp.zeros_like(kcarry_ref)
            vcarry_ref[...] = jnp.zeros_like(vcarry_ref)

        @pl.when(n < nb)
        def _():
            kvar = _kv_variants(kp_ref, kc_ref)
            vvar = _kv_variants(vp_ref, vc_ref)
            pr, p_sink = _attn_probs(q_ref, kvar, bias_ref, sk_ref, _attn_valid(bucket_ref, n), s_ref)
            for h in range(N_HEADS):
                dp_ref[h] = _dot_nt(do_ref[:, _head_lanes(h)], vvar[h // 4][h % 2])
            dp = dp_ref[...]
            dsum = jnp.sum(pr * dp, axis=-1, keepdims=True)
            ds = pr * (dp - dsum)
            ds_sum_ref[...] += ds
            dsink_ref[...] -= jnp.sum(p_sink * dsum, axis=1, keepdims=True)
            dsc_ref[...] = (ds * (HEAD_DIM ** -0.5)).astype(BF16)
            p_ref[...] = pr.astype(BF16)
            for m in range(N_HEADS // 2):
                dqm = _dot(dsc_ref[2 * m], kvar[m // 2][0]) + _dot(dsc_ref[2 * m + 1], kvar[m // 2][1])
                dq_ref[:, m * LANES:(m + 1) * LANES] = dqm.astype(dq_ref.dtype)
            dk_var = [[None, None], [None, None]]
            dv_var = [[None, None], [None, None]]
            for kvh in range(2):
                for e in range(2):
                    heads = [h for h in range(N_HEADS) if h // 4 == kvh and h % 2 == e]
                    dk_var[kvh][e] = sum(_dot_tn(dsc_ref[h], q_ref[:, _head_lanes(h)]) for h in heads)
                    dv_var[kvh][e] = sum(_dot_tn(p_ref[h], do_ref[:, _head_lanes(h)]) for h in heads)
            dk_cat = _merge_kv_grads(dk_var)
            dv_cat = _merge_kv_grads(dv_var)

            @pl.when(n > 0)
            def _():
                dkv_ref[:, :KV_W] = (kcarry_ref[...] + dk_cat[:BLOCK]).astype(BF16)
                dkv_ref[:, KV_W:] = (vcarry_ref[...] + dv_cat[:BLOCK]).astype(BF16)

            kcarry_ref[...] = dk_cat[BLOCK:]
            vcarry_ref[...] = dv_cat[BLOCK:]

        @pl.when(n == nb)
        def _():
            dkv_ref[:, :KV_W] = kcarry_ref[...].astype(BF16)
            dkv_ref[:, KV_W:] = vcarry_ref[...].astype(BF16)
            bk = bucket_ref[...]
            row = lax.broadcasted_iota(jnp.int32, (N_HEADS, ATTN_SMALL_ROWS, LANES), 1)

            def add(b, acc):
                masked = jnp.where((bk == b)[None], ds_sum_ref[...], 0.0)
                val = jnp.sum(jnp.sum(masked, axis=1, keepdims=True), axis=2, keepdims=True)
                return acc + jnp.where(row == b, val, 0.0)

            small_ref[...] = lax.fori_loop(0, N_BUCKETS, add, jnp.where(row == N_BUCKETS, dsink_ref[...], 0.0))

    last = nb - 1
    cur = lambda w: pl.BlockSpec((BLOCK, w), lambda n: (jnp.minimum(n, last), 0))
    prev = lambda w: pl.BlockSpec((BLOCK, w), lambda n: (jnp.clip(n - 1, 0, last), 0))
    smem = pl.BlockSpec(memory_space=pltpu.SMEM)
    return _hosted_call(
        body, carry, _edge_1d(nb + 1), name="attn_bwd", grid=(nb + 1,),
        in_specs=[cur(ATTN_W), cur(ATTN_W), cur(KV_W), prev(KV_W), cur(KV_W), prev(KV_W),
                  _const_spec((BLOCK, 2 * BLOCK)), smem, smem],
        out_specs=[cur(ATTN_W), prev(2 * KV_W), pl.BlockSpec((N_HEADS, ATTN_SMALL_ROWS, LANES), lambda n: (0, 0, 0))],
        out_shape=[jax.ShapeDtypeStruct((T, ATTN_W), BF16), jax.ShapeDtypeStruct((T, 2 * KV_W), BF16),
                   jax.ShapeDtypeStruct((N_HEADS, ATTN_SMALL_ROWS, LANES), F32)],
        scratch_shapes=[pltpu.VMEM((N_HEADS, BLOCK, 2 * BLOCK), F32), pltpu.VMEM((N_HEADS, BLOCK, 2 * BLOCK), F32),
                        pltpu.VMEM((N_HEADS, 1, 1), F32), pltpu.VMEM((BLOCK, KV_W), F32), pltpu.VMEM((BLOCK, KV_W), F32),
                        pltpu.VMEM((N_HEADS, BLOCK, 2 * BLOCK), F32), pltpu.VMEM((N_HEADS, BLOCK, 2 * BLOCK), F32),
                        pltpu.VMEM((N_HEADS, BLOCK, 2 * BLOCK), BF16), pltpu.VMEM((N_HEADS, BLOCK, 2 * BLOCK), BF16)],
        compiler_params=_params(("arbitrary",)), inputs=(q, datt, k, k, v, v, bucket, rel_bias, sinks))


SCAN_UNROLL = 4


def _cmul(ar, ai, br, bi):
    return ar * br - ai * bi, ar * bi + ai * br


def _cmul_conj(ar, ai, br, bi):
    return ar * br + ai * bi, ar * bi - ai * br


def _ssm_discretize(lr, li, ldt):
    dt = jnp.exp(ldt)
    mag = jnp.exp(lr * dt)
    ab_re = mag * jnp.cos(li * dt)
    ab_im = mag * jnp.sin(li * dt)
    nr = ab_re - 1.0
    den = lr * lr + li * li
    f_re = (nr * lr + ab_im * li) / den
    f_im = (ab_im * lr - nr * li) / den
    return ab_re, ab_im, f_re, f_im


def _ssm_prep(lam_re, lam_im, ldt_rep, bd_re, bd_im):
    def body(lr_ref, li_ref, ldt_ref, bdr_ref, bdi_ref, ar_ref, ai_ref, br_ref, bi_ref):
        ab_re, ab_im, f_re, f_im = _ssm_discretize(lr_ref[...], li_ref[...], ldt_ref[...])
        ar_ref[...] = ab_re
        ai_ref[...] = ab_im
        bdr, bdi = bdr_ref[0], bdi_ref[0]
        br_ref[0] = (bdr * f_re - bdi * f_im).astype(BF16)
        bi_ref[0] = (bdi * f_re + bdr * f_im).astype(BF16)

    row = pl.BlockSpec((1, SSM_LANE_BLOCK), lambda j: (0, j))
    mat = pl.BlockSpec((1, LANES, SSM_LANE_BLOCK), lambda j: (j, 0, 0))
    return pl.pallas_call(
        body, name="ssm_prep", grid=(N_SSM_BLOCKS,),
        in_specs=[row, row, row, mat, mat], out_specs=[row, row, mat, mat],
        out_shape=[jax.ShapeDtypeStruct((1, STATES), F32)] * 2 + [jax.ShapeDtypeStruct((N_SSM_BLOCKS, LANES, SSM_LANE_BLOCK), BF16)] * 2,
        compiler_params=_params(("arbitrary",)),
    )(*_in_hbm(lam_re, lam_im, ldt_rep, bd_re, bd_im))


def _ssm_prep_bwd(lam_re, lam_im, ldt_rep, bd_re, bd_im, dbr, dbi, da_re, da_im):
    def body(lr_ref, li_ref, ldt_ref, bdr_ref, bdi_ref, dbr_ref, dbi_ref, dar_ref, dai_ref,
             dbdr_ref, dbdi_ref, dlr_ref, dli_ref, dldt_ref):
        lr, li, ldt = lr_ref[...], li_ref[...], ldt_ref[...]
        (_, _, f_re, f_im), vjp = jax.vjp(_ssm_discretize, lr, li, ldt)
        bdr, bdi, gbr, gbi = bdr_ref[0], bdi_ref[0], dbr_ref[0], dbi_ref[0]
        dbdr_ref[0] = gbr * f_re + gbi * f_im
        dbdi_ref[0] = gbi * f_re - gbr * f_im
        df_re = jnp.sum(gbr * bdr + gbi * bdi, axis=0, keepdims=True)
        df_im = jnp.sum(gbi * bdr - gbr * bdi, axis=0, keepdims=True)
        dlr, dli, dldt = vjp((dar_ref[...], dai_ref[...], df_re, df_im))
        dlr_ref[...] = dlr
        dli_ref[...] = dli
        dldt_ref[...] = dldt

    row = pl.BlockSpec((1, SSM_LANE_BLOCK), lambda j: (0, j))
    mat = pl.BlockSpec((1, LANES, SSM_LANE_BLOCK), lambda j: (j, 0, 0))
    mat_shape = jax.ShapeDtypeStruct((N_SSM_BLOCKS, LANES, SSM_LANE_BLOCK), F32)
    row_shape = jax.ShapeDtypeStruct((1, STATES), F32)
    return pl.pallas_call(
        body, name="ssm_prep_bwd", grid=(N_SSM_BLOCKS,),
        in_specs=[row, row, row, mat, mat, mat, mat, row, row], out_specs=[mat, mat, row, row, row],
        out_shape=[mat_shape, mat_shape, row_shape, row_shape, row_shape],
        compiler_params=_params(("arbitrary",)),
    )(*_in_hbm(lam_re, lam_im, ldt_rep, bd_re, bd_im, dbr, dbi, da_re, da_im))


def _group_sum(x):
    def body(x_ref, o_ref):
        o_ref[...] = jnp.sum(x_ref[...], axis=1, keepdims=True)
    return pl.pallas_call(body, name="ssm_group_sum", grid=(1,), in_specs=[_whole(x.shape)], out_specs=_whole((N_GROUPS, 1)),
                          out_shape=jax.ShapeDtypeStruct((N_GROUPS, 1), F32))(*_in_hbm(x))


def _power_table(ar, ai, p_re_ref, p_im_ref, steps):
    shape = (SUBLANES, SSM_LANE_BLOCK)
    p_re_ref[0:SUBLANES] = jnp.broadcast_to(ar, shape)
    p_im_ref[0:SUBLANES] = jnp.broadcast_to(ai, shape)
    m = 1
    while m < steps:
        rows = m * SUBLANES
        top_re = p_re_ref[rows - SUBLANES:rows]
        top_im = p_im_ref[rows - SUBLANES:rows]
        cur_re = p_re_ref[0:rows].reshape(m, SUBLANES, SSM_LANE_BLOCK)
        cur_im = p_im_ref[0:rows].reshape(m, SUBLANES, SSM_LANE_BLOCK)
        nxt_re, nxt_im = _cmul(cur_re, cur_im, top_re[None], top_im[None])
        p_re_ref[rows:2 * rows] = nxt_re.reshape(rows, SSM_LANE_BLOCK)
        p_im_ref[rows:2 * rows] = nxt_im.reshape(rows, SSM_LANE_BLOCK)
        m *= 2


def _to_segments(src_ref, dst_ref, steps):
    for s in range(SUBLANES):
        dst_ref[pl.ds(s, steps, stride=SUBLANES), :] = src_ref[s * steps:(s + 1) * steps, :]


def _from_segments(src_ref, dst_ref, steps):
    for s in range(SUBLANES):
        dst_ref[s * steps:(s + 1) * steps, :] = src_ref[pl.ds(s, steps, stride=SUBLANES), :]


def _segment_carries(e_re, e_im, an_re, an_im, c_re, c_im, reverse):
    order = range(SUBLANES - 1, -1, -1) if reverse else range(SUBLANES)
    ins_re, ins_im = [None] * SUBLANES, [None] * SUBLANES
    for s in order:
        ins_re[s], ins_im[s] = c_re, c_im
        pr, pi = _cmul(an_re, an_im, c_re, c_im)
        c_re = e_re[s:s + 1] + pr
        c_im = e_im[s:s + 1] + pi
    return jnp.concatenate(ins_re, axis=0), jnp.concatenate(ins_im, axis=0), c_re, c_im


def _ssm_fwd(u, a_re, a_im, b_re, b_im, c_re, c_im, d_skip, chunk, carry=None):
    T = u.shape[0]
    nc = T // chunk
    steps = chunk // SUBLANES
    blk = SSM_LANE_BLOCK

    def body(u_ref, ar_ref, ai_ref, br_ref, bi_ref, cr_ref, ci_ref, dk_ref,
             y_ref, hr_ref, hi_ref, inr_ref, ini_ref, useg_ref, yseg_ref, pr_ref, pi_ref, carry_ref):
        c = pl.program_id(1)
        ar, ai = ar_ref[...], ai_ref[...]

        @pl.when(c == 0)
        def _():
            _power_table(ar, ai, pr_ref, pi_ref, steps)
            carry_ref[...] = jnp.zeros_like(carry_ref)

        _to_segments(u_ref, useg_ref, steps)
        ub = useg_ref[...].astype(BF16)
        hr_ref[...] = _dot(ub, br_ref[0])
        hi_ref[...] = _dot(ub, bi_ref[0])
        first = slice(0, SUBLANES)

        def scan(t4, prev):
            for j in range(SCAN_UNROLL):
                rows = pl.ds(pl.multiple_of((t4 * SCAN_UNROLL + j) * SUBLANES, SUBLANES), SUBLANES)
                pr, pi = _cmul(pr_ref[first, :], pi_ref[first, :], prev[0], prev[1])
                prev = (pr + hr_ref[rows, :], pi + hi_ref[rows, :])
                hr_ref[rows, :] = prev[0]
                hi_ref[rows, :] = prev[1]
            return prev

        zero = jnp.zeros((SUBLANES, blk), F32)
        lax.fori_loop(0, steps // SCAN_UNROLL, scan, (zero, zero))

        top = slice(chunk - SUBLANES, chunk)
        in_re, in_im, out_re, out_im = _segment_carries(
            hr_ref[top, :], hi_ref[top, :], pr_ref[top, :][0:1], pi_ref[top, :][0:1],
            carry_ref[0:1, :], carry_ref[1:2, :], reverse=False)
        carry_ref[0:1, :] = out_re
        carry_ref[1:2, :] = out_im
        inr_ref[...] = in_re
        ini_ref[...] = in_im

        def fix(t4, _):
            for j in range(SCAN_UNROLL):
                rows = pl.ds(pl.multiple_of((t4 * SCAN_UNROLL + j) * SUBLANES, SUBLANES), SUBLANES)
                fr, fi = _cmul(pr_ref[rows, :], pi_ref[rows, :], in_re, in_im)
                hr_ref[rows, :] += fr
                hi_ref[rows, :] += fi
            return 0

        lax.fori_loop(0, steps // SCAN_UNROLL, fix, 0)

        yseg_ref[...] = _dot(hr_ref[...].astype(BF16), cr_ref[0]) - _dot(hi_ref[...].astype(BF16), ci_ref[0])
        _from_segments(yseg_ref, y_ref, steps)
        y_ref[...] += dk_ref[...] * u_ref[...]

    row = pl.BlockSpec((1, blk), lambda j, c: (0, j))
    b_mat = pl.BlockSpec((1, LANES, blk), lambda j, c: (j, 0, 0))
    c_mat = pl.BlockSpec((1, blk, LANES), lambda j, c: (j, 0, 0))
    tok = pl.BlockSpec((chunk, LANES), lambda j, c: (c, j))
    state = pl.BlockSpec((chunk, blk), lambda j, c: (c, j))
    enter = pl.BlockSpec((SUBLANES, blk), lambda j, c: (c, j))
    return _hosted_call(
        body, carry, _edge_2d(N_SSM_BLOCKS, nc), name="ssm_fwd", grid=(N_SSM_BLOCKS, nc),
        in_specs=[tok, row, row, b_mat, b_mat, c_mat, c_mat, pl.BlockSpec((1, LANES), lambda j, c: (0, j))],
        out_specs=[tok, state, state, enter, enter],
        out_shape=[jax.ShapeDtypeStruct((T, SSM_W), F32), jax.ShapeDtypeStruct((T, STATES), F32),
                   jax.ShapeDtypeStruct((T, STATES), F32), jax.ShapeDtypeStruct((nc * SUBLANES, STATES), F32),
                   jax.ShapeDtypeStruct((nc * SUBLANES, STATES), F32)],
        scratch_shapes=[pltpu.VMEM((chunk, LANES), F32), pltpu.VMEM((chunk, LANES), F32),
                        pltpu.VMEM((chunk, blk), F32), pltpu.VMEM((chunk, blk), F32), pltpu.VMEM((SUBLANES, blk), F32)],
        compiler_params=_params(("arbitrary", "arbitrary"), VMEM_MID),
        inputs=(u, a_re, a_im, b_re, b_im, c_re, c_im, d_skip))


def _ssm_bwd(dy, u, h_re, h_im, in_re, in_im, a_re, a_im, b_re, b_im, c_re, c_im, d_skip, chunk, carry=None):
    T = u.shape[0]
    nc = T // chunk
    steps = chunk // SUBLANES
    blk = SSM_LANE_BLOCK

    def body(dy_ref, u_ref, hr_ref, hi_ref, inr_ref, ini_ref, ar_ref, ai_ref, br_ref, bi_ref, cr_ref, ci_ref, dk_ref,
             du_ref, dbr_ref, dbi_ref, dcr_ref, dci_ref, dar_ref, dai_ref, ddk_ref,
             dyseg_ref, useg_ref, duseg_ref, gr_ref, gi_ref, pr_ref, pi_ref, carry_ref, accr_ref, acci_ref):
        c = pl.program_id(1)
        ar, ai = ar_ref[...], ai_ref[...]

        @pl.when(c == 0)
        def _():
            _power_table(ar, ai, pr_ref, pi_ref, steps)
            carry_ref[...] = jnp.zeros_like(carry_ref)
            accr_ref[...] = jnp.zeros_like(accr_ref)
            acci_ref[...] = jnp.zeros_like(acci_ref)

        _to_segments(dy_ref, dyseg_ref, steps)
        _to_segments(u_ref, useg_ref, steps)
        dyb = dyseg_ref[...].astype(BF16)
        ub = useg_ref[...].astype(BF16)
        gr_ref[...] = _dot_nt(dyb, cr_ref[0])
        gi_ref[...] = -_dot_nt(dyb, ci_ref[0])
        dcr = _dot_tn(hr_ref[...].astype(BF16), dyb)
        dci = -_dot_tn(hi_ref[...].astype(BF16), dyb)
        ddk = jnp.sum(dy_ref[...] * u_ref[...], axis=0, keepdims=True)

        first = slice(0, SUBLANES)

        def scan(k4, nxt):
            for j in range(SCAN_UNROLL):
                t = steps - 1 - (k4 * SCAN_UNROLL + j)
                rows = pl.ds(pl.multiple_of(t * SUBLANES, SUBLANES), SUBLANES)
                pr, pi = _cmul_conj(pr_ref[first, :], pi_ref[first, :], nxt[0], nxt[1])
                nxt = (pr + gr_ref[rows, :], pi + gi_ref[rows, :])
                gr_ref[rows, :] = nxt[0]
                gi_ref[rows, :] = nxt[1]
            return nxt

        top = slice(chunk - SUBLANES, chunk)
        zero = jnp.zeros((SUBLANES, blk), F32)
        lax.fori_loop(0, steps // SCAN_UNROLL, scan, (zero, zero))

        gin_re, gin_im, out_re, out_im = _segment_carries(
            gr_ref[0:SUBLANES, :], gi_ref[0:SUBLANES, :], pr_ref[top, :][0:1], -pi_ref[top, :][0:1],
            carry_ref[0:1, :], carry_ref[1:2, :], reverse=True)
        carry_ref[0:1, :] = out_re
        carry_ref[1:2, :] = out_im

        def fix_row(rows, prow, hp_re, hp_im, acc):
            fr, fi = _cmul_conj(pr_ref[prow, :], pi_ref[prow, :], gin_re, gin_im)
            g_re = gr_ref[rows, :] + fr
            g_im = gi_ref[rows, :] + fi
            gr_ref[rows, :] = g_re
            gi_ref[rows, :] = g_im
            return acc[0] + g_re * hp_re + g_im * hp_im, acc[1] + g_im * hp_re - g_re * hp_im

        def fix_at(t, acc):
            aligned = (lambda r: r * SUBLANES) if isinstance(t, int) else (lambda r: pl.multiple_of(r * SUBLANES, SUBLANES))
            rows, before, prow = (pl.ds(aligned(r), SUBLANES) for r in (t, t - 1, steps - 1 - t))
            return fix_row(rows, prow, hr_ref[before, :], hi_ref[before, :], acc)

        def fix(t4, acc):
            for j in range(SCAN_UNROLL):
                acc = fix_at(t4 * SCAN_UNROLL + j, acc)
            return acc

        acc = fix_row(first, top, inr_ref[...], ini_ref[...], (accr_ref[...], acci_ref[...]))
        for t in range(1, SCAN_UNROLL):
            acc = fix_at(t, acc)
        acc_re, acc_im = lax.fori_loop(1, steps // SCAN_UNROLL, fix, acc)
        accr_ref[...] = acc_re
        acci_ref[...] = acc_im

        gbr = gr_ref[...].astype(BF16)
        gbi = gi_ref[...].astype(BF16)
        duseg_ref[...] = _dot_nt(gbr, br_ref[0]) + _dot_nt(gbi, bi_ref[0])
        _from_segments(duseg_ref, dyseg_ref, steps)
        du_ref[...] = (dyseg_ref[...] + dk_ref[...] * dy_ref[...]).astype(BF16)
        dbr = _dot_tn(ub, gbr)
        dbi = _dot_tn(ub, gbi)

        @pl.when(c == 0)
        def _():
            dbr_ref[0] = dbr
            dbi_ref[0] = dbi
            dcr_ref[0] = dcr
            dci_ref[0] = dci
            ddk_ref[...] = ddk

        @pl.when(c > 0)
        def _():
            dbr_ref[0] += dbr
            dbi_ref[0] += dbi
            dcr_ref[0] += dcr
            dci_ref[0] += dci
            ddk_ref[...] += ddk

        @pl.when(c == nc - 1)
        def _():
            dar_ref[...] = jnp.sum(acc_re, axis=0, keepdims=True)
            dai_ref[...] = jnp.sum(acc_im, axis=0, keepdims=True)

    rev = lambda c: nc - 1 - c
    row = pl.BlockSpec((1, blk), lambda j, c: (0, j))
    b_mat = pl.BlockSpec((1, LANES, blk), lambda j, c: (j, 0, 0))
    c_mat = pl.BlockSpec((1, blk, LANES), lambda j, c: (j, 0, 0))
    tok = pl.BlockSpec((chunk, LANES), lambda j, c: (rev(c), j))
    state = pl.BlockSpec((chunk, blk), lambda j, c: (rev(c), j))
    enter = pl.BlockSpec((SUBLANES, blk), lambda j, c: (rev(c), j))
    chan = pl.BlockSpec((1, LANES), lambda j, c: (0, j))
    f32 = lambda *s: jax.ShapeDtypeStruct(s, F32)
    return _hosted_call(
        body, carry, _edge_2d(N_SSM_BLOCKS, nc), name="ssm_bwd", grid=(N_SSM_BLOCKS, nc),
        in_specs=[tok, tok, state, state, enter, enter, row, row, b_mat, b_mat, c_mat, c_mat, chan],
        out_specs=[tok, b_mat, b_mat, c_mat, c_mat, row, row, chan],
        out_shape=[jax.ShapeDtypeStruct((T, SSM_W), BF16), f32(N_SSM_BLOCKS, LANES, blk), f32(N_SSM_BLOCKS, LANES, blk),
                   f32(N_SSM_BLOCKS, blk, LANES), f32(N_SSM_BLOCKS, blk, LANES), f32(1, STATES), f32(1, STATES), f32(1, SSM_W)],
        scratch_shapes=[pltpu.VMEM((chunk, LANES), F32), pltpu.VMEM((chunk, LANES), F32), pltpu.VMEM((chunk, LANES), F32),
                        pltpu.VMEM((chunk, blk), F32), pltpu.VMEM((chunk, blk), F32),
                        pltpu.VMEM((chunk, blk), F32), pltpu.VMEM((chunk, blk), F32),
                        pltpu.VMEM((SUBLANES, blk), F32), pltpu.VMEM((SUBLANES, blk), F32), pltpu.VMEM((SUBLANES, blk), F32)],
        compiler_params=_params(("arbitrary", "arbitrary"), VMEM_BIG),
        inputs=(dy, u, h_re, h_im, in_re, in_im, a_re, a_im, b_re, b_im, c_re, c_im, d_skip))


def _merge_forward(y, att, ga, gs, w_glu, w_ssm, w_attn):
    z = jax.nn.gelu(y)
    zb = z.astype(BF16)
    gl = jax.nn.sigmoid(_dot(zb, w_glu))
    z2b = (z * gl).astype(BF16)
    y_ssm = _dot(z2b, w_ssm)
    y_attn = _dot(att, w_attn)
    sa = jax.nn.sigmoid(ga)
    ss = jax.nn.sigmoid(gs)
    merged = (sa * y_attn + ss * y_ssm).astype(BF16)
    return z, zb, gl, z2b, y_ssm, y_attn, sa, ss, merged


def _merge_fwd(x, y, att, ga, gs, g2, g3, w_glu, w_ssm, w_attn, w_out, tile):
    T = x.shape[0]

    def body(x_ref, y_ref, att_ref, ga_ref, gs_ref, g2_ref, g3_ref, wg_ref, ws_ref, wa_ref, wo_ref, x1_ref, o_ref, h2_ref):
        merged = _merge_forward(y_ref[...], att_ref[...], ga_ref[...], gs_ref[...], wg_ref[...], ws_ref[...], wa_ref[...])[-1]
        o = _dot(merged, wo_ref[...])
        x1 = x_ref[...] + o * _rms_scale(o) * g2_ref[...]
        o_ref[...] = o
        x1_ref[...] = x1
        h2_ref[...] = (x1 * _rms_scale(x1) * g3_ref[...]).astype(BF16)

    tok = lambda w: pl.BlockSpec((tile, w), lambda i: (i, 0))
    vec = _const_spec((1, D_MODEL))
    return pl.pallas_call(
        body, name="merge_fwd", grid=(T // tile,),
        in_specs=[tok(D_MODEL), tok(SSM_W), tok(ATTN_W), tok(D_MODEL), tok(D_MODEL), vec, vec,
                  _const_spec((SSM_W, SSM_W)), _const_spec((SSM_W, D_MODEL)), _const_spec((ATTN_W, D_MODEL)),
                  _const_spec((D_MODEL, D_MODEL))],
        out_specs=[tok(D_MODEL), tok(D_MODEL), tok(D_MODEL)],
        out_shape=_hbm_out([jax.ShapeDtypeStruct((T, D_MODEL), F32), jax.ShapeDtypeStruct((T, D_MODEL), F32),
                            jax.ShapeDtypeStruct((T, D_MODEL), BF16)]),
        compiler_params=_params(("arbitrary",), VMEM_MID),
    )(*_in_hbm(x, y, att, ga, gs, g2, g3, w_glu, w_ssm, w_attn, w_out))


def _merge_bwd(dh2, dx2, x1, o, y, att, ga, gs, g2, g3, w_glu, w_ssm, w_attn, w_out, tile, carry=None):
    T = x1.shape[0]
    n_steps = T // tile

    group = min(2, n_steps)
    staged_widths = (D_MODEL, D_MODEL, ATTN_W, D_MODEL, SSM_W, D_MODEL, SSM_W, SSM_W)

    def body(dh2_ref, dx2_ref, x1_ref, o_ref, y_ref, att_ref, ga_ref, gs_ref, g2_ref, g3_ref, wg_ref, ws_ref, wa_ref, wo_ref,
             dx1_ref, dgates_ref, datt_ref, dy_ref, dwg_hbm, dws_hbm, dwa_hbm, dwo_hbm, dg2_ref, dg3_ref,
             awg_ref, aws_ref, awa_ref, awo_ref, *staged):
        i = pl.program_id(0)
        x1v, ov = x1_ref[...], o_ref[...]
        dxn, dg3 = _rms_bwd(dh2_ref[...], x1v, _rms_scale(x1v), g3_ref[...])
        dx1 = dx2_ref[...] + dxn
        dx1_ref[...] = dx1
        do, dg2 = _rms_bwd(dx1, ov, _rms_scale(ov), g2_ref[...])
        dob = do.astype(BF16)

        yv = y_ref[...]
        att = att_ref[...]
        z, zb, gl, z2b, y_ssm, y_attn, sa, ss, merged = _merge_forward(
            yv, att, ga_ref[...], gs_ref[...], wg_ref[...], ws_ref[...], wa_ref[...])
        dmerged = _dot_nt(dob, wo_ref[...])
        dya = (dmerged * sa).astype(BF16)
        dys = (dmerged * ss).astype(BF16)
        dgates_ref[:, :D_MODEL] = (dmerged * y_attn * sa * (1.0 - sa)).astype(BF16)
        dgates_ref[:, D_MODEL:] = (dmerged * y_ssm * ss * (1.0 - ss)).astype(BF16)
        datt_ref[...] = _dot_nt(dya, wa_ref[...]).astype(BF16)
        dz2 = _dot_nt(dys, ws_ref[...])
        dpre = (dz2 * z * gl * (1.0 - gl)).astype(BF16)
        dz = dz2 * gl + _dot_nt(dpre, wg_ref[...])
        _, gelu_vjp = jax.vjp(jax.nn.gelu, yv)
        dy_ref[...] = gelu_vjp(dz)[0]

        part = pl.ds(pl.multiple_of((i % group) * tile, tile), tile)
        for ref, val in zip(staged, (merged, dob, att, dya, z2b, dys, zb, dpre)):
            ref[part, :] = val

        @pl.when(i == 0)
        def _():
            dg2_ref[...] = dg2
            dg3_ref[...] = dg3

        @pl.when(i > 0)
        def _():
            dg2_ref[...] += dg2
            dg3_ref[...] += dg3

        def weight_grads():
            s_merged, s_dob, s_att, s_dya, s_z2b, s_dys, s_zb, s_dpre = (ref[...] for ref in staged)
            return ((awo_ref, _dot_tn(s_merged, s_dob)), (awa_ref, _dot_tn(s_att, s_dya)),
                    (aws_ref, _dot_tn(s_z2b, s_dys)), (awg_ref, _dot_tn(s_zb, s_dpre)))

        @pl.when(i == group - 1)
        def _():
            for ref, val in weight_grads():
                ref[...] = val

        @pl.when((i % group == group - 1) & (i > group - 1))
        def _():
            for ref, val in weight_grads():
                ref[...] += val

        @pl.when(i == n_steps - 1)
        def _():
            pltpu.sync_copy(awg_ref, dwg_hbm)
            pltpu.sync_copy(aws_ref, dws_hbm)
            pltpu.sync_copy(awa_ref, dwa_hbm)
            pltpu.sync_copy(awo_ref, dwo_hbm)

    tok = lambda w: pl.BlockSpec((tile, w), lambda i: (i, 0))
    vec = _const_spec((1, D_MODEL))
    any_ = pl.BlockSpec(memory_space=pl.ANY)
    vec_out = pl.BlockSpec((1, D_MODEL), lambda i: (0, 0))
    f32 = lambda *s: jax.ShapeDtypeStruct(s, F32)
    bf = lambda *s: jax.ShapeDtypeStruct(s, BF16)
    return _hosted_call(
        body, carry, _edge_1d(n_steps), name="merge_bwd", grid=(n_steps,),
        in_specs=[tok(D_MODEL), tok(D_MODEL), tok(D_MODEL), tok(D_MODEL), tok(SSM_W), tok(ATTN_W), tok(D_MODEL), tok(D_MODEL),
                  vec, vec, _const_spec((SSM_W, SSM_W)), _const_spec((SSM_W, D_MODEL)), _const_spec((ATTN_W, D_MODEL)),
                  _const_spec((D_MODEL, D_MODEL))],
        out_specs=[tok(D_MODEL), tok(2 * D_MODEL), tok(ATTN_W), tok(SSM_W), any_, any_, any_, any_, vec_out, vec_out],
        out_shape=[f32(T, D_MODEL), bf(T, 2 * D_MODEL), bf(T, ATTN_W), f32(T, SSM_W),
                   f32(SSM_W, SSM_W), f32(SSM_W, D_MODEL), f32(ATTN_W, D_MODEL), f32(D_MODEL, D_MODEL),
                   f32(1, D_MODEL), f32(1, D_MODEL)],
        scratch_shapes=[pltpu.VMEM((SSM_W, SSM_W), F32), pltpu.VMEM((SSM_W, D_MODEL), F32),
                        pltpu.VMEM((ATTN_W, D_MODEL), F32), pltpu.VMEM((D_MODEL, D_MODEL), F32)]
        + [pltpu.VMEM((group * tile, wd), BF16) for wd in staged_widths],
        compiler_params=_params(("arbitrary",), VMEM_BIG),
        inputs=(dh2, dx2, x1, o, y, att, ga, gs, g2, g3, w_glu, w_ssm, w_attn, w_out))


FF_SHARD = D_FF // N_DEV


def _mlp_fwd(h2, x1, target, g4, w_ff_in, w_ff_out, tile):
    T = h2.shape[0]
    col_chunk = 2 * FF_SHARD

    def body(h2_ref, x1_ref, tg_ref, g4_ref, wi_ref, wo_ref, a_ref, dfo_ref, dx2_ref, loss_ref, dg4_ref, rr_ref):
        i = pl.program_id(0)
        h2v = h2_ref[...]
        for c in range(D_FF // col_chunk):
            cols = slice(c * col_chunk, (c + 1) * col_chunk)
            a = _dot_nt(h2v, wi_ref[cols, :])
            a_ref[:, cols] = a.astype(BF16)
            ra = jnp.maximum(a, 0.0)
            rr_ref[:, cols] = (ra * ra).astype(BF16)
        f = _dot(rr_ref[...], wo_ref[...])
        r = _rms_scale(f)
        g = g4_ref[...]
        err = x1_ref[...] + f * r * g - tg_ref[...]
        dx2 = err * (1.0 / D_MODEL)
        dx2_ref[...] = dx2
        dfo, dg = _rms_bwd(dx2, f, r, g)
        dfo_ref[...] = dfo.astype(BF16)
        row = lax.broadcasted_iota(jnp.int32, (SUBLANES, LANES), 0)
        col = lax.broadcasted_iota(jnp.int32, (SUBLANES, LANES), 1)
        loss = jnp.where((row == 0) & (col == 0), (0.5 / D_MODEL) * jnp.sum(err * err), 0.0)

        @pl.when(i == 0)
        def _():
            loss_ref[...] = loss
            dg4_ref[...] = dg

        @pl.when(i > 0)
        def _():
            loss_ref[...] += loss
            dg4_ref[...] += dg

    tok = pl.BlockSpec((tile, D_MODEL), lambda i: (i, 0))
    return pl.pallas_call(
        body, name="mlp_fwd", grid=(T // tile,),
        in_specs=[tok, tok, tok, _const_spec((1, D_MODEL)), _const_spec((D_FF, D_MODEL)), _const_spec((D_FF, D_MODEL))],
        out_specs=[pl.BlockSpec((tile, D_FF), lambda i: (i, 0)), tok, tok,
                   pl.BlockSpec((SUBLANES, LANES), lambda i: (0, 0)), pl.BlockSpec((1, D_MODEL), lambda i: (0, 0))],
        out_shape=_hbm_out([jax.ShapeDtypeStruct((T, D_FF), BF16), jax.ShapeDtypeStruct((T, D_MODEL), BF16),
                            jax.ShapeDtypeStruct((T, D_MODEL), F32), jax.ShapeDtypeStruct((SUBLANES, LANES), F32),
                            jax.ShapeDtypeStruct((1, D_MODEL), F32)]),
        scratch_shapes=[pltpu.VMEM((tile, D_FF), BF16)],
        compiler_params=_params(("arbitrary",), VMEM_MAX),
    )(*_in_hbm(h2, x1, target, g4, w_ff_in.reshape(D_FF, D_MODEL), w_ff_out.reshape(D_FF, D_MODEL)))


def _mlp_weight_grads(dfo, a, h2, w_ff_out, row_chunk):
    T = h2.shape[0]

    def body(dfo_ref, h2_ref, a_ref, wo_ref, dwi_ref, dwo_ref, da_ref, rr_ref):
        def rows(r, _):
            sl = pl.ds(pl.multiple_of(r * row_chunk, row_chunk), row_chunk)
            ra = jnp.maximum(a_ref[sl, :].astype(F32), 0.0)
            da_ref[sl, :] = (_dot_nt(dfo_ref[sl, :], wo_ref[0]) * (2.0 * ra)).astype(BF16)
            rr_ref[sl, :] = (ra * ra).astype(BF16)
            return 0

        lax.fori_loop(0, T // row_chunk, rows, 0)
        dwo_ref[0] = _dot_tn(rr_ref[...], dfo_ref[...])
        dwi_ref[0] = _dot_tn(h2_ref[...], da_ref[...])

    return pl.pallas_call(
        body, name="mlp_weight_grads", grid=(N_DEV,),
        in_specs=[_const_spec((T, D_MODEL)), _const_spec((T, D_MODEL)), pl.BlockSpec((T, FF_SHARD), lambda k: (0, k)),
                  pl.BlockSpec((1, FF_SHARD, D_MODEL), lambda k: (k, 0, 0))],
        out_specs=[pl.BlockSpec((1, D_MODEL, FF_SHARD), lambda k: (k, 0, 0)),
                   pl.BlockSpec((1, FF_SHARD, D_MODEL), lambda k: (k, 0, 0)), pl.BlockSpec((T, FF_SHARD), lambda k: (0, k))],
        out_shape=_hbm_out([jax.ShapeDtypeStruct((N_DEV, D_MODEL, FF_SHARD), F32),
                            jax.ShapeDtypeStruct((N_DEV, FF_SHARD, D_MODEL), F32), jax.ShapeDtypeStruct((T, D_FF), BF16)]),
        scratch_shapes=[pltpu.VMEM((T, FF_SHARD), BF16)],
        compiler_params=_params(("arbitrary",), VMEM_MAX),
    )(*_in_hbm(dfo, h2, a, w_ff_out))


def _mlp_input_grad(da, w_ff_in_t, tile):
    T = da.shape[0]

    def body(da_ref, w_ref, o_ref):
        o_ref[...] = _dot(da_ref[...], w_ref[...])

    return pl.pallas_call(
        body, name="mlp_input_grad", grid=(T // tile,),
        in_specs=[pl.BlockSpec((tile, D_FF), lambda i: (i, 0)), _const_spec((D_FF, D_MODEL))],
        out_specs=pl.BlockSpec((tile, D_MODEL), lambda i: (i, 0)),
        out_shape=_hbm_out(jax.ShapeDtypeStruct((T, D_MODEL), F32)),
        compiler_params=_params(("arbitrary",), VMEM_MID),
    )(*_in_hbm(da, w_ff_in_t))


def _block_diag_in(b):
    bt = b.reshape(N_SSM_BLOCKS, GROUPS_PER_BLOCK, GROUP_CH, N_STATE)
    eye = jnp.eye(GROUPS_PER_BLOCK, dtype=b.dtype)
    return jnp.einsum("jacp,ab->jacbp", bt, eye).reshape(N_SSM_BLOCKS, LANES, SSM_LANE_BLOCK)


def _block_diag_in_grad(g):
    g = g.reshape(N_SSM_BLOCKS, GROUPS_PER_BLOCK, GROUP_CH, GROUPS_PER_BLOCK, N_STATE)
    d = jnp.diagonal(g, axis1=1, axis2=3)
    return jnp.transpose(d, (0, 3, 1, 2)).reshape(N_GROUPS, GROUP_CH, N_STATE)


def _block_diag_out(c):
    ct = c.reshape(N_SSM_BLOCKS, GROUPS_PER_BLOCK, GROUP_CH, N_STATE)
    eye = jnp.eye(GROUPS_PER_BLOCK, dtype=c.dtype)
    return jnp.einsum("jacp,ab->japbc", ct, eye).reshape(N_SSM_BLOCKS, SSM_LANE_BLOCK, LANES)


def _block_diag_out_grad(g):
    g = g.reshape(N_SSM_BLOCKS, GROUPS_PER_BLOCK, N_STATE, GROUPS_PER_BLOCK, GROUP_CH)
    d = jnp.diagonal(g, axis1=1, axis2=3)
    return jnp.transpose(d, (0, 3, 2, 1)).reshape(N_GROUPS, GROUP_CH, N_STATE)


def _tiles(T):
    return dict(proj=min(512, T), proj_bwd=min(512, T // 2), merge=min(512, T), merge_bwd=min(256, T),
                mlp_fwd=min(512, T), mlp_bwd=min(512, T), ssm_chunk=min(1024, T))


def _mesh_position():
    x, y, c = lax.axis_index("x"), lax.axis_index("y"), lax.axis_index("c")
    other_chips = [(1 - x, y), (x, 1 - y), (1 - x, 1 - y)]
    return x, y, c, other_chips


def _gather_carry(arrays):
    n = len(arrays)

    def copies(ins, outs, sems):
        send_sems, recv_sems, local_sems = sems
        x, y, c, chips = _mesh_position()
        me, sibling = (x, y, c), (x, y, 1 - c)

        def copy(a, k, block, to, src=None):
            px, py, pc = block
            dst = outs[a].at[4 * px + 2 * py + pc]
            return pltpu.make_async_remote_copy(
                src_ref=dst if src is None else src, dst_ref=dst, send_sem=send_sems.at[7 * a + k],
                recv_sem=recv_sems.at[7 * a + k], device_id=to, device_id_type=MESH_IDS)

        mine = [pltpu.make_async_copy(ins[a], outs[a].at[4 * x + 2 * y + c], local_sems.at[a]) for a in range(n)]
        first = []
        for a in range(n):
            first.append(copy(a, 0, me, sibling, src=ins[a]))
            first += [copy(a, 1 + j, me, (*chip, c), src=ins[a]) for j, chip in enumerate(chips)]
        return copy, mine, first, me, sibling, chips, c

    def start(ins, outs, sems):
        _, mine, first, *_ = copies(ins, outs, sems)
        for cp in mine + first:
            cp.start()

    def finish(ins, outs, sems):
        copy, mine, first, me, sibling, chips, c = copies(ins, outs, sems)
        passed = []
        for a in range(n):
            for j, chip in enumerate(chips):
                copy(a, 1 + j, (*chip, c), me).wait_recv()
                passed.append(copy(a, 4 + j, (*chip, c), sibling))
                passed[-1].start()
        for a in range(n):
            copy(a, 0, sibling, me).wait_recv()
            for j, chip in enumerate(chips):
                copy(a, 4 + j, (*chip, 1 - c), me).wait_recv()
        for cp in first + passed:
            cp.wait_send()
        for cp in mine:
            cp.wait()

    return _Carry(arrays, [jax.ShapeDtypeStruct((N_DEV,) + a.shape, a.dtype) for a in arrays],
                  [pltpu.SemaphoreType.DMA((7 * n,)), pltpu.SemaphoreType.DMA((7 * n,)), pltpu.SemaphoreType.DMA((n,))],
                  start, finish)


def _pairwise_carry(arrays, n_slots, make_copies):
    n = len(arrays)

    def start(ins, outs, sems):
        for cp in make_copies(ins, outs, sems):
            cp.start()

    def finish(ins, outs, sems):
        for cp in make_copies(ins, outs, sems):
            cp.wait()

    return _Carry(arrays, [jax.ShapeDtypeStruct((n_slots,) + a.shape[1:], a.dtype) for a in arrays],
                  [pltpu.SemaphoreType.DMA((n_slots * n,)), pltpu.SemaphoreType.DMA((n_slots * n,))], start, finish)


def _sibling_carry(grads):
    def make_copies(ins, outs, sems):
        x, y, c, _ = _mesh_position()
        return [pltpu.make_async_remote_copy(
            src_ref=ins[a].at[2 * ch + (1 - c)], dst_ref=outs[a].at[ch], send_sem=sems[0].at[4 * a + ch],
            recv_sem=sems[1].at[4 * a + ch], device_id=(x, y, 1 - c), device_id_type=MESH_IDS)
            for a in range(len(grads)) for ch in range(4)]

    return _pairwise_carry(grads, 4, make_copies)


def _chips_carry(sums):
    def make_copies(ins, outs, sems):
        x, y, c, chips = _mesh_position()
        return [pltpu.make_async_remote_copy(
            src_ref=ins[a].at[2 * px + py], dst_ref=outs[a].at[j], send_sem=sems[0].at[3 * a + j],
            recv_sem=sems[1].at[3 * a + j], device_id=(px, py, c), device_id_type=MESH_IDS)
            for a in range(len(sums)) for j, (px, py) in enumerate(chips)]

    return _pairwise_carry(sums, 3, make_copies)


SEM_SPEC = pl.BlockSpec(memory_space=pltpu.SEMAPHORE)
DATAFLOW_EFFECT = pltpu.SideEffectType.DATAFLOW_SIDE_EFFECTING


def _exchange_start(carry, name):
    n = len(carry.inputs)
    lands = [lax.empty(s.shape, s.dtype) for s in carry.out_shapes]

    def body(*refs):
        srcs, zones, sems, token = refs[:n], refs[n:2 * n], refs[2 * n:2 * n + 2], refs[-1]
        carry.start(srcs, zones, sems)
        token[...] = jnp.zeros_like(token)

    outs = pl.pallas_call(
        body, name=name, in_specs=[HBM_SPEC] * (2 * n),
        out_specs=[SEM_SPEC, SEM_SPEC] + [HBM_SPEC] * (2 * n) + [pl.BlockSpec(memory_space=pltpu.VMEM)],
        out_shape=list(carry.sems) + _hbm_out([jax.ShapeDtypeStruct(a.shape, a.dtype) for a in carry.inputs])
        + _hbm_out(carry.out_shapes) + [jax.ShapeDtypeStruct((SUBLANES, LANES), F32)],
        input_output_aliases={j: 2 + j for j in range(2 * n)},
        compiler_params=pltpu.CompilerParams(has_side_effects=DATAFLOW_EFFECT),
    )(*_in_hbm(*carry.inputs, *lands))
    return outs[:-1], outs[-1]


def _exchange_wait(carry, in_flight, after, name):
    n = len(carry.inputs)
    sems, srcs, zones = in_flight[:2], in_flight[2:2 + n], in_flight[2 + n:]

    def body(*refs):
        src_refs, zone_refs, sem_refs = refs[:n], refs[n:2 * n], refs[2 * n:2 * n + 2]
        carry.finish(src_refs, zone_refs, sem_refs)

    outs = pl.pallas_call(
        body, name=name, in_specs=[HBM_SPEC] * (2 * n) + [SEM_SPEC, SEM_SPEC, HBM_SPEC], out_specs=[HBM_SPEC] * (2 * n),
        out_shape=_hbm_out([jax.ShapeDtypeStruct(a.shape, a.dtype) for a in carry.inputs]) + _hbm_out(carry.out_shapes),
        input_output_aliases={j: j for j in range(2 * n)},
        compiler_params=pltpu.CompilerParams(has_side_effects=DATAFLOW_EFFECT),
    )(*srcs, *zones, *sems, after)
    return list(outs[n:])


def _add_sibling(grads8, recvs, core, row_tiles, name):
    k = len(grads8)
    g4 = [g.reshape(4, 2, *g.shape[1:]) for g in grads8]

    def body(core_ref, *refs):
        g_refs, r_refs, o_refs, ob_refs = (refs[j * k:(j + 1) * k] for j in range(4))
        for g_ref, r_ref, o_ref, ob_ref in zip(g_refs, r_refs, o_refs, ob_refs):
            s = g_ref[0] + r_ref[...]
            o_ref[...] = s
            ob_ref[...] = s.astype(BF16)

    def blocks(make):
        return [make(g.shape[1] // row_tiles, g.shape[2]) for g in grads8]

    slot = lambda tr, C: pl.BlockSpec((1, tr, C), lambda ch, r, core_ref: (ch, r, 0))
    outs = pl.pallas_call(
        body, name=name,
        grid_spec=pltpu.PrefetchScalarGridSpec(
            num_scalar_prefetch=1, grid=(4, row_tiles),
            in_specs=blocks(lambda tr, C: pl.BlockSpec((1, 1, tr, C), lambda ch, r, core_ref: (ch, core_ref[0], r, 0)))
            + blocks(slot), out_specs=blocks(slot) + blocks(slot)),
        out_shape=_hbm_out([jax.ShapeDtypeStruct((4,) + g.shape[1:], F32) for g in grads8]
                           + [jax.ShapeDtypeStruct((4,) + g.shape[1:], BF16) for g in grads8]),
        compiler_params=_params(("arbitrary", "arbitrary")),
    )(core, *_in_hbm(*g4, *recvs))
    return list(outs[:k]), list(outs[k:])


def _adam_math(w, g, m, v):
    m = ADAM_B1 * m + (1.0 - ADAM_B1) * g
    v = ADAM_B2 * v + (1.0 - ADAM_B2) * jnp.square(g)
    m_hat = m / (1.0 - ADAM_B1 ** ADAM_STEP)
    v_hat = v / (1.0 - ADAM_B2 ** ADAM_STEP)
    delta = -ADAM_LR * (m_hat / (jnp.sqrt(v_hat) + ADAM_EPS) + ADAM_WD * w)
    return delta, m, v


def _adam_big(ws, ms, vs, chip_sums, recvs, chip, row_tiles, name):
    k = len(ws)

    def body(chip_ref, *refs):
        w_refs, m_refs, v_refs, s_refs, r_refs, g_refs, d_refs, nm_refs, nv_refs = (refs[j * k:(j + 1) * k] for j in range(9))
        for a in range(k):
            r_ref = r_refs[a]
            g = s_refs[a][0] + r_ref[0].astype(F32) + r_ref[1].astype(F32) + r_ref[2].astype(F32)
            g_refs[a][...] = g
            d_refs[a][...], nm_refs[a][...], nv_refs[a][...] = _adam_math(w_refs[a][...], g, m_refs[a][...], v_refs[a][...])

    def blocks(make):
        return [make(w.shape[0] // row_tiles, w.shape[1]) for w in ws]

    blk = lambda tr, C: pl.BlockSpec((tr, C), lambda r, chip_ref: (r, 0))
    outs = pl.pallas_call(
        body, name=name,
        grid_spec=pltpu.PrefetchScalarGridSpec(
            num_scalar_prefetch=1, grid=(row_tiles,),
            in_specs=blocks(blk) * 3 + blocks(lambda tr, C: pl.BlockSpec((1, tr, C), lambda r, chip_ref: (chip_ref[0], r, 0)))
            + blocks(lambda tr, C: pl.BlockSpec((3, tr, C), lambda r, chip_ref: (0, r, 0))),
            out_specs=blocks(blk) * 4),
        out_shape=[jax.ShapeDtypeStruct(w.shape, F32) for w in ws] * 4,
        compiler_params=_params(("arbitrary",)),
    )(chip, *_in_hbm(*ws, *ms, *vs, *chip_sums, *recvs))
    return [list(outs[j * k:(j + 1) * k]) for j in range(4)]


def _sum_partials(partials, name):
    def body(p_ref, g_ref):
        g = p_ref[0]
        for d in range(1, partials.shape[0]):
            g = g + p_ref[d]
        g_ref[...] = g

    return pl.pallas_call(body, name=name, grid=(1,), in_specs=[_whole(partials.shape)], out_specs=_whole(partials.shape[1:]),
                          out_shape=jax.ShapeDtypeStruct(partials.shape[1:], F32))(*_in_hbm(partials))


def _adam_small(ws, ms, vs, gs):
    n = len(ws)

    def body(*refs):
        w_refs, m_refs, v_refs, g_refs = (refs[i * n:(i + 1) * n] for i in range(4))
        d_refs, nm_refs, nv_refs = (refs[(4 + i) * n:(5 + i) * n] for i in range(3))
        for j in range(n):
            d_refs[j][...], nm_refs[j][...], nv_refs[j][...] = _adam_math(
                w_refs[j][...], g_refs[j][...], m_refs[j][...], v_refs[j][...])

    specs = [_whole(w.shape) for w in ws]
    outs = pl.pallas_call(body, name="adam_small", grid=(1,), in_specs=specs * 4, out_specs=specs * 3,
                          out_shape=[jax.ShapeDtypeStruct(w.shape, F32) for w in ws] * 3,
                          compiler_params=_params(("arbitrary",), VMEM_MID))(*_in_hbm(*ws, *ms, *vs, *gs))
    return outs[:n], outs[n:2 * n], outs[2 * n:]


PACK_QUANTUM = SUBLANES * LANES


def _pack(named, names):
    parts = []
    for nme in names:
        flat = named[nme].reshape(-1)
        parts.append(jnp.pad(flat, (0, -flat.size % PACK_QUANTUM)))
    return jnp.concatenate(parts).reshape(-1, LANES)


def _unpack(packed, shapes, names):
    flat = packed.reshape(-1)
    out, pos = {}, 0
    for nme in names:
        size = math.prod(shapes[nme])
        out[nme] = flat[pos:pos + size].reshape(shapes[nme])
        pos += size + (-size % PACK_QUANTUM)
    return out


BIG = ("w_in", "w_glu", "w_attn_branch", "w_ssm_branch", "w_out", "w_ff_in", "w_ff_out")
COLUMN_SHARDED = ("w_in", "w_attn_branch", "w_ssm_branch", "w_ff_in")
SMALL = ("norm_mix_pre", "norm_mix_post", "norm_mlp_pre", "norm_mlp_post", "rel_bias", "sinks", "lam_re", "lam_im",
         "log_dt", "b_re", "b_im", "c_re", "c_im", "d_skip")
SWAPPED_SMALL = ("rel_bias", "b_re", "b_im")
SMALL_LATE = ("norm_mix_pre", "rel_bias", "sinks", "loss")
SMALL_BEFORE_ATTN_BWD = tuple(n for n in SMALL if n not in SMALL_LATE)
ALL_WEIGHTS = ("norm_mix_pre", "norm_mix_post", "norm_mlp_pre", "norm_mlp_post", "w_in", "rel_bias", "sinks", "lam_re",
               "lam_im", "log_dt", "b_re", "b_im", "c_re", "c_im", "d_skip", "w_glu", "w_attn_branch", "w_ssm_branch",
               "w_out", "w_ff_in", "w_ff_out")


def _full_from_gathered(name, gathered):
    _, r, c = gathered.shape
    if name in COLUMN_SHARDED:
        return jnp.transpose(gathered, (1, 0, 2)).reshape(r, N_DEV * c)
    return gathered.reshape(N_DEV * r, c)


def _blocks_from_full(name, full):
    r, c = full.shape
    if name in COLUMN_SHARDED:
        return jnp.transpose(full.reshape(r, N_DEV, c // N_DEV), (1, 0, 2))
    return full.reshape(N_DEV, r // N_DEV, c)


def kernel(x, norm_mix_pre, norm_mix_post, norm_mlp_pre, norm_mlp_post, w_in, rel_bias, sinks, lam_re, lam_im, log_dt, b_re, b_im, c_re, c_im, d_skip, w_glu, w_attn_branch, w_ssm_branch, w_out, w_ff_in, w_ff_out, loss_target, m_norm_mix_pre, m_norm_mix_post, m_norm_mlp_pre, m_norm_mlp_post, m_w_in, m_rel_bias, m_sinks, m_lam_re, m_lam_im, m_log_dt, m_b_re, m_b_im, m_c_re, m_c_im, m_d_skip, m_w_glu, m_w_attn_branch, m_w_ssm_branch, m_w_out, m_w_ff_in, m_w_ff_out, v_norm_mix_pre, v_norm_mix_post, v_norm_mlp_pre, v_norm_mlp_post, v_w_in, v_rel_bias, v_sinks, v_lam_re, v_lam_im, v_log_dt, v_b_re, v_b_im, v_c_re, v_c_im, v_d_skip, v_w_glu, v_w_attn_branch, v_w_ssm_branch, v_w_out, v_w_ff_in, v_w_ff_out):
    args = dict(locals())
    w = {n: args[n] for n in ALL_WEIGHTS}
    m = {n: args["m_" + n] for n in ALL_WEIGHTS}
    v = {n: args["v_" + n] for n in ALL_WEIGHTS}
    core = lax.axis_index("c").astype(jnp.int32).reshape(1)
    chip = (2 * lax.axis_index("x") + lax.axis_index("y")).astype(jnp.int32).reshape(1)
    xs, target = x[0], loss_target[0]
    t = _tiles(xs.shape[0])
    local = lambda d, n: d[n][0].T if n == "w_in" else d[n][0]
    shard = {n: local(w, n).astype(BF16) for n in BIG}
    shard["w_ff_in"] = shard["w_ff_in"].T
    view = lambda n, a: jnp.swapaxes(a, -1, -2) if n in SWAPPED_SMALL else a
    small = {n: (view(n, w[n]) if n == "rel_bias" else view(n, w[n])[0]) for n in SMALL}
    g1, g2, g3, g4 = (small[n].reshape(1, D_MODEL) for n in ("norm_mix_pre", "norm_mix_post", "norm_mlp_pre", "norm_mlp_post"))
    bucket = jnp.asarray(_bucket_table())
    rel_b, sink = small["rel_bias"], small["sinks"].reshape(1, N_HEADS)
    lam_r, lam_i = small["lam_re"].reshape(1, STATES), small["lam_im"].reshape(1, STATES)
    ldt_rep = jnp.repeat(small["log_dt"].reshape(N_GROUPS), N_STATE).reshape(1, STATES)
    bd_re, bd_im = _block_diag_in(small["b_re"]), _block_diag_in(small["b_im"])
    cm_re, cm_im = _block_diag_out(small["c_re"]).astype(BF16), _block_diag_out(small["c_im"]).astype(BF16)
    dsk = small["d_skip"].reshape(1, SSM_W)

    (g_in,) = _run_carry(_gather_carry([shard["w_in"]]), "gather_w_in")
    wf_in = g_in.reshape(IN_W, D_MODEL)
    merge_names = ("w_glu", "w_attn_branch", "w_ssm_branch", "w_out")
    (q, k, vv, u, ga, gs, h), gathered = _in_proj_fwd(xs, g1, wf_in, t["proj"], _gather_carry([shard[n] for n in merge_names]))
    wf = {n: _full_from_gathered(n, g) for n, g in zip(merge_names, gathered)}
    (att,), (wf_ff_in,) = _attn_fwd(q, k, vv, bucket, rel_b, sink, _gather_carry([shard["w_ff_in"]]))
    a_re, a_im, bm_re, bm_im = _ssm_prep(lam_r, lam_i, ldt_rep, bd_re, bd_im)
    (y, h_re, h_im, in_re, in_im), (wf_ff_out,) = _ssm_fwd(
        u, a_re, a_im, bm_re, bm_im, cm_re, cm_im, dsk, t["ssm_chunk"], _gather_carry([shard["w_ff_out"]]))
    x1, o, h2 = _merge_fwd(xs, y, att, ga, gs, g2, g3, wf["w_glu"], wf["w_ssm_branch"], wf["w_attn_branch"], wf["w_out"],
                           t["merge"])
    a, dfo, dx2, loss_blk, dg4 = _mlp_fwd(h2, x1, target, g4, wf_ff_in, wf_ff_out, t["mlp_fwd"])

    groups = {"ff": 4, "merge": 1, "w_in": 2}

    def add_sibling(group, blocks, received):
        return _add_sibling(blocks, received, core, groups[group], "add_sibling_" + group)

    ff_names = ("w_ff_in", "w_ff_out")
    dw_ff_in, dw_ff_out, da = _mlp_weight_grads(dfo, a, h2, wf_ff_out, t["mlp_bwd"])
    dh2 = _mlp_input_grad(da, wf_ff_in.reshape(D_FF, D_MODEL), t["mlp_bwd"])
    ff_blocks = [dw_ff_in, dw_ff_out]
    (dx1, dgates, datt, dy, dw_glu, dw_ssm, dw_attn, dw_out, dg2, dg3), ff_recv = _merge_bwd(
        dh2, dx2, x1, o, y, att, ga, gs, g2, g3, wf["w_glu"], wf["w_ssm_branch"], wf["w_attn_branch"], wf["w_out"],
        t["merge_bwd"], _sibling_carry(ff_blocks))
    ff_sums, ff_sums_bf = add_sibling("ff", ff_blocks, ff_recv)
    merge_blocks = [_blocks_from_full(n, g) for n, g in zip(merge_names, (dw_glu, dw_attn, dw_ssm, dw_out))]
    (du, dbm_re, dbm_im, dcm_re, dcm_im, da_re, da_im, dd_skip), carried = _ssm_bwd(
        dy, u, h_re, h_im, in_re, in_im, a_re, a_im, bm_re, bm_im, cm_re, cm_im, dsk, t["ssm_chunk"],
        _join(_chips_carry(ff_sums_bf), _sibling_carry(merge_blocks)))
    ff_from_chips, merge_recv = carried[:2], carried[2:]
    merge_sums, merge_sums_bf = add_sibling("merge", merge_blocks, merge_recv)
    dbd_re, dbd_im, dlam_re, dlam_im, dldt_rep = _ssm_prep_bwd(lam_r, lam_i, ldt_rep, bd_re, bd_im, dbm_re, dbm_im, da_re, da_im)
    dlog_dt = _group_sum(dldt_rep.reshape(N_GROUPS, N_STATE))
    shapes = {n: view(n, w[n]).shape for n in SMALL}
    shapes["loss"] = (1,)
    small_grads = dict(
        norm_mix_post=dg2, norm_mlp_pre=dg3, norm_mlp_post=dg4, lam_re=dlam_re, lam_im=dlam_im, log_dt=dlog_dt,
        b_re=_block_diag_in_grad(dbd_re), b_im=_block_diag_in_grad(dbd_im),
        c_re=_block_diag_out_grad(dcm_re), c_im=_block_diag_out_grad(dcm_im), d_skip=dd_skip)
    packed_early = _pack({n: small_grads[n].reshape(shapes[n]) for n in SMALL_BEFORE_ATTN_BWD}, SMALL_BEFORE_ATTN_BWD)
    (dq, dkv, attn_small), carried = _attn_bwd(
        q, k, vv, datt, bucket, rel_b, sink, _join(_chips_carry(merge_sums_bf), _gather_carry([packed_early])))
    merge_from_chips, partials_early = carried[:-1], carried[-1]

    dparts = (dq, dkv, du, dgates)
    dw_in_t = _in_proj_weight_grad(h, dparts)
    in_blocks = [dw_in_t.reshape(N_DEV, IN_W // N_DEV, D_MODEL)]
    to_sibling = _sibling_carry(in_blocks)
    in_flight, token = _exchange_start(to_sibling, "w_in_sibling_start")
    n_tiles = xs.shape[0] // t["proj_bwd"]
    (grad_x, dg1), _ = _in_proj_input_grad(xs, g1 + token[0:1, 0:1], wf_in, dx1, dparts, t["proj_bwd"], 0, n_tiles, "in_proj_input_grad")
    in_recv = _exchange_wait(to_sibling, in_flight, dg1, "w_in_sibling_wait")
    in_sums, in_sums_bf = add_sibling("w_in", in_blocks, in_recv)
    to_chips = _chips_carry(in_sums_bf)
    in_flight, token = _exchange_start(to_chips, "w_in_chips_start")
    late = dict(norm_mix_pre=dg1, rel_bias=attn_small[:, :N_BUCKETS, 0], sinks=attn_small[:, N_BUCKETS, 0],
                loss=loss_blk[0:1, 0] + token[0, 0:1])
    packed_late = _pack({n: late[n].reshape(shapes[n]) for n in SMALL_LATE}, SMALL_LATE)
    (partials_late,) = _run_carry(_gather_carry([packed_late]), "gather_late_grads")

    grads, deltas, new_m, new_v = {}, {}, {}, {}

    def adam_group(group, names, sums, received):
        outs = _adam_big(*[[local(d, n) for n in names] for d in (w, m, v)], sums, received, chip, groups[group], "adam_" + group)
        for store, vals in zip((grads, deltas, new_m, new_v), outs):
            store.update({n: (o.T if n == "w_in" else o)[None] for n, o in zip(names, vals)})

    adam_group("ff", ff_names, ff_sums, ff_from_chips)
    adam_group("merge", merge_names, merge_sums, merge_from_chips)

    grads.update(_unpack(_sum_partials(partials_early, "sum_small_grads"), shapes, SMALL_BEFORE_ATTN_BWD))
    grads.update(_unpack(_sum_partials(partials_late, "sum_late_grads"), shapes, SMALL_LATE))
    loss = grads.pop("loss").reshape(())
    small_out = _adam_small(*[[view(n, d[n]) for n in SMALL] for d in (w, m, v)], [grads[n] for n in SMALL])
    for store, vals in zip((deltas, new_m, new_v), small_out):
        store.update(zip(SMALL, vals))
    for store in (grads, deltas, new_m, new_v):
        store.update({n: view(n, store[n]) for n in SWAPPED_SMALL})

    (in_from_chips,) = _exchange_wait(to_chips, in_flight, deltas["norm_mix_pre"], "w_in_chips_wait")
    adam_group("w_in", ("w_in",), in_sums, [in_from_chips])

    return (loss, grad_x[None], *[grads[n] for n in ALL_WEIGHTS], *[deltas[n] for n in ALL_WEIGHTS],
            *[new_m[n] for n in ALL_WEIGHTS], *[new_v[n] for n in ALL_WEIGHTS])
```

```python
import math

import jax
import jax.numpy as jnp
import numpy as np
from jax import lax
from jax.experimental import pallas as pl
from jax.experimental.pallas import tpu as pltpu

F32 = jnp.float32
BF16 = jnp.bfloat16

D_MODEL = 1024
N_HEADS = 8
HEAD_DIM = 64
ATTN_W = 512
KV_W = 128
BLOCK = 128
N_BUCKETS = 32
SSM_W = 512
N_GROUPS = 32
N_STATE = 64
GROUP_CH = 16
STATES = N_GROUPS * N_STATE
D_FF = 4096
IN_W = 3328
SPLITS = (0, 512, 640, 768, 1280, 2304, 3328)
RMS_EPS = 1e-6
NEG_INF = -1e30
SUBLANES = 8
LANES = 128
SSM_LANE_BLOCK = 512
N_SSM_BLOCKS = STATES // SSM_LANE_BLOCK
GROUPS_PER_BLOCK = SSM_LANE_BLOCK // N_STATE
VMEM_BIG = 52 * 1024 * 1024
VMEM_MID = 40 * 1024 * 1024
VMEM_MAX = 60 * 1024 * 1024

ADAM_LR = 0.001
ADAM_B1 = 0.9
ADAM_B2 = 0.999
ADAM_EPS = 1e-08
ADAM_WD = 0.01
ADAM_STEP = 10

N_DEV = 8


def _dot(a, b):
    return jnp.dot(a, b, preferred_element_type=F32)


def _dot_nt(a, b):
    return lax.dot_general(a, b, (((1,), (1,)), ((), ())), preferred_element_type=F32)


def _dot_tn(a, b):
    return lax.dot_general(a, b, (((0,), (0,)), ((), ())), preferred_element_type=F32)


def _rms_scale(x):
    return lax.rsqrt(jnp.mean(x * x, axis=-1, keepdims=True) + RMS_EPS)


def _rms_bwd(dy, x, r, g):
    t = dy * g
    dx = r * t - x * (r * r * r) * jnp.mean(t * x, axis=-1, keepdims=True)
    dg = jnp.sum(dy * x * r, axis=0, keepdims=True)
    return dx, dg


def _const_spec(shape):
    nd = len(shape)
    return pl.BlockSpec(shape, lambda *_: (0,) * nd, pipeline_mode=pl.Buffered(1))


def _in_hbm(*arrays):
    return tuple(pltpu.with_memory_space_constraint(a, pltpu.HBM) for a in arrays)


def _hbm_out(shapes):
    if isinstance(shapes, (list, tuple)):
        return [_hbm_out(s) for s in shapes]
    return shapes if isinstance(shapes, pl.MemoryRef) else pltpu.HBM(shapes.shape, shapes.dtype)


def _whole(shape):
    nd = len(shape)
    return pl.BlockSpec(shape, lambda *_: (0,) * nd)


def _params(sem, vmem=None):
    return pltpu.CompilerParams(dimension_semantics=sem, vmem_limit_bytes=vmem)


MESH_IDS = pl.DeviceIdType.MESH
HBM_SPEC = pl.BlockSpec(memory_space=pl.ANY)


class _Carry:
    def __init__(self, inputs, out_shapes, sems, start, finish):
        self.inputs, self.out_shapes, self.sems, self.start, self.finish = list(inputs), list(out_shapes), list(sems), start, finish


def _join(a, b):
    na_in, na_out, na_sem = len(a.inputs), len(a.out_shapes), len(a.sems)

    def start(ins, outs, sems):
        a.start(ins[:na_in], outs[:na_out], sems[:na_sem])
        b.start(ins[na_in:], outs[na_out:], sems[na_sem:])

    def finish(ins, outs, sems):
        a.finish(ins[:na_in], outs[:na_out], sems[:na_sem])
        b.finish(ins[na_in:], outs[na_out:], sems[na_sem:])

    return _Carry(a.inputs + b.inputs, a.out_shapes + b.out_shapes, a.sems + b.sems, start, finish)


def _hosted_call(body, carry, edge, *, name, grid, in_specs, out_specs, out_shape, scratch_shapes, compiler_params, inputs):
    n_in, n_out = len(in_specs), len(out_specs)
    inputs = [a if s.memory_space == pltpu.SMEM else _in_hbm(a)[0] for a, s in zip(inputs, in_specs)]
    out_shape = _hbm_out(list(out_shape))
    if carry is None:
        outs = pl.pallas_call(body, name=name, grid=grid, in_specs=in_specs, out_specs=out_specs, out_shape=out_shape,
                              scratch_shapes=scratch_shapes, compiler_params=compiler_params)(*inputs)
        return list(outs), []
    c_in, c_out, c_sem = len(carry.inputs), len(carry.out_shapes), len(carry.sems)

    def wrapped(*refs):
        ins, refs = refs[:n_in], refs[n_in:]
        cins, refs = refs[:c_in], refs[c_in:]
        outs, refs = refs[:n_out], refs[n_out:]
        couts, refs = refs[:c_out], refs[c_out:]
        scratch, csems = refs[:len(refs) - c_sem], refs[len(refs) - c_sem:]
        first, last = edge()

        @pl.when(first)
        def _():
            carry.start(cins, couts, csems)

        body(*ins, *outs, *scratch)

        @pl.when(last)
        def _():
            carry.finish(cins, couts, csems)

    outs = pl.pallas_call(
        wrapped, name=name, grid=grid, in_specs=list(in_specs) + [HBM_SPEC] * c_in,
        out_specs=list(out_specs) + [HBM_SPEC] * c_out, out_shape=out_shape + _hbm_out(carry.out_shapes),
        scratch_shapes=list(scratch_shapes) + carry.sems, compiler_params=compiler_params)(*inputs, *_in_hbm(*carry.inputs))
    return list(outs[:n_out]), list(outs[n_out:])


def _edge_1d(n_steps):
    return lambda: (pl.program_id(0) == 0, pl.program_id(0) == n_steps - 1)


def _edge_2d(n0, n1):
    return lambda: ((pl.program_id(0) == 0) & (pl.program_id(1) == 0),
                    (pl.program_id(0) == n0 - 1) & (pl.program_id(1) == n1 - 1))


def _run_carry(carry, name):
    c_in, c_out = len(carry.inputs), len(carry.out_shapes)

    def body(*refs):
        ins, outs, sems = refs[:c_in], refs[c_in:c_in + c_out], refs[c_in + c_out:]
        carry.start(ins, outs, sems)
        carry.finish(ins, outs, sems)

    return pl.pallas_call(body, name=name, in_specs=[HBM_SPEC] * c_in, out_specs=[HBM_SPEC] * c_out,
                          out_shape=_hbm_out(carry.out_shapes), scratch_shapes=carry.sems)(*_in_hbm(*carry.inputs))


def _in_proj_fwd(x, g1, w_in_t, tile, carry=None):
    T = x.shape[0]

    def body(x_ref, g_ref, w_ref, q_ref, k_ref, v_ref, u_ref, ga_ref, gs_ref, h_ref):
        xv = x_ref[...]
        h = (xv * _rms_scale(xv) * g_ref[...]).astype(BF16)
        h_ref[...] = h
        outs = (q_ref, k_ref, v_ref, u_ref, ga_ref, gs_ref)
        for p, o_ref in enumerate(outs):
            o_ref[...] = _dot_nt(h, w_ref[SPLITS[p]:SPLITS[p + 1], :]).astype(o_ref.dtype)

    widths = [SPLITS[p + 1] - SPLITS[p] for p in range(6)] + [D_MODEL]
    dtypes = [BF16, BF16, BF16, F32, F32, F32, BF16]
    return _hosted_call(
        body, carry, _edge_1d(T // tile), name="in_proj_fwd", grid=(T // tile,),
        in_specs=[pl.BlockSpec((tile, D_MODEL), lambda i: (i, 0)), _const_spec((1, D_MODEL)), _const_spec((IN_W, D_MODEL))],
        out_specs=[pl.BlockSpec((tile, w), lambda i: (i, 0)) for w in widths],
        out_shape=[jax.ShapeDtypeStruct((T, w), dt) for w, dt in zip(widths, dtypes)],
        scratch_shapes=[], compiler_params=_params(("arbitrary",), VMEM_MID), inputs=(x, g1, w_in_t))


PROJ_PARTS = (512, 256, 512, 2048)
PROJ_GRAD_BLOCK = 256


def _in_proj_weight_grad(h, dparts):
    T = h.shape[0]
    blocks = [wd // PROJ_GRAD_BLOCK for wd in PROJ_PARTS]
    starts = [sum(blocks[:p]) for p in range(len(blocks))]

    def body(h_ref, *refs):
        part_refs, o_ref = refs[:-1], refs[-1]
        j = pl.program_id(0)
        for p_ref, start, count in zip(part_refs, starts, blocks):
            @pl.when((j >= start) & (j < start + count))
            def _(p_ref=p_ref):
                o_ref[...] = _dot_tn(p_ref[...], h_ref[...])

    def part_spec(start, count):
        return pl.BlockSpec((T, PROJ_GRAD_BLOCK), lambda j: (0, jnp.clip(j - start, 0, count - 1)))

    return pl.pallas_call(
        body, name="in_proj_weight_grad", grid=(sum(blocks),),
        in_specs=[_const_spec((T, D_MODEL))] + [part_spec(s, c) for s, c in zip(starts, blocks)],
        out_specs=pl.BlockSpec((PROJ_GRAD_BLOCK, D_MODEL), lambda j: (j, 0)),
        out_shape=_hbm_out(jax.ShapeDtypeStruct((IN_W, D_MODEL), F32)),
        compiler_params=_params(("arbitrary",), VMEM_MID),
    )(*_in_hbm(h, *dparts))


def _in_proj_input_grad(x, g1, w_in_t, dx1, dparts, tile, first_tile, n_tiles, name, carry=None):
    offsets = [sum(PROJ_PARTS[:p]) for p in range(len(PROJ_PARTS))]

    def body(x_ref, g_ref, w_ref, dx1_ref, *refs):
        part_refs, (gx_ref, dg_ref) = refs[:len(PROJ_PARTS)], refs[len(PROJ_PARTS):]
        i = pl.program_id(0)
        xv = x_ref[...]
        r = _rms_scale(xv)
        g = g_ref[...]
        dh = sum(_dot(p_ref[...], w_ref[off:off + wd, :]) for p_ref, off, wd in zip(part_refs, offsets, PROJ_PARTS))
        dxn, dg = _rms_bwd(dh, xv, r, g)
        gx_ref[...] = dx1_ref[...] + dxn

        @pl.when(i == 0)
        def _():
            dg_ref[...] = dg

        @pl.when(i > 0)
        def _():
            dg_ref[...] += dg

    tok = lambda wd: pl.BlockSpec((tile, wd), lambda i: (i + first_tile, 0))
    return _hosted_call(
        body, carry, _edge_1d(n_tiles), name=name, grid=(n_tiles,),
        in_specs=[tok(D_MODEL), _const_spec((1, D_MODEL)), _const_spec((IN_W, D_MODEL)), tok(D_MODEL)] + [tok(wd) for wd in PROJ_PARTS],
        out_specs=[pl.BlockSpec((tile, D_MODEL), lambda i: (i, 0)), pl.BlockSpec((1, D_MODEL), lambda i: (0, 0))],
        out_shape=[jax.ShapeDtypeStruct((n_tiles * tile, D_MODEL), F32), jax.ShapeDtypeStruct((1, D_MODEL), F32)],
        scratch_shapes=[], compiler_params=_params(("arbitrary",), VMEM_MID), inputs=(x, g1, w_in_t, dx1, *dparts))


def _bucket_table():
    qi = np.arange(BLOCK)[:, None]
    kj = np.arange(2 * BLOCK)[None, :]
    dist = qi + BLOCK - kj
    max_exact = N_BUCKETS // 2
    d = np.maximum(dist, 0)
    df = np.maximum(d, 1).astype(np.float32)
    large = max_exact + (np.log(df / np.float32(max_exact)) / np.float32(math.log(BLOCK / max_exact))
                         * np.float32(N_BUCKETS - max_exact)).astype(np.int32)
    large = np.minimum(large, N_BUCKETS - 1)
    bucket = np.where(d < max_exact, d, large)
    return np.where((dist >= 0) & (dist < BLOCK), bucket, -1).astype(np.int32)


def _build_bias(bucket_ref, rb_ref, bias_ref):
    bk = bucket_ref[...]
    for h in range(N_HEADS):
        def add(b, acc, h=h):
            return acc + jnp.where(bk == b, rb_ref[h, b], 0.0)
        bias_ref[h] = lax.fori_loop(0, N_BUCKETS, add, jnp.zeros((BLOCK, 2 * BLOCK), F32))


def _kv_variants(prev_ref, cur_ref):
    cat = jnp.concatenate([prev_ref[...], cur_ref[...]], axis=0)
    lo = lax.broadcasted_iota(jnp.int32, cat.shape, 1) < HEAD_DIM
    zero = jnp.zeros_like(cat)
    head0_lo = jnp.where(lo, cat, zero)
    head1_hi = jnp.where(lo, zero, cat)
    return ((head0_lo, pltpu.roll(head0_lo, HEAD_DIM, 1)), (pltpu.roll(head1_hi, HEAD_DIM, 1), head1_hi))


def _merge_kv_grads(g):
    lo = lax.broadcasted_iota(jnp.int32, g[0][0].shape, 1) < HEAD_DIM
    return jnp.where(lo, g[0][0] + pltpu.roll(g[0][1], HEAD_DIM, 1), g[1][1] + pltpu.roll(g[1][0], HEAD_DIM, 1))


def _head_lanes(h):
    return slice((h // 2) * LANES, (h // 2 + 1) * LANES)


def _attn_probs(q_ref, kvar, bias_ref, sk_ref, valid, s_ref):
    for h in range(N_HEADS):
        s_ref[h] = _dot_nt(q_ref[:, _head_lanes(h)], kvar[h // 4][h % 2])
    head = lax.broadcasted_iota(jnp.int32, (N_HEADS, 1, 1), 0)
    sink = jnp.zeros((N_HEADS, 1, 1), F32)
    for h in range(N_HEADS):
        sink = jnp.where(head == h, sk_ref[0, h], sink)
    s = jnp.where(valid[None], s_ref[...] * (HEAD_DIM ** -0.5) + bias_ref[...], NEG_INF)
    m = jnp.maximum(jnp.max(s, axis=-1, keepdims=True), sink)
    p = jnp.exp(s - m)
    e_sink = jnp.exp(sink - m)
    inv = 1.0 / (jnp.sum(p, axis=-1, keepdims=True) + e_sink)
    return p * inv, e_sink * inv


def _attn_valid(bucket_ref, n):
    col = lax.broadcasted_iota(jnp.int32, (BLOCK, 2 * BLOCK), 1)
    return (bucket_ref[...] >= 0) & ((n > 0) | (col >= BLOCK))


def _attn_fwd(q, k, v, bucket, rel_bias, sinks, carry=None):
    T = q.shape[0]
    nb = T // BLOCK

    def body(q_ref, kc_ref, kp_ref, vc_ref, vp_ref, bucket_ref, rb_ref, sk_ref, o_ref, bias_ref, s_ref, p_ref):
        n = pl.program_id(0)

        @pl.when(n == 0)
        def _():
            _build_bias(bucket_ref, rb_ref, bias_ref)

        kvar = _kv_variants(kp_ref, kc_ref)
        vvar = _kv_variants(vp_ref, vc_ref)
        pr, _ = _attn_probs(q_ref, kvar, bias_ref, sk_ref, _attn_valid(bucket_ref, n), s_ref)
        p_ref[...] = pr.astype(BF16)
        for m in range(N_HEADS // 2):
            acc = _dot(p_ref[2 * m], vvar[m // 2][0]) + _dot(p_ref[2 * m + 1], vvar[m // 2][1])
            o_ref[:, m * LANES:(m + 1) * LANES] = acc.astype(o_ref.dtype)

    cur = lambda w: pl.BlockSpec((BLOCK, w), lambda n: (n, 0))
    prev = lambda w: pl.BlockSpec((BLOCK, w), lambda n: (jnp.maximum(n - 1, 0), 0))
    smem = pl.BlockSpec(memory_space=pltpu.SMEM)
    return _hosted_call(
        body, carry, _edge_1d(nb), name="attn_fwd", grid=(nb,),
        in_specs=[cur(ATTN_W), cur(KV_W), prev(KV_W), cur(KV_W), prev(KV_W), _const_spec((BLOCK, 2 * BLOCK)), smem, smem],
        out_specs=[cur(ATTN_W)],
        out_shape=[jax.ShapeDtypeStruct((T, ATTN_W), BF16)],
        scratch_shapes=[pltpu.VMEM((N_HEADS, BLOCK, 2 * BLOCK), F32), pltpu.VMEM((N_HEADS, BLOCK, 2 * BLOCK), F32),
                        pltpu.VMEM((N_HEADS, BLOCK, 2 * BLOCK), BF16)],
        compiler_params=_params(("arbitrary",)), inputs=(q, k, k, v, v, bucket, rel_bias, sinks))


ATTN_SMALL_ROWS = N_BUCKETS + SUBLANES


def _attn_bwd(q, k, v, datt, bucket, rel_bias, sinks, carry=None):
    T = q.shape[0]
    nb = T // BLOCK

    def body(q_ref, do_ref, kc_ref, kp_ref, vc_ref, vp_ref, bucket_ref, rb_ref, sk_ref,
             dq_ref, dkv_ref, small_ref, bias_ref, ds_sum_ref, dsink_ref, kcarry_ref, vcarry_ref,
             s_ref, dp_ref, p_ref, dsc_ref):
        n = pl.program_id(0)

        @pl.when(n == 0)
        def _():
            _build_bias(bucket_ref, rb_ref, bias_ref)
            ds_sum_ref[...] = jnp.zeros_like(ds_sum_ref)
            dsink_ref[...] = jnp.zeros_like(dsink_ref)
            kcarry_ref[...] = jnp.zeros_like(kcarry_ref)
            vcarry_ref[...] = jnp.zeros_like(vcarry_ref)

        @pl.when(n < nb)
        def _():
            kvar = _kv_variants(kp_ref, kc_ref)
            vvar = _kv_variants(vp_ref, vc_ref)
            pr, p_sink = _attn_probs(q_ref, kvar, bias_ref, sk_ref, _attn_valid(bucket_ref, n), s_ref)
            for h in range(N_HEADS):
                dp_ref[h] = _dot_nt(do_ref[:, _head_lanes(h)], vvar[h // 4][h % 2])
            dp = dp_ref[...]
            dsum = jnp.sum(pr * dp, axis=-1, keepdims=True)
            ds = pr * (dp - dsum)
            ds_sum_ref[...] += ds
            dsink_ref[...] -= jnp.sum(p_sink * dsum, axis=1, keepdims=True)
            dsc_ref[...] = (ds * (HEAD_DIM ** -0.5)).astype(BF16)
            p_ref[...] = pr.astype(BF16)
            for m in range(N_HEADS // 2):
                dqm = _dot(dsc_ref[2 * m], kvar[m // 2][0]) + _dot(dsc_ref[2 * m + 1], kvar[m // 2][1])
                dq_ref[:, m * LANES:(m + 1) * LANES] = dqm.astype(dq_ref.dtype)
            dk_var = [[None, None], [None, None]]
            dv_var = [[None, None], [None, None]]
            for kvh in range(2):
                for e in range(2):
                    heads = [h for h in range(N_HEADS) if h // 4 == kvh and h % 2 == e]
                    dk_var[kvh][e] = sum(_dot_tn(dsc_ref[h], q_ref[:, _head_lanes(h)]) for h in heads)
                    dv_var[kvh][e] = sum(_dot_tn(p_ref[h], do_ref[:, _head_lanes(h)]) for h in heads)
            dk_cat = _merge_kv_grads(dk_var)
            dv_cat = _merge_kv_grads(dv_var)

            @pl.when(n > 0)
            def _():
                dkv_ref[:, :KV_W] = (kcarry_ref[...] + dk_cat[:BLOCK]).astype(BF16)
                dkv_ref[:, KV_W:] = (vcarry_ref[...] + dv_cat[:BLOCK]).astype(BF16)

            kcarry_ref[...] = dk_cat[BLOCK:]
            vcarry_ref[...] = dv_cat[BLOCK:]

        @pl.when(n == nb)
        def _():
            dkv_ref[:, :KV_W] = kcarry_ref[...].astype(BF16)
            dkv_ref[:, KV_W:] = vcarry_ref[...].astype(BF16)
            bk = bucket_ref[...]
            row = lax.broadcasted_iota(jnp.int32, (N_HEADS, ATTN_SMALL_ROWS, LANES), 1)

            def add(b, acc):
                masked = jnp.where((bk == b)[None], ds_sum_ref[...], 0.0)
                val = jnp.sum(jnp.sum(masked, axis=1, keepdims=True), axis=2, keepdims=True)
                return acc + jnp.where(row == b, val, 0.0)

            small_ref[...] = lax.fori_loop(0, N_BUCKETS, add, jnp.where(row == N_BUCKETS, dsink_ref[...], 0.0))

    last = nb - 1
    cur = lambda w: pl.BlockSpec((BLOCK, w), lambda n: (jnp.minimum(n, last), 0))
    prev = lambda w: pl.BlockSpec((BLOCK, w), lambda n: (jnp.clip(n - 1, 0, last), 0))
    smem = pl.BlockSpec(memory_space=pltpu.SMEM)
    return _hosted_call(
        body, carry, _edge_1d(nb + 1), name="attn_bwd", grid=(nb + 1,),
        in_specs=[cur(ATTN_W), cur(ATTN_W), cur(KV_W), prev(KV_W), cur(KV_W), prev(KV_W),
                  _const_spec((BLOCK, 2 * BLOCK)), smem, smem],
        out_specs=[cur(ATTN_W), prev(2 * KV_W), pl.BlockSpec((N_HEADS, ATTN_SMALL_ROWS, LANES), lambda n: (0, 0, 0))],
        out_shape=[jax.ShapeDtypeStruct((T, ATTN_W), BF16), jax.ShapeDtypeStruct((T, 2 * KV_W), BF16),
                   jax.ShapeDtypeStruct((N_HEADS, ATTN_SMALL_ROWS, LANES), F32)],
        scratch_shapes=[pltpu.VMEM((N_HEADS, BLOCK, 2 * BLOCK), F32), pltpu.VMEM((N_HEADS, BLOCK, 2 * BLOCK), F32),
                        pltpu.VMEM((N_HEADS, 1, 1), F32), pltpu.VMEM((BLOCK, KV_W), F32), pltpu.VMEM((BLOCK, KV_W), F32),
                        pltpu.VMEM((N_HEADS, BLOCK, 2 * BLOCK), F32), pltpu.VMEM((N_HEADS, BLOCK, 2 * BLOCK), F32),
                        pltpu.VMEM((N_HEADS, BLOCK, 2 * BLOCK), BF16), pltpu.VMEM((N_HEADS, BLOCK, 2 * BLOCK), BF16)],
        compiler_params=_params(("arbitrary",)), inputs=(q, datt, k, k, v, v, bucket, rel_bias, sinks))


SCAN_UNROLL = 4


def _cmul(ar, ai, br, bi):
    return ar * br - ai * bi, ar * bi + ai * br


def _cmul_conj(ar, ai, br, bi):
    return ar * br + ai * bi, ar * bi - ai * br


def _ssm_discretize(lr, li, ldt):
    dt = jnp.exp(ldt)
    mag = jnp.exp(lr * dt)
    ab_re = mag * jnp.cos(li * dt)
    ab_im = mag * jnp.sin(li * dt)
    nr = ab_re - 1.0
    den = lr * lr + li * li
    f_re = (nr * lr + ab_im * li) / den
    f_im = (ab_im * lr - nr * li) / den
    return ab_re, ab_im, f_re, f_im


def _ssm_prep(lam_re, lam_im, ldt_rep, bd_re, bd_im):
    def body(lr_ref, li_ref, ldt_ref, bdr_ref, bdi_ref, ar_ref, ai_ref, br_ref, bi_ref):
        ab_re, ab_im, f_re, f_im = _ssm_discretize(lr_ref[...], li_ref[...], ldt_ref[...])
        ar_ref[...] = ab_re
        ai_ref[...] = ab_im
        bdr, bdi = bdr_ref[0], bdi_ref[0]
        br_ref[0] = (bdr * f_re - bdi * f_im).astype(BF16)
        bi_ref[0] = (bdi * f_re + bdr * f_im).astype(BF16)

    row = pl.BlockSpec((1, SSM_LANE_BLOCK), lambda j: (0, j))
    mat = pl.BlockSpec((1, LANES, SSM_LANE_BLOCK), lambda j: (j, 0, 0))
    return pl.pallas_call(
        body, name="ssm_prep", grid=(N_SSM_BLOCKS,),
        in_specs=[row, row, row, mat, mat], out_specs=[row, row, mat, mat],
        out_shape=[jax.ShapeDtypeStruct((1, STATES), F32)] * 2 + [jax.ShapeDtypeStruct((N_SSM_BLOCKS, LANES, SSM_LANE_BLOCK), BF16)] * 2,
        compiler_params=_params(("arbitrary",)),
    )(*_in_hbm(lam_re, lam_im, ldt_rep, bd_re, bd_im))


def _ssm_prep_bwd(lam_re, lam_im, ldt_rep, bd_re, bd_im, dbr, dbi, da_re, da_im):
    def body(lr_ref, li_ref, ldt_ref, bdr_ref, bdi_ref, dbr_ref, dbi_ref, dar_ref, dai_ref,
             dbdr_ref, dbdi_ref, dlr_ref, dli_ref, dldt_ref):
        lr, li, ldt = lr_ref[...], li_ref[...], ldt_ref[...]
        (_, _, f_re, f_im), vjp = jax.vjp(_ssm_discretize, lr, li, ldt)
        bdr, bdi, gbr, gbi = bdr_ref[0], bdi_ref[0], dbr_ref[0], dbi_ref[0]
        dbdr_ref[0] = gbr * f_re + gbi * f_im
        dbdi_ref[0] = gbi * f_re - gbr * f_im
        df_re = jnp.sum(gbr * bdr + gbi * bdi, axis=0, keepdims=True)
        df_im = jnp.sum(gbi * bdr - gbr * bdi, axis=0, keepdims=True)
        dlr, dli, dldt = vjp((dar_ref[...], dai_ref[...], df_re, df_im))
        dlr_ref[...] = dlr
        dli_ref[...] = dli
        dldt_ref[...] = dldt

    row = pl.BlockSpec((1, SSM_LANE_BLOCK), lambda j: (0, j))
    mat = pl.BlockSpec((1, LANES, SSM_LANE_BLOCK), lambda j: (j, 0, 0))
    mat_shape = jax.ShapeDtypeStruct((N_SSM_BLOCKS, LANES, SSM_LANE_BLOCK), F32)
    row_shape = jax.ShapeDtypeStruct((1, STATES), F32)
    return pl.pallas_call(
        body, name="ssm_prep_bwd", grid=(N_SSM_BLOCKS,),
        in_specs=[row, row, row, mat, mat, mat, mat, row, row], out_specs=[mat, mat, row, row, row],
        out_shape=[mat_shape, mat_shape, row_shape, row_shape, row_shape],
        compiler_params=_params(("arbitrary",)),
    )(*_in_hbm(lam_re, lam_im, ldt_rep, bd_re, bd_im, dbr, dbi, da_re, da_im))


def _group_sum(x):
    def body(x_ref, o_ref):
        o_ref[...] = jnp.sum(x_ref[...], axis=1, keepdims=True)
    return pl.pallas_call(body, name="ssm_group_sum", grid=(1,), in_specs=[_whole(x.shape)], out_specs=_whole((N_GROUPS, 1)),
                          out_shape=jax.ShapeDtypeStruct((N_GROUPS, 1), F32))(*_in_hbm(x))


def _power_table(ar, ai, p_re_ref, p_im_ref, steps):
    shape = (SUBLANES, SSM_LANE_BLOCK)
    p_re_ref[0:SUBLANES] = jnp.broadcast_to(ar, shape)
    p_im_ref[0:SUBLANES] = jnp.broadcast_to(ai, shape)
    m = 1
    while m < steps:
        rows = m * SUBLANES
        top_re = p_re_ref[rows - SUBLANES:rows]
        top_im = p_im_ref[rows - SUBLANES:rows]
        cur_re = p_re_ref[0:rows].reshape(m, SUBLANES, SSM_LANE_BLOCK)
        cur_im = p_im_ref[0:rows].reshape(m, SUBLANES, SSM_LANE_BLOCK)
        nxt_re, nxt_im = _cmul(cur_re, cur_im, top_re[None], top_im[None])
        p_re_ref[rows:2 * rows] = nxt_re.reshape(rows, SSM_LANE_BLOCK)
        p_im_ref[rows:2 * rows] = nxt_im.reshape(rows, SSM_LANE_BLOCK)
        m *= 2


def _to_segments(src_ref, dst_ref, steps):
    for s in range(SUBLANES):
        dst_ref[pl.ds(s, steps, stride=SUBLANES), :] = src_ref[s * steps:(s + 1) * steps, :]


def _from_segments(src_ref, dst_ref, steps):
    for s in range(SUBLANES):
        dst_ref[s * steps:(s + 1) * steps, :] = src_ref[pl.ds(s, steps, stride=SUBLANES), :]


def _segment_carries(e_re, e_im, an_re, an_im, c_re, c_im, reverse):
    order = range(SUBLANES - 1, -1, -1) if reverse else range(SUBLANES)
    ins_re, ins_im = [None] * SUBLANES, [None] * SUBLANES
    for s in order:
        ins_re[s], ins_im[s] = c_re, c_im
        pr, pi = _cmul(an_re, an_im, c_re, c_im)
        c_re = e_re[s:s + 1] + pr
        c_im = e_im[s:s + 1] + pi
    return jnp.concatenate(ins_re, axis=0), jnp.concatenate(ins_im, axis=0), c_re, c_im


def _ssm_fwd(u, a_re, a_im, b_re, b_im, c_re, c_im, d_skip, chunk, carry=None):
    T = u.shape[0]
    nc = T // chunk
    steps = chunk // SUBLANES
    blk = SSM_LANE_BLOCK

    def body(u_ref, ar_ref, ai_ref, br_ref, bi_ref, cr_ref, ci_ref, dk_ref,
             y_ref, hr_ref, hi_ref, inr_ref, ini_ref, useg_ref, yseg_ref, pr_ref, pi_ref, carry_ref):
        c = pl.program_id(1)
        ar, ai = ar_ref[...], ai_ref[...]

        @pl.when(c == 0)
        def _():
            _power_table(ar, ai, pr_ref, pi_ref, steps)
            carry_ref[...] = jnp.zeros_like(carry_ref)

        _to_segments(u_ref, useg_ref, steps)
        ub = useg_ref[...].astype(BF16)
        hr_ref[...] = _dot(ub, br_ref[0])
        hi_ref[...] = _dot(ub, bi_ref[0])
        first = slice(0, SUBLANES)

        def scan(t4, prev):
            for j in range(SCAN_UNROLL):
                rows = pl.ds(pl.multiple_of((t4 * SCAN_UNROLL + j) * SUBLANES, SUBLANES), SUBLANES)
                pr, pi = _cmul(pr_ref[first, :], pi_ref[first, :], prev[0], prev[1])
                prev = (pr + hr_ref[rows, :], pi + hi_ref[rows, :])
                hr_ref[rows, :] = prev[0]
                hi_ref[rows, :] = prev[1]
            return prev

        zero = jnp.zeros((SUBLANES, blk), F32)
        lax.fori_loop(0, steps // SCAN_UNROLL, scan, (zero, zero))

        top = slice(chunk - SUBLANES, chunk)
        in_re, in_im, out_re, out_im = _segment_carries(
            hr_ref[top, :], hi_ref[top, :], pr_ref[top, :][0:1], pi_ref[top, :][0:1],
            carry_ref[0:1, :], carry_ref[1:2, :], reverse=False)
        carry_ref[0:1, :] = out_re
        carry_ref[1:2, :] = out_im
        inr_ref[...] = in_re
        ini_ref[...] = in_im

        def fix(t4, _):
            for j in range(SCAN_UNROLL):
                rows = pl.ds(pl.multiple_of((t4 * SCAN_UNROLL + j) * SUBLANES, SUBLANES), SUBLANES)
                fr, fi = _cmul(pr_ref[rows, :], pi_ref[rows, :], in_re, in_im)
                hr_ref[rows, :] += fr
                hi_ref[rows, :] += fi
            return 0

        lax.fori_loop(0, steps // SCAN_UNROLL, fix, 0)

        yseg_ref[...] = _dot(hr_ref[...].astype(BF16), cr_ref[0]) - _dot(hi_ref[...].astype(BF16), ci_ref[0])
        _from_segments(yseg_ref, y_ref, steps)
        y_ref[...] += dk_ref[...] * u_ref[...]

    row = pl.BlockSpec((1, blk), lambda j, c: (0, j))
    b_mat = pl.BlockSpec((1, LANES, blk), lambda j, c: (j, 0, 0))
    c_mat = pl.BlockSpec((1, blk, LANES), lambda j, c: (j, 0, 0))
    tok = pl.BlockSpec((chunk, LANES), lambda j, c: (c, j))
    state = pl.BlockSpec((chunk, blk), lambda j, c: (c, j))
    enter = pl.BlockSpec((SUBLANES, blk), lambda j, c: (c, j))
    return _hosted_call(
        body, carry, _edge_2d(N_SSM_BLOCKS, nc), name="ssm_fwd", grid=(N_SSM_BLOCKS, nc),
        in_specs=[tok, row, row, b_mat, b_mat, c_mat, c_mat, pl.BlockSpec((1, LANES), lambda j, c: (0, j))],
        out_specs=[tok, state, state, enter, enter],
        out_shape=[jax.ShapeDtypeStruct((T, SSM_W), F32), jax.ShapeDtypeStruct((T, STATES), F32),
                   jax.ShapeDtypeStruct((T, STATES), F32), jax.ShapeDtypeStruct((nc * SUBLANES, STATES), F32),
                   jax.ShapeDtypeStruct((nc * SUBLANES, STATES), F32)],
        scratch_shapes=[pltpu.VMEM((chunk, LANES), F32), pltpu.VMEM((chunk, LANES), F32),
                        pltpu.VMEM((chunk, blk), F32), pltpu.VMEM((chunk, blk), F32), pltpu.VMEM((SUBLANES, blk), F32)],
        compiler_params=_params(("arbitrary", "arbitrary"), VMEM_MID),
        inputs=(u, a_re, a_im, b_re, b_im, c_re, c_im, d_skip))


def _ssm_bwd(dy, u, h_re, h_im, in_re, in_im, a_re, a_im, b_re, b_im, c_re, c_im, d_skip, chunk, carry=None):
    T = u.shape[0]
    nc = T // chunk
    steps = chunk // SUBLANES
    blk = SSM_LANE_BLOCK

    def body(dy_ref, u_ref, hr_ref, hi_ref, inr_ref, ini_ref, ar_ref, ai_ref, br_ref, bi_ref, cr_ref, ci_ref, dk_ref,
             du_ref, dbr_ref, dbi_ref, dcr_ref, dci_ref, dar_ref, dai_ref, ddk_ref,
             dyseg_ref, useg_ref, duseg_ref, gr_ref, gi_ref, pr_ref, pi_ref, carry_ref, accr_ref, acci_ref):
        c = pl.program_id(1)
        ar, ai = ar_ref[...], ai_ref[...]

        @pl.when(c == 0)
        def _():
            _power_table(ar, ai, pr_ref, pi_ref, steps)
            carry_ref[...] = jnp.zeros_like(carry_ref)
            accr_ref[...] = jnp.zeros_like(accr_ref)
            acci_ref[...] = jnp.zeros_like(acci_ref)

        _to_segments(dy_ref, dyseg_ref, steps)
        _to_segments(u_ref, useg_ref, steps)
        dyb = dyseg_ref[...].astype(BF16)
        ub = useg_ref[...].astype(BF16)
        gr_ref[...] = _dot_nt(dyb, cr_ref[0])
        gi_ref[...] = -_dot_nt(dyb, ci_ref[0])
        dcr = _dot_tn(hr_ref[...].astype(BF16), dyb)
        dci = -_dot_tn(hi_ref[...].astype(BF16), dyb)
        ddk = jnp.sum(dy_ref[...] * u_ref[...], axis=0, keepdims=True)

        first = slice(0, SUBLANES)

        def scan(k4, nxt):
            for j in range(SCAN_UNROLL):
                t = steps - 1 - (k4 * SCAN_UNROLL + j)
                rows = pl.ds(pl.multiple_of(t * SUBLANES, SUBLANES), SUBLANES)
                pr, pi = _cmul_conj(pr_ref[first, :], pi_ref[first, :], nxt[0], nxt[1])
                nxt = (pr + gr_ref[rows, :], pi + gi_ref[rows, :])
                gr_ref[rows, :] = nxt[0]
                gi_ref[rows, :] = nxt[1]
            return nxt

        top = slice(chunk - SUBLANES, chunk)
        zero = jnp.zeros((SUBLANES, blk), F32)
        lax.fori_loop(0, steps // SCAN_UNROLL, scan, (zero, zero))

        gin_re, gin_im, out_re, out_im = _segment_carries(
            gr_ref[0:SUBLANES, :], gi_ref[0:SUBLANES, :], pr_ref[top, :][0:1], -pi_ref[top, :][0:1],
            carry_ref[0:1, :], carry_ref[1:2, :], reverse=True)
        carry_ref[0:1, :] = out_re
        carry_ref[1:2, :] = out_im

        def fix_row(rows, prow, hp_re, hp_im, acc):
            fr, fi = _cmul_conj(pr_ref[prow, :], pi_ref[prow, :], gin_re, gin_im)
            g_re = gr_ref[rows, :] + fr
            g_im = gi_ref[rows, :] + fi
            gr_ref[rows, :] = g_re
            gi_ref[rows, :] = g_im
            return acc[0] + g_re * hp_re + g_im * hp_im, acc[1] + g_im * hp_re - g_re * hp_im

        def fix_at(t, acc):
            aligned = (lambda r: r * SUBLANES) if isinstance(t, int) else (lambda r: pl.multiple_of(r * SUBLANES, SUBLANES))
            rows, before, prow = (pl.ds(aligned(r), SUBLANES) for r in (t, t - 1, steps - 1 - t))
            return fix_row(rows, prow, hr_ref[before, :], hi_ref[before, :], acc)

        def fix(t4, acc):
            for j in range(SCAN_UNROLL):
                acc = fix_at(t4 * SCAN_UNROLL + j, acc)
            return acc

        acc = fix_row(first, top, inr_ref[...], ini_ref[...], (accr_ref[...], acci_ref[...]))
        for t in range(1, SCAN_UNROLL):
            acc = fix_at(t, acc)
        acc_re, acc_im = lax.fori_loop(1, steps // SCAN_UNROLL, fix, acc)
        accr_ref[...] = acc_re
        acci_ref[...] = acc_im

        gbr = gr_ref[...].astype(BF16)
        gbi = gi_ref[...].astype(BF16)
        duseg_ref[...] = _dot_nt(gbr, br_ref[0]) + _dot_nt(gbi, bi_ref[0])
        _from_segments(duseg_ref, dyseg_ref, steps)
        du_ref[...] = (dyseg_ref[...] + dk_ref[...] * dy_ref[...]).astype(BF16)
        dbr = _dot_tn(ub, gbr)
        dbi = _dot_tn(ub, gbi)

        @pl.when(c == 0)
        def _():
            dbr_ref[0] = dbr
            dbi_ref[0] = dbi
            dcr_ref[0] = dcr
            dci_ref[0] = dci
            ddk_ref[...] = ddk

        @pl.when(c > 0)
        def _():
            dbr_ref[0] += dbr
            dbi_ref[0] += dbi
            dcr_ref[0] += dcr
            dci_ref[0] += dci
            ddk_ref[...] += ddk

        @pl.when(c == nc - 1)
        def _():
            dar_ref[...] = jnp.sum(acc_re, axis=0, keepdims=True)
            dai_ref[...] = jnp.sum(acc_im, axis=0, keepdims=True)

    rev = lambda c: nc - 1 - c
    row = pl.BlockSpec((1, blk), lambda j, c: (0, j))
    b_mat = pl.BlockSpec((1, LANES, blk), lambda j, c: (j, 0, 0))
    c_mat = pl.BlockSpec((1, blk, LANES), lambda j, c: (j, 0, 0))
    tok = pl.BlockSpec((chunk, LANES), lambda j, c: (rev(c), j))
    state = pl.BlockSpec((chunk, blk), lambda j, c: (rev(c), j))
    enter = pl.BlockSpec((SUBLANES, blk), lambda j, c: (rev(c), j))
    chan = pl.BlockSpec((1, LANES), lambda j, c: (0, j))
    f32 = lambda *s: jax.ShapeDtypeStruct(s, F32)
    return _hosted_call(
        body, carry, _edge_2d(N_SSM_BLOCKS, nc), name="ssm_bwd", grid=(N_SSM_BLOCKS, nc),
        in_specs=[tok, tok, state, state, enter, enter, row, row, b_mat, b_mat, c_mat, c_mat, chan],
        out_specs=[tok, b_mat, b_mat, c_mat, c_mat, row, row, chan],
        out_shape=[jax.ShapeDtypeStruct((T, SSM_W), BF16), f32(N_SSM_BLOCKS, LANES, blk), f32(N_SSM_BLOCKS, LANES, blk),
                   f32(N_SSM_BLOCKS, blk, LANES), f32(N_SSM_BLOCKS, blk, LANES), f32(1, STATES), f32(1, STATES), f32(1, SSM_W)],
        scratch_shapes=[pltpu.VMEM((chunk, LANES), F32), pltpu.VMEM((chunk, LANES), F32), pltpu.VMEM((chunk, LANES), F32),
                        pltpu.VMEM((chunk, blk), F32), pltpu.VMEM((chunk, blk), F32),
                        pltpu.VMEM((chunk, blk), F32), pltpu.VMEM((chunk, blk), F32),
                        pltpu.VMEM((SUBLANES, blk), F32), pltpu.VMEM((SUBLANES, blk), F32), pltpu.VMEM((SUBLANES, blk), F32)],
        compiler_params=_params(("arbitrary", "arbitrary"), VMEM_BIG),
        inputs=(dy, u, h_re, h_im, in_re, in_im, a_re, a_im, b_re, b_im, c_re, c_im, d_skip))


def _merge_forward(y, att, ga, gs, w_glu, w_ssm, w_attn):
    z = jax.nn.gelu(y)
    zb = z.astype(BF16)
    gl = jax.nn.sigmoid(_dot(zb, w_glu))
    z2b = (z * gl).astype(BF16)
    y_ssm = _dot(z2b, w_ssm)
    y_attn = _dot(att, w_attn)
    sa = jax.nn.sigmoid(ga)
    ss = jax.nn.sigmoid(gs)
    merged = (sa * y_attn + ss * y_ssm).astype(BF16)
    return z, zb, gl, z2b, y_ssm, y_attn, sa, ss, merged


def _merge_fwd(x, y, att, ga, gs, g2, g3, w_glu, w_ssm, w_attn, w_out, tile):
    T = x.shape[0]

    def body(x_ref, y_ref, att_ref, ga_ref, gs_ref, g2_ref, g3_ref, wg_ref, ws_ref, wa_ref, wo_ref, x1_ref, o_ref, h2_ref):
        merged = _merge_forward(y_ref[...], att_ref[...], ga_ref[...], gs_ref[...], wg_ref[...], ws_ref[...], wa_ref[...])[-1]
        o = _dot(merged, wo_ref[...])
        x1 = x_ref[...] + o * _rms_scale(o) * g2_ref[...]
        o_ref[...] = o
        x1_ref[...] = x1
        h2_ref[...] = (x1 * _rms_scale(x1) * g3_ref[...]).astype(BF16)

    tok = lambda w: pl.BlockSpec((tile, w), lambda i: (i, 0))
    vec = _const_spec((1, D_MODEL))
    return pl.pallas_call(
        body, name="merge_fwd", grid=(T // tile,),
        in_specs=[tok(D_MODEL), tok(SSM_W), tok(ATTN_W), tok(D_MODEL), tok(D_MODEL), vec, vec,
                  _const_spec((SSM_W, SSM_W)), _const_spec((SSM_W, D_MODEL)), _const_spec((ATTN_W, D_MODEL)),
                  _const_spec((D_MODEL, D_MODEL))],
        out_specs=[tok(D_MODEL), tok(D_MODEL), tok(D_MODEL)],
        out_shape=_hbm_out([jax.ShapeDtypeStruct((T, D_MODEL), F32), jax.ShapeDtypeStruct((T, D_MODEL), F32),
                            jax.ShapeDtypeStruct((T, D_MODEL), BF16)]),
        compiler_params=_params(("arbitrary",), VMEM_MID),
    )(*_in_hbm(x, y, att, ga, gs, g2, g3, w_glu, w_ssm, w_attn, w_out))


def _merge_bwd(dh2, dx2, x1, o, y, att, ga, gs, g2, g3, w_glu, w_ssm, w_attn, w_out, tile, carry=None):
    T = x1.shape[0]
    n_steps = T // tile

    group = min(2, n_steps)
    staged_widths = (D_MODEL, D_MODEL, ATTN_W, D_MODEL, SSM_W, D_MODEL, SSM_W, SSM_W)

    def body(dh2_ref, dx2_ref, x1_ref, o_ref, y_ref, att_ref, ga_ref, gs_ref, g2_ref, g3_ref, wg_ref, ws_ref, wa_ref, wo_ref,
             dx1_ref, dgates_ref, datt_ref, dy_ref, dwg_hbm, dws_hbm, dwa_hbm, dwo_hbm, dg2_ref, dg3_ref,
             awg_ref, aws_ref, awa_ref, awo_ref, *staged):
        i = pl.program_id(0)
        x1v, ov = x1_ref[...], o_ref[...]
        dxn, dg3 = _rms_bwd(dh2_ref[...], x1v, _rms_scale(x1v), g3_ref[...])
        dx1 = dx2_ref[...] + dxn
        dx1_ref[...] = dx1
        do, dg2 = _rms_bwd(dx1, ov, _rms_scale(ov), g2_ref[...])
        dob = do.astype(BF16)

        yv = y_ref[...]
        att = att_ref[...]
        z, zb, gl, z2b, y_ssm, y_attn, sa, ss, merged = _merge_forward(
            yv, att, ga_ref[...], gs_ref[...], wg_ref[...], ws_ref[...], wa_ref[...])
        dmerged = _dot_nt(dob, wo_ref[...])
        dya = (dmerged * sa).astype(BF16)
        dys = (dmerged * ss).astype(BF16)
        dgates_ref[:, :D_MODEL] = (dmerged * y_attn * sa * (1.0 - sa)).astype(BF16)
        dgates_ref[:, D_MODEL:] = (dmerged * y_ssm * ss * (1.0 - ss)).astype(BF16)
        datt_ref[...] = _dot_nt(dya, wa_ref[...]).astype(BF16)
        dz2 = _dot_nt(dys, ws_ref[...])
        dpre = (dz2 * z * gl * (1.0 - gl)).astype(BF16)
        dz = dz2 * gl + _dot_nt(dpre, wg_ref[...])
        _, gelu_vjp = jax.vjp(jax.nn.gelu, yv)
        dy_ref[...] = gelu_vjp(dz)[0]

        part = pl.ds(pl.multiple_of((i % group) * tile, tile), tile)
        for ref, val in zip(staged, (merged, dob, att, dya, z2b, dys, zb, dpre)):
            ref[part, :] = val

        @pl.when(i == 0)
        def _():
            dg2_ref[...] = dg2
            dg3_ref[...] = dg3

        @pl.when(i > 0)
        def _():
            dg2_ref[...] += dg2
            dg3_ref[...] += dg3

        def weight_grads():
            s_merged, s_dob, s_att, s_dya, s_z2b, s_dys, s_zb, s_dpre = (ref[...] for ref in staged)
            return ((awo_ref, _dot_tn(s_merged, s_dob)), (awa_ref, _dot_tn(s_att, s_dya)),
                    (aws_ref, _dot_tn(s_z2b, s_dys)), (awg_ref, _dot_tn(s_zb, s_dpre)))

        @pl.when(i == group - 1)
        def _():
            for ref, val in weight_grads():
                ref[...] = val

        @pl.when((i % group == group - 1) & (i > group - 1))
        def _():
            for ref, val in weight_grads():
                ref[...] += val

        @pl.when(i == n_steps - 1)
        def _():
            pltpu.sync_copy(awg_ref, dwg_hbm)
            pltpu.sync_copy(aws_ref, dws_hbm)
            pltpu.sync_copy(awa_ref, dwa_hbm)
            pltpu.sync_copy(awo_ref, dwo_hbm)

    tok = lambda w: pl.BlockSpec((tile, w), lambda i: (i, 0))
    vec = _const_spec((1, D_MODEL))
    any_ = pl.BlockSpec(memory_space=pl.ANY)
    vec_out = pl.BlockSpec((1, D_MODEL), lambda i: (0, 0))
    f32 = lambda *s: jax.ShapeDtypeStruct(s, F32)
    bf = lambda *s: jax.ShapeDtypeStruct(s, BF16)
    return _hosted_call(
        body, carry, _edge_1d(n_steps), name="merge_bwd", grid=(n_steps,),
        in_specs=[tok(D_MODEL), tok(D_MODEL), tok(D_MODEL), tok(D_MODEL), tok(SSM_W), tok(ATTN_W), tok(D_MODEL), tok(D_MODEL),
                  vec, vec, _const_spec((SSM_W, SSM_W)), _const_spec((SSM_W, D_MODEL)), _const_spec((ATTN_W, D_MODEL)),
                  _const_spec((D_MODEL, D_MODEL))],
        out_specs=[tok(D_MODEL), tok(2 * D_MODEL), tok(ATTN_W), tok(SSM_W), any_, any_, any_, any_, vec_out, vec_out],
        out_shape=[f32(T, D_MODEL), bf(T, 2 * D_MODEL), bf(T, ATTN_W), f32(T, SSM_W),
                   f32(SSM_W, SSM_W), f32(SSM_W, D_MODEL), f32(ATTN_W, D_MODEL), f32(D_MODEL, D_MODEL),
                   f32(1, D_MODEL), f32(1, D_MODEL)],
        scratch_shapes=[pltpu.VMEM((SSM_W, SSM_W), F32), pltpu.VMEM((SSM_W, D_MODEL), F32),
                        pltpu.VMEM((ATTN_W, D_MODEL), F32), pltpu.VMEM((D_MODEL, D_MODEL), F32)]
        + [pltpu.VMEM((group * tile, wd), BF16) for wd in staged_widths],
        compiler_params=_params(("arbitrary",), VMEM_BIG),
        inputs=(dh2, dx2, x1, o, y, att, ga, gs, g2, g3, w_glu, w_ssm, w_attn, w_out))


FF_SHARD = D_FF // N_DEV


def _mlp_fwd(h2, x1, target, g4, w_ff_in, w_ff_out, tile):
    T = h2.shape[0]
    col_chunk = 2 * FF_SHARD

    def body(h2_ref, x1_ref, tg_ref, g4_ref, wi_ref, wo_ref, a_ref, dfo_ref, dx2_ref, loss_ref, dg4_ref, rr_ref):
        i = pl.program_id(0)
        h2v = h2_ref[...]
        for c in range(D_FF // col_chunk):
            cols = slice(c * col_chunk, (c + 1) * col_chunk)
            a = _dot_nt(h2v, wi_ref[cols, :])
            a_ref[:, cols] = a.astype(BF16)
            ra = jnp.maximum(a, 0.0)
            rr_ref[:, cols] = (ra * ra).astype(BF16)
        f = _dot(rr_ref[...], wo_ref[...])
        r = _rms_scale(f)
        g = g4_ref[...]
        err = x1_ref[...] + f * r * g - tg_ref[...]
        dx2 = err * (1.0 / D_MODEL)
        dx2_ref[...] = dx2
        dfo, dg = _rms_bwd(dx2, f, r, g)
        dfo_ref[...] = dfo.astype(BF16)
        row = lax.broadcasted_iota(jnp.int32, (SUBLANES, LANES), 0)
        col = lax.broadcasted_iota(jnp.int32, (SUBLANES, LANES), 1)
        loss = jnp.where((row == 0) & (col == 0), (0.5 / D_MODEL) * jnp.sum(err * err), 0.0)

        @pl.when(i == 0)
        def _():
            loss_ref[...] = loss
            dg4_ref[...] = dg

        @pl.when(i > 0)
        def _():
            loss_ref[...] += loss
            dg4_ref[...] += dg

    tok = pl.BlockSpec((tile, D_MODEL), lambda i: (i, 0))
    return pl.pallas_call(
        body, name="mlp_fwd", grid=(T // tile,),
        in_specs=[tok, tok, tok, _const_spec((1, D_MODEL)), _const_spec((D_FF, D_MODEL)), _const_spec((D_FF, D_MODEL))],
        out_specs=[pl.BlockSpec((tile, D_FF), lambda i: (i, 0)), tok, tok,
                   pl.BlockSpec((SUBLANES, LANES), lambda i: (0, 0)), pl.BlockSpec((1, D_MODEL), lambda i: (0, 0))],
        out_shape=_hbm_out([jax.ShapeDtypeStruct((T, D_FF), BF16), jax.ShapeDtypeStruct((T, D_MODEL), BF16),
                            jax.ShapeDtypeStruct((T, D_MODEL), F32), jax.ShapeDtypeStruct((SUBLANES, LANES), F32),
                            jax.ShapeDtypeStruct((1, D_MODEL), F32)]),
        scratch_shapes=[pltpu.VMEM((tile, D_FF), BF16)],
        compiler_params=_params(("arbitrary",), VMEM_MAX),
    )(*_in_hbm(h2, x1, target, g4, w_ff_in.reshape(D_FF, D_MODEL), w_ff_out.reshape(D_FF, D_MODEL)))


def _mlp_weight_grads(dfo, a, h2, w_ff_out, row_chunk):
    T = h2.shape[0]

    def body(dfo_ref, h2_ref, a_ref, wo_ref, dwi_ref, dwo_ref, da_ref, rr_ref):
        def rows(r, _):
            sl = pl.ds(pl.multiple_of(r * row_chunk, row_chunk), row_chunk)
            ra = jnp.maximum(a_ref[sl, :].astype(F32), 0.0)
            da_ref[sl, :] = (_dot_nt(dfo_ref[sl, :], wo_ref[0]) * (2.0 * ra)).astype(BF16)
            rr_ref[sl, :] = (ra * ra).astype(BF16)
            return 0

        lax.fori_loop(0, T // row_chunk, rows, 0)
        dwo_ref[0] = _dot_tn(rr_ref[...], dfo_ref[...])
        dwi_ref[0] = _dot_tn(h2_ref[...], da_ref[...])

    return pl.pallas_call(
        body, name="mlp_weight_grads", grid=(N_DEV,),
        in_specs=[_const_spec((T, D_MODEL)), _const_spec((T, D_MODEL)), pl.BlockSpec((T, FF_SHARD), lambda k: (0, k)),
                  pl.BlockSpec((1, FF_SHARD, D_MODEL), lambda k: (k, 0, 0))],
        out_specs=[pl.BlockSpec((1, D_MODEL, FF_SHARD), lambda k: (k, 0, 0)),
                   pl.BlockSpec((1, FF_SHARD, D_MODEL), lambda k: (k, 0, 0)), pl.BlockSpec((T, FF_SHARD), lambda k: (0, k))],
        out_shape=_hbm_out([jax.ShapeDtypeStruct((N_DEV, D_MODEL, FF_SHARD), F32),
                            jax.ShapeDtypeStruct((N_DEV, FF_SHARD, D_MODEL), F32), jax.ShapeDtypeStruct((T, D_FF), BF16)]),
        scratch_shapes=[pltpu.VMEM((T, FF_SHARD), BF16)],
        compiler_params=_params(("arbitrary",), VMEM_MAX),
    )(*_in_hbm(dfo, h2, a, w_ff_out))


def _mlp_input_grad(da, w_ff_in_t, tile):
    T = da.shape[0]

    def body(da_ref, w_ref, o_ref):
        o_ref[...] = _dot(da_ref[...], w_ref[...])

    return pl.pallas_call(
        body, name="mlp_input_grad", grid=(T // tile,),
        in_specs=[pl.BlockSpec((tile, D_FF), lambda i: (i, 0)), _const_spec((D_FF, D_MODEL))],
        out_specs=pl.BlockSpec((tile, D_MODEL), lambda i: (i, 0)),
        out_shape=_hbm_out(jax.ShapeDtypeStruct((T, D_MODEL), F32)),
        compiler_params=_params(("arbitrary",), VMEM_MID),
    )(*_in_hbm(da, w_ff_in_t))


def _block_diag_in(b):
    bt = b.reshape(N_SSM_BLOCKS, GROUPS_PER_BLOCK, GROUP_CH, N_STATE)
    eye = jnp.eye(GROUPS_PER_BLOCK, dtype=b.dtype)
    return jnp.einsum("jacp,ab->jacbp", bt, eye).reshape(N_SSM_BLOCKS, LANES, SSM_LANE_BLOCK)


def _block_diag_in_grad(g):
    g = g.reshape(N_SSM_BLOCKS, GROUPS_PER_BLOCK, GROUP_CH, GROUPS_PER_BLOCK, N_STATE)
    d = jnp.diagonal(g, axis1=1, axis2=3)
    return jnp.transpose(d, (0, 3, 1, 2)).reshape(N_GROUPS, GROUP_CH, N_STATE)


def _block_diag_out(c):
    ct = c.reshape(N_SSM_BLOCKS, GROUPS_PER_BLOCK, GROUP_CH, N_STATE)
    eye = jnp.eye(GROUPS_PER_BLOCK, dtype=c.dtype)
    return jnp.einsum("jacp,ab->japbc", ct, eye).reshape(N_SSM_BLOCKS, SSM_LANE_BLOCK, LANES)


def _block_diag_out_grad(g):
    g = g.reshape(N_SSM_BLOCKS, GROUPS_PER_BLOCK, N_STATE, GROUPS_PER_BLOCK, GROUP_CH)
    d = jnp.diagonal(g, axis1=1, axis2=3)
    return jnp.transpose(d, (0, 3, 2, 1)).reshape(N_GROUPS, GROUP_CH, N_STATE)


def _tiles(T):
    return dict(proj=min(512, T), proj_bwd=min(512, T // 2), merge=min(512, T), merge_bwd=min(256, T),
                mlp_fwd=min(512, T), mlp_bwd=min(512, T), ssm_chunk=min(1024, T))


def _mesh_position():
    x, y, c = lax.axis_index("x"), lax.axis_index("y"), lax.axis_index("c")
    other_chips = [(1 - x, y), (x, 1 - y), (1 - x, 1 - y)]
    return x, y, c, other_chips


def _gather_carry(arrays):
    n = len(arrays)

    def copies(ins, outs, sems):
        send_sems, recv_sems, local_sems = sems
        x, y, c, chips = _mesh_position()
        me, sibling = (x, y, c), (x, y, 1 - c)

        def copy(a, k, block, to, src=None):
            px, py, pc = block
            dst = outs[a].at[4 * px + 2 * py + pc]
            return pltpu.make_async_remote_copy(
                src_ref=dst if src is None else src, dst_ref=dst, send_sem=send_sems.at[7 * a + k],
                recv_sem=recv_sems.at[7 * a + k], device_id=to, device_id_type=MESH_IDS)

        mine = [pltpu.make_async_copy(ins[a], outs[a].at[4 * x + 2 * y + c], local_sems.at[a]) for a in range(n)]
        first = []
        for a in range(n):
            first.append(copy(a, 0, me, sibling, src=ins[a]))
            first += [copy(a, 1 + j, me, (*chip, c), src=ins[a]) for j, chip in enumerate(chips)]
        return copy, mine, first, me, sibling, chips, c

    def start(ins, outs, sems):
        _, mine, first, *_ = copies(ins, outs, sems)
        for cp in mine + first:
            cp.start()

    def finish(ins, outs, sems):
        copy, mine, first, me, sibling, chips, c = copies(ins, outs, sems)
        passed = []
        for a in range(n):
            for j, chip in enumerate(chips):
                copy(a, 1 + j, (*chip, c), me).wait_recv()
                passed.append(copy(a, 4 + j, (*chip, c), sibling))
                passed[-1].start()
        for a in range(n):
            copy(a, 0, sibling, me).wait_recv()
            for j, chip in enumerate(chips):
                copy(a, 4 + j, (*chip, 1 - c), me).wait_recv()
        for cp in first + passed:
            cp.wait_send()
        for cp in mine:
            cp.wait()

    return _Carry(arrays, [jax.ShapeDtypeStruct((N_DEV,) + a.shape, a.dtype) for a in arrays],
                  [pltpu.SemaphoreType.DMA((7 * n,)), pltpu.SemaphoreType.DMA((7 * n,)), pltpu.SemaphoreType.DMA((n,))],
                  start, finish)


def _pairwise_carry(arrays, n_slots, make_copies):
    n = len(arrays)

    def start(ins, outs, sems):
        for cp in make_copies(ins, outs, sems):
            cp.start()

    def finish(ins, outs, sems):
        for cp in make_copies(ins, outs, sems):
            cp.wait()

    return _Carry(arrays, [jax.ShapeDtypeStruct((n_slots,) + a.shape[1:], a.dtype) for a in arrays],
                  [pltpu.SemaphoreType.DMA((n_slots * n,)), pltpu.SemaphoreType.DMA((n_slots * n,))], start, finish)


def _sibling_carry(grads):
    def make_copies(ins, outs, sems):
        x, y, c, _ = _mesh_position()
        return [pltpu.make_async_remote_copy(
            src_ref=ins[a].at[2 * ch + (1 - c)], dst_ref=outs[a].at[ch], send_sem=sems[0].at[4 * a + ch],
            recv_sem=sems[1].at[4 * a + ch], device_id=(x, y, 1 - c), device_id_type=MESH_IDS)
            for a in range(len(grads)) for ch in range(4)]

    return _pairwise_carry(grads, 4, make_copies)


def _chips_carry(sums):
    def make_copies(ins, outs, sems):
        x, y, c, chips = _mesh_position()
        return [pltpu.make_async_remote_copy(
            src_ref=ins[a].at[2 * px + py], dst_ref=outs[a].at[j], send_sem=sems[0].at[3 * a + j],
            recv_sem=sems[1].at[3 * a + j], device_id=(px, py, c), device_id_type=MESH_IDS)
            for a in range(len(sums)) for j, (px, py) in enumerate(chips)]

    return _pairwise_carry(sums, 3, make_copies)


SEM_SPEC = pl.BlockSpec(memory_space=pltpu.SEMAPHORE)
DATAFLOW_EFFECT = pltpu.SideEffectType.DATAFLOW_SIDE_EFFECTING


def _exchange_start(carry, name, after=()):
    n = len(carry.inputs)
    lands = [lax.empty(s.shape, s.dtype) for s in carry.out_shapes]

    def body(*refs):
        first_out = 2 * n + len(after)
        srcs, zones, sems, token = refs[:n], refs[n:2 * n], refs[first_out:first_out + 2], refs[-1]
        carry.start(srcs, zones, sems)
        token[...] = jnp.zeros_like(token)

    outs = pl.pallas_call(
        body, name=name, in_specs=[HBM_SPEC] * (2 * n + len(after)),
        out_specs=[SEM_SPEC, SEM_SPEC] + [HBM_SPEC] * (2 * n) + [pl.BlockSpec(memory_space=pltpu.VMEM)],
        out_shape=list(carry.sems) + _hbm_out([jax.ShapeDtypeStruct(a.shape, a.dtype) for a in carry.inputs])
        + _hbm_out(carry.out_shapes) + [jax.ShapeDtypeStruct((SUBLANES, LANES), F32)],
        input_output_aliases={j: 2 + j for j in range(2 * n)},
        compiler_params=pltpu.CompilerParams(has_side_effects=DATAFLOW_EFFECT),
    )(*_in_hbm(*carry.inputs, *lands), *after)
    return outs[:-1], outs[-1]


def _exchange_wait(carry, in_flight, after, name):
    n = len(carry.inputs)
    sems, srcs, zones = in_flight[:2], in_flight[2:2 + n], in_flight[2 + n:]

    def body(*refs):
        src_refs, zone_refs, sem_refs = refs[:n], refs[n:2 * n], refs[2 * n:2 * n + 2]
        carry.finish(src_refs, zone_refs, sem_refs)

    outs = pl.pallas_call(
        body, name=name, in_specs=[HBM_SPEC] * (2 * n) + [SEM_SPEC, SEM_SPEC] + [HBM_SPEC] * len(after),
        out_specs=[HBM_SPEC] * (2 * n),
        out_shape=_hbm_out([jax.ShapeDtypeStruct(a.shape, a.dtype) for a in carry.inputs]) + _hbm_out(carry.out_shapes),
        input_output_aliases={j: j for j in range(2 * n)},
        compiler_params=pltpu.CompilerParams(has_side_effects=DATAFLOW_EFFECT),
    )(*srcs, *zones, *sems, *after)
    return list(outs[:n]), list(outs[n:])


def _add_sibling(grads8, recvs, core, row_tiles, name):
    k = len(grads8)
    g4 = [g.reshape(4, 2, *g.shape[1:]) for g in grads8]

    def body(core_ref, *refs):
        g_refs, r_refs, o_refs, ob_refs = (refs[j * k:(j + 1) * k] for j in range(4))
        for g_ref, r_ref, o_ref, ob_ref in zip(g_refs, r_refs, o_refs, ob_refs):
            s = g_ref[0] + r_ref[...]
            o_ref[...] = s
            ob_ref[...] = s.astype(BF16)

    def blocks(make):
        return [make(g.shape[1] // row_tiles, g.shape[2]) for g in grads8]

    slot = lambda tr, C: pl.BlockSpec((1, tr, C), lambda ch, r, core_ref: (ch, r, 0))
    outs = pl.pallas_call(
        body, name=name,
        grid_spec=pltpu.PrefetchScalarGridSpec(
            num_scalar_prefetch=1, grid=(4, row_tiles),
            in_specs=blocks(lambda tr, C: pl.BlockSpec((1, 1, tr, C), lambda ch, r, core_ref: (ch, core_ref[0], r, 0)))
            + blocks(slot), out_specs=blocks(slot) + blocks(slot)),
        out_shape=_hbm_out([jax.ShapeDtypeStruct((4,) + g.shape[1:], F32) for g in grads8]
                           + [jax.ShapeDtypeStruct((4,) + g.shape[1:], BF16) for g in grads8]),
        compiler_params=_params(("arbitrary", "arbitrary")),
    )(core, *_in_hbm(*g4, *recvs))
    return list(outs[:k]), list(outs[k:])


def _adam_math(w, g, m, v):
    m = ADAM_B1 * m + (1.0 - ADAM_B1) * g
    v = ADAM_B2 * v + (1.0 - ADAM_B2) * jnp.square(g)
    m_hat = m / (1.0 - ADAM_B1 ** ADAM_STEP)
    v_hat = v / (1.0 - ADAM_B2 ** ADAM_STEP)
    delta = -ADAM_LR * (m_hat / (jnp.sqrt(v_hat) + ADAM_EPS) + ADAM_WD * w)
    return delta, m, v


def _adam_big(ws, ms, vs, chip_sums, recvs, chip, row_tiles, name):
    k = len(ws)

    def body(chip_ref, *refs):
        w_refs, m_refs, v_refs, s_refs, r_refs, g_refs, d_refs, nm_refs, nv_refs = (refs[j * k:(j + 1) * k] for j in range(9))
        for a in range(k):
            r_ref = r_refs[a]
            g = s_refs[a][0] + r_ref[0].astype(F32) + r_ref[1].astype(F32) + r_ref[2].astype(F32)
            g_refs[a][...] = g
            d_refs[a][...], nm_refs[a][...], nv_refs[a][...] = _adam_math(w_refs[a][...], g, m_refs[a][...], v_refs[a][...])

    def blocks(make):
        return [make(w.shape[0] // row_tiles, w.shape[1]) for w in ws]

    blk = lambda tr, C: pl.BlockSpec((tr, C), lambda r, chip_ref: (r, 0))
    outs = pl.pallas_call(
        body, name=name,
        grid_spec=pltpu.PrefetchScalarGridSpec(
            num_scalar_prefetch=1, grid=(row_tiles,),
            in_specs=blocks(blk) * 3 + blocks(lambda tr, C: pl.BlockSpec((1, tr, C), lambda r, chip_ref: (chip_ref[0], r, 0)))
            + blocks(lambda tr, C: pl.BlockSpec((3, tr, C), lambda r, chip_ref: (0, r, 0))),
            out_specs=blocks(blk) * 4),
        out_shape=[jax.ShapeDtypeStruct(w.shape, F32) for w in ws] * 4,
        compiler_params=_params(("arbitrary",)),
    )(chip, *_in_hbm(*ws, *ms, *vs, *chip_sums, *recvs))
    return [list(outs[j * k:(j + 1) * k]) for j in range(4)]


def _sum_partials(partials, name):
    def body(p_ref, g_ref):
        g = p_ref[0]
        for d in range(1, partials.shape[0]):
            g = g + p_ref[d]
        g_ref[...] = g

    return pl.pallas_call(body, name=name, grid=(1,), in_specs=[_whole(partials.shape)], out_specs=_whole(partials.shape[1:]),
                          out_shape=jax.ShapeDtypeStruct(partials.shape[1:], F32))(*_in_hbm(partials))


def _adam_small(ws, ms, vs, gs):
    n = len(ws)

    def body(*refs):
        w_refs, m_refs, v_refs, g_refs = (refs[i * n:(i + 1) * n] for i in range(4))
        d_refs, nm_refs, nv_refs = (refs[(4 + i) * n:(5 + i) * n] for i in range(3))
        for j in range(n):
            d_refs[j][...], nm_refs[j][...], nv_refs[j][...] = _adam_math(
                w_refs[j][...], g_refs[j][...], m_refs[j][...], v_refs[j][...])

    specs = [_whole(w.shape) for w in ws]
    outs = pl.pallas_call(body, name="adam_small", grid=(1,), in_specs=specs * 4, out_specs=specs * 3,
                          out_shape=[jax.ShapeDtypeStruct(w.shape, F32) for w in ws] * 3,
                          compiler_params=_params(("arbitrary",), VMEM_MID))(*_in_hbm(*ws, *ms, *vs, *gs))
    return outs[:n], outs[n:2 * n], outs[2 * n:]


PACK_QUANTUM = SUBLANES * LANES


def _pack(named, names):
    parts = []
    for nme in names:
        flat = named[nme].reshape(-1)
        parts.append(jnp.pad(flat, (0, -flat.size % PACK_QUANTUM)))
    return jnp.concatenate(parts).reshape(-1, LANES)


def _unpack(packed, shapes, names):
    flat = packed.reshape(-1)
    out, pos = {}, 0
    for nme in names:
        size = math.prod(shapes[nme])
        out[nme] = flat[pos:pos + size].reshape(shapes[nme])
        pos += size + (-size % PACK_QUANTUM)
    return out


BIG = ("w_in", "w_glu", "w_attn_branch", "w_ssm_branch", "w_out", "w_ff_in", "w_ff_out")
COLUMN_SHARDED = ("w_in", "w_attn_branch", "w_ssm_branch", "w_ff_in")
SMALL = ("norm_mix_pre", "norm_mix_post", "norm_mlp_pre", "norm_mlp_post", "rel_bias", "sinks", "lam_re", "lam_im",
         "log_dt", "b_re", "b_im", "c_re", "c_im", "d_skip")
SWAPPED_SMALL = ("rel_bias", "b_re", "b_im")
SMALL_LATE = ("norm_mix_pre", "rel_bias", "sinks", "loss")
SMALL_BEFORE_ATTN_BWD = tuple(n for n in SMALL if n not in SMALL_LATE)
ALL_WEIGHTS = ("norm_mix_pre", "norm_mix_post", "norm_mlp_pre", "norm_mlp_post", "w_in", "rel_bias", "sinks", "lam_re",
               "lam_im", "log_dt", "b_re", "b_im", "c_re", "c_im", "d_skip", "w_glu", "w_attn_branch", "w_ssm_branch",
               "w_out", "w_ff_in", "w_ff_out")


def _full_from_gathered(name, gathered):
    _, r, c = gathered.shape
    if name in COLUMN_SHARDED:
        return jnp.transpose(gathered, (1, 0, 2)).reshape(r, N_DEV * c)
    return gathered.reshape(N_DEV * r, c)


def _blocks_from_full(name, full):
    r, c = full.shape
    if name in COLUMN_SHARDED:
        return jnp.transpose(full.reshape(r, N_DEV, c // N_DEV), (1, 0, 2))
    return full.reshape(N_DEV, r // N_DEV, c)


def kernel(x, norm_mix_pre, norm_mix_post, norm_mlp_pre, norm_mlp_post, w_in, rel_bias, sinks, lam_re, lam_im, log_dt, b_re, b_im, c_re, c_im, d_skip, w_glu, w_attn_branch, w_ssm_branch, w_out, w_ff_in, w_ff_out, loss_target, m_norm_mix_pre, m_norm_mix_post, m_norm_mlp_pre, m_norm_mlp_post, m_w_in, m_rel_bias, m_sinks, m_lam_re, m_lam_im, m_log_dt, m_b_re, m_b_im, m_c_re, m_c_im, m_d_skip, m_w_glu, m_w_attn_branch, m_w_ssm_branch, m_w_out, m_w_ff_in, m_w_ff_out, v_norm_mix_pre, v_norm_mix_post, v_norm_mlp_pre, v_norm_mlp_post, v_w_in, v_rel_bias, v_sinks, v_lam_re, v_lam_im, v_log_dt, v_b_re, v_b_im, v_c_re, v_c_im, v_d_skip, v_w_glu, v_w_attn_branch, v_w_ssm_branch, v_w_out, v_w_ff_in, v_w_ff_out):
    args = dict(locals())
    w = {n: args[n] for n in ALL_WEIGHTS}
    m = {n: args["m_" + n] for n in ALL_WEIGHTS}
    v = {n: args["v_" + n] for n in ALL_WEIGHTS}
    core = lax.axis_index("c").astype(jnp.int32).reshape(1)
    chip = (2 * lax.axis_index("x") + lax.axis_index("y")).astype(jnp.int32).reshape(1)
    xs, target = x[0], loss_target[0]
    t = _tiles(xs.shape[0])
    local = lambda d, n: d[n][0].T if n == "w_in" else d[n][0]
    shard = {n: local(w, n).astype(BF16) for n in BIG}
    shard["w_ff_in"] = shard["w_ff_in"].T
    view = lambda n, a: jnp.swapaxes(a, -1, -2) if n in SWAPPED_SMALL else a
    small = {n: (view(n, w[n]) if n == "rel_bias" else view(n, w[n])[0]) for n in SMALL}
    g1, g2, g3, g4 = (small[n].reshape(1, D_MODEL) for n in ("norm_mix_pre", "norm_mix_post", "norm_mlp_pre", "norm_mlp_post"))
    bucket = jnp.asarray(_bucket_table())
    rel_b, sink = small["rel_bias"], small["sinks"].reshape(1, N_HEADS)
    lam_r, lam_i = small["lam_re"].reshape(1, STATES), small["lam_im"].reshape(1, STATES)
    ldt_rep = jnp.repeat(small["log_dt"].reshape(N_GROUPS), N_STATE).reshape(1, STATES)
    bd_re, bd_im = _block_diag_in(small["b_re"]), _block_diag_in(small["b_im"])
    cm_re, cm_im = _block_diag_out(small["c_re"]).astype(BF16), _block_diag_out(small["c_im"]).astype(BF16)
    dsk = small["d_skip"].reshape(1, SSM_W)

    (g_in,) = _run_carry(_gather_carry([shard["w_in"]]), "gather_w_in")
    wf_in = g_in.reshape(IN_W, D_MODEL)
    merge_names = ("w_glu", "w_attn_branch", "w_ssm_branch", "w_out")
    (q, k, vv, u, ga, gs, h), gathered = _in_proj_fwd(xs, g1, wf_in, t["proj"], _gather_carry([shard[n] for n in merge_names]))
    wf = {n: _full_from_gathered(n, g) for n, g in zip(merge_names, gathered)}
    (att,), (wf_ff_in,) = _attn_fwd(q, k, vv, bucket, rel_b, sink, _gather_carry([shard["w_ff_in"]]))
    a_re, a_im, bm_re, bm_im = _ssm_prep(lam_r, lam_i, ldt_rep, bd_re, bd_im)
    (y, h_re, h_im, in_re, in_im), (wf_ff_out,) = _ssm_fwd(
        u, a_re, a_im, bm_re, bm_im, cm_re, cm_im, dsk, t["ssm_chunk"], _gather_carry([shard["w_ff_out"]]))
    x1, o, h2 = _merge_fwd(xs, y, att, ga, gs, g2, g3, wf["w_glu"], wf["w_ssm_branch"], wf["w_attn_branch"], wf["w_out"],
                           t["merge"])
    a, dfo, dx2, loss_blk, dg4 = _mlp_fwd(h2, x1, target, g4, wf_ff_in, wf_ff_out, t["mlp_fwd"])

    groups = {"ff": 4, "merge": 1, "w_in": 2}

    def add_sibling(group, blocks, received):
        return _add_sibling(blocks, received, core, groups[group], "add_sibling_" + group)

    ff_names = ("w_ff_in", "w_ff_out")
    dw_ff_in, dw_ff_out, da = _mlp_weight_grads(dfo, a, h2, wf_ff_out, t["mlp_bwd"])
    dh2 = _mlp_input_grad(da, wf_ff_in.reshape(D_FF, D_MODEL), t["mlp_bwd"])
    ff_blocks = [dw_ff_in, dw_ff_out]
    (dx1, dgates, datt, dy, dw_glu, dw_ssm, dw_attn, dw_out, dg2, dg3), ff_recv = _merge_bwd(
        dh2, dx2, x1, o, y, att, ga, gs, g2, g3, wf["w_glu"], wf["w_ssm_branch"], wf["w_attn_branch"], wf["w_out"],
        t["merge_bwd"], _sibling_carry(ff_blocks))
    ff_sums, ff_sums_bf = add_sibling("ff", ff_blocks, ff_recv)
    merge_blocks = [_blocks_from_full(n, g) for n, g in zip(merge_names, (dw_glu, dw_attn, dw_ssm, dw_out))]
    (du, dbm_re, dbm_im, dcm_re, dcm_im, da_re, da_im, dd_skip), carried = _ssm_bwd(
        dy, u, h_re, h_im, in_re, in_im, a_re, a_im, bm_re, bm_im, cm_re, cm_im, dsk, t["ssm_chunk"],
        _join(_chips_carry(ff_sums_bf), _sibling_carry(merge_blocks)))
    ff_from_chips, merge_recv = carried[:2], carried[2:]
    merge_sums, merge_sums_bf = add_sibling("merge", merge_blocks, merge_recv)
    dbd_re, dbd_im, dlam_re, dlam_im, dldt_rep = _ssm_prep_bwd(lam_r, lam_i, ldt_rep, bd_re, bd_im, dbm_re, dbm_im, da_re, da_im)
    dlog_dt = _group_sum(dldt_rep.reshape(N_GROUPS, N_STATE))
    shapes = {n: view(n, w[n]).shape for n in SMALL}
    shapes["loss"] = (1,)
    small_grads = dict(
        norm_mix_post=dg2, norm_mlp_pre=dg3, norm_mlp_post=dg4, lam_re=dlam_re, lam_im=dlam_im, log_dt=dlog_dt,
        b_re=_block_diag_in_grad(dbd_re), b_im=_block_diag_in_grad(dbd_im),
        c_re=_block_diag_out_grad(dcm_re), c_im=_block_diag_out_grad(dcm_im), d_skip=dd_skip)
    packed_early = _pack({n: small_grads[n].reshape(shapes[n]) for n in SMALL_BEFORE_ATTN_BWD}, SMALL_BEFORE_ATTN_BWD)
    (dq, dkv, attn_small), carried = _attn_bwd(
        q, k, vv, datt, bucket, rel_b, sink, _join(_chips_carry(merge_sums_bf), _gather_carry([packed_early])))
    merge_from_chips, partials_early = carried[:-1], carried[-1]

    dparts = (dq, dkv, du, dgates)
    dw_in_t = _in_proj_weight_grad(h, dparts)
    in_blocks = [dw_in_t.reshape(N_DEV, IN_W // N_DEV, D_MODEL)]
    to_sibling = _sibling_carry(in_blocks)
    in_flight, token = _exchange_start(to_sibling, "w_in_sibling_start")
    n_tiles = xs.shape[0] // t["proj_bwd"]
    (grad_x, dg1), _ = _in_proj_input_grad(xs, g1 + token[0:1, 0:1], wf_in, dx1, dparts, t["proj_bwd"], 0, n_tiles, "in_proj_input_grad")
    late = dict(norm_mix_pre=dg1, rel_bias=attn_small[:, :N_BUCKETS, 0], sinks=attn_small[:, N_BUCKETS, 0], loss=loss_blk[0:1, 0])
    packed_late = _pack({n: late[n].reshape(shapes[n]) for n in SMALL_LATE}, SMALL_LATE)
    (partials_late,) = _run_carry(_gather_carry([packed_late]), "gather_late_grads")
    in_blocks, in_recv = _exchange_wait(to_sibling, in_flight, [partials_late], "w_in_sibling_wait")
    in_sums, in_sums_bf = add_sibling("w_in", in_blocks, in_recv)
    to_chips = _chips_carry(in_sums_bf)
    in_flight, _ = _exchange_start(to_chips, "w_in_chips_start")

    grads, deltas, new_m, new_v = {}, {}, {}, {}

    def adam_group(group, names, sums, received):
        outs = _adam_big(*[[local(d, n) for n in names] for d in (w, m, v)], sums, received, chip, groups[group], "adam_" + group)
        for store, vals in zip((grads, deltas, new_m, new_v), outs):
            store.update({n: (o.T if n == "w_in" else o)[None] for n, o in zip(names, vals)})

    adam_group("ff", ff_names, ff_sums, ff_from_chips)
    adam_group("merge", merge_names, merge_sums, merge_from_chips)

    grads.update(_unpack(_sum_partials(partials_early, "sum_small_grads"), shapes, SMALL_BEFORE_ATTN_BWD))
    grads.update(_unpack(_sum_partials(partials_late, "sum_late_grads"), shapes, SMALL_LATE))
    loss = grads.pop("loss").reshape(())
    small_out = _adam_small(*[[view(n, d[n]) for n in SMALL] for d in (w, m, v)], [grads[n] for n in SMALL])
    for store, vals in zip((deltas, new_m, new_v), small_out):
        store.update(zip(SMALL, vals))
    for store in (grads, deltas, new_m, new_v):
        store.update({n: view(n, store[n]) for n in SWAPPED_SMALL})

    busy = [new_v["w_ff_out"], new_v["w_out"], deltas["norm_mix_pre"]]
    _, (in_from_chips,) = _exchange_wait(to_chips, in_flight, busy, "w_in_chips_wait")
    adam_group("w_in", ("w_in",), in_sums, [in_from_chips])

    return (loss, grad_x[None], *[grads[n] for n in ALL_WEIGHTS], *[deltas[n] for n in ALL_WEIGHTS],
            *[new_m[n] for n in ALL_WEIGHTS], *[new_v[n] for n in ALL_WEIGHTS])
```

```python
import math

import jax
import jax.numpy as jnp
import numpy as np
from jax import lax
from jax.experimental import pallas as pl
from jax.experimental.pallas import tpu as pltpu

F32 = jnp.float32
BF16 = jnp.bfloat16

D_MODEL = 1024
N_HEADS = 8
HEAD_DIM = 64
ATTN_W = 512
KV_W = 128
BLOCK = 128
N_BUCKETS = 32
SSM_W = 512
N_GROUPS = 32
N_STATE = 64
GROUP_CH = 16
STATES = N_GROUPS * N_STATE
D_FF = 4096
IN_W = 3328
SPLITS = (0, 512, 640, 768, 1280, 2304, 3328)
RMS_EPS = 1e-6
NEG_INF = -1e30
SUBLANES = 8
LANES = 128
SSM_LANE_BLOCK = 512
N_SSM_BLOCKS = STATES // SSM_LANE_BLOCK
GROUPS_PER_BLOCK = SSM_LANE_BLOCK // N_STATE
VMEM_BIG = 52 * 1024 * 1024
VMEM_MID = 40 * 1024 * 1024
VMEM_MAX = 60 * 1024 * 1024

ADAM_LR = 0.001
ADAM_B1 = 0.9
ADAM_B2 = 0.999
ADAM_EPS = 1e-08
ADAM_WD = 0.01
ADAM_STEP = 10

N_DEV = 8


def _dot(a, b):
    return jnp.dot(a, b, preferred_element_type=F32)


def _dot_nt(a, b):
    return lax.dot_general(a, b, (((1,), (1,)), ((), ())), preferred_element_type=F32)


def _dot_tn(a, b):
    return lax.dot_general(a, b, (((0,), (0,)), ((), ())), preferred_element_type=F32)


def _rms_scale(x):
    return lax.rsqrt(jnp.mean(x * x, axis=-1, keepdims=True) + RMS_EPS)


def _rms_bwd(dy, x, r, g):
    t = dy * g
    dx = r * t - x * (r * r * r) * jnp.mean(t * x, axis=-1, keepdims=True)
    dg = jnp.sum(dy * x * r, axis=0, keepdims=True)
    return dx, dg


def _const_spec(shape):
    nd = len(shape)
    return pl.BlockSpec(shape, lambda *_: (0,) * nd, pipeline_mode=pl.Buffered(1))


def _in_hbm(*arrays):
    return tuple(pltpu.with_memory_space_constraint(a, pltpu.HBM) for a in arrays)


def _hbm_out(shapes):
    if isinstance(shapes, (list, tuple)):
        return [_hbm_out(s) for s in shapes]
    return shapes if isinstance(shapes, pl.MemoryRef) else pltpu.HBM(shapes.shape, shapes.dtype)


def _whole(shape):
    nd = len(shape)
    return pl.BlockSpec(shape, lambda *_: (0,) * nd)


def _params(sem, vmem=None):
    return pltpu.CompilerParams(dimension_semantics=sem, vmem_limit_bytes=vmem)


MESH_IDS = pl.DeviceIdType.MESH
HBM_SPEC = pl.BlockSpec(memory_space=pl.ANY)


class _Carry:
    def __init__(self, inputs, out_shapes, sems, start, finish):
        self.inputs, self.out_shapes, self.sems, self.start, self.finish = list(inputs), list(out_shapes), list(sems), start, finish


def _join(a, b):
    na_in, na_out, na_sem = len(a.inputs), len(a.out_shapes), len(a.sems)

    def start(ins, outs, sems):
        a.start(ins[:na_in], outs[:na_out], sems[:na_sem])
        b.start(ins[na_in:], outs[na_out:], sems[na_sem:])

    def finish(ins, outs, sems):
        a.finish(ins[:na_in], outs[:na_out], sems[:na_sem])
        b.finish(ins[na_in:], outs[na_out:], sems[na_sem:])

    return _Carry(a.inputs + b.inputs, a.out_shapes + b.out_shapes, a.sems + b.sems, start, finish)


def _hosted_call(body, carry, edge, *, name, grid, in_specs, out_specs, out_shape, scratch_shapes, compiler_params, inputs):
    n_in, n_out = len(in_specs), len(out_specs)
    inputs = [a if s.memory_space == pltpu.SMEM else _in_hbm(a)[0] for a, s in zip(inputs, in_specs)]
    out_shape = _hbm_out(list(out_shape))
    if carry is None:
        outs = pl.pallas_call(body, name=name, grid=grid, in_specs=in_specs, out_specs=out_specs, out_shape=out_shape,
                              scratch_shapes=scratch_shapes, compiler_params=compiler_params)(*inputs)
        return list(outs), []
    c_in, c_out, c_sem = len(carry.inputs), len(carry.out_shapes), len(carry.sems)

    def wrapped(*refs):
        ins, refs = refs[:n_in], refs[n_in:]
        cins, refs = refs[:c_in], refs[c_in:]
        outs, refs = refs[:n_out], refs[n_out:]
        couts, refs = refs[:c_out], refs[c_out:]
        scratch, csems = refs[:len(refs) - c_sem], refs[len(refs) - c_sem:]
        first, last = edge()

        @pl.when(first)
        def _():
            carry.start(cins, couts, csems)

        body(*ins, *outs, *scratch)

        @pl.when(last)
        def _():
            carry.finish(cins, couts, csems)

    outs = pl.pallas_call(
        wrapped, name=name, grid=grid, in_specs=list(in_specs) + [HBM_SPEC] * c_in,
        out_specs=list(out_specs) + [HBM_SPEC] * c_out, out_shape=out_shape + _hbm_out(carry.out_shapes),
        scratch_shapes=list(scratch_shapes) + carry.sems, compiler_params=compiler_params)(*inputs, *_in_hbm(*carry.inputs))
    return list(outs[:n_out]), list(outs[n_out:])


def _edge_1d(n_steps):
    return lambda: (pl.program_id(0) == 0, pl.program_id(0) == n_steps - 1)


def _edge_2d(n0, n1):
    return lambda: ((pl.program_id(0) == 0) & (pl.program_id(1) == 0),
                    (pl.program_id(0) == n0 - 1) & (pl.program_id(1) == n1 - 1))


def _run_carry(carry, name):
    c_in, c_out = len(carry.inputs), len(carry.out_shapes)

    def body(*refs):
        ins, outs, sems = refs[:c_in], refs[c_in:c_in + c_out], refs[c_in + c_out:]
        carry.start(ins, outs, sems)
        carry.finish(ins, outs, sems)

    return pl.pallas_call(body, name=name, in_specs=[HBM_SPEC] * c_in, out_specs=[HBM_SPEC] * c_out,
                          out_shape=_hbm_out(carry.out_shapes), scratch_shapes=carry.sems)(*_in_hbm(*carry.inputs))


def _in_proj_fwd(x, g1, w_in_t, tile, carry=None):
    T = x.shape[0]

    def body(x_ref, g_ref, w_ref, q_ref, k_ref, v_ref, u_ref, ga_ref, gs_ref, h_ref):
        xv = x_ref[...]
        h = (xv * _rms_scale(xv) * g_ref[...]).astype(BF16)
        h_ref[...] = h
        outs = (q_ref, k_ref, v_ref, u_ref, ga_ref, gs_ref)
        for p, o_ref in enumerate(outs):
            o_ref[...] = _dot_nt(h, w_ref[SPLITS[p]:SPLITS[p + 1], :]).astype(o_ref.dtype)

    widths = [SPLITS[p + 1] - SPLITS[p] for p in range(6)] + [D_MODEL]
    dtypes = [BF16, BF16, BF16, F32, F32, F32, BF16]
    return _hosted_call(
        body, carry, _edge_1d(T // tile), name="in_proj_fwd", grid=(T // tile,),
        in_specs=[pl.BlockSpec((tile, D_MODEL), lambda i: (i, 0)), _const_spec((1, D_MODEL)), _const_spec((IN_W, D_MODEL))],
        out_specs=[pl.BlockSpec((tile, w), lambda i: (i, 0)) for w in widths],
        out_shape=[jax.ShapeDtypeStruct((T, w), dt) for w, dt in zip(widths, dtypes)],
        scratch_shapes=[], compiler_params=_params(("arbitrary",), VMEM_MID), inputs=(x, g1, w_in_t))


PROJ_PARTS = (512, 256, 512, 2048)
PROJ_GRAD_BLOCK = 256


def _in_proj_weight_grad(h, dparts):
    T = h.shape[0]
    blocks = [wd // PROJ_GRAD_BLOCK for wd in PROJ_PARTS]
    starts = [sum(blocks[:p]) for p in range(len(blocks))]

    def body(h_ref, *refs):
        part_refs, o_ref = refs[:-1], refs[-1]
        j = pl.program_id(0)
        for p_ref, start, count in zip(part_refs, starts, blocks):
            @pl.when((j >= start) & (j < start + count))
            def _(p_ref=p_ref):
                o_ref[...] = _dot_tn(p_ref[...], h_ref[...])

    def part_spec(start, count):
        return pl.BlockSpec((T, PROJ_GRAD_BLOCK), lambda j: (0, jnp.clip(j - start, 0, count - 1)))

    return pl.pallas_call(
        body, name="in_proj_weight_grad", grid=(sum(blocks),),
        in_specs=[_const_spec((T, D_MODEL))] + [part_spec(s, c) for s, c in zip(starts, blocks)],
        out_specs=pl.BlockSpec((PROJ_GRAD_BLOCK, D_MODEL), lambda j: (j, 0)),
        out_shape=_hbm_out(jax.ShapeDtypeStruct((IN_W, D_MODEL), F32)),
        compiler_params=_params(("arbitrary",), VMEM_MID),
    )(*_in_hbm(h, *dparts))


def _in_proj_input_grad(x, g1, w_in_t, dx1, dparts, tile, first_tile, n_tiles, name, carry=None):
    offsets = [sum(PROJ_PARTS[:p]) for p in range(len(PROJ_PARTS))]

    def body(x_ref, g_ref, w_ref, dx1_ref, *refs):
        part_refs, (gx_ref, dg_ref) = refs[:len(PROJ_PARTS)], refs[len(PROJ_PARTS):]
        i = pl.program_id(0)
        xv = x_ref[...]
        r = _rms_scale(xv)
        g = g_ref[...]
        dh = sum(_dot(p_ref[...], w_ref[off:off + wd, :]) for p_ref, off, wd in zip(part_refs, offsets, PROJ_PARTS))
        dxn, dg = _rms_bwd(dh, xv, r, g)
        gx_ref[...] = dx1_ref[...] + dxn

        @pl.when(i == 0)
        def _():
            dg_ref[...] = dg

        @pl.when(i > 0)
        def _():
            dg_ref[...] += dg

    tok = lambda wd: pl.BlockSpec((tile, wd), lambda i: (i + first_tile, 0))
    return _hosted_call(
        body, carry, _edge_1d(n_tiles), name=name, grid=(n_tiles,),
        in_specs=[tok(D_MODEL), _const_spec((1, D_MODEL)), _const_spec((IN_W, D_MODEL)), tok(D_MODEL)] + [tok(wd) for wd in PROJ_PARTS],
        out_specs=[pl.BlockSpec((tile, D_MODEL), lambda i: (i, 0)), pl.BlockSpec((1, D_MODEL), lambda i: (0, 0))],
        out_shape=[jax.ShapeDtypeStruct((n_tiles * tile, D_MODEL), F32), jax.ShapeDtypeStruct((1, D_MODEL), F32)],
        scratch_shapes=[], compiler_params=_params(("arbitrary",), VMEM_MID), inputs=(x, g1, w_in_t, dx1, *dparts))


def _bucket_table():
    qi = np.arange(BLOCK)[:, None]
    kj = np.arange(2 * BLOCK)[None, :]
    dist = qi + BLOCK - kj
    max_exact = N_BUCKETS // 2
    d = np.maximum(dist, 0)
    df = np.maximum(d, 1).astype(np.float32)
    large = max_exact + (np.log(df / np.float32(max_exact)) / np.float32(math.log(BLOCK / max_exact))
                         * np.float32(N_BUCKETS - max_exact)).astype(np.int32)
    large = np.minimum(large, N_BUCKETS - 1)
    bucket = np.where(d < max_exact, d, large)
    return np.where((dist >= 0) & (dist < BLOCK), bucket, -1).astype(np.int32)


def _build_bias(bucket_ref, rb_ref, bias_ref):
    bk = bucket_ref[...]
    for h in range(N_HEADS):
        def add(b, acc, h=h):
            return acc + jnp.where(bk == b, rb_ref[h, b], 0.0)
        bias_ref[h] = lax.fori_loop(0, N_BUCKETS, add, jnp.zeros((BLOCK, 2 * BLOCK), F32))


def _kv_variants(prev_ref, cur_ref):
    cat = jnp.concatenate([prev_ref[...], cur_ref[...]], axis=0)
    lo = lax.broadcasted_iota(jnp.int32, cat.shape, 1) < HEAD_DIM
    zero = jnp.zeros_like(cat)
    head0_lo = jnp.where(lo, cat, zero)
    head1_hi = jnp.where(lo, zero, cat)
    return ((head0_lo, pltpu.roll(head0_lo, HEAD_DIM, 1)), (pltpu.roll(head1_hi, HEAD_DIM, 1), head1_hi))


def _merge_kv_grads(g):
    lo = lax.broadcasted_iota(jnp.int32, g[0][0].shape, 1) < HEAD_DIM
    return jnp.where(lo, g[0][0] + pltpu.roll(g[0][1], HEAD_DIM, 1), g[1][1] + pltpu.roll(g[1][0], HEAD_DIM, 1))


def _head_lanes(h):
    return slice((h // 2) * LANES, (h // 2 + 1) * LANES)


def _attn_probs(q_ref, kvar, bias_ref, sk_ref, valid, s_ref):
    for h in range(N_HEADS):
        s_ref[h] = _dot_nt(q_ref[:, _head_lanes(h)], kvar[h // 4][h % 2])
    head = lax.broadcasted_iota(jnp.int32, (N_HEADS, 1, 1), 0)
    sink = jnp.zeros((N_HEADS, 1, 1), F32)
    for h in range(N_HEADS):
        sink = jnp.where(head == h, sk_ref[0, h], sink)
    s = jnp.where(valid[None], s_ref[...] * (HEAD_DIM ** -0.5) + bias_ref[...], NEG_INF)
    m = jnp.maximum(jnp.max(s, axis=-1, keepdims=True), sink)
    p = jnp.exp(s - m)
    e_sink = jnp.exp(sink - m)
    inv = 1.0 / (jnp.sum(p, axis=-1, keepdims=True) + e_sink)
    return p * inv, e_sink * inv


def _attn_valid(bucket_ref, n):
    col = lax.broadcasted_iota(jnp.int32, (BLOCK, 2 * BLOCK), 1)
    return (bucket_ref[...] >= 0) & ((n > 0) | (col >= BLOCK))


def _attn_fwd(q, k, v, bucket, rel_bias, sinks, carry=None):
    T = q.shape[0]
    nb = T // BLOCK

    def body(q_ref, kc_ref, kp_ref, vc_ref, vp_ref, bucket_ref, rb_ref, sk_ref, o_ref, bias_ref, s_ref, p_ref):
        n = pl.program_id(0)

        @pl.when(n == 0)
        def _():
            _build_bias(bucket_ref, rb_ref, bias_ref)

        kvar = _kv_variants(kp_ref, kc_ref)
        vvar = _kv_variants(vp_ref, vc_ref)
        pr, _ = _attn_probs(q_ref, kvar, bias_ref, sk_ref, _attn_valid(bucket_ref, n), s_ref)
        p_ref[...] = pr.astype(BF16)
        for m in range(N_HEADS // 2):
            acc = _dot(p_ref[2 * m], vvar[m // 2][0]) + _dot(p_ref[2 * m + 1], vvar[m // 2][1])
            o_ref[:, m * LANES:(m + 1) * LANES] = acc.astype(o_ref.dtype)

    cur = lambda w: pl.BlockSpec((BLOCK, w), lambda n: (n, 0))
    prev = lambda w: pl.BlockSpec((BLOCK, w), lambda n: (jnp.maximum(n - 1, 0), 0))
    smem = pl.BlockSpec(memory_space=pltpu.SMEM)
    return _hosted_call(
        body, carry, _edge_1d(nb), name="attn_fwd", grid=(nb,),
        in_specs=[cur(ATTN_W), cur(KV_W), prev(KV_W), cur(KV_W), prev(KV_W), _const_spec((BLOCK, 2 * BLOCK)), smem, smem],
        out_specs=[cur(ATTN_W)],
        out_shape=[jax.ShapeDtypeStruct((T, ATTN_W), BF16)],
        scratch_shapes=[pltpu.VMEM((N_HEADS, BLOCK, 2 * BLOCK), F32), pltpu.VMEM((N_HEADS, BLOCK, 2 * BLOCK), F32),
                        pltpu.VMEM((N_HEADS, BLOCK, 2 * BLOCK), BF16)],
        compiler_params=_params(("arbitrary",)), inputs=(q, k, k, v, v, bucket, rel_bias, sinks))


ATTN_SMALL_ROWS = N_BUCKETS + SUBLANES


def _attn_bwd(q, k, v, datt, bucket, rel_bias, sinks, carry=None):
    T = q.shape[0]
    nb = T // BLOCK

    def body(q_ref, do_ref, kc_ref, kp_ref, vc_ref, vp_ref, bucket_ref, rb_ref, sk_ref,
             dq_ref, dkv_ref, small_ref, bias_ref, ds_sum_ref, dsink_ref, kcarry_ref, vcarry_ref,
             s_ref, dp_ref, p_ref, dsc_ref):
        n = pl.program_id(0)

        @pl.when(n == 0)
        def _():
            _build_bias(bucket_ref, rb_ref, bias_ref)
            ds_sum_ref[...] = jnp.zeros_like(ds_sum_ref)
            dsink_ref[...] = jnp.zeros_like(dsink_ref)
            kcarry_ref[...] = jnp.zeros_like(kcarry_ref)
            vcarry_ref[...] = jnp.zeros_like(vcarry_ref)

        @pl.when(n < nb)
        def _():
            kvar = _kv_variants(kp_ref, kc_ref)
            vvar = _kv_variants(vp_ref, vc_ref)
            pr, p_sink = _attn_probs(q_ref, kvar, bias_ref, sk_ref, _attn_valid(bucket_ref, n), s_ref)
            for h in range(N_HEADS):
                dp_ref[h] = _dot_nt(do_ref[:, _head_lanes(h)], vvar[h // 4][h % 2])
            dp = dp_ref[...]
            dsum = jnp.sum(pr * dp, axis=-1, keepdims=True)
            ds = pr * (dp - dsum)
            ds_sum_ref[...] += ds
            dsink_ref[...] -= jnp.sum(p_sink * dsum, axis=1, keepdims=True)
            dsc_ref[...] = (ds * (HEAD_DIM ** -0.5)).astype(BF16)
            p_ref[...] = pr.astype(BF16)
            for m in range(N_HEADS // 2):
                dqm = _dot(dsc_ref[2 * m], kvar[m // 2][0]) + _dot(dsc_ref[2 * m + 1], kvar[m // 2][1])
                dq_ref[:, m * LANES:(m + 1) * LANES] = dqm.astype(dq_ref.dtype)
            dk_var = [[None, None], [None, None]]
            dv_var = [[None, None], [None, None]]
            for kvh in range(2):
                for e in range(2):
                    heads = [h for h in range(N_HEADS) if h // 4 == kvh and h % 2 == e]
                    dk_var[kvh][e] = sum(_dot_tn(dsc_ref[h], q_ref[:, _head_lanes(h)]) for h in heads)
                    dv_var[kvh][e] = sum(_dot_tn(p_ref[h], do_ref[:, _head_lanes(h)]) for h in heads)
            dk_cat = _merge_kv_grads(dk_var)
            dv_cat = _merge_kv_grads(dv_var)

            @pl.when(n > 0)
            def _():
                dkv_ref[:, :KV_W] = (kcarry_ref[...] + dk_cat[:BLOCK]).astype(BF16)
                dkv_ref[:, KV_W:] = (vcarry_ref[...] + dv_cat[:BLOCK]).astype(BF16)

            kcarry_ref[...] = dk_cat[BLOCK:]
            vcarry_ref[...] = dv_cat[BLOCK:]

        @pl.when(n == nb)
        def _():
            dkv_ref[:, :KV_W] = kcarry_ref[...].astype(BF16)
            dkv_ref[:, KV_W:] = vcarry_ref[...].astype(BF16)
            bk = bucket_ref[...]
            row = lax.broadcasted_iota(jnp.int32, (N_HEADS, ATTN_SMALL_ROWS, LANES), 1)

            def add(b, acc):
                masked = jnp.where((bk == b)[None], ds_sum_ref[...], 0.0)
                val = jnp.sum(jnp.sum(masked, axis=1, keepdims=True), axis=2, keepdims=True)
                return acc + jnp.where(row == b, val, 0.0)

            small_ref[...] = lax.fori_loop(0, N_BUCKETS, add, jnp.where(row == N_BUCKETS, dsink_ref[...], 0.0))

    last = nb - 1
    cur = lambda w: pl.BlockSpec((BLOCK, w), lambda n: (jnp.minimum(n, last), 0))
    prev = lambda w: pl.BlockSpec((BLOCK, w), lambda n: (jnp.clip(n - 1, 0, last), 0))
    smem = pl.BlockSpec(memory_space=pltpu.SMEM)
    return _hosted_call(
        body, carry, _edge_1d(nb + 1), name="attn_bwd", grid=(nb + 1,),
        in_specs=[cur(ATTN_W), cur(ATTN_W), cur(KV_W), prev(KV_W), cur(KV_W), prev(KV_W),
                  _const_spec((BLOCK, 2 * BLOCK)), smem, smem],
        out_specs=[cur(ATTN_W), prev(2 * KV_W), pl.BlockSpec((N_HEADS, ATTN_SMALL_ROWS, LANES), lambda n: (0, 0, 0))],
        out_shape=[jax.ShapeDtypeStruct((T, ATTN_W), BF16), jax.ShapeDtypeStruct((T, 2 * KV_W), BF16),
                   jax.ShapeDtypeStruct((N_HEADS, ATTN_SMALL_ROWS, LANES), F32)],
        scratch_shapes=[pltpu.VMEM((N_HEADS, BLOCK, 2 * BLOCK), F32), pltpu.VMEM((N_HEADS, BLOCK, 2 * BLOCK), F32),
                        pltpu.VMEM((N_HEADS, 1, 1), F32), pltpu.VMEM((BLOCK, KV_W), F32), pltpu.VMEM((BLOCK, KV_W), F32),
                        pltpu.VMEM((N_HEADS, BLOCK, 2 * BLOCK), F32), pltpu.VMEM((N_HEADS, BLOCK, 2 * BLOCK), F32),
                        pltpu.VMEM((N_HEADS, BLOCK, 2 * BLOCK), BF16), pltpu.VMEM((N_HEADS, BLOCK, 2 * BLOCK), BF16)],
        compiler_params=_params(("arbitrary",)), inputs=(q, datt, k, k, v, v, bucket, rel_bias, sinks))


SCAN_UNROLL = 4


def _cmul(ar, ai, br, bi):
    return ar * br - ai * bi, ar * bi + ai * br


def _cmul_conj(ar, ai, br, bi):
    return ar * br + ai * bi, ar * bi - ai * br


def _ssm_discretize(lr, li, ldt):
    dt = jnp.exp(ldt)
    mag = jnp.exp(lr * dt)
    ab_re = mag * jnp.cos(li * dt)
    ab_im = mag * jnp.sin(li * dt)
    nr = ab_re - 1.0
    den = lr * lr + li * li
    f_re = (nr * lr + ab_im * li) / den
    f_im = (ab_im * lr - nr * li) / den
    return ab_re, ab_im, f_re, f_im


def _ssm_prep(lam_re, lam_im, ldt_rep, bd_re, bd_im):
    def body(lr_ref, li_ref, ldt_ref, bdr_ref, bdi_ref, ar_ref, ai_ref, br_ref, bi_ref):
        ab_re, ab_im, f_re, f_im = _ssm_discretize(lr_ref[...], li_ref[...], ldt_ref[...])
        ar_ref[...] = ab_re
        ai_ref[...] = ab_im
        bdr, bdi = bdr_ref[0], bdi_ref[0]
        br_ref[0] = (bdr * f_re - bdi * f_im).astype(BF16)
        bi_ref[0] = (bdi * f_re + bdr * f_im).astype(BF16)

    row = pl.BlockSpec((1, SSM_LANE_BLOCK), lambda j: (0, j))
    mat = pl.BlockSpec((1, LANES, SSM_LANE_BLOCK), lambda j: (j, 0, 0))
    return pl.pallas_call(
        body, name="ssm_prep", grid=(N_SSM_BLOCKS,),
        in_specs=[row, row, row, mat, mat], out_specs=[row, row, mat, mat],
        out_shape=[jax.ShapeDtypeStruct((1, STATES), F32)] * 2 + [jax.ShapeDtypeStruct((N_SSM_BLOCKS, LANES, SSM_LANE_BLOCK), BF16)] * 2,
        compiler_params=_params(("arbitrary",)),
    )(*_in_hbm(lam_re, lam_im, ldt_rep, bd_re, bd_im))


def _ssm_prep_bwd(lam_re, lam_im, ldt_rep, bd_re, bd_im, dbr, dbi, da_re, da_im):
    def body(lr_ref, li_ref, ldt_ref, bdr_ref, bdi_ref, dbr_ref, dbi_ref, dar_ref, dai_ref,
             dbdr_ref, dbdi_ref, dlr_ref, dli_ref, dldt_ref):
        lr, li, ldt = lr_ref[...], li_ref[...], ldt_ref[...]
        (_, _, f_re, f_im), vjp = jax.vjp(_ssm_discretize, lr, li, ldt)
        bdr, bdi, gbr, gbi = bdr_ref[0], bdi_ref[0], dbr_ref[0], dbi_ref[0]
        dbdr_ref[0] = gbr * f_re + gbi * f_im
        dbdi_ref[0] = gbi * f_re - gbr * f_im
        df_re = jnp.sum(gbr * bdr + gbi * bdi, axis=0, keepdims=True)
        df_im = jnp.sum(gbi * bdr - gbr * bdi, axis=0, keepdims=True)
        dlr, dli, dldt = vjp((dar_ref[...], dai_ref[...], df_re, df_im))
        dlr_ref[...] = dlr
        dli_ref[...] = dli
        dldt_ref[...] = dldt

    row = pl.BlockSpec((1, SSM_LANE_BLOCK), lambda j: (0, j))
    mat = pl.BlockSpec((1, LANES, SSM_LANE_BLOCK), lambda j: (j, 0, 0))
    mat_shape = jax.ShapeDtypeStruct((N_SSM_BLOCKS, LANES, SSM_LANE_BLOCK), F32)
    row_shape = jax.ShapeDtypeStruct((1, STATES), F32)
    return pl.pallas_call(
        body, name="ssm_prep_bwd", grid=(N_SSM_BLOCKS,),
        in_specs=[row, row, row, mat, mat, mat, mat, row, row], out_specs=[mat, mat, row, row, row],
        out_shape=[mat_shape, mat_shape, row_shape, row_shape, row_shape],
        compiler_params=_params(("arbitrary",)),
    )(*_in_hbm(lam_re, lam_im, ldt_rep, bd_re, bd_im, dbr, dbi, da_re, da_im))


def _group_sum(x):
    def body(x_ref, o_ref):
        o_ref[...] = jnp.sum(x_ref[...], axis=1, keepdims=True)
    return pl.pallas_call(body, name="ssm_group_sum", grid=(1,), in_specs=[_whole(x.shape)], out_specs=_whole((N_GROUPS, 1)),
                          out_shape=jax.ShapeDtypeStruct((N_GROUPS, 1), F32))(*_in_hbm(x))


def _power_table(ar, ai, p_re_ref, p_im_ref, steps):
    shape = (SUBLANES, SSM_LANE_BLOCK)
    p_re_ref[0:SUBLANES] = jnp.broadcast_to(ar, shape)
    p_im_ref[0:SUBLANES] = jnp.broadcast_to(ai, shape)
    m = 1
    while m < steps:
        rows = m * SUBLANES
        top_re = p_re_ref[rows - SUBLANES:rows]
        top_im = p_im_ref[rows - SUBLANES:rows]
        cur_re = p_re_ref[0:rows].reshape(m, SUBLANES, SSM_LANE_BLOCK)
        cur_im = p_im_ref[0:rows].reshape(m, SUBLANES, SSM_LANE_BLOCK)
        nxt_re, nxt_im = _cmul(cur_re, cur_im, top_re[None], top_im[None])
        p_re_ref[rows:2 * rows] = nxt_re.reshape(rows, SSM_LANE_BLOCK)
        p_im_ref[rows:2 * rows] = nxt_im.reshape(rows, SSM_LANE_BLOCK)
        m *= 2


def _to_segments(src_ref, dst_ref, steps):
    for s in range(SUBLANES):
        dst_ref[pl.ds(s, steps, stride=SUBLANES), :] = src_ref[s * steps:(s + 1) * steps, :]


def _from_segments(src_ref, dst_ref, steps):
    for s in range(SUBLANES):
        dst_ref[s * steps:(s + 1) * steps, :] = src_ref[pl.ds(s, steps, stride=SUBLANES), :]


def _segment_carries(e_re, e_im, an_re, an_im, c_re, c_im, reverse):
    order = range(SUBLANES - 1, -1, -1) if reverse else range(SUBLANES)
    ins_re, ins_im = [None] * SUBLANES, [None] * SUBLANES
    for s in order:
        ins_re[s], ins_im[s] = c_re, c_im
        pr, pi = _cmul(an_re, an_im, c_re, c_im)
        c_re = e_re[s:s + 1] + pr
        c_im = e_im[s:s + 1] + pi
    return jnp.concatenate(ins_re, axis=0), jnp.concatenate(ins_im, axis=0), c_re, c_im


def _ssm_fwd(u, a_re, a_im, b_re, b_im, c_re, c_im, d_skip, chunk, carry=None):
    T = u.shape[0]
    nc = T // chunk
    steps = chunk // SUBLANES
    blk = SSM_LANE_BLOCK

    def body(u_ref, ar_ref, ai_ref, br_ref, bi_ref, cr_ref, ci_ref, dk_ref,
             y_ref, hr_ref, hi_ref, inr_ref, ini_ref, useg_ref, yseg_ref, pr_ref, pi_ref, carry_ref):
        c = pl.program_id(1)
        ar, ai = ar_ref[...], ai_ref[...]

        @pl.when(c == 0)
        def _():
            _power_table(ar, ai, pr_ref, pi_ref, steps)
            carry_ref[...] = jnp.zeros_like(carry_ref)

        _to_segments(u_ref, useg_ref, steps)
        ub = useg_ref[...].astype(BF16)
        hr_ref[...] = _dot(ub, br_ref[0])
        hi_ref[...] = _dot(ub, bi_ref[0])
        first = slice(0, SUBLANES)

        def scan(t4, prev):
            for j in range(SCAN_UNROLL):
                rows = pl.ds(pl.multiple_of((t4 * SCAN_UNROLL + j) * SUBLANES, SUBLANES), SUBLANES)
                pr, pi = _cmul(pr_ref[first, :], pi_ref[first, :], prev[0], prev[1])
                prev = (pr + hr_ref[rows, :], pi + hi_ref[rows, :])
                hr_ref[rows, :] = prev[0]
                hi_ref[rows, :] = prev[1]
            return prev

        zero = jnp.zeros((SUBLANES, blk), F32)
        lax.fori_loop(0, steps // SCAN_UNROLL, scan, (zero, zero))

        top = slice(chunk - SUBLANES, chunk)
        in_re, in_im, out_re, out_im = _segment_carries(
            hr_ref[top, :], hi_ref[top, :], pr_ref[top, :][0:1], pi_ref[top, :][0:1],
            carry_ref[0:1, :], carry_ref[1:2, :], reverse=False)
        carry_ref[0:1, :] = out_re
        carry_ref[1:2, :] = out_im
        inr_ref[...] = in_re
        ini_ref[...] = in_im

        def fix(t4, _):
            for j in range(SCAN_UNROLL):
                rows = pl.ds(pl.multiple_of((t4 * SCAN_UNROLL + j) * SUBLANES, SUBLANES), SUBLANES)
                fr, fi = _cmul(pr_ref[rows, :], pi_ref[rows, :], in_re, in_im)
                hr_ref[rows, :] += fr
                hi_ref[rows, :] += fi
            return 0

        lax.fori_loop(0, steps // SCAN_UNROLL, fix, 0)

        yseg_ref[...] = _dot(hr_ref[...].astype(BF16), cr_ref[0]) - _dot(hi_ref[...].astype(BF16), ci_ref[0])
        _from_segments(yseg_ref, y_ref, steps)
        y_ref[...] += dk_ref[...] * u_ref[...]

    row = pl.BlockSpec((1, blk), lambda j, c: (0, j))
    b_mat = pl.BlockSpec((1, LANES, blk), lambda j, c: (j, 0, 0))
    c_mat = pl.BlockSpec((1, blk, LANES), lambda j, c: (j, 0, 0))
    tok = pl.BlockSpec((chunk, LANES), lambda j, c: (c, j))
    state = pl.BlockSpec((chunk, blk), lambda j, c: (c, j))
    enter = pl.BlockSpec((SUBLANES, blk), lambda j, c: (c, j))
    return _hosted_call(
        body, carry, _edge_2d(N_SSM_BLOCKS, nc), name="ssm_fwd", grid=(N_SSM_BLOCKS, nc),
        in_specs=[tok, row, row, b_mat, b_mat, c_mat, c_mat, pl.BlockSpec((1, LANES), lambda j, c: (0, j))],
        out_specs=[tok, state, state, enter, enter],
        out_shape=[jax.ShapeDtypeStruct((T, SSM_W), F32), jax.ShapeDtypeStruct((T, STATES), F32),
                   jax.ShapeDtypeStruct((T, STATES), F32), jax.ShapeDtypeStruct((nc * SUBLANES, STATES), F32),
                   jax.ShapeDtypeStruct((nc * SUBLANES, STATES), F32)],
        scratch_shapes=[pltpu.VMEM((chunk, LANES), F32), pltpu.VMEM((chunk, LANES), F32),
                        pltpu.VMEM((chunk, blk), F32), pltpu.VMEM((chunk, blk), F32), pltpu.VMEM((SUBLANES, blk), F32)],
        compiler_params=_params(("arbitrary", "arbitrary"), VMEM_MID),
        inputs=(u, a_re, a_im, b_re, b_im, c_re, c_im, d_skip))


def _ssm_bwd(dy, u, h_re, h_im, in_re, in_im, a_re, a_im, b_re, b_im, c_re, c_im, d_skip, chunk, carry=None):
    T = u.shape[0]
    nc = T // chunk
    steps = chunk // SUBLANES
    blk = SSM_LANE_BLOCK

    def body(dy_ref, u_ref, hr_ref, hi_ref, inr_ref, ini_ref, ar_ref, ai_ref, br_ref, bi_ref, cr_ref, ci_ref, dk_ref,
             du_ref, dbr_ref, dbi_ref, dcr_ref, dci_ref, dar_ref, dai_ref, ddk_ref,
             dyseg_ref, useg_ref, duseg_ref, gr_ref, gi_ref, pr_ref, pi_ref, carry_ref, accr_ref, acci_ref):
        c = pl.program_id(1)
        ar, ai = ar_ref[...], ai_ref[...]

        @pl.when(c == 0)
        def _():
            _power_table(ar, ai, pr_ref, pi_ref, steps)
            carry_ref[...] = jnp.zeros_like(carry_ref)
            accr_ref[...] = jnp.zeros_like(accr_ref)
            acci_ref[...] = jnp.zeros_like(acci_ref)

        _to_segments(dy_ref, dyseg_ref, steps)
        _to_segments(u_ref, useg_ref, steps)
        dyb = dyseg_ref[...].astype(BF16)
        ub = useg_ref[...].astype(BF16)
        gr_ref[...] = _dot_nt(dyb, cr_ref[0])
        gi_ref[...] = -_dot_nt(dyb, ci_ref[0])
        dcr = _dot_tn(hr_ref[...].astype(BF16), dyb)
        dci = -_dot_tn(hi_ref[...].astype(BF16), dyb)
        ddk = jnp.sum(dy_ref[...] * u_ref[...], axis=0, keepdims=True)

        first = slice(0, SUBLANES)

        def scan(k4, nxt):
            for j in range(SCAN_UNROLL):
                t = steps - 1 - (k4 * SCAN_UNROLL + j)
                rows = pl.ds(pl.multiple_of(t * SUBLANES, SUBLANES), SUBLANES)
                pr, pi = _cmul_conj(pr_ref[first, :], pi_ref[first, :], nxt[0], nxt[1])
                nxt = (pr + gr_ref[rows, :], pi + gi_ref[rows, :])
                gr_ref[rows, :] = nxt[0]
                gi_ref[rows, :] = nxt[1]
            return nxt

        top = slice(chunk - SUBLANES, chunk)
        zero = jnp.zeros((SUBLANES, blk), F32)
        lax.fori_loop(0, steps // SCAN_UNROLL, scan, (zero, zero))

        gin_re, gin_im, out_re, out_im = _segment_carries(
            gr_ref[0:SUBLANES, :], gi_ref[0:SUBLANES, :], pr_ref[top, :][0:1], -pi_ref[top, :][0:1],
            carry_ref[0:1, :], carry_ref[1:2, :], reverse=True)
        carry_ref[0:1, :] = out_re
        carry_ref[1:2, :] = out_im

        def fix_row(rows, prow, hp_re, hp_im, acc):
            fr, fi = _cmul_conj(pr_ref[prow, :], pi_ref[prow, :], gin_re, gin_im)
            g_re = gr_ref[rows, :] + fr
            g_im = gi_ref[rows, :] + fi
            gr_ref[rows, :] = g_re
            gi_ref[rows, :] = g_im
            return acc[0] + g_re * hp_re + g_im * hp_im, acc[1] + g_im * hp_re - g_re * hp_im

        def fix_at(t, acc):
            aligned = (lambda r: r * SUBLANES) if isinstance(t, int) else (lambda r: pl.multiple_of(r * SUBLANES, SUBLANES))
            rows, before, prow = (pl.ds(aligned(r), SUBLANES) for r in (t, t - 1, steps - 1 - t))
            return fix_row(rows, prow, hr_ref[before, :], hi_ref[before, :], acc)

        def fix(t4, acc):
            for j in range(SCAN_UNROLL):
                acc = fix_at(t4 * SCAN_UNROLL + j, acc)
            return acc

        acc = fix_row(first, top, inr_ref[...], ini_ref[...], (accr_ref[...], acci_ref[...]))
        for t in range(1, SCAN_UNROLL):
            acc = fix_at(t, acc)
        acc_re, acc_im = lax.fori_loop(1, steps // SCAN_UNROLL, fix, acc)
        accr_ref[...] = acc_re
        acci_ref[...] = acc_im

        gbr = gr_ref[...].astype(BF16)
        gbi = gi_ref[...].astype(BF16)
        duseg_ref[...] = _dot_nt(gbr, br_ref[0]) + _dot_nt(gbi, bi_ref[0])
        _from_segments(duseg_ref, dyseg_ref, steps)
        du_ref[...] = (dyseg_ref[...] + dk_ref[...] * dy_ref[...]).astype(BF16)
        dbr = _dot_tn(ub, gbr)
        dbi = _dot_tn(ub, gbi)

        @pl.when(c == 0)
        def _():
            dbr_ref[0] = dbr
            dbi_ref[0] = dbi
            dcr_ref[0] = dcr
            dci_ref[0] = dci
            ddk_ref[...] = ddk

        @pl.when(c > 0)
        def _():
            dbr_ref[0] += dbr
            dbi_ref[0] += dbi
            dcr_ref[0] += dcr
            dci_ref[0] += dci
            ddk_ref[...] += ddk

        @pl.when(c == nc - 1)
        def _():
            dar_ref[...] = jnp.sum(acc_re, axis=0, keepdims=True)
            dai_ref[...] = jnp.sum(acc_im, axis=0, keepdims=True)

    rev = lambda c: nc - 1 - c
    row = pl.BlockSpec((1, blk), lambda j, c: (0, j))
    b_mat = pl.BlockSpec((1, LANES, blk), lambda j, c: (j, 0, 0))
    c_mat = pl.BlockSpec((1, blk, LANES), lambda j, c: (j, 0, 0))
    tok = pl.BlockSpec((chunk, LANES), lambda j, c: (rev(c), j))
    state = pl.BlockSpec((chunk, blk), lambda j, c: (rev(c), j))
    enter = pl.BlockSpec((SUBLANES, blk), lambda j, c: (rev(c), j))
    chan = pl.BlockSpec((1, LANES), lambda j, c: (0, j))
    f32 = lambda *s: jax.ShapeDtypeStruct(s, F32)
    return _hosted_call(
        body, carry, _edge_2d(N_SSM_BLOCKS, nc), name="ssm_bwd", grid=(N_SSM_BLOCKS, nc),
        in_specs=[tok, tok, state, state, enter, enter, row, row, b_mat, b_mat, c_mat, c_mat, chan],
        out_specs=[tok, b_mat, b_mat, c_mat, c_mat, row, row, chan],
        out_shape=[jax.ShapeDtypeStruct((T, SSM_W), BF16), f32(N_SSM_BLOCKS, LANES, blk), f32(N_SSM_BLOCKS, LANES, blk),
                   f32(N_SSM_BLOCKS, blk, LANES), f32(N_SSM_BLOCKS, blk, LANES), f32(1, STATES), f32(1, STATES), f32(1, SSM_W)],
        scratch_shapes=[pltpu.VMEM((chunk, LANES), F32), pltpu.VMEM((chunk, LANES), F32), pltpu.VMEM((chunk, LANES), F32),
                        pltpu.VMEM((chunk, blk), F32), pltpu.VMEM((chunk, blk), F32),
                        pltpu.VMEM((chunk, blk), F32), pltpu.VMEM((chunk, blk), F32),
                        pltpu.VMEM((SUBLANES, blk), F32), pltpu.VMEM((SUBLANES, blk), F32), pltpu.VMEM((SUBLANES, blk), F32)],
        compiler_params=_params(("arbitrary", "arbitrary"), VMEM_BIG),
        inputs=(dy, u, h_re, h_im, in_re, in_im, a_re, a_im, b_re, b_im, c_re, c_im, d_skip))


def _merge_forward(y, att, ga, gs, w_glu, w_ssm, w_attn):
    z = jax.nn.gelu(y)
    zb = z.astype(BF16)
    gl = jax.nn.sigmoid(_dot(zb, w_glu))
    z2b = (z * gl).astype(BF16)
    y_ssm = _dot(z2b, w_ssm)
    y_attn = _dot(att, w_attn)
    sa = jax.nn.sigmoid(ga)
    ss = jax.nn.sigmoid(gs)
    merged = (sa * y_attn + ss * y_ssm).astype(BF16)
    return z, zb, gl, z2b, y_ssm, y_attn, sa, ss, merged


def _merge_fwd(x, y, att, ga, gs, g2, g3, w_glu, w_ssm, w_attn, w_out, tile):
    T = x.shape[0]

    def body(x_ref, y_ref, att_ref, ga_ref, gs_ref, g2_ref, g3_ref, wg_ref, ws_ref, wa_ref, wo_ref, x1_ref, o_ref, h2_ref):
        merged = _merge_forward(y_ref[...], att_ref[...], ga_ref[...], gs_ref[...], wg_ref[...], ws_ref[...], wa_ref[...])[-1]
        o = _dot(merged, wo_ref[...])
        x1 = x_ref[...] + o * _rms_scale(o) * g2_ref[...]
        o_ref[...] = o
        x1_ref[...] = x1
        h2_ref[...] = (x1 * _rms_scale(x1) * g3_ref[...]).astype(BF16)

    tok = lambda w: pl.BlockSpec((tile, w), lambda i: (i, 0))
    vec = _const_spec((1, D_MODEL))
    return pl.pallas_call(
        body, name="merge_fwd", grid=(T // tile,),
        in_specs=[tok(D_MODEL), tok(SSM_W), tok(ATTN_W), tok(D_MODEL), tok(D_MODEL), vec, vec,
                  _const_spec((SSM_W, SSM_W)), _const_spec((SSM_W, D_MODEL)), _const_spec((ATTN_W, D_MODEL)),
                  _const_spec((D_MODEL, D_MODEL))],
        out_specs=[tok(D_MODEL), tok(D_MODEL), tok(D_MODEL)],
        out_shape=_hbm_out([jax.ShapeDtypeStruct((T, D_MODEL), F32), jax.ShapeDtypeStruct((T, D_MODEL), F32),
                            jax.ShapeDtypeStruct((T, D_MODEL), BF16)]),
        compiler_params=_params(("arbitrary",), VMEM_MID),
    )(*_in_hbm(x, y, att, ga, gs, g2, g3, w_glu, w_ssm, w_attn, w_out))


def _merge_bwd(dh2, dx2, x1, o, y, att, ga, gs, g2, g3, w_glu, w_ssm, w_attn, w_out, tile, carry=None):
    T = x1.shape[0]
    n_steps = T // tile

    group = min(2, n_steps)
    staged_widths = (D_MODEL, D_MODEL, ATTN_W, D_MODEL, SSM_W, D_MODEL, SSM_W, SSM_W)

    def body(dh2_ref, dx2_ref, x1_ref, o_ref, y_ref, att_ref, ga_ref, gs_ref, g2_ref, g3_ref, wg_ref, ws_ref, wa_ref, wo_ref,
             dx1_ref, dgates_ref, datt_ref, dy_ref, dwg_hbm, dws_hbm, dwa_hbm, dwo_hbm, dg2_ref, dg3_ref,
             awg_ref, aws_ref, awa_ref, awo_ref, *staged):
        i = pl.program_id(0)
        x1v, ov = x1_ref[...], o_ref[...]
        dxn, dg3 = _rms_bwd(dh2_ref[...], x1v, _rms_scale(x1v), g3_ref[...])
        dx1 = dx2_ref[...] + dxn
        dx1_ref[...] = dx1
        do, dg2 = _rms_bwd(dx1, ov, _rms_scale(ov), g2_ref[...])
        dob = do.astype(BF16)

        yv = y_ref[...]
        att = att_ref[...]
        z, zb, gl, z2b, y_ssm, y_attn, sa, ss, merged = _merge_forward(
            yv, att, ga_ref[...], gs_ref[...], wg_ref[...], ws_ref[...], wa_ref[...])
        dmerged = _dot_nt(dob, wo_ref[...])
        dya = (dmerged * sa).astype(BF16)
        dys = (dmerged * ss).astype(BF16)
        dgates_ref[:, :D_MODEL] = (dmerged * y_attn * sa * (1.0 - sa)).astype(BF16)
        dgates_ref[:, D_MODEL:] = (dmerged * y_ssm * ss * (1.0 - ss)).astype(BF16)
        datt_ref[...] = _dot_nt(dya, wa_ref[...]).astype(BF16)
        dz2 = _dot_nt(dys, ws_ref[...])
        dpre = (dz2 * z * gl * (1.0 - gl)).astype(BF16)
        dz = dz2 * gl + _dot_nt(dpre, wg_ref[...])
        _, gelu_vjp = jax.vjp(jax.nn.gelu, yv)
        dy_ref[...] = gelu_vjp(dz)[0]

        part = pl.ds(pl.multiple_of((i % group) * tile, tile), tile)
        for ref, val in zip(staged, (merged, dob, att, dya, z2b, dys, zb, dpre)):
            ref[part, :] = val

        @pl.when(i == 0)
        def _():
            dg2_ref[...] = dg2
            dg3_ref[...] = dg3

        @pl.when(i > 0)
        def _():
            dg2_ref[...] += dg2
            dg3_ref[...] += dg3

        def weight_grads():
            s_merged, s_dob, s_att, s_dya, s_z2b, s_dys, s_zb, s_dpre = (ref[...] for ref in staged)
            return ((awo_ref, _dot_tn(s_merged, s_dob)), (awa_ref, _dot_tn(s_att, s_dya)),
                    (aws_ref, _dot_tn(s_z2b, s_dys)), (awg_ref, _dot_tn(s_zb, s_dpre)))

        @pl.when(i == group - 1)
        def _():
            for ref, val in weight_grads():
                ref[...] = val

        @pl.when((i % group == group - 1) & (i > group - 1))
        def _():
            for ref, val in weight_grads():
                ref[...] += val

        @pl.when(i == n_steps - 1)
        def _():
            pltpu.sync_copy(awg_ref, dwg_hbm)
            pltpu.sync_copy(aws_ref, dws_hbm)
            pltpu.sync_copy(awa_ref, dwa_hbm)
            pltpu.sync_copy(awo_ref, dwo_hbm)

    tok = lambda w: pl.BlockSpec((tile, w), lambda i: (i, 0))
    vec = _const_spec((1, D_MODEL))
    any_ = pl.BlockSpec(memory_space=pl.ANY)
    vec_out = pl.BlockSpec((1, D_MODEL), lambda i: (0, 0))
    f32 = lambda *s: jax.ShapeDtypeStruct(s, F32)
    bf = lambda *s: jax.ShapeDtypeStruct(s, BF16)
    return _hosted_call(
        body, carry, _edge_1d(n_steps), name="merge_bwd", grid=(n_steps,),
        in_specs=[tok(D_MODEL), tok(D_MODEL), tok(D_MODEL), tok(D_MODEL), tok(SSM_W), tok(ATTN_W), tok(D_MODEL), tok(D_MODEL),
                  vec, vec, _const_spec((SSM_W, SSM_W)), _const_spec((SSM_W, D_MODEL)), _const_spec((ATTN_W, D_MODEL)),
                  _const_spec((D_MODEL, D_MODEL))],
        out_specs=[tok(D_MODEL), tok(2 * D_MODEL), tok(ATTN_W), tok(SSM_W), any_, any_, any_, any_, vec_out, vec_out],
        out_shape=[f32(T, D_MODEL), bf(T, 2 * D_MODEL), bf(T, ATTN_W), f32(T, SSM_W),
                   f32(SSM_W, SSM_W), f32(SSM_W, D_MODEL), f32(ATTN_W, D_MODEL), f32(D_MODEL, D_MODEL),
                   f32(1, D_MODEL), f32(1, D_MODEL)],
        scratch_shapes=[pltpu.VMEM((SSM_W, SSM_W), F32), pltpu.VMEM((SSM_W, D_MODEL), F32),
                        pltpu.VMEM((ATTN_W, D_MODEL), F32), pltpu.VMEM((D_MODEL, D_MODEL), F32)]
        + [pltpu.VMEM((group * tile, wd), BF16) for wd in staged_widths],
        compiler_params=_params(("arbitrary",), VMEM_BIG),
        inputs=(dh2, dx2, x1, o, y, att, ga, gs, g2, g3, w_glu, w_ssm, w_attn, w_out))


FF_SHARD = D_FF // N_DEV


def _mlp_fwd(h2, x1, target, g4, w_ff_in, w_ff_out, tile):
    T = h2.shape[0]
    col_chunk = 2 * FF_SHARD

    def body(h2_ref, x1_ref, tg_ref, g4_ref, wi_ref, wo_ref, a_ref, dfo_ref, dx2_ref, loss_ref, dg4_ref, rr_ref):
        i = pl.program_id(0)
        h2v = h2_ref[...]
        for c in range(D_FF // col_chunk):
            cols = slice(c * col_chunk, (c + 1) * col_chunk)
            a = _dot_nt(h2v, wi_ref[cols, :])
            a_ref[:, cols] = a.astype(BF16)
            ra = jnp.maximum(a, 0.0)
            rr_ref[:, cols] = (ra * ra).astype(BF16)
        f = _dot(rr_ref[...], wo_ref[...])
        r = _rms_scale(f)
        g = g4_ref[...]
        err = x1_ref[...] + f * r * g - tg_ref[...]
        dx2 = err * (1.0 / D_MODEL)
        dx2_ref[...] = dx2
        dfo, dg = _rms_bwd(dx2, f, r, g)
        dfo_ref[...] = dfo.astype(BF16)
        row = lax.broadcasted_iota(jnp.int32, (SUBLANES, LANES), 0)
        col = lax.broadcasted_iota(jnp.int32, (SUBLANES, LANES), 1)
        loss = jnp.where((row == 0) & (col == 0), (0.5 / D_MODEL) * jnp.sum(err * err), 0.0)

        @pl.when(i == 0)
        def _():
            loss_ref[...] = loss
            dg4_ref[...] = dg

        @pl.when(i > 0)
        def _():
            loss_ref[...] += loss
            dg4_ref[...] += dg

    tok = pl.BlockSpec((tile, D_MODEL), lambda i: (i, 0))
    return pl.pallas_call(
        body, name="mlp_fwd", grid=(T // tile,),
        in_specs=[tok, tok, tok, _const_spec((1, D_MODEL)), _const_spec((D_FF, D_MODEL)), _const_spec((D_FF, D_MODEL))],
        out_specs=[pl.BlockSpec((tile, D_FF), lambda i: (i, 0)), tok, tok,
                   pl.BlockSpec((SUBLANES, LANES), lambda i: (0, 0)), pl.BlockSpec((1, D_MODEL), lambda i: (0, 0))],
        out_shape=_hbm_out([jax.ShapeDtypeStruct((T, D_FF), BF16), jax.ShapeDtypeStruct((T, D_MODEL), BF16),
                            jax.ShapeDtypeStruct((T, D_MODEL), F32), jax.ShapeDtypeStruct((SUBLANES, LANES), F32),
                            jax.ShapeDtypeStruct((1, D_MODEL), F32)]),
        scratch_shapes=[pltpu.VMEM((tile, D_FF), BF16)],
        compiler_params=_params(("arbitrary",), VMEM_MAX),
    )(*_in_hbm(h2, x1, target, g4, w_ff_in.reshape(D_FF, D_MODEL), w_ff_out.reshape(D_FF, D_MODEL)))


def _mlp_weight_grads(dfo, a, h2, w_ff_out, row_chunk):
    T = h2.shape[0]

    def body(dfo_ref, h2_ref, a_ref, wo_ref, dwi_ref, dwo_ref, da_ref, rr_ref):
        def rows(r, _):
            sl = pl.ds(pl.multiple_of(r * row_chunk, row_chunk), row_chunk)
            ra = jnp.maximum(a_ref[sl, :].astype(F32), 0.0)
            da_ref[sl, :] = (_dot_nt(dfo_ref[sl, :], wo_ref[0]) * (2.0 * ra)).astype(BF16)
            rr_ref[sl, :] = (ra * ra).astype(BF16)
            return 0

        lax.fori_loop(0, T // row_chunk, rows, 0)
        dwo_ref[0] = _dot_tn(rr_ref[...], dfo_ref[...])
        dwi_ref[0] = _dot_tn(h2_ref[...], da_ref[...])

    return pl.pallas_call(
        body, name="mlp_weight_grads", grid=(N_DEV,),
        in_specs=[_const_spec((T, D_MODEL)), _const_spec((T, D_MODEL)), pl.BlockSpec((T, FF_SHARD), lambda k: (0, k)),
                  pl.BlockSpec((1, FF_SHARD, D_MODEL), lambda k: (k, 0, 0))],
        out_specs=[pl.BlockSpec((1, D_MODEL, FF_SHARD), lambda k: (k, 0, 0)),
                   pl.BlockSpec((1, FF_SHARD, D_MODEL), lambda k: (k, 0, 0)), pl.BlockSpec((T, FF_SHARD), lambda k: (0, k))],
        out_shape=_hbm_out([jax.ShapeDtypeStruct((N_DEV, D_MODEL, FF_SHARD), F32),
                            jax.ShapeDtypeStruct((N_DEV, FF_SHARD, D_MODEL), F32), jax.ShapeDtypeStruct((T, D_FF), BF16)]),
        scratch_shapes=[pltpu.VMEM((T, FF_SHARD), BF16)],
        compiler_params=_params(("arbitrary",), VMEM_MAX),
    )(*_in_hbm(dfo, h2, a, w_ff_out))


def _mlp_input_grad(da, w_ff_in_t, tile):
    T = da.shape[0]

    def body(da_ref, w_ref, o_ref):
        o_ref[...] = _dot(da_ref[...], w_ref[...])

    return pl.pallas_call(
        body, name="mlp_input_grad", grid=(T // tile,),
        in_specs=[pl.BlockSpec((tile, D_FF), lambda i: (i, 0)), _const_spec((D_FF, D_MODEL))],
        out_specs=pl.BlockSpec((tile, D_MODEL), lambda i: (i, 0)),
        out_shape=_hbm_out(jax.ShapeDtypeStruct((T, D_MODEL), F32)),
        compiler_params=_params(("arbitrary",), VMEM_MID),
    )(*_in_hbm(da, w_ff_in_t))


def _block_diag_in(b):
    bt = b.reshape(N_SSM_BLOCKS, GROUPS_PER_BLOCK, GROUP_CH, N_STATE)
    eye = jnp.eye(GROUPS_PER_BLOCK, dtype=b.dtype)
    return jnp.einsum("jacp,ab->jacbp", bt, eye).reshape(N_SSM_BLOCKS, LANES, SSM_LANE_BLOCK)


def _block_diag_in_grad(g):
    g = g.reshape(N_SSM_BLOCKS, GROUPS_PER_BLOCK, GROUP_CH, GROUPS_PER_BLOCK, N_STATE)
    d = jnp.diagonal(g, axis1=1, axis2=3)
    return jnp.transpose(d, (0, 3, 1, 2)).reshape(N_GROUPS, GROUP_CH, N_STATE)


def _block_diag_out(c):
    ct = c.reshape(N_SSM_BLOCKS, GROUPS_PER_BLOCK, GROUP_CH, N_STATE)
    eye = jnp.eye(GROUPS_PER_BLOCK, dtype=c.dtype)
    return jnp.einsum("jacp,ab->japbc", ct, eye).reshape(N_SSM_BLOCKS, SSM_LANE_BLOCK, LANES)


def _block_diag_out_grad(g):
    g = g.reshape(N_SSM_BLOCKS, GROUPS_PER_BLOCK, N_STATE, GROUPS_PER_BLOCK, GROUP_CH)
    d = jnp.diagonal(g, axis1=1, axis2=3)
    return jnp.transpose(d, (0, 3, 2, 1)).reshape(N_GROUPS, GROUP_CH, N_STATE)


def _tiles(T):
    return dict(proj=min(512, T), proj_bwd=min(512, T // 2), merge=min(512, T), merge_bwd=min(256, T),
                mlp_fwd=min(512, T), mlp_bwd=min(512, T), ssm_chunk=min(1024, T))


def _mesh_position():
    x, y, c = lax.axis_index("x"), lax.axis_index("y"), lax.axis_index("c")
    other_chips = [(1 - x, y), (x, 1 - y), (1 - x, 1 - y)]
    return x, y, c, other_chips


def _gather_carry(arrays):
    n = len(arrays)

    def copies(ins, outs, sems):
        send_sems, recv_sems, local_sems = sems
        x, y, c, chips = _mesh_position()
        me, sibling = (x, y, c), (x, y, 1 - c)

        def copy(a, k, block, to, src=None):
            px, py, pc = block
            dst = outs[a].at[4 * px + 2 * py + pc]
            return pltpu.make_async_remote_copy(
                src_ref=dst if src is None else src, dst_ref=dst, send_sem=send_sems.at[7 * a + k],
                recv_sem=recv_sems.at[7 * a + k], device_id=to, device_id_type=MESH_IDS)

        mine = [pltpu.make_async_copy(ins[a], outs[a].at[4 * x + 2 * y + c], local_sems.at[a]) for a in range(n)]
        first = []
        for a in range(n):
            first.append(copy(a, 0, me, sibling, src=ins[a]))
            first += [copy(a, 1 + j, me, (*chip, c), src=ins[a]) for j, chip in enumerate(chips)]
        return copy, mine, first, me, sibling, chips, c

    def start(ins, outs, sems):
        _, mine, first, *_ = copies(ins, outs, sems)
        for cp in mine + first:
            cp.start()

    def finish(ins, outs, sems):
        copy, mine, first, me, sibling, chips, c = copies(ins, outs, sems)
        passed = []
        for a in range(n):
            for j, chip in enumerate(chips):
                copy(a, 1 + j, (*chip, c), me).wait_recv()
                passed.append(copy(a, 4 + j, (*chip, c), sibling))
                passed[-1].start()
        for a in range(n):
            copy(a, 0, sibling, me).wait_recv()
            for j, chip in enumerate(chips):
                copy(a, 4 + j, (*chip, 1 - c), me).wait_recv()
        for cp in first + passed:
            cp.wait_send()
        for cp in mine:
            cp.wait()

    return _Carry(arrays, [jax.ShapeDtypeStruct((N_DEV,) + a.shape, a.dtype) for a in arrays],
                  [pltpu.SemaphoreType.DMA((7 * n,)), pltpu.SemaphoreType.DMA((7 * n,)), pltpu.SemaphoreType.DMA((n,))],
                  start, finish)


def _pairwise_carry(arrays, n_slots, make_copies):
    n = len(arrays)

    def start(ins, outs, sems):
        for cp in make_copies(ins, outs, sems):
            cp.start()

    def finish(ins, outs, sems):
        for cp in make_copies(ins, outs, sems):
            cp.wait()

    return _Carry(arrays, [jax.ShapeDtypeStruct((n_slots,) + a.shape[1:], a.dtype) for a in arrays],
                  [pltpu.SemaphoreType.DMA((n_slots * n,)), pltpu.SemaphoreType.DMA((n_slots * n,))], start, finish)


def _sibling_carry(grads):
    def make_copies(ins, outs, sems):
        x, y, c, _ = _mesh_position()
        return [pltpu.make_async_remote_copy(
            src_ref=ins[a].at[2 * ch + (1 - c)], dst_ref=outs[a].at[ch], send_sem=sems[0].at[4 * a + ch],
            recv_sem=sems[1].at[4 * a + ch], device_id=(x, y, 1 - c), device_id_type=MESH_IDS)
            for a in range(len(grads)) for ch in range(4)]

    return _pairwise_carry(grads, 4, make_copies)


def _chips_carry(sums):
    def make_copies(ins, outs, sems):
        x, y, c, chips = _mesh_position()
        return [pltpu.make_async_remote_copy(
            src_ref=ins[a].at[2 * px + py], dst_ref=outs[a].at[j], send_sem=sems[0].at[3 * a + j],
            recv_sem=sems[1].at[3 * a + j], device_id=(px, py, c), device_id_type=MESH_IDS)
            for a in range(len(sums)) for j, (px, py) in enumerate(chips)]

    return _pairwise_carry(sums, 3, make_copies)


SEM_SPEC = pl.BlockSpec(memory_space=pltpu.SEMAPHORE)
DATAFLOW_EFFECT = pltpu.SideEffectType.DATAFLOW_SIDE_EFFECTING


def _exchange_start(carry, name, after=()):
    n = len(carry.inputs)
    lands = [lax.empty(s.shape, s.dtype) for s in carry.out_shapes]

    def body(*refs):
        first_out = 2 * n + len(after)
        srcs, zones, sems, token = refs[:n], refs[n:2 * n], refs[first_out:first_out + 2], refs[-1]
        carry.start(srcs, zones, sems)
        token[...] = jnp.zeros_like(token)

    outs = pl.pallas_call(
        body, name=name, in_specs=[HBM_SPEC] * (2 * n + len(after)),
        out_specs=[SEM_SPEC, SEM_SPEC] + [HBM_SPEC] * (2 * n) + [pl.BlockSpec(memory_space=pltpu.VMEM)],
        out_shape=list(carry.sems) + _hbm_out([jax.ShapeDtypeStruct(a.shape, a.dtype) for a in carry.inputs])
        + _hbm_out(carry.out_shapes) + [jax.ShapeDtypeStruct((SUBLANES, LANES), F32)],
        input_output_aliases={j: 2 + j for j in range(2 * n)},
        compiler_params=pltpu.CompilerParams(has_side_effects=DATAFLOW_EFFECT),
    )(*_in_hbm(*carry.inputs, *lands), *after)
    return outs[:-1], outs[-1]


def _exchange_wait(carry, in_flight, after, name):
    n = len(carry.inputs)
    sems, srcs, zones = in_flight[:2], in_flight[2:2 + n], in_flight[2 + n:]

    def body(*refs):
        src_refs, zone_refs, sem_refs = refs[:n], refs[n:2 * n], refs[2 * n:2 * n + 2]
        carry.finish(src_refs, zone_refs, sem_refs)

    outs = pl.pallas_call(
        body, name=name, in_specs=[HBM_SPEC] * (2 * n) + [SEM_SPEC, SEM_SPEC] + [HBM_SPEC] * len(after),
        out_specs=[HBM_SPEC] * (2 * n),
        out_shape=_hbm_out([jax.ShapeDtypeStruct(a.shape, a.dtype) for a in carry.inputs]) + _hbm_out(carry.out_shapes),
        input_output_aliases={j: j for j in range(2 * n)},
        compiler_params=pltpu.CompilerParams(has_side_effects=DATAFLOW_EFFECT),
    )(*srcs, *zones, *sems, *after)
    return list(outs[:n]), list(outs[n:])


def _add_sibling(grads8, recvs, core, row_tiles, name):
    k = len(grads8)
    g4 = [g.reshape(4, 2, *g.shape[1:]) for g in grads8]

    def body(core_ref, *refs):
        g_refs, r_refs, o_refs, ob_refs = (refs[j * k:(j + 1) * k] for j in range(4))
        for g_ref, r_ref, o_ref, ob_ref in zip(g_refs, r_refs, o_refs, ob_refs):
            s = g_ref[0] + r_ref[...]
            o_ref[...] = s
            ob_ref[...] = s.astype(BF16)

    def blocks(make):
        return [make(g.shape[1] // row_tiles, g.shape[2]) for g in grads8]

    slot = lambda tr, C: pl.BlockSpec((1, tr, C), lambda ch, r, core_ref: (ch, r, 0))
    outs = pl.pallas_call(
        body, name=name,
        grid_spec=pltpu.PrefetchScalarGridSpec(
            num_scalar_prefetch=1, grid=(4, row_tiles),
            in_specs=blocks(lambda tr, C: pl.BlockSpec((1, 1, tr, C), lambda ch, r, core_ref: (ch, core_ref[0], r, 0)))
            + blocks(slot), out_specs=blocks(slot) + blocks(slot)),
        out_shape=_hbm_out([jax.ShapeDtypeStruct((4,) + g.shape[1:], F32) for g in grads8]
                           + [jax.ShapeDtypeStruct((4,) + g.shape[1:], BF16) for g in grads8]),
        compiler_params=_params(("arbitrary", "arbitrary")),
    )(core, *_in_hbm(*g4, *recvs))
    return list(outs[:k]), list(outs[k:])


def _adam_math(w, g, m, v):
    m = ADAM_B1 * m + (1.0 - ADAM_B1) * g
    v = ADAM_B2 * v + (1.0 - ADAM_B2) * jnp.square(g)
    m_hat = m / (1.0 - ADAM_B1 ** ADAM_STEP)
    v_hat = v / (1.0 - ADAM_B2 ** ADAM_STEP)
    delta = -ADAM_LR * (m_hat / (jnp.sqrt(v_hat) + ADAM_EPS) + ADAM_WD * w)
    return delta, m, v


def _adam_big(ws, ms, vs, chip_sums, recvs, chip, row_tiles, name, after=()):
    k = len(ws)

    def body(chip_ref, *refs):
        refs = refs[:5 * k] + refs[5 * k + len(after):]
        w_refs, m_refs, v_refs, s_refs, r_refs, g_refs, d_refs, nm_refs, nv_refs = (refs[j * k:(j + 1) * k] for j in range(9))
        for a in range(k):
            r_ref = r_refs[a]
            g = s_refs[a][0] + r_ref[0].astype(F32) + r_ref[1].astype(F32) + r_ref[2].astype(F32)
            g_refs[a][...] = g
            d_refs[a][...], nm_refs[a][...], nv_refs[a][...] = _adam_math(w_refs[a][...], g, m_refs[a][...], v_refs[a][...])

    def blocks(make):
        return [make(w.shape[0] // row_tiles, w.shape[1]) for w in ws]

    blk = lambda tr, C: pl.BlockSpec((tr, C), lambda r, chip_ref: (r, 0))
    outs = pl.pallas_call(
        body, name=name,
        grid_spec=pltpu.PrefetchScalarGridSpec(
            num_scalar_prefetch=1, grid=(row_tiles,),
            in_specs=blocks(blk) * 3 + blocks(lambda tr, C: pl.BlockSpec((1, tr, C), lambda r, chip_ref: (chip_ref[0], r, 0)))
            + blocks(lambda tr, C: pl.BlockSpec((3, tr, C), lambda r, chip_ref: (0, r, 0))) + [HBM_SPEC] * len(after),
            out_specs=blocks(blk) * 4),
        out_shape=[jax.ShapeDtypeStruct(w.shape, F32) for w in ws] * 4,
        compiler_params=_params(("arbitrary",)),
    )(chip, *_in_hbm(*ws, *ms, *vs, *chip_sums, *recvs), *after)
    return [list(outs[j * k:(j + 1) * k]) for j in range(4)]


def _sum_partials(partials, name):
    def body(p_ref, g_ref):
        g = p_ref[0]
        for d in range(1, partials.shape[0]):
            g = g + p_ref[d]
        g_ref[...] = g

    return pl.pallas_call(body, name=name, grid=(1,), in_specs=[_whole(partials.shape)], out_specs=_whole(partials.shape[1:]),
                          out_shape=jax.ShapeDtypeStruct(partials.shape[1:], F32))(*_in_hbm(partials))


def _adam_small(ws, ms, vs, gs):
    n = len(ws)

    def body(*refs):
        w_refs, m_refs, v_refs, g_refs = (refs[i * n:(i + 1) * n] for i in range(4))
        d_refs, nm_refs, nv_refs = (refs[(4 + i) * n:(5 + i) * n] for i in range(3))
        for j in range(n):
            d_refs[j][...], nm_refs[j][...], nv_refs[j][...] = _adam_math(
                w_refs[j][...], g_refs[j][...], m_refs[j][...], v_refs[j][...])

    specs = [_whole(w.shape) for w in ws]
    outs = pl.pallas_call(body, name="adam_small", grid=(1,), in_specs=specs * 4, out_specs=specs * 3,
                          out_shape=[jax.ShapeDtypeStruct(w.shape, F32) for w in ws] * 3,
                          compiler_params=_params(("arbitrary",), VMEM_MID))(*_in_hbm(*ws, *ms, *vs, *gs))
    return outs[:n], outs[n:2 * n], outs[2 * n:]


PACK_QUANTUM = SUBLANES * LANES


def _pack(named, names):
    parts = []
    for nme in names:
        flat = named[nme].reshape(-1)
        parts.append(jnp.pad(flat, (0, -flat.size % PACK_QUANTUM)))
    return jnp.concatenate(parts).reshape(-1, LANES)


def _unpack(packed, shapes, names):
    flat = packed.reshape(-1)
    out, pos = {}, 0
    for nme in names:
        size = math.prod(shapes[nme])
        out[nme] = flat[pos:pos + size].reshape(shapes[nme])
        pos += size + (-size % PACK_QUANTUM)
    return out


BIG = ("w_in", "w_glu", "w_attn_branch", "w_ssm_branch", "w_out", "w_ff_in", "w_ff_out")
COLUMN_SHARDED = ("w_in", "w_attn_branch", "w_ssm_branch", "w_ff_in")
SMALL = ("norm_mix_pre", "norm_mix_post", "norm_mlp_pre", "norm_mlp_post", "rel_bias", "sinks", "lam_re", "lam_im",
         "log_dt", "b_re", "b_im", "c_re", "c_im", "d_skip")
SWAPPED_SMALL = ("rel_bias", "b_re", "b_im")
SMALL_LATE = ("norm_mix_pre", "rel_bias", "sinks", "loss")
SMALL_BEFORE_ATTN_BWD = tuple(n for n in SMALL if n not in SMALL_LATE)
ALL_WEIGHTS = ("norm_mix_pre", "norm_mix_post", "norm_mlp_pre", "norm_mlp_post", "w_in", "rel_bias", "sinks", "lam_re",
               "lam_im", "log_dt", "b_re", "b_im", "c_re", "c_im", "d_skip", "w_glu", "w_attn_branch", "w_ssm_branch",
               "w_out", "w_ff_in", "w_ff_out")


def _full_from_gathered(name, gathered):
    _, r, c = gathered.shape
    if name in COLUMN_SHARDED:
        return jnp.transpose(gathered, (1, 0, 2)).reshape(r, N_DEV * c)
    return gathered.reshape(N_DEV * r, c)


def _blocks_from_full(name, full):
    r, c = full.shape
    if name in COLUMN_SHARDED:
        return jnp.transpose(full.reshape(r, N_DEV, c // N_DEV), (1, 0, 2))
    return full.reshape(N_DEV, r // N_DEV, c)


def kernel(x, norm_mix_pre, norm_mix_post, norm_mlp_pre, norm_mlp_post, w_in, rel_bias, sinks, lam_re, lam_im, log_dt, b_re, b_im, c_re, c_im, d_skip, w_glu, w_attn_branch, w_ssm_branch, w_out, w_ff_in, w_ff_out, loss_target, m_norm_mix_pre, m_norm_mix_post, m_norm_mlp_pre, m_norm_mlp_post, m_w_in, m_rel_bias, m_sinks, m_lam_re, m_lam_im, m_log_dt, m_b_re, m_b_im, m_c_re, m_c_im, m_d_skip, m_w_glu, m_w_attn_branch, m_w_ssm_branch, m_w_out, m_w_ff_in, m_w_ff_out, v_norm_mix_pre, v_norm_mix_post, v_norm_mlp_pre, v_norm_mlp_post, v_w_in, v_rel_bias, v_sinks, v_lam_re, v_lam_im, v_log_dt, v_b_re, v_b_im, v_c_re, v_c_im, v_d_skip, v_w_glu, v_w_attn_branch, v_w_ssm_branch, v_w_out, v_w_ff_in, v_w_ff_out):
    args = dict(locals())
    w = {n: args[n] for n in ALL_WEIGHTS}
    m = {n: args["m_" + n] for n in ALL_WEIGHTS}
    v = {n: args["v_" + n] for n in ALL_WEIGHTS}
    core = lax.axis_index("c").astype(jnp.int32).reshape(1)
    chip = (2 * lax.axis_index("x") + lax.axis_index("y")).astype(jnp.int32).reshape(1)
    xs, target = x[0], loss_target[0]
    t = _tiles(xs.shape[0])
    local = lambda d, n: d[n][0].T if n == "w_in" else d[n][0]
    shard = {n: local(w, n).astype(BF16) for n in BIG}
    shard["w_ff_in"] = shard["w_ff_in"].T
    view = lambda n, a: jnp.swapaxes(a, -1, -2) if n in SWAPPED_SMALL else a
    small = {n: (view(n, w[n]) if n == "rel_bias" else view(n, w[n])[0]) for n in SMALL}
    g1, g2, g3, g4 = (small[n].reshape(1, D_MODEL) for n in ("norm_mix_pre", "norm_mix_post", "norm_mlp_pre", "norm_mlp_post"))
    bucket = jnp.asarray(_bucket_table())
    rel_b, sink = small["rel_bias"], small["sinks"].reshape(1, N_HEADS)
    lam_r, lam_i = small["lam_re"].reshape(1, STATES), small["lam_im"].reshape(1, STATES)
    ldt_rep = jnp.repeat(small["log_dt"].reshape(N_GROUPS), N_STATE).reshape(1, STATES)
    bd_re, bd_im = _block_diag_in(small["b_re"]), _block_diag_in(small["b_im"])
    cm_re, cm_im = _block_diag_out(small["c_re"]).astype(BF16), _block_diag_out(small["c_im"]).astype(BF16)
    dsk = small["d_skip"].reshape(1, SSM_W)

    (g_in,) = _run_carry(_gather_carry([shard["w_in"]]), "gather_w_in")
    wf_in = g_in.reshape(IN_W, D_MODEL)
    merge_names = ("w_glu", "w_attn_branch", "w_ssm_branch", "w_out")
    (q, k, vv, u, ga, gs, h), gathered = _in_proj_fwd(xs, g1, wf_in, t["proj"], _gather_carry([shard[n] for n in merge_names]))
    wf = {n: _full_from_gathered(n, g) for n, g in zip(merge_names, gathered)}
    (att,), (wf_ff_in,) = _attn_fwd(q, k, vv, bucket, rel_b, sink, _gather_carry([shard["w_ff_in"]]))
    a_re, a_im, bm_re, bm_im = _ssm_prep(lam_r, lam_i, ldt_rep, bd_re, bd_im)
    (y, h_re, h_im, in_re, in_im), (wf_ff_out,) = _ssm_fwd(
        u, a_re, a_im, bm_re, bm_im, cm_re, cm_im, dsk, t["ssm_chunk"], _gather_carry([shard["w_ff_out"]]))
    x1, o, h2 = _merge_fwd(xs, y, att, ga, gs, g2, g3, wf["w_glu"], wf["w_ssm_branch"], wf["w_attn_branch"], wf["w_out"],
                           t["merge"])
    a, dfo, dx2, loss_blk, dg4 = _mlp_fwd(h2, x1, target, g4, wf_ff_in, wf_ff_out, t["mlp_fwd"])

    groups = {"ff": 4, "merge": 1, "w_in": 2}

    def add_sibling(group, blocks, received):
        return _add_sibling(blocks, received, core, groups[group], "add_sibling_" + group)

    ff_names = ("w_ff_in", "w_ff_out")
    dw_ff_in, dw_ff_out, da = _mlp_weight_grads(dfo, a, h2, wf_ff_out, t["mlp_bwd"])
    dh2 = _mlp_input_grad(da, wf_ff_in.reshape(D_FF, D_MODEL), t["mlp_bwd"])
    ff_blocks = [dw_ff_in, dw_ff_out]
    (dx1, dgates, datt, dy, dw_glu, dw_ssm, dw_attn, dw_out, dg2, dg3), ff_recv = _merge_bwd(
        dh2, dx2, x1, o, y, att, ga, gs, g2, g3, wf["w_glu"], wf["w_ssm_branch"], wf["w_attn_branch"], wf["w_out"],
        t["merge_bwd"], _sibling_carry(ff_blocks))
    ff_sums, ff_sums_bf = add_sibling("ff", ff_blocks, ff_recv)
    merge_blocks = [_blocks_from_full(n, g) for n, g in zip(merge_names, (dw_glu, dw_attn, dw_ssm, dw_out))]
    (du, dbm_re, dbm_im, dcm_re, dcm_im, da_re, da_im, dd_skip), carried = _ssm_bwd(
        dy, u, h_re, h_im, in_re, in_im, a_re, a_im, bm_re, bm_im, cm_re, cm_im, dsk, t["ssm_chunk"],
        _join(_chips_carry(ff_sums_bf), _sibling_carry(merge_blocks)))
    ff_from_chips, merge_recv = carried[:2], carried[2:]
    merge_sums, merge_sums_bf = add_sibling("merge", merge_blocks, merge_recv)
    dbd_re, dbd_im, dlam_re, dlam_im, dldt_rep = _ssm_prep_bwd(lam_r, lam_i, ldt_rep, bd_re, bd_im, dbm_re, dbm_im, da_re, da_im)
    dlog_dt = _group_sum(dldt_rep.reshape(N_GROUPS, N_STATE))
    shapes = {n: view(n, w[n]).shape for n in SMALL}
    shapes["loss"] = (1,)
    small_grads = dict(
        norm_mix_post=dg2, norm_mlp_pre=dg3, norm_mlp_post=dg4, lam_re=dlam_re, lam_im=dlam_im, log_dt=dlog_dt,
        b_re=_block_diag_in_grad(dbd_re), b_im=_block_diag_in_grad(dbd_im),
        c_re=_block_diag_out_grad(dcm_re), c_im=_block_diag_out_grad(dcm_im), d_skip=dd_skip)
    packed_early = _pack({n: small_grads[n].reshape(shapes[n]) for n in SMALL_BEFORE_ATTN_BWD}, SMALL_BEFORE_ATTN_BWD)
    (dq, dkv, attn_small), carried = _attn_bwd(
        q, k, vv, datt, bucket, rel_b, sink, _join(_chips_carry(merge_sums_bf), _gather_carry([packed_early])))
    merge_from_chips, partials_early = carried[:-1], carried[-1]

    dparts = (dq, dkv, du, dgates)
    dw_in_t = _in_proj_weight_grad(h, dparts)
    in_blocks = [dw_in_t.reshape(N_DEV, IN_W // N_DEV, D_MODEL)]
    to_sibling = _sibling_carry(in_blocks)
    in_flight, token = _exchange_start(to_sibling, "w_in_sibling_start")
    n_tiles = xs.shape[0] // t["proj_bwd"]
    (grad_x, dg1), _ = _in_proj_input_grad(xs, g1 + token[0:1, 0:1], wf_in, dx1, dparts, t["proj_bwd"], 0, n_tiles, "in_proj_input_grad")
    late = dict(norm_mix_pre=dg1, rel_bias=attn_small[:, :N_BUCKETS, 0], sinks=attn_small[:, N_BUCKETS, 0], loss=loss_blk[0:1, 0])
    packed_late = _pack({n: late[n].reshape(shapes[n]) for n in SMALL_LATE}, SMALL_LATE)
    (partials_late,) = _run_carry(_gather_carry([packed_late]), "gather_late_grads")
    in_blocks, in_recv = _exchange_wait(to_sibling, in_flight, [partials_late], "w_in_sibling_wait")
    in_sums, in_sums_bf = add_sibling("w_in", in_blocks, in_recv)
    to_chips = _chips_carry(in_sums_bf)
    in_flight, chips_started = _exchange_start(to_chips, "w_in_chips_start")

    grads, deltas, new_m, new_v = {}, {}, {}, {}

    def adam_group(group, names, sums, received, after=()):
        outs = _adam_big(*[[local(d, n) for n in names] for d in (w, m, v)], sums, received, chip, groups[group],
                         "adam_" + group, after)
        for store, vals in zip((grads, deltas, new_m, new_v), outs):
            store.update({n: (o.T if n == "w_in" else o)[None] for n, o in zip(names, vals)})

    adam_group("ff", ff_names, ff_sums, ff_from_chips, [chips_started])
    adam_group("merge", merge_names, merge_sums, merge_from_chips, [chips_started])

    grads.update(_unpack(_sum_partials(partials_early, "sum_small_grads"), shapes, SMALL_BEFORE_ATTN_BWD))
    grads.update(_unpack(_sum_partials(partials_late, "sum_late_grads"), shapes, SMALL_LATE))
    loss = grads.pop("loss").reshape(())
    small_out = _adam_small(*[[view(n, d[n]) for n in SMALL] for d in (w, m, v)], [grads[n] for n in SMALL])
    for store, vals in zip((deltas, new_m, new_v), small_out):
        store.update(zip(SMALL, vals))
    for store in (grads, deltas, new_m, new_v):
        store.update({n: view(n, store[n]) for n in SWAPPED_SMALL})

    busy = [new_v["w_ff_out"], new_v["w_out"], deltas["norm_mix_pre"]]
    _, (in_from_chips,) = _exchange_wait(to_chips, in_flight, busy, "w_in_chips_wait")
    adam_group("w_in", ("w_in",), in_sums, [in_from_chips])

    return (loss, grad_x[None], *[grads[n] for n in ALL_WEIGHTS], *[deltas[n] for n in ALL_WEIGHTS],
            *[new_m[n] for n in ALL_WEIGHTS], *[new_v[n] for n in ALL_WEIGHTS])
```

```python
import math

import jax
import jax.numpy as jnp
import numpy as np
from jax import lax
from jax.experimental import pallas as pl
from jax.experimental.pallas import tpu as pltpu

F32 = jnp.float32
BF16 = jnp.bfloat16

D_MODEL = 1024
N_HEADS = 8
HEAD_DIM = 64
ATTN_W = 512
KV_W = 128
BLOCK = 128
N_BUCKETS = 32
SSM_W = 512
N_GROUPS = 32
N_STATE = 64
GROUP_CH = 16
STATES = N_GROUPS * N_STATE
D_FF = 4096
IN_W = 3328
SPLITS = (0, 512, 640, 768, 1280, 2304, 3328)
RMS_EPS = 1e-6
NEG_INF = -1e30
SUBLANES = 8
LANES = 128
SSM_LANE_BLOCK = 512
N_SSM_BLOCKS = STATES // SSM_LANE_BLOCK
GROUPS_PER_BLOCK = SSM_LANE_BLOCK // N_STATE
VMEM_BIG = 52 * 1024 * 1024
VMEM_MID = 40 * 1024 * 1024
VMEM_MAX = 60 * 1024 * 1024

ADAM_LR = 0.001
ADAM_B1 = 0.9
ADAM_B2 = 0.999
ADAM_EPS = 1e-08
ADAM_WD = 0.01
ADAM_STEP = 10

N_DEV = 8


def _dot(a, b):
    return jnp.dot(a, b, preferred_element_type=F32)


def _dot_nt(a, b):
    return lax.dot_general(a, b, (((1,), (1,)), ((), ())), preferred_element_type=F32)


def _dot_tn(a, b):
    return lax.dot_general(a, b, (((0,), (0,)), ((), ())), preferred_element_type=F32)


def _rms_scale(x):
    return lax.rsqrt(jnp.mean(x * x, axis=-1, keepdims=True) + RMS_EPS)


def _rms_bwd(dy, x, r, g):
    t = dy * g
    dx = r * t - x * (r * r * r) * jnp.mean(t * x, axis=-1, keepdims=True)
    dg = jnp.sum(dy * x * r, axis=0, keepdims=True)
    return dx, dg


def _const_spec(shape):
    nd = len(shape)
    return pl.BlockSpec(shape, lambda *_: (0,) * nd, pipeline_mode=pl.Buffered(1))


def _in_hbm(*arrays):
    return tuple(pltpu.with_memory_space_constraint(a, pltpu.HBM) for a in arrays)


def _hbm_out(shapes):
    if isinstance(shapes, (list, tuple)):
        return [_hbm_out(s) for s in shapes]
    return shapes if isinstance(shapes, pl.MemoryRef) else pltpu.HBM(shapes.shape, shapes.dtype)


def _whole(shape):
    nd = len(shape)
    return pl.BlockSpec(shape, lambda *_: (0,) * nd)


def _params(sem, vmem=None):
    return pltpu.CompilerParams(dimension_semantics=sem, vmem_limit_bytes=vmem)


MESH_IDS = pl.DeviceIdType.MESH
HBM_SPEC = pl.BlockSpec(memory_space=pl.ANY)


class _Carry:
    def __init__(self, inputs, out_shapes, sems, start, finish):
        self.inputs, self.out_shapes, self.sems, self.start, self.finish = list(inputs), list(out_shapes), list(sems), start, finish


def _join(a, b):
    na_in, na_out, na_sem = len(a.inputs), len(a.out_shapes), len(a.sems)

    def start(ins, outs, sems):
        a.start(ins[:na_in], outs[:na_out], sems[:na_sem])
        b.start(ins[na_in:], outs[na_out:], sems[na_sem:])

    def finish(ins, outs, sems):
        a.finish(ins[:na_in], outs[:na_out], sems[:na_sem])
        b.finish(ins[na_in:], outs[na_out:], sems[na_sem:])

    return _Carry(a.inputs + b.inputs, a.out_shapes + b.out_shapes, a.sems + b.sems, start, finish)


def _hosted_call(body, carry, edge, *, name, grid, in_specs, out_specs, out_shape, scratch_shapes, compiler_params, inputs):
    n_in, n_out = len(in_specs), len(out_specs)
    inputs = [a if s.memory_space == pltpu.SMEM else _in_hbm(a)[0] for a, s in zip(inputs, in_specs)]
    out_shape = _hbm_out(list(out_shape))
    if carry is None:
        outs = pl.pallas_call(body, name=name, grid=grid, in_specs=in_specs, out_specs=out_specs, out_shape=out_shape,
                              scratch_shapes=scratch_shapes, compiler_params=compiler_params)(*inputs)
        return list(outs), []
    c_in, c_out, c_sem = len(carry.inputs), len(carry.out_shapes), len(carry.sems)

    def wrapped(*refs):
        ins, refs = refs[:n_in], refs[n_in:]
        cins, refs = refs[:c_in], refs[c_in:]
        outs, refs = refs[:n_out], refs[n_out:]
        couts, refs = refs[:c_out], refs[c_out:]
        scratch, csems = refs[:len(refs) - c_sem], refs[len(refs) - c_sem:]
        first, last = edge()

        @pl.when(first)
        def _():
            carry.start(cins, couts, csems)

        body(*ins, *outs, *scratch)

        @pl.when(last)
        def _():
            carry.finish(cins, couts, csems)

    outs = pl.pallas_call(
        wrapped, name=name, grid=grid, in_specs=list(in_specs) + [HBM_SPEC] * c_in,
        out_specs=list(out_specs) + [HBM_SPEC] * c_out, out_shape=out_shape + _hbm_out(carry.out_shapes),
        scratch_shapes=list(scratch_shapes) + carry.sems, compiler_params=compiler_params)(*inputs, *_in_hbm(*carry.inputs))
    return list(outs[:n_out]), list(outs[n_out:])


def _edge_1d(n_steps):
    return lambda: (pl.program_id(0) == 0, pl.program_id(0) == n_steps - 1)


def _edge_2d(n0, n1):
    return lambda: ((pl.program_id(0) == 0) & (pl.program_id(1) == 0),
                    (pl.program_id(0) == n0 - 1) & (pl.program_id(1) == n1 - 1))


def _run_carry(carry, name):
    c_in, c_out = len(carry.inputs), len(carry.out_shapes)

    def body(*refs):
        ins, outs, sems = refs[:c_in], refs[c_in:c_in + c_out], refs[c_in + c_out:]
        carry.start(ins, outs, sems)
        carry.finish(ins, outs, sems)

    return pl.pallas_call(body, name=name, in_specs=[HBM_SPEC] * c_in, out_specs=[HBM_SPEC] * c_out,
                          out_shape=_hbm_out(carry.out_shapes), scratch_shapes=carry.sems)(*_in_hbm(*carry.inputs))


def _in_proj_fwd(x, g1, w_in_t, tile, carry=None):
    T = x.shape[0]

    def body(x_ref, g_ref, w_ref, q_ref, k_ref, v_ref, u_ref, ga_ref, gs_ref, h_ref):
        xv = x_ref[...]
        h = (xv * _rms_scale(xv) * g_ref[...]).astype(BF16)
        h_ref[...] = h
        outs = (q_ref, k_ref, v_ref, u_ref, ga_ref, gs_ref)
        for p, o_ref in enumerate(outs):
            o_ref[...] = _dot_nt(h, w_ref[SPLITS[p]:SPLITS[p + 1], :]).astype(o_ref.dtype)

    widths = [SPLITS[p + 1] - SPLITS[p] for p in range(6)] + [D_MODEL]
    dtypes = [BF16, BF16, BF16, F32, F32, F32, BF16]
    return _hosted_call(
        body, carry, _edge_1d(T // tile), name="in_proj_fwd", grid=(T // tile,),
        in_specs=[pl.BlockSpec((tile, D_MODEL), lambda i: (i, 0)), _const_spec((1, D_MODEL)), _const_spec((IN_W, D_MODEL))],
        out_specs=[pl.BlockSpec((tile, w), lambda i: (i, 0)) for w in widths],
        out_shape=[jax.ShapeDtypeStruct((T, w), dt) for w, dt in zip(widths, dtypes)],
        scratch_shapes=[], compiler_params=_params(("arbitrary",), VMEM_MID), inputs=(x, g1, w_in_t))


PROJ_PARTS = (512, 256, 512, 2048)
PROJ_GRAD_BLOCK = 256


def _in_proj_weight_grad(h, dparts):
    T = h.shape[0]
    blocks = [wd // PROJ_GRAD_BLOCK for wd in PROJ_PARTS]
    starts = [sum(blocks[:p]) for p in range(len(blocks))]

    def body(h_ref, *refs):
        part_refs, o_ref = refs[:-1], refs[-1]
        j = pl.program_id(0)
        for p_ref, start, count in zip(part_refs, starts, blocks):
            @pl.when((j >= start) & (j < start + count))
            def _(p_ref=p_ref):
                o_ref[...] = _dot_tn(p_ref[...], h_ref[...])

    def part_spec(start, count):
        return pl.BlockSpec((T, PROJ_GRAD_BLOCK), lambda j: (0, jnp.clip(j - start, 0, count - 1)))

    return pl.pallas_call(
        body, name="in_proj_weight_grad", grid=(sum(blocks),),
        in_specs=[_const_spec((T, D_MODEL))] + [part_spec(s, c) for s, c in zip(starts, blocks)],
        out_specs=pl.BlockSpec((PROJ_GRAD_BLOCK, D_MODEL), lambda j: (j, 0)),
        out_shape=_hbm_out(jax.ShapeDtypeStruct((IN_W, D_MODEL), F32)),
        compiler_params=_params(("arbitrary",), VMEM_MID),
    )(*_in_hbm(h, *dparts))


def _in_proj_input_grad(x, g1, w_in_t, dx1, dparts, tile, first_tile, n_tiles, name, carry=None):
    offsets = [sum(PROJ_PARTS[:p]) for p in range(len(PROJ_PARTS))]

    def body(x_ref, g_ref, w_ref, dx1_ref, *refs):
        part_refs, (gx_ref, dg_ref) = refs[:len(PROJ_PARTS)], refs[len(PROJ_PARTS):]
        i = pl.program_id(0)
        xv = x_ref[...]
        r = _rms_scale(xv)
        g = g_ref[...]
        dh = sum(_dot(p_ref[...], w_ref[off:off + wd, :]) for p_ref, off, wd in zip(part_refs, offsets, PROJ_PARTS))
        dxn, dg = _rms_bwd(dh, xv, r, g)
        gx_ref[...] = dx1_ref[...] + dxn

        @pl.when(i == 0)
        def _():
            dg_ref[...] = dg

        @pl.when(i > 0)
        def _():
            dg_ref[...] += dg

    tok = lambda wd: pl.BlockSpec((tile, wd), lambda i: (i + first_tile, 0))
    return _hosted_call(
        body, carry, _edge_1d(n_tiles), name=name, grid=(n_tiles,),
        in_specs=[tok(D_MODEL), _const_spec((1, D_MODEL)), _const_spec((IN_W, D_MODEL)), tok(D_MODEL)] + [tok(wd) for wd in PROJ_PARTS],
        out_specs=[pl.BlockSpec((tile, D_MODEL), lambda i: (i, 0)), pl.BlockSpec((1, D_MODEL), lambda i: (0, 0))],
        out_shape=[jax.ShapeDtypeStruct((n_tiles * tile, D_MODEL), F32), jax.ShapeDtypeStruct((1, D_MODEL), F32)],
        scratch_shapes=[], compiler_params=_params(("arbitrary",), VMEM_MID), inputs=(x, g1, w_in_t, dx1, *dparts))


def _bucket_table():
    qi = np.arange(BLOCK)[:, None]
    kj = np.arange(2 * BLOCK)[None, :]
    dist = qi + BLOCK - kj
    max_exact = N_BUCKETS // 2
    d = np.maximum(dist, 0)
    df = np.maximum(d, 1).astype(np.float32)
    large = max_exact + (np.log(df / np.float32(max_exact)) / np.float32(math.log(BLOCK / max_exact))
                         * np.float32(N_BUCKETS - max_exact)).astype(np.int32)
    large = np.minimum(large, N_BUCKETS - 1)
    bucket = np.where(d < max_exact, d, large)
    return np.where((dist >= 0) & (dist < BLOCK), bucket, -1).astype(np.int32)


def _build_bias(bucket_ref, rb_ref, bias_ref):
    bk = bucket_ref[...]
    for h in range(N_HEADS):
        def add(b, acc, h=h):
            return acc + jnp.where(bk == b, rb_ref[h, b], 0.0)
        bias_ref[h] = lax.fori_loop(0, N_BUCKETS, add, jnp.zeros((BLOCK, 2 * BLOCK), F32))


def _kv_variants(prev_ref, cur_ref):
    cat = jnp.concatenate([prev_ref[...], cur_ref[...]], axis=0)
    lo = lax.broadcasted_iota(jnp.int32, cat.shape, 1) < HEAD_DIM
    zero = jnp.zeros_like(cat)
    head0_lo = jnp.where(lo, cat, zero)
    head1_hi = jnp.where(lo, zero, cat)
    return ((head0_lo, pltpu.roll(head0_lo, HEAD_DIM, 1)), (pltpu.roll(head1_hi, HEAD_DIM, 1), head1_hi))


def _merge_kv_grads(g):
    lo = lax.broadcasted_iota(jnp.int32, g[0][0].shape, 1) < HEAD_DIM
    return jnp.where(lo, g[0][0] + pltpu.roll(g[0][1], HEAD_DIM, 1), g[1][1] + pltpu.roll(g[1][0], HEAD_DIM, 1))


def _head_lanes(h):
    return slice((h // 2) * LANES, (h // 2 + 1) * LANES)


def _attn_probs(q_ref, kvar, bias_ref, sk_ref, valid, s_ref):
    for h in range(N_HEADS):
        s_ref[h] = _dot_nt(q_ref[:, _head_lanes(h)], kvar[h // 4][h % 2])
    head = lax.broadcasted_iota(jnp.int32, (N_HEADS, 1, 1), 0)
    sink = jnp.zeros((N_HEADS, 1, 1), F32)
    for h in range(N_HEADS):
        sink = jnp.where(head == h, sk_ref[0, h], sink)
    s = jnp.where(valid[None], s_ref[...] * (HEAD_DIM ** -0.5) + bias_ref[...], NEG_INF)
    m = jnp.maximum(jnp.max(s, axis=-1, keepdims=True), sink)
    p = jnp.exp(s - m)
    e_sink = jnp.exp(sink - m)
    inv = 1.0 / (jnp.sum(p, axis=-1, keepdims=True) + e_sink)
    return p * inv, e_sink * inv


def _attn_valid(bucket_ref, n):
    col = lax.broadcasted_iota(jnp.int32, (BLOCK, 2 * BLOCK), 1)
    return (bucket_ref[...] >= 0) & ((n > 0) | (col >= BLOCK))


def _attn_fwd(q, k, v, bucket, rel_bias, sinks, carry=None):
    T = q.shape[0]
    nb = T // BLOCK

    def body(q_ref, kc_ref, kp_ref, vc_ref, vp_ref, bucket_ref, rb_ref, sk_ref, o_ref, bias_ref, s_ref, p_ref):
        n = pl.program_id(0)

        @pl.when(n == 0)
        def _():
            _build_bias(bucket_ref, rb_ref, bias_ref)

        kvar = _kv_variants(kp_ref, kc_ref)
        vvar = _kv_variants(vp_ref, vc_ref)
        pr, _ = _attn_probs(q_ref, kvar, bias_ref, sk_ref, _attn_valid(bucket_ref, n), s_ref)
        p_ref[...] = pr.astype(BF16)
        for m in range(N_HEADS // 2):
            acc = _dot(p_ref[2 * m], vvar[m // 2][0]) + _dot(p_ref[2 * m + 1], vvar[m // 2][1])
            o_ref[:, m * LANES:(m + 1) * LANES] = acc.astype(o_ref.dtype)

    cur = lambda w: pl.BlockSpec((BLOCK, w), lambda n: (n, 0))
    prev = lambda w: pl.BlockSpec((BLOCK, w), lambda n: (jnp.maximum(n - 1, 0), 0))
    smem = pl.BlockSpec(memory_space=pltpu.SMEM)
    return _hosted_call(
        body, carry, _edge_1d(nb), name="attn_fwd", grid=(nb,),
        in_specs=[cur(ATTN_W), cur(KV_W), prev(KV_W), cur(KV_W), prev(KV_W), _const_spec((BLOCK, 2 * BLOCK)), smem, smem],
        out_specs=[cur(ATTN_W)],
        out_shape=[jax.ShapeDtypeStruct((T, ATTN_W), BF16)],
        scratch_shapes=[pltpu.VMEM((N_HEADS, BLOCK, 2 * BLOCK), F32), pltpu.VMEM((N_HEADS, BLOCK, 2 * BLOCK), F32),
                        pltpu.VMEM((N_HEADS, BLOCK, 2 * BLOCK), BF16)],
        compiler_params=_params(("arbitrary",)), inputs=(q, k, k, v, v, bucket, rel_bias, sinks))


ATTN_SMALL_ROWS = N_BUCKETS + SUBLANES


def _attn_bwd(q, k, v, datt, bucket, rel_bias, sinks, carry=None):
    T = q.shape[0]
    nb = T // BLOCK

    def body(q_ref, do_ref, kc_ref, kp_ref, vc_ref, vp_ref, bucket_ref, rb_ref, sk_ref,
             dq_ref, dkv_ref, small_ref, bias_ref, ds_sum_ref, dsink_ref, kcarry_ref, vcarry_ref,
             s_ref, dp_ref, p_ref, dsc_ref):
        n = pl.program_id(0)

        @pl.when(n == 0)
        def _():
            _build_bias(bucket_ref, rb_ref, bias_ref)
            ds_sum_ref[...] = jnp.zeros_like(ds_sum_ref)
            dsink_ref[...] = jnp.zeros_like(dsink_ref)
            kcarry_ref[...] = jnp.zeros_like(kcarry_ref)
            vcarry_ref[...] = jnp.zeros_like(vcarry_ref)

        @pl.when(n < nb)
        def _():
            kvar = _kv_variants(kp_ref, kc_ref)
            vvar = _kv_variants(vp_ref, vc_ref)
            pr, p_sink = _attn_probs(q_ref, kvar, bias_ref, sk_ref, _attn_valid(bucket_ref, n), s_ref)
            for h in range(N_HEADS):
                dp_ref[h] = _dot_nt(do_ref[:, _head_lanes(h)], vvar[h // 4][h % 2])
            dp = dp_ref[...]
            dsum = jnp.sum(pr * dp, axis=-1, keepdims=True)
            ds = pr * (dp - dsum)
            ds_sum_ref[...] += ds
            dsink_ref[...] -= jnp.sum(p_sink * dsum, axis=1, keepdims=True)
            dsc_ref[...] = (ds * (HEAD_DIM ** -0.5)).astype(BF16)
            p_ref[...] = pr.astype(BF16)
            for m in range(N_HEADS // 2):
                dqm = _dot(dsc_ref[2 * m], kvar[m // 2][0]) + _dot(dsc_ref[2 * m + 1], kvar[m // 2][1])
                dq_ref[:, m * LANES:(m + 1) * LANES] = dqm.astype(dq_ref.dtype)
            dk_var = [[None, None], [None, None]]
            dv_var = [[None, None], [None, None]]
            for kvh in range(2):
                for e in range(2):
                    heads = [h for h in range(N_HEADS) if h // 4 == kvh and h % 2 == e]
                    dk_var[kvh][e] = sum(_dot_tn(dsc_ref[h], q_ref[:, _head_lanes(h)]) for h in heads)
                    dv_var[kvh][e] = sum(_dot_tn(p_ref[h], do_ref[:, _head_lanes(h)]) for h in heads)
            dk_cat = _merge_kv_grads(dk_var)
            dv_cat = _merge_kv_grads(dv_var)

            @pl.when(n > 0)
            def _():
                dkv_ref[:, :KV_W] = (kcarry_ref[...] + dk_cat[:BLOCK]).astype(BF16)
                dkv_ref[:, KV_W:] = (vcarry_ref[...] + dv_cat[:BLOCK]).astype(BF16)

            kcarry_ref[...] = dk_cat[BLOCK:]
            vcarry_ref[...] = dv_cat[BLOCK:]

        @pl.when(n == nb)
        def _():
            dkv_ref[:, :KV_W] = kcarry_ref[...].astype(BF16)
            dkv_ref[:, KV_W:] = vcarry_ref[...].astype(BF16)
            bk = bucket_ref[...]
            row = lax.broadcasted_iota(jnp.int32, (N_HEADS, ATTN_SMALL_ROWS, LANES), 1)

            def add(b, acc):
                masked = jnp.where((bk == b)[None], ds_sum_ref[...], 0.0)
                val = jnp.sum(jnp.sum(masked, axis=1, keepdims=True), axis=2, keepdims=True)
                return acc + jnp.where(row == b, val, 0.0)

            small_ref[...] = lax.fori_loop(0, N_BUCKETS, add, jnp.where(row == N_BUCKETS, dsink_ref[...], 0.0))

    last = nb - 1
    cur = lambda w: pl.BlockSpec((BLOCK, w), lambda n: (jnp.minimum(n, last), 0))
    prev = lambda w: pl.BlockSpec((BLOCK, w), lambda n: (jnp.clip(n - 1, 0, last), 0))
    smem = pl.BlockSpec(memory_space=pltpu.SMEM)
    return _hosted_call(
        body, carry, _edge_1d(nb + 1), name="attn_bwd", grid=(nb + 1,),
        in_specs=[cur(ATTN_W), cur(ATTN_W), cur(KV_W), prev(KV_W), cur(KV_W), prev(KV_W),
                  _const_spec((BLOCK, 2 * BLOCK)), smem, smem],
        out_specs=[cur(ATTN_W), prev(2 * KV_W), pl.BlockSpec((N_HEADS, ATTN_SMALL_ROWS, LANES), lambda n: (0, 0, 0))],
        out_shape=[jax.ShapeDtypeStruct((T, ATTN_W), BF16), jax.ShapeDtypeStruct((T, 2 * KV_W), BF16),
                   jax.ShapeDtypeStruct((N_HEADS, ATTN_SMALL_ROWS, LANES), F32)],
        scratch_shapes=[pltpu.VMEM((N_HEADS, BLOCK, 2 * BLOCK), F32), pltpu.VMEM((N_HEADS, BLOCK, 2 * BLOCK), F32),
                        pltpu.VMEM((N_HEADS, 1, 1), F32), pltpu.VMEM((BLOCK, KV_W), F32), pltpu.VMEM((BLOCK, KV_W), F32),
                        pltpu.VMEM((N_HEADS, BLOCK, 2 * BLOCK), F32), pltpu.VMEM((N_HEADS, BLOCK, 2 * BLOCK), F32),
                        pltpu.VMEM((N_HEADS, BLOCK, 2 * BLOCK), BF16), pltpu.VMEM((N_HEADS, BLOCK, 2 * BLOCK), BF16)],
        compiler_params=_params(("arbitrary",)), inputs=(q, datt, k, k, v, v, bucket, rel_bias, sinks))


SCAN_UNROLL = 4


def _cmul(ar, ai, br, bi):
    return ar * br - ai * bi, ar * bi + ai * br


def _cmul_conj(ar, ai, br, bi):
    return ar * br + ai * bi, ar * bi - ai * br


def _ssm_discretize(lr, li, ldt):
    dt = jnp.exp(ldt)
    mag = jnp.exp(lr * dt)
    ab_re = mag * jnp.cos(li * dt)
    ab_im = mag * jnp.sin(li * dt)
    nr = ab_re - 1.0
    den = lr * lr + li * li
    f_re = (nr * lr + ab_im * li) / den
    f_im = (ab_im * lr - nr * li) / den
    return ab_re, ab_im, f_re, f_im


def _ssm_prep(lam_re, lam_im, ldt_rep, bd_re, bd_im):
    def body(lr_ref, li_ref, ldt_ref, bdr_ref, bdi_ref, ar_ref, ai_ref, br_ref, bi_ref):
        ab_re, ab_im, f_re, f_im = _ssm_discretize(lr_ref[...], li_ref[...], ldt_ref[...])
        ar_ref[...] = ab_re
        ai_ref[...] = ab_im
        bdr, bdi = bdr_ref[0], bdi_ref[0]
        br_ref[0] = (bdr * f_re - bdi * f_im).astype(BF16)
        bi_ref[0] = (bdi * f_re + bdr * f_im).astype(BF16)

    row = pl.BlockSpec((1, SSM_LANE_BLOCK), lambda j: (0, j))
    mat = pl.BlockSpec((1, LANES, SSM_LANE_BLOCK), lambda j: (j, 0, 0))
    return pl.pallas_call(
        body, name="ssm_prep", grid=(N_SSM_BLOCKS,),
        in_specs=[row, row, row, mat, mat], out_specs=[row, row, mat, mat],
        out_shape=[jax.ShapeDtypeStruct((1, STATES), F32)] * 2 + [jax.ShapeDtypeStruct((N_SSM_BLOCKS, LANES, SSM_LANE_BLOCK), BF16)] * 2,
        compiler_params=_params(("arbitrary",)),
    )(*_in_hbm(lam_re, lam_im, ldt_rep, bd_re, bd_im))


def _ssm_prep_bwd(lam_re, lam_im, ldt_rep, bd_re, bd_im, dbr, dbi, da_re, da_im):
    def body(lr_ref, li_ref, ldt_ref, bdr_ref, bdi_ref, dbr_ref, dbi_ref, dar_ref, dai_ref,
             dbdr_ref, dbdi_ref, dlr_ref, dli_ref, dldt_ref):
        lr, li, ldt = lr_ref[...], li_ref[...], ldt_ref[...]
        (_, _, f_re, f_im), vjp = jax.vjp(_ssm_discretize, lr, li, ldt)
        bdr, bdi, gbr, gbi = bdr_ref[0], bdi_ref[0], dbr_ref[0], dbi_ref[0]
        dbdr_ref[0] = gbr * f_re + gbi * f_im
        dbdi_ref[0] = gbi * f_re - gbr * f_im
        df_re = jnp.sum(gbr * bdr + gbi * bdi, axis=0, keepdims=True)
        df_im = jnp.sum(gbi * bdr - gbr * bdi, axis=0, keepdims=True)
        dlr, dli, dldt = vjp((dar_ref[...], dai_ref[...], df_re, df_im))
        dlr_ref[...] = dlr
        dli_ref[...] = dli
        dldt_ref[...] = dldt

    row = pl.BlockSpec((1, SSM_LANE_BLOCK), lambda j: (0, j))
    mat = pl.BlockSpec((1, LANES, SSM_LANE_BLOCK), lambda j: (j, 0, 0))
    mat_shape = jax.ShapeDtypeStruct((N_SSM_BLOCKS, LANES, SSM_LANE_BLOCK), F32)
    row_shape = jax.ShapeDtypeStruct((1, STATES), F32)
    return pl.pallas_call(
        body, name="ssm_prep_bwd", grid=(N_SSM_BLOCKS,),
        in_specs=[row, row, row, mat, mat, mat, mat, row, row], out_specs=[mat, mat, row, row, row],
        out_shape=[mat_shape, mat_shape, row_shape, row_shape, row_shape],
        compiler_params=_params(("arbitrary",)),
    )(*_in_hbm(lam_re, lam_im, ldt_rep, bd_re, bd_im, dbr, dbi, da_re, da_im))


def _group_sum(x):
    def body(x_ref, o_ref):
        o_ref[...] = jnp.sum(x_ref[...], axis=1, keepdims=True)
    return pl.pallas_call(body, name="ssm_group_sum", grid=(1,), in_specs=[_whole(x.shape)], out_specs=_whole((N_GROUPS, 1)),
                          out_shape=jax.ShapeDtypeStruct((N_GROUPS, 1), F32))(*_in_hbm(x))


def _power_table(ar, ai, p_re_ref, p_im_ref, steps):
    shape = (SUBLANES, SSM_LANE_BLOCK)
    p_re_ref[0:SUBLANES] = jnp.broadcast_to(ar, shape)
    p_im_ref[0:SUBLANES] = jnp.broadcast_to(ai, shape)
    m = 1
    while m < steps:
        rows = m * SUBLANES
        top_re = p_re_ref[rows - SUBLANES:rows]
        top_im = p_im_ref[rows - SUBLANES:rows]
        cur_re = p_re_ref[0:rows].reshape(m, SUBLANES, SSM_LANE_BLOCK)
        cur_im = p_im_ref[0:rows].reshape(m, SUBLANES, SSM_LANE_BLOCK)
        nxt_re, nxt_im = _cmul(cur_re, cur_im, top_re[None], top_im[None])
        p_re_ref[rows:2 * rows] = nxt_re.reshape(rows, SSM_LANE_BLOCK)
        p_im_ref[rows:2 * rows] = nxt_im.reshape(rows, SSM_LANE_BLOCK)
        m *= 2


def _to_segments(src_ref, dst_ref, steps):
    for s in range(SUBLANES):
        dst_ref[pl.ds(s, steps, stride=SUBLANES), :] = src_ref[s * steps:(s + 1) * steps, :]


def _from_segments(src_ref, dst_ref, steps):
    for s in range(SUBLANES):
        dst_ref[s * steps:(s + 1) * steps, :] = src_ref[pl.ds(s, steps, stride=SUBLANES), :]


def _segment_carries(e_re, e_im, an_re, an_im, c_re, c_im, reverse):
    order = range(SUBLANES - 1, -1, -1) if reverse else range(SUBLANES)
    ins_re, ins_im = [None] * SUBLANES, [None] * SUBLANES
    for s in order:
        ins_re[s], ins_im[s] = c_re, c_im
        pr, pi = _cmul(an_re, an_im, c_re, c_im)
        c_re = e_re[s:s + 1] + pr
        c_im = e_im[s:s + 1] + pi
    return jnp.concatenate(ins_re, axis=0), jnp.concatenate(ins_im, axis=0), c_re, c_im


def _ssm_fwd(u, a_re, a_im, b_re, b_im, c_re, c_im, d_skip, chunk, carry=None):
    T = u.shape[0]
    nc = T // chunk
    steps = chunk // SUBLANES
    blk = SSM_LANE_BLOCK

    def body(u_ref, ar_ref, ai_ref, br_ref, bi_ref, cr_ref, ci_ref, dk_ref,
             y_ref, hr_ref, hi_ref, inr_ref, ini_ref, useg_ref, yseg_ref, pr_ref, pi_ref, carry_ref):
        c = pl.program_id(1)
        ar, ai = ar_ref[...], ai_ref[...]

        @pl.when(c == 0)
        def _():
            _power_table(ar, ai, pr_ref, pi_ref, steps)
            carry_ref[...] = jnp.zeros_like(carry_ref)

        _to_segments(u_ref, useg_ref, steps)
        ub = useg_ref[...].astype(BF16)
        hr_ref[...] = _dot(ub, br_ref[0])
        hi_ref[...] = _dot(ub, bi_ref[0])
        first = slice(0, SUBLANES)

        def scan(t4, prev):
            for j in range(SCAN_UNROLL):
                rows = pl.ds(pl.multiple_of((t4 * SCAN_UNROLL + j) * SUBLANES, SUBLANES), SUBLANES)
                pr, pi = _cmul(pr_ref[first, :], pi_ref[first, :], prev[0], prev[1])
                prev = (pr + hr_ref[rows, :], pi + hi_ref[rows, :])
                hr_ref[rows, :] = prev[0]
                hi_ref[rows, :] = prev[1]
            return prev

        zero = jnp.zeros((SUBLANES, blk), F32)
        lax.fori_loop(0, steps // SCAN_UNROLL, scan, (zero, zero))

        top = slice(chunk - SUBLANES, chunk)
        in_re, in_im, out_re, out_im = _segment_carries(
            hr_ref[top, :], hi_ref[top, :], pr_ref[top, :][0:1], pi_ref[top, :][0:1],
            carry_ref[0:1, :], carry_ref[1:2, :], reverse=False)
        carry_ref[0:1, :] = out_re
        carry_ref[1:2, :] = out_im
        inr_ref[...] = in_re
        ini_ref[...] = in_im

        def fix(t4, _):
            for j in range(SCAN_UNROLL):
                rows = pl.ds(pl.multiple_of((t4 * SCAN_UNROLL + j) * SUBLANES, SUBLANES), SUBLANES)
                fr, fi = _cmul(pr_ref[rows, :], pi_ref[rows, :], in_re, in_im)
                hr_ref[rows, :] += fr
                hi_ref[rows, :] += fi
            return 0

        lax.fori_loop(0, steps // SCAN_UNROLL, fix, 0)

        yseg_ref[...] = _dot(hr_ref[...].astype(BF16), cr_ref[0]) - _dot(hi_ref[...].astype(BF16), ci_ref[0])
        _from_segments(yseg_ref, y_ref, steps)
        y_ref[...] += dk_ref[...] * u_ref[...]

    row = pl.BlockSpec((1, blk), lambda j, c: (0, j))
    b_mat = pl.BlockSpec((1, LANES, blk), lambda j, c: (j, 0, 0))
    c_mat = pl.BlockSpec((1, blk, LANES), lambda j, c: (j, 0, 0))
    tok = pl.BlockSpec((chunk, LANES), lambda j, c: (c, j))
    state = pl.BlockSpec((chunk, blk), lambda j, c: (c, j))
    enter = pl.BlockSpec((SUBLANES, blk), lambda j, c: (c, j))
    return _hosted_call(
        body, carry, _edge_2d(N_SSM_BLOCKS, nc), name="ssm_fwd", grid=(N_SSM_BLOCKS, nc),
        in_specs=[tok, row, row, b_mat, b_mat, c_mat, c_mat, pl.BlockSpec((1, LANES), lambda j, c: (0, j))],
        out_specs=[tok, state, state, enter, enter],
        out_shape=[jax.ShapeDtypeStruct((T, SSM_W), F32), jax.ShapeDtypeStruct((T, STATES), F32),
                   jax.ShapeDtypeStruct((T, STATES), F32), jax.ShapeDtypeStruct((nc * SUBLANES, STATES), F32),
                   jax.ShapeDtypeStruct((nc * SUBLANES, STATES), F32)],
        scratch_shapes=[pltpu.VMEM((chunk, LANES), F32), pltpu.VMEM((chunk, LANES), F32),
                        pltpu.VMEM((chunk, blk), F32), pltpu.VMEM((chunk, blk), F32), pltpu.VMEM((SUBLANES, blk), F32)],
        compiler_params=_params(("arbitrary", "arbitrary"), VMEM_MID),
        inputs=(u, a_re, a_im, b_re, b_im, c_re, c_im, d_skip))


def _ssm_bwd(dy, u, h_re, h_im, in_re, in_im, a_re, a_im, b_re, b_im, c_re, c_im, d_skip, chunk, carry=None):
    T = u.shape[0]
    nc = T // chunk
    steps = chunk // SUBLANES
    blk = SSM_LANE_BLOCK

    def body(dy_ref, u_ref, hr_ref, hi_ref, inr_ref, ini_ref, ar_ref, ai_ref, br_ref, bi_ref, cr_ref, ci_ref, dk_ref,
             du_ref, dbr_ref, dbi_ref, dcr_ref, dci_ref, dar_ref, dai_ref, ddk_ref,
             dyseg_ref, useg_ref, duseg_ref, gr_ref, gi_ref, pr_ref, pi_ref, carry_ref, accr_ref, acci_ref):
        c = pl.program_id(1)
        ar, ai = ar_ref[...], ai_ref[...]

        @pl.when(c == 0)
        def _():
            _power_table(ar, ai, pr_ref, pi_ref, steps)
            carry_ref[...] = jnp.zeros_like(carry_ref)
            accr_ref[...] = jnp.zeros_like(accr_ref)
            acci_ref[...] = jnp.zeros_like(acci_ref)

        _to_segments(dy_ref, dyseg_ref, steps)
        _to_segments(u_ref, useg_ref, steps)
        dyb = dyseg_ref[...].astype(BF16)
        ub = useg_ref[...].astype(BF16)
        gr_ref[...] = _dot_nt(dyb, cr_ref[0])
        gi_ref[...] = -_dot_nt(dyb, ci_ref[0])
        dcr = _dot_tn(hr_ref[...].astype(BF16), dyb)
        dci = -_dot_tn(hi_ref[...].astype(BF16), dyb)
        ddk = jnp.sum(dy_ref[...] * u_ref[...], axis=0, keepdims=True)

        first = slice(0, SUBLANES)

        def scan(k4, nxt):
            for j in range(SCAN_UNROLL):
                t = steps - 1 - (k4 * SCAN_UNROLL + j)
                rows = pl.ds(pl.multiple_of(t * SUBLANES, SUBLANES), SUBLANES)
                pr, pi = _cmul_conj(pr_ref[first, :], pi_ref[first, :], nxt[0], nxt[1])
                nxt = (pr + gr_ref[rows, :], pi + gi_ref[rows, :])
                gr_ref[rows, :] = nxt[0]
                gi_ref[rows, :] = nxt[1]
            return nxt

        top = slice(chunk - SUBLANES, chunk)
        zero = jnp.zeros((SUBLANES, blk), F32)
        lax.fori_loop(0, steps // SCAN_UNROLL, scan, (zero, zero))

        gin_re, gin_im, out_re, out_im = _segment_carries(
            gr_ref[0:SUBLANES, :], gi_ref[0:SUBLANES, :], pr_ref[top, :][0:1], -pi_ref[top, :][0:1],
            carry_ref[0:1, :], carry_ref[1:2, :], reverse=True)
        carry_ref[0:1, :] = out_re
        carry_ref[1:2, :] = out_im

        def fix_row(rows, prow, hp_re, hp_im, acc):
            fr, fi = _cmul_conj(pr_ref[prow, :], pi_ref[prow, :], gin_re, gin_im)
            g_re = gr_ref[rows, :] + fr
            g_im = gi_ref[rows, :] + fi
            gr_ref[rows, :] = g_re
            gi_ref[rows, :] = g_im
            return acc[0] + g_re * hp_re + g_im * hp_im, acc[1] + g_im * hp_re - g_re * hp_im

        def fix_at(t, acc):
            aligned = (lambda r: r * SUBLANES) if isinstance(t, int) else (lambda r: pl.multiple_of(r * SUBLANES, SUBLANES))
            rows, before, prow = (pl.ds(aligned(r), SUBLANES) for r in (t, t - 1, steps - 1 - t))
            return fix_row(rows, prow, hr_ref[before, :], hi_ref[before, :], acc)

        def fix(t4, acc):
            for j in range(SCAN_UNROLL):
                acc = fix_at(t4 * SCAN_UNROLL + j, acc)
            return acc

        acc = fix_row(first, top, inr_ref[...], ini_ref[...], (accr_ref[...], acci_ref[...]))
        for t in range(1, SCAN_UNROLL):
            acc = fix_at(t, acc)
        acc_re, acc_im = lax.fori_loop(1, steps // SCAN_UNROLL, fix, acc)
        accr_ref[...] = acc_re
        acci_ref[...] = acc_im

        gbr = gr_ref[...].astype(BF16)
        gbi = gi_ref[...].astype(BF16)
        duseg_ref[...] = _dot_nt(gbr, br_ref[0]) + _dot_nt(gbi, bi_ref[0])
        _from_segments(duseg_ref, dyseg_ref, steps)
        du_ref[...] = (dyseg_ref[...] + dk_ref[...] * dy_ref[...]).astype(BF16)
        dbr = _dot_tn(ub, gbr)
        dbi = _dot_tn(ub, gbi)

        @pl.when(c == 0)
        def _():
            dbr_ref[0] = dbr
            dbi_ref[0] = dbi
            dcr_ref[0] = dcr
            dci_ref[0] = dci
            ddk_ref[...] = ddk

        @pl.when(c > 0)
        def _():
            dbr_ref[0] += dbr
            dbi_ref[0] += dbi
            dcr_ref[0] += dcr
            dci_ref[0] += dci
            ddk_ref[...] += ddk

        @pl.when(c == nc - 1)
        def _():
            dar_ref[...] = jnp.sum(acc_re, axis=0, keepdims=True)
            dai_ref[...] = jnp.sum(acc_im, axis=0, keepdims=True)

    rev = lambda c: nc - 1 - c
    row = pl.BlockSpec((1, blk), lambda j, c: (0, j))
    b_mat = pl.BlockSpec((1, LANES, blk), lambda j, c: (j, 0, 0))
    c_mat = pl.BlockSpec((1, blk, LANES), lambda j, c: (j, 0, 0))
    tok = pl.BlockSpec((chunk, LANES), lambda j, c: (rev(c), j))
    state = pl.BlockSpec((chunk, blk), lambda j, c: (rev(c), j))
    enter = pl.BlockSpec((SUBLANES, blk), lambda j, c: (rev(c), j))
    chan = pl.BlockSpec((1, LANES), lambda j, c: (0, j))
    f32 = lambda *s: jax.ShapeDtypeStruct(s, F32)
    return _hosted_call(
        body, carry, _edge_2d(N_SSM_BLOCKS, nc), name="ssm_bwd", grid=(N_SSM_BLOCKS, nc),
        in_specs=[tok, tok, state, state, enter, enter, row, row, b_mat, b_mat, c_mat, c_mat, chan],
        out_specs=[tok, b_mat, b_mat, c_mat, c_mat, row, row, chan],
        out_shape=[jax.ShapeDtypeStruct((T, SSM_W), BF16), f32(N_SSM_BLOCKS, LANES, blk), f32(N_SSM_BLOCKS, LANES, blk),
                   f32(N_SSM_BLOCKS, blk, LANES), f32(N_SSM_BLOCKS, blk, LANES), f32(1, STATES), f32(1, STATES), f32(1, SSM_W)],
        scratch_shapes=[pltpu.VMEM((chunk, LANES), F32), pltpu.VMEM((chunk, LANES), F32), pltpu.VMEM((chunk, LANES), F32),
                        pltpu.VMEM((chunk, blk), F32), pltpu.VMEM((chunk, blk), F32),
                        pltpu.VMEM((chunk, blk), F32), pltpu.VMEM((chunk, blk), F32),
                        pltpu.VMEM((SUBLANES, blk), F32), pltpu.VMEM((SUBLANES, blk), F32), pltpu.VMEM((SUBLANES, blk), F32)],
        compiler_params=_params(("arbitrary", "arbitrary"), VMEM_BIG),
        inputs=(dy, u, h_re, h_im, in_re, in_im, a_re, a_im, b_re, b_im, c_re, c_im, d_skip))


def _merge_forward(y, att, ga, gs, w_glu, w_ssm, w_attn):
    z = jax.nn.gelu(y)
    zb = z.astype(BF16)
    gl = jax.nn.sigmoid(_dot(zb, w_glu))
    z2b = (z * gl).astype(BF16)
    y_ssm = _dot(z2b, w_ssm)
    y_attn = _dot(att, w_attn)
    sa = jax.nn.sigmoid(ga)
    ss = jax.nn.sigmoid(gs)
    merged = (sa * y_attn + ss * y_ssm).astype(BF16)
    return z, zb, gl, z2b, y_ssm, y_attn, sa, ss, merged


def _merge_fwd(x, y, att, ga, gs, g2, g3, w_glu, w_ssm, w_attn, w_out, tile):
    T = x.shape[0]

    def body(x_ref, y_ref, att_ref, ga_ref, gs_ref, g2_ref, g3_ref, wg_ref, ws_ref, wa_ref, wo_ref, x1_ref, o_ref, h2_ref):
        merged = _merge_forward(y_ref[...], att_ref[...], ga_ref[...], gs_ref[...], wg_ref[...], ws_ref[...], wa_ref[...])[-1]
        o = _dot(merged, wo_ref[...])
        x1 = x_ref[...] + o * _rms_scale(o) * g2_ref[...]
        o_ref[...] = o
        x1_ref[...] = x1
        h2_ref[...] = (x1 * _rms_scale(x1) * g3_ref[...]).astype(BF16)

    tok = lambda w: pl.BlockSpec((tile, w), lambda i: (i, 0))
    vec = _const_spec((1, D_MODEL))
    return pl.pallas_call(
        body, name="merge_fwd", grid=(T // tile,),
        in_specs=[tok(D_MODEL), tok(SSM_W), tok(ATTN_W), tok(D_MODEL), tok(D_MODEL), vec, vec,
                  _const_spec((SSM_W, SSM_W)), _const_spec((SSM_W, D_MODEL)), _const_spec((ATTN_W, D_MODEL)),
                  _const_spec((D_MODEL, D_MODEL))],
        out_specs=[tok(D_MODEL), tok(D_MODEL), tok(D_MODEL)],
        out_shape=_hbm_out([jax.ShapeDtypeStruct((T, D_MODEL), F32), jax.ShapeDtypeStruct((T, D_MODEL), F32),
                            jax.ShapeDtypeStruct((T, D_MODEL), BF16)]),
        compiler_params=_params(("arbitrary",), VMEM_MID),
    )(*_in_hbm(x, y, att, ga, gs, g2, g3, w_glu, w_ssm, w_attn, w_out))


def _merge_bwd(dh2, dx2, x1, o, y, att, ga, gs, g2, g3, w_glu, w_ssm, w_attn, w_out, tile, carry=None):
    T = x1.shape[0]
    n_steps = T // tile

    group = min(2, n_steps)
    staged_widths = (D_MODEL, D_MODEL, ATTN_W, D_MODEL, SSM_W, D_MODEL, SSM_W, SSM_W)

    def body(dh2_ref, dx2_ref, x1_ref, o_ref, y_ref, att_ref, ga_ref, gs_ref, g2_ref, g3_ref, wg_ref, ws_ref, wa_ref, wo_ref,
             dx1_ref, dgates_ref, datt_ref, dy_ref, dwg_hbm, dws_hbm, dwa_hbm, dwo_hbm, dg2_ref, dg3_ref,
             awg_ref, aws_ref, awa_ref, awo_ref, *staged):
        i = pl.program_id(0)
        x1v, ov = x1_ref[...], o_ref[...]
        dxn, dg3 = _rms_bwd(dh2_ref[...], x1v, _rms_scale(x1v), g3_ref[...])
        dx1 = dx2_ref[...] + dxn
        dx1_ref[...] = dx1
        do, dg2 = _rms_bwd(dx1, ov, _rms_scale(ov), g2_ref[...])
        dob = do.astype(BF16)

        yv = y_ref[...]
        att = att_ref[...]
        z, zb, gl, z2b, y_ssm, y_attn, sa, ss, merged = _merge_forward(
            yv, att, ga_ref[...], gs_ref[...], wg_ref[...], ws_ref[...], wa_ref[...])
        dmerged = _dot_nt(dob, wo_ref[...])
        dya = (dmerged * sa).astype(BF16)
        dys = (dmerged * ss).astype(BF16)
        dgates_ref[:, :D_MODEL] = (dmerged * y_attn * sa * (1.0 - sa)).astype(BF16)
        dgates_ref[:, D_MODEL:] = (dmerged * y_ssm * ss * (1.0 - ss)).astype(BF16)
        datt_ref[...] = _dot_nt(dya, wa_ref[...]).astype(BF16)
        dz2 = _dot_nt(dys, ws_ref[...])
        dpre = (dz2 * z * gl * (1.0 - gl)).astype(BF16)
        dz = dz2 * gl + _dot_nt(dpre, wg_ref[...])
        _, gelu_vjp = jax.vjp(jax.nn.gelu, yv)
        dy_ref[...] = gelu_vjp(dz)[0]

        part = pl.ds(pl.multiple_of((i % group) * tile, tile), tile)
        for ref, val in zip(staged, (merged, dob, att, dya, z2b, dys, zb, dpre)):
            ref[part, :] = val

        @pl.when(i == 0)
        def _():
            dg2_ref[...] = dg2
            dg3_ref[...] = dg3

        @pl.when(i > 0)
        def _():
            dg2_ref[...] += dg2
            dg3_ref[...] += dg3

        def weight_grads():
            s_merged, s_dob, s_att, s_dya, s_z2b, s_dys, s_zb, s_dpre = (ref[...] for ref in staged)
            return ((awo_ref, _dot_tn(s_merged, s_dob)), (awa_ref, _dot_tn(s_att, s_dya)),
                    (aws_ref, _dot_tn(s_z2b, s_dys)), (awg_ref, _dot_tn(s_zb, s_dpre)))

        @pl.when(i == group - 1)
        def _():
            for ref, val in weight_grads():
                ref[...] = val

        @pl.when((i % group == group - 1) & (i > group - 1))
        def _():
            for ref, val in weight_grads():
                ref[...] += val

        @pl.when(i == n_steps - 1)
        def _():
            pltpu.sync_copy(awg_ref, dwg_hbm)
            pltpu.sync_copy(aws_ref, dws_hbm)
            pltpu.sync_copy(awa_ref, dwa_hbm)
            pltpu.sync_copy(awo_ref, dwo_hbm)

    tok = lambda w: pl.BlockSpec((tile, w), lambda i: (i, 0))
    vec = _const_spec((1, D_MODEL))
    any_ = pl.BlockSpec(memory_space=pl.ANY)
    vec_out = pl.BlockSpec((1, D_MODEL), lambda i: (0, 0))
    f32 = lambda *s: jax.ShapeDtypeStruct(s, F32)
    bf = lambda *s: jax.ShapeDtypeStruct(s, BF16)
    return _hosted_call(
        body, carry, _edge_1d(n_steps), name="merge_bwd", grid=(n_steps,),
        in_specs=[tok(D_MODEL), tok(D_MODEL), tok(D_MODEL), tok(D_MODEL), tok(SSM_W), tok(ATTN_W), tok(D_MODEL), tok(D_MODEL),
                  vec, vec, _const_spec((SSM_W, SSM_W)), _const_spec((SSM_W, D_MODEL)), _const_spec((ATTN_W, D_MODEL)),
                  _const_spec((D_MODEL, D_MODEL))],
        out_specs=[tok(D_MODEL), tok(2 * D_MODEL), tok(ATTN_W), tok(SSM_W), any_, any_, any_, any_, vec_out, vec_out],
        out_shape=[f32(T, D_MODEL), bf(T, 2 * D_MODEL), bf(T, ATTN_W), f32(T, SSM_W),
                   f32(SSM_W, SSM_W), f32(SSM_W, D_MODEL), f32(ATTN_W, D_MODEL), f32(D_MODEL, D_MODEL),
                   f32(1, D_MODEL), f32(1, D_MODEL)],
        scratch_shapes=[pltpu.VMEM((SSM_W, SSM_W), F32), pltpu.VMEM((SSM_W, D_MODEL), F32),
                        pltpu.VMEM((ATTN_W, D_MODEL), F32), pltpu.VMEM((D_MODEL, D_MODEL), F32)]
        + [pltpu.VMEM((group * tile, wd), BF16) for wd in staged_widths],
        compiler_params=_params(("arbitrary",), VMEM_BIG),
        inputs=(dh2, dx2, x1, o, y, att, ga, gs, g2, g3, w_glu, w_ssm, w_attn, w_out))


FF_SHARD = D_FF // N_DEV


def _mlp_fwd(h2, x1, target, g4, w_ff_in, w_ff_out, tile):
    T = h2.shape[0]
    col_chunk = 2 * FF_SHARD

    def body(h2_ref, x1_ref, tg_ref, g4_ref, wi_ref, wo_ref, a_ref, dfo_ref, dx2_ref, loss_ref, dg4_ref, rr_ref):
        i = pl.program_id(0)
        h2v = h2_ref[...]
        for c in range(D_FF // col_chunk):
            cols = slice(c * col_chunk, (c + 1) * col_chunk)
            a = _dot_nt(h2v, wi_ref[cols, :])
            a_ref[:, cols] = a.astype(BF16)
            ra = jnp.maximum(a, 0.0)
            rr_ref[:, cols] = (ra * ra).astype(BF16)
        f = _dot(rr_ref[...], wo_ref[...])
        r = _rms_scale(f)
        g = g4_ref[...]
        err = x1_ref[...] + f * r * g - tg_ref[...]
        dx2 = err * (1.0 / D_MODEL)
        dx2_ref[...] = dx2
        dfo, dg = _rms_bwd(dx2, f, r, g)
        dfo_ref[...] = dfo.astype(BF16)
        row = lax.broadcasted_iota(jnp.int32, (SUBLANES, LANES), 0)
        col = lax.broadcasted_iota(jnp.int32, (SUBLANES, LANES), 1)
        loss = jnp.where((row == 0) & (col == 0), (0.5 / D_MODEL) * jnp.sum(err * err), 0.0)

        @pl.when(i == 0)
        def _():
            loss_ref[...] = loss
            dg4_ref[...] = dg

        @pl.when(i > 0)
        def _():
            loss_ref[...] += loss
            dg4_ref[...] += dg

    tok = pl.BlockSpec((tile, D_MODEL), lambda i: (i, 0))
    return pl.pallas_call(
        body, name="mlp_fwd", grid=(T // tile,),
        in_specs=[tok, tok, tok, _const_spec((1, D_MODEL)), _const_spec((D_FF, D_MODEL)), _const_spec((D_FF, D_MODEL))],
        out_specs=[pl.BlockSpec((tile, D_FF), lambda i: (i, 0)), tok, tok,
                   pl.BlockSpec((SUBLANES, LANES), lambda i: (0, 0)), pl.BlockSpec((1, D_MODEL), lambda i: (0, 0))],
        out_shape=_hbm_out([jax.ShapeDtypeStruct((T, D_FF), BF16), jax.ShapeDtypeStruct((T, D_MODEL), BF16),
                            jax.ShapeDtypeStruct((T, D_MODEL), F32), jax.ShapeDtypeStruct((SUBLANES, LANES), F32),
                            jax.ShapeDtypeStruct((1, D_MODEL), F32)]),
        scratch_shapes=[pltpu.VMEM((tile, D_FF), BF16)],
        compiler_params=_params(("arbitrary",), VMEM_MAX),
    )(*_in_hbm(h2, x1, target, g4, w_ff_in.reshape(D_FF, D_MODEL), w_ff_out.reshape(D_FF, D_MODEL)))


def _mlp_weight_grads(dfo, a, h2, w_ff_out, row_chunk):
    T = h2.shape[0]

    def body(dfo_ref, h2_ref, a_ref, wo_ref, dwi_ref, dwo_ref, da_ref, rr_ref):
        def rows(r, _):
            sl = pl.ds(pl.multiple_of(r * row_chunk, row_chunk), row_chunk)
            ra = jnp.maximum(a_ref[sl, :].astype(F32), 0.0)
            da_ref[sl, :] = (_dot_nt(dfo_ref[sl, :], wo_ref[0]) * (2.0 * ra)).astype(BF16)
            rr_ref[sl, :] = (ra * ra).astype(BF16)
            return 0

        lax.fori_loop(0, T // row_chunk, rows, 0)
        dwo_ref[0] = _dot_tn(rr_ref[...], dfo_ref[...])
        dwi_ref[0] = _dot_tn(h2_ref[...], da_ref[...])

    return pl.pallas_call(
        body, name="mlp_weight_grads", grid=(N_DEV,),
        in_specs=[_const_spec((T, D_MODEL)), _const_spec((T, D_MODEL)), pl.BlockSpec((T, FF_SHARD), lambda k: (0, k)),
                  pl.BlockSpec((1, FF_SHARD, D_MODEL), lambda k: (k, 0, 0))],
        out_specs=[pl.BlockSpec((1, D_MODEL, FF_SHARD), lambda k: (k, 0, 0)),
                   pl.BlockSpec((1, FF_SHARD, D_MODEL), lambda k: (k, 0, 0)), pl.BlockSpec((T, FF_SHARD), lambda k: (0, k))],
        out_shape=_hbm_out([jax.ShapeDtypeStruct((N_DEV, D_MODEL, FF_SHARD), F32),
                            jax.ShapeDtypeStruct((N_DEV, FF_SHARD, D_MODEL), F32), jax.ShapeDtypeStruct((T, D_FF), BF16)]),
        scratch_shapes=[pltpu.VMEM((T, FF_SHARD), BF16)],
        compiler_params=_params(("arbitrary",), VMEM_MAX),
    )(*_in_hbm(dfo, h2, a, w_ff_out))


def _mlp_input_grad(da, w_ff_in_t, tile):
    T = da.shape[0]

    def body(da_ref, w_ref, o_ref):
        o_ref[...] = _dot(da_ref[...], w_ref[...])

    return pl.pallas_call(
        body, name="mlp_input_grad", grid=(T // tile,),
        in_specs=[pl.BlockSpec((tile, D_FF), lambda i: (i, 0)), _const_spec((D_FF, D_MODEL))],
        out_specs=pl.BlockSpec((tile, D_MODEL), lambda i: (i, 0)),
        out_shape=_hbm_out(jax.ShapeDtypeStruct((T, D_MODEL), F32)),
        compiler_params=_params(("arbitrary",), VMEM_MID),
    )(*_in_hbm(da, w_ff_in_t))


def _block_diag_in(b):
    bt = b.reshape(N_SSM_BLOCKS, GROUPS_PER_BLOCK, GROUP_CH, N_STATE)
    eye = jnp.eye(GROUPS_PER_BLOCK, dtype=b.dtype)
    return jnp.einsum("jacp,ab->jacbp", bt, eye).reshape(N_SSM_BLOCKS, LANES, SSM_LANE_BLOCK)


def _block_diag_in_grad(g):
    g = g.reshape(N_SSM_BLOCKS, GROUPS_PER_BLOCK, GROUP_CH, GROUPS_PER_BLOCK, N_STATE)
    d = jnp.diagonal(g, axis1=1, axis2=3)
    return jnp.transpose(d, (0, 3, 1, 2)).reshape(N_GROUPS, GROUP_CH, N_STATE)


def _block_diag_out(c):
    ct = c.reshape(N_SSM_BLOCKS, GROUPS_PER_BLOCK, GROUP_CH, N_STATE)
    eye = jnp.eye(GROUPS_PER_BLOCK, dtype=c.dtype)
    return jnp.einsum("jacp,ab->japbc", ct, eye).reshape(N_SSM_BLOCKS, SSM_LANE_BLOCK, LANES)


def _block_diag_out_grad(g):
    g = g.reshape(N_SSM_BLOCKS, GROUPS_PER_BLOCK, N_STATE, GROUPS_PER_BLOCK, GROUP_CH)
    d = jnp.diagonal(g, axis1=1, axis2=3)
    return jnp.transpose(d, (0, 3, 2, 1)).reshape(N_GROUPS, GROUP_CH, N_STATE)


def _tiles(T):
    return dict(proj=min(512, T), proj_bwd=min(512, T // 2), merge=min(512, T), merge_bwd=min(256, T),
                mlp_fwd=min(512, T), mlp_bwd=min(512, T), ssm_chunk=min(1024, T))


def _mesh_position():
    x, y, c = lax.axis_index("x"), lax.axis_index("y"), lax.axis_index("c")
    other_chips = [(1 - x, y), (x, 1 - y), (1 - x, 1 - y)]
    return x, y, c, other_chips


def _gather_carry(arrays):
    n = len(arrays)

    def copies(ins, outs, sems):
        send_sems, recv_sems, local_sems = sems
        x, y, c, chips = _mesh_position()
        me, sibling = (x, y, c), (x, y, 1 - c)

        def copy(a, k, block, to, src=None):
            px, py, pc = block
            dst = outs[a].at[4 * px + 2 * py + pc]
            return pltpu.make_async_remote_copy(
                src_ref=dst if src is None else src, dst_ref=dst, send_sem=send_sems.at[7 * a + k],
                recv_sem=recv_sems.at[7 * a + k], device_id=to, device_id_type=MESH_IDS)

        mine = [pltpu.make_async_copy(ins[a], outs[a].at[4 * x + 2 * y + c], local_sems.at[a]) for a in range(n)]
        first = []
        for a in range(n):
            first.append(copy(a, 0, me, sibling, src=ins[a]))
            first += [copy(a, 1 + j, me, (*chip, c), src=ins[a]) for j, chip in enumerate(chips)]
        return copy, mine, first, me, sibling, chips, c

    def start(ins, outs, sems):
        _, mine, first, *_ = copies(ins, outs, sems)
        for cp in mine + first:
            cp.start()

    def finish(ins, outs, sems):
        copy, mine, first, me, sibling, chips, c = copies(ins, outs, sems)
        passed = []
        for a in range(n):
            for j, chip in enumerate(chips):
                copy(a, 1 + j, (*chip, c), me).wait_recv()
                passed.append(copy(a, 4 + j, (*chip, c), sibling))
                passed[-1].start()
        for a in range(n):
            copy(a, 0, sibling, me).wait_recv()
            for j, chip in enumerate(chips):
                copy(a, 4 + j, (*chip, 1 - c), me).wait_recv()
        for cp in first + passed:
            cp.wait_send()
        for cp in mine:
            cp.wait()

    return _Carry(arrays, [jax.ShapeDtypeStruct((N_DEV,) + a.shape, a.dtype) for a in arrays],
                  [pltpu.SemaphoreType.DMA((7 * n,)), pltpu.SemaphoreType.DMA((7 * n,)), pltpu.SemaphoreType.DMA((n,))],
                  start, finish)


def _pairwise_carry(arrays, n_slots, make_copies):
    n = len(arrays)

    def start(ins, outs, sems):
        for cp in make_copies(ins, outs, sems):
            cp.start()

    def finish(ins, outs, sems):
        for cp in make_copies(ins, outs, sems):
            cp.wait()

    return _Carry(arrays, [jax.ShapeDtypeStruct((n_slots,) + a.shape[1:], a.dtype) for a in arrays],
                  [pltpu.SemaphoreType.DMA((n_slots * n,)), pltpu.SemaphoreType.DMA((n_slots * n,))], start, finish)


def _sibling_carry(grads):
    def make_copies(ins, outs, sems):
        x, y, c, _ = _mesh_position()
        return [pltpu.make_async_remote_copy(
            src_ref=ins[a].at[2 * ch + (1 - c)], dst_ref=outs[a].at[ch], send_sem=sems[0].at[4 * a + ch],
            recv_sem=sems[1].at[4 * a + ch], device_id=(x, y, 1 - c), device_id_type=MESH_IDS)
            for a in range(len(grads)) for ch in range(4)]

    return _pairwise_carry(grads, 4, make_copies)


def _chips_carry(sums):
    def make_copies(ins, outs, sems):
        x, y, c, chips = _mesh_position()
        return [pltpu.make_async_remote_copy(
            src_ref=ins[a].at[2 * px + py], dst_ref=outs[a].at[j], send_sem=sems[0].at[3 * a + j],
            recv_sem=sems[1].at[3 * a + j], device_id=(px, py, c), device_id_type=MESH_IDS)
            for a in range(len(sums)) for j, (px, py) in enumerate(chips)]

    return _pairwise_carry(sums, 3, make_copies)


SEM_SPEC = pl.BlockSpec(memory_space=pltpu.SEMAPHORE)
DATAFLOW_EFFECT = pltpu.SideEffectType.DATAFLOW_SIDE_EFFECTING


def _exchange_start(carry, name, after=()):
    n = len(carry.inputs)
    lands = [lax.empty(s.shape, s.dtype) for s in carry.out_shapes]

    def body(*refs):
        first_out = 2 * n + len(after)
        srcs, zones, sems, token = refs[:n], refs[n:2 * n], refs[first_out:first_out + 2], refs[-1]
        carry.start(srcs, zones, sems)
        token[...] = jnp.zeros_like(token)

    outs = pl.pallas_call(
        body, name=name, in_specs=[HBM_SPEC] * (2 * n + len(after)),
        out_specs=[SEM_SPEC, SEM_SPEC] + [HBM_SPEC] * (2 * n) + [pl.BlockSpec(memory_space=pltpu.VMEM)],
        out_shape=list(carry.sems) + _hbm_out([jax.ShapeDtypeStruct(a.shape, a.dtype) for a in carry.inputs])
        + _hbm_out(carry.out_shapes) + [jax.ShapeDtypeStruct((SUBLANES, LANES), F32)],
        input_output_aliases={j: 2 + j for j in range(2 * n)},
        compiler_params=pltpu.CompilerParams(has_side_effects=DATAFLOW_EFFECT),
    )(*_in_hbm(*carry.inputs, *lands), *after)
    return outs[:-1], outs[-1]


def _exchange_wait(carry, in_flight, after, name):
    n = len(carry.inputs)
    sems, srcs, zones = in_flight[:2], in_flight[2:2 + n], in_flight[2 + n:]

    def body(*refs):
        src_refs, zone_refs, sem_refs = refs[:n], refs[n:2 * n], refs[2 * n:2 * n + 2]
        carry.finish(src_refs, zone_refs, sem_refs)

    outs = pl.pallas_call(
        body, name=name, in_specs=[HBM_SPEC] * (2 * n) + [SEM_SPEC, SEM_SPEC] + [HBM_SPEC] * len(after),
        out_specs=[HBM_SPEC] * (2 * n),
        out_shape=_hbm_out([jax.ShapeDtypeStruct(a.shape, a.dtype) for a in carry.inputs]) + _hbm_out(carry.out_shapes),
        input_output_aliases={j: j for j in range(2 * n)},
        compiler_params=pltpu.CompilerParams(has_side_effects=DATAFLOW_EFFECT),
    )(*srcs, *zones, *sems, *after)
    return list(outs[:n]), list(outs[n:])


def _add_sibling(grads8, recvs, core, row_tiles, name):
    k = len(grads8)
    g4 = [g.reshape(4, 2, *g.shape[1:]) for g in grads8]

    def body(core_ref, *refs):
        g_refs, r_refs, o_refs, ob_refs = (refs[j * k:(j + 1) * k] for j in range(4))
        for g_ref, r_ref, o_ref, ob_ref in zip(g_refs, r_refs, o_refs, ob_refs):
            s = g_ref[0] + r_ref[...]
            o_ref[...] = s
            ob_ref[...] = s.astype(BF16)

    def blocks(make):
        return [make(g.shape[1] // row_tiles, g.shape[2]) for g in grads8]

    slot = lambda tr, C: pl.BlockSpec((1, tr, C), lambda ch, r, core_ref: (ch, r, 0))
    outs = pl.pallas_call(
        body, name=name,
        grid_spec=pltpu.PrefetchScalarGridSpec(
            num_scalar_prefetch=1, grid=(4, row_tiles),
            in_specs=blocks(lambda tr, C: pl.BlockSpec((1, 1, tr, C), lambda ch, r, core_ref: (ch, core_ref[0], r, 0)))
            + blocks(slot), out_specs=blocks(slot) + blocks(slot)),
        out_shape=_hbm_out([jax.ShapeDtypeStruct((4,) + g.shape[1:], F32) for g in grads8]
                           + [jax.ShapeDtypeStruct((4,) + g.shape[1:], BF16) for g in grads8]),
        compiler_params=_params(("arbitrary", "arbitrary")),
    )(core, *_in_hbm(*g4, *recvs))
    return list(outs[:k]), list(outs[k:])


def _adam_math(w, g, m, v):
    m = ADAM_B1 * m + (1.0 - ADAM_B1) * g
    v = ADAM_B2 * v + (1.0 - ADAM_B2) * jnp.square(g)
    m_hat = m / (1.0 - ADAM_B1 ** ADAM_STEP)
    v_hat = v / (1.0 - ADAM_B2 ** ADAM_STEP)
    delta = -ADAM_LR * (m_hat / (jnp.sqrt(v_hat) + ADAM_EPS) + ADAM_WD * w)
    return delta, m, v


def _adam_big(ws, ms, vs, chip_sums, recvs, chip, row_tiles, name, after=()):
    k = len(ws)

    def body(chip_ref, *refs):
        refs = refs[:5 * k] + refs[5 * k + len(after):]
        w_refs, m_refs, v_refs, s_refs, r_refs, g_refs, d_refs, nm_refs, nv_refs = (refs[j * k:(j + 1) * k] for j in range(9))
        for a in range(k):
            r_ref = r_refs[a]
            g = s_refs[a][0] + r_ref[0].astype(F32) + r_ref[1].astype(F32) + r_ref[2].astype(F32)
            g_refs[a][...] = g
            d_refs[a][...], nm_refs[a][...], nv_refs[a][...] = _adam_math(w_refs[a][...], g, m_refs[a][...], v_refs[a][...])

    def blocks(make):
        return [make(w.shape[0] // row_tiles, w.shape[1]) for w in ws]

    blk = lambda tr, C: pl.BlockSpec((tr, C), lambda r, chip_ref: (r, 0))
    outs = pl.pallas_call(
        body, name=name,
        grid_spec=pltpu.PrefetchScalarGridSpec(
            num_scalar_prefetch=1, grid=(row_tiles,),
            in_specs=blocks(blk) * 3 + blocks(lambda tr, C: pl.BlockSpec((1, tr, C), lambda r, chip_ref: (chip_ref[0], r, 0)))
            + blocks(lambda tr, C: pl.BlockSpec((3, tr, C), lambda r, chip_ref: (0, r, 0))) + [HBM_SPEC] * len(after),
            out_specs=blocks(blk) * 4),
        out_shape=[jax.ShapeDtypeStruct(w.shape, F32) for w in ws] * 4,
        compiler_params=_params(("arbitrary",)),
    )(chip, *_in_hbm(*ws, *ms, *vs, *chip_sums, *recvs), *after)
    return [list(outs[j * k:(j + 1) * k]) for j in range(4)]


def _sum_partials(partials, name, after=()):
    def body(p_ref, *refs):
        g = p_ref[0]
        for d in range(1, partials.shape[0]):
            g = g + p_ref[d]
        refs[-1][...] = g

    return pl.pallas_call(body, name=name, grid=(1,), in_specs=[_whole(partials.shape)] + [HBM_SPEC] * len(after),
                          out_specs=_whole(partials.shape[1:]),
                          out_shape=jax.ShapeDtypeStruct(partials.shape[1:], F32))(*_in_hbm(partials), *after)


def _adam_small(ws, ms, vs, gs):
    n = len(ws)

    def body(*refs):
        w_refs, m_refs, v_refs, g_refs = (refs[i * n:(i + 1) * n] for i in range(4))
        d_refs, nm_refs, nv_refs = (refs[(4 + i) * n:(5 + i) * n] for i in range(3))
        for j in range(n):
            d_refs[j][...], nm_refs[j][...], nv_refs[j][...] = _adam_math(
                w_refs[j][...], g_refs[j][...], m_refs[j][...], v_refs[j][...])

    specs = [_whole(w.shape) for w in ws]
    outs = pl.pallas_call(body, name="adam_small", grid=(1,), in_specs=specs * 4, out_specs=specs * 3,
                          out_shape=[jax.ShapeDtypeStruct(w.shape, F32) for w in ws] * 3,
                          compiler_params=_params(("arbitrary",), VMEM_MID))(*_in_hbm(*ws, *ms, *vs, *gs))
    return outs[:n], outs[n:2 * n], outs[2 * n:]


PACK_QUANTUM = SUBLANES * LANES


def _pack(named, names):
    parts = []
    for nme in names:
        flat = named[nme].reshape(-1)
        parts.append(jnp.pad(flat, (0, -flat.size % PACK_QUANTUM)))
    return jnp.concatenate(parts).reshape(-1, LANES)


def _unpack(packed, shapes, names):
    flat = packed.reshape(-1)
    out, pos = {}, 0
    for nme in names:
        size = math.prod(shapes[nme])
        out[nme] = flat[pos:pos + size].reshape(shapes[nme])
        pos += size + (-size % PACK_QUANTUM)
    return out


BIG = ("w_in", "w_glu", "w_attn_branch", "w_ssm_branch", "w_out", "w_ff_in", "w_ff_out")
COLUMN_SHARDED = ("w_in", "w_attn_branch", "w_ssm_branch", "w_ff_in")
SMALL = ("norm_mix_pre", "norm_mix_post", "norm_mlp_pre", "norm_mlp_post", "rel_bias", "sinks", "lam_re", "lam_im",
         "log_dt", "b_re", "b_im", "c_re", "c_im", "d_skip")
SWAPPED_SMALL = ("rel_bias", "b_re", "b_im")
SMALL_LATE = ("norm_mix_pre", "rel_bias", "sinks", "loss")
SMALL_BEFORE_ATTN_BWD = tuple(n for n in SMALL if n not in SMALL_LATE)
ALL_WEIGHTS = ("norm_mix_pre", "norm_mix_post", "norm_mlp_pre", "norm_mlp_post", "w_in", "rel_bias", "sinks", "lam_re",
               "lam_im", "log_dt", "b_re", "b_im", "c_re", "c_im", "d_skip", "w_glu", "w_attn_branch", "w_ssm_branch",
               "w_out", "w_ff_in", "w_ff_out")


def _full_from_gathered(name, gathered):
    _, r, c = gathered.shape
    if name in COLUMN_SHARDED:
        return jnp.transpose(gathered, (1, 0, 2)).reshape(r, N_DEV * c)
    return gathered.reshape(N_DEV * r, c)


def _blocks_from_full(name, full):
    r, c = full.shape
    if name in COLUMN_SHARDED:
        return jnp.transpose(full.reshape(r, N_DEV, c // N_DEV), (1, 0, 2))
    return full.reshape(N_DEV, r // N_DEV, c)


def kernel(x, norm_mix_pre, norm_mix_post, norm_mlp_pre, norm_mlp_post, w_in, rel_bias, sinks, lam_re, lam_im, log_dt, b_re, b_im, c_re, c_im, d_skip, w_glu, w_attn_branch, w_ssm_branch, w_out, w_ff_in, w_ff_out, loss_target, m_norm_mix_pre, m_norm_mix_post, m_norm_mlp_pre, m_norm_mlp_post, m_w_in, m_rel_bias, m_sinks, m_lam_re, m_lam_im, m_log_dt, m_b_re, m_b_im, m_c_re, m_c_im, m_d_skip, m_w_glu, m_w_attn_branch, m_w_ssm_branch, m_w_out, m_w_ff_in, m_w_ff_out, v_norm_mix_pre, v_norm_mix_post, v_norm_mlp_pre, v_norm_mlp_post, v_w_in, v_rel_bias, v_sinks, v_lam_re, v_lam_im, v_log_dt, v_b_re, v_b_im, v_c_re, v_c_im, v_d_skip, v_w_glu, v_w_attn_branch, v_w_ssm_branch, v_w_out, v_w_ff_in, v_w_ff_out):
    args = dict(locals())
    w = {n: args[n] for n in ALL_WEIGHTS}
    m = {n: args["m_" + n] for n in ALL_WEIGHTS}
    v = {n: args["v_" + n] for n in ALL_WEIGHTS}
    core = lax.axis_index("c").astype(jnp.int32).reshape(1)
    chip = (2 * lax.axis_index("x") + lax.axis_index("y")).astype(jnp.int32).reshape(1)
    xs, target = x[0], loss_target[0]
    t = _tiles(xs.shape[0])
    local = lambda d, n: d[n][0].T if n == "w_in" else d[n][0]
    shard = {n: local(w, n).astype(BF16) for n in BIG}
    shard["w_ff_in"] = shard["w_ff_in"].T
    view = lambda n, a: jnp.swapaxes(a, -1, -2) if n in SWAPPED_SMALL else a
    small = {n: (view(n, w[n]) if n == "rel_bias" else view(n, w[n])[0]) for n in SMALL}
    g1, g2, g3, g4 = (small[n].reshape(1, D_MODEL) for n in ("norm_mix_pre", "norm_mix_post", "norm_mlp_pre", "norm_mlp_post"))
    bucket = jnp.asarray(_bucket_table())
    rel_b, sink = small["rel_bias"], small["sinks"].reshape(1, N_HEADS)
    lam_r, lam_i = small["lam_re"].reshape(1, STATES), small["lam_im"].reshape(1, STATES)
    ldt_rep = jnp.repeat(small["log_dt"].reshape(N_GROUPS), N_STATE).reshape(1, STATES)
    bd_re, bd_im = _block_diag_in(small["b_re"]), _block_diag_in(small["b_im"])
    cm_re, cm_im = _block_diag_out(small["c_re"]).astype(BF16), _block_diag_out(small["c_im"]).astype(BF16)
    dsk = small["d_skip"].reshape(1, SSM_W)

    (g_in,) = _run_carry(_gather_carry([shard["w_in"]]), "gather_w_in")
    wf_in = g_in.reshape(IN_W, D_MODEL)
    merge_names = ("w_glu", "w_attn_branch", "w_ssm_branch", "w_out")
    (q, k, vv, u, ga, gs, h), gathered = _in_proj_fwd(xs, g1, wf_in, t["proj"], _gather_carry([shard[n] for n in merge_names]))
    wf = {n: _full_from_gathered(n, g) for n, g in zip(merge_names, gathered)}
    (att,), (wf_ff_in,) = _attn_fwd(q, k, vv, bucket, rel_b, sink, _gather_carry([shard["w_ff_in"]]))
    a_re, a_im, bm_re, bm_im = _ssm_prep(lam_r, lam_i, ldt_rep, bd_re, bd_im)
    (y, h_re, h_im, in_re, in_im), (wf_ff_out,) = _ssm_fwd(
        u, a_re, a_im, bm_re, bm_im, cm_re, cm_im, dsk, t["ssm_chunk"], _gather_carry([shard["w_ff_out"]]))
    x1, o, h2 = _merge_fwd(xs, y, att, ga, gs, g2, g3, wf["w_glu"], wf["w_ssm_branch"], wf["w_attn_branch"], wf["w_out"],
                           t["merge"])
    a, dfo, dx2, loss_blk, dg4 = _mlp_fwd(h2, x1, target, g4, wf_ff_in, wf_ff_out, t["mlp_fwd"])

    groups = {"ff": 4, "merge": 1, "w_in": 2}

    def add_sibling(group, blocks, received):
        return _add_sibling(blocks, received, core, groups[group], "add_sibling_" + group)

    ff_names = ("w_ff_in", "w_ff_out")
    dw_ff_in, dw_ff_out, da = _mlp_weight_grads(dfo, a, h2, wf_ff_out, t["mlp_bwd"])
    dh2 = _mlp_input_grad(da, wf_ff_in.reshape(D_FF, D_MODEL), t["mlp_bwd"])
    ff_blocks = [dw_ff_in, dw_ff_out]
    (dx1, dgates, datt, dy, dw_glu, dw_ssm, dw_attn, dw_out, dg2, dg3), ff_recv = _merge_bwd(
        dh2, dx2, x1, o, y, att, ga, gs, g2, g3, wf["w_glu"], wf["w_ssm_branch"], wf["w_attn_branch"], wf["w_out"],
        t["merge_bwd"], _sibling_carry(ff_blocks))
    ff_sums, ff_sums_bf = add_sibling("ff", ff_blocks, ff_recv)
    merge_blocks = [_blocks_from_full(n, g) for n, g in zip(merge_names, (dw_glu, dw_attn, dw_ssm, dw_out))]
    (du, dbm_re, dbm_im, dcm_re, dcm_im, da_re, da_im, dd_skip), carried = _ssm_bwd(
        dy, u, h_re, h_im, in_re, in_im, a_re, a_im, bm_re, bm_im, cm_re, cm_im, dsk, t["ssm_chunk"],
        _join(_chips_carry(ff_sums_bf), _sibling_carry(merge_blocks)))
    ff_from_chips, merge_recv = carried[:2], carried[2:]
    merge_sums, merge_sums_bf = add_sibling("merge", merge_blocks, merge_recv)
    dbd_re, dbd_im, dlam_re, dlam_im, dldt_rep = _ssm_prep_bwd(lam_r, lam_i, ldt_rep, bd_re, bd_im, dbm_re, dbm_im, da_re, da_im)
    dlog_dt = _group_sum(dldt_rep.reshape(N_GROUPS, N_STATE))
    shapes = {n: view(n, w[n]).shape for n in SMALL}
    shapes["loss"] = (1,)
    small_grads = dict(
        norm_mix_post=dg2, norm_mlp_pre=dg3, norm_mlp_post=dg4, lam_re=dlam_re, lam_im=dlam_im, log_dt=dlog_dt,
        b_re=_block_diag_in_grad(dbd_re), b_im=_block_diag_in_grad(dbd_im),
        c_re=_block_diag_out_grad(dcm_re), c_im=_block_diag_out_grad(dcm_im), d_skip=dd_skip)
    packed_early = _pack({n: small_grads[n].reshape(shapes[n]) for n in SMALL_BEFORE_ATTN_BWD}, SMALL_BEFORE_ATTN_BWD)
    (dq, dkv, attn_small), carried = _attn_bwd(
        q, k, vv, datt, bucket, rel_b, sink, _join(_chips_carry(merge_sums_bf), _gather_carry([packed_early])))
    merge_from_chips, partials_early = carried[:-1], carried[-1]

    dparts = (dq, dkv, du, dgates)
    dw_in_t = _in_proj_weight_grad(h, dparts)
    in_blocks = [dw_in_t.reshape(N_DEV, IN_W // N_DEV, D_MODEL)]
    to_sibling = _sibling_carry(in_blocks)
    in_flight, token = _exchange_start(to_sibling, "w_in_sibling_start")
    n_tiles = xs.shape[0] // t["proj_bwd"]
    (grad_x, dg1), _ = _in_proj_input_grad(xs, g1 + token[0:1, 0:1], wf_in, dx1, dparts, t["proj_bwd"], 0, n_tiles, "in_proj_input_grad")
    late = dict(norm_mix_pre=dg1, rel_bias=attn_small[:, :N_BUCKETS, 0], sinks=attn_small[:, N_BUCKETS, 0], loss=loss_blk[0:1, 0])
    packed_late = _pack({n: late[n].reshape(shapes[n]) for n in SMALL_LATE}, SMALL_LATE)
    (partials_late,) = _run_carry(_gather_carry([packed_late]), "gather_late_grads")
    in_blocks, in_recv = _exchange_wait(to_sibling, in_flight, [partials_late], "w_in_sibling_wait")
    in_sums, in_sums_bf = add_sibling("w_in", in_blocks, in_recv)
    to_chips = _chips_carry(in_sums_bf)
    in_flight, chips_started = _exchange_start(to_chips, "w_in_chips_start")

    grads, deltas, new_m, new_v = {}, {}, {}, {}

    def adam_group(group, names, sums, received, after=()):
        outs = _adam_big(*[[local(d, n) for n in names] for d in (w, m, v)], sums, received, chip, groups[group],
                         "adam_" + group, after)
        for store, vals in zip((grads, deltas, new_m, new_v), outs):
            store.update({n: (o.T if n == "w_in" else o)[None] for n, o in zip(names, vals)})

    adam_group("ff", ff_names, ff_sums, ff_from_chips, [chips_started])
    adam_group("merge", merge_names, merge_sums, merge_from_chips, [chips_started])

    grads.update(_unpack(_sum_partials(partials_early, "sum_small_grads", [chips_started]), shapes, SMALL_BEFORE_ATTN_BWD))
    grads.update(_unpack(_sum_partials(partials_late, "sum_late_grads", [chips_started]), shapes, SMALL_LATE))
    loss = grads.pop("loss").reshape(())
    small_out = _adam_small(*[[view(n, d[n]) for n in SMALL] for d in (w, m, v)], [grads[n] for n in SMALL])
    for store, vals in zip((deltas, new_m, new_v), small_out):
        store.update(zip(SMALL, vals))
    for store in (grads, deltas, new_m, new_v):
        store.update({n: view(n, store[n]) for n in SWAPPED_SMALL})

    busy = [new_v["w_ff_out"], new_v["w_out"], deltas["norm_mix_pre"]]
    _, (in_from_chips,) = _exchange_wait(to_chips, in_flight, busy, "w_in_chips_wait")
    adam_group("w_in", ("w_in",), in_sums, [in_from_chips])

    return (loss, grad_x[None], *[grads[n] for n in ALL_WEIGHTS], *[deltas[n] for n in ALL_WEIGHTS],
            *[new_m[n] for n in ALL_WEIGHTS], *[new_v[n] for n in ALL_WEIGHTS])
```

```python
import math

import jax
import jax.numpy as jnp
import numpy as np
from jax import lax
from jax.experimental import pallas as pl
from jax.experimental.pallas import tpu as pltpu

F32 = jnp.float32
BF16 = jnp.bfloat16

D_MODEL = 1024
N_HEADS = 8
HEAD_DIM = 64
ATTN_W = 512
KV_W = 128
BLOCK = 128
N_BUCKETS = 32
SSM_W = 512
N_GROUPS = 32
N_STATE = 64
GROUP_CH = 16
STATES = N_GROUPS * N_STATE
D_FF = 4096
IN_W = 3328
SPLITS = (0, 512, 640, 768, 1280, 2304, 3328)
RMS_EPS = 1e-6
NEG_INF = -1e30
SUBLANES = 8
LANES = 128
SSM_LANE_BLOCK = 512
N_SSM_BLOCKS = STATES // SSM_LANE_BLOCK
GROUPS_PER_BLOCK = SSM_LANE_BLOCK // N_STATE
VMEM_BIG = 52 * 1024 * 1024
VMEM_MID = 40 * 1024 * 1024
VMEM_MAX = 60 * 1024 * 1024

ADAM_LR = 0.001
ADAM_B1 = 0.9
ADAM_B2 = 0.999
ADAM_EPS = 1e-08
ADAM_WD = 0.01
ADAM_STEP = 10

N_DEV = 8


def _dot(a, b):
    return jnp.dot(a, b, preferred_element_type=F32)


def _dot_nt(a, b):
    return lax.dot_general(a, b, (((1,), (1,)), ((), ())), preferred_element_type=F32)


def _dot_tn(a, b):
    return lax.dot_general(a, b, (((0,), (0,)), ((), ())), preferred_element_type=F32)


def _rms_scale(x):
    return lax.rsqrt(jnp.mean(x * x, axis=-1, keepdims=True) + RMS_EPS)


def _rms_bwd(dy, x, r, g):
    t = dy * g
    dx = r * t - x * (r * r * r) * jnp.mean(t * x, axis=-1, keepdims=True)
    dg = jnp.sum(dy * x * r, axis=0, keepdims=True)
    return dx, dg


def _const_spec(shape):
    nd = len(shape)
    return pl.BlockSpec(shape, lambda *_: (0,) * nd, pipeline_mode=pl.Buffered(1))


def _in_hbm(*arrays):
    return tuple(pltpu.with_memory_space_constraint(a, pltpu.HBM) for a in arrays)


def _hbm_out(shapes):
    if isinstance(shapes, (list, tuple)):
        return [_hbm_out(s) for s in shapes]
    return shapes if isinstance(shapes, pl.MemoryRef) else pltpu.HBM(shapes.shape, shapes.dtype)


def _whole(shape):
    nd = len(shape)
    return pl.BlockSpec(shape, lambda *_: (0,) * nd)


def _params(sem, vmem=None):
    return pltpu.CompilerParams(dimension_semantics=sem, vmem_limit_bytes=vmem)


MESH_IDS = pl.DeviceIdType.MESH
HBM_SPEC = pl.BlockSpec(memory_space=pl.ANY)


class _Carry:
    def __init__(self, inputs, out_shapes, sems, start, finish):
        self.inputs, self.out_shapes, self.sems, self.start, self.finish = list(inputs), list(out_shapes), list(sems), start, finish


def _join(a, b):
    na_in, na_out, na_sem = len(a.inputs), len(a.out_shapes), len(a.sems)

    def start(ins, outs, sems):
        a.start(ins[:na_in], outs[:na_out], sems[:na_sem])
        b.start(ins[na_in:], outs[na_out:], sems[na_sem:])

    def finish(ins, outs, sems):
        a.finish(ins[:na_in], outs[:na_out], sems[:na_sem])
        b.finish(ins[na_in:], outs[na_out:], sems[na_sem:])

    return _Carry(a.inputs + b.inputs, a.out_shapes + b.out_shapes, a.sems + b.sems, start, finish)


def _hosted_call(body, carry, edge, *, name, grid, in_specs, out_specs, out_shape, scratch_shapes, compiler_params, inputs):
    n_in, n_out = len(in_specs), len(out_specs)
    inputs = [a if s.memory_space == pltpu.SMEM else _in_hbm(a)[0] for a, s in zip(inputs, in_specs)]
    out_shape = _hbm_out(list(out_shape))
    if carry is None:
        outs = pl.pallas_call(body, name=name, grid=grid, in_specs=in_specs, out_specs=out_specs, out_shape=out_shape,
                              scratch_shapes=scratch_shapes, compiler_params=compiler_params)(*inputs)
        return list(outs), []
    c_in, c_out, c_sem = len(carry.inputs), len(carry.out_shapes), len(carry.sems)

    def wrapped(*refs):
        ins, refs = refs[:n_in], refs[n_in:]
        cins, refs = refs[:c_in], refs[c_in:]
        outs, refs = refs[:n_out], refs[n_out:]
        couts, refs = refs[:c_out], refs[c_out:]
        scratch, csems = refs[:len(refs) - c_sem], refs[len(refs) - c_sem:]
        first, last = edge()

        @pl.when(first)
        def _():
            carry.start(cins, couts, csems)

        body(*ins, *outs, *scratch)

        @pl.when(last)
        def _():
            carry.finish(cins, couts, csems)

    outs = pl.pallas_call(
        wrapped, name=name, grid=grid, in_specs=list(in_specs) + [HBM_SPEC] * c_in,
        out_specs=list(out_specs) + [HBM_SPEC] * c_out, out_shape=out_shape + _hbm_out(carry.out_shapes),
        scratch_shapes=list(scratch_shapes) + carry.sems, compiler_params=compiler_params)(*inputs, *_in_hbm(*carry.inputs))
    return list(outs[:n_out]), list(outs[n_out:])


def _edge_1d(n_steps):
    return lambda: (pl.program_id(0) == 0, pl.program_id(0) == n_steps - 1)


def _edge_2d(n0, n1):
    return lambda: ((pl.program_id(0) == 0) & (pl.program_id(1) == 0),
                    (pl.program_id(0) == n0 - 1) & (pl.program_id(1) == n1 - 1))


def _run_carry(carry, name):
    c_in, c_out = len(carry.inputs), len(carry.out_shapes)

    def body(*refs):
        ins, outs, sems = refs[:c_in], refs[c_in:c_in + c_out], refs[c_in + c_out:]
        carry.start(ins, outs, sems)
        carry.finish(ins, outs, sems)

    return pl.pallas_call(body, name=name, in_specs=[HBM_SPEC] * c_in, out_specs=[HBM_SPEC] * c_out,
                          out_shape=_hbm_out(carry.out_shapes), scratch_shapes=carry.sems)(*_in_hbm(*carry.inputs))


def _in_proj_fwd(x, g1, w_in_t, tile, carry=None):
    T = x.shape[0]

    def body(x_ref, g_ref, w_ref, q_ref, k_ref, v_ref, u_ref, ga_ref, gs_ref, h_ref):
        xv = x_ref[...]
        h = (xv * _rms_scale(xv) * g_ref[...]).astype(BF16)
        h_ref[...] = h
        outs = (q_ref, k_ref, v_ref, u_ref, ga_ref, gs_ref)
        for p, o_ref in enumerate(outs):
            o_ref[...] = _dot_nt(h, w_ref[SPLITS[p]:SPLITS[p + 1], :]).astype(o_ref.dtype)

    widths = [SPLITS[p + 1] - SPLITS[p] for p in range(6)] + [D_MODEL]
    dtypes = [BF16, BF16, BF16, F32, F32, F32, BF16]
    return _hosted_call(
        body, carry, _edge_1d(T // tile), name="in_proj_fwd", grid=(T // tile,),
        in_specs=[pl.BlockSpec((tile, D_MODEL), lambda i: (i, 0)), _const_spec((1, D_MODEL)), _const_spec((IN_W, D_MODEL))],
        out_specs=[pl.BlockSpec((tile, w), lambda i: (i, 0)) for w in widths],
        out_shape=[jax.ShapeDtypeStruct((T, w), dt) for w, dt in zip(widths, dtypes)],
        scratch_shapes=[], compiler_params=_params(("arbitrary",), VMEM_MID), inputs=(x, g1, w_in_t))


PROJ_PARTS = (512, 256, 512, 2048)
PROJ_GRAD_BLOCK = 256


def _in_proj_weight_grad(h, dparts):
    T = h.shape[0]
    blocks = [wd // PROJ_GRAD_BLOCK for wd in PROJ_PARTS]
    starts = [sum(blocks[:p]) for p in range(len(blocks))]

    def body(h_ref, *refs):
        part_refs, o_ref = refs[:-1], refs[-1]
        j = pl.program_id(0)
        for p_ref, start, count in zip(part_refs, starts, blocks):
            @pl.when((j >= start) & (j < start + count))
            def _(p_ref=p_ref):
                o_ref[...] = _dot_tn(p_ref[...], h_ref[...])

    def part_spec(start, count):
        return pl.BlockSpec((T, PROJ_GRAD_BLOCK), lambda j: (0, jnp.clip(j - start, 0, count - 1)))

    return pl.pallas_call(
        body, name="in_proj_weight_grad", grid=(sum(blocks),),
        in_specs=[_const_spec((T, D_MODEL))] + [part_spec(s, c) for s, c in zip(starts, blocks)],
        out_specs=pl.BlockSpec((PROJ_GRAD_BLOCK, D_MODEL), lambda j: (j, 0)),
        out_shape=_hbm_out(jax.ShapeDtypeStruct((IN_W, D_MODEL), F32)),
        compiler_params=_params(("arbitrary",), VMEM_MID),
    )(*_in_hbm(h, *dparts))


def _in_proj_input_grad(x, g1, w_in_t, dx1, dparts, tile, first_tile, n_tiles, name, carry=None):
    offsets = [sum(PROJ_PARTS[:p]) for p in range(len(PROJ_PARTS))]

    def body(x_ref, g_ref, w_ref, dx1_ref, *refs):
        part_refs, (gx_ref, dg_ref) = refs[:len(PROJ_PARTS)], refs[len(PROJ_PARTS):]
        i = pl.program_id(0)
        xv = x_ref[...]
        r = _rms_scale(xv)
        g = g_ref[...]
        dh = sum(_dot(p_ref[...], w_ref[off:off + wd, :]) for p_ref, off, wd in zip(part_refs, offsets, PROJ_PARTS))
        dxn, dg = _rms_bwd(dh, xv, r, g)
        gx_ref[...] = dx1_ref[...] + dxn

        @pl.when(i == 0)
        def _():
            dg_ref[...] = dg

        @pl.when(i > 0)
        def _():
            dg_ref[...] += dg

    tok = lambda wd: pl.BlockSpec((tile, wd), lambda i: (i + first_tile, 0))
    return _hosted_call(
        body, carry, _edge_1d(n_tiles), name=name, grid=(n_tiles,),
        in_specs=[tok(D_MODEL), _const_spec((1, D_MODEL)), _const_spec((IN_W, D_MODEL)), tok(D_MODEL)] + [tok(wd) for wd in PROJ_PARTS],
        out_specs=[pl.BlockSpec((tile, D_MODEL), lambda i: (i, 0)), pl.BlockSpec((1, D_MODEL), lambda i: (0, 0))],
        out_shape=[jax.ShapeDtypeStruct((n_tiles * tile, D_MODEL), F32), jax.ShapeDtypeStruct((1, D_MODEL), F32)],
        scratch_shapes=[], compiler_params=_params(("arbitrary",), VMEM_MID), inputs=(x, g1, w_in_t, dx1, *dparts))


def _bucket_table():
    qi = np.arange(BLOCK)[:, None]
    kj = np.arange(2 * BLOCK)[None, :]
    dist = qi + BLOCK - kj
    max_exact = N_BUCKETS // 2
    d = np.maximum(dist, 0)
    df = np.maximum(d, 1).astype(np.float32)
    large = max_exact + (np.log(df / np.float32(max_exact)) / np.float32(math.log(BLOCK / max_exact))
                         * np.float32(N_BUCKETS - max_exact)).astype(np.int32)
    large = np.minimum(large, N_BUCKETS - 1)
    bucket = np.where(d < max_exact, d, large)
    return np.where((dist >= 0) & (dist < BLOCK), bucket, -1).astype(np.int32)


def _build_bias(bucket_ref, rb_ref, bias_ref):
    bk = bucket_ref[...]
    for h in range(N_HEADS):
        def add(b, acc, h=h):
            return acc + jnp.where(bk == b, rb_ref[h, b], 0.0)
        bias_ref[h] = lax.fori_loop(0, N_BUCKETS, add, jnp.zeros((BLOCK, 2 * BLOCK), F32))


def _kv_variants(prev_ref, cur_ref):
    cat = jnp.concatenate([prev_ref[...], cur_ref[...]], axis=0)
    lo = lax.broadcasted_iota(jnp.int32, cat.shape, 1) < HEAD_DIM
    zero = jnp.zeros_like(cat)
    head0_lo = jnp.where(lo, cat, zero)
    head1_hi = jnp.where(lo, zero, cat)
    return ((head0_lo, pltpu.roll(head0_lo, HEAD_DIM, 1)), (pltpu.roll(head1_hi, HEAD_DIM, 1), head1_hi))


def _merge_kv_grads(g):
    lo = lax.broadcasted_iota(jnp.int32, g[0][0].shape, 1) < HEAD_DIM
    return jnp.where(lo, g[0][0] + pltpu.roll(g[0][1], HEAD_DIM, 1), g[1][1] + pltpu.roll(g[1][0], HEAD_DIM, 1))


def _head_lanes(h):
    return slice((h // 2) * LANES, (h // 2 + 1) * LANES)


def _attn_probs(q_ref, kvar, bias_ref, sk_ref, valid, s_ref):
    for h in range(N_HEADS):
        s_ref[h] = _dot_nt(q_ref[:, _head_lanes(h)], kvar[h // 4][h % 2])
    head = lax.broadcasted_iota(jnp.int32, (N_HEADS, 1, 1), 0)
    sink = jnp.zeros((N_HEADS, 1, 1), F32)
    for h in range(N_HEADS):
        sink = jnp.where(head == h, sk_ref[0, h], sink)
    s = jnp.where(valid[None], s_ref[...] * (HEAD_DIM ** -0.5) + bias_ref[...], NEG_INF)
    m = jnp.maximum(jnp.max(s, axis=-1, keepdims=True), sink)
    p = jnp.exp(s - m)
    e_sink = jnp.exp(sink - m)
    inv = 1.0 / (jnp.sum(p, axis=-1, keepdims=True) + e_sink)
    return p * inv, e_sink * inv


def _attn_valid(bucket_ref, n):
    col = lax.broadcasted_iota(jnp.int32, (BLOCK, 2 * BLOCK), 1)
    return (bucket_ref[...] >= 0) & ((n > 0) | (col >= BLOCK))


def _attn_fwd(q, k, v, bucket, rel_bias, sinks, carry=None):
    T = q.shape[0]
    nb = T // BLOCK

    def body(q_ref, kc_ref, kp_ref, vc_ref, vp_ref, bucket_ref, rb_ref, sk_ref, o_ref, bias_ref, s_ref, p_ref):
        n = pl.program_id(0)

        @pl.when(n == 0)
        def _():
            _build_bias(bucket_ref, rb_ref, bias_ref)

        kvar = _kv_variants(kp_ref, kc_ref)
        vvar = _kv_variants(vp_ref, vc_ref)
        pr, _ = _attn_probs(q_ref, kvar, bias_ref, sk_ref, _attn_valid(bucket_ref, n), s_ref)
        p_ref[...] = pr.astype(BF16)
        for m in range(N_HEADS // 2):
            acc = _dot(p_ref[2 * m], vvar[m // 2][0]) + _dot(p_ref[2 * m + 1], vvar[m // 2][1])
            o_ref[:, m * LANES:(m + 1) * LANES] = acc.astype(o_ref.dtype)

    cur = lambda w: pl.BlockSpec((BLOCK, w), lambda n: (n, 0))
    prev = lambda w: pl.BlockSpec((BLOCK, w), lambda n: (jnp.maximum(n - 1, 0), 0))
    smem = pl.BlockSpec(memory_space=pltpu.SMEM)
    return _hosted_call(
        body, carry, _edge_1d(nb), name="attn_fwd", grid=(nb,),
        in_specs=[cur(ATTN_W), cur(KV_W), prev(KV_W), cur(KV_W), prev(KV_W), _const_spec((BLOCK, 2 * BLOCK)), smem, smem],
        out_specs=[cur(ATTN_W)],
        out_shape=[jax.ShapeDtypeStruct((T, ATTN_W), BF16)],
        scratch_shapes=[pltpu.VMEM((N_HEADS, BLOCK, 2 * BLOCK), F32), pltpu.VMEM((N_HEADS, BLOCK, 2 * BLOCK), F32),
                        pltpu.VMEM((N_HEADS, BLOCK, 2 * BLOCK), BF16)],
        compiler_params=_params(("arbitrary",)), inputs=(q, k, k, v, v, bucket, rel_bias, sinks))


ATTN_SMALL_ROWS = N_BUCKETS + SUBLANES


def _attn_bwd(q, k, v, datt, bucket, rel_bias, sinks, carry=None):
    T = q.shape[0]
    nb = T // BLOCK

    def body(q_ref, do_ref, kc_ref, kp_ref, vc_ref, vp_ref, bucket_ref, rb_ref, sk_ref,
             dq_ref, dkv_ref, small_ref, bias_ref, ds_sum_ref, dsink_ref, kcarry_ref, vcarry_ref,
             s_ref, dp_ref, p_ref, dsc_ref):
        n = pl.program_id(0)

        @pl.when(n == 0)
        def _():
            _build_bias(bucket_ref, rb_ref, bias_ref)
            ds_sum_ref[...] = jnp.zeros_like(ds_sum_ref)
            dsink_ref[...] = jnp.zeros_like(dsink_ref)
            kcarry_ref[...] = jnp.zeros_like(kcarry_ref)
            vcarry_ref[...] = jnp.zeros_like(vcarry_ref)

        @pl.when(n < nb)
        def _():
            kvar = _kv_variants(kp_ref, kc_ref)
            vvar = _kv_variants(vp_ref, vc_ref)
            pr, p_sink = _attn_probs(q_ref, kvar, bias_ref, sk_ref, _attn_valid(bucket_ref, n), s_ref)
            for h in range(N_HEADS):
                dp_ref[h] = _dot_nt(do_ref[:, _head_lanes(h)], vvar[h // 4][h % 2])
            dp = dp_ref[...]
            dsum = jnp.sum(pr * dp, axis=-1, keepdims=True)
            ds = pr * (dp - dsum)
            ds_sum_ref[...] += ds
            dsink_ref[...] -= jnp.sum(p_sink * dsum, axis=1, keepdims=True)
            dsc_ref[...] = (ds * (HEAD_DIM ** -0.5)).astype(BF16)
            p_ref[...] = pr.astype(BF16)
            for m in range(N_HEADS // 2):
                dqm = _dot(dsc_ref[2 * m], kvar[m // 2][0]) + _dot(dsc_ref[2 * m + 1], kvar[m // 2][1])
                dq_ref[:, m * LANES:(m + 1) * LANES] = dqm.astype(dq_ref.dtype)
            dk_var = [[None, None], [None, None]]
            dv_var = [[None, None], [None, None]]
            for kvh in range(2):
                for e in range(2):
                    heads = [h for h in range(N_HEADS) if h // 4 == kvh and h % 2 == e]
                    dk_var[kvh][e] = sum(_dot_tn(dsc_ref[h], q_ref[:, _head_lanes(h)]) for h in heads)
                    dv_var[kvh][e] = sum(_dot_tn(p_ref[h], do_ref[:, _head_lanes(h)]) for h in heads)
            dk_cat = _merge_kv_grads(dk_var)
            dv_cat = _merge_kv_grads(dv_var)

            @pl.when(n > 0)
            def _():
                dkv_ref[:, :KV_W] = (kcarry_ref[...] + dk_cat[:BLOCK]).astype(BF16)
                dkv_ref[:, KV_W:] = (vcarry_ref[...] + dv_cat[:BLOCK]).astype(BF16)

            kcarry_ref[...] = dk_cat[BLOCK:]
            vcarry_ref[...] = dv_cat[BLOCK:]

        @pl.when(n == nb)
        def _():
            dkv_ref[:, :KV_W] = kcarry_ref[...].astype(BF16)
            dkv_ref[:, KV_W:] = vcarry_ref[...].astype(BF16)
            bk = bucket_ref[...]
            row = lax.broadcasted_iota(jnp.int32, (N_HEADS, ATTN_SMALL_ROWS, LANES), 1)

            def add(b, acc):
                masked = jnp.where((bk == b)[None], ds_sum_ref[...], 0.0)
                val = jnp.sum(jnp.sum(masked, axis=1, keepdims=True), axis=2, keepdims=True)
                return acc + jnp.where(row == b, val, 0.0)

            small_ref[...] = lax.fori_loop(0, N_BUCKETS, add, jnp.where(row == N_BUCKETS, dsink_ref[...], 0.0))

    last = nb - 1
    cur = lambda w: pl.BlockSpec((BLOCK, w), lambda n: (jnp.minimum(n, last), 0))
    prev = lambda w: pl.BlockSpec((BLOCK, w), lambda n: (jnp.clip(n - 1, 0, last), 0))
    smem = pl.BlockSpec(memory_space=pltpu.SMEM)
    return _hosted_call(
        body, carry, _edge_1d(nb + 1), name="attn_bwd", grid=(nb + 1,),
        in_specs=[cur(ATTN_W), cur(ATTN_W), cur(KV_W), prev(KV_W), cur(KV_W), prev(KV_W),
                  _const_spec((BLOCK, 2 * BLOCK)), smem, smem],
        out_specs=[cur(ATTN_W), prev(2 * KV_W), pl.BlockSpec((N_HEADS, ATTN_SMALL_ROWS, LANES), lambda n: (0, 0, 0))],
        out_shape=[jax.ShapeDtypeStruct((T, ATTN_W), BF16), jax.ShapeDtypeStruct((T, 2 * KV_W), BF16),
                   jax.ShapeDtypeStruct((N_HEADS, ATTN_SMALL_ROWS, LANES), F32)],
        scratch_shapes=[pltpu.VMEM((N_HEADS, BLOCK, 2 * BLOCK), F32), pltpu.VMEM((N_HEADS, BLOCK, 2 * BLOCK), F32),
                        pltpu.VMEM((N_HEADS, 1, 1), F32), pltpu.VMEM((BLOCK, KV_W), F32), pltpu.VMEM((BLOCK, KV_W), F32),
                        pltpu.VMEM((N_HEADS, BLOCK, 2 * BLOCK), F32), pltpu.VMEM((N_HEADS, BLOCK, 2 * BLOCK), F32),
                        pltpu.VMEM((N_HEADS, BLOCK, 2 * BLOCK), BF16), pltpu.VMEM((N_HEADS, BLOCK, 2 * BLOCK), BF16)],
        compiler_params=_params(("arbitrary",)), inputs=(q, datt, k, k, v, v, bucket, rel_bias, sinks))


SCAN_UNROLL = 4


def _cmul(ar, ai, br, bi):
    return ar * br - ai * bi, ar * bi + ai * br


def _cmul_conj(ar, ai, br, bi):
    return ar * br + ai * bi, ar * bi - ai * br


def _ssm_discretize(lr, li, ldt):
    dt = jnp.exp(ldt)
    mag = jnp.exp(lr * dt)
    ab_re = mag * jnp.cos(li * dt)
    ab_im = mag * jnp.sin(li * dt)
    nr = ab_re - 1.0
    den = lr * lr + li * li
    f_re = (nr * lr + ab_im * li) / den
    f_im = (ab_im * lr - nr * li) / den
    return ab_re, ab_im, f_re, f_im


def _ssm_prep(lam_re, lam_im, ldt_rep, bd_re, bd_im):
    def body(lr_ref, li_ref, ldt_ref, bdr_ref, bdi_ref, ar_ref, ai_ref, br_ref, bi_ref):
        ab_re, ab_im, f_re, f_im = _ssm_discretize(lr_ref[...], li_ref[...], ldt_ref[...])
        ar_ref[...] = ab_re
        ai_ref[...] = ab_im
        bdr, bdi = bdr_ref[0], bdi_ref[0]
        br_ref[0] = (bdr * f_re - bdi * f_im).astype(BF16)
        bi_ref[0] = (bdi * f_re + bdr * f_im).astype(BF16)

    row = pl.BlockSpec((1, SSM_LANE_BLOCK), lambda j: (0, j))
    mat = pl.BlockSpec((1, LANES, SSM_LANE_BLOCK), lambda j: (j, 0, 0))
    return pl.pallas_call(
        body, name="ssm_prep", grid=(N_SSM_BLOCKS,),
        in_specs=[row, row, row, mat, mat], out_specs=[row, row, mat, mat],
        out_shape=[jax.ShapeDtypeStruct((1, STATES), F32)] * 2 + [jax.ShapeDtypeStruct((N_SSM_BLOCKS, LANES, SSM_LANE_BLOCK), BF16)] * 2,
        compiler_params=_params(("arbitrary",)),
    )(*_in_hbm(lam_re, lam_im, ldt_rep, bd_re, bd_im))


def _ssm_prep_bwd(lam_re, lam_im, ldt_rep, bd_re, bd_im, dbr, dbi, da_re, da_im):
    def body(lr_ref, li_ref, ldt_ref, bdr_ref, bdi_ref, dbr_ref, dbi_ref, dar_ref, dai_ref,
             dbdr_ref, dbdi_ref, dlr_ref, dli_ref, dldt_ref):
        lr, li, ldt = lr_ref[...], li_ref[...], ldt_ref[...]
        (_, _, f_re, f_im), vjp = jax.vjp(_ssm_discretize, lr, li, ldt)
        bdr, bdi, gbr, gbi = bdr_ref[0], bdi_ref[0], dbr_ref[0], dbi_ref[0]
        dbdr_ref[0] = gbr * f_re + gbi * f_im
        dbdi_ref[0] = gbi * f_re - gbr * f_im
        df_re = jnp.sum(gbr * bdr + gbi * bdi, axis=0, keepdims=True)
        df_im = jnp.sum(gbi * bdr - gbr * bdi, axis=0, keepdims=True)
        dlr, dli, dldt = vjp((dar_ref[...], dai_ref[...], df_re, df_im))
        dlr_ref[...] = dlr
        dli_ref[...] = dli
        dldt_ref[...] = dldt

    row = pl.BlockSpec((1, SSM_LANE_BLOCK), lambda j: (0, j))
    mat = pl.BlockSpec((1, LANES, SSM_LANE_BLOCK), lambda j: (j, 0, 0))
    mat_shape = jax.ShapeDtypeStruct((N_SSM_BLOCKS, LANES, SSM_LANE_BLOCK), F32)
    row_shape = jax.ShapeDtypeStruct((1, STATES), F32)
    return pl.pallas_call(
        body, name="ssm_prep_bwd", grid=(N_SSM_BLOCKS,),
        in_specs=[row, row, row, mat, mat, mat, mat, row, row], out_specs=[mat, mat, row, row, row],
        out_shape=[mat_shape, mat_shape, row_shape, row_shape, row_shape],
        compiler_params=_params(("arbitrary",)),
    )(*_in_hbm(lam_re, lam_im, ldt_rep, bd_re, bd_im, dbr, dbi, da_re, da_im))


def _group_sum(x):
    def body(x_ref, o_ref):
        o_ref[...] = jnp.sum(x_ref[...], axis=1, keepdims=True)
    return pl.pallas_call(body, name="ssm_group_sum", grid=(1,), in_specs=[_whole(x.shape)], out_specs=_whole((N_GROUPS, 1)),
                          out_shape=jax.ShapeDtypeStruct((N_GROUPS, 1), F32))(*_in_hbm(x))


def _power_table(ar, ai, p_re_ref, p_im_ref, steps):
    shape = (SUBLANES, SSM_LANE_BLOCK)
    p_re_ref[0:SUBLANES] = jnp.broadcast_to(ar, shape)
    p_im_ref[0:SUBLANES] = jnp.broadcast_to(ai, shape)
    m = 1
    while m < steps:
        rows = m * SUBLANES
        top_re = p_re_ref[rows - SUBLANES:rows]
        top_im = p_im_ref[rows - SUBLANES:rows]
        cur_re = p_re_ref[0:rows].reshape(m, SUBLANES, SSM_LANE_BLOCK)
        cur_im = p_im_ref[0:rows].reshape(m, SUBLANES, SSM_LANE_BLOCK)
        nxt_re, nxt_im = _cmul(cur_re, cur_im, top_re[None], top_im[None])
        p_re_ref[rows:2 * rows] = nxt_re.reshape(rows, SSM_LANE_BLOCK)
        p_im_ref[rows:2 * rows] = nxt_im.reshape(rows, SSM_LANE_BLOCK)
        m *= 2


def _to_segments(src_ref, dst_ref, steps):
    for s in range(SUBLANES):
        dst_ref[pl.ds(s, steps, stride=SUBLANES), :] = src_ref[s * steps:(s + 1) * steps, :]


def _from_segments(src_ref, dst_ref, steps):
    for s in range(SUBLANES):
        dst_ref[s * steps:(s + 1) * steps, :] = src_ref[pl.ds(s, steps, stride=SUBLANES), :]


def _segment_carries(e_re, e_im, an_re, an_im, c_re, c_im, reverse):
    order = range(SUBLANES - 1, -1, -1) if reverse else range(SUBLANES)
    ins_re, ins_im = [None] * SUBLANES, [None] * SUBLANES
    for s in order:
        ins_re[s], ins_im[s] = c_re, c_im
        pr, pi = _cmul(an_re, an_im, c_re, c_im)
        c_re = e_re[s:s + 1] + pr
        c_im = e_im[s:s + 1] + pi
    return jnp.concatenate(ins_re, axis=0), jnp.concatenate(ins_im, axis=0), c_re, c_im


def _ssm_fwd(u, a_re, a_im, b_re, b_im, c_re, c_im, d_skip, chunk, carry=None):
    T = u.shape[0]
    nc = T // chunk
    steps = chunk // SUBLANES
    blk = SSM_LANE_BLOCK

    def body(u_ref, ar_ref, ai_ref, br_ref, bi_ref, cr_ref, ci_ref, dk_ref,
             y_ref, hr_ref, hi_ref, inr_ref, ini_ref, useg_ref, yseg_ref, pr_ref, pi_ref, carry_ref):
        c = pl.program_id(1)
        ar, ai = ar_ref[...], ai_ref[...]

        @pl.when(c == 0)
        def _():
            _power_table(ar, ai, pr_ref, pi_ref, steps)
            carry_ref[...] = jnp.zeros_like(carry_ref)

        _to_segments(u_ref, useg_ref, steps)
        ub = useg_ref[...].astype(BF16)
        hr_ref[...] = _dot(ub, br_ref[0])
        hi_ref[...] = _dot(ub, bi_ref[0])
        first = slice(0, SUBLANES)

        def scan(t4, prev):
            for j in range(SCAN_UNROLL):
                rows = pl.ds(pl.multiple_of((t4 * SCAN_UNROLL + j) * SUBLANES, SUBLANES), SUBLANES)
                pr, pi = _cmul(pr_ref[first, :], pi_ref[first, :], prev[0], prev[1])
                prev = (pr + hr_ref[rows, :], pi + hi_ref[rows, :])
                hr_ref[rows, :] = prev[0]
                hi_ref[rows, :] = prev[1]
            return prev

        zero = jnp.zeros((SUBLANES, blk), F32)
        lax.fori_loop(0, steps // SCAN_UNROLL, scan, (zero, zero))

        top = slice(chunk - SUBLANES, chunk)
        in_re, in_im, out_re, out_im = _segment_carries(
            hr_ref[top, :], hi_ref[top, :], pr_ref[top, :][0:1], pi_ref[top, :][0:1],
            carry_ref[0:1, :], carry_ref[1:2, :], reverse=False)
        carry_ref[0:1, :] = out_re
        carry_ref[1:2, :] = out_im
        inr_ref[...] = in_re
        ini_ref[...] = in_im

        def fix(t4, _):
            for j in range(SCAN_UNROLL):
                rows = pl.ds(pl.multiple_of((t4 * SCAN_UNROLL + j) * SUBLANES, SUBLANES), SUBLANES)
                fr, fi = _cmul(pr_ref[rows, :], pi_ref[rows, :], in_re, in_im)
                hr_ref[rows, :] += fr
                hi_ref[rows, :] += fi
            return 0

        lax.fori_loop(0, steps // SCAN_UNROLL, fix, 0)

        yseg_ref[...] = _dot(hr_ref[...].astype(BF16), cr_ref[0]) - _dot(hi_ref[...].astype(BF16), ci_ref[0])
        _from_segments(yseg_ref, y_ref, steps)
        y_ref[...] += dk_ref[...] * u_ref[...]

    row = pl.BlockSpec((1, blk), lambda j, c: (0, j))
    b_mat = pl.BlockSpec((1, LANES, blk), lambda j, c: (j, 0, 0))
    c_mat = pl.BlockSpec((1, blk, LANES), lambda j, c: (j, 0, 0))
    tok = pl.BlockSpec((chunk, LANES), lambda j, c: (c, j))
    state = pl.BlockSpec((chunk, blk), lambda j, c: (c, j))
    enter = pl.BlockSpec((SUBLANES, blk), lambda j, c: (c, j))
    return _hosted_call(
        body, carry, _edge_2d(N_SSM_BLOCKS, nc), name="ssm_fwd", grid=(N_SSM_BLOCKS, nc),
        in_specs=[tok, row, row, b_mat, b_mat, c_mat, c_mat, pl.BlockSpec((1, LANES), lambda j, c: (0, j))],
        out_specs=[tok, state, state, enter, enter],
        out_shape=[jax.ShapeDtypeStruct((T, SSM_W), F32), jax.ShapeDtypeStruct((T, STATES), F32),
                   jax.ShapeDtypeStruct((T, STATES), F32), jax.ShapeDtypeStruct((nc * SUBLANES, STATES), F32),
                   jax.ShapeDtypeStruct((nc * SUBLANES, STATES), F32)],
        scratch_shapes=[pltpu.VMEM((chunk, LANES), F32), pltpu.VMEM((chunk, LANES), F32),
                        pltpu.VMEM((chunk, blk), F32), pltpu.VMEM((chunk, blk), F32), pltpu.VMEM((SUBLANES, blk), F32)],
        compiler_params=_params(("arbitrary", "arbitrary"), VMEM_MID),
        inputs=(u, a_re, a_im, b_re, b_im, c_re, c_im, d_skip))


def _ssm_bwd(dy, u, h_re, h_im, in_re, in_im, a_re, a_im, b_re, b_im, c_re, c_im, d_skip, chunk, carry=None):
    T = u.shape[0]
    nc = T // chunk
    steps = chunk // SUBLANES
    blk = SSM_LANE_BLOCK

    def body(dy_ref, u_ref, hr_ref, hi_ref, inr_ref, ini_ref, ar_ref, ai_ref, br_ref, bi_ref, cr_ref, ci_ref, dk_ref,
             du_ref, dbr_ref, dbi_ref, dcr_ref, dci_ref, dar_ref, dai_ref, ddk_ref,
             dyseg_ref, useg_ref, duseg_ref, gr_ref, gi_ref, pr_ref, pi_ref, carry_ref, accr_ref, acci_ref):
        c = pl.program_id(1)
        ar, ai = ar_ref[...], ai_ref[...]

        @pl.when(c == 0)
        def _():
            _power_table(ar, ai, pr_ref, pi_ref, steps)
            carry_ref[...] = jnp.zeros_like(carry_ref)
            accr_ref[...] = jnp.zeros_like(accr_ref)
            acci_ref[...] = jnp.zeros_like(acci_ref)

        _to_segments(dy_ref, dyseg_ref, steps)
        _to_segments(u_ref, useg_ref, steps)
        dyb = dyseg_ref[...].astype(BF16)
        ub = useg_ref[...].astype(BF16)
        gr_ref[...] = _dot_nt(dyb, cr_ref[0])
        gi_ref[...] = -_dot_nt(dyb, ci_ref[0])
        dcr = _dot_tn(hr_ref[...].astype(BF16), dyb)
        dci = -_dot_tn(hi_ref[...].astype(BF16), dyb)
        ddk = jnp.sum(dy_ref[...] * u_ref[...], axis=0, keepdims=True)

        first = slice(0, SUBLANES)

        def scan(k4, nxt):
            for j in range(SCAN_UNROLL):
                t = steps - 1 - (k4 * SCAN_UNROLL + j)
                rows = pl.ds(pl.multiple_of(t * SUBLANES, SUBLANES), SUBLANES)
                pr, pi = _cmul_conj(pr_ref[first, :], pi_ref[first, :], nxt[0], nxt[1])
                nxt = (pr + gr_ref[rows, :], pi + gi_ref[rows, :])
                gr_ref[rows, :] = nxt[0]
                gi_ref[rows, :] = nxt[1]
            return nxt

        top = slice(chunk - SUBLANES, chunk)
        zero = jnp.zeros((SUBLANES, blk), F32)
        lax.fori_loop(0, steps // SCAN_UNROLL, scan, (zero, zero))

        gin_re, gin_im, out_re, out_im = _segment_carries(
            gr_ref[0:SUBLANES, :], gi_ref[0:SUBLANES, :], pr_ref[top, :][0:1], -pi_ref[top, :][0:1],
            carry_ref[0:1, :], carry_ref[1:2, :], reverse=True)
        carry_ref[0:1, :] = out_re
        carry_ref[1:2, :] = out_im

        def fix_row(rows, prow, hp_re, hp_im, acc):
            fr, fi = _cmul_conj(pr_ref[prow, :], pi_ref[prow, :], gin_re, gin_im)
            g_re = gr_ref[rows, :] + fr
            g_im = gi_ref[rows, :] + fi
            gr_ref[rows, :] = g_re
            gi_ref[rows, :] = g_im
            return acc[0] + g_re * hp_re + g_im * hp_im, acc[1] + g_im * hp_re - g_re * hp_im

        def fix_at(t, acc):
            aligned = (lambda r: r * SUBLANES) if isinstance(t, int) else (lambda r: pl.multiple_of(r * SUBLANES, SUBLANES))
            rows, before, prow = (pl.ds(aligned(r), SUBLANES) for r in (t, t - 1, steps - 1 - t))
            return fix_row(rows, prow, hr_ref[before, :], hi_ref[before, :], acc)

        def fix(t4, acc):
            for j in range(SCAN_UNROLL):
                acc = fix_at(t4 * SCAN_UNROLL + j, acc)
            return acc

        acc = fix_row(first, top, inr_ref[...], ini_ref[...], (accr_ref[...], acci_ref[...]))
        for t in range(1, SCAN_UNROLL):
            acc = fix_at(t, acc)
        acc_re, acc_im = lax.fori_loop(1, steps // SCAN_UNROLL, fix, acc)
        accr_ref[...] = acc_re
        acci_ref[...] = acc_im

        gbr = gr_ref[...].astype(BF16)
        gbi = gi_ref[...].astype(BF16)
        duseg_ref[...] = _dot_nt(gbr, br_ref[0]) + _dot_nt(gbi, bi_ref[0])
        _from_segments(duseg_ref, dyseg_ref, steps)
        du_ref[...] = (dyseg_ref[...] + dk_ref[...] * dy_ref[...]).astype(BF16)
        dbr = _dot_tn(ub, gbr)
        dbi = _dot_tn(ub, gbi)

        @pl.when(c == 0)
        def _():
            dbr_ref[0] = dbr
            dbi_ref[0] = dbi
            dcr_ref[0] = dcr
            dci_ref[0] = dci
            ddk_ref[...] = ddk

        @pl.when(c > 0)
        def _():
            dbr_ref[0] += dbr
            dbi_ref[0] += dbi
            dcr_ref[0] += dcr
            dci_ref[0] += dci
            ddk_ref[...] += ddk

        @pl.when(c == nc - 1)
        def _():
            dar_ref[...] = jnp.sum(acc_re, axis=0, keepdims=True)
            dai_ref[...] = jnp.sum(acc_im, axis=0, keepdims=True)

    rev = lambda c: nc - 1 - c
    row = pl.BlockSpec((1, blk), lambda j, c: (0, j))
    b_mat = pl.BlockSpec((1, LANES, blk), lambda j, c: (j, 0, 0))
    c_mat = pl.BlockSpec((1, blk, LANES), lambda j, c: (j, 0, 0))
    tok = pl.BlockSpec((chunk, LANES), lambda j, c: (rev(c), j))
    state = pl.BlockSpec((chunk, blk), lambda j, c: (rev(c), j))
    enter = pl.BlockSpec((SUBLANES, blk), lambda j, c: (rev(c), j))
    chan = pl.BlockSpec((1, LANES), lambda j, c: (0, j))
    f32 = lambda *s: jax.ShapeDtypeStruct(s, F32)
    return _hosted_call(
        body, carry, _edge_2d(N_SSM_BLOCKS, nc), name="ssm_bwd", grid=(N_SSM_BLOCKS, nc),
        in_specs=[tok, tok, state, state, enter, enter, row, row, b_mat, b_mat, c_mat, c_mat, chan],
        out_specs=[tok, b_mat, b_mat, c_mat, c_mat, row, row, chan],
        out_shape=[jax.ShapeDtypeStruct((T, SSM_W), BF16), f32(N_SSM_BLOCKS, LANES, blk), f32(N_SSM_BLOCKS, LANES, blk),
                   f32(N_SSM_BLOCKS, blk, LANES), f32(N_SSM_BLOCKS, blk, LANES), f32(1, STATES), f32(1, STATES), f32(1, SSM_W)],
        scratch_shapes=[pltpu.VMEM((chunk, LANES), F32), pltpu.VMEM((chunk, LANES), F32), pltpu.VMEM((chunk, LANES), F32),
                        pltpu.VMEM((chunk, blk), F32), pltpu.VMEM((chunk, blk), F32),
                        pltpu.VMEM((chunk, blk), F32), pltpu.VMEM((chunk, blk), F32),
                        pltpu.VMEM((SUBLANES, blk), F32), pltpu.VMEM((SUBLANES, blk), F32), pltpu.VMEM((SUBLANES, blk), F32)],
        compiler_params=_params(("arbitrary", "arbitrary"), VMEM_BIG),
        inputs=(dy, u, h_re, h_im, in_re, in_im, a_re, a_im, b_re, b_im, c_re, c_im, d_skip))


def _merge_forward(y, att, ga, gs, w_glu, w_ssm, w_attn):
    z = jax.nn.gelu(y)
    zb = z.astype(BF16)
    gl = jax.nn.sigmoid(_dot(zb, w_glu))
    z2b = (z * gl).astype(BF16)
    y_ssm = _dot(z2b, w_ssm)
    y_attn = _dot(att, w_attn)
    sa = jax.nn.sigmoid(ga)
    ss = jax.nn.sigmoid(gs)
    merged = (sa * y_attn + ss * y_ssm).astype(BF16)
    return z, zb, gl, z2b, y_ssm, y_attn, sa, ss, merged


def _merge_fwd(x, y, att, ga, gs, g2, g3, w_glu, w_ssm, w_attn, w_out, tile):
    T = x.shape[0]

    def body(x_ref, y_ref, att_ref, ga_ref, gs_ref, g2_ref, g3_ref, wg_ref, ws_ref, wa_ref, wo_ref, x1_ref, o_ref, h2_ref):
        merged = _merge_forward(y_ref[...], att_ref[...], ga_ref[...], gs_ref[...], wg_ref[...], ws_ref[...], wa_ref[...])[-1]
        o = _dot(merged, wo_ref[...])
        x1 = x_ref[...] + o * _rms_scale(o) * g2_ref[...]
        o_ref[...] = o
        x1_ref[...] = x1
        h2_ref[...] = (x1 * _rms_scale(x1) * g3_ref[...]).astype(BF16)

    tok = lambda w: pl.BlockSpec((tile, w), lambda i: (i, 0))
    vec = _const_spec((1, D_MODEL))
    return pl.pallas_call(
        body, name="merge_fwd", grid=(T // tile,),
        in_specs=[tok(D_MODEL), tok(SSM_W), tok(ATTN_W), tok(D_MODEL), tok(D_MODEL), vec, vec,
                  _const_spec((SSM_W, SSM_W)), _const_spec((SSM_W, D_MODEL)), _const_spec((ATTN_W, D_MODEL)),
                  _const_spec((D_MODEL, D_MODEL))],
        out_specs=[tok(D_MODEL), tok(D_MODEL), tok(D_MODEL)],
        out_shape=_hbm_out([jax.ShapeDtypeStruct((T, D_MODEL), F32), jax.ShapeDtypeStruct((T, D_MODEL), F32),
                            jax.ShapeDtypeStruct((T, D_MODEL), BF16)]),
        compiler_params=_params(("arbitrary",), VMEM_MID),
    )(*_in_hbm(x, y, att, ga, gs, g2, g3, w_glu, w_ssm, w_attn, w_out))


def _merge_bwd(dh2, dx2, x1, o, y, att, ga, gs, g2, g3, w_glu, w_ssm, w_attn, w_out, tile, carry=None):
    T = x1.shape[0]
    n_steps = T // tile

    group = min(2, n_steps)
    staged_widths = (D_MODEL, D_MODEL, ATTN_W, D_MODEL, SSM_W, D_MODEL, SSM_W, SSM_W)

    def body(dh2_ref, dx2_ref, x1_ref, o_ref, y_ref, att_ref, ga_ref, gs_ref, g2_ref, g3_ref, wg_ref, ws_ref, wa_ref, wo_ref,
             dx1_ref, dgates_ref, datt_ref, dy_ref, dwg_hbm, dws_hbm, dwa_hbm, dwo_hbm, dg2_ref, dg3_ref,
             awg_ref, aws_ref, awa_ref, awo_ref, *staged):
        i = pl.program_id(0)
        x1v, ov = x1_ref[...], o_ref[...]
        dxn, dg3 = _rms_bwd(dh2_ref[...], x1v, _rms_scale(x1v), g3_ref[...])
        dx1 = dx2_ref[...] + dxn
        dx1_ref[...] = dx1
        do, dg2 = _rms_bwd(dx1, ov, _rms_scale(ov), g2_ref[...])
        dob = do.astype(BF16)

        yv = y_ref[...]
        att = att_ref[...]
        z, zb, gl, z2b, y_ssm, y_attn, sa, ss, merged = _merge_forward(
            yv, att, ga_ref[...], gs_ref[...], wg_ref[...], ws_ref[...], wa_ref[...])
        dmerged = _dot_nt(dob, wo_ref[...])
        dya = (dmerged * sa).astype(BF16)
        dys = (dmerged * ss).astype(BF16)
        dgates_ref[:, :D_MODEL] = (dmerged * y_attn * sa * (1.0 - sa)).astype(BF16)
        dgates_ref[:, D_MODEL:] = (dmerged * y_ssm * ss * (1.0 - ss)).astype(BF16)
        datt_ref[...] = _dot_nt(dya, wa_ref[...]).astype(BF16)
        dz2 = _dot_nt(dys, ws_ref[...])
        dpre = (dz2 * z * gl * (1.0 - gl)).astype(BF16)
        dz = dz2 * gl + _dot_nt(dpre, wg_ref[...])
        _, gelu_vjp = jax.vjp(jax.nn.gelu, yv)
        dy_ref[...] = gelu_vjp(dz)[0]

        part = pl.ds(pl.multiple_of((i % group) * tile, tile), tile)
        for ref, val in zip(staged, (merged, dob, att, dya, z2b, dys, zb, dpre)):
            ref[part, :] = val

        @pl.when(i == 0)
        def _():
            dg2_ref[...] = dg2
            dg3_ref[...] = dg3

        @pl.when(i > 0)
        def _():
            dg2_ref[...] += dg2
            dg3_ref[...] += dg3

        def weight_grads():
            s_merged, s_dob, s_att, s_dya, s_z2b, s_dys, s_zb, s_dpre = (ref[...] for ref in staged)
            return ((awo_ref, _dot_tn(s_merged, s_dob)), (awa_ref, _dot_tn(s_att, s_dya)),
                    (aws_ref, _dot_tn(s_z2b, s_dys)), (awg_ref, _dot_tn(s_zb, s_dpre)))

        @pl.when(i == group - 1)
        def _():
            for ref, val in weight_grads():
                ref[...] = val

        @pl.when((i % group == group - 1) & (i > group - 1))
        def _():
            for ref, val in weight_grads():
                ref[...] += val

        @pl.when(i == n_steps - 1)
        def _():
            pltpu.sync_copy(awg_ref, dwg_hbm)
            pltpu.sync_copy(aws_ref, dws_hbm)
            pltpu.sync_copy(awa_ref, dwa_hbm)
            pltpu.sync_copy(awo_ref, dwo_hbm)

    tok = lambda w: pl.BlockSpec((tile, w), lambda i: (i, 0))
    vec = _const_spec((1, D_MODEL))
    any_ = pl.BlockSpec(memory_space=pl.ANY)
    vec_out = pl.BlockSpec((1, D_MODEL), lambda i: (0, 0))
    f32 = lambda *s: jax.ShapeDtypeStruct(s, F32)
    bf = lambda *s: jax.ShapeDtypeStruct(s, BF16)
    return _hosted_call(
        body, carry, _edge_1d(n_steps), name="merge_bwd", grid=(n_steps,),
        in_specs=[tok(D_MODEL), tok(D_MODEL), tok(D_MODEL), tok(D_MODEL), tok(SSM_W), tok(ATTN_W), tok(D_MODEL), tok(D_MODEL),
                  vec, vec, _const_spec((SSM_W, SSM_W)), _const_spec((SSM_W, D_MODEL)), _const_spec((ATTN_W, D_MODEL)),
                  _const_spec((D_MODEL, D_MODEL))],
        out_specs=[tok(D_MODEL), tok(2 * D_MODEL), tok(ATTN_W), tok(SSM_W), any_, any_, any_, any_, vec_out, vec_out],
        out_shape=[f32(T, D_MODEL), bf(T, 2 * D_MODEL), bf(T, ATTN_W), f32(T, SSM_W),
                   f32(SSM_W, SSM_W), f32(SSM_W, D_MODEL), f32(ATTN_W, D_MODEL), f32(D_MODEL, D_MODEL),
                   f32(1, D_MODEL), f32(1, D_MODEL)],
        scratch_shapes=[pltpu.VMEM((SSM_W, SSM_W), F32), pltpu.VMEM((SSM_W, D_MODEL), F32),
                        pltpu.VMEM((ATTN_W, D_MODEL), F32), pltpu.VMEM((D_MODEL, D_MODEL), F32)]
        + [pltpu.VMEM((group * tile, wd), BF16) for wd in staged_widths],
        compiler_params=_params(("arbitrary",), VMEM_BIG),
        inputs=(dh2, dx2, x1, o, y, att, ga, gs, g2, g3, w_glu, w_ssm, w_attn, w_out))


FF_SHARD = D_FF // N_DEV


def _mlp_fwd(h2, x1, target, g4, w_ff_in, w_ff_out, tile):
    T = h2.shape[0]
    col_chunk = 2 * FF_SHARD

    def body(h2_ref, x1_ref, tg_ref, g4_ref, wi_ref, wo_ref, a_ref, dfo_ref, dx2_ref, loss_ref, dg4_ref, rr_ref):
        i = pl.program_id(0)
        h2v = h2_ref[...]
        for c in range(D_FF // col_chunk):
            cols = slice(c * col_chunk, (c + 1) * col_chunk)
            a = _dot_nt(h2v, wi_ref[cols, :])
            a_ref[:, cols] = a.astype(BF16)
            ra = jnp.maximum(a, 0.0)
            rr_ref[:, cols] = (ra * ra).astype(BF16)
        f = _dot(rr_ref[...], wo_ref[...])
        r = _rms_scale(f)
        g = g4_ref[...]
        err = x1_ref[...] + f * r * g - tg_ref[...]
        dx2 = err * (1.0 / D_MODEL)
        dx2_ref[...] = dx2
        dfo, dg = _rms_bwd(dx2, f, r, g)
        dfo_ref[...] = dfo.astype(BF16)
        row = lax.broadcasted_iota(jnp.int32, (SUBLANES, LANES), 0)
        col = lax.broadcasted_iota(jnp.int32, (SUBLANES, LANES), 1)
        loss = jnp.where((row == 0) & (col == 0), (0.5 / D_MODEL) * jnp.sum(err * err), 0.0)

        @pl.when(i == 0)
        def _():
            loss_ref[...] = loss
            dg4_ref[...] = dg

        @pl.when(i > 0)
        def _():
            loss_ref[...] += loss
            dg4_ref[...] += dg

    tok = pl.BlockSpec((tile, D_MODEL), lambda i: (i, 0))
    return pl.pallas_call(
        body, name="mlp_fwd", grid=(T // tile,),
        in_specs=[tok, tok, tok, _const_spec((1, D_MODEL)), _const_spec((D_FF, D_MODEL)), _const_spec((D_FF, D_MODEL))],
        out_specs=[pl.BlockSpec((tile, D_FF), lambda i: (i, 0)), tok, tok,
                   pl.BlockSpec((SUBLANES, LANES), lambda i: (0, 0)), pl.BlockSpec((1, D_MODEL), lambda i: (0, 0))],
        out_shape=_hbm_out([jax.ShapeDtypeStruct((T, D_FF), BF16), jax.ShapeDtypeStruct((T, D_MODEL), BF16),
                            jax.ShapeDtypeStruct((T, D_MODEL), F32), jax.ShapeDtypeStruct((SUBLANES, LANES), F32),
                            jax.ShapeDtypeStruct((1, D_MODEL), F32)]),
        scratch_shapes=[pltpu.VMEM((tile, D_FF), BF16)],
        compiler_params=_params(("arbitrary",), VMEM_MAX),
    )(*_in_hbm(h2, x1, target, g4, w_ff_in.reshape(D_FF, D_MODEL), w_ff_out.reshape(D_FF, D_MODEL)))


def _mlp_weight_grads(dfo, a, h2, w_ff_out, row_chunk):
    T = h2.shape[0]

    def body(dfo_ref, h2_ref, a_ref, wo_ref, dwi_ref, dwo_ref, da_ref, rr_ref):
        def rows(r, _):
            sl = pl.ds(pl.multiple_of(r * row_chunk, row_chunk), row_chunk)
            ra = jnp.maximum(a_ref[sl, :].astype(F32), 0.0)
            da_ref[sl, :] = (_dot_nt(dfo_ref[sl, :], wo_ref[0]) * (2.0 * ra)).astype(BF16)
            rr_ref[sl, :] = (ra * ra).astype(BF16)
            return 0

        lax.fori_loop(0, T // row_chunk, rows, 0)
        dwo_ref[0] = _dot_tn(rr_ref[...], dfo_ref[...])
        dwi_ref[0] = _dot_tn(h2_ref[...], da_ref[...])

    return pl.pallas_call(
        body, name="mlp_weight_grads", grid=(N_DEV,),
        in_specs=[_const_spec((T, D_MODEL)), _const_spec((T, D_MODEL)), pl.BlockSpec((T, FF_SHARD), lambda k: (0, k)),
                  pl.BlockSpec((1, FF_SHARD, D_MODEL), lambda k: (k, 0, 0))],
        out_specs=[pl.BlockSpec((1, D_MODEL, FF_SHARD), lambda k: (k, 0, 0)),
                   pl.BlockSpec((1, FF_SHARD, D_MODEL), lambda k: (k, 0, 0)), pl.BlockSpec((T, FF_SHARD), lambda k: (0, k))],
        out_shape=_hbm_out([jax.ShapeDtypeStruct((N_DEV, D_MODEL, FF_SHARD), F32),
                            jax.ShapeDtypeStruct((N_DEV, FF_SHARD, D_MODEL), F32), jax.ShapeDtypeStruct((T, D_FF), BF16)]),
        scratch_shapes=[pltpu.VMEM((T, FF_SHARD), BF16)],
        compiler_params=_params(("arbitrary",), VMEM_MAX),
    )(*_in_hbm(dfo, h2, a, w_ff_out))


def _mlp_input_grad(da, w_ff_in_t, tile):
    T = da.shape[0]

    def body(da_ref, w_ref, o_ref):
        o_ref[...] = _dot(da_ref[...], w_ref[...])

    return pl.pallas_call(
        body, name="mlp_input_grad", grid=(T // tile,),
        in_specs=[pl.BlockSpec((tile, D_FF), lambda i: (i, 0)), _const_spec((D_FF, D_MODEL))],
        out_specs=pl.BlockSpec((tile, D_MODEL), lambda i: (i, 0)),
        out_shape=_hbm_out(jax.ShapeDtypeStruct((T, D_MODEL), F32)),
        compiler_params=_params(("arbitrary",), VMEM_MID),
    )(*_in_hbm(da, w_ff_in_t))


def _block_diag_in(b):
    bt = b.reshape(N_SSM_BLOCKS, GROUPS_PER_BLOCK, GROUP_CH, N_STATE)
    eye = jnp.eye(GROUPS_PER_BLOCK, dtype=b.dtype)
    return jnp.einsum("jacp,ab->jacbp", bt, eye).reshape(N_SSM_BLOCKS, LANES, SSM_LANE_BLOCK)


def _block_diag_in_grad(g):
    g = g.reshape(N_SSM_BLOCKS, GROUPS_PER_BLOCK, GROUP_CH, GROUPS_PER_BLOCK, N_STATE)
    d = jnp.diagonal(g, axis1=1, axis2=3)
    return jnp.transpose(d, (0, 3, 1, 2)).reshape(N_GROUPS, GROUP_CH, N_STATE)


def _block_diag_out(c):
    ct = c.reshape(N_SSM_BLOCKS, GROUPS_PER_BLOCK, GROUP_CH, N_STATE)
    eye = jnp.eye(GROUPS_PER_BLOCK, dtype=c.dtype)
    return jnp.einsum("jacp,ab->japbc", ct, eye).reshape(N_SSM_BLOCKS, SSM_LANE_BLOCK, LANES)


def _block_diag_out_grad(g):
    g = g.reshape(N_SSM_BLOCKS, GROUPS_PER_BLOCK, N_STATE, GROUPS_PER_BLOCK, GROUP_CH)
    d = jnp.diagonal(g, axis1=1, axis2=3)
    return jnp.transpose(d, (0, 3, 2, 1)).reshape(N_GROUPS, GROUP_CH, N_STATE)


def _tiles(T):
    return dict(proj=min(512, T), proj_bwd=min(512, T // 2), merge=min(512, T), merge_bwd=min(256, T),
                mlp_fwd=min(512, T), mlp_bwd=min(512, T), ssm_chunk=min(1024, T))


def _mesh_position():
    x, y, c = lax.axis_index("x"), lax.axis_index("y"), lax.axis_index("c")
    other_chips = [(1 - x, y), (x, 1 - y), (1 - x, 1 - y)]
    return x, y, c, other_chips


def _gather_carry(arrays):
    n = len(arrays)

    def copies(ins, outs, sems):
        send_sems, recv_sems, local_sems = sems
        x, y, c, chips = _mesh_position()
        me, sibling = (x, y, c), (x, y, 1 - c)

        def copy(a, k, block, to, src=None):
            px, py, pc = block
            dst = outs[a].at[4 * px + 2 * py + pc]
            return pltpu.make_async_remote_copy(
                src_ref=dst if src is None else src, dst_ref=dst, send_sem=send_sems.at[7 * a + k],
                recv_sem=recv_sems.at[7 * a + k], device_id=to, device_id_type=MESH_IDS)

        mine = [pltpu.make_async_copy(ins[a], outs[a].at[4 * x + 2 * y + c], local_sems.at[a]) for a in range(n)]
        first = []
        for a in range(n):
            first.append(copy(a, 0, me, sibling, src=ins[a]))
            first += [copy(a, 1 + j, me, (*chip, c), src=ins[a]) for j, chip in enumerate(chips)]
        return copy, mine, first, me, sibling, chips, c

    def start(ins, outs, sems):
        _, mine, first, *_ = copies(ins, outs, sems)
        for cp in mine + first:
            cp.start()

    def finish(ins, outs, sems):
        copy, mine, first, me, sibling, chips, c = copies(ins, outs, sems)
        passed = []
        for a in range(n):
            for j, chip in enumerate(chips):
                copy(a, 1 + j, (*chip, c), me).wait_recv()
                passed.append(copy(a, 4 + j, (*chip, c), sibling))
                passed[-1].start()
        for a in range(n):
            copy(a, 0, sibling, me).wait_recv()
            for j, chip in enumerate(chips):
                copy(a, 4 + j, (*chip, 1 - c), me).wait_recv()
        for cp in first + passed:
            cp.wait_send()
        for cp in mine:
            cp.wait()

    return _Carry(arrays, [jax.ShapeDtypeStruct((N_DEV,) + a.shape, a.dtype) for a in arrays],
                  [pltpu.SemaphoreType.DMA((7 * n,)), pltpu.SemaphoreType.DMA((7 * n,)), pltpu.SemaphoreType.DMA((n,))],
                  start, finish)


def _pairwise_carry(arrays, n_slots, make_copies):
    n = len(arrays)

    def start(ins, outs, sems):
        for cp in make_copies(ins, outs, sems):
            cp.start()

    def finish(ins, outs, sems):
        for cp in make_copies(ins, outs, sems):
            cp.wait()

    return _Carry(arrays, [jax.ShapeDtypeStruct((n_slots,) + a.shape[1:], a.dtype) for a in arrays],
                  [pltpu.SemaphoreType.DMA((n_slots * n,)), pltpu.SemaphoreType.DMA((n_slots * n,))], start, finish)


def _sibling_carry(grads):
    def make_copies(ins, outs, sems):
        x, y, c, _ = _mesh_position()
        return [pltpu.make_async_remote_copy(
            src_ref=ins[a].at[2 * ch + (1 - c)], dst_ref=outs[a].at[ch], send_sem=sems[0].at[4 * a + ch],
            recv_sem=sems[1].at[4 * a + ch], device_id=(x, y, 1 - c), device_id_type=MESH_IDS)
            for a in range(len(grads)) for ch in range(4)]

    return _pairwise_carry(grads, 4, make_copies)


def _chips_carry(sums):
    def make_copies(ins, outs, sems):
        x, y, c, chips = _mesh_position()
        return [pltpu.make_async_remote_copy(
            src_ref=ins[a].at[2 * px + py], dst_ref=outs[a].at[j], send_sem=sems[0].at[3 * a + j],
            recv_sem=sems[1].at[3 * a + j], device_id=(px, py, c), device_id_type=MESH_IDS)
            for a in range(len(sums)) for j, (px, py) in enumerate(chips)]

    return _pairwise_carry(sums, 3, make_copies)


def _everyone_carry(arrays):
    def make_copies(ins, outs, sems):
        x, y, c, _ = _mesh_position()
        flip = lambda v, bit: 1 - v if bit else v
        return [pltpu.make_async_remote_copy(
            src_ref=ins[a], dst_ref=outs[a].at[r - 1], send_sem=sems[0].at[7 * a + r - 1], recv_sem=sems[1].at[7 * a + r - 1],
            device_id=(flip(x, r & 4), flip(y, r & 2), flip(c, r & 1)), device_id_type=MESH_IDS)
            for a in range(len(arrays)) for r in range(1, N_DEV)]

    carry = _pairwise_carry([jax.ShapeDtypeStruct((1,) + a.shape, a.dtype) for a in arrays], N_DEV - 1, make_copies)
    carry.inputs = list(arrays)
    return carry


def _sum_everyone(own, received, me, name, after=()):
    def body(me_ref, own_ref, r_ref, *refs):
        g = None
        for d in range(N_DEV):
            relation = jnp.bitwise_xor(d, me_ref[0])
            part = jnp.where(relation == 0, own_ref[...], r_ref[jnp.maximum(relation - 1, 0)])
            g = part if g is None else g + part
        refs[-1][...] = g

    whole = lambda shape: pl.BlockSpec(shape, lambda i, me_ref: (0,) * len(shape))
    return pl.pallas_call(
        body, name=name,
        grid_spec=pltpu.PrefetchScalarGridSpec(
            num_scalar_prefetch=1, grid=(1,), in_specs=[whole(own.shape), whole(received.shape)] + [HBM_SPEC] * len(after),
            out_specs=whole(own.shape)),
        out_shape=jax.ShapeDtypeStruct(own.shape, F32))(me, *_in_hbm(own, received), *after)


SEM_SPEC = pl.BlockSpec(memory_space=pltpu.SEMAPHORE)
DATAFLOW_EFFECT = pltpu.SideEffectType.DATAFLOW_SIDE_EFFECTING


def _exchange_start(carry, name, after=()):
    n = len(carry.inputs)
    lands = [lax.empty(s.shape, s.dtype) for s in carry.out_shapes]

    def body(*refs):
        first_out = 2 * n + len(after)
        srcs, zones, sems, token = refs[:n], refs[n:2 * n], refs[first_out:first_out + 2], refs[-1]
        carry.start(srcs, zones, sems)
        token[...] = jnp.zeros_like(token)

    outs = pl.pallas_call(
        body, name=name, in_specs=[HBM_SPEC] * (2 * n + len(after)),
        out_specs=[SEM_SPEC, SEM_SPEC] + [HBM_SPEC] * (2 * n) + [pl.BlockSpec(memory_space=pltpu.VMEM)],
        out_shape=list(carry.sems) + _hbm_out([jax.ShapeDtypeStruct(a.shape, a.dtype) for a in carry.inputs])
        + _hbm_out(carry.out_shapes) + [jax.ShapeDtypeStruct((SUBLANES, LANES), F32)],
        input_output_aliases={j: 2 + j for j in range(2 * n)},
        compiler_params=pltpu.CompilerParams(has_side_effects=DATAFLOW_EFFECT),
    )(*_in_hbm(*carry.inputs, *lands), *after)
    return outs[:-1], outs[-1]


def _exchange_wait(carry, in_flight, after, name):
    n = len(carry.inputs)
    sems, srcs, zones = in_flight[:2], in_flight[2:2 + n], in_flight[2 + n:]

    def body(*refs):
        src_refs, zone_refs, sem_refs = refs[:n], refs[n:2 * n], refs[2 * n:2 * n + 2]
        carry.finish(src_refs, zone_refs, sem_refs)

    outs = pl.pallas_call(
        body, name=name, in_specs=[HBM_SPEC] * (2 * n) + [SEM_SPEC, SEM_SPEC] + [HBM_SPEC] * len(after),
        out_specs=[HBM_SPEC] * (2 * n),
        out_shape=_hbm_out([jax.ShapeDtypeStruct(a.shape, a.dtype) for a in carry.inputs]) + _hbm_out(carry.out_shapes),
        input_output_aliases={j: j for j in range(2 * n)},
        compiler_params=pltpu.CompilerParams(has_side_effects=DATAFLOW_EFFECT),
    )(*srcs, *zones, *sems, *after)
    return list(outs[:n]), list(outs[n:])


def _add_sibling(grads8, recvs, core, row_tiles, name):
    k = len(grads8)
    g4 = [g.reshape(4, 2, *g.shape[1:]) for g in grads8]

    def body(core_ref, *refs):
        g_refs, r_refs, o_refs, ob_refs = (refs[j * k:(j + 1) * k] for j in range(4))
        for g_ref, r_ref, o_ref, ob_ref in zip(g_refs, r_refs, o_refs, ob_refs):
            s = g_ref[0] + r_ref[...]
            o_ref[...] = s
            ob_ref[...] = s.astype(BF16)

    def blocks(make):
        return [make(g.shape[1] // row_tiles, g.shape[2]) for g in grads8]

    slot = lambda tr, C: pl.BlockSpec((1, tr, C), lambda ch, r, core_ref: (ch, r, 0))
    outs = pl.pallas_call(
        body, name=name,
        grid_spec=pltpu.PrefetchScalarGridSpec(
            num_scalar_prefetch=1, grid=(4, row_tiles),
            in_specs=blocks(lambda tr, C: pl.BlockSpec((1, 1, tr, C), lambda ch, r, core_ref: (ch, core_ref[0], r, 0)))
            + blocks(slot), out_specs=blocks(slot) + blocks(slot)),
        out_shape=_hbm_out([jax.ShapeDtypeStruct((4,) + g.shape[1:], F32) for g in grads8]
                           + [jax.ShapeDtypeStruct((4,) + g.shape[1:], BF16) for g in grads8]),
        compiler_params=_params(("arbitrary", "arbitrary")),
    )(core, *_in_hbm(*g4, *recvs))
    return list(outs[:k]), list(outs[k:])


def _adam_math(w, g, m, v):
    m = ADAM_B1 * m + (1.0 - ADAM_B1) * g
    v = ADAM_B2 * v + (1.0 - ADAM_B2) * jnp.square(g)
    m_hat = m / (1.0 - ADAM_B1 ** ADAM_STEP)
    v_hat = v / (1.0 - ADAM_B2 ** ADAM_STEP)
    delta = -ADAM_LR * (m_hat / (jnp.sqrt(v_hat) + ADAM_EPS) + ADAM_WD * w)
    return delta, m, v


def _adam_big(ws, ms, vs, chip_sums, recvs, chip, row_tiles, name, after=()):
    k = len(ws)

    def body(chip_ref, *refs):
        refs = refs[:5 * k] + refs[5 * k + len(after):]
        w_refs, m_refs, v_refs, s_refs, r_refs, g_refs, d_refs, nm_refs, nv_refs = (refs[j * k:(j + 1) * k] for j in range(9))
        for a in range(k):
            r_ref = r_refs[a]
            g = s_refs[a][0] + r_ref[0].astype(F32) + r_ref[1].astype(F32) + r_ref[2].astype(F32)
            g_refs[a][...] = g
            d_refs[a][...], nm_refs[a][...], nv_refs[a][...] = _adam_math(w_refs[a][...], g, m_refs[a][...], v_refs[a][...])

    def blocks(make):
        return [make(w.shape[0] // row_tiles, w.shape[1]) for w in ws]

    blk = lambda tr, C: pl.BlockSpec((tr, C), lambda r, chip_ref: (r, 0))
    outs = pl.pallas_call(
        body, name=name,
        grid_spec=pltpu.PrefetchScalarGridSpec(
            num_scalar_prefetch=1, grid=(row_tiles,),
            in_specs=blocks(blk) * 3 + blocks(lambda tr, C: pl.BlockSpec((1, tr, C), lambda r, chip_ref: (chip_ref[0], r, 0)))
            + blocks(lambda tr, C: pl.BlockSpec((3, tr, C), lambda r, chip_ref: (0, r, 0))) + [HBM_SPEC] * len(after),
            out_specs=blocks(blk) * 4),
        out_shape=[jax.ShapeDtypeStruct(w.shape, F32) for w in ws] * 4,
        compiler_params=_params(("arbitrary",)),
    )(chip, *_in_hbm(*ws, *ms, *vs, *chip_sums, *recvs), *after)
    return [list(outs[j * k:(j + 1) * k]) for j in range(4)]


def _sum_partials(partials, name, after=()):
    def body(p_ref, *refs):
        g = p_ref[0]
        for d in range(1, partials.shape[0]):
            g = g + p_ref[d]
        refs[-1][...] = g

    return pl.pallas_call(body, name=name, grid=(1,), in_specs=[_whole(partials.shape)] + [HBM_SPEC] * len(after),
                          out_specs=_whole(partials.shape[1:]),
                          out_shape=jax.ShapeDtypeStruct(partials.shape[1:], F32))(*_in_hbm(partials), *after)


def _adam_small(ws, ms, vs, gs):
    n = len(ws)

    def body(*refs):
        w_refs, m_refs, v_refs, g_refs = (refs[i * n:(i + 1) * n] for i in range(4))
        d_refs, nm_refs, nv_refs = (refs[(4 + i) * n:(5 + i) * n] for i in range(3))
        for j in range(n):
            d_refs[j][...], nm_refs[j][...], nv_refs[j][...] = _adam_math(
                w_refs[j][...], g_refs[j][...], m_refs[j][...], v_refs[j][...])

    specs = [_whole(w.shape) for w in ws]
    outs = pl.pallas_call(body, name="adam_small", grid=(1,), in_specs=specs * 4, out_specs=specs * 3,
                          out_shape=[jax.ShapeDtypeStruct(w.shape, F32) for w in ws] * 3,
                          compiler_params=_params(("arbitrary",), VMEM_MID))(*_in_hbm(*ws, *ms, *vs, *gs))
    return outs[:n], outs[n:2 * n], outs[2 * n:]


PACK_QUANTUM = SUBLANES * LANES


def _pack(named, names):
    parts = []
    for nme in names:
        flat = named[nme].reshape(-1)
        parts.append(jnp.pad(flat, (0, -flat.size % PACK_QUANTUM)))
    return jnp.concatenate(parts).reshape(-1, LANES)


def _unpack(packed, shapes, names):
    flat = packed.reshape(-1)
    out, pos = {}, 0
    for nme in names:
        size = math.prod(shapes[nme])
        out[nme] = flat[pos:pos + size].reshape(shapes[nme])
        pos += size + (-size % PACK_QUANTUM)
    return out


BIG = ("w_in", "w_glu", "w_attn_branch", "w_ssm_branch", "w_out", "w_ff_in", "w_ff_out")
COLUMN_SHARDED = ("w_in", "w_attn_branch", "w_ssm_branch", "w_ff_in")
SMALL = ("norm_mix_pre", "norm_mix_post", "norm_mlp_pre", "norm_mlp_post", "rel_bias", "sinks", "lam_re", "lam_im",
         "log_dt", "b_re", "b_im", "c_re", "c_im", "d_skip")
SWAPPED_SMALL = ("rel_bias", "b_re", "b_im")
SMALL_LATE = ("norm_mix_pre", "rel_bias", "sinks", "loss")
SMALL_BEFORE_ATTN_BWD = tuple(n for n in SMALL if n not in SMALL_LATE)
ALL_WEIGHTS = ("norm_mix_pre", "norm_mix_post", "norm_mlp_pre", "norm_mlp_post", "w_in", "rel_bias", "sinks", "lam_re",
               "lam_im", "log_dt", "b_re", "b_im", "c_re", "c_im", "d_skip", "w_glu", "w_attn_branch", "w_ssm_branch",
               "w_out", "w_ff_in", "w_ff_out")


def _full_from_gathered(name, gathered):
    _, r, c = gathered.shape
    if name in COLUMN_SHARDED:
        return jnp.transpose(gathered, (1, 0, 2)).reshape(r, N_DEV * c)
    return gathered.reshape(N_DEV * r, c)


def _blocks_from_full(name, full):
    r, c = full.shape
    if name in COLUMN_SHARDED:
        return jnp.transpose(full.reshape(r, N_DEV, c // N_DEV), (1, 0, 2))
    return full.reshape(N_DEV, r // N_DEV, c)


def kernel(x, norm_mix_pre, norm_mix_post, norm_mlp_pre, norm_mlp_post, w_in, rel_bias, sinks, lam_re, lam_im, log_dt, b_re, b_im, c_re, c_im, d_skip, w_glu, w_attn_branch, w_ssm_branch, w_out, w_ff_in, w_ff_out, loss_target, m_norm_mix_pre, m_norm_mix_post, m_norm_mlp_pre, m_norm_mlp_post, m_w_in, m_rel_bias, m_sinks, m_lam_re, m_lam_im, m_log_dt, m_b_re, m_b_im, m_c_re, m_c_im, m_d_skip, m_w_glu, m_w_attn_branch, m_w_ssm_branch, m_w_out, m_w_ff_in, m_w_ff_out, v_norm_mix_pre, v_norm_mix_post, v_norm_mlp_pre, v_norm_mlp_post, v_w_in, v_rel_bias, v_sinks, v_lam_re, v_lam_im, v_log_dt, v_b_re, v_b_im, v_c_re, v_c_im, v_d_skip, v_w_glu, v_w_attn_branch, v_w_ssm_branch, v_w_out, v_w_ff_in, v_w_ff_out):
    args = dict(locals())
    w = {n: args[n] for n in ALL_WEIGHTS}
    m = {n: args["m_" + n] for n in ALL_WEIGHTS}
    v = {n: args["v_" + n] for n in ALL_WEIGHTS}
    core = lax.axis_index("c").astype(jnp.int32).reshape(1)
    chip = (2 * lax.axis_index("x") + lax.axis_index("y")).astype(jnp.int32).reshape(1)
    xs, target = x[0], loss_target[0]
    t = _tiles(xs.shape[0])
    local = lambda d, n: d[n][0].T if n == "w_in" else d[n][0]
    shard = {n: local(w, n).astype(BF16) for n in BIG}
    shard["w_ff_in"] = shard["w_ff_in"].T
    view = lambda n, a: jnp.swapaxes(a, -1, -2) if n in SWAPPED_SMALL else a
    small = {n: (view(n, w[n]) if n == "rel_bias" else view(n, w[n])[0]) for n in SMALL}
    g1, g2, g3, g4 = (small[n].reshape(1, D_MODEL) for n in ("norm_mix_pre", "norm_mix_post", "norm_mlp_pre", "norm_mlp_post"))
    bucket = jnp.asarray(_bucket_table())
    rel_b, sink = small["rel_bias"], small["sinks"].reshape(1, N_HEADS)
    lam_r, lam_i = small["lam_re"].reshape(1, STATES), small["lam_im"].reshape(1, STATES)
    ldt_rep = jnp.repeat(small["log_dt"].reshape(N_GROUPS), N_STATE).reshape(1, STATES)
    bd_re, bd_im = _block_diag_in(small["b_re"]), _block_diag_in(small["b_im"])
    cm_re, cm_im = _block_diag_out(small["c_re"]).astype(BF16), _block_diag_out(small["c_im"]).astype(BF16)
    dsk = small["d_skip"].reshape(1, SSM_W)

    (g_in,) = _run_carry(_gather_carry([shard["w_in"]]), "gather_w_in")
    wf_in = g_in.reshape(IN_W, D_MODEL)
    merge_names = ("w_glu", "w_attn_branch", "w_ssm_branch", "w_out")
    (q, k, vv, u, ga, gs, h), gathered = _in_proj_fwd(xs, g1, wf_in, t["proj"], _gather_carry([shard[n] for n in merge_names]))
    wf = {n: _full_from_gathered(n, g) for n, g in zip(merge_names, gathered)}
    (att,), (wf_ff_in,) = _attn_fwd(q, k, vv, bucket, rel_b, sink, _gather_carry([shard["w_ff_in"]]))
    a_re, a_im, bm_re, bm_im = _ssm_prep(lam_r, lam_i, ldt_rep, bd_re, bd_im)
    (y, h_re, h_im, in_re, in_im), (wf_ff_out,) = _ssm_fwd(
        u, a_re, a_im, bm_re, bm_im, cm_re, cm_im, dsk, t["ssm_chunk"], _gather_carry([shard["w_ff_out"]]))
    x1, o, h2 = _merge_fwd(xs, y, att, ga, gs, g2, g3, wf["w_glu"], wf["w_ssm_branch"], wf["w_attn_branch"], wf["w_out"],
                           t["merge"])
    a, dfo, dx2, loss_blk, dg4 = _mlp_fwd(h2, x1, target, g4, wf_ff_in, wf_ff_out, t["mlp_fwd"])

    groups = {"ff": 4, "merge": 1, "w_in": 2}

    def add_sibling(group, blocks, received):
        return _add_sibling(blocks, received, core, groups[group], "add_sibling_" + group)

    ff_names = ("w_ff_in", "w_ff_out")
    dw_ff_in, dw_ff_out, da = _mlp_weight_grads(dfo, a, h2, wf_ff_out, t["mlp_bwd"])
    dh2 = _mlp_input_grad(da, wf_ff_in.reshape(D_FF, D_MODEL), t["mlp_bwd"])
    ff_blocks = [dw_ff_in, dw_ff_out]
    (dx1, dgates, datt, dy, dw_glu, dw_ssm, dw_attn, dw_out, dg2, dg3), ff_recv = _merge_bwd(
        dh2, dx2, x1, o, y, att, ga, gs, g2, g3, wf["w_glu"], wf["w_ssm_branch"], wf["w_attn_branch"], wf["w_out"],
        t["merge_bwd"], _sibling_carry(ff_blocks))
    ff_sums, ff_sums_bf = add_sibling("ff", ff_blocks, ff_recv)
    merge_blocks = [_blocks_from_full(n, g) for n, g in zip(merge_names, (dw_glu, dw_attn, dw_ssm, dw_out))]
    (du, dbm_re, dbm_im, dcm_re, dcm_im, da_re, da_im, dd_skip), carried = _ssm_bwd(
        dy, u, h_re, h_im, in_re, in_im, a_re, a_im, bm_re, bm_im, cm_re, cm_im, dsk, t["ssm_chunk"],
        _join(_chips_carry(ff_sums_bf), _sibling_carry(merge_blocks)))
    ff_from_chips, merge_recv = carried[:2], carried[2:]
    merge_sums, merge_sums_bf = add_sibling("merge", merge_blocks, merge_recv)
    dbd_re, dbd_im, dlam_re, dlam_im, dldt_rep = _ssm_prep_bwd(lam_r, lam_i, ldt_rep, bd_re, bd_im, dbm_re, dbm_im, da_re, da_im)
    dlog_dt = _group_sum(dldt_rep.reshape(N_GROUPS, N_STATE))
    shapes = {n: view(n, w[n]).shape for n in SMALL}
    shapes["loss"] = (1,)
    small_grads = dict(
        norm_mix_post=dg2, norm_mlp_pre=dg3, norm_mlp_post=dg4, lam_re=dlam_re, lam_im=dlam_im, log_dt=dlog_dt,
        b_re=_block_diag_in_grad(dbd_re), b_im=_block_diag_in_grad(dbd_im),
        c_re=_block_diag_out_grad(dcm_re), c_im=_block_diag_out_grad(dcm_im), d_skip=dd_skip)
    packed_early = _pack({n: small_grads[n].reshape(shapes[n]) for n in SMALL_BEFORE_ATTN_BWD}, SMALL_BEFORE_ATTN_BWD)
    (dq, dkv, attn_small), carried = _attn_bwd(
        q, k, vv, datt, bucket, rel_b, sink, _join(_chips_carry(merge_sums_bf), _gather_carry([packed_early])))
    merge_from_chips, partials_early = carried[:-1], carried[-1]

    dparts = (dq, dkv, du, dgates)
    dw_in_t = _in_proj_weight_grad(h, dparts)
    in_blocks = [dw_in_t.reshape(N_DEV, IN_W // N_DEV, D_MODEL)]
    to_sibling = _sibling_carry(in_blocks)
    in_flight, token = _exchange_start(to_sibling, "w_in_sibling_start")
    n_tiles = xs.shape[0] // t["proj_bwd"]
    (grad_x, dg1), _ = _in_proj_input_grad(xs, g1 + token[0:1, 0:1], wf_in, dx1, dparts, t["proj_bwd"], 0, n_tiles, "in_proj_input_grad")
    late = dict(norm_mix_pre=dg1, rel_bias=attn_small[:, :N_BUCKETS, 0], sinks=attn_small[:, N_BUCKETS, 0], loss=loss_blk[0:1, 0])
    packed_late = _pack({n: late[n].reshape(shapes[n]) for n in SMALL_LATE}, SMALL_LATE)
    to_everyone = _everyone_carry([packed_late])
    late_in_flight, _ = _exchange_start(to_everyone, "late_grads_start")
    in_blocks, in_recv = _exchange_wait(to_sibling, in_flight, [dg1], "w_in_sibling_wait")
    in_sums, in_sums_bf = add_sibling("w_in", in_blocks, in_recv)
    to_chips = _chips_carry(in_sums_bf)
    in_flight, chips_started = _exchange_start(to_chips, "w_in_chips_start")
    (packed_late,), (late_received,) = _exchange_wait(to_everyone, late_in_flight, [chips_started], "late_grads_wait")

    grads, deltas, new_m, new_v = {}, {}, {}, {}

    def adam_group(group, names, sums, received, after=()):
        outs = _adam_big(*[[local(d, n) for n in names] for d in (w, m, v)], sums, received, chip, groups[group],
                         "adam_" + group, after)
        for store, vals in zip((grads, deltas, new_m, new_v), outs):
            store.update({n: (o.T if n == "w_in" else o)[None] for n, o in zip(names, vals)})

    adam_group("ff", ff_names, ff_sums, ff_from_chips, [chips_started])
    adam_group("merge", merge_names, merge_sums, merge_from_chips, [chips_started])

    grads.update(_unpack(_sum_partials(partials_early, "sum_small_grads", [chips_started]), shapes, SMALL_BEFORE_ATTN_BWD))
    grads.update(_unpack(_sum_everyone(packed_late, late_received, 2 * chip + core, "sum_late_grads"), shapes, SMALL_LATE))
    loss = grads.pop("loss").reshape(())
    small_out = _adam_small(*[[view(n, d[n]) for n in SMALL] for d in (w, m, v)], [grads[n] for n in SMALL])
    for store, vals in zip((deltas, new_m, new_v), small_out):
        store.update(zip(SMALL, vals))
    for store in (grads, deltas, new_m, new_v):
        store.update({n: view(n, store[n]) for n in SWAPPED_SMALL})

    busy = [new_v["w_ff_out"], new_v["w_out"], deltas["norm_mix_pre"]]
    _, (in_from_chips,) = _exchange_wait(to_chips, in_flight, busy, "w_in_chips_wait")
    adam_group("w_in", ("w_in",), in_sums, [in_from_chips])

    return (loss, grad_x[None], *[grads[n] for n in ALL_WEIGHTS], *[deltas[n] for n in ALL_WEIGHTS],
            *[new_m[n] for n in ALL_WEIGHTS], *[new_v[n] for n in ALL_WEIGHTS])
```

```python
import math

import jax
import jax.numpy as jnp
import numpy as np
from jax import lax
from jax.experimental import pallas as pl
from jax.experimental.pallas import tpu as pltpu

F32 = jnp.float32
BF16 = jnp.bfloat16

D_MODEL = 1024
N_HEADS = 8
HEAD_DIM = 64
ATTN_W = 512
KV_W = 128
BLOCK = 128
N_BUCKETS = 32
SSM_W = 512
N_GROUPS = 32
N_STATE = 64
GROUP_CH = 16
STATES = N_GROUPS * N_STATE
D_FF = 4096
IN_W = 3328
SPLITS = (0, 512, 640, 768, 1280, 2304, 3328)
RMS_EPS = 1e-6
NEG_INF = -1e30
SUBLANES = 8
LANES = 128
SSM_LANE_BLOCK = 512
N_SSM_BLOCKS = STATES // SSM_LANE_BLOCK
GROUPS_PER_BLOCK = SSM_LANE_BLOCK // N_STATE
VMEM_BIG = 52 * 1024 * 1024
VMEM_MID = 40 * 1024 * 1024
VMEM_MAX = 60 * 1024 * 1024

ADAM_LR = 0.001
ADAM_B1 = 0.9
ADAM_B2 = 0.999
ADAM_EPS = 1e-08
ADAM_WD = 0.01
ADAM_STEP = 10

N_DEV = 8


def _dot(a, b):
    return jnp.dot(a, b, preferred_element_type=F32)


def _dot_nt(a, b):
    return lax.dot_general(a, b, (((1,), (1,)), ((), ())), preferred_element_type=F32)


def _dot_tn(a, b):
    return lax.dot_general(a, b, (((0,), (0,)), ((), ())), preferred_element_type=F32)


def _rms_scale(x):
    return lax.rsqrt(jnp.mean(x * x, axis=-1, keepdims=True) + RMS_EPS)


def _rms_bwd(dy, x, r, g):
    t = dy * g
    dx = r * t - x * (r * r * r) * jnp.mean(t * x, axis=-1, keepdims=True)
    dg = jnp.sum(dy * x * r, axis=0, keepdims=True)
    return dx, dg


def _const_spec(shape):
    nd = len(shape)
    return pl.BlockSpec(shape, lambda *_: (0,) * nd, pipeline_mode=pl.Buffered(1))


def _in_hbm(*arrays):
    return tuple(pltpu.with_memory_space_constraint(a, pltpu.HBM) for a in arrays)


def _hbm_out(shapes):
    if isinstance(shapes, (list, tuple)):
        return [_hbm_out(s) for s in shapes]
    return shapes if isinstance(shapes, pl.MemoryRef) else pltpu.HBM(shapes.shape, shapes.dtype)


def _whole(shape):
    nd = len(shape)
    return pl.BlockSpec(shape, lambda *_: (0,) * nd)


def _params(sem, vmem=None):
    return pltpu.CompilerParams(dimension_semantics=sem, vmem_limit_bytes=vmem)


MESH_IDS = pl.DeviceIdType.MESH
HBM_SPEC = pl.BlockSpec(memory_space=pl.ANY)


class _Carry:
    def __init__(self, inputs, out_shapes, sems, start, finish):
        self.inputs, self.out_shapes, self.sems, self.start, self.finish = list(inputs), list(out_shapes), list(sems), start, finish


def _join(a, b):
    na_in, na_out, na_sem = len(a.inputs), len(a.out_shapes), len(a.sems)

    def start(ins, outs, sems):
        a.start(ins[:na_in], outs[:na_out], sems[:na_sem])
        b.start(ins[na_in:], outs[na_out:], sems[na_sem:])

    def finish(ins, outs, sems):
        a.finish(ins[:na_in], outs[:na_out], sems[:na_sem])
        b.finish(ins[na_in:], outs[na_out:], sems[na_sem:])

    return _Carry(a.inputs + b.inputs, a.out_shapes + b.out_shapes, a.sems + b.sems, start, finish)


def _hosted_call(body, carry, edge, *, name, grid, in_specs, out_specs, out_shape, scratch_shapes, compiler_params, inputs):
    n_in, n_out = len(in_specs), len(out_specs)
    inputs = [a if s.memory_space == pltpu.SMEM else _in_hbm(a)[0] for a, s in zip(inputs, in_specs)]
    out_shape = _hbm_out(list(out_shape))
    if carry is None:
        outs = pl.pallas_call(body, name=name, grid=grid, in_specs=in_specs, out_specs=out_specs, out_shape=out_shape,
                              scratch_shapes=scratch_shapes, compiler_params=compiler_params)(*inputs)
        return list(outs), []
    c_in, c_out, c_sem = len(carry.inputs), len(carry.out_shapes), len(carry.sems)

    def wrapped(*refs):
        ins, refs = refs[:n_in], refs[n_in:]
        cins, refs = refs[:c_in], refs[c_in:]
        outs, refs = refs[:n_out], refs[n_out:]
        couts, refs = refs[:c_out], refs[c_out:]
        scratch, csems = refs[:len(refs) - c_sem], refs[len(refs) - c_sem:]
        first, last = edge()

        @pl.when(first)
        def _():
            carry.start(cins, couts, csems)

        body(*ins, *outs, *scratch)

        @pl.when(last)
        def _():
            carry.finish(cins, couts, csems)

    outs = pl.pallas_call(
        wrapped, name=name, grid=grid, in_specs=list(in_specs) + [HBM_SPEC] * c_in,
        out_specs=list(out_specs) + [HBM_SPEC] * c_out, out_shape=out_shape + _hbm_out(carry.out_shapes),
        scratch_shapes=list(scratch_shapes) + carry.sems, compiler_params=compiler_params)(*inputs, *_in_hbm(*carry.inputs))
    return list(outs[:n_out]), list(outs[n_out:])


def _edge_1d(n_steps):
    return lambda: (pl.program_id(0) == 0, pl.program_id(0) == n_steps - 1)


def _edge_2d(n0, n1):
    return lambda: ((pl.program_id(0) == 0) & (pl.program_id(1) == 0),
                    (pl.program_id(0) == n0 - 1) & (pl.program_id(1) == n1 - 1))


def _run_carry(carry, name):
    c_in, c_out = len(carry.inputs), len(carry.out_shapes)

    def body(*refs):
        ins, outs, sems = refs[:c_in], refs[c_in:c_in + c_out], refs[c_in + c_out:]
        carry.start(ins, outs, sems)
        carry.finish(ins, outs, sems)

    return pl.pallas_call(body, name=name, in_specs=[HBM_SPEC] * c_in, out_specs=[HBM_SPEC] * c_out,
                          out_shape=_hbm_out(carry.out_shapes), scratch_shapes=carry.sems)(*_in_hbm(*carry.inputs))


def _in_proj_fwd(x, g1, w_in_t, tile, carry=None):
    T = x.shape[0]

    def body(x_ref, g_ref, w_ref, q_ref, k_ref, v_ref, u_ref, ga_ref, gs_ref, h_ref):
        xv = x_ref[...]
        h = (xv * _rms_scale(xv) * g_ref[...]).astype(BF16)
        h_ref[...] = h
        outs = (q_ref, k_ref, v_ref, u_ref, ga_ref, gs_ref)
        for p, o_ref in enumerate(outs):
            o_ref[...] = _dot_nt(h, w_ref[SPLITS[p]:SPLITS[p + 1], :]).astype(o_ref.dtype)

    widths = [SPLITS[p + 1] - SPLITS[p] for p in range(6)] + [D_MODEL]
    dtypes = [BF16, BF16, BF16, F32, F32, F32, BF16]
    return _hosted_call(
        body, carry, _edge_1d(T // tile), name="in_proj_fwd", grid=(T // tile,),
        in_specs=[pl.BlockSpec((tile, D_MODEL), lambda i: (i, 0)), _const_spec((1, D_MODEL)), _const_spec((IN_W, D_MODEL))],
        out_specs=[pl.BlockSpec((tile, w), lambda i: (i, 0)) for w in widths],
        out_shape=[jax.ShapeDtypeStruct((T, w), dt) for w, dt in zip(widths, dtypes)],
        scratch_shapes=[], compiler_params=_params(("arbitrary",), VMEM_MID), inputs=(x, g1, w_in_t))


PROJ_PARTS = (512, 256, 512, 2048)
PROJ_GRAD_BLOCK = 256


def _in_proj_weight_grad(h, dparts):
    T = h.shape[0]
    blocks = [wd // PROJ_GRAD_BLOCK for wd in PROJ_PARTS]
    starts = [sum(blocks[:p]) for p in range(len(blocks))]

    def body(h_ref, *refs):
        part_refs, o_ref = refs[:-1], refs[-1]
        j = pl.program_id(0)
        for p_ref, start, count in zip(part_refs, starts, blocks):
            @pl.when((j >= start) & (j < start + count))
            def _(p_ref=p_ref):
                o_ref[...] = _dot_tn(p_ref[...], h_ref[...])

    def part_spec(start, count):
        return pl.BlockSpec((T, PROJ_GRAD_BLOCK), lambda j: (0, jnp.clip(j - start, 0, count - 1)))

    return pl.pallas_call(
        body, name="in_proj_weight_grad", grid=(sum(blocks),),
        in_specs=[_const_spec((T, D_MODEL))] + [part_spec(s, c) for s, c in zip(starts, blocks)],
        out_specs=pl.BlockSpec((PROJ_GRAD_BLOCK, D_MODEL), lambda j: (j, 0)),
        out_shape=_hbm_out(jax.ShapeDtypeStruct((IN_W, D_MODEL), F32)),
        compiler_params=_params(("arbitrary",), VMEM_MID),
    )(*_in_hbm(h, *dparts))


def _in_proj_input_grad(x, g1, w_in_t, dx1, dparts, tile, first_tile, n_tiles, name, carry=None):
    offsets = [sum(PROJ_PARTS[:p]) for p in range(len(PROJ_PARTS))]

    def body(x_ref, g_ref, w_ref, dx1_ref, *refs):
        part_refs, (gx_ref, dg_ref) = refs[:len(PROJ_PARTS)], refs[len(PROJ_PARTS):]
        i = pl.program_id(0)
        xv = x_ref[...]
        r = _rms_scale(xv)
        g = g_ref[...]
        dh = sum(_dot(p_ref[...], w_ref[off:off + wd, :]) for p_ref, off, wd in zip(part_refs, offsets, PROJ_PARTS))
        dxn, dg = _rms_bwd(dh, xv, r, g)
        gx_ref[...] = dx1_ref[...] + dxn

        @pl.when(i == 0)
        def _():
            dg_ref[...] = dg

        @pl.when(i > 0)
        def _():
            dg_ref[...] += dg

    tok = lambda wd: pl.BlockSpec((tile, wd), lambda i: (i + first_tile, 0))
    return _hosted_call(
        body, carry, _edge_1d(n_tiles), name=name, grid=(n_tiles,),
        in_specs=[tok(D_MODEL), _const_spec((1, D_MODEL)), _const_spec((IN_W, D_MODEL)), tok(D_MODEL)] + [tok(wd) for wd in PROJ_PARTS],
        out_specs=[pl.BlockSpec((tile, D_MODEL), lambda i: (i, 0)), pl.BlockSpec((1, D_MODEL), lambda i: (0, 0))],
        out_shape=[jax.ShapeDtypeStruct((n_tiles * tile, D_MODEL), F32), jax.ShapeDtypeStruct((1, D_MODEL), F32)],
        scratch_shapes=[], compiler_params=_params(("arbitrary",), VMEM_MID), inputs=(x, g1, w_in_t, dx1, *dparts))


def _bucket_table():
    qi = np.arange(BLOCK)[:, None]
    kj = np.arange(2 * BLOCK)[None, :]
    dist = qi + BLOCK - kj
    max_exact = N_BUCKETS // 2
    d = np.maximum(dist, 0)
    df = np.maximum(d, 1).astype(np.float32)
    large = max_exact + (np.log(df / np.float32(max_exact)) / np.float32(math.log(BLOCK / max_exact))
                         * np.float32(N_BUCKETS - max_exact)).astype(np.int32)
    large = np.minimum(large, N_BUCKETS - 1)
    bucket = np.where(d < max_exact, d, large)
    return np.where((dist >= 0) & (dist < BLOCK), bucket, -1).astype(np.int32)


def _build_bias(bucket_ref, rb_ref, bias_ref):
    bk = bucket_ref[...]
    for h in range(N_HEADS):
        def add(b, acc, h=h):
            return acc + jnp.where(bk == b, rb_ref[h, b], 0.0)
        bias_ref[h] = lax.fori_loop(0, N_BUCKETS, add, jnp.zeros((BLOCK, 2 * BLOCK), F32))


def _kv_variants(prev_ref, cur_ref):
    cat = jnp.concatenate([prev_ref[...], cur_ref[...]], axis=0)
    lo = lax.broadcasted_iota(jnp.int32, cat.shape, 1) < HEAD_DIM
    zero = jnp.zeros_like(cat)
    head0_lo = jnp.where(lo, cat, zero)
    head1_hi = jnp.where(lo, zero, cat)
    return ((head0_lo, pltpu.roll(head0_lo, HEAD_DIM, 1)), (pltpu.roll(head1_hi, HEAD_DIM, 1), head1_hi))


def _merge_kv_grads(g):
    lo = lax.broadcasted_iota(jnp.int32, g[0][0].shape, 1) < HEAD_DIM
    return jnp.where(lo, g[0][0] + pltpu.roll(g[0][1], HEAD_DIM, 1), g[1][1] + pltpu.roll(g[1][0], HEAD_DIM, 1))


def _head_lanes(h):
    return slice((h // 2) * LANES, (h // 2 + 1) * LANES)


def _attn_probs(q_ref, kvar, bias_ref, sk_ref, valid, s_ref):
    for h in range(N_HEADS):
        s_ref[h] = _dot_nt(q_ref[:, _head_lanes(h)], kvar[h // 4][h % 2])
    head = lax.broadcasted_iota(jnp.int32, (N_HEADS, 1, 1), 0)
    sink = jnp.zeros((N_HEADS, 1, 1), F32)
    for h in range(N_HEADS):
        sink = jnp.where(head == h, sk_ref[0, h], sink)
    s = jnp.where(valid[None], s_ref[...] * (HEAD_DIM ** -0.5) + bias_ref[...], NEG_INF)
    m = jnp.maximum(jnp.max(s, axis=-1, keepdims=True), sink)
    p = jnp.exp(s - m)
    e_sink = jnp.exp(sink - m)
    inv = 1.0 / (jnp.sum(p, axis=-1, keepdims=True) + e_sink)
    return p * inv, e_sink * inv


def _attn_valid(bucket_ref, n):
    col = lax.broadcasted_iota(jnp.int32, (BLOCK, 2 * BLOCK), 1)
    return (bucket_ref[...] >= 0) & ((n > 0) | (col >= BLOCK))


def _attn_fwd(q, k, v, bucket, rel_bias, sinks, carry=None):
    T = q.shape[0]
    nb = T // BLOCK

    def body(q_ref, kc_ref, kp_ref, vc_ref, vp_ref, bucket_ref, rb_ref, sk_ref, o_ref, bias_ref, s_ref, p_ref):
        n = pl.program_id(0)

        @pl.when(n == 0)
        def _():
            _build_bias(bucket_ref, rb_ref, bias_ref)

        kvar = _kv_variants(kp_ref, kc_ref)
        vvar = _kv_variants(vp_ref, vc_ref)
        pr, _ = _attn_probs(q_ref, kvar, bias_ref, sk_ref, _attn_valid(bucket_ref, n), s_ref)
        p_ref[...] = pr.astype(BF16)
        for m in range(N_HEADS // 2):
            acc = _dot(p_ref[2 * m], vvar[m // 2][0]) + _dot(p_ref[2 * m + 1], vvar[m // 2][1])
            o_ref[:, m * LANES:(m + 1) * LANES] = acc.astype(o_ref.dtype)

    cur = lambda w: pl.BlockSpec((BLOCK, w), lambda n: (n, 0))
    prev = lambda w: pl.BlockSpec((BLOCK, w), lambda n: (jnp.maximum(n - 1, 0), 0))
    smem = pl.BlockSpec(memory_space=pltpu.SMEM)
    return _hosted_call(
        body, carry, _edge_1d(nb), name="attn_fwd", grid=(nb,),
        in_specs=[cur(ATTN_W), cur(KV_W), prev(KV_W), cur(KV_W), prev(KV_W), _const_spec((BLOCK, 2 * BLOCK)), smem, smem],
        out_specs=[cur(ATTN_W)],
        out_shape=[jax.ShapeDtypeStruct((T, ATTN_W), BF16)],
        scratch_shapes=[pltpu.VMEM((N_HEADS, BLOCK, 2 * BLOCK), F32), pltpu.VMEM((N_HEADS, BLOCK, 2 * BLOCK), F32),
                        pltpu.VMEM((N_HEADS, BLOCK, 2 * BLOCK), BF16)],
        compiler_params=_params(("arbitrary",)), inputs=(q, k, k, v, v, bucket, rel_bias, sinks))


ATTN_SMALL_ROWS = N_BUCKETS + SUBLANES


def _attn_bwd(q, k, v, datt, bucket, rel_bias, sinks, carry=None):
    T = q.shape[0]
    nb = T // BLOCK

    def body(q_ref, do_ref, kc_ref, kp_ref, vc_ref, vp_ref, bucket_ref, rb_ref, sk_ref,
             dq_ref, dkv_ref, small_ref, bias_ref, ds_sum_ref, dsink_ref, kcarry_ref, vcarry_ref,
             s_ref, dp_ref, p_ref, dsc_ref):
        n = pl.program_id(0)

        @pl.when(n == 0)
        def _():
            _build_bias(bucket_ref, rb_ref, bias_ref)
            ds_sum_ref[...] = jnp.zeros_like(ds_sum_ref)
            dsink_ref[...] = jnp.zeros_like(dsink_ref)
            kcarry_ref[...] = jnp.zeros_like(kcarry_ref)
            vcarry_ref[...] = jnp.zeros_like(vcarry_ref)

        @pl.when(n < nb)
        def _():
            kvar = _kv_variants(kp_ref, kc_ref)
            vvar = _kv_variants(vp_ref, vc_ref)
            pr, p_sink = _attn_probs(q_ref, kvar, bias_ref, sk_ref, _attn_valid(bucket_ref, n), s_ref)
            for h in range(N_HEADS):
                dp_ref[h] = _dot_nt(do_ref[:, _head_lanes(h)], vvar[h // 4][h % 2])
            dp = dp_ref[...]
            dsum = jnp.sum(pr * dp, axis=-1, keepdims=True)
            ds = pr * (dp - dsum)
            ds_sum_ref[...] += ds
            dsink_ref[...] -= jnp.sum(p_sink * dsum, axis=1, keepdims=True)
            dsc_ref[...] = (ds * (HEAD_DIM ** -0.5)).astype(BF16)
            p_ref[...] = pr.astype(BF16)
            for m in range(N_HEADS // 2):
                dqm = _dot(dsc_ref[2 * m], kvar[m // 2][0]) + _dot(dsc_ref[2 * m + 1], kvar[m // 2][1])
                dq_ref[:, m * LANES:(m + 1) * LANES] = dqm.astype(dq_ref.dtype)
            dk_var = [[None, None], [None, None]]
            dv_var = [[None, None], [None, None]]
            for kvh in range(2):
                for e in range(2):
                    heads = [h for h in range(N_HEADS) if h // 4 == kvh and h % 2 == e]
                    dk_var[kvh][e] = sum(_dot_tn(dsc_ref[h], q_ref[:, _head_lanes(h)]) for h in heads)
                    dv_var[kvh][e] = sum(_dot_tn(p_ref[h], do_ref[:, _head_lanes(h)]) for h in heads)
            dk_cat = _merge_kv_grads(dk_var)
            dv_cat = _merge_kv_grads(dv_var)

            @pl.when(n > 0)
            def _():
                dkv_ref[:, :KV_W] = (kcarry_ref[...] + dk_cat[:BLOCK]).astype(BF16)
                dkv_ref[:, KV_W:] = (vcarry_ref[...] + dv_cat[:BLOCK]).astype(BF16)

            kcarry_ref[...] = dk_cat[BLOCK:]
            vcarry_ref[...] = dv_cat[BLOCK:]

        @pl.when(n == nb)
        def _():
            dkv_ref[:, :KV_W] = kcarry_ref[...].astype(BF16)
            dkv_ref[:, KV_W:] = vcarry_ref[...].astype(BF16)
            bk = bucket_ref[...]
            row = lax.broadcasted_iota(jnp.int32, (N_HEADS, ATTN_SMALL_ROWS, LANES), 1)

            def add(b, acc):
                masked = jnp.where((bk == b)[None], ds_sum_ref[...], 0.0)
                val = jnp.sum(jnp.sum(masked, axis=1, keepdims=True), axis=2, keepdims=True)
                return acc + jnp.where(row == b, val, 0.0)

            small_ref[...] = lax.fori_loop(0, N_BUCKETS, add, jnp.where(row == N_BUCKETS, dsink_ref[...], 0.0))

    last = nb - 1
    cur = lambda w: pl.BlockSpec((BLOCK, w), lambda n: (jnp.minimum(n, last), 0))
    prev = lambda w: pl.BlockSpec((BLOCK, w), lambda n: (jnp.clip(n - 1, 0, last), 0))
    smem = pl.BlockSpec(memory_space=pltpu.SMEM)
    return _hosted_call(
        body, carry, _edge_1d(nb + 1), name="attn_bwd", grid=(nb + 1,),
        in_specs=[cur(ATTN_W), cur(ATTN_W), cur(KV_W), prev(KV_W), cur(KV_W), prev(KV_W),
                  _const_spec((BLOCK, 2 * BLOCK)), smem, smem],
        out_specs=[cur(ATTN_W), prev(2 * KV_W), pl.BlockSpec((N_HEADS, ATTN_SMALL_ROWS, LANES), lambda n: (0, 0, 0))],
        out_shape=[jax.ShapeDtypeStruct((T, ATTN_W), BF16), jax.ShapeDtypeStruct((T, 2 * KV_W), BF16),
                   jax.ShapeDtypeStruct((N_HEADS, ATTN_SMALL_ROWS, LANES), F32)],
        scratch_shapes=[pltpu.VMEM((N_HEADS, BLOCK, 2 * BLOCK), F32), pltpu.VMEM((N_HEADS, BLOCK, 2 * BLOCK), F32),
                        pltpu.VMEM((N_HEADS, 1, 1), F32), pltpu.VMEM((BLOCK, KV_W), F32), pltpu.VMEM((BLOCK, KV_W), F32),
                        pltpu.VMEM((N_HEADS, BLOCK, 2 * BLOCK), F32), pltpu.VMEM((N_HEADS, BLOCK, 2 * BLOCK), F32),
                        pltpu.VMEM((N_HEADS, BLOCK, 2 * BLOCK), BF16), pltpu.VMEM((N_HEADS, BLOCK, 2 * BLOCK), BF16)],
        compiler_params=_params(("arbitrary",)), inputs=(q, datt, k, k, v, v, bucket, rel_bias, sinks))


SCAN_UNROLL = 4


def _cmul(ar, ai, br, bi):
    return ar * br - ai * bi, ar * bi + ai * br


def _cmul_conj(ar, ai, br, bi):
    return ar * br + ai * bi, ar * bi - ai * br


def _ssm_discretize(lr, li, ldt):
    dt = jnp.exp(ldt)
    mag = jnp.exp(lr * dt)
    ab_re = mag * jnp.cos(li * dt)
    ab_im = mag * jnp.sin(li * dt)
    nr = ab_re - 1.0
    den = lr * lr + li * li
    f_re = (nr * lr + ab_im * li) / den
    f_im = (ab_im * lr - nr * li) / den
    return ab_re, ab_im, f_re, f_im


def _ssm_prep(lam_re, lam_im, ldt_rep, bd_re, bd_im):
    def body(lr_ref, li_ref, ldt_ref, bdr_ref, bdi_ref, ar_ref, ai_ref, br_ref, bi_ref):
        ab_re, ab_im, f_re, f_im = _ssm_discretize(lr_ref[...], li_ref[...], ldt_ref[...])
        ar_ref[...] = ab_re
        ai_ref[...] = ab_im
        bdr, bdi = bdr_ref[0], bdi_ref[0]
        br_ref[0] = (bdr * f_re - bdi * f_im).astype(BF16)
        bi_ref[0] = (bdi * f_re + bdr * f_im).astype(BF16)

    row = pl.BlockSpec((1, SSM_LANE_BLOCK), lambda j: (0, j))
    mat = pl.BlockSpec((1, LANES, SSM_LANE_BLOCK), lambda j: (j, 0, 0))
    return pl.pallas_call(
        body, name="ssm_prep", grid=(N_SSM_BLOCKS,),
        in_specs=[row, row, row, mat, mat], out_specs=[row, row, mat, mat],
        out_shape=[jax.ShapeDtypeStruct((1, STATES), F32)] * 2 + [jax.ShapeDtypeStruct((N_SSM_BLOCKS, LANES, SSM_LANE_BLOCK), BF16)] * 2,
        compiler_params=_params(("arbitrary",)),
    )(*_in_hbm(lam_re, lam_im, ldt_rep, bd_re, bd_im))


def _ssm_prep_bwd(lam_re, lam_im, ldt_rep, bd_re, bd_im, dbr, dbi, da_re, da_im):
    def body(lr_ref, li_ref, ldt_ref, bdr_ref, bdi_ref, dbr_ref, dbi_ref, dar_ref, dai_ref,
             dbdr_ref, dbdi_ref, dlr_ref, dli_ref, dldt_ref):
        lr, li, ldt = lr_ref[...], li_ref[...], ldt_ref[...]
        (_, _, f_re, f_im), vjp = jax.vjp(_ssm_discretize, lr, li, ldt)
        bdr, bdi, gbr, gbi = bdr_ref[0], bdi_ref[0], dbr_ref[0], dbi_ref[0]
        dbdr_ref[0] = gbr * f_re + gbi * f_im
        dbdi_ref[0] = gbi * f_re - gbr * f_im
        df_re = jnp.sum(gbr * bdr + gbi * bdi, axis=0, keepdims=True)
        df_im = jnp.sum(gbi * bdr - gbr * bdi, axis=0, keepdims=True)
        dlr, dli, dldt = vjp((dar_ref[...], dai_ref[...], df_re, df_im))
        dlr_ref[...] = dlr
        dli_ref[...] = dli
        dldt_ref[...] = dldt

    row = pl.BlockSpec((1, SSM_LANE_BLOCK), lambda j: (0, j))
    mat = pl.BlockSpec((1, LANES, SSM_LANE_BLOCK), lambda j: (j, 0, 0))
    mat_shape = jax.ShapeDtypeStruct((N_SSM_BLOCKS, LANES, SSM_LANE_BLOCK), F32)
    row_shape = jax.ShapeDtypeStruct((1, STATES), F32)
    return pl.pallas_call(
        body, name="ssm_prep_bwd", grid=(N_SSM_BLOCKS,),
        in_specs=[row, row, row, mat, mat, mat, mat, row, row], out_specs=[mat, mat, row, row, row],
        out_shape=[mat_shape, mat_shape, row_shape, row_shape, row_shape],
        compiler_params=_params(("arbitrary",)),
    )(*_in_hbm(lam_re, lam_im, ldt_rep, bd_re, bd_im, dbr, dbi, da_re, da_im))


def _group_sum(x):
    def body(x_ref, o_ref):
        o_ref[...] = jnp.sum(x_ref[...], axis=1, keepdims=True)
    return pl.pallas_call(body, name="ssm_group_sum", grid=(1,), in_specs=[_whole(x.shape)], out_specs=_whole((N_GROUPS, 1)),
                          out_shape=jax.ShapeDtypeStruct((N_GROUPS, 1), F32))(*_in_hbm(x))


def _power_table(ar, ai, p_re_ref, p_im_ref, steps):
    shape = (SUBLANES, SSM_LANE_BLOCK)
    p_re_ref[0:SUBLANES] = jnp.broadcast_to(ar, shape)
    p_im_ref[0:SUBLANES] = jnp.broadcast_to(ai, shape)
    m = 1
    while m < steps:
        rows = m * SUBLANES
        top_re = p_re_ref[rows - SUBLANES:rows]
        top_im = p_im_ref[rows - SUBLANES:rows]
        cur_re = p_re_ref[0:rows].reshape(m, SUBLANES, SSM_LANE_BLOCK)
        cur_im = p_im_ref[0:rows].reshape(m, SUBLANES, SSM_LANE_BLOCK)
        nxt_re, nxt_im = _cmul(cur_re, cur_im, top_re[None], top_im[None])
        p_re_ref[rows:2 * rows] = nxt_re.reshape(rows, SSM_LANE_BLOCK)
        p_im_ref[rows:2 * rows] = nxt_im.reshape(rows, SSM_LANE_BLOCK)
        m *= 2


def _to_segments(src_ref, dst_ref, steps):
    for s in range(SUBLANES):
        dst_ref[pl.ds(s, steps, stride=SUBLANES), :] = src_ref[s * steps:(s + 1) * steps, :]


def _from_segments(src_ref, dst_ref, steps):
    for s in range(SUBLANES):
        dst_ref[s * steps:(s + 1) * steps, :] = src_ref[pl.ds(s, steps, stride=SUBLANES), :]


def _segment_carries(e_re, e_im, an_re, an_im, c_re, c_im, reverse):
    order = range(SUBLANES - 1, -1, -1) if reverse else range(SUBLANES)
    ins_re, ins_im = [None] * SUBLANES, [None] * SUBLANES
    for s in order:
        ins_re[s], ins_im[s] = c_re, c_im
        pr, pi = _cmul(an_re, an_im, c_re, c_im)
        c_re = e_re[s:s + 1] + pr
        c_im = e_im[s:s + 1] + pi
    return jnp.concatenate(ins_re, axis=0), jnp.concatenate(ins_im, axis=0), c_re, c_im


def _ssm_fwd(u, a_re, a_im, b_re, b_im, c_re, c_im, d_skip, chunk, carry=None):
    T = u.shape[0]
    nc = T // chunk
    steps = chunk // SUBLANES
    blk = SSM_LANE_BLOCK

    def body(u_ref, ar_ref, ai_ref, br_ref, bi_ref, cr_ref, ci_ref, dk_ref,
             y_ref, hr_ref, hi_ref, inr_ref, ini_ref, useg_ref, yseg_ref, pr_ref, pi_ref, carry_ref):
        c = pl.program_id(1)
        ar, ai = ar_ref[...], ai_ref[...]

        @pl.when(c == 0)
        def _():
            _power_table(ar, ai, pr_ref, pi_ref, steps)
            carry_ref[...] = jnp.zeros_like(carry_ref)

        _to_segments(u_ref, useg_ref, steps)
        ub = useg_ref[...].astype(BF16)
        hr_ref[...] = _dot(ub, br_ref[0])
        hi_ref[...] = _dot(ub, bi_ref[0])
        first = slice(0, SUBLANES)

        def scan(t4, prev):
            for j in range(SCAN_UNROLL):
                rows = pl.ds(pl.multiple_of((t4 * SCAN_UNROLL + j) * SUBLANES, SUBLANES), SUBLANES)
                pr, pi = _cmul(pr_ref[first, :], pi_ref[first, :], prev[0], prev[1])
                prev = (pr + hr_ref[rows, :], pi + hi_ref[rows, :])
                hr_ref[rows, :] = prev[0]
                hi_ref[rows, :] = prev[1]
            return prev

        zero = jnp.zeros((SUBLANES, blk), F32)
        lax.fori_loop(0, steps // SCAN_UNROLL, scan, (zero, zero))

        top = slice(chunk - SUBLANES, chunk)
        in_re, in_im, out_re, out_im = _segment_carries(
            hr_ref[top, :], hi_ref[top, :], pr_ref[top, :][0:1], pi_ref[top, :][0:1],
            carry_ref[0:1, :], carry_ref[1:2, :], reverse=False)
        carry_ref[0:1, :] = out_re
        carry_ref[1:2, :] = out_im
        inr_ref[...] = in_re
        ini_ref[...] = in_im

        def fix(t4, _):
            for j in range(SCAN_UNROLL):
                rows = pl.ds(pl.multiple_of((t4 * SCAN_UNROLL + j) * SUBLANES, SUBLANES), SUBLANES)
                fr, fi = _cmul(pr_ref[rows, :], pi_ref[rows, :], in_re, in_im)
                hr_ref[rows, :] += fr
                hi_ref[rows, :] += fi
            return 0

        lax.fori_loop(0, steps // SCAN_UNROLL, fix, 0)

        yseg_ref[...] = _dot(hr_ref[...].astype(BF16), cr_ref[0]) - _dot(hi_ref[...].astype(BF16), ci_ref[0])
        _from_segments(yseg_ref, y_ref, steps)
        y_ref[...] += dk_ref[...] * u_ref[...]

    row = pl.BlockSpec((1, blk), lambda j, c: (0, j))
    b_mat = pl.BlockSpec((1, LANES, blk), lambda j, c: (j, 0, 0))
    c_mat = pl.BlockSpec((1, blk, LANES), lambda j, c: (j, 0, 0))
    tok = pl.BlockSpec((chunk, LANES), lambda j, c: (c, j))
    state = pl.BlockSpec((chunk, blk), lambda j, c: (c, j))
    enter = pl.BlockSpec((SUBLANES, blk), lambda j, c: (c, j))
    return _hosted_call(
        body, carry, _edge_2d(N_SSM_BLOCKS, nc), name="ssm_fwd", grid=(N_SSM_BLOCKS, nc),
        in_specs=[tok, row, row, b_mat, b_mat, c_mat, c_mat, pl.BlockSpec((1, LANES), lambda j, c: (0, j))],
        out_specs=[tok, state, state, enter, enter],
        out_shape=[jax.ShapeDtypeStruct((T, SSM_W), F32), jax.ShapeDtypeStruct((T, STATES), F32),
                   jax.ShapeDtypeStruct((T, STATES), F32), jax.ShapeDtypeStruct((nc * SUBLANES, STATES), F32),
                   jax.ShapeDtypeStruct((nc * SUBLANES, STATES), F32)],
        scratch_shapes=[pltpu.VMEM((chunk, LANES), F32), pltpu.VMEM((chunk, LANES), F32),
                        pltpu.VMEM((chunk, blk), F32), pltpu.VMEM((chunk, blk), F32), pltpu.VMEM((SUBLANES, blk), F32)],
        compiler_params=_params(("arbitrary", "arbitrary"), VMEM_MID),
        inputs=(u, a_re, a_im, b_re, b_im, c_re, c_im, d_skip))


def _ssm_bwd(dy, u, h_re, h_im, in_re, in_im, a_re, a_im, b_re, b_im, c_re, c_im, d_skip, chunk, carry=None):
    T = u.shape[0]
    nc = T // chunk
    steps = chunk // SUBLANES
    blk = SSM_LANE_BLOCK

    def body(dy_ref, u_ref, hr_ref, hi_ref, inr_ref, ini_ref, ar_ref, ai_ref, br_ref, bi_ref, cr_ref, ci_ref, dk_ref,
             du_ref, dbr_ref, dbi_ref, dcr_ref, dci_ref, dar_ref, dai_ref, ddk_ref,
             dyseg_ref, useg_ref, duseg_ref, gr_ref, gi_ref, pr_ref, pi_ref, carry_ref, accr_ref, acci_ref):
        c = pl.program_id(1)
        ar, ai = ar_ref[...], ai_ref[...]

        @pl.when(c == 0)
        def _():
            _power_table(ar, ai, pr_ref, pi_ref, steps)
            carry_ref[...] = jnp.zeros_like(carry_ref)
            accr_ref[...] = jnp.zeros_like(accr_ref)
            acci_ref[...] = jnp.zeros_like(acci_ref)

        _to_segments(dy_ref, dyseg_ref, steps)
        _to_segments(u_ref, useg_ref, steps)
        dyb = dyseg_ref[...].astype(BF16)
        ub = useg_ref[...].astype(BF16)
        gr_ref[...] = _dot_nt(dyb, cr_ref[0])
        gi_ref[...] = -_dot_nt(dyb, ci_ref[0])
        dcr = _dot_tn(hr_ref[...].astype(BF16), dyb)
        dci = -_dot_tn(hi_ref[...].astype(BF16), dyb)
        ddk = jnp.sum(dy_ref[...] * u_ref[...], axis=0, keepdims=True)

        first = slice(0, SUBLANES)

        def scan(k4, nxt):
            for j in range(SCAN_UNROLL):
                t = steps - 1 - (k4 * SCAN_UNROLL + j)
                rows = pl.ds(pl.multiple_of(t * SUBLANES, SUBLANES), SUBLANES)
                pr, pi = _cmul_conj(pr_ref[first, :], pi_ref[first, :], nxt[0], nxt[1])
                nxt = (pr + gr_ref[rows, :], pi + gi_ref[rows, :])
                gr_ref[rows, :] = nxt[0]
                gi_ref[rows, :] = nxt[1]
            return nxt

        top = slice(chunk - SUBLANES, chunk)
        zero = jnp.zeros((SUBLANES, blk), F32)
        lax.fori_loop(0, steps // SCAN_UNROLL, scan, (zero, zero))

        gin_re, gin_im, out_re, out_im = _segment_carries(
            gr_ref[0:SUBLANES, :], gi_ref[0:SUBLANES, :], pr_ref[top, :][0:1], -pi_ref[top, :][0:1],
            carry_ref[0:1, :], carry_ref[1:2, :], reverse=True)
        carry_ref[0:1, :] = out_re
        carry_ref[1:2, :] = out_im

        def fix_row(rows, prow, hp_re, hp_im, acc):
            fr, fi = _cmul_conj(pr_ref[prow, :], pi_ref[prow, :], gin_re, gin_im)
            g_re = gr_ref[rows, :] + fr
            g_im = gi_ref[rows, :] + fi
            gr_ref[rows, :] = g_re
            gi_ref[rows, :] = g_im
            return acc[0] + g_re * hp_re + g_im * hp_im, acc[1] + g_im * hp_re - g_re * hp_im

        def fix_at(t, acc):
            aligned = (lambda r: r * SUBLANES) if isinstance(t, int) else (lambda r: pl.multiple_of(r * SUBLANES, SUBLANES))
            rows, before, prow = (pl.ds(aligned(r), SUBLANES) for r in (t, t - 1, steps - 1 - t))
            return fix_row(rows, prow, hr_ref[before, :], hi_ref[before, :], acc)

        def fix(t4, acc):
            for j in range(SCAN_UNROLL):
                acc = fix_at(t4 * SCAN_UNROLL + j, acc)
            return acc

        acc = fix_row(first, top, inr_ref[...], ini_ref[...], (accr_ref[...], acci_ref[...]))
        for t in range(1, SCAN_UNROLL):
            acc = fix_at(t, acc)
        acc_re, acc_im = lax.fori_loop(1, steps // SCAN_UNROLL, fix, acc)
        accr_ref[...] = acc_re
        acci_ref[...] = acc_im

        gbr = gr_ref[...].astype(BF16)
        gbi = gi_ref[...].astype(BF16)
        duseg_ref[...] = _dot_nt(gbr, br_ref[0]) + _dot_nt(gbi, bi_ref[0])
        _from_segments(duseg_ref, dyseg_ref, steps)
        du_ref[...] = (dyseg_ref[...] + dk_ref[...] * dy_ref[...]).astype(BF16)
        dbr = _dot_tn(ub, gbr)
        dbi = _dot_tn(ub, gbi)

        @pl.when(c == 0)
        def _():
            dbr_ref[0] = dbr
            dbi_ref[0] = dbi
            dcr_ref[0] = dcr
            dci_ref[0] = dci
            ddk_ref[...] = ddk

        @pl.when(c > 0)
        def _():
            dbr_ref[0] += dbr
            dbi_ref[0] += dbi
            dcr_ref[0] += dcr
            dci_ref[0] += dci
            ddk_ref[...] += ddk

        @pl.when(c == nc - 1)
        def _():
            dar_ref[...] = jnp.sum(acc_re, axis=0, keepdims=True)
            dai_ref[...] = jnp.sum(acc_im, axis=0, keepdims=True)

    rev = lambda c: nc - 1 - c
    row = pl.BlockSpec((1, blk), lambda j, c: (0, j))
    b_mat = pl.BlockSpec((1, LANES, blk), lambda j, c: (j, 0, 0))
    c_mat = pl.BlockSpec((1, blk, LANES), lambda j, c: (j, 0, 0))
    tok = pl.BlockSpec((chunk, LANES), lambda j, c: (rev(c), j))
    state = pl.BlockSpec((chunk, blk), lambda j, c: (rev(c), j))
    enter = pl.BlockSpec((SUBLANES, blk), lambda j, c: (rev(c), j))
    chan = pl.BlockSpec((1, LANES), lambda j, c: (0, j))
    f32 = lambda *s: jax.ShapeDtypeStruct(s, F32)
    return _hosted_call(
        body, carry, _edge_2d(N_SSM_BLOCKS, nc), name="ssm_bwd", grid=(N_SSM_BLOCKS, nc),
        in_specs=[tok, tok, state, state, enter, enter, row, row, b_mat, b_mat, c_mat, c_mat, chan],
        out_specs=[tok, b_mat, b_mat, c_mat, c_mat, row, row, chan],
        out_shape=[jax.ShapeDtypeStruct((T, SSM_W), BF16), f32(N_SSM_BLOCKS, LANES, blk), f32(N_SSM_BLOCKS, LANES, blk),
                   f32(N_SSM_BLOCKS, blk, LANES), f32(N_SSM_BLOCKS, blk, LANES), f32(1, STATES), f32(1, STATES), f32(1, SSM_W)],
        scratch_shapes=[pltpu.VMEM((chunk, LANES), F32), pltpu.VMEM((chunk, LANES), F32), pltpu.VMEM((chunk, LANES), F32),
                        pltpu.VMEM((chunk, blk), F32), pltpu.VMEM((chunk, blk), F32),
                        pltpu.VMEM((chunk, blk), F32), pltpu.VMEM((chunk, blk), F32),
                        pltpu.VMEM((SUBLANES, blk), F32), pltpu.VMEM((SUBLANES, blk), F32), pltpu.VMEM((SUBLANES, blk), F32)],
        compiler_params=_params(("arbitrary", "arbitrary"), VMEM_BIG),
        inputs=(dy, u, h_re, h_im, in_re, in_im, a_re, a_im, b_re, b_im, c_re, c_im, d_skip))


def _merge_forward(y, att, ga, gs, w_glu, w_ssm, w_attn):
    z = jax.nn.gelu(y)
    zb = z.astype(BF16)
    gl = jax.nn.sigmoid(_dot(zb, w_glu))
    z2b = (z * gl).astype(BF16)
    y_ssm = _dot(z2b, w_ssm)
    y_attn = _dot(att, w_attn)
    sa = jax.nn.sigmoid(ga)
    ss = jax.nn.sigmoid(gs)
    merged = (sa * y_attn + ss * y_ssm).astype(BF16)
    return z, zb, gl, z2b, y_ssm, y_attn, sa, ss, merged


def _merge_fwd(x, y, att, ga, gs, g2, g3, w_glu, w_ssm, w_attn, w_out, tile):
    T = x.shape[0]

    def body(x_ref, y_ref, att_ref, ga_ref, gs_ref, g2_ref, g3_ref, wg_ref, ws_ref, wa_ref, wo_ref, x1_ref, o_ref, h2_ref):
        merged = _merge_forward(y_ref[...], att_ref[...], ga_ref[...], gs_ref[...], wg_ref[...], ws_ref[...], wa_ref[...])[-1]
        o = _dot(merged, wo_ref[...])
        x1 = x_ref[...] + o * _rms_scale(o) * g2_ref[...]
        o_ref[...] = o
        x1_ref[...] = x1
        h2_ref[...] = (x1 * _rms_scale(x1) * g3_ref[...]).astype(BF16)

    tok = lambda w: pl.BlockSpec((tile, w), lambda i: (i, 0))
    vec = _const_spec((1, D_MODEL))
    return pl.pallas_call(
        body, name="merge_fwd", grid=(T // tile,),
        in_specs=[tok(D_MODEL), tok(SSM_W), tok(ATTN_W), tok(D_MODEL), tok(D_MODEL), vec, vec,
                  _const_spec((SSM_W, SSM_W)), _const_spec((SSM_W, D_MODEL)), _const_spec((ATTN_W, D_MODEL)),
                  _const_spec((D_MODEL, D_MODEL))],
        out_specs=[tok(D_MODEL), tok(D_MODEL), tok(D_MODEL)],
        out_shape=_hbm_out([jax.ShapeDtypeStruct((T, D_MODEL), F32), jax.ShapeDtypeStruct((T, D_MODEL), F32),
                            jax.ShapeDtypeStruct((T, D_MODEL), BF16)]),
        compiler_params=_params(("arbitrary",), VMEM_MID),
    )(*_in_hbm(x, y, att, ga, gs, g2, g3, w_glu, w_ssm, w_attn, w_out))


def _merge_bwd(dh2, dx2, x1, o, y, att, ga, gs, g2, g3, w_glu, w_ssm, w_attn, w_out, tile, carry=None):
    T = x1.shape[0]
    n_steps = T // tile

    group = min(2, n_steps)
    staged_widths = (D_MODEL, D_MODEL, ATTN_W, D_MODEL, SSM_W, D_MODEL, SSM_W, SSM_W)

    def body(dh2_ref, dx2_ref, x1_ref, o_ref, y_ref, att_ref, ga_ref, gs_ref, g2_ref, g3_ref, wg_ref, ws_ref, wa_ref, wo_ref,
             dx1_ref, dgates_ref, datt_ref, dy_ref, dwg_hbm, dws_hbm, dwa_hbm, dwo_hbm, dg2_ref, dg3_ref,
             awg_ref, aws_ref, awa_ref, awo_ref, *staged):
        i = pl.program_id(0)
        x1v, ov = x1_ref[...], o_ref[...]
        dxn, dg3 = _rms_bwd(dh2_ref[...], x1v, _rms_scale(x1v), g3_ref[...])
        dx1 = dx2_ref[...] + dxn
        dx1_ref[...] = dx1
        do, dg2 = _rms_bwd(dx1, ov, _rms_scale(ov), g2_ref[...])
        dob = do.astype(BF16)

        yv = y_ref[...]
        att = att_ref[...]
        z, zb, gl, z2b, y_ssm, y_attn, sa, ss, merged = _merge_forward(
            yv, att, ga_ref[...], gs_ref[...], wg_ref[...], ws_ref[...], wa_ref[...])
        dmerged = _dot_nt(dob, wo_ref[...])
        dya = (dmerged * sa).astype(BF16)
        dys = (dmerged * ss).astype(BF16)
        dgates_ref[:, :D_MODEL] = (dmerged * y_attn * sa * (1.0 - sa)).astype(BF16)
        dgates_ref[:, D_MODEL:] = (dmerged * y_ssm * ss * (1.0 - ss)).astype(BF16)
        datt_ref[...] = _dot_nt(dya, wa_ref[...]).astype(BF16)
        dz2 = _dot_nt(dys, ws_ref[...])
        dpre = (dz2 * z * gl * (1.0 - gl)).astype(BF16)
        dz = dz2 * gl + _dot_nt(dpre, wg_ref[...])
        _, gelu_vjp = jax.vjp(jax.nn.gelu, yv)
        dy_ref[...] = gelu_vjp(dz)[0]

        part = pl.ds(pl.multiple_of((i % group) * tile, tile), tile)
        for ref, val in zip(staged, (merged, dob, att, dya, z2b, dys, zb, dpre)):
            ref[part, :] = val

        @pl.when(i == 0)
        def _():
            dg2_ref[...] = dg2
            dg3_ref[...] = dg3

        @pl.when(i > 0)
        def _():
            dg2_ref[...] += dg2
            dg3_ref[...] += dg3

        def weight_grads():
            s_merged, s_dob, s_att, s_dya, s_z2b, s_dys, s_zb, s_dpre = (ref[...] for ref in staged)
            return ((awo_ref, _dot_tn(s_merged, s_dob)), (awa_ref, _dot_tn(s_att, s_dya)),
                    (aws_ref, _dot_tn(s_z2b, s_dys)), (awg_ref, _dot_tn(s_zb, s_dpre)))

        @pl.when(i == group - 1)
        def _():
            for ref, val in weight_grads():
                ref[...] = val

        @pl.when((i % group == group - 1) & (i > group - 1))
        def _():
            for ref, val in weight_grads():
                ref[...] += val

        @pl.when(i == n_steps - 1)
        def _():
            pltpu.sync_copy(awg_ref, dwg_hbm)
            pltpu.sync_copy(aws_ref, dws_hbm)
            pltpu.sync_copy(awa_ref, dwa_hbm)
            pltpu.sync_copy(awo_ref, dwo_hbm)

    tok = lambda w: pl.BlockSpec((tile, w), lambda i: (i, 0))
    vec = _const_spec((1, D_MODEL))
    any_ = pl.BlockSpec(memory_space=pl.ANY)
    vec_out = pl.BlockSpec((1, D_MODEL), lambda i: (0, 0))
    f32 = lambda *s: jax.ShapeDtypeStruct(s, F32)
    bf = lambda *s: jax.ShapeDtypeStruct(s, BF16)
    return _hosted_call(
        body, carry, _edge_1d(n_steps), name="merge_bwd", grid=(n_steps,),
        in_specs=[tok(D_MODEL), tok(D_MODEL), tok(D_MODEL), tok(D_MODEL), tok(SSM_W), tok(ATTN_W), tok(D_MODEL), tok(D_MODEL),
                  vec, vec, _const_spec((SSM_W, SSM_W)), _const_spec((SSM_W, D_MODEL)), _const_spec((ATTN_W, D_MODEL)),
                  _const_spec((D_MODEL, D_MODEL))],
        out_specs=[tok(D_MODEL), tok(2 * D_MODEL), tok(ATTN_W), tok(SSM_W), any_, any_, any_, any_, vec_out, vec_out],
        out_shape=[f32(T, D_MODEL), bf(T, 2 * D_MODEL), bf(T, ATTN_W), f32(T, SSM_W),
                   f32(SSM_W, SSM_W), f32(SSM_W, D_MODEL), f32(ATTN_W, D_MODEL), f32(D_MODEL, D_MODEL),
                   f32(1, D_MODEL), f32(1, D_MODEL)],
        scratch_shapes=[pltpu.VMEM((SSM_W, SSM_W), F32), pltpu.VMEM((SSM_W, D_MODEL), F32),
                        pltpu.VMEM((ATTN_W, D_MODEL), F32), pltpu.VMEM((D_MODEL, D_MODEL), F32)]
        + [pltpu.VMEM((group * tile, wd), BF16) for wd in staged_widths],
        compiler_params=_params(("arbitrary",), VMEM_BIG),
        inputs=(dh2, dx2, x1, o, y, att, ga, gs, g2, g3, w_glu, w_ssm, w_attn, w_out))


FF_SHARD = D_FF // N_DEV


def _mlp_fwd(h2, x1, target, g4, w_ff_in, w_ff_out, tile):
    T = h2.shape[0]
    col_chunk = 2 * FF_SHARD

    def body(h2_ref, x1_ref, tg_ref, g4_ref, wi_ref, wo_ref, a_ref, dfo_ref, dx2_ref, loss_ref, dg4_ref, rr_ref):
        i = pl.program_id(0)
        h2v = h2_ref[...]
        for c in range(D_FF // col_chunk):
            cols = slice(c * col_chunk, (c + 1) * col_chunk)
            a = _dot_nt(h2v, wi_ref[cols, :])
            a_ref[:, cols] = a.astype(BF16)
            ra = jnp.maximum(a, 0.0)
            rr_ref[:, cols] = (ra * ra).astype(BF16)
        f = _dot(rr_ref[...], wo_ref[...])
        r = _rms_scale(f)
        g = g4_ref[...]
        err = x1_ref[...] + f * r * g - tg_ref[...]
        dx2 = err * (1.0 / D_MODEL)
        dx2_ref[...] = dx2
        dfo, dg = _rms_bwd(dx2, f, r, g)
        dfo_ref[...] = dfo.astype(BF16)
        row = lax.broadcasted_iota(jnp.int32, (SUBLANES, LANES), 0)
        col = lax.broadcasted_iota(jnp.int32, (SUBLANES, LANES), 1)
        loss = jnp.where((row == 0) & (col == 0), (0.5 / D_MODEL) * jnp.sum(err * err), 0.0)

        @pl.when(i == 0)
        def _():
            loss_ref[...] = loss
            dg4_ref[...] = dg

        @pl.when(i > 0)
        def _():
            loss_ref[...] += loss
            dg4_ref[...] += dg

    tok = pl.BlockSpec((tile, D_MODEL), lambda i: (i, 0))
    return pl.pallas_call(
        body, name="mlp_fwd", grid=(T // tile,),
        in_specs=[tok, tok, tok, _const_spec((1, D_MODEL)), _const_spec((D_FF, D_MODEL)), _const_spec((D_FF, D_MODEL))],
        out_specs=[pl.BlockSpec((tile, D_FF), lambda i: (i, 0)), tok, tok,
                   pl.BlockSpec((SUBLANES, LANES), lambda i: (0, 0)), pl.BlockSpec((1, D_MODEL), lambda i: (0, 0))],
        out_shape=_hbm_out([jax.ShapeDtypeStruct((T, D_FF), BF16), jax.ShapeDtypeStruct((T, D_MODEL), BF16),
                            jax.ShapeDtypeStruct((T, D_MODEL), F32), jax.ShapeDtypeStruct((SUBLANES, LANES), F32),
                            jax.ShapeDtypeStruct((1, D_MODEL), F32)]),
        scratch_shapes=[pltpu.VMEM((tile, D_FF), BF16)],
        compiler_params=_params(("arbitrary",), VMEM_MAX),
    )(*_in_hbm(h2, x1, target, g4, w_ff_in.reshape(D_FF, D_MODEL), w_ff_out.reshape(D_FF, D_MODEL)))


def _mlp_weight_grads(dfo, a, h2, w_ff_out, row_chunk):
    T = h2.shape[0]

    def body(dfo_ref, h2_ref, a_ref, wo_ref, dwi_ref, dwo_ref, da_ref, rr_ref):
        def rows(r, _):
            sl = pl.ds(pl.multiple_of(r * row_chunk, row_chunk), row_chunk)
            ra = jnp.maximum(a_ref[sl, :].astype(F32), 0.0)
            da_ref[sl, :] = (_dot_nt(dfo_ref[sl, :], wo_ref[0]) * (2.0 * ra)).astype(BF16)
            rr_ref[sl, :] = (ra * ra).astype(BF16)
            return 0

        lax.fori_loop(0, T // row_chunk, rows, 0)
        dwo_ref[0] = _dot_tn(rr_ref[...], dfo_ref[...])
        dwi_ref[0] = _dot_tn(h2_ref[...], da_ref[...])

    return pl.pallas_call(
        body, name="mlp_weight_grads", grid=(N_DEV,),
        in_specs=[_const_spec((T, D_MODEL)), _const_spec((T, D_MODEL)), pl.BlockSpec((T, FF_SHARD), lambda k: (0, k)),
                  pl.BlockSpec((1, FF_SHARD, D_MODEL), lambda k: (k, 0, 0))],
        out_specs=[pl.BlockSpec((1, D_MODEL, FF_SHARD), lambda k: (k, 0, 0)),
                   pl.BlockSpec((1, FF_SHARD, D_MODEL), lambda k: (k, 0, 0)), pl.BlockSpec((T, FF_SHARD), lambda k: (0, k))],
        out_shape=_hbm_out([jax.ShapeDtypeStruct((N_DEV, D_MODEL, FF_SHARD), F32),
                            jax.ShapeDtypeStruct((N_DEV, FF_SHARD, D_MODEL), F32), jax.ShapeDtypeStruct((T, D_FF), BF16)]),
        scratch_shapes=[pltpu.VMEM((T, FF_SHARD), BF16)],
        compiler_params=_params(("arbitrary",), VMEM_MAX),
    )(*_in_hbm(dfo, h2, a, w_ff_out))


def _mlp_input_grad(da, w_ff_in_t, tile):
    T = da.shape[0]

    def body(da_ref, w_ref, o_ref):
        o_ref[...] = _dot(da_ref[...], w_ref[...])

    return pl.pallas_call(
        body, name="mlp_input_grad", grid=(T // tile,),
        in_specs=[pl.BlockSpec((tile, D_FF), lambda i: (i, 0)), _const_spec((D_FF, D_MODEL))],
        out_specs=pl.BlockSpec((tile, D_MODEL), lambda i: (i, 0)),
        out_shape=_hbm_out(jax.ShapeDtypeStruct((T, D_MODEL), F32)),
        compiler_params=_params(("arbitrary",), VMEM_MID),
    )(*_in_hbm(da, w_ff_in_t))


def _block_diag_in(b):
    bt = b.reshape(N_SSM_BLOCKS, GROUPS_PER_BLOCK, GROUP_CH, N_STATE)
    eye = jnp.eye(GROUPS_PER_BLOCK, dtype=b.dtype)
    return jnp.einsum("jacp,ab->jacbp", bt, eye).reshape(N_SSM_BLOCKS, LANES, SSM_LANE_BLOCK)


def _block_diag_in_grad(g):
    g = g.reshape(N_SSM_BLOCKS, GROUPS_PER_BLOCK, GROUP_CH, GROUPS_PER_BLOCK, N_STATE)
    d = jnp.diagonal(g, axis1=1, axis2=3)
    return jnp.transpose(d, (0, 3, 1, 2)).reshape(N_GROUPS, GROUP_CH, N_STATE)


def _block_diag_out(c):
    ct = c.reshape(N_SSM_BLOCKS, GROUPS_PER_BLOCK, GROUP_CH, N_STATE)
    eye = jnp.eye(GROUPS_PER_BLOCK, dtype=c.dtype)
    return jnp.einsum("jacp,ab->japbc", ct, eye).reshape(N_SSM_BLOCKS, SSM_LANE_BLOCK, LANES)


def _block_diag_out_grad(g):
    g = g.reshape(N_SSM_BLOCKS, GROUPS_PER_BLOCK, N_STATE, GROUPS_PER_BLOCK, GROUP_CH)
    d = jnp.diagonal(g, axis1=1, axis2=3)
    return jnp.transpose(d, (0, 3, 2, 1)).reshape(N_GROUPS, GROUP_CH, N_STATE)


def _tiles(T):
    return dict(proj=min(512, T), proj_bwd=min(512, T // 2), merge=min(512, T), merge_bwd=min(256, T),
                mlp_fwd=min(512, T), mlp_bwd=min(512, T), ssm_chunk=min(1024, T))


def _mesh_position():
    x, y, c = lax.axis_index("x"), lax.axis_index("y"), lax.axis_index("c")
    other_chips = [(1 - x, y), (x, 1 - y), (1 - x, 1 - y)]
    return x, y, c, other_chips


def _gather_carry(arrays):
    n = len(arrays)

    def copies(ins, outs, sems):
        send_sems, recv_sems, local_sems = sems
        x, y, c, chips = _mesh_position()
        me, sibling = (x, y, c), (x, y, 1 - c)

        def copy(a, k, block, to, src=None):
            px, py, pc = block
            dst = outs[a].at[4 * px + 2 * py + pc]
            return pltpu.make_async_remote_copy(
                src_ref=dst if src is None else src, dst_ref=dst, send_sem=send_sems.at[7 * a + k],
                recv_sem=recv_sems.at[7 * a + k], device_id=to, device_id_type=MESH_IDS)

        mine = [pltpu.make_async_copy(ins[a], outs[a].at[4 * x + 2 * y + c], local_sems.at[a]) for a in range(n)]
        first = []
        for a in range(n):
            first.append(copy(a, 0, me, sibling, src=ins[a]))
            first += [copy(a, 1 + j, me, (*chip, c), src=ins[a]) for j, chip in enumerate(chips)]
        return copy, mine, first, me, sibling, chips, c

    def start(ins, outs, sems):
        _, mine, first, *_ = copies(ins, outs, sems)
        for cp in mine + first:
            cp.start()

    def finish(ins, outs, sems):
        copy, mine, first, me, sibling, chips, c = copies(ins, outs, sems)
        passed = []
        for a in range(n):
            for j, chip in enumerate(chips):
                copy(a, 1 + j, (*chip, c), me).wait_recv()
                passed.append(copy(a, 4 + j, (*chip, c), sibling))
                passed[-1].start()
        for a in range(n):
            copy(a, 0, sibling, me).wait_recv()
            for j, chip in enumerate(chips):
                copy(a, 4 + j, (*chip, 1 - c), me).wait_recv()
        for cp in first + passed:
            cp.wait_send()
        for cp in mine:
            cp.wait()

    return _Carry(arrays, [jax.ShapeDtypeStruct((N_DEV,) + a.shape, a.dtype) for a in arrays],
                  [pltpu.SemaphoreType.DMA((7 * n,)), pltpu.SemaphoreType.DMA((7 * n,)), pltpu.SemaphoreType.DMA((n,))],
                  start, finish)


def _pairwise_carry(arrays, n_slots, make_copies):
    n = len(arrays)

    def start(ins, outs, sems):
        for cp in make_copies(ins, outs, sems):
            cp.start()

    def finish(ins, outs, sems):
        for cp in make_copies(ins, outs, sems):
            cp.wait()

    return _Carry(arrays, [jax.ShapeDtypeStruct((n_slots,) + a.shape[1:], a.dtype) for a in arrays],
                  [pltpu.SemaphoreType.DMA((n_slots * n,)), pltpu.SemaphoreType.DMA((n_slots * n,))], start, finish)


def _sibling_carry(grads):
    def make_copies(ins, outs, sems):
        x, y, c, _ = _mesh_position()
        return [pltpu.make_async_remote_copy(
            src_ref=ins[a].at[2 * ch + (1 - c)], dst_ref=outs[a].at[ch], send_sem=sems[0].at[4 * a + ch],
            recv_sem=sems[1].at[4 * a + ch], device_id=(x, y, 1 - c), device_id_type=MESH_IDS)
            for a in range(len(grads)) for ch in range(4)]

    return _pairwise_carry(grads, 4, make_copies)


def _chips_carry(sums):
    def make_copies(ins, outs, sems):
        x, y, c, chips = _mesh_position()
        return [pltpu.make_async_remote_copy(
            src_ref=ins[a].at[2 * px + py], dst_ref=outs[a].at[j], send_sem=sems[0].at[3 * a + j],
            recv_sem=sems[1].at[3 * a + j], device_id=(px, py, c), device_id_type=MESH_IDS)
            for a in range(len(sums)) for j, (px, py) in enumerate(chips)]

    return _pairwise_carry(sums, 3, make_copies)


def _everyone_carry(arrays):
    def make_copies(ins, outs, sems):
        x, y, c, _ = _mesh_position()
        flip = lambda v, bit: 1 - v if bit else v
        return [pltpu.make_async_remote_copy(
            src_ref=ins[a], dst_ref=outs[a].at[r - 1], send_sem=sems[0].at[7 * a + r - 1], recv_sem=sems[1].at[7 * a + r - 1],
            device_id=(flip(x, r & 4), flip(y, r & 2), flip(c, r & 1)), device_id_type=MESH_IDS)
            for a in range(len(arrays)) for r in range(1, N_DEV)]

    carry = _pairwise_carry([jax.ShapeDtypeStruct((1,) + a.shape, a.dtype) for a in arrays], N_DEV - 1, make_copies)
    carry.inputs = list(arrays)
    return carry


def _sum_everyone(own, received, me, name, after=()):
    def body(me_ref, own_ref, r_ref, *refs):
        g = None
        for d in range(N_DEV):
            relation = jnp.bitwise_xor(d, me_ref[0])
            part = jnp.where(relation == 0, own_ref[...], r_ref[jnp.maximum(relation - 1, 0)])
            g = part if g is None else g + part
        refs[-1][...] = g

    whole = lambda shape: pl.BlockSpec(shape, lambda i, me_ref: (0,) * len(shape))
    return pl.pallas_call(
        body, name=name,
        grid_spec=pltpu.PrefetchScalarGridSpec(
            num_scalar_prefetch=1, grid=(1,), in_specs=[whole(own.shape), whole(received.shape)] + [HBM_SPEC] * len(after),
            out_specs=whole(own.shape)),
        out_shape=jax.ShapeDtypeStruct(own.shape, F32))(me, *_in_hbm(own, received), *after)


SEM_SPEC = pl.BlockSpec(memory_space=pltpu.SEMAPHORE)
DATAFLOW_EFFECT = pltpu.SideEffectType.DATAFLOW_SIDE_EFFECTING


def _exchange_start(carry, name, after=()):
    n = len(carry.inputs)
    lands = [lax.empty(s.shape, s.dtype) for s in carry.out_shapes]

    def body(*refs):
        first_out = 2 * n + len(after)
        srcs, zones, sems, token = refs[:n], refs[n:2 * n], refs[first_out:first_out + 2], refs[-1]
        carry.start(srcs, zones, sems)
        token[...] = jnp.zeros_like(token)

    outs = pl.pallas_call(
        body, name=name, in_specs=[HBM_SPEC] * (2 * n + len(after)),
        out_specs=[SEM_SPEC, SEM_SPEC] + [HBM_SPEC] * (2 * n) + [pl.BlockSpec(memory_space=pltpu.VMEM)],
        out_shape=list(carry.sems) + _hbm_out([jax.ShapeDtypeStruct(a.shape, a.dtype) for a in carry.inputs])
        + _hbm_out(carry.out_shapes) + [jax.ShapeDtypeStruct((SUBLANES, LANES), F32)],
        input_output_aliases={j: 2 + j for j in range(2 * n)},
        compiler_params=pltpu.CompilerParams(has_side_effects=DATAFLOW_EFFECT),
    )(*_in_hbm(*carry.inputs, *lands), *after)
    return outs[:-1], outs[-1]


def _exchange_wait(carry, in_flight, after, name):
    n = len(carry.inputs)
    sems, srcs, zones = in_flight[:2], in_flight[2:2 + n], in_flight[2 + n:]

    def body(*refs):
        src_refs, zone_refs, sem_refs = refs[:n], refs[n:2 * n], refs[2 * n:2 * n + 2]
        carry.finish(src_refs, zone_refs, sem_refs)

    outs = pl.pallas_call(
        body, name=name, in_specs=[HBM_SPEC] * (2 * n) + [SEM_SPEC, SEM_SPEC] + [HBM_SPEC] * len(after),
        out_specs=[HBM_SPEC] * (2 * n),
        out_shape=_hbm_out([jax.ShapeDtypeStruct(a.shape, a.dtype) for a in carry.inputs]) + _hbm_out(carry.out_shapes),
        input_output_aliases={j: j for j in range(2 * n)},
        compiler_params=pltpu.CompilerParams(has_side_effects=DATAFLOW_EFFECT),
    )(*srcs, *zones, *sems, *after)
    return list(outs[:n]), list(outs[n:])


def _add_sibling(grads8, recvs, core, row_tiles, name):
    k = len(grads8)
    g4 = [g.reshape(4, 2, *g.shape[1:]) for g in grads8]

    def body(core_ref, *refs):
        g_refs, r_refs, o_refs, ob_refs = (refs[j * k:(j + 1) * k] for j in range(4))
        for g_ref, r_ref, o_ref, ob_ref in zip(g_refs, r_refs, o_refs, ob_refs):
            s = g_ref[0] + r_ref[...]
            o_ref[...] = s
            ob_ref[...] = s.astype(BF16)

    def blocks(make):
        return [make(g.shape[1] // row_tiles, g.shape[2]) for g in grads8]

    slot = lambda tr, C: pl.BlockSpec((1, tr, C), lambda ch, r, core_ref: (ch, r, 0))
    outs = pl.pallas_call(
        body, name=name,
        grid_spec=pltpu.PrefetchScalarGridSpec(
            num_scalar_prefetch=1, grid=(4, row_tiles),
            in_specs=blocks(lambda tr, C: pl.BlockSpec((1, 1, tr, C), lambda ch, r, core_ref: (ch, core_ref[0], r, 0)))
            + blocks(slot), out_specs=blocks(slot) + blocks(slot)),
        out_shape=_hbm_out([jax.ShapeDtypeStruct((4,) + g.shape[1:], F32) for g in grads8]
                           + [jax.ShapeDtypeStruct((4,) + g.shape[1:], BF16) for g in grads8]),
        compiler_params=_params(("arbitrary", "arbitrary")),
    )(core, *_in_hbm(*g4, *recvs))
    return list(outs[:k]), list(outs[k:])


def _adam_math(w, g, m, v):
    m = ADAM_B1 * m + (1.0 - ADAM_B1) * g
    v = ADAM_B2 * v + (1.0 - ADAM_B2) * jnp.square(g)
    m_hat = m / (1.0 - ADAM_B1 ** ADAM_STEP)
    v_hat = v / (1.0 - ADAM_B2 ** ADAM_STEP)
    delta = -ADAM_LR * (m_hat / (jnp.sqrt(v_hat) + ADAM_EPS) + ADAM_WD * w)
    return delta, m, v


def _adam_big(ws, ms, vs, chip_sums, recvs, chip, row_tiles, name, after=()):
    k = len(ws)

    def body(chip_ref, *refs):
        refs = refs[:5 * k] + refs[5 * k + len(after):]
        w_refs, m_refs, v_refs, s_refs, r_refs, g_refs, d_refs, nm_refs, nv_refs = (refs[j * k:(j + 1) * k] for j in range(9))
        for a in range(k):
            r_ref = r_refs[a]
            g = s_refs[a][0] + r_ref[0].astype(F32) + r_ref[1].astype(F32) + r_ref[2].astype(F32)
            g_refs[a][...] = g
            d_refs[a][...], nm_refs[a][...], nv_refs[a][...] = _adam_math(w_refs[a][...], g, m_refs[a][...], v_refs[a][...])

    def blocks(make):
        return [make(w.shape[0] // row_tiles, w.shape[1]) for w in ws]

    blk = lambda tr, C: pl.BlockSpec((tr, C), lambda r, chip_ref: (r, 0))
    outs = pl.pallas_call(
        body, name=name,
        grid_spec=pltpu.PrefetchScalarGridSpec(
            num_scalar_prefetch=1, grid=(row_tiles,),
            in_specs=blocks(blk) * 3 + blocks(lambda tr, C: pl.BlockSpec((1, tr, C), lambda r, chip_ref: (chip_ref[0], r, 0)))
            + blocks(lambda tr, C: pl.BlockSpec((3, tr, C), lambda r, chip_ref: (0, r, 0))) + [HBM_SPEC] * len(after),
            out_specs=blocks(blk) * 4),
        out_shape=[jax.ShapeDtypeStruct(w.shape, F32) for w in ws] * 4,
        compiler_params=_params(("arbitrary",)),
    )(chip, *_in_hbm(*ws, *ms, *vs, *chip_sums, *recvs), *after)
    return [list(outs[j * k:(j + 1) * k]) for j in range(4)]


def _sum_partials(partials, name, after=()):
    def body(p_ref, *refs):
        g = p_ref[0]
        for d in range(1, partials.shape[0]):
            g = g + p_ref[d]
        refs[-1][...] = g

    return pl.pallas_call(body, name=name, grid=(1,), in_specs=[_whole(partials.shape)] + [HBM_SPEC] * len(after),
                          out_specs=_whole(partials.shape[1:]),
                          out_shape=jax.ShapeDtypeStruct(partials.shape[1:], F32))(*_in_hbm(partials), *after)


def _adam_small(ws, ms, vs, gs):
    n = len(ws)

    def body(*refs):
        w_refs, m_refs, v_refs, g_refs = (refs[i * n:(i + 1) * n] for i in range(4))
        d_refs, nm_refs, nv_refs = (refs[(4 + i) * n:(5 + i) * n] for i in range(3))
        for j in range(n):
            d_refs[j][...], nm_refs[j][...], nv_refs[j][...] = _adam_math(
                w_refs[j][...], g_refs[j][...], m_refs[j][...], v_refs[j][...])

    specs = [_whole(w.shape) for w in ws]
    outs = pl.pallas_call(body, name="adam_small", grid=(1,), in_specs=specs * 4, out_specs=specs * 3,
                          out_shape=[jax.ShapeDtypeStruct(w.shape, F32) for w in ws] * 3,
                          compiler_params=_params(("arbitrary",), VMEM_MID))(*_in_hbm(*ws, *ms, *vs, *gs))
    return outs[:n], outs[n:2 * n], outs[2 * n:]


PACK_QUANTUM = SUBLANES * LANES


def _pack(named, names):
    parts = []
    for nme in names:
        flat = named[nme].reshape(-1)
        parts.append(jnp.pad(flat, (0, -flat.size % PACK_QUANTUM)))
    return jnp.concatenate(parts).reshape(-1, LANES)


def _unpack(packed, shapes, names):
    flat = packed.reshape(-1)
    out, pos = {}, 0
    for nme in names:
        size = math.prod(shapes[nme])
        out[nme] = flat[pos:pos + size].reshape(shapes[nme])
        pos += size + (-size % PACK_QUANTUM)
    return out


BIG = ("w_in", "w_glu", "w_attn_branch", "w_ssm_branch", "w_out", "w_ff_in", "w_ff_out")
COLUMN_SHARDED = ("w_in", "w_attn_branch", "w_ssm_branch", "w_ff_in")
SMALL = ("norm_mix_pre", "norm_mix_post", "norm_mlp_pre", "norm_mlp_post", "rel_bias", "sinks", "lam_re", "lam_im",
         "log_dt", "b_re", "b_im", "c_re", "c_im", "d_skip")
SWAPPED_SMALL = ("rel_bias", "b_re", "b_im")
SMALL_LATE = ("norm_mix_pre", "rel_bias", "sinks", "loss")
SMALL_BEFORE_ATTN_BWD = tuple(n for n in SMALL if n not in SMALL_LATE)
ALL_WEIGHTS = ("norm_mix_pre", "norm_mix_post", "norm_mlp_pre", "norm_mlp_post", "w_in", "rel_bias", "sinks", "lam_re",
               "lam_im", "log_dt", "b_re", "b_im", "c_re", "c_im", "d_skip", "w_glu", "w_attn_branch", "w_ssm_branch",
               "w_out", "w_ff_in", "w_ff_out")


def _full_from_gathered(name, gathered):
    _, r, c = gathered.shape
    if name in COLUMN_SHARDED:
        return jnp.transpose(gathered, (1, 0, 2)).reshape(r, N_DEV * c)
    return gathered.reshape(N_DEV * r, c)


def _blocks_from_full(name, full):
    r, c = full.shape
    if name in COLUMN_SHARDED:
        return jnp.transpose(full.reshape(r, N_DEV, c // N_DEV), (1, 0, 2))
    return full.reshape(N_DEV, r // N_DEV, c)


def kernel(x, norm_mix_pre, norm_mix_post, norm_mlp_pre, norm_mlp_post, w_in, rel_bias, sinks, lam_re, lam_im, log_dt, b_re, b_im, c_re, c_im, d_skip, w_glu, w_attn_branch, w_ssm_branch, w_out, w_ff_in, w_ff_out, loss_target, m_norm_mix_pre, m_norm_mix_post, m_norm_mlp_pre, m_norm_mlp_post, m_w_in, m_rel_bias, m_sinks, m_lam_re, m_lam_im, m_log_dt, m_b_re, m_b_im, m_c_re, m_c_im, m_d_skip, m_w_glu, m_w_attn_branch, m_w_ssm_branch, m_w_out, m_w_ff_in, m_w_ff_out, v_norm_mix_pre, v_norm_mix_post, v_norm_mlp_pre, v_norm_mlp_post, v_w_in, v_rel_bias, v_sinks, v_lam_re, v_lam_im, v_log_dt, v_b_re, v_b_im, v_c_re, v_c_im, v_d_skip, v_w_glu, v_w_attn_branch, v_w_ssm_branch, v_w_out, v_w_ff_in, v_w_ff_out):
    args = dict(locals())
    w = {n: args[n] for n in ALL_WEIGHTS}
    m = {n: args["m_" + n] for n in ALL_WEIGHTS}
    v = {n: args["v_" + n] for n in ALL_WEIGHTS}
    core = lax.axis_index("c").astype(jnp.int32).reshape(1)
    chip = (2 * lax.axis_index("x") + lax.axis_index("y")).astype(jnp.int32).reshape(1)
    xs, target = x[0], loss_target[0]
    t = _tiles(xs.shape[0])
    local = lambda d, n: d[n][0].T if n == "w_in" else d[n][0]
    shard = {n: local(w, n).astype(BF16) for n in BIG}
    shard["w_ff_in"] = shard["w_ff_in"].T
    view = lambda n, a: jnp.swapaxes(a, -1, -2) if n in SWAPPED_SMALL else a
    small = {n: (view(n, w[n]) if n == "rel_bias" else view(n, w[n])[0]) for n in SMALL}
    g1, g2, g3, g4 = (small[n].reshape(1, D_MODEL) for n in ("norm_mix_pre", "norm_mix_post", "norm_mlp_pre", "norm_mlp_post"))
    bucket = jnp.asarray(_bucket_table())
    rel_b, sink = small["rel_bias"], small["sinks"].reshape(1, N_HEADS)
    lam_r, lam_i = small["lam_re"].reshape(1, STATES), small["lam_im"].reshape(1, STATES)
    ldt_rep = jnp.repeat(small["log_dt"].reshape(N_GROUPS), N_STATE).reshape(1, STATES)
    bd_re, bd_im = _block_diag_in(small["b_re"]), _block_diag_in(small["b_im"])
    cm_re, cm_im = _block_diag_out(small["c_re"]).astype(BF16), _block_diag_out(small["c_im"]).astype(BF16)
    dsk = small["d_skip"].reshape(1, SSM_W)

    (g_in,) = _run_carry(_gather_carry([shard["w_in"]]), "gather_w_in")
    wf_in = g_in.reshape(IN_W, D_MODEL)
    merge_names = ("w_glu", "w_attn_branch", "w_ssm_branch", "w_out")
    (q, k, vv, u, ga, gs, h), gathered = _in_proj_fwd(xs, g1, wf_in, t["proj"], _gather_carry([shard[n] for n in merge_names]))
    wf = {n: _full_from_gathered(n, g) for n, g in zip(merge_names, gathered)}
    (att,), (wf_ff_in,) = _attn_fwd(q, k, vv, bucket, rel_b, sink, _gather_carry([shard["w_ff_in"]]))
    a_re, a_im, bm_re, bm_im = _ssm_prep(lam_r, lam_i, ldt_rep, bd_re, bd_im)
    (y, h_re, h_im, in_re, in_im), (wf_ff_out,) = _ssm_fwd(
        u, a_re, a_im, bm_re, bm_im, cm_re, cm_im, dsk, t["ssm_chunk"], _gather_carry([shard["w_ff_out"]]))
    x1, o, h2 = _merge_fwd(xs, y, att, ga, gs, g2, g3, wf["w_glu"], wf["w_ssm_branch"], wf["w_attn_branch"], wf["w_out"],
                           t["merge"])
    a, dfo, dx2, loss_blk, dg4 = _mlp_fwd(h2, x1, target, g4, wf_ff_in, wf_ff_out, t["mlp_fwd"])

    groups = {"ff": 4, "merge": 1, "w_in": 2}

    def add_sibling(group, blocks, received):
        return _add_sibling(blocks, received, core, groups[group], "add_sibling_" + group)

    ff_names = ("w_ff_in", "w_ff_out")
    dw_ff_in, dw_ff_out, da = _mlp_weight_grads(dfo, a, h2, wf_ff_out, t["mlp_bwd"])
    dh2 = _mlp_input_grad(da, wf_ff_in.reshape(D_FF, D_MODEL), t["mlp_bwd"])
    ff_blocks = [dw_ff_in, dw_ff_out]
    (dx1, dgates, datt, dy, dw_glu, dw_ssm, dw_attn, dw_out, dg2, dg3), ff_recv = _merge_bwd(
        dh2, dx2, x1, o, y, att, ga, gs, g2, g3, wf["w_glu"], wf["w_ssm_branch"], wf["w_attn_branch"], wf["w_out"],
        t["merge_bwd"], _sibling_carry(ff_blocks))
    ff_sums, ff_sums_bf = add_sibling("ff", ff_blocks, ff_recv)
    merge_blocks = [_blocks_from_full(n, g) for n, g in zip(merge_names, (dw_glu, dw_attn, dw_ssm, dw_out))]
    (du, dbm_re, dbm_im, dcm_re, dcm_im, da_re, da_im, dd_skip), carried = _ssm_bwd(
        dy, u, h_re, h_im, in_re, in_im, a_re, a_im, bm_re, bm_im, cm_re, cm_im, dsk, t["ssm_chunk"],
        _join(_chips_carry(ff_sums_bf), _sibling_carry(merge_blocks)))
    ff_from_chips, merge_recv = carried[:2], carried[2:]
    merge_sums, merge_sums_bf = add_sibling("merge", merge_blocks, merge_recv)
    dbd_re, dbd_im, dlam_re, dlam_im, dldt_rep = _ssm_prep_bwd(lam_r, lam_i, ldt_rep, bd_re, bd_im, dbm_re, dbm_im, da_re, da_im)
    dlog_dt = _group_sum(dldt_rep.reshape(N_GROUPS, N_STATE))
    shapes = {n: view(n, w[n]).shape for n in SMALL}
    shapes["loss"] = (1,)
    small_grads = dict(
        norm_mix_post=dg2, norm_mlp_pre=dg3, norm_mlp_post=dg4, lam_re=dlam_re, lam_im=dlam_im, log_dt=dlog_dt,
        b_re=_block_diag_in_grad(dbd_re), b_im=_block_diag_in_grad(dbd_im),
        c_re=_block_diag_out_grad(dcm_re), c_im=_block_diag_out_grad(dcm_im), d_skip=dd_skip)
    packed_early = _pack({n: small_grads[n].reshape(shapes[n]) for n in SMALL_BEFORE_ATTN_BWD}, SMALL_BEFORE_ATTN_BWD)
    (dq, dkv, attn_small), carried = _attn_bwd(
        q, k, vv, datt, bucket, rel_b, sink, _join(_chips_carry(merge_sums_bf), _gather_carry([packed_early])))
    merge_from_chips, partials_early = carried[:-1], carried[-1]

    dparts = (dq, dkv, du, dgates)
    dw_in_t = _in_proj_weight_grad(h, dparts)
    in_blocks = [dw_in_t.reshape(N_DEV, IN_W // N_DEV, D_MODEL)]
    to_sibling = _sibling_carry(in_blocks)
    in_flight, token = _exchange_start(to_sibling, "w_in_sibling_start")
    n_tiles = xs.shape[0] // t["proj_bwd"]
    (grad_x, dg1), _ = _in_proj_input_grad(xs, g1 + token[0:1, 0:1], wf_in, dx1, dparts, t["proj_bwd"], 0, n_tiles, "in_proj_input_grad")
    late = dict(norm_mix_pre=dg1, rel_bias=attn_small[:, :N_BUCKETS, 0], sinks=attn_small[:, N_BUCKETS, 0], loss=loss_blk[0:1, 0])
    packed_late = _pack({n: late[n].reshape(shapes[n]) for n in SMALL_LATE}, SMALL_LATE)
    to_everyone = _everyone_carry([packed_late])
    late_in_flight, late_started = _exchange_start(to_everyone, "late_grads_start")
    in_blocks, in_recv = _exchange_wait(to_sibling, in_flight, [late_started], "w_in_sibling_wait")
    in_sums, in_sums_bf = add_sibling("w_in", in_blocks, in_recv)
    to_chips = _chips_carry(in_sums_bf)
    in_flight, chips_started = _exchange_start(to_chips, "w_in_chips_start")
    (packed_late,), (late_received,) = _exchange_wait(to_everyone, late_in_flight, [chips_started], "late_grads_wait")

    grads, deltas, new_m, new_v = {}, {}, {}, {}

    def adam_group(group, names, sums, received, after=()):
        outs = _adam_big(*[[local(d, n) for n in names] for d in (w, m, v)], sums, received, chip, groups[group],
                         "adam_" + group, after)
        for store, vals in zip((grads, deltas, new_m, new_v), outs):
            store.update({n: (o.T if n == "w_in" else o)[None] for n, o in zip(names, vals)})

    adam_group("ff", ff_names, ff_sums, ff_from_chips, [chips_started])
    adam_group("merge", merge_names, merge_sums, merge_from_chips, [chips_started])

    grads.update(_unpack(_sum_partials(partials_early, "sum_small_grads", [chips_started]), shapes, SMALL_BEFORE_ATTN_BWD))
    grads.update(_unpack(_sum_everyone(packed_late, late_received, 2 * chip + core, "sum_late_grads"), shapes, SMALL_LATE))
    loss = grads.pop("loss").reshape(())
    small_out = _adam_small(*[[view(n, d[n]) for n in SMALL] for d in (w, m, v)], [grads[n] for n in SMALL])
    for store, vals in zip((deltas, new_m, new_v), small_out):
        store.update(zip(SMALL, vals))
    for store in (grads, deltas, new_m, new_v):
        store.update({n: view(n, store[n]) for n in SWAPPED_SMALL})

    busy = [new_v["w_ff_out"], new_v["w_out"], deltas["norm_mix_pre"]]
    _, (in_from_chips,) = _exchange_wait(to_chips, in_flight, busy, "w_in_chips_wait")
    adam_group("w_in", ("w_in",), in_sums, [in_from_chips])

    return (loss, grad_x[None], *[grads[n] for n in ALL_WEIGHTS], *[deltas[n] for n in ALL_WEIGHTS],
            *[new_m[n] for n in ALL_WEIGHTS], *[new_v[n] for n in ALL_WEIGHTS])
```

```python
import math

import jax
import jax.numpy as jnp
import numpy as np
from jax import lax
from jax.experimental import pallas as pl
from jax.experimental.pallas import tpu as pltpu

F32 = jnp.float32
BF16 = jnp.bfloat16

D_MODEL = 1024
N_HEADS = 8
HEAD_DIM = 64
ATTN_W = 512
KV_W = 128
BLOCK = 128
N_BUCKETS = 32
SSM_W = 512
N_GROUPS = 32
N_STATE = 64
GROUP_CH = 16
STATES = N_GROUPS * N_STATE
D_FF = 4096
IN_W = 3328
SPLITS = (0, 512, 640, 768, 1280, 2304, 3328)
RMS_EPS = 1e-6
NEG_INF = -1e30
SUBLANES = 8
LANES = 128
SSM_LANE_BLOCK = 512
N_SSM_BLOCKS = STATES // SSM_LANE_BLOCK
GROUPS_PER_BLOCK = SSM_LANE_BLOCK // N_STATE
VMEM_BIG = 52 * 1024 * 1024
VMEM_MID = 40 * 1024 * 1024
VMEM_MAX = 60 * 1024 * 1024

ADAM_LR = 0.001
ADAM_B1 = 0.9
ADAM_B2 = 0.999
ADAM_EPS = 1e-08
ADAM_WD = 0.01
ADAM_STEP = 10

N_DEV = 8


def _dot(a, b):
    return jnp.dot(a, b, preferred_element_type=F32)


def _dot_nt(a, b):
    return lax.dot_general(a, b, (((1,), (1,)), ((), ())), preferred_element_type=F32)


def _dot_tn(a, b):
    return lax.dot_general(a, b, (((0,), (0,)), ((), ())), preferred_element_type=F32)


def _rms_scale(x):
    return lax.rsqrt(jnp.mean(x * x, axis=-1, keepdims=True) + RMS_EPS)


def _rms_bwd(dy, x, r, g):
    t = dy * g
    dx = r * t - x * (r * r * r) * jnp.mean(t * x, axis=-1, keepdims=True)
    dg = jnp.sum(dy * x * r, axis=0, keepdims=True)
    return dx, dg


def _const_spec(shape):
    nd = len(shape)
    return pl.BlockSpec(shape, lambda *_: (0,) * nd, pipeline_mode=pl.Buffered(1))


def _in_hbm(*arrays):
    return tuple(pltpu.with_memory_space_constraint(a, pltpu.HBM) for a in arrays)


def _hbm_out(shapes):
    if isinstance(shapes, (list, tuple)):
        return [_hbm_out(s) for s in shapes]
    return shapes if isinstance(shapes, pl.MemoryRef) else pltpu.HBM(shapes.shape, shapes.dtype)


def _whole(shape):
    nd = len(shape)
    return pl.BlockSpec(shape, lambda *_: (0,) * nd)


def _params(sem, vmem=None):
    return pltpu.CompilerParams(dimension_semantics=sem, vmem_limit_bytes=vmem)


MESH_IDS = pl.DeviceIdType.MESH
HBM_SPEC = pl.BlockSpec(memory_space=pl.ANY)


class _Carry:
    def __init__(self, inputs, out_shapes, sems, start, finish, middle=None):
        self.inputs, self.out_shapes, self.sems = list(inputs), list(out_shapes), list(sems)
        self.start, self.middle, self.finish = start, middle, finish


def _join(a, b):
    na_in, na_out, na_sem = len(a.inputs), len(a.out_shapes), len(a.sems)

    def both(phase):
        def run(ins, outs, sems):
            for carry, lo in ((a, True), (b, False)):
                part = (lambda seq, n: seq[:n] if lo else seq[n:])
                if getattr(carry, phase) is not None:
                    getattr(carry, phase)(part(ins, na_in), part(outs, na_out), part(sems, na_sem))
        return run

    middle = both("middle") if (a.middle or b.middle) else None
    return _Carry(a.inputs + b.inputs, a.out_shapes + b.out_shapes, a.sems + b.sems, both("start"), both("finish"), middle)


def _hosted_call(body, carry, edge, *, name, grid, in_specs, out_specs, out_shape, scratch_shapes, compiler_params, inputs):
    n_in, n_out = len(in_specs), len(out_specs)
    inputs = [a if s.memory_space == pltpu.SMEM else _in_hbm(a)[0] for a, s in zip(inputs, in_specs)]
    out_shape = _hbm_out(list(out_shape))
    if carry is None:
        outs = pl.pallas_call(body, name=name, grid=grid, in_specs=in_specs, out_specs=out_specs, out_shape=out_shape,
                              scratch_shapes=scratch_shapes, compiler_params=compiler_params)(*inputs)
        return list(outs), []
    c_in, c_out, c_sem = len(carry.inputs), len(carry.out_shapes), len(carry.sems)

    def wrapped(*refs):
        ins, refs = refs[:n_in], refs[n_in:]
        cins, refs = refs[:c_in], refs[c_in:]
        outs, refs = refs[:n_out], refs[n_out:]
        couts, refs = refs[:c_out], refs[c_out:]
        scratch, csems = refs[:len(refs) - c_sem], refs[len(refs) - c_sem:]
        first, middle, last = edge()

        @pl.when(first)
        def _():
            carry.start(cins, couts, csems)

        body(*ins, *outs, *scratch)

        if carry.middle is not None:
            @pl.when(middle)
            def _():
                carry.middle(cins, couts, csems)

        @pl.when(last)
        def _():
            carry.finish(cins, couts, csems)

    outs = pl.pallas_call(
        wrapped, name=name, grid=grid, in_specs=list(in_specs) + [HBM_SPEC] * c_in,
        out_specs=list(out_specs) + [HBM_SPEC] * c_out, out_shape=out_shape + _hbm_out(carry.out_shapes),
        scratch_shapes=list(scratch_shapes) + carry.sems, compiler_params=compiler_params)(*inputs, *_in_hbm(*carry.inputs))
    return list(outs[:n_out]), list(outs[n_out:])


def _pass_on_step(n_steps):
    return (7 * n_steps) // 8


def _edge_1d(n_steps):
    return lambda: (pl.program_id(0) == 0, pl.program_id(0) == _pass_on_step(n_steps), pl.program_id(0) == n_steps - 1)


def _edge_2d(n0, n1):
    def edge():
        step = pl.program_id(0) * n1 + pl.program_id(1)
        return step == 0, step == _pass_on_step(n0 * n1), step == n0 * n1 - 1
    return edge


def _run_carry(carry, name):
    c_in, c_out = len(carry.inputs), len(carry.out_shapes)

    def body(*refs):
        ins, outs, sems = refs[:c_in], refs[c_in:c_in + c_out], refs[c_in + c_out:]
        carry.start(ins, outs, sems)
        if carry.middle is not None:
            carry.middle(ins, outs, sems)
        carry.finish(ins, outs, sems)

    return pl.pallas_call(body, name=name, in_specs=[HBM_SPEC] * c_in, out_specs=[HBM_SPEC] * c_out,
                          out_shape=_hbm_out(carry.out_shapes), scratch_shapes=carry.sems)(*_in_hbm(*carry.inputs))


def _in_proj_fwd(x, g1, w_in_t, tile, carry=None):
    T = x.shape[0]

    def body(x_ref, g_ref, w_ref, q_ref, k_ref, v_ref, u_ref, ga_ref, gs_ref, h_ref):
        xv = x_ref[...]
        h = (xv * _rms_scale(xv) * g_ref[...]).astype(BF16)
        h_ref[...] = h
        outs = (q_ref, k_ref, v_ref, u_ref, ga_ref, gs_ref)
        for p, o_ref in enumerate(outs):
            o_ref[...] = _dot_nt(h, w_ref[SPLITS[p]:SPLITS[p + 1], :]).astype(o_ref.dtype)

    widths = [SPLITS[p + 1] - SPLITS[p] for p in range(6)] + [D_MODEL]
    dtypes = [BF16, BF16, BF16, F32, F32, F32, BF16]
    return _hosted_call(
        body, carry, _edge_1d(T // tile), name="in_proj_fwd", grid=(T // tile,),
        in_specs=[pl.BlockSpec((tile, D_MODEL), lambda i: (i, 0)), _const_spec((1, D_MODEL)), _const_spec((IN_W, D_MODEL))],
        out_specs=[pl.BlockSpec((tile, w), lambda i: (i, 0)) for w in widths],
        out_shape=[jax.ShapeDtypeStruct((T, w), dt) for w, dt in zip(widths, dtypes)],
        scratch_shapes=[], compiler_params=_params(("arbitrary",), VMEM_MID), inputs=(x, g1, w_in_t))


PROJ_PARTS = (512, 256, 512, 2048)
PROJ_GRAD_BLOCK = 256


def _in_proj_weight_grad(h, dparts):
    T = h.shape[0]
    blocks = [wd // PROJ_GRAD_BLOCK for wd in PROJ_PARTS]
    starts = [sum(blocks[:p]) for p in range(len(blocks))]

    def body(h_ref, *refs):
        part_refs, o_ref = refs[:-1], refs[-1]
        j = pl.program_id(0)
        for p_ref, start, count in zip(part_refs, starts, blocks):
            @pl.when((j >= start) & (j < start + count))
            def _(p_ref=p_ref):
                o_ref[...] = _dot_tn(p_ref[...], h_ref[...])

    def part_spec(start, count):
        return pl.BlockSpec((T, PROJ_GRAD_BLOCK), lambda j: (0, jnp.clip(j - start, 0, count - 1)))

    return pl.pallas_call(
        body, name="in_proj_weight_grad", grid=(sum(blocks),),
        in_specs=[_const_spec((T, D_MODEL))] + [part_spec(s, c) for s, c in zip(starts, blocks)],
        out_specs=pl.BlockSpec((PROJ_GRAD_BLOCK, D_MODEL), lambda j: (j, 0)),
        out_shape=_hbm_out(jax.ShapeDtypeStruct((IN_W, D_MODEL), F32)),
        compiler_params=_params(("arbitrary",), VMEM_MID),
    )(*_in_hbm(h, *dparts))


def _in_proj_input_grad(x, g1, w_in_t, dx1, dparts, tile, first_tile, n_tiles, name, carry=None):
    offsets = [sum(PROJ_PARTS[:p]) for p in range(len(PROJ_PARTS))]

    def body(x_ref, g_ref, w_ref, dx1_ref, *refs):
        part_refs, (gx_ref, dg_ref) = refs[:len(PROJ_PARTS)], refs[len(PROJ_PARTS):]
        i = pl.program_id(0)
        xv = x_ref[...]
        r = _rms_scale(xv)
        g = g_ref[...]
        dh = sum(_dot(p_ref[...], w_ref[off:off + wd, :]) for p_ref, off, wd in zip(part_refs, offsets, PROJ_PARTS))
        dxn, dg = _rms_bwd(dh, xv, r, g)
        gx_ref[...] = dx1_ref[...] + dxn

        @pl.when(i == 0)
        def _():
            dg_ref[...] = dg

        @pl.when(i > 0)
        def _():
            dg_ref[...] += dg

    tok = lambda wd: pl.BlockSpec((tile, wd), lambda i: (i + first_tile, 0))
    return _hosted_call(
        body, carry, _edge_1d(n_tiles), name=name, grid=(n_tiles,),
        in_specs=[tok(D_MODEL), _const_spec((1, D_MODEL)), _const_spec((IN_W, D_MODEL)), tok(D_MODEL)] + [tok(wd) for wd in PROJ_PARTS],
        out_specs=[pl.BlockSpec((tile, D_MODEL), lambda i: (i, 0)), pl.BlockSpec((1, D_MODEL), lambda i: (0, 0))],
        out_shape=[jax.ShapeDtypeStruct((n_tiles * tile, D_MODEL), F32), jax.ShapeDtypeStruct((1, D_MODEL), F32)],
        scratch_shapes=[], compiler_params=_params(("arbitrary",), VMEM_MID), inputs=(x, g1, w_in_t, dx1, *dparts))


def _bucket_table():
    qi = np.arange(BLOCK)[:, None]
    kj = np.arange(2 * BLOCK)[None, :]
    dist = qi + BLOCK - kj
    max_exact = N_BUCKETS // 2
    d = np.maximum(dist, 0)
    df = np.maximum(d, 1).astype(np.float32)
    large = max_exact + (np.log(df / np.float32(max_exact)) / np.float32(math.log(BLOCK / max_exact))
                         * np.float32(N_BUCKETS - max_exact)).astype(np.int32)
    large = np.minimum(large, N_BUCKETS - 1)
    bucket = np.where(d < max_exact, d, large)
    return np.where((dist >= 0) & (dist < BLOCK), bucket, -1).astype(np.int32)


def _build_bias(bucket_ref, rb_ref, bias_ref):
    bk = bucket_ref[...]
    for h in range(N_HEADS):
        def add(b, acc, h=h):
            return acc + jnp.where(bk == b, rb_ref[h, b], 0.0)
        bias_ref[h] = lax.fori_loop(0, N_BUCKETS, add, jnp.zeros((BLOCK, 2 * BLOCK), F32))


def _kv_variants(prev_ref, cur_ref):
    cat = jnp.concatenate([prev_ref[...], cur_ref[...]], axis=0)
    lo = lax.broadcasted_iota(jnp.int32, cat.shape, 1) < HEAD_DIM
    zero = jnp.zeros_like(cat)
    head0_lo = jnp.where(lo, cat, zero)
    head1_hi = jnp.where(lo, zero, cat)
    return ((head0_lo, pltpu.roll(head0_lo, HEAD_DIM, 1)), (pltpu.roll(head1_hi, HEAD_DIM, 1), head1_hi))


def _merge_kv_grads(g):
    lo = lax.broadcasted_iota(jnp.int32, g[0][0].shape, 1) < HEAD_DIM
    return jnp.where(lo, g[0][0] + pltpu.roll(g[0][1], HEAD_DIM, 1), g[1][1] + pltpu.roll(g[1][0], HEAD_DIM, 1))


def _head_lanes(h):
    return slice((h // 2) * LANES, (h // 2 + 1) * LANES)


def _attn_probs(q_ref, kvar, bias_ref, sk_ref, valid, s_ref):
    for h in range(N_HEADS):
        s_ref[h] = _dot_nt(q_ref[:, _head_lanes(h)], kvar[h // 4][h % 2])
    head = lax.broadcasted_iota(jnp.int32, (N_HEADS, 1, 1), 0)
    sink = jnp.zeros((N_HEADS, 1, 1), F32)
    for h in range(N_HEADS):
        sink = jnp.where(head == h, sk_ref[0, h], sink)
    s = jnp.where(valid[None], s_ref[...] * (HEAD_DIM ** -0.5) + bias_ref[...], NEG_INF)
    m = jnp.maximum(jnp.max(s, axis=-1, keepdims=True), sink)
    p = jnp.exp(s - m)
    e_sink = jnp.exp(sink - m)
    inv = 1.0 / (jnp.sum(p, axis=-1, keepdims=True) + e_sink)
    return p * inv, e_sink * inv


def _attn_valid(bucket_ref, n):
    col = lax.broadcasted_iota(jnp.int32, (BLOCK, 2 * BLOCK), 1)
    return (bucket_ref[...] >= 0) & ((n > 0) | (col >= BLOCK))


def _attn_fwd(q, k, v, bucket, rel_bias, sinks, carry=None):
    T = q.shape[0]
    nb = T // BLOCK

    def body(q_ref, kc_ref, kp_ref, vc_ref, vp_ref, bucket_ref, rb_ref, sk_ref, o_ref, bias_ref, s_ref, p_ref):
        n = pl.program_id(0)

        @pl.when(n == 0)
        def _():
            _build_bias(bucket_ref, rb_ref, bias_ref)

        kvar = _kv_variants(kp_ref, kc_ref)
        vvar = _kv_variants(vp_ref, vc_ref)
        pr, _ = _attn_probs(q_ref, kvar, bias_ref, sk_ref, _attn_valid(bucket_ref, n), s_ref)
        p_ref[...] = pr.astype(BF16)
        for m in range(N_HEADS // 2):
            acc = _dot(p_ref[2 * m], vvar[m // 2][0]) + _dot(p_ref[2 * m + 1], vvar[m // 2][1])
            o_ref[:, m * LANES:(m + 1) * LANES] = acc.astype(o_ref.dtype)

    cur = lambda w: pl.BlockSpec((BLOCK, w), lambda n: (n, 0))
    prev = lambda w: pl.BlockSpec((BLOCK, w), lambda n: (jnp.maximum(n - 1, 0), 0))
    smem = pl.BlockSpec(memory_space=pltpu.SMEM)
    return _hosted_call(
        body, carry, _edge_1d(nb), name="attn_fwd", grid=(nb,),
        in_specs=[cur(ATTN_W), cur(KV_W), prev(KV_W), cur(KV_W), prev(KV_W), _const_spec((BLOCK, 2 * BLOCK)), smem, smem],
        out_specs=[cur(ATTN_W)],
        out_shape=[jax.ShapeDtypeStruct((T, ATTN_W), BF16)],
        scratch_shapes=[pltpu.VMEM((N_HEADS, BLOCK, 2 * BLOCK), F32), pltpu.VMEM((N_HEADS, BLOCK, 2 * BLOCK), F32),
                        pltpu.VMEM((N_HEADS, BLOCK, 2 * BLOCK), BF16)],
        compiler_params=_params(("arbitrary",)), inputs=(q, k, k, v, v, bucket, rel_bias, sinks))


ATTN_SMALL_ROWS = N_BUCKETS + SUBLANES


def _attn_bwd(q, k, v, datt, bucket, rel_bias, sinks, carry=None):
    T = q.shape[0]
    nb = T // BLOCK

    def body(q_ref, do_ref, kc_ref, kp_ref, vc_ref, vp_ref, bucket_ref, rb_ref, sk_ref,
             dq_ref, dkv_ref, small_ref, bias_ref, ds_sum_ref, dsink_ref, kcarry_ref, vcarry_ref,
             s_ref, dp_ref, p_ref, dsc_ref):
        n = pl.program_id(0)

        @pl.when(n == 0)
        def _():
            _build_bias(bucket_ref, rb_ref, bias_ref)
            ds_sum_ref[...] = jnp.zeros_like(ds_sum_ref)
            dsink_ref[...] = jnp.zeros_like(dsink_ref)
            kcarry_ref[...] = jnp.zeros_like(kcarry_ref)
            vcarry_ref[...] = jnp.zeros_like(vcarry_ref)

        @pl.when(n < nb)
        def _():
            kvar = _kv_variants(kp_ref, kc_ref)
            vvar = _kv_variants(vp_ref, vc_ref)
            pr, p_sink = _attn_probs(q_ref, kvar, bias_ref, sk_ref, _attn_valid(bucket_ref, n), s_ref)
            for h in range(N_HEADS):
                dp_ref[h] = _dot_nt(do_ref[:, _head_lanes(h)], vvar[h // 4][h % 2])
            dp = dp_ref[...]
            dsum = jnp.sum(pr * dp, axis=-1, keepdims=True)
            ds = pr * (dp - dsum)
            ds_sum_ref[...] += ds
            dsink_ref[...] -= jnp.sum(p_sink * dsum, axis=1, keepdims=True)
            dsc_ref[...] = (ds * (HEAD_DIM ** -0.5)).astype(BF16)
            p_ref[...] = pr.astype(BF16)
            for m in range(N_HEADS // 2):
                dqm = _dot(dsc_ref[2 * m], kvar[m // 2][0]) + _dot(dsc_ref[2 * m + 1], kvar[m // 2][1])
                dq_ref[:, m * LANES:(m + 1) * LANES] = dqm.astype(dq_ref.dtype)
            dk_var = [[None, None], [None, None]]
            dv_var = [[None, None], [None, None]]
            for kvh in range(2):
                for e in range(2):
                    heads = [h for h in range(N_HEADS) if h // 4 == kvh and h % 2 == e]
                    dk_var[kvh][e] = sum(_dot_tn(dsc_ref[h], q_ref[:, _head_lanes(h)]) for h in heads)
                    dv_var[kvh][e] = sum(_dot_tn(p_ref[h], do_ref[:, _head_lanes(h)]) for h in heads)
            dk_cat = _merge_kv_grads(dk_var)
            dv_cat = _merge_kv_grads(dv_var)

            @pl.when(n > 0)
            def _():
                dkv_ref[:, :KV_W] = (kcarry_ref[...] + dk_cat[:BLOCK]).astype(BF16)
                dkv_ref[:, KV_W:] = (vcarry_ref[...] + dv_cat[:BLOCK]).astype(BF16)

            kcarry_ref[...] = dk_cat[BLOCK:]
            vcarry_ref[...] = dv_cat[BLOCK:]

        @pl.when(n == nb)
        def _():
            dkv_ref[:, :KV_W] = kcarry_ref[...].astype(BF16)
            dkv_ref[:, KV_W:] = vcarry_ref[...].astype(BF16)
            bk = bucket_ref[...]
            row = lax.broadcasted_iota(jnp.int32, (N_HEADS, ATTN_SMALL_ROWS, LANES), 1)

            def add(b, acc):
                masked = jnp.where((bk == b)[None], ds_sum_ref[...], 0.0)
                val = jnp.sum(jnp.sum(masked, axis=1, keepdims=True), axis=2, keepdims=True)
                return acc + jnp.where(row == b, val, 0.0)

            small_ref[...] = lax.fori_loop(0, N_BUCKETS, add, jnp.where(row == N_BUCKETS, dsink_ref[...], 0.0))

    last = nb - 1
    cur = lambda w: pl.BlockSpec((BLOCK, w), lambda n: (jnp.minimum(n, last), 0))
    prev = lambda w: pl.BlockSpec((BLOCK, w), lambda n: (jnp.clip(n - 1, 0, last), 0))
    smem = pl.BlockSpec(memory_space=pltpu.SMEM)
    return _hosted_call(
        body, carry, _edge_1d(nb + 1), name="attn_bwd", grid=(nb + 1,),
        in_specs=[cur(ATTN_W), cur(ATTN_W), cur(KV_W), prev(KV_W), cur(KV_W), prev(KV_W),
                  _const_spec((BLOCK, 2 * BLOCK)), smem, smem],
        out_specs=[cur(ATTN_W), prev(2 * KV_W), pl.BlockSpec((N_HEADS, ATTN_SMALL_ROWS, LANES), lambda n: (0, 0, 0))],
        out_shape=[jax.ShapeDtypeStruct((T, ATTN_W), BF16), jax.ShapeDtypeStruct((T, 2 * KV_W), BF16),
                   jax.ShapeDtypeStruct((N_HEADS, ATTN_SMALL_ROWS, LANES), F32)],
        scratch_shapes=[pltpu.VMEM((N_HEADS, BLOCK, 2 * BLOCK), F32), pltpu.VMEM((N_HEADS, BLOCK, 2 * BLOCK), F32),
                        pltpu.VMEM((N_HEADS, 1, 1), F32), pltpu.VMEM((BLOCK, KV_W), F32), pltpu.VMEM((BLOCK, KV_W), F32),
                        pltpu.VMEM((N_HEADS, BLOCK, 2 * BLOCK), F32), pltpu.VMEM((N_HEADS, BLOCK, 2 * BLOCK), F32),
                        pltpu.VMEM((N_HEADS, BLOCK, 2 * BLOCK), BF16), pltpu.VMEM((N_HEADS, BLOCK, 2 * BLOCK), BF16)],
        compiler_params=_params(("arbitrary",)), inputs=(q, datt, k, k, v, v, bucket, rel_bias, sinks))


SCAN_UNROLL = 4


def _cmul(ar, ai, br, bi):
    return ar * br - ai * bi, ar * bi + ai * br


def _cmul_conj(ar, ai, br, bi):
    return ar * br + ai * bi, ar * bi - ai * br


def _ssm_discretize(lr, li, ldt):
    dt = jnp.exp(ldt)
    mag = jnp.exp(lr * dt)
    ab_re = mag * jnp.cos(li * dt)
    ab_im = mag * jnp.sin(li * dt)
    nr = ab_re - 1.0
    den = lr * lr + li * li
    f_re = (nr * lr + ab_im * li) / den
    f_im = (ab_im * lr - nr * li) / den
    return ab_re, ab_im, f_re, f_im


def _ssm_prep(lam_re, lam_im, ldt_rep, bd_re, bd_im):
    def body(lr_ref, li_ref, ldt_ref, bdr_ref, bdi_ref, ar_ref, ai_ref, br_ref, bi_ref):
        ab_re, ab_im, f_re, f_im = _ssm_discretize(lr_ref[...], li_ref[...], ldt_ref[...])
        ar_ref[...] = ab_re
        ai_ref[...] = ab_im
        bdr, bdi = bdr_ref[0], bdi_ref[0]
        br_ref[0] = (bdr * f_re - bdi * f_im).astype(BF16)
        bi_ref[0] = (bdi * f_re + bdr * f_im).astype(BF16)

    row = pl.BlockSpec((1, SSM_LANE_BLOCK), lambda j: (0, j))
    mat = pl.BlockSpec((1, LANES, SSM_LANE_BLOCK), lambda j: (j, 0, 0))
    return pl.pallas_call(
        body, name="ssm_prep", grid=(N_SSM_BLOCKS,),
        in_specs=[row, row, row, mat, mat], out_specs=[row, row, mat, mat],
        out_shape=[jax.ShapeDtypeStruct((1, STATES), F32)] * 2 + [jax.ShapeDtypeStruct((N_SSM_BLOCKS, LANES, SSM_LANE_BLOCK), BF16)] * 2,
        compiler_params=_params(("arbitrary",)),
    )(*_in_hbm(lam_re, lam_im, ldt_rep, bd_re, bd_im))


def _ssm_prep_bwd(lam_re, lam_im, ldt_rep, bd_re, bd_im, dbr, dbi, da_re, da_im):
    def body(lr_ref, li_ref, ldt_ref, bdr_ref, bdi_ref, dbr_ref, dbi_ref, dar_ref, dai_ref,
             dbdr_ref, dbdi_ref, dlr_ref, dli_ref, dldt_ref):
        lr, li, ldt = lr_ref[...], li_ref[...], ldt_ref[...]
        (_, _, f_re, f_im), vjp = jax.vjp(_ssm_discretize, lr, li, ldt)
        bdr, bdi, gbr, gbi = bdr_ref[0], bdi_ref[0], dbr_ref[0], dbi_ref[0]
        dbdr_ref[0] = gbr * f_re + gbi * f_im
        dbdi_ref[0] = gbi * f_re - gbr * f_im
        df_re = jnp.sum(gbr * bdr + gbi * bdi, axis=0, keepdims=True)
        df_im = jnp.sum(gbi * bdr - gbr * bdi, axis=0, keepdims=True)
        dlr, dli, dldt = vjp((dar_ref[...], dai_ref[...], df_re, df_im))
        dlr_ref[...] = dlr
        dli_ref[...] = dli
        dldt_ref[...] = dldt

    row = pl.BlockSpec((1, SSM_LANE_BLOCK), lambda j: (0, j))
    mat = pl.BlockSpec((1, LANES, SSM_LANE_BLOCK), lambda j: (j, 0, 0))
    mat_shape = jax.ShapeDtypeStruct((N_SSM_BLOCKS, LANES, SSM_LANE_BLOCK), F32)
    row_shape = jax.ShapeDtypeStruct((1, STATES), F32)
    return pl.pallas_call(
        body, name="ssm_prep_bwd", grid=(N_SSM_BLOCKS,),
        in_specs=[row, row, row, mat, mat, mat, mat, row, row], out_specs=[mat, mat, row, row, row],
        out_shape=[mat_shape, mat_shape, row_shape, row_shape, row_shape],
        compiler_params=_params(("arbitrary",)),
    )(*_in_hbm(lam_re, lam_im, ldt_rep, bd_re, bd_im, dbr, dbi, da_re, da_im))


def _group_sum(x):
    def body(x_ref, o_ref):
        o_ref[...] = jnp.sum(x_ref[...], axis=1, keepdims=True)
    return pl.pallas_call(body, name="ssm_group_sum", grid=(1,), in_specs=[_whole(x.shape)], out_specs=_whole((N_GROUPS, 1)),
                          out_shape=jax.ShapeDtypeStruct((N_GROUPS, 1), F32))(*_in_hbm(x))


def _power_table(ar, ai, p_re_ref, p_im_ref, steps):
    shape = (SUBLANES, SSM_LANE_BLOCK)
    p_re_ref[0:SUBLANES] = jnp.broadcast_to(ar, shape)
    p_im_ref[0:SUBLANES] = jnp.broadcast_to(ai, shape)
    m = 1
    while m < steps:
        rows = m * SUBLANES
        top_re = p_re_ref[rows - SUBLANES:rows]
        top_im = p_im_ref[rows - SUBLANES:rows]
        cur_re = p_re_ref[0:rows].reshape(m, SUBLANES, SSM_LANE_BLOCK)
        cur_im = p_im_ref[0:rows].reshape(m, SUBLANES, SSM_LANE_BLOCK)
        nxt_re, nxt_im = _cmul(cur_re, cur_im, top_re[None], top_im[None])
        p_re_ref[rows:2 * rows] = nxt_re.reshape(rows, SSM_LANE_BLOCK)
        p_im_ref[rows:2 * rows] = nxt_im.reshape(rows, SSM_LANE_BLOCK)
        m *= 2


def _to_segments(src_ref, dst_ref, steps):
    for s in range(SUBLANES):
        dst_ref[pl.ds(s, steps, stride=SUBLANES), :] = src_ref[s * steps:(s + 1) * steps, :]


def _from_segments(src_ref, dst_ref, steps):
    for s in range(SUBLANES):
        dst_ref[s * steps:(s + 1) * steps, :] = src_ref[pl.ds(s, steps, stride=SUBLANES), :]


def _segment_carries(e_re, e_im, an_re, an_im, c_re, c_im, reverse):
    order = range(SUBLANES - 1, -1, -1) if reverse else range(SUBLANES)
    ins_re, ins_im = [None] * SUBLANES, [None] * SUBLANES
    for s in order:
        ins_re[s], ins_im[s] = c_re, c_im
        pr, pi = _cmul(an_re, an_im, c_re, c_im)
        c_re = e_re[s:s + 1] + pr
        c_im = e_im[s:s + 1] + pi
    return jnp.concatenate(ins_re, axis=0), jnp.concatenate(ins_im, axis=0), c_re, c_im


def _ssm_fwd(u, a_re, a_im, b_re, b_im, c_re, c_im, d_skip, chunk, carry=None):
    T = u.shape[0]
    nc = T // chunk
    steps = chunk // SUBLANES
    blk = SSM_LANE_BLOCK

    def body(u_ref, ar_ref, ai_ref, br_ref, bi_ref, cr_ref, ci_ref, dk_ref,
             y_ref, hr_ref, hi_ref, inr_ref, ini_ref, useg_ref, yseg_ref, pr_ref, pi_ref, carry_ref):
        c = pl.program_id(1)
        ar, ai = ar_ref[...], ai_ref[...]

        @pl.when(c == 0)
        def _():
            _power_table(ar, ai, pr_ref, pi_ref, steps)
            carry_ref[...] = jnp.zeros_like(carry_ref)

        _to_segments(u_ref, useg_ref, steps)
        ub = useg_ref[...].astype(BF16)
        hr_ref[...] = _dot(ub, br_ref[0])
        hi_ref[...] = _dot(ub, bi_ref[0])
        first = slice(0, SUBLANES)

        def scan(t4, prev):
            for j in range(SCAN_UNROLL):
                rows = pl.ds(pl.multiple_of((t4 * SCAN_UNROLL + j) * SUBLANES, SUBLANES), SUBLANES)
                pr, pi = _cmul(pr_ref[first, :], pi_ref[first, :], prev[0], prev[1])
                prev = (pr + hr_ref[rows, :], pi + hi_ref[rows, :])
                hr_ref[rows, :] = prev[0]
                hi_ref[rows, :] = prev[1]
            return prev

        zero = jnp.zeros((SUBLANES, blk), F32)
        lax.fori_loop(0, steps // SCAN_UNROLL, scan, (zero, zero))

        top = slice(chunk - SUBLANES, chunk)
        in_re, in_im, out_re, out_im = _segment_carries(
            hr_ref[top, :], hi_ref[top, :], pr_ref[top, :][0:1], pi_ref[top, :][0:1],
            carry_ref[0:1, :], carry_ref[1:2, :], reverse=False)
        carry_ref[0:1, :] = out_re
        carry_ref[1:2, :] = out_im
        inr_ref[...] = in_re
        ini_ref[...] = in_im

        def fix(t4, _):
            for j in range(SCAN_UNROLL):
                rows = pl.ds(pl.multiple_of((t4 * SCAN_UNROLL + j) * SUBLANES, SUBLANES), SUBLANES)
                fr, fi = _cmul(pr_ref[rows, :], pi_ref[rows, :], in_re, in_im)
                hr_ref[rows, :] += fr
                hi_ref[rows, :] += fi
            return 0

        lax.fori_loop(0, steps // SCAN_UNROLL, fix, 0)

        yseg_ref[...] = _dot(hr_ref[...].astype(BF16), cr_ref[0]) - _dot(hi_ref[...].astype(BF16), ci_ref[0])
        _from_segments(yseg_ref, y_ref, steps)
        y_ref[...] += dk_ref[...] * u_ref[...]

    row = pl.BlockSpec((1, blk), lambda j, c: (0, j))
    b_mat = pl.BlockSpec((1, LANES, blk), lambda j, c: (j, 0, 0))
    c_mat = pl.BlockSpec((1, blk, LANES), lambda j, c: (j, 0, 0))
    tok = pl.BlockSpec((chunk, LANES), lambda j, c: (c, j))
    state = pl.BlockSpec((chunk, blk), lambda j, c: (c, j))
    enter = pl.BlockSpec((SUBLANES, blk), lambda j, c: (c, j))
    return _hosted_call(
        body, carry, _edge_2d(N_SSM_BLOCKS, nc), name="ssm_fwd", grid=(N_SSM_BLOCKS, nc),
        in_specs=[tok, row, row, b_mat, b_mat, c_mat, c_mat, pl.BlockSpec((1, LANES), lambda j, c: (0, j))],
        out_specs=[tok, state, state, enter, enter],
        out_shape=[jax.ShapeDtypeStruct((T, SSM_W), F32), jax.ShapeDtypeStruct((T, STATES), F32),
                   jax.ShapeDtypeStruct((T, STATES), F32), jax.ShapeDtypeStruct((nc * SUBLANES, STATES), F32),
                   jax.ShapeDtypeStruct((nc * SUBLANES, STATES), F32)],
        scratch_shapes=[pltpu.VMEM((chunk, LANES), F32), pltpu.VMEM((chunk, LANES), F32),
                        pltpu.VMEM((chunk, blk), F32), pltpu.VMEM((chunk, blk), F32), pltpu.VMEM((SUBLANES, blk), F32)],
        compiler_params=_params(("arbitrary", "arbitrary"), VMEM_MID),
        inputs=(u, a_re, a_im, b_re, b_im, c_re, c_im, d_skip))


def _ssm_bwd(dy, u, h_re, h_im, in_re, in_im, a_re, a_im, b_re, b_im, c_re, c_im, d_skip, chunk, carry=None):
    T = u.shape[0]
    nc = T // chunk
    steps = chunk // SUBLANES
    blk = SSM_LANE_BLOCK

    def body(dy_ref, u_ref, hr_ref, hi_ref, inr_ref, ini_ref, ar_ref, ai_ref, br_ref, bi_ref, cr_ref, ci_ref, dk_ref,
             du_ref, dbr_ref, dbi_ref, dcr_ref, dci_ref, dar_ref, dai_ref, ddk_ref,
             dyseg_ref, useg_ref, duseg_ref, gr_ref, gi_ref, pr_ref, pi_ref, carry_ref, accr_ref, acci_ref):
        c = pl.program_id(1)
        ar, ai = ar_ref[...], ai_ref[...]

        @pl.when(c == 0)
        def _():
            _power_table(ar, ai, pr_ref, pi_ref, steps)
            carry_ref[...] = jnp.zeros_like(carry_ref)
            accr_ref[...] = jnp.zeros_like(accr_ref)
            acci_ref[...] = jnp.zeros_like(acci_ref)

        _to_segments(dy_ref, dyseg_ref, steps)
        _to_segments(u_ref, useg_ref, steps)
        dyb = dyseg_ref[...].astype(BF16)
        ub = useg_ref[...].astype(BF16)
        gr_ref[...] = _dot_nt(dyb, cr_ref[0])
        gi_ref[...] = -_dot_nt(dyb, ci_ref[0])
        dcr = _dot_tn(hr_ref[...].astype(BF16), dyb)
        dci = -_dot_tn(hi_ref[...].astype(BF16), dyb)
        ddk = jnp.sum(dy_ref[...] * u_ref[...], axis=0, keepdims=True)

        first = slice(0, SUBLANES)

        def scan(k4, nxt):
            for j in range(SCAN_UNROLL):
                t = steps - 1 - (k4 * SCAN_UNROLL + j)
                rows = pl.ds(pl.multiple_of(t * SUBLANES, SUBLANES), SUBLANES)
                pr, pi = _cmul_conj(pr_ref[first, :], pi_ref[first, :], nxt[0], nxt[1])
                nxt = (pr + gr_ref[rows, :], pi + gi_ref[rows, :])
                gr_ref[rows, :] = nxt[0]
                gi_ref[rows, :] = nxt[1]
            return nxt

        top = slice(chunk - SUBLANES, chunk)
        zero = jnp.zeros((SUBLANES, blk), F32)
        lax.fori_loop(0, steps // SCAN_UNROLL, scan, (zero, zero))

        gin_re, gin_im, out_re, out_im = _segment_carries(
            gr_ref[0:SUBLANES, :], gi_ref[0:SUBLANES, :], pr_ref[top, :][0:1], -pi_ref[top, :][0:1],
            carry_ref[0:1, :], carry_ref[1:2, :], reverse=True)
        carry_ref[0:1, :] = out_re
        carry_ref[1:2, :] = out_im

        def fix_row(rows, prow, hp_re, hp_im, acc):
            fr, fi = _cmul_conj(pr_ref[prow, :], pi_ref[prow, :], gin_re, gin_im)
            g_re = gr_ref[rows, :] + fr
            g_im = gi_ref[rows, :] + fi
            gr_ref[rows, :] = g_re
            gi_ref[rows, :] = g_im
            return acc[0] + g_re * hp_re + g_im * hp_im, acc[1] + g_im * hp_re - g_re * hp_im

        def fix_at(t, acc):
            aligned = (lambda r: r * SUBLANES) if isinstance(t, int) else (lambda r: pl.multiple_of(r * SUBLANES, SUBLANES))
            rows, before, prow = (pl.ds(aligned(r), SUBLANES) for r in (t, t - 1, steps - 1 - t))
            return fix_row(rows, prow, hr_ref[before, :], hi_ref[before, :], acc)

        def fix(t4, acc):
            for j in range(SCAN_UNROLL):
                acc = fix_at(t4 * SCAN_UNROLL + j, acc)
            return acc

        acc = fix_row(first, top, inr_ref[...], ini_ref[...], (accr_ref[...], acci_ref[...]))
        for t in range(1, SCAN_UNROLL):
            acc = fix_at(t, acc)
        acc_re, acc_im = lax.fori_loop(1, steps // SCAN_UNROLL, fix, acc)
        accr_ref[...] = acc_re
        acci_ref[...] = acc_im

        gbr = gr_ref[...].astype(BF16)
        gbi = gi_ref[...].astype(BF16)
        duseg_ref[...] = _dot_nt(gbr, br_ref[0]) + _dot_nt(gbi, bi_ref[0])
        _from_segments(duseg_ref, dyseg_ref, steps)
        du_ref[...] = (dyseg_ref[...] + dk_ref[...] * dy_ref[...]).astype(BF16)
        dbr = _dot_tn(ub, gbr)
        dbi = _dot_tn(ub, gbi)

        @pl.when(c == 0)
        def _():
            dbr_ref[0] = dbr
            dbi_ref[0] = dbi
            dcr_ref[0] = dcr
            dci_ref[0] = dci
            ddk_ref[...] = ddk

        @pl.when(c > 0)
        def _():
            dbr_ref[0] += dbr
            dbi_ref[0] += dbi
            dcr_ref[0] += dcr
            dci_ref[0] += dci
            ddk_ref[...] += ddk

        @pl.when(c == nc - 1)
        def _():
            dar_ref[...] = jnp.sum(acc_re, axis=0, keepdims=True)
            dai_ref[...] = jnp.sum(acc_im, axis=0, keepdims=True)

    rev = lambda c: nc - 1 - c
    row = pl.BlockSpec((1, blk), lambda j, c: (0, j))
    b_mat = pl.BlockSpec((1, LANES, blk), lambda j, c: (j, 0, 0))
    c_mat = pl.BlockSpec((1, blk, LANES), lambda j, c: (j, 0, 0))
    tok = pl.BlockSpec((chunk, LANES), lambda j, c: (rev(c), j))
    state = pl.BlockSpec((chunk, blk), lambda j, c: (rev(c), j))
    enter = pl.BlockSpec((SUBLANES, blk), lambda j, c: (rev(c), j))
    chan = pl.BlockSpec((1, LANES), lambda j, c: (0, j))
    f32 = lambda *s: jax.ShapeDtypeStruct(s, F32)
    return _hosted_call(
        body, carry, _edge_2d(N_SSM_BLOCKS, nc), name="ssm_bwd", grid=(N_SSM_BLOCKS, nc),
        in_specs=[tok, tok, state, state, enter, enter, row, row, b_mat, b_mat, c_mat, c_mat, chan],
        out_specs=[tok, b_mat, b_mat, c_mat, c_mat, row, row, chan],
        out_shape=[jax.ShapeDtypeStruct((T, SSM_W), BF16), f32(N_SSM_BLOCKS, LANES, blk), f32(N_SSM_BLOCKS, LANES, blk),
                   f32(N_SSM_BLOCKS, blk, LANES), f32(N_SSM_BLOCKS, blk, LANES), f32(1, STATES), f32(1, STATES), f32(1, SSM_W)],
        scratch_shapes=[pltpu.VMEM((chunk, LANES), F32), pltpu.VMEM((chunk, LANES), F32), pltpu.VMEM((chunk, LANES), F32),
                        pltpu.VMEM((chunk, blk), F32), pltpu.VMEM((chunk, blk), F32),
                        pltpu.VMEM((chunk, blk), F32), pltpu.VMEM((chunk, blk), F32),
                        pltpu.VMEM((SUBLANES, blk), F32), pltpu.VMEM((SUBLANES, blk), F32), pltpu.VMEM((SUBLANES, blk), F32)],
        compiler_params=_params(("arbitrary", "arbitrary"), VMEM_BIG),
        inputs=(dy, u, h_re, h_im, in_re, in_im, a_re, a_im, b_re, b_im, c_re, c_im, d_skip))


def _merge_forward(y, att, ga, gs, w_glu, w_ssm, w_attn):
    z = jax.nn.gelu(y)
    zb = z.astype(BF16)
    gl = jax.nn.sigmoid(_dot(zb, w_glu))
    z2b = (z * gl).astype(BF16)
    y_ssm = _dot(z2b, w_ssm)
    y_attn = _dot(att, w_attn)
    sa = jax.nn.sigmoid(ga)
    ss = jax.nn.sigmoid(gs)
    merged = (sa * y_attn + ss * y_ssm).astype(BF16)
    return z, zb, gl, z2b, y_ssm, y_attn, sa, ss, merged


def _merge_fwd(x, y, att, ga, gs, g2, g3, w_glu, w_ssm, w_attn, w_out, tile):
    T = x.shape[0]

    def body(x_ref, y_ref, att_ref, ga_ref, gs_ref, g2_ref, g3_ref, wg_ref, ws_ref, wa_ref, wo_ref, x1_ref, o_ref, h2_ref):
        merged = _merge_forward(y_ref[...], att_ref[...], ga_ref[...], gs_ref[...], wg_ref[...], ws_ref[...], wa_ref[...])[-1]
        o = _dot(merged, wo_ref[...])
        x1 = x_ref[...] + o * _rms_scale(o) * g2_ref[...]
        o_ref[...] = o
        x1_ref[...] = x1
        h2_ref[...] = (x1 * _rms_scale(x1) * g3_ref[...]).astype(BF16)

    tok = lambda w: pl.BlockSpec((tile, w), lambda i: (i, 0))
    vec = _const_spec((1, D_MODEL))
    return pl.pallas_call(
        body, name="merge_fwd", grid=(T // tile,),
        in_specs=[tok(D_MODEL), tok(SSM_W), tok(ATTN_W), tok(D_MODEL), tok(D_MODEL), vec, vec,
                  _const_spec((SSM_W, SSM_W)), _const_spec((SSM_W, D_MODEL)), _const_spec((ATTN_W, D_MODEL)),
                  _const_spec((D_MODEL, D_MODEL))],
        out_specs=[tok(D_MODEL), tok(D_MODEL), tok(D_MODEL)],
        out_shape=_hbm_out([jax.ShapeDtypeStruct((T, D_MODEL), F32), jax.ShapeDtypeStruct((T, D_MODEL), F32),
                            jax.ShapeDtypeStruct((T, D_MODEL), BF16)]),
        compiler_params=_params(("arbitrary",), VMEM_MID),
    )(*_in_hbm(x, y, att, ga, gs, g2, g3, w_glu, w_ssm, w_attn, w_out))


def _merge_bwd(dh2, dx2, x1, o, y, att, ga, gs, g2, g3, w_glu, w_ssm, w_attn, w_out, tile, carry=None):
    T = x1.shape[0]
    n_steps = T // tile

    group = min(2, n_steps)
    staged_widths = (D_MODEL, D_MODEL, ATTN_W, D_MODEL, SSM_W, D_MODEL, SSM_W, SSM_W)

    def body(dh2_ref, dx2_ref, x1_ref, o_ref, y_ref, att_ref, ga_ref, gs_ref, g2_ref, g3_ref, wg_ref, ws_ref, wa_ref, wo_ref,
             dx1_ref, dgates_ref, datt_ref, dy_ref, dwg_hbm, dws_hbm, dwa_hbm, dwo_hbm, dg2_ref, dg3_ref,
             awg_ref, aws_ref, awa_ref, awo_ref, *staged):
        i = pl.program_id(0)
        x1v, ov = x1_ref[...], o_ref[...]
        dxn, dg3 = _rms_bwd(dh2_ref[...], x1v, _rms_scale(x1v), g3_ref[...])
        dx1 = dx2_ref[...] + dxn
        dx1_ref[...] = dx1
        do, dg2 = _rms_bwd(dx1, ov, _rms_scale(ov), g2_ref[...])
        dob = do.astype(BF16)

        yv = y_ref[...]
        att = att_ref[...]
        z, zb, gl, z2b, y_ssm, y_attn, sa, ss, merged = _merge_forward(
            yv, att, ga_ref[...], gs_ref[...], wg_ref[...], ws_ref[...], wa_ref[...])
        dmerged = _dot_nt(dob, wo_ref[...])
        dya = (dmerged * sa).astype(BF16)
        dys = (dmerged * ss).astype(BF16)
        dgates_ref[:, :D_MODEL] = (dmerged * y_attn * sa * (1.0 - sa)).astype(BF16)
        dgates_ref[:, D_MODEL:] = (dmerged * y_ssm * ss * (1.0 - ss)).astype(BF16)
        datt_ref[...] = _dot_nt(dya, wa_ref[...]).astype(BF16)
        dz2 = _dot_nt(dys, ws_ref[...])
        dpre = (dz2 * z * gl * (1.0 - gl)).astype(BF16)
        dz = dz2 * gl + _dot_nt(dpre, wg_ref[...])
        _, gelu_vjp = jax.vjp(jax.nn.gelu, yv)
        dy_ref[...] = gelu_vjp(dz)[0]

        part = pl.ds(pl.multiple_of((i % group) * tile, tile), tile)
        for ref, val in zip(staged, (merged, dob, att, dya, z2b, dys, zb, dpre)):
            ref[part, :] = val

        @pl.when(i == 0)
        def _():
            dg2_ref[...] = dg2
            dg3_ref[...] = dg3

        @pl.when(i > 0)
        def _():
            dg2_ref[...] += dg2
            dg3_ref[...] += dg3

        def weight_grads():
            s_merged, s_dob, s_att, s_dya, s_z2b, s_dys, s_zb, s_dpre = (ref[...] for ref in staged)
            return ((awo_ref, _dot_tn(s_merged, s_dob)), (awa_ref, _dot_tn(s_att, s_dya)),
                    (aws_ref, _dot_tn(s_z2b, s_dys)), (awg_ref, _dot_tn(s_zb, s_dpre)))

        @pl.when(i == group - 1)
        def _():
            for ref, val in weight_grads():
                ref[...] = val

        @pl.when((i % group == group - 1) & (i > group - 1))
        def _():
            for ref, val in weight_grads():
                ref[...] += val

        @pl.when(i == n_steps - 1)
        def _():
            pltpu.sync_copy(awg_ref, dwg_hbm)
            pltpu.sync_copy(aws_ref, dws_hbm)
            pltpu.sync_copy(awa_ref, dwa_hbm)
            pltpu.sync_copy(awo_ref, dwo_hbm)

    tok = lambda w: pl.BlockSpec((tile, w), lambda i: (i, 0))
    vec = _const_spec((1, D_MODEL))
    any_ = pl.BlockSpec(memory_space=pl.ANY)
    vec_out = pl.BlockSpec((1, D_MODEL), lambda i: (0, 0))
    f32 = lambda *s: jax.ShapeDtypeStruct(s, F32)
    bf = lambda *s: jax.ShapeDtypeStruct(s, BF16)
    return _hosted_call(
        body, carry, _edge_1d(n_steps), name="merge_bwd", grid=(n_steps,),
        in_specs=[tok(D_MODEL), tok(D_MODEL), tok(D_MODEL), tok(D_MODEL), tok(SSM_W), tok(ATTN_W), tok(D_MODEL), tok(D_MODEL),
                  vec, vec, _const_spec((SSM_W, SSM_W)), _const_spec((SSM_W, D_MODEL)), _const_spec((ATTN_W, D_MODEL)),
                  _const_spec((D_MODEL, D_MODEL))],
        out_specs=[tok(D_MODEL), tok(2 * D_MODEL), tok(ATTN_W), tok(SSM_W), any_, any_, any_, any_, vec_out, vec_out],
        out_shape=[f32(T, D_MODEL), bf(T, 2 * D_MODEL), bf(T, ATTN_W), f32(T, SSM_W),
                   f32(SSM_W, SSM_W), f32(SSM_W, D_MODEL), f32(ATTN_W, D_MODEL), f32(D_MODEL, D_MODEL),
                   f32(1, D_MODEL), f32(1, D_MODEL)],
        scratch_shapes=[pltpu.VMEM((SSM_W, SSM_W), F32), pltpu.VMEM((SSM_W, D_MODEL), F32),
                        pltpu.VMEM((ATTN_W, D_MODEL), F32), pltpu.VMEM((D_MODEL, D_MODEL), F32)]
        + [pltpu.VMEM((group * tile, wd), BF16) for wd in staged_widths],
        compiler_params=_params(("arbitrary",), VMEM_BIG),
        inputs=(dh2, dx2, x1, o, y, att, ga, gs, g2, g3, w_glu, w_ssm, w_attn, w_out))


FF_SHARD = D_FF // N_DEV


def _mlp_fwd(h2, x1, target, g4, w_ff_in, w_ff_out, tile):
    T = h2.shape[0]
    col_chunk = 2 * FF_SHARD

    def body(h2_ref, x1_ref, tg_ref, g4_ref, wi_ref, wo_ref, a_ref, dfo_ref, dx2_ref, loss_ref, dg4_ref, rr_ref):
        i = pl.program_id(0)
        h2v = h2_ref[...]
        for c in range(D_FF // col_chunk):
            cols = slice(c * col_chunk, (c + 1) * col_chunk)
            a = _dot_nt(h2v, wi_ref[cols, :])
            a_ref[:, cols] = a.astype(BF16)
            ra = jnp.maximum(a, 0.0)
            rr_ref[:, cols] = (ra * ra).astype(BF16)
        f = _dot(rr_ref[...], wo_ref[...])
        r = _rms_scale(f)
        g = g4_ref[...]
        err = x1_ref[...] + f * r * g - tg_ref[...]
        dx2 = err * (1.0 / D_MODEL)
        dx2_ref[...] = dx2
        dfo, dg = _rms_bwd(dx2, f, r, g)
        dfo_ref[...] = dfo.astype(BF16)
        row = lax.broadcasted_iota(jnp.int32, (SUBLANES, LANES), 0)
        col = lax.broadcasted_iota(jnp.int32, (SUBLANES, LANES), 1)
        loss = jnp.where((row == 0) & (col == 0), (0.5 / D_MODEL) * jnp.sum(err * err), 0.0)

        @pl.when(i == 0)
        def _():
            loss_ref[...] = loss
            dg4_ref[...] = dg

        @pl.when(i > 0)
        def _():
            loss_ref[...] += loss
            dg4_ref[...] += dg

    tok = pl.BlockSpec((tile, D_MODEL), lambda i: (i, 0))
    return pl.pallas_call(
        body, name="mlp_fwd", grid=(T // tile,),
        in_specs=[tok, tok, tok, _const_spec((1, D_MODEL)), _const_spec((D_FF, D_MODEL)), _const_spec((D_FF, D_MODEL))],
        out_specs=[pl.BlockSpec((tile, D_FF), lambda i: (i, 0)), tok, tok,
                   pl.BlockSpec((SUBLANES, LANES), lambda i: (0, 0)), pl.BlockSpec((1, D_MODEL), lambda i: (0, 0))],
        out_shape=_hbm_out([jax.ShapeDtypeStruct((T, D_FF), BF16), jax.ShapeDtypeStruct((T, D_MODEL), BF16),
                            jax.ShapeDtypeStruct((T, D_MODEL), F32), jax.ShapeDtypeStruct((SUBLANES, LANES), F32),
                            jax.ShapeDtypeStruct((1, D_MODEL), F32)]),
        scratch_shapes=[pltpu.VMEM((tile, D_FF), BF16)],
        compiler_params=_params(("arbitrary",), VMEM_MAX),
    )(*_in_hbm(h2, x1, target, g4, w_ff_in.reshape(D_FF, D_MODEL), w_ff_out.reshape(D_FF, D_MODEL)))


def _mlp_weight_grads(dfo, a, h2, w_ff_out, row_chunk):
    T = h2.shape[0]

    def body(dfo_ref, h2_ref, a_ref, wo_ref, dwi_ref, dwo_ref, da_ref, rr_ref):
        def rows(r, _):
            sl = pl.ds(pl.multiple_of(r * row_chunk, row_chunk), row_chunk)
            ra = jnp.maximum(a_ref[sl, :].astype(F32), 0.0)
            da_ref[sl, :] = (_dot_nt(dfo_ref[sl, :], wo_ref[0]) * (2.0 * ra)).astype(BF16)
            rr_ref[sl, :] = (ra * ra).astype(BF16)
            return 0

        lax.fori_loop(0, T // row_chunk, rows, 0)
        dwo_ref[0] = _dot_tn(rr_ref[...], dfo_ref[...])
        dwi_ref[0] = _dot_tn(h2_ref[...], da_ref[...])

    return pl.pallas_call(
        body, name="mlp_weight_grads", grid=(N_DEV,),
        in_specs=[_const_spec((T, D_MODEL)), _const_spec((T, D_MODEL)), pl.BlockSpec((T, FF_SHARD), lambda k: (0, k)),
                  pl.BlockSpec((1, FF_SHARD, D_MODEL), lambda k: (k, 0, 0))],
        out_specs=[pl.BlockSpec((1, D_MODEL, FF_SHARD), lambda k: (k, 0, 0)),
                   pl.BlockSpec((1, FF_SHARD, D_MODEL), lambda k: (k, 0, 0)), pl.BlockSpec((T, FF_SHARD), lambda k: (0, k))],
        out_shape=_hbm_out([jax.ShapeDtypeStruct((N_DEV, D_MODEL, FF_SHARD), F32),
                            jax.ShapeDtypeStruct((N_DEV, FF_SHARD, D_MODEL), F32), jax.ShapeDtypeStruct((T, D_FF), BF16)]),
        scratch_shapes=[pltpu.VMEM((T, FF_SHARD), BF16)],
        compiler_params=_params(("arbitrary",), VMEM_MAX),
    )(*_in_hbm(dfo, h2, a, w_ff_out))


def _mlp_input_grad(da, w_ff_in_t, tile):
    T = da.shape[0]

    def body(da_ref, w_ref, o_ref):
        o_ref[...] = _dot(da_ref[...], w_ref[...])

    return pl.pallas_call(
        body, name="mlp_input_grad", grid=(T // tile,),
        in_specs=[pl.BlockSpec((tile, D_FF), lambda i: (i, 0)), _const_spec((D_FF, D_MODEL))],
        out_specs=pl.BlockSpec((tile, D_MODEL), lambda i: (i, 0)),
        out_shape=_hbm_out(jax.ShapeDtypeStruct((T, D_MODEL), F32)),
        compiler_params=_params(("arbitrary",), VMEM_MID),
    )(*_in_hbm(da, w_ff_in_t))


def _block_diag_in(b):
    bt = b.reshape(N_SSM_BLOCKS, GROUPS_PER_BLOCK, GROUP_CH, N_STATE)
    eye = jnp.eye(GROUPS_PER_BLOCK, dtype=b.dtype)
    return jnp.einsum("jacp,ab->jacbp", bt, eye).reshape(N_SSM_BLOCKS, LANES, SSM_LANE_BLOCK)


def _block_diag_in_grad(g):
    g = g.reshape(N_SSM_BLOCKS, GROUPS_PER_BLOCK, GROUP_CH, GROUPS_PER_BLOCK, N_STATE)
    d = jnp.diagonal(g, axis1=1, axis2=3)
    return jnp.transpose(d, (0, 3, 1, 2)).reshape(N_GROUPS, GROUP_CH, N_STATE)


def _block_diag_out(c):
    ct = c.reshape(N_SSM_BLOCKS, GROUPS_PER_BLOCK, GROUP_CH, N_STATE)
    eye = jnp.eye(GROUPS_PER_BLOCK, dtype=c.dtype)
    return jnp.einsum("jacp,ab->japbc", ct, eye).reshape(N_SSM_BLOCKS, SSM_LANE_BLOCK, LANES)


def _block_diag_out_grad(g):
    g = g.reshape(N_SSM_BLOCKS, GROUPS_PER_BLOCK, N_STATE, GROUPS_PER_BLOCK, GROUP_CH)
    d = jnp.diagonal(g, axis1=1, axis2=3)
    return jnp.transpose(d, (0, 3, 2, 1)).reshape(N_GROUPS, GROUP_CH, N_STATE)


def _tiles(T):
    return dict(proj=min(512, T), proj_bwd=min(512, T // 2), merge=min(512, T), merge_bwd=min(256, T),
                mlp_fwd=min(512, T), mlp_bwd=min(512, T), ssm_chunk=min(1024, T))


def _mesh_position():
    x, y, c = lax.axis_index("x"), lax.axis_index("y"), lax.axis_index("c")
    other_chips = [(1 - x, y), (x, 1 - y), (1 - x, 1 - y)]
    return x, y, c, other_chips


def _gather_carry(arrays):
    n = len(arrays)

    def copies(ins, outs, sems):
        send_sems, recv_sems, local_sems = sems
        x, y, c, chips = _mesh_position()
        me, sibling = (x, y, c), (x, y, 1 - c)

        def copy(a, k, block, to, src=None):
            px, py, pc = block
            dst = outs[a].at[4 * px + 2 * py + pc]
            return pltpu.make_async_remote_copy(
                src_ref=dst if src is None else src, dst_ref=dst, send_sem=send_sems.at[7 * a + k],
                recv_sem=recv_sems.at[7 * a + k], device_id=to, device_id_type=MESH_IDS)

        mine = [pltpu.make_async_copy(ins[a], outs[a].at[4 * x + 2 * y + c], local_sems.at[a]) for a in range(n)]
        first = []
        for a in range(n):
            first.append(copy(a, 0, me, sibling, src=ins[a]))
            first += [copy(a, 1 + j, me, (*chip, c), src=ins[a]) for j, chip in enumerate(chips)]
        return copy, mine, first, me, sibling, chips, c

    def start(ins, outs, sems):
        _, mine, first, *_ = copies(ins, outs, sems)
        for cp in mine + first:
            cp.start()

    def passed_on(copy, sibling, chips, c):
        return [copy(a, 4 + j, (*chip, c), sibling) for a in range(n) for j, chip in enumerate(chips)]

    def middle(ins, outs, sems):
        copy, _, _, me, sibling, chips, c = copies(ins, outs, sems)
        for a in range(n):
            for j, chip in enumerate(chips):
                copy(a, 1 + j, (*chip, c), me).wait_recv()
        for cp in passed_on(copy, sibling, chips, c):
            cp.start()

    def finish(ins, outs, sems):
        copy, mine, first, me, sibling, chips, c = copies(ins, outs, sems)
        for a in range(n):
            copy(a, 0, sibling, me).wait_recv()
            for j, chip in enumerate(chips):
                copy(a, 4 + j, (*chip, 1 - c), me).wait_recv()
        for cp in first + passed_on(copy, sibling, chips, c):
            cp.wait_send()
        for cp in mine:
            cp.wait()

    return _Carry(arrays, [jax.ShapeDtypeStruct((N_DEV,) + a.shape, a.dtype) for a in arrays],
                  [pltpu.SemaphoreType.DMA((7 * n,)), pltpu.SemaphoreType.DMA((7 * n,)), pltpu.SemaphoreType.DMA((n,))],
                  start, finish, middle)


def _pairwise_carry(arrays, n_slots, make_copies):
    n = len(arrays)

    def start(ins, outs, sems):
        for cp in make_copies(ins, outs, sems):
            cp.start()

    def finish(ins, outs, sems):
        for cp in make_copies(ins, outs, sems):
            cp.wait()

    return _Carry(arrays, [jax.ShapeDtypeStruct((n_slots,) + a.shape[1:], a.dtype) for a in arrays],
                  [pltpu.SemaphoreType.DMA((n_slots * n,)), pltpu.SemaphoreType.DMA((n_slots * n,))], start, finish)


def _sibling_carry(grads):
    def make_copies(ins, outs, sems):
        x, y, c, _ = _mesh_position()
        return [pltpu.make_async_remote_copy(
            src_ref=ins[a].at[2 * ch + (1 - c)], dst_ref=outs[a].at[ch], send_sem=sems[0].at[4 * a + ch],
            recv_sem=sems[1].at[4 * a + ch], device_id=(x, y, 1 - c), device_id_type=MESH_IDS)
            for a in range(len(grads)) for ch in range(4)]

    return _pairwise_carry(grads, 4, make_copies)


def _chips_carry(sums):
    def make_copies(ins, outs, sems):
        x, y, c, chips = _mesh_position()
        return [pltpu.make_async_remote_copy(
            src_ref=ins[a].at[2 * px + py], dst_ref=outs[a].at[j], send_sem=sems[0].at[3 * a + j],
            recv_sem=sems[1].at[3 * a + j], device_id=(px, py, c), device_id_type=MESH_IDS)
            for a in range(len(sums)) for j, (px, py) in enumerate(chips)]

    return _pairwise_carry(sums, 3, make_copies)


def _everyone_carry(arrays):
    def make_copies(ins, outs, sems):
        x, y, c, _ = _mesh_position()
        flip = lambda v, bit: 1 - v if bit else v
        return [pltpu.make_async_remote_copy(
            src_ref=ins[a], dst_ref=outs[a].at[r - 1], send_sem=sems[0].at[7 * a + r - 1], recv_sem=sems[1].at[7 * a + r - 1],
            device_id=(flip(x, r & 4), flip(y, r & 2), flip(c, r & 1)), device_id_type=MESH_IDS)
            for a in range(len(arrays)) for r in range(1, N_DEV)]

    carry = _pairwise_carry([jax.ShapeDtypeStruct((1,) + a.shape, a.dtype) for a in arrays], N_DEV - 1, make_copies)
    carry.inputs = list(arrays)
    return carry


def _sum_everyone(own, received, me, name, after=()):
    def body(me_ref, own_ref, r_ref, *refs):
        g = None
        for d in range(N_DEV):
            relation = jnp.bitwise_xor(d, me_ref[0])
            part = jnp.where(relation == 0, own_ref[...], r_ref[jnp.maximum(relation - 1, 0)])
            g = part if g is None else g + part
        refs[-1][...] = g

    whole = lambda shape: pl.BlockSpec(shape, lambda i, me_ref: (0,) * len(shape))
    return pl.pallas_call(
        body, name=name,
        grid_spec=pltpu.PrefetchScalarGridSpec(
            num_scalar_prefetch=1, grid=(1,), in_specs=[whole(own.shape), whole(received.shape)] + [HBM_SPEC] * len(after),
            out_specs=whole(own.shape)),
        out_shape=jax.ShapeDtypeStruct(own.shape, F32))(me, *_in_hbm(own, received), *after)


SEM_SPEC = pl.BlockSpec(memory_space=pltpu.SEMAPHORE)
DATAFLOW_EFFECT = pltpu.SideEffectType.DATAFLOW_SIDE_EFFECTING


def _exchange_start(carry, name, after=()):
    n = len(carry.inputs)
    lands = [lax.empty(s.shape, s.dtype) for s in carry.out_shapes]

    def body(*refs):
        first_out = 2 * n + len(after)
        srcs, zones, sems, token = refs[:n], refs[n:2 * n], refs[first_out:first_out + 2], refs[-1]
        carry.start(srcs, zones, sems)
        token[...] = jnp.zeros_like(token)

    outs = pl.pallas_call(
        body, name=name, in_specs=[HBM_SPEC] * (2 * n + len(after)),
        out_specs=[SEM_SPEC, SEM_SPEC] + [HBM_SPEC] * (2 * n) + [pl.BlockSpec(memory_space=pltpu.VMEM)],
        out_shape=list(carry.sems) + _hbm_out([jax.ShapeDtypeStruct(a.shape, a.dtype) for a in carry.inputs])
        + _hbm_out(carry.out_shapes) + [jax.ShapeDtypeStruct((SUBLANES, LANES), F32)],
        input_output_aliases={j: 2 + j for j in range(2 * n)},
        compiler_params=pltpu.CompilerParams(has_side_effects=DATAFLOW_EFFECT),
    )(*_in_hbm(*carry.inputs, *lands), *after)
    return outs[:-1], outs[-1]


def _exchange_wait(carry, in_flight, after, name):
    n = len(carry.inputs)
    sems, srcs, zones = in_flight[:2], in_flight[2:2 + n], in_flight[2 + n:]

    def body(*refs):
        src_refs, zone_refs, sem_refs = refs[:n], refs[n:2 * n], refs[2 * n:2 * n + 2]
        carry.finish(src_refs, zone_refs, sem_refs)

    outs = pl.pallas_call(
        body, name=name, in_specs=[HBM_SPEC] * (2 * n) + [SEM_SPEC, SEM_SPEC] + [HBM_SPEC] * len(after),
        out_specs=[HBM_SPEC] * (2 * n),
        out_shape=_hbm_out([jax.ShapeDtypeStruct(a.shape, a.dtype) for a in carry.inputs]) + _hbm_out(carry.out_shapes),
        input_output_aliases={j: j for j in range(2 * n)},
        compiler_params=pltpu.CompilerParams(has_side_effects=DATAFLOW_EFFECT),
    )(*srcs, *zones, *sems, *after)
    return list(outs[:n]), list(outs[n:])


def _add_sibling(grads8, recvs, core, row_tiles, name):
    k = len(grads8)
    g4 = [g.reshape(4, 2, *g.shape[1:]) for g in grads8]

    def body(core_ref, *refs):
        g_refs, r_refs, o_refs, ob_refs = (refs[j * k:(j + 1) * k] for j in range(4))
        for g_ref, r_ref, o_ref, ob_ref in zip(g_refs, r_refs, o_refs, ob_refs):
            s = g_ref[0] + r_ref[...]
            o_ref[...] = s
            ob_ref[...] = s.astype(BF16)

    def blocks(make):
        return [make(g.shape[1] // row_tiles, g.shape[2]) for g in grads8]

    slot = lambda tr, C: pl.BlockSpec((1, tr, C), lambda ch, r, core_ref: (ch, r, 0))
    outs = pl.pallas_call(
        body, name=name,
        grid_spec=pltpu.PrefetchScalarGridSpec(
            num_scalar_prefetch=1, grid=(4, row_tiles),
            in_specs=blocks(lambda tr, C: pl.BlockSpec((1, 1, tr, C), lambda ch, r, core_ref: (ch, core_ref[0], r, 0)))
            + blocks(slot), out_specs=blocks(slot) + blocks(slot)),
        out_shape=_hbm_out([jax.ShapeDtypeStruct((4,) + g.shape[1:], F32) for g in grads8]
                           + [jax.ShapeDtypeStruct((4,) + g.shape[1:], BF16) for g in grads8]),
        compiler_params=_params(("arbitrary", "arbitrary")),
    )(core, *_in_hbm(*g4, *recvs))
    return list(outs[:k]), list(outs[k:])


def _adam_math(w, g, m, v):
    m = ADAM_B1 * m + (1.0 - ADAM_B1) * g
    v = ADAM_B2 * v + (1.0 - ADAM_B2) * jnp.square(g)
    m_hat = m / (1.0 - ADAM_B1 ** ADAM_STEP)
    v_hat = v / (1.0 - ADAM_B2 ** ADAM_STEP)
    delta = -ADAM_LR * (m_hat / (jnp.sqrt(v_hat) + ADAM_EPS) + ADAM_WD * w)
    return delta, m, v


def _adam_big(ws, ms, vs, chip_sums, recvs, chip, row_tiles, name, after=()):
    k = len(ws)

    def body(chip_ref, *refs):
        refs = refs[:5 * k] + refs[5 * k + len(after):]
        w_refs, m_refs, v_refs, s_refs, r_refs, g_refs, d_refs, nm_refs, nv_refs = (refs[j * k:(j + 1) * k] for j in range(9))
        for a in range(k):
            r_ref = r_refs[a]
            g = s_refs[a][0] + r_ref[0].astype(F32) + r_ref[1].astype(F32) + r_ref[2].astype(F32)
            g_refs[a][...] = g
            d_refs[a][...], nm_refs[a][...], nv_refs[a][...] = _adam_math(w_refs[a][...], g, m_refs[a][...], v_refs[a][...])

    def blocks(make):
        return [make(w.shape[0] // row_tiles, w.shape[1]) for w in ws]

    blk = lambda tr, C: pl.BlockSpec((tr, C), lambda r, chip_ref: (r, 0))
    outs = pl.pallas_call(
        body, name=name,
        grid_spec=pltpu.PrefetchScalarGridSpec(
            num_scalar_prefetch=1, grid=(row_tiles,),
            in_specs=blocks(blk) * 3 + blocks(lambda tr, C: pl.BlockSpec((1, tr, C), lambda r, chip_ref: (chip_ref[0], r, 0)))
            + blocks(lambda tr, C: pl.BlockSpec((3, tr, C), lambda r, chip_ref: (0, r, 0))) + [HBM_SPEC] * len(after),
            out_specs=blocks(blk) * 4),
        out_shape=[jax.ShapeDtypeStruct(w.shape, F32) for w in ws] * 4,
        compiler_params=_params(("arbitrary",)),
    )(chip, *_in_hbm(*ws, *ms, *vs, *chip_sums, *recvs), *after)
    return [list(outs[j * k:(j + 1) * k]) for j in range(4)]


def _sum_partials(partials, name, after=()):
    def body(p_ref, *refs):
        g = p_ref[0]
        for d in range(1, partials.shape[0]):
            g = g + p_ref[d]
        refs[-1][...] = g

    return pl.pallas_call(body, name=name, grid=(1,), in_specs=[_whole(partials.shape)] + [HBM_SPEC] * len(after),
                          out_specs=_whole(partials.shape[1:]),
                          out_shape=jax.ShapeDtypeStruct(partials.shape[1:], F32))(*_in_hbm(partials), *after)


def _adam_small(ws, ms, vs, gs):
    n = len(ws)

    def body(*refs):
        w_refs, m_refs, v_refs, g_refs = (refs[i * n:(i + 1) * n] for i in range(4))
        d_refs, nm_refs, nv_refs = (refs[(4 + i) * n:(5 + i) * n] for i in range(3))
        for j in range(n):
            d_refs[j][...], nm_refs[j][...], nv_refs[j][...] = _adam_math(
                w_refs[j][...], g_refs[j][...], m_refs[j][...], v_refs[j][...])

    specs = [_whole(w.shape) for w in ws]
    outs = pl.pallas_call(body, name="adam_small", grid=(1,), in_specs=specs * 4, out_specs=specs * 3,
                          out_shape=[jax.ShapeDtypeStruct(w.shape, F32) for w in ws] * 3,
                          compiler_params=_params(("arbitrary",), VMEM_MID))(*_in_hbm(*ws, *ms, *vs, *gs))
    return outs[:n], outs[n:2 * n], outs[2 * n:]


PACK_QUANTUM = SUBLANES * LANES


def _pack(named, names):
    parts = []
    for nme in names:
        flat = named[nme].reshape(-1)
        parts.append(jnp.pad(flat, (0, -flat.size % PACK_QUANTUM)))
    return jnp.concatenate(parts).reshape(-1, LANES)


def _unpack(packed, shapes, names):
    flat = packed.reshape(-1)
    out, pos = {}, 0
    for nme in names:
        size = math.prod(shapes[nme])
        out[nme] = flat[pos:pos + size].reshape(shapes[nme])
        pos += size + (-size % PACK_QUANTUM)
    return out


BIG = ("w_in", "w_glu", "w_attn_branch", "w_ssm_branch", "w_out", "w_ff_in", "w_ff_out")
COLUMN_SHARDED = ("w_in", "w_attn_branch", "w_ssm_branch", "w_ff_in")
SMALL = ("norm_mix_pre", "norm_mix_post", "norm_mlp_pre", "norm_mlp_post", "rel_bias", "sinks", "lam_re", "lam_im",
         "log_dt", "b_re", "b_im", "c_re", "c_im", "d_skip")
SWAPPED_SMALL = ("rel_bias", "b_re", "b_im")
SMALL_LATE = ("norm_mix_pre", "rel_bias", "sinks", "loss")
SMALL_BEFORE_ATTN_BWD = tuple(n for n in SMALL if n not in SMALL_LATE)
ALL_WEIGHTS = ("norm_mix_pre", "norm_mix_post", "norm_mlp_pre", "norm_mlp_post", "w_in", "rel_bias", "sinks", "lam_re",
               "lam_im", "log_dt", "b_re", "b_im", "c_re", "c_im", "d_skip", "w_glu", "w_attn_branch", "w_ssm_branch",
               "w_out", "w_ff_in", "w_ff_out")


def _full_from_gathered(name, gathered):
    _, r, c = gathered.shape
    if name in COLUMN_SHARDED:
        return jnp.transpose(gathered, (1, 0, 2)).reshape(r, N_DEV * c)
    return gathered.reshape(N_DEV * r, c)


def _blocks_from_full(name, full):
    r, c = full.shape
    if name in COLUMN_SHARDED:
        return jnp.transpose(full.reshape(r, N_DEV, c // N_DEV), (1, 0, 2))
    return full.reshape(N_DEV, r // N_DEV, c)


def kernel(x, norm_mix_pre, norm_mix_post, norm_mlp_pre, norm_mlp_post, w_in, rel_bias, sinks, lam_re, lam_im, log_dt, b_re, b_im, c_re, c_im, d_skip, w_glu, w_attn_branch, w_ssm_branch, w_out, w_ff_in, w_ff_out, loss_target, m_norm_mix_pre, m_norm_mix_post, m_norm_mlp_pre, m_norm_mlp_post, m_w_in, m_rel_bias, m_sinks, m_lam_re, m_lam_im, m_log_dt, m_b_re, m_b_im, m_c_re, m_c_im, m_d_skip, m_w_glu, m_w_attn_branch, m_w_ssm_branch, m_w_out, m_w_ff_in, m_w_ff_out, v_norm_mix_pre, v_norm_mix_post, v_norm_mlp_pre, v_norm_mlp_post, v_w_in, v_rel_bias, v_sinks, v_lam_re, v_lam_im, v_log_dt, v_b_re, v_b_im, v_c_re, v_c_im, v_d_skip, v_w_glu, v_w_attn_branch, v_w_ssm_branch, v_w_out, v_w_ff_in, v_w_ff_out):
    args = dict(locals())
    w = {n: args[n] for n in ALL_WEIGHTS}
    m = {n: args["m_" + n] for n in ALL_WEIGHTS}
    v = {n: args["v_" + n] for n in ALL_WEIGHTS}
    core = lax.axis_index("c").astype(jnp.int32).reshape(1)
    chip = (2 * lax.axis_index("x") + lax.axis_index("y")).astype(jnp.int32).reshape(1)
    xs, target = x[0], loss_target[0]
    t = _tiles(xs.shape[0])
    local = lambda d, n: d[n][0].T if n == "w_in" else d[n][0]
    shard = {n: local(w, n).astype(BF16) for n in BIG}
    shard["w_ff_in"] = shard["w_ff_in"].T
    view = lambda n, a: jnp.swapaxes(a, -1, -2) if n in SWAPPED_SMALL else a
    small = {n: (view(n, w[n]) if n == "rel_bias" else view(n, w[n])[0]) for n in SMALL}
    g1, g2, g3, g4 = (small[n].reshape(1, D_MODEL) for n in ("norm_mix_pre", "norm_mix_post", "norm_mlp_pre", "norm_mlp_post"))
    bucket = jnp.asarray(_bucket_table())
    rel_b, sink = small["rel_bias"], small["sinks"].reshape(1, N_HEADS)
    lam_r, lam_i = small["lam_re"].reshape(1, STATES), small["lam_im"].reshape(1, STATES)
    ldt_rep = jnp.repeat(small["log_dt"].reshape(N_GROUPS), N_STATE).reshape(1, STATES)
    bd_re, bd_im = _block_diag_in(small["b_re"]), _block_diag_in(small["b_im"])
    cm_re, cm_im = _block_diag_out(small["c_re"]).astype(BF16), _block_diag_out(small["c_im"]).astype(BF16)
    dsk = small["d_skip"].reshape(1, SSM_W)

    (g_in,) = _run_carry(_gather_carry([shard["w_in"]]), "gather_w_in")
    wf_in = g_in.reshape(IN_W, D_MODEL)
    merge_names = ("w_glu", "w_attn_branch", "w_ssm_branch", "w_out")
    (q, k, vv, u, ga, gs, h), gathered = _in_proj_fwd(xs, g1, wf_in, t["proj"], _gather_carry([shard[n] for n in merge_names]))
    wf = {n: _full_from_gathered(n, g) for n, g in zip(merge_names, gathered)}
    (att,), (wf_ff_in,) = _attn_fwd(q, k, vv, bucket, rel_b, sink, _gather_carry([shard["w_ff_in"]]))
    a_re, a_im, bm_re, bm_im = _ssm_prep(lam_r, lam_i, ldt_rep, bd_re, bd_im)
    (y, h_re, h_im, in_re, in_im), (wf_ff_out,) = _ssm_fwd(
        u, a_re, a_im, bm_re, bm_im, cm_re, cm_im, dsk, t["ssm_chunk"], _gather_carry([shard["w_ff_out"]]))
    x1, o, h2 = _merge_fwd(xs, y, att, ga, gs, g2, g3, wf["w_glu"], wf["w_ssm_branch"], wf["w_attn_branch"], wf["w_out"],
                           t["merge"])
    a, dfo, dx2, loss_blk, dg4 = _mlp_fwd(h2, x1, target, g4, wf_ff_in, wf_ff_out, t["mlp_fwd"])

    groups = {"ff": 4, "merge": 1, "w_in": 2}

    def add_sibling(group, blocks, received):
        return _add_sibling(blocks, received, core, groups[group], "add_sibling_" + group)

    ff_names = ("w_ff_in", "w_ff_out")
    dw_ff_in, dw_ff_out, da = _mlp_weight_grads(dfo, a, h2, wf_ff_out, t["mlp_bwd"])
    dh2 = _mlp_input_grad(da, wf_ff_in.reshape(D_FF, D_MODEL), t["mlp_bwd"])
    ff_blocks = [dw_ff_in, dw_ff_out]
    (dx1, dgates, datt, dy, dw_glu, dw_ssm, dw_attn, dw_out, dg2, dg3), ff_recv = _merge_bwd(
        dh2, dx2, x1, o, y, att, ga, gs, g2, g3, wf["w_glu"], wf["w_ssm_branch"], wf["w_attn_branch"], wf["w_out"],
        t["merge_bwd"], _sibling_carry(ff_blocks))
    ff_sums, ff_sums_bf = add_sibling("ff", ff_blocks, ff_recv)
    merge_blocks = [_blocks_from_full(n, g) for n, g in zip(merge_names, (dw_glu, dw_attn, dw_ssm, dw_out))]
    (du, dbm_re, dbm_im, dcm_re, dcm_im, da_re, da_im, dd_skip), carried = _ssm_bwd(
        dy, u, h_re, h_im, in_re, in_im, a_re, a_im, bm_re, bm_im, cm_re, cm_im, dsk, t["ssm_chunk"],
        _join(_chips_carry(ff_sums_bf), _sibling_carry(merge_blocks)))
    ff_from_chips, merge_recv = carried[:2], carried[2:]
    merge_sums, merge_sums_bf = add_sibling("merge", merge_blocks, merge_recv)
    dbd_re, dbd_im, dlam_re, dlam_im, dldt_rep = _ssm_prep_bwd(lam_r, lam_i, ldt_rep, bd_re, bd_im, dbm_re, dbm_im, da_re, da_im)
    dlog_dt = _group_sum(dldt_rep.reshape(N_GROUPS, N_STATE))
    shapes = {n: view(n, w[n]).shape for n in SMALL}
    shapes["loss"] = (1,)
    small_grads = dict(
        norm_mix_post=dg2, norm_mlp_pre=dg3, norm_mlp_post=dg4, lam_re=dlam_re, lam_im=dlam_im, log_dt=dlog_dt,
        b_re=_block_diag_in_grad(dbd_re), b_im=_block_diag_in_grad(dbd_im),
        c_re=_block_diag_out_grad(dcm_re), c_im=_block_diag_out_grad(dcm_im), d_skip=dd_skip)
    packed_early = _pack({n: small_grads[n].reshape(shapes[n]) for n in SMALL_BEFORE_ATTN_BWD}, SMALL_BEFORE_ATTN_BWD)
    (dq, dkv, attn_small), carried = _attn_bwd(
        q, k, vv, datt, bucket, rel_b, sink, _join(_chips_carry(merge_sums_bf), _gather_carry([packed_early])))
    merge_from_chips, partials_early = carried[:-1], carried[-1]

    dparts = (dq, dkv, du, dgates)
    dw_in_t = _in_proj_weight_grad(h, dparts)
    in_blocks = [dw_in_t.reshape(N_DEV, IN_W // N_DEV, D_MODEL)]
    to_sibling = _sibling_carry(in_blocks)
    in_flight, token = _exchange_start(to_sibling, "w_in_sibling_start")
    n_tiles = xs.shape[0] // t["proj_bwd"]
    (grad_x, dg1), _ = _in_proj_input_grad(xs, g1 + token[0:1, 0:1], wf_in, dx1, dparts, t["proj_bwd"], 0, n_tiles, "in_proj_input_grad")
    late = dict(norm_mix_pre=dg1, rel_bias=attn_small[:, :N_BUCKETS, 0], sinks=attn_small[:, N_BUCKETS, 0], loss=loss_blk[0:1, 0])
    packed_late = _pack({n: late[n].reshape(shapes[n]) for n in SMALL_LATE}, SMALL_LATE)
    to_everyone = _everyone_carry([packed_late])
    late_in_flight, late_started = _exchange_start(to_everyone, "late_grads_start")
    in_blocks, in_recv = _exchange_wait(to_sibling, in_flight, [late_started], "w_in_sibling_wait")
    in_sums, in_sums_bf = add_sibling("w_in", in_blocks, in_recv)
    to_chips = _chips_carry(in_sums_bf)
    in_flight, chips_started = _exchange_start(to_chips, "w_in_chips_start")
    (packed_late,), (late_received,) = _exchange_wait(to_everyone, late_in_flight, [chips_started], "late_grads_wait")

    grads, deltas, new_m, new_v = {}, {}, {}, {}

    def adam_group(group, names, sums, received, after=()):
        outs = _adam_big(*[[local(d, n) for n in names] for d in (w, m, v)], sums, received, chip, groups[group],
                         "adam_" + group, after)
        for store, vals in zip((grads, deltas, new_m, new_v), outs):
            store.update({n: (o.T if n == "w_in" else o)[None] for n, o in zip(names, vals)})

    adam_group("ff", ff_names, ff_sums, ff_from_chips, [chips_started])
    adam_group("merge", merge_names, merge_sums, merge_from_chips, [chips_started])

    grads.update(_unpack(_sum_partials(partials_early, "sum_small_grads", [chips_started]), shapes, SMALL_BEFORE_ATTN_BWD))
    grads.update(_unpack(_sum_everyone(packed_late, late_received, 2 * chip + core, "sum_late_grads"), shapes, SMALL_LATE))
    loss = grads.pop("loss").reshape(())
    small_out = _adam_small(*[[view(n, d[n]) for n in SMALL] for d in (w, m, v)], [grads[n] for n in SMALL])
    for store, vals in zip((deltas, new_m, new_v), small_out):
        store.update(zip(SMALL, vals))
    for store in (grads, deltas, new_m, new_v):
        store.update({n: view(n, store[n]) for n in SWAPPED_SMALL})

    busy = [new_v["w_ff_out"], new_v["w_out"], deltas["norm_mix_pre"]]
    _, (in_from_chips,) = _exchange_wait(to_chips, in_flight, busy, "w_in_chips_wait")
    adam_group("w_in", ("w_in",), in_sums, [in_from_chips])

    return (loss, grad_x[None], *[grads[n] for n in ALL_WEIGHTS], *[deltas[n] for n in ALL_WEIGHTS],
            *[new_m[n] for n in ALL_WEIGHTS], *[new_v[n] for n in ALL_WEIGHTS])
```

```python
import math

import jax
import jax.numpy as jnp
import numpy as np
from jax import lax
from jax.experimental import pallas as pl
from jax.experimental.pallas import tpu as pltpu

F32 = jnp.float32
BF16 = jnp.bfloat16

D_MODEL = 1024
N_HEADS = 8
HEAD_DIM = 64
ATTN_W = 512
KV_W = 128
BLOCK = 128
N_BUCKETS = 32
SSM_W = 512
N_GROUPS = 32
N_STATE = 64
GROUP_CH = 16
STATES = N_GROUPS * N_STATE
D_FF = 4096
IN_W = 3328
SPLITS = (0, 512, 640, 768, 1280, 2304, 3328)
RMS_EPS = 1e-6
NEG_INF = -1e30
SUBLANES = 8
LANES = 128
SSM_LANE_BLOCK = 512
N_SSM_BLOCKS = STATES // SSM_LANE_BLOCK
GROUPS_PER_BLOCK = SSM_LANE_BLOCK // N_STATE
VMEM_BIG = 52 * 1024 * 1024
VMEM_MID = 40 * 1024 * 1024
VMEM_MAX = 60 * 1024 * 1024

ADAM_LR = 0.001
ADAM_B1 = 0.9
ADAM_B2 = 0.999
ADAM_EPS = 1e-08
ADAM_WD = 0.01
ADAM_STEP = 10

N_DEV = 8


def _dot(a, b):
    return jnp.dot(a, b, preferred_element_type=F32)


def _dot_nt(a, b):
    return lax.dot_general(a, b, (((1,), (1,)), ((), ())), preferred_element_type=F32)


def _dot_tn(a, b):
    return lax.dot_general(a, b, (((0,), (0,)), ((), ())), preferred_element_type=F32)


def _rms_scale(x):
    return lax.rsqrt(jnp.mean(x * x, axis=-1, keepdims=True) + RMS_EPS)


def _rms_bwd(dy, x, r, g):
    t = dy * g
    dx = r * t - x * (r * r * r) * jnp.mean(t * x, axis=-1, keepdims=True)
    dg = jnp.sum(dy * x * r, axis=0, keepdims=True)
    return dx, dg


def _const_spec(shape):
    nd = len(shape)
    return pl.BlockSpec(shape, lambda *_: (0,) * nd, pipeline_mode=pl.Buffered(1))


def _in_hbm(*arrays):
    return tuple(pltpu.with_memory_space_constraint(a, pltpu.HBM) for a in arrays)


def _hbm_out(shapes):
    if isinstance(shapes, (list, tuple)):
        return [_hbm_out(s) for s in shapes]
    return shapes if isinstance(shapes, pl.MemoryRef) else pltpu.HBM(shapes.shape, shapes.dtype)


def _whole(shape):
    nd = len(shape)
    return pl.BlockSpec(shape, lambda *_: (0,) * nd)


def _params(sem, vmem=None):
    return pltpu.CompilerParams(dimension_semantics=sem, vmem_limit_bytes=vmem)


MESH_IDS = pl.DeviceIdType.MESH
HBM_SPEC = pl.BlockSpec(memory_space=pl.ANY)


class _Carry:
    def __init__(self, inputs, out_shapes, sems, start, finish, middle=None):
        self.inputs, self.out_shapes, self.sems = list(inputs), list(out_shapes), list(sems)
        self.start, self.middle, self.finish = start, middle, finish


def _join(a, b):
    na_in, na_out, na_sem = len(a.inputs), len(a.out_shapes), len(a.sems)

    def both(phase):
        def run(ins, outs, sems):
            for carry, lo in ((a, True), (b, False)):
                part = (lambda seq, n: seq[:n] if lo else seq[n:])
                if getattr(carry, phase) is not None:
                    getattr(carry, phase)(part(ins, na_in), part(outs, na_out), part(sems, na_sem))
        return run

    middle = both("middle") if (a.middle or b.middle) else None
    return _Carry(a.inputs + b.inputs, a.out_shapes + b.out_shapes, a.sems + b.sems, both("start"), both("finish"), middle)


def _hosted_call(body, carry, edge, *, name, grid, in_specs, out_specs, out_shape, scratch_shapes, compiler_params, inputs):
    n_in, n_out = len(in_specs), len(out_specs)
    inputs = [a if s.memory_space == pltpu.SMEM else _in_hbm(a)[0] for a, s in zip(inputs, in_specs)]
    out_shape = _hbm_out(list(out_shape))
    if carry is None:
        outs = pl.pallas_call(body, name=name, grid=grid, in_specs=in_specs, out_specs=out_specs, out_shape=out_shape,
                              scratch_shapes=scratch_shapes, compiler_params=compiler_params)(*inputs)
        return list(outs), []
    c_in, c_out, c_sem = len(carry.inputs), len(carry.out_shapes), len(carry.sems)

    def wrapped(*refs):
        ins, refs = refs[:n_in], refs[n_in:]
        cins, refs = refs[:c_in], refs[c_in:]
        outs, refs = refs[:n_out], refs[n_out:]
        couts, refs = refs[:c_out], refs[c_out:]
        scratch, csems = refs[:len(refs) - c_sem], refs[len(refs) - c_sem:]
        first, middle, last = edge()

        @pl.when(first)
        def _():
            carry.start(cins, couts, csems)

        body(*ins, *outs, *scratch)

        if carry.middle is not None:
            @pl.when(middle)
            def _():
                carry.middle(cins, couts, csems)

        @pl.when(last)
        def _():
            carry.finish(cins, couts, csems)

    outs = pl.pallas_call(
        wrapped, name=name, grid=grid, in_specs=list(in_specs) + [HBM_SPEC] * c_in,
        out_specs=list(out_specs) + [HBM_SPEC] * c_out, out_shape=out_shape + _hbm_out(carry.out_shapes),
        scratch_shapes=list(scratch_shapes) + carry.sems, compiler_params=compiler_params)(*inputs, *_in_hbm(*carry.inputs))
    return list(outs[:n_out]), list(outs[n_out:])


def _pass_on_step(n_steps):
    return max(0, min((7 * n_steps) // 8, n_steps - 2))


def _edge_1d(n_steps, pass_on_last=False):
    middle = n_steps - 1 if pass_on_last else _pass_on_step(n_steps)
    return lambda: (pl.program_id(0) == 0, pl.program_id(0) == middle, pl.program_id(0) == n_steps - 1)


def _edge_2d(n0, n1):
    def edge():
        step = pl.program_id(0) * n1 + pl.program_id(1)
        return step == 0, step == _pass_on_step(n0 * n1), step == n0 * n1 - 1
    return edge


def _run_carry(carry, name):
    c_in, c_out = len(carry.inputs), len(carry.out_shapes)

    def body(*refs):
        ins, outs, sems = refs[:c_in], refs[c_in:c_in + c_out], refs[c_in + c_out:]
        carry.start(ins, outs, sems)
        if carry.middle is not None:
            carry.middle(ins, outs, sems)
        carry.finish(ins, outs, sems)

    return pl.pallas_call(body, name=name, in_specs=[HBM_SPEC] * c_in, out_specs=[HBM_SPEC] * c_out,
                          out_shape=_hbm_out(carry.out_shapes), scratch_shapes=carry.sems)(*_in_hbm(*carry.inputs))


def _in_proj_fwd(x, g1, w_in_t, tile, carry=None):
    T = x.shape[0]

    def body(x_ref, g_ref, w_ref, q_ref, k_ref, v_ref, u_ref, ga_ref, gs_ref, h_ref):
        xv = x_ref[...]
        h = (xv * _rms_scale(xv) * g_ref[...]).astype(BF16)
        h_ref[...] = h
        outs = (q_ref, k_ref, v_ref, u_ref, ga_ref, gs_ref)
        for p, o_ref in enumerate(outs):
            o_ref[...] = _dot_nt(h, w_ref[SPLITS[p]:SPLITS[p + 1], :]).astype(o_ref.dtype)

    widths = [SPLITS[p + 1] - SPLITS[p] for p in range(6)] + [D_MODEL]
    dtypes = [BF16, BF16, BF16, F32, F32, F32, BF16]
    return _hosted_call(
        body, carry, _edge_1d(T // tile), name="in_proj_fwd", grid=(T // tile,),
        in_specs=[pl.BlockSpec((tile, D_MODEL), lambda i: (i, 0)), _const_spec((1, D_MODEL)), _const_spec((IN_W, D_MODEL))],
        out_specs=[pl.BlockSpec((tile, w), lambda i: (i, 0)) for w in widths],
        out_shape=[jax.ShapeDtypeStruct((T, w), dt) for w, dt in zip(widths, dtypes)],
        scratch_shapes=[], compiler_params=_params(("arbitrary",), VMEM_MID), inputs=(x, g1, w_in_t))


PROJ_PARTS = (512, 256, 512, 2048)
PROJ_GRAD_BLOCK = 256


def _in_proj_weight_grad(h, dparts):
    T = h.shape[0]
    blocks = [wd // PROJ_GRAD_BLOCK for wd in PROJ_PARTS]
    starts = [sum(blocks[:p]) for p in range(len(blocks))]

    def body(h_ref, *refs):
        part_refs, o_ref = refs[:-1], refs[-1]
        j = pl.program_id(0)
        for p_ref, start, count in zip(part_refs, starts, blocks):
            @pl.when((j >= start) & (j < start + count))
            def _(p_ref=p_ref):
                o_ref[...] = _dot_tn(p_ref[...], h_ref[...])

    def part_spec(start, count):
        return pl.BlockSpec((T, PROJ_GRAD_BLOCK), lambda j: (0, jnp.clip(j - start, 0, count - 1)))

    return pl.pallas_call(
        body, name="in_proj_weight_grad", grid=(sum(blocks),),
        in_specs=[_const_spec((T, D_MODEL))] + [part_spec(s, c) for s, c in zip(starts, blocks)],
        out_specs=pl.BlockSpec((PROJ_GRAD_BLOCK, D_MODEL), lambda j: (j, 0)),
        out_shape=_hbm_out(jax.ShapeDtypeStruct((IN_W, D_MODEL), F32)),
        compiler_params=_params(("arbitrary",), VMEM_MID),
    )(*_in_hbm(h, *dparts))


def _in_proj_input_grad(x, g1, w_in_t, dx1, dparts, tile, first_tile, n_tiles, name, carry=None):
    offsets = [sum(PROJ_PARTS[:p]) for p in range(len(PROJ_PARTS))]

    def body(x_ref, g_ref, w_ref, dx1_ref, *refs):
        part_refs, (gx_ref, dg_ref) = refs[:len(PROJ_PARTS)], refs[len(PROJ_PARTS):]
        i = pl.program_id(0)
        xv = x_ref[...]
        r = _rms_scale(xv)
        g = g_ref[...]
        dh = sum(_dot(p_ref[...], w_ref[off:off + wd, :]) for p_ref, off, wd in zip(part_refs, offsets, PROJ_PARTS))
        dxn, dg = _rms_bwd(dh, xv, r, g)
        gx_ref[...] = dx1_ref[...] + dxn

        @pl.when(i == 0)
        def _():
            dg_ref[...] = dg

        @pl.when(i > 0)
        def _():
            dg_ref[...] += dg

    tok = lambda wd: pl.BlockSpec((tile, wd), lambda i: (i + first_tile, 0))
    return _hosted_call(
        body, carry, _edge_1d(n_tiles), name=name, grid=(n_tiles,),
        in_specs=[tok(D_MODEL), _const_spec((1, D_MODEL)), _const_spec((IN_W, D_MODEL)), tok(D_MODEL)] + [tok(wd) for wd in PROJ_PARTS],
        out_specs=[pl.BlockSpec((tile, D_MODEL), lambda i: (i, 0)), pl.BlockSpec((1, D_MODEL), lambda i: (0, 0))],
        out_shape=[jax.ShapeDtypeStruct((n_tiles * tile, D_MODEL), F32), jax.ShapeDtypeStruct((1, D_MODEL), F32)],
        scratch_shapes=[], compiler_params=_params(("arbitrary",), VMEM_MID), inputs=(x, g1, w_in_t, dx1, *dparts))


def _bucket_table():
    qi = np.arange(BLOCK)[:, None]
    kj = np.arange(2 * BLOCK)[None, :]
    dist = qi + BLOCK - kj
    max_exact = N_BUCKETS // 2
    d = np.maximum(dist, 0)
    df = np.maximum(d, 1).astype(np.float32)
    large = max_exact + (np.log(df / np.float32(max_exact)) / np.float32(math.log(BLOCK / max_exact))
                         * np.float32(N_BUCKETS - max_exact)).astype(np.int32)
    large = np.minimum(large, N_BUCKETS - 1)
    bucket = np.where(d < max_exact, d, large)
    return np.where((dist >= 0) & (dist < BLOCK), bucket, -1).astype(np.int32)


def _build_bias(bucket_ref, rb_ref, bias_ref):
    bk = bucket_ref[...]
    for h in range(N_HEADS):
        def add(b, acc, h=h):
            return acc + jnp.where(bk == b, rb_ref[h, b], 0.0)
        bias_ref[h] = lax.fori_loop(0, N_BUCKETS, add, jnp.zeros((BLOCK, 2 * BLOCK), F32))


def _kv_variants(prev_ref, cur_ref):
    cat = jnp.concatenate([prev_ref[...], cur_ref[...]], axis=0)
    lo = lax.broadcasted_iota(jnp.int32, cat.shape, 1) < HEAD_DIM
    zero = jnp.zeros_like(cat)
    head0_lo = jnp.where(lo, cat, zero)
    head1_hi = jnp.where(lo, zero, cat)
    return ((head0_lo, pltpu.roll(head0_lo, HEAD_DIM, 1)), (pltpu.roll(head1_hi, HEAD_DIM, 1), head1_hi))


def _merge_kv_grads(g):
    lo = lax.broadcasted_iota(jnp.int32, g[0][0].shape, 1) < HEAD_DIM
    return jnp.where(lo, g[0][0] + pltpu.roll(g[0][1], HEAD_DIM, 1), g[1][1] + pltpu.roll(g[1][0], HEAD_DIM, 1))


def _head_lanes(h):
    return slice((h // 2) * LANES, (h // 2 + 1) * LANES)


def _attn_probs(q_ref, kvar, bias_ref, sk_ref, valid, s_ref):
    for h in range(N_HEADS):
        s_ref[h] = _dot_nt(q_ref[:, _head_lanes(h)], kvar[h // 4][h % 2])
    head = lax.broadcasted_iota(jnp.int32, (N_HEADS, 1, 1), 0)
    sink = jnp.zeros((N_HEADS, 1, 1), F32)
    for h in range(N_HEADS):
        sink = jnp.where(head == h, sk_ref[0, h], sink)
    s = jnp.where(valid[None], s_ref[...] * (HEAD_DIM ** -0.5) + bias_ref[...], NEG_INF)
    m = jnp.maximum(jnp.max(s, axis=-1, keepdims=True), sink)
    p = jnp.exp(s - m)
    e_sink = jnp.exp(sink - m)
    inv = 1.0 / (jnp.sum(p, axis=-1, keepdims=True) + e_sink)
    return p * inv, e_sink * inv


def _attn_valid(bucket_ref, n):
    col = lax.broadcasted_iota(jnp.int32, (BLOCK, 2 * BLOCK), 1)
    return (bucket_ref[...] >= 0) & ((n > 0) | (col >= BLOCK))


def _attn_fwd(q, k, v, bucket, rel_bias, sinks, carry=None):
    T = q.shape[0]
    nb = T // BLOCK

    def body(q_ref, kc_ref, kp_ref, vc_ref, vp_ref, bucket_ref, rb_ref, sk_ref, o_ref, bias_ref, s_ref, p_ref):
        n = pl.program_id(0)

        @pl.when(n == 0)
        def _():
            _build_bias(bucket_ref, rb_ref, bias_ref)

        kvar = _kv_variants(kp_ref, kc_ref)
        vvar = _kv_variants(vp_ref, vc_ref)
        pr, _ = _attn_probs(q_ref, kvar, bias_ref, sk_ref, _attn_valid(bucket_ref, n), s_ref)
        p_ref[...] = pr.astype(BF16)
        for m in range(N_HEADS // 2):
            acc = _dot(p_ref[2 * m], vvar[m // 2][0]) + _dot(p_ref[2 * m + 1], vvar[m // 2][1])
            o_ref[:, m * LANES:(m + 1) * LANES] = acc.astype(o_ref.dtype)

    cur = lambda w: pl.BlockSpec((BLOCK, w), lambda n: (n, 0))
    prev = lambda w: pl.BlockSpec((BLOCK, w), lambda n: (jnp.maximum(n - 1, 0), 0))
    smem = pl.BlockSpec(memory_space=pltpu.SMEM)
    return _hosted_call(
        body, carry, _edge_1d(nb, pass_on_last=True), name="attn_fwd", grid=(nb,),
        in_specs=[cur(ATTN_W), cur(KV_W), prev(KV_W), cur(KV_W), prev(KV_W), _const_spec((BLOCK, 2 * BLOCK)), smem, smem],
        out_specs=[cur(ATTN_W)],
        out_shape=[jax.ShapeDtypeStruct((T, ATTN_W), BF16)],
        scratch_shapes=[pltpu.VMEM((N_HEADS, BLOCK, 2 * BLOCK), F32), pltpu.VMEM((N_HEADS, BLOCK, 2 * BLOCK), F32),
                        pltpu.VMEM((N_HEADS, BLOCK, 2 * BLOCK), BF16)],
        compiler_params=_params(("arbitrary",)), inputs=(q, k, k, v, v, bucket, rel_bias, sinks))


ATTN_SMALL_ROWS = N_BUCKETS + SUBLANES


def _attn_bwd(q, k, v, datt, bucket, rel_bias, sinks, carry=None):
    T = q.shape[0]
    nb = T // BLOCK

    def body(q_ref, do_ref, kc_ref, kp_ref, vc_ref, vp_ref, bucket_ref, rb_ref, sk_ref,
             dq_ref, dkv_ref, small_ref, bias_ref, ds_sum_ref, dsink_ref, kcarry_ref, vcarry_ref,
             s_ref, dp_ref, p_ref, dsc_ref):
        n = pl.program_id(0)

        @pl.when(n == 0)
        def _():
            _build_bias(bucket_ref, rb_ref, bias_ref)
            ds_sum_ref[...] = jnp.zeros_like(ds_sum_ref)
            dsink_ref[...] = jnp.zeros_like(dsink_ref)
            kcarry_ref[...] = jnp.zeros_like(kcarry_ref)
            vcarry_ref[...] = jnp.zeros_like(vcarry_ref)

        @pl.when(n < nb)
        def _():
            kvar = _kv_variants(kp_ref, kc_ref)
            vvar = _kv_variants(vp_ref, vc_ref)
            pr, p_sink = _attn_probs(q_ref, kvar, bias_ref, sk_ref, _attn_valid(bucket_ref, n), s_ref)
            for h in range(N_HEADS):
                dp_ref[h] = _dot_nt(do_ref[:, _head_lanes(h)], vvar[h // 4][h % 2])
            dp = dp_ref[...]
            dsum = jnp.sum(pr * dp, axis=-1, keepdims=True)
            ds = pr * (dp - dsum)
            ds_sum_ref[...] += ds
            dsink_ref[...] -= jnp.sum(p_sink * dsum, axis=1, keepdims=True)
            dsc_ref[...] = (ds * (HEAD_DIM ** -0.5)).astype(BF16)
            p_ref[...] = pr.astype(BF16)
            for m in range(N_HEADS // 2):
                dqm = _dot(dsc_ref[2 * m], kvar[m // 2][0]) + _dot(dsc_ref[2 * m + 1], kvar[m // 2][1])
                dq_ref[:, m * LANES:(m + 1) * LANES] = dqm.astype(dq_ref.dtype)
            dk_var = [[None, None], [None, None]]
            dv_var = [[None, None], [None, None]]
            for kvh in range(2):
                for e in range(2):
                    heads = [h for h in range(N_HEADS) if h // 4 == kvh and h % 2 == e]
                    dk_var[kvh][e] = sum(_dot_tn(dsc_ref[h], q_ref[:, _head_lanes(h)]) for h in heads)
                    dv_var[kvh][e] = sum(_dot_tn(p_ref[h], do_ref[:, _head_lanes(h)]) for h in heads)
            dk_cat = _merge_kv_grads(dk_var)
            dv_cat = _merge_kv_grads(dv_var)

            @pl.when(n > 0)
            def _():
                dkv_ref[:, :KV_W] = (kcarry_ref[...] + dk_cat[:BLOCK]).astype(BF16)
                dkv_ref[:, KV_W:] = (vcarry_ref[...] + dv_cat[:BLOCK]).astype(BF16)

            kcarry_ref[...] = dk_cat[BLOCK:]
            vcarry_ref[...] = dv_cat[BLOCK:]

        @pl.when(n == nb)
        def _():
            dkv_ref[:, :KV_W] = kcarry_ref[...].astype(BF16)
            dkv_ref[:, KV_W:] = vcarry_ref[...].astype(BF16)
            bk = bucket_ref[...]
            row = lax.broadcasted_iota(jnp.int32, (N_HEADS, ATTN_SMALL_ROWS, LANES), 1)

            def add(b, acc):
                masked = jnp.where((bk == b)[None], ds_sum_ref[...], 0.0)
                val = jnp.sum(jnp.sum(masked, axis=1, keepdims=True), axis=2, keepdims=True)
                return acc + jnp.where(row == b, val, 0.0)

            small_ref[...] = lax.fori_loop(0, N_BUCKETS, add, jnp.where(row == N_BUCKETS, dsink_ref[...], 0.0))

    last = nb - 1
    cur = lambda w: pl.BlockSpec((BLOCK, w), lambda n: (jnp.minimum(n, last), 0))
    prev = lambda w: pl.BlockSpec((BLOCK, w), lambda n: (jnp.clip(n - 1, 0, last), 0))
    smem = pl.BlockSpec(memory_space=pltpu.SMEM)
    return _hosted_call(
        body, carry, _edge_1d(nb + 1), name="attn_bwd", grid=(nb + 1,),
        in_specs=[cur(ATTN_W), cur(ATTN_W), cur(KV_W), prev(KV_W), cur(KV_W), prev(KV_W),
                  _const_spec((BLOCK, 2 * BLOCK)), smem, smem],
        out_specs=[cur(ATTN_W), prev(2 * KV_W), pl.BlockSpec((N_HEADS, ATTN_SMALL_ROWS, LANES), lambda n: (0, 0, 0))],
        out_shape=[jax.ShapeDtypeStruct((T, ATTN_W), BF16), jax.ShapeDtypeStruct((T, 2 * KV_W), BF16),
                   jax.ShapeDtypeStruct((N_HEADS, ATTN_SMALL_ROWS, LANES), F32)],
        scratch_shapes=[pltpu.VMEM((N_HEADS, BLOCK, 2 * BLOCK), F32), pltpu.VMEM((N_HEADS, BLOCK, 2 * BLOCK), F32),
                        pltpu.VMEM((N_HEADS, 1, 1), F32), pltpu.VMEM((BLOCK, KV_W), F32), pltpu.VMEM((BLOCK, KV_W), F32),
                        pltpu.VMEM((N_HEADS, BLOCK, 2 * BLOCK), F32), pltpu.VMEM((N_HEADS, BLOCK, 2 * BLOCK), F32),
                        pltpu.VMEM((N_HEADS, BLOCK, 2 * BLOCK), BF16), pltpu.VMEM((N_HEADS, BLOCK, 2 * BLOCK), BF16)],
        compiler_params=_params(("arbitrary",)), inputs=(q, datt, k, k, v, v, bucket, rel_bias, sinks))


SCAN_UNROLL = 4


def _cmul(ar, ai, br, bi):
    return ar * br - ai * bi, ar * bi + ai * br


def _cmul_conj(ar, ai, br, bi):
    return ar * br + ai * bi, ar * bi - ai * br


def _ssm_discretize(lr, li, ldt):
    dt = jnp.exp(ldt)
    mag = jnp.exp(lr * dt)
    ab_re = mag * jnp.cos(li * dt)
    ab_im = mag * jnp.sin(li * dt)
    nr = ab_re - 1.0
    den = lr * lr + li * li
    f_re = (nr * lr + ab_im * li) / den
    f_im = (ab_im * lr - nr * li) / den
    return ab_re, ab_im, f_re, f_im


def _ssm_prep(lam_re, lam_im, ldt_rep, bd_re, bd_im):
    def body(lr_ref, li_ref, ldt_ref, bdr_ref, bdi_ref, ar_ref, ai_ref, br_ref, bi_ref):
        ab_re, ab_im, f_re, f_im = _ssm_discretize(lr_ref[...], li_ref[...], ldt_ref[...])
        ar_ref[...] = ab_re
        ai_ref[...] = ab_im
        bdr, bdi = bdr_ref[0], bdi_ref[0]
        br_ref[0] = (bdr * f_re - bdi * f_im).astype(BF16)
        bi_ref[0] = (bdi * f_re + bdr * f_im).astype(BF16)

    row = pl.BlockSpec((1, SSM_LANE_BLOCK), lambda j: (0, j))
    mat = pl.BlockSpec((1, LANES, SSM_LANE_BLOCK), lambda j: (j, 0, 0))
    return pl.pallas_call(
        body, name="ssm_prep", grid=(N_SSM_BLOCKS,),
        in_specs=[row, row, row, mat, mat], out_specs=[row, row, mat, mat],
        out_shape=[jax.ShapeDtypeStruct((1, STATES), F32)] * 2 + [jax.ShapeDtypeStruct((N_SSM_BLOCKS, LANES, SSM_LANE_BLOCK), BF16)] * 2,
        compiler_params=_params(("arbitrary",)),
    )(*_in_hbm(lam_re, lam_im, ldt_rep, bd_re, bd_im))


def _ssm_prep_bwd(lam_re, lam_im, ldt_rep, bd_re, bd_im, dbr, dbi, da_re, da_im):
    def body(lr_ref, li_ref, ldt_ref, bdr_ref, bdi_ref, dbr_ref, dbi_ref, dar_ref, dai_ref,
             dbdr_ref, dbdi_ref, dlr_ref, dli_ref, dldt_ref):
        lr, li, ldt = lr_ref[...], li_ref[...], ldt_ref[...]
        (_, _, f_re, f_im), vjp = jax.vjp(_ssm_discretize, lr, li, ldt)
        bdr, bdi, gbr, gbi = bdr_ref[0], bdi_ref[0], dbr_ref[0], dbi_ref[0]
        dbdr_ref[0] = gbr * f_re + gbi * f_im
        dbdi_ref[0] = gbi * f_re - gbr * f_im
        df_re = jnp.sum(gbr * bdr + gbi * bdi, axis=0, keepdims=True)
        df_im = jnp.sum(gbi * bdr - gbr * bdi, axis=0, keepdims=True)
        dlr, dli, dldt = vjp((dar_ref[...], dai_ref[...], df_re, df_im))
        dlr_ref[...] = dlr
        dli_ref[...] = dli
        dldt_ref[...] = dldt

    row = pl.BlockSpec((1, SSM_LANE_BLOCK), lambda j: (0, j))
    mat = pl.BlockSpec((1, LANES, SSM_LANE_BLOCK), lambda j: (j, 0, 0))
    mat_shape = jax.ShapeDtypeStruct((N_SSM_BLOCKS, LANES, SSM_LANE_BLOCK), F32)
    row_shape = jax.ShapeDtypeStruct((1, STATES), F32)
    return pl.pallas_call(
        body, name="ssm_prep_bwd", grid=(N_SSM_BLOCKS,),
        in_specs=[row, row, row, mat, mat, mat, mat, row, row], out_specs=[mat, mat, row, row, row],
        out_shape=[mat_shape, mat_shape, row_shape, row_shape, row_shape],
        compiler_params=_params(("arbitrary",)),
    )(*_in_hbm(lam_re, lam_im, ldt_rep, bd_re, bd_im, dbr, dbi, da_re, da_im))


def _group_sum(x):
    def body(x_ref, o_ref):
        o_ref[...] = jnp.sum(x_ref[...], axis=1, keepdims=True)
    return pl.pallas_call(body, name="ssm_group_sum", grid=(1,), in_specs=[_whole(x.shape)], out_specs=_whole((N_GROUPS, 1)),
                          out_shape=jax.ShapeDtypeStruct((N_GROUPS, 1), F32))(*_in_hbm(x))


def _power_table(ar, ai, p_re_ref, p_im_ref, steps):
    shape = (SUBLANES, SSM_LANE_BLOCK)
    p_re_ref[0:SUBLANES] = jnp.broadcast_to(ar, shape)
    p_im_ref[0:SUBLANES] = jnp.broadcast_to(ai, shape)
    m = 1
    while m < steps:
        rows = m * SUBLANES
        top_re = p_re_ref[rows - SUBLANES:rows]
        top_im = p_im_ref[rows - SUBLANES:rows]
        cur_re = p_re_ref[0:rows].reshape(m, SUBLANES, SSM_LANE_BLOCK)
        cur_im = p_im_ref[0:rows].reshape(m, SUBLANES, SSM_LANE_BLOCK)
        nxt_re, nxt_im = _cmul(cur_re, cur_im, top_re[None], top_im[None])
        p_re_ref[rows:2 * rows] = nxt_re.reshape(rows, SSM_LANE_BLOCK)
        p_im_ref[rows:2 * rows] = nxt_im.reshape(rows, SSM_LANE_BLOCK)
        m *= 2


def _to_segments(src_ref, dst_ref, steps):
    for s in range(SUBLANES):
        dst_ref[pl.ds(s, steps, stride=SUBLANES), :] = src_ref[s * steps:(s + 1) * steps, :]


def _from_segments(src_ref, dst_ref, steps):
    for s in range(SUBLANES):
        dst_ref[s * steps:(s + 1) * steps, :] = src_ref[pl.ds(s, steps, stride=SUBLANES), :]


def _segment_carries(e_re, e_im, an_re, an_im, c_re, c_im, reverse):
    order = range(SUBLANES - 1, -1, -1) if reverse else range(SUBLANES)
    ins_re, ins_im = [None] * SUBLANES, [None] * SUBLANES
    for s in order:
        ins_re[s], ins_im[s] = c_re, c_im
        pr, pi = _cmul(an_re, an_im, c_re, c_im)
        c_re = e_re[s:s + 1] + pr
        c_im = e_im[s:s + 1] + pi
    return jnp.concatenate(ins_re, axis=0), jnp.concatenate(ins_im, axis=0), c_re, c_im


def _ssm_fwd(u, a_re, a_im, b_re, b_im, c_re, c_im, d_skip, chunk, carry=None):
    T = u.shape[0]
    nc = T // chunk
    steps = chunk // SUBLANES
    blk = SSM_LANE_BLOCK

    def body(u_ref, ar_ref, ai_ref, br_ref, bi_ref, cr_ref, ci_ref, dk_ref,
             y_ref, hr_ref, hi_ref, inr_ref, ini_ref, useg_ref, yseg_ref, pr_ref, pi_ref, carry_ref):
        c = pl.program_id(1)
        ar, ai = ar_ref[...], ai_ref[...]

        @pl.when(c == 0)
        def _():
            _power_table(ar, ai, pr_ref, pi_ref, steps)
            carry_ref[...] = jnp.zeros_like(carry_ref)

        _to_segments(u_ref, useg_ref, steps)
        ub = useg_ref[...].astype(BF16)
        hr_ref[...] = _dot(ub, br_ref[0])
        hi_ref[...] = _dot(ub, bi_ref[0])
        first = slice(0, SUBLANES)

        def scan(t4, prev):
            for j in range(SCAN_UNROLL):
                rows = pl.ds(pl.multiple_of((t4 * SCAN_UNROLL + j) * SUBLANES, SUBLANES), SUBLANES)
                pr, pi = _cmul(pr_ref[first, :], pi_ref[first, :], prev[0], prev[1])
                prev = (pr + hr_ref[rows, :], pi + hi_ref[rows, :])
                hr_ref[rows, :] = prev[0]
                hi_ref[rows, :] = prev[1]
            return prev

        zero = jnp.zeros((SUBLANES, blk), F32)
        lax.fori_loop(0, steps // SCAN_UNROLL, scan, (zero, zero))

        top = slice(chunk - SUBLANES, chunk)
        in_re, in_im, out_re, out_im = _segment_carries(
            hr_ref[top, :], hi_ref[top, :], pr_ref[top, :][0:1], pi_ref[top, :][0:1],
            carry_ref[0:1, :], carry_ref[1:2, :], reverse=False)
        carry_ref[0:1, :] = out_re
        carry_ref[1:2, :] = out_im
        inr_ref[...] = in_re
        ini_ref[...] = in_im

        def fix(t4, _):
            for j in range(SCAN_UNROLL):
                rows = pl.ds(pl.multiple_of((t4 * SCAN_UNROLL + j) * SUBLANES, SUBLANES), SUBLANES)
                fr, fi = _cmul(pr_ref[rows, :], pi_ref[rows, :], in_re, in_im)
                hr_ref[rows, :] += fr
                hi_ref[rows, :] += fi
            return 0

        lax.fori_loop(0, steps // SCAN_UNROLL, fix, 0)

        yseg_ref[...] = _dot(hr_ref[...].astype(BF16), cr_ref[0]) - _dot(hi_ref[...].astype(BF16), ci_ref[0])
        _from_segments(yseg_ref, y_ref, steps)
        y_ref[...] += dk_ref[...] * u_ref[...]

    row = pl.BlockSpec((1, blk), lambda j, c: (0, j))
    b_mat = pl.BlockSpec((1, LANES, blk), lambda j, c: (j, 0, 0))
    c_mat = pl.BlockSpec((1, blk, LANES), lambda j, c: (j, 0, 0))
    tok = pl.BlockSpec((chunk, LANES), lambda j, c: (c, j))
    state = pl.BlockSpec((chunk, blk), lambda j, c: (c, j))
    enter = pl.BlockSpec((SUBLANES, blk), lambda j, c: (c, j))
    return _hosted_call(
        body, carry, _edge_2d(N_SSM_BLOCKS, nc), name="ssm_fwd", grid=(N_SSM_BLOCKS, nc),
        in_specs=[tok, row, row, b_mat, b_mat, c_mat, c_mat, pl.BlockSpec((1, LANES), lambda j, c: (0, j))],
        out_specs=[tok, state, state, enter, enter],
        out_shape=[jax.ShapeDtypeStruct((T, SSM_W), F32), jax.ShapeDtypeStruct((T, STATES), F32),
                   jax.ShapeDtypeStruct((T, STATES), F32), jax.ShapeDtypeStruct((nc * SUBLANES, STATES), F32),
                   jax.ShapeDtypeStruct((nc * SUBLANES, STATES), F32)],
        scratch_shapes=[pltpu.VMEM((chunk, LANES), F32), pltpu.VMEM((chunk, LANES), F32),
                        pltpu.VMEM((chunk, blk), F32), pltpu.VMEM((chunk, blk), F32), pltpu.VMEM((SUBLANES, blk), F32)],
        compiler_params=_params(("arbitrary", "arbitrary"), VMEM_MID),
        inputs=(u, a_re, a_im, b_re, b_im, c_re, c_im, d_skip))


def _ssm_bwd(dy, u, h_re, h_im, in_re, in_im, a_re, a_im, b_re, b_im, c_re, c_im, d_skip, chunk, carry=None):
    T = u.shape[0]
    nc = T // chunk
    steps = chunk // SUBLANES
    blk = SSM_LANE_BLOCK

    def body(dy_ref, u_ref, hr_ref, hi_ref, inr_ref, ini_ref, ar_ref, ai_ref, br_ref, bi_ref, cr_ref, ci_ref, dk_ref,
             du_ref, dbr_ref, dbi_ref, dcr_ref, dci_ref, dar_ref, dai_ref, ddk_ref,
             dyseg_ref, useg_ref, duseg_ref, gr_ref, gi_ref, pr_ref, pi_ref, carry_ref, accr_ref, acci_ref):
        c = pl.program_id(1)
        ar, ai = ar_ref[...], ai_ref[...]

        @pl.when(c == 0)
        def _():
            _power_table(ar, ai, pr_ref, pi_ref, steps)
            carry_ref[...] = jnp.zeros_like(carry_ref)
            accr_ref[...] = jnp.zeros_like(accr_ref)
            acci_ref[...] = jnp.zeros_like(acci_ref)

        _to_segments(dy_ref, dyseg_ref, steps)
        _to_segments(u_ref, useg_ref, steps)
        dyb = dyseg_ref[...].astype(BF16)
        ub = useg_ref[...].astype(BF16)
        gr_ref[...] = _dot_nt(dyb, cr_ref[0])
        gi_ref[...] = -_dot_nt(dyb, ci_ref[0])
        dcr = _dot_tn(hr_ref[...].astype(BF16), dyb)
        dci = -_dot_tn(hi_ref[...].astype(BF16), dyb)
        ddk = jnp.sum(dy_ref[...] * u_ref[...], axis=0, keepdims=True)

        first = slice(0, SUBLANES)

        def scan(k4, nxt):
            for j in range(SCAN_UNROLL):
                t = steps - 1 - (k4 * SCAN_UNROLL + j)
                rows = pl.ds(pl.multiple_of(t * SUBLANES, SUBLANES), SUBLANES)
                pr, pi = _cmul_conj(pr_ref[first, :], pi_ref[first, :], nxt[0], nxt[1])
                nxt = (pr + gr_ref[rows, :], pi + gi_ref[rows, :])
                gr_ref[rows, :] = nxt[0]
                gi_ref[rows, :] = nxt[1]
            return nxt

        top = slice(chunk - SUBLANES, chunk)
        zero = jnp.zeros((SUBLANES, blk), F32)
        lax.fori_loop(0, steps // SCAN_UNROLL, scan, (zero, zero))

        gin_re, gin_im, out_re, out_im = _segment_carries(
            gr_ref[0:SUBLANES, :], gi_ref[0:SUBLANES, :], pr_ref[top, :][0:1], -pi_ref[top, :][0:1],
            carry_ref[0:1, :], carry_ref[1:2, :], reverse=True)
        carry_ref[0:1, :] = out_re
        carry_ref[1:2, :] = out_im

        def fix_row(rows, prow, hp_re, hp_im, acc):
            fr, fi = _cmul_conj(pr_ref[prow, :], pi_ref[prow, :], gin_re, gin_im)
            g_re = gr_ref[rows, :] + fr
            g_im = gi_ref[rows, :] + fi
            gr_ref[rows, :] = g_re
            gi_ref[rows, :] = g_im
            return acc[0] + g_re * hp_re + g_im * hp_im, acc[1] + g_im * hp_re - g_re * hp_im

        def fix_at(t, acc):
            aligned = (lambda r: r * SUBLANES) if isinstance(t, int) else (lambda r: pl.multiple_of(r * SUBLANES, SUBLANES))
            rows, before, prow = (pl.ds(aligned(r), SUBLANES) for r in (t, t - 1, steps - 1 - t))
            return fix_row(rows, prow, hr_ref[before, :], hi_ref[before, :], acc)

        def fix(t4, acc):
            for j in range(SCAN_UNROLL):
                acc = fix_at(t4 * SCAN_UNROLL + j, acc)
            return acc

        acc = fix_row(first, top, inr_ref[...], ini_ref[...], (accr_ref[...], acci_ref[...]))
        for t in range(1, SCAN_UNROLL):
            acc = fix_at(t, acc)
        acc_re, acc_im = lax.fori_loop(1, steps // SCAN_UNROLL, fix, acc)
        accr_ref[...] = acc_re
        acci_ref[...] = acc_im

        gbr = gr_ref[...].astype(BF16)
        gbi = gi_ref[...].astype(BF16)
        duseg_ref[...] = _dot_nt(gbr, br_ref[0]) + _dot_nt(gbi, bi_ref[0])
        _from_segments(duseg_ref, dyseg_ref, steps)
        du_ref[...] = (dyseg_ref[...] + dk_ref[...] * dy_ref[...]).astype(BF16)
        dbr = _dot_tn(ub, gbr)
        dbi = _dot_tn(ub, gbi)

        @pl.when(c == 0)
        def _():
            dbr_ref[0] = dbr
            dbi_ref[0] = dbi
            dcr_ref[0] = dcr
            dci_ref[0] = dci
            ddk_ref[...] = ddk

        @pl.when(c > 0)
        def _():
            dbr_ref[0] += dbr
            dbi_ref[0] += dbi
            dcr_ref[0] += dcr
            dci_ref[0] += dci
            ddk_ref[...] += ddk

        @pl.when(c == nc - 1)
        def _():
            dar_ref[...] = jnp.sum(acc_re, axis=0, keepdims=True)
            dai_ref[...] = jnp.sum(acc_im, axis=0, keepdims=True)

    rev = lambda c: nc - 1 - c
    row = pl.BlockSpec((1, blk), lambda j, c: (0, j))
    b_mat = pl.BlockSpec((1, LANES, blk), lambda j, c: (j, 0, 0))
    c_mat = pl.BlockSpec((1, blk, LANES), lambda j, c: (j, 0, 0))
    tok = pl.BlockSpec((chunk, LANES), lambda j, c: (rev(c), j))
    state = pl.BlockSpec((chunk, blk), lambda j, c: (rev(c), j))
    enter = pl.BlockSpec((SUBLANES, blk), lambda j, c: (rev(c), j))
    chan = pl.BlockSpec((1, LANES), lambda j, c: (0, j))
    f32 = lambda *s: jax.ShapeDtypeStruct(s, F32)
    return _hosted_call(
        body, carry, _edge_2d(N_SSM_BLOCKS, nc), name="ssm_bwd", grid=(N_SSM_BLOCKS, nc),
        in_specs=[tok, tok, state, state, enter, enter, row, row, b_mat, b_mat, c_mat, c_mat, chan],
        out_specs=[tok, b_mat, b_mat, c_mat, c_mat, row, row, chan],
        out_shape=[jax.ShapeDtypeStruct((T, SSM_W), BF16), f32(N_SSM_BLOCKS, LANES, blk), f32(N_SSM_BLOCKS, LANES, blk),
                   f32(N_SSM_BLOCKS, blk, LANES), f32(N_SSM_BLOCKS, blk, LANES), f32(1, STATES), f32(1, STATES), f32(1, SSM_W)],
        scratch_shapes=[pltpu.VMEM((chunk, LANES), F32), pltpu.VMEM((chunk, LANES), F32), pltpu.VMEM((chunk, LANES), F32),
                        pltpu.VMEM((chunk, blk), F32), pltpu.VMEM((chunk, blk), F32),
                        pltpu.VMEM((chunk, blk), F32), pltpu.VMEM((chunk, blk), F32),
                        pltpu.VMEM((SUBLANES, blk), F32), pltpu.VMEM((SUBLANES, blk), F32), pltpu.VMEM((SUBLANES, blk), F32)],
        compiler_params=_params(("arbitrary", "arbitrary"), VMEM_BIG),
        inputs=(dy, u, h_re, h_im, in_re, in_im, a_re, a_im, b_re, b_im, c_re, c_im, d_skip))


def _merge_forward(y, att, ga, gs, w_glu, w_ssm, w_attn):
    z = jax.nn.gelu(y)
    zb = z.astype(BF16)
    gl = jax.nn.sigmoid(_dot(zb, w_glu))
    z2b = (z * gl).astype(BF16)
    y_ssm = _dot(z2b, w_ssm)
    y_attn = _dot(att, w_attn)
    sa = jax.nn.sigmoid(ga)
    ss = jax.nn.sigmoid(gs)
    merged = (sa * y_attn + ss * y_ssm).astype(BF16)
    return z, zb, gl, z2b, y_ssm, y_attn, sa, ss, merged


def _merge_fwd(x, y, att, ga, gs, g2, g3, w_glu, w_ssm, w_attn, w_out, tile):
    T = x.shape[0]

    def body(x_ref, y_ref, att_ref, ga_ref, gs_ref, g2_ref, g3_ref, wg_ref, ws_ref, wa_ref, wo_ref, x1_ref, o_ref, h2_ref):
        merged = _merge_forward(y_ref[...], att_ref[...], ga_ref[...], gs_ref[...], wg_ref[...], ws_ref[...], wa_ref[...])[-1]
        o = _dot(merged, wo_ref[...])
        x1 = x_ref[...] + o * _rms_scale(o) * g2_ref[...]
        o_ref[...] = o
        x1_ref[...] = x1
        h2_ref[...] = (x1 * _rms_scale(x1) * g3_ref[...]).astype(BF16)

    tok = lambda w: pl.BlockSpec((tile, w), lambda i: (i, 0))
    vec = _const_spec((1, D_MODEL))
    return pl.pallas_call(
        body, name="merge_fwd", grid=(T // tile,),
        in_specs=[tok(D_MODEL), tok(SSM_W), tok(ATTN_W), tok(D_MODEL), tok(D_MODEL), vec, vec,
                  _const_spec((SSM_W, SSM_W)), _const_spec((SSM_W, D_MODEL)), _const_spec((ATTN_W, D_MODEL)),
                  _const_spec((D_MODEL, D_MODEL))],
        out_specs=[tok(D_MODEL), tok(D_MODEL), tok(D_MODEL)],
        out_shape=_hbm_out([jax.ShapeDtypeStruct((T, D_MODEL), F32), jax.ShapeDtypeStruct((T, D_MODEL), F32),
                            jax.ShapeDtypeStruct((T, D_MODEL), BF16)]),
        compiler_params=_params(("arbitrary",), VMEM_MID),
    )(*_in_hbm(x, y, att, ga, gs, g2, g3, w_glu, w_ssm, w_attn, w_out))


def _merge_bwd(dh2, dx2, x1, o, y, att, ga, gs, g2, g3, w_glu, w_ssm, w_attn, w_out, tile, carry=None):
    T = x1.shape[0]
    n_steps = T // tile

    group = min(2, n_steps)
    staged_widths = (D_MODEL, D_MODEL, ATTN_W, D_MODEL, SSM_W, D_MODEL, SSM_W, SSM_W)

    def body(dh2_ref, dx2_ref, x1_ref, o_ref, y_ref, att_ref, ga_ref, gs_ref, g2_ref, g3_ref, wg_ref, ws_ref, wa_ref, wo_ref,
             dx1_ref, dgates_ref, datt_ref, dy_ref, dwg_hbm, dws_hbm, dwa_hbm, dwo_hbm, dg2_ref, dg3_ref,
             awg_ref, aws_ref, awa_ref, awo_ref, *staged):
        i = pl.program_id(0)
        x1v, ov = x1_ref[...], o_ref[...]
        dxn, dg3 = _rms_bwd(dh2_ref[...], x1v, _rms_scale(x1v), g3_ref[...])
        dx1 = dx2_ref[...] + dxn
        dx1_ref[...] = dx1
        do, dg2 = _rms_bwd(dx1, ov, _rms_scale(ov), g2_ref[...])
        dob = do.astype(BF16)

        yv = y_ref[...]
        att = att_ref[...]
        z, zb, gl, z2b, y_ssm, y_attn, sa, ss, merged = _merge_forward(
            yv, att, ga_ref[...], gs_ref[...], wg_ref[...], ws_ref[...], wa_ref[...])
        dmerged = _dot_nt(dob, wo_ref[...])
        dya = (dmerged * sa).astype(BF16)
        dys = (dmerged * ss).astype(BF16)
        dgates_ref[:, :D_MODEL] = (dmerged * y_attn * sa * (1.0 - sa)).astype(BF16)
        dgates_ref[:, D_MODEL:] = (dmerged * y_ssm * ss * (1.0 - ss)).astype(BF16)
        datt_ref[...] = _dot_nt(dya, wa_ref[...]).astype(BF16)
        dz2 = _dot_nt(dys, ws_ref[...])
        dpre = (dz2 * z * gl * (1.0 - gl)).astype(BF16)
        dz = dz2 * gl + _dot_nt(dpre, wg_ref[...])
        _, gelu_vjp = jax.vjp(jax.nn.gelu, yv)
        dy_ref[...] = gelu_vjp(dz)[0]

        part = pl.ds(pl.multiple_of((i % group) * tile, tile), tile)
        for ref, val in zip(staged, (merged, dob, att, dya, z2b, dys, zb, dpre)):
            ref[part, :] = val

        @pl.when(i == 0)
        def _():
            dg2_ref[...] = dg2
            dg3_ref[...] = dg3

        @pl.when(i > 0)
        def _():
            dg2_ref[...] += dg2
            dg3_ref[...] += dg3

        def weight_grads():
            s_merged, s_dob, s_att, s_dya, s_z2b, s_dys, s_zb, s_dpre = (ref[...] for ref in staged)
            return ((awo_ref, _dot_tn(s_merged, s_dob)), (awa_ref, _dot_tn(s_att, s_dya)),
                    (aws_ref, _dot_tn(s_z2b, s_dys)), (awg_ref, _dot_tn(s_zb, s_dpre)))

        @pl.when(i == group - 1)
        def _():
            for ref, val in weight_grads():
                ref[...] = val

        @pl.when((i % group == group - 1) & (i > group - 1))
        def _():
            for ref, val in weight_grads():
                ref[...] += val

        @pl.when(i == n_steps - 1)
        def _():
            pltpu.sync_copy(awg_ref, dwg_hbm)
            pltpu.sync_copy(aws_ref, dws_hbm)
            pltpu.sync_copy(awa_ref, dwa_hbm)
            pltpu.sync_copy(awo_ref, dwo_hbm)

    tok = lambda w: pl.BlockSpec((tile, w), lambda i: (i, 0))
    vec = _const_spec((1, D_MODEL))
    any_ = pl.BlockSpec(memory_space=pl.ANY)
    vec_out = pl.BlockSpec((1, D_MODEL), lambda i: (0, 0))
    f32 = lambda *s: jax.ShapeDtypeStruct(s, F32)
    bf = lambda *s: jax.ShapeDtypeStruct(s, BF16)
    return _hosted_call(
        body, carry, _edge_1d(n_steps), name="merge_bwd", grid=(n_steps,),
        in_specs=[tok(D_MODEL), tok(D_MODEL), tok(D_MODEL), tok(D_MODEL), tok(SSM_W), tok(ATTN_W), tok(D_MODEL), tok(D_MODEL),
                  vec, vec, _const_spec((SSM_W, SSM_W)), _const_spec((SSM_W, D_MODEL)), _const_spec((ATTN_W, D_MODEL)),
                  _const_spec((D_MODEL, D_MODEL))],
        out_specs=[tok(D_MODEL), tok(2 * D_MODEL), tok(ATTN_W), tok(SSM_W), any_, any_, any_, any_, vec_out, vec_out],
        out_shape=[f32(T, D_MODEL), bf(T, 2 * D_MODEL), bf(T, ATTN_W), f32(T, SSM_W),
                   f32(SSM_W, SSM_W), f32(SSM_W, D_MODEL), f32(ATTN_W, D_MODEL), f32(D_MODEL, D_MODEL),
                   f32(1, D_MODEL), f32(1, D_MODEL)],
        scratch_shapes=[pltpu.VMEM((SSM_W, SSM_W), F32), pltpu.VMEM((SSM_W, D_MODEL), F32),
                        pltpu.VMEM((ATTN_W, D_MODEL), F32), pltpu.VMEM((D_MODEL, D_MODEL), F32)]
        + [pltpu.VMEM((group * tile, wd), BF16) for wd in staged_widths],
        compiler_params=_params(("arbitrary",), VMEM_BIG),
        inputs=(dh2, dx2, x1, o, y, att, ga, gs, g2, g3, w_glu, w_ssm, w_attn, w_out))


FF_SHARD = D_FF // N_DEV


def _mlp_fwd(h2, x1, target, g4, w_ff_in, w_ff_out, tile):
    T = h2.shape[0]
    col_chunk = 2 * FF_SHARD

    def body(h2_ref, x1_ref, tg_ref, g4_ref, wi_ref, wo_ref, a_ref, dfo_ref, dx2_ref, loss_ref, dg4_ref, rr_ref):
        i = pl.program_id(0)
        h2v = h2_ref[...]
        for c in range(D_FF // col_chunk):
            cols = slice(c * col_chunk, (c + 1) * col_chunk)
            a = _dot_nt(h2v, wi_ref[cols, :])
            a_ref[:, cols] = a.astype(BF16)
            ra = jnp.maximum(a, 0.0)
            rr_ref[:, cols] = (ra * ra).astype(BF16)
        f = _dot(rr_ref[...], wo_ref[...])
        r = _rms_scale(f)
        g = g4_ref[...]
        err = x1_ref[...] + f * r * g - tg_ref[...]
        dx2 = err * (1.0 / D_MODEL)
        dx2_ref[...] = dx2
        dfo, dg = _rms_bwd(dx2, f, r, g)
        dfo_ref[...] = dfo.astype(BF16)
        row = lax.broadcasted_iota(jnp.int32, (SUBLANES, LANES), 0)
        col = lax.broadcasted_iota(jnp.int32, (SUBLANES, LANES), 1)
        loss = jnp.where((row == 0) & (col == 0), (0.5 / D_MODEL) * jnp.sum(err * err), 0.0)

        @pl.when(i == 0)
        def _():
            loss_ref[...] = loss
            dg4_ref[...] = dg

        @pl.when(i > 0)
        def _():
            loss_ref[...] += loss
            dg4_ref[...] += dg

    tok = pl.BlockSpec((tile, D_MODEL), lambda i: (i, 0))
    return pl.pallas_call(
        body, name="mlp_fwd", grid=(T // tile,),
        in_specs=[tok, tok, tok, _const_spec((1, D_MODEL)), _const_spec((D_FF, D_MODEL)), _const_spec((D_FF, D_MODEL))],
        out_specs=[pl.BlockSpec((tile, D_FF), lambda i: (i, 0)), tok, tok,
                   pl.BlockSpec((SUBLANES, LANES), lambda i: (0, 0)), pl.BlockSpec((1, D_MODEL), lambda i: (0, 0))],
        out_shape=_hbm_out([jax.ShapeDtypeStruct((T, D_FF), BF16), jax.ShapeDtypeStruct((T, D_MODEL), BF16),
                            jax.ShapeDtypeStruct((T, D_MODEL), F32), jax.ShapeDtypeStruct((SUBLANES, LANES), F32),
                            jax.ShapeDtypeStruct((1, D_MODEL), F32)]),
        scratch_shapes=[pltpu.VMEM((tile, D_FF), BF16)],
        compiler_params=_params(("arbitrary",), VMEM_MAX),
    )(*_in_hbm(h2, x1, target, g4, w_ff_in.reshape(D_FF, D_MODEL), w_ff_out.reshape(D_FF, D_MODEL)))


def _mlp_weight_grads(dfo, a, h2, w_ff_out, row_chunk):
    T = h2.shape[0]

    def body(dfo_ref, h2_ref, a_ref, wo_ref, dwi_ref, dwo_ref, da_ref, rr_ref):
        def rows(r, _):
            sl = pl.ds(pl.multiple_of(r * row_chunk, row_chunk), row_chunk)
            ra = jnp.maximum(a_ref[sl, :].astype(F32), 0.0)
            da_ref[sl, :] = (_dot_nt(dfo_ref[sl, :], wo_ref[0]) * (2.0 * ra)).astype(BF16)
            rr_ref[sl, :] = (ra * ra).astype(BF16)
            return 0

        lax.fori_loop(0, T // row_chunk, rows, 0)
        dwo_ref[0] = _dot_tn(rr_ref[...], dfo_ref[...])
        dwi_ref[0] = _dot_tn(h2_ref[...], da_ref[...])

    return pl.pallas_call(
        body, name="mlp_weight_grads", grid=(N_DEV,),
        in_specs=[_const_spec((T, D_MODEL)), _const_spec((T, D_MODEL)), pl.BlockSpec((T, FF_SHARD), lambda k: (0, k)),
                  pl.BlockSpec((1, FF_SHARD, D_MODEL), lambda k: (k, 0, 0))],
        out_specs=[pl.BlockSpec((1, D_MODEL, FF_SHARD), lambda k: (k, 0, 0)),
                   pl.BlockSpec((1, FF_SHARD, D_MODEL), lambda k: (k, 0, 0)), pl.BlockSpec((T, FF_SHARD), lambda k: (0, k))],
        out_shape=_hbm_out([jax.ShapeDtypeStruct((N_DEV, D_MODEL, FF_SHARD), F32),
                            jax.ShapeDtypeStruct((N_DEV, FF_SHARD, D_MODEL), F32), jax.ShapeDtypeStruct((T, D_FF), BF16)]),
        scratch_shapes=[pltpu.VMEM((T, FF_SHARD), BF16)],
        compiler_params=_params(("arbitrary",), VMEM_MAX),
    )(*_in_hbm(dfo, h2, a, w_ff_out))


def _mlp_input_grad(da, w_ff_in_t, tile):
    T = da.shape[0]

    def body(da_ref, w_ref, o_ref):
        o_ref[...] = _dot(da_ref[...], w_ref[...])

    return pl.pallas_call(
        body, name="mlp_input_grad", grid=(T // tile,),
        in_specs=[pl.BlockSpec((tile, D_FF), lambda i: (i, 0)), _const_spec((D_FF, D_MODEL))],
        out_specs=pl.BlockSpec((tile, D_MODEL), lambda i: (i, 0)),
        out_shape=_hbm_out(jax.ShapeDtypeStruct((T, D_MODEL), F32)),
        compiler_params=_params(("arbitrary",), VMEM_MID),
    )(*_in_hbm(da, w_ff_in_t))


def _block_diag_in(b):
    bt = b.reshape(N_SSM_BLOCKS, GROUPS_PER_BLOCK, GROUP_CH, N_STATE)
    eye = jnp.eye(GROUPS_PER_BLOCK, dtype=b.dtype)
    return jnp.einsum("jacp,ab->jacbp", bt, eye).reshape(N_SSM_BLOCKS, LANES, SSM_LANE_BLOCK)


def _block_diag_in_grad(g):
    g = g.reshape(N_SSM_BLOCKS, GROUPS_PER_BLOCK, GROUP_CH, GROUPS_PER_BLOCK, N_STATE)
    d = jnp.diagonal(g, axis1=1, axis2=3)
    return jnp.transpose(d, (0, 3, 1, 2)).reshape(N_GROUPS, GROUP_CH, N_STATE)


def _block_diag_out(c):
    ct = c.reshape(N_SSM_BLOCKS, GROUPS_PER_BLOCK, GROUP_CH, N_STATE)
    eye = jnp.eye(GROUPS_PER_BLOCK, dtype=c.dtype)
    return jnp.einsum("jacp,ab->japbc", ct, eye).reshape(N_SSM_BLOCKS, SSM_LANE_BLOCK, LANES)


def _block_diag_out_grad(g):
    g = g.reshape(N_SSM_BLOCKS, GROUPS_PER_BLOCK, N_STATE, GROUPS_PER_BLOCK, GROUP_CH)
    d = jnp.diagonal(g, axis1=1, axis2=3)
    return jnp.transpose(d, (0, 3, 2, 1)).reshape(N_GROUPS, GROUP_CH, N_STATE)


def _tiles(T):
    return dict(proj=min(512, T), proj_bwd=min(512, T // 2), merge=min(512, T), merge_bwd=min(256, T),
                mlp_fwd=min(512, T), mlp_bwd=min(512, T), ssm_chunk=min(1024, T))


def _mesh_position():
    x, y, c = lax.axis_index("x"), lax.axis_index("y"), lax.axis_index("c")
    other_chips = [(1 - x, y), (x, 1 - y), (1 - x, 1 - y)]
    return x, y, c, other_chips


def _gather_carry(arrays):
    n = len(arrays)

    def copies(ins, outs, sems):
        send_sems, recv_sems, local_sems = sems
        x, y, c, chips = _mesh_position()
        me, sibling = (x, y, c), (x, y, 1 - c)

        def copy(a, k, block, to, src=None):
            px, py, pc = block
            dst = outs[a].at[4 * px + 2 * py + pc]
            return pltpu.make_async_remote_copy(
                src_ref=dst if src is None else src, dst_ref=dst, send_sem=send_sems.at[7 * a + k],
                recv_sem=recv_sems.at[7 * a + k], device_id=to, device_id_type=MESH_IDS)

        mine = [pltpu.make_async_copy(ins[a], outs[a].at[4 * x + 2 * y + c], local_sems.at[a]) for a in range(n)]
        first = []
        for a in range(n):
            first.append(copy(a, 0, me, sibling, src=ins[a]))
            first += [copy(a, 1 + j, me, (*chip, c), src=ins[a]) for j, chip in enumerate(chips)]
        return copy, mine, first, me, sibling, chips, c

    def start(ins, outs, sems):
        _, mine, first, *_ = copies(ins, outs, sems)
        for cp in mine + first:
            cp.start()

    def passed_on(copy, sibling, chips, c):
        return [copy(a, 4 + j, (*chip, c), sibling) for a in range(n) for j, chip in enumerate(chips)]

    def middle(ins, outs, sems):
        copy, _, _, me, sibling, chips, c = copies(ins, outs, sems)
        for a in range(n):
            for j, chip in enumerate(chips):
                copy(a, 1 + j, (*chip, c), me).wait_recv()
        for cp in passed_on(copy, sibling, chips, c):
            cp.start()

    def finish(ins, outs, sems):
        copy, mine, first, me, sibling, chips, c = copies(ins, outs, sems)
        for a in range(n):
            copy(a, 0, sibling, me).wait_recv()
            for j, chip in enumerate(chips):
                copy(a, 4 + j, (*chip, 1 - c), me).wait_recv()
        for cp in first + passed_on(copy, sibling, chips, c):
            cp.wait_send()
        for cp in mine:
            cp.wait()

    return _Carry(arrays, [jax.ShapeDtypeStruct((N_DEV,) + a.shape, a.dtype) for a in arrays],
                  [pltpu.SemaphoreType.DMA((7 * n,)), pltpu.SemaphoreType.DMA((7 * n,)), pltpu.SemaphoreType.DMA((n,))],
                  start, finish, middle)


def _pairwise_carry(arrays, n_slots, make_copies):
    n = len(arrays)

    def start(ins, outs, sems):
        for cp in make_copies(ins, outs, sems):
            cp.start()

    def finish(ins, outs, sems):
        for cp in make_copies(ins, outs, sems):
            cp.wait()

    return _Carry(arrays, [jax.ShapeDtypeStruct((n_slots,) + a.shape[1:], a.dtype) for a in arrays],
                  [pltpu.SemaphoreType.DMA((n_slots * n,)), pltpu.SemaphoreType.DMA((n_slots * n,))], start, finish)


def _sibling_carry(grads):
    def make_copies(ins, outs, sems):
        x, y, c, _ = _mesh_position()
        return [pltpu.make_async_remote_copy(
            src_ref=ins[a].at[2 * ch + (1 - c)], dst_ref=outs[a].at[ch], send_sem=sems[0].at[4 * a + ch],
            recv_sem=sems[1].at[4 * a + ch], device_id=(x, y, 1 - c), device_id_type=MESH_IDS)
            for a in range(len(grads)) for ch in range(4)]

    return _pairwise_carry(grads, 4, make_copies)


def _chips_carry(sums):
    def make_copies(ins, outs, sems):
        x, y, c, chips = _mesh_position()
        return [pltpu.make_async_remote_copy(
            src_ref=ins[a].at[2 * px + py], dst_ref=outs[a].at[j], send_sem=sems[0].at[3 * a + j],
            recv_sem=sems[1].at[3 * a + j], device_id=(px, py, c), device_id_type=MESH_IDS)
            for a in range(len(sums)) for j, (px, py) in enumerate(chips)]

    return _pairwise_carry(sums, 3, make_copies)


def _everyone_carry(arrays):
    def make_copies(ins, outs, sems):
        x, y, c, _ = _mesh_position()
        flip = lambda v, bit: 1 - v if bit else v
        return [pltpu.make_async_remote_copy(
            src_ref=ins[a], dst_ref=outs[a].at[r - 1], send_sem=sems[0].at[7 * a + r - 1], recv_sem=sems[1].at[7 * a + r - 1],
            device_id=(flip(x, r & 4), flip(y, r & 2), flip(c, r & 1)), device_id_type=MESH_IDS)
            for a in range(len(arrays)) for r in range(1, N_DEV)]

    carry = _pairwise_carry([jax.ShapeDtypeStruct((1,) + a.shape, a.dtype) for a in arrays], N_DEV - 1, make_copies)
    carry.inputs = list(arrays)
    return carry


def _sum_everyone(own, received, me, name, after=()):
    def body(me_ref, own_ref, r_ref, *refs):
        g = None
        for d in range(N_DEV):
            relation = jnp.bitwise_xor(d, me_ref[0])
            part = jnp.where(relation == 0, own_ref[...], r_ref[jnp.maximum(relation - 1, 0)])
            g = part if g is None else g + part
        refs[-1][...] = g

    whole = lambda shape: pl.BlockSpec(shape, lambda i, me_ref: (0,) * len(shape))
    return pl.pallas_call(
        body, name=name,
        grid_spec=pltpu.PrefetchScalarGridSpec(
            num_scalar_prefetch=1, grid=(1,), in_specs=[whole(own.shape), whole(received.shape)] + [HBM_SPEC] * len(after),
            out_specs=whole(own.shape)),
        out_shape=jax.ShapeDtypeStruct(own.shape, F32))(me, *_in_hbm(own, received), *after)


SEM_SPEC = pl.BlockSpec(memory_space=pltpu.SEMAPHORE)
DATAFLOW_EFFECT = pltpu.SideEffectType.DATAFLOW_SIDE_EFFECTING


def _exchange_start(carry, name, after=()):
    n = len(carry.inputs)
    lands = [lax.empty(s.shape, s.dtype) for s in carry.out_shapes]

    def body(*refs):
        first_out = 2 * n + len(after)
        srcs, zones, sems, token = refs[:n], refs[n:2 * n], refs[first_out:first_out + 2], refs[-1]
        carry.start(srcs, zones, sems)
        token[...] = jnp.zeros_like(token)

    outs = pl.pallas_call(
        body, name=name, in_specs=[HBM_SPEC] * (2 * n + len(after)),
        out_specs=[SEM_SPEC, SEM_SPEC] + [HBM_SPEC] * (2 * n) + [pl.BlockSpec(memory_space=pltpu.VMEM)],
        out_shape=list(carry.sems) + _hbm_out([jax.ShapeDtypeStruct(a.shape, a.dtype) for a in carry.inputs])
        + _hbm_out(carry.out_shapes) + [jax.ShapeDtypeStruct((SUBLANES, LANES), F32)],
        input_output_aliases={j: 2 + j for j in range(2 * n)},
        compiler_params=pltpu.CompilerParams(has_side_effects=DATAFLOW_EFFECT),
    )(*_in_hbm(*carry.inputs, *lands), *after)
    return outs[:-1], outs[-1]


def _exchange_wait(carry, in_flight, after, name):
    n = len(carry.inputs)
    sems, srcs, zones = in_flight[:2], in_flight[2:2 + n], in_flight[2 + n:]

    def body(*refs):
        src_refs, zone_refs, sem_refs = refs[:n], refs[n:2 * n], refs[2 * n:2 * n + 2]
        carry.finish(src_refs, zone_refs, sem_refs)

    outs = pl.pallas_call(
        body, name=name, in_specs=[HBM_SPEC] * (2 * n) + [SEM_SPEC, SEM_SPEC] + [HBM_SPEC] * len(after),
        out_specs=[HBM_SPEC] * (2 * n),
        out_shape=_hbm_out([jax.ShapeDtypeStruct(a.shape, a.dtype) for a in carry.inputs]) + _hbm_out(carry.out_shapes),
        input_output_aliases={j: j for j in range(2 * n)},
        compiler_params=pltpu.CompilerParams(has_side_effects=DATAFLOW_EFFECT),
    )(*srcs, *zones, *sems, *after)
    return list(outs[:n]), list(outs[n:])


def _add_sibling(grads8, recvs, core, row_tiles, name):
    k = len(grads8)
    g4 = [g.reshape(4, 2, *g.shape[1:]) for g in grads8]

    def body(core_ref, *refs):
        g_refs, r_refs, o_refs, ob_refs = (refs[j * k:(j + 1) * k] for j in range(4))
        for g_ref, r_ref, o_ref, ob_ref in zip(g_refs, r_refs, o_refs, ob_refs):
            s = g_ref[0] + r_ref[...]
            o_ref[...] = s
            ob_ref[...] = s.astype(BF16)

    def blocks(make):
        return [make(g.shape[1] // row_tiles, g.shape[2]) for g in grads8]

    slot = lambda tr, C: pl.BlockSpec((1, tr, C), lambda ch, r, core_ref: (ch, r, 0))
    outs = pl.pallas_call(
        body, name=name,
        grid_spec=pltpu.PrefetchScalarGridSpec(
            num_scalar_prefetch=1, grid=(4, row_tiles),
            in_specs=blocks(lambda tr, C: pl.BlockSpec((1, 1, tr, C), lambda ch, r, core_ref: (ch, core_ref[0], r, 0)))
            + blocks(slot), out_specs=blocks(slot) + blocks(slot)),
        out_shape=_hbm_out([jax.ShapeDtypeStruct((4,) + g.shape[1:], F32) for g in grads8]
                           + [jax.ShapeDtypeStruct((4,) + g.shape[1:], BF16) for g in grads8]),
        compiler_params=_params(("arbitrary", "arbitrary")),
    )(core, *_in_hbm(*g4, *recvs))
    return list(outs[:k]), list(outs[k:])


def _adam_math(w, g, m, v):
    m = ADAM_B1 * m + (1.0 - ADAM_B1) * g
    v = ADAM_B2 * v + (1.0 - ADAM_B2) * jnp.square(g)
    m_hat = m / (1.0 - ADAM_B1 ** ADAM_STEP)
    v_hat = v / (1.0 - ADAM_B2 ** ADAM_STEP)
    delta = -ADAM_LR * (m_hat / (jnp.sqrt(v_hat) + ADAM_EPS) + ADAM_WD * w)
    return delta, m, v


def _adam_big(ws, ms, vs, chip_sums, recvs, chip, row_tiles, name, after=()):
    k = len(ws)

    def body(chip_ref, *refs):
        refs = refs[:5 * k] + refs[5 * k + len(after):]
        w_refs, m_refs, v_refs, s_refs, r_refs, g_refs, d_refs, nm_refs, nv_refs = (refs[j * k:(j + 1) * k] for j in range(9))
        for a in range(k):
            r_ref = r_refs[a]
            g = s_refs[a][0] + r_ref[0].astype(F32) + r_ref[1].astype(F32) + r_ref[2].astype(F32)
            g_refs[a][...] = g
            d_refs[a][...], nm_refs[a][...], nv_refs[a][...] = _adam_math(w_refs[a][...], g, m_refs[a][...], v_refs[a][...])

    def blocks(make):
        return [make(w.shape[0] // row_tiles, w.shape[1]) for w in ws]

    blk = lambda tr, C: pl.BlockSpec((tr, C), lambda r, chip_ref: (r, 0))
    outs = pl.pallas_call(
        body, name=name,
        grid_spec=pltpu.PrefetchScalarGridSpec(
            num_scalar_prefetch=1, grid=(row_tiles,),
            in_specs=blocks(blk) * 3 + blocks(lambda tr, C: pl.BlockSpec((1, tr, C), lambda r, chip_ref: (chip_ref[0], r, 0)))
            + blocks(lambda tr, C: pl.BlockSpec((3, tr, C), lambda r, chip_ref: (0, r, 0))) + [HBM_SPEC] * len(after),
            out_specs=blocks(blk) * 4),
        out_shape=[jax.ShapeDtypeStruct(w.shape, F32) for w in ws] * 4,
        compiler_params=_params(("arbitrary",)),
    )(chip, *_in_hbm(*ws, *ms, *vs, *chip_sums, *recvs), *after)
    return [list(outs[j * k:(j + 1) * k]) for j in range(4)]


def _sum_partials(partials, name, after=()):
    def body(p_ref, *refs):
        g = p_ref[0]
        for d in range(1, partials.shape[0]):
            g = g + p_ref[d]
        refs[-1][...] = g

    return pl.pallas_call(body, name=name, grid=(1,), in_specs=[_whole(partials.shape)] + [HBM_SPEC] * len(after),
                          out_specs=_whole(partials.shape[1:]),
                          out_shape=jax.ShapeDtypeStruct(partials.shape[1:], F32))(*_in_hbm(partials), *after)


def _adam_small(ws, ms, vs, gs):
    n = len(ws)

    def body(*refs):
        w_refs, m_refs, v_refs, g_refs = (refs[i * n:(i + 1) * n] for i in range(4))
        d_refs, nm_refs, nv_refs = (refs[(4 + i) * n:(5 + i) * n] for i in range(3))
        for j in range(n):
            d_refs[j][...], nm_refs[j][...], nv_refs[j][...] = _adam_math(
                w_refs[j][...], g_refs[j][...], m_refs[j][...], v_refs[j][...])

    specs = [_whole(w.shape) for w in ws]
    outs = pl.pallas_call(body, name="adam_small", grid=(1,), in_specs=specs * 4, out_specs=specs * 3,
                          out_shape=[jax.ShapeDtypeStruct(w.shape, F32) for w in ws] * 3,
                          compiler_params=_params(("arbitrary",), VMEM_MID))(*_in_hbm(*ws, *ms, *vs, *gs))
    return outs[:n], outs[n:2 * n], outs[2 * n:]


PACK_QUANTUM = SUBLANES * LANES


def _pack(named, names):
    parts = []
    for nme in names:
        flat = named[nme].reshape(-1)
        parts.append(jnp.pad(flat, (0, -flat.size % PACK_QUANTUM)))
    return jnp.concatenate(parts).reshape(-1, LANES)


def _unpack(packed, shapes, names):
    flat = packed.reshape(-1)
    out, pos = {}, 0
    for nme in names:
        size = math.prod(shapes[nme])
        out[nme] = flat[pos:pos + size].reshape(shapes[nme])
        pos += size + (-size % PACK_QUANTUM)
    return out


BIG = ("w_in", "w_glu", "w_attn_branch", "w_ssm_branch", "w_out", "w_ff_in", "w_ff_out")
COLUMN_SHARDED = ("w_in", "w_attn_branch", "w_ssm_branch", "w_ff_in")
SMALL = ("norm_mix_pre", "norm_mix_post", "norm_mlp_pre", "norm_mlp_post", "rel_bias", "sinks", "lam_re", "lam_im",
         "log_dt", "b_re", "b_im", "c_re", "c_im", "d_skip")
SWAPPED_SMALL = ("rel_bias", "b_re", "b_im")
SMALL_LATE = ("norm_mix_pre", "rel_bias", "sinks", "loss")
SMALL_BEFORE_ATTN_BWD = tuple(n for n in SMALL if n not in SMALL_LATE)
ALL_WEIGHTS = ("norm_mix_pre", "norm_mix_post", "norm_mlp_pre", "norm_mlp_post", "w_in", "rel_bias", "sinks", "lam_re",
               "lam_im", "log_dt", "b_re", "b_im", "c_re", "c_im", "d_skip", "w_glu", "w_attn_branch", "w_ssm_branch",
               "w_out", "w_ff_in", "w_ff_out")


def _full_from_gathered(name, gathered):
    _, r, c = gathered.shape
    if name in COLUMN_SHARDED:
        return jnp.transpose(gathered, (1, 0, 2)).reshape(r, N_DEV * c)
    return gathered.reshape(N_DEV * r, c)


def _blocks_from_full(name, full):
    r, c = full.shape
    if name in COLUMN_SHARDED:
        return jnp.transpose(full.reshape(r, N_DEV, c // N_DEV), (1, 0, 2))
    return full.reshape(N_DEV, r // N_DEV, c)


def kernel(x, norm_mix_pre, norm_mix_post, norm_mlp_pre, norm_mlp_post, w_in, rel_bias, sinks, lam_re, lam_im, log_dt, b_re, b_im, c_re, c_im, d_skip, w_glu, w_attn_branch, w_ssm_branch, w_out, w_ff_in, w_ff_out, loss_target, m_norm_mix_pre, m_norm_mix_post, m_norm_mlp_pre, m_norm_mlp_post, m_w_in, m_rel_bias, m_sinks, m_lam_re, m_lam_im, m_log_dt, m_b_re, m_b_im, m_c_re, m_c_im, m_d_skip, m_w_glu, m_w_attn_branch, m_w_ssm_branch, m_w_out, m_w_ff_in, m_w_ff_out, v_norm_mix_pre, v_norm_mix_post, v_norm_mlp_pre, v_norm_mlp_post, v_w_in, v_rel_bias, v_sinks, v_lam_re, v_lam_im, v_log_dt, v_b_re, v_b_im, v_c_re, v_c_im, v_d_skip, v_w_glu, v_w_attn_branch, v_w_ssm_branch, v_w_out, v_w_ff_in, v_w_ff_out):
    args = dict(locals())
    w = {n: args[n] for n in ALL_WEIGHTS}
    m = {n: args["m_" + n] for n in ALL_WEIGHTS}
    v = {n: args["v_" + n] for n in ALL_WEIGHTS}
    core = lax.axis_index("c").astype(jnp.int32).reshape(1)
    chip = (2 * lax.axis_index("x") + lax.axis_index("y")).astype(jnp.int32).reshape(1)
    xs, target = x[0], loss_target[0]
    t = _tiles(xs.shape[0])
    local = lambda d, n: d[n][0].T if n == "w_in" else d[n][0]
    shard = {n: local(w, n).astype(BF16) for n in BIG}
    shard["w_ff_in"] = shard["w_ff_in"].T
    view = lambda n, a: jnp.swapaxes(a, -1, -2) if n in SWAPPED_SMALL else a
    small = {n: (view(n, w[n]) if n == "rel_bias" else view(n, w[n])[0]) for n in SMALL}
    g1, g2, g3, g4 = (small[n].reshape(1, D_MODEL) for n in ("norm_mix_pre", "norm_mix_post", "norm_mlp_pre", "norm_mlp_post"))
    bucket = jnp.asarray(_bucket_table())
    rel_b, sink = small["rel_bias"], small["sinks"].reshape(1, N_HEADS)
    lam_r, lam_i = small["lam_re"].reshape(1, STATES), small["lam_im"].reshape(1, STATES)
    ldt_rep = jnp.repeat(small["log_dt"].reshape(N_GROUPS), N_STATE).reshape(1, STATES)
    bd_re, bd_im = _block_diag_in(small["b_re"]), _block_diag_in(small["b_im"])
    cm_re, cm_im = _block_diag_out(small["c_re"]).astype(BF16), _block_diag_out(small["c_im"]).astype(BF16)
    dsk = small["d_skip"].reshape(1, SSM_W)

    (g_in,) = _run_carry(_gather_carry([shard["w_in"]]), "gather_w_in")
    wf_in = g_in.reshape(IN_W, D_MODEL)
    merge_names = ("w_glu", "w_attn_branch", "w_ssm_branch", "w_out")
    (q, k, vv, u, ga, gs, h), gathered = _in_proj_fwd(xs, g1, wf_in, t["proj"], _gather_carry([shard[n] for n in merge_names]))
    wf = {n: _full_from_gathered(n, g) for n, g in zip(merge_names, gathered)}
    (att,), (wf_ff_in,) = _attn_fwd(q, k, vv, bucket, rel_b, sink, _gather_carry([shard["w_ff_in"]]))
    a_re, a_im, bm_re, bm_im = _ssm_prep(lam_r, lam_i, ldt_rep, bd_re, bd_im)
    (y, h_re, h_im, in_re, in_im), (wf_ff_out,) = _ssm_fwd(
        u, a_re, a_im, bm_re, bm_im, cm_re, cm_im, dsk, t["ssm_chunk"], _gather_carry([shard["w_ff_out"]]))
    x1, o, h2 = _merge_fwd(xs, y, att, ga, gs, g2, g3, wf["w_glu"], wf["w_ssm_branch"], wf["w_attn_branch"], wf["w_out"],
                           t["merge"])
    a, dfo, dx2, loss_blk, dg4 = _mlp_fwd(h2, x1, target, g4, wf_ff_in, wf_ff_out, t["mlp_fwd"])

    groups = {"ff": 4, "merge": 1, "w_in": 2}

    def add_sibling(group, blocks, received):
        return _add_sibling(blocks, received, core, groups[group], "add_sibling_" + group)

    ff_names = ("w_ff_in", "w_ff_out")
    dw_ff_in, dw_ff_out, da = _mlp_weight_grads(dfo, a, h2, wf_ff_out, t["mlp_bwd"])
    dh2 = _mlp_input_grad(da, wf_ff_in.reshape(D_FF, D_MODEL), t["mlp_bwd"])
    ff_blocks = [dw_ff_in, dw_ff_out]
    (dx1, dgates, datt, dy, dw_glu, dw_ssm, dw_attn, dw_out, dg2, dg3), ff_recv = _merge_bwd(
        dh2, dx2, x1, o, y, att, ga, gs, g2, g3, wf["w_glu"], wf["w_ssm_branch"], wf["w_attn_branch"], wf["w_out"],
        t["merge_bwd"], _sibling_carry(ff_blocks))
    ff_sums, ff_sums_bf = add_sibling("ff", ff_blocks, ff_recv)
    merge_blocks = [_blocks_from_full(n, g) for n, g in zip(merge_names, (dw_glu, dw_attn, dw_ssm, dw_out))]
    (du, dbm_re, dbm_im, dcm_re, dcm_im, da_re, da_im, dd_skip), carried = _ssm_bwd(
        dy, u, h_re, h_im, in_re, in_im, a_re, a_im, bm_re, bm_im, cm_re, cm_im, dsk, t["ssm_chunk"],
        _join(_chips_carry(ff_sums_bf), _sibling_carry(merge_blocks)))
    ff_from_chips, merge_recv = carried[:2], carried[2:]
    merge_sums, merge_sums_bf = add_sibling("merge", merge_blocks, merge_recv)
    dbd_re, dbd_im, dlam_re, dlam_im, dldt_rep = _ssm_prep_bwd(lam_r, lam_i, ldt_rep, bd_re, bd_im, dbm_re, dbm_im, da_re, da_im)
    dlog_dt = _group_sum(dldt_rep.reshape(N_GROUPS, N_STATE))
    shapes = {n: view(n, w[n]).shape for n in SMALL}
    shapes["loss"] = (1,)
    small_grads = dict(
        norm_mix_post=dg2, norm_mlp_pre=dg3, norm_mlp_post=dg4, lam_re=dlam_re, lam_im=dlam_im, log_dt=dlog_dt,
        b_re=_block_diag_in_grad(dbd_re), b_im=_block_diag_in_grad(dbd_im),
        c_re=_block_diag_out_grad(dcm_re), c_im=_block_diag_out_grad(dcm_im), d_skip=dd_skip)
    packed_early = _pack({n: small_grads[n].reshape(shapes[n]) for n in SMALL_BEFORE_ATTN_BWD}, SMALL_BEFORE_ATTN_BWD)
    (dq, dkv, attn_small), carried = _attn_bwd(
        q, k, vv, datt, bucket, rel_b, sink, _join(_chips_carry(merge_sums_bf), _gather_carry([packed_early])))
    merge_from_chips, partials_early = carried[:-1], carried[-1]

    dparts = (dq, dkv, du, dgates)
    dw_in_t = _in_proj_weight_grad(h, dparts)
    in_blocks = [dw_in_t.reshape(N_DEV, IN_W // N_DEV, D_MODEL)]
    to_sibling = _sibling_carry(in_blocks)
    in_flight, token = _exchange_start(to_sibling, "w_in_sibling_start")
    n_tiles = xs.shape[0] // t["proj_bwd"]
    (grad_x, dg1), _ = _in_proj_input_grad(xs, g1 + token[0:1, 0:1], wf_in, dx1, dparts, t["proj_bwd"], 0, n_tiles, "in_proj_input_grad")
    late = dict(norm_mix_pre=dg1, rel_bias=attn_small[:, :N_BUCKETS, 0], sinks=attn_small[:, N_BUCKETS, 0], loss=loss_blk[0:1, 0])
    packed_late = _pack({n: late[n].reshape(shapes[n]) for n in SMALL_LATE}, SMALL_LATE)
    to_everyone = _everyone_carry([packed_late])
    late_in_flight, late_started = _exchange_start(to_everyone, "late_grads_start")
    in_blocks, in_recv = _exchange_wait(to_sibling, in_flight, [late_started], "w_in_sibling_wait")
    in_sums, in_sums_bf = add_sibling("w_in", in_blocks, in_recv)
    to_chips = _chips_carry(in_sums_bf)
    in_flight, chips_started = _exchange_start(to_chips, "w_in_chips_start")
    (packed_late,), (late_received,) = _exchange_wait(to_everyone, late_in_flight, [chips_started], "late_grads_wait")

    grads, deltas, new_m, new_v = {}, {}, {}, {}

    def adam_group(group, names, sums, received, after=()):
        outs = _adam_big(*[[local(d, n) for n in names] for d in (w, m, v)], sums, received, chip, groups[group],
                         "adam_" + group, after)
        for store, vals in zip((grads, deltas, new_m, new_v), outs):
            store.update({n: (o.T if n == "w_in" else o)[None] for n, o in zip(names, vals)})

    adam_group("ff", ff_names, ff_sums, ff_from_chips, [chips_started])
    adam_group("merge", merge_names, merge_sums, merge_from_chips, [chips_started])

    grads.update(_unpack(_sum_partials(partials_early, "sum_small_grads", [chips_started]), shapes, SMALL_BEFORE_ATTN_BWD))
    grads.update(_unpack(_sum_everyone(packed_late, late_received, 2 * chip + core, "sum_late_grads"), shapes, SMALL_LATE))
    loss = grads.pop("loss").reshape(())
    small_out = _adam_small(*[[view(n, d[n]) for n in SMALL] for d in (w, m, v)], [grads[n] for n in SMALL])
    for store, vals in zip((deltas, new_m, new_v), small_out):
        store.update(zip(SMALL, vals))
    for store in (grads, deltas, new_m, new_v):
        store.update({n: view(n, store[n]) for n in SWAPPED_SMALL})

    busy = [new_v["w_ff_out"], new_v["w_out"], deltas["norm_mix_pre"]]
    _, (in_from_chips,) = _exchange_wait(to_chips, in_flight, busy, "w_in_chips_wait")
    adam_group("w_in", ("w_in",), in_sums, [in_from_chips])

    return (loss, grad_x[None], *[grads[n] for n in ALL_WEIGHTS], *[deltas[n] for n in ALL_WEIGHTS],
            *[new_m[n] for n in ALL_WEIGHTS], *[new_v[n] for n in ALL_WEIGHTS])
```

```python
import math

import jax
import jax.numpy as jnp
import numpy as np
from jax import lax
from jax.experimental import pallas as pl
from jax.experimental.pallas import tpu as pltpu

F32 = jnp.float32
BF16 = jnp.bfloat16

D_MODEL = 1024
N_HEADS = 8
HEAD_DIM = 64
ATTN_W = 512
KV_W = 128
BLOCK = 128
N_BUCKETS = 32
SSM_W = 512
N_GROUPS = 32
N_STATE = 64
GROUP_CH = 16
STATES = N_GROUPS * N_STATE
D_FF = 4096
IN_W = 3328
SPLITS = (0, 512, 640, 768, 1280, 2304, 3328)
RMS_EPS = 1e-6
NEG_INF = -1e30
SUBLANES = 8
LANES = 128
SSM_LANE_BLOCK = 512
N_SSM_BLOCKS = STATES // SSM_LANE_BLOCK
GROUPS_PER_BLOCK = SSM_LANE_BLOCK // N_STATE
VMEM_BIG = 52 * 1024 * 1024
VMEM_MID = 40 * 1024 * 1024
VMEM_MAX = 60 * 1024 * 1024

ADAM_LR = 0.001
ADAM_B1 = 0.9
ADAM_B2 = 0.999
ADAM_EPS = 1e-08
ADAM_WD = 0.01
ADAM_STEP = 10

N_DEV = 8


def _dot(a, b):
    return jnp.dot(a, b, preferred_element_type=F32)


def _dot_nt(a, b):
    return lax.dot_general(a, b, (((1,), (1,)), ((), ())), preferred_element_type=F32)


def _dot_tn(a, b):
    return lax.dot_general(a, b, (((0,), (0,)), ((), ())), preferred_element_type=F32)


def _rms_scale(x):
    return lax.rsqrt(jnp.mean(x * x, axis=-1, keepdims=True) + RMS_EPS)


def _rms_bwd(dy, x, r, g):
    t = dy * g
    dx = r * t - x * (r * r * r) * jnp.mean(t * x, axis=-1, keepdims=True)
    dg = jnp.sum(dy * x * r, axis=0, keepdims=True)
    return dx, dg


def _const_spec(shape):
    nd = len(shape)
    return pl.BlockSpec(shape, lambda *_: (0,) * nd, pipeline_mode=pl.Buffered(1))


def _in_hbm(*arrays):
    return tuple(pltpu.with_memory_space_constraint(a, pltpu.HBM) for a in arrays)


def _hbm_out(shapes):
    if isinstance(shapes, (list, tuple)):
        return [_hbm_out(s) for s in shapes]
    return shapes if isinstance(shapes, pl.MemoryRef) else pltpu.HBM(shapes.shape, shapes.dtype)


def _whole(shape):
    nd = len(shape)
    return pl.BlockSpec(shape, lambda *_: (0,) * nd)


def _params(sem, vmem=None):
    return pltpu.CompilerParams(dimension_semantics=sem, vmem_limit_bytes=vmem)


MESH_IDS = pl.DeviceIdType.MESH
HBM_SPEC = pl.BlockSpec(memory_space=pl.ANY)


class _Carry:
    def __init__(self, inputs, out_shapes, sems, start, finish, middle=None):
        self.inputs, self.out_shapes, self.sems = list(inputs), list(out_shapes), list(sems)
        self.start, self.middle, self.finish = start, middle, finish


def _join(a, b):
    na_in, na_out, na_sem = len(a.inputs), len(a.out_shapes), len(a.sems)

    def both(phase):
        def run(ins, outs, sems):
            for carry, lo in ((a, True), (b, False)):
                part = (lambda seq, n: seq[:n] if lo else seq[n:])
                if getattr(carry, phase) is not None:
                    getattr(carry, phase)(part(ins, na_in), part(outs, na_out), part(sems, na_sem))
        return run

    middle = both("middle") if (a.middle or b.middle) else None
    return _Carry(a.inputs + b.inputs, a.out_shapes + b.out_shapes, a.sems + b.sems, both("start"), both("finish"), middle)


def _hosted_call(body, carry, edge, *, name, grid, in_specs, out_specs, out_shape, scratch_shapes, compiler_params, inputs):
    n_in, n_out = len(in_specs), len(out_specs)
    inputs = [a if s.memory_space == pltpu.SMEM else _in_hbm(a)[0] for a, s in zip(inputs, in_specs)]
    out_shape = _hbm_out(list(out_shape))
    if carry is None:
        outs = pl.pallas_call(body, name=name, grid=grid, in_specs=in_specs, out_specs=out_specs, out_shape=out_shape,
                              scratch_shapes=scratch_shapes, compiler_params=compiler_params)(*inputs)
        return list(outs), []
    c_in, c_out, c_sem = len(carry.inputs), len(carry.out_shapes), len(carry.sems)

    def wrapped(*refs):
        ins, refs = refs[:n_in], refs[n_in:]
        cins, refs = refs[:c_in], refs[c_in:]
        outs, refs = refs[:n_out], refs[n_out:]
        couts, refs = refs[:c_out], refs[c_out:]
        scratch, csems = refs[:len(refs) - c_sem], refs[len(refs) - c_sem:]
        first, middle, last = edge()

        @pl.when(first)
        def _():
            carry.start(cins, couts, csems)

        body(*ins, *outs, *scratch)

        if carry.middle is not None:
            @pl.when(middle)
            def _():
                carry.middle(cins, couts, csems)

        @pl.when(last)
        def _():
            carry.finish(cins, couts, csems)

    outs = pl.pallas_call(
        wrapped, name=name, grid=grid, in_specs=list(in_specs) + [HBM_SPEC] * c_in,
        out_specs=list(out_specs) + [HBM_SPEC] * c_out, out_shape=out_shape + _hbm_out(carry.out_shapes),
        scratch_shapes=list(scratch_shapes) + carry.sems, compiler_params=compiler_params)(*inputs, *_in_hbm(*carry.inputs))
    return list(outs[:n_out]), list(outs[n_out:])


def _pass_on_step(n_steps):
    return max(0, min((7 * n_steps) // 8, n_steps - 2))


def _edge_1d(n_steps, pass_on_last=False):
    middle = n_steps - 1 if pass_on_last else _pass_on_step(n_steps)
    return lambda: (pl.program_id(0) == 0, pl.program_id(0) == middle, pl.program_id(0) == n_steps - 1)


def _edge_2d(n0, n1):
    def edge():
        step = pl.program_id(0) * n1 + pl.program_id(1)
        return step == 0, step == _pass_on_step(n0 * n1), step == n0 * n1 - 1
    return edge


def _run_carry(carry, name):
    c_in, c_out = len(carry.inputs), len(carry.out_shapes)

    def body(*refs):
        ins, outs, sems = refs[:c_in], refs[c_in:c_in + c_out], refs[c_in + c_out:]
        carry.start(ins, outs, sems)
        if carry.middle is not None:
            carry.middle(ins, outs, sems)
        carry.finish(ins, outs, sems)

    return pl.pallas_call(body, name=name, in_specs=[HBM_SPEC] * c_in, out_specs=[HBM_SPEC] * c_out,
                          out_shape=_hbm_out(carry.out_shapes), scratch_shapes=carry.sems)(*_in_hbm(*carry.inputs))


def _in_proj_fwd(x, g1, w_in_t, tile, carry=None):
    T = x.shape[0]

    def body(x_ref, g_ref, w_ref, q_ref, k_ref, v_ref, u_ref, ga_ref, gs_ref, h_ref):
        xv = x_ref[...]
        h = (xv * _rms_scale(xv) * g_ref[...]).astype(BF16)
        h_ref[...] = h
        outs = (q_ref, k_ref, v_ref, u_ref, ga_ref, gs_ref)
        for p, o_ref in enumerate(outs):
            o_ref[...] = _dot_nt(h, w_ref[SPLITS[p]:SPLITS[p + 1], :]).astype(o_ref.dtype)

    widths = [SPLITS[p + 1] - SPLITS[p] for p in range(6)] + [D_MODEL]
    dtypes = [BF16, BF16, BF16, F32, F32, F32, BF16]
    return _hosted_call(
        body, carry, _edge_1d(T // tile), name="in_proj_fwd", grid=(T // tile,),
        in_specs=[pl.BlockSpec((tile, D_MODEL), lambda i: (i, 0)), _const_spec((1, D_MODEL)), _const_spec((IN_W, D_MODEL))],
        out_specs=[pl.BlockSpec((tile, w), lambda i: (i, 0)) for w in widths],
        out_shape=[jax.ShapeDtypeStruct((T, w), dt) for w, dt in zip(widths, dtypes)],
        scratch_shapes=[], compiler_params=_params(("arbitrary",), VMEM_MID), inputs=(x, g1, w_in_t))


PROJ_PARTS = (512, 256, 512, 2048)
PROJ_GRAD_BLOCK = 256


def _in_proj_weight_grad(h, dparts):
    T = h.shape[0]
    blocks = [wd // PROJ_GRAD_BLOCK for wd in PROJ_PARTS]
    starts = [sum(blocks[:p]) for p in range(len(blocks))]

    def body(h_ref, *refs):
        part_refs, o_ref = refs[:-1], refs[-1]
        j = pl.program_id(0)
        for p_ref, start, count in zip(part_refs, starts, blocks):
            @pl.when((j >= start) & (j < start + count))
            def _(p_ref=p_ref):
                o_ref[...] = _dot_tn(p_ref[...], h_ref[...])

    def part_spec(start, count):
        return pl.BlockSpec((T, PROJ_GRAD_BLOCK), lambda j: (0, jnp.clip(j - start, 0, count - 1)))

    return pl.pallas_call(
        body, name="in_proj_weight_grad", grid=(sum(blocks),),
        in_specs=[_const_spec((T, D_MODEL))] + [part_spec(s, c) for s, c in zip(starts, blocks)],
        out_specs=pl.BlockSpec((PROJ_GRAD_BLOCK, D_MODEL), lambda j: (j, 0)),
        out_shape=_hbm_out(jax.ShapeDtypeStruct((IN_W, D_MODEL), F32)),
        compiler_params=_params(("arbitrary",), VMEM_MID),
    )(*_in_hbm(h, *dparts))


def _in_proj_input_grad(x, g1, w_in_t, dx1, dparts, tile, first_tile, n_tiles, name, carry=None):
    offsets = [sum(PROJ_PARTS[:p]) for p in range(len(PROJ_PARTS))]

    def body(x_ref, g_ref, w_ref, dx1_ref, *refs):
        part_refs, (gx_ref, dg_ref) = refs[:len(PROJ_PARTS)], refs[len(PROJ_PARTS):]
        i = pl.program_id(0)
        xv = x_ref[...]
        r = _rms_scale(xv)
        g = g_ref[...]
        dh = sum(_dot(p_ref[...], w_ref[off:off + wd, :]) for p_ref, off, wd in zip(part_refs, offsets, PROJ_PARTS))
        dxn, dg = _rms_bwd(dh, xv, r, g)
        gx_ref[...] = dx1_ref[...] + dxn

        @pl.when(i == 0)
        def _():
            dg_ref[...] = dg

        @pl.when(i > 0)
        def _():
            dg_ref[...] += dg

    tok = lambda wd: pl.BlockSpec((tile, wd), lambda i: (i + first_tile, 0))
    return _hosted_call(
        body, carry, _edge_1d(n_tiles), name=name, grid=(n_tiles,),
        in_specs=[tok(D_MODEL), _const_spec((1, D_MODEL)), _const_spec((IN_W, D_MODEL)), tok(D_MODEL)] + [tok(wd) for wd in PROJ_PARTS],
        out_specs=[pl.BlockSpec((tile, D_MODEL), lambda i: (i, 0)), pl.BlockSpec((1, D_MODEL), lambda i: (0, 0))],
        out_shape=[jax.ShapeDtypeStruct((n_tiles * tile, D_MODEL), F32), jax.ShapeDtypeStruct((1, D_MODEL), F32)],
        scratch_shapes=[], compiler_params=_params(("arbitrary",), VMEM_MID), inputs=(x, g1, w_in_t, dx1, *dparts))


def _bucket_table():
    qi = np.arange(BLOCK)[:, None]
    kj = np.arange(2 * BLOCK)[None, :]
    dist = qi + BLOCK - kj
    max_exact = N_BUCKETS // 2
    d = np.maximum(dist, 0)
    df = np.maximum(d, 1).astype(np.float32)
    large = max_exact + (np.log(df / np.float32(max_exact)) / np.float32(math.log(BLOCK / max_exact))
                         * np.float32(N_BUCKETS - max_exact)).astype(np.int32)
    large = np.minimum(large, N_BUCKETS - 1)
    bucket = np.where(d < max_exact, d, large)
    return np.where((dist >= 0) & (dist < BLOCK), bucket, -1).astype(np.int32)


def _build_bias(bucket_ref, rb_ref, bias_ref):
    bk = bucket_ref[...]
    for h in range(N_HEADS):
        def add(b, acc, h=h):
            return acc + jnp.where(bk == b, rb_ref[h, b], 0.0)
        bias_ref[h] = lax.fori_loop(0, N_BUCKETS, add, jnp.zeros((BLOCK, 2 * BLOCK), F32))


def _kv_variants(prev_ref, cur_ref):
    cat = jnp.concatenate([prev_ref[...], cur_ref[...]], axis=0)
    lo = lax.broadcasted_iota(jnp.int32, cat.shape, 1) < HEAD_DIM
    zero = jnp.zeros_like(cat)
    head0_lo = jnp.where(lo, cat, zero)
    head1_hi = jnp.where(lo, zero, cat)
    return ((head0_lo, pltpu.roll(head0_lo, HEAD_DIM, 1)), (pltpu.roll(head1_hi, HEAD_DIM, 1), head1_hi))


def _merge_kv_grads(g):
    lo = lax.broadcasted_iota(jnp.int32, g[0][0].shape, 1) < HEAD_DIM
    return jnp.where(lo, g[0][0] + pltpu.roll(g[0][1], HEAD_DIM, 1), g[1][1] + pltpu.roll(g[1][0], HEAD_DIM, 1))


def _head_lanes(h):
    return slice((h // 2) * LANES, (h // 2 + 1) * LANES)


def _attn_probs(q_ref, kvar, bias_ref, sk_ref, valid, s_ref):
    for h in range(N_HEADS):
        s_ref[h] = _dot_nt(q_ref[:, _head_lanes(h)], kvar[h // 4][h % 2])
    head = lax.broadcasted_iota(jnp.int32, (N_HEADS, 1, 1), 0)
    sink = jnp.zeros((N_HEADS, 1, 1), F32)
    for h in range(N_HEADS):
        sink = jnp.where(head == h, sk_ref[0, h], sink)
    s = jnp.where(valid[None], s_ref[...] * (HEAD_DIM ** -0.5) + bias_ref[...], NEG_INF)
    m = jnp.maximum(jnp.max(s, axis=-1, keepdims=True), sink)
    p = jnp.exp(s - m)
    e_sink = jnp.exp(sink - m)
    inv = 1.0 / (jnp.sum(p, axis=-1, keepdims=True) + e_sink)
    return p * inv, e_sink * inv


def _attn_valid(bucket_ref, n):
    col = lax.broadcasted_iota(jnp.int32, (BLOCK, 2 * BLOCK), 1)
    return (bucket_ref[...] >= 0) & ((n > 0) | (col >= BLOCK))


def _attn_fwd(q, k, v, bucket, rel_bias, sinks, carry=None):
    T = q.shape[0]
    nb = T // BLOCK

    def body(q_ref, kc_ref, kp_ref, vc_ref, vp_ref, bucket_ref, rb_ref, sk_ref, o_ref, bias_ref, s_ref, p_ref):
        n = pl.program_id(0)

        @pl.when(n == 0)
        def _():
            _build_bias(bucket_ref, rb_ref, bias_ref)

        kvar = _kv_variants(kp_ref, kc_ref)
        vvar = _kv_variants(vp_ref, vc_ref)
        pr, _ = _attn_probs(q_ref, kvar, bias_ref, sk_ref, _attn_valid(bucket_ref, n), s_ref)
        p_ref[...] = pr.astype(BF16)
        for m in range(N_HEADS // 2):
            acc = _dot(p_ref[2 * m], vvar[m // 2][0]) + _dot(p_ref[2 * m + 1], vvar[m // 2][1])
            o_ref[:, m * LANES:(m + 1) * LANES] = acc.astype(o_ref.dtype)

    cur = lambda w: pl.BlockSpec((BLOCK, w), lambda n: (n, 0))
    prev = lambda w: pl.BlockSpec((BLOCK, w), lambda n: (jnp.maximum(n - 1, 0), 0))
    smem = pl.BlockSpec(memory_space=pltpu.SMEM)
    return _hosted_call(
        body, carry, _edge_1d(nb, pass_on_last=True), name="attn_fwd", grid=(nb,),
        in_specs=[cur(ATTN_W), cur(KV_W), prev(KV_W), cur(KV_W), prev(KV_W), _const_spec((BLOCK, 2 * BLOCK)), smem, smem],
        out_specs=[cur(ATTN_W)],
        out_shape=[jax.ShapeDtypeStruct((T, ATTN_W), BF16)],
        scratch_shapes=[pltpu.VMEM((N_HEADS, BLOCK, 2 * BLOCK), F32), pltpu.VMEM((N_HEADS, BLOCK, 2 * BLOCK), F32),
                        pltpu.VMEM((N_HEADS, BLOCK, 2 * BLOCK), BF16)],
        compiler_params=_params(("arbitrary",)), inputs=(q, k, k, v, v, bucket, rel_bias, sinks))


ATTN_SMALL_ROWS = N_BUCKETS + SUBLANES


def _attn_bwd(q, k, v, datt, bucket, rel_bias, sinks, carry=None):
    T = q.shape[0]
    nb = T // BLOCK

    def body(q_ref, do_ref, kc_ref, kp_ref, vc_ref, vp_ref, bucket_ref, rb_ref, sk_ref,
             dq_ref, dkv_ref, small_ref, bias_ref, ds_sum_ref, dsink_ref, kcarry_ref, vcarry_ref,
             s_ref, dp_ref, p_ref, dsc_ref):
        n = pl.program_id(0)

        @pl.when(n == 0)
        def _():
            _build_bias(bucket_ref, rb_ref, bias_ref)
            ds_sum_ref[...] = jnp.zeros_like(ds_sum_ref)
            dsink_ref[...] = jnp.zeros_like(dsink_ref)
            kcarry_ref[...] = jnp.zeros_like(kcarry_ref)
            vcarry_ref[...] = jnp.zeros_like(vcarry_ref)

        @pl.when(n < nb)
        def _():
            kvar = _kv_variants(kp_ref, kc_ref)
            vvar = _kv_variants(vp_ref, vc_ref)
            pr, p_sink = _attn_probs(q_ref, kvar, bias_ref, sk_ref, _attn_valid(bucket_ref, n), s_ref)
            for h in range(N_HEADS):
                dp_ref[h] = _dot_nt(do_ref[:, _head_lanes(h)], vvar[h // 4][h % 2])
            dp = dp_ref[...]
            dsum = jnp.sum(pr * dp, axis=-1, keepdims=True)
            ds = pr * (dp - dsum)
            ds_sum_ref[...] += ds
            dsink_ref[...] -= jnp.sum(p_sink * dsum, axis=1, keepdims=True)
            dsc_ref[...] = (ds * (HEAD_DIM ** -0.5)).astype(BF16)
            p_ref[...] = pr.astype(BF16)
            for m in range(N_HEADS // 2):
                dqm = _dot(dsc_ref[2 * m], kvar[m // 2][0]) + _dot(dsc_ref[2 * m + 1], kvar[m // 2][1])
                dq_ref[:, m * LANES:(m + 1) * LANES] = dqm.astype(dq_ref.dtype)
            dk_var = [[None, None], [None, None]]
            dv_var = [[None, None], [None, None]]
            for kvh in range(2):
                for e in range(2):
                    heads = [h for h in range(N_HEADS) if h // 4 == kvh and h % 2 == e]
                    dk_var[kvh][e] = sum(_dot_tn(dsc_ref[h], q_ref[:, _head_lanes(h)]) for h in heads)
                    dv_var[kvh][e] = sum(_dot_tn(p_ref[h], do_ref[:, _head_lanes(h)]) for h in heads)
            dk_cat = _merge_kv_grads(dk_var)
            dv_cat = _merge_kv_grads(dv_var)

            @pl.when(n > 0)
            def _():
                dkv_ref[:, :KV_W] = (kcarry_ref[...] + dk_cat[:BLOCK]).astype(BF16)
                dkv_ref[:, KV_W:] = (vcarry_ref[...] + dv_cat[:BLOCK]).astype(BF16)

            kcarry_ref[...] = dk_cat[BLOCK:]
            vcarry_ref[...] = dv_cat[BLOCK:]

        @pl.when(n == nb)
        def _():
            dkv_ref[:, :KV_W] = kcarry_ref[...].astype(BF16)
            dkv_ref[:, KV_W:] = vcarry_ref[...].astype(BF16)
            bk = bucket_ref[...]
            row = lax.broadcasted_iota(jnp.int32, (N_HEADS, ATTN_SMALL_ROWS, LANES), 1)

            def add(b, acc):
                masked = jnp.where((bk == b)[None], ds_sum_ref[...], 0.0)
                val = jnp.sum(jnp.sum(masked, axis=1, keepdims=True), axis=2, keepdims=True)
                return acc + jnp.where(row == b, val, 0.0)

            small_ref[...] = lax.fori_loop(0, N_BUCKETS, add, jnp.where(row == N_BUCKETS, dsink_ref[...], 0.0))

    last = nb - 1
    cur = lambda w: pl.BlockSpec((BLOCK, w), lambda n: (jnp.minimum(n, last), 0))
    prev = lambda w: pl.BlockSpec((BLOCK, w), lambda n: (jnp.clip(n - 1, 0, last), 0))
    smem = pl.BlockSpec(memory_space=pltpu.SMEM)
    return _hosted_call(
        body, carry, _edge_1d(nb + 1), name="attn_bwd", grid=(nb + 1,),
        in_specs=[cur(ATTN_W), cur(ATTN_W), cur(KV_W), prev(KV_W), cur(KV_W), prev(KV_W),
                  _const_spec((BLOCK, 2 * BLOCK)), smem, smem],
        out_specs=[cur(ATTN_W), prev(2 * KV_W), pl.BlockSpec((N_HEADS, ATTN_SMALL_ROWS, LANES), lambda n: (0, 0, 0))],
        out_shape=[jax.ShapeDtypeStruct((T, ATTN_W), BF16), jax.ShapeDtypeStruct((T, 2 * KV_W), BF16),
                   jax.ShapeDtypeStruct((N_HEADS, ATTN_SMALL_ROWS, LANES), F32)],
        scratch_shapes=[pltpu.VMEM((N_HEADS, BLOCK, 2 * BLOCK), F32), pltpu.VMEM((N_HEADS, BLOCK, 2 * BLOCK), F32),
                        pltpu.VMEM((N_HEADS, 1, 1), F32), pltpu.VMEM((BLOCK, KV_W), F32), pltpu.VMEM((BLOCK, KV_W), F32),
                        pltpu.VMEM((N_HEADS, BLOCK, 2 * BLOCK), F32), pltpu.VMEM((N_HEADS, BLOCK, 2 * BLOCK), F32),
                        pltpu.VMEM((N_HEADS, BLOCK, 2 * BLOCK), BF16), pltpu.VMEM((N_HEADS, BLOCK, 2 * BLOCK), BF16)],
        compiler_params=_params(("arbitrary",)), inputs=(q, datt, k, k, v, v, bucket, rel_bias, sinks))


SCAN_UNROLL = 4


def _cmul(ar, ai, br, bi):
    return ar * br - ai * bi, ar * bi + ai * br


def _cmul_conj(ar, ai, br, bi):
    return ar * br + ai * bi, ar * bi - ai * br


def _ssm_discretize(lr, li, ldt):
    dt = jnp.exp(ldt)
    mag = jnp.exp(lr * dt)
    ab_re = mag * jnp.cos(li * dt)
    ab_im = mag * jnp.sin(li * dt)
    nr = ab_re - 1.0
    den = lr * lr + li * li
    f_re = (nr * lr + ab_im * li) / den
    f_im = (ab_im * lr - nr * li) / den
    return ab_re, ab_im, f_re, f_im


def _ssm_prep(lam_re, lam_im, ldt_rep, bd_re, bd_im):
    def body(lr_ref, li_ref, ldt_ref, bdr_ref, bdi_ref, ar_ref, ai_ref, br_ref, bi_ref):
        ab_re, ab_im, f_re, f_im = _ssm_discretize(lr_ref[...], li_ref[...], ldt_ref[...])
        ar_ref[...] = ab_re
        ai_ref[...] = ab_im
        bdr, bdi = bdr_ref[0], bdi_ref[0]
        br_ref[0] = (bdr * f_re - bdi * f_im).astype(BF16)
        bi_ref[0] = (bdi * f_re + bdr * f_im).astype(BF16)

    row = pl.BlockSpec((1, SSM_LANE_BLOCK), lambda j: (0, j))
    mat = pl.BlockSpec((1, LANES, SSM_LANE_BLOCK), lambda j: (j, 0, 0))
    return pl.pallas_call(
        body, name="ssm_prep", grid=(N_SSM_BLOCKS,),
        in_specs=[row, row, row, mat, mat], out_specs=[row, row, mat, mat],
        out_shape=[jax.ShapeDtypeStruct((1, STATES), F32)] * 2 + [jax.ShapeDtypeStruct((N_SSM_BLOCKS, LANES, SSM_LANE_BLOCK), BF16)] * 2,
        compiler_params=_params(("arbitrary",)),
    )(*_in_hbm(lam_re, lam_im, ldt_rep, bd_re, bd_im))


def _ssm_prep_bwd(lam_re, lam_im, ldt_rep, bd_re, bd_im, dbr, dbi, da_re, da_im):
    def body(lr_ref, li_ref, ldt_ref, bdr_ref, bdi_ref, dbr_ref, dbi_ref, dar_ref, dai_ref,
             dbdr_ref, dbdi_ref, dlr_ref, dli_ref, dldt_ref):
        lr, li, ldt = lr_ref[...], li_ref[...], ldt_ref[...]
        (_, _, f_re, f_im), vjp = jax.vjp(_ssm_discretize, lr, li, ldt)
        bdr, bdi, gbr, gbi = bdr_ref[0], bdi_ref[0], dbr_ref[0], dbi_ref[0]
        dbdr_ref[0] = gbr * f_re + gbi * f_im
        dbdi_ref[0] = gbi * f_re - gbr * f_im
        df_re = jnp.sum(gbr * bdr + gbi * bdi, axis=0, keepdims=True)
        df_im = jnp.sum(gbi * bdr - gbr * bdi, axis=0, keepdims=True)
        dlr, dli, dldt = vjp((dar_ref[...], dai_ref[...], df_re, df_im))
        dlr_ref[...] = dlr
        dli_ref[...] = dli
        dldt_ref[...] = dldt

    row = pl.BlockSpec((1, SSM_LANE_BLOCK), lambda j: (0, j))
    mat = pl.BlockSpec((1, LANES, SSM_LANE_BLOCK), lambda j: (j, 0, 0))
    mat_shape = jax.ShapeDtypeStruct((N_SSM_BLOCKS, LANES, SSM_LANE_BLOCK), F32)
    row_shape = jax.ShapeDtypeStruct((1, STATES), F32)
    return pl.pallas_call(
        body, name="ssm_prep_bwd", grid=(N_SSM_BLOCKS,),
        in_specs=[row, row, row, mat, mat, mat, mat, row, row], out_specs=[mat, mat, row, row, row],
        out_shape=[mat_shape, mat_shape, row_shape, row_shape, row_shape],
        compiler_params=_params(("arbitrary",)),
    )(*_in_hbm(lam_re, lam_im, ldt_rep, bd_re, bd_im, dbr, dbi, da_re, da_im))


def _group_sum(x):
    def body(x_ref, o_ref):
        o_ref[...] = jnp.sum(x_ref[...], axis=1, keepdims=True)
    return pl.pallas_call(body, name="ssm_group_sum", grid=(1,), in_specs=[_whole(x.shape)], out_specs=_whole((N_GROUPS, 1)),
                          out_shape=jax.ShapeDtypeStruct((N_GROUPS, 1), F32))(*_in_hbm(x))


def _power_table(ar, ai, p_re_ref, p_im_ref, steps):
    shape = (SUBLANES, SSM_LANE_BLOCK)
    p_re_ref[0:SUBLANES] = jnp.broadcast_to(ar, shape)
    p_im_ref[0:SUBLANES] = jnp.broadcast_to(ai, shape)
    m = 1
    while m < steps:
        rows = m * SUBLANES
        top_re = p_re_ref[rows - SUBLANES:rows]
        top_im = p_im_ref[rows - SUBLANES:rows]
        cur_re = p_re_ref[0:rows].reshape(m, SUBLANES, SSM_LANE_BLOCK)
        cur_im = p_im_ref[0:rows].reshape(m, SUBLANES, SSM_LANE_BLOCK)
        nxt_re, nxt_im = _cmul(cur_re, cur_im, top_re[None], top_im[None])
        p_re_ref[rows:2 * rows] = nxt_re.reshape(rows, SSM_LANE_BLOCK)
        p_im_ref[rows:2 * rows] = nxt_im.reshape(rows, SSM_LANE_BLOCK)
        m *= 2


def _to_segments(src_ref, dst_ref, steps):
    for s in range(SUBLANES):
        dst_ref[pl.ds(s, steps, stride=SUBLANES), :] = src_ref[s * steps:(s + 1) * steps, :]


def _from_segments(src_ref, dst_ref, steps):
    for s in range(SUBLANES):
        dst_ref[s * steps:(s + 1) * steps, :] = src_ref[pl.ds(s, steps, stride=SUBLANES), :]


def _segment_carries(e_re, e_im, an_re, an_im, c_re, c_im, reverse):
    order = range(SUBLANES - 1, -1, -1) if reverse else range(SUBLANES)
    ins_re, ins_im = [None] * SUBLANES, [None] * SUBLANES
    for s in order:
        ins_re[s], ins_im[s] = c_re, c_im
        pr, pi = _cmul(an_re, an_im, c_re, c_im)
        c_re = e_re[s:s + 1] + pr
        c_im = e_im[s:s + 1] + pi
    return jnp.concatenate(ins_re, axis=0), jnp.concatenate(ins_im, axis=0), c_re, c_im


def _ssm_fwd(u, a_re, a_im, b_re, b_im, c_re, c_im, d_skip, chunk, carry=None):
    T = u.shape[0]
    nc = T // chunk
    steps = chunk // SUBLANES
    blk = SSM_LANE_BLOCK

    def body(u_ref, ar_ref, ai_ref, br_ref, bi_ref, cr_ref, ci_ref, dk_ref,
             y_ref, hr_ref, hi_ref, inr_ref, ini_ref, useg_ref, yseg_ref, pr_ref, pi_ref, carry_ref):
        c = pl.program_id(1)
        ar, ai = ar_ref[...], ai_ref[...]

        @pl.when(c == 0)
        def _():
            _power_table(ar, ai, pr_ref, pi_ref, steps)
            carry_ref[...] = jnp.zeros_like(carry_ref)

        _to_segments(u_ref, useg_ref, steps)
        ub = useg_ref[...].astype(BF16)
        hr_ref[...] = _dot(ub, br_ref[0])
        hi_ref[...] = _dot(ub, bi_ref[0])
        first = slice(0, SUBLANES)

        def scan(t4, prev):
            for j in range(SCAN_UNROLL):
                rows = pl.ds(pl.multiple_of((t4 * SCAN_UNROLL + j) * SUBLANES, SUBLANES), SUBLANES)
                pr, pi = _cmul(pr_ref[first, :], pi_ref[first, :], prev[0], prev[1])
                prev = (pr + hr_ref[rows, :], pi + hi_ref[rows, :])
                hr_ref[rows, :] = prev[0]
                hi_ref[rows, :] = prev[1]
            return prev

        zero = jnp.zeros((SUBLANES, blk), F32)
        lax.fori_loop(0, steps // SCAN_UNROLL, scan, (zero, zero))

        top = slice(chunk - SUBLANES, chunk)
        in_re, in_im, out_re, out_im = _segment_carries(
            hr_ref[top, :], hi_ref[top, :], pr_ref[top, :][0:1], pi_ref[top, :][0:1],
            carry_ref[0:1, :], carry_ref[1:2, :], reverse=False)
        carry_ref[0:1, :] = out_re
        carry_ref[1:2, :] = out_im
        inr_ref[...] = in_re
        ini_ref[...] = in_im

        def fix(t4, _):
            for j in range(SCAN_UNROLL):
                rows = pl.ds(pl.multiple_of((t4 * SCAN_UNROLL + j) * SUBLANES, SUBLANES), SUBLANES)
                fr, fi = _cmul(pr_ref[rows, :], pi_ref[rows, :], in_re, in_im)
                hr_ref[rows, :] += fr
                hi_ref[rows, :] += fi
            return 0

        lax.fori_loop(0, steps // SCAN_UNROLL, fix, 0)

        yseg_ref[...] = _dot(hr_ref[...].astype(BF16), cr_ref[0]) - _dot(hi_ref[...].astype(BF16), ci_ref[0])
        _from_segments(yseg_ref, y_ref, steps)
        y_ref[...] += dk_ref[...] * u_ref[...]

    row = pl.BlockSpec((1, blk), lambda j, c: (0, j))
    b_mat = pl.BlockSpec((1, LANES, blk), lambda j, c: (j, 0, 0))
    c_mat = pl.BlockSpec((1, blk, LANES), lambda j, c: (j, 0, 0))
    tok = pl.BlockSpec((chunk, LANES), lambda j, c: (c, j))
    state = pl.BlockSpec((chunk, blk), lambda j, c: (c, j))
    enter = pl.BlockSpec((SUBLANES, blk), lambda j, c: (c, j))
    return _hosted_call(
        body, carry, _edge_2d(N_SSM_BLOCKS, nc), name="ssm_fwd", grid=(N_SSM_BLOCKS, nc),
        in_specs=[tok, row, row, b_mat, b_mat, c_mat, c_mat, pl.BlockSpec((1, LANES), lambda j, c: (0, j))],
        out_specs=[tok, state, state, enter, enter],
        out_shape=[jax.ShapeDtypeStruct((T, SSM_W), F32), jax.ShapeDtypeStruct((T, STATES), F32),
                   jax.ShapeDtypeStruct((T, STATES), F32), jax.ShapeDtypeStruct((nc * SUBLANES, STATES), F32),
                   jax.ShapeDtypeStruct((nc * SUBLANES, STATES), F32)],
        scratch_shapes=[pltpu.VMEM((chunk, LANES), F32), pltpu.VMEM((chunk, LANES), F32),
                        pltpu.VMEM((chunk, blk), F32), pltpu.VMEM((chunk, blk), F32), pltpu.VMEM((SUBLANES, blk), F32)],
        compiler_params=_params(("arbitrary", "arbitrary"), VMEM_MID),
        inputs=(u, a_re, a_im, b_re, b_im, c_re, c_im, d_skip))


def _ssm_bwd(dy, u, h_re, h_im, in_re, in_im, a_re, a_im, b_re, b_im, c_re, c_im, d_skip, chunk, carry=None):
    T = u.shape[0]
    nc = T // chunk
    steps = chunk // SUBLANES
    blk = SSM_LANE_BLOCK

    def body(dy_ref, u_ref, hr_ref, hi_ref, inr_ref, ini_ref, ar_ref, ai_ref, br_ref, bi_ref, cr_ref, ci_ref, dk_ref,
             du_ref, dbr_ref, dbi_ref, dcr_ref, dci_ref, dar_ref, dai_ref, ddk_ref,
             dyseg_ref, useg_ref, duseg_ref, gr_ref, gi_ref, pr_ref, pi_ref, carry_ref, accr_ref, acci_ref):
        c = pl.program_id(1)
        ar, ai = ar_ref[...], ai_ref[...]

        @pl.when(c == 0)
        def _():
            _power_table(ar, ai, pr_ref, pi_ref, steps)
            carry_ref[...] = jnp.zeros_like(carry_ref)
            accr_ref[...] = jnp.zeros_like(accr_ref)
            acci_ref[...] = jnp.zeros_like(acci_ref)

        _to_segments(dy_ref, dyseg_ref, steps)
        _to_segments(u_ref, useg_ref, steps)
        dyb = dyseg_ref[...].astype(BF16)
        ub = useg_ref[...].astype(BF16)
        gr_ref[...] = _dot_nt(dyb, cr_ref[0])
        gi_ref[...] = -_dot_nt(dyb, ci_ref[0])
        dcr = _dot_tn(hr_ref[...].astype(BF16), dyb)
        dci = -_dot_tn(hi_ref[...].astype(BF16), dyb)
        ddk = jnp.sum(dy_ref[...] * u_ref[...], axis=0, keepdims=True)

        first = slice(0, SUBLANES)

        def scan(k4, nxt):
            for j in range(SCAN_UNROLL):
                t = steps - 1 - (k4 * SCAN_UNROLL + j)
                rows = pl.ds(pl.multiple_of(t * SUBLANES, SUBLANES), SUBLANES)
                pr, pi = _cmul_conj(pr_ref[first, :], pi_ref[first, :], nxt[0], nxt[1])
                nxt = (pr + gr_ref[rows, :], pi + gi_ref[rows, :])
                gr_ref[rows, :] = nxt[0]
                gi_ref[rows, :] = nxt[1]
            return nxt

        top = slice(chunk - SUBLANES, chunk)
        zero = jnp.zeros((SUBLANES, blk), F32)
        lax.fori_loop(0, steps // SCAN_UNROLL, scan, (zero, zero))

        gin_re, gin_im, out_re, out_im = _segment_carries(
            gr_ref[0:SUBLANES, :], gi_ref[0:SUBLANES, :], pr_ref[top, :][0:1], -pi_ref[top, :][0:1],
            carry_ref[0:1, :], carry_ref[1:2, :], reverse=True)
        carry_ref[0:1, :] = out_re
        carry_ref[1:2, :] = out_im

        def fix_row(rows, prow, hp_re, hp_im, acc):
            fr, fi = _cmul_conj(pr_ref[prow, :], pi_ref[prow, :], gin_re, gin_im)
            g_re = gr_ref[rows, :] + fr
            g_im = gi_ref[rows, :] + fi
            gr_ref[rows, :] = g_re
            gi_ref[rows, :] = g_im
            return acc[0] + g_re * hp_re + g_im * hp_im, acc[1] + g_im * hp_re - g_re * hp_im

        def fix_at(t, acc):
            aligned = (lambda r: r * SUBLANES) if isinstance(t, int) else (lambda r: pl.multiple_of(r * SUBLANES, SUBLANES))
            rows, before, prow = (pl.ds(aligned(r), SUBLANES) for r in (t, t - 1, steps - 1 - t))
            return fix_row(rows, prow, hr_ref[before, :], hi_ref[before, :], acc)

        def fix(t4, acc):
            for j in range(SCAN_UNROLL):
                acc = fix_at(t4 * SCAN_UNROLL + j, acc)
            return acc

        acc = fix_row(first, top, inr_ref[...], ini_ref[...], (accr_ref[...], acci_ref[...]))
        for t in range(1, SCAN_UNROLL):
            acc = fix_at(t, acc)
        acc_re, acc_im = lax.fori_loop(1, steps // SCAN_UNROLL, fix, acc)
        accr_ref[...] = acc_re
        acci_ref[...] = acc_im

        gbr = gr_ref[...].astype(BF16)
        gbi = gi_ref[...].astype(BF16)
        duseg_ref[...] = _dot_nt(gbr, br_ref[0]) + _dot_nt(gbi, bi_ref[0])
        _from_segments(duseg_ref, dyseg_ref, steps)
        du_ref[...] = (dyseg_ref[...] + dk_ref[...] * dy_ref[...]).astype(BF16)
        dbr = _dot_tn(ub, gbr)
        dbi = _dot_tn(ub, gbi)

        @pl.when(c == 0)
        def _():
            dbr_ref[0] = dbr
            dbi_ref[0] = dbi
            dcr_ref[0] = dcr
            dci_ref[0] = dci
            ddk_ref[...] = ddk

        @pl.when(c > 0)
        def _():
            dbr_ref[0] += dbr
            dbi_ref[0] += dbi
            dcr_ref[0] += dcr
            dci_ref[0] += dci
            ddk_ref[...] += ddk

        @pl.when(c == nc - 1)
        def _():
            dar_ref[...] = jnp.sum(acc_re, axis=0, keepdims=True)
            dai_ref[...] = jnp.sum(acc_im, axis=0, keepdims=True)

    rev = lambda c: nc - 1 - c
    row = pl.BlockSpec((1, blk), lambda j, c: (0, j))
    b_mat = pl.BlockSpec((1, LANES, blk), lambda j, c: (j, 0, 0))
    c_mat = pl.BlockSpec((1, blk, LANES), lambda j, c: (j, 0, 0))
    tok = pl.BlockSpec((chunk, LANES), lambda j, c: (rev(c), j))
    state = pl.BlockSpec((chunk, blk), lambda j, c: (rev(c), j))
    enter = pl.BlockSpec((SUBLANES, blk), lambda j, c: (rev(c), j))
    chan = pl.BlockSpec((1, LANES), lambda j, c: (0, j))
    f32 = lambda *s: jax.ShapeDtypeStruct(s, F32)
    return _hosted_call(
        body, carry, _edge_2d(N_SSM_BLOCKS, nc), name="ssm_bwd", grid=(N_SSM_BLOCKS, nc),
        in_specs=[tok, tok, state, state, enter, enter, row, row, b_mat, b_mat, c_mat, c_mat, chan],
        out_specs=[tok, b_mat, b_mat, c_mat, c_mat, row, row, chan],
        out_shape=[jax.ShapeDtypeStruct((T, SSM_W), BF16), f32(N_SSM_BLOCKS, LANES, blk), f32(N_SSM_BLOCKS, LANES, blk),
                   f32(N_SSM_BLOCKS, blk, LANES), f32(N_SSM_BLOCKS, blk, LANES), f32(1, STATES), f32(1, STATES), f32(1, SSM_W)],
        scratch_shapes=[pltpu.VMEM((chunk, LANES), F32), pltpu.VMEM((chunk, LANES), F32), pltpu.VMEM((chunk, LANES), F32),
                        pltpu.VMEM((chunk, blk), F32), pltpu.VMEM((chunk, blk), F32),
                        pltpu.VMEM((chunk, blk), F32), pltpu.VMEM((chunk, blk), F32),
                        pltpu.VMEM((SUBLANES, blk), F32), pltpu.VMEM((SUBLANES, blk), F32), pltpu.VMEM((SUBLANES, blk), F32)],
        compiler_params=_params(("arbitrary", "arbitrary"), VMEM_BIG),
        inputs=(dy, u, h_re, h_im, in_re, in_im, a_re, a_im, b_re, b_im, c_re, c_im, d_skip))


def _merge_forward(y, att, ga, gs, w_glu, w_ssm, w_attn):
    z = jax.nn.gelu(y)
    zb = z.astype(BF16)
    gl = jax.nn.sigmoid(_dot(zb, w_glu))
    z2b = (z * gl).astype(BF16)
    y_ssm = _dot(z2b, w_ssm)
    y_attn = _dot(att, w_attn)
    sa = jax.nn.sigmoid(ga)
    ss = jax.nn.sigmoid(gs)
    merged = (sa * y_attn + ss * y_ssm).astype(BF16)
    return z, zb, gl, z2b, y_ssm, y_attn, sa, ss, merged


def _merge_fwd(x, y, att, ga, gs, g2, g3, w_glu, w_ssm, w_attn, w_out, tile):
    T = x.shape[0]

    def body(x_ref, y_ref, att_ref, ga_ref, gs_ref, g2_ref, g3_ref, wg_ref, ws_ref, wa_ref, wo_ref, x1_ref, o_ref, h2_ref):
        merged = _merge_forward(y_ref[...], att_ref[...], ga_ref[...], gs_ref[...], wg_ref[...], ws_ref[...], wa_ref[...])[-1]
        o = _dot(merged, wo_ref[...])
        x1 = x_ref[...] + o * _rms_scale(o) * g2_ref[...]
        o_ref[...] = o
        x1_ref[...] = x1
        h2_ref[...] = (x1 * _rms_scale(x1) * g3_ref[...]).astype(BF16)

    tok = lambda w: pl.BlockSpec((tile, w), lambda i: (i, 0))
    vec = _const_spec((1, D_MODEL))
    return pl.pallas_call(
        body, name="merge_fwd", grid=(T // tile,),
        in_specs=[tok(D_MODEL), tok(SSM_W), tok(ATTN_W), tok(D_MODEL), tok(D_MODEL), vec, vec,
                  _const_spec((SSM_W, SSM_W)), _const_spec((SSM_W, D_MODEL)), _const_spec((ATTN_W, D_MODEL)),
                  _const_spec((D_MODEL, D_MODEL))],
        out_specs=[tok(D_MODEL), tok(D_MODEL), tok(D_MODEL)],
        out_shape=_hbm_out([jax.ShapeDtypeStruct((T, D_MODEL), F32), jax.ShapeDtypeStruct((T, D_MODEL), F32),
                            jax.ShapeDtypeStruct((T, D_MODEL), BF16)]),
        compiler_params=_params(("arbitrary",), VMEM_MID),
    )(*_in_hbm(x, y, att, ga, gs, g2, g3, w_glu, w_ssm, w_attn, w_out))


def _merge_bwd(dh2, dx2, x1, o, y, att, ga, gs, g2, g3, w_glu, w_ssm, w_attn, w_out, tile, carry=None):
    T = x1.shape[0]
    n_steps = T // tile

    group = min(2, n_steps)
    staged_widths = (D_MODEL, D_MODEL, ATTN_W, D_MODEL, SSM_W, D_MODEL, SSM_W, SSM_W)

    def body(dh2_ref, dx2_ref, x1_ref, o_ref, y_ref, att_ref, ga_ref, gs_ref, g2_ref, g3_ref, wg_ref, ws_ref, wa_ref, wo_ref,
             dx1_ref, dgates_ref, datt_ref, dy_ref, dwg_hbm, dws_hbm, dwa_hbm, dwo_hbm, dg2_ref, dg3_ref,
             awg_ref, aws_ref, awa_ref, awo_ref, *staged):
        i = pl.program_id(0)
        x1v, ov = x1_ref[...], o_ref[...]
        dxn, dg3 = _rms_bwd(dh2_ref[...], x1v, _rms_scale(x1v), g3_ref[...])
        dx1 = dx2_ref[...] + dxn
        dx1_ref[...] = dx1
        do, dg2 = _rms_bwd(dx1, ov, _rms_scale(ov), g2_ref[...])
        dob = do.astype(BF16)

        yv = y_ref[...]
        att = att_ref[...]
        z, zb, gl, z2b, y_ssm, y_attn, sa, ss, merged = _merge_forward(
            yv, att, ga_ref[...], gs_ref[...], wg_ref[...], ws_ref[...], wa_ref[...])
        dmerged = _dot_nt(dob, wo_ref[...])
        dya = (dmerged * sa).astype(BF16)
        dys = (dmerged * ss).astype(BF16)
        dgates_ref[:, :D_MODEL] = (dmerged * y_attn * sa * (1.0 - sa)).astype(BF16)
        dgates_ref[:, D_MODEL:] = (dmerged * y_ssm * ss * (1.0 - ss)).astype(BF16)
        datt_ref[...] = _dot_nt(dya, wa_ref[...]).astype(BF16)
        dz2 = _dot_nt(dys, ws_ref[...])
        dpre = (dz2 * z * gl * (1.0 - gl)).astype(BF16)
        dz = dz2 * gl + _dot_nt(dpre, wg_ref[...])
        _, gelu_vjp = jax.vjp(jax.nn.gelu, yv)
        dy_ref[...] = gelu_vjp(dz)[0]

        part = pl.ds(pl.multiple_of((i % group) * tile, tile), tile)
        for ref, val in zip(staged, (merged, dob, att, dya, z2b, dys, zb, dpre)):
            ref[part, :] = val

        @pl.when(i == 0)
        def _():
            dg2_ref[...] = dg2
            dg3_ref[...] = dg3

        @pl.when(i > 0)
        def _():
            dg2_ref[...] += dg2
            dg3_ref[...] += dg3

        def weight_grads():
            s_merged, s_dob, s_att, s_dya, s_z2b, s_dys, s_zb, s_dpre = (ref[...] for ref in staged)
            return ((awo_ref, _dot_tn(s_merged, s_dob)), (awa_ref, _dot_tn(s_att, s_dya)),
                    (aws_ref, _dot_tn(s_z2b, s_dys)), (awg_ref, _dot_tn(s_zb, s_dpre)))

        @pl.when(i == group - 1)
        def _():
            for ref, val in weight_grads():
                ref[...] = val

        @pl.when((i % group == group - 1) & (i > group - 1))
        def _():
            for ref, val in weight_grads():
                ref[...] += val

        @pl.when(i == n_steps - 1)
        def _():
            pltpu.sync_copy(awg_ref, dwg_hbm)
            pltpu.sync_copy(aws_ref, dws_hbm)
            pltpu.sync_copy(awa_ref, dwa_hbm)
            pltpu.sync_copy(awo_ref, dwo_hbm)

    tok = lambda w: pl.BlockSpec((tile, w), lambda i: (i, 0))
    vec = _const_spec((1, D_MODEL))
    any_ = pl.BlockSpec(memory_space=pl.ANY)
    vec_out = pl.BlockSpec((1, D_MODEL), lambda i: (0, 0))
    f32 = lambda *s: jax.ShapeDtypeStruct(s, F32)
    bf = lambda *s: jax.ShapeDtypeStruct(s, BF16)
    return _hosted_call(
        body, carry, _edge_1d(n_steps), name="merge_bwd", grid=(n_steps,),
        in_specs=[tok(D_MODEL), tok(D_MODEL), tok(D_MODEL), tok(D_MODEL), tok(SSM_W), tok(ATTN_W), tok(D_MODEL), tok(D_MODEL),
                  vec, vec, _const_spec((SSM_W, SSM_W)), _const_spec((SSM_W, D_MODEL)), _const_spec((ATTN_W, D_MODEL)),
                  _const_spec((D_MODEL, D_MODEL))],
        out_specs=[tok(D_MODEL), tok(2 * D_MODEL), tok(ATTN_W), tok(SSM_W), any_, any_, any_, any_, vec_out, vec_out],
        out_shape=[f32(T, D_MODEL), bf(T, 2 * D_MODEL), bf(T, ATTN_W), f32(T, SSM_W),
                   f32(SSM_W, SSM_W), f32(SSM_W, D_MODEL), f32(ATTN_W, D_MODEL), f32(D_MODEL, D_MODEL),
                   f32(1, D_MODEL), f32(1, D_MODEL)],
        scratch_shapes=[pltpu.VMEM((SSM_W, SSM_W), F32), pltpu.VMEM((SSM_W, D_MODEL), F32),
                        pltpu.VMEM((ATTN_W, D_MODEL), F32), pltpu.VMEM((D_MODEL, D_MODEL), F32)]
        + [pltpu.VMEM((group * tile, wd), BF16) for wd in staged_widths],
        compiler_params=_params(("arbitrary",), VMEM_BIG),
        inputs=(dh2, dx2, x1, o, y, att, ga, gs, g2, g3, w_glu, w_ssm, w_attn, w_out))


FF_SHARD = D_FF // N_DEV


def _mlp_fwd(h2, x1, target, g4, w_ff_in, w_ff_out, tile):
    T = h2.shape[0]
    col_chunk = 2 * FF_SHARD

    def body(h2_ref, x1_ref, tg_ref, g4_ref, wi_ref, wo_ref, a_ref, dfo_ref, dx2_ref, loss_ref, dg4_ref, rr_ref):
        i = pl.program_id(0)
        h2v = h2_ref[...]
        for c in range(D_FF // col_chunk):
            cols = slice(c * col_chunk, (c + 1) * col_chunk)
            a = _dot_nt(h2v, wi_ref[cols, :])
            a_ref[:, cols] = a.astype(BF16)
            ra = jnp.maximum(a, 0.0)
            rr_ref[:, cols] = (ra * ra).astype(BF16)
        f = _dot(rr_ref[...], wo_ref[...])
        r = _rms_scale(f)
        g = g4_ref[...]
        err = x1_ref[...] + f * r * g - tg_ref[...]
        dx2 = err * (1.0 / D_MODEL)
        dx2_ref[...] = dx2
        dfo, dg = _rms_bwd(dx2, f, r, g)
        dfo_ref[...] = dfo.astype(BF16)
        row = lax.broadcasted_iota(jnp.int32, (SUBLANES, LANES), 0)
        col = lax.broadcasted_iota(jnp.int32, (SUBLANES, LANES), 1)
        loss = jnp.where((row == 0) & (col == 0), (0.5 / D_MODEL) * jnp.sum(err * err), 0.0)

        @pl.when(i == 0)
        def _():
            loss_ref[...] = loss
            dg4_ref[...] = dg

        @pl.when(i > 0)
        def _():
            loss_ref[...] += loss
            dg4_ref[...] += dg

    tok = pl.BlockSpec((tile, D_MODEL), lambda i: (i, 0))
    return pl.pallas_call(
        body, name="mlp_fwd", grid=(T // tile,),
        in_specs=[tok, tok, tok, _const_spec((1, D_MODEL)), _const_spec((D_FF, D_MODEL)), _const_spec((D_FF, D_MODEL))],
        out_specs=[pl.BlockSpec((tile, D_FF), lambda i: (i, 0)), tok, tok,
                   pl.BlockSpec((SUBLANES, LANES), lambda i: (0, 0)), pl.BlockSpec((1, D_MODEL), lambda i: (0, 0))],
        out_shape=_hbm_out([jax.ShapeDtypeStruct((T, D_FF), BF16), jax.ShapeDtypeStruct((T, D_MODEL), BF16),
                            jax.ShapeDtypeStruct((T, D_MODEL), F32), jax.ShapeDtypeStruct((SUBLANES, LANES), F32),
                            jax.ShapeDtypeStruct((1, D_MODEL), F32)]),
        scratch_shapes=[pltpu.VMEM((tile, D_FF), BF16)],
        compiler_params=_params(("arbitrary",), VMEM_MAX),
    )(*_in_hbm(h2, x1, target, g4, w_ff_in.reshape(D_FF, D_MODEL), w_ff_out.reshape(D_FF, D_MODEL)))


def _mlp_weight_grads(dfo, a, h2, w_ff_out, row_chunk):
    T = h2.shape[0]

    def body(dfo_ref, h2_ref, a_ref, wo_ref, dwi_ref, dwo_ref, da_ref, rr_ref):
        def rows(r, _):
            sl = pl.ds(pl.multiple_of(r * row_chunk, row_chunk), row_chunk)
            ra = jnp.maximum(a_ref[sl, :].astype(F32), 0.0)
            da_ref[sl, :] = (_dot_nt(dfo_ref[sl, :], wo_ref[0]) * (2.0 * ra)).astype(BF16)
            rr_ref[sl, :] = (ra * ra).astype(BF16)
            return 0

        lax.fori_loop(0, T // row_chunk, rows, 0)
        dwo_ref[0] = _dot_tn(rr_ref[...], dfo_ref[...])
        dwi_ref[0] = _dot_tn(h2_ref[...], da_ref[...])

    return pl.pallas_call(
        body, name="mlp_weight_grads", grid=(N_DEV,),
        in_specs=[_const_spec((T, D_MODEL)), _const_spec((T, D_MODEL)), pl.BlockSpec((T, FF_SHARD), lambda k: (0, k)),
                  pl.BlockSpec((1, FF_SHARD, D_MODEL), lambda k: (k, 0, 0))],
        out_specs=[pl.BlockSpec((1, D_MODEL, FF_SHARD), lambda k: (k, 0, 0)),
                   pl.BlockSpec((1, FF_SHARD, D_MODEL), lambda k: (k, 0, 0)), pl.BlockSpec((T, FF_SHARD), lambda k: (0, k))],
        out_shape=_hbm_out([jax.ShapeDtypeStruct((N_DEV, D_MODEL, FF_SHARD), F32),
                            jax.ShapeDtypeStruct((N_DEV, FF_SHARD, D_MODEL), F32), jax.ShapeDtypeStruct((T, D_FF), BF16)]),
        scratch_shapes=[pltpu.VMEM((T, FF_SHARD), BF16)],
        compiler_params=_params(("arbitrary",), VMEM_MAX),
    )(*_in_hbm(dfo, h2, a, w_ff_out))


def _mlp_input_grad(da, w_ff_in_t, tile):
    T = da.shape[0]

    def body(da_ref, w_ref, o_ref):
        o_ref[...] = _dot(da_ref[...], w_ref[...])

    return pl.pallas_call(
        body, name="mlp_input_grad", grid=(T // tile,),
        in_specs=[pl.BlockSpec((tile, D_FF), lambda i: (i, 0)), _const_spec((D_FF, D_MODEL))],
        out_specs=pl.BlockSpec((tile, D_MODEL), lambda i: (i, 0)),
        out_shape=_hbm_out(jax.ShapeDtypeStruct((T, D_MODEL), F32)),
        compiler_params=_params(("arbitrary",), VMEM_MID),
    )(*_in_hbm(da, w_ff_in_t))


def _block_diag_in(b):
    bt = b.reshape(N_SSM_BLOCKS, GROUPS_PER_BLOCK, GROUP_CH, N_STATE)
    eye = jnp.eye(GROUPS_PER_BLOCK, dtype=b.dtype)
    return jnp.einsum("jacp,ab->jacbp", bt, eye).reshape(N_SSM_BLOCKS, LANES, SSM_LANE_BLOCK)


def _block_diag_in_grad(g):
    g = g.reshape(N_SSM_BLOCKS, GROUPS_PER_BLOCK, GROUP_CH, GROUPS_PER_BLOCK, N_STATE)
    d = jnp.diagonal(g, axis1=1, axis2=3)
    return jnp.transpose(d, (0, 3, 1, 2)).reshape(N_GROUPS, GROUP_CH, N_STATE)


def _block_diag_out(c):
    ct = c.reshape(N_SSM_BLOCKS, GROUPS_PER_BLOCK, GROUP_CH, N_STATE)
    eye = jnp.eye(GROUPS_PER_BLOCK, dtype=c.dtype)
    return jnp.einsum("jacp,ab->japbc", ct, eye).reshape(N_SSM_BLOCKS, SSM_LANE_BLOCK, LANES)


def _block_diag_out_grad(g):
    g = g.reshape(N_SSM_BLOCKS, GROUPS_PER_BLOCK, N_STATE, GROUPS_PER_BLOCK, GROUP_CH)
    d = jnp.diagonal(g, axis1=1, axis2=3)
    return jnp.transpose(d, (0, 3, 2, 1)).reshape(N_GROUPS, GROUP_CH, N_STATE)


def _tiles(T):
    return dict(proj=min(512, T), proj_bwd=min(512, T // 2), merge=min(512, T), merge_bwd=min(256, T),
                mlp_fwd=min(512, T), mlp_bwd=min(512, T), ssm_chunk=min(1024, T))


def _mesh_position():
    x, y, c = lax.axis_index("x"), lax.axis_index("y"), lax.axis_index("c")
    other_chips = [(1 - x, y), (x, 1 - y), (1 - x, 1 - y)]
    return x, y, c, other_chips


def _gather_carry(arrays):
    n = len(arrays)

    def copies(ins, outs, sems):
        send_sems, recv_sems, local_sems = sems
        x, y, c, chips = _mesh_position()
        me, sibling = (x, y, c), (x, y, 1 - c)

        def copy(a, k, block, to, src=None):
            px, py, pc = block
            dst = outs[a].at[4 * px + 2 * py + pc]
            return pltpu.make_async_remote_copy(
                src_ref=dst if src is None else src, dst_ref=dst, send_sem=send_sems.at[7 * a + k],
                recv_sem=recv_sems.at[7 * a + k], device_id=to, device_id_type=MESH_IDS)

        mine = [pltpu.make_async_copy(ins[a], outs[a].at[4 * x + 2 * y + c], local_sems.at[a]) for a in range(n)]
        first = []
        for a in range(n):
            first.append(copy(a, 0, me, sibling, src=ins[a]))
            first += [copy(a, 1 + j, me, (*chip, c), src=ins[a]) for j, chip in enumerate(chips)]
        return copy, mine, first, me, sibling, chips, c

    def start(ins, outs, sems):
        _, mine, first, *_ = copies(ins, outs, sems)
        for cp in mine + first:
            cp.start()

    def passed_on(copy, sibling, chips, c):
        return [copy(a, 4 + j, (*chip, c), sibling) for a in range(n) for j, chip in enumerate(chips)]

    def middle(ins, outs, sems):
        copy, _, _, me, sibling, chips, c = copies(ins, outs, sems)
        for a in range(n):
            for j, chip in enumerate(chips):
                copy(a, 1 + j, (*chip, c), me).wait_recv()
                copy(a, 4 + j, (*chip, c), sibling).start()

    def finish(ins, outs, sems):
        copy, mine, first, me, sibling, chips, c = copies(ins, outs, sems)
        for a in range(n):
            copy(a, 0, sibling, me).wait_recv()
            for j, chip in enumerate(chips):
                copy(a, 4 + j, (*chip, 1 - c), me).wait_recv()
        for cp in first + passed_on(copy, sibling, chips, c):
            cp.wait_send()
        for cp in mine:
            cp.wait()

    return _Carry(arrays, [jax.ShapeDtypeStruct((N_DEV,) + a.shape, a.dtype) for a in arrays],
                  [pltpu.SemaphoreType.DMA((7 * n,)), pltpu.SemaphoreType.DMA((7 * n,)), pltpu.SemaphoreType.DMA((n,))],
                  start, finish, middle)


def _pairwise_carry(arrays, n_slots, make_copies):
    n = len(arrays)

    def start(ins, outs, sems):
        for cp in make_copies(ins, outs, sems):
            cp.start()

    def finish(ins, outs, sems):
        for cp in make_copies(ins, outs, sems):
            cp.wait()

    return _Carry(arrays, [jax.ShapeDtypeStruct((n_slots,) + a.shape[1:], a.dtype) for a in arrays],
                  [pltpu.SemaphoreType.DMA((n_slots * n,)), pltpu.SemaphoreType.DMA((n_slots * n,))], start, finish)


def _sibling_carry(grads):
    def make_copies(ins, outs, sems):
        x, y, c, _ = _mesh_position()
        return [pltpu.make_async_remote_copy(
            src_ref=ins[a].at[2 * ch + (1 - c)], dst_ref=outs[a].at[ch], send_sem=sems[0].at[4 * a + ch],
            recv_sem=sems[1].at[4 * a + ch], device_id=(x, y, 1 - c), device_id_type=MESH_IDS)
            for a in range(len(grads)) for ch in range(4)]

    return _pairwise_carry(grads, 4, make_copies)


def _chips_carry(sums):
    def make_copies(ins, outs, sems):
        x, y, c, chips = _mesh_position()
        return [pltpu.make_async_remote_copy(
            src_ref=ins[a].at[2 * px + py], dst_ref=outs[a].at[j], send_sem=sems[0].at[3 * a + j],
            recv_sem=sems[1].at[3 * a + j], device_id=(px, py, c), device_id_type=MESH_IDS)
            for a in range(len(sums)) for j, (px, py) in enumerate(chips)]

    return _pairwise_carry(sums, 3, make_copies)


def _everyone_carry(arrays):
    def make_copies(ins, outs, sems):
        x, y, c, _ = _mesh_position()
        flip = lambda v, bit: 1 - v if bit else v
        return [pltpu.make_async_remote_copy(
            src_ref=ins[a], dst_ref=outs[a].at[r - 1], send_sem=sems[0].at[7 * a + r - 1], recv_sem=sems[1].at[7 * a + r - 1],
            device_id=(flip(x, r & 4), flip(y, r & 2), flip(c, r & 1)), device_id_type=MESH_IDS)
            for a in range(len(arrays)) for r in range(1, N_DEV)]

    carry = _pairwise_carry([jax.ShapeDtypeStruct((1,) + a.shape, a.dtype) for a in arrays], N_DEV - 1, make_copies)
    carry.inputs = list(arrays)
    return carry


def _sum_everyone(own, received, me, name, after=()):
    def body(me_ref, own_ref, r_ref, *refs):
        g = None
        for d in range(N_DEV):
            relation = jnp.bitwise_xor(d, me_ref[0])
            part = jnp.where(relation == 0, own_ref[...], r_ref[jnp.maximum(relation - 1, 0)])
            g = part if g is None else g + part
        refs[-1][...] = g

    whole = lambda shape: pl.BlockSpec(shape, lambda i, me_ref: (0,) * len(shape))
    return pl.pallas_call(
        body, name=name,
        grid_spec=pltpu.PrefetchScalarGridSpec(
            num_scalar_prefetch=1, grid=(1,), in_specs=[whole(own.shape), whole(received.shape)] + [HBM_SPEC] * len(after),
            out_specs=whole(own.shape)),
        out_shape=jax.ShapeDtypeStruct(own.shape, F32))(me, *_in_hbm(own, received), *after)


SEM_SPEC = pl.BlockSpec(memory_space=pltpu.SEMAPHORE)
DATAFLOW_EFFECT = pltpu.SideEffectType.DATAFLOW_SIDE_EFFECTING


def _exchange_start(carry, name, after=()):
    n = len(carry.inputs)
    lands = [lax.empty(s.shape, s.dtype) for s in carry.out_shapes]

    def body(*refs):
        first_out = 2 * n + len(after)
        srcs, zones, sems, token = refs[:n], refs[n:2 * n], refs[first_out:first_out + 2], refs[-1]
        carry.start(srcs, zones, sems)
        token[...] = jnp.zeros_like(token)

    outs = pl.pallas_call(
        body, name=name, in_specs=[HBM_SPEC] * (2 * n + len(after)),
        out_specs=[SEM_SPEC, SEM_SPEC] + [HBM_SPEC] * (2 * n) + [pl.BlockSpec(memory_space=pltpu.VMEM)],
        out_shape=list(carry.sems) + _hbm_out([jax.ShapeDtypeStruct(a.shape, a.dtype) for a in carry.inputs])
        + _hbm_out(carry.out_shapes) + [jax.ShapeDtypeStruct((SUBLANES, LANES), F32)],
        input_output_aliases={j: 2 + j for j in range(2 * n)},
        compiler_params=pltpu.CompilerParams(has_side_effects=DATAFLOW_EFFECT),
    )(*_in_hbm(*carry.inputs, *lands), *after)
    return outs[:-1], outs[-1]


def _exchange_wait(carry, in_flight, after, name):
    n = len(carry.inputs)
    sems, srcs, zones = in_flight[:2], in_flight[2:2 + n], in_flight[2 + n:]

    def body(*refs):
        src_refs, zone_refs, sem_refs = refs[:n], refs[n:2 * n], refs[2 * n:2 * n + 2]
        carry.finish(src_refs, zone_refs, sem_refs)

    outs = pl.pallas_call(
        body, name=name, in_specs=[HBM_SPEC] * (2 * n) + [SEM_SPEC, SEM_SPEC] + [HBM_SPEC] * len(after),
        out_specs=[HBM_SPEC] * (2 * n),
        out_shape=_hbm_out([jax.ShapeDtypeStruct(a.shape, a.dtype) for a in carry.inputs]) + _hbm_out(carry.out_shapes),
        input_output_aliases={j: j for j in range(2 * n)},
        compiler_params=pltpu.CompilerParams(has_side_effects=DATAFLOW_EFFECT),
    )(*srcs, *zones, *sems, *after)
    return list(outs[:n]), list(outs[n:])


def _add_sibling(grads8, recvs, core, row_tiles, name):
    k = len(grads8)
    g4 = [g.reshape(4, 2, *g.shape[1:]) for g in grads8]

    def body(core_ref, *refs):
        g_refs, r_refs, o_refs, ob_refs = (refs[j * k:(j + 1) * k] for j in range(4))
        for g_ref, r_ref, o_ref, ob_ref in zip(g_refs, r_refs, o_refs, ob_refs):
            s = g_ref[0] + r_ref[...]
            o_ref[...] = s
            ob_ref[...] = s.astype(BF16)

    def blocks(make):
        return [make(g.shape[1] // row_tiles, g.shape[2]) for g in grads8]

    slot = lambda tr, C: pl.BlockSpec((1, tr, C), lambda ch, r, core_ref: (ch, r, 0))
    outs = pl.pallas_call(
        body, name=name,
        grid_spec=pltpu.PrefetchScalarGridSpec(
            num_scalar_prefetch=1, grid=(4, row_tiles),
            in_specs=blocks(lambda tr, C: pl.BlockSpec((1, 1, tr, C), lambda ch, r, core_ref: (ch, core_ref[0], r, 0)))
            + blocks(slot), out_specs=blocks(slot) + blocks(slot)),
        out_shape=_hbm_out([jax.ShapeDtypeStruct((4,) + g.shape[1:], F32) for g in grads8]
                           + [jax.ShapeDtypeStruct((4,) + g.shape[1:], BF16) for g in grads8]),
        compiler_params=_params(("arbitrary", "arbitrary")),
    )(core, *_in_hbm(*g4, *recvs))
    return list(outs[:k]), list(outs[k:])


def _adam_math(w, g, m, v):
    m = ADAM_B1 * m + (1.0 - ADAM_B1) * g
    v = ADAM_B2 * v + (1.0 - ADAM_B2) * jnp.square(g)
    m_hat = m / (1.0 - ADAM_B1 ** ADAM_STEP)
    v_hat = v / (1.0 - ADAM_B2 ** ADAM_STEP)
    delta = -ADAM_LR * (m_hat / (jnp.sqrt(v_hat) + ADAM_EPS) + ADAM_WD * w)
    return delta, m, v


def _adam_big(ws, ms, vs, chip_sums, recvs, chip, row_tiles, name, after=()):
    k = len(ws)

    def body(chip_ref, *refs):
        refs = refs[:5 * k] + refs[5 * k + len(after):]
        w_refs, m_refs, v_refs, s_refs, r_refs, g_refs, d_refs, nm_refs, nv_refs = (refs[j * k:(j + 1) * k] for j in range(9))
        for a in range(k):
            r_ref = r_refs[a]
            g = s_refs[a][0] + r_ref[0].astype(F32) + r_ref[1].astype(F32) + r_ref[2].astype(F32)
            g_refs[a][...] = g
            d_refs[a][...], nm_refs[a][...], nv_refs[a][...] = _adam_math(w_refs[a][...], g, m_refs[a][...], v_refs[a][...])

    def blocks(make):
        return [make(w.shape[0] // row_tiles, w.shape[1]) for w in ws]

    blk = lambda tr, C: pl.BlockSpec((tr, C), lambda r, chip_ref: (r, 0))
    outs = pl.pallas_call(
        body, name=name,
        grid_spec=pltpu.PrefetchScalarGridSpec(
            num_scalar_prefetch=1, grid=(row_tiles,),
            in_specs=blocks(blk) * 3 + blocks(lambda tr, C: pl.BlockSpec((1, tr, C), lambda r, chip_ref: (chip_ref[0], r, 0)))
            + blocks(lambda tr, C: pl.BlockSpec((3, tr, C), lambda r, chip_ref: (0, r, 0))) + [HBM_SPEC] * len(after),
            out_specs=blocks(blk) * 4),
        out_shape=[jax.ShapeDtypeStruct(w.shape, F32) for w in ws] * 4,
        compiler_params=_params(("arbitrary",)),
    )(chip, *_in_hbm(*ws, *ms, *vs, *chip_sums, *recvs), *after)
    return [list(outs[j * k:(j + 1) * k]) for j in range(4)]


def _sum_partials(partials, name, after=()):
    def body(p_ref, *refs):
        g = p_ref[0]
        for d in range(1, partials.shape[0]):
            g = g + p_ref[d]
        refs[-1][...] = g

    return pl.pallas_call(body, name=name, grid=(1,), in_specs=[_whole(partials.shape)] + [HBM_SPEC] * len(after),
                          out_specs=_whole(partials.shape[1:]),
                          out_shape=jax.ShapeDtypeStruct(partials.shape[1:], F32))(*_in_hbm(partials), *after)


def _adam_small(ws, ms, vs, gs):
    n = len(ws)

    def body(*refs):
        w_refs, m_refs, v_refs, g_refs = (refs[i * n:(i + 1) * n] for i in range(4))
        d_refs, nm_refs, nv_refs = (refs[(4 + i) * n:(5 + i) * n] for i in range(3))
        for j in range(n):
            d_refs[j][...], nm_refs[j][...], nv_refs[j][...] = _adam_math(
                w_refs[j][...], g_refs[j][...], m_refs[j][...], v_refs[j][...])

    specs = [_whole(w.shape) for w in ws]
    outs = pl.pallas_call(body, name="adam_small", grid=(1,), in_specs=specs * 4, out_specs=specs * 3,
                          out_shape=[jax.ShapeDtypeStruct(w.shape, F32) for w in ws] * 3,
                          compiler_params=_params(("arbitrary",), VMEM_MID))(*_in_hbm(*ws, *ms, *vs, *gs))
    return outs[:n], outs[n:2 * n], outs[2 * n:]


PACK_QUANTUM = SUBLANES * LANES


def _pack(named, names):
    parts = []
    for nme in names:
        flat = named[nme].reshape(-1)
        parts.append(jnp.pad(flat, (0, -flat.size % PACK_QUANTUM)))
    return jnp.concatenate(parts).reshape(-1, LANES)


def _unpack(packed, shapes, names):
    flat = packed.reshape(-1)
    out, pos = {}, 0
    for nme in names:
        size = math.prod(shapes[nme])
        out[nme] = flat[pos:pos + size].reshape(shapes[nme])
        pos += size + (-size % PACK_QUANTUM)
    return out


BIG = ("w_in", "w_glu", "w_attn_branch", "w_ssm_branch", "w_out", "w_ff_in", "w_ff_out")
COLUMN_SHARDED = ("w_in", "w_attn_branch", "w_ssm_branch", "w_ff_in")
SMALL = ("norm_mix_pre", "norm_mix_post", "norm_mlp_pre", "norm_mlp_post", "rel_bias", "sinks", "lam_re", "lam_im",
         "log_dt", "b_re", "b_im", "c_re", "c_im", "d_skip")
SWAPPED_SMALL = ("rel_bias", "b_re", "b_im")
SMALL_LATE = ("norm_mix_pre", "rel_bias", "sinks", "loss")
SMALL_BEFORE_ATTN_BWD = tuple(n for n in SMALL if n not in SMALL_LATE)
ALL_WEIGHTS = ("norm_mix_pre", "norm_mix_post", "norm_mlp_pre", "norm_mlp_post", "w_in", "rel_bias", "sinks", "lam_re",
               "lam_im", "log_dt", "b_re", "b_im", "c_re", "c_im", "d_skip", "w_glu", "w_attn_branch", "w_ssm_branch",
               "w_out", "w_ff_in", "w_ff_out")


def _full_from_gathered(name, gathered):
    _, r, c = gathered.shape
    if name in COLUMN_SHARDED:
        return jnp.transpose(gathered, (1, 0, 2)).reshape(r, N_DEV * c)
    return gathered.reshape(N_DEV * r, c)


def _blocks_from_full(name, full):
    r, c = full.shape
    if name in COLUMN_SHARDED:
        return jnp.transpose(full.reshape(r, N_DEV, c // N_DEV), (1, 0, 2))
    return full.reshape(N_DEV, r // N_DEV, c)


def kernel(x, norm_mix_pre, norm_mix_post, norm_mlp_pre, norm_mlp_post, w_in, rel_bias, sinks, lam_re, lam_im, log_dt, b_re, b_im, c_re, c_im, d_skip, w_glu, w_attn_branch, w_ssm_branch, w_out, w_ff_in, w_ff_out, loss_target, m_norm_mix_pre, m_norm_mix_post, m_norm_mlp_pre, m_norm_mlp_post, m_w_in, m_rel_bias, m_sinks, m_lam_re, m_lam_im, m_log_dt, m_b_re, m_b_im, m_c_re, m_c_im, m_d_skip, m_w_glu, m_w_attn_branch, m_w_ssm_branch, m_w_out, m_w_ff_in, m_w_ff_out, v_norm_mix_pre, v_norm_mix_post, v_norm_mlp_pre, v_norm_mlp_post, v_w_in, v_rel_bias, v_sinks, v_lam_re, v_lam_im, v_log_dt, v_b_re, v_b_im, v_c_re, v_c_im, v_d_skip, v_w_glu, v_w_attn_branch, v_w_ssm_branch, v_w_out, v_w_ff_in, v_w_ff_out):
    args = dict(locals())
    w = {n: args[n] for n in ALL_WEIGHTS}
    m = {n: args["m_" + n] for n in ALL_WEIGHTS}
    v = {n: args["v_" + n] for n in ALL_WEIGHTS}
    core = lax.axis_index("c").astype(jnp.int32).reshape(1)
    chip = (2 * lax.axis_index("x") + lax.axis_index("y")).astype(jnp.int32).reshape(1)
    xs, target = x[0], loss_target[0]
    t = _tiles(xs.shape[0])
    local = lambda d, n: d[n][0].T if n == "w_in" else d[n][0]
    shard = {n: local(w, n).astype(BF16) for n in BIG}
    shard["w_ff_in"] = shard["w_ff_in"].T
    view = lambda n, a: jnp.swapaxes(a, -1, -2) if n in SWAPPED_SMALL else a
    small = {n: (view(n, w[n]) if n == "rel_bias" else view(n, w[n])[0]) for n in SMALL}
    g1, g2, g3, g4 = (small[n].reshape(1, D_MODEL) for n in ("norm_mix_pre", "norm_mix_post", "norm_mlp_pre", "norm_mlp_post"))
    bucket = jnp.asarray(_bucket_table())
    rel_b, sink = small["rel_bias"], small["sinks"].reshape(1, N_HEADS)
    lam_r, lam_i = small["lam_re"].reshape(1, STATES), small["lam_im"].reshape(1, STATES)
    ldt_rep = jnp.repeat(small["log_dt"].reshape(N_GROUPS), N_STATE).reshape(1, STATES)
    bd_re, bd_im = _block_diag_in(small["b_re"]), _block_diag_in(small["b_im"])
    cm_re, cm_im = _block_diag_out(small["c_re"]).astype(BF16), _block_diag_out(small["c_im"]).astype(BF16)
    dsk = small["d_skip"].reshape(1, SSM_W)

    (g_in,) = _run_carry(_gather_carry([shard["w_in"]]), "gather_w_in")
    wf_in = g_in.reshape(IN_W, D_MODEL)
    merge_names = ("w_glu", "w_attn_branch", "w_ssm_branch", "w_out")
    (q, k, vv, u, ga, gs, h), gathered = _in_proj_fwd(xs, g1, wf_in, t["proj"], _gather_carry([shard[n] for n in merge_names]))
    wf = {n: _full_from_gathered(n, g) for n, g in zip(merge_names, gathered)}
    (att,), (wf_ff_in,) = _attn_fwd(q, k, vv, bucket, rel_b, sink, _gather_carry([shard["w_ff_in"]]))
    a_re, a_im, bm_re, bm_im = _ssm_prep(lam_r, lam_i, ldt_rep, bd_re, bd_im)
    (y, h_re, h_im, in_re, in_im), (wf_ff_out,) = _ssm_fwd(
        u, a_re, a_im, bm_re, bm_im, cm_re, cm_im, dsk, t["ssm_chunk"], _gather_carry([shard["w_ff_out"]]))
    x1, o, h2 = _merge_fwd(xs, y, att, ga, gs, g2, g3, wf["w_glu"], wf["w_ssm_branch"], wf["w_attn_branch"], wf["w_out"],
                           t["merge"])
    a, dfo, dx2, loss_blk, dg4 = _mlp_fwd(h2, x1, target, g4, wf_ff_in, wf_ff_out, t["mlp_fwd"])

    groups = {"ff": 4, "merge": 1, "w_in": 2}

    def add_sibling(group, blocks, received):
        return _add_sibling(blocks, received, core, groups[group], "add_sibling_" + group)

    ff_names = ("w_ff_in", "w_ff_out")
    dw_ff_in, dw_ff_out, da = _mlp_weight_grads(dfo, a, h2, wf_ff_out, t["mlp_bwd"])
    dh2 = _mlp_input_grad(da, wf_ff_in.reshape(D_FF, D_MODEL), t["mlp_bwd"])
    ff_blocks = [dw_ff_in, dw_ff_out]
    (dx1, dgates, datt, dy, dw_glu, dw_ssm, dw_attn, dw_out, dg2, dg3), ff_recv = _merge_bwd(
        dh2, dx2, x1, o, y, att, ga, gs, g2, g3, wf["w_glu"], wf["w_ssm_branch"], wf["w_attn_branch"], wf["w_out"],
        t["merge_bwd"], _sibling_carry(ff_blocks))
    ff_sums, ff_sums_bf = add_sibling("ff", ff_blocks, ff_recv)
    merge_blocks = [_blocks_from_full(n, g) for n, g in zip(merge_names, (dw_glu, dw_attn, dw_ssm, dw_out))]
    (du, dbm_re, dbm_im, dcm_re, dcm_im, da_re, da_im, dd_skip), carried = _ssm_bwd(
        dy, u, h_re, h_im, in_re, in_im, a_re, a_im, bm_re, bm_im, cm_re, cm_im, dsk, t["ssm_chunk"],
        _join(_chips_carry(ff_sums_bf), _sibling_carry(merge_blocks)))
    ff_from_chips, merge_recv = carried[:2], carried[2:]
    merge_sums, merge_sums_bf = add_sibling("merge", merge_blocks, merge_recv)
    dbd_re, dbd_im, dlam_re, dlam_im, dldt_rep = _ssm_prep_bwd(lam_r, lam_i, ldt_rep, bd_re, bd_im, dbm_re, dbm_im, da_re, da_im)
    dlog_dt = _group_sum(dldt_rep.reshape(N_GROUPS, N_STATE))
    shapes = {n: view(n, w[n]).shape for n in SMALL}
    shapes["loss"] = (1,)
    small_grads = dict(
        norm_mix_post=dg2, norm_mlp_pre=dg3, norm_mlp_post=dg4, lam_re=dlam_re, lam_im=dlam_im, log_dt=dlog_dt,
        b_re=_block_diag_in_grad(dbd_re), b_im=_block_diag_in_grad(dbd_im),
        c_re=_block_diag_out_grad(dcm_re), c_im=_block_diag_out_grad(dcm_im), d_skip=dd_skip)
    packed_early = _pack({n: small_grads[n].reshape(shapes[n]) for n in SMALL_BEFORE_ATTN_BWD}, SMALL_BEFORE_ATTN_BWD)
    (dq, dkv, attn_small), carried = _attn_bwd(
        q, k, vv, datt, bucket, rel_b, sink, _join(_chips_carry(merge_sums_bf), _gather_carry([packed_early])))
    merge_from_chips, partials_early = carried[:-1], carried[-1]

    dparts = (dq, dkv, du, dgates)
    dw_in_t = _in_proj_weight_grad(h, dparts)
    in_blocks = [dw_in_t.reshape(N_DEV, IN_W // N_DEV, D_MODEL)]
    to_sibling = _sibling_carry(in_blocks)
    in_flight, token = _exchange_start(to_sibling, "w_in_sibling_start")
    n_tiles = xs.shape[0] // t["proj_bwd"]
    (grad_x, dg1), _ = _in_proj_input_grad(xs, g1 + token[0:1, 0:1], wf_in, dx1, dparts, t["proj_bwd"], 0, n_tiles, "in_proj_input_grad")
    late = dict(norm_mix_pre=dg1, rel_bias=attn_small[:, :N_BUCKETS, 0], sinks=attn_small[:, N_BUCKETS, 0], loss=loss_blk[0:1, 0])
    packed_late = _pack({n: late[n].reshape(shapes[n]) for n in SMALL_LATE}, SMALL_LATE)
    to_everyone = _everyone_carry([packed_late])
    late_in_flight, late_started = _exchange_start(to_everyone, "late_grads_start")
    in_blocks, in_recv = _exchange_wait(to_sibling, in_flight, [late_started], "w_in_sibling_wait")
    in_sums, in_sums_bf = add_sibling("w_in", in_blocks, in_recv)
    to_chips = _chips_carry(in_sums_bf)
    in_flight, chips_started = _exchange_start(to_chips, "w_in_chips_start")
    (packed_late,), (late_received,) = _exchange_wait(to_everyone, late_in_flight, [chips_started], "late_grads_wait")

    grads, deltas, new_m, new_v = {}, {}, {}, {}

    def adam_group(group, names, sums, received, after=()):
        outs = _adam_big(*[[local(d, n) for n in names] for d in (w, m, v)], sums, received, chip, groups[group],
                         "adam_" + group, after)
        for store, vals in zip((grads, deltas, new_m, new_v), outs):
            store.update({n: (o.T if n == "w_in" else o)[None] for n, o in zip(names, vals)})

    adam_group("ff", ff_names, ff_sums, ff_from_chips, [chips_started])
    adam_group("merge", merge_names, merge_sums, merge_from_chips, [chips_started])

    grads.update(_unpack(_sum_partials(partials_early, "sum_small_grads", [chips_started]), shapes, SMALL_BEFORE_ATTN_BWD))
    grads.update(_unpack(_sum_everyone(packed_late, late_received, 2 * chip + core, "sum_late_grads"), shapes, SMALL_LATE))
    loss = grads.pop("loss").reshape(())
    small_out = _adam_small(*[[view(n, d[n]) for n in SMALL] for d in (w, m, v)], [grads[n] for n in SMALL])
    for store, vals in zip((deltas, new_m, new_v), small_out):
        store.update(zip(SMALL, vals))
    for store in (grads, deltas, new_m, new_v):
        store.update({n: view(n, store[n]) for n in SWAPPED_SMALL})

    busy = [new_v["w_ff_out"], new_v["w_out"], deltas["norm_mix_pre"]]
    _, (in_from_chips,) = _exchange_wait(to_chips, in_flight, busy, "w_in_chips_wait")
    adam_group("w_in", ("w_in",), in_sums, [in_from_chips])

    return (loss, grad_x[None], *[grads[n] for n in ALL_WEIGHTS], *[deltas[n] for n in ALL_WEIGHTS],
            *[new_m[n] for n in ALL_WEIGHTS], *[new_v[n] for n in ALL_WEIGHTS])
```

```python
import math

import jax
import jax.numpy as jnp
import numpy as np
from jax import lax
from jax.experimental import pallas as pl
from jax.experimental.pallas import tpu as pltpu

F32 = jnp.float32
BF16 = jnp.bfloat16

D_MODEL = 1024
N_HEADS = 8
HEAD_DIM = 64
ATTN_W = 512
KV_W = 128
BLOCK = 128
N_BUCKETS = 32
SSM_W = 512
N_GROUPS = 32
N_STATE = 64
GROUP_CH = 16
STATES = N_GROUPS * N_STATE
D_FF = 4096
IN_W = 3328
SPLITS = (0, 512, 640, 768, 1280, 2304, 3328)
RMS_EPS = 1e-6
NEG_INF = -1e30
SUBLANES = 8
LANES = 128
SSM_LANE_BLOCK = 512
N_SSM_BLOCKS = STATES // SSM_LANE_BLOCK
GROUPS_PER_BLOCK = SSM_LANE_BLOCK // N_STATE
VMEM_BIG = 52 * 1024 * 1024
VMEM_MID = 40 * 1024 * 1024
VMEM_MAX = 60 * 1024 * 1024

ADAM_LR = 0.001
ADAM_B1 = 0.9
ADAM_B2 = 0.999
ADAM_EPS = 1e-08
ADAM_WD = 0.01
ADAM_STEP = 10

N_DEV = 8


def _dot(a, b):
    return jnp.dot(a, b, preferred_element_type=F32)


def _dot_nt(a, b):
    return lax.dot_general(a, b, (((1,), (1,)), ((), ())), preferred_element_type=F32)


def _dot_tn(a, b):
    return lax.dot_general(a, b, (((0,), (0,)), ((), ())), preferred_element_type=F32)


def _rms_scale(x):
    return lax.rsqrt(jnp.mean(x * x, axis=-1, keepdims=True) + RMS_EPS)


def _rms_bwd(dy, x, r, g):
    t = dy * g
    dx = r * t - x * (r * r * r) * jnp.mean(t * x, axis=-1, keepdims=True)
    dg = jnp.sum(dy * x * r, axis=0, keepdims=True)
    return dx, dg


def _const_spec(shape):
    nd = len(shape)
    return pl.BlockSpec(shape, lambda *_: (0,) * nd, pipeline_mode=pl.Buffered(1))


def _in_hbm(*arrays):
    return tuple(pltpu.with_memory_space_constraint(a, pltpu.HBM) for a in arrays)


def _hbm_out(shapes):
    if isinstance(shapes, (list, tuple)):
        return [_hbm_out(s) for s in shapes]
    return shapes if isinstance(shapes, pl.MemoryRef) else pltpu.HBM(shapes.shape, shapes.dtype)


def _whole(shape):
    nd = len(shape)
    return pl.BlockSpec(shape, lambda *_: (0,) * nd)


def _params(sem, vmem=None):
    return pltpu.CompilerParams(dimension_semantics=sem, vmem_limit_bytes=vmem)


MESH_IDS = pl.DeviceIdType.MESH
HBM_SPEC = pl.BlockSpec(memory_space=pl.ANY)


class _Carry:
    def __init__(self, inputs, out_shapes, sems, start, finish, middle=None):
        self.inputs, self.out_shapes, self.sems = list(inputs), list(out_shapes), list(sems)
        self.start, self.middle, self.finish = start, middle, finish


def _join(a, b):
    na_in, na_out, na_sem = len(a.inputs), len(a.out_shapes), len(a.sems)

    def both(phase):
        def run(ins, outs, sems):
            for carry, lo in ((a, True), (b, False)):
                part = (lambda seq, n: seq[:n] if lo else seq[n:])
                if getattr(carry, phase) is not None:
                    getattr(carry, phase)(part(ins, na_in), part(outs, na_out), part(sems, na_sem))
        return run

    middle = both("middle") if (a.middle or b.middle) else None
    return _Carry(a.inputs + b.inputs, a.out_shapes + b.out_shapes, a.sems + b.sems, both("start"), both("finish"), middle)


def _hosted_call(body, carry, edge, *, name, grid, in_specs, out_specs, out_shape, scratch_shapes, compiler_params, inputs):
    n_in, n_out = len(in_specs), len(out_specs)
    inputs = [a if s.memory_space == pltpu.SMEM else _in_hbm(a)[0] for a, s in zip(inputs, in_specs)]
    out_shape = _hbm_out(list(out_shape))
    if carry is None:
        outs = pl.pallas_call(body, name=name, grid=grid, in_specs=in_specs, out_specs=out_specs, out_shape=out_shape,
                              scratch_shapes=scratch_shapes, compiler_params=compiler_params)(*inputs)
        return list(outs), []
    c_in, c_out, c_sem = len(carry.inputs), len(carry.out_shapes), len(carry.sems)

    def wrapped(*refs):
        ins, refs = refs[:n_in], refs[n_in:]
        cins, refs = refs[:c_in], refs[c_in:]
        outs, refs = refs[:n_out], refs[n_out:]
        couts, refs = refs[:c_out], refs[c_out:]
        scratch, csems = refs[:len(refs) - c_sem], refs[len(refs) - c_sem:]
        first, middle, last = edge()

        @pl.when(first)
        def _():
            carry.start(cins, couts, csems)

        body(*ins, *outs, *scratch)

        if carry.middle is not None:
            @pl.when(middle)
            def _():
                carry.middle(cins, couts, csems)

        @pl.when(last)
        def _():
            carry.finish(cins, couts, csems)

    outs = pl.pallas_call(
        wrapped, name=name, grid=grid, in_specs=list(in_specs) + [HBM_SPEC] * c_in,
        out_specs=list(out_specs) + [HBM_SPEC] * c_out, out_shape=out_shape + _hbm_out(carry.out_shapes),
        scratch_shapes=list(scratch_shapes) + carry.sems, compiler_params=compiler_params)(*inputs, *_in_hbm(*carry.inputs))
    return list(outs[:n_out]), list(outs[n_out:])


def _pass_on_step(n_steps):
    return max(0, min((7 * n_steps) // 8, n_steps - 2))


def _edge_1d(n_steps, pass_on_last=False):
    middle = n_steps - 1 if pass_on_last else _pass_on_step(n_steps)
    return lambda: (pl.program_id(0) == 0, pl.program_id(0) == middle, pl.program_id(0) == n_steps - 1)


def _edge_2d(n0, n1):
    def edge():
        step = pl.program_id(0) * n1 + pl.program_id(1)
        return step == 0, step == _pass_on_step(n0 * n1), step == n0 * n1 - 1
    return edge


def _run_carry(carry, name):
    c_in, c_out = len(carry.inputs), len(carry.out_shapes)

    def body(*refs):
        ins, outs, sems = refs[:c_in], refs[c_in:c_in + c_out], refs[c_in + c_out:]
        carry.start(ins, outs, sems)
        if carry.middle is not None:
            carry.middle(ins, outs, sems)
        carry.finish(ins, outs, sems)

    return pl.pallas_call(body, name=name, in_specs=[HBM_SPEC] * c_in, out_specs=[HBM_SPEC] * c_out,
                          out_shape=_hbm_out(carry.out_shapes), scratch_shapes=carry.sems)(*_in_hbm(*carry.inputs))


def _in_proj_fwd(x, g1, w_in_t, tile, carry=None):
    T = x.shape[0]

    def body(x_ref, g_ref, w_ref, q_ref, k_ref, v_ref, u_ref, ga_ref, gs_ref, h_ref):
        xv = x_ref[...]
        h = (xv * _rms_scale(xv) * g_ref[...]).astype(BF16)
        h_ref[...] = h
        outs = (q_ref, k_ref, v_ref, u_ref, ga_ref, gs_ref)
        for p, o_ref in enumerate(outs):
            o_ref[...] = _dot_nt(h, w_ref[SPLITS[p]:SPLITS[p + 1], :]).astype(o_ref.dtype)

    widths = [SPLITS[p + 1] - SPLITS[p] for p in range(6)] + [D_MODEL]
    dtypes = [BF16, BF16, BF16, F32, F32, F32, BF16]
    return _hosted_call(
        body, carry, _edge_1d(T // tile), name="in_proj_fwd", grid=(T // tile,),
        in_specs=[pl.BlockSpec((tile, D_MODEL), lambda i: (i, 0)), _const_spec((1, D_MODEL)), _const_spec((IN_W, D_MODEL))],
        out_specs=[pl.BlockSpec((tile, w), lambda i: (i, 0)) for w in widths],
        out_shape=[jax.ShapeDtypeStruct((T, w), dt) for w, dt in zip(widths, dtypes)],
        scratch_shapes=[], compiler_params=_params(("arbitrary",), VMEM_MID), inputs=(x, g1, w_in_t))


PROJ_PARTS = (512, 256, 512, 2048)
PROJ_GRAD_BLOCK = 256


def _in_proj_weight_grad(h, dparts):
    T = h.shape[0]
    blocks = [wd // PROJ_GRAD_BLOCK for wd in PROJ_PARTS]
    starts = [sum(blocks[:p]) for p in range(len(blocks))]

    def body(h_ref, *refs):
        part_refs, o_ref = refs[:-1], refs[-1]
        j = pl.program_id(0)
        for p_ref, start, count in zip(part_refs, starts, blocks):
            @pl.when((j >= start) & (j < start + count))
            def _(p_ref=p_ref):
                o_ref[...] = _dot_tn(p_ref[...], h_ref[...])

    def part_spec(start, count):
        return pl.BlockSpec((T, PROJ_GRAD_BLOCK), lambda j: (0, jnp.clip(j - start, 0, count - 1)))

    return pl.pallas_call(
        body, name="in_proj_weight_grad", grid=(sum(blocks),),
        in_specs=[_const_spec((T, D_MODEL))] + [part_spec(s, c) for s, c in zip(starts, blocks)],
        out_specs=pl.BlockSpec((PROJ_GRAD_BLOCK, D_MODEL), lambda j: (j, 0)),
        out_shape=_hbm_out(jax.ShapeDtypeStruct((IN_W, D_MODEL), F32)),
        compiler_params=_params(("arbitrary",), VMEM_MID),
    )(*_in_hbm(h, *dparts))


def _in_proj_input_grad(x, g1, w_in_t, dx1, dparts, tile, first_tile, n_tiles, name, carry=None):
    offsets = [sum(PROJ_PARTS[:p]) for p in range(len(PROJ_PARTS))]

    def body(x_ref, g_ref, w_ref, dx1_ref, *refs):
        part_refs, (gx_ref, dg_ref) = refs[:len(PROJ_PARTS)], refs[len(PROJ_PARTS):]
        i = pl.program_id(0)
        xv = x_ref[...]
        r = _rms_scale(xv)
        g = g_ref[...]
        dh = sum(_dot(p_ref[...], w_ref[off:off + wd, :]) for p_ref, off, wd in zip(part_refs, offsets, PROJ_PARTS))
        dxn, dg = _rms_bwd(dh, xv, r, g)
        gx_ref[...] = dx1_ref[...] + dxn

        @pl.when(i == 0)
        def _():
            dg_ref[...] = dg

        @pl.when(i > 0)
        def _():
            dg_ref[...] += dg

    tok = lambda wd: pl.BlockSpec((tile, wd), lambda i: (i + first_tile, 0))
    return _hosted_call(
        body, carry, _edge_1d(n_tiles), name=name, grid=(n_tiles,),
        in_specs=[tok(D_MODEL), _const_spec((1, D_MODEL)), _const_spec((IN_W, D_MODEL)), tok(D_MODEL)] + [tok(wd) for wd in PROJ_PARTS],
        out_specs=[pl.BlockSpec((tile, D_MODEL), lambda i: (i, 0)), pl.BlockSpec((1, D_MODEL), lambda i: (0, 0))],
        out_shape=[jax.ShapeDtypeStruct((n_tiles * tile, D_MODEL), F32), jax.ShapeDtypeStruct((1, D_MODEL), F32)],
        scratch_shapes=[], compiler_params=_params(("arbitrary",), VMEM_MID), inputs=(x, g1, w_in_t, dx1, *dparts))


def _bucket_table():
    qi = np.arange(BLOCK)[:, None]
    kj = np.arange(2 * BLOCK)[None, :]
    dist = qi + BLOCK - kj
    max_exact = N_BUCKETS // 2
    d = np.maximum(dist, 0)
    df = np.maximum(d, 1).astype(np.float32)
    large = max_exact + (np.log(df / np.float32(max_exact)) / np.float32(math.log(BLOCK / max_exact))
                         * np.float32(N_BUCKETS - max_exact)).astype(np.int32)
    large = np.minimum(large, N_BUCKETS - 1)
    bucket = np.where(d < max_exact, d, large)
    return np.where((dist >= 0) & (dist < BLOCK), bucket, -1).astype(np.int32)


def _build_bias(bucket_ref, rb_ref, bias_ref):
    bk = bucket_ref[...]
    for h in range(N_HEADS):
        def add(b, acc, h=h):
            return acc + jnp.where(bk == b, rb_ref[h, b], 0.0)
        bias_ref[h] = lax.fori_loop(0, N_BUCKETS, add, jnp.zeros((BLOCK, 2 * BLOCK), F32))


def _kv_variants(prev_ref, cur_ref):
    cat = jnp.concatenate([prev_ref[...], cur_ref[...]], axis=0)
    lo = lax.broadcasted_iota(jnp.int32, cat.shape, 1) < HEAD_DIM
    zero = jnp.zeros_like(cat)
    head0_lo = jnp.where(lo, cat, zero)
    head1_hi = jnp.where(lo, zero, cat)
    return ((head0_lo, pltpu.roll(head0_lo, HEAD_DIM, 1)), (pltpu.roll(head1_hi, HEAD_DIM, 1), head1_hi))


def _merge_kv_grads(g):
    lo = lax.broadcasted_iota(jnp.int32, g[0][0].shape, 1) < HEAD_DIM
    return jnp.where(lo, g[0][0] + pltpu.roll(g[0][1], HEAD_DIM, 1), g[1][1] + pltpu.roll(g[1][0], HEAD_DIM, 1))


def _head_lanes(h):
    return slice((h // 2) * LANES, (h // 2 + 1) * LANES)


def _attn_probs(q_ref, kvar, bias_ref, sk_ref, valid, s_ref):
    for h in range(N_HEADS):
        s_ref[h] = _dot_nt(q_ref[:, _head_lanes(h)], kvar[h // 4][h % 2])
    head = lax.broadcasted_iota(jnp.int32, (N_HEADS, 1, 1), 0)
    sink = jnp.zeros((N_HEADS, 1, 1), F32)
    for h in range(N_HEADS):
        sink = jnp.where(head == h, sk_ref[0, h], sink)
    s = jnp.where(valid[None], s_ref[...] * (HEAD_DIM ** -0.5) + bias_ref[...], NEG_INF)
    m = jnp.maximum(jnp.max(s, axis=-1, keepdims=True), sink)
    p = jnp.exp(s - m)
    e_sink = jnp.exp(sink - m)
    inv = 1.0 / (jnp.sum(p, axis=-1, keepdims=True) + e_sink)
    return p * inv, e_sink * inv


def _attn_valid(bucket_ref, n):
    col = lax.broadcasted_iota(jnp.int32, (BLOCK, 2 * BLOCK), 1)
    return (bucket_ref[...] >= 0) & ((n > 0) | (col >= BLOCK))


def _attn_fwd(q, k, v, bucket, rel_bias, sinks, carry=None):
    T = q.shape[0]
    nb = T // BLOCK

    def body(q_ref, kc_ref, kp_ref, vc_ref, vp_ref, bucket_ref, rb_ref, sk_ref, o_ref, bias_ref, s_ref, p_ref):
        n = pl.program_id(0)

        @pl.when(n == 0)
        def _():
            _build_bias(bucket_ref, rb_ref, bias_ref)

        kvar = _kv_variants(kp_ref, kc_ref)
        vvar = _kv_variants(vp_ref, vc_ref)
        pr, _ = _attn_probs(q_ref, kvar, bias_ref, sk_ref, _attn_valid(bucket_ref, n), s_ref)
        p_ref[...] = pr.astype(BF16)
        for m in range(N_HEADS // 2):
            acc = _dot(p_ref[2 * m], vvar[m // 2][0]) + _dot(p_ref[2 * m + 1], vvar[m // 2][1])
            o_ref[:, m * LANES:(m + 1) * LANES] = acc.astype(o_ref.dtype)

    cur = lambda w: pl.BlockSpec((BLOCK, w), lambda n: (n, 0))
    prev = lambda w: pl.BlockSpec((BLOCK, w), lambda n: (jnp.maximum(n - 1, 0), 0))
    smem = pl.BlockSpec(memory_space=pltpu.SMEM)
    return _hosted_call(
        body, carry, _edge_1d(nb, pass_on_last=True), name="attn_fwd", grid=(nb,),
        in_specs=[cur(ATTN_W), cur(KV_W), prev(KV_W), cur(KV_W), prev(KV_W), _const_spec((BLOCK, 2 * BLOCK)), smem, smem],
        out_specs=[cur(ATTN_W)],
        out_shape=[jax.ShapeDtypeStruct((T, ATTN_W), BF16)],
        scratch_shapes=[pltpu.VMEM((N_HEADS, BLOCK, 2 * BLOCK), F32), pltpu.VMEM((N_HEADS, BLOCK, 2 * BLOCK), F32),
                        pltpu.VMEM((N_HEADS, BLOCK, 2 * BLOCK), BF16)],
        compiler_params=_params(("arbitrary",)), inputs=(q, k, k, v, v, bucket, rel_bias, sinks))


ATTN_SMALL_ROWS = N_BUCKETS + SUBLANES


def _attn_bwd(q, k, v, datt, bucket, rel_bias, sinks, carry=None):
    T = q.shape[0]
    nb = T // BLOCK

    def body(q_ref, do_ref, kc_ref, kp_ref, vc_ref, vp_ref, bucket_ref, rb_ref, sk_ref,
             dq_ref, dkv_ref, small_ref, bias_ref, ds_sum_ref, dsink_ref, kcarry_ref, vcarry_ref,
             s_ref, dp_ref, p_ref, dsc_ref):
        n = pl.program_id(0)

        @pl.when(n == 0)
        def _():
            _build_bias(bucket_ref, rb_ref, bias_ref)
            ds_sum_ref[...] = jnp.zeros_like(ds_sum_ref)
            dsink_ref[...] = jnp.zeros_like(dsink_ref)
            kcarry_ref[...] = jnp.zeros_like(kcarry_ref)
            vcarry_ref[...] = jnp.zeros_like(vcarry_ref)

        @pl.when(n < nb)
        def _():
            kvar = _kv_variants(kp_ref, kc_ref)
            vvar = _kv_variants(vp_ref, vc_ref)
            pr, p_sink = _attn_probs(q_ref, kvar, bias_ref, sk_ref, _attn_valid(bucket_ref, n), s_ref)
            for h in range(N_HEADS):
                dp_ref[h] = _dot_nt(do_ref[:, _head_lanes(h)], vvar[h // 4][h % 2])
            dp = dp_ref[...]
            dsum = jnp.sum(pr * dp, axis=-1, keepdims=True)
            ds = pr * (dp - dsum)
            ds_sum_ref[...] += ds
            dsink_ref[...] -= jnp.sum(p_sink * dsum, axis=1, keepdims=True)
            dsc_ref[...] = (ds * (HEAD_DIM ** -0.5)).astype(BF16)
            p_ref[...] = pr.astype(BF16)
            for m in range(N_HEADS // 2):
                dqm = _dot(dsc_ref[2 * m], kvar[m // 2][0]) + _dot(dsc_ref[2 * m + 1], kvar[m // 2][1])
                dq_ref[:, m * LANES:(m + 1) * LANES] = dqm.astype(dq_ref.dtype)
            dk_var = [[None, None], [None, None]]
            dv_var = [[None, None], [None, None]]
            for kvh in range(2):
                for e in range(2):
                    heads = [h for h in range(N_HEADS) if h // 4 == kvh and h % 2 == e]
                    dk_var[kvh][e] = sum(_dot_tn(dsc_ref[h], q_ref[:, _head_lanes(h)]) for h in heads)
                    dv_var[kvh][e] = sum(_dot_tn(p_ref[h], do_ref[:, _head_lanes(h)]) for h in heads)
            dk_cat = _merge_kv_grads(dk_var)
            dv_cat = _merge_kv_grads(dv_var)

            @pl.when(n > 0)
            def _():
                dkv_ref[:, :KV_W] = (kcarry_ref[...] + dk_cat[:BLOCK]).astype(BF16)
                dkv_ref[:, KV_W:] = (vcarry_ref[...] + dv_cat[:BLOCK]).astype(BF16)

            kcarry_ref[...] = dk_cat[BLOCK:]
            vcarry_ref[...] = dv_cat[BLOCK:]

        @pl.when(n == nb)
        def _():
            dkv_ref[:, :KV_W] = kcarry_ref[...].astype(BF16)
            dkv_ref[:, KV_W:] = vcarry_ref[...].astype(BF16)
            bk = bucket_ref[...]
            row = lax.broadcasted_iota(jnp.int32, (N_HEADS, ATTN_SMALL_ROWS, LANES), 1)

            def add(b, acc):
                masked = jnp.where((bk == b)[None], ds_sum_ref[...], 0.0)
                val = jnp.sum(jnp.sum(masked, axis=1, keepdims=True), axis=2, keepdims=True)
                return acc + jnp.where(row == b, val, 0.0)

            small_ref[...] = lax.fori_loop(0, N_BUCKETS, add, jnp.where(row == N_BUCKETS, dsink_ref[...], 0.0))

    last = nb - 1
    cur = lambda w: pl.BlockSpec((BLOCK, w), lambda n: (jnp.minimum(n, last), 0))
    prev = lambda w: pl.BlockSpec((BLOCK, w), lambda n: (jnp.clip(n - 1, 0, last), 0))
    smem = pl.BlockSpec(memory_space=pltpu.SMEM)
    return _hosted_call(
        body, carry, _edge_1d(nb + 1), name="attn_bwd", grid=(nb + 1,),
        in_specs=[cur(ATTN_W), cur(ATTN_W), cur(KV_W), prev(KV_W), cur(KV_W), prev(KV_W),
                  _const_spec((BLOCK, 2 * BLOCK)), smem, smem],
        out_specs=[cur(ATTN_W), prev(2 * KV_W), pl.BlockSpec((N_HEADS, ATTN_SMALL_ROWS, LANES), lambda n: (0, 0, 0))],
        out_shape=[jax.ShapeDtypeStruct((T, ATTN_W), BF16), jax.ShapeDtypeStruct((T, 2 * KV_W), BF16),
                   jax.ShapeDtypeStruct((N_HEADS, ATTN_SMALL_ROWS, LANES), F32)],
        scratch_shapes=[pltpu.VMEM((N_HEADS, BLOCK, 2 * BLOCK), F32), pltpu.VMEM((N_HEADS, BLOCK, 2 * BLOCK), F32),
                        pltpu.VMEM((N_HEADS, 1, 1), F32), pltpu.VMEM((BLOCK, KV_W), F32), pltpu.VMEM((BLOCK, KV_W), F32),
                        pltpu.VMEM((N_HEADS, BLOCK, 2 * BLOCK), F32), pltpu.VMEM((N_HEADS, BLOCK, 2 * BLOCK), F32),
                        pltpu.VMEM((N_HEADS, BLOCK, 2 * BLOCK), BF16), pltpu.VMEM((N_HEADS, BLOCK, 2 * BLOCK), BF16)],
        compiler_params=_params(("arbitrary",)), inputs=(q, datt, k, k, v, v, bucket, rel_bias, sinks))


SCAN_UNROLL = 4


def _cmul(ar, ai, br, bi):
    return ar * br - ai * bi, ar * bi + ai * br


def _cmul_conj(ar, ai, br, bi):
    return ar * br + ai * bi, ar * bi - ai * br


def _ssm_discretize(lr, li, ldt):
    dt = jnp.exp(ldt)
    mag = jnp.exp(lr * dt)
    ab_re = mag * jnp.cos(li * dt)
    ab_im = mag * jnp.sin(li * dt)
    nr = ab_re - 1.0
    den = lr * lr + li * li
    f_re = (nr * lr + ab_im * li) / den
    f_im = (ab_im * lr - nr * li) / den
    return ab_re, ab_im, f_re, f_im


def _ssm_prep(lam_re, lam_im, ldt_rep, bd_re, bd_im):
    def body(lr_ref, li_ref, ldt_ref, bdr_ref, bdi_ref, ar_ref, ai_ref, br_ref, bi_ref):
        ab_re, ab_im, f_re, f_im = _ssm_discretize(lr_ref[...], li_ref[...], ldt_ref[...])
        ar_ref[...] = ab_re
        ai_ref[...] = ab_im
        bdr, bdi = bdr_ref[0], bdi_ref[0]
        br_ref[0] = (bdr * f_re - bdi * f_im).astype(BF16)
        bi_ref[0] = (bdi * f_re + bdr * f_im).astype(BF16)

    row = pl.BlockSpec((1, SSM_LANE_BLOCK), lambda j: (0, j))
    mat = pl.BlockSpec((1, LANES, SSM_LANE_BLOCK), lambda j: (j, 0, 0))
    return pl.pallas_call(
        body, name="ssm_prep", grid=(N_SSM_BLOCKS,),
        in_specs=[row, row, row, mat, mat], out_specs=[row, row, mat, mat],
        out_shape=[jax.ShapeDtypeStruct((1, STATES), F32)] * 2 + [jax.ShapeDtypeStruct((N_SSM_BLOCKS, LANES, SSM_LANE_BLOCK), BF16)] * 2,
        compiler_params=_params(("arbitrary",)),
    )(*_in_hbm(lam_re, lam_im, ldt_rep, bd_re, bd_im))


def _ssm_prep_bwd(lam_re, lam_im, ldt_rep, bd_re, bd_im, dbr, dbi, da_re, da_im):
    def body(lr_ref, li_ref, ldt_ref, bdr_ref, bdi_ref, dbr_ref, dbi_ref, dar_ref, dai_ref,
             dbdr_ref, dbdi_ref, dlr_ref, dli_ref, dldt_ref):
        lr, li, ldt = lr_ref[...], li_ref[...], ldt_ref[...]
        (_, _, f_re, f_im), vjp = jax.vjp(_ssm_discretize, lr, li, ldt)
        bdr, bdi, gbr, gbi = bdr_ref[0], bdi_ref[0], dbr_ref[0], dbi_ref[0]
        dbdr_ref[0] = gbr * f_re + gbi * f_im
        dbdi_ref[0] = gbi * f_re - gbr * f_im
        df_re = jnp.sum(gbr * bdr + gbi * bdi, axis=0, keepdims=True)
        df_im = jnp.sum(gbi * bdr - gbr * bdi, axis=0, keepdims=True)
        dlr, dli, dldt = vjp((dar_ref[...], dai_ref[...], df_re, df_im))
        dlr_ref[...] = dlr
        dli_ref[...] = dli
        dldt_ref[...] = dldt

    row = pl.BlockSpec((1, SSM_LANE_BLOCK), lambda j: (0, j))
    mat = pl.BlockSpec((1, LANES, SSM_LANE_BLOCK), lambda j: (j, 0, 0))
    mat_shape = jax.ShapeDtypeStruct((N_SSM_BLOCKS, LANES, SSM_LANE_BLOCK), F32)
    row_shape = jax.ShapeDtypeStruct((1, STATES), F32)
    return pl.pallas_call(
        body, name="ssm_prep_bwd", grid=(N_SSM_BLOCKS,),
        in_specs=[row, row, row, mat, mat, mat, mat, row, row], out_specs=[mat, mat, row, row, row],
        out_shape=[mat_shape, mat_shape, row_shape, row_shape, row_shape],
        compiler_params=_params(("arbitrary",)),
    )(*_in_hbm(lam_re, lam_im, ldt_rep, bd_re, bd_im, dbr, dbi, da_re, da_im))


def _group_sum(x):
    def body(x_ref, o_ref):
        o_ref[...] = jnp.sum(x_ref[...], axis=1, keepdims=True)
    return pl.pallas_call(body, name="ssm_group_sum", grid=(1,), in_specs=[_whole(x.shape)], out_specs=_whole((N_GROUPS, 1)),
                          out_shape=jax.ShapeDtypeStruct((N_GROUPS, 1), F32))(*_in_hbm(x))


def _power_table(ar, ai, p_re_ref, p_im_ref, steps):
    shape = (SUBLANES, SSM_LANE_BLOCK)
    p_re_ref[0:SUBLANES] = jnp.broadcast_to(ar, shape)
    p_im_ref[0:SUBLANES] = jnp.broadcast_to(ai, shape)
    m = 1
    while m < steps:
        rows = m * SUBLANES
        top_re = p_re_ref[rows - SUBLANES:rows]
        top_im = p_im_ref[rows - SUBLANES:rows]
        cur_re = p_re_ref[0:rows].reshape(m, SUBLANES, SSM_LANE_BLOCK)
        cur_im = p_im_ref[0:rows].reshape(m, SUBLANES, SSM_LANE_BLOCK)
        nxt_re, nxt_im = _cmul(cur_re, cur_im, top_re[None], top_im[None])
        p_re_ref[rows:2 * rows] = nxt_re.reshape(rows, SSM_LANE_BLOCK)
        p_im_ref[rows:2 * rows] = nxt_im.reshape(rows, SSM_LANE_BLOCK)
        m *= 2


def _to_segments(src_ref, dst_ref, steps):
    for s in range(SUBLANES):
        dst_ref[pl.ds(s, steps, stride=SUBLANES), :] = src_ref[s * steps:(s + 1) * steps, :]


def _from_segments(src_ref, dst_ref, steps):
    for s in range(SUBLANES):
        dst_ref[s * steps:(s + 1) * steps, :] = src_ref[pl.ds(s, steps, stride=SUBLANES), :]


def _segment_carries(e_re, e_im, an_re, an_im, c_re, c_im, reverse):
    order = range(SUBLANES - 1, -1, -1) if reverse else range(SUBLANES)
    ins_re, ins_im = [None] * SUBLANES, [None] * SUBLANES
    for s in order:
        ins_re[s], ins_im[s] = c_re, c_im
        pr, pi = _cmul(an_re, an_im, c_re, c_im)
        c_re = e_re[s:s + 1] + pr
        c_im = e_im[s:s + 1] + pi
    return jnp.concatenate(ins_re, axis=0), jnp.concatenate(ins_im, axis=0), c_re, c_im


def _ssm_fwd(u, a_re, a_im, b_re, b_im, c_re, c_im, d_skip, chunk, carry=None):
    T = u.shape[0]
    nc = T // chunk
    steps = chunk // SUBLANES
    blk = SSM_LANE_BLOCK

    def body(u_ref, ar_ref, ai_ref, br_ref, bi_ref, cr_ref, ci_ref, dk_ref,
             y_ref, hr_ref, hi_ref, inr_ref, ini_ref, useg_ref, yseg_ref, pr_ref, pi_ref, carry_ref):
        c = pl.program_id(1)
        ar, ai = ar_ref[...], ai_ref[...]

        @pl.when(c == 0)
        def _():
            _power_table(ar, ai, pr_ref, pi_ref, steps)
            carry_ref[...] = jnp.zeros_like(carry_ref)

        _to_segments(u_ref, useg_ref, steps)
        ub = useg_ref[...].astype(BF16)
        hr_ref[...] = _dot(ub, br_ref[0])
        hi_ref[...] = _dot(ub, bi_ref[0])
        first = slice(0, SUBLANES)

        def scan(t4, prev):
            for j in range(SCAN_UNROLL):
                rows = pl.ds(pl.multiple_of((t4 * SCAN_UNROLL + j) * SUBLANES, SUBLANES), SUBLANES)
                pr, pi = _cmul(pr_ref[first, :], pi_ref[first, :], prev[0], prev[1])
                prev = (pr + hr_ref[rows, :], pi + hi_ref[rows, :])
                hr_ref[rows, :] = prev[0]
                hi_ref[rows, :] = prev[1]
            return prev

        zero = jnp.zeros((SUBLANES, blk), F32)
        lax.fori_loop(0, steps // SCAN_UNROLL, scan, (zero, zero))

        top = slice(chunk - SUBLANES, chunk)
        in_re, in_im, out_re, out_im = _segment_carries(
            hr_ref[top, :], hi_ref[top, :], pr_ref[top, :][0:1], pi_ref[top, :][0:1],
            carry_ref[0:1, :], carry_ref[1:2, :], reverse=False)
        carry_ref[0:1, :] = out_re
        carry_ref[1:2, :] = out_im
        inr_ref[...] = in_re
        ini_ref[...] = in_im

        def fix(t4, _):
            for j in range(SCAN_UNROLL):
                rows = pl.ds(pl.multiple_of((t4 * SCAN_UNROLL + j) * SUBLANES, SUBLANES), SUBLANES)
                fr, fi = _cmul(pr_ref[rows, :], pi_ref[rows, :], in_re, in_im)
                hr_ref[rows, :] += fr
                hi_ref[rows, :] += fi
            return 0

        lax.fori_loop(0, steps // SCAN_UNROLL, fix, 0)

        yseg_ref[...] = _dot(hr_ref[...].astype(BF16), cr_ref[0]) - _dot(hi_ref[...].astype(BF16), ci_ref[0])
        _from_segments(yseg_ref, y_ref, steps)
        y_ref[...] += dk_ref[...] * u_ref[...]

    row = pl.BlockSpec((1, blk), lambda j, c: (0, j))
    b_mat = pl.BlockSpec((1, LANES, blk), lambda j, c: (j, 0, 0))
    c_mat = pl.BlockSpec((1, blk, LANES), lambda j, c: (j, 0, 0))
    tok = pl.BlockSpec((chunk, LANES), lambda j, c: (c, j))
    state = pl.BlockSpec((chunk, blk), lambda j, c: (c, j))
    enter = pl.BlockSpec((SUBLANES, blk), lambda j, c: (c, j))
    return _hosted_call(
        body, carry, _edge_2d(N_SSM_BLOCKS, nc), name="ssm_fwd", grid=(N_SSM_BLOCKS, nc),
        in_specs=[tok, row, row, b_mat, b_mat, c_mat, c_mat, pl.BlockSpec((1, LANES), lambda j, c: (0, j))],
        out_specs=[tok, state, state, enter, enter],
        out_shape=[jax.ShapeDtypeStruct((T, SSM_W), F32), jax.ShapeDtypeStruct((T, STATES), F32),
                   jax.ShapeDtypeStruct((T, STATES), F32), jax.ShapeDtypeStruct((nc * SUBLANES, STATES), F32),
                   jax.ShapeDtypeStruct((nc * SUBLANES, STATES), F32)],
        scratch_shapes=[pltpu.VMEM((chunk, LANES), F32), pltpu.VMEM((chunk, LANES), F32),
                        pltpu.VMEM((chunk, blk), F32), pltpu.VMEM((chunk, blk), F32), pltpu.VMEM((SUBLANES, blk), F32)],
        compiler_params=_params(("arbitrary", "arbitrary"), VMEM_MID),
        inputs=(u, a_re, a_im, b_re, b_im, c_re, c_im, d_skip))


def _ssm_bwd(dy, u, h_re, h_im, in_re, in_im, a_re, a_im, b_re, b_im, c_re, c_im, d_skip, chunk, carry=None):
    T = u.shape[0]
    nc = T // chunk
    steps = chunk // SUBLANES
    blk = SSM_LANE_BLOCK

    def body(dy_ref, u_ref, hr_ref, hi_ref, inr_ref, ini_ref, ar_ref, ai_ref, br_ref, bi_ref, cr_ref, ci_ref, dk_ref,
             du_ref, dbr_ref, dbi_ref, dcr_ref, dci_ref, dar_ref, dai_ref, ddk_ref,
             dyseg_ref, useg_ref, duseg_ref, gr_ref, gi_ref, pr_ref, pi_ref, carry_ref, accr_ref, acci_ref):
        c = pl.program_id(1)
        ar, ai = ar_ref[...], ai_ref[...]

        @pl.when(c == 0)
        def _():
            _power_table(ar, ai, pr_ref, pi_ref, steps)
            carry_ref[...] = jnp.zeros_like(carry_ref)
            accr_ref[...] = jnp.zeros_like(accr_ref)
            acci_ref[...] = jnp.zeros_like(acci_ref)

        _to_segments(dy_ref, dyseg_ref, steps)
        _to_segments(u_ref, useg_ref, steps)
        dyb = dyseg_ref[...].astype(BF16)
        ub = useg_ref[...].astype(BF16)
        gr_ref[...] = _dot_nt(dyb, cr_ref[0])
        gi_ref[...] = -_dot_nt(dyb, ci_ref[0])
        dcr = _dot_tn(hr_ref[...].astype(BF16), dyb)
        dci = -_dot_tn(hi_ref[...].astype(BF16), dyb)
        ddk = jnp.sum(dy_ref[...] * u_ref[...], axis=0, keepdims=True)

        first = slice(0, SUBLANES)

        def scan(k4, nxt):
            for j in range(SCAN_UNROLL):
                t = steps - 1 - (k4 * SCAN_UNROLL + j)
                rows = pl.ds(pl.multiple_of(t * SUBLANES, SUBLANES), SUBLANES)
                pr, pi = _cmul_conj(pr_ref[first, :], pi_ref[first, :], nxt[0], nxt[1])
                nxt = (pr + gr_ref[rows, :], pi + gi_ref[rows, :])
                gr_ref[rows, :] = nxt[0]
                gi_ref[rows, :] = nxt[1]
            return nxt

        top = slice(chunk - SUBLANES, chunk)
        zero = jnp.zeros((SUBLANES, blk), F32)
        lax.fori_loop(0, steps // SCAN_UNROLL, scan, (zero, zero))

        gin_re, gin_im, out_re, out_im = _segment_carries(
            gr_ref[0:SUBLANES, :], gi_ref[0:SUBLANES, :], pr_ref[top, :][0:1], -pi_ref[top, :][0:1],
            carry_ref[0:1, :], carry_ref[1:2, :], reverse=True)
        carry_ref[0:1, :] = out_re
        carry_ref[1:2, :] = out_im

        def fix_row(rows, prow, hp_re, hp_im, acc):
            fr, fi = _cmul_conj(pr_ref[prow, :], pi_ref[prow, :], gin_re, gin_im)
            g_re = gr_ref[rows, :] + fr
            g_im = gi_ref[rows, :] + fi
            gr_ref[rows, :] = g_re
            gi_ref[rows, :] = g_im
            return acc[0] + g_re * hp_re + g_im * hp_im, acc[1] + g_im * hp_re - g_re * hp_im

        def fix_at(t, acc):
            aligned = (lambda r: r * SUBLANES) if isinstance(t, int) else (lambda r: pl.multiple_of(r * SUBLANES, SUBLANES))
            rows, before, prow = (pl.ds(aligned(r), SUBLANES) for r in (t, t - 1, steps - 1 - t))
            return fix_row(rows, prow, hr_ref[before, :], hi_ref[before, :], acc)

        def fix(t4, acc):
            for j in range(SCAN_UNROLL):
                acc = fix_at(t4 * SCAN_UNROLL + j, acc)
            return acc

        acc = fix_row(first, top, inr_ref[...], ini_ref[...], (accr_ref[...], acci_ref[...]))
        for t in range(1, SCAN_UNROLL):
            acc = fix_at(t, acc)
        acc_re, acc_im = lax.fori_loop(1, steps // SCAN_UNROLL, fix, acc)
        accr_ref[...] = acc_re
        acci_ref[...] = acc_im

        gbr = gr_ref[...].astype(BF16)
        gbi = gi_ref[...].astype(BF16)
        duseg_ref[...] = _dot_nt(gbr, br_ref[0]) + _dot_nt(gbi, bi_ref[0])
        _from_segments(duseg_ref, dyseg_ref, steps)
        du_ref[...] = (dyseg_ref[...] + dk_ref[...] * dy_ref[...]).astype(BF16)
        dbr = _dot_tn(ub, gbr)
        dbi = _dot_tn(ub, gbi)

        @pl.when(c == 0)
        def _():
            dbr_ref[0] = dbr
            dbi_ref[0] = dbi
            dcr_ref[0] = dcr
            dci_ref[0] = dci
            ddk_ref[...] = ddk

        @pl.when(c > 0)
        def _():
            dbr_ref[0] += dbr
            dbi_ref[0] += dbi
            dcr_ref[0] += dcr
            dci_ref[0] += dci
            ddk_ref[...] += ddk

        @pl.when(c == nc - 1)
        def _():
            dar_ref[...] = jnp.sum(acc_re, axis=0, keepdims=True)
            dai_ref[...] = jnp.sum(acc_im, axis=0, keepdims=True)

    rev = lambda c: nc - 1 - c
    row = pl.BlockSpec((1, blk), lambda j, c: (0, j))
    b_mat = pl.BlockSpec((1, LANES, blk), lambda j, c: (j, 0, 0))
    c_mat = pl.BlockSpec((1, blk, LANES), lambda j, c: (j, 0, 0))
    tok = pl.BlockSpec((chunk, LANES), lambda j, c: (rev(c), j))
    state = pl.BlockSpec((chunk, blk), lambda j, c: (rev(c), j))
    enter = pl.BlockSpec((SUBLANES, blk), lambda j, c: (rev(c), j))
    chan = pl.BlockSpec((1, LANES), lambda j, c: (0, j))
    f32 = lambda *s: jax.ShapeDtypeStruct(s, F32)
    return _hosted_call(
        body, carry, _edge_2d(N_SSM_BLOCKS, nc), name="ssm_bwd", grid=(N_SSM_BLOCKS, nc),
        in_specs=[tok, tok, state, state, enter, enter, row, row, b_mat, b_mat, c_mat, c_mat, chan],
        out_specs=[tok, b_mat, b_mat, c_mat, c_mat, row, row, chan],
        out_shape=[jax.ShapeDtypeStruct((T, SSM_W), BF16), f32(N_SSM_BLOCKS, LANES, blk), f32(N_SSM_BLOCKS, LANES, blk),
                   f32(N_SSM_BLOCKS, blk, LANES), f32(N_SSM_BLOCKS, blk, LANES), f32(1, STATES), f32(1, STATES), f32(1, SSM_W)],
        scratch_shapes=[pltpu.VMEM((chunk, LANES), F32), pltpu.VMEM((chunk, LANES), F32), pltpu.VMEM((chunk, LANES), F32),
                        pltpu.VMEM((chunk, blk), F32), pltpu.VMEM((chunk, blk), F32),
                        pltpu.VMEM((chunk, blk), F32), pltpu.VMEM((chunk, blk), F32),
                        pltpu.VMEM((SUBLANES, blk), F32), pltpu.VMEM((SUBLANES, blk), F32), pltpu.VMEM((SUBLANES, blk), F32)],
        compiler_params=_params(("arbitrary", "arbitrary"), VMEM_BIG),
        inputs=(dy, u, h_re, h_im, in_re, in_im, a_re, a_im, b_re, b_im, c_re, c_im, d_skip))


def _merge_forward(y, att, ga, gs, w_glu, w_ssm, w_attn):
    z = jax.nn.gelu(y)
    zb = z.astype(BF16)
    gl = jax.nn.sigmoid(_dot(zb, w_glu))
    z2b = (z * gl).astype(BF16)
    y_ssm = _dot(z2b, w_ssm)
    y_attn = _dot(att, w_attn)
    sa = jax.nn.sigmoid(ga)
    ss = jax.nn.sigmoid(gs)
    merged = (sa * y_attn + ss * y_ssm).astype(BF16)
    return z, zb, gl, z2b, y_ssm, y_attn, sa, ss, merged


def _merge_fwd(x, y, att, ga, gs, g2, g3, w_glu, w_ssm, w_attn, w_out, tile):
    T = x.shape[0]

    def body(x_ref, y_ref, att_ref, ga_ref, gs_ref, g2_ref, g3_ref, wg_ref, ws_ref, wa_ref, wo_ref, x1_ref, o_ref, h2_ref):
        merged = _merge_forward(y_ref[...], att_ref[...], ga_ref[...], gs_ref[...], wg_ref[...], ws_ref[...], wa_ref[...])[-1]
        o = _dot(merged, wo_ref[...])
        x1 = x_ref[...] + o * _rms_scale(o) * g2_ref[...]
        o_ref[...] = o
        x1_ref[...] = x1
        h2_ref[...] = (x1 * _rms_scale(x1) * g3_ref[...]).astype(BF16)

    tok = lambda w: pl.BlockSpec((tile, w), lambda i: (i, 0))
    vec = _const_spec((1, D_MODEL))
    return pl.pallas_call(
        body, name="merge_fwd", grid=(T // tile,),
        in_specs=[tok(D_MODEL), tok(SSM_W), tok(ATTN_W), tok(D_MODEL), tok(D_MODEL), vec, vec,
                  _const_spec((SSM_W, SSM_W)), _const_spec((SSM_W, D_MODEL)), _const_spec((ATTN_W, D_MODEL)),
                  _const_spec((D_MODEL, D_MODEL))],
        out_specs=[tok(D_MODEL), tok(D_MODEL), tok(D_MODEL)],
        out_shape=_hbm_out([jax.ShapeDtypeStruct((T, D_MODEL), F32), jax.ShapeDtypeStruct((T, D_MODEL), F32),
                            jax.ShapeDtypeStruct((T, D_MODEL), BF16)]),
        compiler_params=_params(("arbitrary",), VMEM_MID),
    )(*_in_hbm(x, y, att, ga, gs, g2, g3, w_glu, w_ssm, w_attn, w_out))


def _merge_bwd(dh2, dx2, x1, o, y, att, ga, gs, g2, g3, w_glu, w_ssm, w_attn, w_out, tile, carry=None):
    T = x1.shape[0]
    n_steps = T // tile

    group = min(2, n_steps)
    staged_widths = (D_MODEL, D_MODEL, ATTN_W, D_MODEL, SSM_W, D_MODEL, SSM_W, SSM_W)

    def body(dh2_ref, dx2_ref, x1_ref, o_ref, y_ref, att_ref, ga_ref, gs_ref, g2_ref, g3_ref, wg_ref, ws_ref, wa_ref, wo_ref,
             dx1_ref, dgates_ref, datt_ref, dy_ref, dwg_hbm, dws_hbm, dwa_hbm, dwo_hbm, dg2_ref, dg3_ref,
             awg_ref, aws_ref, awa_ref, awo_ref, *staged):
        i = pl.program_id(0)
        x1v, ov = x1_ref[...], o_ref[...]
        dxn, dg3 = _rms_bwd(dh2_ref[...], x1v, _rms_scale(x1v), g3_ref[...])
        dx1 = dx2_ref[...] + dxn
        dx1_ref[...] = dx1
        do, dg2 = _rms_bwd(dx1, ov, _rms_scale(ov), g2_ref[...])
        dob = do.astype(BF16)

        yv = y_ref[...]
        att = att_ref[...]
        z, zb, gl, z2b, y_ssm, y_attn, sa, ss, merged = _merge_forward(
            yv, att, ga_ref[...], gs_ref[...], wg_ref[...], ws_ref[...], wa_ref[...])
        dmerged = _dot_nt(dob, wo_ref[...])
        dya = (dmerged * sa).astype(BF16)
        dys = (dmerged * ss).astype(BF16)
        dgates_ref[:, :D_MODEL] = (dmerged * y_attn * sa * (1.0 - sa)).astype(BF16)
        dgates_ref[:, D_MODEL:] = (dmerged * y_ssm * ss * (1.0 - ss)).astype(BF16)
        datt_ref[...] = _dot_nt(dya, wa_ref[...]).astype(BF16)
        dz2 = _dot_nt(dys, ws_ref[...])
        dpre = (dz2 * z * gl * (1.0 - gl)).astype(BF16)
        dz = dz2 * gl + _dot_nt(dpre, wg_ref[...])
        _, gelu_vjp = jax.vjp(jax.nn.gelu, yv)
        dy_ref[...] = gelu_vjp(dz)[0]

        part = pl.ds(pl.multiple_of((i % group) * tile, tile), tile)
        for ref, val in zip(staged, (merged, dob, att, dya, z2b, dys, zb, dpre)):
            ref[part, :] = val

        @pl.when(i == 0)
        def _():
            dg2_ref[...] = dg2
            dg3_ref[...] = dg3

        @pl.when(i > 0)
        def _():
            dg2_ref[...] += dg2
            dg3_ref[...] += dg3

        def weight_grads():
            s_merged, s_dob, s_att, s_dya, s_z2b, s_dys, s_zb, s_dpre = (ref[...] for ref in staged)
            return ((awo_ref, _dot_tn(s_merged, s_dob)), (awa_ref, _dot_tn(s_att, s_dya)),
                    (aws_ref, _dot_tn(s_z2b, s_dys)), (awg_ref, _dot_tn(s_zb, s_dpre)))

        @pl.when(i == group - 1)
        def _():
            for ref, val in weight_grads():
                ref[...] = val

        @pl.when((i % group == group - 1) & (i > group - 1))
        def _():
            for ref, val in weight_grads():
                ref[...] += val

        @pl.when(i == n_steps - 1)
        def _():
            pltpu.sync_copy(awg_ref, dwg_hbm)
            pltpu.sync_copy(aws_ref, dws_hbm)
            pltpu.sync_copy(awa_ref, dwa_hbm)
            pltpu.sync_copy(awo_ref, dwo_hbm)

    tok = lambda w: pl.BlockSpec((tile, w), lambda i: (i, 0))
    vec = _const_spec((1, D_MODEL))
    any_ = pl.BlockSpec(memory_space=pl.ANY)
    vec_out = pl.BlockSpec((1, D_MODEL), lambda i: (0, 0))
    f32 = lambda *s: jax.ShapeDtypeStruct(s, F32)
    bf = lambda *s: jax.ShapeDtypeStruct(s, BF16)
    return _hosted_call(
        body, carry, _edge_1d(n_steps), name="merge_bwd", grid=(n_steps,),
        in_specs=[tok(D_MODEL), tok(D_MODEL), tok(D_MODEL), tok(D_MODEL), tok(SSM_W), tok(ATTN_W), tok(D_MODEL), tok(D_MODEL),
                  vec, vec, _const_spec((SSM_W, SSM_W)), _const_spec((SSM_W, D_MODEL)), _const_spec((ATTN_W, D_MODEL)),
                  _const_spec((D_MODEL, D_MODEL))],
        out_specs=[tok(D_MODEL), tok(2 * D_MODEL), tok(ATTN_W), tok(SSM_W), any_, any_, any_, any_, vec_out, vec_out],
        out_shape=[f32(T, D_MODEL), bf(T, 2 * D_MODEL), bf(T, ATTN_W), f32(T, SSM_W),
                   f32(SSM_W, SSM_W), f32(SSM_W, D_MODEL), f32(ATTN_W, D_MODEL), f32(D_MODEL, D_MODEL),
                   f32(1, D_MODEL), f32(1, D_MODEL)],
        scratch_shapes=[pltpu.VMEM((SSM_W, SSM_W), F32), pltpu.VMEM((SSM_W, D_MODEL), F32),
                        pltpu.VMEM((ATTN_W, D_MODEL), F32), pltpu.VMEM((D_MODEL, D_MODEL), F32)]
        + [pltpu.VMEM((group * tile, wd), BF16) for wd in staged_widths],
        compiler_params=_params(("arbitrary",), VMEM_BIG),
        inputs=(dh2, dx2, x1, o, y, att, ga, gs, g2, g3, w_glu, w_ssm, w_attn, w_out))


FF_SHARD = D_FF // N_DEV


def _mlp_fwd(h2, x1, target, g4, w_ff_in, w_ff_out, tile):
    T = h2.shape[0]
    col_chunk = 2 * FF_SHARD

    def body(h2_ref, x1_ref, tg_ref, g4_ref, wi_ref, wo_ref, a_ref, dfo_ref, dx2_ref, loss_ref, dg4_ref, rr_ref):
        i = pl.program_id(0)
        h2v = h2_ref[...]
        for c in range(D_FF // col_chunk):
            cols = slice(c * col_chunk, (c + 1) * col_chunk)
            a = _dot_nt(h2v, wi_ref[cols, :])
            a_ref[:, cols] = a.astype(BF16)
            ra = jnp.maximum(a, 0.0)
            rr_ref[:, cols] = (ra * ra).astype(BF16)
        f = _dot(rr_ref[...], wo_ref[...])
        r = _rms_scale(f)
        g = g4_ref[...]
        err = x1_ref[...] + f * r * g - tg_ref[...]
        dx2 = err * (1.0 / D_MODEL)
        dx2_ref[...] = dx2
        dfo, dg = _rms_bwd(dx2, f, r, g)
        dfo_ref[...] = dfo.astype(BF16)
        row = lax.broadcasted_iota(jnp.int32, (SUBLANES, LANES), 0)
        col = lax.broadcasted_iota(jnp.int32, (SUBLANES, LANES), 1)
        loss = jnp.where((row == 0) & (col == 0), (0.5 / D_MODEL) * jnp.sum(err * err), 0.0)

        @pl.when(i == 0)
        def _():
            loss_ref[...] = loss
            dg4_ref[...] = dg

        @pl.when(i > 0)
        def _():
            loss_ref[...] += loss
            dg4_ref[...] += dg

    tok = pl.BlockSpec((tile, D_MODEL), lambda i: (i, 0))
    return pl.pallas_call(
        body, name="mlp_fwd", grid=(T // tile,),
        in_specs=[tok, tok, tok, _const_spec((1, D_MODEL)), _const_spec((D_FF, D_MODEL)), _const_spec((D_FF, D_MODEL))],
        out_specs=[pl.BlockSpec((tile, D_FF), lambda i: (i, 0)), tok, tok,
                   pl.BlockSpec((SUBLANES, LANES), lambda i: (0, 0)), pl.BlockSpec((1, D_MODEL), lambda i: (0, 0))],
        out_shape=_hbm_out([jax.ShapeDtypeStruct((T, D_FF), BF16), jax.ShapeDtypeStruct((T, D_MODEL), BF16),
                            jax.ShapeDtypeStruct((T, D_MODEL), F32), jax.ShapeDtypeStruct((SUBLANES, LANES), F32),
                            jax.ShapeDtypeStruct((1, D_MODEL), F32)]),
        scratch_shapes=[pltpu.VMEM((tile, D_FF), BF16)],
        compiler_params=_params(("arbitrary",), VMEM_MAX),
    )(*_in_hbm(h2, x1, target, g4, w_ff_in.reshape(D_FF, D_MODEL), w_ff_out.reshape(D_FF, D_MODEL)))


def _mlp_weight_grads(dfo, a, h2, w_ff_out, row_chunk):
    T = h2.shape[0]

    def body(dfo_ref, h2_ref, a_ref, wo_ref, dwi_ref, dwo_ref, da_ref, rr_ref):
        def rows(r, _):
            sl = pl.ds(pl.multiple_of(r * row_chunk, row_chunk), row_chunk)
            ra = jnp.maximum(a_ref[sl, :].astype(F32), 0.0)
            da_ref[sl, :] = (_dot_nt(dfo_ref[sl, :], wo_ref[0]) * (2.0 * ra)).astype(BF16)
            rr_ref[sl, :] = (ra * ra).astype(BF16)
            return 0

        lax.fori_loop(0, T // row_chunk, rows, 0)
        dwo_ref[0] = _dot_tn(rr_ref[...], dfo_ref[...])
        dwi_ref[0] = _dot_tn(h2_ref[...], da_ref[...])

    return pl.pallas_call(
        body, name="mlp_weight_grads", grid=(N_DEV,),
        in_specs=[_const_spec((T, D_MODEL)), _const_spec((T, D_MODEL)), pl.BlockSpec((T, FF_SHARD), lambda k: (0, k)),
                  pl.BlockSpec((1, FF_SHARD, D_MODEL), lambda k: (k, 0, 0))],
        out_specs=[pl.BlockSpec((1, D_MODEL, FF_SHARD), lambda k: (k, 0, 0)),
                   pl.BlockSpec((1, FF_SHARD, D_MODEL), lambda k: (k, 0, 0)), pl.BlockSpec((T, FF_SHARD), lambda k: (0, k))],
        out_shape=_hbm_out([jax.ShapeDtypeStruct((N_DEV, D_MODEL, FF_SHARD), F32),
                            jax.ShapeDtypeStruct((N_DEV, FF_SHARD, D_MODEL), F32), jax.ShapeDtypeStruct((T, D_FF), BF16)]),
        scratch_shapes=[pltpu.VMEM((T, FF_SHARD), BF16)],
        compiler_params=_params(("arbitrary",), VMEM_MAX),
    )(*_in_hbm(dfo, h2, a, w_ff_out))


def _mlp_input_grad(da, w_ff_in_t, tile):
    T = da.shape[0]

    def body(da_ref, w_ref, o_ref):
        o_ref[...] = _dot(da_ref[...], w_ref[...])

    return pl.pallas_call(
        body, name="mlp_input_grad", grid=(T // tile,),
        in_specs=[pl.BlockSpec((tile, D_FF), lambda i: (i, 0)), _const_spec((D_FF, D_MODEL))],
        out_specs=pl.BlockSpec((tile, D_MODEL), lambda i: (i, 0)),
        out_shape=_hbm_out(jax.ShapeDtypeStruct((T, D_MODEL), F32)),
        compiler_params=_params(("arbitrary",), VMEM_MID),
    )(*_in_hbm(da, w_ff_in_t))


def _block_diag_in(b):
    bt = b.reshape(N_SSM_BLOCKS, GROUPS_PER_BLOCK, GROUP_CH, N_STATE)
    eye = jnp.eye(GROUPS_PER_BLOCK, dtype=b.dtype)
    return jnp.einsum("jacp,ab->jacbp", bt, eye).reshape(N_SSM_BLOCKS, LANES, SSM_LANE_BLOCK)


def _block_diag_in_grad(g):
    g = g.reshape(N_SSM_BLOCKS, GROUPS_PER_BLOCK, GROUP_CH, GROUPS_PER_BLOCK, N_STATE)
    d = jnp.diagonal(g, axis1=1, axis2=3)
    return jnp.transpose(d, (0, 3, 1, 2)).reshape(N_GROUPS, GROUP_CH, N_STATE)


def _block_diag_out(c):
    ct = c.reshape(N_SSM_BLOCKS, GROUPS_PER_BLOCK, GROUP_CH, N_STATE)
    eye = jnp.eye(GROUPS_PER_BLOCK, dtype=c.dtype)
    return jnp.einsum("jacp,ab->japbc", ct, eye).reshape(N_SSM_BLOCKS, SSM_LANE_BLOCK, LANES)


def _block_diag_out_grad(g):
    g = g.reshape(N_SSM_BLOCKS, GROUPS_PER_BLOCK, N_STATE, GROUPS_PER_BLOCK, GROUP_CH)
    d = jnp.diagonal(g, axis1=1, axis2=3)
    return jnp.transpose(d, (0, 3, 2, 1)).reshape(N_GROUPS, GROUP_CH, N_STATE)


def _tiles(T):
    return dict(proj=min(512, T), proj_bwd=min(512, T // 2), merge=min(512, T), merge_bwd=min(256, T),
                mlp_fwd=min(512, T), mlp_bwd=min(512, T), ssm_chunk=min(1024, T))


def _mesh_position():
    x, y, c = lax.axis_index("x"), lax.axis_index("y"), lax.axis_index("c")
    other_chips = [(1 - x, y), (x, 1 - y), (1 - x, 1 - y)]
    return x, y, c, other_chips


def _gather_carry(arrays):
    n = len(arrays)

    def copies(ins, outs, sems):
        send_sems, recv_sems, local_sems = sems
        x, y, c, chips = _mesh_position()
        me, sibling = (x, y, c), (x, y, 1 - c)

        def copy(a, k, block, to, src=None):
            px, py, pc = block
            dst = outs[a].at[4 * px + 2 * py + pc]
            return pltpu.make_async_remote_copy(
                src_ref=dst if src is None else src, dst_ref=dst, send_sem=send_sems.at[7 * a + k],
                recv_sem=recv_sems.at[7 * a + k], device_id=to, device_id_type=MESH_IDS)

        mine = [pltpu.make_async_copy(ins[a], outs[a].at[4 * x + 2 * y + c], local_sems.at[a]) for a in range(n)]
        first = []
        for a in range(n):
            first.append(copy(a, 0, me, sibling, src=ins[a]))
            first += [copy(a, 1 + j, me, (*chip, c), src=ins[a]) for j, chip in enumerate(chips)]
        return copy, mine, first, me, sibling, chips, c

    def start(ins, outs, sems):
        _, mine, first, *_ = copies(ins, outs, sems)
        for cp in mine + first:
            cp.start()

    def passed_on(copy, sibling, chips, c):
        return [copy(a, 4 + j, (*chip, c), sibling) for a in range(n) for j, chip in enumerate(chips)]

    def middle(ins, outs, sems):
        copy, _, _, me, sibling, chips, c = copies(ins, outs, sems)
        for a in range(n):
            for j, chip in enumerate(chips):
                copy(a, 1 + j, (*chip, c), me).wait_recv()
                copy(a, 4 + j, (*chip, c), sibling).start()

    def finish(ins, outs, sems):
        copy, mine, first, me, sibling, chips, c = copies(ins, outs, sems)
        for a in range(n):
            copy(a, 0, sibling, me).wait_recv()
            for j, chip in enumerate(chips):
                copy(a, 4 + j, (*chip, 1 - c), me).wait_recv()
        for cp in first + passed_on(copy, sibling, chips, c):
            cp.wait_send()
        for cp in mine:
            cp.wait()

    return _Carry(arrays, [jax.ShapeDtypeStruct((N_DEV,) + a.shape, a.dtype) for a in arrays],
                  [pltpu.SemaphoreType.DMA((7 * n,)), pltpu.SemaphoreType.DMA((7 * n,)), pltpu.SemaphoreType.DMA((n,))],
                  start, finish, middle)


def _pairwise_carry(arrays, n_slots, make_copies):
    n = len(arrays)

    def start(ins, outs, sems):
        for cp in make_copies(ins, outs, sems):
            cp.start()

    def finish(ins, outs, sems):
        for cp in make_copies(ins, outs, sems):
            cp.wait()

    return _Carry(arrays, [jax.ShapeDtypeStruct((n_slots,) + a.shape[1:], a.dtype) for a in arrays],
                  [pltpu.SemaphoreType.DMA((n_slots * n,)), pltpu.SemaphoreType.DMA((n_slots * n,))], start, finish)


def _sibling_carry(grads):
    def make_copies(ins, outs, sems):
        x, y, c, _ = _mesh_position()
        return [pltpu.make_async_remote_copy(
            src_ref=ins[a].at[2 * ch + (1 - c)], dst_ref=outs[a].at[ch], send_sem=sems[0].at[4 * a + ch],
            recv_sem=sems[1].at[4 * a + ch], device_id=(x, y, 1 - c), device_id_type=MESH_IDS)
            for a in range(len(grads)) for ch in range(4)]

    return _pairwise_carry(grads, 4, make_copies)


def _chips_carry(sums):
    def make_copies(ins, outs, sems):
        x, y, c, chips = _mesh_position()
        return [pltpu.make_async_remote_copy(
            src_ref=ins[a].at[2 * px + py], dst_ref=outs[a].at[j], send_sem=sems[0].at[3 * a + j],
            recv_sem=sems[1].at[3 * a + j], device_id=(px, py, c), device_id_type=MESH_IDS)
            for a in range(len(sums)) for j, (px, py) in enumerate(chips)]

    return _pairwise_carry(sums, 3, make_copies)


def _everyone_carry(arrays):
    def make_copies(ins, outs, sems):
        x, y, c, _ = _mesh_position()
        flip = lambda v, bit: 1 - v if bit else v
        return [pltpu.make_async_remote_copy(
            src_ref=ins[a], dst_ref=outs[a].at[r - 1], send_sem=sems[0].at[7 * a + r - 1], recv_sem=sems[1].at[7 * a + r - 1],
            device_id=(flip(x, r & 4), flip(y, r & 2), flip(c, r & 1)), device_id_type=MESH_IDS)
            for a in range(len(arrays)) for r in range(1, N_DEV)]

    carry = _pairwise_carry([jax.ShapeDtypeStruct((1,) + a.shape, a.dtype) for a in arrays], N_DEV - 1, make_copies)
    carry.inputs = list(arrays)
    return carry


def _sum_everyone(own, received, me, name, after=()):
    def body(me_ref, own_ref, r_ref, *refs):
        g = None
        for d in range(N_DEV):
            relation = jnp.bitwise_xor(d, me_ref[0])
            part = jnp.where(relation == 0, own_ref[...], r_ref[jnp.maximum(relation - 1, 0)])
            g = part if g is None else g + part
        refs[-1][...] = g

    whole = lambda shape: pl.BlockSpec(shape, lambda i, me_ref: (0,) * len(shape))
    return pl.pallas_call(
        body, name=name,
        grid_spec=pltpu.PrefetchScalarGridSpec(
            num_scalar_prefetch=1, grid=(1,), in_specs=[whole(own.shape), whole(received.shape)] + [HBM_SPEC] * len(after),
            out_specs=whole(own.shape)),
        out_shape=jax.ShapeDtypeStruct(own.shape, F32))(me, *_in_hbm(own, received), *after)


SEM_SPEC = pl.BlockSpec(memory_space=pltpu.SEMAPHORE)
DATAFLOW_EFFECT = pltpu.SideEffectType.DATAFLOW_SIDE_EFFECTING


def _exchange_start(carry, name, after=()):
    n = len(carry.inputs)
    lands = [lax.empty(s.shape, s.dtype) for s in carry.out_shapes]

    def body(*refs):
        first_out = 2 * n + len(after)
        srcs, zones, sems, token = refs[:n], refs[n:2 * n], refs[first_out:first_out + 2], refs[-1]
        carry.start(srcs, zones, sems)
        token[...] = jnp.zeros_like(token)

    outs = pl.pallas_call(
        body, name=name, in_specs=[HBM_SPEC] * (2 * n + len(after)),
        out_specs=[SEM_SPEC, SEM_SPEC] + [HBM_SPEC] * (2 * n) + [pl.BlockSpec(memory_space=pltpu.VMEM)],
        out_shape=list(carry.sems) + _hbm_out([jax.ShapeDtypeStruct(a.shape, a.dtype) for a in carry.inputs])
        + _hbm_out(carry.out_shapes) + [jax.ShapeDtypeStruct((SUBLANES, LANES), F32)],
        input_output_aliases={j: 2 + j for j in range(2 * n)},
        compiler_params=pltpu.CompilerParams(has_side_effects=DATAFLOW_EFFECT),
    )(*_in_hbm(*carry.inputs, *lands), *after)
    return outs[:-1], outs[-1]


def _exchange_wait(carry, in_flight, after, name):
    n = len(carry.inputs)
    sems, srcs, zones = in_flight[:2], in_flight[2:2 + n], in_flight[2 + n:]

    def body(*refs):
        src_refs, zone_refs, sem_refs = refs[:n], refs[n:2 * n], refs[2 * n:2 * n + 2]
        carry.finish(src_refs, zone_refs, sem_refs)

    outs = pl.pallas_call(
        body, name=name, in_specs=[HBM_SPEC] * (2 * n) + [SEM_SPEC, SEM_SPEC] + [HBM_SPEC] * len(after),
        out_specs=[HBM_SPEC] * (2 * n),
        out_shape=_hbm_out([jax.ShapeDtypeStruct(a.shape, a.dtype) for a in carry.inputs]) + _hbm_out(carry.out_shapes),
        input_output_aliases={j: j for j in range(2 * n)},
        compiler_params=pltpu.CompilerParams(has_side_effects=DATAFLOW_EFFECT),
    )(*srcs, *zones, *sems, *after)
    return list(outs[:n]), list(outs[n:])


def _add_sibling(grads8, recvs, place, row_tiles, name):
    k = len(grads8)
    g4 = [g.reshape(4, 2, *g.shape[1:]) for g in grads8]

    def body(place_ref, *refs):
        g_refs, r_refs, o_refs, ob_refs = (refs[j * k:(j + 1) * k] for j in range(4))
        own = pl.program_id(1) == place_ref[1]
        for g_ref, r_ref, o_ref, ob_ref in zip(g_refs, r_refs, o_refs, ob_refs):
            s = g_ref[0] + r_ref[...]
            ob_ref[...] = s.astype(BF16)

            @pl.when(own)
            def _(o_ref=o_ref, s=s):
                o_ref[...] = s[0]

    def blocks(make):
        return [make(g.shape[1] // row_tiles, g.shape[2]) for g in grads8]

    slot = lambda tr, C: pl.BlockSpec((1, tr, C), lambda r, ch, place_ref: (ch, r, 0))
    outs = pl.pallas_call(
        body, name=name,
        grid_spec=pltpu.PrefetchScalarGridSpec(
            num_scalar_prefetch=1, grid=(row_tiles, 4),
            in_specs=blocks(lambda tr, C: pl.BlockSpec((1, 1, tr, C), lambda r, ch, place_ref: (ch, place_ref[0], r, 0)))
            + blocks(slot),
            out_specs=blocks(lambda tr, C: pl.BlockSpec((tr, C), lambda r, ch, place_ref: (r, 0))) + blocks(slot)),
        out_shape=_hbm_out([jax.ShapeDtypeStruct(g.shape[1:], F32) for g in grads8]
                           + [jax.ShapeDtypeStruct((4,) + g.shape[1:], BF16) for g in grads8]),
        compiler_params=_params(("arbitrary", "arbitrary")),
    )(place, *_in_hbm(*g4, *recvs))
    return list(outs[:k]), list(outs[k:])


def _adam_math(w, g, m, v):
    m = ADAM_B1 * m + (1.0 - ADAM_B1) * g
    v = ADAM_B2 * v + (1.0 - ADAM_B2) * jnp.square(g)
    m_hat = m / (1.0 - ADAM_B1 ** ADAM_STEP)
    v_hat = v / (1.0 - ADAM_B2 ** ADAM_STEP)
    delta = -ADAM_LR * (m_hat / (jnp.sqrt(v_hat) + ADAM_EPS) + ADAM_WD * w)
    return delta, m, v


def _adam_big(ws, ms, vs, chip_sums, recvs, row_tiles, name, after=()):
    k = len(ws)

    def body(*refs):
        refs = refs[:5 * k] + refs[5 * k + len(after):]
        w_refs, m_refs, v_refs, s_refs, r_refs, g_refs, d_refs, nm_refs, nv_refs = (refs[j * k:(j + 1) * k] for j in range(9))
        for a in range(k):
            r_ref = r_refs[a]
            g = s_refs[a][...] + r_ref[0].astype(F32) + r_ref[1].astype(F32) + r_ref[2].astype(F32)
            g_refs[a][...] = g
            d_refs[a][...], nm_refs[a][...], nv_refs[a][...] = _adam_math(w_refs[a][...], g, m_refs[a][...], v_refs[a][...])

    def blocks(make):
        return [make(w.shape[0] // row_tiles, w.shape[1]) for w in ws]

    blk = lambda tr, C: pl.BlockSpec((tr, C), lambda r: (r, 0))
    outs = pl.pallas_call(
        body, name=name, grid=(row_tiles,),
        in_specs=blocks(blk) * 4 + blocks(lambda tr, C: pl.BlockSpec((3, tr, C), lambda r: (0, r, 0))) + [HBM_SPEC] * len(after),
        out_specs=blocks(blk) * 4,
        out_shape=[jax.ShapeDtypeStruct(w.shape, F32) for w in ws] * 4,
        compiler_params=_params(("arbitrary",)),
    )(*_in_hbm(*ws, *ms, *vs, *chip_sums, *recvs), *after)
    return [list(outs[j * k:(j + 1) * k]) for j in range(4)]


def _sum_partials(partials, name, after=()):
    def body(p_ref, *refs):
        g = p_ref[0]
        for d in range(1, partials.shape[0]):
            g = g + p_ref[d]
        refs[-1][...] = g

    return pl.pallas_call(body, name=name, grid=(1,), in_specs=[_whole(partials.shape)] + [HBM_SPEC] * len(after),
                          out_specs=_whole(partials.shape[1:]),
                          out_shape=jax.ShapeDtypeStruct(partials.shape[1:], F32))(*_in_hbm(partials), *after)


def _adam_small(ws, ms, vs, gs):
    n = len(ws)

    def body(*refs):
        w_refs, m_refs, v_refs, g_refs = (refs[i * n:(i + 1) * n] for i in range(4))
        d_refs, nm_refs, nv_refs = (refs[(4 + i) * n:(5 + i) * n] for i in range(3))
        for j in range(n):
            d_refs[j][...], nm_refs[j][...], nv_refs[j][...] = _adam_math(
                w_refs[j][...], g_refs[j][...], m_refs[j][...], v_refs[j][...])

    specs = [_whole(w.shape) for w in ws]
    outs = pl.pallas_call(body, name="adam_small", grid=(1,), in_specs=specs * 4, out_specs=specs * 3,
                          out_shape=[jax.ShapeDtypeStruct(w.shape, F32) for w in ws] * 3,
                          compiler_params=_params(("arbitrary",), VMEM_MID))(*_in_hbm(*ws, *ms, *vs, *gs))
    return outs[:n], outs[n:2 * n], outs[2 * n:]


PACK_QUANTUM = SUBLANES * LANES


def _pack(named, names):
    parts = []
    for nme in names:
        flat = named[nme].reshape(-1)
        parts.append(jnp.pad(flat, (0, -flat.size % PACK_QUANTUM)))
    return jnp.concatenate(parts).reshape(-1, LANES)


def _unpack(packed, shapes, names):
    flat = packed.reshape(-1)
    out, pos = {}, 0
    for nme in names:
        size = math.prod(shapes[nme])
        out[nme] = flat[pos:pos + size].reshape(shapes[nme])
        pos += size + (-size % PACK_QUANTUM)
    return out


BIG = ("w_in", "w_glu", "w_attn_branch", "w_ssm_branch", "w_out", "w_ff_in", "w_ff_out")
COLUMN_SHARDED = ("w_in", "w_attn_branch", "w_ssm_branch", "w_ff_in")
SMALL = ("norm_mix_pre", "norm_mix_post", "norm_mlp_pre", "norm_mlp_post", "rel_bias", "sinks", "lam_re", "lam_im",
         "log_dt", "b_re", "b_im", "c_re", "c_im", "d_skip")
SWAPPED_SMALL = ("rel_bias", "b_re", "b_im")
SMALL_LATE = ("norm_mix_pre", "rel_bias", "sinks", "loss")
SMALL_BEFORE_ATTN_BWD = tuple(n for n in SMALL if n not in SMALL_LATE)
ALL_WEIGHTS = ("norm_mix_pre", "norm_mix_post", "norm_mlp_pre", "norm_mlp_post", "w_in", "rel_bias", "sinks", "lam_re",
               "lam_im", "log_dt", "b_re", "b_im", "c_re", "c_im", "d_skip", "w_glu", "w_attn_branch", "w_ssm_branch",
               "w_out", "w_ff_in", "w_ff_out")


def _full_from_gathered(name, gathered):
    _, r, c = gathered.shape
    if name in COLUMN_SHARDED:
        return jnp.transpose(gathered, (1, 0, 2)).reshape(r, N_DEV * c)
    return gathered.reshape(N_DEV * r, c)


def _blocks_from_full(name, full):
    r, c = full.shape
    if name in COLUMN_SHARDED:
        return jnp.transpose(full.reshape(r, N_DEV, c // N_DEV), (1, 0, 2))
    return full.reshape(N_DEV, r // N_DEV, c)


def kernel(x, norm_mix_pre, norm_mix_post, norm_mlp_pre, norm_mlp_post, w_in, rel_bias, sinks, lam_re, lam_im, log_dt, b_re, b_im, c_re, c_im, d_skip, w_glu, w_attn_branch, w_ssm_branch, w_out, w_ff_in, w_ff_out, loss_target, m_norm_mix_pre, m_norm_mix_post, m_norm_mlp_pre, m_norm_mlp_post, m_w_in, m_rel_bias, m_sinks, m_lam_re, m_lam_im, m_log_dt, m_b_re, m_b_im, m_c_re, m_c_im, m_d_skip, m_w_glu, m_w_attn_branch, m_w_ssm_branch, m_w_out, m_w_ff_in, m_w_ff_out, v_norm_mix_pre, v_norm_mix_post, v_norm_mlp_pre, v_norm_mlp_post, v_w_in, v_rel_bias, v_sinks, v_lam_re, v_lam_im, v_log_dt, v_b_re, v_b_im, v_c_re, v_c_im, v_d_skip, v_w_glu, v_w_attn_branch, v_w_ssm_branch, v_w_out, v_w_ff_in, v_w_ff_out):
    args = dict(locals())
    w = {n: args[n] for n in ALL_WEIGHTS}
    m = {n: args["m_" + n] for n in ALL_WEIGHTS}
    v = {n: args["v_" + n] for n in ALL_WEIGHTS}
    core = lax.axis_index("c").astype(jnp.int32).reshape(1)
    chip = (2 * lax.axis_index("x") + lax.axis_index("y")).astype(jnp.int32).reshape(1)
    xs, target = x[0], loss_target[0]
    t = _tiles(xs.shape[0])
    local = lambda d, n: d[n][0].T if n == "w_in" else d[n][0]
    shard = {n: local(w, n).astype(BF16) for n in BIG}
    shard["w_ff_in"] = shard["w_ff_in"].T
    view = lambda n, a: jnp.swapaxes(a, -1, -2) if n in SWAPPED_SMALL else a
    small = {n: (view(n, w[n]) if n == "rel_bias" else view(n, w[n])[0]) for n in SMALL}
    g1, g2, g3, g4 = (small[n].reshape(1, D_MODEL) for n in ("norm_mix_pre", "norm_mix_post", "norm_mlp_pre", "norm_mlp_post"))
    bucket = jnp.asarray(_bucket_table())
    rel_b, sink = small["rel_bias"], small["sinks"].reshape(1, N_HEADS)
    lam_r, lam_i = small["lam_re"].reshape(1, STATES), small["lam_im"].reshape(1, STATES)
    ldt_rep = jnp.repeat(small["log_dt"].reshape(N_GROUPS), N_STATE).reshape(1, STATES)
    bd_re, bd_im = _block_diag_in(small["b_re"]), _block_diag_in(small["b_im"])
    cm_re, cm_im = _block_diag_out(small["c_re"]).astype(BF16), _block_diag_out(small["c_im"]).astype(BF16)
    dsk = small["d_skip"].reshape(1, SSM_W)

    (g_in,) = _run_carry(_gather_carry([shard["w_in"]]), "gather_w_in")
    wf_in = g_in.reshape(IN_W, D_MODEL)
    merge_names = ("w_glu", "w_attn_branch", "w_ssm_branch", "w_out")
    (q, k, vv, u, ga, gs, h), gathered = _in_proj_fwd(xs, g1, wf_in, t["proj"], _gather_carry([shard[n] for n in merge_names]))
    wf = {n: _full_from_gathered(n, g) for n, g in zip(merge_names, gathered)}
    (att,), (wf_ff_in,) = _attn_fwd(q, k, vv, bucket, rel_b, sink, _gather_carry([shard["w_ff_in"]]))
    a_re, a_im, bm_re, bm_im = _ssm_prep(lam_r, lam_i, ldt_rep, bd_re, bd_im)
    (y, h_re, h_im, in_re, in_im), (wf_ff_out,) = _ssm_fwd(
        u, a_re, a_im, bm_re, bm_im, cm_re, cm_im, dsk, t["ssm_chunk"], _gather_carry([shard["w_ff_out"]]))
    x1, o, h2 = _merge_fwd(xs, y, att, ga, gs, g2, g3, wf["w_glu"], wf["w_ssm_branch"], wf["w_attn_branch"], wf["w_out"],
                           t["merge"])
    a, dfo, dx2, loss_blk, dg4 = _mlp_fwd(h2, x1, target, g4, wf_ff_in, wf_ff_out, t["mlp_fwd"])

    groups = {"ff": 4, "merge": 1, "w_in": 2}

    def add_sibling(group, blocks, received):
        return _add_sibling(blocks, received, jnp.concatenate([core, chip]), groups[group], "add_sibling_" + group)

    ff_names = ("w_ff_in", "w_ff_out")
    dw_ff_in, dw_ff_out, da = _mlp_weight_grads(dfo, a, h2, wf_ff_out, t["mlp_bwd"])
    dh2 = _mlp_input_grad(da, wf_ff_in.reshape(D_FF, D_MODEL), t["mlp_bwd"])
    ff_blocks = [dw_ff_in, dw_ff_out]
    (dx1, dgates, datt, dy, dw_glu, dw_ssm, dw_attn, dw_out, dg2, dg3), ff_recv = _merge_bwd(
        dh2, dx2, x1, o, y, att, ga, gs, g2, g3, wf["w_glu"], wf["w_ssm_branch"], wf["w_attn_branch"], wf["w_out"],
        t["merge_bwd"], _sibling_carry(ff_blocks))
    ff_sums, ff_sums_bf = add_sibling("ff", ff_blocks, ff_recv)
    merge_blocks = [_blocks_from_full(n, g) for n, g in zip(merge_names, (dw_glu, dw_attn, dw_ssm, dw_out))]
    (du, dbm_re, dbm_im, dcm_re, dcm_im, da_re, da_im, dd_skip), carried = _ssm_bwd(
        dy, u, h_re, h_im, in_re, in_im, a_re, a_im, bm_re, bm_im, cm_re, cm_im, dsk, t["ssm_chunk"],
        _join(_chips_carry(ff_sums_bf), _sibling_carry(merge_blocks)))
    ff_from_chips, merge_recv = carried[:2], carried[2:]
    merge_sums, merge_sums_bf = add_sibling("merge", merge_blocks, merge_recv)
    dbd_re, dbd_im, dlam_re, dlam_im, dldt_rep = _ssm_prep_bwd(lam_r, lam_i, ldt_rep, bd_re, bd_im, dbm_re, dbm_im, da_re, da_im)
    dlog_dt = _group_sum(dldt_rep.reshape(N_GROUPS, N_STATE))
    shapes = {n: view(n, w[n]).shape for n in SMALL}
    shapes["loss"] = (1,)
    small_grads = dict(
        norm_mix_post=dg2, norm_mlp_pre=dg3, norm_mlp_post=dg4, lam_re=dlam_re, lam_im=dlam_im, log_dt=dlog_dt,
        b_re=_block_diag_in_grad(dbd_re), b_im=_block_diag_in_grad(dbd_im),
        c_re=_block_diag_out_grad(dcm_re), c_im=_block_diag_out_grad(dcm_im), d_skip=dd_skip)
    packed_early = _pack({n: small_grads[n].reshape(shapes[n]) for n in SMALL_BEFORE_ATTN_BWD}, SMALL_BEFORE_ATTN_BWD)
    (dq, dkv, attn_small), carried = _attn_bwd(
        q, k, vv, datt, bucket, rel_b, sink, _join(_chips_carry(merge_sums_bf), _gather_carry([packed_early])))
    merge_from_chips, partials_early = carried[:-1], carried[-1]

    dparts = (dq, dkv, du, dgates)
    dw_in_t = _in_proj_weight_grad(h, dparts)
    in_blocks = [dw_in_t.reshape(N_DEV, IN_W // N_DEV, D_MODEL)]
    to_sibling = _sibling_carry(in_blocks)
    in_flight, token = _exchange_start(to_sibling, "w_in_sibling_start")
    n_tiles = xs.shape[0] // t["proj_bwd"]
    (grad_x, dg1), _ = _in_proj_input_grad(xs, g1 + token[0:1, 0:1], wf_in, dx1, dparts, t["proj_bwd"], 0, n_tiles, "in_proj_input_grad")
    late = dict(norm_mix_pre=dg1, rel_bias=attn_small[:, :N_BUCKETS, 0], sinks=attn_small[:, N_BUCKETS, 0], loss=loss_blk[0:1, 0])
    packed_late = _pack({n: late[n].reshape(shapes[n]) for n in SMALL_LATE}, SMALL_LATE)
    to_everyone = _everyone_carry([packed_late])
    late_in_flight, late_started = _exchange_start(to_everyone, "late_grads_start")
    in_blocks, in_recv = _exchange_wait(to_sibling, in_flight, [late_started], "w_in_sibling_wait")
    in_sums, in_sums_bf = add_sibling("w_in", in_blocks, in_recv)
    to_chips = _chips_carry(in_sums_bf)
    in_flight, chips_started = _exchange_start(to_chips, "w_in_chips_start")
    (packed_late,), (late_received,) = _exchange_wait(to_everyone, late_in_flight, [chips_started], "late_grads_wait")

    grads, deltas, new_m, new_v = {}, {}, {}, {}

    def adam_group(group, names, sums, received, after=()):
        outs = _adam_big(*[[local(d, n) for n in names] for d in (w, m, v)], sums, received, groups[group],
                         "adam_" + group, after)
        for store, vals in zip((grads, deltas, new_m, new_v), outs):
            store.update({n: (o.T if n == "w_in" else o)[None] for n, o in zip(names, vals)})

    adam_group("ff", ff_names, ff_sums, ff_from_chips, [chips_started])
    adam_group("merge", merge_names, merge_sums, merge_from_chips, [chips_started])

    grads.update(_unpack(_sum_partials(partials_early, "sum_small_grads", [chips_started]), shapes, SMALL_BEFORE_ATTN_BWD))
    grads.update(_unpack(_sum_everyone(packed_late, late_received, 2 * chip + core, "sum_late_grads"), shapes, SMALL_LATE))
    loss = grads.pop("loss").reshape(())
    small_out = _adam_small(*[[view(n, d[n]) for n in SMALL] for d in (w, m, v)], [grads[n] for n in SMALL])
    for store, vals in zip((deltas, new_m, new_v), small_out):
        store.update(zip(SMALL, vals))
    for store in (grads, deltas, new_m, new_v):
        store.update({n: view(n, store[n]) for n in SWAPPED_SMALL})

    busy = [new_v["w_ff_out"], new_v["w_out"], deltas["norm_mix_pre"]]
    _, (in_from_chips,) = _exchange_wait(to_chips, in_flight, busy, "w_in_chips_wait")
    adam_group("w_in", ("w_in",), in_sums, [in_from_chips])

    return (loss, grad_x[None], *[grads[n] for n in ALL_WEIGHTS], *[deltas[n] for n in ALL_WEIGHTS],
            *[new_m[n] for n in ALL_WEIGHTS], *[new_v[n] for n in ALL_WEIGHTS])
```

```python
import math

import jax
import jax.numpy as jnp
import numpy as np
from jax import lax
from jax.experimental import pallas as pl
from jax.experimental.pallas import tpu as pltpu

F32 = jnp.float32
BF16 = jnp.bfloat16

D_MODEL = 1024
N_HEADS = 8
HEAD_DIM = 64
ATTN_W = 512
KV_W = 128
BLOCK = 128
N_BUCKETS = 32
SSM_W = 512
N_GROUPS = 32
N_STATE = 64
GROUP_CH = 16
STATES = N_GROUPS * N_STATE
D_FF = 4096
IN_W = 3328
SPLITS = (0, 512, 640, 768, 1280, 2304, 3328)
RMS_EPS = 1e-6
NEG_INF = -1e30
SUBLANES = 8
LANES = 128
SSM_LANE_BLOCK = 512
N_SSM_BLOCKS = STATES // SSM_LANE_BLOCK
GROUPS_PER_BLOCK = SSM_LANE_BLOCK // N_STATE
VMEM_BIG = 52 * 1024 * 1024
VMEM_MID = 40 * 1024 * 1024
VMEM_MAX = 60 * 1024 * 1024

ADAM_LR = 0.001
ADAM_B1 = 0.9
ADAM_B2 = 0.999
ADAM_EPS = 1e-08
ADAM_WD = 0.01
ADAM_STEP = 10

N_DEV = 8


def _dot(a, b):
    return jnp.dot(a, b, preferred_element_type=F32)


def _dot_nt(a, b):
    return lax.dot_general(a, b, (((1,), (1,)), ((), ())), preferred_element_type=F32)


def _dot_tn(a, b):
    return lax.dot_general(a, b, (((0,), (0,)), ((), ())), preferred_element_type=F32)


def _rms_scale(x):
    return lax.rsqrt(jnp.mean(x * x, axis=-1, keepdims=True) + RMS_EPS)


def _rms_bwd(dy, x, r, g):
    t = dy * g
    dx = r * t - x * (r * r * r) * jnp.mean(t * x, axis=-1, keepdims=True)
    dg = jnp.sum(dy * x * r, axis=0, keepdims=True)
    return dx, dg


def _const_spec(shape):
    nd = len(shape)
    return pl.BlockSpec(shape, lambda *_: (0,) * nd, pipeline_mode=pl.Buffered(1))


def _in_hbm(*arrays):
    return tuple(pltpu.with_memory_space_constraint(a, pltpu.HBM) for a in arrays)


def _hbm_out(shapes):
    if isinstance(shapes, (list, tuple)):
        return [_hbm_out(s) for s in shapes]
    return shapes if isinstance(shapes, pl.MemoryRef) else pltpu.HBM(shapes.shape, shapes.dtype)


def _whole(shape):
    nd = len(shape)
    return pl.BlockSpec(shape, lambda *_: (0,) * nd)


def _params(sem, vmem=None):
    return pltpu.CompilerParams(dimension_semantics=sem, vmem_limit_bytes=vmem)


MESH_IDS = pl.DeviceIdType.MESH
HBM_SPEC = pl.BlockSpec(memory_space=pl.ANY)


class _Carry:
    def __init__(self, inputs, out_shapes, sems, start, finish, middle=None):
        self.inputs, self.out_shapes, self.sems = list(inputs), list(out_shapes), list(sems)
        self.start, self.middle, self.finish = start, middle, finish


def _join(a, b):
    na_in, na_out, na_sem = len(a.inputs), len(a.out_shapes), len(a.sems)

    def both(phase):
        def run(ins, outs, sems):
            for carry, lo in ((a, True), (b, False)):
                part = (lambda seq, n: seq[:n] if lo else seq[n:])
                if getattr(carry, phase) is not None:
                    getattr(carry, phase)(part(ins, na_in), part(outs, na_out), part(sems, na_sem))
        return run

    middle = both("middle") if (a.middle or b.middle) else None
    return _Carry(a.inputs + b.inputs, a.out_shapes + b.out_shapes, a.sems + b.sems, both("start"), both("finish"), middle)


def _hosted_call(body, carry, edge, *, name, grid, in_specs, out_specs, out_shape, scratch_shapes, compiler_params, inputs):
    n_in, n_out = len(in_specs), len(out_specs)
    inputs = [a if s.memory_space == pltpu.SMEM else _in_hbm(a)[0] for a, s in zip(inputs, in_specs)]
    out_shape = _hbm_out(list(out_shape))
    if carry is None:
        outs = pl.pallas_call(body, name=name, grid=grid, in_specs=in_specs, out_specs=out_specs, out_shape=out_shape,
                              scratch_shapes=scratch_shapes, compiler_params=compiler_params)(*inputs)
        return list(outs), []
    c_in, c_out, c_sem = len(carry.inputs), len(carry.out_shapes), len(carry.sems)

    def wrapped(*refs):
        ins, refs = refs[:n_in], refs[n_in:]
        cins, refs = refs[:c_in], refs[c_in:]
        outs, refs = refs[:n_out], refs[n_out:]
        couts, refs = refs[:c_out], refs[c_out:]
        scratch, csems = refs[:len(refs) - c_sem], refs[len(refs) - c_sem:]
        first, middle, last = edge()

        @pl.when(first)
        def _():
            carry.start(cins, couts, csems)

        body(*ins, *outs, *scratch)

        if carry.middle is not None:
            @pl.when(middle)
            def _():
                carry.middle(cins, couts, csems)

        @pl.when(last)
        def _():
            carry.finish(cins, couts, csems)

    outs = pl.pallas_call(
        wrapped, name=name, grid=grid, in_specs=list(in_specs) + [HBM_SPEC] * c_in,
        out_specs=list(out_specs) + [HBM_SPEC] * c_out, out_shape=out_shape + _hbm_out(carry.out_shapes),
        scratch_shapes=list(scratch_shapes) + carry.sems, compiler_params=compiler_params)(*inputs, *_in_hbm(*carry.inputs))
    return list(outs[:n_out]), list(outs[n_out:])


def _pass_on_step(n_steps):
    return max(0, min((7 * n_steps) // 8, n_steps - 2))


def _edge_1d(n_steps, pass_on_last=False):
    middle = n_steps - 1 if pass_on_last else _pass_on_step(n_steps)
    return lambda: (pl.program_id(0) == 0, pl.program_id(0) == middle, pl.program_id(0) == n_steps - 1)


def _edge_2d(n0, n1):
    def edge():
        step = pl.program_id(0) * n1 + pl.program_id(1)
        return step == 0, step == _pass_on_step(n0 * n1), step == n0 * n1 - 1
    return edge


def _run_carry(carry, name):
    c_in, c_out = len(carry.inputs), len(carry.out_shapes)

    def body(*refs):
        ins, outs, sems = refs[:c_in], refs[c_in:c_in + c_out], refs[c_in + c_out:]
        carry.start(ins, outs, sems)
        if carry.middle is not None:
            carry.middle(ins, outs, sems)
        carry.finish(ins, outs, sems)

    return pl.pallas_call(body, name=name, in_specs=[HBM_SPEC] * c_in, out_specs=[HBM_SPEC] * c_out,
                          out_shape=_hbm_out(carry.out_shapes), scratch_shapes=carry.sems)(*_in_hbm(*carry.inputs))


def _in_proj_fwd(x, g1, w_in_t, tile, carry=None):
    T = x.shape[0]

    def body(x_ref, g_ref, w_ref, q_ref, k_ref, v_ref, u_ref, ga_ref, gs_ref, h_ref):
        xv = x_ref[...]
        h = (xv * _rms_scale(xv) * g_ref[...]).astype(BF16)
        h_ref[...] = h
        outs = (q_ref, k_ref, v_ref, u_ref, ga_ref, gs_ref)
        for p, o_ref in enumerate(outs):
            o_ref[...] = _dot_nt(h, w_ref[SPLITS[p]:SPLITS[p + 1], :]).astype(o_ref.dtype)

    widths = [SPLITS[p + 1] - SPLITS[p] for p in range(6)] + [D_MODEL]
    dtypes = [BF16, BF16, BF16, F32, F32, F32, BF16]
    return _hosted_call(
        body, carry, _edge_1d(T // tile), name="in_proj_fwd", grid=(T // tile,),
        in_specs=[pl.BlockSpec((tile, D_MODEL), lambda i: (i, 0)), _const_spec((1, D_MODEL)), _const_spec((IN_W, D_MODEL))],
        out_specs=[pl.BlockSpec((tile, w), lambda i: (i, 0)) for w in widths],
        out_shape=[jax.ShapeDtypeStruct((T, w), dt) for w, dt in zip(widths, dtypes)],
        scratch_shapes=[], compiler_params=_params(("arbitrary",), VMEM_MID), inputs=(x, g1, w_in_t))


PROJ_PARTS = (512, 256, 512, 2048)
PROJ_GRAD_BLOCK = 256


def _in_proj_weight_grad(h, dparts):
    T = h.shape[0]
    blocks = [wd // PROJ_GRAD_BLOCK for wd in PROJ_PARTS]
    starts = [sum(blocks[:p]) for p in range(len(blocks))]

    def body(h_ref, *refs):
        part_refs, o_ref = refs[:-1], refs[-1]
        j = pl.program_id(0)
        for p_ref, start, count in zip(part_refs, starts, blocks):
            @pl.when((j >= start) & (j < start + count))
            def _(p_ref=p_ref):
                o_ref[...] = _dot_tn(p_ref[...], h_ref[...])

    def part_spec(start, count):
        return pl.BlockSpec((T, PROJ_GRAD_BLOCK), lambda j: (0, jnp.clip(j - start, 0, count - 1)))

    return pl.pallas_call(
        body, name="in_proj_weight_grad", grid=(sum(blocks),),
        in_specs=[_const_spec((T, D_MODEL))] + [part_spec(s, c) for s, c in zip(starts, blocks)],
        out_specs=pl.BlockSpec((PROJ_GRAD_BLOCK, D_MODEL), lambda j: (j, 0)),
        out_shape=_hbm_out(jax.ShapeDtypeStruct((IN_W, D_MODEL), F32)),
        compiler_params=_params(("arbitrary",), VMEM_MID),
    )(*_in_hbm(h, *dparts))


def _in_proj_input_grad(x, g1, w_in_t, dx1, dparts, tile, first_tile, n_tiles, name, carry=None):
    offsets = [sum(PROJ_PARTS[:p]) for p in range(len(PROJ_PARTS))]

    def body(x_ref, g_ref, w_ref, dx1_ref, *refs):
        part_refs, (gx_ref, dg_ref) = refs[:len(PROJ_PARTS)], refs[len(PROJ_PARTS):]
        i = pl.program_id(0)
        xv = x_ref[...]
        r = _rms_scale(xv)
        g = g_ref[...]
        dh = sum(_dot(p_ref[...], w_ref[off:off + wd, :]) for p_ref, off, wd in zip(part_refs, offsets, PROJ_PARTS))
        dxn, dg = _rms_bwd(dh, xv, r, g)
        gx_ref[...] = dx1_ref[...] + dxn

        @pl.when(i == 0)
        def _():
            dg_ref[...] = dg

        @pl.when(i > 0)
        def _():
            dg_ref[...] += dg

    tok = lambda wd: pl.BlockSpec((tile, wd), lambda i: (i + first_tile, 0))
    return _hosted_call(
        body, carry, _edge_1d(n_tiles), name=name, grid=(n_tiles,),
        in_specs=[tok(D_MODEL), _const_spec((1, D_MODEL)), _const_spec((IN_W, D_MODEL)), tok(D_MODEL)] + [tok(wd) for wd in PROJ_PARTS],
        out_specs=[pl.BlockSpec((tile, D_MODEL), lambda i: (i, 0)), pl.BlockSpec((1, D_MODEL), lambda i: (0, 0))],
        out_shape=[jax.ShapeDtypeStruct((n_tiles * tile, D_MODEL), F32), jax.ShapeDtypeStruct((1, D_MODEL), F32)],
        scratch_shapes=[], compiler_params=_params(("arbitrary",), VMEM_MID), inputs=(x, g1, w_in_t, dx1, *dparts))


def _bucket_table():
    qi = np.arange(BLOCK)[:, None]
    kj = np.arange(2 * BLOCK)[None, :]
    dist = qi + BLOCK - kj
    max_exact = N_BUCKETS // 2
    d = np.maximum(dist, 0)
    df = np.maximum(d, 1).astype(np.float32)
    large = max_exact + (np.log(df / np.float32(max_exact)) / np.float32(math.log(BLOCK / max_exact))
                         * np.float32(N_BUCKETS - max_exact)).astype(np.int32)
    large = np.minimum(large, N_BUCKETS - 1)
    bucket = np.where(d < max_exact, d, large)
    return np.where((dist >= 0) & (dist < BLOCK), bucket, -1).astype(np.int32)


def _build_bias(bucket_ref, rb_ref, bias_ref):
    bk = bucket_ref[...]
    for h in range(N_HEADS):
        def add(b, acc, h=h):
            return acc + jnp.where(bk == b, rb_ref[h, b], 0.0)
        bias_ref[h] = lax.fori_loop(0, N_BUCKETS, add, jnp.zeros((BLOCK, 2 * BLOCK), F32))


def _kv_variants(prev_ref, cur_ref):
    cat = jnp.concatenate([prev_ref[...], cur_ref[...]], axis=0)
    lo = lax.broadcasted_iota(jnp.int32, cat.shape, 1) < HEAD_DIM
    zero = jnp.zeros_like(cat)
    head0_lo = jnp.where(lo, cat, zero)
    head1_hi = jnp.where(lo, zero, cat)
    return ((head0_lo, pltpu.roll(head0_lo, HEAD_DIM, 1)), (pltpu.roll(head1_hi, HEAD_DIM, 1), head1_hi))


def _merge_kv_grads(g):
    lo = lax.broadcasted_iota(jnp.int32, g[0][0].shape, 1) < HEAD_DIM
    return jnp.where(lo, g[0][0] + pltpu.roll(g[0][1], HEAD_DIM, 1), g[1][1] + pltpu.roll(g[1][0], HEAD_DIM, 1))


def _head_lanes(h):
    return slice((h // 2) * LANES, (h // 2 + 1) * LANES)


def _attn_probs(q_ref, kvar, bias_ref, sk_ref, valid, s_ref):
    for h in range(N_HEADS):
        s_ref[h] = _dot_nt(q_ref[:, _head_lanes(h)], kvar[h // 4][h % 2])
    head = lax.broadcasted_iota(jnp.int32, (N_HEADS, 1, 1), 0)
    sink = jnp.zeros((N_HEADS, 1, 1), F32)
    for h in range(N_HEADS):
        sink = jnp.where(head == h, sk_ref[0, h], sink)
    s = jnp.where(valid[None], s_ref[...] * (HEAD_DIM ** -0.5) + bias_ref[...], NEG_INF)
    m = jnp.maximum(jnp.max(s, axis=-1, keepdims=True), sink)
    p = jnp.exp(s - m)
    e_sink = jnp.exp(sink - m)
    inv = 1.0 / (jnp.sum(p, axis=-1, keepdims=True) + e_sink)
    return p * inv, e_sink * inv


def _attn_valid(bucket_ref, n):
    col = lax.broadcasted_iota(jnp.int32, (BLOCK, 2 * BLOCK), 1)
    return (bucket_ref[...] >= 0) & ((n > 0) | (col >= BLOCK))


def _attn_fwd(q, k, v, bucket, rel_bias, sinks, carry=None):
    T = q.shape[0]
    nb = T // BLOCK

    def body(q_ref, kc_ref, kp_ref, vc_ref, vp_ref, bucket_ref, rb_ref, sk_ref, o_ref, bias_ref, s_ref, p_ref):
        n = pl.program_id(0)

        @pl.when(n == 0)
        def _():
            _build_bias(bucket_ref, rb_ref, bias_ref)

        kvar = _kv_variants(kp_ref, kc_ref)
        vvar = _kv_variants(vp_ref, vc_ref)
        pr, _ = _attn_probs(q_ref, kvar, bias_ref, sk_ref, _attn_valid(bucket_ref, n), s_ref)
        p_ref[...] = pr.astype(BF16)
        for m in range(N_HEADS // 2):
            acc = _dot(p_ref[2 * m], vvar[m // 2][0]) + _dot(p_ref[2 * m + 1], vvar[m // 2][1])
            o_ref[:, m * LANES:(m + 1) * LANES] = acc.astype(o_ref.dtype)

    cur = lambda w: pl.BlockSpec((BLOCK, w), lambda n: (n, 0))
    prev = lambda w: pl.BlockSpec((BLOCK, w), lambda n: (jnp.maximum(n - 1, 0), 0))
    smem = pl.BlockSpec(memory_space=pltpu.SMEM)
    return _hosted_call(
        body, carry, _edge_1d(nb, pass_on_last=True), name="attn_fwd", grid=(nb,),
        in_specs=[cur(ATTN_W), cur(KV_W), prev(KV_W), cur(KV_W), prev(KV_W), _const_spec((BLOCK, 2 * BLOCK)), smem, smem],
        out_specs=[cur(ATTN_W)],
        out_shape=[jax.ShapeDtypeStruct((T, ATTN_W), BF16)],
        scratch_shapes=[pltpu.VMEM((N_HEADS, BLOCK, 2 * BLOCK), F32), pltpu.VMEM((N_HEADS, BLOCK, 2 * BLOCK), F32),
                        pltpu.VMEM((N_HEADS, BLOCK, 2 * BLOCK), BF16)],
        compiler_params=_params(("arbitrary",)), inputs=(q, k, k, v, v, bucket, rel_bias, sinks))


ATTN_SMALL_ROWS = N_BUCKETS + SUBLANES


def _attn_bwd(q, k, v, datt, bucket, rel_bias, sinks, carry=None):
    T = q.shape[0]
    nb = T // BLOCK

    def body(q_ref, do_ref, kc_ref, kp_ref, vc_ref, vp_ref, bucket_ref, rb_ref, sk_ref,
             dq_ref, dkv_ref, small_ref, bias_ref, ds_sum_ref, dsink_ref, kcarry_ref, vcarry_ref,
             s_ref, dp_ref, p_ref, dsc_ref):
        n = pl.program_id(0)

        @pl.when(n == 0)
        def _():
            _build_bias(bucket_ref, rb_ref, bias_ref)
            ds_sum_ref[...] = jnp.zeros_like(ds_sum_ref)
            dsink_ref[...] = jnp.zeros_like(dsink_ref)
            kcarry_ref[...] = jnp.zeros_like(kcarry_ref)
            vcarry_ref[...] = jnp.zeros_like(vcarry_ref)

        @pl.when(n < nb)
        def _():
            kvar = _kv_variants(kp_ref, kc_ref)
            vvar = _kv_variants(vp_ref, vc_ref)
            pr, p_sink = _attn_probs(q_ref, kvar, bias_ref, sk_ref, _attn_valid(bucket_ref, n), s_ref)
            for h in range(N_HEADS):
                dp_ref[h] = _dot_nt(do_ref[:, _head_lanes(h)], vvar[h // 4][h % 2])
            dp = dp_ref[...]
            dsum = jnp.sum(pr * dp, axis=-1, keepdims=True)
            ds = pr * (dp - dsum)
            ds_sum_ref[...] += ds
            dsink_ref[...] -= jnp.sum(p_sink * dsum, axis=1, keepdims=True)
            dsc_ref[...] = (ds * (HEAD_DIM ** -0.5)).astype(BF16)
            p_ref[...] = pr.astype(BF16)
            for m in range(N_HEADS // 2):
                dqm = _dot(dsc_ref[2 * m], kvar[m // 2][0]) + _dot(dsc_ref[2 * m + 1], kvar[m // 2][1])
                dq_ref[:, m * LANES:(m + 1) * LANES] = dqm.astype(dq_ref.dtype)
            dk_var = [[None, None], [None, None]]
            dv_var = [[None, None], [None, None]]
            for kvh in range(2):
                for e in range(2):
                    heads = [h for h in range(N_HEADS) if h // 4 == kvh and h % 2 == e]
                    dk_var[kvh][e] = sum(_dot_tn(dsc_ref[h], q_ref[:, _head_lanes(h)]) for h in heads)
                    dv_var[kvh][e] = sum(_dot_tn(p_ref[h], do_ref[:, _head_lanes(h)]) for h in heads)
            dk_cat = _merge_kv_grads(dk_var)
            dv_cat = _merge_kv_grads(dv_var)

            @pl.when(n > 0)
            def _():
                dkv_ref[:, :KV_W] = (kcarry_ref[...] + dk_cat[:BLOCK]).astype(BF16)
                dkv_ref[:, KV_W:] = (vcarry_ref[...] + dv_cat[:BLOCK]).astype(BF16)

            kcarry_ref[...] = dk_cat[BLOCK:]
            vcarry_ref[...] = dv_cat[BLOCK:]

        @pl.when(n == nb)
        def _():
            dkv_ref[:, :KV_W] = kcarry_ref[...].astype(BF16)
            dkv_ref[:, KV_W:] = vcarry_ref[...].astype(BF16)
            bk = bucket_ref[...]
            row = lax.broadcasted_iota(jnp.int32, (N_HEADS, ATTN_SMALL_ROWS, LANES), 1)

            def add(b, acc):
                masked = jnp.where((bk == b)[None], ds_sum_ref[...], 0.0)
                val = jnp.sum(jnp.sum(masked, axis=1, keepdims=True), axis=2, keepdims=True)
                return acc + jnp.where(row == b, val, 0.0)

            small_ref[...] = lax.fori_loop(0, N_BUCKETS, add, jnp.where(row == N_BUCKETS, dsink_ref[...], 0.0))

    last = nb - 1
    cur = lambda w: pl.BlockSpec((BLOCK, w), lambda n: (jnp.minimum(n, last), 0))
    prev = lambda w: pl.BlockSpec((BLOCK, w), lambda n: (jnp.clip(n - 1, 0, last), 0))
    smem = pl.BlockSpec(memory_space=pltpu.SMEM)
    return _hosted_call(
        body, carry, _edge_1d(nb + 1), name="attn_bwd", grid=(nb + 1,),
        in_specs=[cur(ATTN_W), cur(ATTN_W), cur(KV_W), prev(KV_W), cur(KV_W), prev(KV_W),
                  _const_spec((BLOCK, 2 * BLOCK)), smem, smem],
        out_specs=[cur(ATTN_W), prev(2 * KV_W), pl.BlockSpec((N_HEADS, ATTN_SMALL_ROWS, LANES), lambda n: (0, 0, 0))],
        out_shape=[jax.ShapeDtypeStruct((T, ATTN_W), BF16), jax.ShapeDtypeStruct((T, 2 * KV_W), BF16),
                   jax.ShapeDtypeStruct((N_HEADS, ATTN_SMALL_ROWS, LANES), F32)],
        scratch_shapes=[pltpu.VMEM((N_HEADS, BLOCK, 2 * BLOCK), F32), pltpu.VMEM((N_HEADS, BLOCK, 2 * BLOCK), F32),
                        pltpu.VMEM((N_HEADS, 1, 1), F32), pltpu.VMEM((BLOCK, KV_W), F32), pltpu.VMEM((BLOCK, KV_W), F32),
                        pltpu.VMEM((N_HEADS, BLOCK, 2 * BLOCK), F32), pltpu.VMEM((N_HEADS, BLOCK, 2 * BLOCK), F32),
                        pltpu.VMEM((N_HEADS, BLOCK, 2 * BLOCK), BF16), pltpu.VMEM((N_HEADS, BLOCK, 2 * BLOCK), BF16)],
        compiler_params=_params(("arbitrary",)), inputs=(q, datt, k, k, v, v, bucket, rel_bias, sinks))


SCAN_UNROLL = 4


def _cmul(ar, ai, br, bi):
    return ar * br - ai * bi, ar * bi + ai * br


def _cmul_conj(ar, ai, br, bi):
    return ar * br + ai * bi, ar * bi - ai * br


def _ssm_discretize(lr, li, ldt):
    dt = jnp.exp(ldt)
    mag = jnp.exp(lr * dt)
    ab_re = mag * jnp.cos(li * dt)
    ab_im = mag * jnp.sin(li * dt)
    nr = ab_re - 1.0
    den = lr * lr + li * li
    f_re = (nr * lr + ab_im * li) / den
    f_im = (ab_im * lr - nr * li) / den
    return ab_re, ab_im, f_re, f_im


def _ssm_prep(lam_re, lam_im, ldt_rep, bd_re, bd_im):
    def body(lr_ref, li_ref, ldt_ref, bdr_ref, bdi_ref, ar_ref, ai_ref, br_ref, bi_ref):
        ab_re, ab_im, f_re, f_im = _ssm_discretize(lr_ref[...], li_ref[...], ldt_ref[...])
        ar_ref[...] = ab_re
        ai_ref[...] = ab_im
        bdr, bdi = bdr_ref[0], bdi_ref[0]
        br_ref[0] = (bdr * f_re - bdi * f_im).astype(BF16)
        bi_ref[0] = (bdi * f_re + bdr * f_im).astype(BF16)

    row = pl.BlockSpec((1, SSM_LANE_BLOCK), lambda j: (0, j))
    mat = pl.BlockSpec((1, LANES, SSM_LANE_BLOCK), lambda j: (j, 0, 0))
    return pl.pallas_call(
        body, name="ssm_prep", grid=(N_SSM_BLOCKS,),
        in_specs=[row, row, row, mat, mat], out_specs=[row, row, mat, mat],
        out_shape=[jax.ShapeDtypeStruct((1, STATES), F32)] * 2 + [jax.ShapeDtypeStruct((N_SSM_BLOCKS, LANES, SSM_LANE_BLOCK), BF16)] * 2,
        compiler_params=_params(("arbitrary",)),
    )(*_in_hbm(lam_re, lam_im, ldt_rep, bd_re, bd_im))


def _ssm_prep_bwd(lam_re, lam_im, ldt_rep, bd_re, bd_im, dbr, dbi, da_re, da_im):
    def body(lr_ref, li_ref, ldt_ref, bdr_ref, bdi_ref, dbr_ref, dbi_ref, dar_ref, dai_ref,
             dbdr_ref, dbdi_ref, dlr_ref, dli_ref, dldt_ref):
        lr, li, ldt = lr_ref[...], li_ref[...], ldt_ref[...]
        (_, _, f_re, f_im), vjp = jax.vjp(_ssm_discretize, lr, li, ldt)
        bdr, bdi, gbr, gbi = bdr_ref[0], bdi_ref[0], dbr_ref[0], dbi_ref[0]
        dbdr_ref[0] = gbr * f_re + gbi * f_im
        dbdi_ref[0] = gbi * f_re - gbr * f_im
        df_re = jnp.sum(gbr * bdr + gbi * bdi, axis=0, keepdims=True)
        df_im = jnp.sum(gbi * bdr - gbr * bdi, axis=0, keepdims=True)
        dlr, dli, dldt = vjp((dar_ref[...], dai_ref[...], df_re, df_im))
        dlr_ref[...] = dlr
        dli_ref[...] = dli
        dldt_ref[...] = dldt

    row = pl.BlockSpec((1, SSM_LANE_BLOCK), lambda j: (0, j))
    mat = pl.BlockSpec((1, LANES, SSM_LANE_BLOCK), lambda j: (j, 0, 0))
    mat_shape = jax.ShapeDtypeStruct((N_SSM_BLOCKS, LANES, SSM_LANE_BLOCK), F32)
    row_shape = jax.ShapeDtypeStruct((1, STATES), F32)
    return pl.pallas_call(
        body, name="ssm_prep_bwd", grid=(N_SSM_BLOCKS,),
        in_specs=[row, row, row, mat, mat, mat, mat, row, row], out_specs=[mat, mat, row, row, row],
        out_shape=[mat_shape, mat_shape, row_shape, row_shape, row_shape],
        compiler_params=_params(("arbitrary",)),
    )(*_in_hbm(lam_re, lam_im, ldt_rep, bd_re, bd_im, dbr, dbi, da_re, da_im))


def _group_sum(x):
    def body(x_ref, o_ref):
        o_ref[...] = jnp.sum(x_ref[...], axis=1, keepdims=True)
    return pl.pallas_call(body, name="ssm_group_sum", grid=(1,), in_specs=[_whole(x.shape)], out_specs=_whole((N_GROUPS, 1)),
                          out_shape=jax.ShapeDtypeStruct((N_GROUPS, 1), F32))(*_in_hbm(x))


def _power_table(ar, ai, p_re_ref, p_im_ref, steps):
    shape = (SUBLANES, SSM_LANE_BLOCK)
    p_re_ref[0:SUBLANES] = jnp.broadcast_to(ar, shape)
    p_im_ref[0:SUBLANES] = jnp.broadcast_to(ai, shape)
    m = 1
    while m < steps:
        rows = m * SUBLANES
        top_re = p_re_ref[rows - SUBLANES:rows]
        top_im = p_im_ref[rows - SUBLANES:rows]
        cur_re = p_re_ref[0:rows].reshape(m, SUBLANES, SSM_LANE_BLOCK)
        cur_im = p_im_ref[0:rows].reshape(m, SUBLANES, SSM_LANE_BLOCK)
        nxt_re, nxt_im = _cmul(cur_re, cur_im, top_re[None], top_im[None])
        p_re_ref[rows:2 * rows] = nxt_re.reshape(rows, SSM_LANE_BLOCK)
        p_im_ref[rows:2 * rows] = nxt_im.reshape(rows, SSM_LANE_BLOCK)
        m *= 2


def _to_segments(src_ref, dst_ref, steps):
    for s in range(SUBLANES):
        dst_ref[pl.ds(s, steps, stride=SUBLANES), :] = src_ref[s * steps:(s + 1) * steps, :]


def _from_segments(src_ref, dst_ref, steps):
    for s in range(SUBLANES):
        dst_ref[s * steps:(s + 1) * steps, :] = src_ref[pl.ds(s, steps, stride=SUBLANES), :]


def _segment_carries(e_re, e_im, an_re, an_im, c_re, c_im, reverse):
    order = range(SUBLANES - 1, -1, -1) if reverse else range(SUBLANES)
    ins_re, ins_im = [None] * SUBLANES, [None] * SUBLANES
    for s in order:
        ins_re[s], ins_im[s] = c_re, c_im
        pr, pi = _cmul(an_re, an_im, c_re, c_im)
        c_re = e_re[s:s + 1] + pr
        c_im = e_im[s:s + 1] + pi
    return jnp.concatenate(ins_re, axis=0), jnp.concatenate(ins_im, axis=0), c_re, c_im


def _ssm_fwd(u, a_re, a_im, b_re, b_im, c_re, c_im, d_skip, chunk, carry=None):
    T = u.shape[0]
    nc = T // chunk
    steps = chunk // SUBLANES
    blk = SSM_LANE_BLOCK

    def body(u_ref, ar_ref, ai_ref, br_ref, bi_ref, cr_ref, ci_ref, dk_ref,
             y_ref, hr_ref, hi_ref, inr_ref, ini_ref, useg_ref, yseg_ref, pr_ref, pi_ref, carry_ref):
        c = pl.program_id(1)
        ar, ai = ar_ref[...], ai_ref[...]

        @pl.when(c == 0)
        def _():
            _power_table(ar, ai, pr_ref, pi_ref, steps)
            carry_ref[...] = jnp.zeros_like(carry_ref)

        _to_segments(u_ref, useg_ref, steps)
        ub = useg_ref[...].astype(BF16)
        hr_ref[...] = _dot(ub, br_ref[0])
        hi_ref[...] = _dot(ub, bi_ref[0])
        first = slice(0, SUBLANES)

        def scan(t4, prev):
            for j in range(SCAN_UNROLL):
                rows = pl.ds(pl.multiple_of((t4 * SCAN_UNROLL + j) * SUBLANES, SUBLANES), SUBLANES)
                pr, pi = _cmul(pr_ref[first, :], pi_ref[first, :], prev[0], prev[1])
                prev = (pr + hr_ref[rows, :], pi + hi_ref[rows, :])
                hr_ref[rows, :] = prev[0]
                hi_ref[rows, :] = prev[1]
            return prev

        zero = jnp.zeros((SUBLANES, blk), F32)
        lax.fori_loop(0, steps // SCAN_UNROLL, scan, (zero, zero))

        top = slice(chunk - SUBLANES, chunk)
        in_re, in_im, out_re, out_im = _segment_carries(
            hr_ref[top, :], hi_ref[top, :], pr_ref[top, :][0:1], pi_ref[top, :][0:1],
            carry_ref[0:1, :], carry_ref[1:2, :], reverse=False)
        carry_ref[0:1, :] = out_re
        carry_ref[1:2, :] = out_im
        inr_ref[...] = in_re
        ini_ref[...] = in_im

        def fix(t4, _):
            for j in range(SCAN_UNROLL):
                rows = pl.ds(pl.multiple_of((t4 * SCAN_UNROLL + j) * SUBLANES, SUBLANES), SUBLANES)
                fr, fi = _cmul(pr_ref[rows, :], pi_ref[rows, :], in_re, in_im)
                hr_ref[rows, :] += fr
                hi_ref[rows, :] += fi
            return 0

        lax.fori_loop(0, steps // SCAN_UNROLL, fix, 0)

        yseg_ref[...] = _dot(hr_ref[...].astype(BF16), cr_ref[0]) - _dot(hi_ref[...].astype(BF16), ci_ref[0])
        _from_segments(yseg_ref, y_ref, steps)
        y_ref[...] += dk_ref[...] * u_ref[...]

    row = pl.BlockSpec((1, blk), lambda j, c: (0, j))
    b_mat = pl.BlockSpec((1, LANES, blk), lambda j, c: (j, 0, 0))
    c_mat = pl.BlockSpec((1, blk, LANES), lambda j, c: (j, 0, 0))
    tok = pl.BlockSpec((chunk, LANES), lambda j, c: (c, j))
    state = pl.BlockSpec((chunk, blk), lambda j, c: (c, j))
    enter = pl.BlockSpec((SUBLANES, blk), lambda j, c: (c, j))
    return _hosted_call(
        body, carry, _edge_2d(N_SSM_BLOCKS, nc), name="ssm_fwd", grid=(N_SSM_BLOCKS, nc),
        in_specs=[tok, row, row, b_mat, b_mat, c_mat, c_mat, pl.BlockSpec((1, LANES), lambda j, c: (0, j))],
        out_specs=[tok, state, state, enter, enter],
        out_shape=[jax.ShapeDtypeStruct((T, SSM_W), F32), jax.ShapeDtypeStruct((T, STATES), F32),
                   jax.ShapeDtypeStruct((T, STATES), F32), jax.ShapeDtypeStruct((nc * SUBLANES, STATES), F32),
                   jax.ShapeDtypeStruct((nc * SUBLANES, STATES), F32)],
        scratch_shapes=[pltpu.VMEM((chunk, LANES), F32), pltpu.VMEM((chunk, LANES), F32),
                        pltpu.VMEM((chunk, blk), F32), pltpu.VMEM((chunk, blk), F32), pltpu.VMEM((SUBLANES, blk), F32)],
        compiler_params=_params(("arbitrary", "arbitrary"), VMEM_MID),
        inputs=(u, a_re, a_im, b_re, b_im, c_re, c_im, d_skip))


def _ssm_bwd(dy, u, h_re, h_im, in_re, in_im, a_re, a_im, b_re, b_im, c_re, c_im, d_skip, chunk, carry=None):
    T = u.shape[0]
    nc = T // chunk
    steps = chunk // SUBLANES
    blk = SSM_LANE_BLOCK

    def body(dy_ref, u_ref, hr_ref, hi_ref, inr_ref, ini_ref, ar_ref, ai_ref, br_ref, bi_ref, cr_ref, ci_ref, dk_ref,
             du_ref, dbr_ref, dbi_ref, dcr_ref, dci_ref, dar_ref, dai_ref, ddk_ref,
             dyseg_ref, useg_ref, duseg_ref, gr_ref, gi_ref, pr_ref, pi_ref, carry_ref, accr_ref, acci_ref):
        c = pl.program_id(1)
        ar, ai = ar_ref[...], ai_ref[...]

        @pl.when(c == 0)
        def _():
            _power_table(ar, ai, pr_ref, pi_ref, steps)
            carry_ref[...] = jnp.zeros_like(carry_ref)
            accr_ref[...] = jnp.zeros_like(accr_ref)
            acci_ref[...] = jnp.zeros_like(acci_ref)

        _to_segments(dy_ref, dyseg_ref, steps)
        _to_segments(u_ref, useg_ref, steps)
        dyb = dyseg_ref[...].astype(BF16)
        ub = useg_ref[...].astype(BF16)
        gr_ref[...] = _dot_nt(dyb, cr_ref[0])
        gi_ref[...] = -_dot_nt(dyb, ci_ref[0])
        dcr = _dot_tn(hr_ref[...].astype(BF16), dyb)
        dci = -_dot_tn(hi_ref[...].astype(BF16), dyb)
        ddk = jnp.sum(dy_ref[...] * u_ref[...], axis=0, keepdims=True)

        first = slice(0, SUBLANES)

        def scan(k4, nxt):
            for j in range(SCAN_UNROLL):
                t = steps - 1 - (k4 * SCAN_UNROLL + j)
                rows = pl.ds(pl.multiple_of(t * SUBLANES, SUBLANES), SUBLANES)
                pr, pi = _cmul_conj(pr_ref[first, :], pi_ref[first, :], nxt[0], nxt[1])
                nxt = (pr + gr_ref[rows, :], pi + gi_ref[rows, :])
                gr_ref[rows, :] = nxt[0]
                gi_ref[rows, :] = nxt[1]
            return nxt

        top = slice(chunk - SUBLANES, chunk)
        zero = jnp.zeros((SUBLANES, blk), F32)
        lax.fori_loop(0, steps // SCAN_UNROLL, scan, (zero, zero))

        gin_re, gin_im, out_re, out_im = _segment_carries(
            gr_ref[0:SUBLANES, :], gi_ref[0:SUBLANES, :], pr_ref[top, :][0:1], -pi_ref[top, :][0:1],
            carry_ref[0:1, :], carry_ref[1:2, :], reverse=True)
        carry_ref[0:1, :] = out_re
        carry_ref[1:2, :] = out_im

        def fix_row(rows, prow, hp_re, hp_im, acc):
            fr, fi = _cmul_conj(pr_ref[prow, :], pi_ref[prow, :], gin_re, gin_im)
            g_re = gr_ref[rows, :] + fr
            g_im = gi_ref[rows, :] + fi
            gr_ref[rows, :] = g_re
            gi_ref[rows, :] = g_im
            return acc[0] + g_re * hp_re + g_im * hp_im, acc[1] + g_im * hp_re - g_re * hp_im

        def fix_at(t, acc):
            aligned = (lambda r: r * SUBLANES) if isinstance(t, int) else (lambda r: pl.multiple_of(r * SUBLANES, SUBLANES))
            rows, before, prow = (pl.ds(aligned(r), SUBLANES) for r in (t, t - 1, steps - 1 - t))
            return fix_row(rows, prow, hr_ref[before, :], hi_ref[before, :], acc)

        def fix(t4, acc):
            for j in range(SCAN_UNROLL):
                acc = fix_at(t4 * SCAN_UNROLL + j, acc)
            return acc

        acc = fix_row(first, top, inr_ref[...], ini_ref[...], (accr_ref[...], acci_ref[...]))
        for t in range(1, SCAN_UNROLL):
            acc = fix_at(t, acc)
        acc_re, acc_im = lax.fori_loop(1, steps // SCAN_UNROLL, fix, acc)
        accr_ref[...] = acc_re
        acci_ref[...] = acc_im

        gbr = gr_ref[...].astype(BF16)
        gbi = gi_ref[...].astype(BF16)
        duseg_ref[...] = _dot_nt(gbr, br_ref[0]) + _dot_nt(gbi, bi_ref[0])
        _from_segments(duseg_ref, dyseg_ref, steps)
        du_ref[...] = (dyseg_ref[...] + dk_ref[...] * dy_ref[...]).astype(BF16)
        dbr = _dot_tn(ub, gbr)
        dbi = _dot_tn(ub, gbi)

        @pl.when(c == 0)
        def _():
            dbr_ref[0] = dbr
            dbi_ref[0] = dbi
            dcr_ref[0] = dcr
            dci_ref[0] = dci
            ddk_ref[...] = ddk

        @pl.when(c > 0)
        def _():
            dbr_ref[0] += dbr
            dbi_ref[0] += dbi
            dcr_ref[0] += dcr
            dci_ref[0] += dci
            ddk_ref[...] += ddk

        @pl.when(c == nc - 1)
        def _():
            dar_ref[...] = jnp.sum(acc_re, axis=0, keepdims=True)
            dai_ref[...] = jnp.sum(acc_im, axis=0, keepdims=True)

    rev = lambda c: nc - 1 - c
    row = pl.BlockSpec((1, blk), lambda j, c: (0, j))
    b_mat = pl.BlockSpec((1, LANES, blk), lambda j, c: (j, 0, 0))
    c_mat = pl.BlockSpec((1, blk, LANES), lambda j, c: (j, 0, 0))
    tok = pl.BlockSpec((chunk, LANES), lambda j, c: (rev(c), j))
    state = pl.BlockSpec((chunk, blk), lambda j, c: (rev(c), j))
    enter = pl.BlockSpec((SUBLANES, blk), lambda j, c: (rev(c), j))
    chan = pl.BlockSpec((1, LANES), lambda j, c: (0, j))
    f32 = lambda *s: jax.ShapeDtypeStruct(s, F32)
    return _hosted_call(
        body, carry, _edge_2d(N_SSM_BLOCKS, nc), name="ssm_bwd", grid=(N_SSM_BLOCKS, nc),
        in_specs=[tok, tok, state, state, enter, enter, row, row, b_mat, b_mat, c_mat, c_mat, chan],
        out_specs=[tok, b_mat, b_mat, c_mat, c_mat, row, row, chan],
        out_shape=[jax.ShapeDtypeStruct((T, SSM_W), BF16), f32(N_SSM_BLOCKS, LANES, blk), f32(N_SSM_BLOCKS, LANES, blk),
                   f32(N_SSM_BLOCKS, blk, LANES), f32(N_SSM_BLOCKS, blk, LANES), f32(1, STATES), f32(1, STATES), f32(1, SSM_W)],
        scratch_shapes=[pltpu.VMEM((chunk, LANES), F32), pltpu.VMEM((chunk, LANES), F32), pltpu.VMEM((chunk, LANES), F32),
                        pltpu.VMEM((chunk, blk), F32), pltpu.VMEM((chunk, blk), F32),
                        pltpu.VMEM((chunk, blk), F32), pltpu.VMEM((chunk, blk), F32),
                        pltpu.VMEM((SUBLANES, blk), F32), pltpu.VMEM((SUBLANES, blk), F32), pltpu.VMEM((SUBLANES, blk), F32)],
        compiler_params=_params(("arbitrary", "arbitrary"), VMEM_BIG),
        inputs=(dy, u, h_re, h_im, in_re, in_im, a_re, a_im, b_re, b_im, c_re, c_im, d_skip))


def _merge_forward(y, att, ga, gs, w_glu, w_ssm, w_attn):
    z = jax.nn.gelu(y)
    zb = z.astype(BF16)
    gl = jax.nn.sigmoid(_dot(zb, w_glu))
    z2b = (z * gl).astype(BF16)
    y_ssm = _dot(z2b, w_ssm)
    y_attn = _dot(att, w_attn)
    sa = jax.nn.sigmoid(ga)
    ss = jax.nn.sigmoid(gs)
    merged = (sa * y_attn + ss * y_ssm).astype(BF16)
    return z, zb, gl, z2b, y_ssm, y_attn, sa, ss, merged


def _merge_fwd(x, y, att, ga, gs, g2, g3, w_glu, w_ssm, w_attn, w_out, tile):
    T = x.shape[0]

    def body(x_ref, y_ref, att_ref, ga_ref, gs_ref, g2_ref, g3_ref, wg_ref, ws_ref, wa_ref, wo_ref, x1_ref, o_ref, h2_ref):
        merged = _merge_forward(y_ref[...], att_ref[...], ga_ref[...], gs_ref[...], wg_ref[...], ws_ref[...], wa_ref[...])[-1]
        o = _dot(merged, wo_ref[...])
        x1 = x_ref[...] + o * _rms_scale(o) * g2_ref[...]
        o_ref[...] = o
        x1_ref[...] = x1
        h2_ref[...] = (x1 * _rms_scale(x1) * g3_ref[...]).astype(BF16)

    tok = lambda w: pl.BlockSpec((tile, w), lambda i: (i, 0))
    vec = _const_spec((1, D_MODEL))
    return pl.pallas_call(
        body, name="merge_fwd", grid=(T // tile,),
        in_specs=[tok(D_MODEL), tok(SSM_W), tok(ATTN_W), tok(D_MODEL), tok(D_MODEL), vec, vec,
                  _const_spec((SSM_W, SSM_W)), _const_spec((SSM_W, D_MODEL)), _const_spec((ATTN_W, D_MODEL)),
                  _const_spec((D_MODEL, D_MODEL))],
        out_specs=[tok(D_MODEL), tok(D_MODEL), tok(D_MODEL)],
        out_shape=_hbm_out([jax.ShapeDtypeStruct((T, D_MODEL), F32), jax.ShapeDtypeStruct((T, D_MODEL), F32),
                            jax.ShapeDtypeStruct((T, D_MODEL), BF16)]),
        compiler_params=_params(("arbitrary",), VMEM_MID),
    )(*_in_hbm(x, y, att, ga, gs, g2, g3, w_glu, w_ssm, w_attn, w_out))


def _merge_bwd(dh2, dx2, x1, o, y, att, ga, gs, g2, g3, w_glu, w_ssm, w_attn, w_out, tile, carry=None):
    T = x1.shape[0]
    n_steps = T // tile

    group = min(2, n_steps)
    staged_widths = (D_MODEL, D_MODEL, ATTN_W, D_MODEL, SSM_W, D_MODEL, SSM_W, SSM_W)

    def body(dh2_ref, dx2_ref, x1_ref, o_ref, y_ref, att_ref, ga_ref, gs_ref, g2_ref, g3_ref, wg_ref, ws_ref, wa_ref, wo_ref,
             dx1_ref, dgates_ref, datt_ref, dy_ref, dwg_hbm, dws_hbm, dwa_hbm, dwo_hbm, dg2_ref, dg3_ref,
             awg_ref, aws_ref, awa_ref, awo_ref, *staged):
        i = pl.program_id(0)
        x1v, ov = x1_ref[...], o_ref[...]
        dxn, dg3 = _rms_bwd(dh2_ref[...], x1v, _rms_scale(x1v), g3_ref[...])
        dx1 = dx2_ref[...] + dxn
        dx1_ref[...] = dx1
        do, dg2 = _rms_bwd(dx1, ov, _rms_scale(ov), g2_ref[...])
        dob = do.astype(BF16)

        yv = y_ref[...]
        att = att_ref[...]
        z, zb, gl, z2b, y_ssm, y_attn, sa, ss, merged = _merge_forward(
            yv, att, ga_ref[...], gs_ref[...], wg_ref[...], ws_ref[...], wa_ref[...])
        dmerged = _dot_nt(dob, wo_ref[...])
        dya = (dmerged * sa).astype(BF16)
        dys = (dmerged * ss).astype(BF16)
        dgates_ref[:, :D_MODEL] = (dmerged * y_attn * sa * (1.0 - sa)).astype(BF16)
        dgates_ref[:, D_MODEL:] = (dmerged * y_ssm * ss * (1.0 - ss)).astype(BF16)
        datt_ref[...] = _dot_nt(dya, wa_ref[...]).astype(BF16)
        dz2 = _dot_nt(dys, ws_ref[...])
        dpre = (dz2 * z * gl * (1.0 - gl)).astype(BF16)
        dz = dz2 * gl + _dot_nt(dpre, wg_ref[...])
        _, gelu_vjp = jax.vjp(jax.nn.gelu, yv)
        dy_ref[...] = gelu_vjp(dz)[0]

        part = pl.ds(pl.multiple_of((i % group) * tile, tile), tile)
        for ref, val in zip(staged, (merged, dob, att, dya, z2b, dys, zb, dpre)):
            ref[part, :] = val

        @pl.when(i == 0)
        def _():
            dg2_ref[...] = dg2
            dg3_ref[...] = dg3

        @pl.when(i > 0)
        def _():
            dg2_ref[...] += dg2
            dg3_ref[...] += dg3

        def weight_grads():
            s_merged, s_dob, s_att, s_dya, s_z2b, s_dys, s_zb, s_dpre = (ref[...] for ref in staged)
            return ((awo_ref, _dot_tn(s_merged, s_dob)), (awa_ref, _dot_tn(s_att, s_dya)),
                    (aws_ref, _dot_tn(s_z2b, s_dys)), (awg_ref, _dot_tn(s_zb, s_dpre)))

        @pl.when(i == group - 1)
        def _():
            for ref, val in weight_grads():
                ref[...] = val

        @pl.when((i % group == group - 1) & (i > group - 1))
        def _():
            for ref, val in weight_grads():
                ref[...] += val

        @pl.when(i == n_steps - 1)
        def _():
            pltpu.sync_copy(awg_ref, dwg_hbm)
            pltpu.sync_copy(aws_ref, dws_hbm)
            pltpu.sync_copy(awa_ref, dwa_hbm)
            pltpu.sync_copy(awo_ref, dwo_hbm)

    tok = lambda w: pl.BlockSpec((tile, w), lambda i: (i, 0))
    vec = _const_spec((1, D_MODEL))
    any_ = pl.BlockSpec(memory_space=pl.ANY)
    vec_out = pl.BlockSpec((1, D_MODEL), lambda i: (0, 0))
    f32 = lambda *s: jax.ShapeDtypeStruct(s, F32)
    bf = lambda *s: jax.ShapeDtypeStruct(s, BF16)
    return _hosted_call(
        body, carry, _edge_1d(n_steps), name="merge_bwd", grid=(n_steps,),
        in_specs=[tok(D_MODEL), tok(D_MODEL), tok(D_MODEL), tok(D_MODEL), tok(SSM_W), tok(ATTN_W), tok(D_MODEL), tok(D_MODEL),
                  vec, vec, _const_spec((SSM_W, SSM_W)), _const_spec((SSM_W, D_MODEL)), _const_spec((ATTN_W, D_MODEL)),
                  _const_spec((D_MODEL, D_MODEL))],
        out_specs=[tok(D_MODEL), tok(2 * D_MODEL), tok(ATTN_W), tok(SSM_W), any_, any_, any_, any_, vec_out, vec_out],
        out_shape=[f32(T, D_MODEL), bf(T, 2 * D_MODEL), bf(T, ATTN_W), f32(T, SSM_W),
                   f32(SSM_W, SSM_W), f32(SSM_W, D_MODEL), f32(ATTN_W, D_MODEL), f32(D_MODEL, D_MODEL),
                   f32(1, D_MODEL), f32(1, D_MODEL)],
        scratch_shapes=[pltpu.VMEM((SSM_W, SSM_W), F32), pltpu.VMEM((SSM_W, D_MODEL), F32),
                        pltpu.VMEM((ATTN_W, D_MODEL), F32), pltpu.VMEM((D_MODEL, D_MODEL), F32)]
        + [pltpu.VMEM((group * tile, wd), BF16) for wd in staged_widths],
        compiler_params=_params(("arbitrary",), VMEM_BIG),
        inputs=(dh2, dx2, x1, o, y, att, ga, gs, g2, g3, w_glu, w_ssm, w_attn, w_out))


FF_SHARD = D_FF // N_DEV


def _mlp_fwd(h2, x1, target, g4, w_ff_in, w_ff_out, tile):
    T = h2.shape[0]
    col_chunk = 2 * FF_SHARD

    def body(h2_ref, x1_ref, tg_ref, g4_ref, wi_ref, wo_ref, a_ref, dfo_ref, dx2_ref, loss_ref, dg4_ref, rr_ref):
        i = pl.program_id(0)
        h2v = h2_ref[...]
        for c in range(D_FF // col_chunk):
            cols = slice(c * col_chunk, (c + 1) * col_chunk)
            a = _dot_nt(h2v, wi_ref[cols, :])
            a_ref[:, cols] = a.astype(BF16)
            ra = jnp.maximum(a, 0.0)
            rr_ref[:, cols] = (ra * ra).astype(BF16)
        f = _dot(rr_ref[...], wo_ref[...])
        r = _rms_scale(f)
        g = g4_ref[...]
        err = x1_ref[...] + f * r * g - tg_ref[...]
        dx2 = err * (1.0 / D_MODEL)
        dx2_ref[...] = dx2
        dfo, dg = _rms_bwd(dx2, f, r, g)
        dfo_ref[...] = dfo.astype(BF16)
        row = lax.broadcasted_iota(jnp.int32, (SUBLANES, LANES), 0)
        col = lax.broadcasted_iota(jnp.int32, (SUBLANES, LANES), 1)
        loss = jnp.where((row == 0) & (col == 0), (0.5 / D_MODEL) * jnp.sum(err * err), 0.0)

        @pl.when(i == 0)
        def _():
            loss_ref[...] = loss
            dg4_ref[...] = dg

        @pl.when(i > 0)
        def _():
            loss_ref[...] += loss
            dg4_ref[...] += dg

    tok = pl.BlockSpec((tile, D_MODEL), lambda i: (i, 0))
    return pl.pallas_call(
        body, name="mlp_fwd", grid=(T // tile,),
        in_specs=[tok, tok, tok, _const_spec((1, D_MODEL)), _const_spec((D_FF, D_MODEL)), _const_spec((D_FF, D_MODEL))],
        out_specs=[pl.BlockSpec((tile, D_FF), lambda i: (i, 0)), tok, tok,
                   pl.BlockSpec((SUBLANES, LANES), lambda i: (0, 0)), pl.BlockSpec((1, D_MODEL), lambda i: (0, 0))],
        out_shape=_hbm_out([jax.ShapeDtypeStruct((T, D_FF), BF16), jax.ShapeDtypeStruct((T, D_MODEL), BF16),
                            jax.ShapeDtypeStruct((T, D_MODEL), F32), jax.ShapeDtypeStruct((SUBLANES, LANES), F32),
                            jax.ShapeDtypeStruct((1, D_MODEL), F32)]),
        scratch_shapes=[pltpu.VMEM((tile, D_FF), BF16)],
        compiler_params=_params(("arbitrary",), VMEM_MAX),
    )(*_in_hbm(h2, x1, target, g4, w_ff_in.reshape(D_FF, D_MODEL), w_ff_out.reshape(D_FF, D_MODEL)))


def _mlp_weight_grads(dfo, a, h2, w_ff_out, row_chunk):
    T = h2.shape[0]

    def body(dfo_ref, h2_ref, a_ref, wo_ref, dwi_ref, dwo_ref, da_ref, rr_ref):
        def rows(r, _):
            sl = pl.ds(pl.multiple_of(r * row_chunk, row_chunk), row_chunk)
            ra = jnp.maximum(a_ref[sl, :].astype(F32), 0.0)
            da_ref[sl, :] = (_dot_nt(dfo_ref[sl, :], wo_ref[0]) * (2.0 * ra)).astype(BF16)
            rr_ref[sl, :] = (ra * ra).astype(BF16)
            return 0

        lax.fori_loop(0, T // row_chunk, rows, 0)
        dwo_ref[0] = _dot_tn(rr_ref[...], dfo_ref[...])
        dwi_ref[0] = _dot_tn(h2_ref[...], da_ref[...])

    return pl.pallas_call(
        body, name="mlp_weight_grads", grid=(N_DEV,),
        in_specs=[_const_spec((T, D_MODEL)), _const_spec((T, D_MODEL)), pl.BlockSpec((T, FF_SHARD), lambda k: (0, k)),
                  pl.BlockSpec((1, FF_SHARD, D_MODEL), lambda k: (k, 0, 0))],
        out_specs=[pl.BlockSpec((1, D_MODEL, FF_SHARD), lambda k: (k, 0, 0)),
                   pl.BlockSpec((1, FF_SHARD, D_MODEL), lambda k: (k, 0, 0)), pl.BlockSpec((T, FF_SHARD), lambda k: (0, k))],
        out_shape=_hbm_out([jax.ShapeDtypeStruct((N_DEV, D_MODEL, FF_SHARD), F32),
                            jax.ShapeDtypeStruct((N_DEV, FF_SHARD, D_MODEL), F32), jax.ShapeDtypeStruct((T, D_FF), BF16)]),
        scratch_shapes=[pltpu.VMEM((T, FF_SHARD), BF16)],
        compiler_params=_params(("arbitrary",), VMEM_MAX),
    )(*_in_hbm(dfo, h2, a, w_ff_out))


def _mlp_input_grad(da, w_ff_in_t, tile):
    T = da.shape[0]

    def body(da_ref, w_ref, o_ref):
        o_ref[...] = _dot(da_ref[...], w_ref[...])

    return pl.pallas_call(
        body, name="mlp_input_grad", grid=(T // tile,),
        in_specs=[pl.BlockSpec((tile, D_FF), lambda i: (i, 0)), _const_spec((D_FF, D_MODEL))],
        out_specs=pl.BlockSpec((tile, D_MODEL), lambda i: (i, 0)),
        out_shape=_hbm_out(jax.ShapeDtypeStruct((T, D_MODEL), F32)),
        compiler_params=_params(("arbitrary",), VMEM_MID),
    )(*_in_hbm(da, w_ff_in_t))


def _block_diag_in(b):
    bt = b.reshape(N_SSM_BLOCKS, GROUPS_PER_BLOCK, GROUP_CH, N_STATE)
    eye = jnp.eye(GROUPS_PER_BLOCK, dtype=b.dtype)
    return jnp.einsum("jacp,ab->jacbp", bt, eye).reshape(N_SSM_BLOCKS, LANES, SSM_LANE_BLOCK)


def _block_diag_in_grad(g):
    g = g.reshape(N_SSM_BLOCKS, GROUPS_PER_BLOCK, GROUP_CH, GROUPS_PER_BLOCK, N_STATE)
    d = jnp.diagonal(g, axis1=1, axis2=3)
    return jnp.transpose(d, (0, 3, 1, 2)).reshape(N_GROUPS, GROUP_CH, N_STATE)


def _block_diag_out(c):
    ct = c.reshape(N_SSM_BLOCKS, GROUPS_PER_BLOCK, GROUP_CH, N_STATE)
    eye = jnp.eye(GROUPS_PER_BLOCK, dtype=c.dtype)
    return jnp.einsum("jacp,ab->japbc", ct, eye).reshape(N_SSM_BLOCKS, SSM_LANE_BLOCK, LANES)


def _block_diag_out_grad(g):
    g = g.reshape(N_SSM_BLOCKS, GROUPS_PER_BLOCK, N_STATE, GROUPS_PER_BLOCK, GROUP_CH)
    d = jnp.diagonal(g, axis1=1, axis2=3)
    return jnp.transpose(d, (0, 3, 2, 1)).reshape(N_GROUPS, GROUP_CH, N_STATE)


def _tiles(T):
    return dict(proj=min(512, T), proj_bwd=min(512, T // 2), merge=min(512, T), merge_bwd=min(256, T),
                mlp_fwd=min(512, T), mlp_bwd=min(512, T), ssm_chunk=min(1024, T))


def _mesh_position():
    x, y, c = lax.axis_index("x"), lax.axis_index("y"), lax.axis_index("c")
    other_chips = [(1 - x, y), (x, 1 - y), (1 - x, 1 - y)]
    return x, y, c, other_chips


def _gather_carry(arrays):
    n = len(arrays)

    def copies(ins, outs, sems):
        send_sems, recv_sems, local_sems = sems
        x, y, c, chips = _mesh_position()
        me, sibling = (x, y, c), (x, y, 1 - c)

        def copy(a, k, block, to, src=None):
            px, py, pc = block
            dst = outs[a].at[4 * px + 2 * py + pc]
            return pltpu.make_async_remote_copy(
                src_ref=dst if src is None else src, dst_ref=dst, send_sem=send_sems.at[7 * a + k],
                recv_sem=recv_sems.at[7 * a + k], device_id=to, device_id_type=MESH_IDS)

        mine = [pltpu.make_async_copy(ins[a], outs[a].at[4 * x + 2 * y + c], local_sems.at[a]) for a in range(n)]
        first = []
        for a in range(n):
            first.append(copy(a, 0, me, sibling, src=ins[a]))
            first += [copy(a, 1 + j, me, (*chip, c), src=ins[a]) for j, chip in enumerate(chips)]
        return copy, mine, first, me, sibling, chips, c

    def start(ins, outs, sems):
        _, mine, first, *_ = copies(ins, outs, sems)
        for cp in mine + first:
            cp.start()

    def passed_on(copy, sibling, chips, c):
        return [copy(a, 4 + j, (*chip, c), sibling) for a in range(n) for j, chip in enumerate(chips)]

    def middle(ins, outs, sems):
        copy, _, _, me, sibling, chips, c = copies(ins, outs, sems)
        for a in range(n):
            for j, chip in enumerate(chips):
                copy(a, 1 + j, (*chip, c), me).wait_recv()
                copy(a, 4 + j, (*chip, c), sibling).start()

    def finish(ins, outs, sems):
        copy, mine, first, me, sibling, chips, c = copies(ins, outs, sems)
        for a in range(n):
            copy(a, 0, sibling, me).wait_recv()
            for j, chip in enumerate(chips):
                copy(a, 4 + j, (*chip, 1 - c), me).wait_recv()
        for cp in first + passed_on(copy, sibling, chips, c):
            cp.wait_send()
        for cp in mine:
            cp.wait()

    return _Carry(arrays, [jax.ShapeDtypeStruct((N_DEV,) + a.shape, a.dtype) for a in arrays],
                  [pltpu.SemaphoreType.DMA((7 * n,)), pltpu.SemaphoreType.DMA((7 * n,)), pltpu.SemaphoreType.DMA((n,))],
                  start, finish, middle)


def _pairwise_carry(arrays, n_slots, make_copies):
    n = len(arrays)

    def start(ins, outs, sems):
        for cp in make_copies(ins, outs, sems):
            cp.start()

    def finish(ins, outs, sems):
        for cp in make_copies(ins, outs, sems):
            cp.wait()

    return _Carry(arrays, [jax.ShapeDtypeStruct((n_slots,) + a.shape[1:], a.dtype) for a in arrays],
                  [pltpu.SemaphoreType.DMA((n_slots * n,)), pltpu.SemaphoreType.DMA((n_slots * n,))], start, finish)


def _sibling_carry(grads):
    def make_copies(ins, outs, sems):
        x, y, c, _ = _mesh_position()
        return [pltpu.make_async_remote_copy(
            src_ref=ins[a].at[2 * ch + (1 - c)], dst_ref=outs[a].at[ch], send_sem=sems[0].at[4 * a + ch],
            recv_sem=sems[1].at[4 * a + ch], device_id=(x, y, 1 - c), device_id_type=MESH_IDS)
            for a in range(len(grads)) for ch in range(4)]

    return _pairwise_carry(grads, 4, make_copies)


def _chips_carry(sums):
    def make_copies(ins, outs, sems):
        x, y, c, chips = _mesh_position()
        return [pltpu.make_async_remote_copy(
            src_ref=ins[a].at[2 * px + py], dst_ref=outs[a].at[j], send_sem=sems[0].at[3 * a + j],
            recv_sem=sems[1].at[3 * a + j], device_id=(px, py, c), device_id_type=MESH_IDS)
            for a in range(len(sums)) for j, (px, py) in enumerate(chips)]

    return _pairwise_carry(sums, 3, make_copies)


def _everyone_carry(arrays):
    def make_copies(ins, outs, sems):
        x, y, c, _ = _mesh_position()
        flip = lambda v, bit: 1 - v if bit else v
        return [pltpu.make_async_remote_copy(
            src_ref=ins[a], dst_ref=outs[a].at[r - 1], send_sem=sems[0].at[7 * a + r - 1], recv_sem=sems[1].at[7 * a + r - 1],
            device_id=(flip(x, r & 4), flip(y, r & 2), flip(c, r & 1)), device_id_type=MESH_IDS)
            for a in range(len(arrays)) for r in range(1, N_DEV)]

    carry = _pairwise_carry([jax.ShapeDtypeStruct((1,) + a.shape, a.dtype) for a in arrays], N_DEV - 1, make_copies)
    carry.inputs = list(arrays)
    return carry


def _sum_everyone(own, received, me, name, after=()):
    def body(me_ref, own_ref, r_ref, *refs):
        g = None
        for d in range(N_DEV):
            relation = jnp.bitwise_xor(d, me_ref[0])
            part = jnp.where(relation == 0, own_ref[...], r_ref[jnp.maximum(relation - 1, 0)])
            g = part if g is None else g + part
        refs[-1][...] = g

    whole = lambda shape: pl.BlockSpec(shape, lambda i, me_ref: (0,) * len(shape))
    return pl.pallas_call(
        body, name=name,
        grid_spec=pltpu.PrefetchScalarGridSpec(
            num_scalar_prefetch=1, grid=(1,), in_specs=[whole(own.shape), whole(received.shape)] + [HBM_SPEC] * len(after),
            out_specs=whole(own.shape)),
        out_shape=jax.ShapeDtypeStruct(own.shape, F32))(me, *_in_hbm(own, received), *after)


SEM_SPEC = pl.BlockSpec(memory_space=pltpu.SEMAPHORE)
DATAFLOW_EFFECT = pltpu.SideEffectType.DATAFLOW_SIDE_EFFECTING


def _exchange_start(carry, name, after=()):
    n, n_sems = len(carry.inputs), len(carry.sems)
    lands = [lax.empty(s.shape, s.dtype) for s in carry.out_shapes]

    def body(*refs):
        first_out = 2 * n + len(after)
        srcs, zones, sems, token = refs[:n], refs[n:2 * n], refs[first_out:first_out + n_sems], refs[-1]
        carry.start(srcs, zones, sems)
        token[...] = jnp.zeros_like(token)

    outs = pl.pallas_call(
        body, name=name, in_specs=[HBM_SPEC] * (2 * n + len(after)),
        out_specs=[SEM_SPEC] * n_sems + [HBM_SPEC] * (2 * n) + [pl.BlockSpec(memory_space=pltpu.VMEM)],
        out_shape=list(carry.sems) + _hbm_out([jax.ShapeDtypeStruct(a.shape, a.dtype) for a in carry.inputs])
        + _hbm_out(carry.out_shapes) + [jax.ShapeDtypeStruct((SUBLANES, LANES), F32)],
        input_output_aliases={j: n_sems + j for j in range(2 * n)},
        compiler_params=pltpu.CompilerParams(has_side_effects=DATAFLOW_EFFECT),
    )(*_in_hbm(*carry.inputs, *lands), *after)
    return outs[:-1], outs[-1]


def _exchange_wait(carry, in_flight, after, name):
    n = len(carry.inputs)
    sems, srcs, zones = in_flight[:2], in_flight[2:2 + n], in_flight[2 + n:]

    def body(*refs):
        src_refs, zone_refs, sem_refs = refs[:n], refs[n:2 * n], refs[2 * n:2 * n + 2]
        carry.finish(src_refs, zone_refs, sem_refs)

    outs = pl.pallas_call(
        body, name=name, in_specs=[HBM_SPEC] * (2 * n) + [SEM_SPEC, SEM_SPEC] + [HBM_SPEC] * len(after),
        out_specs=[HBM_SPEC] * (2 * n),
        out_shape=_hbm_out([jax.ShapeDtypeStruct(a.shape, a.dtype) for a in carry.inputs]) + _hbm_out(carry.out_shapes),
        input_output_aliases={j: j for j in range(2 * n)},
        compiler_params=pltpu.CompilerParams(has_side_effects=DATAFLOW_EFFECT),
    )(*srcs, *zones, *sems, *after)
    return list(outs[:n]), list(outs[n:])


def _add_sibling(grads8, recvs, core, row_tiles, name):
    k = len(grads8)
    g4 = [g.reshape(4, 2, *g.shape[1:]) for g in grads8]

    def body(core_ref, *refs):
        g_refs, r_refs, o_refs, ob_refs = (refs[j * k:(j + 1) * k] for j in range(4))
        for g_ref, r_ref, o_ref, ob_ref in zip(g_refs, r_refs, o_refs, ob_refs):
            s = g_ref[0] + r_ref[...]
            o_ref[...] = s
            ob_ref[...] = s.astype(BF16)

    def blocks(make):
        return [make(g.shape[1] // row_tiles, g.shape[2]) for g in grads8]

    slot = lambda tr, C: pl.BlockSpec((1, tr, C), lambda ch, r, core_ref: (ch, r, 0))
    outs = pl.pallas_call(
        body, name=name,
        grid_spec=pltpu.PrefetchScalarGridSpec(
            num_scalar_prefetch=1, grid=(4, row_tiles),
            in_specs=blocks(lambda tr, C: pl.BlockSpec((1, 1, tr, C), lambda ch, r, core_ref: (ch, core_ref[0], r, 0)))
            + blocks(slot), out_specs=blocks(slot) + blocks(slot)),
        out_shape=_hbm_out([jax.ShapeDtypeStruct((4,) + g.shape[1:], F32) for g in grads8]
                           + [jax.ShapeDtypeStruct((4,) + g.shape[1:], BF16) for g in grads8]),
        compiler_params=_params(("arbitrary", "arbitrary")),
    )(core, *_in_hbm(*g4, *recvs))
    return list(outs[:k]), list(outs[k:])


def _adam_math(w, g, m, v):
    m = ADAM_B1 * m + (1.0 - ADAM_B1) * g
    v = ADAM_B2 * v + (1.0 - ADAM_B2) * jnp.square(g)
    m_hat = m / (1.0 - ADAM_B1 ** ADAM_STEP)
    v_hat = v / (1.0 - ADAM_B2 ** ADAM_STEP)
    delta = -ADAM_LR * (m_hat / (jnp.sqrt(v_hat) + ADAM_EPS) + ADAM_WD * w)
    return delta, m, v


def _adam_big(ws, ms, vs, chip_sums, recvs, chip, row_tiles, name, after=()):
    k = len(ws)

    def body(chip_ref, *refs):
        refs = refs[:5 * k] + refs[5 * k + len(after):]
        w_refs, m_refs, v_refs, s_refs, r_refs, g_refs, d_refs, nm_refs, nv_refs = (refs[j * k:(j + 1) * k] for j in range(9))
        for a in range(k):
            r_ref = r_refs[a]
            g = s_refs[a][0] + r_ref[0].astype(F32) + r_ref[1].astype(F32) + r_ref[2].astype(F32)
            g_refs[a][...] = g
            d_refs[a][...], nm_refs[a][...], nv_refs[a][...] = _adam_math(w_refs[a][...], g, m_refs[a][...], v_refs[a][...])

    def blocks(make):
        return [make(w.shape[0] // row_tiles, w.shape[1]) for w in ws]

    blk = lambda tr, C: pl.BlockSpec((tr, C), lambda r, chip_ref: (r, 0))
    outs = pl.pallas_call(
        body, name=name,
        grid_spec=pltpu.PrefetchScalarGridSpec(
            num_scalar_prefetch=1, grid=(row_tiles,),
            in_specs=blocks(blk) * 3 + blocks(lambda tr, C: pl.BlockSpec((1, tr, C), lambda r, chip_ref: (chip_ref[0], r, 0)))
            + blocks(lambda tr, C: pl.BlockSpec((3, tr, C), lambda r, chip_ref: (0, r, 0))) + [HBM_SPEC] * len(after),
            out_specs=blocks(blk) * 4),
        out_shape=[jax.ShapeDtypeStruct(w.shape, F32) for w in ws] * 4,
        compiler_params=_params(("arbitrary",)),
    )(chip, *_in_hbm(*ws, *ms, *vs, *chip_sums, *recvs), *after)
    return [list(outs[j * k:(j + 1) * k]) for j in range(4)]


def _sum_partials(partials, name, after=()):
    def body(p_ref, *refs):
        g = p_ref[0]
        for d in range(1, partials.shape[0]):
            g = g + p_ref[d]
        refs[-1][...] = g

    return pl.pallas_call(body, name=name, grid=(1,), in_specs=[_whole(partials.shape)] + [HBM_SPEC] * len(after),
                          out_specs=_whole(partials.shape[1:]),
                          out_shape=jax.ShapeDtypeStruct(partials.shape[1:], F32))(*_in_hbm(partials), *after)


def _adam_small(ws, ms, vs, gs):
    n = len(ws)

    def body(*refs):
        w_refs, m_refs, v_refs, g_refs = (refs[i * n:(i + 1) * n] for i in range(4))
        d_refs, nm_refs, nv_refs = (refs[(4 + i) * n:(5 + i) * n] for i in range(3))
        for j in range(n):
            d_refs[j][...], nm_refs[j][...], nv_refs[j][...] = _adam_math(
                w_refs[j][...], g_refs[j][...], m_refs[j][...], v_refs[j][...])

    specs = [_whole(w.shape) for w in ws]
    outs = pl.pallas_call(body, name="adam_small", grid=(1,), in_specs=specs * 4, out_specs=specs * 3,
                          out_shape=[jax.ShapeDtypeStruct(w.shape, F32) for w in ws] * 3,
                          compiler_params=_params(("arbitrary",), VMEM_MID))(*_in_hbm(*ws, *ms, *vs, *gs))
    return outs[:n], outs[n:2 * n], outs[2 * n:]


PACK_QUANTUM = SUBLANES * LANES


def _pack(named, names):
    parts = []
    for nme in names:
        flat = named[nme].reshape(-1)
        parts.append(jnp.pad(flat, (0, -flat.size % PACK_QUANTUM)))
    return jnp.concatenate(parts).reshape(-1, LANES)


def _unpack(packed, shapes, names):
    flat = packed.reshape(-1)
    out, pos = {}, 0
    for nme in names:
        size = math.prod(shapes[nme])
        out[nme] = flat[pos:pos + size].reshape(shapes[nme])
        pos += size + (-size % PACK_QUANTUM)
    return out


BIG = ("w_in", "w_glu", "w_attn_branch", "w_ssm_branch", "w_out", "w_ff_in", "w_ff_out")
COLUMN_SHARDED = ("w_in", "w_attn_branch", "w_ssm_branch", "w_ff_in")
SMALL = ("norm_mix_pre", "norm_mix_post", "norm_mlp_pre", "norm_mlp_post", "rel_bias", "sinks", "lam_re", "lam_im",
         "log_dt", "b_re", "b_im", "c_re", "c_im", "d_skip")
SWAPPED_SMALL = ("rel_bias", "b_re", "b_im")
SMALL_LATE = ("norm_mix_pre", "rel_bias", "sinks", "loss")
SMALL_BEFORE_ATTN_BWD = tuple(n for n in SMALL if n not in SMALL_LATE)
ALL_WEIGHTS = ("norm_mix_pre", "norm_mix_post", "norm_mlp_pre", "norm_mlp_post", "w_in", "rel_bias", "sinks", "lam_re",
               "lam_im", "log_dt", "b_re", "b_im", "c_re", "c_im", "d_skip", "w_glu", "w_attn_branch", "w_ssm_branch",
               "w_out", "w_ff_in", "w_ff_out")


def _full_from_gathered(name, gathered):
    _, r, c = gathered.shape
    if name in COLUMN_SHARDED:
        return jnp.transpose(gathered, (1, 0, 2)).reshape(r, N_DEV * c)
    return gathered.reshape(N_DEV * r, c)


def _blocks_from_full(name, full):
    r, c = full.shape
    if name in COLUMN_SHARDED:
        return jnp.transpose(full.reshape(r, N_DEV, c // N_DEV), (1, 0, 2))
    return full.reshape(N_DEV, r // N_DEV, c)


def kernel(x, norm_mix_pre, norm_mix_post, norm_mlp_pre, norm_mlp_post, w_in, rel_bias, sinks, lam_re, lam_im, log_dt, b_re, b_im, c_re, c_im, d_skip, w_glu, w_attn_branch, w_ssm_branch, w_out, w_ff_in, w_ff_out, loss_target, m_norm_mix_pre, m_norm_mix_post, m_norm_mlp_pre, m_norm_mlp_post, m_w_in, m_rel_bias, m_sinks, m_lam_re, m_lam_im, m_log_dt, m_b_re, m_b_im, m_c_re, m_c_im, m_d_skip, m_w_glu, m_w_attn_branch, m_w_ssm_branch, m_w_out, m_w_ff_in, m_w_ff_out, v_norm_mix_pre, v_norm_mix_post, v_norm_mlp_pre, v_norm_mlp_post, v_w_in, v_rel_bias, v_sinks, v_lam_re, v_lam_im, v_log_dt, v_b_re, v_b_im, v_c_re, v_c_im, v_d_skip, v_w_glu, v_w_attn_branch, v_w_ssm_branch, v_w_out, v_w_ff_in, v_w_ff_out):
    args = dict(locals())
    w = {n: args[n] for n in ALL_WEIGHTS}
    m = {n: args["m_" + n] for n in ALL_WEIGHTS}
    v = {n: args["v_" + n] for n in ALL_WEIGHTS}
    core = lax.axis_index("c").astype(jnp.int32).reshape(1)
    chip = (2 * lax.axis_index("x") + lax.axis_index("y")).astype(jnp.int32).reshape(1)
    xs, target = x[0], loss_target[0]
    t = _tiles(xs.shape[0])
    local = lambda d, n: d[n][0].T if n == "w_in" else d[n][0]
    shard = {n: local(w, n).astype(BF16) for n in BIG}
    shard["w_ff_in"] = shard["w_ff_in"].T
    view = lambda n, a: jnp.swapaxes(a, -1, -2) if n in SWAPPED_SMALL else a
    small = {n: (view(n, w[n]) if n == "rel_bias" else view(n, w[n])[0]) for n in SMALL}
    g1, g2, g3, g4 = (small[n].reshape(1, D_MODEL) for n in ("norm_mix_pre", "norm_mix_post", "norm_mlp_pre", "norm_mlp_post"))
    bucket = jnp.asarray(_bucket_table())
    rel_b, sink = small["rel_bias"], small["sinks"].reshape(1, N_HEADS)
    lam_r, lam_i = small["lam_re"].reshape(1, STATES), small["lam_im"].reshape(1, STATES)
    ldt_rep = jnp.repeat(small["log_dt"].reshape(N_GROUPS), N_STATE).reshape(1, STATES)
    bd_re, bd_im = _block_diag_in(small["b_re"]), _block_diag_in(small["b_im"])
    cm_re, cm_im = _block_diag_out(small["c_re"]).astype(BF16), _block_diag_out(small["c_im"]).astype(BF16)
    dsk = small["d_skip"].reshape(1, SSM_W)

    (g_in,) = _run_carry(_gather_carry([shard["w_in"]]), "gather_w_in")
    wf_in = g_in.reshape(IN_W, D_MODEL)
    merge_names = ("w_glu", "w_attn_branch", "w_ssm_branch", "w_out")
    (q, k, vv, u, ga, gs, h), gathered = _in_proj_fwd(xs, g1, wf_in, t["proj"], _gather_carry([shard[n] for n in merge_names]))
    wf = {n: _full_from_gathered(n, g) for n, g in zip(merge_names, gathered)}
    (att,), (wf_ff_in,) = _attn_fwd(q, k, vv, bucket, rel_b, sink, _gather_carry([shard["w_ff_in"]]))
    a_re, a_im, bm_re, bm_im = _ssm_prep(lam_r, lam_i, ldt_rep, bd_re, bd_im)
    (y, h_re, h_im, in_re, in_im), (wf_ff_out,) = _ssm_fwd(
        u, a_re, a_im, bm_re, bm_im, cm_re, cm_im, dsk, t["ssm_chunk"], _gather_carry([shard["w_ff_out"]]))
    x1, o, h2 = _merge_fwd(xs, y, att, ga, gs, g2, g3, wf["w_glu"], wf["w_ssm_branch"], wf["w_attn_branch"], wf["w_out"],
                           t["merge"])
    a, dfo, dx2, loss_blk, dg4 = _mlp_fwd(h2, x1, target, g4, wf_ff_in, wf_ff_out, t["mlp_fwd"])

    groups = {"ff": 4, "merge": 1, "w_in": 2}

    def add_sibling(group, blocks, received):
        return _add_sibling(blocks, received, core, groups[group], "add_sibling_" + group)

    ff_names = ("w_ff_in", "w_ff_out")
    dw_ff_in, dw_ff_out, da = _mlp_weight_grads(dfo, a, h2, wf_ff_out, t["mlp_bwd"])
    dh2 = _mlp_input_grad(da, wf_ff_in.reshape(D_FF, D_MODEL), t["mlp_bwd"])
    ff_blocks = [dw_ff_in, dw_ff_out]
    (dx1, dgates, datt, dy, dw_glu, dw_ssm, dw_attn, dw_out, dg2, dg3), ff_recv = _merge_bwd(
        dh2, dx2, x1, o, y, att, ga, gs, g2, g3, wf["w_glu"], wf["w_ssm_branch"], wf["w_attn_branch"], wf["w_out"],
        t["merge_bwd"], _sibling_carry(ff_blocks))
    ff_sums, ff_sums_bf = add_sibling("ff", ff_blocks, ff_recv)
    merge_blocks = [_blocks_from_full(n, g) for n, g in zip(merge_names, (dw_glu, dw_attn, dw_ssm, dw_out))]
    (du, dbm_re, dbm_im, dcm_re, dcm_im, da_re, da_im, dd_skip), carried = _ssm_bwd(
        dy, u, h_re, h_im, in_re, in_im, a_re, a_im, bm_re, bm_im, cm_re, cm_im, dsk, t["ssm_chunk"],
        _join(_chips_carry(ff_sums_bf), _sibling_carry(merge_blocks)))
    ff_from_chips, merge_recv = carried[:2], carried[2:]
    merge_sums, merge_sums_bf = add_sibling("merge", merge_blocks, merge_recv)
    dbd_re, dbd_im, dlam_re, dlam_im, dldt_rep = _ssm_prep_bwd(lam_r, lam_i, ldt_rep, bd_re, bd_im, dbm_re, dbm_im, da_re, da_im)
    dlog_dt = _group_sum(dldt_rep.reshape(N_GROUPS, N_STATE))
    shapes = {n: view(n, w[n]).shape for n in SMALL}
    shapes["loss"] = (1,)
    small_grads = dict(
        norm_mix_post=dg2, norm_mlp_pre=dg3, norm_mlp_post=dg4, lam_re=dlam_re, lam_im=dlam_im, log_dt=dlog_dt,
        b_re=_block_diag_in_grad(dbd_re), b_im=_block_diag_in_grad(dbd_im),
        c_re=_block_diag_out_grad(dcm_re), c_im=_block_diag_out_grad(dcm_im), d_skip=dd_skip)
    packed_early = _pack({n: small_grads[n].reshape(shapes[n]) for n in SMALL_BEFORE_ATTN_BWD}, SMALL_BEFORE_ATTN_BWD)
    (dq, dkv, attn_small), carried = _attn_bwd(
        q, k, vv, datt, bucket, rel_b, sink, _join(_chips_carry(merge_sums_bf), _gather_carry([packed_early])))
    merge_from_chips, partials_early = carried[:-1], carried[-1]

    dparts = (dq, dkv, du, dgates)
    dw_in_t = _in_proj_weight_grad(h, dparts)
    in_blocks = [dw_in_t.reshape(N_DEV, IN_W // N_DEV, D_MODEL)]
    to_sibling = _sibling_carry(in_blocks)
    in_flight, token = _exchange_start(to_sibling, "w_in_sibling_start")
    n_tiles = xs.shape[0] // t["proj_bwd"]
    (grad_x, dg1), _ = _in_proj_input_grad(xs, g1 + token[0:1, 0:1], wf_in, dx1, dparts, t["proj_bwd"], 0, n_tiles, "in_proj_input_grad")
    late = dict(norm_mix_pre=dg1, rel_bias=attn_small[:, :N_BUCKETS, 0], sinks=attn_small[:, N_BUCKETS, 0], loss=loss_blk[0:1, 0])
    packed_late = _pack({n: late[n].reshape(shapes[n]) for n in SMALL_LATE}, SMALL_LATE)
    to_everyone = _everyone_carry([packed_late])
    in_blocks, in_recv = _exchange_wait(to_sibling, in_flight, [packed_late], "w_in_sibling_wait")
    in_sums, in_sums_bf = add_sibling("w_in", in_blocks, in_recv)
    to_chips = _chips_carry(in_sums_bf)
    started, chips_started = _exchange_start(_join(to_everyone, to_chips), "late_grads_and_w_in_chips_start")
    late_in_flight, in_flight = [[started[j] for j in js] for js in ((0, 1, 4, 6), (2, 3, 5, 7))]

    grads, deltas, new_m, new_v = {}, {}, {}, {}

    def adam_group(group, names, sums, received, after=()):
        outs = _adam_big(*[[local(d, n) for n in names] for d in (w, m, v)], sums, received, chip, groups[group],
                         "adam_" + group, after)
        for store, vals in zip((grads, deltas, new_m, new_v), outs):
            store.update({n: (o.T if n == "w_in" else o)[None] for n, o in zip(names, vals)})

    adam_group("ff", ff_names, ff_sums, ff_from_chips, [chips_started])
    adam_group("merge", merge_names, merge_sums, merge_from_chips, [chips_started])

    grads.update(_unpack(_sum_partials(partials_early, "sum_small_grads", [chips_started]), shapes, SMALL_BEFORE_ATTN_BWD))
    (packed_late,), (late_received,) = _exchange_wait(
        to_everyone, late_in_flight, [new_v["w_ff_out"], new_v["w_out"]], "late_grads_wait")
    grads.update(_unpack(_sum_everyone(packed_late, late_received, 2 * chip + core, "sum_late_grads"), shapes, SMALL_LATE))
    loss = grads.pop("loss").reshape(())
    small_out = _adam_small(*[[view(n, d[n]) for n in SMALL] for d in (w, m, v)], [grads[n] for n in SMALL])
    for store, vals in zip((deltas, new_m, new_v), small_out):
        store.update(zip(SMALL, vals))
    for store in (grads, deltas, new_m, new_v):
        store.update({n: view(n, store[n]) for n in SWAPPED_SMALL})

    busy = [new_v["w_ff_out"], new_v["w_out"], deltas["norm_mix_pre"]]
    _, (in_from_chips,) = _exchange_wait(to_chips, in_flight, busy, "w_in_chips_wait")
    adam_group("w_in", ("w_in",), in_sums, [in_from_chips])

    return (loss, grad_x[None], *[grads[n] for n in ALL_WEIGHTS], *[deltas[n] for n in ALL_WEIGHTS],
            *[new_m[n] for n in ALL_WEIGHTS], *[new_v[n] for n in ALL_WEIGHTS])
```

```python
import math

import jax
import jax.numpy as jnp
import numpy as np
from jax import lax
from jax.experimental import pallas as pl
from jax.experimental.pallas import tpu as pltpu

F32 = jnp.float32
BF16 = jnp.bfloat16

D_MODEL = 1024
N_HEADS = 8
HEAD_DIM = 64
ATTN_W = 512
KV_W = 128
BLOCK = 128
N_BUCKETS = 32
SSM_W = 512
N_GROUPS = 32
N_STATE = 64
GROUP_CH = 16
STATES = N_GROUPS * N_STATE
D_FF = 4096
IN_W = 3328
SPLITS = (0, 512, 640, 768, 1280, 2304, 3328)
RMS_EPS = 1e-6
NEG_INF = -1e30
SUBLANES = 8
LANES = 128
SSM_LANE_BLOCK = 512
N_SSM_BLOCKS = STATES // SSM_LANE_BLOCK
GROUPS_PER_BLOCK = SSM_LANE_BLOCK // N_STATE
VMEM_BIG = 52 * 1024 * 1024
VMEM_MID = 40 * 1024 * 1024
VMEM_MAX = 60 * 1024 * 1024

ADAM_LR = 0.001
ADAM_B1 = 0.9
ADAM_B2 = 0.999
ADAM_EPS = 1e-08
ADAM_WD = 0.01
ADAM_STEP = 10

N_DEV = 8


def _dot(a, b):
    return jnp.dot(a, b, preferred_element_type=F32)


def _dot_nt(a, b):
    return lax.dot_general(a, b, (((1,), (1,)), ((), ())), preferred_element_type=F32)


def _dot_tn(a, b):
    return lax.dot_general(a, b, (((0,), (0,)), ((), ())), preferred_element_type=F32)


def _rms_scale(x):
    return lax.rsqrt(jnp.mean(x * x, axis=-1, keepdims=True) + RMS_EPS)


def _rms_bwd(dy, x, r, g):
    t = dy * g
    dx = r * t - x * (r * r * r) * jnp.mean(t * x, axis=-1, keepdims=True)
    dg = jnp.sum(dy * x * r, axis=0, keepdims=True)
    return dx, dg


def _const_spec(shape):
    nd = len(shape)
    return pl.BlockSpec(shape, lambda *_: (0,) * nd, pipeline_mode=pl.Buffered(1))


def _in_hbm(*arrays):
    return tuple(pltpu.with_memory_space_constraint(a, pltpu.HBM) for a in arrays)


def _hbm_out(shapes):
    if isinstance(shapes, (list, tuple)):
        return [_hbm_out(s) for s in shapes]
    return shapes if isinstance(shapes, pl.MemoryRef) else pltpu.HBM(shapes.shape, shapes.dtype)


def _whole(shape):
    nd = len(shape)
    return pl.BlockSpec(shape, lambda *_: (0,) * nd)


def _params(sem, vmem=None):
    return pltpu.CompilerParams(dimension_semantics=sem, vmem_limit_bytes=vmem)


MESH_IDS = pl.DeviceIdType.MESH
HBM_SPEC = pl.BlockSpec(memory_space=pl.ANY)


class _Carry:
    def __init__(self, inputs, out_shapes, sems, start, finish, middle=None):
        self.inputs, self.out_shapes, self.sems = list(inputs), list(out_shapes), list(sems)
        self.start, self.middle, self.finish = start, middle, finish


def _join(a, b):
    na_in, na_out, na_sem = len(a.inputs), len(a.out_shapes), len(a.sems)

    def both(phase):
        def run(ins, outs, sems):
            for carry, lo in ((a, True), (b, False)):
                part = (lambda seq, n: seq[:n] if lo else seq[n:])
                if getattr(carry, phase) is not None:
                    getattr(carry, phase)(part(ins, na_in), part(outs, na_out), part(sems, na_sem))
        return run

    middle = both("middle") if (a.middle or b.middle) else None
    return _Carry(a.inputs + b.inputs, a.out_shapes + b.out_shapes, a.sems + b.sems, both("start"), both("finish"), middle)


def _hosted_call(body, carry, edge, *, name, grid, in_specs, out_specs, out_shape, scratch_shapes, compiler_params, inputs):
    n_in, n_out = len(in_specs), len(out_specs)
    inputs = [a if s.memory_space == pltpu.SMEM else _in_hbm(a)[0] for a, s in zip(inputs, in_specs)]
    out_shape = _hbm_out(list(out_shape))
    if carry is None:
        outs = pl.pallas_call(body, name=name, grid=grid, in_specs=in_specs, out_specs=out_specs, out_shape=out_shape,
                              scratch_shapes=scratch_shapes, compiler_params=compiler_params)(*inputs)
        return list(outs), []
    c_in, c_out, c_sem = len(carry.inputs), len(carry.out_shapes), len(carry.sems)

    def wrapped(*refs):
        ins, refs = refs[:n_in], refs[n_in:]
        cins, refs = refs[:c_in], refs[c_in:]
        outs, refs = refs[:n_out], refs[n_out:]
        couts, refs = refs[:c_out], refs[c_out:]
        scratch, csems = refs[:len(refs) - c_sem], refs[len(refs) - c_sem:]
        first, middle, last = edge()

        @pl.when(first)
        def _():
            carry.start(cins, couts, csems)

        body(*ins, *outs, *scratch)

        if carry.middle is not None:
            @pl.when(middle)
            def _():
                carry.middle(cins, couts, csems)

        @pl.when(last)
        def _():
            carry.finish(cins, couts, csems)

    outs = pl.pallas_call(
        wrapped, name=name, grid=grid, in_specs=list(in_specs) + [HBM_SPEC] * c_in,
        out_specs=list(out_specs) + [HBM_SPEC] * c_out, out_shape=out_shape + _hbm_out(carry.out_shapes),
        scratch_shapes=list(scratch_shapes) + carry.sems, compiler_params=compiler_params)(*inputs, *_in_hbm(*carry.inputs))
    return list(outs[:n_out]), list(outs[n_out:])


def _pass_on_step(n_steps):
    return max(0, min((7 * n_steps) // 8, n_steps - 2))


def _edge_1d(n_steps, pass_on_last=False):
    middle = n_steps - 1 if pass_on_last else _pass_on_step(n_steps)
    return lambda: (pl.program_id(0) == 0, pl.program_id(0) == middle, pl.program_id(0) == n_steps - 1)


def _edge_2d(n0, n1):
    def edge():
        step = pl.program_id(0) * n1 + pl.program_id(1)
        return step == 0, step == _pass_on_step(n0 * n1), step == n0 * n1 - 1
    return edge


def _run_carry(carry, name):
    c_in, c_out = len(carry.inputs), len(carry.out_shapes)

    def body(*refs):
        ins, outs, sems = refs[:c_in], refs[c_in:c_in + c_out], refs[c_in + c_out:]
        carry.start(ins, outs, sems)
        if carry.middle is not None:
            carry.middle(ins, outs, sems)
        carry.finish(ins, outs, sems)

    return pl.pallas_call(body, name=name, in_specs=[HBM_SPEC] * c_in, out_specs=[HBM_SPEC] * c_out,
                          out_shape=_hbm_out(carry.out_shapes), scratch_shapes=carry.sems)(*_in_hbm(*carry.inputs))


def _in_proj_fwd(x, g1, w_in_t, tile, carry=None):
    T = x.shape[0]

    def body(x_ref, g_ref, w_ref, q_ref, k_ref, v_ref, u_ref, ga_ref, gs_ref, h_ref):
        xv = x_ref[...]
        h = (xv * _rms_scale(xv) * g_ref[...]).astype(BF16)
        h_ref[...] = h
        outs = (q_ref, k_ref, v_ref, u_ref, ga_ref, gs_ref)
        for p, o_ref in enumerate(outs):
            o_ref[...] = _dot_nt(h, w_ref[SPLITS[p]:SPLITS[p + 1], :]).astype(o_ref.dtype)

    widths = [SPLITS[p + 1] - SPLITS[p] for p in range(6)] + [D_MODEL]
    dtypes = [BF16, BF16, BF16, F32, F32, F32, BF16]
    return _hosted_call(
        body, carry, _edge_1d(T // tile), name="in_proj_fwd", grid=(T // tile,),
        in_specs=[pl.BlockSpec((tile, D_MODEL), lambda i: (i, 0)), _const_spec((1, D_MODEL)), _const_spec((IN_W, D_MODEL))],
        out_specs=[pl.BlockSpec((tile, w), lambda i: (i, 0)) for w in widths],
        out_shape=[jax.ShapeDtypeStruct((T, w), dt) for w, dt in zip(widths, dtypes)],
        scratch_shapes=[], compiler_params=_params(("arbitrary",), VMEM_MID), inputs=(x, g1, w_in_t))


PROJ_PARTS = (512, 256, 512, 2048)
PROJ_GRAD_BLOCK = 256


def _in_proj_weight_grad(h, dparts):
    T = h.shape[0]
    blocks = [wd // PROJ_GRAD_BLOCK for wd in PROJ_PARTS]
    starts = [sum(blocks[:p]) for p in range(len(blocks))]

    def body(h_ref, *refs):
        part_refs, o_ref = refs[:-1], refs[-1]
        j = pl.program_id(0)
        for p_ref, start, count in zip(part_refs, starts, blocks):
            @pl.when((j >= start) & (j < start + count))
            def _(p_ref=p_ref):
                o_ref[...] = _dot_tn(p_ref[...], h_ref[...])

    def part_spec(start, count):
        return pl.BlockSpec((T, PROJ_GRAD_BLOCK), lambda j: (0, jnp.clip(j - start, 0, count - 1)))

    return pl.pallas_call(
        body, name="in_proj_weight_grad", grid=(sum(blocks),),
        in_specs=[_const_spec((T, D_MODEL))] + [part_spec(s, c) for s, c in zip(starts, blocks)],
        out_specs=pl.BlockSpec((PROJ_GRAD_BLOCK, D_MODEL), lambda j: (j, 0)),
        out_shape=_hbm_out(jax.ShapeDtypeStruct((IN_W, D_MODEL), F32)),
        compiler_params=_params(("arbitrary",), VMEM_MID),
    )(*_in_hbm(h, *dparts))


def _in_proj_input_grad(x, g1, w_in_t, dx1, dparts, tile, first_tile, n_tiles, name, carry=None):
    offsets = [sum(PROJ_PARTS[:p]) for p in range(len(PROJ_PARTS))]

    def body(x_ref, g_ref, w_ref, dx1_ref, *refs):
        part_refs, (gx_ref, dg_ref) = refs[:len(PROJ_PARTS)], refs[len(PROJ_PARTS):]
        i = pl.program_id(0)
        xv = x_ref[...]
        r = _rms_scale(xv)
        g = g_ref[...]
        dh = sum(_dot(p_ref[...], w_ref[off:off + wd, :]) for p_ref, off, wd in zip(part_refs, offsets, PROJ_PARTS))
        dxn, dg = _rms_bwd(dh, xv, r, g)
        gx_ref[...] = dx1_ref[...] + dxn

        @pl.when(i == 0)
        def _():
            dg_ref[...] = dg

        @pl.when(i > 0)
        def _():
            dg_ref[...] += dg

    tok = lambda wd: pl.BlockSpec((tile, wd), lambda i: (i + first_tile, 0))
    return _hosted_call(
        body, carry, _edge_1d(n_tiles), name=name, grid=(n_tiles,),
        in_specs=[tok(D_MODEL), _const_spec((1, D_MODEL)), _const_spec((IN_W, D_MODEL)), tok(D_MODEL)] + [tok(wd) for wd in PROJ_PARTS],
        out_specs=[pl.BlockSpec((tile, D_MODEL), lambda i: (i, 0)), pl.BlockSpec((1, D_MODEL), lambda i: (0, 0))],
        out_shape=[jax.ShapeDtypeStruct((n_tiles * tile, D_MODEL), F32), jax.ShapeDtypeStruct((1, D_MODEL), F32)],
        scratch_shapes=[], compiler_params=_params(("arbitrary",), VMEM_MID), inputs=(x, g1, w_in_t, dx1, *dparts))


def _bucket_table():
    qi = np.arange(BLOCK)[:, None]
    kj = np.arange(2 * BLOCK)[None, :]
    dist = qi + BLOCK - kj
    max_exact = N_BUCKETS // 2
    d = np.maximum(dist, 0)
    df = np.maximum(d, 1).astype(np.float32)
    large = max_exact + (np.log(df / np.float32(max_exact)) / np.float32(math.log(BLOCK / max_exact))
                         * np.float32(N_BUCKETS - max_exact)).astype(np.int32)
    large = np.minimum(large, N_BUCKETS - 1)
    bucket = np.where(d < max_exact, d, large)
    return np.where((dist >= 0) & (dist < BLOCK), bucket, -1).astype(np.int32)


def _build_bias(bucket_ref, rb_ref, bias_ref):
    bk = bucket_ref[...]
    for h in range(N_HEADS):
        def add(b, acc, h=h):
            return acc + jnp.where(bk == b, rb_ref[h, b], 0.0)
        bias_ref[h] = lax.fori_loop(0, N_BUCKETS, add, jnp.zeros((BLOCK, 2 * BLOCK), F32))


def _kv_variants(prev_ref, cur_ref):
    cat = jnp.concatenate([prev_ref[...], cur_ref[...]], axis=0)
    lo = lax.broadcasted_iota(jnp.int32, cat.shape, 1) < HEAD_DIM
    zero = jnp.zeros_like(cat)
    head0_lo = jnp.where(lo, cat, zero)
    head1_hi = jnp.where(lo, zero, cat)
    return ((head0_lo, pltpu.roll(head0_lo, HEAD_DIM, 1)), (pltpu.roll(head1_hi, HEAD_DIM, 1), head1_hi))


def _merge_kv_grads(g):
    lo = lax.broadcasted_iota(jnp.int32, g[0][0].shape, 1) < HEAD_DIM
    return jnp.where(lo, g[0][0] + pltpu.roll(g[0][1], HEAD_DIM, 1), g[1][1] + pltpu.roll(g[1][0], HEAD_DIM, 1))


def _head_lanes(h):
    return slice((h // 2) * LANES, (h // 2 + 1) * LANES)


def _attn_probs(q_ref, kvar, bias_ref, sk_ref, valid, s_ref):
    for h in range(N_HEADS):
        s_ref[h] = _dot_nt(q_ref[:, _head_lanes(h)], kvar[h // 4][h % 2])
    head = lax.broadcasted_iota(jnp.int32, (N_HEADS, 1, 1), 0)
    sink = jnp.zeros((N_HEADS, 1, 1), F32)
    for h in range(N_HEADS):
        sink = jnp.where(head == h, sk_ref[0, h], sink)
    s = jnp.where(valid[None], s_ref[...] * (HEAD_DIM ** -0.5) + bias_ref[...], NEG_INF)
    m = jnp.maximum(jnp.max(s, axis=-1, keepdims=True), sink)
    p = jnp.exp(s - m)
    e_sink = jnp.exp(sink - m)
    inv = 1.0 / (jnp.sum(p, axis=-1, keepdims=True) + e_sink)
    return p * inv, e_sink * inv


def _attn_valid(bucket_ref, n):
    col = lax.broadcasted_iota(jnp.int32, (BLOCK, 2 * BLOCK), 1)
    return (bucket_ref[...] >= 0) & ((n > 0) | (col >= BLOCK))


def _attn_fwd(q, k, v, bucket, rel_bias, sinks, carry=None):
    T = q.shape[0]
    nb = T // BLOCK

    def body(q_ref, kc_ref, kp_ref, vc_ref, vp_ref, bucket_ref, rb_ref, sk_ref, o_ref, bias_ref, s_ref, p_ref):
        n = pl.program_id(0)

        @pl.when(n == 0)
        def _():
            _build_bias(bucket_ref, rb_ref, bias_ref)

        kvar = _kv_variants(kp_ref, kc_ref)
        vvar = _kv_variants(vp_ref, vc_ref)
        pr, _ = _attn_probs(q_ref, kvar, bias_ref, sk_ref, _attn_valid(bucket_ref, n), s_ref)
        p_ref[...] = pr.astype(BF16)
        for m in range(N_HEADS // 2):
            acc = _dot(p_ref[2 * m], vvar[m // 2][0]) + _dot(p_ref[2 * m + 1], vvar[m // 2][1])
            o_ref[:, m * LANES:(m + 1) * LANES] = acc.astype(o_ref.dtype)

    cur = lambda w: pl.BlockSpec((BLOCK, w), lambda n: (n, 0))
    prev = lambda w: pl.BlockSpec((BLOCK, w), lambda n: (jnp.maximum(n - 1, 0), 0))
    smem = pl.BlockSpec(memory_space=pltpu.SMEM)
    return _hosted_call(
        body, carry, _edge_1d(nb, pass_on_last=True), name="attn_fwd", grid=(nb,),
        in_specs=[cur(ATTN_W), cur(KV_W), prev(KV_W), cur(KV_W), prev(KV_W), _const_spec((BLOCK, 2 * BLOCK)), smem, smem],
        out_specs=[cur(ATTN_W)],
        out_shape=[jax.ShapeDtypeStruct((T, ATTN_W), BF16)],
        scratch_shapes=[pltpu.VMEM((N_HEADS, BLOCK, 2 * BLOCK), F32), pltpu.VMEM((N_HEADS, BLOCK, 2 * BLOCK), F32),
                        pltpu.VMEM((N_HEADS, BLOCK, 2 * BLOCK), BF16)],
        compiler_params=_params(("arbitrary",)), inputs=(q, k, k, v, v, bucket, rel_bias, sinks))


ATTN_SMALL_ROWS = N_BUCKETS + SUBLANES


def _attn_bwd(q, k, v, datt, bucket, rel_bias, sinks, carry=None):
    T = q.shape[0]
    nb = T // BLOCK

    def body(q_ref, do_ref, kc_ref, kp_ref, vc_ref, vp_ref, bucket_ref, rb_ref, sk_ref,
             dq_ref, dkv_ref, small_ref, bias_ref, ds_sum_ref, dsink_ref, kcarry_ref, vcarry_ref,
             s_ref, dp_ref, p_ref, dsc_ref):
        n = pl.program_id(0)

        @pl.when(n == 0)
        def _():
            _build_bias(bucket_ref, rb_ref, bias_ref)
            ds_sum_ref[...] = jnp.zeros_like(ds_sum_ref)
            dsink_ref[...] = jnp.zeros_like(dsink_ref)
            kcarry_ref[...] = jnp.zeros_like(kcarry_ref)
            vcarry_ref[...] = jnp.zeros_like(vcarry_ref)

        @pl.when(n < nb)
        def _():
            kvar = _kv_variants(kp_ref, kc_ref)
            vvar = _kv_variants(vp_ref, vc_ref)
            pr, p_sink = _attn_probs(q_ref, kvar, bias_ref, sk_ref, _attn_valid(bucket_ref, n), s_ref)
            for h in range(N_HEADS):
                dp_ref[h] = _dot_nt(do_ref[:, _head_lanes(h)], vvar[h // 4][h % 2])
            dp = dp_ref[...]
            dsum = jnp.sum(pr * dp, axis=-1, keepdims=True)
            ds = pr * (dp - dsum)
            ds_sum_ref[...] += ds
            dsink_ref[...] -= jnp.sum(p_sink * dsum, axis=1, keepdims=True)
            dsc_ref[...] = (ds * (HEAD_DIM ** -0.5)).astype(BF16)
            p_ref[...] = pr.astype(BF16)
            for m in range(N_HEADS // 2):
                dqm = _dot(dsc_ref[2 * m], kvar[m // 2][0]) + _dot(dsc_ref[2 * m + 1], kvar[m // 2][1])
                dq_ref[:, m * LANES:(m + 1) * LANES] = dqm.astype(dq_ref.dtype)
            dk_var = [[None, None], [None, None]]
            dv_var = [[None, None], [None, None]]
            for kvh in range(2):
                for e in range(2):
                    heads = [h for h in range(N_HEADS) if h // 4 == kvh and h % 2 == e]
                    dk_var[kvh][e] = sum(_dot_tn(dsc_ref[h], q_ref[:, _head_lanes(h)]) for h in heads)
                    dv_var[kvh][e] = sum(_dot_tn(p_ref[h], do_ref[:, _head_lanes(h)]) for h in heads)
            dk_cat = _merge_kv_grads(dk_var)
            dv_cat = _merge_kv_grads(dv_var)

            @pl.when(n > 0)
            def _():
                dkv_ref[:, :KV_W] = (kcarry_ref[...] + dk_cat[:BLOCK]).astype(BF16)
                dkv_ref[:, KV_W:] = (vcarry_ref[...] + dv_cat[:BLOCK]).astype(BF16)

            kcarry_ref[...] = dk_cat[BLOCK:]
            vcarry_ref[...] = dv_cat[BLOCK:]

        @pl.when(n == nb)
        def _():
            dkv_ref[:, :KV_W] = kcarry_ref[...].astype(BF16)
            dkv_ref[:, KV_W:] = vcarry_ref[...].astype(BF16)
            bk = bucket_ref[...]
            row = lax.broadcasted_iota(jnp.int32, (N_HEADS, ATTN_SMALL_ROWS, LANES), 1)

            def add(b, acc):
                masked = jnp.where((bk == b)[None], ds_sum_ref[...], 0.0)
                val = jnp.sum(jnp.sum(masked, axis=1, keepdims=True), axis=2, keepdims=True)
                return acc + jnp.where(row == b, val, 0.0)

            small_ref[...] = lax.fori_loop(0, N_BUCKETS, add, jnp.where(row == N_BUCKETS, dsink_ref[...], 0.0))

    last = nb - 1
    cur = lambda w: pl.BlockSpec((BLOCK, w), lambda n: (jnp.minimum(n, last), 0))
    prev = lambda w: pl.BlockSpec((BLOCK, w), lambda n: (jnp.clip(n - 1, 0, last), 0))
    smem = pl.BlockSpec(memory_space=pltpu.SMEM)
    return _hosted_call(
        body, carry, _edge_1d(nb + 1), name="attn_bwd", grid=(nb + 1,),
        in_specs=[cur(ATTN_W), cur(ATTN_W), cur(KV_W), prev(KV_W), cur(KV_W), prev(KV_W),
                  _const_spec((BLOCK, 2 * BLOCK)), smem, smem],
        out_specs=[cur(ATTN_W), prev(2 * KV_W), pl.BlockSpec((N_HEADS, ATTN_SMALL_ROWS, LANES), lambda n: (0, 0, 0))],
        out_shape=[jax.ShapeDtypeStruct((T, ATTN_W), BF16), jax.ShapeDtypeStruct((T, 2 * KV_W), BF16),
                   jax.ShapeDtypeStruct((N_HEADS, ATTN_SMALL_ROWS, LANES), F32)],
        scratch_shapes=[pltpu.VMEM((N_HEADS, BLOCK, 2 * BLOCK), F32), pltpu.VMEM((N_HEADS, BLOCK, 2 * BLOCK), F32),
                        pltpu.VMEM((N_HEADS, 1, 1), F32), pltpu.VMEM((BLOCK, KV_W), F32), pltpu.VMEM((BLOCK, KV_W), F32),
                        pltpu.VMEM((N_HEADS, BLOCK, 2 * BLOCK), F32), pltpu.VMEM((N_HEADS, BLOCK, 2 * BLOCK), F32),
                        pltpu.VMEM((N_HEADS, BLOCK, 2 * BLOCK), BF16), pltpu.VMEM((N_HEADS, BLOCK, 2 * BLOCK), BF16)],
        compiler_params=_params(("arbitrary",)), inputs=(q, datt, k, k, v, v, bucket, rel_bias, sinks))


SCAN_UNROLL = 4


def _cmul(ar, ai, br, bi):
    return ar * br - ai * bi, ar * bi + ai * br


def _cmul_conj(ar, ai, br, bi):
    return ar * br + ai * bi, ar * bi - ai * br


def _ssm_discretize(lr, li, ldt):
    dt = jnp.exp(ldt)
    mag = jnp.exp(lr * dt)
    ab_re = mag * jnp.cos(li * dt)
    ab_im = mag * jnp.sin(li * dt)
    nr = ab_re - 1.0
    den = lr * lr + li * li
    f_re = (nr * lr + ab_im * li) / den
    f_im = (ab_im * lr - nr * li) / den
    return ab_re, ab_im, f_re, f_im


def _ssm_prep(lam_re, lam_im, ldt_rep, bd_re, bd_im):
    def body(lr_ref, li_ref, ldt_ref, bdr_ref, bdi_ref, ar_ref, ai_ref, br_ref, bi_ref):
        ab_re, ab_im, f_re, f_im = _ssm_discretize(lr_ref[...], li_ref[...], ldt_ref[...])
        ar_ref[...] = ab_re
        ai_ref[...] = ab_im
        bdr, bdi = bdr_ref[0], bdi_ref[0]
        br_ref[0] = (bdr * f_re - bdi * f_im).astype(BF16)
        bi_ref[0] = (bdi * f_re + bdr * f_im).astype(BF16)

    row = pl.BlockSpec((1, SSM_LANE_BLOCK), lambda j: (0, j))
    mat = pl.BlockSpec((1, LANES, SSM_LANE_BLOCK), lambda j: (j, 0, 0))
    return pl.pallas_call(
        body, name="ssm_prep", grid=(N_SSM_BLOCKS,),
        in_specs=[row, row, row, mat, mat], out_specs=[row, row, mat, mat],
        out_shape=[jax.ShapeDtypeStruct((1, STATES), F32)] * 2 + [jax.ShapeDtypeStruct((N_SSM_BLOCKS, LANES, SSM_LANE_BLOCK), BF16)] * 2,
        compiler_params=_params(("arbitrary",)),
    )(*_in_hbm(lam_re, lam_im, ldt_rep, bd_re, bd_im))


def _ssm_prep_bwd(lam_re, lam_im, ldt_rep, bd_re, bd_im, dbr, dbi, da_re, da_im):
    def body(lr_ref, li_ref, ldt_ref, bdr_ref, bdi_ref, dbr_ref, dbi_ref, dar_ref, dai_ref,
             dbdr_ref, dbdi_ref, dlr_ref, dli_ref, dldt_ref):
        lr, li, ldt = lr_ref[...], li_ref[...], ldt_ref[...]
        (_, _, f_re, f_im), vjp = jax.vjp(_ssm_discretize, lr, li, ldt)
        bdr, bdi, gbr, gbi = bdr_ref[0], bdi_ref[0], dbr_ref[0], dbi_ref[0]
        dbdr_ref[0] = gbr * f_re + gbi * f_im
        dbdi_ref[0] = gbi * f_re - gbr * f_im
        df_re = jnp.sum(gbr * bdr + gbi * bdi, axis=0, keepdims=True)
        df_im = jnp.sum(gbi * bdr - gbr * bdi, axis=0, keepdims=True)
        dlr, dli, dldt = vjp((dar_ref[...], dai_ref[...], df_re, df_im))
        dlr_ref[...] = dlr
        dli_ref[...] = dli
        dldt_ref[...] = dldt

    row = pl.BlockSpec((1, SSM_LANE_BLOCK), lambda j: (0, j))
    mat = pl.BlockSpec((1, LANES, SSM_LANE_BLOCK), lambda j: (j, 0, 0))
    mat_shape = jax.ShapeDtypeStruct((N_SSM_BLOCKS, LANES, SSM_LANE_BLOCK), F32)
    row_shape = jax.ShapeDtypeStruct((1, STATES), F32)
    return pl.pallas_call(
        body, name="ssm_prep_bwd", grid=(N_SSM_BLOCKS,),
        in_specs=[row, row, row, mat, mat, mat, mat, row, row], out_specs=[mat, mat, row, row, row],
        out_shape=[mat_shape, mat_shape, row_shape, row_shape, row_shape],
        compiler_params=_params(("arbitrary",)),
    )(*_in_hbm(lam_re, lam_im, ldt_rep, bd_re, bd_im, dbr, dbi, da_re, da_im))


def _group_sum(x):
    def body(x_ref, o_ref):
        o_ref[...] = jnp.sum(x_ref[...], axis=1, keepdims=True)
    return pl.pallas_call(body, name="ssm_group_sum", grid=(1,), in_specs=[_whole(x.shape)], out_specs=_whole((N_GROUPS, 1)),
                          out_shape=jax.ShapeDtypeStruct((N_GROUPS, 1), F32))(*_in_hbm(x))


def _power_table(ar, ai, p_re_ref, p_im_ref, steps):
    shape = (SUBLANES, SSM_LANE_BLOCK)
    p_re_ref[0:SUBLANES] = jnp.broadcast_to(ar, shape)
    p_im_ref[0:SUBLANES] = jnp.broadcast_to(ai, shape)
    m = 1
    while m < steps:
        rows = m * SUBLANES
        top_re = p_re_ref[rows - SUBLANES:rows]
        top_im = p_im_ref[rows - SUBLANES:rows]
        cur_re = p_re_ref[0:rows].reshape(m, SUBLANES, SSM_LANE_BLOCK)
        cur_im = p_im_ref[0:rows].reshape(m, SUBLANES, SSM_LANE_BLOCK)
        nxt_re, nxt_im = _cmul(cur_re, cur_im, top_re[None], top_im[None])
        p_re_ref[rows:2 * rows] = nxt_re.reshape(rows, SSM_LANE_BLOCK)
        p_im_ref[rows:2 * rows] = nxt_im.reshape(rows, SSM_LANE_BLOCK)
        m *= 2


def _to_segments(src_ref, dst_ref, steps):
    for s in range(SUBLANES):
        dst_ref[pl.ds(s, steps, stride=SUBLANES), :] = src_ref[s * steps:(s + 1) * steps, :]


def _from_segments(src_ref, dst_ref, steps):
    for s in range(SUBLANES):
        dst_ref[s * steps:(s + 1) * steps, :] = src_ref[pl.ds(s, steps, stride=SUBLANES), :]


def _segment_carries(e_re, e_im, an_re, an_im, c_re, c_im, reverse):
    order = range(SUBLANES - 1, -1, -1) if reverse else range(SUBLANES)
    ins_re, ins_im = [None] * SUBLANES, [None] * SUBLANES
    for s in order:
        ins_re[s], ins_im[s] = c_re, c_im
        pr, pi = _cmul(an_re, an_im, c_re, c_im)
        c_re = e_re[s:s + 1] + pr
        c_im = e_im[s:s + 1] + pi
    return jnp.concatenate(ins_re, axis=0), jnp.concatenate(ins_im, axis=0), c_re, c_im


def _ssm_fwd(u, a_re, a_im, b_re, b_im, c_re, c_im, d_skip, chunk, carry=None):
    T = u.shape[0]
    nc = T // chunk
    steps = chunk // SUBLANES
    blk = SSM_LANE_BLOCK

    def body(u_ref, ar_ref, ai_ref, br_ref, bi_ref, cr_ref, ci_ref, dk_ref,
             y_ref, hr_ref, hi_ref, inr_ref, ini_ref, useg_ref, yseg_ref, pr_ref, pi_ref, carry_ref):
        c = pl.program_id(1)
        ar, ai = ar_ref[...], ai_ref[...]

        @pl.when(c == 0)
        def _():
            _power_table(ar, ai, pr_ref, pi_ref, steps)
            carry_ref[...] = jnp.zeros_like(carry_ref)

        _to_segments(u_ref, useg_ref, steps)
        ub = useg_ref[...].astype(BF16)
        hr_ref[...] = _dot(ub, br_ref[0])
        hi_ref[...] = _dot(ub, bi_ref[0])
        first = slice(0, SUBLANES)

        def scan(t4, prev):
            for j in range(SCAN_UNROLL):
                rows = pl.ds(pl.multiple_of((t4 * SCAN_UNROLL + j) * SUBLANES, SUBLANES), SUBLANES)
                pr, pi = _cmul(pr_ref[first, :], pi_ref[first, :], prev[0], prev[1])
                prev = (pr + hr_ref[rows, :], pi + hi_ref[rows, :])
                hr_ref[rows, :] = prev[0]
                hi_ref[rows, :] = prev[1]
            return prev

        zero = jnp.zeros((SUBLANES, blk), F32)
        lax.fori_loop(0, steps // SCAN_UNROLL, scan, (zero, zero))

        top = slice(chunk - SUBLANES, chunk)
        in_re, in_im, out_re, out_im = _segment_carries(
            hr_ref[top, :], hi_ref[top, :], pr_ref[top, :][0:1], pi_ref[top, :][0:1],
            carry_ref[0:1, :], carry_ref[1:2, :], reverse=False)
        carry_ref[0:1, :] = out_re
        carry_ref[1:2, :] = out_im
        inr_ref[...] = in_re
        ini_ref[...] = in_im

        def fix(t4, _):
            for j in range(SCAN_UNROLL):
                rows = pl.ds(pl.multiple_of((t4 * SCAN_UNROLL + j) * SUBLANES, SUBLANES), SUBLANES)
                fr, fi = _cmul(pr_ref[rows, :], pi_ref[rows, :], in_re, in_im)
                hr_ref[rows, :] += fr
                hi_ref[rows, :] += fi
            return 0

        lax.fori_loop(0, steps // SCAN_UNROLL, fix, 0)

        yseg_ref[...] = _dot(hr_ref[...].astype(BF16), cr_ref[0]) - _dot(hi_ref[...].astype(BF16), ci_ref[0])
        _from_segments(yseg_ref, y_ref, steps)
        y_ref[...] += dk_ref[...] * u_ref[...]

    row = pl.BlockSpec((1, blk), lambda j, c: (0, j))
    b_mat = pl.BlockSpec((1, LANES, blk), lambda j, c: (j, 0, 0))
    c_mat = pl.BlockSpec((1, blk, LANES), lambda j, c: (j, 0, 0))
    tok = pl.BlockSpec((chunk, LANES), lambda j, c: (c, j))
    state = pl.BlockSpec((chunk, blk), lambda j, c: (c, j))
    enter = pl.BlockSpec((SUBLANES, blk), lambda j, c: (c, j))
    return _hosted_call(
        body, carry, _edge_2d(N_SSM_BLOCKS, nc), name="ssm_fwd", grid=(N_SSM_BLOCKS, nc),
        in_specs=[tok, row, row, b_mat, b_mat, c_mat, c_mat, pl.BlockSpec((1, LANES), lambda j, c: (0, j))],
        out_specs=[tok, state, state, enter, enter],
        out_shape=[jax.ShapeDtypeStruct((T, SSM_W), F32), jax.ShapeDtypeStruct((T, STATES), F32),
                   jax.ShapeDtypeStruct((T, STATES), F32), jax.ShapeDtypeStruct((nc * SUBLANES, STATES), F32),
                   jax.ShapeDtypeStruct((nc * SUBLANES, STATES), F32)],
        scratch_shapes=[pltpu.VMEM((chunk, LANES), F32), pltpu.VMEM((chunk, LANES), F32),
                        pltpu.VMEM((chunk, blk), F32), pltpu.VMEM((chunk, blk), F32), pltpu.VMEM((SUBLANES, blk), F32)],
        compiler_params=_params(("arbitrary", "arbitrary"), VMEM_MID),
        inputs=(u, a_re, a_im, b_re, b_im, c_re, c_im, d_skip))


def _ssm_bwd(dy, u, h_re, h_im, in_re, in_im, a_re, a_im, b_re, b_im, c_re, c_im, d_skip, chunk, carry=None):
    T = u.shape[0]
    nc = T // chunk
    steps = chunk // SUBLANES
    blk = SSM_LANE_BLOCK

    def body(dy_ref, u_ref, hr_ref, hi_ref, inr_ref, ini_ref, ar_ref, ai_ref, br_ref, bi_ref, cr_ref, ci_ref, dk_ref,
             du_ref, dbr_ref, dbi_ref, dcr_ref, dci_ref, dar_ref, dai_ref, ddk_ref,
             dyseg_ref, useg_ref, duseg_ref, gr_ref, gi_ref, pr_ref, pi_ref, carry_ref, accr_ref, acci_ref):
        c = pl.program_id(1)
        ar, ai = ar_ref[...], ai_ref[...]

        @pl.when(c == 0)
        def _():
            _power_table(ar, ai, pr_ref, pi_ref, steps)
            carry_ref[...] = jnp.zeros_like(carry_ref)
            accr_ref[...] = jnp.zeros_like(accr_ref)
            acci_ref[...] = jnp.zeros_like(acci_ref)

        _to_segments(dy_ref, dyseg_ref, steps)
        _to_segments(u_ref, useg_ref, steps)
        dyb = dyseg_ref[...].astype(BF16)
        ub = useg_ref[...].astype(BF16)
        gr_ref[...] = _dot_nt(dyb, cr_ref[0])
        gi_ref[...] = -_dot_nt(dyb, ci_ref[0])
        dcr = _dot_tn(hr_ref[...].astype(BF16), dyb)
        dci = -_dot_tn(hi_ref[...].astype(BF16), dyb)
        ddk = jnp.sum(dy_ref[...] * u_ref[...], axis=0, keepdims=True)

        first = slice(0, SUBLANES)

        def scan(k4, nxt):
            for j in range(SCAN_UNROLL):
                t = steps - 1 - (k4 * SCAN_UNROLL + j)
                rows = pl.ds(pl.multiple_of(t * SUBLANES, SUBLANES), SUBLANES)
                pr, pi = _cmul_conj(pr_ref[first, :], pi_ref[first, :], nxt[0], nxt[1])
                nxt = (pr + gr_ref[rows, :], pi + gi_ref[rows, :])
                gr_ref[rows, :] = nxt[0]
                gi_ref[rows, :] = nxt[1]
            return nxt

        top = slice(chunk - SUBLANES, chunk)
        zero = jnp.zeros((SUBLANES, blk), F32)
        lax.fori_loop(0, steps // SCAN_UNROLL, scan, (zero, zero))

        gin_re, gin_im, out_re, out_im = _segment_carries(
            gr_ref[0:SUBLANES, :], gi_ref[0:SUBLANES, :], pr_ref[top, :][0:1], -pi_ref[top, :][0:1],
            carry_ref[0:1, :], carry_ref[1:2, :], reverse=True)
        carry_ref[0:1, :] = out_re
        carry_ref[1:2, :] = out_im

        def fix_row(rows, prow, hp_re, hp_im, acc):
            fr, fi = _cmul_conj(pr_ref[prow, :], pi_ref[prow, :], gin_re, gin_im)
            g_re = gr_ref[rows, :] + fr
            g_im = gi_ref[rows, :] + fi
            gr_ref[rows, :] = g_re
            gi_ref[rows, :] = g_im
            return acc[0] + g_re * hp_re + g_im * hp_im, acc[1] + g_im * hp_re - g_re * hp_im

        def fix_at(t, acc):
            aligned = (lambda r: r * SUBLANES) if isinstance(t, int) else (lambda r: pl.multiple_of(r * SUBLANES, SUBLANES))
            rows, before, prow = (pl.ds(aligned(r), SUBLANES) for r in (t, t - 1, steps - 1 - t))
            return fix_row(rows, prow, hr_ref[before, :], hi_ref[before, :], acc)

        def fix(t4, acc):
            for j in range(SCAN_UNROLL):
                acc = fix_at(t4 * SCAN_UNROLL + j, acc)
            return acc

        acc = fix_row(first, top, inr_ref[...], ini_ref[...], (accr_ref[...], acci_ref[...]))
        for t in range(1, SCAN_UNROLL):
            acc = fix_at(t, acc)
        acc_re, acc_im = lax.fori_loop(1, steps // SCAN_UNROLL, fix, acc)
        accr_ref[...] = acc_re
        acci_ref[...] = acc_im

        gbr = gr_ref[...].astype(BF16)
        gbi = gi_ref[...].astype(BF16)
        duseg_ref[...] = _dot_nt(gbr, br_ref[0]) + _dot_nt(gbi, bi_ref[0])
        _from_segments(duseg_ref, dyseg_ref, steps)
        du_ref[...] = (dyseg_ref[...] + dk_ref[...] * dy_ref[...]).astype(BF16)
        dbr = _dot_tn(ub, gbr)
        dbi = _dot_tn(ub, gbi)

        @pl.when(c == 0)
        def _():
            dbr_ref[0] = dbr
            dbi_ref[0] = dbi
            dcr_ref[0] = dcr
            dci_ref[0] = dci
            ddk_ref[...] = ddk

        @pl.when(c > 0)
        def _():
            dbr_ref[0] += dbr
            dbi_ref[0] += dbi
            dcr_ref[0] += dcr
            dci_ref[0] += dci
            ddk_ref[...] += ddk

        @pl.when(c == nc - 1)
        def _():
            dar_ref[...] = jnp.sum(acc_re, axis=0, keepdims=True)
            dai_ref[...] = jnp.sum(acc_im, axis=0, keepdims=True)

    rev = lambda c: nc - 1 - c
    row = pl.BlockSpec((1, blk), lambda j, c: (0, j))
    b_mat = pl.BlockSpec((1, LANES, blk), lambda j, c: (j, 0, 0))
    c_mat = pl.BlockSpec((1, blk, LANES), lambda j, c: (j, 0, 0))
    tok = pl.BlockSpec((chunk, LANES), lambda j, c: (rev(c), j))
    state = pl.BlockSpec((chunk, blk), lambda j, c: (rev(c), j))
    enter = pl.BlockSpec((SUBLANES, blk), lambda j, c: (rev(c), j))
    chan = pl.BlockSpec((1, LANES), lambda j, c: (0, j))
    f32 = lambda *s: jax.ShapeDtypeStruct(s, F32)
    return _hosted_call(
        body, carry, _edge_2d(N_SSM_BLOCKS, nc), name="ssm_bwd", grid=(N_SSM_BLOCKS, nc),
        in_specs=[tok, tok, state, state, enter, enter, row, row, b_mat, b_mat, c_mat, c_mat, chan],
        out_specs=[tok, b_mat, b_mat, c_mat, c_mat, row, row, chan],
        out_shape=[jax.ShapeDtypeStruct((T, SSM_W), BF16), f32(N_SSM_BLOCKS, LANES, blk), f32(N_SSM_BLOCKS, LANES, blk),
                   f32(N_SSM_BLOCKS, blk, LANES), f32(N_SSM_BLOCKS, blk, LANES), f32(1, STATES), f32(1, STATES), f32(1, SSM_W)],
        scratch_shapes=[pltpu.VMEM((chunk, LANES), F32), pltpu.VMEM((chunk, LANES), F32), pltpu.VMEM((chunk, LANES), F32),
                        pltpu.VMEM((chunk, blk), F32), pltpu.VMEM((chunk, blk), F32),
                        pltpu.VMEM((chunk, blk), F32), pltpu.VMEM((chunk, blk), F32),
                        pltpu.VMEM((SUBLANES, blk), F32), pltpu.VMEM((SUBLANES, blk), F32), pltpu.VMEM((SUBLANES, blk), F32)],
        compiler_params=_params(("arbitrary", "arbitrary"), VMEM_BIG),
        inputs=(dy, u, h_re, h_im, in_re, in_im, a_re, a_im, b_re, b_im, c_re, c_im, d_skip))


def _merge_forward(y, att, ga, gs, w_glu, w_ssm, w_attn):
    z = jax.nn.gelu(y)
    zb = z.astype(BF16)
    gl = jax.nn.sigmoid(_dot(zb, w_glu))
    z2b = (z * gl).astype(BF16)
    y_ssm = _dot(z2b, w_ssm)
    y_attn = _dot(att, w_attn)
    sa = jax.nn.sigmoid(ga)
    ss = jax.nn.sigmoid(gs)
    merged = (sa * y_attn + ss * y_ssm).astype(BF16)
    return z, zb, gl, z2b, y_ssm, y_attn, sa, ss, merged


def _merge_fwd(x, y, att, ga, gs, g2, g3, w_glu, w_ssm, w_attn, w_out, tile):
    T = x.shape[0]

    def body(x_ref, y_ref, att_ref, ga_ref, gs_ref, g2_ref, g3_ref, wg_ref, ws_ref, wa_ref, wo_ref, x1_ref, o_ref, h2_ref):
        merged = _merge_forward(y_ref[...], att_ref[...], ga_ref[...], gs_ref[...], wg_ref[...], ws_ref[...], wa_ref[...])[-1]
        o = _dot(merged, wo_ref[...])
        x1 = x_ref[...] + o * _rms_scale(o) * g2_ref[...]
        o_ref[...] = o
        x1_ref[...] = x1
        h2_ref[...] = (x1 * _rms_scale(x1) * g3_ref[...]).astype(BF16)

    tok = lambda w: pl.BlockSpec((tile, w), lambda i: (i, 0))
    vec = _const_spec((1, D_MODEL))
    return pl.pallas_call(
        body, name="merge_fwd", grid=(T // tile,),
        in_specs=[tok(D_MODEL), tok(SSM_W), tok(ATTN_W), tok(D_MODEL), tok(D_MODEL), vec, vec,
                  _const_spec((SSM_W, SSM_W)), _const_spec((SSM_W, D_MODEL)), _const_spec((ATTN_W, D_MODEL)),
                  _const_spec((D_MODEL, D_MODEL))],
        out_specs=[tok(D_MODEL), tok(D_MODEL), tok(D_MODEL)],
        out_shape=_hbm_out([jax.ShapeDtypeStruct((T, D_MODEL), F32), jax.ShapeDtypeStruct((T, D_MODEL), F32),
                            jax.ShapeDtypeStruct((T, D_MODEL), BF16)]),
        compiler_params=_params(("arbitrary",), VMEM_MID),
    )(*_in_hbm(x, y, att, ga, gs, g2, g3, w_glu, w_ssm, w_attn, w_out))


def _merge_bwd(dh2, dx2, x1, o, y, att, ga, gs, g2, g3, w_glu, w_ssm, w_attn, w_out, tile, carry=None):
    T = x1.shape[0]
    n_steps = T // tile

    group = min(2, n_steps)
    staged_widths = (D_MODEL, D_MODEL, ATTN_W, D_MODEL, SSM_W, D_MODEL, SSM_W, SSM_W)

    def body(dh2_ref, dx2_ref, x1_ref, o_ref, y_ref, att_ref, ga_ref, gs_ref, g2_ref, g3_ref, wg_ref, ws_ref, wa_ref, wo_ref,
             dx1_ref, dgates_ref, datt_ref, dy_ref, dwg_hbm, dws_hbm, dwa_hbm, dwo_hbm, dg2_ref, dg3_ref,
             awg_ref, aws_ref, awa_ref, awo_ref, *staged):
        i = pl.program_id(0)
        x1v, ov = x1_ref[...], o_ref[...]
        dxn, dg3 = _rms_bwd(dh2_ref[...], x1v, _rms_scale(x1v), g3_ref[...])
        dx1 = dx2_ref[...] + dxn
        dx1_ref[...] = dx1
        do, dg2 = _rms_bwd(dx1, ov, _rms_scale(ov), g2_ref[...])
        dob = do.astype(BF16)

        yv = y_ref[...]
        att = att_ref[...]
        z, zb, gl, z2b, y_ssm, y_attn, sa, ss, merged = _merge_forward(
            yv, att, ga_ref[...], gs_ref[...], wg_ref[...], ws_ref[...], wa_ref[...])
        dmerged = _dot_nt(dob, wo_ref[...])
        dya = (dmerged * sa).astype(BF16)
        dys = (dmerged * ss).astype(BF16)
        dgates_ref[:, :D_MODEL] = (dmerged * y_attn * sa * (1.0 - sa)).astype(BF16)
        dgates_ref[:, D_MODEL:] = (dmerged * y_ssm * ss * (1.0 - ss)).astype(BF16)
        datt_ref[...] = _dot_nt(dya, wa_ref[...]).astype(BF16)
        dz2 = _dot_nt(dys, ws_ref[...])
        dpre = (dz2 * z * gl * (1.0 - gl)).astype(BF16)
        dz = dz2 * gl + _dot_nt(dpre, wg_ref[...])
        _, gelu_vjp = jax.vjp(jax.nn.gelu, yv)
        dy_ref[...] = gelu_vjp(dz)[0]

        part = pl.ds(pl.multiple_of((i % group) * tile, tile), tile)
        for ref, val in zip(staged, (merged, dob, att, dya, z2b, dys, zb, dpre)):
            ref[part, :] = val

        @pl.when(i == 0)
        def _():
            dg2_ref[...] = dg2
            dg3_ref[...] = dg3

        @pl.when(i > 0)
        def _():
            dg2_ref[...] += dg2
            dg3_ref[...] += dg3

        def weight_grads():
            s_merged, s_dob, s_att, s_dya, s_z2b, s_dys, s_zb, s_dpre = (ref[...] for ref in staged)
            return ((awo_ref, _dot_tn(s_merged, s_dob)), (awa_ref, _dot_tn(s_att, s_dya)),
                    (aws_ref, _dot_tn(s_z2b, s_dys)), (awg_ref, _dot_tn(s_zb, s_dpre)))

        @pl.when(i == group - 1)
        def _():
            for ref, val in weight_grads():
                ref[...] = val

        @pl.when((i % group == group - 1) & (i > group - 1))
        def _():
            for ref, val in weight_grads():
                ref[...] += val

        @pl.when(i == n_steps - 1)
        def _():
            pltpu.sync_copy(awg_ref, dwg_hbm)
            pltpu.sync_copy(aws_ref, dws_hbm)
            pltpu.sync_copy(awa_ref, dwa_hbm)
            pltpu.sync_copy(awo_ref, dwo_hbm)

    tok = lambda w: pl.BlockSpec((tile, w), lambda i: (i, 0))
    vec = _const_spec((1, D_MODEL))
    any_ = pl.BlockSpec(memory_space=pl.ANY)
    vec_out = pl.BlockSpec((1, D_MODEL), lambda i: (0, 0))
    f32 = lambda *s: jax.ShapeDtypeStruct(s, F32)
    bf = lambda *s: jax.ShapeDtypeStruct(s, BF16)
    return _hosted_call(
        body, carry, _edge_1d(n_steps), name="merge_bwd", grid=(n_steps,),
        in_specs=[tok(D_MODEL), tok(D_MODEL), tok(D_MODEL), tok(D_MODEL), tok(SSM_W), tok(ATTN_W), tok(D_MODEL), tok(D_MODEL),
                  vec, vec, _const_spec((SSM_W, SSM_W)), _const_spec((SSM_W, D_MODEL)), _const_spec((ATTN_W, D_MODEL)),
                  _const_spec((D_MODEL, D_MODEL))],
        out_specs=[tok(D_MODEL), tok(2 * D_MODEL), tok(ATTN_W), tok(SSM_W), any_, any_, any_, any_, vec_out, vec_out],
        out_shape=[f32(T, D_MODEL), bf(T, 2 * D_MODEL), bf(T, ATTN_W), f32(T, SSM_W),
                   f32(SSM_W, SSM_W), f32(SSM_W, D_MODEL), f32(ATTN_W, D_MODEL), f32(D_MODEL, D_MODEL),
                   f32(1, D_MODEL), f32(1, D_MODEL)],
        scratch_shapes=[pltpu.VMEM((SSM_W, SSM_W), F32), pltpu.VMEM((SSM_W, D_MODEL), F32),
                        pltpu.VMEM((ATTN_W, D_MODEL), F32), pltpu.VMEM((D_MODEL, D_MODEL), F32)]
        + [pltpu.VMEM((group * tile, wd), BF16) for wd in staged_widths],
        compiler_params=_params(("arbitrary",), VMEM_BIG),
        inputs=(dh2, dx2, x1, o, y, att, ga, gs, g2, g3, w_glu, w_ssm, w_attn, w_out))


FF_SHARD = D_FF // N_DEV


def _mlp_fwd(h2, x1, target, g4, w_ff_in, w_ff_out, tile):
    T = h2.shape[0]
    col_chunk = 2 * FF_SHARD

    def body(h2_ref, x1_ref, tg_ref, g4_ref, wi_ref, wo_ref, a_ref, dfo_ref, dx2_ref, loss_ref, dg4_ref, rr_ref):
        i = pl.program_id(0)
        h2v = h2_ref[...]
        for c in range(D_FF // col_chunk):
            cols = slice(c * col_chunk, (c + 1) * col_chunk)
            a = _dot_nt(h2v, wi_ref[cols, :])
            a_ref[:, cols] = a.astype(BF16)
            ra = jnp.maximum(a, 0.0)
            rr_ref[:, cols] = (ra * ra).astype(BF16)
        f = _dot(rr_ref[...], wo_ref[...])
        r = _rms_scale(f)
        g = g4_ref[...]
        err = x1_ref[...] + f * r * g - tg_ref[...]
        dx2 = err * (1.0 / D_MODEL)
        dx2_ref[...] = dx2
        dfo, dg = _rms_bwd(dx2, f, r, g)
        dfo_ref[...] = dfo.astype(BF16)
        row = lax.broadcasted_iota(jnp.int32, (SUBLANES, LANES), 0)
        col = lax.broadcasted_iota(jnp.int32, (SUBLANES, LANES), 1)
        loss = jnp.where((row == 0) & (col == 0), (0.5 / D_MODEL) * jnp.sum(err * err), 0.0)

        @pl.when(i == 0)
        def _():
            loss_ref[...] = loss
            dg4_ref[...] = dg

        @pl.when(i > 0)
        def _():
            loss_ref[...] += loss
            dg4_ref[...] += dg

    tok = pl.BlockSpec((tile, D_MODEL), lambda i: (i, 0))
    return pl.pallas_call(
        body, name="mlp_fwd", grid=(T // tile,),
        in_specs=[tok, tok, tok, _const_spec((1, D_MODEL)), _const_spec((D_FF, D_MODEL)), _const_spec((D_FF, D_MODEL))],
        out_specs=[pl.BlockSpec((tile, D_FF), lambda i: (i, 0)), tok, tok,
                   pl.BlockSpec((SUBLANES, LANES), lambda i: (0, 0)), pl.BlockSpec((1, D_MODEL), lambda i: (0, 0))],
        out_shape=_hbm_out([jax.ShapeDtypeStruct((T, D_FF), BF16), jax.ShapeDtypeStruct((T, D_MODEL), BF16),
                            jax.ShapeDtypeStruct((T, D_MODEL), F32), jax.ShapeDtypeStruct((SUBLANES, LANES), F32),
                            jax.ShapeDtypeStruct((1, D_MODEL), F32)]),
        scratch_shapes=[pltpu.VMEM((tile, D_FF), BF16)],
        compiler_params=_params(("arbitrary",), VMEM_MAX),
    )(*_in_hbm(h2, x1, target, g4, w_ff_in.reshape(D_FF, D_MODEL), w_ff_out.reshape(D_FF, D_MODEL)))


def _mlp_weight_grads(dfo, a, h2, w_ff_out, row_chunk):
    T = h2.shape[0]

    def body(dfo_ref, h2_ref, a_ref, wo_ref, dwi_ref, dwo_ref, da_ref, rr_ref):
        def rows(r, _):
            sl = pl.ds(pl.multiple_of(r * row_chunk, row_chunk), row_chunk)
            ra = jnp.maximum(a_ref[sl, :].astype(F32), 0.0)
            da_ref[sl, :] = (_dot_nt(dfo_ref[sl, :], wo_ref[0]) * (2.0 * ra)).astype(BF16)
            rr_ref[sl, :] = (ra * ra).astype(BF16)
            return 0

        lax.fori_loop(0, T // row_chunk, rows, 0)
        dwo_ref[0] = _dot_tn(rr_ref[...], dfo_ref[...])
        dwi_ref[0] = _dot_tn(h2_ref[...], da_ref[...])

    return pl.pallas_call(
        body, name="mlp_weight_grads", grid=(N_DEV,),
        in_specs=[_const_spec((T, D_MODEL)), _const_spec((T, D_MODEL)), pl.BlockSpec((T, FF_SHARD), lambda k: (0, k)),
                  pl.BlockSpec((1, FF_SHARD, D_MODEL), lambda k: (k, 0, 0))],
        out_specs=[pl.BlockSpec((1, D_MODEL, FF_SHARD), lambda k: (k, 0, 0)),
                   pl.BlockSpec((1, FF_SHARD, D_MODEL), lambda k: (k, 0, 0)), pl.BlockSpec((T, FF_SHARD), lambda k: (0, k))],
        out_shape=_hbm_out([jax.ShapeDtypeStruct((N_DEV, D_MODEL, FF_SHARD), F32),
                            jax.ShapeDtypeStruct((N_DEV, FF_SHARD, D_MODEL), F32), jax.ShapeDtypeStruct((T, D_FF), BF16)]),
        scratch_shapes=[pltpu.VMEM((T, FF_SHARD), BF16)],
        compiler_params=_params(("arbitrary",), VMEM_MAX),
    )(*_in_hbm(dfo, h2, a, w_ff_out))


def _mlp_input_grad(da, w_ff_in_t, tile):
    T = da.shape[0]

    def body(da_ref, w_ref, o_ref):
        o_ref[...] = _dot(da_ref[...], w_ref[...])

    return pl.pallas_call(
        body, name="mlp_input_grad", grid=(T // tile,),
        in_specs=[pl.BlockSpec((tile, D_FF), lambda i: (i, 0)), _const_spec((D_FF, D_MODEL))],
        out_specs=pl.BlockSpec((tile, D_MODEL), lambda i: (i, 0)),
        out_shape=_hbm_out(jax.ShapeDtypeStruct((T, D_MODEL), F32)),
        compiler_params=_params(("arbitrary",), VMEM_MID),
    )(*_in_hbm(da, w_ff_in_t))


def _block_diag_in(b):
    bt = b.reshape(N_SSM_BLOCKS, GROUPS_PER_BLOCK, GROUP_CH, N_STATE)
    eye = jnp.eye(GROUPS_PER_BLOCK, dtype=b.dtype)
    return jnp.einsum("jacp,ab->jacbp", bt, eye).reshape(N_SSM_BLOCKS, LANES, SSM_LANE_BLOCK)


def _block_diag_in_grad(g):
    g = g.reshape(N_SSM_BLOCKS, GROUPS_PER_BLOCK, GROUP_CH, GROUPS_PER_BLOCK, N_STATE)
    d = jnp.diagonal(g, axis1=1, axis2=3)
    return jnp.transpose(d, (0, 3, 1, 2)).reshape(N_GROUPS, GROUP_CH, N_STATE)


def _block_diag_out(c):
    ct = c.reshape(N_SSM_BLOCKS, GROUPS_PER_BLOCK, GROUP_CH, N_STATE)
    eye = jnp.eye(GROUPS_PER_BLOCK, dtype=c.dtype)
    return jnp.einsum("jacp,ab->japbc", ct, eye).reshape(N_SSM_BLOCKS, SSM_LANE_BLOCK, LANES)


def _block_diag_out_grad(g):
    g = g.reshape(N_SSM_BLOCKS, GROUPS_PER_BLOCK, N_STATE, GROUPS_PER_BLOCK, GROUP_CH)
    d = jnp.diagonal(g, axis1=1, axis2=3)
    return jnp.transpose(d, (0, 3, 2, 1)).reshape(N_GROUPS, GROUP_CH, N_STATE)


def _tiles(T):
    return dict(proj=min(512, T), proj_bwd=min(512, T // 2), merge=min(512, T), merge_bwd=min(256, T),
                mlp_fwd=min(512, T), mlp_bwd=min(512, T), ssm_chunk=min(1024, T))


def _mesh_position():
    x, y, c = lax.axis_index("x"), lax.axis_index("y"), lax.axis_index("c")
    other_chips = [(1 - x, y), (x, 1 - y), (1 - x, 1 - y)]
    return x, y, c, other_chips


def _gather_carry(arrays):
    n = len(arrays)

    def copies(ins, outs, sems):
        send_sems, recv_sems, local_sems = sems
        x, y, c, chips = _mesh_position()
        me, sibling = (x, y, c), (x, y, 1 - c)

        def copy(a, k, block, to, src=None):
            px, py, pc = block
            dst = outs[a].at[4 * px + 2 * py + pc]
            return pltpu.make_async_remote_copy(
                src_ref=dst if src is None else src, dst_ref=dst, send_sem=send_sems.at[7 * a + k],
                recv_sem=recv_sems.at[7 * a + k], device_id=to, device_id_type=MESH_IDS)

        mine = [pltpu.make_async_copy(ins[a], outs[a].at[4 * x + 2 * y + c], local_sems.at[a]) for a in range(n)]
        first = []
        for a in range(n):
            first.append(copy(a, 0, me, sibling, src=ins[a]))
            first += [copy(a, 1 + j, me, (*chip, c), src=ins[a]) for j, chip in enumerate(chips)]
        return copy, mine, first, me, sibling, chips, c

    def start(ins, outs, sems):
        _, mine, first, *_ = copies(ins, outs, sems)
        for cp in mine + first:
            cp.start()

    def passed_on(copy, sibling, chips, c):
        return [copy(a, 4 + j, (*chip, c), sibling) for a in range(n) for j, chip in enumerate(chips)]

    def middle(ins, outs, sems):
        copy, _, _, me, sibling, chips, c = copies(ins, outs, sems)
        for a in range(n):
            for j, chip in enumerate(chips):
                copy(a, 1 + j, (*chip, c), me).wait_recv()
                copy(a, 4 + j, (*chip, c), sibling).start()

    def finish(ins, outs, sems):
        copy, mine, first, me, sibling, chips, c = copies(ins, outs, sems)
        for a in range(n):
            copy(a, 0, sibling, me).wait_recv()
            for j, chip in enumerate(chips):
                copy(a, 4 + j, (*chip, 1 - c), me).wait_recv()
        for cp in first + passed_on(copy, sibling, chips, c):
            cp.wait_send()
        for cp in mine:
            cp.wait()

    return _Carry(arrays, [jax.ShapeDtypeStruct((N_DEV,) + a.shape, a.dtype) for a in arrays],
                  [pltpu.SemaphoreType.DMA((7 * n,)), pltpu.SemaphoreType.DMA((7 * n,)), pltpu.SemaphoreType.DMA((n,))],
                  start, finish, middle)


def _pairwise_carry(arrays, n_slots, make_copies):
    n = len(arrays)

    def start(ins, outs, sems):
        for cp in make_copies(ins, outs, sems):
            cp.start()

    def finish(ins, outs, sems):
        for cp in make_copies(ins, outs, sems):
            cp.wait()

    return _Carry(arrays, [jax.ShapeDtypeStruct((n_slots,) + a.shape[1:], a.dtype) for a in arrays],
                  [pltpu.SemaphoreType.DMA((n_slots * n,)), pltpu.SemaphoreType.DMA((n_slots * n,))], start, finish)


def _sibling_carry(grads):
    def make_copies(ins, outs, sems):
        x, y, c, _ = _mesh_position()
        return [pltpu.make_async_remote_copy(
            src_ref=ins[a].at[2 * ch + (1 - c)], dst_ref=outs[a].at[ch], send_sem=sems[0].at[4 * a + ch],
            recv_sem=sems[1].at[4 * a + ch], device_id=(x, y, 1 - c), device_id_type=MESH_IDS)
            for a in range(len(grads)) for ch in range(4)]

    return _pairwise_carry(grads, 4, make_copies)


def _chips_carry(sums):
    def make_copies(ins, outs, sems):
        x, y, c, chips = _mesh_position()
        return [pltpu.make_async_remote_copy(
            src_ref=ins[a].at[2 * px + py], dst_ref=outs[a].at[j], send_sem=sems[0].at[3 * a + j],
            recv_sem=sems[1].at[3 * a + j], device_id=(px, py, c), device_id_type=MESH_IDS)
            for a in range(len(sums)) for j, (px, py) in enumerate(chips)]

    return _pairwise_carry(sums, 3, make_copies)


def _everyone_carry(arrays):
    def make_copies(ins, outs, sems):
        x, y, c, _ = _mesh_position()
        flip = lambda v, bit: 1 - v if bit else v
        return [pltpu.make_async_remote_copy(
            src_ref=ins[a], dst_ref=outs[a].at[r - 1], send_sem=sems[0].at[7 * a + r - 1], recv_sem=sems[1].at[7 * a + r - 1],
            device_id=(flip(x, r & 4), flip(y, r & 2), flip(c, r & 1)), device_id_type=MESH_IDS)
            for a in range(len(arrays)) for r in range(1, N_DEV)]

    carry = _pairwise_carry([jax.ShapeDtypeStruct((1,) + a.shape, a.dtype) for a in arrays], N_DEV - 1, make_copies)
    carry.inputs = list(arrays)
    return carry


def _sum_everyone(own, received, me, name, after=()):
    def body(me_ref, own_ref, r_ref, *refs):
        g = None
        for d in range(N_DEV):
            relation = jnp.bitwise_xor(d, me_ref[0])
            part = jnp.where(relation == 0, own_ref[...], r_ref[jnp.maximum(relation - 1, 0)])
            g = part if g is None else g + part
        refs[-1][...] = g

    whole = lambda shape: pl.BlockSpec(shape, lambda i, me_ref: (0,) * len(shape))
    return pl.pallas_call(
        body, name=name,
        grid_spec=pltpu.PrefetchScalarGridSpec(
            num_scalar_prefetch=1, grid=(1,), in_specs=[whole(own.shape), whole(received.shape)] + [HBM_SPEC] * len(after),
            out_specs=whole(own.shape)),
        out_shape=jax.ShapeDtypeStruct(own.shape, F32))(me, *_in_hbm(own, received), *after)


SEM_SPEC = pl.BlockSpec(memory_space=pltpu.SEMAPHORE)
DATAFLOW_EFFECT = pltpu.SideEffectType.DATAFLOW_SIDE_EFFECTING


def _exchange_start(carry, name, after=()):
    n, n_sems = len(carry.inputs), len(carry.sems)
    lands = [lax.empty(s.shape, s.dtype) for s in carry.out_shapes]

    def body(*refs):
        first_out = 2 * n + len(after)
        srcs, zones, sems, token = refs[:n], refs[n:2 * n], refs[first_out:first_out + n_sems], refs[-1]
        carry.start(srcs, zones, sems)
        token[...] = jnp.zeros_like(token)

    outs = pl.pallas_call(
        body, name=name, in_specs=[HBM_SPEC] * (2 * n + len(after)),
        out_specs=[SEM_SPEC] * n_sems + [HBM_SPEC] * (2 * n) + [pl.BlockSpec(memory_space=pltpu.VMEM)],
        out_shape=list(carry.sems) + _hbm_out([jax.ShapeDtypeStruct(a.shape, a.dtype) for a in carry.inputs])
        + _hbm_out(carry.out_shapes) + [jax.ShapeDtypeStruct((SUBLANES, LANES), F32)],
        input_output_aliases={j: n_sems + j for j in range(2 * n)},
        compiler_params=pltpu.CompilerParams(has_side_effects=DATAFLOW_EFFECT),
    )(*_in_hbm(*carry.inputs, *lands), *after)
    return outs[:-1], outs[-1]


def _exchange_wait(carry, in_flight, after, name):
    n = len(carry.inputs)
    sems, srcs, zones = in_flight[:2], in_flight[2:2 + n], in_flight[2 + n:]

    def body(*refs):
        src_refs, zone_refs, sem_refs = refs[:n], refs[n:2 * n], refs[2 * n:2 * n + 2]
        carry.finish(src_refs, zone_refs, sem_refs)

    outs = pl.pallas_call(
        body, name=name, in_specs=[HBM_SPEC] * (2 * n) + [SEM_SPEC, SEM_SPEC] + [HBM_SPEC] * len(after),
        out_specs=[HBM_SPEC] * (2 * n),
        out_shape=_hbm_out([jax.ShapeDtypeStruct(a.shape, a.dtype) for a in carry.inputs]) + _hbm_out(carry.out_shapes),
        input_output_aliases={j: j for j in range(2 * n)},
        compiler_params=pltpu.CompilerParams(has_side_effects=DATAFLOW_EFFECT),
    )(*srcs, *zones, *sems, *after)
    return list(outs[:n]), list(outs[n:])


def _add_sibling(grads8, recvs, place, row_tiles, name):
    k = len(grads8)
    g4 = [g.reshape(4, 2, *g.shape[1:]) for g in grads8]

    def body(place_ref, *refs):
        g_refs, r_refs, o_refs, ob_refs = (refs[j * k:(j + 1) * k] for j in range(4))
        own = pl.program_id(1) == place_ref[1]
        for g_ref, r_ref, o_ref, ob_ref in zip(g_refs, r_refs, o_refs, ob_refs):
            s = g_ref[0] + r_ref[...]
            ob_ref[...] = s.astype(BF16)

            @pl.when(own)
            def _(o_ref=o_ref, s=s):
                o_ref[...] = s[0]

    def blocks(make):
        return [make(g.shape[1] // row_tiles, g.shape[2]) for g in grads8]

    slot = lambda tr, C: pl.BlockSpec((1, tr, C), lambda r, ch, place_ref: (ch, r, 0))
    outs = pl.pallas_call(
        body, name=name,
        grid_spec=pltpu.PrefetchScalarGridSpec(
            num_scalar_prefetch=1, grid=(row_tiles, 4),
            in_specs=blocks(lambda tr, C: pl.BlockSpec((1, 1, tr, C), lambda r, ch, place_ref: (ch, place_ref[0], r, 0)))
            + blocks(slot),
            out_specs=blocks(lambda tr, C: pl.BlockSpec((tr, C), lambda r, ch, place_ref: (r, 0))) + blocks(slot)),
        out_shape=_hbm_out([jax.ShapeDtypeStruct(g.shape[1:], F32) for g in grads8]
                           + [jax.ShapeDtypeStruct((4,) + g.shape[1:], BF16) for g in grads8]),
        compiler_params=_params(("arbitrary", "arbitrary")),
    )(place, *_in_hbm(*g4, *recvs))
    return list(outs[:k]), list(outs[k:])


def _adam_math(w, g, m, v):
    m = ADAM_B1 * m + (1.0 - ADAM_B1) * g
    v = ADAM_B2 * v + (1.0 - ADAM_B2) * jnp.square(g)
    m_hat = m / (1.0 - ADAM_B1 ** ADAM_STEP)
    v_hat = v / (1.0 - ADAM_B2 ** ADAM_STEP)
    delta = -ADAM_LR * (m_hat / (jnp.sqrt(v_hat) + ADAM_EPS) + ADAM_WD * w)
    return delta, m, v


def _adam_big(ws, ms, vs, chip_sums, recvs, row_tiles, name, after=()):
    k = len(ws)

    def body(*refs):
        refs = refs[:5 * k] + refs[5 * k + len(after):]
        w_refs, m_refs, v_refs, s_refs, r_refs, g_refs, d_refs, nm_refs, nv_refs = (refs[j * k:(j + 1) * k] for j in range(9))
        for a in range(k):
            r_ref = r_refs[a]
            g = s_refs[a][...] + r_ref[0].astype(F32) + r_ref[1].astype(F32) + r_ref[2].astype(F32)
            g_refs[a][...] = g
            d_refs[a][...], nm_refs[a][...], nv_refs[a][...] = _adam_math(w_refs[a][...], g, m_refs[a][...], v_refs[a][...])

    def blocks(make):
        return [make(w.shape[0] // row_tiles, w.shape[1]) for w in ws]

    blk = lambda tr, C: pl.BlockSpec((tr, C), lambda r: (r, 0))
    outs = pl.pallas_call(
        body, name=name, grid=(row_tiles,),
        in_specs=blocks(blk) * 4 + blocks(lambda tr, C: pl.BlockSpec((3, tr, C), lambda r: (0, r, 0))) + [HBM_SPEC] * len(after),
        out_specs=blocks(blk) * 4,
        out_shape=[jax.ShapeDtypeStruct(w.shape, F32) for w in ws] * 4,
        compiler_params=_params(("arbitrary",)),
    )(*_in_hbm(*ws, *ms, *vs, *chip_sums, *recvs), *after)
    return [list(outs[j * k:(j + 1) * k]) for j in range(4)]


def _sum_partials(partials, name, after=()):
    def body(p_ref, *refs):
        g = p_ref[0]
        for d in range(1, partials.shape[0]):
            g = g + p_ref[d]
        refs[-1][...] = g

    return pl.pallas_call(body, name=name, grid=(1,), in_specs=[_whole(partials.shape)] + [HBM_SPEC] * len(after),
                          out_specs=_whole(partials.shape[1:]),
                          out_shape=jax.ShapeDtypeStruct(partials.shape[1:], F32))(*_in_hbm(partials), *after)


def _adam_small(ws, ms, vs, gs):
    n = len(ws)

    def body(*refs):
        w_refs, m_refs, v_refs, g_refs = (refs[i * n:(i + 1) * n] for i in range(4))
        d_refs, nm_refs, nv_refs = (refs[(4 + i) * n:(5 + i) * n] for i in range(3))
        for j in range(n):
            d_refs[j][...], nm_refs[j][...], nv_refs[j][...] = _adam_math(
                w_refs[j][...], g_refs[j][...], m_refs[j][...], v_refs[j][...])

    specs = [_whole(w.shape) for w in ws]
    outs = pl.pallas_call(body, name="adam_small", grid=(1,), in_specs=specs * 4, out_specs=specs * 3,
                          out_shape=[jax.ShapeDtypeStruct(w.shape, F32) for w in ws] * 3,
                          compiler_params=_params(("arbitrary",), VMEM_MID))(*_in_hbm(*ws, *ms, *vs, *gs))
    return outs[:n], outs[n:2 * n], outs[2 * n:]


PACK_QUANTUM = SUBLANES * LANES


def _pack(named, names):
    parts = []
    for nme in names:
        flat = named[nme].reshape(-1)
        parts.append(jnp.pad(flat, (0, -flat.size % PACK_QUANTUM)))
    return jnp.concatenate(parts).reshape(-1, LANES)


def _unpack(packed, shapes, names):
    flat = packed.reshape(-1)
    out, pos = {}, 0
    for nme in names:
        size = math.prod(shapes[nme])
        out[nme] = flat[pos:pos + size].reshape(shapes[nme])
        pos += size + (-size % PACK_QUANTUM)
    return out


BIG = ("w_in", "w_glu", "w_attn_branch", "w_ssm_branch", "w_out", "w_ff_in", "w_ff_out")
COLUMN_SHARDED = ("w_in", "w_attn_branch", "w_ssm_branch", "w_ff_in")
SMALL = ("norm_mix_pre", "norm_mix_post", "norm_mlp_pre", "norm_mlp_post", "rel_bias", "sinks", "lam_re", "lam_im",
         "log_dt", "b_re", "b_im", "c_re", "c_im", "d_skip")
SWAPPED_SMALL = ("rel_bias", "b_re", "b_im")
SMALL_LATE = ("norm_mix_pre", "rel_bias", "sinks", "loss")
SMALL_BEFORE_ATTN_BWD = tuple(n for n in SMALL if n not in SMALL_LATE)
ALL_WEIGHTS = ("norm_mix_pre", "norm_mix_post", "norm_mlp_pre", "norm_mlp_post", "w_in", "rel_bias", "sinks", "lam_re",
               "lam_im", "log_dt", "b_re", "b_im", "c_re", "c_im", "d_skip", "w_glu", "w_attn_branch", "w_ssm_branch",
               "w_out", "w_ff_in", "w_ff_out")


def _full_from_gathered(name, gathered):
    _, r, c = gathered.shape
    if name in COLUMN_SHARDED:
        return jnp.transpose(gathered, (1, 0, 2)).reshape(r, N_DEV * c)
    return gathered.reshape(N_DEV * r, c)


def _blocks_from_full(name, full):
    r, c = full.shape
    if name in COLUMN_SHARDED:
        return jnp.transpose(full.reshape(r, N_DEV, c // N_DEV), (1, 0, 2))
    return full.reshape(N_DEV, r // N_DEV, c)


def kernel(x, norm_mix_pre, norm_mix_post, norm_mlp_pre, norm_mlp_post, w_in, rel_bias, sinks, lam_re, lam_im, log_dt, b_re, b_im, c_re, c_im, d_skip, w_glu, w_attn_branch, w_ssm_branch, w_out, w_ff_in, w_ff_out, loss_target, m_norm_mix_pre, m_norm_mix_post, m_norm_mlp_pre, m_norm_mlp_post, m_w_in, m_rel_bias, m_sinks, m_lam_re, m_lam_im, m_log_dt, m_b_re, m_b_im, m_c_re, m_c_im, m_d_skip, m_w_glu, m_w_attn_branch, m_w_ssm_branch, m_w_out, m_w_ff_in, m_w_ff_out, v_norm_mix_pre, v_norm_mix_post, v_norm_mlp_pre, v_norm_mlp_post, v_w_in, v_rel_bias, v_sinks, v_lam_re, v_lam_im, v_log_dt, v_b_re, v_b_im, v_c_re, v_c_im, v_d_skip, v_w_glu, v_w_attn_branch, v_w_ssm_branch, v_w_out, v_w_ff_in, v_w_ff_out):
    args = dict(locals())
    w = {n: args[n] for n in ALL_WEIGHTS}
    m = {n: args["m_" + n] for n in ALL_WEIGHTS}
    v = {n: args["v_" + n] for n in ALL_WEIGHTS}
    core = lax.axis_index("c").astype(jnp.int32).reshape(1)
    chip = (2 * lax.axis_index("x") + lax.axis_index("y")).astype(jnp.int32).reshape(1)
    xs, target = x[0], loss_target[0]
    t = _tiles(xs.shape[0])
    local = lambda d, n: d[n][0].T if n == "w_in" else d[n][0]
    shard = {n: local(w, n).astype(BF16) for n in BIG}
    shard["w_ff_in"] = shard["w_ff_in"].T
    view = lambda n, a: jnp.swapaxes(a, -1, -2) if n in SWAPPED_SMALL else a
    small = {n: (view(n, w[n]) if n == "rel_bias" else view(n, w[n])[0]) for n in SMALL}
    g1, g2, g3, g4 = (small[n].reshape(1, D_MODEL) for n in ("norm_mix_pre", "norm_mix_post", "norm_mlp_pre", "norm_mlp_post"))
    bucket = jnp.asarray(_bucket_table())
    rel_b, sink = small["rel_bias"], small["sinks"].reshape(1, N_HEADS)
    lam_r, lam_i = small["lam_re"].reshape(1, STATES), small["lam_im"].reshape(1, STATES)
    ldt_rep = jnp.repeat(small["log_dt"].reshape(N_GROUPS), N_STATE).reshape(1, STATES)
    bd_re, bd_im = _block_diag_in(small["b_re"]), _block_diag_in(small["b_im"])
    cm_re, cm_im = _block_diag_out(small["c_re"]).astype(BF16), _block_diag_out(small["c_im"]).astype(BF16)
    dsk = small["d_skip"].reshape(1, SSM_W)

    (g_in,) = _run_carry(_gather_carry([shard["w_in"]]), "gather_w_in")
    wf_in = g_in.reshape(IN_W, D_MODEL)
    merge_names = ("w_glu", "w_attn_branch", "w_ssm_branch", "w_out")
    (q, k, vv, u, ga, gs, h), gathered = _in_proj_fwd(xs, g1, wf_in, t["proj"], _gather_carry([shard[n] for n in merge_names]))
    wf = {n: _full_from_gathered(n, g) for n, g in zip(merge_names, gathered)}
    (att,), (wf_ff_in,) = _attn_fwd(q, k, vv, bucket, rel_b, sink, _gather_carry([shard["w_ff_in"]]))
    a_re, a_im, bm_re, bm_im = _ssm_prep(lam_r, lam_i, ldt_rep, bd_re, bd_im)
    (y, h_re, h_im, in_re, in_im), (wf_ff_out,) = _ssm_fwd(
        u, a_re, a_im, bm_re, bm_im, cm_re, cm_im, dsk, t["ssm_chunk"], _gather_carry([shard["w_ff_out"]]))
    x1, o, h2 = _merge_fwd(xs, y, att, ga, gs, g2, g3, wf["w_glu"], wf["w_ssm_branch"], wf["w_attn_branch"], wf["w_out"],
                           t["merge"])
    a, dfo, dx2, loss_blk, dg4 = _mlp_fwd(h2, x1, target, g4, wf_ff_in, wf_ff_out, t["mlp_fwd"])

    groups = {"ff": 4, "merge": 1, "w_in": 2}

    def add_sibling(group, blocks, received):
        return _add_sibling(blocks, received, jnp.concatenate([core, chip]), groups[group], "add_sibling_" + group)

    ff_names = ("w_ff_in", "w_ff_out")
    dw_ff_in, dw_ff_out, da = _mlp_weight_grads(dfo, a, h2, wf_ff_out, t["mlp_bwd"])
    dh2 = _mlp_input_grad(da, wf_ff_in.reshape(D_FF, D_MODEL), t["mlp_bwd"])
    ff_blocks = [dw_ff_in, dw_ff_out]
    (dx1, dgates, datt, dy, dw_glu, dw_ssm, dw_attn, dw_out, dg2, dg3), ff_recv = _merge_bwd(
        dh2, dx2, x1, o, y, att, ga, gs, g2, g3, wf["w_glu"], wf["w_ssm_branch"], wf["w_attn_branch"], wf["w_out"],
        t["merge_bwd"], _sibling_carry(ff_blocks))
    ff_sums, ff_sums_bf = add_sibling("ff", ff_blocks, ff_recv)
    merge_blocks = [_blocks_from_full(n, g) for n, g in zip(merge_names, (dw_glu, dw_attn, dw_ssm, dw_out))]
    (du, dbm_re, dbm_im, dcm_re, dcm_im, da_re, da_im, dd_skip), carried = _ssm_bwd(
        dy, u, h_re, h_im, in_re, in_im, a_re, a_im, bm_re, bm_im, cm_re, cm_im, dsk, t["ssm_chunk"],
        _join(_chips_carry(ff_sums_bf), _sibling_carry(merge_blocks)))
    ff_from_chips, merge_recv = carried[:2], carried[2:]
    merge_sums, merge_sums_bf = add_sibling("merge", merge_blocks, merge_recv)
    dbd_re, dbd_im, dlam_re, dlam_im, dldt_rep = _ssm_prep_bwd(lam_r, lam_i, ldt_rep, bd_re, bd_im, dbm_re, dbm_im, da_re, da_im)
    dlog_dt = _group_sum(dldt_rep.reshape(N_GROUPS, N_STATE))
    shapes = {n: view(n, w[n]).shape for n in SMALL}
    shapes["loss"] = (1,)
    small_grads = dict(
        norm_mix_post=dg2, norm_mlp_pre=dg3, norm_mlp_post=dg4, lam_re=dlam_re, lam_im=dlam_im, log_dt=dlog_dt,
        b_re=_block_diag_in_grad(dbd_re), b_im=_block_diag_in_grad(dbd_im),
        c_re=_block_diag_out_grad(dcm_re), c_im=_block_diag_out_grad(dcm_im), d_skip=dd_skip)
    packed_early = _pack({n: small_grads[n].reshape(shapes[n]) for n in SMALL_BEFORE_ATTN_BWD}, SMALL_BEFORE_ATTN_BWD)
    (dq, dkv, attn_small), carried = _attn_bwd(
        q, k, vv, datt, bucket, rel_b, sink, _join(_chips_carry(merge_sums_bf), _gather_carry([packed_early])))
    merge_from_chips, partials_early = carried[:-1], carried[-1]

    dparts = (dq, dkv, du, dgates)
    dw_in_t = _in_proj_weight_grad(h, dparts)
    in_blocks = [dw_in_t.reshape(N_DEV, IN_W // N_DEV, D_MODEL)]
    to_sibling = _sibling_carry(in_blocks)
    in_flight, token = _exchange_start(to_sibling, "w_in_sibling_start")
    n_tiles = xs.shape[0] // t["proj_bwd"]
    (grad_x, dg1), _ = _in_proj_input_grad(xs, g1 + token[0:1, 0:1], wf_in, dx1, dparts, t["proj_bwd"], 0, n_tiles, "in_proj_input_grad")
    late = dict(norm_mix_pre=dg1, rel_bias=attn_small[:, :N_BUCKETS, 0], sinks=attn_small[:, N_BUCKETS, 0], loss=loss_blk[0:1, 0])
    packed_late = _pack({n: late[n].reshape(shapes[n]) for n in SMALL_LATE}, SMALL_LATE)
    to_everyone = _everyone_carry([packed_late])
    in_blocks, in_recv = _exchange_wait(to_sibling, in_flight, [packed_late], "w_in_sibling_wait")
    in_sums, in_sums_bf = add_sibling("w_in", in_blocks, in_recv)
    to_chips = _chips_carry(in_sums_bf)
    started, chips_started = _exchange_start(_join(to_everyone, to_chips), "late_grads_and_w_in_chips_start")
    late_in_flight, in_flight = [[started[j] for j in js] for js in ((0, 1, 4, 6), (2, 3, 5, 7))]

    grads, deltas, new_m, new_v = {}, {}, {}, {}

    def adam_group(group, names, sums, received, after=()):
        outs = _adam_big(*[[local(d, n) for n in names] for d in (w, m, v)], sums, received, groups[group],
                         "adam_" + group, after)
        for store, vals in zip((grads, deltas, new_m, new_v), outs):
            store.update({n: (o.T if n == "w_in" else o)[None] for n, o in zip(names, vals)})

    adam_group("ff", ff_names, ff_sums, ff_from_chips, [chips_started])
    adam_group("merge", merge_names, merge_sums, merge_from_chips, [chips_started])

    grads.update(_unpack(_sum_partials(partials_early, "sum_small_grads", [chips_started]), shapes, SMALL_BEFORE_ATTN_BWD))
    (packed_late,), (late_received,) = _exchange_wait(
        to_everyone, late_in_flight, [new_v["w_ff_out"], new_v["w_out"]], "late_grads_wait")
    grads.update(_unpack(_sum_everyone(packed_late, late_received, 2 * chip + core, "sum_late_grads"), shapes, SMALL_LATE))
    loss = grads.pop("loss").reshape(())
    small_out = _adam_small(*[[view(n, d[n]) for n in SMALL] for d in (w, m, v)], [grads[n] for n in SMALL])
    for store, vals in zip((deltas, new_m, new_v), small_out):
        store.update(zip(SMALL, vals))
    for store in (grads, deltas, new_m, new_v):
        store.update({n: view(n, store[n]) for n in SWAPPED_SMALL})

    busy = [new_v["w_ff_out"], new_v["w_out"], deltas["norm_mix_pre"]]
    _, (in_from_chips,) = _exchange_wait(to_chips, in_flight, busy, "w_in_chips_wait")
    adam_group("w_in", ("w_in",), in_sums, [in_from_chips])

    return (loss, grad_x[None], *[grads[n] for n in ALL_WEIGHTS], *[deltas[n] for n in ALL_WEIGHTS],
            *[new_m[n] for n in ALL_WEIGHTS], *[new_v[n] for n in ALL_WEIGHTS])
```

```python
import math

import jax
import jax.numpy as jnp
import numpy as np
from jax import lax
from jax.experimental import pallas as pl
from jax.experimental.pallas import tpu as pltpu

F32 = jnp.float32
BF16 = jnp.bfloat16

D_MODEL = 1024
N_HEADS = 8
HEAD_DIM = 64
ATTN_W = 512
KV_W = 128
BLOCK = 128
N_BUCKETS = 32
SSM_W = 512
N_GROUPS = 32
N_STATE = 64
GROUP_CH = 16
STATES = N_GROUPS * N_STATE
D_FF = 4096
IN_W = 3328
SPLITS = (0, 512, 640, 768, 1280, 2304, 3328)
RMS_EPS = 1e-6
NEG_INF = -1e30
SUBLANES = 8
LANES = 128
SSM_LANE_BLOCK = 512
N_SSM_BLOCKS = STATES // SSM_LANE_BLOCK
GROUPS_PER_BLOCK = SSM_LANE_BLOCK // N_STATE
VMEM_BIG = 52 * 1024 * 1024
VMEM_MID = 40 * 1024 * 1024
VMEM_MAX = 60 * 1024 * 1024

ADAM_LR = 0.001
ADAM_B1 = 0.9
ADAM_B2 = 0.999
ADAM_EPS = 1e-08
ADAM_WD = 0.01
ADAM_STEP = 10

N_DEV = 8


def _dot(a, b):
    return jnp.dot(a, b, preferred_element_type=F32)


def _dot_nt(a, b):
    return lax.dot_general(a, b, (((1,), (1,)), ((), ())), preferred_element_type=F32)


def _dot_tn(a, b):
    return lax.dot_general(a, b, (((0,), (0,)), ((), ())), preferred_element_type=F32)


def _rms_scale(x):
    return lax.rsqrt(jnp.mean(x * x, axis=-1, keepdims=True) + RMS_EPS)


def _rms_bwd(dy, x, r, g):
    t = dy * g
    dx = r * t - x * (r * r * r) * jnp.mean(t * x, axis=-1, keepdims=True)
    dg = jnp.sum(dy * x * r, axis=0, keepdims=True)
    return dx, dg


def _const_spec(shape):
    nd = len(shape)
    return pl.BlockSpec(shape, lambda *_: (0,) * nd, pipeline_mode=pl.Buffered(1))


def _in_hbm(*arrays):
    return tuple(pltpu.with_memory_space_constraint(a, pltpu.HBM) for a in arrays)


def _hbm_out(shapes):
    if isinstance(shapes, (list, tuple)):
        return [_hbm_out(s) for s in shapes]
    return shapes if isinstance(shapes, pl.MemoryRef) else pltpu.HBM(shapes.shape, shapes.dtype)


def _whole(shape):
    nd = len(shape)
    return pl.BlockSpec(shape, lambda *_: (0,) * nd)


def _params(sem, vmem=None):
    return pltpu.CompilerParams(dimension_semantics=sem, vmem_limit_bytes=vmem)


MESH_IDS = pl.DeviceIdType.MESH
HBM_SPEC = pl.BlockSpec(memory_space=pl.ANY)


class _Carry:
    def __init__(self, inputs, out_shapes, sems, start, finish, middle=None):
        self.inputs, self.out_shapes, self.sems = list(inputs), list(out_shapes), list(sems)
        self.start, self.middle, self.finish = start, middle, finish


def _join(a, b):
    na_in, na_out, na_sem = len(a.inputs), len(a.out_shapes), len(a.sems)

    def both(phase):
        def run(ins, outs, sems):
            for carry, lo in ((a, True), (b, False)):
                part = (lambda seq, n: seq[:n] if lo else seq[n:])
                if getattr(carry, phase) is not None:
                    getattr(carry, phase)(part(ins, na_in), part(outs, na_out), part(sems, na_sem))
        return run

    middle = both("middle") if (a.middle or b.middle) else None
    return _Carry(a.inputs + b.inputs, a.out_shapes + b.out_shapes, a.sems + b.sems, both("start"), both("finish"), middle)


def _hosted_call(body, carry, edge, *, name, grid, in_specs, out_specs, out_shape, scratch_shapes, compiler_params, inputs):
    n_in, n_out = len(in_specs), len(out_specs)
    inputs = [a if s.memory_space == pltpu.SMEM else _in_hbm(a)[0] for a, s in zip(inputs, in_specs)]
    out_shape = _hbm_out(list(out_shape))
    if carry is None:
        outs = pl.pallas_call(body, name=name, grid=grid, in_specs=in_specs, out_specs=out_specs, out_shape=out_shape,
                              scratch_shapes=scratch_shapes, compiler_params=compiler_params)(*inputs)
        return list(outs), []
    c_in, c_out, c_sem = len(carry.inputs), len(carry.out_shapes), len(carry.sems)

    def wrapped(*refs):
        ins, refs = refs[:n_in], refs[n_in:]
        cins, refs = refs[:c_in], refs[c_in:]
        outs, refs = refs[:n_out], refs[n_out:]
        couts, refs = refs[:c_out], refs[c_out:]
        scratch, csems = refs[:len(refs) - c_sem], refs[len(refs) - c_sem:]
        first, middle, last = edge()

        @pl.when(first)
        def _():
            carry.start(cins, couts, csems)

        body(*ins, *outs, *scratch)

        if carry.middle is not None:
            @pl.when(middle)
            def _():
                carry.middle(cins, couts, csems)

        @pl.when(last)
        def _():
            carry.finish(cins, couts, csems)

    outs = pl.pallas_call(
        wrapped, name=name, grid=grid, in_specs=list(in_specs) + [HBM_SPEC] * c_in,
        out_specs=list(out_specs) + [HBM_SPEC] * c_out, out_shape=out_shape + _hbm_out(carry.out_shapes),
        scratch_shapes=list(scratch_shapes) + carry.sems, compiler_params=compiler_params)(*inputs, *_in_hbm(*carry.inputs))
    return list(outs[:n_out]), list(outs[n_out:])


def _pass_on_step(n_steps):
    return max(0, min((7 * n_steps) // 8, n_steps - 2))


def _edge_1d(n_steps, pass_on_last=False):
    middle = n_steps - 1 if pass_on_last else _pass_on_step(n_steps)
    return lambda: (pl.program_id(0) == 0, pl.program_id(0) == middle, pl.program_id(0) == n_steps - 1)


def _edge_2d(n0, n1):
    def edge():
        step = pl.program_id(0) * n1 + pl.program_id(1)
        return step == 0, step == _pass_on_step(n0 * n1), step == n0 * n1 - 1
    return edge


def _in_proj_fwd(x, g1, w_in_t, tile, carry=None):
    T = x.shape[0]

    def body(x_ref, g_ref, w_ref, q_ref, k_ref, v_ref, u_ref, ga_ref, gs_ref, h_ref):
        xv = x_ref[...]
        h = (xv * _rms_scale(xv) * g_ref[...]).astype(BF16)
        h_ref[...] = h
        outs = (q_ref, k_ref, v_ref, u_ref, ga_ref, gs_ref)
        for p, o_ref in enumerate(outs):
            o_ref[...] = _dot_nt(h, w_ref[SPLITS[p]:SPLITS[p + 1], :]).astype(o_ref.dtype)

    widths = [SPLITS[p + 1] - SPLITS[p] for p in range(6)] + [D_MODEL]
    dtypes = [BF16, BF16, BF16, F32, F32, F32, BF16]
    return _hosted_call(
        body, carry, _edge_1d(T // tile), name="in_proj_fwd", grid=(T // tile,),
        in_specs=[pl.BlockSpec((tile, D_MODEL), lambda i: (i, 0)), _const_spec((1, D_MODEL)), _const_spec((IN_W, D_MODEL))],
        out_specs=[pl.BlockSpec((tile, w), lambda i: (i, 0)) for w in widths],
        out_shape=[jax.ShapeDtypeStruct((T, w), dt) for w, dt in zip(widths, dtypes)],
        scratch_shapes=[], compiler_params=_params(("arbitrary",), VMEM_MID), inputs=(x, g1, w_in_t))


PROJ_PARTS = (512, 256, 512, 2048)
PROJ_GRAD_BLOCK = 256


def _in_proj_weight_grad(h, dparts):
    T = h.shape[0]
    blocks = [wd // PROJ_GRAD_BLOCK for wd in PROJ_PARTS]
    starts = [sum(blocks[:p]) for p in range(len(blocks))]

    def body(h_ref, *refs):
        part_refs, o_ref = refs[:-1], refs[-1]
        j = pl.program_id(0)
        for p_ref, start, count in zip(part_refs, starts, blocks):
            @pl.when((j >= start) & (j < start + count))
            def _(p_ref=p_ref):
                o_ref[...] = _dot_tn(p_ref[...], h_ref[...])

    def part_spec(start, count):
        return pl.BlockSpec((T, PROJ_GRAD_BLOCK), lambda j: (0, jnp.clip(j - start, 0, count - 1)))

    return pl.pallas_call(
        body, name="in_proj_weight_grad", grid=(sum(blocks),),
        in_specs=[_const_spec((T, D_MODEL))] + [part_spec(s, c) for s, c in zip(starts, blocks)],
        out_specs=pl.BlockSpec((PROJ_GRAD_BLOCK, D_MODEL), lambda j: (j, 0)),
        out_shape=_hbm_out(jax.ShapeDtypeStruct((IN_W, D_MODEL), F32)),
        compiler_params=_params(("arbitrary",), VMEM_MID),
    )(*_in_hbm(h, *dparts))


def _in_proj_input_grad(x, g1, w_in_t, dx1, dparts, tile, first_tile, n_tiles, name, carry=None):
    offsets = [sum(PROJ_PARTS[:p]) for p in range(len(PROJ_PARTS))]

    def body(x_ref, g_ref, w_ref, dx1_ref, *refs):
        part_refs, (gx_ref, dg_ref) = refs[:len(PROJ_PARTS)], refs[len(PROJ_PARTS):]
        i = pl.program_id(0)
        xv = x_ref[...]
        r = _rms_scale(xv)
        g = g_ref[...]
        dh = sum(_dot(p_ref[...], w_ref[off:off + wd, :]) for p_ref, off, wd in zip(part_refs, offsets, PROJ_PARTS))
        dxn, dg = _rms_bwd(dh, xv, r, g)
        gx_ref[...] = dx1_ref[...] + dxn

        @pl.when(i == 0)
        def _():
            dg_ref[...] = dg

        @pl.when(i > 0)
        def _():
            dg_ref[...] += dg

    tok = lambda wd: pl.BlockSpec((tile, wd), lambda i: (i + first_tile, 0))
    return _hosted_call(
        body, carry, _edge_1d(n_tiles), name=name, grid=(n_tiles,),
        in_specs=[tok(D_MODEL), _const_spec((1, D_MODEL)), _const_spec((IN_W, D_MODEL)), tok(D_MODEL)] + [tok(wd) for wd in PROJ_PARTS],
        out_specs=[pl.BlockSpec((tile, D_MODEL), lambda i: (i, 0)), pl.BlockSpec((1, D_MODEL), lambda i: (0, 0))],
        out_shape=[jax.ShapeDtypeStruct((n_tiles * tile, D_MODEL), F32), jax.ShapeDtypeStruct((1, D_MODEL), F32)],
        scratch_shapes=[], compiler_params=_params(("arbitrary",), VMEM_MID), inputs=(x, g1, w_in_t, dx1, *dparts))


def _bucket_table():
    qi = np.arange(BLOCK)[:, None]
    kj = np.arange(2 * BLOCK)[None, :]
    dist = qi + BLOCK - kj
    max_exact = N_BUCKETS // 2
    d = np.maximum(dist, 0)
    df = np.maximum(d, 1).astype(np.float32)
    large = max_exact + (np.log(df / np.float32(max_exact)) / np.float32(math.log(BLOCK / max_exact))
                         * np.float32(N_BUCKETS - max_exact)).astype(np.int32)
    large = np.minimum(large, N_BUCKETS - 1)
    bucket = np.where(d < max_exact, d, large)
    return np.where((dist >= 0) & (dist < BLOCK), bucket, -1).astype(np.int32)


def _build_bias(bucket_ref, rb_ref, bias_ref):
    bk = bucket_ref[...]
    for h in range(N_HEADS):
        def add(b, acc, h=h):
            return acc + jnp.where(bk == b, rb_ref[h, b], 0.0)
        bias_ref[h] = lax.fori_loop(0, N_BUCKETS, add, jnp.zeros((BLOCK, 2 * BLOCK), F32))


def _kv_variants(prev_ref, cur_ref):
    cat = jnp.concatenate([prev_ref[...], cur_ref[...]], axis=0)
    lo = lax.broadcasted_iota(jnp.int32, cat.shape, 1) < HEAD_DIM
    zero = jnp.zeros_like(cat)
    head0_lo = jnp.where(lo, cat, zero)
    head1_hi = jnp.where(lo, zero, cat)
    return ((head0_lo, pltpu.roll(head0_lo, HEAD_DIM, 1)), (pltpu.roll(head1_hi, HEAD_DIM, 1), head1_hi))


def _merge_kv_grads(g):
    lo = lax.broadcasted_iota(jnp.int32, g[0][0].shape, 1) < HEAD_DIM
    return jnp.where(lo, g[0][0] + pltpu.roll(g[0][1], HEAD_DIM, 1), g[1][1] + pltpu.roll(g[1][0], HEAD_DIM, 1))


def _head_lanes(h):
    return slice((h // 2) * LANES, (h // 2 + 1) * LANES)


def _attn_probs(q_ref, kvar, bias_ref, sk_ref, valid, s_ref):
    for h in range(N_HEADS):
        s_ref[h] = _dot_nt(q_ref[:, _head_lanes(h)], kvar[h // 4][h % 2])
    head = lax.broadcasted_iota(jnp.int32, (N_HEADS, 1, 1), 0)
    sink = jnp.zeros((N_HEADS, 1, 1), F32)
    for h in range(N_HEADS):
        sink = jnp.where(head == h, sk_ref[0, h], sink)
    s = jnp.where(valid[None], s_ref[...] * (HEAD_DIM ** -0.5) + bias_ref[...], NEG_INF)
    m = jnp.maximum(jnp.max(s, axis=-1, keepdims=True), sink)
    p = jnp.exp(s - m)
    e_sink = jnp.exp(sink - m)
    inv = 1.0 / (jnp.sum(p, axis=-1, keepdims=True) + e_sink)
    return p * inv, e_sink * inv


def _attn_valid(bucket_ref, n):
    col = lax.broadcasted_iota(jnp.int32, (BLOCK, 2 * BLOCK), 1)
    return (bucket_ref[...] >= 0) & ((n > 0) | (col >= BLOCK))


def _attn_fwd(q, k, v, bucket, rel_bias, sinks, carry=None):
    T = q.shape[0]
    nb = T // BLOCK

    def body(q_ref, kc_ref, kp_ref, vc_ref, vp_ref, bucket_ref, rb_ref, sk_ref, o_ref, bias_ref, s_ref, p_ref):
        n = pl.program_id(0)

        @pl.when(n == 0)
        def _():
            _build_bias(bucket_ref, rb_ref, bias_ref)

        kvar = _kv_variants(kp_ref, kc_ref)
        vvar = _kv_variants(vp_ref, vc_ref)
        pr, _ = _attn_probs(q_ref, kvar, bias_ref, sk_ref, _attn_valid(bucket_ref, n), s_ref)
        p_ref[...] = pr.astype(BF16)
        for m in range(N_HEADS // 2):
            acc = _dot(p_ref[2 * m], vvar[m // 2][0]) + _dot(p_ref[2 * m + 1], vvar[m // 2][1])
            o_ref[:, m * LANES:(m + 1) * LANES] = acc.astype(o_ref.dtype)

    cur = lambda w: pl.BlockSpec((BLOCK, w), lambda n: (n, 0))
    prev = lambda w: pl.BlockSpec((BLOCK, w), lambda n: (jnp.maximum(n - 1, 0), 0))
    smem = pl.BlockSpec(memory_space=pltpu.SMEM)
    return _hosted_call(
        body, carry, _edge_1d(nb, pass_on_last=True), name="attn_fwd", grid=(nb,),
        in_specs=[cur(ATTN_W), cur(KV_W), prev(KV_W), cur(KV_W), prev(KV_W), _const_spec((BLOCK, 2 * BLOCK)), smem, smem],
        out_specs=[cur(ATTN_W)],
        out_shape=[jax.ShapeDtypeStruct((T, ATTN_W), BF16)],
        scratch_shapes=[pltpu.VMEM((N_HEADS, BLOCK, 2 * BLOCK), F32), pltpu.VMEM((N_HEADS, BLOCK, 2 * BLOCK), F32),
                        pltpu.VMEM((N_HEADS, BLOCK, 2 * BLOCK), BF16)],
        compiler_params=_params(("arbitrary",)), inputs=(q, k, k, v, v, bucket, rel_bias, sinks))


ATTN_SMALL_ROWS = N_BUCKETS + SUBLANES


def _attn_bwd(q, k, v, datt, bucket, rel_bias, sinks, carry=None):
    T = q.shape[0]
    nb = T // BLOCK

    def body(q_ref, do_ref, kc_ref, kp_ref, vc_ref, vp_ref, bucket_ref, rb_ref, sk_ref,
             dq_ref, dkv_ref, small_ref, bias_ref, ds_sum_ref, dsink_ref, kcarry_ref, vcarry_ref,
             s_ref, dp_ref, p_ref, dsc_ref):
        n = pl.program_id(0)

        @pl.when(n == 0)
        def _():
            _build_bias(bucket_ref, rb_ref, bias_ref)
            ds_sum_ref[...] = jnp.zeros_like(ds_sum_ref)
            dsink_ref[...] = jnp.zeros_like(dsink_ref)
            kcarry_ref[...] = jnp.zeros_like(kcarry_ref)
            vcarry_ref[...] = jnp.zeros_like(vcarry_ref)

        @pl.when(n < nb)
        def _():
            kvar = _kv_variants(kp_ref, kc_ref)
            vvar = _kv_variants(vp_ref, vc_ref)
            pr, p_sink = _attn_probs(q_ref, kvar, bias_ref, sk_ref, _attn_valid(bucket_ref, n), s_ref)
            for h in range(N_HEADS):
                dp_ref[h] = _dot_nt(do_ref[:, _head_lanes(h)], vvar[h // 4][h % 2])
            dp = dp_ref[...]
            dsum = jnp.sum(pr * dp, axis=-1, keepdims=True)
            ds = pr * (dp - dsum)
            ds_sum_ref[...] += ds
            dsink_ref[...] -= jnp.sum(p_sink * dsum, axis=1, keepdims=True)
            dsc_ref[...] = (ds * (HEAD_DIM ** -0.5)).astype(BF16)
            p_ref[...] = pr.astype(BF16)
            for m in range(N_HEADS // 2):
                dqm = _dot(dsc_ref[2 * m], kvar[m // 2][0]) + _dot(dsc_ref[2 * m + 1], kvar[m // 2][1])
                dq_ref[:, m * LANES:(m + 1) * LANES] = dqm.astype(dq_ref.dtype)
            dk_var = [[None, None], [None, None]]
            dv_var = [[None, None], [None, None]]
            for kvh in range(2):
                for e in range(2):
                    heads = [h for h in range(N_HEADS) if h // 4 == kvh and h % 2 == e]
                    dk_var[kvh][e] = sum(_dot_tn(dsc_ref[h], q_ref[:, _head_lanes(h)]) for h in heads)
                    dv_var[kvh][e] = sum(_dot_tn(p_ref[h], do_ref[:, _head_lanes(h)]) for h in heads)
            dk_cat = _merge_kv_grads(dk_var)
            dv_cat = _merge_kv_grads(dv_var)

            @pl.when(n > 0)
            def _():
                dkv_ref[:, :KV_W] = (kcarry_ref[...] + dk_cat[:BLOCK]).astype(BF16)
                dkv_ref[:, KV_W:] = (vcarry_ref[...] + dv_cat[:BLOCK]).astype(BF16)

            kcarry_ref[...] = dk_cat[BLOCK:]
            vcarry_ref[...] = dv_cat[BLOCK:]

        @pl.when(n == nb)
        def _():
            dkv_ref[:, :KV_W] = kcarry_ref[...].astype(BF16)
            dkv_ref[:, KV_W:] = vcarry_ref[...].astype(BF16)
            bk = bucket_ref[...]
            row = lax.broadcasted_iota(jnp.int32, (N_HEADS, ATTN_SMALL_ROWS, LANES), 1)

            def add(b, acc):
                masked = jnp.where((bk == b)[None], ds_sum_ref[...], 0.0)
                val = jnp.sum(jnp.sum(masked, axis=1, keepdims=True), axis=2, keepdims=True)
                return acc + jnp.where(row == b, val, 0.0)

            small_ref[...] = lax.fori_loop(0, N_BUCKETS, add, jnp.where(row == N_BUCKETS, dsink_ref[...], 0.0))

    last = nb - 1
    cur = lambda w: pl.BlockSpec((BLOCK, w), lambda n: (jnp.minimum(n, last), 0))
    prev = lambda w: pl.BlockSpec((BLOCK, w), lambda n: (jnp.clip(n - 1, 0, last), 0))
    smem = pl.BlockSpec(memory_space=pltpu.SMEM)
    return _hosted_call(
        body, carry, _edge_1d(nb + 1), name="attn_bwd", grid=(nb + 1,),
        in_specs=[cur(ATTN_W), cur(ATTN_W), cur(KV_W), prev(KV_W), cur(KV_W), prev(KV_W),
                  _const_spec((BLOCK, 2 * BLOCK)), smem, smem],
        out_specs=[cur(ATTN_W), prev(2 * KV_W), pl.BlockSpec((N_HEADS, ATTN_SMALL_ROWS, LANES), lambda n: (0, 0, 0))],
        out_shape=[jax.ShapeDtypeStruct((T, ATTN_W), BF16), jax.ShapeDtypeStruct((T, 2 * KV_W), BF16),
                   jax.ShapeDtypeStruct((N_HEADS, ATTN_SMALL_ROWS, LANES), F32)],
        scratch_shapes=[pltpu.VMEM((N_HEADS, BLOCK, 2 * BLOCK), F32), pltpu.VMEM((N_HEADS, BLOCK, 2 * BLOCK), F32),
                        pltpu.VMEM((N_HEADS, 1, 1), F32), pltpu.VMEM((BLOCK, KV_W), F32), pltpu.VMEM((BLOCK, KV_W), F32),
                        pltpu.VMEM((N_HEADS, BLOCK, 2 * BLOCK), F32), pltpu.VMEM((N_HEADS, BLOCK, 2 * BLOCK), F32),
                        pltpu.VMEM((N_HEADS, BLOCK, 2 * BLOCK), BF16), pltpu.VMEM((N_HEADS, BLOCK, 2 * BLOCK), BF16)],
        compiler_params=_params(("arbitrary",)), inputs=(q, datt, k, k, v, v, bucket, rel_bias, sinks))


SCAN_UNROLL = 4


def _cmul(ar, ai, br, bi):
    return ar * br - ai * bi, ar * bi + ai * br


def _cmul_conj(ar, ai, br, bi):
    return ar * br + ai * bi, ar * bi - ai * br


def _ssm_discretize(lr, li, ldt):
    dt = jnp.exp(ldt)
    mag = jnp.exp(lr * dt)
    ab_re = mag * jnp.cos(li * dt)
    ab_im = mag * jnp.sin(li * dt)
    nr = ab_re - 1.0
    den = lr * lr + li * li
    f_re = (nr * lr + ab_im * li) / den
    f_im = (ab_im * lr - nr * li) / den
    return ab_re, ab_im, f_re, f_im


def _ssm_prep(lam_re, lam_im, ldt_rep, bd_re, bd_im):
    def body(lr_ref, li_ref, ldt_ref, bdr_ref, bdi_ref, ar_ref, ai_ref, br_ref, bi_ref):
        ab_re, ab_im, f_re, f_im = _ssm_discretize(lr_ref[...], li_ref[...], ldt_ref[...])
        ar_ref[...] = ab_re
        ai_ref[...] = ab_im
        bdr, bdi = bdr_ref[0], bdi_ref[0]
        br_ref[0] = (bdr * f_re - bdi * f_im).astype(BF16)
        bi_ref[0] = (bdi * f_re + bdr * f_im).astype(BF16)

    row = pl.BlockSpec((1, SSM_LANE_BLOCK), lambda j: (0, j))
    mat = pl.BlockSpec((1, LANES, SSM_LANE_BLOCK), lambda j: (j, 0, 0))
    return pl.pallas_call(
        body, name="ssm_prep", grid=(N_SSM_BLOCKS,),
        in_specs=[row, row, row, mat, mat], out_specs=[row, row, mat, mat],
        out_shape=[jax.ShapeDtypeStruct((1, STATES), F32)] * 2 + [jax.ShapeDtypeStruct((N_SSM_BLOCKS, LANES, SSM_LANE_BLOCK), BF16)] * 2,
        compiler_params=_params(("arbitrary",)),
    )(*_in_hbm(lam_re, lam_im, ldt_rep, bd_re, bd_im))


def _ssm_prep_bwd(lam_re, lam_im, ldt_rep, bd_re, bd_im, dbr, dbi, da_re, da_im):
    def body(lr_ref, li_ref, ldt_ref, bdr_ref, bdi_ref, dbr_ref, dbi_ref, dar_ref, dai_ref,
             dbdr_ref, dbdi_ref, dlr_ref, dli_ref, dldt_ref):
        lr, li, ldt = lr_ref[...], li_ref[...], ldt_ref[...]
        (_, _, f_re, f_im), vjp = jax.vjp(_ssm_discretize, lr, li, ldt)
        bdr, bdi, gbr, gbi = bdr_ref[0], bdi_ref[0], dbr_ref[0], dbi_ref[0]
        dbdr_ref[0] = gbr * f_re + gbi * f_im
        dbdi_ref[0] = gbi * f_re - gbr * f_im
        df_re = jnp.sum(gbr * bdr + gbi * bdi, axis=0, keepdims=True)
        df_im = jnp.sum(gbi * bdr - gbr * bdi, axis=0, keepdims=True)
        dlr, dli, dldt = vjp((dar_ref[...], dai_ref[...], df_re, df_im))
        dlr_ref[...] = dlr
        dli_ref[...] = dli
        dldt_ref[...] = dldt

    row = pl.BlockSpec((1, SSM_LANE_BLOCK), lambda j: (0, j))
    mat = pl.BlockSpec((1, LANES, SSM_LANE_BLOCK), lambda j: (j, 0, 0))
    mat_shape = jax.ShapeDtypeStruct((N_SSM_BLOCKS, LANES, SSM_LANE_BLOCK), F32)
    row_shape = jax.ShapeDtypeStruct((1, STATES), F32)
    return pl.pallas_call(
        body, name="ssm_prep_bwd", grid=(N_SSM_BLOCKS,),
        in_specs=[row, row, row, mat, mat, mat, mat, row, row], out_specs=[mat, mat, row, row, row],
        out_shape=[mat_shape, mat_shape, row_shape, row_shape, row_shape],
        compiler_params=_params(("arbitrary",)),
    )(*_in_hbm(lam_re, lam_im, ldt_rep, bd_re, bd_im, dbr, dbi, da_re, da_im))


def _group_sum(x):
    def body(x_ref, o_ref):
        o_ref[...] = jnp.sum(x_ref[...], axis=1, keepdims=True)
    return pl.pallas_call(body, name="ssm_group_sum", grid=(1,), in_specs=[_whole(x.shape)], out_specs=_whole((N_GROUPS, 1)),
                          out_shape=jax.ShapeDtypeStruct((N_GROUPS, 1), F32))(*_in_hbm(x))


def _power_table(ar, ai, p_re_ref, p_im_ref, steps):
    shape = (SUBLANES, SSM_LANE_BLOCK)
    p_re_ref[0:SUBLANES] = jnp.broadcast_to(ar, shape)
    p_im_ref[0:SUBLANES] = jnp.broadcast_to(ai, shape)
    m = 1
    while m < steps:
        rows = m * SUBLANES
        top_re = p_re_ref[rows - SUBLANES:rows]
        top_im = p_im_ref[rows - SUBLANES:rows]
        cur_re = p_re_ref[0:rows].reshape(m, SUBLANES, SSM_LANE_BLOCK)
        cur_im = p_im_ref[0:rows].reshape(m, SUBLANES, SSM_LANE_BLOCK)
        nxt_re, nxt_im = _cmul(cur_re, cur_im, top_re[None], top_im[None])
        p_re_ref[rows:2 * rows] = nxt_re.reshape(rows, SSM_LANE_BLOCK)
        p_im_ref[rows:2 * rows] = nxt_im.reshape(rows, SSM_LANE_BLOCK)
        m *= 2


def _to_segments(src_ref, dst_ref, steps):
    for s in range(SUBLANES):
        dst_ref[pl.ds(s, steps, stride=SUBLANES), :] = src_ref[s * steps:(s + 1) * steps, :]


def _from_segments(src_ref, dst_ref, steps):
    for s in range(SUBLANES):
        dst_ref[s * steps:(s + 1) * steps, :] = src_ref[pl.ds(s, steps, stride=SUBLANES), :]


def _segment_carries(e_re, e_im, an_re, an_im, c_re, c_im, reverse):
    order = range(SUBLANES - 1, -1, -1) if reverse else range(SUBLANES)
    ins_re, ins_im = [None] * SUBLANES, [None] * SUBLANES
    for s in order:
        ins_re[s], ins_im[s] = c_re, c_im
        pr, pi = _cmul(an_re, an_im, c_re, c_im)
        c_re = e_re[s:s + 1] + pr
        c_im = e_im[s:s + 1] + pi
    return jnp.concatenate(ins_re, axis=0), jnp.concatenate(ins_im, axis=0), c_re, c_im


def _ssm_fwd(u, a_re, a_im, b_re, b_im, c_re, c_im, d_skip, chunk, carry=None):
    T = u.shape[0]
    nc = T // chunk
    steps = chunk // SUBLANES
    blk = SSM_LANE_BLOCK

    def body(u_ref, ar_ref, ai_ref, br_ref, bi_ref, cr_ref, ci_ref, dk_ref,
             y_ref, hr_ref, hi_ref, inr_ref, ini_ref, useg_ref, yseg_ref, pr_ref, pi_ref, carry_ref):
        c = pl.program_id(1)
        ar, ai = ar_ref[...], ai_ref[...]

        @pl.when(c == 0)
        def _():
            _power_table(ar, ai, pr_ref, pi_ref, steps)
            carry_ref[...] = jnp.zeros_like(carry_ref)

        _to_segments(u_ref, useg_ref, steps)
        ub = useg_ref[...].astype(BF16)
        hr_ref[...] = _dot(ub, br_ref[0])
        hi_ref[...] = _dot(ub, bi_ref[0])
        first = slice(0, SUBLANES)

        def scan(t4, prev):
            for j in range(SCAN_UNROLL):
                rows = pl.ds(pl.multiple_of((t4 * SCAN_UNROLL + j) * SUBLANES, SUBLANES), SUBLANES)
                pr, pi = _cmul(pr_ref[first, :], pi_ref[first, :], prev[0], prev[1])
                prev = (pr + hr_ref[rows, :], pi + hi_ref[rows, :])
                hr_ref[rows, :] = prev[0]
                hi_ref[rows, :] = prev[1]
            return prev

        zero = jnp.zeros((SUBLANES, blk), F32)
        lax.fori_loop(0, steps // SCAN_UNROLL, scan, (zero, zero))

        top = slice(chunk - SUBLANES, chunk)
        in_re, in_im, out_re, out_im = _segment_carries(
            hr_ref[top, :], hi_ref[top, :], pr_ref[top, :][0:1], pi_ref[top, :][0:1],
            carry_ref[0:1, :], carry_ref[1:2, :], reverse=False)
        carry_ref[0:1, :] = out_re
        carry_ref[1:2, :] = out_im
        inr_ref[...] = in_re
        ini_ref[...] = in_im

        def fix(t4, _):
            for j in range(SCAN_UNROLL):
                rows = pl.ds(pl.multiple_of((t4 * SCAN_UNROLL + j) * SUBLANES, SUBLANES), SUBLANES)
                fr, fi = _cmul(pr_ref[rows, :], pi_ref[rows, :], in_re, in_im)
                hr_ref[rows, :] += fr
                hi_ref[rows, :] += fi
            return 0

        lax.fori_loop(0, steps // SCAN_UNROLL, fix, 0)

        yseg_ref[...] = _dot(hr_ref[...].astype(BF16), cr_ref[0]) - _dot(hi_ref[...].astype(BF16), ci_ref[0])
        _from_segments(yseg_ref, y_ref, steps)
        y_ref[...] += dk_ref[...] * u_ref[...]

    row = pl.BlockSpec((1, blk), lambda j, c: (0, j))
    b_mat = pl.BlockSpec((1, LANES, blk), lambda j, c: (j, 0, 0))
    c_mat = pl.BlockSpec((1, blk, LANES), lambda j, c: (j, 0, 0))
    tok = pl.BlockSpec((chunk, LANES), lambda j, c: (c, j))
    state = pl.BlockSpec((chunk, blk), lambda j, c: (c, j))
    enter = pl.BlockSpec((SUBLANES, blk), lambda j, c: (c, j))
    return _hosted_call(
        body, carry, _edge_2d(N_SSM_BLOCKS, nc), name="ssm_fwd", grid=(N_SSM_BLOCKS, nc),
        in_specs=[tok, row, row, b_mat, b_mat, c_mat, c_mat, pl.BlockSpec((1, LANES), lambda j, c: (0, j))],
        out_specs=[tok, state, state, enter, enter],
        out_shape=[jax.ShapeDtypeStruct((T, SSM_W), F32), jax.ShapeDtypeStruct((T, STATES), F32),
                   jax.ShapeDtypeStruct((T, STATES), F32), jax.ShapeDtypeStruct((nc * SUBLANES, STATES), F32),
                   jax.ShapeDtypeStruct((nc * SUBLANES, STATES), F32)],
        scratch_shapes=[pltpu.VMEM((chunk, LANES), F32), pltpu.VMEM((chunk, LANES), F32),
                        pltpu.VMEM((chunk, blk), F32), pltpu.VMEM((chunk, blk), F32), pltpu.VMEM((SUBLANES, blk), F32)],
        compiler_params=_params(("arbitrary", "arbitrary"), VMEM_MID),
        inputs=(u, a_re, a_im, b_re, b_im, c_re, c_im, d_skip))


def _ssm_bwd(dy, u, h_re, h_im, in_re, in_im, a_re, a_im, b_re, b_im, c_re, c_im, d_skip, chunk, carry=None):
    T = u.shape[0]
    nc = T // chunk
    steps = chunk // SUBLANES
    blk = SSM_LANE_BLOCK

    def body(dy_ref, u_ref, hr_ref, hi_ref, inr_ref, ini_ref, ar_ref, ai_ref, br_ref, bi_ref, cr_ref, ci_ref, dk_ref,
             du_ref, dbr_ref, dbi_ref, dcr_ref, dci_ref, dar_ref, dai_ref, ddk_ref,
             dyseg_ref, useg_ref, duseg_ref, gr_ref, gi_ref, pr_ref, pi_ref, carry_ref, accr_ref, acci_ref):
        c = pl.program_id(1)
        ar, ai = ar_ref[...], ai_ref[...]

        @pl.when(c == 0)
        def _():
            _power_table(ar, ai, pr_ref, pi_ref, steps)
            carry_ref[...] = jnp.zeros_like(carry_ref)
            accr_ref[...] = jnp.zeros_like(accr_ref)
            acci_ref[...] = jnp.zeros_like(acci_ref)

        _to_segments(dy_ref, dyseg_ref, steps)
        _to_segments(u_ref, useg_ref, steps)
        dyb = dyseg_ref[...].astype(BF16)
        ub = useg_ref[...].astype(BF16)
        gr_ref[...] = _dot_nt(dyb, cr_ref[0])
        gi_ref[...] = -_dot_nt(dyb, ci_ref[0])
        dcr = _dot_tn(hr_ref[...].astype(BF16), dyb)
        dci = -_dot_tn(hi_ref[...].astype(BF16), dyb)
        ddk = jnp.sum(dy_ref[...] * u_ref[...], axis=0, keepdims=True)

        first = slice(0, SUBLANES)

        def scan(k4, nxt):
            for j in range(SCAN_UNROLL):
                t = steps - 1 - (k4 * SCAN_UNROLL + j)
                rows = pl.ds(pl.multiple_of(t * SUBLANES, SUBLANES), SUBLANES)
                pr, pi = _cmul_conj(pr_ref[first, :], pi_ref[first, :], nxt[0], nxt[1])
                nxt = (pr + gr_ref[rows, :], pi + gi_ref[rows, :])
                gr_ref[rows, :] = nxt[0]
                gi_ref[rows, :] = nxt[1]
            return nxt

        top = slice(chunk - SUBLANES, chunk)
        zero = jnp.zeros((SUBLANES, blk), F32)
        lax.fori_loop(0, steps // SCAN_UNROLL, scan, (zero, zero))

        gin_re, gin_im, out_re, out_im = _segment_carries(
            gr_ref[0:SUBLANES, :], gi_ref[0:SUBLANES, :], pr_ref[top, :][0:1], -pi_ref[top, :][0:1],
            carry_ref[0:1, :], carry_ref[1:2, :], reverse=True)
        carry_ref[0:1, :] = out_re
        carry_ref[1:2, :] = out_im

        def fix_row(rows, prow, hp_re, hp_im, acc):
            fr, fi = _cmul_conj(pr_ref[prow, :], pi_ref[prow, :], gin_re, gin_im)
            g_re = gr_ref[rows, :] + fr
            g_im = gi_ref[rows, :] + fi
            gr_ref[rows, :] = g_re
            gi_ref[rows, :] = g_im
            return acc[0] + g_re * hp_re + g_im * hp_im, acc[1] + g_im * hp_re - g_re * hp_im

        def fix_at(t, acc):
            aligned = (lambda r: r * SUBLANES) if isinstance(t, int) else (lambda r: pl.multiple_of(r * SUBLANES, SUBLANES))
            rows, before, prow = (pl.ds(aligned(r), SUBLANES) for r in (t, t - 1, steps - 1 - t))
            return fix_row(rows, prow, hr_ref[before, :], hi_ref[before, :], acc)

        def fix(t4, acc):
            for j in range(SCAN_UNROLL):
                acc = fix_at(t4 * SCAN_UNROLL + j, acc)
            return acc

        acc = fix_row(first, top, inr_ref[...], ini_ref[...], (accr_ref[...], acci_ref[...]))
        for t in range(1, SCAN_UNROLL):
            acc = fix_at(t, acc)
        acc_re, acc_im = lax.fori_loop(1, steps // SCAN_UNROLL, fix, acc)
        accr_ref[...] = acc_re
        acci_ref[...] = acc_im

        gbr = gr_ref[...].astype(BF16)
        gbi = gi_ref[...].astype(BF16)
        duseg_ref[...] = _dot_nt(gbr, br_ref[0]) + _dot_nt(gbi, bi_ref[0])
        _from_segments(duseg_ref, dyseg_ref, steps)
        du_ref[...] = (dyseg_ref[...] + dk_ref[...] * dy_ref[...]).astype(BF16)
        dbr = _dot_tn(ub, gbr)
        dbi = _dot_tn(ub, gbi)

        @pl.when(c == 0)
        def _():
            dbr_ref[0] = dbr
            dbi_ref[0] = dbi
            dcr_ref[0] = dcr
            dci_ref[0] = dci
            ddk_ref[...] = ddk

        @pl.when(c > 0)
        def _():
            dbr_ref[0] += dbr
            dbi_ref[0] += dbi
            dcr_ref[0] += dcr
            dci_ref[0] += dci
            ddk_ref[...] += ddk

        @pl.when(c == nc - 1)
        def _():
            dar_ref[...] = jnp.sum(acc_re, axis=0, keepdims=True)
            dai_ref[...] = jnp.sum(acc_im, axis=0, keepdims=True)

    rev = lambda c: nc - 1 - c
    row = pl.BlockSpec((1, blk), lambda j, c: (0, j))
    b_mat = pl.BlockSpec((1, LANES, blk), lambda j, c: (j, 0, 0))
    c_mat = pl.BlockSpec((1, blk, LANES), lambda j, c: (j, 0, 0))
    tok = pl.BlockSpec((chunk, LANES), lambda j, c: (rev(c), j))
    state = pl.BlockSpec((chunk, blk), lambda j, c: (rev(c), j))
    enter = pl.BlockSpec((SUBLANES, blk), lambda j, c: (rev(c), j))
    chan = pl.BlockSpec((1, LANES), lambda j, c: (0, j))
    f32 = lambda *s: jax.ShapeDtypeStruct(s, F32)
    return _hosted_call(
        body, carry, _edge_2d(N_SSM_BLOCKS, nc), name="ssm_bwd", grid=(N_SSM_BLOCKS, nc),
        in_specs=[tok, tok, state, state, enter, enter, row, row, b_mat, b_mat, c_mat, c_mat, chan],
        out_specs=[tok, b_mat, b_mat, c_mat, c_mat, row, row, chan],
        out_shape=[jax.ShapeDtypeStruct((T, SSM_W), BF16), f32(N_SSM_BLOCKS, LANES, blk), f32(N_SSM_BLOCKS, LANES, blk),
                   f32(N_SSM_BLOCKS, blk, LANES), f32(N_SSM_BLOCKS, blk, LANES), f32(1, STATES), f32(1, STATES), f32(1, SSM_W)],
        scratch_shapes=[pltpu.VMEM((chunk, LANES), F32), pltpu.VMEM((chunk, LANES), F32), pltpu.VMEM((chunk, LANES), F32),
                        pltpu.VMEM((chunk, blk), F32), pltpu.VMEM((chunk, blk), F32),
                        pltpu.VMEM((chunk, blk), F32), pltpu.VMEM((chunk, blk), F32),
                        pltpu.VMEM((SUBLANES, blk), F32), pltpu.VMEM((SUBLANES, blk), F32), pltpu.VMEM((SUBLANES, blk), F32)],
        compiler_params=_params(("arbitrary", "arbitrary"), VMEM_BIG),
        inputs=(dy, u, h_re, h_im, in_re, in_im, a_re, a_im, b_re, b_im, c_re, c_im, d_skip))


def _merge_forward(y, att, ga, gs, w_glu, w_ssm, w_attn):
    z = jax.nn.gelu(y)
    zb = z.astype(BF16)
    gl = jax.nn.sigmoid(_dot(zb, w_glu))
    z2b = (z * gl).astype(BF16)
    y_ssm = _dot(z2b, w_ssm)
    y_attn = _dot(att, w_attn)
    sa = jax.nn.sigmoid(ga)
    ss = jax.nn.sigmoid(gs)
    merged = (sa * y_attn + ss * y_ssm).astype(BF16)
    return z, zb, gl, z2b, y_ssm, y_attn, sa, ss, merged


def _merge_fwd(x, y, att, ga, gs, g2, g3, w_glu, w_ssm, w_attn, w_out, tile):
    T = x.shape[0]

    def body(x_ref, y_ref, att_ref, ga_ref, gs_ref, g2_ref, g3_ref, wg_ref, ws_ref, wa_ref, wo_ref, x1_ref, o_ref, h2_ref):
        merged = _merge_forward(y_ref[...], att_ref[...], ga_ref[...], gs_ref[...], wg_ref[...], ws_ref[...], wa_ref[...])[-1]
        o = _dot(merged, wo_ref[...])
        x1 = x_ref[...] + o * _rms_scale(o) * g2_ref[...]
        o_ref[...] = o
        x1_ref[...] = x1
        h2_ref[...] = (x1 * _rms_scale(x1) * g3_ref[...]).astype(BF16)

    tok = lambda w: pl.BlockSpec((tile, w), lambda i: (i, 0))
    vec = _const_spec((1, D_MODEL))
    return pl.pallas_call(
        body, name="merge_fwd", grid=(T // tile,),
        in_specs=[tok(D_MODEL), tok(SSM_W), tok(ATTN_W), tok(D_MODEL), tok(D_MODEL), vec, vec,
                  _const_spec((SSM_W, SSM_W)), _const_spec((SSM_W, D_MODEL)), _const_spec((ATTN_W, D_MODEL)),
                  _const_spec((D_MODEL, D_MODEL))],
        out_specs=[tok(D_MODEL), tok(D_MODEL), tok(D_MODEL)],
        out_shape=_hbm_out([jax.ShapeDtypeStruct((T, D_MODEL), F32), jax.ShapeDtypeStruct((T, D_MODEL), F32),
                            jax.ShapeDtypeStruct((T, D_MODEL), BF16)]),
        compiler_params=_params(("arbitrary",), VMEM_MID),
    )(*_in_hbm(x, y, att, ga, gs, g2, g3, w_glu, w_ssm, w_attn, w_out))


def _merge_bwd(dh2, dx2, x1, o, y, att, ga, gs, g2, g3, w_glu, w_ssm, w_attn, w_out, tile, carry=None):
    T = x1.shape[0]
    n_steps = T // tile

    group = min(2, n_steps)
    staged_widths = (D_MODEL, D_MODEL, ATTN_W, D_MODEL, SSM_W, D_MODEL, SSM_W, SSM_W)

    def body(dh2_ref, dx2_ref, x1_ref, o_ref, y_ref, att_ref, ga_ref, gs_ref, g2_ref, g3_ref, wg_ref, ws_ref, wa_ref, wo_ref,
             dx1_ref, dgates_ref, datt_ref, dy_ref, dwg_hbm, dws_hbm, dwa_hbm, dwo_hbm, dg2_ref, dg3_ref,
             awg_ref, aws_ref, awa_ref, awo_ref, *staged):
        i = pl.program_id(0)
        x1v, ov = x1_ref[...], o_ref[...]
        dxn, dg3 = _rms_bwd(dh2_ref[...], x1v, _rms_scale(x1v), g3_ref[...])
        dx1 = dx2_ref[...] + dxn
        dx1_ref[...] = dx1
        do, dg2 = _rms_bwd(dx1, ov, _rms_scale(ov), g2_ref[...])
        dob = do.astype(BF16)

        yv = y_ref[...]
        att = att_ref[...]
        z, zb, gl, z2b, y_ssm, y_attn, sa, ss, merged = _merge_forward(
            yv, att, ga_ref[...], gs_ref[...], wg_ref[...], ws_ref[...], wa_ref[...])
        dmerged = _dot_nt(dob, wo_ref[...])
        dya = (dmerged * sa).astype(BF16)
        dys = (dmerged * ss).astype(BF16)
        dgates_ref[:, :D_MODEL] = (dmerged * y_attn * sa * (1.0 - sa)).astype(BF16)
        dgates_ref[:, D_MODEL:] = (dmerged * y_ssm * ss * (1.0 - ss)).astype(BF16)
        datt_ref[...] = _dot_nt(dya, wa_ref[...]).astype(BF16)
        dz2 = _dot_nt(dys, ws_ref[...])
        dpre = (dz2 * z * gl * (1.0 - gl)).astype(BF16)
        dz = dz2 * gl + _dot_nt(dpre, wg_ref[...])
        _, gelu_vjp = jax.vjp(jax.nn.gelu, yv)
        dy_ref[...] = gelu_vjp(dz)[0]

        part = pl.ds(pl.multiple_of((i % group) * tile, tile), tile)
        for ref, val in zip(staged, (merged, dob, att, dya, z2b, dys, zb, dpre)):
            ref[part, :] = val

        @pl.when(i == 0)
        def _():
            dg2_ref[...] = dg2
            dg3_ref[...] = dg3

        @pl.when(i > 0)
        def _():
            dg2_ref[...] += dg2
            dg3_ref[...] += dg3

        def weight_grads():
            s_merged, s_dob, s_att, s_dya, s_z2b, s_dys, s_zb, s_dpre = (ref[...] for ref in staged)
            return ((awo_ref, _dot_tn(s_merged, s_dob)), (awa_ref, _dot_tn(s_att, s_dya)),
                    (aws_ref, _dot_tn(s_z2b, s_dys)), (awg_ref, _dot_tn(s_zb, s_dpre)))

        @pl.when(i == group - 1)
        def _():
            for ref, val in weight_grads():
                ref[...] = val

        @pl.when((i % group == group - 1) & (i > group - 1))
        def _():
            for ref, val in weight_grads():
                ref[...] += val

        @pl.when(i == n_steps - 1)
        def _():
            pltpu.sync_copy(awg_ref, dwg_hbm)
            pltpu.sync_copy(aws_ref, dws_hbm)
            pltpu.sync_copy(awa_ref, dwa_hbm)
            pltpu.sync_copy(awo_ref, dwo_hbm)

    tok = lambda w: pl.BlockSpec((tile, w), lambda i: (i, 0))
    vec = _const_spec((1, D_MODEL))
    any_ = pl.BlockSpec(memory_space=pl.ANY)
    vec_out = pl.BlockSpec((1, D_MODEL), lambda i: (0, 0))
    f32 = lambda *s: jax.ShapeDtypeStruct(s, F32)
    bf = lambda *s: jax.ShapeDtypeStruct(s, BF16)
    return _hosted_call(
        body, carry, _edge_1d(n_steps), name="merge_bwd", grid=(n_steps,),
        in_specs=[tok(D_MODEL), tok(D_MODEL), tok(D_MODEL), tok(D_MODEL), tok(SSM_W), tok(ATTN_W), tok(D_MODEL), tok(D_MODEL),
                  vec, vec, _const_spec((SSM_W, SSM_W)), _const_spec((SSM_W, D_MODEL)), _const_spec((ATTN_W, D_MODEL)),
                  _const_spec((D_MODEL, D_MODEL))],
        out_specs=[tok(D_MODEL), tok(2 * D_MODEL), tok(ATTN_W), tok(SSM_W), any_, any_, any_, any_, vec_out, vec_out],
        out_shape=[f32(T, D_MODEL), bf(T, 2 * D_MODEL), bf(T, ATTN_W), f32(T, SSM_W),
                   f32(SSM_W, SSM_W), f32(SSM_W, D_MODEL), f32(ATTN_W, D_MODEL), f32(D_MODEL, D_MODEL),
                   f32(1, D_MODEL), f32(1, D_MODEL)],
        scratch_shapes=[pltpu.VMEM((SSM_W, SSM_W), F32), pltpu.VMEM((SSM_W, D_MODEL), F32),
                        pltpu.VMEM((ATTN_W, D_MODEL), F32), pltpu.VMEM((D_MODEL, D_MODEL), F32)]
        + [pltpu.VMEM((group * tile, wd), BF16) for wd in staged_widths],
        compiler_params=_params(("arbitrary",), VMEM_BIG),
        inputs=(dh2, dx2, x1, o, y, att, ga, gs, g2, g3, w_glu, w_ssm, w_attn, w_out))


FF_SHARD = D_FF // N_DEV


def _mlp_fwd(h2, x1, target, g4, w_ff_in, w_ff_out, tile):
    T = h2.shape[0]
    col_chunk = 2 * FF_SHARD

    def body(h2_ref, x1_ref, tg_ref, g4_ref, wi_ref, wo_ref, a_ref, dfo_ref, dx2_ref, loss_ref, dg4_ref, rr_ref):
        i = pl.program_id(0)
        h2v = h2_ref[...]
        for c in range(D_FF // col_chunk):
            cols = slice(c * col_chunk, (c + 1) * col_chunk)
            a = _dot_nt(h2v, wi_ref[cols, :])
            a_ref[:, cols] = a.astype(BF16)
            ra = jnp.maximum(a, 0.0)
            rr_ref[:, cols] = (ra * ra).astype(BF16)
        f = _dot(rr_ref[...], wo_ref[...])
        r = _rms_scale(f)
        g = g4_ref[...]
        err = x1_ref[...] + f * r * g - tg_ref[...]
        dx2 = err * (1.0 / D_MODEL)
        dx2_ref[...] = dx2
        dfo, dg = _rms_bwd(dx2, f, r, g)
        dfo_ref[...] = dfo.astype(BF16)
        row = lax.broadcasted_iota(jnp.int32, (SUBLANES, LANES), 0)
        col = lax.broadcasted_iota(jnp.int32, (SUBLANES, LANES), 1)
        loss = jnp.where((row == 0) & (col == 0), (0.5 / D_MODEL) * jnp.sum(err * err), 0.0)

        @pl.when(i == 0)
        def _():
            loss_ref[...] = loss
            dg4_ref[...] = dg

        @pl.when(i > 0)
        def _():
            loss_ref[...] += loss
            dg4_ref[...] += dg

    tok = pl.BlockSpec((tile, D_MODEL), lambda i: (i, 0))
    return pl.pallas_call(
        body, name="mlp_fwd", grid=(T // tile,),
        in_specs=[tok, tok, tok, _const_spec((1, D_MODEL)), _const_spec((D_FF, D_MODEL)), _const_spec((D_FF, D_MODEL))],
        out_specs=[pl.BlockSpec((tile, D_FF), lambda i: (i, 0)), tok, tok,
                   pl.BlockSpec((SUBLANES, LANES), lambda i: (0, 0)), pl.BlockSpec((1, D_MODEL), lambda i: (0, 0))],
        out_shape=_hbm_out([jax.ShapeDtypeStruct((T, D_FF), BF16), jax.ShapeDtypeStruct((T, D_MODEL), BF16),
                            jax.ShapeDtypeStruct((T, D_MODEL), F32), jax.ShapeDtypeStruct((SUBLANES, LANES), F32),
                            jax.ShapeDtypeStruct((1, D_MODEL), F32)]),
        scratch_shapes=[pltpu.VMEM((tile, D_FF), BF16)],
        compiler_params=_params(("arbitrary",), VMEM_MAX),
    )(*_in_hbm(h2, x1, target, g4, w_ff_in.reshape(D_FF, D_MODEL), w_ff_out.reshape(D_FF, D_MODEL)))


def _mlp_weight_grads(dfo, a, h2, w_ff_out, row_chunk):
    T = h2.shape[0]

    def body(dfo_ref, h2_ref, a_ref, wo_ref, dwi_ref, dwo_ref, da_ref, rr_ref):
        def rows(r, _):
            sl = pl.ds(pl.multiple_of(r * row_chunk, row_chunk), row_chunk)
            ra = jnp.maximum(a_ref[sl, :].astype(F32), 0.0)
            da_ref[sl, :] = (_dot_nt(dfo_ref[sl, :], wo_ref[0]) * (2.0 * ra)).astype(BF16)
            rr_ref[sl, :] = (ra * ra).astype(BF16)
            return 0

        lax.fori_loop(0, T // row_chunk, rows, 0)
        dwo_ref[0] = _dot_tn(rr_ref[...], dfo_ref[...])
        dwi_ref[0] = _dot_tn(h2_ref[...], da_ref[...])

    return pl.pallas_call(
        body, name="mlp_weight_grads", grid=(N_DEV,),
        in_specs=[_const_spec((T, D_MODEL)), _const_spec((T, D_MODEL)), pl.BlockSpec((T, FF_SHARD), lambda k: (0, k)),
                  pl.BlockSpec((1, FF_SHARD, D_MODEL), lambda k: (k, 0, 0))],
        out_specs=[pl.BlockSpec((1, D_MODEL, FF_SHARD), lambda k: (k, 0, 0)),
                   pl.BlockSpec((1, FF_SHARD, D_MODEL), lambda k: (k, 0, 0)), pl.BlockSpec((T, FF_SHARD), lambda k: (0, k))],
        out_shape=_hbm_out([jax.ShapeDtypeStruct((N_DEV, D_MODEL, FF_SHARD), F32),
                            jax.ShapeDtypeStruct((N_DEV, FF_SHARD, D_MODEL), F32), jax.ShapeDtypeStruct((T, D_FF), BF16)]),
        scratch_shapes=[pltpu.VMEM((T, FF_SHARD), BF16)],
        compiler_params=_params(("arbitrary",), VMEM_MAX),
    )(*_in_hbm(dfo, h2, a, w_ff_out))


def _mlp_input_grad(da, w_ff_in_t, tile):
    T = da.shape[0]

    def body(da_ref, w_ref, o_ref):
        o_ref[...] = _dot(da_ref[...], w_ref[...])

    return pl.pallas_call(
        body, name="mlp_input_grad", grid=(T // tile,),
        in_specs=[pl.BlockSpec((tile, D_FF), lambda i: (i, 0)), _const_spec((D_FF, D_MODEL))],
        out_specs=pl.BlockSpec((tile, D_MODEL), lambda i: (i, 0)),
        out_shape=_hbm_out(jax.ShapeDtypeStruct((T, D_MODEL), F32)),
        compiler_params=_params(("arbitrary",), VMEM_MID),
    )(*_in_hbm(da, w_ff_in_t))


def _block_diag_in(b):
    bt = b.reshape(N_SSM_BLOCKS, GROUPS_PER_BLOCK, GROUP_CH, N_STATE)
    eye = jnp.eye(GROUPS_PER_BLOCK, dtype=b.dtype)
    return jnp.einsum("jacp,ab->jacbp", bt, eye).reshape(N_SSM_BLOCKS, LANES, SSM_LANE_BLOCK)


def _block_diag_in_grad(g):
    g = g.reshape(N_SSM_BLOCKS, GROUPS_PER_BLOCK, GROUP_CH, GROUPS_PER_BLOCK, N_STATE)
    d = jnp.diagonal(g, axis1=1, axis2=3)
    return jnp.transpose(d, (0, 3, 1, 2)).reshape(N_GROUPS, GROUP_CH, N_STATE)


def _block_diag_out(c):
    ct = c.reshape(N_SSM_BLOCKS, GROUPS_PER_BLOCK, GROUP_CH, N_STATE)
    eye = jnp.eye(GROUPS_PER_BLOCK, dtype=c.dtype)
    return jnp.einsum("jacp,ab->japbc", ct, eye).reshape(N_SSM_BLOCKS, SSM_LANE_BLOCK, LANES)


def _block_diag_out_grad(g):
    g = g.reshape(N_SSM_BLOCKS, GROUPS_PER_BLOCK, N_STATE, GROUPS_PER_BLOCK, GROUP_CH)
    d = jnp.diagonal(g, axis1=1, axis2=3)
    return jnp.transpose(d, (0, 3, 2, 1)).reshape(N_GROUPS, GROUP_CH, N_STATE)


def _tiles(T):
    return dict(proj=min(512, T), proj_bwd=min(512, T // 2), merge=min(512, T), merge_bwd=min(256, T),
                mlp_fwd=min(512, T), mlp_bwd=min(512, T), ssm_chunk=min(1024, T))


def _mesh_position():
    x, y, c = lax.axis_index("x"), lax.axis_index("y"), lax.axis_index("c")
    other_chips = [(1 - x, y), (x, 1 - y), (1 - x, 1 - y)]
    return x, y, c, other_chips


def _gather_carry(arrays, pass_on=True):
    n = len(arrays)

    def copies(ins, outs, sems):
        send_sems, recv_sems, local_sems = sems
        x, y, c, chips = _mesh_position()
        me, sibling = (x, y, c), (x, y, 1 - c)

        def copy(a, k, block, to, src=None):
            px, py, pc = block
            dst = outs[a].at[4 * px + 2 * py + pc]
            return pltpu.make_async_remote_copy(
                src_ref=dst if src is None else src, dst_ref=dst, send_sem=send_sems.at[7 * a + k],
                recv_sem=recv_sems.at[7 * a + k], device_id=to, device_id_type=MESH_IDS)

        mine = [pltpu.make_async_copy(ins[a], outs[a].at[4 * x + 2 * y + c], local_sems.at[a]) for a in range(n)]
        first = []
        for a in range(n):
            first.append(copy(a, 0, me, sibling, src=ins[a]))
            first += [copy(a, 1 + j, me, (*chip, c), src=ins[a]) for j, chip in enumerate(chips)]
        return copy, mine, first, me, sibling, chips, c

    def start(ins, outs, sems):
        _, mine, first, *_ = copies(ins, outs, sems)
        for cp in mine + first:
            cp.start()

    def passed_on(copy, sibling, chips, c):
        return [copy(a, 4 + j, (*chip, c), sibling) for a in range(n) for j, chip in enumerate(chips)]

    def middle(ins, outs, sems):
        copy, _, _, me, sibling, chips, c = copies(ins, outs, sems)
        for a in range(n):
            for j, chip in enumerate(chips):
                copy(a, 1 + j, (*chip, c), me).wait_recv()
                copy(a, 4 + j, (*chip, c), sibling).start()

    def finish(ins, outs, sems):
        copy, mine, first, me, sibling, chips, c = copies(ins, outs, sems)
        for a in range(n):
            copy(a, 0, sibling, me).wait_recv()
            for j, chip in enumerate(chips):
                (copy(a, 4 + j, (*chip, 1 - c), me) if pass_on else copy(a, 1 + j, (*chip, c), me)).wait_recv()
        for cp in first + (passed_on(copy, sibling, chips, c) if pass_on else []):
            cp.wait_send()
        for cp in mine:
            cp.wait()

    return _Carry(arrays, [jax.ShapeDtypeStruct((N_DEV,) + a.shape, a.dtype) for a in arrays],
                  [pltpu.SemaphoreType.DMA((7 * n,)), pltpu.SemaphoreType.DMA((7 * n,)), pltpu.SemaphoreType.DMA((n,))],
                  start, finish, middle if pass_on else None)


def _gather_pass_on(gathered, name):
    n = len(gathered)

    def body(*refs):
        zones, send_sems, recv_sems = refs[:n], refs[2 * n], refs[2 * n + 1]
        x, y, c, chips = _mesh_position()
        copies = []
        for a in range(n):
            for j, (px, py) in enumerate(chips):
                block = zones[a].at[4 * px + 2 * py + c]
                copies.append(pltpu.make_async_remote_copy(
                    src_ref=block, dst_ref=block, send_sem=send_sems.at[3 * a + j], recv_sem=recv_sems.at[3 * a + j],
                    device_id=(x, y, 1 - c), device_id_type=MESH_IDS))
        for cp in copies:
            cp.start()
        for cp in copies:
            cp.wait()

    return pl.pallas_call(
        body, name=name, in_specs=[HBM_SPEC] * n, out_specs=[HBM_SPEC] * n,
        out_shape=_hbm_out([jax.ShapeDtypeStruct(g.shape, g.dtype) for g in gathered]),
        scratch_shapes=[pltpu.SemaphoreType.DMA((3 * n,)), pltpu.SemaphoreType.DMA((3 * n,))],
        input_output_aliases={a: a for a in range(n)})(*gathered)


def _pairwise_carry(arrays, n_slots, make_copies):
    n = len(arrays)

    def start(ins, outs, sems):
        for cp in make_copies(ins, outs, sems):
            cp.start()

    def finish(ins, outs, sems):
        for cp in make_copies(ins, outs, sems):
            cp.wait()

    return _Carry(arrays, [jax.ShapeDtypeStruct((n_slots,) + a.shape[1:], a.dtype) for a in arrays],
                  [pltpu.SemaphoreType.DMA((n_slots * n,)), pltpu.SemaphoreType.DMA((n_slots * n,))], start, finish)


def _sibling_carry(grads):
    def make_copies(ins, outs, sems):
        x, y, c, _ = _mesh_position()
        return [pltpu.make_async_remote_copy(
            src_ref=ins[a].at[2 * ch + (1 - c)], dst_ref=outs[a].at[ch], send_sem=sems[0].at[4 * a + ch],
            recv_sem=sems[1].at[4 * a + ch], device_id=(x, y, 1 - c), device_id_type=MESH_IDS)
            for a in range(len(grads)) for ch in range(4)]

    return _pairwise_carry(grads, 4, make_copies)


def _chips_carry(sums):
    def make_copies(ins, outs, sems):
        x, y, c, chips = _mesh_position()
        return [pltpu.make_async_remote_copy(
            src_ref=ins[a].at[2 * px + py], dst_ref=outs[a].at[j], send_sem=sems[0].at[3 * a + j],
            recv_sem=sems[1].at[3 * a + j], device_id=(px, py, c), device_id_type=MESH_IDS)
            for a in range(len(sums)) for j, (px, py) in enumerate(chips)]

    return _pairwise_carry(sums, 3, make_copies)


def _everyone_carry(arrays):
    def make_copies(ins, outs, sems):
        x, y, c, _ = _mesh_position()
        flip = lambda v, bit: 1 - v if bit else v
        return [pltpu.make_async_remote_copy(
            src_ref=ins[a], dst_ref=outs[a].at[r - 1], send_sem=sems[0].at[7 * a + r - 1], recv_sem=sems[1].at[7 * a + r - 1],
            device_id=(flip(x, r & 4), flip(y, r & 2), flip(c, r & 1)), device_id_type=MESH_IDS)
            for a in range(len(arrays)) for r in range(1, N_DEV)]

    carry = _pairwise_carry([jax.ShapeDtypeStruct((1,) + a.shape, a.dtype) for a in arrays], N_DEV - 1, make_copies)
    carry.inputs = list(arrays)
    return carry


def _sum_everyone(own, received, me, name, after=()):
    def body(me_ref, own_ref, r_ref, *refs):
        g = None
        for d in range(N_DEV):
            relation = jnp.bitwise_xor(d, me_ref[0])
            part = jnp.where(relation == 0, own_ref[...], r_ref[jnp.maximum(relation - 1, 0)])
            g = part if g is None else g + part
        refs[-1][...] = g

    whole = lambda shape: pl.BlockSpec(shape, lambda i, me_ref: (0,) * len(shape))
    return pl.pallas_call(
        body, name=name,
        grid_spec=pltpu.PrefetchScalarGridSpec(
            num_scalar_prefetch=1, grid=(1,), in_specs=[whole(own.shape), whole(received.shape)] + [HBM_SPEC] * len(after),
            out_specs=whole(own.shape)),
        out_shape=jax.ShapeDtypeStruct(own.shape, F32))(me, *_in_hbm(own, received), *after)


SEM_SPEC = pl.BlockSpec(memory_space=pltpu.SEMAPHORE)
DATAFLOW_EFFECT = pltpu.SideEffectType.DATAFLOW_SIDE_EFFECTING


def _exchange_start(carry, name, after=()):
    n, n_sems = len(carry.inputs), len(carry.sems)
    lands = [lax.empty(s.shape, s.dtype) for s in carry.out_shapes]

    def body(*refs):
        first_out = 2 * n + len(after)
        srcs, zones, sems, token = refs[:n], refs[n:2 * n], refs[first_out:first_out + n_sems], refs[-1]
        carry.start(srcs, zones, sems)
        token[...] = jnp.zeros_like(token)

    outs = pl.pallas_call(
        body, name=name, in_specs=[HBM_SPEC] * (2 * n + len(after)),
        out_specs=[SEM_SPEC] * n_sems + [HBM_SPEC] * (2 * n) + [pl.BlockSpec(memory_space=pltpu.VMEM)],
        out_shape=list(carry.sems) + _hbm_out([jax.ShapeDtypeStruct(a.shape, a.dtype) for a in carry.inputs])
        + _hbm_out(carry.out_shapes) + [jax.ShapeDtypeStruct((SUBLANES, LANES), F32)],
        input_output_aliases={j: n_sems + j for j in range(2 * n)},
        compiler_params=pltpu.CompilerParams(has_side_effects=DATAFLOW_EFFECT),
    )(*_in_hbm(*carry.inputs, *lands), *after)
    return outs[:-1], outs[-1]


def _exchange_wait(carry, in_flight, after, name):
    n, n_sems = len(carry.inputs), len(carry.sems)
    sems, srcs, zones = in_flight[:n_sems], in_flight[n_sems:n_sems + n], in_flight[n_sems + n:]

    def body(*refs):
        src_refs, zone_refs, sem_refs = refs[:n], refs[n:2 * n], refs[2 * n:2 * n + n_sems]
        carry.finish(src_refs, zone_refs, sem_refs)

    outs = pl.pallas_call(
        body, name=name, in_specs=[HBM_SPEC] * (2 * n) + [SEM_SPEC] * n_sems + [HBM_SPEC] * len(after),
        out_specs=[HBM_SPEC] * (2 * n),
        out_shape=_hbm_out([jax.ShapeDtypeStruct(a.shape, a.dtype) for a in carry.inputs]) + _hbm_out(carry.out_shapes),
        input_output_aliases={j: j for j in range(2 * n)},
        compiler_params=pltpu.CompilerParams(has_side_effects=DATAFLOW_EFFECT),
    )(*srcs, *zones, *sems, *after)
    return list(outs[:n]), list(outs[n:])


def _add_sibling(grads8, recvs, place, row_tiles, name):
    k = len(grads8)
    g4 = [g.reshape(4, 2, *g.shape[1:]) for g in grads8]

    def body(place_ref, *refs):
        g_refs, r_refs, o_refs, ob_refs = (refs[j * k:(j + 1) * k] for j in range(4))
        own = pl.program_id(1) == place_ref[1]
        for g_ref, r_ref, o_ref, ob_ref in zip(g_refs, r_refs, o_refs, ob_refs):
            s = g_ref[0] + r_ref[...]
            ob_ref[...] = s.astype(BF16)

            @pl.when(own)
            def _(o_ref=o_ref, s=s):
                o_ref[...] = s[0]

    def blocks(make):
        return [make(g.shape[1] // row_tiles, g.shape[2]) for g in grads8]

    slot = lambda tr, C: pl.BlockSpec((1, tr, C), lambda r, ch, place_ref: (ch, r, 0))
    outs = pl.pallas_call(
        body, name=name,
        grid_spec=pltpu.PrefetchScalarGridSpec(
            num_scalar_prefetch=1, grid=(row_tiles, 4),
            in_specs=blocks(lambda tr, C: pl.BlockSpec((1, 1, tr, C), lambda r, ch, place_ref: (ch, place_ref[0], r, 0)))
            + blocks(slot),
            out_specs=blocks(lambda tr, C: pl.BlockSpec((tr, C), lambda r, ch, place_ref: (r, 0))) + blocks(slot)),
        out_shape=_hbm_out([jax.ShapeDtypeStruct(g.shape[1:], F32) for g in grads8]
                           + [jax.ShapeDtypeStruct((4,) + g.shape[1:], BF16) for g in grads8]),
        compiler_params=_params(("arbitrary", "arbitrary")),
    )(place, *_in_hbm(*g4, *recvs))
    return list(outs[:k]), list(outs[k:])


def _adam_math(w, g, m, v):
    m = ADAM_B1 * m + (1.0 - ADAM_B1) * g
    v = ADAM_B2 * v + (1.0 - ADAM_B2) * jnp.square(g)
    m_hat = m / (1.0 - ADAM_B1 ** ADAM_STEP)
    v_hat = v / (1.0 - ADAM_B2 ** ADAM_STEP)
    delta = -ADAM_LR * (m_hat / (jnp.sqrt(v_hat) + ADAM_EPS) + ADAM_WD * w)
    return delta, m, v


def _adam_big(ws, ms, vs, chip_sums, recvs, row_tiles, name, after=()):
    k = len(ws)

    def body(*refs):
        refs = refs[:5 * k] + refs[5 * k + len(after):]
        w_refs, m_refs, v_refs, s_refs, r_refs, g_refs, d_refs, nm_refs, nv_refs = (refs[j * k:(j + 1) * k] for j in range(9))
        for a in range(k):
            r_ref = r_refs[a]
            g = s_refs[a][...] + r_ref[0].astype(F32) + r_ref[1].astype(F32) + r_ref[2].astype(F32)
            g_refs[a][...] = g
            d_refs[a][...], nm_refs[a][...], nv_refs[a][...] = _adam_math(w_refs[a][...], g, m_refs[a][...], v_refs[a][...])

    def blocks(make):
        return [make(w.shape[0] // row_tiles, w.shape[1]) for w in ws]

    blk = lambda tr, C: pl.BlockSpec((tr, C), lambda r: (r, 0))
    outs = pl.pallas_call(
        body, name=name, grid=(row_tiles,),
        in_specs=blocks(blk) * 4 + blocks(lambda tr, C: pl.BlockSpec((3, tr, C), lambda r: (0, r, 0))) + [HBM_SPEC] * len(after),
        out_specs=blocks(blk) * 4,
        out_shape=[jax.ShapeDtypeStruct(w.shape, F32) for w in ws] * 4,
        compiler_params=_params(("arbitrary",)),
    )(*_in_hbm(*ws, *ms, *vs, *chip_sums, *recvs), *after)
    return [list(outs[j * k:(j + 1) * k]) for j in range(4)]


def _sum_partials(partials, name, after=()):
    def body(p_ref, *refs):
        g = p_ref[0]
        for d in range(1, partials.shape[0]):
            g = g + p_ref[d]
        refs[-1][...] = g

    return pl.pallas_call(body, name=name, grid=(1,), in_specs=[_whole(partials.shape)] + [HBM_SPEC] * len(after),
                          out_specs=_whole(partials.shape[1:]),
                          out_shape=jax.ShapeDtypeStruct(partials.shape[1:], F32))(*_in_hbm(partials), *after)


def _adam_small(ws, ms, vs, gs):
    n = len(ws)

    def body(*refs):
        w_refs, m_refs, v_refs, g_refs = (refs[i * n:(i + 1) * n] for i in range(4))
        d_refs, nm_refs, nv_refs = (refs[(4 + i) * n:(5 + i) * n] for i in range(3))
        for j in range(n):
            d_refs[j][...], nm_refs[j][...], nv_refs[j][...] = _adam_math(
                w_refs[j][...], g_refs[j][...], m_refs[j][...], v_refs[j][...])

    specs = [_whole(w.shape) for w in ws]
    outs = pl.pallas_call(body, name="adam_small", grid=(1,), in_specs=specs * 4, out_specs=specs * 3,
                          out_shape=[jax.ShapeDtypeStruct(w.shape, F32) for w in ws] * 3,
                          compiler_params=_params(("arbitrary",), VMEM_MID))(*_in_hbm(*ws, *ms, *vs, *gs))
    return outs[:n], outs[n:2 * n], outs[2 * n:]


PACK_QUANTUM = SUBLANES * LANES


def _pack(named, names):
    parts = []
    for nme in names:
        flat = named[nme].reshape(-1)
        parts.append(jnp.pad(flat, (0, -flat.size % PACK_QUANTUM)))
    return jnp.concatenate(parts).reshape(-1, LANES)


def _unpack(packed, shapes, names):
    flat = packed.reshape(-1)
    out, pos = {}, 0
    for nme in names:
        size = math.prod(shapes[nme])
        out[nme] = flat[pos:pos + size].reshape(shapes[nme])
        pos += size + (-size % PACK_QUANTUM)
    return out


BIG = ("w_in", "w_glu", "w_attn_branch", "w_ssm_branch", "w_out", "w_ff_in", "w_ff_out")
COLUMN_SHARDED = ("w_in", "w_attn_branch", "w_ssm_branch", "w_ff_in")
SMALL = ("norm_mix_pre", "norm_mix_post", "norm_mlp_pre", "norm_mlp_post", "rel_bias", "sinks", "lam_re", "lam_im",
         "log_dt", "b_re", "b_im", "c_re", "c_im", "d_skip")
SWAPPED_SMALL = ("rel_bias", "b_re", "b_im")
SMALL_LATE = ("norm_mix_pre", "rel_bias", "sinks", "loss")
SMALL_BEFORE_ATTN_BWD = tuple(n for n in SMALL if n not in SMALL_LATE)
ALL_WEIGHTS = ("norm_mix_pre", "norm_mix_post", "norm_mlp_pre", "norm_mlp_post", "w_in", "rel_bias", "sinks", "lam_re",
               "lam_im", "log_dt", "b_re", "b_im", "c_re", "c_im", "d_skip", "w_glu", "w_attn_branch", "w_ssm_branch",
               "w_out", "w_ff_in", "w_ff_out")


def _full_from_gathered(name, gathered):
    _, r, c = gathered.shape
    if name in COLUMN_SHARDED:
        return jnp.transpose(gathered, (1, 0, 2)).reshape(r, N_DEV * c)
    return gathered.reshape(N_DEV * r, c)


def _blocks_from_full(name, full):
    r, c = full.shape
    if name in COLUMN_SHARDED:
        return jnp.transpose(full.reshape(r, N_DEV, c // N_DEV), (1, 0, 2))
    return full.reshape(N_DEV, r // N_DEV, c)


def kernel(x, norm_mix_pre, norm_mix_post, norm_mlp_pre, norm_mlp_post, w_in, rel_bias, sinks, lam_re, lam_im, log_dt, b_re, b_im, c_re, c_im, d_skip, w_glu, w_attn_branch, w_ssm_branch, w_out, w_ff_in, w_ff_out, loss_target, m_norm_mix_pre, m_norm_mix_post, m_norm_mlp_pre, m_norm_mlp_post, m_w_in, m_rel_bias, m_sinks, m_lam_re, m_lam_im, m_log_dt, m_b_re, m_b_im, m_c_re, m_c_im, m_d_skip, m_w_glu, m_w_attn_branch, m_w_ssm_branch, m_w_out, m_w_ff_in, m_w_ff_out, v_norm_mix_pre, v_norm_mix_post, v_norm_mlp_pre, v_norm_mlp_post, v_w_in, v_rel_bias, v_sinks, v_lam_re, v_lam_im, v_log_dt, v_b_re, v_b_im, v_c_re, v_c_im, v_d_skip, v_w_glu, v_w_attn_branch, v_w_ssm_branch, v_w_out, v_w_ff_in, v_w_ff_out):
    args = dict(locals())
    w = {n: args[n] for n in ALL_WEIGHTS}
    m = {n: args["m_" + n] for n in ALL_WEIGHTS}
    v = {n: args["v_" + n] for n in ALL_WEIGHTS}
    core = lax.axis_index("c").astype(jnp.int32).reshape(1)
    chip = (2 * lax.axis_index("x") + lax.axis_index("y")).astype(jnp.int32).reshape(1)
    xs, target = x[0], loss_target[0]
    t = _tiles(xs.shape[0])
    local = lambda d, n: d[n][0].T if n == "w_in" else d[n][0]
    gather_in = _gather_carry([local(w, "w_in").astype(BF16)], pass_on=False)
    gather_in_flight, token = _exchange_start(gather_in, "gather_w_in_start")
    one = token[0:1, 0:1] + 1.0
    shard = {n: (local(w, n) * one).astype(BF16) for n in BIG if n != "w_in"}
    shard["w_ff_in"] = shard["w_ff_in"].T
    view = lambda n, a: jnp.swapaxes(a, -1, -2) if n in SWAPPED_SMALL else a
    small = {n: (view(n, w[n]) if n == "rel_bias" else view(n, w[n])[0]) for n in SMALL}
    g1, g2, g3, g4 = (small[n].reshape(1, D_MODEL) for n in ("norm_mix_pre", "norm_mix_post", "norm_mlp_pre", "norm_mlp_post"))
    bucket = jnp.asarray(_bucket_table())
    rel_b, sink = small["rel_bias"], small["sinks"].reshape(1, N_HEADS)
    lam_r, lam_i = small["lam_re"].reshape(1, STATES), small["lam_im"].reshape(1, STATES)
    ldt_rep = jnp.repeat(small["log_dt"].reshape(N_GROUPS), N_STATE).reshape(1, STATES)
    bd_re, bd_im = _block_diag_in(small["b_re"] * one), _block_diag_in(small["b_im"] * one)
    cm_re, cm_im = _block_diag_out(small["c_re"] * one).astype(BF16), _block_diag_out(small["c_im"] * one).astype(BF16)
    dsk = small["d_skip"].reshape(1, SSM_W)
    a_re, a_im, bm_re, bm_im = _ssm_prep(lam_r, lam_i, ldt_rep, bd_re, bd_im)

    travelled_behind = list(shard.values()) + [a_re, a_im, bm_re, bm_im, cm_re, cm_im]
    _, landed = _exchange_wait(gather_in, gather_in_flight, travelled_behind, "gather_w_in_wait")
    (g_in,) = _gather_pass_on(landed, "gather_w_in_pass_on")
    wf_in = g_in.reshape(IN_W, D_MODEL)
    merge_names = ("w_glu", "w_attn_branch", "w_ssm_branch", "w_out")
    (q, k, vv, u, ga, gs, h), gathered = _in_proj_fwd(xs, g1, wf_in, t["proj"], _gather_carry([shard[n] for n in merge_names]))
    wf = {n: _full_from_gathered(n, g) for n, g in zip(merge_names, gathered)}
    (att,), (wf_ff_in,) = _attn_fwd(q, k, vv, bucket, rel_b, sink, _gather_carry([shard["w_ff_in"]]))
    (y, h_re, h_im, in_re, in_im), (wf_ff_out,) = _ssm_fwd(
        u, a_re, a_im, bm_re, bm_im, cm_re, cm_im, dsk, t["ssm_chunk"], _gather_carry([shard["w_ff_out"]]))
    x1, o, h2 = _merge_fwd(xs, y, att, ga, gs, g2, g3, wf["w_glu"], wf["w_ssm_branch"], wf["w_attn_branch"], wf["w_out"],
                           t["merge"])
    a, dfo, dx2, loss_blk, dg4 = _mlp_fwd(h2, x1, target, g4, wf_ff_in, wf_ff_out, t["mlp_fwd"])

    groups = {"ff": 4, "merge": 1, "w_in": 2}

    def add_sibling(group, blocks, received):
        return _add_sibling(blocks, received, jnp.concatenate([core, chip]), groups[group], "add_sibling_" + group)

    ff_names = ("w_ff_in", "w_ff_out")
    dw_ff_in, dw_ff_out, da = _mlp_weight_grads(dfo, a, h2, wf_ff_out, t["mlp_bwd"])
    dh2 = _mlp_input_grad(da, wf_ff_in.reshape(D_FF, D_MODEL), t["mlp_bwd"])
    ff_blocks = [dw_ff_in, dw_ff_out]
    (dx1, dgates, datt, dy, dw_glu, dw_ssm, dw_attn, dw_out, dg2, dg3), ff_recv = _merge_bwd(
        dh2, dx2, x1, o, y, att, ga, gs, g2, g3, wf["w_glu"], wf["w_ssm_branch"], wf["w_attn_branch"], wf["w_out"],
        t["merge_bwd"], _sibling_carry(ff_blocks))
    ff_sums, ff_sums_bf = add_sibling("ff", ff_blocks, ff_recv)
    merge_blocks = [_blocks_from_full(n, g) for n, g in zip(merge_names, (dw_glu, dw_attn, dw_ssm, dw_out))]
    (du, dbm_re, dbm_im, dcm_re, dcm_im, da_re, da_im, dd_skip), carried = _ssm_bwd(
        dy, u, h_re, h_im, in_re, in_im, a_re, a_im, bm_re, bm_im, cm_re, cm_im, dsk, t["ssm_chunk"],
        _join(_chips_carry(ff_sums_bf), _sibling_carry(merge_blocks)))
    ff_from_chips, merge_recv = carried[:2], carried[2:]
    merge_sums, merge_sums_bf = add_sibling("merge", merge_blocks, merge_recv)
    dbd_re, dbd_im, dlam_re, dlam_im, dldt_rep = _ssm_prep_bwd(lam_r, lam_i, ldt_rep, bd_re, bd_im, dbm_re, dbm_im, da_re, da_im)
    dlog_dt = _group_sum(dldt_rep.reshape(N_GROUPS, N_STATE))
    shapes = {n: view(n, w[n]).shape for n in SMALL}
    shapes["loss"] = (1,)
    small_grads = dict(
        norm_mix_post=dg2, norm_mlp_pre=dg3, norm_mlp_post=dg4, lam_re=dlam_re, lam_im=dlam_im, log_dt=dlog_dt,
        b_re=_block_diag_in_grad(dbd_re), b_im=_block_diag_in_grad(dbd_im),
        c_re=_block_diag_out_grad(dcm_re), c_im=_block_diag_out_grad(dcm_im), d_skip=dd_skip)
    packed_early = _pack({n: small_grads[n].reshape(shapes[n]) for n in SMALL_BEFORE_ATTN_BWD}, SMALL_BEFORE_ATTN_BWD)
    (dq, dkv, attn_small), carried = _attn_bwd(
        q, k, vv, datt, bucket, rel_b, sink, _join(_chips_carry(merge_sums_bf), _gather_carry([packed_early])))
    merge_from_chips, partials_early = carried[:-1], carried[-1]

    dparts = (dq, dkv, du, dgates)
    dw_in_t = _in_proj_weight_grad(h, dparts)
    in_blocks = [dw_in_t.reshape(N_DEV, IN_W // N_DEV, D_MODEL)]
    to_sibling = _sibling_carry(in_blocks)
    in_flight, token = _exchange_start(to_sibling, "w_in_sibling_start")
    n_tiles = xs.shape[0] // t["proj_bwd"]
    (grad_x, dg1), _ = _in_proj_input_grad(xs, g1 + token[0:1, 0:1], wf_in, dx1, dparts, t["proj_bwd"], 0, n_tiles, "in_proj_input_grad")
    late = dict(norm_mix_pre=dg1, rel_bias=attn_small[:, :N_BUCKETS, 0], sinks=attn_small[:, N_BUCKETS, 0], loss=loss_blk[0:1, 0])
    packed_late = _pack({n: late[n].reshape(shapes[n]) for n in SMALL_LATE}, SMALL_LATE)
    to_everyone = _everyone_carry([packed_late])
    in_blocks, in_recv = _exchange_wait(to_sibling, in_flight, [packed_late], "w_in_sibling_wait")
    in_sums, in_sums_bf = add_sibling("w_in", in_blocks, in_recv)
    to_chips = _chips_carry(in_sums_bf)
    started, chips_started = _exchange_start(_join(to_everyone, to_chips), "late_grads_and_w_in_chips_start")
    late_in_flight, in_flight = [[started[j] for j in js] for js in ((0, 1, 4, 6), (2, 3, 5, 7))]

    grads, deltas, new_m, new_v = {}, {}, {}, {}

    def adam_group(group, names, sums, received, after=()):
        outs = _adam_big(*[[local(d, n) for n in names] for d in (w, m, v)], sums, received, groups[group],
                         "adam_" + group, after)
        for store, vals in zip((grads, deltas, new_m, new_v), outs):
            store.update({n: (o.T if n == "w_in" else o)[None] for n, o in zip(names, vals)})

    adam_group("ff", ff_names, ff_sums, ff_from_chips, [chips_started])
    adam_group("merge", merge_names, merge_sums, merge_from_chips, [chips_started])

    grads.update(_unpack(_sum_partials(partials_early, "sum_small_grads", [chips_started]), shapes, SMALL_BEFORE_ATTN_BWD))
    (packed_late,), (late_received,) = _exchange_wait(
        to_everyone, late_in_flight, [new_v["w_ff_out"], new_v["w_out"]], "late_grads_wait")
    grads.update(_unpack(_sum_everyone(packed_late, late_received, 2 * chip + core, "sum_late_grads"), shapes, SMALL_LATE))
    loss = grads.pop("loss").reshape(())
    small_out = _adam_small(*[[view(n, d[n]) for n in SMALL] for d in (w, m, v)], [grads[n] for n in SMALL])
    for store, vals in zip((deltas, new_m, new_v), small_out):
        store.update(zip(SMALL, vals))
    for store in (grads, deltas, new_m, new_v):
        store.update({n: view(n, store[n]) for n in SWAPPED_SMALL})

    busy = [new_v["w_ff_out"], new_v["w_out"], deltas["norm_mix_pre"]]
    _, (in_from_chips,) = _exchange_wait(to_chips, in_flight, busy, "w_in_chips_wait")
    adam_group("w_in", ("w_in",), in_sums, [in_from_chips])

    return (loss, grad_x[None], *[grads[n] for n in ALL_WEIGHTS], *[deltas[n] for n in ALL_WEIGHTS],
            *[new_m[n] for n in ALL_WEIGHTS], *[new_v[n] for n in ALL_WEIGHTS])
```

```python
import math

import jax
import jax.numpy as jnp
import numpy as np
from jax import lax
from jax.experimental import pallas as pl
from jax.experimental.pallas import tpu as pltpu

F32 = jnp.float32
BF16 = jnp.bfloat16

D_MODEL = 1024
N_HEADS = 8
HEAD_DIM = 64
ATTN_W = 512
KV_W = 128
BLOCK = 128
N_BUCKETS = 32
SSM_W = 512
N_GROUPS = 32
N_STATE = 64
GROUP_CH = 16
STATES = N_GROUPS * N_STATE
D_FF = 4096
IN_W = 3328
SPLITS = (0, 512, 640, 768, 1280, 2304, 3328)
RMS_EPS = 1e-6
NEG_INF = -1e30
SUBLANES = 8
LANES = 128
SSM_LANE_BLOCK = 512
N_SSM_BLOCKS = STATES // SSM_LANE_BLOCK
GROUPS_PER_BLOCK = SSM_LANE_BLOCK // N_STATE
VMEM_BIG = 52 * 1024 * 1024
VMEM_MID = 40 * 1024 * 1024
VMEM_MAX = 60 * 1024 * 1024

ADAM_LR = 0.001
ADAM_B1 = 0.9
ADAM_B2 = 0.999
ADAM_EPS = 1e-08
ADAM_WD = 0.01
ADAM_STEP = 10

N_DEV = 8


def _dot(a, b):
    return jnp.dot(a, b, preferred_element_type=F32)


def _dot_nt(a, b):
    return lax.dot_general(a, b, (((1,), (1,)), ((), ())), preferred_element_type=F32)


def _dot_tn(a, b):
    return lax.dot_general(a, b, (((0,), (0,)), ((), ())), preferred_element_type=F32)


def _rms_scale(x):
    return lax.rsqrt(jnp.mean(x * x, axis=-1, keepdims=True) + RMS_EPS)


def _rms_bwd(dy, x, r, g):
    t = dy * g
    dx = r * t - x * (r * r * r) * jnp.mean(t * x, axis=-1, keepdims=True)
    dg = jnp.sum(dy * x * r, axis=0, keepdims=True)
    return dx, dg


def _const_spec(shape):
    nd = len(shape)
    return pl.BlockSpec(shape, lambda *_: (0,) * nd, pipeline_mode=pl.Buffered(1))


def _in_hbm(*arrays):
    return tuple(pltpu.with_memory_space_constraint(a, pltpu.HBM) for a in arrays)


def _hbm_out(shapes):
    if isinstance(shapes, (list, tuple)):
        return [_hbm_out(s) for s in shapes]
    return shapes if isinstance(shapes, pl.MemoryRef) else pltpu.HBM(shapes.shape, shapes.dtype)


def _whole(shape):
    nd = len(shape)
    return pl.BlockSpec(shape, lambda *_: (0,) * nd)


def _params(sem, vmem=None):
    return pltpu.CompilerParams(dimension_semantics=sem, vmem_limit_bytes=vmem)


MESH_IDS = pl.DeviceIdType.MESH
HBM_SPEC = pl.BlockSpec(memory_space=pl.ANY)


class _Carry:
    def __init__(self, inputs, out_shapes, sems, start, finish, middle=None):
        self.inputs, self.out_shapes, self.sems = list(inputs), list(out_shapes), list(sems)
        self.start, self.middle, self.finish = start, middle, finish


def _join(a, b):
    na_in, na_out, na_sem = len(a.inputs), len(a.out_shapes), len(a.sems)

    def both(phase):
        def run(ins, outs, sems):
            for carry, lo in ((a, True), (b, False)):
                part = (lambda seq, n: seq[:n] if lo else seq[n:])
                if getattr(carry, phase) is not None:
                    getattr(carry, phase)(part(ins, na_in), part(outs, na_out), part(sems, na_sem))
        return run

    middle = both("middle") if (a.middle or b.middle) else None
    return _Carry(a.inputs + b.inputs, a.out_shapes + b.out_shapes, a.sems + b.sems, both("start"), both("finish"), middle)


def _hosted_call(body, carry, edge, *, name, grid, in_specs, out_specs, out_shape, scratch_shapes, compiler_params, inputs):
    n_in, n_out = len(in_specs), len(out_specs)
    inputs = [a if s.memory_space == pltpu.SMEM else _in_hbm(a)[0] for a, s in zip(inputs, in_specs)]
    out_shape = _hbm_out(list(out_shape))
    if carry is None:
        outs = pl.pallas_call(body, name=name, grid=grid, in_specs=in_specs, out_specs=out_specs, out_shape=out_shape,
                              scratch_shapes=scratch_shapes, compiler_params=compiler_params)(*inputs)
        return list(outs), []
    c_in, c_out, c_sem = len(carry.inputs), len(carry.out_shapes), len(carry.sems)

    def wrapped(*refs):
        ins, refs = refs[:n_in], refs[n_in:]
        cins, refs = refs[:c_in], refs[c_in:]
        outs, refs = refs[:n_out], refs[n_out:]
        couts, refs = refs[:c_out], refs[c_out:]
        scratch, csems = refs[:len(refs) - c_sem], refs[len(refs) - c_sem:]
        first, middle, last = edge()

        @pl.when(first)
        def _():
            carry.start(cins, couts, csems)

        body(*ins, *outs, *scratch)

        if carry.middle is not None:
            @pl.when(middle)
            def _():
                carry.middle(cins, couts, csems)

        @pl.when(last)
        def _():
            carry.finish(cins, couts, csems)

    outs = pl.pallas_call(
        wrapped, name=name, grid=grid, in_specs=list(in_specs) + [HBM_SPEC] * c_in,
        out_specs=list(out_specs) + [HBM_SPEC] * c_out, out_shape=out_shape + _hbm_out(carry.out_shapes),
        scratch_shapes=list(scratch_shapes) + carry.sems, compiler_params=compiler_params)(*inputs, *_in_hbm(*carry.inputs))
    return list(outs[:n_out]), list(outs[n_out:])


def _pass_on_step(n_steps):
    return max(0, min((7 * n_steps) // 8, n_steps - 2))


def _edge_1d(n_steps, pass_on_last=False):
    middle = n_steps - 1 if pass_on_last else _pass_on_step(n_steps)
    return lambda: (pl.program_id(0) == 0, pl.program_id(0) == middle, pl.program_id(0) == n_steps - 1)


def _edge_2d(n0, n1):
    def edge():
        step = pl.program_id(0) * n1 + pl.program_id(1)
        return step == 0, step == _pass_on_step(n0 * n1), step == n0 * n1 - 1
    return edge


def _pre_norm(x, g1, tile):
    T = x.shape[0]

    def body(x_ref, g_ref, h_ref):
        xv = x_ref[...]
        h_ref[...] = (xv * _rms_scale(xv) * g_ref[...]).astype(BF16)

    rows = pl.BlockSpec((tile, D_MODEL), lambda i: (i, 0))
    return pl.pallas_call(body, name="pre_norm", grid=(T // tile,), in_specs=[rows, _const_spec((1, D_MODEL))],
                          out_specs=rows, out_shape=_hbm_out([jax.ShapeDtypeStruct((T, D_MODEL), BF16)])[0],
                          compiler_params=_params(("arbitrary",)))(*_in_hbm(x, g1))


def _in_proj_fwd(h, w_in_t, tile, carry=None):
    T = h.shape[0]

    def body(h_ref, w_ref, q_ref, k_ref, v_ref, u_ref, ga_ref, gs_ref):
        h = h_ref[...]
        outs = (q_ref, k_ref, v_ref, u_ref, ga_ref, gs_ref)
        for p, o_ref in enumerate(outs):
            o_ref[...] = _dot_nt(h, w_ref[SPLITS[p]:SPLITS[p + 1], :]).astype(o_ref.dtype)

    widths = [SPLITS[p + 1] - SPLITS[p] for p in range(6)]
    dtypes = [BF16, BF16, BF16, F32, F32, F32]
    return _hosted_call(
        body, carry, _edge_1d(T // tile), name="in_proj_fwd", grid=(T // tile,),
        in_specs=[pl.BlockSpec((tile, D_MODEL), lambda i: (i, 0)), _const_spec((IN_W, D_MODEL))],
        out_specs=[pl.BlockSpec((tile, w), lambda i: (i, 0)) for w in widths],
        out_shape=[jax.ShapeDtypeStruct((T, w), dt) for w, dt in zip(widths, dtypes)],
        scratch_shapes=[], compiler_params=_params(("arbitrary",), VMEM_MID), inputs=(h, w_in_t))


PROJ_PARTS = (512, 256, 512, 2048)
PROJ_GRAD_BLOCK = 256


def _in_proj_weight_grad(h, dparts):
    T = h.shape[0]
    blocks = [wd // PROJ_GRAD_BLOCK for wd in PROJ_PARTS]
    starts = [sum(blocks[:p]) for p in range(len(blocks))]

    def body(h_ref, *refs):
        part_refs, o_ref = refs[:-1], refs[-1]
        j = pl.program_id(0)
        for p_ref, start, count in zip(part_refs, starts, blocks):
            @pl.when((j >= start) & (j < start + count))
            def _(p_ref=p_ref):
                o_ref[...] = _dot_tn(p_ref[...], h_ref[...])

    def part_spec(start, count):
        return pl.BlockSpec((T, PROJ_GRAD_BLOCK), lambda j: (0, jnp.clip(j - start, 0, count - 1)))

    return pl.pallas_call(
        body, name="in_proj_weight_grad", grid=(sum(blocks),),
        in_specs=[_const_spec((T, D_MODEL))] + [part_spec(s, c) for s, c in zip(starts, blocks)],
        out_specs=pl.BlockSpec((PROJ_GRAD_BLOCK, D_MODEL), lambda j: (j, 0)),
        out_shape=_hbm_out(jax.ShapeDtypeStruct((IN_W, D_MODEL), F32)),
        compiler_params=_params(("arbitrary",), VMEM_MID),
    )(*_in_hbm(h, *dparts))


def _in_proj_input_grad(x, g1, w_in_t, dx1, dparts, tile, first_tile, n_tiles, name, carry=None):
    offsets = [sum(PROJ_PARTS[:p]) for p in range(len(PROJ_PARTS))]

    def body(x_ref, g_ref, w_ref, dx1_ref, *refs):
        part_refs, (gx_ref, dg_ref) = refs[:len(PROJ_PARTS)], refs[len(PROJ_PARTS):]
        i = pl.program_id(0)
        xv = x_ref[...]
        r = _rms_scale(xv)
        g = g_ref[...]
        dh = sum(_dot(p_ref[...], w_ref[off:off + wd, :]) for p_ref, off, wd in zip(part_refs, offsets, PROJ_PARTS))
        dxn, dg = _rms_bwd(dh, xv, r, g)
        gx_ref[...] = dx1_ref[...] + dxn

        @pl.when(i == 0)
        def _():
            dg_ref[...] = dg

        @pl.when(i > 0)
        def _():
            dg_ref[...] += dg

    tok = lambda wd: pl.BlockSpec((tile, wd), lambda i: (i + first_tile, 0))
    return _hosted_call(
        body, carry, _edge_1d(n_tiles), name=name, grid=(n_tiles,),
        in_specs=[tok(D_MODEL), _const_spec((1, D_MODEL)), _const_spec((IN_W, D_MODEL)), tok(D_MODEL)] + [tok(wd) for wd in PROJ_PARTS],
        out_specs=[pl.BlockSpec((tile, D_MODEL), lambda i: (i, 0)), pl.BlockSpec((1, D_MODEL), lambda i: (0, 0))],
        out_shape=[jax.ShapeDtypeStruct((n_tiles * tile, D_MODEL), F32), jax.ShapeDtypeStruct((1, D_MODEL), F32)],
        scratch_shapes=[], compiler_params=_params(("arbitrary",), VMEM_MID), inputs=(x, g1, w_in_t, dx1, *dparts))


def _bucket_table():
    qi = np.arange(BLOCK)[:, None]
    kj = np.arange(2 * BLOCK)[None, :]
    dist = qi + BLOCK - kj
    max_exact = N_BUCKETS // 2
    d = np.maximum(dist, 0)
    df = np.maximum(d, 1).astype(np.float32)
    large = max_exact + (np.log(df / np.float32(max_exact)) / np.float32(math.log(BLOCK / max_exact))
                         * np.float32(N_BUCKETS - max_exact)).astype(np.int32)
    large = np.minimum(large, N_BUCKETS - 1)
    bucket = np.where(d < max_exact, d, large)
    return np.where((dist >= 0) & (dist < BLOCK), bucket, -1).astype(np.int32)


def _build_bias(bucket_ref, rb_ref, bias_ref):
    bk = bucket_ref[...]
    for h in range(N_HEADS):
        def add(b, acc, h=h):
            return acc + jnp.where(bk == b, rb_ref[h, b], 0.0)
        bias_ref[h] = lax.fori_loop(0, N_BUCKETS, add, jnp.zeros((BLOCK, 2 * BLOCK), F32))


def _kv_variants(prev_ref, cur_ref):
    cat = jnp.concatenate([prev_ref[...], cur_ref[...]], axis=0)
    lo = lax.broadcasted_iota(jnp.int32, cat.shape, 1) < HEAD_DIM
    zero = jnp.zeros_like(cat)
    head0_lo = jnp.where(lo, cat, zero)
    head1_hi = jnp.where(lo, zero, cat)
    return ((head0_lo, pltpu.roll(head0_lo, HEAD_DIM, 1)), (pltpu.roll(head1_hi, HEAD_DIM, 1), head1_hi))


def _merge_kv_grads(g):
    lo = lax.broadcasted_iota(jnp.int32, g[0][0].shape, 1) < HEAD_DIM
    return jnp.where(lo, g[0][0] + pltpu.roll(g[0][1], HEAD_DIM, 1), g[1][1] + pltpu.roll(g[1][0], HEAD_DIM, 1))


def _head_lanes(h):
    return slice((h // 2) * LANES, (h // 2 + 1) * LANES)


def _attn_probs(q_ref, kvar, bias_ref, sk_ref, valid, s_ref):
    for h in range(N_HEADS):
        s_ref[h] = _dot_nt(q_ref[:, _head_lanes(h)], kvar[h // 4][h % 2])
    head = lax.broadcasted_iota(jnp.int32, (N_HEADS, 1, 1), 0)
    sink = jnp.zeros((N_HEADS, 1, 1), F32)
    for h in range(N_HEADS):
        sink = jnp.where(head == h, sk_ref[0, h], sink)
    s = jnp.where(valid[None], s_ref[...] * (HEAD_DIM ** -0.5) + bias_ref[...], NEG_INF)
    m = jnp.maximum(jnp.max(s, axis=-1, keepdims=True), sink)
    p = jnp.exp(s - m)
    e_sink = jnp.exp(sink - m)
    inv = 1.0 / (jnp.sum(p, axis=-1, keepdims=True) + e_sink)
    return p * inv, e_sink * inv


def _attn_valid(bucket_ref, n):
    col = lax.broadcasted_iota(jnp.int32, (BLOCK, 2 * BLOCK), 1)
    return (bucket_ref[...] >= 0) & ((n > 0) | (col >= BLOCK))


def _attn_fwd(q, k, v, bucket, rel_bias, sinks, carry=None):
    T = q.shape[0]
    nb = T // BLOCK

    def body(q_ref, kc_ref, kp_ref, vc_ref, vp_ref, bucket_ref, rb_ref, sk_ref, o_ref, bias_ref, s_ref, p_ref):
        n = pl.program_id(0)

        @pl.when(n == 0)
        def _():
            _build_bias(bucket_ref, rb_ref, bias_ref)

        kvar = _kv_variants(kp_ref, kc_ref)
        vvar = _kv_variants(vp_ref, vc_ref)
        pr, _ = _attn_probs(q_ref, kvar, bias_ref, sk_ref, _attn_valid(bucket_ref, n), s_ref)
        p_ref[...] = pr.astype(BF16)
        for m in range(N_HEADS // 2):
            acc = _dot(p_ref[2 * m], vvar[m // 2][0]) + _dot(p_ref[2 * m + 1], vvar[m // 2][1])
            o_ref[:, m * LANES:(m + 1) * LANES] = acc.astype(o_ref.dtype)

    cur = lambda w: pl.BlockSpec((BLOCK, w), lambda n: (n, 0))
    prev = lambda w: pl.BlockSpec((BLOCK, w), lambda n: (jnp.maximum(n - 1, 0), 0))
    smem = pl.BlockSpec(memory_space=pltpu.SMEM)
    return _hosted_call(
        body, carry, _edge_1d(nb, pass_on_last=True), name="attn_fwd", grid=(nb,),
        in_specs=[cur(ATTN_W), cur(KV_W), prev(KV_W), cur(KV_W), prev(KV_W), _const_spec((BLOCK, 2 * BLOCK)), smem, smem],
        out_specs=[cur(ATTN_W)],
        out_shape=[jax.ShapeDtypeStruct((T, ATTN_W), BF16)],
        scratch_shapes=[pltpu.VMEM((N_HEADS, BLOCK, 2 * BLOCK), F32), pltpu.VMEM((N_HEADS, BLOCK, 2 * BLOCK), F32),
                        pltpu.VMEM((N_HEADS, BLOCK, 2 * BLOCK), BF16)],
        compiler_params=_params(("arbitrary",)), inputs=(q, k, k, v, v, bucket, rel_bias, sinks))


ATTN_SMALL_ROWS = N_BUCKETS + SUBLANES


def _attn_bwd(q, k, v, datt, bucket, rel_bias, sinks, carry=None):
    T = q.shape[0]
    nb = T // BLOCK

    def body(q_ref, do_ref, kc_ref, kp_ref, vc_ref, vp_ref, bucket_ref, rb_ref, sk_ref,
             dq_ref, dkv_ref, small_ref, bias_ref, ds_sum_ref, dsink_ref, kcarry_ref, vcarry_ref,
             s_ref, dp_ref, p_ref, dsc_ref):
        n = pl.program_id(0)

        @pl.when(n == 0)
        def _():
            _build_bias(bucket_ref, rb_ref, bias_ref)
            ds_sum_ref[...] = jnp.zeros_like(ds_sum_ref)
            dsink_ref[...] = jnp.zeros_like(dsink_ref)
            kcarry_ref[...] = jnp.zeros_like(kcarry_ref)
            vcarry_ref[...] = jnp.zeros_like(vcarry_ref)

        @pl.when(n < nb)
        def _():
            kvar = _kv_variants(kp_ref, kc_ref)
            vvar = _kv_variants(vp_ref, vc_ref)
            pr, p_sink = _attn_probs(q_ref, kvar, bias_ref, sk_ref, _attn_valid(bucket_ref, n), s_ref)
            for h in range(N_HEADS):
                dp_ref[h] = _dot_nt(do_ref[:, _head_lanes(h)], vvar[h // 4][h % 2])
            dp = dp_ref[...]
            dsum = jnp.sum(pr * dp, axis=-1, keepdims=True)
            ds = pr * (dp - dsum)
            ds_sum_ref[...] += ds
            dsink_ref[...] -= jnp.sum(p_sink * dsum, axis=1, keepdims=True)
            dsc_ref[...] = (ds * (HEAD_DIM ** -0.5)).astype(BF16)
            p_ref[...] = pr.astype(BF16)
            for m in range(N_HEADS // 2):
                dqm = _dot(dsc_ref[2 * m], kvar[m // 2][0]) + _dot(dsc_ref[2 * m + 1], kvar[m // 2][1])
                dq_ref[:, m * LANES:(m + 1) * LANES] = dqm.astype(dq_ref.dtype)
            dk_var = [[None, None], [None, None]]
            dv_var = [[None, None], [None, None]]
            for kvh in range(2):
                for e in range(2):
                    heads = [h for h in range(N_HEADS) if h // 4 == kvh and h % 2 == e]
                    dk_var[kvh][e] = sum(_dot_tn(dsc_ref[h], q_ref[:, _head_lanes(h)]) for h in heads)
                    dv_var[kvh][e] = sum(_dot_tn(p_ref[h], do_ref[:, _head_lanes(h)]) for h in heads)
            dk_cat = _merge_kv_grads(dk_var)
            dv_cat = _merge_kv_grads(dv_var)

            @pl.when(n > 0)
            def _():
                dkv_ref[:, :KV_W] = (kcarry_ref[...] + dk_cat[:BLOCK]).astype(BF16)
                dkv_ref[:, KV_W:] = (vcarry_ref[...] + dv_cat[:BLOCK]).astype(BF16)

            kcarry_ref[...] = dk_cat[BLOCK:]
            vcarry_ref[...] = dv_cat[BLOCK:]

        @pl.when(n == nb)
        def _():
            dkv_ref[:, :KV_W] = kcarry_ref[...].astype(BF16)
            dkv_ref[:, KV_W:] = vcarry_ref[...].astype(BF16)
            bk = bucket_ref[...]
            row = lax.broadcasted_iota(jnp.int32, (N_HEADS, ATTN_SMALL_ROWS, LANES), 1)

            def add(b, acc):
                masked = jnp.where((bk == b)[None], ds_sum_ref[...], 0.0)
                val = jnp.sum(jnp.sum(masked, axis=1, keepdims=True), axis=2, keepdims=True)
                return acc + jnp.where(row == b, val, 0.0)

            small_ref[...] = lax.fori_loop(0, N_BUCKETS, add, jnp.where(row == N_BUCKETS, dsink_ref[...], 0.0))

    last = nb - 1
    cur = lambda w: pl.BlockSpec((BLOCK, w), lambda n: (jnp.minimum(n, last), 0))
    prev = lambda w: pl.BlockSpec((BLOCK, w), lambda n: (jnp.clip(n - 1, 0, last), 0))
    smem = pl.BlockSpec(memory_space=pltpu.SMEM)
    return _hosted_call(
        body, carry, _edge_1d(nb + 1), name="attn_bwd", grid=(nb + 1,),
        in_specs=[cur(ATTN_W), cur(ATTN_W), cur(KV_W), prev(KV_W), cur(KV_W), prev(KV_W),
                  _const_spec((BLOCK, 2 * BLOCK)), smem, smem],
        out_specs=[cur(ATTN_W), prev(2 * KV_W), pl.BlockSpec((N_HEADS, ATTN_SMALL_ROWS, LANES), lambda n: (0, 0, 0))],
        out_shape=[jax.ShapeDtypeStruct((T, ATTN_W), BF16), jax.ShapeDtypeStruct((T, 2 * KV_W), BF16),
                   jax.ShapeDtypeStruct((N_HEADS, ATTN_SMALL_ROWS, LANES), F32)],
        scratch_shapes=[pltpu.VMEM((N_HEADS, BLOCK, 2 * BLOCK), F32), pltpu.VMEM((N_HEADS, BLOCK, 2 * BLOCK), F32),
                        pltpu.VMEM((N_HEADS, 1, 1), F32), pltpu.VMEM((BLOCK, KV_W), F32), pltpu.VMEM((BLOCK, KV_W), F32),
                        pltpu.VMEM((N_HEADS, BLOCK, 2 * BLOCK), F32), pltpu.VMEM((N_HEADS, BLOCK, 2 * BLOCK), F32),
                        pltpu.VMEM((N_HEADS, BLOCK, 2 * BLOCK), BF16), pltpu.VMEM((N_HEADS, BLOCK, 2 * BLOCK), BF16)],
        compiler_params=_params(("arbitrary",)), inputs=(q, datt, k, k, v, v, bucket, rel_bias, sinks))


SCAN_UNROLL = 4


def _cmul(ar, ai, br, bi):
    return ar * br - ai * bi, ar * bi + ai * br


def _cmul_conj(ar, ai, br, bi):
    return ar * br + ai * bi, ar * bi - ai * br


def _ssm_discretize(lr, li, ldt):
    dt = jnp.exp(ldt)
    mag = jnp.exp(lr * dt)
    ab_re = mag * jnp.cos(li * dt)
    ab_im = mag * jnp.sin(li * dt)
    nr = ab_re - 1.0
    den = lr * lr + li * li
    f_re = (nr * lr + ab_im * li) / den
    f_im = (ab_im * lr - nr * li) / den
    return ab_re, ab_im, f_re, f_im


def _ssm_prep(lam_re, lam_im, ldt_rep, bd_re, bd_im):
    def body(lr_ref, li_ref, ldt_ref, bdr_ref, bdi_ref, ar_ref, ai_ref, br_ref, bi_ref):
        ab_re, ab_im, f_re, f_im = _ssm_discretize(lr_ref[...], li_ref[...], ldt_ref[...])
        ar_ref[...] = ab_re
        ai_ref[...] = ab_im
        bdr, bdi = bdr_ref[0], bdi_ref[0]
        br_ref[0] = (bdr * f_re - bdi * f_im).astype(BF16)
        bi_ref[0] = (bdi * f_re + bdr * f_im).astype(BF16)

    row = pl.BlockSpec((1, SSM_LANE_BLOCK), lambda j: (0, j))
    mat = pl.BlockSpec((1, LANES, SSM_LANE_BLOCK), lambda j: (j, 0, 0))
    return pl.pallas_call(
        body, name="ssm_prep", grid=(N_SSM_BLOCKS,),
        in_specs=[row, row, row, mat, mat], out_specs=[row, row, mat, mat],
        out_shape=[jax.ShapeDtypeStruct((1, STATES), F32)] * 2 + [jax.ShapeDtypeStruct((N_SSM_BLOCKS, LANES, SSM_LANE_BLOCK), BF16)] * 2,
        compiler_params=_params(("arbitrary",)),
    )(*_in_hbm(lam_re, lam_im, ldt_rep, bd_re, bd_im))


def _ssm_prep_bwd(lam_re, lam_im, ldt_rep, bd_re, bd_im, dbr, dbi, da_re, da_im):
    def body(lr_ref, li_ref, ldt_ref, bdr_ref, bdi_ref, dbr_ref, dbi_ref, dar_ref, dai_ref,
             dbdr_ref, dbdi_ref, dlr_ref, dli_ref, dldt_ref):
        lr, li, ldt = lr_ref[...], li_ref[...], ldt_ref[...]
        (_, _, f_re, f_im), vjp = jax.vjp(_ssm_discretize, lr, li, ldt)
        bdr, bdi, gbr, gbi = bdr_ref[0], bdi_ref[0], dbr_ref[0], dbi_ref[0]
        dbdr_ref[0] = gbr * f_re + gbi * f_im
        dbdi_ref[0] = gbi * f_re - gbr * f_im
        df_re = jnp.sum(gbr * bdr + gbi * bdi, axis=0, keepdims=True)
        df_im = jnp.sum(gbi * bdr - gbr * bdi, axis=0, keepdims=True)
        dlr, dli, dldt = vjp((dar_ref[...], dai_ref[...], df_re, df_im))
        dlr_ref[...] = dlr
        dli_ref[...] = dli
        dldt_ref[...] = dldt

    row = pl.BlockSpec((1, SSM_LANE_BLOCK), lambda j: (0, j))
    mat = pl.BlockSpec((1, LANES, SSM_LANE_BLOCK), lambda j: (j, 0, 0))
    mat_shape = jax.ShapeDtypeStruct((N_SSM_BLOCKS, LANES, SSM_LANE_BLOCK), F32)
    row_shape = jax.ShapeDtypeStruct((1, STATES), F32)
    return pl.pallas_call(
        body, name="ssm_prep_bwd", grid=(N_SSM_BLOCKS,),
        in_specs=[row, row, row, mat, mat, mat, mat, row, row], out_specs=[mat, mat, row, row, row],
        out_shape=[mat_shape, mat_shape, row_shape, row_shape, row_shape],
        compiler_params=_params(("arbitrary",)),
    )(*_in_hbm(lam_re, lam_im, ldt_rep, bd_re, bd_im, dbr, dbi, da_re, da_im))


def _group_sum(x):
    def body(x_ref, o_ref):
        o_ref[...] = jnp.sum(x_ref[...], axis=1, keepdims=True)
    return pl.pallas_call(body, name="ssm_group_sum", grid=(1,), in_specs=[_whole(x.shape)], out_specs=_whole((N_GROUPS, 1)),
                          out_shape=jax.ShapeDtypeStruct((N_GROUPS, 1), F32))(*_in_hbm(x))


def _power_table(ar, ai, p_re_ref, p_im_ref, steps):
    shape = (SUBLANES, SSM_LANE_BLOCK)
    p_re_ref[0:SUBLANES] = jnp.broadcast_to(ar, shape)
    p_im_ref[0:SUBLANES] = jnp.broadcast_to(ai, shape)
    m = 1
    while m < steps:
        rows = m * SUBLANES
        top_re = p_re_ref[rows - SUBLANES:rows]
        top_im = p_im_ref[rows - SUBLANES:rows]
        cur_re = p_re_ref[0:rows].reshape(m, SUBLANES, SSM_LANE_BLOCK)
        cur_im = p_im_ref[0:rows].reshape(m, SUBLANES, SSM_LANE_BLOCK)
        nxt_re, nxt_im = _cmul(cur_re, cur_im, top_re[None], top_im[None])
        p_re_ref[rows:2 * rows] = nxt_re.reshape(rows, SSM_LANE_BLOCK)
        p_im_ref[rows:2 * rows] = nxt_im.reshape(rows, SSM_LANE_BLOCK)
        m *= 2


def _to_segments(src_ref, dst_ref, steps):
    for s in range(SUBLANES):
        dst_ref[pl.ds(s, steps, stride=SUBLANES), :] = src_ref[s * steps:(s + 1) * steps, :]


def _from_segments(src_ref, dst_ref, steps):
    for s in range(SUBLANES):
        dst_ref[s * steps:(s + 1) * steps, :] = src_ref[pl.ds(s, steps, stride=SUBLANES), :]


def _segment_carries(e_re, e_im, an_re, an_im, c_re, c_im, reverse):
    order = range(SUBLANES - 1, -1, -1) if reverse else range(SUBLANES)
    ins_re, ins_im = [None] * SUBLANES, [None] * SUBLANES
    for s in order:
        ins_re[s], ins_im[s] = c_re, c_im
        pr, pi = _cmul(an_re, an_im, c_re, c_im)
        c_re = e_re[s:s + 1] + pr
        c_im = e_im[s:s + 1] + pi
    return jnp.concatenate(ins_re, axis=0), jnp.concatenate(ins_im, axis=0), c_re, c_im


def _ssm_fwd(u, a_re, a_im, b_re, b_im, c_re, c_im, d_skip, chunk, carry=None):
    T = u.shape[0]
    nc = T // chunk
    steps = chunk // SUBLANES
    blk = SSM_LANE_BLOCK

    def body(u_ref, ar_ref, ai_ref, br_ref, bi_ref, cr_ref, ci_ref, dk_ref,
             y_ref, hr_ref, hi_ref, inr_ref, ini_ref, useg_ref, yseg_ref, pr_ref, pi_ref, carry_ref):
        c = pl.program_id(1)
        ar, ai = ar_ref[...], ai_ref[...]

        @pl.when(c == 0)
        def _():
            _power_table(ar, ai, pr_ref, pi_ref, steps)
            carry_ref[...] = jnp.zeros_like(carry_ref)

        _to_segments(u_ref, useg_ref, steps)
        ub = useg_ref[...].astype(BF16)
        hr_ref[...] = _dot(ub, br_ref[0])
        hi_ref[...] = _dot(ub, bi_ref[0])
        first = slice(0, SUBLANES)

        def scan(t4, prev):
            for j in range(SCAN_UNROLL):
                rows = pl.ds(pl.multiple_of((t4 * SCAN_UNROLL + j) * SUBLANES, SUBLANES), SUBLANES)
                pr, pi = _cmul(pr_ref[first, :], pi_ref[first, :], prev[0], prev[1])
                prev = (pr + hr_ref[rows, :], pi + hi_ref[rows, :])
                hr_ref[rows, :] = prev[0]
                hi_ref[rows, :] = prev[1]
            return prev

        zero = jnp.zeros((SUBLANES, blk), F32)
        lax.fori_loop(0, steps // SCAN_UNROLL, scan, (zero, zero))

        top = slice(chunk - SUBLANES, chunk)
        in_re, in_im, out_re, out_im = _segment_carries(
            hr_ref[top, :], hi_ref[top, :], pr_ref[top, :][0:1], pi_ref[top, :][0:1],
            carry_ref[0:1, :], carry_ref[1:2, :], reverse=False)
        carry_ref[0:1, :] = out_re
        carry_ref[1:2, :] = out_im
        inr_ref[...] = in_re
        ini_ref[...] = in_im

        def fix(t4, _):
            for j in range(SCAN_UNROLL):
                rows = pl.ds(pl.multiple_of((t4 * SCAN_UNROLL + j) * SUBLANES, SUBLANES), SUBLANES)
                fr, fi = _cmul(pr_ref[rows, :], pi_ref[rows, :], in_re, in_im)
                hr_ref[rows, :] += fr
                hi_ref[rows, :] += fi
            return 0

        lax.fori_loop(0, steps // SCAN_UNROLL, fix, 0)

        yseg_ref[...] = _dot(hr_ref[...].astype(BF16), cr_ref[0]) - _dot(hi_ref[...].astype(BF16), ci_ref[0])
        _from_segments(yseg_ref, y_ref, steps)
        y_ref[...] += dk_ref[...] * u_ref[...]

    row = pl.BlockSpec((1, blk), lambda j, c: (0, j))
    b_mat = pl.BlockSpec((1, LANES, blk), lambda j, c: (j, 0, 0))
    c_mat = pl.BlockSpec((1, blk, LANES), lambda j, c: (j, 0, 0))
    tok = pl.BlockSpec((chunk, LANES), lambda j, c: (c, j))
    state = pl.BlockSpec((chunk, blk), lambda j, c: (c, j))
    enter = pl.BlockSpec((SUBLANES, blk), lambda j, c: (c, j))
    return _hosted_call(
        body, carry, _edge_2d(N_SSM_BLOCKS, nc), name="ssm_fwd", grid=(N_SSM_BLOCKS, nc),
        in_specs=[tok, row, row, b_mat, b_mat, c_mat, c_mat, pl.BlockSpec((1, LANES), lambda j, c: (0, j))],
        out_specs=[tok, state, state, enter, enter],
        out_shape=[jax.ShapeDtypeStruct((T, SSM_W), F32), jax.ShapeDtypeStruct((T, STATES), F32),
                   jax.ShapeDtypeStruct((T, STATES), F32), jax.ShapeDtypeStruct((nc * SUBLANES, STATES), F32),
                   jax.ShapeDtypeStruct((nc * SUBLANES, STATES), F32)],
        scratch_shapes=[pltpu.VMEM((chunk, LANES), F32), pltpu.VMEM((chunk, LANES), F32),
                        pltpu.VMEM((chunk, blk), F32), pltpu.VMEM((chunk, blk), F32), pltpu.VMEM((SUBLANES, blk), F32)],
        compiler_params=_params(("arbitrary", "arbitrary"), VMEM_MID),
        inputs=(u, a_re, a_im, b_re, b_im, c_re, c_im, d_skip))


def _ssm_bwd(dy, u, h_re, h_im, in_re, in_im, a_re, a_im, b_re, b_im, c_re, c_im, d_skip, chunk, carry=None):
    T = u.shape[0]
    nc = T // chunk
    steps = chunk // SUBLANES
    blk = SSM_LANE_BLOCK

    def body(dy_ref, u_ref, hr_ref, hi_ref, inr_ref, ini_ref, ar_ref, ai_ref, br_ref, bi_ref, cr_ref, ci_ref, dk_ref,
             du_ref, dbr_ref, dbi_ref, dcr_ref, dci_ref, dar_ref, dai_ref, ddk_ref,
             dyseg_ref, useg_ref, duseg_ref, gr_ref, gi_ref, pr_ref, pi_ref, carry_ref, accr_ref, acci_ref):
        c = pl.program_id(1)
        ar, ai = ar_ref[...], ai_ref[...]

        @pl.when(c == 0)
        def _():
            _power_table(ar, ai, pr_ref, pi_ref, steps)
            carry_ref[...] = jnp.zeros_like(carry_ref)
            accr_ref[...] = jnp.zeros_like(accr_ref)
            acci_ref[...] = jnp.zeros_like(acci_ref)

        _to_segments(dy_ref, dyseg_ref, steps)
        _to_segments(u_ref, useg_ref, steps)
        dyb = dyseg_ref[...].astype(BF16)
        ub = useg_ref[...].astype(BF16)
        gr_ref[...] = _dot_nt(dyb, cr_ref[0])
        gi_ref[...] = -_dot_nt(dyb, ci_ref[0])
        dcr = _dot_tn(hr_ref[...].astype(BF16), dyb)
        dci = -_dot_tn(hi_ref[...].astype(BF16), dyb)
        ddk = jnp.sum(dy_ref[...] * u_ref[...], axis=0, keepdims=True)

        first = slice(0, SUBLANES)

        def scan(k4, nxt):
            for j in range(SCAN_UNROLL):
                t = steps - 1 - (k4 * SCAN_UNROLL + j)
                rows = pl.ds(pl.multiple_of(t * SUBLANES, SUBLANES), SUBLANES)
                pr, pi = _cmul_conj(pr_ref[first, :], pi_ref[first, :], nxt[0], nxt[1])
                nxt = (pr + gr_ref[rows, :], pi + gi_ref[rows, :])
                gr_ref[rows, :] = nxt[0]
                gi_ref[rows, :] = nxt[1]
            return nxt

        top = slice(chunk - SUBLANES, chunk)
        zero = jnp.zeros((SUBLANES, blk), F32)
        lax.fori_loop(0, steps // SCAN_UNROLL, scan, (zero, zero))

        gin_re, gin_im, out_re, out_im = _segment_carries(
            gr_ref[0:SUBLANES, :], gi_ref[0:SUBLANES, :], pr_ref[top, :][0:1], -pi_ref[top, :][0:1],
            carry_ref[0:1, :], carry_ref[1:2, :], reverse=True)
        carry_ref[0:1, :] = out_re
        carry_ref[1:2, :] = out_im

        def fix_row(rows, prow, hp_re, hp_im, acc):
            fr, fi = _cmul_conj(pr_ref[prow, :], pi_ref[prow, :], gin_re, gin_im)
            g_re = gr_ref[rows, :] + fr
            g_im = gi_ref[rows, :] + fi
            gr_ref[rows, :] = g_re
            gi_ref[rows, :] = g_im
            return acc[0] + g_re * hp_re + g_im * hp_im, acc[1] + g_im * hp_re - g_re * hp_im

        def fix_at(t, acc):
            aligned = (lambda r: r * SUBLANES) if isinstance(t, int) else (lambda r: pl.multiple_of(r * SUBLANES, SUBLANES))
            rows, before, prow = (pl.ds(aligned(r), SUBLANES) for r in (t, t - 1, steps - 1 - t))
            return fix_row(rows, prow, hr_ref[before, :], hi_ref[before, :], acc)

        def fix(t4, acc):
            for j in range(SCAN_UNROLL):
                acc = fix_at(t4 * SCAN_UNROLL + j, acc)
            return acc

        acc = fix_row(first, top, inr_ref[...], ini_ref[...], (accr_ref[...], acci_ref[...]))
        for t in range(1, SCAN_UNROLL):
            acc = fix_at(t, acc)
        acc_re, acc_im = lax.fori_loop(1, steps // SCAN_UNROLL, fix, acc)
        accr_ref[...] = acc_re
        acci_ref[...] = acc_im

        gbr = gr_ref[...].astype(BF16)
        gbi = gi_ref[...].astype(BF16)
        duseg_ref[...] = _dot_nt(gbr, br_ref[0]) + _dot_nt(gbi, bi_ref[0])
        _from_segments(duseg_ref, dyseg_ref, steps)
        du_ref[...] = (dyseg_ref[...] + dk_ref[...] * dy_ref[...]).astype(BF16)
        dbr = _dot_tn(ub, gbr)
        dbi = _dot_tn(ub, gbi)

        @pl.when(c == 0)
        def _():
            dbr_ref[0] = dbr
            dbi_ref[0] = dbi
            dcr_ref[0] = dcr
            dci_ref[0] = dci
            ddk_ref[...] = ddk

        @pl.when(c > 0)
        def _():
            dbr_ref[0] += dbr
            dbi_ref[0] += dbi
            dcr_ref[0] += dcr
            dci_ref[0] += dci
            ddk_ref[...] += ddk

        @pl.when(c == nc - 1)
        def _():
            dar_ref[...] = jnp.sum(acc_re, axis=0, keepdims=True)
            dai_ref[...] = jnp.sum(acc_im, axis=0, keepdims=True)

    rev = lambda c: nc - 1 - c
    row = pl.BlockSpec((1, blk), lambda j, c: (0, j))
    b_mat = pl.BlockSpec((1, LANES, blk), lambda j, c: (j, 0, 0))
    c_mat = pl.BlockSpec((1, blk, LANES), lambda j, c: (j, 0, 0))
    tok = pl.BlockSpec((chunk, LANES), lambda j, c: (rev(c), j))
    state = pl.BlockSpec((chunk, blk), lambda j, c: (rev(c), j))
    enter = pl.BlockSpec((SUBLANES, blk), lambda j, c: (rev(c), j))
    chan = pl.BlockSpec((1, LANES), lambda j, c: (0, j))
    f32 = lambda *s: jax.ShapeDtypeStruct(s, F32)
    return _hosted_call(
        body, carry, _edge_2d(N_SSM_BLOCKS, nc), name="ssm_bwd", grid=(N_SSM_BLOCKS, nc),
        in_specs=[tok, tok, state, state, enter, enter, row, row, b_mat, b_mat, c_mat, c_mat, chan],
        out_specs=[tok, b_mat, b_mat, c_mat, c_mat, row, row, chan],
        out_shape=[jax.ShapeDtypeStruct((T, SSM_W), BF16), f32(N_SSM_BLOCKS, LANES, blk), f32(N_SSM_BLOCKS, LANES, blk),
                   f32(N_SSM_BLOCKS, blk, LANES), f32(N_SSM_BLOCKS, blk, LANES), f32(1, STATES), f32(1, STATES), f32(1, SSM_W)],
        scratch_shapes=[pltpu.VMEM((chunk, LANES), F32), pltpu.VMEM((chunk, LANES), F32), pltpu.VMEM((chunk, LANES), F32),
                        pltpu.VMEM((chunk, blk), F32), pltpu.VMEM((chunk, blk), F32),
                        pltpu.VMEM((chunk, blk), F32), pltpu.VMEM((chunk, blk), F32),
                        pltpu.VMEM((SUBLANES, blk), F32), pltpu.VMEM((SUBLANES, blk), F32), pltpu.VMEM((SUBLANES, blk), F32)],
        compiler_params=_params(("arbitrary", "arbitrary"), VMEM_BIG),
        inputs=(dy, u, h_re, h_im, in_re, in_im, a_re, a_im, b_re, b_im, c_re, c_im, d_skip))


def _merge_forward(y, att, ga, gs, w_glu, w_ssm, w_attn):
    z = jax.nn.gelu(y)
    zb = z.astype(BF16)
    gl = jax.nn.sigmoid(_dot(zb, w_glu))
    z2b = (z * gl).astype(BF16)
    y_ssm = _dot(z2b, w_ssm)
    y_attn = _dot(att, w_attn)
    sa = jax.nn.sigmoid(ga)
    ss = jax.nn.sigmoid(gs)
    merged = (sa * y_attn + ss * y_ssm).astype(BF16)
    return z, zb, gl, z2b, y_ssm, y_attn, sa, ss, merged


def _merge_fwd(x, y, att, ga, gs, g2, g3, w_glu, w_ssm, w_attn, w_out, tile):
    T = x.shape[0]

    def body(x_ref, y_ref, att_ref, ga_ref, gs_ref, g2_ref, g3_ref, wg_ref, ws_ref, wa_ref, wo_ref, x1_ref, o_ref, h2_ref):
        merged = _merge_forward(y_ref[...], att_ref[...], ga_ref[...], gs_ref[...], wg_ref[...], ws_ref[...], wa_ref[...])[-1]
        o = _dot(merged, wo_ref[...])
        x1 = x_ref[...] + o * _rms_scale(o) * g2_ref[...]
        o_ref[...] = o
        x1_ref[...] = x1
        h2_ref[...] = (x1 * _rms_scale(x1) * g3_ref[...]).astype(BF16)

    tok = lambda w: pl.BlockSpec((tile, w), lambda i: (i, 0))
    vec = _const_spec((1, D_MODEL))
    return pl.pallas_call(
        body, name="merge_fwd", grid=(T // tile,),
        in_specs=[tok(D_MODEL), tok(SSM_W), tok(ATTN_W), tok(D_MODEL), tok(D_MODEL), vec, vec,
                  _const_spec((SSM_W, SSM_W)), _const_spec((SSM_W, D_MODEL)), _const_spec((ATTN_W, D_MODEL)),
                  _const_spec((D_MODEL, D_MODEL))],
        out_specs=[tok(D_MODEL), tok(D_MODEL), tok(D_MODEL)],
        out_shape=_hbm_out([jax.ShapeDtypeStruct((T, D_MODEL), F32), jax.ShapeDtypeStruct((T, D_MODEL), F32),
                            jax.ShapeDtypeStruct((T, D_MODEL), BF16)]),
        compiler_params=_params(("arbitrary",), VMEM_MID),
    )(*_in_hbm(x, y, att, ga, gs, g2, g3, w_glu, w_ssm, w_attn, w_out))


def _merge_bwd(dh2, dx2, x1, o, y, att, ga, gs, g2, g3, w_glu, w_ssm, w_attn, w_out, tile, carry=None):
    T = x1.shape[0]
    n_steps = T // tile

    group = min(2, n_steps)
    staged_widths = (D_MODEL, D_MODEL, ATTN_W, D_MODEL, SSM_W, D_MODEL, SSM_W, SSM_W)

    def body(dh2_ref, dx2_ref, x1_ref, o_ref, y_ref, att_ref, ga_ref, gs_ref, g2_ref, g3_ref, wg_ref, ws_ref, wa_ref, wo_ref,
             dx1_ref, dgates_ref, datt_ref, dy_ref, dwg_hbm, dws_hbm, dwa_hbm, dwo_hbm, dg2_ref, dg3_ref,
             awg_ref, aws_ref, awa_ref, awo_ref, *staged):
        i = pl.program_id(0)
        x1v, ov = x1_ref[...], o_ref[...]
        dxn, dg3 = _rms_bwd(dh2_ref[...], x1v, _rms_scale(x1v), g3_ref[...])
        dx1 = dx2_ref[...] + dxn
        dx1_ref[...] = dx1
        do, dg2 = _rms_bwd(dx1, ov, _rms_scale(ov), g2_ref[...])
        dob = do.astype(BF16)

        yv = y_ref[...]
        att = att_ref[...]
        z, zb, gl, z2b, y_ssm, y_attn, sa, ss, merged = _merge_forward(
            yv, att, ga_ref[...], gs_ref[...], wg_ref[...], ws_ref[...], wa_ref[...])
        dmerged = _dot_nt(dob, wo_ref[...])
        dya = (dmerged * sa).astype(BF16)
        dys = (dmerged * ss).astype(BF16)
        dgates_ref[:, :D_MODEL] = (dmerged * y_attn * sa * (1.0 - sa)).astype(BF16)
        dgates_ref[:, D_MODEL:] = (dmerged * y_ssm * ss * (1.0 - ss)).astype(BF16)
        datt_ref[...] = _dot_nt(dya, wa_ref[...]).astype(BF16)
        dz2 = _dot_nt(dys, ws_ref[...])
        dpre = (dz2 * z * gl * (1.0 - gl)).astype(BF16)
        dz = dz2 * gl + _dot_nt(dpre, wg_ref[...])
        _, gelu_vjp = jax.vjp(jax.nn.gelu, yv)
        dy_ref[...] = gelu_vjp(dz)[0]

        part = pl.ds(pl.multiple_of((i % group) * tile, tile), tile)
        for ref, val in zip(staged, (merged, dob, att, dya, z2b, dys, zb, dpre)):
            ref[part, :] = val

        @pl.when(i == 0)
        def _():
            dg2_ref[...] = dg2
            dg3_ref[...] = dg3

        @pl.when(i > 0)
        def _():
            dg2_ref[...] += dg2
            dg3_ref[...] += dg3

        def weight_grads():
            s_merged, s_dob, s_att, s_dya, s_z2b, s_dys, s_zb, s_dpre = (ref[...] for ref in staged)
            return ((awo_ref, _dot_tn(s_merged, s_dob)), (awa_ref, _dot_tn(s_att, s_dya)),
                    (aws_ref, _dot_tn(s_z2b, s_dys)), (awg_ref, _dot_tn(s_zb, s_dpre)))

        @pl.when(i == group - 1)
        def _():
            for ref, val in weight_grads():
                ref[...] = val

        @pl.when((i % group == group - 1) & (i > group - 1))
        def _():
            for ref, val in weight_grads():
                ref[...] += val

        @pl.when(i == n_steps - 1)
        def _():
            pltpu.sync_copy(awg_ref, dwg_hbm)
            pltpu.sync_copy(aws_ref, dws_hbm)
            pltpu.sync_copy(awa_ref, dwa_hbm)
            pltpu.sync_copy(awo_ref, dwo_hbm)

    tok = lambda w: pl.BlockSpec((tile, w), lambda i: (i, 0))
    vec = _const_spec((1, D_MODEL))
    any_ = pl.BlockSpec(memory_space=pl.ANY)
    vec_out = pl.BlockSpec((1, D_MODEL), lambda i: (0, 0))
    f32 = lambda *s: jax.ShapeDtypeStruct(s, F32)
    bf = lambda *s: jax.ShapeDtypeStruct(s, BF16)
    return _hosted_call(
        body, carry, _edge_1d(n_steps), name="merge_bwd", grid=(n_steps,),
        in_specs=[tok(D_MODEL), tok(D_MODEL), tok(D_MODEL), tok(D_MODEL), tok(SSM_W), tok(ATTN_W), tok(D_MODEL), tok(D_MODEL),
                  vec, vec, _const_spec((SSM_W, SSM_W)), _const_spec((SSM_W, D_MODEL)), _const_spec((ATTN_W, D_MODEL)),
                  _const_spec((D_MODEL, D_MODEL))],
        out_specs=[tok(D_MODEL), tok(2 * D_MODEL), tok(ATTN_W), tok(SSM_W), any_, any_, any_, any_, vec_out, vec_out],
        out_shape=[f32(T, D_MODEL), bf(T, 2 * D_MODEL), bf(T, ATTN_W), f32(T, SSM_W),
                   f32(SSM_W, SSM_W), f32(SSM_W, D_MODEL), f32(ATTN_W, D_MODEL), f32(D_MODEL, D_MODEL),
                   f32(1, D_MODEL), f32(1, D_MODEL)],
        scratch_shapes=[pltpu.VMEM((SSM_W, SSM_W), F32), pltpu.VMEM((SSM_W, D_MODEL), F32),
                        pltpu.VMEM((ATTN_W, D_MODEL), F32), pltpu.VMEM((D_MODEL, D_MODEL), F32)]
        + [pltpu.VMEM((group * tile, wd), BF16) for wd in staged_widths],
        compiler_params=_params(("arbitrary",), VMEM_BIG),
        inputs=(dh2, dx2, x1, o, y, att, ga, gs, g2, g3, w_glu, w_ssm, w_attn, w_out))


FF_SHARD = D_FF // N_DEV


def _mlp_fwd(h2, x1, target, g4, w_ff_in, w_ff_out, tile):
    T = h2.shape[0]
    col_chunk = 2 * FF_SHARD

    def body(h2_ref, x1_ref, tg_ref, g4_ref, wi_ref, wo_ref, a_ref, dfo_ref, dx2_ref, loss_ref, dg4_ref, rr_ref):
        i = pl.program_id(0)
        h2v = h2_ref[...]
        for c in range(D_FF // col_chunk):
            cols = slice(c * col_chunk, (c + 1) * col_chunk)
            a = _dot_nt(h2v, wi_ref[cols, :])
            a_ref[:, cols] = a.astype(BF16)
            ra = jnp.maximum(a, 0.0)
            rr_ref[:, cols] = (ra * ra).astype(BF16)
        f = _dot(rr_ref[...], wo_ref[...])
        r = _rms_scale(f)
        g = g4_ref[...]
        err = x1_ref[...] + f * r * g - tg_ref[...]
        dx2 = err * (1.0 / D_MODEL)
        dx2_ref[...] = dx2
        dfo, dg = _rms_bwd(dx2, f, r, g)
        dfo_ref[...] = dfo.astype(BF16)
        row = lax.broadcasted_iota(jnp.int32, (SUBLANES, LANES), 0)
        col = lax.broadcasted_iota(jnp.int32, (SUBLANES, LANES), 1)
        loss = jnp.where((row == 0) & (col == 0), (0.5 / D_MODEL) * jnp.sum(err * err), 0.0)

        @pl.when(i == 0)
        def _():
            loss_ref[...] = loss
            dg4_ref[...] = dg

        @pl.when(i > 0)
        def _():
            loss_ref[...] += loss
            dg4_ref[...] += dg

    tok = pl.BlockSpec((tile, D_MODEL), lambda i: (i, 0))
    return pl.pallas_call(
        body, name="mlp_fwd", grid=(T // tile,),
        in_specs=[tok, tok, tok, _const_spec((1, D_MODEL)), _const_spec((D_FF, D_MODEL)), _const_spec((D_FF, D_MODEL))],
        out_specs=[pl.BlockSpec((tile, D_FF), lambda i: (i, 0)), tok, tok,
                   pl.BlockSpec((SUBLANES, LANES), lambda i: (0, 0)), pl.BlockSpec((1, D_MODEL), lambda i: (0, 0))],
        out_shape=_hbm_out([jax.ShapeDtypeStruct((T, D_FF), BF16), jax.ShapeDtypeStruct((T, D_MODEL), BF16),
                            jax.ShapeDtypeStruct((T, D_MODEL), F32), jax.ShapeDtypeStruct((SUBLANES, LANES), F32),
                            jax.ShapeDtypeStruct((1, D_MODEL), F32)]),
        scratch_shapes=[pltpu.VMEM((tile, D_FF), BF16)],
        compiler_params=_params(("arbitrary",), VMEM_MAX),
    )(*_in_hbm(h2, x1, target, g4, w_ff_in.reshape(D_FF, D_MODEL), w_ff_out.reshape(D_FF, D_MODEL)))


def _mlp_weight_grads(dfo, a, h2, w_ff_out, row_chunk):
    T = h2.shape[0]

    def body(dfo_ref, h2_ref, a_ref, wo_ref, dwi_ref, dwo_ref, da_ref, rr_ref):
        def rows(r, _):
            sl = pl.ds(pl.multiple_of(r * row_chunk, row_chunk), row_chunk)
            ra = jnp.maximum(a_ref[sl, :].astype(F32), 0.0)
            da_ref[sl, :] = (_dot_nt(dfo_ref[sl, :], wo_ref[0]) * (2.0 * ra)).astype(BF16)
            rr_ref[sl, :] = (ra * ra).astype(BF16)
            return 0

        lax.fori_loop(0, T // row_chunk, rows, 0)
        dwo_ref[0] = _dot_tn(rr_ref[...], dfo_ref[...])
        dwi_ref[0] = _dot_tn(h2_ref[...], da_ref[...])

    return pl.pallas_call(
        body, name="mlp_weight_grads", grid=(N_DEV,),
        in_specs=[_const_spec((T, D_MODEL)), _const_spec((T, D_MODEL)), pl.BlockSpec((T, FF_SHARD), lambda k: (0, k)),
                  pl.BlockSpec((1, FF_SHARD, D_MODEL), lambda k: (k, 0, 0))],
        out_specs=[pl.BlockSpec((1, D_MODEL, FF_SHARD), lambda k: (k, 0, 0)),
                   pl.BlockSpec((1, FF_SHARD, D_MODEL), lambda k: (k, 0, 0)), pl.BlockSpec((T, FF_SHARD), lambda k: (0, k))],
        out_shape=_hbm_out([jax.ShapeDtypeStruct((N_DEV, D_MODEL, FF_SHARD), F32),
                            jax.ShapeDtypeStruct((N_DEV, FF_SHARD, D_MODEL), F32), jax.ShapeDtypeStruct((T, D_FF), BF16)]),
        scratch_shapes=[pltpu.VMEM((T, FF_SHARD), BF16)],
        compiler_params=_params(("arbitrary",), VMEM_MAX),
    )(*_in_hbm(dfo, h2, a, w_ff_out))


def _mlp_input_grad(da, w_ff_in_t, tile):
    T = da.shape[0]

    def body(da_ref, w_ref, o_ref):
        o_ref[...] = _dot(da_ref[...], w_ref[...])

    return pl.pallas_call(
        body, name="mlp_input_grad", grid=(T // tile,),
        in_specs=[pl.BlockSpec((tile, D_FF), lambda i: (i, 0)), _const_spec((D_FF, D_MODEL))],
        out_specs=pl.BlockSpec((tile, D_MODEL), lambda i: (i, 0)),
        out_shape=_hbm_out(jax.ShapeDtypeStruct((T, D_MODEL), F32)),
        compiler_params=_params(("arbitrary",), VMEM_MID),
    )(*_in_hbm(da, w_ff_in_t))


def _block_diag_in(b):
    bt = b.reshape(N_SSM_BLOCKS, GROUPS_PER_BLOCK, GROUP_CH, N_STATE)
    eye = jnp.eye(GROUPS_PER_BLOCK, dtype=b.dtype)
    return jnp.einsum("jacp,ab->jacbp", bt, eye).reshape(N_SSM_BLOCKS, LANES, SSM_LANE_BLOCK)


def _block_diag_in_grad(g):
    g = g.reshape(N_SSM_BLOCKS, GROUPS_PER_BLOCK, GROUP_CH, GROUPS_PER_BLOCK, N_STATE)
    d = jnp.diagonal(g, axis1=1, axis2=3)
    return jnp.transpose(d, (0, 3, 1, 2)).reshape(N_GROUPS, GROUP_CH, N_STATE)


def _block_diag_out(c):
    ct = c.reshape(N_SSM_BLOCKS, GROUPS_PER_BLOCK, GROUP_CH, N_STATE)
    eye = jnp.eye(GROUPS_PER_BLOCK, dtype=c.dtype)
    return jnp.einsum("jacp,ab->japbc", ct, eye).reshape(N_SSM_BLOCKS, SSM_LANE_BLOCK, LANES)


def _block_diag_out_grad(g):
    g = g.reshape(N_SSM_BLOCKS, GROUPS_PER_BLOCK, N_STATE, GROUPS_PER_BLOCK, GROUP_CH)
    d = jnp.diagonal(g, axis1=1, axis2=3)
    return jnp.transpose(d, (0, 3, 2, 1)).reshape(N_GROUPS, GROUP_CH, N_STATE)


def _tiles(T):
    return dict(proj=min(512, T), proj_bwd=min(512, T // 2), merge=min(512, T), merge_bwd=min(256, T),
                mlp_fwd=min(512, T), mlp_bwd=min(512, T), ssm_chunk=min(1024, T))


def _mesh_position():
    x, y, c = lax.axis_index("x"), lax.axis_index("y"), lax.axis_index("c")
    other_chips = [(1 - x, y), (x, 1 - y), (1 - x, 1 - y)]
    return x, y, c, other_chips


def _gather_carry(arrays, pass_on=True):
    n = len(arrays)

    def copies(ins, outs, sems):
        send_sems, recv_sems, local_sems = sems
        x, y, c, chips = _mesh_position()
        me, sibling = (x, y, c), (x, y, 1 - c)

        def copy(a, k, block, to, src=None):
            px, py, pc = block
            dst = outs[a].at[4 * px + 2 * py + pc]
            return pltpu.make_async_remote_copy(
                src_ref=dst if src is None else src, dst_ref=dst, send_sem=send_sems.at[7 * a + k],
                recv_sem=recv_sems.at[7 * a + k], device_id=to, device_id_type=MESH_IDS)

        mine = [pltpu.make_async_copy(ins[a], outs[a].at[4 * x + 2 * y + c], local_sems.at[a]) for a in range(n)]
        first = []
        for a in range(n):
            first.append(copy(a, 0, me, sibling, src=ins[a]))
            first += [copy(a, 1 + j, me, (*chip, c), src=ins[a]) for j, chip in enumerate(chips)]
        return copy, mine, first, me, sibling, chips, c

    def start(ins, outs, sems):
        _, mine, first, *_ = copies(ins, outs, sems)
        for cp in mine + first:
            cp.start()

    def passed_on(copy, sibling, chips, c):
        return [copy(a, 4 + j, (*chip, c), sibling) for a in range(n) for j, chip in enumerate(chips)]

    def middle(ins, outs, sems):
        copy, _, _, me, sibling, chips, c = copies(ins, outs, sems)
        for a in range(n):
            for j, chip in enumerate(chips):
                copy(a, 1 + j, (*chip, c), me).wait_recv()
                copy(a, 4 + j, (*chip, c), sibling).start()

    def finish(ins, outs, sems):
        copy, mine, first, me, sibling, chips, c = copies(ins, outs, sems)
        for a in range(n):
            copy(a, 0, sibling, me).wait_recv()
            for j, chip in enumerate(chips):
                (copy(a, 4 + j, (*chip, 1 - c), me) if pass_on else copy(a, 1 + j, (*chip, c), me)).wait_recv()
        for cp in first + (passed_on(copy, sibling, chips, c) if pass_on else []):
            cp.wait_send()
        for cp in mine:
            cp.wait()

    return _Carry(arrays, [jax.ShapeDtypeStruct((N_DEV,) + a.shape, a.dtype) for a in arrays],
                  [pltpu.SemaphoreType.DMA((7 * n,)), pltpu.SemaphoreType.DMA((7 * n,)), pltpu.SemaphoreType.DMA((n,))],
                  start, finish, middle if pass_on else None)


def _gather_pass_on(gathered, name):
    n = len(gathered)

    def body(*refs):
        zones, send_sems, recv_sems = refs[:n], refs[2 * n], refs[2 * n + 1]
        x, y, c, chips = _mesh_position()
        copies = []
        for a in range(n):
            for j, (px, py) in enumerate(chips):
                block = zones[a].at[4 * px + 2 * py + c]
                copies.append(pltpu.make_async_remote_copy(
                    src_ref=block, dst_ref=block, send_sem=send_sems.at[3 * a + j], recv_sem=recv_sems.at[3 * a + j],
                    device_id=(x, y, 1 - c), device_id_type=MESH_IDS))
        for cp in copies:
            cp.start()
        for cp in copies:
            cp.wait()

    return pl.pallas_call(
        body, name=name, in_specs=[HBM_SPEC] * n, out_specs=[HBM_SPEC] * n,
        out_shape=_hbm_out([jax.ShapeDtypeStruct(g.shape, g.dtype) for g in gathered]),
        scratch_shapes=[pltpu.SemaphoreType.DMA((3 * n,)), pltpu.SemaphoreType.DMA((3 * n,))],
        input_output_aliases={a: a for a in range(n)})(*gathered)


def _pairwise_carry(arrays, n_slots, make_copies):
    n = len(arrays)

    def start(ins, outs, sems):
        for cp in make_copies(ins, outs, sems):
            cp.start()

    def finish(ins, outs, sems):
        for cp in make_copies(ins, outs, sems):
            cp.wait()

    return _Carry(arrays, [jax.ShapeDtypeStruct((n_slots,) + a.shape[1:], a.dtype) for a in arrays],
                  [pltpu.SemaphoreType.DMA((n_slots * n,)), pltpu.SemaphoreType.DMA((n_slots * n,))], start, finish)


def _sibling_carry(grads):
    def make_copies(ins, outs, sems):
        x, y, c, _ = _mesh_position()
        return [pltpu.make_async_remote_copy(
            src_ref=ins[a].at[2 * ch + (1 - c)], dst_ref=outs[a].at[ch], send_sem=sems[0].at[4 * a + ch],
            recv_sem=sems[1].at[4 * a + ch], device_id=(x, y, 1 - c), device_id_type=MESH_IDS)
            for a in range(len(grads)) for ch in range(4)]

    return _pairwise_carry(grads, 4, make_copies)


def _chips_carry(sums):
    def make_copies(ins, outs, sems):
        x, y, c, chips = _mesh_position()
        return [pltpu.make_async_remote_copy(
            src_ref=ins[a].at[2 * px + py], dst_ref=outs[a].at[j], send_sem=sems[0].at[3 * a + j],
            recv_sem=sems[1].at[3 * a + j], device_id=(px, py, c), device_id_type=MESH_IDS)
            for a in range(len(sums)) for j, (px, py) in enumerate(chips)]

    return _pairwise_carry(sums, 3, make_copies)


def _everyone_carry(arrays):
    def make_copies(ins, outs, sems):
        x, y, c, _ = _mesh_position()
        flip = lambda v, bit: 1 - v if bit else v
        return [pltpu.make_async_remote_copy(
            src_ref=ins[a], dst_ref=outs[a].at[r - 1], send_sem=sems[0].at[7 * a + r - 1], recv_sem=sems[1].at[7 * a + r - 1],
            device_id=(flip(x, r & 4), flip(y, r & 2), flip(c, r & 1)), device_id_type=MESH_IDS)
            for a in range(len(arrays)) for r in range(1, N_DEV)]

    carry = _pairwise_carry([jax.ShapeDtypeStruct((1,) + a.shape, a.dtype) for a in arrays], N_DEV - 1, make_copies)
    carry.inputs = list(arrays)
    return carry


def _sum_everyone(own, received, me, name, after=()):
    def body(me_ref, own_ref, r_ref, *refs):
        g = None
        for d in range(N_DEV):
            relation = jnp.bitwise_xor(d, me_ref[0])
            part = jnp.where(relation == 0, own_ref[...], r_ref[jnp.maximum(relation - 1, 0)])
            g = part if g is None else g + part
        refs[-1][...] = g

    whole = lambda shape: pl.BlockSpec(shape, lambda i, me_ref: (0,) * len(shape))
    return pl.pallas_call(
        body, name=name,
        grid_spec=pltpu.PrefetchScalarGridSpec(
            num_scalar_prefetch=1, grid=(1,), in_specs=[whole(own.shape), whole(received.shape)] + [HBM_SPEC] * len(after),
            out_specs=whole(own.shape)),
        out_shape=jax.ShapeDtypeStruct(own.shape, F32))(me, *_in_hbm(own, received), *after)


SEM_SPEC = pl.BlockSpec(memory_space=pltpu.SEMAPHORE)
DATAFLOW_EFFECT = pltpu.SideEffectType.DATAFLOW_SIDE_EFFECTING


def _exchange_start(carry, name, after=()):
    n, n_sems = len(carry.inputs), len(carry.sems)
    lands = [lax.empty(s.shape, s.dtype) for s in carry.out_shapes]

    def body(*refs):
        first_out = 2 * n + len(after)
        srcs, zones, sems, token = refs[:n], refs[n:2 * n], refs[first_out:first_out + n_sems], refs[-1]
        carry.start(srcs, zones, sems)
        token[...] = jnp.zeros_like(token)

    outs = pl.pallas_call(
        body, name=name, in_specs=[HBM_SPEC] * (2 * n + len(after)),
        out_specs=[SEM_SPEC] * n_sems + [HBM_SPEC] * (2 * n) + [pl.BlockSpec(memory_space=pltpu.VMEM)],
        out_shape=list(carry.sems) + _hbm_out([jax.ShapeDtypeStruct(a.shape, a.dtype) for a in carry.inputs])
        + _hbm_out(carry.out_shapes) + [jax.ShapeDtypeStruct((SUBLANES, LANES), F32)],
        input_output_aliases={j: n_sems + j for j in range(2 * n)},
        compiler_params=pltpu.CompilerParams(has_side_effects=DATAFLOW_EFFECT),
    )(*_in_hbm(*carry.inputs, *lands), *after)
    return outs[:-1], outs[-1]


def _exchange_wait(carry, in_flight, after, name):
    n, n_sems = len(carry.inputs), len(carry.sems)
    sems, srcs, zones = in_flight[:n_sems], in_flight[n_sems:n_sems + n], in_flight[n_sems + n:]

    def body(*refs):
        src_refs, zone_refs, sem_refs = refs[:n], refs[n:2 * n], refs[2 * n:2 * n + n_sems]
        carry.finish(src_refs, zone_refs, sem_refs)

    outs = pl.pallas_call(
        body, name=name, in_specs=[HBM_SPEC] * (2 * n) + [SEM_SPEC] * n_sems + [HBM_SPEC] * len(after),
        out_specs=[HBM_SPEC] * (2 * n),
        out_shape=_hbm_out([jax.ShapeDtypeStruct(a.shape, a.dtype) for a in carry.inputs]) + _hbm_out(carry.out_shapes),
        input_output_aliases={j: j for j in range(2 * n)},
        compiler_params=pltpu.CompilerParams(has_side_effects=DATAFLOW_EFFECT),
    )(*srcs, *zones, *sems, *after)
    return list(outs[:n]), list(outs[n:])


def _add_sibling(grads8, recvs, place, row_tiles, name):
    k = len(grads8)
    g4 = [g.reshape(4, 2, *g.shape[1:]) for g in grads8]

    def body(place_ref, *refs):
        g_refs, r_refs, o_refs, ob_refs = (refs[j * k:(j + 1) * k] for j in range(4))
        own = pl.program_id(1) == place_ref[1]
        for g_ref, r_ref, o_ref, ob_ref in zip(g_refs, r_refs, o_refs, ob_refs):
            s = g_ref[0] + r_ref[...]
            ob_ref[...] = s.astype(BF16)

            @pl.when(own)
            def _(o_ref=o_ref, s=s):
                o_ref[...] = s[0]

    def blocks(make):
        return [make(g.shape[1] // row_tiles, g.shape[2]) for g in grads8]

    slot = lambda tr, C: pl.BlockSpec((1, tr, C), lambda r, ch, place_ref: (ch, r, 0))
    outs = pl.pallas_call(
        body, name=name,
        grid_spec=pltpu.PrefetchScalarGridSpec(
            num_scalar_prefetch=1, grid=(row_tiles, 4),
            in_specs=blocks(lambda tr, C: pl.BlockSpec((1, 1, tr, C), lambda r, ch, place_ref: (ch, place_ref[0], r, 0)))
            + blocks(slot),
            out_specs=blocks(lambda tr, C: pl.BlockSpec((tr, C), lambda r, ch, place_ref: (r, 0))) + blocks(slot)),
        out_shape=_hbm_out([jax.ShapeDtypeStruct(g.shape[1:], F32) for g in grads8]
                           + [jax.ShapeDtypeStruct((4,) + g.shape[1:], BF16) for g in grads8]),
        compiler_params=_params(("arbitrary", "arbitrary")),
    )(place, *_in_hbm(*g4, *recvs))
    return list(outs[:k]), list(outs[k:])


def _adam_math(w, g, m, v):
    m = ADAM_B1 * m + (1.0 - ADAM_B1) * g
    v = ADAM_B2 * v + (1.0 - ADAM_B2) * jnp.square(g)
    m_hat = m / (1.0 - ADAM_B1 ** ADAM_STEP)
    v_hat = v / (1.0 - ADAM_B2 ** ADAM_STEP)
    delta = -ADAM_LR * (m_hat / (jnp.sqrt(v_hat) + ADAM_EPS) + ADAM_WD * w)
    return delta, m, v


def _adam_big(ws, ms, vs, chip_sums, recvs, row_tiles, name, after=()):
    k = len(ws)

    def body(*refs):
        refs = refs[:5 * k] + refs[5 * k + len(after):]
        w_refs, m_refs, v_refs, s_refs, r_refs, g_refs, d_refs, nm_refs, nv_refs = (refs[j * k:(j + 1) * k] for j in range(9))
        for a in range(k):
            r_ref = r_refs[a]
            g = s_refs[a][...] + r_ref[0].astype(F32) + r_ref[1].astype(F32) + r_ref[2].astype(F32)
            g_refs[a][...] = g
            d_refs[a][...], nm_refs[a][...], nv_refs[a][...] = _adam_math(w_refs[a][...], g, m_refs[a][...], v_refs[a][...])

    def blocks(make):
        return [make(w.shape[0] // row_tiles, w.shape[1]) for w in ws]

    blk = lambda tr, C: pl.BlockSpec((tr, C), lambda r: (r, 0))
    outs = pl.pallas_call(
        body, name=name, grid=(row_tiles,),
        in_specs=blocks(blk) * 4 + blocks(lambda tr, C: pl.BlockSpec((3, tr, C), lambda r: (0, r, 0))) + [HBM_SPEC] * len(after),
        out_specs=blocks(blk) * 4,
        out_shape=[jax.ShapeDtypeStruct(w.shape, F32) for w in ws] * 4,
        compiler_params=_params(("arbitrary",)),
    )(*_in_hbm(*ws, *ms, *vs, *chip_sums, *recvs), *after)
    return [list(outs[j * k:(j + 1) * k]) for j in range(4)]


def _sum_partials(partials, name, after=()):
    def body(p_ref, *refs):
        g = p_ref[0]
        for d in range(1, partials.shape[0]):
            g = g + p_ref[d]
        refs[-1][...] = g

    return pl.pallas_call(body, name=name, grid=(1,), in_specs=[_whole(partials.shape)] + [HBM_SPEC] * len(after),
                          out_specs=_whole(partials.shape[1:]),
                          out_shape=jax.ShapeDtypeStruct(partials.shape[1:], F32))(*_in_hbm(partials), *after)


def _adam_small(ws, ms, vs, gs):
    n = len(ws)

    def body(*refs):
        w_refs, m_refs, v_refs, g_refs = (refs[i * n:(i + 1) * n] for i in range(4))
        d_refs, nm_refs, nv_refs = (refs[(4 + i) * n:(5 + i) * n] for i in range(3))
        for j in range(n):
            d_refs[j][...], nm_refs[j][...], nv_refs[j][...] = _adam_math(
                w_refs[j][...], g_refs[j][...], m_refs[j][...], v_refs[j][...])

    specs = [_whole(w.shape) for w in ws]
    outs = pl.pallas_call(body, name="adam_small", grid=(1,), in_specs=specs * 4, out_specs=specs * 3,
                          out_shape=[jax.ShapeDtypeStruct(w.shape, F32) for w in ws] * 3,
                          compiler_params=_params(("arbitrary",), VMEM_MID))(*_in_hbm(*ws, *ms, *vs, *gs))
    return outs[:n], outs[n:2 * n], outs[2 * n:]


PACK_QUANTUM = SUBLANES * LANES


def _pack(named, names):
    parts = []
    for nme in names:
        flat = named[nme].reshape(-1)
        parts.append(jnp.pad(flat, (0, -flat.size % PACK_QUANTUM)))
    return jnp.concatenate(parts).reshape(-1, LANES)


def _unpack(packed, shapes, names):
    flat = packed.reshape(-1)
    out, pos = {}, 0
    for nme in names:
        size = math.prod(shapes[nme])
        out[nme] = flat[pos:pos + size].reshape(shapes[nme])
        pos += size + (-size % PACK_QUANTUM)
    return out


BIG = ("w_in", "w_glu", "w_attn_branch", "w_ssm_branch", "w_out", "w_ff_in", "w_ff_out")
COLUMN_SHARDED = ("w_in", "w_attn_branch", "w_ssm_branch", "w_ff_in")
SMALL = ("norm_mix_pre", "norm_mix_post", "norm_mlp_pre", "norm_mlp_post", "rel_bias", "sinks", "lam_re", "lam_im",
         "log_dt", "b_re", "b_im", "c_re", "c_im", "d_skip")
SWAPPED_SMALL = ("rel_bias", "b_re", "b_im")
SMALL_LATE = ("norm_mix_pre", "rel_bias", "sinks", "loss")
SMALL_BEFORE_ATTN_BWD = tuple(n for n in SMALL if n not in SMALL_LATE)
ALL_WEIGHTS = ("norm_mix_pre", "norm_mix_post", "norm_mlp_pre", "norm_mlp_post", "w_in", "rel_bias", "sinks", "lam_re",
               "lam_im", "log_dt", "b_re", "b_im", "c_re", "c_im", "d_skip", "w_glu", "w_attn_branch", "w_ssm_branch",
               "w_out", "w_ff_in", "w_ff_out")


def _full_from_gathered(name, gathered):
    _, r, c = gathered.shape
    if name in COLUMN_SHARDED:
        return jnp.transpose(gathered, (1, 0, 2)).reshape(r, N_DEV * c)
    return gathered.reshape(N_DEV * r, c)


def _blocks_from_full(name, full):
    r, c = full.shape
    if name in COLUMN_SHARDED:
        return jnp.transpose(full.reshape(r, N_DEV, c // N_DEV), (1, 0, 2))
    return full.reshape(N_DEV, r // N_DEV, c)


def kernel(x, norm_mix_pre, norm_mix_post, norm_mlp_pre, norm_mlp_post, w_in, rel_bias, sinks, lam_re, lam_im, log_dt, b_re, b_im, c_re, c_im, d_skip, w_glu, w_attn_branch, w_ssm_branch, w_out, w_ff_in, w_ff_out, loss_target, m_norm_mix_pre, m_norm_mix_post, m_norm_mlp_pre, m_norm_mlp_post, m_w_in, m_rel_bias, m_sinks, m_lam_re, m_lam_im, m_log_dt, m_b_re, m_b_im, m_c_re, m_c_im, m_d_skip, m_w_glu, m_w_attn_branch, m_w_ssm_branch, m_w_out, m_w_ff_in, m_w_ff_out, v_norm_mix_pre, v_norm_mix_post, v_norm_mlp_pre, v_norm_mlp_post, v_w_in, v_rel_bias, v_sinks, v_lam_re, v_lam_im, v_log_dt, v_b_re, v_b_im, v_c_re, v_c_im, v_d_skip, v_w_glu, v_w_attn_branch, v_w_ssm_branch, v_w_out, v_w_ff_in, v_w_ff_out):
    args = dict(locals())
    w = {n: args[n] for n in ALL_WEIGHTS}
    m = {n: args["m_" + n] for n in ALL_WEIGHTS}
    v = {n: args["v_" + n] for n in ALL_WEIGHTS}
    core = lax.axis_index("c").astype(jnp.int32).reshape(1)
    chip = (2 * lax.axis_index("x") + lax.axis_index("y")).astype(jnp.int32).reshape(1)
    xs, target = x[0], loss_target[0]
    t = _tiles(xs.shape[0])
    local = lambda d, n: d[n][0].T if n == "w_in" else d[n][0]
    gather_in = _gather_carry([local(w, "w_in").astype(BF16)], pass_on=False)
    gather_in_flight, token = _exchange_start(gather_in, "gather_w_in_start")
    one = token[0:1, 0:1] + 1.0
    shard = {n: (local(w, n) * one).astype(BF16) for n in BIG if n != "w_in"}
    shard["w_ff_in"] = shard["w_ff_in"].T
    view = lambda n, a: jnp.swapaxes(a, -1, -2) if n in SWAPPED_SMALL else a
    small = {n: (view(n, w[n]) if n == "rel_bias" else view(n, w[n])[0]) for n in SMALL}
    g1, g2, g3, g4 = (small[n].reshape(1, D_MODEL) for n in ("norm_mix_pre", "norm_mix_post", "norm_mlp_pre", "norm_mlp_post"))
    bucket = jnp.asarray(_bucket_table())
    rel_b, sink = small["rel_bias"], small["sinks"].reshape(1, N_HEADS)
    lam_r, lam_i = small["lam_re"].reshape(1, STATES), small["lam_im"].reshape(1, STATES)
    ldt_rep = jnp.repeat(small["log_dt"].reshape(N_GROUPS), N_STATE).reshape(1, STATES)
    bd_re, bd_im = _block_diag_in(small["b_re"] * one), _block_diag_in(small["b_im"] * one)
    cm_re, cm_im = _block_diag_out(small["c_re"] * one).astype(BF16), _block_diag_out(small["c_im"] * one).astype(BF16)
    dsk = small["d_skip"].reshape(1, SSM_W)
    a_re, a_im, bm_re, bm_im = _ssm_prep(lam_r, lam_i, ldt_rep, bd_re, bd_im)
    h = _pre_norm(xs, g1 * one, t["proj"])
    w_in_adam = [[local(d, "w_in") * one] for d in (w, m, v)]

    travelled_behind = list(shard.values()) + [a_re, a_im, bm_re, bm_im, cm_re, cm_im, h] + [d[0] for d in w_in_adam]
    _, landed = _exchange_wait(gather_in, gather_in_flight, travelled_behind, "gather_w_in_wait")
    (g_in,) = _gather_pass_on(landed, "gather_w_in_pass_on")
    wf_in = g_in.reshape(IN_W, D_MODEL)
    merge_names = ("w_glu", "w_attn_branch", "w_ssm_branch", "w_out")
    (q, k, vv, u, ga, gs), gathered = _in_proj_fwd(h, wf_in, t["proj"], _gather_carry([shard[n] for n in merge_names]))
    wf = {n: _full_from_gathered(n, g) for n, g in zip(merge_names, gathered)}
    (att,), (wf_ff_in,) = _attn_fwd(q, k, vv, bucket, rel_b, sink, _gather_carry([shard["w_ff_in"]]))
    (y, h_re, h_im, in_re, in_im), (wf_ff_out,) = _ssm_fwd(
        u, a_re, a_im, bm_re, bm_im, cm_re, cm_im, dsk, t["ssm_chunk"], _gather_carry([shard["w_ff_out"]]))
    x1, o, h2 = _merge_fwd(xs, y, att, ga, gs, g2, g3, wf["w_glu"], wf["w_ssm_branch"], wf["w_attn_branch"], wf["w_out"],
                           t["merge"])
    a, dfo, dx2, loss_blk, dg4 = _mlp_fwd(h2, x1, target, g4, wf_ff_in, wf_ff_out, t["mlp_fwd"])

    groups = {"ff": 4, "merge": 1, "w_in": 2}

    def add_sibling(group, blocks, received):
        return _add_sibling(blocks, received, jnp.concatenate([core, chip]), groups[group], "add_sibling_" + group)

    ff_names = ("w_ff_in", "w_ff_out")
    dw_ff_in, dw_ff_out, da = _mlp_weight_grads(dfo, a, h2, wf_ff_out, t["mlp_bwd"])
    dh2 = _mlp_input_grad(da, wf_ff_in.reshape(D_FF, D_MODEL), t["mlp_bwd"])
    ff_blocks = [dw_ff_in, dw_ff_out]
    (dx1, dgates, datt, dy, dw_glu, dw_ssm, dw_attn, dw_out, dg2, dg3), ff_recv = _merge_bwd(
        dh2, dx2, x1, o, y, att, ga, gs, g2, g3, wf["w_glu"], wf["w_ssm_branch"], wf["w_attn_branch"], wf["w_out"],
        t["merge_bwd"], _sibling_carry(ff_blocks))
    ff_sums, ff_sums_bf = add_sibling("ff", ff_blocks, ff_recv)
    merge_blocks = [_blocks_from_full(n, g) for n, g in zip(merge_names, (dw_glu, dw_attn, dw_ssm, dw_out))]
    (du, dbm_re, dbm_im, dcm_re, dcm_im, da_re, da_im, dd_skip), carried = _ssm_bwd(
        dy, u, h_re, h_im, in_re, in_im, a_re, a_im, bm_re, bm_im, cm_re, cm_im, dsk, t["ssm_chunk"],
        _join(_chips_carry(ff_sums_bf), _sibling_carry(merge_blocks)))
    ff_from_chips, merge_recv = carried[:2], carried[2:]
    merge_sums, merge_sums_bf = add_sibling("merge", merge_blocks, merge_recv)
    dbd_re, dbd_im, dlam_re, dlam_im, dldt_rep = _ssm_prep_bwd(lam_r, lam_i, ldt_rep, bd_re, bd_im, dbm_re, dbm_im, da_re, da_im)
    dlog_dt = _group_sum(dldt_rep.reshape(N_GROUPS, N_STATE))
    shapes = {n: view(n, w[n]).shape for n in SMALL}
    shapes["loss"] = (1,)
    small_grads = dict(
        norm_mix_post=dg2, norm_mlp_pre=dg3, norm_mlp_post=dg4, lam_re=dlam_re, lam_im=dlam_im, log_dt=dlog_dt,
        b_re=_block_diag_in_grad(dbd_re), b_im=_block_diag_in_grad(dbd_im),
        c_re=_block_diag_out_grad(dcm_re), c_im=_block_diag_out_grad(dcm_im), d_skip=dd_skip)
    packed_early = _pack({n: small_grads[n].reshape(shapes[n]) for n in SMALL_BEFORE_ATTN_BWD}, SMALL_BEFORE_ATTN_BWD)
    (dq, dkv, attn_small), carried = _attn_bwd(
        q, k, vv, datt, bucket, rel_b, sink, _join(_chips_carry(merge_sums_bf), _gather_carry([packed_early])))
    merge_from_chips, partials_early = carried[:-1], carried[-1]

    dparts = (dq, dkv, du, dgates)
    dw_in_t = _in_proj_weight_grad(h, dparts)
    in_blocks = [dw_in_t.reshape(N_DEV, IN_W // N_DEV, D_MODEL)]
    to_sibling = _sibling_carry(in_blocks)
    in_flight, token = _exchange_start(to_sibling, "w_in_sibling_start")
    n_tiles = xs.shape[0] // t["proj_bwd"]
    (grad_x, dg1), _ = _in_proj_input_grad(xs, g1 + token[0:1, 0:1], wf_in, dx1, dparts, t["proj_bwd"], 0, n_tiles, "in_proj_input_grad")
    late = dict(norm_mix_pre=dg1, rel_bias=attn_small[:, :N_BUCKETS, 0], sinks=attn_small[:, N_BUCKETS, 0], loss=loss_blk[0:1, 0])
    packed_late = _pack({n: late[n].reshape(shapes[n]) for n in SMALL_LATE}, SMALL_LATE)
    to_everyone = _everyone_carry([packed_late])
    in_blocks, in_recv = _exchange_wait(to_sibling, in_flight, [packed_late], "w_in_sibling_wait")
    in_sums, in_sums_bf = add_sibling("w_in", in_blocks, in_recv)
    to_chips = _chips_carry(in_sums_bf)
    started, chips_started = _exchange_start(_join(to_everyone, to_chips), "late_grads_and_w_in_chips_start")
    late_in_flight, in_flight = [[started[j] for j in js] for js in ((0, 1, 4, 6), (2, 3, 5, 7))]

    grads, deltas, new_m, new_v = {}, {}, {}, {}

    def adam_group(group, names, sums, received, after=(), wmv=None):
        wmv = wmv or [[local(d, n) for n in names] for d in (w, m, v)]
        outs = _adam_big(*wmv, sums, received, groups[group], "adam_" + group, after)
        for store, vals in zip((grads, deltas, new_m, new_v), outs):
            store.update({n: (o.T if n == "w_in" else o)[None] for n, o in zip(names, vals)})

    adam_group("ff", ff_names, ff_sums, ff_from_chips, [chips_started])
    adam_group("merge", merge_names, merge_sums, merge_from_chips, [chips_started])

    grads.update(_unpack(_sum_partials(partials_early, "sum_small_grads", [chips_started]), shapes, SMALL_BEFORE_ATTN_BWD))
    (packed_late,), (late_received,) = _exchange_wait(
        to_everyone, late_in_flight, [new_v["w_ff_out"], new_v["w_out"]], "late_grads_wait")
    grads.update(_unpack(_sum_everyone(packed_late, late_received, 2 * chip + core, "sum_late_grads"), shapes, SMALL_LATE))
    loss = grads.pop("loss").reshape(())
    small_out = _adam_small(*[[view(n, d[n]) for n in SMALL] for d in (w, m, v)], [grads[n] for n in SMALL])
    for store, vals in zip((deltas, new_m, new_v), small_out):
        store.update(zip(SMALL, vals))
    for store in (grads, deltas, new_m, new_v):
        store.update({n: view(n, store[n]) for n in SWAPPED_SMALL})

    busy = [new_v["w_ff_out"], new_v["w_out"], deltas["norm_mix_pre"]]
    _, (in_from_chips,) = _exchange_wait(to_chips, in_flight, busy, "w_in_chips_wait")
    adam_group("w_in", ("w_in",), in_sums, [in_from_chips], wmv=w_in_adam)

    return (loss, grad_x[None], *[grads[n] for n in ALL_WEIGHTS], *[deltas[n] for n in ALL_WEIGHTS],
            *[new_m[n] for n in ALL_WEIGHTS], *[new_v[n] for n in ALL_WEIGHTS])
```

```python
import math

import jax
import jax.numpy as jnp
import numpy as np
from jax import lax
from jax.experimental import pallas as pl
from jax.experimental.pallas import tpu as pltpu

F32 = jnp.float32
BF16 = jnp.bfloat16

D_MODEL = 1024
N_HEADS = 8
HEAD_DIM = 64
ATTN_W = 512
KV_W = 128
BLOCK = 128
N_BUCKETS = 32
SSM_W = 512
N_GROUPS = 32
N_STATE = 64
GROUP_CH = 16
STATES = N_GROUPS * N_STATE
D_FF = 4096
IN_W = 3328
SPLITS = (0, 512, 640, 768, 1280, 2304, 3328)
RMS_EPS = 1e-6
NEG_INF = -1e30
SUBLANES = 8
LANES = 128
SSM_LANE_BLOCK = 512
N_SSM_BLOCKS = STATES // SSM_LANE_BLOCK
GROUPS_PER_BLOCK = SSM_LANE_BLOCK // N_STATE
VMEM_BIG = 52 * 1024 * 1024
VMEM_MID = 40 * 1024 * 1024
VMEM_MAX = 60 * 1024 * 1024

ADAM_LR = 0.001
ADAM_B1 = 0.9
ADAM_B2 = 0.999
ADAM_EPS = 1e-08
ADAM_WD = 0.01
ADAM_STEP = 10

N_DEV = 8


def _dot(a, b):
    return jnp.dot(a, b, preferred_element_type=F32)


def _dot_nt(a, b):
    return lax.dot_general(a, b, (((1,), (1,)), ((), ())), preferred_element_type=F32)


def _dot_tn(a, b):
    return lax.dot_general(a, b, (((0,), (0,)), ((), ())), preferred_element_type=F32)


def _rms_scale(x):
    return lax.rsqrt(jnp.mean(x * x, axis=-1, keepdims=True) + RMS_EPS)


def _rms_bwd(dy, x, r, g):
    t = dy * g
    dx = r * t - x * (r * r * r) * jnp.mean(t * x, axis=-1, keepdims=True)
    dg = jnp.sum(dy * x * r, axis=0, keepdims=True)
    return dx, dg


def _const_spec(shape):
    nd = len(shape)
    return pl.BlockSpec(shape, lambda *_: (0,) * nd, pipeline_mode=pl.Buffered(1))


def _in_hbm(*arrays):
    return tuple(pltpu.with_memory_space_constraint(a, pltpu.HBM) for a in arrays)


def _hbm_out(shapes):
    if isinstance(shapes, (list, tuple)):
        return [_hbm_out(s) for s in shapes]
    return shapes if isinstance(shapes, pl.MemoryRef) else pltpu.HBM(shapes.shape, shapes.dtype)


def _whole(shape):
    nd = len(shape)
    return pl.BlockSpec(shape, lambda *_: (0,) * nd)


def _params(sem, vmem=None):
    return pltpu.CompilerParams(dimension_semantics=sem, vmem_limit_bytes=vmem)


MESH_IDS = pl.DeviceIdType.MESH
HBM_SPEC = pl.BlockSpec(memory_space=pl.ANY)


class _Carry:
    def __init__(self, inputs, out_shapes, sems, start, finish, middle=None):
        self.inputs, self.out_shapes, self.sems = list(inputs), list(out_shapes), list(sems)
        self.start, self.middle, self.finish = start, middle, finish


def _join(a, b):
    na_in, na_out, na_sem = len(a.inputs), len(a.out_shapes), len(a.sems)

    def both(phase):
        def run(ins, outs, sems):
            for carry, lo in ((a, True), (b, False)):
                part = (lambda seq, n: seq[:n] if lo else seq[n:])
                if getattr(carry, phase) is not None:
                    getattr(carry, phase)(part(ins, na_in), part(outs, na_out), part(sems, na_sem))
        return run

    middle = both("middle") if (a.middle or b.middle) else None
    return _Carry(a.inputs + b.inputs, a.out_shapes + b.out_shapes, a.sems + b.sems, both("start"), both("finish"), middle)


def _hosted_call(body, carry, edge, *, name, grid, in_specs, out_specs, out_shape, scratch_shapes, compiler_params, inputs):
    n_in, n_out = len(in_specs), len(out_specs)
    inputs = [a if s.memory_space == pltpu.SMEM else _in_hbm(a)[0] for a, s in zip(inputs, in_specs)]
    out_shape = _hbm_out(list(out_shape))
    if carry is None:
        outs = pl.pallas_call(body, name=name, grid=grid, in_specs=in_specs, out_specs=out_specs, out_shape=out_shape,
                              scratch_shapes=scratch_shapes, compiler_params=compiler_params)(*inputs)
        return list(outs), []
    c_in, c_out, c_sem = len(carry.inputs), len(carry.out_shapes), len(carry.sems)

    def wrapped(*refs):
        ins, refs = refs[:n_in], refs[n_in:]
        cins, refs = refs[:c_in], refs[c_in:]
        outs, refs = refs[:n_out], refs[n_out:]
        couts, refs = refs[:c_out], refs[c_out:]
        scratch, csems = refs[:len(refs) - c_sem], refs[len(refs) - c_sem:]
        first, middle, last = edge()

        @pl.when(first)
        def _():
            carry.start(cins, couts, csems)

        body(*ins, *outs, *scratch)

        if carry.middle is not None:
            @pl.when(middle)
            def _():
                carry.middle(cins, couts, csems)

        @pl.when(last)
        def _():
            carry.finish(cins, couts, csems)

    outs = pl.pallas_call(
        wrapped, name=name, grid=grid, in_specs=list(in_specs) + [HBM_SPEC] * c_in,
        out_specs=list(out_specs) + [HBM_SPEC] * c_out, out_shape=out_shape + _hbm_out(carry.out_shapes),
        scratch_shapes=list(scratch_shapes) + carry.sems, compiler_params=compiler_params)(*inputs, *_in_hbm(*carry.inputs))
    return list(outs[:n_out]), list(outs[n_out:])


def _pass_on_step(n_steps):
    return max(0, min((7 * n_steps) // 8, n_steps - 2))


def _edge_1d(n_steps, pass_on_last=False):
    middle = n_steps - 1 if pass_on_last else _pass_on_step(n_steps)
    return lambda: (pl.program_id(0) == 0, pl.program_id(0) == middle, pl.program_id(0) == n_steps - 1)


def _edge_2d(n0, n1):
    def edge():
        step = pl.program_id(0) * n1 + pl.program_id(1)
        return step == 0, step == _pass_on_step(n0 * n1), step == n0 * n1 - 1
    return edge


def _in_proj_fwd(x, g1, w_in_t, tile, carry=None):
    T = x.shape[0]

    def body(x_ref, g_ref, w_ref, q_ref, k_ref, v_ref, u_ref, ga_ref, gs_ref, h_ref):
        xv = x_ref[...]
        h = (xv * _rms_scale(xv) * g_ref[...]).astype(BF16)
        h_ref[...] = h
        outs = (q_ref, k_ref, v_ref, u_ref, ga_ref, gs_ref)
        for p, o_ref in enumerate(outs):
            o_ref[...] = _dot_nt(h, w_ref[SPLITS[p]:SPLITS[p + 1], :]).astype(o_ref.dtype)

    widths = [SPLITS[p + 1] - SPLITS[p] for p in range(6)] + [D_MODEL]
    dtypes = [BF16, BF16, BF16, F32, F32, F32, BF16]
    return _hosted_call(
        body, carry, _edge_1d(T // tile), name="in_proj_fwd", grid=(T // tile,),
        in_specs=[pl.BlockSpec((tile, D_MODEL), lambda i: (i, 0)), _const_spec((1, D_MODEL)), _const_spec((IN_W, D_MODEL))],
        out_specs=[pl.BlockSpec((tile, w), lambda i: (i, 0)) for w in widths],
        out_shape=[jax.ShapeDtypeStruct((T, w), dt) for w, dt in zip(widths, dtypes)],
        scratch_shapes=[], compiler_params=_params(("arbitrary",), VMEM_MID), inputs=(x, g1, w_in_t))


PROJ_PARTS = (512, 256, 512, 2048)
PROJ_GRAD_BLOCK = 256


def _in_proj_weight_grad(h, dparts):
    T = h.shape[0]
    blocks = [wd // PROJ_GRAD_BLOCK for wd in PROJ_PARTS]
    starts = [sum(blocks[:p]) for p in range(len(blocks))]

    def body(h_ref, *refs):
        part_refs, o_ref = refs[:-1], refs[-1]
        j = pl.program_id(0)
        for p_ref, start, count in zip(part_refs, starts, blocks):
            @pl.when((j >= start) & (j < start + count))
            def _(p_ref=p_ref):
                o_ref[...] = _dot_tn(p_ref[...], h_ref[...])

    def part_spec(start, count):
        return pl.BlockSpec((T, PROJ_GRAD_BLOCK), lambda j: (0, jnp.clip(j - start, 0, count - 1)))

    return pl.pallas_call(
        body, name="in_proj_weight_grad", grid=(sum(blocks),),
        in_specs=[_const_spec((T, D_MODEL))] + [part_spec(s, c) for s, c in zip(starts, blocks)],
        out_specs=pl.BlockSpec((PROJ_GRAD_BLOCK, D_MODEL), lambda j: (j, 0)),
        out_shape=_hbm_out(jax.ShapeDtypeStruct((IN_W, D_MODEL), F32)),
        compiler_params=_params(("arbitrary",), VMEM_MID),
    )(*_in_hbm(h, *dparts))


def _in_proj_input_grad(x, g1, w_in_t, dx1, dparts, tile, first_tile, n_tiles, name, carry=None):
    offsets = [sum(PROJ_PARTS[:p]) for p in range(len(PROJ_PARTS))]

    def body(x_ref, g_ref, w_ref, dx1_ref, *refs):
        part_refs, (gx_ref, dg_ref) = refs[:len(PROJ_PARTS)], refs[len(PROJ_PARTS):]
        i = pl.program_id(0)
        xv = x_ref[...]
        r = _rms_scale(xv)
        g = g_ref[...]
        dh = sum(_dot(p_ref[...], w_ref[off:off + wd, :]) for p_ref, off, wd in zip(part_refs, offsets, PROJ_PARTS))
        dxn, dg = _rms_bwd(dh, xv, r, g)
        gx_ref[...] = dx1_ref[...] + dxn

        @pl.when(i == 0)
        def _():
            dg_ref[...] = dg

        @pl.when(i > 0)
        def _():
            dg_ref[...] += dg

    tok = lambda wd: pl.BlockSpec((tile, wd), lambda i: (i + first_tile, 0))
    return _hosted_call(
        body, carry, _edge_1d(n_tiles), name=name, grid=(n_tiles,),
        in_specs=[tok(D_MODEL), _const_spec((1, D_MODEL)), _const_spec((IN_W, D_MODEL)), tok(D_MODEL)] + [tok(wd) for wd in PROJ_PARTS],
        out_specs=[pl.BlockSpec((tile, D_MODEL), lambda i: (i, 0)), pl.BlockSpec((1, D_MODEL), lambda i: (0, 0))],
        out_shape=[jax.ShapeDtypeStruct((n_tiles * tile, D_MODEL), F32), jax.ShapeDtypeStruct((1, D_MODEL), F32)],
        scratch_shapes=[], compiler_params=_params(("arbitrary",), VMEM_MID), inputs=(x, g1, w_in_t, dx1, *dparts))


def _bucket_table():
    qi = np.arange(BLOCK)[:, None]
    kj = np.arange(2 * BLOCK)[None, :]
    dist = qi + BLOCK - kj
    max_exact = N_BUCKETS // 2
    d = np.maximum(dist, 0)
    df = np.maximum(d, 1).astype(np.float32)
    large = max_exact + (np.log(df / np.float32(max_exact)) / np.float32(math.log(BLOCK / max_exact))
                         * np.float32(N_BUCKETS - max_exact)).astype(np.int32)
    large = np.minimum(large, N_BUCKETS - 1)
    bucket = np.where(d < max_exact, d, large)
    return np.where((dist >= 0) & (dist < BLOCK), bucket, -1).astype(np.int32)


def _build_bias(bucket_ref, rb_ref, bias_ref):
    bk = bucket_ref[...]
    for h in range(N_HEADS):
        def add(b, acc, h=h):
            return acc + jnp.where(bk == b, rb_ref[h, b], 0.0)
        bias_ref[h] = lax.fori_loop(0, N_BUCKETS, add, jnp.zeros((BLOCK, 2 * BLOCK), F32))


def _kv_variants(prev_ref, cur_ref):
    cat = jnp.concatenate([prev_ref[...], cur_ref[...]], axis=0)
    lo = lax.broadcasted_iota(jnp.int32, cat.shape, 1) < HEAD_DIM
    zero = jnp.zeros_like(cat)
    head0_lo = jnp.where(lo, cat, zero)
    head1_hi = jnp.where(lo, zero, cat)
    return ((head0_lo, pltpu.roll(head0_lo, HEAD_DIM, 1)), (pltpu.roll(head1_hi, HEAD_DIM, 1), head1_hi))


def _merge_kv_grads(g):
    lo = lax.broadcasted_iota(jnp.int32, g[0][0].shape, 1) < HEAD_DIM
    return jnp.where(lo, g[0][0] + pltpu.roll(g[0][1], HEAD_DIM, 1), g[1][1] + pltpu.roll(g[1][0], HEAD_DIM, 1))


def _head_lanes(h):
    return slice((h // 2) * LANES, (h // 2 + 1) * LANES)


def _attn_probs(q_ref, kvar, bias_ref, sk_ref, valid, s_ref):
    for h in range(N_HEADS):
        s_ref[h] = _dot_nt(q_ref[:, _head_lanes(h)], kvar[h // 4][h % 2])
    head = lax.broadcasted_iota(jnp.int32, (N_HEADS, 1, 1), 0)
    sink = jnp.zeros((N_HEADS, 1, 1), F32)
    for h in range(N_HEADS):
        sink = jnp.where(head == h, sk_ref[0, h], sink)
    s = jnp.where(valid[None], s_ref[...] * (HEAD_DIM ** -0.5) + bias_ref[...], NEG_INF)
    m = jnp.maximum(jnp.max(s, axis=-1, keepdims=True), sink)
    p = jnp.exp(s - m)
    e_sink = jnp.exp(sink - m)
    inv = 1.0 / (jnp.sum(p, axis=-1, keepdims=True) + e_sink)
    return p * inv, e_sink * inv


def _attn_valid(bucket_ref, n):
    col = lax.broadcasted_iota(jnp.int32, (BLOCK, 2 * BLOCK), 1)
    return (bucket_ref[...] >= 0) & ((n > 0) | (col >= BLOCK))


def _attn_fwd(q, k, v, bucket, rel_bias, sinks, carry=None):
    T = q.shape[0]
    nb = T // BLOCK

    def body(q_ref, kc_ref, kp_ref, vc_ref, vp_ref, bucket_ref, rb_ref, sk_ref, o_ref, bias_ref, s_ref, p_ref):
        n = pl.program_id(0)

        @pl.when(n == 0)
        def _():
            _build_bias(bucket_ref, rb_ref, bias_ref)

        kvar = _kv_variants(kp_ref, kc_ref)
        vvar = _kv_variants(vp_ref, vc_ref)
        pr, _ = _attn_probs(q_ref, kvar, bias_ref, sk_ref, _attn_valid(bucket_ref, n), s_ref)
        p_ref[...] = pr.astype(BF16)
        for m in range(N_HEADS // 2):
            acc = _dot(p_ref[2 * m], vvar[m // 2][0]) + _dot(p_ref[2 * m + 1], vvar[m // 2][1])
            o_ref[:, m * LANES:(m + 1) * LANES] = acc.astype(o_ref.dtype)

    cur = lambda w: pl.BlockSpec((BLOCK, w), lambda n: (n, 0))
    prev = lambda w: pl.BlockSpec((BLOCK, w), lambda n: (jnp.maximum(n - 1, 0), 0))
    smem = pl.BlockSpec(memory_space=pltpu.SMEM)
    return _hosted_call(
        body, carry, _edge_1d(nb, pass_on_last=True), name="attn_fwd", grid=(nb,),
        in_specs=[cur(ATTN_W), cur(KV_W), prev(KV_W), cur(KV_W), prev(KV_W), _const_spec((BLOCK, 2 * BLOCK)), smem, smem],
        out_specs=[cur(ATTN_W)],
        out_shape=[jax.ShapeDtypeStruct((T, ATTN_W), BF16)],
        scratch_shapes=[pltpu.VMEM((N_HEADS, BLOCK, 2 * BLOCK), F32), pltpu.VMEM((N_HEADS, BLOCK, 2 * BLOCK), F32),
                        pltpu.VMEM((N_HEADS, BLOCK, 2 * BLOCK), BF16)],
        compiler_params=_params(("arbitrary",)), inputs=(q, k, k, v, v, bucket, rel_bias, sinks))


ATTN_SMALL_ROWS = N_BUCKETS + SUBLANES


def _attn_bwd(q, k, v, datt, bucket, rel_bias, sinks, carry=None):
    T = q.shape[0]
    nb = T // BLOCK

    def body(q_ref, do_ref, kc_ref, kp_ref, vc_ref, vp_ref, bucket_ref, rb_ref, sk_ref,
             dq_ref, dkv_ref, small_ref, bias_ref, ds_sum_ref, dsink_ref, kcarry_ref, vcarry_ref,
             s_ref, dp_ref, p_ref, dsc_ref):
        n = pl.program_id(0)

        @pl.when(n == 0)
        def _():
            _build_bias(bucket_ref, rb_ref, bias_ref)
            ds_sum_ref[...] = jnp.zeros_like(ds_sum_ref)
            dsink_ref[...] = jnp.zeros_like(dsink_ref)
            kcarry_ref[...] = jnp.zeros_like(kcarry_ref)
            vcarry_ref[...] = jnp.zeros_like(vcarry_ref)

        @pl.when(n < nb)
        def _():
            kvar = _kv_variants(kp_ref, kc_ref)
            vvar = _kv_variants(vp_ref, vc_ref)
            pr, p_sink = _attn_probs(q_ref, kvar, bias_ref, sk_ref, _attn_valid(bucket_ref, n), s_ref)
            for h in range(N_HEADS):
                dp_ref[h] = _dot_nt(do_ref[:, _head_lanes(h)], vvar[h // 4][h % 2])
            dp = dp_ref[...]
            dsum = jnp.sum(pr * dp, axis=-1, keepdims=True)
            ds = pr * (dp - dsum)
            ds_sum_ref[...] += ds
            dsink_ref[...] -= jnp.sum(p_sink * dsum, axis=1, keepdims=True)
            dsc_ref[...] = (ds * (HEAD_DIM ** -0.5)).astype(BF16)
            p_ref[...] = pr.astype(BF16)
            for m in range(N_HEADS // 2):
                dqm = _dot(dsc_ref[2 * m], kvar[m // 2][0]) + _dot(dsc_ref[2 * m + 1], kvar[m // 2][1])
                dq_ref[:, m * LANES:(m + 1) * LANES] = dqm.astype(dq_ref.dtype)
            dk_var = [[None, None], [None, None]]
            dv_var = [[None, None], [None, None]]
            for kvh in range(2):
                for e in range(2):
                    heads = [h for h in range(N_HEADS) if h // 4 == kvh and h % 2 == e]
                    dk_var[kvh][e] = sum(_dot_tn(dsc_ref[h], q_ref[:, _head_lanes(h)]) for h in heads)
                    dv_var[kvh][e] = sum(_dot_tn(p_ref[h], do_ref[:, _head_lanes(h)]) for h in heads)
            dk_cat = _merge_kv_grads(dk_var)
            dv_cat = _merge_kv_grads(dv_var)

            @pl.when(n > 0)
            def _():
                dkv_ref[:, :KV_W] = (kcarry_ref[...] + dk_cat[:BLOCK]).astype(BF16)
                dkv_ref[:, KV_W:] = (vcarry_ref[...] + dv_cat[:BLOCK]).astype(BF16)

            kcarry_ref[...] = dk_cat[BLOCK:]
            vcarry_ref[...] = dv_cat[BLOCK:]

        @pl.when(n == nb)
        def _():
            dkv_ref[:, :KV_W] = kcarry_ref[...].astype(BF16)
            dkv_ref[:, KV_W:] = vcarry_ref[...].astype(BF16)
            bk = bucket_ref[...]
            row = lax.broadcasted_iota(jnp.int32, (N_HEADS, ATTN_SMALL_ROWS, LANES), 1)

            def add(b, acc):
                masked = jnp.where((bk == b)[None], ds_sum_ref[...], 0.0)
                val = jnp.sum(jnp.sum(masked, axis=1, keepdims=True), axis=2, keepdims=True)
                return acc + jnp.where(row == b, val, 0.0)

            small_ref[...] = lax.fori_loop(0, N_BUCKETS, add, jnp.where(row == N_BUCKETS, dsink_ref[...], 0.0))

    last = nb - 1
    cur = lambda w: pl.BlockSpec((BLOCK, w), lambda n: (jnp.minimum(n, last), 0))
    prev = lambda w: pl.BlockSpec((BLOCK, w), lambda n: (jnp.clip(n - 1, 0, last), 0))
    smem = pl.BlockSpec(memory_space=pltpu.SMEM)
    return _hosted_call(
        body, carry, _edge_1d(nb + 1), name="attn_bwd", grid=(nb + 1,),
        in_specs=[cur(ATTN_W), cur(ATTN_W), cur(KV_W), prev(KV_W), cur(KV_W), prev(KV_W),
                  _const_spec((BLOCK, 2 * BLOCK)), smem, smem],
        out_specs=[cur(ATTN_W), prev(2 * KV_W), pl.BlockSpec((N_HEADS, ATTN_SMALL_ROWS, LANES), lambda n: (0, 0, 0))],
        out_shape=[jax.ShapeDtypeStruct((T, ATTN_W), BF16), jax.ShapeDtypeStruct((T, 2 * KV_W), BF16),
                   jax.ShapeDtypeStruct((N_HEADS, ATTN_SMALL_ROWS, LANES), F32)],
        scratch_shapes=[pltpu.VMEM((N_HEADS, BLOCK, 2 * BLOCK), F32), pltpu.VMEM((N_HEADS, BLOCK, 2 * BLOCK), F32),
                        pltpu.VMEM((N_HEADS, 1, 1), F32), pltpu.VMEM((BLOCK, KV_W), F32), pltpu.VMEM((BLOCK, KV_W), F32),
                        pltpu.VMEM((N_HEADS, BLOCK, 2 * BLOCK), F32), pltpu.VMEM((N_HEADS, BLOCK, 2 * BLOCK), F32),
                        pltpu.VMEM((N_HEADS, BLOCK, 2 * BLOCK), BF16), pltpu.VMEM((N_HEADS, BLOCK, 2 * BLOCK), BF16)],
        compiler_params=_params(("arbitrary",)), inputs=(q, datt, k, k, v, v, bucket, rel_bias, sinks))


SCAN_UNROLL = 4


def _cmul(ar, ai, br, bi):
    return ar * br - ai * bi, ar * bi + ai * br


def _cmul_conj(ar, ai, br, bi):
    return ar * br + ai * bi, ar * bi - ai * br


def _ssm_discretize(lr, li, ldt):
    dt = jnp.exp(ldt)
    mag = jnp.exp(lr * dt)
    ab_re = mag * jnp.cos(li * dt)
    ab_im = mag * jnp.sin(li * dt)
    nr = ab_re - 1.0
    den = lr * lr + li * li
    f_re = (nr * lr + ab_im * li) / den
    f_im = (ab_im * lr - nr * li) / den
    return ab_re, ab_im, f_re, f_im


def _ssm_prep(lam_re, lam_im, ldt_rep, bd_re, bd_im):
    def body(lr_ref, li_ref, ldt_ref, bdr_ref, bdi_ref, ar_ref, ai_ref, br_ref, bi_ref):
        ab_re, ab_im, f_re, f_im = _ssm_discretize(lr_ref[...], li_ref[...], ldt_ref[...])
        ar_ref[...] = ab_re
        ai_ref[...] = ab_im
        bdr, bdi = bdr_ref[0], bdi_ref[0]
        br_ref[0] = (bdr * f_re - bdi * f_im).astype(BF16)
        bi_ref[0] = (bdi * f_re + bdr * f_im).astype(BF16)

    row = pl.BlockSpec((1, SSM_LANE_BLOCK), lambda j: (0, j))
    mat = pl.BlockSpec((1, LANES, SSM_LANE_BLOCK), lambda j: (j, 0, 0))
    return pl.pallas_call(
        body, name="ssm_prep", grid=(N_SSM_BLOCKS,),
        in_specs=[row, row, row, mat, mat], out_specs=[row, row, mat, mat],
        out_shape=[jax.ShapeDtypeStruct((1, STATES), F32)] * 2 + [jax.ShapeDtypeStruct((N_SSM_BLOCKS, LANES, SSM_LANE_BLOCK), BF16)] * 2,
        compiler_params=_params(("arbitrary",)),
    )(*_in_hbm(lam_re, lam_im, ldt_rep, bd_re, bd_im))


def _ssm_prep_bwd(lam_re, lam_im, ldt_rep, bd_re, bd_im, dbr, dbi, da_re, da_im):
    def body(lr_ref, li_ref, ldt_ref, bdr_ref, bdi_ref, dbr_ref, dbi_ref, dar_ref, dai_ref,
             dbdr_ref, dbdi_ref, dlr_ref, dli_ref, dldt_ref):
        lr, li, ldt = lr_ref[...], li_ref[...], ldt_ref[...]
        (_, _, f_re, f_im), vjp = jax.vjp(_ssm_discretize, lr, li, ldt)
        bdr, bdi, gbr, gbi = bdr_ref[0], bdi_ref[0], dbr_ref[0], dbi_ref[0]
        dbdr_ref[0] = gbr * f_re + gbi * f_im
        dbdi_ref[0] = gbi * f_re - gbr * f_im
        df_re = jnp.sum(gbr * bdr + gbi * bdi, axis=0, keepdims=True)
        df_im = jnp.sum(gbi * bdr - gbr * bdi, axis=0, keepdims=True)
        dlr, dli, dldt = vjp((dar_ref[...], dai_ref[...], df_re, df_im))
        dlr_ref[...] = dlr
        dli_ref[...] = dli
        dldt_ref[...] = dldt

    row = pl.BlockSpec((1, SSM_LANE_BLOCK), lambda j: (0, j))
    mat = pl.BlockSpec((1, LANES, SSM_LANE_BLOCK), lambda j: (j, 0, 0))
    mat_shape = jax.ShapeDtypeStruct((N_SSM_BLOCKS, LANES, SSM_LANE_BLOCK), F32)
    row_shape = jax.ShapeDtypeStruct((1, STATES), F32)
    return pl.pallas_call(
        body, name="ssm_prep_bwd", grid=(N_SSM_BLOCKS,),
        in_specs=[row, row, row, mat, mat, mat, mat, row, row], out_specs=[mat, mat, row, row, row],
        out_shape=[mat_shape, mat_shape, row_shape, row_shape, row_shape],
        compiler_params=_params(("arbitrary",)),
    )(*_in_hbm(lam_re, lam_im, ldt_rep, bd_re, bd_im, dbr, dbi, da_re, da_im))


def _group_sum(x):
    def body(x_ref, o_ref):
        o_ref[...] = jnp.sum(x_ref[...], axis=1, keepdims=True)
    return pl.pallas_call(body, name="ssm_group_sum", grid=(1,), in_specs=[_whole(x.shape)], out_specs=_whole((N_GROUPS, 1)),
                          out_shape=jax.ShapeDtypeStruct((N_GROUPS, 1), F32))(*_in_hbm(x))


def _power_table(ar, ai, p_re_ref, p_im_ref, steps):
    shape = (SUBLANES, SSM_LANE_BLOCK)
    p_re_ref[0:SUBLANES] = jnp.broadcast_to(ar, shape)
    p_im_ref[0:SUBLANES] = jnp.broadcast_to(ai, shape)
    m = 1
    while m < steps:
        rows = m * SUBLANES
        top_re = p_re_ref[rows - SUBLANES:rows]
        top_im = p_im_ref[rows - SUBLANES:rows]
        cur_re = p_re_ref[0:rows].reshape(m, SUBLANES, SSM_LANE_BLOCK)
        cur_im = p_im_ref[0:rows].reshape(m, SUBLANES, SSM_LANE_BLOCK)
        nxt_re, nxt_im = _cmul(cur_re, cur_im, top_re[None], top_im[None])
        p_re_ref[rows:2 * rows] = nxt_re.reshape(rows, SSM_LANE_BLOCK)
        p_im_ref[rows:2 * rows] = nxt_im.reshape(rows, SSM_LANE_BLOCK)
        m *= 2


def _to_segments(src_ref, dst_ref, steps):
    for s in range(SUBLANES):
        dst_ref[pl.ds(s, steps, stride=SUBLANES), :] = src_ref[s * steps:(s + 1) * steps, :]


def _from_segments(src_ref, dst_ref, steps):
    for s in range(SUBLANES):
        dst_ref[s * steps:(s + 1) * steps, :] = src_ref[pl.ds(s, steps, stride=SUBLANES), :]


def _segment_carries(e_re, e_im, an_re, an_im, c_re, c_im, reverse):
    order = range(SUBLANES - 1, -1, -1) if reverse else range(SUBLANES)
    ins_re, ins_im = [None] * SUBLANES, [None] * SUBLANES
    for s in order:
        ins_re[s], ins_im[s] = c_re, c_im
        pr, pi = _cmul(an_re, an_im, c_re, c_im)
        c_re = e_re[s:s + 1] + pr
        c_im = e_im[s:s + 1] + pi
    return jnp.concatenate(ins_re, axis=0), jnp.concatenate(ins_im, axis=0), c_re, c_im


def _ssm_fwd(u, a_re, a_im, b_re, b_im, c_re, c_im, d_skip, chunk, carry=None):
    T = u.shape[0]
    nc = T // chunk
    steps = chunk // SUBLANES
    blk = SSM_LANE_BLOCK

    def body(u_ref, ar_ref, ai_ref, br_ref, bi_ref, cr_ref, ci_ref, dk_ref,
             y_ref, hr_ref, hi_ref, inr_ref, ini_ref, useg_ref, yseg_ref, pr_ref, pi_ref, carry_ref):
        c = pl.program_id(1)
        ar, ai = ar_ref[...], ai_ref[...]

        @pl.when(c == 0)
        def _():
            _power_table(ar, ai, pr_ref, pi_ref, steps)
            carry_ref[...] = jnp.zeros_like(carry_ref)

        _to_segments(u_ref, useg_ref, steps)
        ub = useg_ref[...].astype(BF16)
        hr_ref[...] = _dot(ub, br_ref[0])
        hi_ref[...] = _dot(ub, bi_ref[0])
        first = slice(0, SUBLANES)

        def scan(t4, prev):
            for j in range(SCAN_UNROLL):
                rows = pl.ds(pl.multiple_of((t4 * SCAN_UNROLL + j) * SUBLANES, SUBLANES), SUBLANES)
                pr, pi = _cmul(pr_ref[first, :], pi_ref[first, :], prev[0], prev[1])
                prev = (pr + hr_ref[rows, :], pi + hi_ref[rows, :])
                hr_ref[rows, :] = prev[0]
                hi_ref[rows, :] = prev[1]
            return prev

        zero = jnp.zeros((SUBLANES, blk), F32)
        lax.fori_loop(0, steps // SCAN_UNROLL, scan, (zero, zero))

        top = slice(chunk - SUBLANES, chunk)
        in_re, in_im, out_re, out_im = _segment_carries(
            hr_ref[top, :], hi_ref[top, :], pr_ref[top, :][0:1], pi_ref[top, :][0:1],
            carry_ref[0:1, :], carry_ref[1:2, :], reverse=False)
        carry_ref[0:1, :] = out_re
        carry_ref[1:2, :] = out_im
        inr_ref[...] = in_re
        ini_ref[...] = in_im

        def fix(t4, _):
            for j in range(SCAN_UNROLL):
                rows = pl.ds(pl.multiple_of((t4 * SCAN_UNROLL + j) * SUBLANES, SUBLANES), SUBLANES)
                fr, fi = _cmul(pr_ref[rows, :], pi_ref[rows, :], in_re, in_im)
                hr_ref[rows, :] += fr
                hi_ref[rows, :] += fi
            return 0

        lax.fori_loop(0, steps // SCAN_UNROLL, fix, 0)

        yseg_ref[...] = _dot(hr_ref[...].astype(BF16), cr_ref[0]) - _dot(hi_ref[...].astype(BF16), ci_ref[0])
        _from_segments(yseg_ref, y_ref, steps)
        y_ref[...] += dk_ref[...] * u_ref[...]

    row = pl.BlockSpec((1, blk), lambda j, c: (0, j))
    b_mat = pl.BlockSpec((1, LANES, blk), lambda j, c: (j, 0, 0))
    c_mat = pl.BlockSpec((1, blk, LANES), lambda j, c: (j, 0, 0))
    tok = pl.BlockSpec((chunk, LANES), lambda j, c: (c, j))
    state = pl.BlockSpec((chunk, blk), lambda j, c: (c, j))
    enter = pl.BlockSpec((SUBLANES, blk), lambda j, c: (c, j))
    return _hosted_call(
        body, carry, _edge_2d(N_SSM_BLOCKS, nc), name="ssm_fwd", grid=(N_SSM_BLOCKS, nc),
        in_specs=[tok, row, row, b_mat, b_mat, c_mat, c_mat, pl.BlockSpec((1, LANES), lambda j, c: (0, j))],
        out_specs=[tok, state, state, enter, enter],
        out_shape=[jax.ShapeDtypeStruct((T, SSM_W), F32), jax.ShapeDtypeStruct((T, STATES), F32),
                   jax.ShapeDtypeStruct((T, STATES), F32), jax.ShapeDtypeStruct((nc * SUBLANES, STATES), F32),
                   jax.ShapeDtypeStruct((nc * SUBLANES, STATES), F32)],
        scratch_shapes=[pltpu.VMEM((chunk, LANES), F32), pltpu.VMEM((chunk, LANES), F32),
                        pltpu.VMEM((chunk, blk), F32), pltpu.VMEM((chunk, blk), F32), pltpu.VMEM((SUBLANES, blk), F32)],
        compiler_params=_params(("arbitrary", "arbitrary"), VMEM_MID),
        inputs=(u, a_re, a_im, b_re, b_im, c_re, c_im, d_skip))


def _ssm_bwd(dy, u, h_re, h_im, in_re, in_im, a_re, a_im, b_re, b_im, c_re, c_im, d_skip, chunk, carry=None):
    T = u.shape[0]
    nc = T // chunk
    steps = chunk // SUBLANES
    blk = SSM_LANE_BLOCK

    def body(dy_ref, u_ref, hr_ref, hi_ref, inr_ref, ini_ref, ar_ref, ai_ref, br_ref, bi_ref, cr_ref, ci_ref, dk_ref,
             du_ref, dbr_ref, dbi_ref, dcr_ref, dci_ref, dar_ref, dai_ref, ddk_ref,
             dyseg_ref, useg_ref, duseg_ref, gr_ref, gi_ref, pr_ref, pi_ref, carry_ref, accr_ref, acci_ref):
        c = pl.program_id(1)
        ar, ai = ar_ref[...], ai_ref[...]

        @pl.when(c == 0)
        def _():
            _power_table(ar, ai, pr_ref, pi_ref, steps)
            carry_ref[...] = jnp.zeros_like(carry_ref)
            accr_ref[...] = jnp.zeros_like(accr_ref)
            acci_ref[...] = jnp.zeros_like(acci_ref)

        _to_segments(dy_ref, dyseg_ref, steps)
        _to_segments(u_ref, useg_ref, steps)
        dyb = dyseg_ref[...].astype(BF16)
        ub = useg_ref[...].astype(BF16)
        gr_ref[...] = _dot_nt(dyb, cr_ref[0])
        gi_ref[...] = -_dot_nt(dyb, ci_ref[0])
        dcr = _dot_tn(hr_ref[...].astype(BF16), dyb)
        dci = -_dot_tn(hi_ref[...].astype(BF16), dyb)
        ddk = jnp.sum(dy_ref[...] * u_ref[...], axis=0, keepdims=True)

        first = slice(0, SUBLANES)

        def scan(k4, nxt):
            for j in range(SCAN_UNROLL):
                t = steps - 1 - (k4 * SCAN_UNROLL + j)
                rows = pl.ds(pl.multiple_of(t * SUBLANES, SUBLANES), SUBLANES)
                pr, pi = _cmul_conj(pr_ref[first, :], pi_ref[first, :], nxt[0], nxt[1])
                nxt = (pr + gr_ref[rows, :], pi + gi_ref[rows, :])
                gr_ref[rows, :] = nxt[0]
                gi_ref[rows, :] = nxt[1]
            return nxt

        top = slice(chunk - SUBLANES, chunk)
        zero = jnp.zeros((SUBLANES, blk), F32)
        lax.fori_loop(0, steps // SCAN_UNROLL, scan, (zero, zero))

        gin_re, gin_im, out_re, out_im = _segment_carries(
            gr_ref[0:SUBLANES, :], gi_ref[0:SUBLANES, :], pr_ref[top, :][0:1], -pi_ref[top, :][0:1],
            carry_ref[0:1, :], carry_ref[1:2, :], reverse=True)
        carry_ref[0:1, :] = out_re
        carry_ref[1:2, :] = out_im

        def fix_row(rows, prow, hp_re, hp_im, acc):
            fr, fi = _cmul_conj(pr_ref[prow, :], pi_ref[prow, :], gin_re, gin_im)
            g_re = gr_ref[rows, :] + fr
            g_im = gi_ref[rows, :] + fi
            gr_ref[rows, :] = g_re
            gi_ref[rows, :] = g_im
            return acc[0] + g_re * hp_re + g_im * hp_im, acc[1] + g_im * hp_re - g_re * hp_im

        def fix_at(t, acc):
            aligned = (lambda r: r * SUBLANES) if isinstance(t, int) else (lambda r: pl.multiple_of(r * SUBLANES, SUBLANES))
            rows, before, prow = (pl.ds(aligned(r), SUBLANES) for r in (t, t - 1, steps - 1 - t))
            return fix_row(rows, prow, hr_ref[before, :], hi_ref[before, :], acc)

        def fix(t4, acc):
            for j in range(SCAN_UNROLL):
                acc = fix_at(t4 * SCAN_UNROLL + j, acc)
            return acc

        acc = fix_row(first, top, inr_ref[...], ini_ref[...], (accr_ref[...], acci_ref[...]))
        for t in range(1, SCAN_UNROLL):
            acc = fix_at(t, acc)
        acc_re, acc_im = lax.fori_loop(1, steps // SCAN_UNROLL, fix, acc)
        accr_ref[...] = acc_re
        acci_ref[...] = acc_im

        gbr = gr_ref[...].astype(BF16)
        gbi = gi_ref[...].astype(BF16)
        duseg_ref[...] = _dot_nt(gbr, br_ref[0]) + _dot_nt(gbi, bi_ref[0])
        _from_segments(duseg_ref, dyseg_ref, steps)
        du_ref[...] = (dyseg_ref[...] + dk_ref[...] * dy_ref[...]).astype(BF16)
        dbr = _dot_tn(ub, gbr)
        dbi = _dot_tn(ub, gbi)

        @pl.when(c == 0)
        def _():
            dbr_ref[0] = dbr
            dbi_ref[0] = dbi
            dcr_ref[0] = dcr
            dci_ref[0] = dci
            ddk_ref[...] = ddk

        @pl.when(c > 0)
        def _():
            dbr_ref[0] += dbr
            dbi_ref[0] += dbi
            dcr_ref[0] += dcr
            dci_ref[0] += dci
            ddk_ref[...] += ddk

        @pl.when(c == nc - 1)
        def _():
            dar_ref[...] = jnp.sum(acc_re, axis=0, keepdims=True)
            dai_ref[...] = jnp.sum(acc_im, axis=0, keepdims=True)

    rev = lambda c: nc - 1 - c
    row = pl.BlockSpec((1, blk), lambda j, c: (0, j))
    b_mat = pl.BlockSpec((1, LANES, blk), lambda j, c: (j, 0, 0))
    c_mat = pl.BlockSpec((1, blk, LANES), lambda j, c: (j, 0, 0))
    tok = pl.BlockSpec((chunk, LANES), lambda j, c: (rev(c), j))
    state = pl.BlockSpec((chunk, blk), lambda j, c: (rev(c), j))
    enter = pl.BlockSpec((SUBLANES, blk), lambda j, c: (rev(c), j))
    chan = pl.BlockSpec((1, LANES), lambda j, c: (0, j))
    f32 = lambda *s: jax.ShapeDtypeStruct(s, F32)
    return _hosted_call(
        body, carry, _edge_2d(N_SSM_BLOCKS, nc), name="ssm_bwd", grid=(N_SSM_BLOCKS, nc),
        in_specs=[tok, tok, state, state, enter, enter, row, row, b_mat, b_mat, c_mat, c_mat, chan],
        out_specs=[tok, b_mat, b_mat, c_mat, c_mat, row, row, chan],
        out_shape=[jax.ShapeDtypeStruct((T, SSM_W), BF16), f32(N_SSM_BLOCKS, LANES, blk), f32(N_SSM_BLOCKS, LANES, blk),
                   f32(N_SSM_BLOCKS, blk, LANES), f32(N_SSM_BLOCKS, blk, LANES), f32(1, STATES), f32(1, STATES), f32(1, SSM_W)],
        scratch_shapes=[pltpu.VMEM((chunk, LANES), F32), pltpu.VMEM((chunk, LANES), F32), pltpu.VMEM((chunk, LANES), F32),
                        pltpu.VMEM((chunk, blk), F32), pltpu.VMEM((chunk, blk), F32),
                        pltpu.VMEM((chunk, blk), F32), pltpu.VMEM((chunk, blk), F32),
                        pltpu.VMEM((SUBLANES, blk), F32), pltpu.VMEM((SUBLANES, blk), F32), pltpu.VMEM((SUBLANES, blk), F32)],
        compiler_params=_params(("arbitrary", "arbitrary"), VMEM_BIG),
        inputs=(dy, u, h_re, h_im, in_re, in_im, a_re, a_im, b_re, b_im, c_re, c_im, d_skip))


def _merge_forward(y, att, ga, gs, w_glu, w_ssm, w_attn):
    z = jax.nn.gelu(y)
    zb = z.astype(BF16)
    gl = jax.nn.sigmoid(_dot(zb, w_glu))
    z2b = (z * gl).astype(BF16)
    y_ssm = _dot(z2b, w_ssm)
    y_attn = _dot(att, w_attn)
    sa = jax.nn.sigmoid(ga)
    ss = jax.nn.sigmoid(gs)
    merged = (sa * y_attn + ss * y_ssm).astype(BF16)
    return z, zb, gl, z2b, y_ssm, y_attn, sa, ss, merged


def _merge_fwd(x, y, att, ga, gs, g2, g3, w_glu, w_ssm, w_attn, w_out, tile):
    T = x.shape[0]

    def body(x_ref, y_ref, att_ref, ga_ref, gs_ref, g2_ref, g3_ref, wg_ref, ws_ref, wa_ref, wo_ref, x1_ref, o_ref, h2_ref):
        merged = _merge_forward(y_ref[...], att_ref[...], ga_ref[...], gs_ref[...], wg_ref[...], ws_ref[...], wa_ref[...])[-1]
        o = _dot(merged, wo_ref[...])
        x1 = x_ref[...] + o * _rms_scale(o) * g2_ref[...]
        o_ref[...] = o
        x1_ref[...] = x1
        h2_ref[...] = (x1 * _rms_scale(x1) * g3_ref[...]).astype(BF16)

    tok = lambda w: pl.BlockSpec((tile, w), lambda i: (i, 0))
    vec = _const_spec((1, D_MODEL))
    return pl.pallas_call(
        body, name="merge_fwd", grid=(T // tile,),
        in_specs=[tok(D_MODEL), tok(SSM_W), tok(ATTN_W), tok(D_MODEL), tok(D_MODEL), vec, vec,
                  _const_spec((SSM_W, SSM_W)), _const_spec((SSM_W, D_MODEL)), _const_spec((ATTN_W, D_MODEL)),
                  _const_spec((D_MODEL, D_MODEL))],
        out_specs=[tok(D_MODEL), tok(D_MODEL), tok(D_MODEL)],
        out_shape=_hbm_out([jax.ShapeDtypeStruct((T, D_MODEL), F32), jax.ShapeDtypeStruct((T, D_MODEL), F32),
                            jax.ShapeDtypeStruct((T, D_MODEL), BF16)]),
        compiler_params=_params(("arbitrary",), VMEM_MID),
    )(*_in_hbm(x, y, att, ga, gs, g2, g3, w_glu, w_ssm, w_attn, w_out))


def _merge_bwd(dh2, dx2, x1, o, y, att, ga, gs, g2, g3, w_glu, w_ssm, w_attn, w_out, tile, carry=None):
    T = x1.shape[0]
    n_steps = T // tile

    group = min(2, n_steps)
    staged_widths = (D_MODEL, D_MODEL, ATTN_W, D_MODEL, SSM_W, D_MODEL, SSM_W, SSM_W)

    def body(dh2_ref, dx2_ref, x1_ref, o_ref, y_ref, att_ref, ga_ref, gs_ref, g2_ref, g3_ref, wg_ref, ws_ref, wa_ref, wo_ref,
             dx1_ref, dgates_ref, datt_ref, dy_ref, dwg_hbm, dws_hbm, dwa_hbm, dwo_hbm, dg2_ref, dg3_ref,
             awg_ref, aws_ref, awa_ref, awo_ref, *staged):
        i = pl.program_id(0)
        x1v, ov = x1_ref[...], o_ref[...]
        dxn, dg3 = _rms_bwd(dh2_ref[...], x1v, _rms_scale(x1v), g3_ref[...])
        dx1 = dx2_ref[...] + dxn
        dx1_ref[...] = dx1
        do, dg2 = _rms_bwd(dx1, ov, _rms_scale(ov), g2_ref[...])
        dob = do.astype(BF16)

        yv = y_ref[...]
        att = att_ref[...]
        z, zb, gl, z2b, y_ssm, y_attn, sa, ss, merged = _merge_forward(
            yv, att, ga_ref[...], gs_ref[...], wg_ref[...], ws_ref[...], wa_ref[...])
        dmerged = _dot_nt(dob, wo_ref[...])
        dya = (dmerged * sa).astype(BF16)
        dys = (dmerged * ss).astype(BF16)
        dgates_ref[:, :D_MODEL] = (dmerged * y_attn * sa * (1.0 - sa)).astype(BF16)
        dgates_ref[:, D_MODEL:] = (dmerged * y_ssm * ss * (1.0 - ss)).astype(BF16)
        datt_ref[...] = _dot_nt(dya, wa_ref[...]).astype(BF16)
        dz2 = _dot_nt(dys, ws_ref[...])
        dpre = (dz2 * z * gl * (1.0 - gl)).astype(BF16)
        dz = dz2 * gl + _dot_nt(dpre, wg_ref[...])
        _, gelu_vjp = jax.vjp(jax.nn.gelu, yv)
        dy_ref[...] = gelu_vjp(dz)[0]

        part = pl.ds(pl.multiple_of((i % group) * tile, tile), tile)
        for ref, val in zip(staged, (merged, dob, att, dya, z2b, dys, zb, dpre)):
            ref[part, :] = val

        @pl.when(i == 0)
        def _():
            dg2_ref[...] = dg2
            dg3_ref[...] = dg3

        @pl.when(i > 0)
        def _():
            dg2_ref[...] += dg2
            dg3_ref[...] += dg3

        def weight_grads():
            s_merged, s_dob, s_att, s_dya, s_z2b, s_dys, s_zb, s_dpre = (ref[...] for ref in staged)
            return ((awo_ref, _dot_tn(s_merged, s_dob)), (awa_ref, _dot_tn(s_att, s_dya)),
                    (aws_ref, _dot_tn(s_z2b, s_dys)), (awg_ref, _dot_tn(s_zb, s_dpre)))

        @pl.when(i == group - 1)
        def _():
            for ref, val in weight_grads():
                ref[...] = val

        @pl.when((i % group == group - 1) & (i > group - 1))
        def _():
            for ref, val in weight_grads():
                ref[...] += val

        @pl.when(i == n_steps - 1)
        def _():
            pltpu.sync_copy(awg_ref, dwg_hbm)
            pltpu.sync_copy(aws_ref, dws_hbm)
            pltpu.sync_copy(awa_ref, dwa_hbm)
            pltpu.sync_copy(awo_ref, dwo_hbm)

    tok = lambda w: pl.BlockSpec((tile, w), lambda i: (i, 0))
    vec = _const_spec((1, D_MODEL))
    any_ = pl.BlockSpec(memory_space=pl.ANY)
    vec_out = pl.BlockSpec((1, D_MODEL), lambda i: (0, 0))
    f32 = lambda *s: jax.ShapeDtypeStruct(s, F32)
    bf = lambda *s: jax.ShapeDtypeStruct(s, BF16)
    return _hosted_call(
        body, carry, _edge_1d(n_steps), name="merge_bwd", grid=(n_steps,),
        in_specs=[tok(D_MODEL), tok(D_MODEL), tok(D_MODEL), tok(D_MODEL), tok(SSM_W), tok(ATTN_W), tok(D_MODEL), tok(D_MODEL),
                  vec, vec, _const_spec((SSM_W, SSM_W)), _const_spec((SSM_W, D_MODEL)), _const_spec((ATTN_W, D_MODEL)),
                  _const_spec((D_MODEL, D_MODEL))],
        out_specs=[tok(D_MODEL), tok(2 * D_MODEL), tok(ATTN_W), tok(SSM_W), any_, any_, any_, any_, vec_out, vec_out],
        out_shape=[f32(T, D_MODEL), bf(T, 2 * D_MODEL), bf(T, ATTN_W), f32(T, SSM_W),
                   f32(SSM_W, SSM_W), f32(SSM_W, D_MODEL), f32(ATTN_W, D_MODEL), f32(D_MODEL, D_MODEL),
                   f32(1, D_MODEL), f32(1, D_MODEL)],
        scratch_shapes=[pltpu.VMEM((SSM_W, SSM_W), F32), pltpu.VMEM((SSM_W, D_MODEL), F32),
                        pltpu.VMEM((ATTN_W, D_MODEL), F32), pltpu.VMEM((D_MODEL, D_MODEL), F32)]
        + [pltpu.VMEM((group * tile, wd), BF16) for wd in staged_widths],
        compiler_params=_params(("arbitrary",), VMEM_BIG),
        inputs=(dh2, dx2, x1, o, y, att, ga, gs, g2, g3, w_glu, w_ssm, w_attn, w_out))


FF_SHARD = D_FF // N_DEV


def _mlp_fwd(h2, x1, target, g4, w_ff_in, w_ff_out, tile):
    T = h2.shape[0]
    col_chunk = 2 * FF_SHARD

    def body(h2_ref, x1_ref, tg_ref, g4_ref, wi_ref, wo_ref, a_ref, dfo_ref, dx2_ref, loss_ref, dg4_ref, rr_ref):
        i = pl.program_id(0)
        h2v = h2_ref[...]
        for c in range(D_FF // col_chunk):
            cols = slice(c * col_chunk, (c + 1) * col_chunk)
            a = _dot_nt(h2v, wi_ref[cols, :])
            a_ref[:, cols] = a.astype(BF16)
            ra = jnp.maximum(a, 0.0)
            rr_ref[:, cols] = (ra * ra).astype(BF16)
        f = _dot(rr_ref[...], wo_ref[...])
        r = _rms_scale(f)
        g = g4_ref[...]
        err = x1_ref[...] + f * r * g - tg_ref[...]
        dx2 = err * (1.0 / D_MODEL)
        dx2_ref[...] = dx2
        dfo, dg = _rms_bwd(dx2, f, r, g)
        dfo_ref[...] = dfo.astype(BF16)
        row = lax.broadcasted_iota(jnp.int32, (SUBLANES, LANES), 0)
        col = lax.broadcasted_iota(jnp.int32, (SUBLANES, LANES), 1)
        loss = jnp.where((row == 0) & (col == 0), (0.5 / D_MODEL) * jnp.sum(err * err), 0.0)

        @pl.when(i == 0)
        def _():
            loss_ref[...] = loss
            dg4_ref[...] = dg

        @pl.when(i > 0)
        def _():
            loss_ref[...] += loss
            dg4_ref[...] += dg

    tok = pl.BlockSpec((tile, D_MODEL), lambda i: (i, 0))
    return pl.pallas_call(
        body, name="mlp_fwd", grid=(T // tile,),
        in_specs=[tok, tok, tok, _const_spec((1, D_MODEL)), _const_spec((D_FF, D_MODEL)), _const_spec((D_FF, D_MODEL))],
        out_specs=[pl.BlockSpec((tile, D_FF), lambda i: (i, 0)), tok, tok,
                   pl.BlockSpec((SUBLANES, LANES), lambda i: (0, 0)), pl.BlockSpec((1, D_MODEL), lambda i: (0, 0))],
        out_shape=_hbm_out([jax.ShapeDtypeStruct((T, D_FF), BF16), jax.ShapeDtypeStruct((T, D_MODEL), BF16),
                            jax.ShapeDtypeStruct((T, D_MODEL), F32), jax.ShapeDtypeStruct((SUBLANES, LANES), F32),
                            jax.ShapeDtypeStruct((1, D_MODEL), F32)]),
        scratch_shapes=[pltpu.VMEM((tile, D_FF), BF16)],
        compiler_params=_params(("arbitrary",), VMEM_MAX),
    )(*_in_hbm(h2, x1, target, g4, w_ff_in.reshape(D_FF, D_MODEL), w_ff_out.reshape(D_FF, D_MODEL)))


def _mlp_weight_grads(dfo, a, h2, w_ff_out, row_chunk):
    T = h2.shape[0]

    def body(dfo_ref, h2_ref, a_ref, wo_ref, dwi_ref, dwo_ref, da_ref, rr_ref):
        def rows(r, _):
            sl = pl.ds(pl.multiple_of(r * row_chunk, row_chunk), row_chunk)
            ra = jnp.maximum(a_ref[sl, :].astype(F32), 0.0)
            da_ref[sl, :] = (_dot_nt(dfo_ref[sl, :], wo_ref[0]) * (2.0 * ra)).astype(BF16)
            rr_ref[sl, :] = (ra * ra).astype(BF16)
            return 0

        lax.fori_loop(0, T // row_chunk, rows, 0)
        dwo_ref[0] = _dot_tn(rr_ref[...], dfo_ref[...])
        dwi_ref[0] = _dot_tn(h2_ref[...], da_ref[...])

    return pl.pallas_call(
        body, name="mlp_weight_grads", grid=(N_DEV,),
        in_specs=[_const_spec((T, D_MODEL)), _const_spec((T, D_MODEL)), pl.BlockSpec((T, FF_SHARD), lambda k: (0, k)),
                  pl.BlockSpec((1, FF_SHARD, D_MODEL), lambda k: (k, 0, 0))],
        out_specs=[pl.BlockSpec((1, D_MODEL, FF_SHARD), lambda k: (k, 0, 0)),
                   pl.BlockSpec((1, FF_SHARD, D_MODEL), lambda k: (k, 0, 0)), pl.BlockSpec((T, FF_SHARD), lambda k: (0, k))],
        out_shape=_hbm_out([jax.ShapeDtypeStruct((N_DEV, D_MODEL, FF_SHARD), F32),
                            jax.ShapeDtypeStruct((N_DEV, FF_SHARD, D_MODEL), F32), jax.ShapeDtypeStruct((T, D_FF), BF16)]),
        scratch_shapes=[pltpu.VMEM((T, FF_SHARD), BF16)],
        compiler_params=_params(("arbitrary",), VMEM_MAX),
    )(*_in_hbm(dfo, h2, a, w_ff_out))


def _mlp_input_grad(da, w_ff_in_t, tile):
    T = da.shape[0]

    def body(da_ref, w_ref, o_ref):
        o_ref[...] = _dot(da_ref[...], w_ref[...])

    return pl.pallas_call(
        body, name="mlp_input_grad", grid=(T // tile,),
        in_specs=[pl.BlockSpec((tile, D_FF), lambda i: (i, 0)), _const_spec((D_FF, D_MODEL))],
        out_specs=pl.BlockSpec((tile, D_MODEL), lambda i: (i, 0)),
        out_shape=_hbm_out(jax.ShapeDtypeStruct((T, D_MODEL), F32)),
        compiler_params=_params(("arbitrary",), VMEM_MID),
    )(*_in_hbm(da, w_ff_in_t))


def _block_diag_in(b):
    bt = b.reshape(N_SSM_BLOCKS, GROUPS_PER_BLOCK, GROUP_CH, N_STATE)
    eye = jnp.eye(GROUPS_PER_BLOCK, dtype=b.dtype)
    return jnp.einsum("jacp,ab->jacbp", bt, eye).reshape(N_SSM_BLOCKS, LANES, SSM_LANE_BLOCK)


def _block_diag_in_grad(g):
    g = g.reshape(N_SSM_BLOCKS, GROUPS_PER_BLOCK, GROUP_CH, GROUPS_PER_BLOCK, N_STATE)
    d = jnp.diagonal(g, axis1=1, axis2=3)
    return jnp.transpose(d, (0, 3, 1, 2)).reshape(N_GROUPS, GROUP_CH, N_STATE)


def _block_diag_out(c):
    ct = c.reshape(N_SSM_BLOCKS, GROUPS_PER_BLOCK, GROUP_CH, N_STATE)
    eye = jnp.eye(GROUPS_PER_BLOCK, dtype=c.dtype)
    return jnp.einsum("jacp,ab->japbc", ct, eye).reshape(N_SSM_BLOCKS, SSM_LANE_BLOCK, LANES)


def _block_diag_out_grad(g):
    g = g.reshape(N_SSM_BLOCKS, GROUPS_PER_BLOCK, N_STATE, GROUPS_PER_BLOCK, GROUP_CH)
    d = jnp.diagonal(g, axis1=1, axis2=3)
    return jnp.transpose(d, (0, 3, 2, 1)).reshape(N_GROUPS, GROUP_CH, N_STATE)


def _tiles(T):
    return dict(proj=min(512, T), proj_bwd=min(512, T // 2), merge=min(512, T), merge_bwd=min(256, T),
                mlp_fwd=min(512, T), mlp_bwd=min(512, T), ssm_chunk=min(1024, T))


def _mesh_position():
    x, y, c = lax.axis_index("x"), lax.axis_index("y"), lax.axis_index("c")
    other_chips = [(1 - x, y), (x, 1 - y), (1 - x, 1 - y)]
    return x, y, c, other_chips


def _gather_carry(arrays, pass_on=True):
    n = len(arrays)

    def copies(ins, outs, sems):
        send_sems, recv_sems, local_sems = sems
        x, y, c, chips = _mesh_position()
        me, sibling = (x, y, c), (x, y, 1 - c)

        def copy(a, k, block, to, src=None):
            px, py, pc = block
            dst = outs[a].at[4 * px + 2 * py + pc]
            return pltpu.make_async_remote_copy(
                src_ref=dst if src is None else src, dst_ref=dst, send_sem=send_sems.at[7 * a + k],
                recv_sem=recv_sems.at[7 * a + k], device_id=to, device_id_type=MESH_IDS)

        mine = [pltpu.make_async_copy(ins[a], outs[a].at[4 * x + 2 * y + c], local_sems.at[a]) for a in range(n)]
        first = []
        for a in range(n):
            first.append(copy(a, 0, me, sibling, src=ins[a]))
            first += [copy(a, 1 + j, me, (*chip, c), src=ins[a]) for j, chip in enumerate(chips)]
        return copy, mine, first, me, sibling, chips, c

    def start(ins, outs, sems):
        _, mine, first, *_ = copies(ins, outs, sems)
        for cp in mine + first:
            cp.start()

    def passed_on(copy, sibling, chips, c):
        return [copy(a, 4 + j, (*chip, c), sibling) for a in range(n) for j, chip in enumerate(chips)]

    def middle(ins, outs, sems):
        copy, _, _, me, sibling, chips, c = copies(ins, outs, sems)
        for a in range(n):
            for j, chip in enumerate(chips):
                copy(a, 1 + j, (*chip, c), me).wait_recv()
                copy(a, 4 + j, (*chip, c), sibling).start()

    def finish(ins, outs, sems):
        copy, mine, first, me, sibling, chips, c = copies(ins, outs, sems)
        for a in range(n):
            copy(a, 0, sibling, me).wait_recv()
            for j, chip in enumerate(chips):
                (copy(a, 4 + j, (*chip, 1 - c), me) if pass_on else copy(a, 1 + j, (*chip, c), me)).wait_recv()
        for cp in first + (passed_on(copy, sibling, chips, c) if pass_on else []):
            cp.wait_send()
        for cp in mine:
            cp.wait()

    return _Carry(arrays, [jax.ShapeDtypeStruct((N_DEV,) + a.shape, a.dtype) for a in arrays],
                  [pltpu.SemaphoreType.DMA((7 * n,)), pltpu.SemaphoreType.DMA((7 * n,)), pltpu.SemaphoreType.DMA((n,))],
                  start, finish, middle if pass_on else None)


def _gather_pass_on(gathered, name):
    n = len(gathered)

    def body(*refs):
        zones, send_sems, recv_sems = refs[:n], refs[2 * n], refs[2 * n + 1]
        x, y, c, chips = _mesh_position()
        copies = []
        for a in range(n):
            for j, (px, py) in enumerate(chips):
                block = zones[a].at[4 * px + 2 * py + c]
                copies.append(pltpu.make_async_remote_copy(
                    src_ref=block, dst_ref=block, send_sem=send_sems.at[3 * a + j], recv_sem=recv_sems.at[3 * a + j],
                    device_id=(x, y, 1 - c), device_id_type=MESH_IDS))
        for cp in copies:
            cp.start()
        for cp in copies:
            cp.wait()

    return pl.pallas_call(
        body, name=name, in_specs=[HBM_SPEC] * n, out_specs=[HBM_SPEC] * n,
        out_shape=_hbm_out([jax.ShapeDtypeStruct(g.shape, g.dtype) for g in gathered]),
        scratch_shapes=[pltpu.SemaphoreType.DMA((3 * n,)), pltpu.SemaphoreType.DMA((3 * n,))],
        input_output_aliases={a: a for a in range(n)})(*gathered)


def _pairwise_carry(arrays, n_slots, make_copies):
    n = len(arrays)

    def start(ins, outs, sems):
        for cp in make_copies(ins, outs, sems):
            cp.start()

    def finish(ins, outs, sems):
        for cp in make_copies(ins, outs, sems):
            cp.wait()

    return _Carry(arrays, [jax.ShapeDtypeStruct((n_slots,) + a.shape[1:], a.dtype) for a in arrays],
                  [pltpu.SemaphoreType.DMA((n_slots * n,)), pltpu.SemaphoreType.DMA((n_slots * n,))], start, finish)


def _sibling_carry(grads):
    def make_copies(ins, outs, sems):
        x, y, c, _ = _mesh_position()
        return [pltpu.make_async_remote_copy(
            src_ref=ins[a].at[2 * ch + (1 - c)], dst_ref=outs[a].at[ch], send_sem=sems[0].at[4 * a + ch],
            recv_sem=sems[1].at[4 * a + ch], device_id=(x, y, 1 - c), device_id_type=MESH_IDS)
            for a in range(len(grads)) for ch in range(4)]

    return _pairwise_carry(grads, 4, make_copies)


def _chips_carry(sums):
    def make_copies(ins, outs, sems):
        x, y, c, chips = _mesh_position()
        return [pltpu.make_async_remote_copy(
            src_ref=ins[a].at[2 * px + py], dst_ref=outs[a].at[j], send_sem=sems[0].at[3 * a + j],
            recv_sem=sems[1].at[3 * a + j], device_id=(px, py, c), device_id_type=MESH_IDS)
            for a in range(len(sums)) for j, (px, py) in enumerate(chips)]

    return _pairwise_carry(sums, 3, make_copies)


def _everyone_carry(arrays):
    def make_copies(ins, outs, sems):
        x, y, c, _ = _mesh_position()
        flip = lambda v, bit: 1 - v if bit else v
        return [pltpu.make_async_remote_copy(
            src_ref=ins[a], dst_ref=outs[a].at[r - 1], send_sem=sems[0].at[7 * a + r - 1], recv_sem=sems[1].at[7 * a + r - 1],
            device_id=(flip(x, r & 4), flip(y, r & 2), flip(c, r & 1)), device_id_type=MESH_IDS)
            for a in range(len(arrays)) for r in range(1, N_DEV)]

    carry = _pairwise_carry([jax.ShapeDtypeStruct((1,) + a.shape, a.dtype) for a in arrays], N_DEV - 1, make_copies)
    carry.inputs = list(arrays)
    return carry


def _sum_everyone(own, received, me, name, after=()):
    def body(me_ref, own_ref, r_ref, *refs):
        g = None
        for d in range(N_DEV):
            relation = jnp.bitwise_xor(d, me_ref[0])
            part = jnp.where(relation == 0, own_ref[...], r_ref[jnp.maximum(relation - 1, 0)])
            g = part if g is None else g + part
        refs[-1][...] = g

    whole = lambda shape: pl.BlockSpec(shape, lambda i, me_ref: (0,) * len(shape))
    return pl.pallas_call(
        body, name=name,
        grid_spec=pltpu.PrefetchScalarGridSpec(
            num_scalar_prefetch=1, grid=(1,), in_specs=[whole(own.shape), whole(received.shape)] + [HBM_SPEC] * len(after),
            out_specs=whole(own.shape)),
        out_shape=jax.ShapeDtypeStruct(own.shape, F32))(me, *_in_hbm(own, received), *after)


SEM_SPEC = pl.BlockSpec(memory_space=pltpu.SEMAPHORE)
DATAFLOW_EFFECT = pltpu.SideEffectType.DATAFLOW_SIDE_EFFECTING


def _exchange_start(carry, name, after=()):
    n, n_sems = len(carry.inputs), len(carry.sems)
    lands = [lax.empty(s.shape, s.dtype) for s in carry.out_shapes]

    def body(*refs):
        first_out = 2 * n + len(after)
        srcs, zones, sems, token = refs[:n], refs[n:2 * n], refs[first_out:first_out + n_sems], refs[-1]
        carry.start(srcs, zones, sems)
        token[...] = jnp.zeros_like(token)

    outs = pl.pallas_call(
        body, name=name, in_specs=[HBM_SPEC] * (2 * n + len(after)),
        out_specs=[SEM_SPEC] * n_sems + [HBM_SPEC] * (2 * n) + [pl.BlockSpec(memory_space=pltpu.VMEM)],
        out_shape=list(carry.sems) + _hbm_out([jax.ShapeDtypeStruct(a.shape, a.dtype) for a in carry.inputs])
        + _hbm_out(carry.out_shapes) + [jax.ShapeDtypeStruct((SUBLANES, LANES), F32)],
        input_output_aliases={j: n_sems + j for j in range(2 * n)},
        compiler_params=pltpu.CompilerParams(has_side_effects=DATAFLOW_EFFECT),
    )(*_in_hbm(*carry.inputs, *lands), *after)
    return outs[:-1], outs[-1]


def _exchange_wait(carry, in_flight, after, name):
    n, n_sems = len(carry.inputs), len(carry.sems)
    sems, srcs, zones = in_flight[:n_sems], in_flight[n_sems:n_sems + n], in_flight[n_sems + n:]

    def body(*refs):
        src_refs, zone_refs, sem_refs = refs[:n], refs[n:2 * n], refs[2 * n:2 * n + n_sems]
        carry.finish(src_refs, zone_refs, sem_refs)

    outs = pl.pallas_call(
        body, name=name, in_specs=[HBM_SPEC] * (2 * n) + [SEM_SPEC] * n_sems + [HBM_SPEC] * len(after),
        out_specs=[HBM_SPEC] * (2 * n),
        out_shape=_hbm_out([jax.ShapeDtypeStruct(a.shape, a.dtype) for a in carry.inputs]) + _hbm_out(carry.out_shapes),
        input_output_aliases={j: j for j in range(2 * n)},
        compiler_params=pltpu.CompilerParams(has_side_effects=DATAFLOW_EFFECT),
    )(*srcs, *zones, *sems, *after)
    return list(outs[:n]), list(outs[n:])


def _add_sibling(grads8, recvs, place, row_tiles, name):
    k = len(grads8)
    g4 = [g.reshape(4, 2, *g.shape[1:]) for g in grads8]

    def body(place_ref, *refs):
        g_refs, r_refs, o_refs, ob_refs = (refs[j * k:(j + 1) * k] for j in range(4))
        own = pl.program_id(1) == place_ref[1]
        for g_ref, r_ref, o_ref, ob_ref in zip(g_refs, r_refs, o_refs, ob_refs):
            s = g_ref[0] + r_ref[...]
            ob_ref[...] = s.astype(BF16)

            @pl.when(own)
            def _(o_ref=o_ref, s=s):
                o_ref[...] = s[0]

    def blocks(make):
        return [make(g.shape[1] // row_tiles, g.shape[2]) for g in grads8]

    slot = lambda tr, C: pl.BlockSpec((1, tr, C), lambda r, ch, place_ref: (ch, r, 0))
    outs = pl.pallas_call(
        body, name=name,
        grid_spec=pltpu.PrefetchScalarGridSpec(
            num_scalar_prefetch=1, grid=(row_tiles, 4),
            in_specs=blocks(lambda tr, C: pl.BlockSpec((1, 1, tr, C), lambda r, ch, place_ref: (ch, place_ref[0], r, 0)))
            + blocks(slot),
            out_specs=blocks(lambda tr, C: pl.BlockSpec((tr, C), lambda r, ch, place_ref: (r, 0))) + blocks(slot)),
        out_shape=_hbm_out([jax.ShapeDtypeStruct(g.shape[1:], F32) for g in grads8]
                           + [jax.ShapeDtypeStruct((4,) + g.shape[1:], BF16) for g in grads8]),
        compiler_params=_params(("arbitrary", "arbitrary")),
    )(place, *_in_hbm(*g4, *recvs))
    return list(outs[:k]), list(outs[k:])


def _adam_math(w, g, m, v):
    m = ADAM_B1 * m + (1.0 - ADAM_B1) * g
    v = ADAM_B2 * v + (1.0 - ADAM_B2) * jnp.square(g)
    m_hat = m / (1.0 - ADAM_B1 ** ADAM_STEP)
    v_hat = v / (1.0 - ADAM_B2 ** ADAM_STEP)
    delta = -ADAM_LR * (m_hat / (jnp.sqrt(v_hat) + ADAM_EPS) + ADAM_WD * w)
    return delta, m, v


def _adam_big(ws, ms, vs, chip_sums, recvs, row_tiles, name, after=()):
    k = len(ws)

    def body(*refs):
        refs = refs[:5 * k] + refs[5 * k + len(after):]
        w_refs, m_refs, v_refs, s_refs, r_refs, g_refs, d_refs, nm_refs, nv_refs = (refs[j * k:(j + 1) * k] for j in range(9))
        for a in range(k):
            r_ref = r_refs[a]
            g = s_refs[a][...] + r_ref[0].astype(F32) + r_ref[1].astype(F32) + r_ref[2].astype(F32)
            g_refs[a][...] = g
            d_refs[a][...], nm_refs[a][...], nv_refs[a][...] = _adam_math(w_refs[a][...], g, m_refs[a][...], v_refs[a][...])

    def blocks(make):
        return [make(w.shape[0] // row_tiles, w.shape[1]) for w in ws]

    blk = lambda tr, C: pl.BlockSpec((tr, C), lambda r: (r, 0))
    outs = pl.pallas_call(
        body, name=name, grid=(row_tiles,),
        in_specs=blocks(blk) * 4 + blocks(lambda tr, C: pl.BlockSpec((3, tr, C), lambda r: (0, r, 0))) + [HBM_SPEC] * len(after),
        out_specs=blocks(blk) * 4,
        out_shape=[jax.ShapeDtypeStruct(w.shape, F32) for w in ws] * 4,
        compiler_params=_params(("arbitrary",)),
    )(*_in_hbm(*ws, *ms, *vs, *chip_sums, *recvs), *after)
    return [list(outs[j * k:(j + 1) * k]) for j in range(4)]


def _sum_partials(partials, name, after=()):
    def body(p_ref, *refs):
        g = p_ref[0]
        for d in range(1, partials.shape[0]):
            g = g + p_ref[d]
        refs[-1][...] = g

    return pl.pallas_call(body, name=name, grid=(1,), in_specs=[_whole(partials.shape)] + [HBM_SPEC] * len(after),
                          out_specs=_whole(partials.shape[1:]),
                          out_shape=jax.ShapeDtypeStruct(partials.shape[1:], F32))(*_in_hbm(partials), *after)


def _adam_small(ws, ms, vs, gs):
    n = len(ws)

    def body(*refs):
        w_refs, m_refs, v_refs, g_refs = (refs[i * n:(i + 1) * n] for i in range(4))
        d_refs, nm_refs, nv_refs = (refs[(4 + i) * n:(5 + i) * n] for i in range(3))
        for j in range(n):
            d_refs[j][...], nm_refs[j][...], nv_refs[j][...] = _adam_math(
                w_refs[j][...], g_refs[j][...], m_refs[j][...], v_refs[j][...])

    specs = [_whole(w.shape) for w in ws]
    outs = pl.pallas_call(body, name="adam_small", grid=(1,), in_specs=specs * 4, out_specs=specs * 3,
                          out_shape=[jax.ShapeDtypeStruct(w.shape, F32) for w in ws] * 3,
                          compiler_params=_params(("arbitrary",), VMEM_MID))(*_in_hbm(*ws, *ms, *vs, *gs))
    return outs[:n], outs[n:2 * n], outs[2 * n:]


PACK_QUANTUM = SUBLANES * LANES


def _pack(named, names):
    parts = []
    for nme in names:
        flat = named[nme].reshape(-1)
        parts.append(jnp.pad(flat, (0, -flat.size % PACK_QUANTUM)))
    return jnp.concatenate(parts).reshape(-1, LANES)


def _unpack(packed, shapes, names):
    flat = packed.reshape(-1)
    out, pos = {}, 0
    for nme in names:
        size = math.prod(shapes[nme])
        out[nme] = flat[pos:pos + size].reshape(shapes[nme])
        pos += size + (-size % PACK_QUANTUM)
    return out


BIG = ("w_in", "w_glu", "w_attn_branch", "w_ssm_branch", "w_out", "w_ff_in", "w_ff_out")
COLUMN_SHARDED = ("w_in", "w_attn_branch", "w_ssm_branch", "w_ff_in")
SMALL = ("norm_mix_pre", "norm_mix_post", "norm_mlp_pre", "norm_mlp_post", "rel_bias", "sinks", "lam_re", "lam_im",
         "log_dt", "b_re", "b_im", "c_re", "c_im", "d_skip")
SWAPPED_SMALL = ("rel_bias", "b_re", "b_im")
SMALL_LATE = ("norm_mix_pre", "rel_bias", "sinks", "loss")
SMALL_BEFORE_ATTN_BWD = tuple(n for n in SMALL if n not in SMALL_LATE)
ALL_WEIGHTS = ("norm_mix_pre", "norm_mix_post", "norm_mlp_pre", "norm_mlp_post", "w_in", "rel_bias", "sinks", "lam_re",
               "lam_im", "log_dt", "b_re", "b_im", "c_re", "c_im", "d_skip", "w_glu", "w_attn_branch", "w_ssm_branch",
               "w_out", "w_ff_in", "w_ff_out")


def _full_from_gathered(name, gathered):
    _, r, c = gathered.shape
    if name in COLUMN_SHARDED:
        return jnp.transpose(gathered, (1, 0, 2)).reshape(r, N_DEV * c)
    return gathered.reshape(N_DEV * r, c)


def _blocks_from_full(name, full):
    r, c = full.shape
    if name in COLUMN_SHARDED:
        return jnp.transpose(full.reshape(r, N_DEV, c // N_DEV), (1, 0, 2))
    return full.reshape(N_DEV, r // N_DEV, c)


def kernel(x, norm_mix_pre, norm_mix_post, norm_mlp_pre, norm_mlp_post, w_in, rel_bias, sinks, lam_re, lam_im, log_dt, b_re, b_im, c_re, c_im, d_skip, w_glu, w_attn_branch, w_ssm_branch, w_out, w_ff_in, w_ff_out, loss_target, m_norm_mix_pre, m_norm_mix_post, m_norm_mlp_pre, m_norm_mlp_post, m_w_in, m_rel_bias, m_sinks, m_lam_re, m_lam_im, m_log_dt, m_b_re, m_b_im, m_c_re, m_c_im, m_d_skip, m_w_glu, m_w_attn_branch, m_w_ssm_branch, m_w_out, m_w_ff_in, m_w_ff_out, v_norm_mix_pre, v_norm_mix_post, v_norm_mlp_pre, v_norm_mlp_post, v_w_in, v_rel_bias, v_sinks, v_lam_re, v_lam_im, v_log_dt, v_b_re, v_b_im, v_c_re, v_c_im, v_d_skip, v_w_glu, v_w_attn_branch, v_w_ssm_branch, v_w_out, v_w_ff_in, v_w_ff_out):
    args = dict(locals())
    w = {n: args[n] for n in ALL_WEIGHTS}
    m = {n: args["m_" + n] for n in ALL_WEIGHTS}
    v = {n: args["v_" + n] for n in ALL_WEIGHTS}
    core = lax.axis_index("c").astype(jnp.int32).reshape(1)
    chip = (2 * lax.axis_index("x") + lax.axis_index("y")).astype(jnp.int32).reshape(1)
    xs, target = x[0], loss_target[0]
    t = _tiles(xs.shape[0])
    local = lambda d, n: d[n][0].T if n == "w_in" else d[n][0]
    gather_in = _gather_carry([local(w, "w_in").astype(BF16)], pass_on=False)
    gather_in_flight, token = _exchange_start(gather_in, "gather_w_in_start")
    one = token[0:1, 0:1] + 1.0
    shard = {n: (local(w, n) * one).astype(BF16) for n in BIG if n != "w_in"}
    shard["w_ff_in"] = shard["w_ff_in"].T
    view = lambda n, a: jnp.swapaxes(a, -1, -2) if n in SWAPPED_SMALL else a
    small = {n: (view(n, w[n]) if n == "rel_bias" else view(n, w[n])[0]) for n in SMALL}
    g1, g2, g3, g4 = (small[n].reshape(1, D_MODEL) for n in ("norm_mix_pre", "norm_mix_post", "norm_mlp_pre", "norm_mlp_post"))
    bucket = jnp.asarray(_bucket_table())
    rel_b, sink = small["rel_bias"], small["sinks"].reshape(1, N_HEADS)
    lam_r, lam_i = small["lam_re"].reshape(1, STATES), small["lam_im"].reshape(1, STATES)
    ldt_rep = jnp.repeat(small["log_dt"].reshape(N_GROUPS), N_STATE).reshape(1, STATES)
    bd_re, bd_im = _block_diag_in(small["b_re"] * one), _block_diag_in(small["b_im"] * one)
    cm_re, cm_im = _block_diag_out(small["c_re"] * one).astype(BF16), _block_diag_out(small["c_im"] * one).astype(BF16)
    dsk = small["d_skip"].reshape(1, SSM_W)
    a_re, a_im, bm_re, bm_im = _ssm_prep(lam_r, lam_i, ldt_rep, bd_re, bd_im)

    travelled_behind = list(shard.values()) + [a_re, a_im, bm_re, bm_im, cm_re, cm_im]
    _, landed = _exchange_wait(gather_in, gather_in_flight, travelled_behind, "gather_w_in_wait")
    (g_in,) = _gather_pass_on(landed, "gather_w_in_pass_on")
    wf_in = g_in.reshape(IN_W, D_MODEL)
    merge_names = ("w_glu", "w_attn_branch", "w_ssm_branch", "w_out")
    (q, k, vv, u, ga, gs, h), gathered = _in_proj_fwd(xs, g1, wf_in, t["proj"], _gather_carry([shard[n] for n in merge_names]))
    wf = {n: _full_from_gathered(n, g) for n, g in zip(merge_names, gathered)}
    (att,), (wf_ff_in,) = _attn_fwd(q, k, vv, bucket, rel_b, sink, _gather_carry([shard["w_ff_in"]]))
    (y, h_re, h_im, in_re, in_im), (wf_ff_out,) = _ssm_fwd(
        u, a_re, a_im, bm_re, bm_im, cm_re, cm_im, dsk, t["ssm_chunk"], _gather_carry([shard["w_ff_out"]]))
    x1, o, h2 = _merge_fwd(xs, y, att, ga, gs, g2, g3, wf["w_glu"], wf["w_ssm_branch"], wf["w_attn_branch"], wf["w_out"],
                           t["merge"])
    a, dfo, dx2, loss_blk, dg4 = _mlp_fwd(h2, x1, target, g4, wf_ff_in, wf_ff_out, t["mlp_fwd"])

    groups = {"ff": 4, "merge": 1, "w_in": 2}

    def add_sibling(group, blocks, received):
        return _add_sibling(blocks, received, jnp.concatenate([core, chip]), groups[group], "add_sibling_" + group)

    ff_names = ("w_ff_in", "w_ff_out")
    dw_ff_in, dw_ff_out, da = _mlp_weight_grads(dfo, a, h2, wf_ff_out, t["mlp_bwd"])
    dh2 = _mlp_input_grad(da, wf_ff_in.reshape(D_FF, D_MODEL), t["mlp_bwd"])
    ff_blocks = [dw_ff_in, dw_ff_out]
    (dx1, dgates, datt, dy, dw_glu, dw_ssm, dw_attn, dw_out, dg2, dg3), ff_recv = _merge_bwd(
        dh2, dx2, x1, o, y, att, ga, gs, g2, g3, wf["w_glu"], wf["w_ssm_branch"], wf["w_attn_branch"], wf["w_out"],
        t["merge_bwd"], _sibling_carry(ff_blocks))
    ff_sums, ff_sums_bf = add_sibling("ff", ff_blocks, ff_recv)
    merge_blocks = [_blocks_from_full(n, g) for n, g in zip(merge_names, (dw_glu, dw_attn, dw_ssm, dw_out))]
    (du, dbm_re, dbm_im, dcm_re, dcm_im, da_re, da_im, dd_skip), carried = _ssm_bwd(
        dy, u, h_re, h_im, in_re, in_im, a_re, a_im, bm_re, bm_im, cm_re, cm_im, dsk, t["ssm_chunk"],
        _join(_chips_carry(ff_sums_bf), _sibling_carry(merge_blocks)))
    ff_from_chips, merge_recv = carried[:2], carried[2:]
    merge_sums, merge_sums_bf = add_sibling("merge", merge_blocks, merge_recv)
    dbd_re, dbd_im, dlam_re, dlam_im, dldt_rep = _ssm_prep_bwd(lam_r, lam_i, ldt_rep, bd_re, bd_im, dbm_re, dbm_im, da_re, da_im)
    dlog_dt = _group_sum(dldt_rep.reshape(N_GROUPS, N_STATE))
    shapes = {n: view(n, w[n]).shape for n in SMALL}
    shapes["loss"] = (1,)
    small_grads = dict(
        norm_mix_post=dg2, norm_mlp_pre=dg3, norm_mlp_post=dg4, lam_re=dlam_re, lam_im=dlam_im, log_dt=dlog_dt,
        b_re=_block_diag_in_grad(dbd_re), b_im=_block_diag_in_grad(dbd_im),
        c_re=_block_diag_out_grad(dcm_re), c_im=_block_diag_out_grad(dcm_im), d_skip=dd_skip)
    packed_early = _pack({n: small_grads[n].reshape(shapes[n]) for n in SMALL_BEFORE_ATTN_BWD}, SMALL_BEFORE_ATTN_BWD)
    (dq, dkv, attn_small), carried = _attn_bwd(
        q, k, vv, datt, bucket, rel_b, sink, _join(_chips_carry(merge_sums_bf), _gather_carry([packed_early])))
    merge_from_chips, partials_early = carried[:-1], carried[-1]

    dparts = (dq, dkv, du, dgates)
    dw_in_t = _in_proj_weight_grad(h, dparts)
    in_blocks = [dw_in_t.reshape(N_DEV, IN_W // N_DEV, D_MODEL)]
    n_tiles = xs.shape[0] // t["proj_bwd"]
    (grad_x, dg1), in_recv = _in_proj_input_grad(
        xs, g1, wf_in, dx1, dparts, t["proj_bwd"], 0, n_tiles, "in_proj_input_grad", _sibling_carry(in_blocks))
    late = dict(norm_mix_pre=dg1, rel_bias=attn_small[:, :N_BUCKETS, 0], sinks=attn_small[:, N_BUCKETS, 0], loss=loss_blk[0:1, 0])
    packed_late = _pack({n: late[n].reshape(shapes[n]) for n in SMALL_LATE}, SMALL_LATE)
    to_everyone = _everyone_carry([packed_late])
    in_sums, in_sums_bf = add_sibling("w_in", in_blocks, in_recv)
    to_chips = _chips_carry(in_sums_bf)
    started, chips_started = _exchange_start(_join(to_everyone, to_chips), "late_grads_and_w_in_chips_start")
    late_in_flight, in_flight = [[started[j] for j in js] for js in ((0, 1, 4, 6), (2, 3, 5, 7))]

    grads, deltas, new_m, new_v = {}, {}, {}, {}

    def adam_group(group, names, sums, received, after=()):
        outs = _adam_big(*[[local(d, n) for n in names] for d in (w, m, v)], sums, received, groups[group],
                         "adam_" + group, after)
        for store, vals in zip((grads, deltas, new_m, new_v), outs):
            store.update({n: (o.T if n == "w_in" else o)[None] for n, o in zip(names, vals)})

    adam_group("ff", ff_names, ff_sums, ff_from_chips, [chips_started])
    adam_group("merge", merge_names, merge_sums, merge_from_chips, [chips_started])

    grads.update(_unpack(_sum_partials(partials_early, "sum_small_grads", [chips_started]), shapes, SMALL_BEFORE_ATTN_BWD))
    (packed_late,), (late_received,) = _exchange_wait(
        to_everyone, late_in_flight, [new_v["w_ff_out"], new_v["w_out"]], "late_grads_wait")
    grads.update(_unpack(_sum_everyone(packed_late, late_received, 2 * chip + core, "sum_late_grads"), shapes, SMALL_LATE))
    loss = grads.pop("loss").reshape(())
    small_out = _adam_small(*[[view(n, d[n]) for n in SMALL] for d in (w, m, v)], [grads[n] for n in SMALL])
    for store, vals in zip((deltas, new_m, new_v), small_out):
        store.update(zip(SMALL, vals))
    for store in (grads, deltas, new_m, new_v):
        store.update({n: view(n, store[n]) for n in SWAPPED_SMALL})

    busy = [new_v["w_ff_out"], new_v["w_out"], deltas["norm_mix_pre"]]
    _, (in_from_chips,) = _exchange_wait(to_chips, in_flight, busy, "w_in_chips_wait")
    adam_group("w_in", ("w_in",), in_sums, [in_from_chips])

    return (loss, grad_x[None], *[grads[n] for n in ALL_WEIGHTS], *[deltas[n] for n in ALL_WEIGHTS],
            *[new_m[n] for n in ALL_WEIGHTS], *[new_v[n] for n in ALL_WEIGHTS])
```

```python
import math

import jax
import jax.numpy as jnp
import numpy as np
from jax import lax
from jax.experimental import pallas as pl
from jax.experimental.pallas import tpu as pltpu

F32 = jnp.float32
BF16 = jnp.bfloat16

D_MODEL = 1024
N_HEADS = 8
HEAD_DIM = 64
ATTN_W = 512
KV_W = 128
BLOCK = 128
N_BUCKETS = 32
SSM_W = 512
N_GROUPS = 32
N_STATE = 64
GROUP_CH = 16
STATES = N_GROUPS * N_STATE
D_FF = 4096
IN_W = 3328
SPLITS = (0, 512, 640, 768, 1280, 2304, 3328)
RMS_EPS = 1e-6
NEG_INF = -1e30
SUBLANES = 8
LANES = 128
SSM_LANE_BLOCK = 512
N_SSM_BLOCKS = STATES // SSM_LANE_BLOCK
GROUPS_PER_BLOCK = SSM_LANE_BLOCK // N_STATE
VMEM_BIG = 52 * 1024 * 1024
VMEM_MID = 40 * 1024 * 1024
VMEM_MAX = 60 * 1024 * 1024

ADAM_LR = 0.001
ADAM_B1 = 0.9
ADAM_B2 = 0.999
ADAM_EPS = 1e-08
ADAM_WD = 0.01
ADAM_STEP = 10

N_DEV = 8


def _dot(a, b):
    return jnp.dot(a, b, preferred_element_type=F32)


def _dot_nt(a, b):
    return lax.dot_general(a, b, (((1,), (1,)), ((), ())), preferred_element_type=F32)


def _dot_tn(a, b):
    return lax.dot_general(a, b, (((0,), (0,)), ((), ())), preferred_element_type=F32)


def _rms_scale(x):
    return lax.rsqrt(jnp.mean(x * x, axis=-1, keepdims=True) + RMS_EPS)


def _rms_bwd(dy, x, r, g):
    t = dy * g
    dx = r * t - x * (r * r * r) * jnp.mean(t * x, axis=-1, keepdims=True)
    dg = jnp.sum(dy * x * r, axis=0, keepdims=True)
    return dx, dg


def _const_spec(shape):
    nd = len(shape)
    return pl.BlockSpec(shape, lambda *_: (0,) * nd, pipeline_mode=pl.Buffered(1))


def _in_hbm(*arrays):
    return tuple(pltpu.with_memory_space_constraint(a, pltpu.HBM) for a in arrays)


def _hbm_out(shapes):
    if isinstance(shapes, (list, tuple)):
        return [_hbm_out(s) for s in shapes]
    return shapes if isinstance(shapes, pl.MemoryRef) else pltpu.HBM(shapes.shape, shapes.dtype)


def _whole(shape):
    nd = len(shape)
    return pl.BlockSpec(shape, lambda *_: (0,) * nd)


def _params(sem, vmem=None):
    return pltpu.CompilerParams(dimension_semantics=sem, vmem_limit_bytes=vmem)


MESH_IDS = pl.DeviceIdType.MESH
HBM_SPEC = pl.BlockSpec(memory_space=pl.ANY)


class _Carry:
    def __init__(self, inputs, out_shapes, sems, start, finish, middle=None):
        self.inputs, self.out_shapes, self.sems = list(inputs), list(out_shapes), list(sems)
        self.start, self.middle, self.finish = start, middle, finish


def _join(a, b):
    na_in, na_out, na_sem = len(a.inputs), len(a.out_shapes), len(a.sems)

    def both(phase):
        def run(ins, outs, sems):
            for carry, lo in ((a, True), (b, False)):
                part = (lambda seq, n: seq[:n] if lo else seq[n:])
                if getattr(carry, phase) is not None:
                    getattr(carry, phase)(part(ins, na_in), part(outs, na_out), part(sems, na_sem))
        return run

    middle = both("middle") if (a.middle or b.middle) else None
    return _Carry(a.inputs + b.inputs, a.out_shapes + b.out_shapes, a.sems + b.sems, both("start"), both("finish"), middle)


def _hosted_call(body, carry, edge, *, name, grid, in_specs, out_specs, out_shape, scratch_shapes, compiler_params, inputs):
    n_in, n_out = len(in_specs), len(out_specs)
    inputs = [a if s.memory_space == pltpu.SMEM else _in_hbm(a)[0] for a, s in zip(inputs, in_specs)]
    out_shape = _hbm_out(list(out_shape))
    if carry is None:
        outs = pl.pallas_call(body, name=name, grid=grid, in_specs=in_specs, out_specs=out_specs, out_shape=out_shape,
                              scratch_shapes=scratch_shapes, compiler_params=compiler_params)(*inputs)
        return list(outs), []
    c_in, c_out, c_sem = len(carry.inputs), len(carry.out_shapes), len(carry.sems)

    def wrapped(*refs):
        ins, refs = refs[:n_in], refs[n_in:]
        cins, refs = refs[:c_in], refs[c_in:]
        outs, refs = refs[:n_out], refs[n_out:]
        couts, refs = refs[:c_out], refs[c_out:]
        scratch, csems = refs[:len(refs) - c_sem], refs[len(refs) - c_sem:]
        first, middle, last = edge()

        @pl.when(first)
        def _():
            carry.start(cins, couts, csems)

        body(*ins, *outs, *scratch)

        if carry.middle is not None:
            @pl.when(middle)
            def _():
                carry.middle(cins, couts, csems)

        @pl.when(last)
        def _():
            carry.finish(cins, couts, csems)

    outs = pl.pallas_call(
        wrapped, name=name, grid=grid, in_specs=list(in_specs) + [HBM_SPEC] * c_in,
        out_specs=list(out_specs) + [HBM_SPEC] * c_out, out_shape=out_shape + _hbm_out(carry.out_shapes),
        scratch_shapes=list(scratch_shapes) + carry.sems, compiler_params=compiler_params)(*inputs, *_in_hbm(*carry.inputs))
    return list(outs[:n_out]), list(outs[n_out:])


def _pass_on_step(n_steps):
    return max(0, min((7 * n_steps) // 8, n_steps - 2))


def _edge_1d(n_steps, pass_on_last=False):
    middle = n_steps - 1 if pass_on_last else _pass_on_step(n_steps)
    return lambda: (pl.program_id(0) == 0, pl.program_id(0) == middle, pl.program_id(0) == n_steps - 1)


def _edge_2d(n0, n1):
    def edge():
        step = pl.program_id(0) * n1 + pl.program_id(1)
        return step == 0, step == _pass_on_step(n0 * n1), step == n0 * n1 - 1
    return edge


def _in_proj_fwd(x, g1, w_in_t, tile, carry=None):
    T = x.shape[0]

    def body(x_ref, g_ref, w_ref, q_ref, k_ref, v_ref, u_ref, ga_ref, gs_ref, h_ref):
        xv = x_ref[...]
        h = (xv * _rms_scale(xv) * g_ref[...]).astype(BF16)
        h_ref[...] = h
        outs = (q_ref, k_ref, v_ref, u_ref, ga_ref, gs_ref)
        for p, o_ref in enumerate(outs):
            o_ref[...] = _dot_nt(h, w_ref[SPLITS[p]:SPLITS[p + 1], :]).astype(o_ref.dtype)

    widths = [SPLITS[p + 1] - SPLITS[p] for p in range(6)] + [D_MODEL]
    dtypes = [BF16, BF16, BF16, F32, F32, F32, BF16]
    return _hosted_call(
        body, carry, _edge_1d(T // tile), name="in_proj_fwd", grid=(T // tile,),
        in_specs=[pl.BlockSpec((tile, D_MODEL), lambda i: (i, 0)), _const_spec((1, D_MODEL)), _const_spec((IN_W, D_MODEL))],
        out_specs=[pl.BlockSpec((tile, w), lambda i: (i, 0)) for w in widths],
        out_shape=[jax.ShapeDtypeStruct((T, w), dt) for w, dt in zip(widths, dtypes)],
        scratch_shapes=[], compiler_params=_params(("arbitrary",), VMEM_MID), inputs=(x, g1, w_in_t))


PROJ_PARTS = (512, 256, 512, 2048)
PROJ_GRAD_BLOCK = 256


def _in_proj_weight_grad(h, dparts):
    T = h.shape[0]
    blocks = [wd // PROJ_GRAD_BLOCK for wd in PROJ_PARTS]
    starts = [sum(blocks[:p]) for p in range(len(blocks))]

    def body(h_ref, *refs):
        part_refs, o_ref = refs[:-1], refs[-1]
        j = pl.program_id(0)
        for p_ref, start, count in zip(part_refs, starts, blocks):
            @pl.when((j >= start) & (j < start + count))
            def _(p_ref=p_ref):
                o_ref[...] = _dot_tn(p_ref[...], h_ref[...])

    def part_spec(start, count):
        return pl.BlockSpec((T, PROJ_GRAD_BLOCK), lambda j: (0, jnp.clip(j - start, 0, count - 1)))

    return pl.pallas_call(
        body, name="in_proj_weight_grad", grid=(sum(blocks),),
        in_specs=[_const_spec((T, D_MODEL))] + [part_spec(s, c) for s, c in zip(starts, blocks)],
        out_specs=pl.BlockSpec((PROJ_GRAD_BLOCK, D_MODEL), lambda j: (j, 0)),
        out_shape=_hbm_out(jax.ShapeDtypeStruct((IN_W, D_MODEL), F32)),
        compiler_params=_params(("arbitrary",), VMEM_MID),
    )(*_in_hbm(h, *dparts))


def _in_proj_input_grad(x, g1, w_in_t, dx1, dparts, tile, first_tile, n_tiles, name, carry=None):
    offsets = [sum(PROJ_PARTS[:p]) for p in range(len(PROJ_PARTS))]

    def body(x_ref, g_ref, w_ref, dx1_ref, *refs):
        part_refs, (gx_ref, dg_ref) = refs[:len(PROJ_PARTS)], refs[len(PROJ_PARTS):]
        i = pl.program_id(0)
        xv = x_ref[...]
        r = _rms_scale(xv)
        g = g_ref[...]
        dh = sum(_dot(p_ref[...], w_ref[off:off + wd, :]) for p_ref, off, wd in zip(part_refs, offsets, PROJ_PARTS))
        dxn, dg = _rms_bwd(dh, xv, r, g)
        gx_ref[...] = dx1_ref[...] + dxn

        @pl.when(i == 0)
        def _():
            dg_ref[...] = dg

        @pl.when(i > 0)
        def _():
            dg_ref[...] += dg

    tok = lambda wd: pl.BlockSpec((tile, wd), lambda i: (i + first_tile, 0))
    return _hosted_call(
        body, carry, _edge_1d(n_tiles), name=name, grid=(n_tiles,),
        in_specs=[tok(D_MODEL), _const_spec((1, D_MODEL)), _const_spec((IN_W, D_MODEL)), tok(D_MODEL)] + [tok(wd) for wd in PROJ_PARTS],
        out_specs=[pl.BlockSpec((tile, D_MODEL), lambda i: (i, 0)), pl.BlockSpec((1, D_MODEL), lambda i: (0, 0))],
        out_shape=[jax.ShapeDtypeStruct((n_tiles * tile, D_MODEL), F32), jax.ShapeDtypeStruct((1, D_MODEL), F32)],
        scratch_shapes=[], compiler_params=_params(("arbitrary",), VMEM_MID), inputs=(x, g1, w_in_t, dx1, *dparts))


def _bucket_table():
    qi = np.arange(BLOCK)[:, None]
    kj = np.arange(2 * BLOCK)[None, :]
    dist = qi + BLOCK - kj
    max_exact = N_BUCKETS // 2
    d = np.maximum(dist, 0)
    df = np.maximum(d, 1).astype(np.float32)
    large = max_exact + (np.log(df / np.float32(max_exact)) / np.float32(math.log(BLOCK / max_exact))
                         * np.float32(N_BUCKETS - max_exact)).astype(np.int32)
    large = np.minimum(large, N_BUCKETS - 1)
    bucket = np.where(d < max_exact, d, large)
    return np.where((dist >= 0) & (dist < BLOCK), bucket, -1).astype(np.int32)


def _build_bias(bucket_ref, rb_ref, bias_ref):
    bk = bucket_ref[...]
    for h in range(N_HEADS):
        def add(b, acc, h=h):
            return acc + jnp.where(bk == b, rb_ref[h, b], 0.0)
        bias_ref[h] = lax.fori_loop(0, N_BUCKETS, add, jnp.zeros((BLOCK, 2 * BLOCK), F32))


def _kv_variants(prev_ref, cur_ref):
    cat = jnp.concatenate([prev_ref[...], cur_ref[...]], axis=0)
    lo = lax.broadcasted_iota(jnp.int32, cat.shape, 1) < HEAD_DIM
    zero = jnp.zeros_like(cat)
    head0_lo = jnp.where(lo, cat, zero)
    head1_hi = jnp.where(lo, zero, cat)
    return ((head0_lo, pltpu.roll(head0_lo, HEAD_DIM, 1)), (pltpu.roll(head1_hi, HEAD_DIM, 1), head1_hi))


def _merge_kv_grads(g):
    lo = lax.broadcasted_iota(jnp.int32, g[0][0].shape, 1) < HEAD_DIM
    return jnp.where(lo, g[0][0] + pltpu.roll(g[0][1], HEAD_DIM, 1), g[1][1] + pltpu.roll(g[1][0], HEAD_DIM, 1))


def _head_lanes(h):
    return slice((h // 2) * LANES, (h // 2 + 1) * LANES)


def _attn_probs(q_ref, kvar, bias_ref, sk_ref, valid, s_ref):
    for h in range(N_HEADS):
        s_ref[h] = _dot_nt(q_ref[:, _head_lanes(h)], kvar[h // 4][h % 2])
    head = lax.broadcasted_iota(jnp.int32, (N_HEADS, 1, 1), 0)
    sink = jnp.zeros((N_HEADS, 1, 1), F32)
    for h in range(N_HEADS):
        sink = jnp.where(head == h, sk_ref[0, h], sink)
    s = jnp.where(valid[None], s_ref[...] * (HEAD_DIM ** -0.5) + bias_ref[...], NEG_INF)
    m = jnp.maximum(jnp.max(s, axis=-1, keepdims=True), sink)
    p = jnp.exp(s - m)
    e_sink = jnp.exp(sink - m)
    inv = 1.0 / (jnp.sum(p, axis=-1, keepdims=True) + e_sink)
    return p * inv, e_sink * inv


def _attn_valid(bucket_ref, n):
    col = lax.broadcasted_iota(jnp.int32, (BLOCK, 2 * BLOCK), 1)
    return (bucket_ref[...] >= 0) & ((n > 0) | (col >= BLOCK))


def _attn_fwd(q, k, v, bucket, rel_bias, sinks, carry=None):
    T = q.shape[0]
    nb = T // BLOCK

    def body(q_ref, kc_ref, kp_ref, vc_ref, vp_ref, bucket_ref, rb_ref, sk_ref, o_ref, bias_ref, s_ref, p_ref):
        n = pl.program_id(0)

        @pl.when(n == 0)
        def _():
            _build_bias(bucket_ref, rb_ref, bias_ref)

        kvar = _kv_variants(kp_ref, kc_ref)
        vvar = _kv_variants(vp_ref, vc_ref)
        pr, _ = _attn_probs(q_ref, kvar, bias_ref, sk_ref, _attn_valid(bucket_ref, n), s_ref)
        p_ref[...] = pr.astype(BF16)
        for m in range(N_HEADS // 2):
            acc = _dot(p_ref[2 * m], vvar[m // 2][0]) + _dot(p_ref[2 * m + 1], vvar[m // 2][1])
            o_ref[:, m * LANES:(m + 1) * LANES] = acc.astype(o_ref.dtype)

    cur = lambda w: pl.BlockSpec((BLOCK, w), lambda n: (n, 0))
    prev = lambda w: pl.BlockSpec((BLOCK, w), lambda n: (jnp.maximum(n - 1, 0), 0))
    smem = pl.BlockSpec(memory_space=pltpu.SMEM)
    return _hosted_call(
        body, carry, _edge_1d(nb, pass_on_last=True), name="attn_fwd", grid=(nb,),
        in_specs=[cur(ATTN_W), cur(KV_W), prev(KV_W), cur(KV_W), prev(KV_W), _const_spec((BLOCK, 2 * BLOCK)), smem, smem],
        out_specs=[cur(ATTN_W)],
        out_shape=[jax.ShapeDtypeStruct((T, ATTN_W), BF16)],
        scratch_shapes=[pltpu.VMEM((N_HEADS, BLOCK, 2 * BLOCK), F32), pltpu.VMEM((N_HEADS, BLOCK, 2 * BLOCK), F32),
                        pltpu.VMEM((N_HEADS, BLOCK, 2 * BLOCK), BF16)],
        compiler_params=_params(("arbitrary",)), inputs=(q, k, k, v, v, bucket, rel_bias, sinks))


ATTN_SMALL_ROWS = N_BUCKETS + SUBLANES


def _attn_bwd(q, k, v, datt, bucket, rel_bias, sinks, carry=None):
    T = q.shape[0]
    nb = T // BLOCK

    def body(q_ref, do_ref, kc_ref, kp_ref, vc_ref, vp_ref, bucket_ref, rb_ref, sk_ref,
             dq_ref, dkv_ref, small_ref, bias_ref, ds_sum_ref, dsink_ref, kcarry_ref, vcarry_ref,
             s_ref, dp_ref, p_ref, dsc_ref):
        n = pl.program_id(0)

        @pl.when(n == 0)
        def _():
            _build_bias(bucket_ref, rb_ref, bias_ref)
            ds_sum_ref[...] = jnp.zeros_like(ds_sum_ref)
            dsink_ref[...] = jnp.zeros_like(dsink_ref)
            kcarry_ref[...] = jnp.zeros_like(kcarry_ref)
            vcarry_ref[...] = jnp.zeros_like(vcarry_ref)

        @pl.when(n < nb)
        def _():
            kvar = _kv_variants(kp_ref, kc_ref)
            vvar = _kv_variants(vp_ref, vc_ref)
            pr, p_sink = _attn_probs(q_ref, kvar, bias_ref, sk_ref, _attn_valid(bucket_ref, n), s_ref)
            for h in range(N_HEADS):
                dp_ref[h] = _dot_nt(do_ref[:, _head_lanes(h)], vvar[h // 4][h % 2])
            dp = dp_ref[...]
            dsum = jnp.sum(pr * dp, axis=-1, keepdims=True)
            ds = pr * (dp - dsum)
            ds_sum_ref[...] += ds
            dsink_ref[...] -= jnp.sum(p_sink * dsum, axis=1, keepdims=True)
            dsc_ref[...] = (ds * (HEAD_DIM ** -0.5)).astype(BF16)
            p_ref[...] = pr.astype(BF16)
            for m in range(N_HEADS // 2):
                dqm = _dot(dsc_ref[2 * m], kvar[m // 2][0]) + _dot(dsc_ref[2 * m + 1], kvar[m // 2][1])
                dq_ref[:, m * LANES:(m + 1) * LANES] = dqm.astype(dq_ref.dtype)
            dk_var = [[None, None], [None, None]]
            dv_var = [[None, None], [None, None]]
            for kvh in range(2):
                for e in range(2):
                    heads = [h for h in range(N_HEADS) if h // 4 == kvh and h % 2 == e]
                    dk_var[kvh][e] = sum(_dot_tn(dsc_ref[h], q_ref[:, _head_lanes(h)]) for h in heads)
                    dv_var[kvh][e] = sum(_dot_tn(p_ref[h], do_ref[:, _head_lanes(h)]) for h in heads)
            dk_cat = _merge_kv_grads(dk_var)
            dv_cat = _merge_kv_grads(dv_var)

            @pl.when(n > 0)
            def _():
                dkv_ref[:, :KV_W] = (kcarry_ref[...] + dk_cat[:BLOCK]).astype(BF16)
                dkv_ref[:, KV_W:] = (vcarry_ref[...] + dv_cat[:BLOCK]).astype(BF16)

            kcarry_ref[...] = dk_cat[BLOCK:]
            vcarry_ref[...] = dv_cat[BLOCK:]

        @pl.when(n == nb)
        def _():
            dkv_ref[:, :KV_W] = kcarry_ref[...].astype(BF16)
            dkv_ref[:, KV_W:] = vcarry_ref[...].astype(BF16)
            bk = bucket_ref[...]
            row = lax.broadcasted_iota(jnp.int32, (N_HEADS, ATTN_SMALL_ROWS, LANES), 1)

            def add(b, acc):
                masked = jnp.where((bk == b)[None], ds_sum_ref[...], 0.0)
                val = jnp.sum(jnp.sum(masked, axis=1, keepdims=True), axis=2, keepdims=True)
                return acc + jnp.where(row == b, val, 0.0)

            small_ref[...] = lax.fori_loop(0, N_BUCKETS, add, jnp.where(row == N_BUCKETS, dsink_ref[...], 0.0))

    last = nb - 1
    cur = lambda w: pl.BlockSpec((BLOCK, w), lambda n: (jnp.minimum(n, last), 0))
    prev = lambda w: pl.BlockSpec((BLOCK, w), lambda n: (jnp.clip(n - 1, 0, last), 0))
    smem = pl.BlockSpec(memory_space=pltpu.SMEM)
    return _hosted_call(
        body, carry, _edge_1d(nb + 1), name="attn_bwd", grid=(nb + 1,),
        in_specs=[cur(ATTN_W), cur(ATTN_W), cur(KV_W), prev(KV_W), cur(KV_W), prev(KV_W),
                  _const_spec((BLOCK, 2 * BLOCK)), smem, smem],
        out_specs=[cur(ATTN_W), prev(2 * KV_W), pl.BlockSpec((N_HEADS, ATTN_SMALL_ROWS, LANES), lambda n: (0, 0, 0))],
        out_shape=[jax.ShapeDtypeStruct((T, ATTN_W), BF16), jax.ShapeDtypeStruct((T, 2 * KV_W), BF16),
                   jax.ShapeDtypeStruct((N_HEADS, ATTN_SMALL_ROWS, LANES), F32)],
        scratch_shapes=[pltpu.VMEM((N_HEADS, BLOCK, 2 * BLOCK), F32), pltpu.VMEM((N_HEADS, BLOCK, 2 * BLOCK), F32),
                        pltpu.VMEM((N_HEADS, 1, 1), F32), pltpu.VMEM((BLOCK, KV_W), F32), pltpu.VMEM((BLOCK, KV_W), F32),
                        pltpu.VMEM((N_HEADS, BLOCK, 2 * BLOCK), F32), pltpu.VMEM((N_HEADS, BLOCK, 2 * BLOCK), F32),
                        pltpu.VMEM((N_HEADS, BLOCK, 2 * BLOCK), BF16), pltpu.VMEM((N_HEADS, BLOCK, 2 * BLOCK), BF16)],
        compiler_params=_params(("arbitrary",)), inputs=(q, datt, k, k, v, v, bucket, rel_bias, sinks))


SCAN_UNROLL = 4


def _cmul(ar, ai, br, bi):
    return ar * br - ai * bi, ar * bi + ai * br


def _cmul_conj(ar, ai, br, bi):
    return ar * br + ai * bi, ar * bi - ai * br


def _ssm_discretize(lr, li, ldt):
    dt = jnp.exp(ldt)
    mag = jnp.exp(lr * dt)
    ab_re = mag * jnp.cos(li * dt)
    ab_im = mag * jnp.sin(li * dt)
    nr = ab_re - 1.0
    den = lr * lr + li * li
    f_re = (nr * lr + ab_im * li) / den
    f_im = (ab_im * lr - nr * li) / den
    return ab_re, ab_im, f_re, f_im


def _ssm_prep(lam_re, lam_im, ldt_rep, bd_re, bd_im):
    def body(lr_ref, li_ref, ldt_ref, bdr_ref, bdi_ref, ar_ref, ai_ref, br_ref, bi_ref):
        ab_re, ab_im, f_re, f_im = _ssm_discretize(lr_ref[...], li_ref[...], ldt_ref[...])
        ar_ref[...] = ab_re
        ai_ref[...] = ab_im
        bdr, bdi = bdr_ref[0], bdi_ref[0]
        br_ref[0] = (bdr * f_re - bdi * f_im).astype(BF16)
        bi_ref[0] = (bdi * f_re + bdr * f_im).astype(BF16)

    row = pl.BlockSpec((1, SSM_LANE_BLOCK), lambda j: (0, j))
    mat = pl.BlockSpec((1, LANES, SSM_LANE_BLOCK), lambda j: (j, 0, 0))
    return pl.pallas_call(
        body, name="ssm_prep", grid=(N_SSM_BLOCKS,),
        in_specs=[row, row, row, mat, mat], out_specs=[row, row, mat, mat],
        out_shape=[jax.ShapeDtypeStruct((1, STATES), F32)] * 2 + [jax.ShapeDtypeStruct((N_SSM_BLOCKS, LANES, SSM_LANE_BLOCK), BF16)] * 2,
        compiler_params=_params(("arbitrary",)),
    )(*_in_hbm(lam_re, lam_im, ldt_rep, bd_re, bd_im))


def _ssm_prep_bwd(lam_re, lam_im, ldt_rep, bd_re, bd_im, dbr, dbi, da_re, da_im):
    def body(lr_ref, li_ref, ldt_ref, bdr_ref, bdi_ref, dbr_ref, dbi_ref, dar_ref, dai_ref,
             dbdr_ref, dbdi_ref, dlr_ref, dli_ref, dldt_ref):
        lr, li, ldt = lr_ref[...], li_ref[...], ldt_ref[...]
        (_, _, f_re, f_im), vjp = jax.vjp(_ssm_discretize, lr, li, ldt)
        bdr, bdi, gbr, gbi = bdr_ref[0], bdi_ref[0], dbr_ref[0], dbi_ref[0]
        dbdr_ref[0] = gbr * f_re + gbi * f_im
        dbdi_ref[0] = gbi * f_re - gbr * f_im
        df_re = jnp.sum(gbr * bdr + gbi * bdi, axis=0, keepdims=True)
        df_im = jnp.sum(gbi * bdr - gbr * bdi, axis=0, keepdims=True)
        dlr, dli, dldt = vjp((dar_ref[...], dai_ref[...], df_re, df_im))
        dlr_ref[...] = dlr
        dli_ref[...] = dli
        dldt_ref[...] = dldt

    row = pl.BlockSpec((1, SSM_LANE_BLOCK), lambda j: (0, j))
    mat = pl.BlockSpec((1, LANES, SSM_LANE_BLOCK), lambda j: (j, 0, 0))
    mat_shape = jax.ShapeDtypeStruct((N_SSM_BLOCKS, LANES, SSM_LANE_BLOCK), F32)
    row_shape = jax.ShapeDtypeStruct((1, STATES), F32)
    return pl.pallas_call(
        body, name="ssm_prep_bwd", grid=(N_SSM_BLOCKS,),
        in_specs=[row, row, row, mat, mat, mat, mat, row, row], out_specs=[mat, mat, row, row, row],
        out_shape=[mat_shape, mat_shape, row_shape, row_shape, row_shape],
        compiler_params=_params(("arbitrary",)),
    )(*_in_hbm(lam_re, lam_im, ldt_rep, bd_re, bd_im, dbr, dbi, da_re, da_im))


def _group_sum(x):
    def body(x_ref, o_ref):
        o_ref[...] = jnp.sum(x_ref[...], axis=1, keepdims=True)
    return pl.pallas_call(body, name="ssm_group_sum", grid=(1,), in_specs=[_whole(x.shape)], out_specs=_whole((N_GROUPS, 1)),
                          out_shape=jax.ShapeDtypeStruct((N_GROUPS, 1), F32))(*_in_hbm(x))


def _power_table(ar, ai, p_re_ref, p_im_ref, steps):
    shape = (SUBLANES, SSM_LANE_BLOCK)
    p_re_ref[0:SUBLANES] = jnp.broadcast_to(ar, shape)
    p_im_ref[0:SUBLANES] = jnp.broadcast_to(ai, shape)
    m = 1
    while m < steps:
        rows = m * SUBLANES
        top_re = p_re_ref[rows - SUBLANES:rows]
        top_im = p_im_ref[rows - SUBLANES:rows]
        cur_re = p_re_ref[0:rows].reshape(m, SUBLANES, SSM_LANE_BLOCK)
        cur_im = p_im_ref[0:rows].reshape(m, SUBLANES, SSM_LANE_BLOCK)
        nxt_re, nxt_im = _cmul(cur_re, cur_im, top_re[None], top_im[None])
        p_re_ref[rows:2 * rows] = nxt_re.reshape(rows, SSM_LANE_BLOCK)
        p_im_ref[rows:2 * rows] = nxt_im.reshape(rows, SSM_LANE_BLOCK)
        m *= 2


def _to_segments(src_ref, dst_ref, steps):
    for s in range(SUBLANES):
        dst_ref[pl.ds(s, steps, stride=SUBLANES), :] = src_ref[s * steps:(s + 1) * steps, :]


def _from_segments(src_ref, dst_ref, steps):
    for s in range(SUBLANES):
        dst_ref[s * steps:(s + 1) * steps, :] = src_ref[pl.ds(s, steps, stride=SUBLANES), :]


def _segment_carries(e_re, e_im, an_re, an_im, c_re, c_im, reverse):
    order = range(SUBLANES - 1, -1, -1) if reverse else range(SUBLANES)
    ins_re, ins_im = [None] * SUBLANES, [None] * SUBLANES
    for s in order:
        ins_re[s], ins_im[s] = c_re, c_im
        pr, pi = _cmul(an_re, an_im, c_re, c_im)
        c_re = e_re[s:s + 1] + pr
        c_im = e_im[s:s + 1] + pi
    return jnp.concatenate(ins_re, axis=0), jnp.concatenate(ins_im, axis=0), c_re, c_im


def _ssm_fwd(u, a_re, a_im, b_re, b_im, c_re, c_im, d_skip, chunk, carry=None):
    T = u.shape[0]
    nc = T // chunk
    steps = chunk // SUBLANES
    blk = SSM_LANE_BLOCK

    def body(u_ref, ar_ref, ai_ref, br_ref, bi_ref, cr_ref, ci_ref, dk_ref,
             y_ref, hr_ref, hi_ref, inr_ref, ini_ref, useg_ref, yseg_ref, pr_ref, pi_ref, carry_ref):
        c = pl.program_id(1)
        ar, ai = ar_ref[...], ai_ref[...]

        @pl.when(c == 0)
        def _():
            _power_table(ar, ai, pr_ref, pi_ref, steps)
            carry_ref[...] = jnp.zeros_like(carry_ref)

        _to_segments(u_ref, useg_ref, steps)
        ub = useg_ref[...].astype(BF16)
        hr_ref[...] = _dot(ub, br_ref[0])
        hi_ref[...] = _dot(ub, bi_ref[0])
        first = slice(0, SUBLANES)

        def scan(t4, prev):
            for j in range(SCAN_UNROLL):
                rows = pl.ds(pl.multiple_of((t4 * SCAN_UNROLL + j) * SUBLANES, SUBLANES), SUBLANES)
                pr, pi = _cmul(pr_ref[first, :], pi_ref[first, :], prev[0], prev[1])
                prev = (pr + hr_ref[rows, :], pi + hi_ref[rows, :])
                hr_ref[rows, :] = prev[0]
                hi_ref[rows, :] = prev[1]
            return prev

        zero = jnp.zeros((SUBLANES, blk), F32)
        lax.fori_loop(0, steps // SCAN_UNROLL, scan, (zero, zero))

        top = slice(chunk - SUBLANES, chunk)
        in_re, in_im, out_re, out_im = _segment_carries(
            hr_ref[top, :], hi_ref[top, :], pr_ref[top, :][0:1], pi_ref[top, :][0:1],
            carry_ref[0:1, :], carry_ref[1:2, :], reverse=False)
        carry_ref[0:1, :] = out_re
        carry_ref[1:2, :] = out_im
        inr_ref[...] = in_re
        ini_ref[...] = in_im

        def fix(t4, _):
            for j in range(SCAN_UNROLL):
                rows = pl.ds(pl.multiple_of((t4 * SCAN_UNROLL + j) * SUBLANES, SUBLANES), SUBLANES)
                fr, fi = _cmul(pr_ref[rows, :], pi_ref[rows, :], in_re, in_im)
                hr_ref[rows, :] += fr
                hi_ref[rows, :] += fi
            return 0

        lax.fori_loop(0, steps // SCAN_UNROLL, fix, 0)

        yseg_ref[...] = _dot(hr_ref[...].astype(BF16), cr_ref[0]) - _dot(hi_ref[...].astype(BF16), ci_ref[0])
        _from_segments(yseg_ref, y_ref, steps)
        y_ref[...] += dk_ref[...] * u_ref[...]

    row = pl.BlockSpec((1, blk), lambda j, c: (0, j))
    b_mat = pl.BlockSpec((1, LANES, blk), lambda j, c: (j, 0, 0))
    c_mat = pl.BlockSpec((1, blk, LANES), lambda j, c: (j, 0, 0))
    tok = pl.BlockSpec((chunk, LANES), lambda j, c: (c, j))
    state = pl.BlockSpec((chunk, blk), lambda j, c: (c, j))
    enter = pl.BlockSpec((SUBLANES, blk), lambda j, c: (c, j))
    return _hosted_call(
        body, carry, _edge_2d(N_SSM_BLOCKS, nc), name="ssm_fwd", grid=(N_SSM_BLOCKS, nc),
        in_specs=[tok, row, row, b_mat, b_mat, c_mat, c_mat, pl.BlockSpec((1, LANES), lambda j, c: (0, j))],
        out_specs=[tok, state, state, enter, enter],
        out_shape=[jax.ShapeDtypeStruct((T, SSM_W), F32), jax.ShapeDtypeStruct((T, STATES), F32),
                   jax.ShapeDtypeStruct((T, STATES), F32), jax.ShapeDtypeStruct((nc * SUBLANES, STATES), F32),
                   jax.ShapeDtypeStruct((nc * SUBLANES, STATES), F32)],
        scratch_shapes=[pltpu.VMEM((chunk, LANES), F32), pltpu.VMEM((chunk, LANES), F32),
                        pltpu.VMEM((chunk, blk), F32), pltpu.VMEM((chunk, blk), F32), pltpu.VMEM((SUBLANES, blk), F32)],
        compiler_params=_params(("arbitrary", "arbitrary"), VMEM_MID),
        inputs=(u, a_re, a_im, b_re, b_im, c_re, c_im, d_skip))


def _ssm_bwd(dy, u, h_re, h_im, in_re, in_im, a_re, a_im, b_re, b_im, c_re, c_im, d_skip, chunk, carry=None):
    T = u.shape[0]
    nc = T // chunk
    steps = chunk // SUBLANES
    blk = SSM_LANE_BLOCK

    def body(dy_ref, u_ref, hr_ref, hi_ref, inr_ref, ini_ref, ar_ref, ai_ref, br_ref, bi_ref, cr_ref, ci_ref, dk_ref,
             du_ref, dbr_ref, dbi_ref, dcr_ref, dci_ref, dar_ref, dai_ref, ddk_ref,
             dyseg_ref, useg_ref, duseg_ref, gr_ref, gi_ref, pr_ref, pi_ref, carry_ref, accr_ref, acci_ref):
        c = pl.program_id(1)
        ar, ai = ar_ref[...], ai_ref[...]

        @pl.when(c == 0)
        def _():
            _power_table(ar, ai, pr_ref, pi_ref, steps)
            carry_ref[...] = jnp.zeros_like(carry_ref)
            accr_ref[...] = jnp.zeros_like(accr_ref)
            acci_ref[...] = jnp.zeros_like(acci_ref)

        _to_segments(dy_ref, dyseg_ref, steps)
        _to_segments(u_ref, useg_ref, steps)
        dyb = dyseg_ref[...].astype(BF16)
        ub = useg_ref[...].astype(BF16)
        gr_ref[...] = _dot_nt(dyb, cr_ref[0])
        gi_ref[...] = -_dot_nt(dyb, ci_ref[0])
        dcr = _dot_tn(hr_ref[...].astype(BF16), dyb)
        dci = -_dot_tn(hi_ref[...].astype(BF16), dyb)
        ddk = jnp.sum(dy_ref[...] * u_ref[...], axis=0, keepdims=True)

        first = slice(0, SUBLANES)

        def scan(k4, nxt):
            for j in range(SCAN_UNROLL):
                t = steps - 1 - (k4 * SCAN_UNROLL + j)
                rows = pl.ds(pl.multiple_of(t * SUBLANES, SUBLANES), SUBLANES)
                pr, pi = _cmul_conj(pr_ref[first, :], pi_ref[first, :], nxt[0], nxt[1])
                nxt = (pr + gr_ref[rows, :], pi + gi_ref[rows, :])
                gr_ref[rows, :] = nxt[0]
                gi_ref[rows, :] = nxt[1]
            return nxt

        top = slice(chunk - SUBLANES, chunk)
        zero = jnp.zeros((SUBLANES, blk), F32)
        lax.fori_loop(0, steps // SCAN_UNROLL, scan, (zero, zero))

        gin_re, gin_im, out_re, out_im = _segment_carries(
            gr_ref[0:SUBLANES, :], gi_ref[0:SUBLANES, :], pr_ref[top, :][0:1], -pi_ref[top, :][0:1],
            carry_ref[0:1, :], carry_ref[1:2, :], reverse=True)
        carry_ref[0:1, :] = out_re
        carry_ref[1:2, :] = out_im

        def fix_row(rows, prow, hp_re, hp_im, acc):
            fr, fi = _cmul_conj(pr_ref[prow, :], pi_ref[prow, :], gin_re, gin_im)
            g_re = gr_ref[rows, :] + fr
            g_im = gi_ref[rows, :] + fi
            gr_ref[rows, :] = g_re
            gi_ref[rows, :] = g_im
            return acc[0] + g_re * hp_re + g_im * hp_im, acc[1] + g_im * hp_re - g_re * hp_im

        def fix_at(t, acc):
            aligned = (lambda r: r * SUBLANES) if isinstance(t, int) else (lambda r: pl.multiple_of(r * SUBLANES, SUBLANES))
            rows, before, prow = (pl.ds(aligned(r), SUBLANES) for r in (t, t - 1, steps - 1 - t))
            return fix_row(rows, prow, hr_ref[before, :], hi_ref[before, :], acc)

        def fix(t4, acc):
            for j in range(SCAN_UNROLL):
                acc = fix_at(t4 * SCAN_UNROLL + j, acc)
            return acc

        acc = fix_row(first, top, inr_ref[...], ini_ref[...], (accr_ref[...], acci_ref[...]))
        for t in range(1, SCAN_UNROLL):
            acc = fix_at(t, acc)
        acc_re, acc_im = lax.fori_loop(1, steps // SCAN_UNROLL, fix, acc)
        accr_ref[...] = acc_re
        acci_ref[...] = acc_im

        gbr = gr_ref[...].astype(BF16)
        gbi = gi_ref[...].astype(BF16)
        duseg_ref[...] = _dot_nt(gbr, br_ref[0]) + _dot_nt(gbi, bi_ref[0])
        _from_segments(duseg_ref, dyseg_ref, steps)
        du_ref[...] = (dyseg_ref[...] + dk_ref[...] * dy_ref[...]).astype(BF16)
        dbr = _dot_tn(ub, gbr)
        dbi = _dot_tn(ub, gbi)

        @pl.when(c == 0)
        def _():
            dbr_ref[0] = dbr
            dbi_ref[0] = dbi
            dcr_ref[0] = dcr
            dci_ref[0] = dci
            ddk_ref[...] = ddk

        @pl.when(c > 0)
        def _():
            dbr_ref[0] += dbr
            dbi_ref[0] += dbi
            dcr_ref[0] += dcr
            dci_ref[0] += dci
            ddk_ref[...] += ddk

        @pl.when(c == nc - 1)
        def _():
            dar_ref[...] = jnp.sum(acc_re, axis=0, keepdims=True)
            dai_ref[...] = jnp.sum(acc_im, axis=0, keepdims=True)

    rev = lambda c: nc - 1 - c
    row = pl.BlockSpec((1, blk), lambda j, c: (0, j))
    b_mat = pl.BlockSpec((1, LANES, blk), lambda j, c: (j, 0, 0))
    c_mat = pl.BlockSpec((1, blk, LANES), lambda j, c: (j, 0, 0))
    tok = pl.BlockSpec((chunk, LANES), lambda j, c: (rev(c), j))
    state = pl.BlockSpec((chunk, blk), lambda j, c: (rev(c), j))
    enter = pl.BlockSpec((SUBLANES, blk), lambda j, c: (rev(c), j))
    chan = pl.BlockSpec((1, LANES), lambda j, c: (0, j))
    f32 = lambda *s: jax.ShapeDtypeStruct(s, F32)
    return _hosted_call(
        body, carry, _edge_2d(N_SSM_BLOCKS, nc), name="ssm_bwd", grid=(N_SSM_BLOCKS, nc),
        in_specs=[tok, tok, state, state, enter, enter, row, row, b_mat, b_mat, c_mat, c_mat, chan],
        out_specs=[tok, b_mat, b_mat, c_mat, c_mat, row, row, chan],
        out_shape=[jax.ShapeDtypeStruct((T, SSM_W), BF16), f32(N_SSM_BLOCKS, LANES, blk), f32(N_SSM_BLOCKS, LANES, blk),
                   f32(N_SSM_BLOCKS, blk, LANES), f32(N_SSM_BLOCKS, blk, LANES), f32(1, STATES), f32(1, STATES), f32(1, SSM_W)],
        scratch_shapes=[pltpu.VMEM((chunk, LANES), F32), pltpu.VMEM((chunk, LANES), F32), pltpu.VMEM((chunk, LANES), F32),
                        pltpu.VMEM((chunk, blk), F32), pltpu.VMEM((chunk, blk), F32),
                        pltpu.VMEM((chunk, blk), F32), pltpu.VMEM((chunk, blk), F32),
                        pltpu.VMEM((SUBLANES, blk), F32), pltpu.VMEM((SUBLANES, blk), F32), pltpu.VMEM((SUBLANES, blk), F32)],
        compiler_params=_params(("arbitrary", "arbitrary"), VMEM_BIG),
        inputs=(dy, u, h_re, h_im, in_re, in_im, a_re, a_im, b_re, b_im, c_re, c_im, d_skip))


def _merge_forward(y, att, ga, gs, w_glu, w_ssm, w_attn):
    z = jax.nn.gelu(y)
    zb = z.astype(BF16)
    gl = jax.nn.sigmoid(_dot(zb, w_glu))
    z2b = (z * gl).astype(BF16)
    y_ssm = _dot(z2b, w_ssm)
    y_attn = _dot(att, w_attn)
    sa = jax.nn.sigmoid(ga)
    ss = jax.nn.sigmoid(gs)
    merged = (sa * y_attn + ss * y_ssm).astype(BF16)
    return z, zb, gl, z2b, y_ssm, y_attn, sa, ss, merged


def _merge_fwd(x, y, att, ga, gs, g2, g3, w_glu, w_ssm, w_attn, w_out, tile):
    T = x.shape[0]

    def body(x_hbm, y_hbm, att_hbm, ga_hbm, gs_hbm, g2_ref, g3_ref, wg_ref, ws_ref, wa_ref, wo_ref, x1_hbm, o_hbm, h2_hbm):
        def step(x_ref, y_ref, att_ref, ga_ref, gs_ref, x1_ref, o_ref, h2_ref):
            merged = _merge_forward(y_ref[...], att_ref[...], ga_ref[...], gs_ref[...], wg_ref[...], ws_ref[...], wa_ref[...])[-1]
            o = _dot(merged, wo_ref[...])
            x1 = x_ref[...] + o * _rms_scale(o) * g2_ref[...]
            o_ref[...] = o
            x1_ref[...] = x1
            h2_ref[...] = (x1 * _rms_scale(x1) * g3_ref[...]).astype(BF16)

        tok = lambda w, **mode: pl.BlockSpec((tile, w), lambda i: (i, 0), **mode)
        tok_in = lambda w: tok(w, pipeline_mode=pl.Buffered(3))
        pltpu.emit_pipeline(
            step, grid=(T // tile,),
            in_specs=[tok_in(D_MODEL), tok_in(SSM_W), tok_in(ATTN_W), tok_in(D_MODEL), tok_in(D_MODEL)],
            out_specs=[tok(D_MODEL), tok(D_MODEL), tok(D_MODEL)],
        )(x_hbm, y_hbm, att_hbm, ga_hbm, gs_hbm, x1_hbm, o_hbm, h2_hbm)

    in_vmem = pl.BlockSpec(memory_space=pltpu.VMEM)
    return pl.pallas_call(
        body, name="merge_fwd", in_specs=[HBM_SPEC] * 5 + [in_vmem] * 6, out_specs=[HBM_SPEC] * 3,
        out_shape=_hbm_out([jax.ShapeDtypeStruct((T, D_MODEL), F32), jax.ShapeDtypeStruct((T, D_MODEL), F32),
                            jax.ShapeDtypeStruct((T, D_MODEL), BF16)]),
        compiler_params=pltpu.CompilerParams(vmem_limit_bytes=VMEM_BIG),
    )(*_in_hbm(x, y, att, ga, gs), g2, g3, w_glu, w_ssm, w_attn, w_out)


def _merge_bwd(dh2, dx2, x1, o, y, att, ga, gs, g2, g3, w_glu, w_ssm, w_attn, w_out, tile, carry=None):
    T = x1.shape[0]
    n_steps = T // tile

    group = min(2, n_steps)
    staged_widths = (D_MODEL, D_MODEL, ATTN_W, D_MODEL, SSM_W, D_MODEL, SSM_W, SSM_W)

    def body(dh2_ref, dx2_ref, x1_ref, o_ref, y_ref, att_ref, ga_ref, gs_ref, g2_ref, g3_ref, wg_ref, ws_ref, wa_ref, wo_ref,
             dx1_ref, dgates_ref, datt_ref, dy_ref, dwg_hbm, dws_hbm, dwa_hbm, dwo_hbm, dg2_ref, dg3_ref,
             awg_ref, aws_ref, awa_ref, awo_ref, *staged):
        i = pl.program_id(0)
        x1v, ov = x1_ref[...], o_ref[...]
        dxn, dg3 = _rms_bwd(dh2_ref[...], x1v, _rms_scale(x1v), g3_ref[...])
        dx1 = dx2_ref[...] + dxn
        dx1_ref[...] = dx1
        do, dg2 = _rms_bwd(dx1, ov, _rms_scale(ov), g2_ref[...])
        dob = do.astype(BF16)

        yv = y_ref[...]
        att = att_ref[...]
        z, zb, gl, z2b, y_ssm, y_attn, sa, ss, merged = _merge_forward(
            yv, att, ga_ref[...], gs_ref[...], wg_ref[...], ws_ref[...], wa_ref[...])
        dmerged = _dot_nt(dob, wo_ref[...])
        dya = (dmerged * sa).astype(BF16)
        dys = (dmerged * ss).astype(BF16)
        dgates_ref[:, :D_MODEL] = (dmerged * y_attn * sa * (1.0 - sa)).astype(BF16)
        dgates_ref[:, D_MODEL:] = (dmerged * y_ssm * ss * (1.0 - ss)).astype(BF16)
        datt_ref[...] = _dot_nt(dya, wa_ref[...]).astype(BF16)
        dz2 = _dot_nt(dys, ws_ref[...])
        dpre = (dz2 * z * gl * (1.0 - gl)).astype(BF16)
        dz = dz2 * gl + _dot_nt(dpre, wg_ref[...])
        _, gelu_vjp = jax.vjp(jax.nn.gelu, yv)
        dy_ref[...] = gelu_vjp(dz)[0]

        part = pl.ds(pl.multiple_of((i % group) * tile, tile), tile)
        for ref, val in zip(staged, (merged, dob, att, dya, z2b, dys, zb, dpre)):
            ref[part, :] = val

        @pl.when(i == 0)
        def _():
            dg2_ref[...] = dg2
            dg3_ref[...] = dg3

        @pl.when(i > 0)
        def _():
            dg2_ref[...] += dg2
            dg3_ref[...] += dg3

        def weight_grads():
            s_merged, s_dob, s_att, s_dya, s_z2b, s_dys, s_zb, s_dpre = (ref[...] for ref in staged)
            return ((awo_ref, _dot_tn(s_merged, s_dob)), (awa_ref, _dot_tn(s_att, s_dya)),
                    (aws_ref, _dot_tn(s_z2b, s_dys)), (awg_ref, _dot_tn(s_zb, s_dpre)))

        @pl.when(i == group - 1)
        def _():
            for ref, val in weight_grads():
                ref[...] = val

        @pl.when((i % group == group - 1) & (i > group - 1))
        def _():
            for ref, val in weight_grads():
                ref[...] += val

        @pl.when(i == n_steps - 1)
        def _():
            pltpu.sync_copy(awg_ref, dwg_hbm)
            pltpu.sync_copy(aws_ref, dws_hbm)
            pltpu.sync_copy(awa_ref, dwa_hbm)
            pltpu.sync_copy(awo_ref, dwo_hbm)

    tok = lambda w: pl.BlockSpec((tile, w), lambda i: (i, 0))
    vec = _const_spec((1, D_MODEL))
    any_ = pl.BlockSpec(memory_space=pl.ANY)
    vec_out = pl.BlockSpec((1, D_MODEL), lambda i: (0, 0))
    f32 = lambda *s: jax.ShapeDtypeStruct(s, F32)
    bf = lambda *s: jax.ShapeDtypeStruct(s, BF16)
    return _hosted_call(
        body, carry, _edge_1d(n_steps), name="merge_bwd", grid=(n_steps,),
        in_specs=[tok(D_MODEL), tok(D_MODEL), tok(D_MODEL), tok(D_MODEL), tok(SSM_W), tok(ATTN_W), tok(D_MODEL), tok(D_MODEL),
                  vec, vec, _const_spec((SSM_W, SSM_W)), _const_spec((SSM_W, D_MODEL)), _const_spec((ATTN_W, D_MODEL)),
                  _const_spec((D_MODEL, D_MODEL))],
        out_specs=[tok(D_MODEL), tok(2 * D_MODEL), tok(ATTN_W), tok(SSM_W), any_, any_, any_, any_, vec_out, vec_out],
        out_shape=[f32(T, D_MODEL), bf(T, 2 * D_MODEL), bf(T, ATTN_W), f32(T, SSM_W),
                   f32(SSM_W, SSM_W), f32(SSM_W, D_MODEL), f32(ATTN_W, D_MODEL), f32(D_MODEL, D_MODEL),
                   f32(1, D_MODEL), f32(1, D_MODEL)],
        scratch_shapes=[pltpu.VMEM((SSM_W, SSM_W), F32), pltpu.VMEM((SSM_W, D_MODEL), F32),
                        pltpu.VMEM((ATTN_W, D_MODEL), F32), pltpu.VMEM((D_MODEL, D_MODEL), F32)]
        + [pltpu.VMEM((group * tile, wd), BF16) for wd in staged_widths],
        compiler_params=_params(("arbitrary",), VMEM_BIG),
        inputs=(dh2, dx2, x1, o, y, att, ga, gs, g2, g3, w_glu, w_ssm, w_attn, w_out))


FF_SHARD = D_FF // N_DEV


def _mlp_fwd(h2, x1, target, g4, w_ff_in, w_ff_out, tile):
    T = h2.shape[0]
    col_chunk = 2 * FF_SHARD

    def body(h2_ref, x1_ref, tg_ref, g4_ref, wi_ref, wo_ref, a_ref, dfo_ref, dx2_ref, loss_ref, dg4_ref, rr_ref):
        i = pl.program_id(0)
        h2v = h2_ref[...]
        for c in range(D_FF // col_chunk):
            cols = slice(c * col_chunk, (c + 1) * col_chunk)
            a = _dot_nt(h2v, wi_ref[cols, :])
            a_ref[:, cols] = a.astype(BF16)
            ra = jnp.maximum(a, 0.0)
            rr_ref[:, cols] = (ra * ra).astype(BF16)
        f = _dot(rr_ref[...], wo_ref[...])
        r = _rms_scale(f)
        g = g4_ref[...]
        err = x1_ref[...] + f * r * g - tg_ref[...]
        dx2 = err * (1.0 / D_MODEL)
        dx2_ref[...] = dx2
        dfo, dg = _rms_bwd(dx2, f, r, g)
        dfo_ref[...] = dfo.astype(BF16)
        row = lax.broadcasted_iota(jnp.int32, (SUBLANES, LANES), 0)
        col = lax.broadcasted_iota(jnp.int32, (SUBLANES, LANES), 1)
        loss = jnp.where((row == 0) & (col == 0), (0.5 / D_MODEL) * jnp.sum(err * err), 0.0)

        @pl.when(i == 0)
        def _():
            loss_ref[...] = loss
            dg4_ref[...] = dg

        @pl.when(i > 0)
        def _():
            loss_ref[...] += loss
            dg4_ref[...] += dg

    tok = pl.BlockSpec((tile, D_MODEL), lambda i: (i, 0))
    return pl.pallas_call(
        body, name="mlp_fwd", grid=(T // tile,),
        in_specs=[tok, tok, tok, _const_spec((1, D_MODEL)), _const_spec((D_FF, D_MODEL)), _const_spec((D_FF, D_MODEL))],
        out_specs=[pl.BlockSpec((tile, D_FF), lambda i: (i, 0)), tok, tok,
                   pl.BlockSpec((SUBLANES, LANES), lambda i: (0, 0)), pl.BlockSpec((1, D_MODEL), lambda i: (0, 0))],
        out_shape=_hbm_out([jax.ShapeDtypeStruct((T, D_FF), BF16), jax.ShapeDtypeStruct((T, D_MODEL), BF16),
                            jax.ShapeDtypeStruct((T, D_MODEL), F32), jax.ShapeDtypeStruct((SUBLANES, LANES), F32),
                            jax.ShapeDtypeStruct((1, D_MODEL), F32)]),
        scratch_shapes=[pltpu.VMEM((tile, D_FF), BF16)],
        compiler_params=_params(("arbitrary",), VMEM_MAX),
    )(*_in_hbm(h2, x1, target, g4, w_ff_in.reshape(D_FF, D_MODEL), w_ff_out.reshape(D_FF, D_MODEL)))


def _mlp_weight_grads(dfo, a, h2, w_ff_out, row_chunk):
    T = h2.shape[0]

    def body(dfo_ref, h2_ref, a_ref, wo_ref, dwi_ref, dwo_ref, da_ref, rr_ref):
        def rows(r, _):
            sl = pl.ds(pl.multiple_of(r * row_chunk, row_chunk), row_chunk)
            ra = jnp.maximum(a_ref[sl, :].astype(F32), 0.0)
            da_ref[sl, :] = (_dot_nt(dfo_ref[sl, :], wo_ref[0]) * (2.0 * ra)).astype(BF16)
            rr_ref[sl, :] = (ra * ra).astype(BF16)
            return 0

        lax.fori_loop(0, T // row_chunk, rows, 0)
        dwo_ref[0] = _dot_tn(rr_ref[...], dfo_ref[...])
        dwi_ref[0] = _dot_tn(h2_ref[...], da_ref[...])

    return pl.pallas_call(
        body, name="mlp_weight_grads", grid=(N_DEV,),
        in_specs=[_const_spec((T, D_MODEL)), _const_spec((T, D_MODEL)), pl.BlockSpec((T, FF_SHARD), lambda k: (0, k)),
                  pl.BlockSpec((1, FF_SHARD, D_MODEL), lambda k: (k, 0, 0))],
        out_specs=[pl.BlockSpec((1, D_MODEL, FF_SHARD), lambda k: (k, 0, 0)),
                   pl.BlockSpec((1, FF_SHARD, D_MODEL), lambda k: (k, 0, 0)), pl.BlockSpec((T, FF_SHARD), lambda k: (0, k))],
        out_shape=_hbm_out([jax.ShapeDtypeStruct((N_DEV, D_MODEL, FF_SHARD), F32),
                            jax.ShapeDtypeStruct((N_DEV, FF_SHARD, D_MODEL), F32), jax.ShapeDtypeStruct((T, D_FF), BF16)]),
        scratch_shapes=[pltpu.VMEM((T, FF_SHARD), BF16)],
        compiler_params=_params(("arbitrary",), VMEM_MAX),
    )(*_in_hbm(dfo, h2, a, w_ff_out))


def _mlp_input_grad(da, w_ff_in_t, tile):
    T = da.shape[0]

    def body(da_ref, w_ref, o_ref):
        o_ref[...] = _dot(da_ref[...], w_ref[...])

    return pl.pallas_call(
        body, name="mlp_input_grad", grid=(T // tile,),
        in_specs=[pl.BlockSpec((tile, D_FF), lambda i: (i, 0)), _const_spec((D_FF, D_MODEL))],
        out_specs=pl.BlockSpec((tile, D_MODEL), lambda i: (i, 0)),
        out_shape=_hbm_out(jax.ShapeDtypeStruct((T, D_MODEL), F32)),
        compiler_params=_params(("arbitrary",), VMEM_MID),
    )(*_in_hbm(da, w_ff_in_t))


def _block_diag_in(b):
    bt = b.reshape(N_SSM_BLOCKS, GROUPS_PER_BLOCK, GROUP_CH, N_STATE)
    eye = jnp.eye(GROUPS_PER_BLOCK, dtype=b.dtype)
    return jnp.einsum("jacp,ab->jacbp", bt, eye).reshape(N_SSM_BLOCKS, LANES, SSM_LANE_BLOCK)


def _block_diag_in_grad(g):
    g = g.reshape(N_SSM_BLOCKS, GROUPS_PER_BLOCK, GROUP_CH, GROUPS_PER_BLOCK, N_STATE)
    d = jnp.diagonal(g, axis1=1, axis2=3)
    return jnp.transpose(d, (0, 3, 1, 2)).reshape(N_GROUPS, GROUP_CH, N_STATE)


def _block_diag_out(c):
    ct = c.reshape(N_SSM_BLOCKS, GROUPS_PER_BLOCK, GROUP_CH, N_STATE)
    eye = jnp.eye(GROUPS_PER_BLOCK, dtype=c.dtype)
    return jnp.einsum("jacp,ab->japbc", ct, eye).reshape(N_SSM_BLOCKS, SSM_LANE_BLOCK, LANES)


def _block_diag_out_grad(g):
    g = g.reshape(N_SSM_BLOCKS, GROUPS_PER_BLOCK, N_STATE, GROUPS_PER_BLOCK, GROUP_CH)
    d = jnp.diagonal(g, axis1=1, axis2=3)
    return jnp.transpose(d, (0, 3, 2, 1)).reshape(N_GROUPS, GROUP_CH, N_STATE)


def _tiles(T):
    return dict(proj=min(512, T), proj_bwd=min(512, T // 2), merge=min(512, T), merge_bwd=min(256, T),
                mlp_fwd=min(512, T), mlp_bwd=min(512, T), ssm_chunk=min(1024, T))


def _mesh_position():
    x, y, c = lax.axis_index("x"), lax.axis_index("y"), lax.axis_index("c")
    other_chips = [(1 - x, y), (x, 1 - y), (1 - x, 1 - y)]
    return x, y, c, other_chips


def _gather_carry(arrays, pass_on=True):
    n = len(arrays)

    def copies(ins, outs, sems):
        send_sems, recv_sems, local_sems = sems
        x, y, c, chips = _mesh_position()
        me, sibling = (x, y, c), (x, y, 1 - c)

        def copy(a, k, block, to, src=None):
            px, py, pc = block
            dst = outs[a].at[4 * px + 2 * py + pc]
            return pltpu.make_async_remote_copy(
                src_ref=dst if src is None else src, dst_ref=dst, send_sem=send_sems.at[7 * a + k],
                recv_sem=recv_sems.at[7 * a + k], device_id=to, device_id_type=MESH_IDS)

        mine = [pltpu.make_async_copy(ins[a], outs[a].at[4 * x + 2 * y + c], local_sems.at[a]) for a in range(n)]
        first = []
        for a in range(n):
            first.append(copy(a, 0, me, sibling, src=ins[a]))
            first += [copy(a, 1 + j, me, (*chip, c), src=ins[a]) for j, chip in enumerate(chips)]
        return copy, mine, first, me, sibling, chips, c

    def start(ins, outs, sems):
        _, mine, first, *_ = copies(ins, outs, sems)
        for cp in mine + first:
            cp.start()

    def passed_on(copy, sibling, chips, c):
        return [copy(a, 4 + j, (*chip, c), sibling) for a in range(n) for j, chip in enumerate(chips)]

    def middle(ins, outs, sems):
        copy, _, _, me, sibling, chips, c = copies(ins, outs, sems)
        for a in range(n):
            for j, chip in enumerate(chips):
                copy(a, 1 + j, (*chip, c), me).wait_recv()
                copy(a, 4 + j, (*chip, c), sibling).start()

    def finish(ins, outs, sems):
        copy, mine, first, me, sibling, chips, c = copies(ins, outs, sems)
        for a in range(n):
            copy(a, 0, sibling, me).wait_recv()
            for j, chip in enumerate(chips):
                (copy(a, 4 + j, (*chip, 1 - c), me) if pass_on else copy(a, 1 + j, (*chip, c), me)).wait_recv()
        for cp in first + (passed_on(copy, sibling, chips, c) if pass_on else []):
            cp.wait_send()
        for cp in mine:
            cp.wait()

    return _Carry(arrays, [jax.ShapeDtypeStruct((N_DEV,) + a.shape, a.dtype) for a in arrays],
                  [pltpu.SemaphoreType.DMA((7 * n,)), pltpu.SemaphoreType.DMA((7 * n,)), pltpu.SemaphoreType.DMA((n,))],
                  start, finish, middle if pass_on else None)


def _gather_pass_on(gathered, name):
    n = len(gathered)

    def body(*refs):
        zones, send_sems, recv_sems = refs[:n], refs[2 * n], refs[2 * n + 1]
        x, y, c, chips = _mesh_position()
        copies = []
        for a in range(n):
            for j, (px, py) in enumerate(chips):
                block = zones[a].at[4 * px + 2 * py + c]
                copies.append(pltpu.make_async_remote_copy(
                    src_ref=block, dst_ref=block, send_sem=send_sems.at[3 * a + j], recv_sem=recv_sems.at[3 * a + j],
                    device_id=(x, y, 1 - c), device_id_type=MESH_IDS))
        for cp in copies:
            cp.start()
        for cp in copies:
            cp.wait()

    return pl.pallas_call(
        body, name=name, in_specs=[HBM_SPEC] * n, out_specs=[HBM_SPEC] * n,
        out_shape=_hbm_out([jax.ShapeDtypeStruct(g.shape, g.dtype) for g in gathered]),
        scratch_shapes=[pltpu.SemaphoreType.DMA((3 * n,)), pltpu.SemaphoreType.DMA((3 * n,))],
        input_output_aliases={a: a for a in range(n)})(*gathered)


def _pairwise_carry(arrays, n_slots, make_copies):
    n = len(arrays)

    def start(ins, outs, sems):
        for cp in make_copies(ins, outs, sems):
            cp.start()

    def finish(ins, outs, sems):
        for cp in make_copies(ins, outs, sems):
            cp.wait()

    return _Carry(arrays, [jax.ShapeDtypeStruct((n_slots,) + a.shape[1:], a.dtype) for a in arrays],
                  [pltpu.SemaphoreType.DMA((n_slots * n,)), pltpu.SemaphoreType.DMA((n_slots * n,))], start, finish)


def _sibling_carry(grads):
    def make_copies(ins, outs, sems):
        x, y, c, _ = _mesh_position()
        return [pltpu.make_async_remote_copy(
            src_ref=ins[a].at[2 * ch + (1 - c)], dst_ref=outs[a].at[ch], send_sem=sems[0].at[4 * a + ch],
            recv_sem=sems[1].at[4 * a + ch], device_id=(x, y, 1 - c), device_id_type=MESH_IDS)
            for a in range(len(grads)) for ch in range(4)]

    return _pairwise_carry(grads, 4, make_copies)


def _chips_carry(sums):
    def make_copies(ins, outs, sems):
        x, y, c, chips = _mesh_position()
        return [pltpu.make_async_remote_copy(
            src_ref=ins[a].at[2 * px + py], dst_ref=outs[a].at[j], send_sem=sems[0].at[3 * a + j],
            recv_sem=sems[1].at[3 * a + j], device_id=(px, py, c), device_id_type=MESH_IDS)
            for a in range(len(sums)) for j, (px, py) in enumerate(chips)]

    return _pairwise_carry(sums, 3, make_copies)


def _everyone_carry(arrays):
    def make_copies(ins, outs, sems):
        x, y, c, _ = _mesh_position()
        flip = lambda v, bit: 1 - v if bit else v
        return [pltpu.make_async_remote_copy(
            src_ref=ins[a], dst_ref=outs[a].at[r - 1], send_sem=sems[0].at[7 * a + r - 1], recv_sem=sems[1].at[7 * a + r - 1],
            device_id=(flip(x, r & 4), flip(y, r & 2), flip(c, r & 1)), device_id_type=MESH_IDS)
            for a in range(len(arrays)) for r in range(1, N_DEV)]

    carry = _pairwise_carry([jax.ShapeDtypeStruct((1,) + a.shape, a.dtype) for a in arrays], N_DEV - 1, make_copies)
    carry.inputs = list(arrays)
    return carry


def _sum_everyone(own, received, me, name, after=()):
    def body(me_ref, own_ref, r_ref, *refs):
        g = None
        for d in range(N_DEV):
            relation = jnp.bitwise_xor(d, me_ref[0])
            part = jnp.where(relation == 0, own_ref[...], r_ref[jnp.maximum(relation - 1, 0)])
            g = part if g is None else g + part
        refs[-1][...] = g

    whole = lambda shape: pl.BlockSpec(shape, lambda i, me_ref: (0,) * len(shape))
    return pl.pallas_call(
        body, name=name,
        grid_spec=pltpu.PrefetchScalarGridSpec(
            num_scalar_prefetch=1, grid=(1,), in_specs=[whole(own.shape), whole(received.shape)] + [HBM_SPEC] * len(after),
            out_specs=whole(own.shape)),
        out_shape=jax.ShapeDtypeStruct(own.shape, F32))(me, *_in_hbm(own, received), *after)


SEM_SPEC = pl.BlockSpec(memory_space=pltpu.SEMAPHORE)
DATAFLOW_EFFECT = pltpu.SideEffectType.DATAFLOW_SIDE_EFFECTING


def _exchange_start(carry, name, after=()):
    n, n_sems = len(carry.inputs), len(carry.sems)
    lands = [lax.empty(s.shape, s.dtype) for s in carry.out_shapes]

    def body(*refs):
        first_out = 2 * n + len(after)
        srcs, zones, sems, token = refs[:n], refs[n:2 * n], refs[first_out:first_out + n_sems], refs[-1]
        carry.start(srcs, zones, sems)
        token[...] = jnp.zeros_like(token)

    outs = pl.pallas_call(
        body, name=name, in_specs=[HBM_SPEC] * (2 * n + len(after)),
        out_specs=[SEM_SPEC] * n_sems + [HBM_SPEC] * (2 * n) + [pl.BlockSpec(memory_space=pltpu.VMEM)],
        out_shape=list(carry.sems) + _hbm_out([jax.ShapeDtypeStruct(a.shape, a.dtype) for a in carry.inputs])
        + _hbm_out(carry.out_shapes) + [jax.ShapeDtypeStruct((SUBLANES, LANES), F32)],
        input_output_aliases={j: n_sems + j for j in range(2 * n)},
        compiler_params=pltpu.CompilerParams(has_side_effects=DATAFLOW_EFFECT),
    )(*_in_hbm(*carry.inputs, *lands), *after)
    return outs[:-1], outs[-1]


def _exchange_wait(carry, in_flight, after, name):
    n, n_sems = len(carry.inputs), len(carry.sems)
    sems, srcs, zones = in_flight[:n_sems], in_flight[n_sems:n_sems + n], in_flight[n_sems + n:]

    def body(*refs):
        src_refs, zone_refs, sem_refs = refs[:n], refs[n:2 * n], refs[2 * n:2 * n + n_sems]
        carry.finish(src_refs, zone_refs, sem_refs)

    outs = pl.pallas_call(
        body, name=name, in_specs=[HBM_SPEC] * (2 * n) + [SEM_SPEC] * n_sems + [HBM_SPEC] * len(after),
        out_specs=[HBM_SPEC] * (2 * n),
        out_shape=_hbm_out([jax.ShapeDtypeStruct(a.shape, a.dtype) for a in carry.inputs]) + _hbm_out(carry.out_shapes),
        input_output_aliases={j: j for j in range(2 * n)},
        compiler_params=pltpu.CompilerParams(has_side_effects=DATAFLOW_EFFECT),
    )(*srcs, *zones, *sems, *after)
    return list(outs[:n]), list(outs[n:])


def _add_sibling(grads8, recvs, place, row_tiles, name):
    k = len(grads8)
    g4 = [g.reshape(4, 2, *g.shape[1:]) for g in grads8]

    def body(place_ref, *refs):
        g_refs, r_refs, o_refs, ob_refs = (refs[j * k:(j + 1) * k] for j in range(4))
        own = pl.program_id(1) == place_ref[1]
        for g_ref, r_ref, o_ref, ob_ref in zip(g_refs, r_refs, o_refs, ob_refs):
            s = g_ref[0] + r_ref[...]
            ob_ref[...] = s.astype(BF16)

            @pl.when(own)
            def _(o_ref=o_ref, s=s):
                o_ref[...] = s[0]

    def blocks(make):
        return [make(g.shape[1] // row_tiles, g.shape[2]) for g in grads8]

    slot = lambda tr, C: pl.BlockSpec((1, tr, C), lambda r, ch, place_ref: (ch, r, 0))
    outs = pl.pallas_call(
        body, name=name,
        grid_spec=pltpu.PrefetchScalarGridSpec(
            num_scalar_prefetch=1, grid=(row_tiles, 4),
            in_specs=blocks(lambda tr, C: pl.BlockSpec((1, 1, tr, C), lambda r, ch, place_ref: (ch, place_ref[0], r, 0)))
            + blocks(slot),
            out_specs=blocks(lambda tr, C: pl.BlockSpec((tr, C), lambda r, ch, place_ref: (r, 0))) + blocks(slot)),
        out_shape=_hbm_out([jax.ShapeDtypeStruct(g.shape[1:], F32) for g in grads8]
                           + [jax.ShapeDtypeStruct((4,) + g.shape[1:], BF16) for g in grads8]),
        compiler_params=_params(("arbitrary", "arbitrary")),
    )(place, *_in_hbm(*g4, *recvs))
    return list(outs[:k]), list(outs[k:])


def _adam_math(w, g, m, v):
    m = ADAM_B1 * m + (1.0 - ADAM_B1) * g
    v = ADAM_B2 * v + (1.0 - ADAM_B2) * jnp.square(g)
    m_hat = m / (1.0 - ADAM_B1 ** ADAM_STEP)
    v_hat = v / (1.0 - ADAM_B2 ** ADAM_STEP)
    delta = -ADAM_LR * (m_hat / (jnp.sqrt(v_hat) + ADAM_EPS) + ADAM_WD * w)
    return delta, m, v


def _adam_big(ws, ms, vs, chip_sums, recvs, row_tiles, name, after=()):
    k = len(ws)

    def body(*refs):
        refs = refs[:5 * k] + refs[5 * k + len(after):]
        w_refs, m_refs, v_refs, s_refs, r_refs, g_refs, d_refs, nm_refs, nv_refs = (refs[j * k:(j + 1) * k] for j in range(9))
        for a in range(k):
            r_ref = r_refs[a]
            g = s_refs[a][...] + r_ref[0].astype(F32) + r_ref[1].astype(F32) + r_ref[2].astype(F32)
            g_refs[a][...] = g
            d_refs[a][...], nm_refs[a][...], nv_refs[a][...] = _adam_math(w_refs[a][...], g, m_refs[a][...], v_refs[a][...])

    def blocks(make):
        return [make(w.shape[0] // row_tiles, w.shape[1]) for w in ws]

    blk = lambda tr, C: pl.BlockSpec((tr, C), lambda r: (r, 0))
    outs = pl.pallas_call(
        body, name=name, grid=(row_tiles,),
        in_specs=blocks(blk) * 4 + blocks(lambda tr, C: pl.BlockSpec((3, tr, C), lambda r: (0, r, 0))) + [HBM_SPEC] * len(after),
        out_specs=blocks(blk) * 4,
        out_shape=[jax.ShapeDtypeStruct(w.shape, F32) for w in ws] * 4,
        compiler_params=_params(("arbitrary",)),
    )(*_in_hbm(*ws, *ms, *vs, *chip_sums, *recvs), *after)
    return [list(outs[j * k:(j + 1) * k]) for j in range(4)]


def _sum_partials(partials, name, after=()):
    def body(p_ref, *refs):
        g = p_ref[0]
        for d in range(1, partials.shape[0]):
            g = g + p_ref[d]
        refs[-1][...] = g

    return pl.pallas_call(body, name=name, grid=(1,), in_specs=[_whole(partials.shape)] + [HBM_SPEC] * len(after),
                          out_specs=_whole(partials.shape[1:]),
                          out_shape=jax.ShapeDtypeStruct(partials.shape[1:], F32))(*_in_hbm(partials), *after)


def _adam_small(ws, ms, vs, gs):
    n = len(ws)

    def body(*refs):
        w_refs, m_refs, v_refs, g_refs = (refs[i * n:(i + 1) * n] for i in range(4))
        d_refs, nm_refs, nv_refs = (refs[(4 + i) * n:(5 + i) * n] for i in range(3))
        for j in range(n):
            d_refs[j][...], nm_refs[j][...], nv_refs[j][...] = _adam_math(
                w_refs[j][...], g_refs[j][...], m_refs[j][...], v_refs[j][...])

    specs = [_whole(w.shape) for w in ws]
    outs = pl.pallas_call(body, name="adam_small", grid=(1,), in_specs=specs * 4, out_specs=specs * 3,
                          out_shape=[jax.ShapeDtypeStruct(w.shape, F32) for w in ws] * 3,
                          compiler_params=_params(("arbitrary",), VMEM_MID))(*_in_hbm(*ws, *ms, *vs, *gs))
    return outs[:n], outs[n:2 * n], outs[2 * n:]


PACK_QUANTUM = SUBLANES * LANES


def _pack(named, names):
    parts = []
    for nme in names:
        flat = named[nme].reshape(-1)
        parts.append(jnp.pad(flat, (0, -flat.size % PACK_QUANTUM)))
    return jnp.concatenate(parts).reshape(-1, LANES)


def _unpack(packed, shapes, names):
    flat = packed.reshape(-1)
    out, pos = {}, 0
    for nme in names:
        size = math.prod(shapes[nme])
        out[nme] = flat[pos:pos + size].reshape(shapes[nme])
        pos += size + (-size % PACK_QUANTUM)
    return out


BIG = ("w_in", "w_glu", "w_attn_branch", "w_ssm_branch", "w_out", "w_ff_in", "w_ff_out")
COLUMN_SHARDED = ("w_in", "w_attn_branch", "w_ssm_branch", "w_ff_in")
SMALL = ("norm_mix_pre", "norm_mix_post", "norm_mlp_pre", "norm_mlp_post", "rel_bias", "sinks", "lam_re", "lam_im",
         "log_dt", "b_re", "b_im", "c_re", "c_im", "d_skip")
SWAPPED_SMALL = ("rel_bias", "b_re", "b_im")
SMALL_LATE = ("norm_mix_pre", "rel_bias", "sinks", "loss")
SMALL_BEFORE_ATTN_BWD = tuple(n for n in SMALL if n not in SMALL_LATE)
ALL_WEIGHTS = ("norm_mix_pre", "norm_mix_post", "norm_mlp_pre", "norm_mlp_post", "w_in", "rel_bias", "sinks", "lam_re",
               "lam_im", "log_dt", "b_re", "b_im", "c_re", "c_im", "d_skip", "w_glu", "w_attn_branch", "w_ssm_branch",
               "w_out", "w_ff_in", "w_ff_out")


def _full_from_gathered(name, gathered):
    _, r, c = gathered.shape
    if name in COLUMN_SHARDED:
        return jnp.transpose(gathered, (1, 0, 2)).reshape(r, N_DEV * c)
    return gathered.reshape(N_DEV * r, c)


def _blocks_from_full(name, full):
    r, c = full.shape
    if name in COLUMN_SHARDED:
        return jnp.transpose(full.reshape(r, N_DEV, c // N_DEV), (1, 0, 2))
    return full.reshape(N_DEV, r // N_DEV, c)


def kernel(x, norm_mix_pre, norm_mix_post, norm_mlp_pre, norm_mlp_post, w_in, rel_bias, sinks, lam_re, lam_im, log_dt, b_re, b_im, c_re, c_im, d_skip, w_glu, w_attn_branch, w_ssm_branch, w_out, w_ff_in, w_ff_out, loss_target, m_norm_mix_pre, m_norm_mix_post, m_norm_mlp_pre, m_norm_mlp_post, m_w_in, m_rel_bias, m_sinks, m_lam_re, m_lam_im, m_log_dt, m_b_re, m_b_im, m_c_re, m_c_im, m_d_skip, m_w_glu, m_w_attn_branch, m_w_ssm_branch, m_w_out, m_w_ff_in, m_w_ff_out, v_norm_mix_pre, v_norm_mix_post, v_norm_mlp_pre, v_norm_mlp_post, v_w_in, v_rel_bias, v_sinks, v_lam_re, v_lam_im, v_log_dt, v_b_re, v_b_im, v_c_re, v_c_im, v_d_skip, v_w_glu, v_w_attn_branch, v_w_ssm_branch, v_w_out, v_w_ff_in, v_w_ff_out):
    args = dict(locals())
    w = {n: args[n] for n in ALL_WEIGHTS}
    m = {n: args["m_" + n] for n in ALL_WEIGHTS}
    v = {n: args["v_" + n] for n in ALL_WEIGHTS}
    core = lax.axis_index("c").astype(jnp.int32).reshape(1)
    chip = (2 * lax.axis_index("x") + lax.axis_index("y")).astype(jnp.int32).reshape(1)
    xs, target = x[0], loss_target[0]
    t = _tiles(xs.shape[0])
    local = lambda d, n: d[n][0].T if n == "w_in" else d[n][0]
    gather_in = _gather_carry([local(w, "w_in").astype(BF16)], pass_on=False)
    gather_in_flight, token = _exchange_start(gather_in, "gather_w_in_start")
    one = token[0:1, 0:1] + 1.0
    shard = {n: (local(w, n) * one).astype(BF16) for n in BIG if n != "w_in"}
    shard["w_ff_in"] = shard["w_ff_in"].T
    view = lambda n, a: jnp.swapaxes(a, -1, -2) if n in SWAPPED_SMALL else a
    small = {n: (view(n, w[n]) if n == "rel_bias" else view(n, w[n])[0]) for n in SMALL}
    g1, g2, g3, g4 = (small[n].reshape(1, D_MODEL) for n in ("norm_mix_pre", "norm_mix_post", "norm_mlp_pre", "norm_mlp_post"))
    bucket = jnp.asarray(_bucket_table())
    rel_b, sink = small["rel_bias"], small["sinks"].reshape(1, N_HEADS)
    lam_r, lam_i = small["lam_re"].reshape(1, STATES), small["lam_im"].reshape(1, STATES)
    ldt_rep = jnp.repeat(small["log_dt"].reshape(N_GROUPS), N_STATE).reshape(1, STATES)
    bd_re, bd_im = _block_diag_in(small["b_re"] * one), _block_diag_in(small["b_im"] * one)
    cm_re, cm_im = _block_diag_out(small["c_re"] * one).astype(BF16), _block_diag_out(small["c_im"] * one).astype(BF16)
    dsk = small["d_skip"].reshape(1, SSM_W)
    a_re, a_im, bm_re, bm_im = _ssm_prep(lam_r, lam_i, ldt_rep, bd_re, bd_im)

    travelled_behind = list(shard.values()) + [a_re, a_im, bm_re, bm_im, cm_re, cm_im]
    _, landed = _exchange_wait(gather_in, gather_in_flight, travelled_behind, "gather_w_in_wait")
    (g_in,) = _gather_pass_on(landed, "gather_w_in_pass_on")
    wf_in = g_in.reshape(IN_W, D_MODEL)
    merge_names = ("w_glu", "w_attn_branch", "w_ssm_branch", "w_out")
    (q, k, vv, u, ga, gs, h), gathered = _in_proj_fwd(xs, g1, wf_in, t["proj"], _gather_carry([shard[n] for n in merge_names]))
    wf = {n: _full_from_gathered(n, g) for n, g in zip(merge_names, gathered)}
    (att,), (wf_ff_in,) = _attn_fwd(q, k, vv, bucket, rel_b, sink, _gather_carry([shard["w_ff_in"]]))
    (y, h_re, h_im, in_re, in_im), (wf_ff_out,) = _ssm_fwd(
        u, a_re, a_im, bm_re, bm_im, cm_re, cm_im, dsk, t["ssm_chunk"], _gather_carry([shard["w_ff_out"]]))
    x1, o, h2 = _merge_fwd(xs, y, att, ga, gs, g2, g3, wf["w_glu"], wf["w_ssm_branch"], wf["w_attn_branch"], wf["w_out"],
                           t["merge"])
    a, dfo, dx2, loss_blk, dg4 = _mlp_fwd(h2, x1, target, g4, wf_ff_in, wf_ff_out, t["mlp_fwd"])

    groups = {"ff": 4, "merge": 1, "w_in": 2}

    def add_sibling(group, blocks, received):
        return _add_sibling(blocks, received, jnp.concatenate([core, chip]), groups[group], "add_sibling_" + group)

    ff_names = ("w_ff_in", "w_ff_out")
    dw_ff_in, dw_ff_out, da = _mlp_weight_grads(dfo, a, h2, wf_ff_out, t["mlp_bwd"])
    dh2 = _mlp_input_grad(da, wf_ff_in.reshape(D_FF, D_MODEL), t["mlp_bwd"])
    ff_blocks = [dw_ff_in, dw_ff_out]
    (dx1, dgates, datt, dy, dw_glu, dw_ssm, dw_attn, dw_out, dg2, dg3), ff_recv = _merge_bwd(
        dh2, dx2, x1, o, y, att, ga, gs, g2, g3, wf["w_glu"], wf["w_ssm_branch"], wf["w_attn_branch"], wf["w_out"],
        t["merge_bwd"], _sibling_carry(ff_blocks))
    ff_sums, ff_sums_bf = add_sibling("ff", ff_blocks, ff_recv)
    merge_blocks = [_blocks_from_full(n, g) for n, g in zip(merge_names, (dw_glu, dw_attn, dw_ssm, dw_out))]
    (du, dbm_re, dbm_im, dcm_re, dcm_im, da_re, da_im, dd_skip), carried = _ssm_bwd(
        dy, u, h_re, h_im, in_re, in_im, a_re, a_im, bm_re, bm_im, cm_re, cm_im, dsk, t["ssm_chunk"],
        _join(_chips_carry(ff_sums_bf), _sibling_carry(merge_blocks)))
    ff_from_chips, merge_recv = carried[:2], carried[2:]
    merge_sums, merge_sums_bf = add_sibling("merge", merge_blocks, merge_recv)
    dbd_re, dbd_im, dlam_re, dlam_im, dldt_rep = _ssm_prep_bwd(lam_r, lam_i, ldt_rep, bd_re, bd_im, dbm_re, dbm_im, da_re, da_im)
    dlog_dt = _group_sum(dldt_rep.reshape(N_GROUPS, N_STATE))
    shapes = {n: view(n, w[n]).shape for n in SMALL}
    shapes["loss"] = (1,)
    small_grads = dict(
        norm_mix_post=dg2, norm_mlp_pre=dg3, norm_mlp_post=dg4, lam_re=dlam_re, lam_im=dlam_im, log_dt=dlog_dt,
        b_re=_block_diag_in_grad(dbd_re), b_im=_block_diag_in_grad(dbd_im),
        c_re=_block_diag_out_grad(dcm_re), c_im=_block_diag_out_grad(dcm_im), d_skip=dd_skip)
    packed_early = _pack({n: small_grads[n].reshape(shapes[n]) for n in SMALL_BEFORE_ATTN_BWD}, SMALL_BEFORE_ATTN_BWD)
    (dq, dkv, attn_small), carried = _attn_bwd(
        q, k, vv, datt, bucket, rel_b, sink, _join(_chips_carry(merge_sums_bf), _gather_carry([packed_early])))
    merge_from_chips, partials_early = carried[:-1], carried[-1]

    dparts = (dq, dkv, du, dgates)
    dw_in_t = _in_proj_weight_grad(h, dparts)
    in_blocks = [dw_in_t.reshape(N_DEV, IN_W // N_DEV, D_MODEL)]
    to_sibling = _sibling_carry(in_blocks)
    in_flight, token = _exchange_start(to_sibling, "w_in_sibling_start")
    n_tiles = xs.shape[0] // t["proj_bwd"]
    (grad_x, dg1), _ = _in_proj_input_grad(xs, g1 + token[0:1, 0:1], wf_in, dx1, dparts, t["proj_bwd"], 0, n_tiles, "in_proj_input_grad")
    late = dict(norm_mix_pre=dg1, rel_bias=attn_small[:, :N_BUCKETS, 0], sinks=attn_small[:, N_BUCKETS, 0], loss=loss_blk[0:1, 0])
    packed_late = _pack({n: late[n].reshape(shapes[n]) for n in SMALL_LATE}, SMALL_LATE)
    to_everyone = _everyone_carry([packed_late])
    in_blocks, in_recv = _exchange_wait(to_sibling, in_flight, [packed_late], "w_in_sibling_wait")
    in_sums, in_sums_bf = add_sibling("w_in", in_blocks, in_recv)
    to_chips = _chips_carry(in_sums_bf)
    started, chips_started = _exchange_start(_join(to_everyone, to_chips), "late_grads_and_w_in_chips_start")
    late_in_flight, in_flight = [[started[j] for j in js] for js in ((0, 1, 4, 6), (2, 3, 5, 7))]

    grads, deltas, new_m, new_v = {}, {}, {}, {}

    def adam_group(group, names, sums, received, after=()):
        outs = _adam_big(*[[local(d, n) for n in names] for d in (w, m, v)], sums, received, groups[group],
                         "adam_" + group, after)
        for store, vals in zip((grads, deltas, new_m, new_v), outs):
            store.update({n: (o.T if n == "w_in" else o)[None] for n, o in zip(names, vals)})

    adam_group("ff", ff_names, ff_sums, ff_from_chips, [chips_started])
    adam_group("merge", merge_names, merge_sums, merge_from_chips, [chips_started])

    grads.update(_unpack(_sum_partials(partials_early, "sum_small_grads", [chips_started]), shapes, SMALL_BEFORE_ATTN_BWD))
    (packed_late,), (late_received,) = _exchange_wait(
        to_everyone, late_in_flight, [new_v["w_ff_out"], new_v["w_out"]], "late_grads_wait")
    grads.update(_unpack(_sum_everyone(packed_late, late_received, 2 * chip + core, "sum_late_grads"), shapes, SMALL_LATE))
    loss = grads.pop("loss").reshape(())
    small_out = _adam_small(*[[view(n, d[n]) for n in SMALL] for d in (w, m, v)], [grads[n] for n in SMALL])
    for store, vals in zip((deltas, new_m, new_v), small_out):
        store.update(zip(SMALL, vals))
    for store in (grads, deltas, new_m, new_v):
        store.update({n: view(n, store[n]) for n in SWAPPED_SMALL})

    busy = [new_v["w_ff_out"], new_v["w_out"], deltas["norm_mix_pre"]]
    _, (in_from_chips,) = _exchange_wait(to_chips, in_flight, busy, "w_in_chips_wait")
    adam_group("w_in", ("w_in",), in_sums, [in_from_chips])

    return (loss, grad_x[None], *[grads[n] for n in ALL_WEIGHTS], *[deltas[n] for n in ALL_WEIGHTS],
            *[new_m[n] for n in ALL_WEIGHTS], *[new_v[n] for n in ALL_WEIGHTS])
```

```python
import math

import jax
import jax.numpy as jnp
import numpy as np
from jax import lax
from jax.experimental import pallas as pl
from jax.experimental.pallas import tpu as pltpu

F32 = jnp.float32
BF16 = jnp.bfloat16

D_MODEL = 1024
N_HEADS = 8
HEAD_DIM = 64
ATTN_W = 512
KV_W = 128
BLOCK = 128
N_BUCKETS = 32
SSM_W = 512
N_GROUPS = 32
N_STATE = 64
GROUP_CH = 16
STATES = N_GROUPS * N_STATE
D_FF = 4096
IN_W = 3328
SPLITS = (0, 512, 640, 768, 1280, 2304, 3328)
RMS_EPS = 1e-6
NEG_INF = -1e30
SUBLANES = 8
LANES = 128
SSM_LANE_BLOCK = 512
N_SSM_BLOCKS = STATES // SSM_LANE_BLOCK
GROUPS_PER_BLOCK = SSM_LANE_BLOCK // N_STATE
VMEM_BIG = 52 * 1024 * 1024
VMEM_MID = 40 * 1024 * 1024
VMEM_MAX = 60 * 1024 * 1024

ADAM_LR = 0.001
ADAM_B1 = 0.9
ADAM_B2 = 0.999
ADAM_EPS = 1e-08
ADAM_WD = 0.01
ADAM_STEP = 10

N_DEV = 8


def _dot(a, b):
    return jnp.dot(a, b, preferred_element_type=F32)


def _dot_nt(a, b):
    return lax.dot_general(a, b, (((1,), (1,)), ((), ())), preferred_element_type=F32)


def _dot_tn(a, b):
    return lax.dot_general(a, b, (((0,), (0,)), ((), ())), preferred_element_type=F32)


def _rms_scale(x):
    return lax.rsqrt(jnp.mean(x * x, axis=-1, keepdims=True) + RMS_EPS)


def _rms_bwd(dy, x, r, g):
    t = dy * g
    dx = r * t - x * (r * r * r) * jnp.mean(t * x, axis=-1, keepdims=True)
    dg = jnp.sum(dy * x * r, axis=0, keepdims=True)
    return dx, dg


def _const_spec(shape):
    nd = len(shape)
    return pl.BlockSpec(shape, lambda *_: (0,) * nd, pipeline_mode=pl.Buffered(1))


def _in_hbm(*arrays):
    return tuple(pltpu.with_memory_space_constraint(a, pltpu.HBM) for a in arrays)


def _hbm_out(shapes):
    if isinstance(shapes, (list, tuple)):
        return [_hbm_out(s) for s in shapes]
    return shapes if isinstance(shapes, pl.MemoryRef) else pltpu.HBM(shapes.shape, shapes.dtype)


def _whole(shape):
    nd = len(shape)
    return pl.BlockSpec(shape, lambda *_: (0,) * nd)


def _params(sem, vmem=None):
    return pltpu.CompilerParams(dimension_semantics=sem, vmem_limit_bytes=vmem)


MESH_IDS = pl.DeviceIdType.MESH
HBM_SPEC = pl.BlockSpec(memory_space=pl.ANY)


class _Carry:
    def __init__(self, inputs, out_shapes, sems, start, finish, middle=None):
        self.inputs, self.out_shapes, self.sems = list(inputs), list(out_shapes), list(sems)
        self.start, self.middle, self.finish = start, middle, finish


def _join(a, b):
    na_in, na_out, na_sem = len(a.inputs), len(a.out_shapes), len(a.sems)

    def both(phase):
        def run(ins, outs, sems):
            for carry, lo in ((a, True), (b, False)):
                part = (lambda seq, n: seq[:n] if lo else seq[n:])
                if getattr(carry, phase) is not None:
                    getattr(carry, phase)(part(ins, na_in), part(outs, na_out), part(sems, na_sem))
        return run

    middle = both("middle") if (a.middle or b.middle) else None
    return _Carry(a.inputs + b.inputs, a.out_shapes + b.out_shapes, a.sems + b.sems, both("start"), both("finish"), middle)


def _hosted_call(body, carry, edge, *, name, grid, in_specs, out_specs, out_shape, scratch_shapes, compiler_params, inputs):
    n_in, n_out = len(in_specs), len(out_specs)
    inputs = [a if s.memory_space == pltpu.SMEM else _in_hbm(a)[0] for a, s in zip(inputs, in_specs)]
    out_shape = _hbm_out(list(out_shape))
    if carry is None:
        outs = pl.pallas_call(body, name=name, grid=grid, in_specs=in_specs, out_specs=out_specs, out_shape=out_shape,
                              scratch_shapes=scratch_shapes, compiler_params=compiler_params)(*inputs)
        return list(outs), []
    c_in, c_out, c_sem = len(carry.inputs), len(carry.out_shapes), len(carry.sems)

    def wrapped(*refs):
        ins, refs = refs[:n_in], refs[n_in:]
        cins, refs = refs[:c_in], refs[c_in:]
        outs, refs = refs[:n_out], refs[n_out:]
        couts, refs = refs[:c_out], refs[c_out:]
        scratch, csems = refs[:len(refs) - c_sem], refs[len(refs) - c_sem:]
        first, middle, last = edge()

        @pl.when(first)
        def _():
            carry.start(cins, couts, csems)

        body(*ins, *outs, *scratch)

        if carry.middle is not None:
            @pl.when(middle)
            def _():
                carry.middle(cins, couts, csems)

        @pl.when(last)
        def _():
            carry.finish(cins, couts, csems)

    outs = pl.pallas_call(
        wrapped, name=name, grid=grid, in_specs=list(in_specs) + [HBM_SPEC] * c_in,
        out_specs=list(out_specs) + [HBM_SPEC] * c_out, out_shape=out_shape + _hbm_out(carry.out_shapes),
        scratch_shapes=list(scratch_shapes) + carry.sems, compiler_params=compiler_params)(*inputs, *_in_hbm(*carry.inputs))
    return list(outs[:n_out]), list(outs[n_out:])


def _pass_on_step(n_steps):
    return max(0, min((7 * n_steps) // 8, n_steps - 2))


def _edge_1d(n_steps, pass_on_last=False):
    middle = n_steps - 1 if pass_on_last else _pass_on_step(n_steps)
    return lambda: (pl.program_id(0) == 0, pl.program_id(0) == middle, pl.program_id(0) == n_steps - 1)


def _edge_2d(n0, n1):
    def edge():
        step = pl.program_id(0) * n1 + pl.program_id(1)
        return step == 0, step == _pass_on_step(n0 * n1), step == n0 * n1 - 1
    return edge


def _in_proj_fwd(x, g1, w_in_t, tile, carry=None):
    T = x.shape[0]

    def body(x_ref, g_ref, w_ref, q_ref, k_ref, v_ref, u_ref, ga_ref, gs_ref, h_ref):
        xv = x_ref[...]
        h = (xv * _rms_scale(xv) * g_ref[...]).astype(BF16)
        h_ref[...] = h
        outs = (q_ref, k_ref, v_ref, u_ref, ga_ref, gs_ref)
        for p, o_ref in enumerate(outs):
            o_ref[...] = _dot_nt(h, w_ref[SPLITS[p]:SPLITS[p + 1], :]).astype(o_ref.dtype)

    widths = [SPLITS[p + 1] - SPLITS[p] for p in range(6)] + [D_MODEL]
    dtypes = [BF16, BF16, BF16, F32, F32, F32, BF16]
    return _hosted_call(
        body, carry, _edge_1d(T // tile), name="in_proj_fwd", grid=(T // tile,),
        in_specs=[pl.BlockSpec((tile, D_MODEL), lambda i: (i, 0)), _const_spec((1, D_MODEL)), _const_spec((IN_W, D_MODEL))],
        out_specs=[pl.BlockSpec((tile, w), lambda i: (i, 0)) for w in widths],
        out_shape=[jax.ShapeDtypeStruct((T, w), dt) for w, dt in zip(widths, dtypes)],
        scratch_shapes=[], compiler_params=_params(("arbitrary",), VMEM_MID), inputs=(x, g1, w_in_t))


PROJ_PARTS = (512, 256, 512, 2048)
PROJ_GRAD_BLOCK = 256


def _in_proj_weight_grad(h, dparts):
    T = h.shape[0]
    blocks = [wd // PROJ_GRAD_BLOCK for wd in PROJ_PARTS]
    starts = [sum(blocks[:p]) for p in range(len(blocks))]

    def body(h_ref, *refs):
        part_refs, o_ref = refs[:-1], refs[-1]
        j = pl.program_id(0)
        for p_ref, start, count in zip(part_refs, starts, blocks):
            @pl.when((j >= start) & (j < start + count))
            def _(p_ref=p_ref):
                o_ref[...] = _dot_tn(p_ref[...], h_ref[...])

    def part_spec(start, count):
        return pl.BlockSpec((T, PROJ_GRAD_BLOCK), lambda j: (0, jnp.clip(j - start, 0, count - 1)))

    return pl.pallas_call(
        body, name="in_proj_weight_grad", grid=(sum(blocks),),
        in_specs=[_const_spec((T, D_MODEL))] + [part_spec(s, c) for s, c in zip(starts, blocks)],
        out_specs=pl.BlockSpec((PROJ_GRAD_BLOCK, D_MODEL), lambda j: (j, 0)),
        out_shape=_hbm_out(jax.ShapeDtypeStruct((IN_W, D_MODEL), F32)),
        compiler_params=_params(("arbitrary",), VMEM_MID),
    )(*_in_hbm(h, *dparts))


def _in_proj_input_grad(x, g1, w_in_t, dx1, dparts, tile, first_tile, n_tiles, name, carry=None):
    offsets = [sum(PROJ_PARTS[:p]) for p in range(len(PROJ_PARTS))]

    def body(x_ref, g_ref, w_ref, dx1_ref, *refs):
        part_refs, (gx_ref, dg_ref) = refs[:len(PROJ_PARTS)], refs[len(PROJ_PARTS):]
        i = pl.program_id(0)
        xv = x_ref[...]
        r = _rms_scale(xv)
        g = g_ref[...]
        dh = sum(_dot(p_ref[...], w_ref[off:off + wd, :]) for p_ref, off, wd in zip(part_refs, offsets, PROJ_PARTS))
        dxn, dg = _rms_bwd(dh, xv, r, g)
        gx_ref[...] = dx1_ref[...] + dxn

        @pl.when(i == 0)
        def _():
            dg_ref[...] = dg

        @pl.when(i > 0)
        def _():
            dg_ref[...] += dg

    tok = lambda wd: pl.BlockSpec((tile, wd), lambda i: (i + first_tile, 0))
    return _hosted_call(
        body, carry, _edge_1d(n_tiles), name=name, grid=(n_tiles,),
        in_specs=[tok(D_MODEL), _const_spec((1, D_MODEL)), _const_spec((IN_W, D_MODEL)), tok(D_MODEL)] + [tok(wd) for wd in PROJ_PARTS],
        out_specs=[pl.BlockSpec((tile, D_MODEL), lambda i: (i, 0)), pl.BlockSpec((1, D_MODEL), lambda i: (0, 0))],
        out_shape=[jax.ShapeDtypeStruct((n_tiles * tile, D_MODEL), F32), jax.ShapeDtypeStruct((1, D_MODEL), F32)],
        scratch_shapes=[], compiler_params=_params(("arbitrary",), VMEM_MID), inputs=(x, g1, w_in_t, dx1, *dparts))


def _bucket_table():
    qi = np.arange(BLOCK)[:, None]
    kj = np.arange(2 * BLOCK)[None, :]
    dist = qi + BLOCK - kj
    max_exact = N_BUCKETS // 2
    d = np.maximum(dist, 0)
    df = np.maximum(d, 1).astype(np.float32)
    large = max_exact + (np.log(df / np.float32(max_exact)) / np.float32(math.log(BLOCK / max_exact))
                         * np.float32(N_BUCKETS - max_exact)).astype(np.int32)
    large = np.minimum(large, N_BUCKETS - 1)
    bucket = np.where(d < max_exact, d, large)
    return np.where((dist >= 0) & (dist < BLOCK), bucket, -1).astype(np.int32)


def _build_bias(bucket_ref, rb_ref, bias_ref):
    bk = bucket_ref[...]
    for h in range(N_HEADS):
        def add(b, acc, h=h):
            return acc + jnp.where(bk == b, rb_ref[h, b], 0.0)
        bias_ref[h] = lax.fori_loop(0, N_BUCKETS, add, jnp.zeros((BLOCK, 2 * BLOCK), F32))


def _kv_variants(prev_ref, cur_ref):
    cat = jnp.concatenate([prev_ref[...], cur_ref[...]], axis=0)
    lo = lax.broadcasted_iota(jnp.int32, cat.shape, 1) < HEAD_DIM
    zero = jnp.zeros_like(cat)
    head0_lo = jnp.where(lo, cat, zero)
    head1_hi = jnp.where(lo, zero, cat)
    return ((head0_lo, pltpu.roll(head0_lo, HEAD_DIM, 1)), (pltpu.roll(head1_hi, HEAD_DIM, 1), head1_hi))


def _merge_kv_grads(g):
    lo = lax.broadcasted_iota(jnp.int32, g[0][0].shape, 1) < HEAD_DIM
    return jnp.where(lo, g[0][0] + pltpu.roll(g[0][1], HEAD_DIM, 1), g[1][1] + pltpu.roll(g[1][0], HEAD_DIM, 1))


def _head_lanes(h):
    return slice((h // 2) * LANES, (h // 2 + 1) * LANES)


def _attn_probs(q_ref, kvar, bias_ref, sk_ref, valid, s_ref):
    for h in range(N_HEADS):
        s_ref[h] = _dot_nt(q_ref[:, _head_lanes(h)], kvar[h // 4][h % 2])
    head = lax.broadcasted_iota(jnp.int32, (N_HEADS, 1, 1), 0)
    sink = jnp.zeros((N_HEADS, 1, 1), F32)
    for h in range(N_HEADS):
        sink = jnp.where(head == h, sk_ref[0, h], sink)
    s = jnp.where(valid[None], s_ref[...] * (HEAD_DIM ** -0.5) + bias_ref[...], NEG_INF)
    m = jnp.maximum(jnp.max(s, axis=-1, keepdims=True), sink)
    p = jnp.exp(s - m)
    e_sink = jnp.exp(sink - m)
    inv = 1.0 / (jnp.sum(p, axis=-1, keepdims=True) + e_sink)
    return p * inv, e_sink * inv


def _attn_valid(bucket_ref, n):
    col = lax.broadcasted_iota(jnp.int32, (BLOCK, 2 * BLOCK), 1)
    return (bucket_ref[...] >= 0) & ((n > 0) | (col >= BLOCK))


def _attn_fwd(q, k, v, bucket, rel_bias, sinks, carry=None):
    T = q.shape[0]
    nb = T // BLOCK

    def body(q_ref, kc_ref, kp_ref, vc_ref, vp_ref, bucket_ref, rb_ref, sk_ref, o_ref, bias_ref, s_ref, p_ref):
        n = pl.program_id(0)

        @pl.when(n == 0)
        def _():
            _build_bias(bucket_ref, rb_ref, bias_ref)

        kvar = _kv_variants(kp_ref, kc_ref)
        vvar = _kv_variants(vp_ref, vc_ref)
        pr, _ = _attn_probs(q_ref, kvar, bias_ref, sk_ref, _attn_valid(bucket_ref, n), s_ref)
        p_ref[...] = pr.astype(BF16)
        for m in range(N_HEADS // 2):
            acc = _dot(p_ref[2 * m], vvar[m // 2][0]) + _dot(p_ref[2 * m + 1], vvar[m // 2][1])
            o_ref[:, m * LANES:(m + 1) * LANES] = acc.astype(o_ref.dtype)

    cur = lambda w: pl.BlockSpec((BLOCK, w), lambda n: (n, 0))
    prev = lambda w: pl.BlockSpec((BLOCK, w), lambda n: (jnp.maximum(n - 1, 0), 0))
    smem = pl.BlockSpec(memory_space=pltpu.SMEM)
    return _hosted_call(
        body, carry, _edge_1d(nb, pass_on_last=True), name="attn_fwd", grid=(nb,),
        in_specs=[cur(ATTN_W), cur(KV_W), prev(KV_W), cur(KV_W), prev(KV_W), _const_spec((BLOCK, 2 * BLOCK)), smem, smem],
        out_specs=[cur(ATTN_W)],
        out_shape=[jax.ShapeDtypeStruct((T, ATTN_W), BF16)],
        scratch_shapes=[pltpu.VMEM((N_HEADS, BLOCK, 2 * BLOCK), F32), pltpu.VMEM((N_HEADS, BLOCK, 2 * BLOCK), F32),
                        pltpu.VMEM((N_HEADS, BLOCK, 2 * BLOCK), BF16)],
        compiler_params=_params(("arbitrary",)), inputs=(q, k, k, v, v, bucket, rel_bias, sinks))


ATTN_SMALL_ROWS = N_BUCKETS + SUBLANES


def _attn_bwd(q, k, v, datt, bucket, rel_bias, sinks, carry=None):
    T = q.shape[0]
    nb = T // BLOCK

    def body(q_ref, do_ref, kc_ref, kp_ref, vc_ref, vp_ref, bucket_ref, rb_ref, sk_ref,
             dq_ref, dkv_ref, small_ref, bias_ref, ds_sum_ref, dsink_ref, kcarry_ref, vcarry_ref,
             s_ref, dp_ref, p_ref, dsc_ref):
        n = pl.program_id(0)

        @pl.when(n == 0)
        def _():
            _build_bias(bucket_ref, rb_ref, bias_ref)
            ds_sum_ref[...] = jnp.zeros_like(ds_sum_ref)
            dsink_ref[...] = jnp.zeros_like(dsink_ref)
            kcarry_ref[...] = jnp.zeros_like(kcarry_ref)
            vcarry_ref[...] = jnp.zeros_like(vcarry_ref)

        @pl.when(n < nb)
        def _():
            kvar = _kv_variants(kp_ref, kc_ref)
            vvar = _kv_variants(vp_ref, vc_ref)
            pr, p_sink = _attn_probs(q_ref, kvar, bias_ref, sk_ref, _attn_valid(bucket_ref, n), s_ref)
            for h in range(N_HEADS):
                dp_ref[h] = _dot_nt(do_ref[:, _head_lanes(h)], vvar[h // 4][h % 2])
            dp = dp_ref[...]
            dsum = jnp.sum(pr * dp, axis=-1, keepdims=True)
            ds = pr * (dp - dsum)
            ds_sum_ref[...] += ds
            dsink_ref[...] -= jnp.sum(p_sink * dsum, axis=1, keepdims=True)
            dsc_ref[...] = (ds * (HEAD_DIM ** -0.5)).astype(BF16)
            p_ref[...] = pr.astype(BF16)
            for m in range(N_HEADS // 2):
                dqm = _dot(dsc_ref[2 * m], kvar[m // 2][0]) + _dot(dsc_ref[2 * m + 1], kvar[m // 2][1])
                dq_ref[:, m * LANES:(m + 1) * LANES] = dqm.astype(dq_ref.dtype)
            dk_var = [[None, None], [None, None]]
            dv_var = [[None, None], [None, None]]
            for kvh in range(2):
                for e in range(2):
                    heads = [h for h in range(N_HEADS) if h // 4 == kvh and h % 2 == e]
                    dk_var[kvh][e] = sum(_dot_tn(dsc_ref[h], q_ref[:, _head_lanes(h)]) for h in heads)
                    dv_var[kvh][e] = sum(_dot_tn(p_ref[h], do_ref[:, _head_lanes(h)]) for h in heads)
            dk_cat = _merge_kv_grads(dk_var)
            dv_cat = _merge_kv_grads(dv_var)

            @pl.when(n > 0)
            def _():
                dkv_ref[:, :KV_W] = (kcarry_ref[...] + dk_cat[:BLOCK]).astype(BF16)
                dkv_ref[:, KV_W:] = (vcarry_ref[...] + dv_cat[:BLOCK]).astype(BF16)

            kcarry_ref[...] = dk_cat[BLOCK:]
            vcarry_ref[...] = dv_cat[BLOCK:]

        @pl.when(n == nb)
        def _():
            dkv_ref[:, :KV_W] = kcarry_ref[...].astype(BF16)
            dkv_ref[:, KV_W:] = vcarry_ref[...].astype(BF16)
            bk = bucket_ref[...]
            row = lax.broadcasted_iota(jnp.int32, (N_HEADS, ATTN_SMALL_ROWS, LANES), 1)

            def add(b, acc):
                masked = jnp.where((bk == b)[None], ds_sum_ref[...], 0.0)
                val = jnp.sum(jnp.sum(masked, axis=1, keepdims=True), axis=2, keepdims=True)
                return acc + jnp.where(row == b, val, 0.0)

            small_ref[...] = lax.fori_loop(0, N_BUCKETS, add, jnp.where(row == N_BUCKETS, dsink_ref[...], 0.0))

    last = nb - 1
    cur = lambda w: pl.BlockSpec((BLOCK, w), lambda n: (jnp.minimum(n, last), 0))
    prev = lambda w: pl.BlockSpec((BLOCK, w), lambda n: (jnp.clip(n - 1, 0, last), 0))
    smem = pl.BlockSpec(memory_space=pltpu.SMEM)
    return _hosted_call(
        body, carry, _edge_1d(nb + 1), name="attn_bwd", grid=(nb + 1,),
        in_specs=[cur(ATTN_W), cur(ATTN_W), cur(KV_W), prev(KV_W), cur(KV_W), prev(KV_W),
                  _const_spec((BLOCK, 2 * BLOCK)), smem, smem],
        out_specs=[cur(ATTN_W), prev(2 * KV_W), pl.BlockSpec((N_HEADS, ATTN_SMALL_ROWS, LANES), lambda n: (0, 0, 0))],
        out_shape=[jax.ShapeDtypeStruct((T, ATTN_W), BF16), jax.ShapeDtypeStruct((T, 2 * KV_W), BF16),
                   jax.ShapeDtypeStruct((N_HEADS, ATTN_SMALL_ROWS, LANES), F32)],
        scratch_shapes=[pltpu.VMEM((N_HEADS, BLOCK, 2 * BLOCK), F32), pltpu.VMEM((N_HEADS, BLOCK, 2 * BLOCK), F32),
                        pltpu.VMEM((N_HEADS, 1, 1), F32), pltpu.VMEM((BLOCK, KV_W), F32), pltpu.VMEM((BLOCK, KV_W), F32),
                        pltpu.VMEM((N_HEADS, BLOCK, 2 * BLOCK), F32), pltpu.VMEM((N_HEADS, BLOCK, 2 * BLOCK), F32),
                        pltpu.VMEM((N_HEADS, BLOCK, 2 * BLOCK), BF16), pltpu.VMEM((N_HEADS, BLOCK, 2 * BLOCK), BF16)],
        compiler_params=_params(("arbitrary",)), inputs=(q, datt, k, k, v, v, bucket, rel_bias, sinks))


SCAN_UNROLL = 4


def _cmul(ar, ai, br, bi):
    return ar * br - ai * bi, ar * bi + ai * br


def _cmul_conj(ar, ai, br, bi):
    return ar * br + ai * bi, ar * bi - ai * br


def _ssm_discretize(lr, li, ldt):
    dt = jnp.exp(ldt)
    mag = jnp.exp(lr * dt)
    ab_re = mag * jnp.cos(li * dt)
    ab_im = mag * jnp.sin(li * dt)
    nr = ab_re - 1.0
    den = lr * lr + li * li
    f_re = (nr * lr + ab_im * li) / den
    f_im = (ab_im * lr - nr * li) / den
    return ab_re, ab_im, f_re, f_im


def _ssm_prep(lam_re, lam_im, ldt_rep, bd_re, bd_im):
    def body(lr_ref, li_ref, ldt_ref, bdr_ref, bdi_ref, ar_ref, ai_ref, br_ref, bi_ref):
        ab_re, ab_im, f_re, f_im = _ssm_discretize(lr_ref[...], li_ref[...], ldt_ref[...])
        ar_ref[...] = ab_re
        ai_ref[...] = ab_im
        bdr, bdi = bdr_ref[0], bdi_ref[0]
        br_ref[0] = (bdr * f_re - bdi * f_im).astype(BF16)
        bi_ref[0] = (bdi * f_re + bdr * f_im).astype(BF16)

    row = pl.BlockSpec((1, SSM_LANE_BLOCK), lambda j: (0, j))
    mat = pl.BlockSpec((1, LANES, SSM_LANE_BLOCK), lambda j: (j, 0, 0))
    return pl.pallas_call(
        body, name="ssm_prep", grid=(N_SSM_BLOCKS,),
        in_specs=[row, row, row, mat, mat], out_specs=[row, row, mat, mat],
        out_shape=[jax.ShapeDtypeStruct((1, STATES), F32)] * 2 + [jax.ShapeDtypeStruct((N_SSM_BLOCKS, LANES, SSM_LANE_BLOCK), BF16)] * 2,
        compiler_params=_params(("arbitrary",)),
    )(*_in_hbm(lam_re, lam_im, ldt_rep, bd_re, bd_im))


def _ssm_prep_bwd(lam_re, lam_im, ldt_rep, bd_re, bd_im, dbr, dbi, da_re, da_im):
    def body(lr_ref, li_ref, ldt_ref, bdr_ref, bdi_ref, dbr_ref, dbi_ref, dar_ref, dai_ref,
             dbdr_ref, dbdi_ref, dlr_ref, dli_ref, dldt_ref):
        lr, li, ldt = lr_ref[...], li_ref[...], ldt_ref[...]
        (_, _, f_re, f_im), vjp = jax.vjp(_ssm_discretize, lr, li, ldt)
        bdr, bdi, gbr, gbi = bdr_ref[0], bdi_ref[0], dbr_ref[0], dbi_ref[0]
        dbdr_ref[0] = gbr * f_re + gbi * f_im
        dbdi_ref[0] = gbi * f_re - gbr * f_im
        df_re = jnp.sum(gbr * bdr + gbi * bdi, axis=0, keepdims=True)
        df_im = jnp.sum(gbi * bdr - gbr * bdi, axis=0, keepdims=True)
        dlr, dli, dldt = vjp((dar_ref[...], dai_ref[...], df_re, df_im))
        dlr_ref[...] = dlr
        dli_ref[...] = dli
        dldt_ref[...] = dldt

    row = pl.BlockSpec((1, SSM_LANE_BLOCK), lambda j: (0, j))
    mat = pl.BlockSpec((1, LANES, SSM_LANE_BLOCK), lambda j: (j, 0, 0))
    mat_shape = jax.ShapeDtypeStruct((N_SSM_BLOCKS, LANES, SSM_LANE_BLOCK), F32)
    row_shape = jax.ShapeDtypeStruct((1, STATES), F32)
    return pl.pallas_call(
        body, name="ssm_prep_bwd", grid=(N_SSM_BLOCKS,),
        in_specs=[row, row, row, mat, mat, mat, mat, row, row], out_specs=[mat, mat, row, row, row],
        out_shape=[mat_shape, mat_shape, row_shape, row_shape, row_shape],
        compiler_params=_params(("arbitrary",)),
    )(*_in_hbm(lam_re, lam_im, ldt_rep, bd_re, bd_im, dbr, dbi, da_re, da_im))


def _group_sum(x):
    def body(x_ref, o_ref):
        o_ref[...] = jnp.sum(x_ref[...], axis=1, keepdims=True)
    return pl.pallas_call(body, name="ssm_group_sum", grid=(1,), in_specs=[_whole(x.shape)], out_specs=_whole((N_GROUPS, 1)),
                          out_shape=jax.ShapeDtypeStruct((N_GROUPS, 1), F32))(*_in_hbm(x))


def _power_table(ar, ai, p_re_ref, p_im_ref, steps):
    shape = (SUBLANES, SSM_LANE_BLOCK)
    p_re_ref[0:SUBLANES] = jnp.broadcast_to(ar, shape)
    p_im_ref[0:SUBLANES] = jnp.broadcast_to(ai, shape)
    m = 1
    while m < steps:
        rows = m * SUBLANES
        top_re = p_re_ref[rows - SUBLANES:rows]
        top_im = p_im_ref[rows - SUBLANES:rows]
        cur_re = p_re_ref[0:rows].reshape(m, SUBLANES, SSM_LANE_BLOCK)
        cur_im = p_im_ref[0:rows].reshape(m, SUBLANES, SSM_LANE_BLOCK)
        nxt_re, nxt_im = _cmul(cur_re, cur_im, top_re[None], top_im[None])
        p_re_ref[rows:2 * rows] = nxt_re.reshape(rows, SSM_LANE_BLOCK)
        p_im_ref[rows:2 * rows] = nxt_im.reshape(rows, SSM_LANE_BLOCK)
        m *= 2


def _to_segments(src_ref, dst_ref, steps):
    for s in range(SUBLANES):
        dst_ref[pl.ds(s, steps, stride=SUBLANES), :] = src_ref[s * steps:(s + 1) * steps, :]


def _from_segments(src_ref, dst_ref, steps):
    for s in range(SUBLANES):
        dst_ref[s * steps:(s + 1) * steps, :] = src_ref[pl.ds(s, steps, stride=SUBLANES), :]


def _segment_carries(e_re, e_im, an_re, an_im, c_re, c_im, reverse):
    order = range(SUBLANES - 1, -1, -1) if reverse else range(SUBLANES)
    ins_re, ins_im = [None] * SUBLANES, [None] * SUBLANES
    for s in order:
        ins_re[s], ins_im[s] = c_re, c_im
        pr, pi = _cmul(an_re, an_im, c_re, c_im)
        c_re = e_re[s:s + 1] + pr
        c_im = e_im[s:s + 1] + pi
    return jnp.concatenate(ins_re, axis=0), jnp.concatenate(ins_im, axis=0), c_re, c_im


def _ssm_fwd(u, a_re, a_im, b_re, b_im, c_re, c_im, d_skip, chunk, carry=None):
    T = u.shape[0]
    nc = T // chunk
    steps = chunk // SUBLANES
    blk = SSM_LANE_BLOCK

    def body(u_ref, ar_ref, ai_ref, br_ref, bi_ref, cr_ref, ci_ref, dk_ref,
             y_ref, hr_ref, hi_ref, inr_ref, ini_ref, useg_ref, yseg_ref, pr_ref, pi_ref, carry_ref):
        c = pl.program_id(1)
        ar, ai = ar_ref[...], ai_ref[...]

        @pl.when(c == 0)
        def _():
            _power_table(ar, ai, pr_ref, pi_ref, steps)
            carry_ref[...] = jnp.zeros_like(carry_ref)

        _to_segments(u_ref, useg_ref, steps)
        ub = useg_ref[...].astype(BF16)
        hr_ref[...] = _dot(ub, br_ref[0])
        hi_ref[...] = _dot(ub, bi_ref[0])
        first = slice(0, SUBLANES)

        def scan(t4, prev):
            for j in range(SCAN_UNROLL):
                rows = pl.ds(pl.multiple_of((t4 * SCAN_UNROLL + j) * SUBLANES, SUBLANES), SUBLANES)
                pr, pi = _cmul(pr_ref[first, :], pi_ref[first, :], prev[0], prev[1])
                prev = (pr + hr_ref[rows, :], pi + hi_ref[rows, :])
                hr_ref[rows, :] = prev[0]
                hi_ref[rows, :] = prev[1]
            return prev

        zero = jnp.zeros((SUBLANES, blk), F32)
        lax.fori_loop(0, steps // SCAN_UNROLL, scan, (zero, zero))

        top = slice(chunk - SUBLANES, chunk)
        in_re, in_im, out_re, out_im = _segment_carries(
            hr_ref[top, :], hi_ref[top, :], pr_ref[top, :][0:1], pi_ref[top, :][0:1],
            carry_ref[0:1, :], carry_ref[1:2, :], reverse=False)
        carry_ref[0:1, :] = out_re
        carry_ref[1:2, :] = out_im
        inr_ref[...] = in_re
        ini_ref[...] = in_im

        def fix(t4, _):
            for j in range(SCAN_UNROLL):
                rows = pl.ds(pl.multiple_of((t4 * SCAN_UNROLL + j) * SUBLANES, SUBLANES), SUBLANES)
                fr, fi = _cmul(pr_ref[rows, :], pi_ref[rows, :], in_re, in_im)
                hr_ref[rows, :] += fr
                hi_ref[rows, :] += fi
            return 0

        lax.fori_loop(0, steps // SCAN_UNROLL, fix, 0)

        yseg_ref[...] = _dot(hr_ref[...].astype(BF16), cr_ref[0]) - _dot(hi_ref[...].astype(BF16), ci_ref[0])
        _from_segments(yseg_ref, y_ref, steps)
        y_ref[...] += dk_ref[...] * u_ref[...]

    row = pl.BlockSpec((1, blk), lambda j, c: (0, j))
    b_mat = pl.BlockSpec((1, LANES, blk), lambda j, c: (j, 0, 0))
    c_mat = pl.BlockSpec((1, blk, LANES), lambda j, c: (j, 0, 0))
    tok = pl.BlockSpec((chunk, LANES), lambda j, c: (c, j))
    state = pl.BlockSpec((chunk, blk), lambda j, c: (c, j))
    enter = pl.BlockSpec((SUBLANES, blk), lambda j, c: (c, j))
    return _hosted_call(
        body, carry, _edge_2d(N_SSM_BLOCKS, nc), name="ssm_fwd", grid=(N_SSM_BLOCKS, nc),
        in_specs=[tok, row, row, b_mat, b_mat, c_mat, c_mat, pl.BlockSpec((1, LANES), lambda j, c: (0, j))],
        out_specs=[tok, state, state, enter, enter],
        out_shape=[jax.ShapeDtypeStruct((T, SSM_W), F32), jax.ShapeDtypeStruct((T, STATES), F32),
                   jax.ShapeDtypeStruct((T, STATES), F32), jax.ShapeDtypeStruct((nc * SUBLANES, STATES), F32),
                   jax.ShapeDtypeStruct((nc * SUBLANES, STATES), F32)],
        scratch_shapes=[pltpu.VMEM((chunk, LANES), F32), pltpu.VMEM((chunk, LANES), F32),
                        pltpu.VMEM((chunk, blk), F32), pltpu.VMEM((chunk, blk), F32), pltpu.VMEM((SUBLANES, blk), F32)],
        compiler_params=_params(("arbitrary", "arbitrary"), VMEM_MID),
        inputs=(u, a_re, a_im, b_re, b_im, c_re, c_im, d_skip))


def _ssm_bwd(dy, u, h_re, h_im, in_re, in_im, a_re, a_im, b_re, b_im, c_re, c_im, d_skip, chunk, carry=None):
    T = u.shape[0]
    nc = T // chunk
    steps = chunk // SUBLANES
    blk = SSM_LANE_BLOCK

    def body(dy_ref, u_ref, hr_ref, hi_ref, inr_ref, ini_ref, ar_ref, ai_ref, br_ref, bi_ref, cr_ref, ci_ref, dk_ref,
             du_ref, dbr_ref, dbi_ref, dcr_ref, dci_ref, dar_ref, dai_ref, ddk_ref,
             dyseg_ref, useg_ref, duseg_ref, gr_ref, gi_ref, pr_ref, pi_ref, carry_ref, accr_ref, acci_ref):
        c = pl.program_id(1)
        ar, ai = ar_ref[...], ai_ref[...]

        @pl.when(c == 0)
        def _():
            _power_table(ar, ai, pr_ref, pi_ref, steps)
            carry_ref[...] = jnp.zeros_like(carry_ref)
            accr_ref[...] = jnp.zeros_like(accr_ref)
            acci_ref[...] = jnp.zeros_like(acci_ref)

        _to_segments(dy_ref, dyseg_ref, steps)
        _to_segments(u_ref, useg_ref, steps)
        dyb = dyseg_ref[...].astype(BF16)
        ub = useg_ref[...].astype(BF16)
        gr_ref[...] = _dot_nt(dyb, cr_ref[0])
        gi_ref[...] = -_dot_nt(dyb, ci_ref[0])
        dcr = _dot_tn(hr_ref[...].astype(BF16), dyb)
        dci = -_dot_tn(hi_ref[...].astype(BF16), dyb)
        ddk = jnp.sum(dy_ref[...] * u_ref[...], axis=0, keepdims=True)

        first = slice(0, SUBLANES)

        def scan(k4, nxt):
            for j in range(SCAN_UNROLL):
                t = steps - 1 - (k4 * SCAN_UNROLL + j)
                rows = pl.ds(pl.multiple_of(t * SUBLANES, SUBLANES), SUBLANES)
                pr, pi = _cmul_conj(pr_ref[first, :], pi_ref[first, :], nxt[0], nxt[1])
                nxt = (pr + gr_ref[rows, :], pi + gi_ref[rows, :])
                gr_ref[rows, :] = nxt[0]
                gi_ref[rows, :] = nxt[1]
            return nxt

        top = slice(chunk - SUBLANES, chunk)
        zero = jnp.zeros((SUBLANES, blk), F32)
        lax.fori_loop(0, steps // SCAN_UNROLL, scan, (zero, zero))

        gin_re, gin_im, out_re, out_im = _segment_carries(
            gr_ref[0:SUBLANES, :], gi_ref[0:SUBLANES, :], pr_ref[top, :][0:1], -pi_ref[top, :][0:1],
            carry_ref[0:1, :], carry_ref[1:2, :], reverse=True)
        carry_ref[0:1, :] = out_re
        carry_ref[1:2, :] = out_im

        def fix_row(rows, prow, hp_re, hp_im, acc):
            fr, fi = _cmul_conj(pr_ref[prow, :], pi_ref[prow, :], gin_re, gin_im)
            g_re = gr_ref[rows, :] + fr
            g_im = gi_ref[rows, :] + fi
            gr_ref[rows, :] = g_re
            gi_ref[rows, :] = g_im
            return acc[0] + g_re * hp_re + g_im * hp_im, acc[1] + g_im * hp_re - g_re * hp_im

        def fix_at(t, acc):
            aligned = (lambda r: r * SUBLANES) if isinstance(t, int) else (lambda r: pl.multiple_of(r * SUBLANES, SUBLANES))
            rows, before, prow = (pl.ds(aligned(r), SUBLANES) for r in (t, t - 1, steps - 1 - t))
            return fix_row(rows, prow, hr_ref[before, :], hi_ref[before, :], acc)

        def fix(t4, acc):
            for j in range(SCAN_UNROLL):
                acc = fix_at(t4 * SCAN_UNROLL + j, acc)
            return acc

        acc = fix_row(first, top, inr_ref[...], ini_ref[...], (accr_ref[...], acci_ref[...]))
        for t in range(1, SCAN_UNROLL):
            acc = fix_at(t, acc)
        acc_re, acc_im = lax.fori_loop(1, steps // SCAN_UNROLL, fix, acc)
        accr_ref[...] = acc_re
        acci_ref[...] = acc_im

        gbr = gr_ref[...].astype(BF16)
        gbi = gi_ref[...].astype(BF16)
        duseg_ref[...] = _dot_nt(gbr, br_ref[0]) + _dot_nt(gbi, bi_ref[0])
        _from_segments(duseg_ref, dyseg_ref, steps)
        du_ref[...] = (dyseg_ref[...] + dk_ref[...] * dy_ref[...]).astype(BF16)
        dbr = _dot_tn(ub, gbr)
        dbi = _dot_tn(ub, gbi)

        @pl.when(c == 0)
        def _():
            dbr_ref[0] = dbr
            dbi_ref[0] = dbi
            dcr_ref[0] = dcr
            dci_ref[0] = dci
            ddk_ref[...] = ddk

        @pl.when(c > 0)
        def _():
            dbr_ref[0] += dbr
            dbi_ref[0] += dbi
            dcr_ref[0] += dcr
            dci_ref[0] += dci
            ddk_ref[...] += ddk

        @pl.when(c == nc - 1)
        def _():
            dar_ref[...] = jnp.sum(acc_re, axis=0, keepdims=True)
            dai_ref[...] = jnp.sum(acc_im, axis=0, keepdims=True)

    rev = lambda c: nc - 1 - c
    row = pl.BlockSpec((1, blk), lambda j, c: (0, j))
    b_mat = pl.BlockSpec((1, LANES, blk), lambda j, c: (j, 0, 0))
    c_mat = pl.BlockSpec((1, blk, LANES), lambda j, c: (j, 0, 0))
    tok = pl.BlockSpec((chunk, LANES), lambda j, c: (rev(c), j))
    state = pl.BlockSpec((chunk, blk), lambda j, c: (rev(c), j))
    enter = pl.BlockSpec((SUBLANES, blk), lambda j, c: (rev(c), j))
    chan = pl.BlockSpec((1, LANES), lambda j, c: (0, j))
    f32 = lambda *s: jax.ShapeDtypeStruct(s, F32)
    return _hosted_call(
        body, carry, _edge_2d(N_SSM_BLOCKS, nc), name="ssm_bwd", grid=(N_SSM_BLOCKS, nc),
        in_specs=[tok, tok, state, state, enter, enter, row, row, b_mat, b_mat, c_mat, c_mat, chan],
        out_specs=[tok, b_mat, b_mat, c_mat, c_mat, row, row, chan],
        out_shape=[jax.ShapeDtypeStruct((T, SSM_W), BF16), f32(N_SSM_BLOCKS, LANES, blk), f32(N_SSM_BLOCKS, LANES, blk),
                   f32(N_SSM_BLOCKS, blk, LANES), f32(N_SSM_BLOCKS, blk, LANES), f32(1, STATES), f32(1, STATES), f32(1, SSM_W)],
        scratch_shapes=[pltpu.VMEM((chunk, LANES), F32), pltpu.VMEM((chunk, LANES), F32), pltpu.VMEM((chunk, LANES), F32),
                        pltpu.VMEM((chunk, blk), F32), pltpu.VMEM((chunk, blk), F32),
                        pltpu.VMEM((chunk, blk), F32), pltpu.VMEM((chunk, blk), F32),
                        pltpu.VMEM((SUBLANES, blk), F32), pltpu.VMEM((SUBLANES, blk), F32), pltpu.VMEM((SUBLANES, blk), F32)],
        compiler_params=_params(("arbitrary", "arbitrary"), VMEM_BIG),
        inputs=(dy, u, h_re, h_im, in_re, in_im, a_re, a_im, b_re, b_im, c_re, c_im, d_skip))


def _merge_forward(y, att, ga, gs, w_glu, w_ssm, w_attn):
    z = jax.nn.gelu(y)
    zb = z.astype(BF16)
    gl = jax.nn.sigmoid(_dot(zb, w_glu))
    z2b = (z * gl).astype(BF16)
    y_ssm = _dot(z2b, w_ssm)
    y_attn = _dot(att, w_attn)
    sa = jax.nn.sigmoid(ga)
    ss = jax.nn.sigmoid(gs)
    merged = (sa * y_attn + ss * y_ssm).astype(BF16)
    return z, zb, gl, z2b, y_ssm, y_attn, sa, ss, merged


def _merge_fwd(x, y, att, ga, gs, g2, g3, w_glu, w_ssm, w_attn, w_out, tile):
    T = x.shape[0]

    def body(x_hbm, y_hbm, att_hbm, ga_hbm, gs_hbm, g2_ref, g3_ref, wg_ref, ws_ref, wa_ref, wo_ref, x1_hbm, o_hbm, h2_hbm):
        def step(x_ref, y_ref, att_ref, ga_ref, gs_ref, x1_ref, o_ref, h2_ref):
            merged = _merge_forward(y_ref[...], att_ref[...], ga_ref[...], gs_ref[...], wg_ref[...], ws_ref[...], wa_ref[...])[-1]
            o = _dot(merged, wo_ref[...])
            x1 = x_ref[...] + o * _rms_scale(o) * g2_ref[...]
            o_ref[...] = o
            x1_ref[...] = x1
            h2_ref[...] = (x1 * _rms_scale(x1) * g3_ref[...]).astype(BF16)

        tok = lambda w, **mode: pl.BlockSpec((tile, w), lambda i: (i, 0), **mode)
        tok_in = lambda w: tok(w, pipeline_mode=pl.Buffered(3))
        pltpu.emit_pipeline(
            step, grid=(T // tile,),
            in_specs=[tok_in(D_MODEL), tok_in(SSM_W), tok_in(ATTN_W), tok_in(D_MODEL), tok_in(D_MODEL)],
            out_specs=[tok(D_MODEL), tok(D_MODEL), tok(D_MODEL)],
        )(x_hbm, y_hbm, att_hbm, ga_hbm, gs_hbm, x1_hbm, o_hbm, h2_hbm)

    in_vmem = pl.BlockSpec(memory_space=pltpu.VMEM)
    return pl.pallas_call(
        body, name="merge_fwd", in_specs=[HBM_SPEC] * 5 + [in_vmem] * 6, out_specs=[HBM_SPEC] * 3,
        out_shape=_hbm_out([jax.ShapeDtypeStruct((T, D_MODEL), F32), jax.ShapeDtypeStruct((T, D_MODEL), F32),
                            jax.ShapeDtypeStruct((T, D_MODEL), BF16)]),
        compiler_params=pltpu.CompilerParams(vmem_limit_bytes=VMEM_BIG),
    )(*_in_hbm(x, y, att, ga, gs), g2, g3, w_glu, w_ssm, w_attn, w_out)


def _merge_bwd(dh2, dx2, x1, o, y, att, ga, gs, g2, g3, w_glu, w_ssm, w_attn, w_out, tile, carry=None):
    T = x1.shape[0]
    n_steps = T // tile

    group = min(2, n_steps)
    staged_widths = (D_MODEL, D_MODEL, ATTN_W, D_MODEL, SSM_W, D_MODEL, SSM_W, SSM_W)

    def body(dh2_ref, dx2_ref, x1_ref, o_ref, y_ref, att_ref, ga_ref, gs_ref, g2_ref, g3_ref, wg_ref, ws_ref, wa_ref, wo_ref,
             dx1_ref, dgates_ref, datt_ref, dy_ref, dwg_hbm, dws_hbm, dwa_hbm, dwo_hbm, dg2_ref, dg3_ref,
             awg_ref, aws_ref, awa_ref, awo_ref, *staged):
        i = pl.program_id(0)
        x1v, ov = x1_ref[...], o_ref[...]
        dxn, dg3 = _rms_bwd(dh2_ref[...], x1v, _rms_scale(x1v), g3_ref[...])
        dx1 = dx2_ref[...] + dxn
        dx1_ref[...] = dx1
        do, dg2 = _rms_bwd(dx1, ov, _rms_scale(ov), g2_ref[...])
        dob = do.astype(BF16)

        yv = y_ref[...]
        att = att_ref[...]
        z, zb, gl, z2b, y_ssm, y_attn, sa, ss, merged = _merge_forward(
            yv, att, ga_ref[...], gs_ref[...], wg_ref[...], ws_ref[...], wa_ref[...])
        dmerged = _dot_nt(dob, wo_ref[...])
        dya = (dmerged * sa).astype(BF16)
        dys = (dmerged * ss).astype(BF16)
        dgates_ref[:, :D_MODEL] = (dmerged * y_attn * sa * (1.0 - sa)).astype(BF16)
        dgates_ref[:, D_MODEL:] = (dmerged * y_ssm * ss * (1.0 - ss)).astype(BF16)
        datt_ref[...] = _dot_nt(dya, wa_ref[...]).astype(BF16)
        dz2 = _dot_nt(dys, ws_ref[...])
        dpre = (dz2 * z * gl * (1.0 - gl)).astype(BF16)
        dz = dz2 * gl + _dot_nt(dpre, wg_ref[...])
        _, gelu_vjp = jax.vjp(jax.nn.gelu, yv)
        dy_ref[...] = gelu_vjp(dz)[0]

        part = pl.ds(pl.multiple_of((i % group) * tile, tile), tile)
        for ref, val in zip(staged, (merged, dob, att, dya, z2b, dys, zb, dpre)):
            ref[part, :] = val

        @pl.when(i == 0)
        def _():
            dg2_ref[...] = dg2
            dg3_ref[...] = dg3

        @pl.when(i > 0)
        def _():
            dg2_ref[...] += dg2
            dg3_ref[...] += dg3

        def weight_grads():
            s_merged, s_dob, s_att, s_dya, s_z2b, s_dys, s_zb, s_dpre = (ref[...] for ref in staged)
            return ((awo_ref, _dot_tn(s_merged, s_dob)), (awa_ref, _dot_tn(s_att, s_dya)),
                    (aws_ref, _dot_tn(s_z2b, s_dys)), (awg_ref, _dot_tn(s_zb, s_dpre)))

        @pl.when(i == group - 1)
        def _():
            for ref, val in weight_grads():
                ref[...] = val

        @pl.when((i % group == group - 1) & (i > group - 1))
        def _():
            for ref, val in weight_grads():
                ref[...] += val

        @pl.when(i == n_steps - 1)
        def _():
            pltpu.sync_copy(awg_ref, dwg_hbm)
            pltpu.sync_copy(aws_ref, dws_hbm)
            pltpu.sync_copy(awa_ref, dwa_hbm)
            pltpu.sync_copy(awo_ref, dwo_hbm)

    tok = lambda w: pl.BlockSpec((tile, w), lambda i: (i, 0))
    vec = _const_spec((1, D_MODEL))
    any_ = pl.BlockSpec(memory_space=pl.ANY)
    vec_out = pl.BlockSpec((1, D_MODEL), lambda i: (0, 0))
    f32 = lambda *s: jax.ShapeDtypeStruct(s, F32)
    bf = lambda *s: jax.ShapeDtypeStruct(s, BF16)
    return _hosted_call(
        body, carry, _edge_1d(n_steps), name="merge_bwd", grid=(n_steps,),
        in_specs=[tok(D_MODEL), tok(D_MODEL), tok(D_MODEL), tok(D_MODEL), tok(SSM_W), tok(ATTN_W), tok(D_MODEL), tok(D_MODEL),
                  vec, vec, _const_spec((SSM_W, SSM_W)), _const_spec((SSM_W, D_MODEL)), _const_spec((ATTN_W, D_MODEL)),
                  _const_spec((D_MODEL, D_MODEL))],
        out_specs=[tok(D_MODEL), tok(2 * D_MODEL), tok(ATTN_W), tok(SSM_W), any_, any_, any_, any_, vec_out, vec_out],
        out_shape=[f32(T, D_MODEL), bf(T, 2 * D_MODEL), bf(T, ATTN_W), f32(T, SSM_W),
                   f32(SSM_W, SSM_W), f32(SSM_W, D_MODEL), f32(ATTN_W, D_MODEL), f32(D_MODEL, D_MODEL),
                   f32(1, D_MODEL), f32(1, D_MODEL)],
        scratch_shapes=[pltpu.VMEM((SSM_W, SSM_W), F32), pltpu.VMEM((SSM_W, D_MODEL), F32),
                        pltpu.VMEM((ATTN_W, D_MODEL), F32), pltpu.VMEM((D_MODEL, D_MODEL), F32)]
        + [pltpu.VMEM((group * tile, wd), BF16) for wd in staged_widths],
        compiler_params=_params(("arbitrary",), VMEM_BIG),
        inputs=(dh2, dx2, x1, o, y, att, ga, gs, g2, g3, w_glu, w_ssm, w_attn, w_out))


FF_SHARD = D_FF // N_DEV


def _mlp_fwd(h2, x1, target, g4, w_ff_in, w_ff_out, tile):
    T = h2.shape[0]
    col_chunk = 2 * FF_SHARD

    def body(h2_ref, x1_ref, tg_ref, g4_ref, wi_ref, wo_ref, a_ref, dfo_ref, dx2_ref, loss_ref, dg4_ref, rr_ref):
        i = pl.program_id(0)
        h2v = h2_ref[...]
        for c in range(D_FF // col_chunk):
            cols = slice(c * col_chunk, (c + 1) * col_chunk)
            a = _dot_nt(h2v, wi_ref[cols, :])
            a_ref[:, cols] = a.astype(BF16)
            ra = jnp.maximum(a, 0.0)
            rr_ref[:, cols] = (ra * ra).astype(BF16)
        f = _dot(rr_ref[...], wo_ref[...])
        r = _rms_scale(f)
        g = g4_ref[...]
        err = x1_ref[...] + f * r * g - tg_ref[...]
        dx2 = err * (1.0 / D_MODEL)
        dx2_ref[...] = dx2
        dfo, dg = _rms_bwd(dx2, f, r, g)
        dfo_ref[...] = dfo.astype(BF16)
        row = lax.broadcasted_iota(jnp.int32, (SUBLANES, LANES), 0)
        col = lax.broadcasted_iota(jnp.int32, (SUBLANES, LANES), 1)
        loss = jnp.where((row == 0) & (col == 0), (0.5 / D_MODEL) * jnp.sum(err * err), 0.0)

        @pl.when(i == 0)
        def _():
            loss_ref[...] = loss
            dg4_ref[...] = dg

        @pl.when(i > 0)
        def _():
            loss_ref[...] += loss
            dg4_ref[...] += dg

    tok = pl.BlockSpec((tile, D_MODEL), lambda i: (i, 0))
    return pl.pallas_call(
        body, name="mlp_fwd", grid=(T // tile,),
        in_specs=[tok, tok, tok, _const_spec((1, D_MODEL)), _const_spec((D_FF, D_MODEL)), _const_spec((D_FF, D_MODEL))],
        out_specs=[pl.BlockSpec((tile, D_FF), lambda i: (i, 0)), tok, tok,
                   pl.BlockSpec((SUBLANES, LANES), lambda i: (0, 0)), pl.BlockSpec((1, D_MODEL), lambda i: (0, 0))],
        out_shape=_hbm_out([jax.ShapeDtypeStruct((T, D_FF), BF16), jax.ShapeDtypeStruct((T, D_MODEL), BF16),
                            jax.ShapeDtypeStruct((T, D_MODEL), F32), jax.ShapeDtypeStruct((SUBLANES, LANES), F32),
                            jax.ShapeDtypeStruct((1, D_MODEL), F32)]),
        scratch_shapes=[pltpu.VMEM((tile, D_FF), BF16)],
        compiler_params=_params(("arbitrary",), VMEM_MAX),
    )(*_in_hbm(h2, x1, target, g4, w_ff_in.reshape(D_FF, D_MODEL), w_ff_out.reshape(D_FF, D_MODEL)))


def _mlp_weight_grads(dfo, a, h2, w_ff_out, row_chunk):
    T = h2.shape[0]

    def body(dfo_ref, h2_ref, a_ref, wo_ref, dwi_ref, dwo_ref, da_ref, rr_ref):
        def rows(r, _):
            sl = pl.ds(pl.multiple_of(r * row_chunk, row_chunk), row_chunk)
            ra = jnp.maximum(a_ref[sl, :].astype(F32), 0.0)
            da_ref[sl, :] = (_dot_nt(dfo_ref[sl, :], wo_ref[0]) * (2.0 * ra)).astype(BF16)
            rr_ref[sl, :] = (ra * ra).astype(BF16)
            return 0

        lax.fori_loop(0, T // row_chunk, rows, 0)
        dwo_ref[0] = _dot_tn(rr_ref[...], dfo_ref[...])
        dwi_ref[0] = _dot_tn(h2_ref[...], da_ref[...])

    return pl.pallas_call(
        body, name="mlp_weight_grads", grid=(N_DEV,),
        in_specs=[_const_spec((T, D_MODEL)), _const_spec((T, D_MODEL)), pl.BlockSpec((T, FF_SHARD), lambda k: (0, k)),
                  pl.BlockSpec((1, FF_SHARD, D_MODEL), lambda k: (k, 0, 0))],
        out_specs=[pl.BlockSpec((1, D_MODEL, FF_SHARD), lambda k: (k, 0, 0)),
                   pl.BlockSpec((1, FF_SHARD, D_MODEL), lambda k: (k, 0, 0)), pl.BlockSpec((T, FF_SHARD), lambda k: (0, k))],
        out_shape=_hbm_out([jax.ShapeDtypeStruct((N_DEV, D_MODEL, FF_SHARD), F32),
                            jax.ShapeDtypeStruct((N_DEV, FF_SHARD, D_MODEL), F32), jax.ShapeDtypeStruct((T, D_FF), BF16)]),
        scratch_shapes=[pltpu.VMEM((T, FF_SHARD), BF16)],
        compiler_params=_params(("arbitrary",), VMEM_MAX),
    )(*_in_hbm(dfo, h2, a, w_ff_out))


def _mlp_input_grad(da, w_ff_in_t, tile):
    T = da.shape[0]

    def body(da_ref, w_ref, o_ref):
        o_ref[...] = _dot(da_ref[...], w_ref[...])

    return pl.pallas_call(
        body, name="mlp_input_grad", grid=(T // tile,),
        in_specs=[pl.BlockSpec((tile, D_FF), lambda i: (i, 0)), _const_spec((D_FF, D_MODEL))],
        out_specs=pl.BlockSpec((tile, D_MODEL), lambda i: (i, 0)),
        out_shape=_hbm_out(jax.ShapeDtypeStruct((T, D_MODEL), F32)),
        compiler_params=_params(("arbitrary",), VMEM_MID),
    )(*_in_hbm(da, w_ff_in_t))


def _block_diag_in(b):
    bt = b.reshape(N_SSM_BLOCKS, GROUPS_PER_BLOCK, GROUP_CH, N_STATE)
    eye = jnp.eye(GROUPS_PER_BLOCK, dtype=b.dtype)
    return jnp.einsum("jacp,ab->jacbp", bt, eye).reshape(N_SSM_BLOCKS, LANES, SSM_LANE_BLOCK)


def _block_diag_in_grad(g):
    g = g.reshape(N_SSM_BLOCKS, GROUPS_PER_BLOCK, GROUP_CH, GROUPS_PER_BLOCK, N_STATE)
    d = jnp.diagonal(g, axis1=1, axis2=3)
    return jnp.transpose(d, (0, 3, 1, 2)).reshape(N_GROUPS, GROUP_CH, N_STATE)


def _block_diag_out(c):
    ct = c.reshape(N_SSM_BLOCKS, GROUPS_PER_BLOCK, GROUP_CH, N_STATE)
    eye = jnp.eye(GROUPS_PER_BLOCK, dtype=c.dtype)
    return jnp.einsum("jacp,ab->japbc", ct, eye).reshape(N_SSM_BLOCKS, SSM_LANE_BLOCK, LANES)


def _block_diag_out_grad(g):
    g = g.reshape(N_SSM_BLOCKS, GROUPS_PER_BLOCK, N_STATE, GROUPS_PER_BLOCK, GROUP_CH)
    d = jnp.diagonal(g, axis1=1, axis2=3)
    return jnp.transpose(d, (0, 3, 2, 1)).reshape(N_GROUPS, GROUP_CH, N_STATE)


def _tiles(T):
    return dict(proj=min(512, T), proj_bwd=min(512, T // 2), merge=min(512, T), merge_bwd=min(256, T),
                mlp_fwd=min(512, T), mlp_bwd=min(512, T), ssm_chunk=min(1024, T))


def _mesh_position():
    x, y, c = lax.axis_index("x"), lax.axis_index("y"), lax.axis_index("c")
    other_chips = [(1 - x, y), (x, 1 - y), (1 - x, 1 - y)]
    return x, y, c, other_chips


def _gather_carry(arrays, pass_on=True):
    n = len(arrays)

    def copies(ins, outs, sems):
        send_sems, recv_sems, local_sems = sems
        x, y, c, chips = _mesh_position()
        me, sibling = (x, y, c), (x, y, 1 - c)

        def copy(a, k, block, to, src=None):
            px, py, pc = block
            dst = outs[a].at[4 * px + 2 * py + pc]
            return pltpu.make_async_remote_copy(
                src_ref=dst if src is None else src, dst_ref=dst, send_sem=send_sems.at[7 * a + k],
                recv_sem=recv_sems.at[7 * a + k], device_id=to, device_id_type=MESH_IDS)

        mine = [pltpu.make_async_copy(ins[a], outs[a].at[4 * x + 2 * y + c], local_sems.at[a]) for a in range(n)]
        first = []
        for a in range(n):
            first.append(copy(a, 0, me, sibling, src=ins[a]))
            first += [copy(a, 1 + j, me, (*chip, c), src=ins[a]) for j, chip in enumerate(chips)]
        return copy, mine, first, me, sibling, chips, c

    def start(ins, outs, sems):
        _, mine, first, *_ = copies(ins, outs, sems)
        for cp in mine + first:
            cp.start()

    def passed_on(copy, sibling, chips, c):
        return [copy(a, 4 + j, (*chip, c), sibling) for a in range(n) for j, chip in enumerate(chips)]

    def middle(ins, outs, sems):
        copy, _, _, me, sibling, chips, c = copies(ins, outs, sems)
        for a in range(n):
            for j, chip in enumerate(chips):
                copy(a, 1 + j, (*chip, c), me).wait_recv()
                copy(a, 4 + j, (*chip, c), sibling).start()

    def finish(ins, outs, sems):
        copy, mine, first, me, sibling, chips, c = copies(ins, outs, sems)
        for a in range(n):
            copy(a, 0, sibling, me).wait_recv()
            for j, chip in enumerate(chips):
                (copy(a, 4 + j, (*chip, 1 - c), me) if pass_on else copy(a, 1 + j, (*chip, c), me)).wait_recv()
        for cp in first + (passed_on(copy, sibling, chips, c) if pass_on else []):
            cp.wait_send()
        for cp in mine:
            cp.wait()

    return _Carry(arrays, [jax.ShapeDtypeStruct((N_DEV,) + a.shape, a.dtype) for a in arrays],
                  [pltpu.SemaphoreType.DMA((7 * n,)), pltpu.SemaphoreType.DMA((7 * n,)), pltpu.SemaphoreType.DMA((n,))],
                  start, finish, middle if pass_on else None)


def _gather_pass_on(gathered, name):
    n = len(gathered)

    def body(*refs):
        zones, send_sems, recv_sems = refs[:n], refs[2 * n], refs[2 * n + 1]
        x, y, c, chips = _mesh_position()
        copies = []
        for a in range(n):
            for j, (px, py) in enumerate(chips):
                block = zones[a].at[4 * px + 2 * py + c]
                copies.append(pltpu.make_async_remote_copy(
                    src_ref=block, dst_ref=block, send_sem=send_sems.at[3 * a + j], recv_sem=recv_sems.at[3 * a + j],
                    device_id=(x, y, 1 - c), device_id_type=MESH_IDS))
        for cp in copies:
            cp.start()
        for cp in copies:
            cp.wait()

    return pl.pallas_call(
        body, name=name, in_specs=[HBM_SPEC] * n, out_specs=[HBM_SPEC] * n,
        out_shape=_hbm_out([jax.ShapeDtypeStruct(g.shape, g.dtype) for g in gathered]),
        scratch_shapes=[pltpu.SemaphoreType.DMA((3 * n,)), pltpu.SemaphoreType.DMA((3 * n,))],
        input_output_aliases={a: a for a in range(n)})(*gathered)


def _pairwise_carry(arrays, n_slots, make_copies):
    n = len(arrays)

    def start(ins, outs, sems):
        for cp in make_copies(ins, outs, sems):
            cp.start()

    def finish(ins, outs, sems):
        for cp in make_copies(ins, outs, sems):
            cp.wait()

    return _Carry(arrays, [jax.ShapeDtypeStruct((n_slots,) + a.shape[1:], a.dtype) for a in arrays],
                  [pltpu.SemaphoreType.DMA((n_slots * n,)), pltpu.SemaphoreType.DMA((n_slots * n,))], start, finish)


def _sibling_carry(grads):
    def make_copies(ins, outs, sems):
        x, y, c, _ = _mesh_position()
        return [pltpu.make_async_remote_copy(
            src_ref=ins[a].at[2 * ch + (1 - c)], dst_ref=outs[a].at[ch], send_sem=sems[0].at[4 * a + ch],
            recv_sem=sems[1].at[4 * a + ch], device_id=(x, y, 1 - c), device_id_type=MESH_IDS)
            for a in range(len(grads)) for ch in range(4)]

    return _pairwise_carry(grads, 4, make_copies)


def _chips_carry(sums):
    def make_copies(ins, outs, sems):
        x, y, c, chips = _mesh_position()
        return [pltpu.make_async_remote_copy(
            src_ref=ins[a].at[2 * px + py], dst_ref=outs[a].at[j], send_sem=sems[0].at[3 * a + j],
            recv_sem=sems[1].at[3 * a + j], device_id=(px, py, c), device_id_type=MESH_IDS)
            for a in range(len(sums)) for j, (px, py) in enumerate(chips)]

    return _pairwise_carry(sums, 3, make_copies)


def _everyone_carry(arrays):
    def make_copies(ins, outs, sems):
        x, y, c, _ = _mesh_position()
        flip = lambda v, bit: 1 - v if bit else v
        return [pltpu.make_async_remote_copy(
            src_ref=ins[a], dst_ref=outs[a].at[r - 1], send_sem=sems[0].at[7 * a + r - 1], recv_sem=sems[1].at[7 * a + r - 1],
            device_id=(flip(x, r & 4), flip(y, r & 2), flip(c, r & 1)), device_id_type=MESH_IDS)
            for a in range(len(arrays)) for r in range(1, N_DEV)]

    carry = _pairwise_carry([jax.ShapeDtypeStruct((1,) + a.shape, a.dtype) for a in arrays], N_DEV - 1, make_copies)
    carry.inputs = list(arrays)
    return carry


def _sum_everyone(own, received, me, name, after=()):
    def body(me_ref, own_ref, r_ref, *refs):
        g = None
        for d in range(N_DEV):
            relation = jnp.bitwise_xor(d, me_ref[0])
            part = jnp.where(relation == 0, own_ref[...], r_ref[jnp.maximum(relation - 1, 0)])
            g = part if g is None else g + part
        refs[-1][...] = g

    whole = lambda shape: pl.BlockSpec(shape, lambda i, me_ref: (0,) * len(shape))
    return pl.pallas_call(
        body, name=name,
        grid_spec=pltpu.PrefetchScalarGridSpec(
            num_scalar_prefetch=1, grid=(1,), in_specs=[whole(own.shape), whole(received.shape)] + [HBM_SPEC] * len(after),
            out_specs=whole(own.shape)),
        out_shape=jax.ShapeDtypeStruct(own.shape, F32))(me, *_in_hbm(own, received), *after)


SEM_SPEC = pl.BlockSpec(memory_space=pltpu.SEMAPHORE)
DATAFLOW_EFFECT = pltpu.SideEffectType.DATAFLOW_SIDE_EFFECTING


def _exchange_start(carry, name, after=()):
    n, n_sems = len(carry.inputs), len(carry.sems)
    lands = [lax.empty(s.shape, s.dtype) for s in carry.out_shapes]

    def body(*refs):
        first_out = 2 * n + len(after)
        srcs, zones, sems, token = refs[:n], refs[n:2 * n], refs[first_out:first_out + n_sems], refs[-1]
        carry.start(srcs, zones, sems)
        token[...] = jnp.zeros_like(token)

    outs = pl.pallas_call(
        body, name=name, in_specs=[HBM_SPEC] * (2 * n + len(after)),
        out_specs=[SEM_SPEC] * n_sems + [HBM_SPEC] * (2 * n) + [pl.BlockSpec(memory_space=pltpu.VMEM)],
        out_shape=list(carry.sems) + _hbm_out([jax.ShapeDtypeStruct(a.shape, a.dtype) for a in carry.inputs])
        + _hbm_out(carry.out_shapes) + [jax.ShapeDtypeStruct((SUBLANES, LANES), F32)],
        input_output_aliases={j: n_sems + j for j in range(2 * n)},
        compiler_params=pltpu.CompilerParams(has_side_effects=DATAFLOW_EFFECT),
    )(*_in_hbm(*carry.inputs, *lands), *after)
    return outs[:-1], outs[-1]


def _exchange_wait(carry, in_flight, after, name):
    n, n_sems = len(carry.inputs), len(carry.sems)
    sems, srcs, zones = in_flight[:n_sems], in_flight[n_sems:n_sems + n], in_flight[n_sems + n:]

    def body(*refs):
        src_refs, zone_refs, sem_refs = refs[:n], refs[n:2 * n], refs[2 * n:2 * n + n_sems]
        carry.finish(src_refs, zone_refs, sem_refs)

    outs = pl.pallas_call(
        body, name=name, in_specs=[HBM_SPEC] * (2 * n) + [SEM_SPEC] * n_sems + [HBM_SPEC] * len(after),
        out_specs=[HBM_SPEC] * (2 * n),
        out_shape=_hbm_out([jax.ShapeDtypeStruct(a.shape, a.dtype) for a in carry.inputs]) + _hbm_out(carry.out_shapes),
        input_output_aliases={j: j for j in range(2 * n)},
        compiler_params=pltpu.CompilerParams(has_side_effects=DATAFLOW_EFFECT),
    )(*srcs, *zones, *sems, *after)
    return list(outs[:n]), list(outs[n:])


def _add_sibling(grads8, recvs, place, row_tiles, name):
    k = len(grads8)
    g4 = [g.reshape(4, 2, *g.shape[1:]) for g in grads8]

    def body(place_ref, *refs):
        g_refs, r_refs, o_refs, ob_refs = (refs[j * k:(j + 1) * k] for j in range(4))
        own = pl.program_id(1) == place_ref[1]
        for g_ref, r_ref, o_ref, ob_ref in zip(g_refs, r_refs, o_refs, ob_refs):
            s = g_ref[0] + r_ref[...]
            ob_ref[...] = s.astype(BF16)

            @pl.when(own)
            def _(o_ref=o_ref, s=s):
                o_ref[...] = s[0]

    def blocks(make):
        return [make(g.shape[1] // row_tiles, g.shape[2]) for g in grads8]

    slot = lambda tr, C: pl.BlockSpec((1, tr, C), lambda r, ch, place_ref: (ch, r, 0))
    outs = pl.pallas_call(
        body, name=name,
        grid_spec=pltpu.PrefetchScalarGridSpec(
            num_scalar_prefetch=1, grid=(row_tiles, 4),
            in_specs=blocks(lambda tr, C: pl.BlockSpec((1, 1, tr, C), lambda r, ch, place_ref: (ch, place_ref[0], r, 0)))
            + blocks(slot),
            out_specs=blocks(lambda tr, C: pl.BlockSpec((tr, C), lambda r, ch, place_ref: (r, 0))) + blocks(slot)),
        out_shape=_hbm_out([jax.ShapeDtypeStruct(g.shape[1:], F32) for g in grads8]
                           + [jax.ShapeDtypeStruct((4,) + g.shape[1:], BF16) for g in grads8]),
        compiler_params=_params(("arbitrary", "arbitrary")),
    )(place, *_in_hbm(*g4, *recvs))
    return list(outs[:k]), list(outs[k:])


def _adam_math(w, g, m, v):
    m = ADAM_B1 * m + (1.0 - ADAM_B1) * g
    v = ADAM_B2 * v + (1.0 - ADAM_B2) * jnp.square(g)
    m_hat = m / (1.0 - ADAM_B1 ** ADAM_STEP)
    v_hat = v / (1.0 - ADAM_B2 ** ADAM_STEP)
    delta = -ADAM_LR * (m_hat / (jnp.sqrt(v_hat) + ADAM_EPS) + ADAM_WD * w)
    return delta, m, v


def _adam_big(ws, ms, vs, chip_sums, recvs, row_tiles, name, after=()):
    k = len(ws)

    def step(*refs):
        w_refs, m_refs, v_refs, s_refs, r_refs, g_refs, d_refs, nm_refs, nv_refs = (refs[j * k:(j + 1) * k] for j in range(9))
        for a in range(k):
            r_ref = r_refs[a]
            g = s_refs[a][...] + r_ref[0].astype(F32) + r_ref[1].astype(F32) + r_ref[2].astype(F32)
            g_refs[a][...] = g
            d_refs[a][...], nm_refs[a][...], nv_refs[a][...] = _adam_math(w_refs[a][...], g, m_refs[a][...], v_refs[a][...])

    def blocks(make):
        return [make(w.shape[0] // row_tiles, w.shape[1]) for w in ws]

    blk = lambda tr, C, **mode: pl.BlockSpec((tr, C), lambda r: (r, 0), **mode)
    third = dict(pipeline_mode=pl.Buffered(3))
    in_specs = (blocks(lambda tr, C: blk(tr, C, **third)) * 4
                + blocks(lambda tr, C: pl.BlockSpec((3, tr, C), lambda r: (0, r, 0), **third)))

    def body(*refs):
        refs = refs[:5 * k] + refs[5 * k + len(after):]
        pltpu.emit_pipeline(step, grid=(row_tiles,), in_specs=in_specs, out_specs=blocks(blk) * 4)(*refs)

    outs = pl.pallas_call(
        body, name=name, in_specs=[HBM_SPEC] * (5 * k + len(after)), out_specs=[HBM_SPEC] * (4 * k),
        out_shape=[jax.ShapeDtypeStruct(w.shape, F32) for w in ws] * 4,
        compiler_params=pltpu.CompilerParams(vmem_limit_bytes=VMEM_MID),
    )(*_in_hbm(*ws, *ms, *vs, *chip_sums, *recvs), *after)
    return [list(outs[j * k:(j + 1) * k]) for j in range(4)]


def _sum_partials(partials, name, after=()):
    def body(p_ref, *refs):
        g = p_ref[0]
        for d in range(1, partials.shape[0]):
            g = g + p_ref[d]
        refs[-1][...] = g

    return pl.pallas_call(body, name=name, grid=(1,), in_specs=[_whole(partials.shape)] + [HBM_SPEC] * len(after),
                          out_specs=_whole(partials.shape[1:]),
                          out_shape=jax.ShapeDtypeStruct(partials.shape[1:], F32))(*_in_hbm(partials), *after)


def _adam_small(ws, ms, vs, gs):
    n = len(ws)

    def body(*refs):
        w_refs, m_refs, v_refs, g_refs = (refs[i * n:(i + 1) * n] for i in range(4))
        d_refs, nm_refs, nv_refs = (refs[(4 + i) * n:(5 + i) * n] for i in range(3))
        for j in range(n):
            d_refs[j][...], nm_refs[j][...], nv_refs[j][...] = _adam_math(
                w_refs[j][...], g_refs[j][...], m_refs[j][...], v_refs[j][...])

    specs = [_whole(w.shape) for w in ws]
    outs = pl.pallas_call(body, name="adam_small", grid=(1,), in_specs=specs * 4, out_specs=specs * 3,
                          out_shape=[jax.ShapeDtypeStruct(w.shape, F32) for w in ws] * 3,
                          compiler_params=_params(("arbitrary",), VMEM_MID))(*_in_hbm(*ws, *ms, *vs, *gs))
    return outs[:n], outs[n:2 * n], outs[2 * n:]


PACK_QUANTUM = SUBLANES * LANES


def _pack(named, names):
    parts = []
    for nme in names:
        flat = named[nme].reshape(-1)
        parts.append(jnp.pad(flat, (0, -flat.size % PACK_QUANTUM)))
    return jnp.concatenate(parts).reshape(-1, LANES)


def _unpack(packed, shapes, names):
    flat = packed.reshape(-1)
    out, pos = {}, 0
    for nme in names:
        size = math.prod(shapes[nme])
        out[nme] = flat[pos:pos + size].reshape(shapes[nme])
        pos += size + (-size % PACK_QUANTUM)
    return out


BIG = ("w_in", "w_glu", "w_attn_branch", "w_ssm_branch", "w_out", "w_ff_in", "w_ff_out")
COLUMN_SHARDED = ("w_in", "w_attn_branch", "w_ssm_branch", "w_ff_in")
SMALL = ("norm_mix_pre", "norm_mix_post", "norm_mlp_pre", "norm_mlp_post", "rel_bias", "sinks", "lam_re", "lam_im",
         "log_dt", "b_re", "b_im", "c_re", "c_im", "d_skip")
SWAPPED_SMALL = ("rel_bias", "b_re", "b_im")
SMALL_LATE = ("norm_mix_pre", "rel_bias", "sinks", "loss")
SMALL_BEFORE_ATTN_BWD = tuple(n for n in SMALL if n not in SMALL_LATE)
ALL_WEIGHTS = ("norm_mix_pre", "norm_mix_post", "norm_mlp_pre", "norm_mlp_post", "w_in", "rel_bias", "sinks", "lam_re",
               "lam_im", "log_dt", "b_re", "b_im", "c_re", "c_im", "d_skip", "w_glu", "w_attn_branch", "w_ssm_branch",
               "w_out", "w_ff_in", "w_ff_out")


def _full_from_gathered(name, gathered):
    _, r, c = gathered.shape
    if name in COLUMN_SHARDED:
        return jnp.transpose(gathered, (1, 0, 2)).reshape(r, N_DEV * c)
    return gathered.reshape(N_DEV * r, c)


def _blocks_from_full(name, full):
    r, c = full.shape
    if name in COLUMN_SHARDED:
        return jnp.transpose(full.reshape(r, N_DEV, c // N_DEV), (1, 0, 2))
    return full.reshape(N_DEV, r // N_DEV, c)


def kernel(x, norm_mix_pre, norm_mix_post, norm_mlp_pre, norm_mlp_post, w_in, rel_bias, sinks, lam_re, lam_im, log_dt, b_re, b_im, c_re, c_im, d_skip, w_glu, w_attn_branch, w_ssm_branch, w_out, w_ff_in, w_ff_out, loss_target, m_norm_mix_pre, m_norm_mix_post, m_norm_mlp_pre, m_norm_mlp_post, m_w_in, m_rel_bias, m_sinks, m_lam_re, m_lam_im, m_log_dt, m_b_re, m_b_im, m_c_re, m_c_im, m_d_skip, m_w_glu, m_w_attn_branch, m_w_ssm_branch, m_w_out, m_w_ff_in, m_w_ff_out, v_norm_mix_pre, v_norm_mix_post, v_norm_mlp_pre, v_norm_mlp_post, v_w_in, v_rel_bias, v_sinks, v_lam_re, v_lam_im, v_log_dt, v_b_re, v_b_im, v_c_re, v_c_im, v_d_skip, v_w_glu, v_w_attn_branch, v_w_ssm_branch, v_w_out, v_w_ff_in, v_w_ff_out):
    args = dict(locals())
    w = {n: args[n] for n in ALL_WEIGHTS}
    m = {n: args["m_" + n] for n in ALL_WEIGHTS}
    v = {n: args["v_" + n] for n in ALL_WEIGHTS}
    core = lax.axis_index("c").astype(jnp.int32).reshape(1)
    chip = (2 * lax.axis_index("x") + lax.axis_index("y")).astype(jnp.int32).reshape(1)
    xs, target = x[0], loss_target[0]
    t = _tiles(xs.shape[0])
    local = lambda d, n: d[n][0].T if n == "w_in" else d[n][0]
    gather_in = _gather_carry([local(w, "w_in").astype(BF16)], pass_on=False)
    gather_in_flight, token = _exchange_start(gather_in, "gather_w_in_start")
    one = token[0:1, 0:1] + 1.0
    shard = {n: (local(w, n) * one).astype(BF16) for n in BIG if n != "w_in"}
    shard["w_ff_in"] = shard["w_ff_in"].T
    view = lambda n, a: jnp.swapaxes(a, -1, -2) if n in SWAPPED_SMALL else a
    small = {n: (view(n, w[n]) if n == "rel_bias" else view(n, w[n])[0]) for n in SMALL}
    g1, g2, g3, g4 = (small[n].reshape(1, D_MODEL) for n in ("norm_mix_pre", "norm_mix_post", "norm_mlp_pre", "norm_mlp_post"))
    bucket = jnp.asarray(_bucket_table())
    rel_b, sink = small["rel_bias"], small["sinks"].reshape(1, N_HEADS)
    lam_r, lam_i = small["lam_re"].reshape(1, STATES), small["lam_im"].reshape(1, STATES)
    ldt_rep = jnp.repeat(small["log_dt"].reshape(N_GROUPS), N_STATE).reshape(1, STATES)
    bd_re, bd_im = _block_diag_in(small["b_re"] * one), _block_diag_in(small["b_im"] * one)
    cm_re, cm_im = _block_diag_out(small["c_re"] * one).astype(BF16), _block_diag_out(small["c_im"] * one).astype(BF16)
    dsk = small["d_skip"].reshape(1, SSM_W)
    a_re, a_im, bm_re, bm_im = _ssm_prep(lam_r, lam_i, ldt_rep, bd_re, bd_im)

    travelled_behind = list(shard.values()) + [a_re, a_im, bm_re, bm_im, cm_re, cm_im]
    _, landed = _exchange_wait(gather_in, gather_in_flight, travelled_behind, "gather_w_in_wait")
    (g_in,) = _gather_pass_on(landed, "gather_w_in_pass_on")
    wf_in = g_in.reshape(IN_W, D_MODEL)
    merge_names = ("w_glu", "w_attn_branch", "w_ssm_branch", "w_out")
    (q, k, vv, u, ga, gs, h), gathered = _in_proj_fwd(xs, g1, wf_in, t["proj"], _gather_carry([shard[n] for n in merge_names]))
    wf = {n: _full_from_gathered(n, g) for n, g in zip(merge_names, gathered)}
    (att,), (wf_ff_in,) = _attn_fwd(q, k, vv, bucket, rel_b, sink, _gather_carry([shard["w_ff_in"]]))
    (y, h_re, h_im, in_re, in_im), (wf_ff_out,) = _ssm_fwd(
        u, a_re, a_im, bm_re, bm_im, cm_re, cm_im, dsk, t["ssm_chunk"], _gather_carry([shard["w_ff_out"]]))
    x1, o, h2 = _merge_fwd(xs, y, att, ga, gs, g2, g3, wf["w_glu"], wf["w_ssm_branch"], wf["w_attn_branch"], wf["w_out"],
                           t["merge"])
    a, dfo, dx2, loss_blk, dg4 = _mlp_fwd(h2, x1, target, g4, wf_ff_in, wf_ff_out, t["mlp_fwd"])

    groups = {"ff": 4, "merge": 1, "w_in": 2}

    def add_sibling(group, blocks, received):
        return _add_sibling(blocks, received, jnp.concatenate([core, chip]), groups[group], "add_sibling_" + group)

    ff_names = ("w_ff_in", "w_ff_out")
    dw_ff_in, dw_ff_out, da = _mlp_weight_grads(dfo, a, h2, wf_ff_out, t["mlp_bwd"])
    dh2 = _mlp_input_grad(da, wf_ff_in.reshape(D_FF, D_MODEL), t["mlp_bwd"])
    ff_blocks = [dw_ff_in, dw_ff_out]
    (dx1, dgates, datt, dy, dw_glu, dw_ssm, dw_attn, dw_out, dg2, dg3), ff_recv = _merge_bwd(
        dh2, dx2, x1, o, y, att, ga, gs, g2, g3, wf["w_glu"], wf["w_ssm_branch"], wf["w_attn_branch"], wf["w_out"],
        t["merge_bwd"], _sibling_carry(ff_blocks))
    ff_sums, ff_sums_bf = add_sibling("ff", ff_blocks, ff_recv)
    merge_blocks = [_blocks_from_full(n, g) for n, g in zip(merge_names, (dw_glu, dw_attn, dw_ssm, dw_out))]
    (du, dbm_re, dbm_im, dcm_re, dcm_im, da_re, da_im, dd_skip), carried = _ssm_bwd(
        dy, u, h_re, h_im, in_re, in_im, a_re, a_im, bm_re, bm_im, cm_re, cm_im, dsk, t["ssm_chunk"],
        _join(_chips_carry(ff_sums_bf), _sibling_carry(merge_blocks)))
    ff_from_chips, merge_recv = carried[:2], carried[2:]
    merge_sums, merge_sums_bf = add_sibling("merge", merge_blocks, merge_recv)
    dbd_re, dbd_im, dlam_re, dlam_im, dldt_rep = _ssm_prep_bwd(lam_r, lam_i, ldt_rep, bd_re, bd_im, dbm_re, dbm_im, da_re, da_im)
    dlog_dt = _group_sum(dldt_rep.reshape(N_GROUPS, N_STATE))
    shapes = {n: view(n, w[n]).shape for n in SMALL}
    shapes["loss"] = (1,)
    small_grads = dict(
        norm_mix_post=dg2, norm_mlp_pre=dg3, norm_mlp_post=dg4, lam_re=dlam_re, lam_im=dlam_im, log_dt=dlog_dt,
        b_re=_block_diag_in_grad(dbd_re), b_im=_block_diag_in_grad(dbd_im),
        c_re=_block_diag_out_grad(dcm_re), c_im=_block_diag_out_grad(dcm_im), d_skip=dd_skip)
    packed_early = _pack({n: small_grads[n].reshape(shapes[n]) for n in SMALL_BEFORE_ATTN_BWD}, SMALL_BEFORE_ATTN_BWD)
    (dq, dkv, attn_small), carried = _attn_bwd(
        q, k, vv, datt, bucket, rel_b, sink, _join(_chips_carry(merge_sums_bf), _gather_carry([packed_early])))
    merge_from_chips, partials_early = carried[:-1], carried[-1]

    dparts = (dq, dkv, du, dgates)
    dw_in_t = _in_proj_weight_grad(h, dparts)
    in_blocks = [dw_in_t.reshape(N_DEV, IN_W // N_DEV, D_MODEL)]
    to_sibling = _sibling_carry(in_blocks)
    in_flight, token = _exchange_start(to_sibling, "w_in_sibling_start")
    n_tiles = xs.shape[0] // t["proj_bwd"]
    (grad_x, dg1), _ = _in_proj_input_grad(xs, g1 + token[0:1, 0:1], wf_in, dx1, dparts, t["proj_bwd"], 0, n_tiles, "in_proj_input_grad")
    late = dict(norm_mix_pre=dg1, rel_bias=attn_small[:, :N_BUCKETS, 0], sinks=attn_small[:, N_BUCKETS, 0], loss=loss_blk[0:1, 0])
    packed_late = _pack({n: late[n].reshape(shapes[n]) for n in SMALL_LATE}, SMALL_LATE)
    to_everyone = _everyone_carry([packed_late])
    in_blocks, in_recv = _exchange_wait(to_sibling, in_flight, [packed_late], "w_in_sibling_wait")
    in_sums, in_sums_bf = add_sibling("w_in", in_blocks, in_recv)
    to_chips = _chips_carry(in_sums_bf)
    started, chips_started = _exchange_start(_join(to_everyone, to_chips), "late_grads_and_w_in_chips_start")
    late_in_flight, in_flight = [[started[j] for j in js] for js in ((0, 1, 4, 6), (2, 3, 5, 7))]

    grads, deltas, new_m, new_v = {}, {}, {}, {}

    def adam_group(group, names, sums, received, after=()):
        outs = _adam_big(*[[local(d, n) for n in names] for d in (w, m, v)], sums, received, groups[group],
                         "adam_" + group, after)
        for store, vals in zip((grads, deltas, new_m, new_v), outs):
            store.update({n: (o.T if n == "w_in" else o)[None] for n, o in zip(names, vals)})

    adam_group("ff", ff_names, ff_sums, ff_from_chips, [chips_started])
    adam_group("merge", merge_names, merge_sums, merge_from_chips, [chips_started])

    grads.update(_unpack(_sum_partials(partials_early, "sum_small_grads", [chips_started]), shapes, SMALL_BEFORE_ATTN_BWD))
    (packed_late,), (late_received,) = _exchange_wait(
        to_everyone, late_in_flight, [new_v["w_ff_out"], new_v["w_out"]], "late_grads_wait")
    grads.update(_unpack(_sum_everyone(packed_late, late_received, 2 * chip + core, "sum_late_grads"), shapes, SMALL_LATE))
    loss = grads.pop("loss").reshape(())
    small_out = _adam_small(*[[view(n, d[n]) for n in SMALL] for d in (w, m, v)], [grads[n] for n in SMALL])
    for store, vals in zip((deltas, new_m, new_v), small_out):
        store.update(zip(SMALL, vals))
    for store in (grads, deltas, new_m, new_v):
        store.update({n: view(n, store[n]) for n in SWAPPED_SMALL})

    busy = [new_v["w_ff_out"], new_v["w_out"], deltas["norm_mix_pre"]]
    _, (in_from_chips,) = _exchange_wait(to_chips, in_flight, busy, "w_in_chips_wait")
    adam_group("w_in", ("w_in",), in_sums, [in_from_chips])

    return (loss, grad_x[None], *[grads[n] for n in ALL_WEIGHTS], *[deltas[n] for n in ALL_WEIGHTS],
            *[new_m[n] for n in ALL_WEIGHTS], *[new_v[n] for n in ALL_WEIGHTS])
```

```python
import math

import jax
import jax.numpy as jnp
import numpy as np
from jax import lax
from jax.experimental import pallas as pl
from jax.experimental.pallas import tpu as pltpu

F32 = jnp.float32
BF16 = jnp.bfloat16

D_MODEL = 1024
N_HEADS = 8
HEAD_DIM = 64
ATTN_W = 512
KV_W = 128
BLOCK = 128
N_BUCKETS = 32
SSM_W = 512
N_GROUPS = 32
N_STATE = 64
GROUP_CH = 16
STATES = N_GROUPS * N_STATE
D_FF = 4096
IN_W = 3328
SPLITS = (0, 512, 640, 768, 1280, 2304, 3328)
RMS_EPS = 1e-6
NEG_INF = -1e30
SUBLANES = 8
LANES = 128
SSM_LANE_BLOCK = 512
N_SSM_BLOCKS = STATES // SSM_LANE_BLOCK
GROUPS_PER_BLOCK = SSM_LANE_BLOCK // N_STATE
VMEM_BIG = 52 * 1024 * 1024
VMEM_MID = 40 * 1024 * 1024
VMEM_MAX = 60 * 1024 * 1024

ADAM_LR = 0.001
ADAM_B1 = 0.9
ADAM_B2 = 0.999
ADAM_EPS = 1e-08
ADAM_WD = 0.01
ADAM_STEP = 10

N_DEV = 8


def _dot(a, b):
    return jnp.dot(a, b, preferred_element_type=F32)


def _dot_nt(a, b):
    return lax.dot_general(a, b, (((1,), (1,)), ((), ())), preferred_element_type=F32)


def _dot_tn(a, b):
    return lax.dot_general(a, b, (((0,), (0,)), ((), ())), preferred_element_type=F32)


def _rms_scale(x):
    return lax.rsqrt(jnp.mean(x * x, axis=-1, keepdims=True) + RMS_EPS)


def _rms_bwd(dy, x, r, g):
    t = dy * g
    dx = r * t - x * (r * r * r) * jnp.mean(t * x, axis=-1, keepdims=True)
    dg = jnp.sum(dy * x * r, axis=0, keepdims=True)
    return dx, dg


def _const_spec(shape):
    nd = len(shape)
    return pl.BlockSpec(shape, lambda *_: (0,) * nd, pipeline_mode=pl.Buffered(1))


def _in_hbm(*arrays):
    return tuple(pltpu.with_memory_space_constraint(a, pltpu.HBM) for a in arrays)


def _hbm_out(shapes):
    if isinstance(shapes, (list, tuple)):
        return [_hbm_out(s) for s in shapes]
    return shapes if isinstance(shapes, pl.MemoryRef) else pltpu.HBM(shapes.shape, shapes.dtype)


def _whole(shape):
    nd = len(shape)
    return pl.BlockSpec(shape, lambda *_: (0,) * nd)


def _params(sem, vmem=None):
    return pltpu.CompilerParams(dimension_semantics=sem, vmem_limit_bytes=vmem)


MESH_IDS = pl.DeviceIdType.MESH
HBM_SPEC = pl.BlockSpec(memory_space=pl.ANY)


class _Carry:
    def __init__(self, inputs, out_shapes, sems, start, finish, middle=None):
        self.inputs, self.out_shapes, self.sems = list(inputs), list(out_shapes), list(sems)
        self.start, self.middle, self.finish = start, middle, finish


def _join(a, b):
    na_in, na_out, na_sem = len(a.inputs), len(a.out_shapes), len(a.sems)

    def both(phase):
        def run(ins, outs, sems):
            for carry, lo in ((a, True), (b, False)):
                part = (lambda seq, n: seq[:n] if lo else seq[n:])
                if getattr(carry, phase) is not None:
                    getattr(carry, phase)(part(ins, na_in), part(outs, na_out), part(sems, na_sem))
        return run

    middle = both("middle") if (a.middle or b.middle) else None
    return _Carry(a.inputs + b.inputs, a.out_shapes + b.out_shapes, a.sems + b.sems, both("start"), both("finish"), middle)


def _hosted_call(body, carry, edge, *, name, grid, in_specs, out_specs, out_shape, scratch_shapes, compiler_params, inputs):
    n_in, n_out = len(in_specs), len(out_specs)
    inputs = [a if s.memory_space == pltpu.SMEM else _in_hbm(a)[0] for a, s in zip(inputs, in_specs)]
    out_shape = _hbm_out(list(out_shape))
    if carry is None:
        outs = pl.pallas_call(body, name=name, grid=grid, in_specs=in_specs, out_specs=out_specs, out_shape=out_shape,
                              scratch_shapes=scratch_shapes, compiler_params=compiler_params)(*inputs)
        return list(outs), []
    c_in, c_out, c_sem = len(carry.inputs), len(carry.out_shapes), len(carry.sems)

    def wrapped(*refs):
        ins, refs = refs[:n_in], refs[n_in:]
        cins, refs = refs[:c_in], refs[c_in:]
        outs, refs = refs[:n_out], refs[n_out:]
        couts, refs = refs[:c_out], refs[c_out:]
        scratch, csems = refs[:len(refs) - c_sem], refs[len(refs) - c_sem:]
        first, middle, last = edge()

        @pl.when(first)
        def _():
            carry.start(cins, couts, csems)

        body(*ins, *outs, *scratch)

        if carry.middle is not None:
            @pl.when(middle)
            def _():
                carry.middle(cins, couts, csems)

        @pl.when(last)
        def _():
            carry.finish(cins, couts, csems)

    outs = pl.pallas_call(
        wrapped, name=name, grid=grid, in_specs=list(in_specs) + [HBM_SPEC] * c_in,
        out_specs=list(out_specs) + [HBM_SPEC] * c_out, out_shape=out_shape + _hbm_out(carry.out_shapes),
        scratch_shapes=list(scratch_shapes) + carry.sems, compiler_params=compiler_params)(*inputs, *_in_hbm(*carry.inputs))
    return list(outs[:n_out]), list(outs[n_out:])


def _pass_on_step(n_steps):
    return max(0, min((7 * n_steps) // 8, n_steps - 2))


def _edge_1d(n_steps, pass_on_last=False):
    middle = n_steps - 1 if pass_on_last else _pass_on_step(n_steps)
    return lambda: (pl.program_id(0) == 0, pl.program_id(0) == middle, pl.program_id(0) == n_steps - 1)


def _edge_2d(n0, n1):
    def edge():
        step = pl.program_id(0) * n1 + pl.program_id(1)
        return step == 0, step == _pass_on_step(n0 * n1), step == n0 * n1 - 1
    return edge


def _in_proj_fwd(x, g1, w_in_t, tile, carry=None):
    T = x.shape[0]

    def body(x_ref, g_ref, w_ref, q_ref, k_ref, v_ref, u_ref, ga_ref, gs_ref, h_ref):
        xv = x_ref[...]
        h = (xv * _rms_scale(xv) * g_ref[...]).astype(BF16)
        h_ref[...] = h
        outs = (q_ref, k_ref, v_ref, u_ref, ga_ref, gs_ref)
        for p, o_ref in enumerate(outs):
            o_ref[...] = _dot_nt(h, w_ref[SPLITS[p]:SPLITS[p + 1], :]).astype(o_ref.dtype)

    widths = [SPLITS[p + 1] - SPLITS[p] for p in range(6)] + [D_MODEL]
    dtypes = [BF16, BF16, BF16, F32, F32, F32, BF16]
    return _hosted_call(
        body, carry, _edge_1d(T // tile), name="in_proj_fwd", grid=(T // tile,),
        in_specs=[pl.BlockSpec((tile, D_MODEL), lambda i: (i, 0)), _const_spec((1, D_MODEL)), _const_spec((IN_W, D_MODEL))],
        out_specs=[pl.BlockSpec((tile, w), lambda i: (i, 0)) for w in widths],
        out_shape=[jax.ShapeDtypeStruct((T, w), dt) for w, dt in zip(widths, dtypes)],
        scratch_shapes=[], compiler_params=_params(("arbitrary",), VMEM_MID), inputs=(x, g1, w_in_t))


PROJ_PARTS = (512, 256, 512, 2048)
PROJ_GRAD_BLOCK = 256


def _in_proj_weight_grad(h, dparts):
    T = h.shape[0]
    blocks = [wd // PROJ_GRAD_BLOCK for wd in PROJ_PARTS]
    starts = [sum(blocks[:p]) for p in range(len(blocks))]

    def body(h_ref, *refs):
        part_refs, o_ref = refs[:-1], refs[-1]
        j = pl.program_id(0)
        for p_ref, start, count in zip(part_refs, starts, blocks):
            @pl.when((j >= start) & (j < start + count))
            def _(p_ref=p_ref):
                o_ref[...] = _dot_tn(p_ref[...], h_ref[...])

    def part_spec(start, count):
        return pl.BlockSpec((T, PROJ_GRAD_BLOCK), lambda j: (0, jnp.clip(j - start, 0, count - 1)))

    return pl.pallas_call(
        body, name="in_proj_weight_grad", grid=(sum(blocks),),
        in_specs=[_const_spec((T, D_MODEL))] + [part_spec(s, c) for s, c in zip(starts, blocks)],
        out_specs=pl.BlockSpec((PROJ_GRAD_BLOCK, D_MODEL), lambda j: (j, 0)),
        out_shape=_hbm_out(jax.ShapeDtypeStruct((IN_W, D_MODEL), F32)),
        compiler_params=_params(("arbitrary",), VMEM_MID),
    )(*_in_hbm(h, *dparts))


def _in_proj_input_grad(x, g1, w_in_t, dx1, dparts, tile, first_tile, n_tiles, name, carry=None):
    offsets = [sum(PROJ_PARTS[:p]) for p in range(len(PROJ_PARTS))]

    def body(x_ref, g_ref, w_ref, dx1_ref, *refs):
        part_refs, (gx_ref, dg_ref) = refs[:len(PROJ_PARTS)], refs[len(PROJ_PARTS):]
        i = pl.program_id(0)
        xv = x_ref[...]
        r = _rms_scale(xv)
        g = g_ref[...]
        dh = sum(_dot(p_ref[...], w_ref[off:off + wd, :]) for p_ref, off, wd in zip(part_refs, offsets, PROJ_PARTS))
        dxn, dg = _rms_bwd(dh, xv, r, g)
        gx_ref[...] = dx1_ref[...] + dxn

        @pl.when(i == 0)
        def _():
            dg_ref[...] = dg

        @pl.when(i > 0)
        def _():
            dg_ref[...] += dg

    tok = lambda wd: pl.BlockSpec((tile, wd), lambda i: (i + first_tile, 0))
    return _hosted_call(
        body, carry, _edge_1d(n_tiles), name=name, grid=(n_tiles,),
        in_specs=[tok(D_MODEL), _const_spec((1, D_MODEL)), _const_spec((IN_W, D_MODEL)), tok(D_MODEL)] + [tok(wd) for wd in PROJ_PARTS],
        out_specs=[pl.BlockSpec((tile, D_MODEL), lambda i: (i, 0)), pl.BlockSpec((1, D_MODEL), lambda i: (0, 0))],
        out_shape=[jax.ShapeDtypeStruct((n_tiles * tile, D_MODEL), F32), jax.ShapeDtypeStruct((1, D_MODEL), F32)],
        scratch_shapes=[], compiler_params=_params(("arbitrary",), VMEM_MID), inputs=(x, g1, w_in_t, dx1, *dparts))


def _bucket_table():
    qi = np.arange(BLOCK)[:, None]
    kj = np.arange(2 * BLOCK)[None, :]
    dist = qi + BLOCK - kj
    max_exact = N_BUCKETS // 2
    d = np.maximum(dist, 0)
    df = np.maximum(d, 1).astype(np.float32)
    large = max_exact + (np.log(df / np.float32(max_exact)) / np.float32(math.log(BLOCK / max_exact))
                         * np.float32(N_BUCKETS - max_exact)).astype(np.int32)
    large = np.minimum(large, N_BUCKETS - 1)
    bucket = np.where(d < max_exact, d, large)
    return np.where((dist >= 0) & (dist < BLOCK), bucket, -1).astype(np.int32)


def _build_bias(bucket_ref, rb_ref, bias_ref):
    bk = bucket_ref[...]
    for h in range(N_HEADS):
        def add(b, acc, h=h):
            return acc + jnp.where(bk == b, rb_ref[h, b], 0.0)
        bias_ref[h] = lax.fori_loop(0, N_BUCKETS, add, jnp.zeros((BLOCK, 2 * BLOCK), F32))


def _kv_variants(prev_ref, cur_ref):
    cat = jnp.concatenate([prev_ref[...], cur_ref[...]], axis=0)
    lo = lax.broadcasted_iota(jnp.int32, cat.shape, 1) < HEAD_DIM
    zero = jnp.zeros_like(cat)
    head0_lo = jnp.where(lo, cat, zero)
    head1_hi = jnp.where(lo, zero, cat)
    return ((head0_lo, pltpu.roll(head0_lo, HEAD_DIM, 1)), (pltpu.roll(head1_hi, HEAD_DIM, 1), head1_hi))


def _merge_kv_grads(g):
    lo = lax.broadcasted_iota(jnp.int32, g[0][0].shape, 1) < HEAD_DIM
    return jnp.where(lo, g[0][0] + pltpu.roll(g[0][1], HEAD_DIM, 1), g[1][1] + pltpu.roll(g[1][0], HEAD_DIM, 1))


def _head_lanes(h):
    return slice((h // 2) * LANES, (h // 2 + 1) * LANES)


def _attn_probs(q_ref, kvar, bias_ref, sk_ref, valid, s_ref):
    for h in range(N_HEADS):
        s_ref[h] = _dot_nt(q_ref[:, _head_lanes(h)], kvar[h // 4][h % 2])
    head = lax.broadcasted_iota(jnp.int32, (N_HEADS, 1, 1), 0)
    sink = jnp.zeros((N_HEADS, 1, 1), F32)
    for h in range(N_HEADS):
        sink = jnp.where(head == h, sk_ref[0, h], sink)
    s = jnp.where(valid[None], s_ref[...] * (HEAD_DIM ** -0.5) + bias_ref[...], NEG_INF)
    m = jnp.maximum(jnp.max(s, axis=-1, keepdims=True), sink)
    p = jnp.exp(s - m)
    e_sink = jnp.exp(sink - m)
    inv = 1.0 / (jnp.sum(p, axis=-1, keepdims=True) + e_sink)
    return p * inv, e_sink * inv


def _attn_valid(bucket_ref, n):
    col = lax.broadcasted_iota(jnp.int32, (BLOCK, 2 * BLOCK), 1)
    return (bucket_ref[...] >= 0) & ((n > 0) | (col >= BLOCK))


def _attn_fwd(q, k, v, bucket, rel_bias, sinks, carry=None):
    T = q.shape[0]
    nb = T // BLOCK

    def body(q_ref, kc_ref, kp_ref, vc_ref, vp_ref, bucket_ref, rb_ref, sk_ref, o_ref, bias_ref, s_ref, p_ref):
        n = pl.program_id(0)

        @pl.when(n == 0)
        def _():
            _build_bias(bucket_ref, rb_ref, bias_ref)

        kvar = _kv_variants(kp_ref, kc_ref)
        vvar = _kv_variants(vp_ref, vc_ref)
        pr, _ = _attn_probs(q_ref, kvar, bias_ref, sk_ref, _attn_valid(bucket_ref, n), s_ref)
        p_ref[...] = pr.astype(BF16)
        for m in range(N_HEADS // 2):
            acc = _dot(p_ref[2 * m], vvar[m // 2][0]) + _dot(p_ref[2 * m + 1], vvar[m // 2][1])
            o_ref[:, m * LANES:(m + 1) * LANES] = acc.astype(o_ref.dtype)

    cur = lambda w: pl.BlockSpec((BLOCK, w), lambda n: (n, 0))
    prev = lambda w: pl.BlockSpec((BLOCK, w), lambda n: (jnp.maximum(n - 1, 0), 0))
    smem = pl.BlockSpec(memory_space=pltpu.SMEM)
    return _hosted_call(
        body, carry, _edge_1d(nb, pass_on_last=True), name="attn_fwd", grid=(nb,),
        in_specs=[cur(ATTN_W), cur(KV_W), prev(KV_W), cur(KV_W), prev(KV_W), _const_spec((BLOCK, 2 * BLOCK)), smem, smem],
        out_specs=[cur(ATTN_W)],
        out_shape=[jax.ShapeDtypeStruct((T, ATTN_W), BF16)],
        scratch_shapes=[pltpu.VMEM((N_HEADS, BLOCK, 2 * BLOCK), F32), pltpu.VMEM((N_HEADS, BLOCK, 2 * BLOCK), F32),
                        pltpu.VMEM((N_HEADS, BLOCK, 2 * BLOCK), BF16)],
        compiler_params=_params(("arbitrary",)), inputs=(q, k, k, v, v, bucket, rel_bias, sinks))


ATTN_SMALL_ROWS = N_BUCKETS + SUBLANES


def _attn_bwd(q, k, v, datt, bucket, rel_bias, sinks, carry=None):
    T = q.shape[0]
    nb = T // BLOCK

    def body(q_ref, do_ref, kc_ref, kp_ref, vc_ref, vp_ref, bucket_ref, rb_ref, sk_ref,
             dq_ref, dkv_ref, small_ref, bias_ref, ds_sum_ref, dsink_ref, kcarry_ref, vcarry_ref,
             s_ref, dp_ref, p_ref, dsc_ref):
        n = pl.program_id(0)

        @pl.when(n == 0)
        def _():
            _build_bias(bucket_ref, rb_ref, bias_ref)
            ds_sum_ref[...] = jnp.zeros_like(ds_sum_ref)
            dsink_ref[...] = jnp.zeros_like(dsink_ref)
            kcarry_ref[...] = jnp.zeros_like(kcarry_ref)
            vcarry_ref[...] = jnp.zeros_like(vcarry_ref)

        @pl.when(n < nb)
        def _():
            kvar = _kv_variants(kp_ref, kc_ref)
            vvar = _kv_variants(vp_ref, vc_ref)
            pr, p_sink = _attn_probs(q_ref, kvar, bias_ref, sk_ref, _attn_valid(bucket_ref, n), s_ref)
            for h in range(N_HEADS):
                dp_ref[h] = _dot_nt(do_ref[:, _head_lanes(h)], vvar[h // 4][h % 2])
            dp = dp_ref[...]
            dsum = jnp.sum(pr * dp, axis=-1, keepdims=True)
            ds = pr * (dp - dsum)
            ds_sum_ref[...] += ds
            dsink_ref[...] -= jnp.sum(p_sink * dsum, axis=1, keepdims=True)
            dsc_ref[...] = (ds * (HEAD_DIM ** -0.5)).astype(BF16)
            p_ref[...] = pr.astype(BF16)
            for m in range(N_HEADS // 2):
                dqm = _dot(dsc_ref[2 * m], kvar[m // 2][0]) + _dot(dsc_ref[2 * m + 1], kvar[m // 2][1])
                dq_ref[:, m * LANES:(m + 1) * LANES] = dqm.astype(dq_ref.dtype)
            dk_var = [[None, None], [None, None]]
            dv_var = [[None, None], [None, None]]
            for kvh in range(2):
                for e in range(2):
                    heads = [h for h in range(N_HEADS) if h // 4 == kvh and h % 2 == e]
                    dk_var[kvh][e] = sum(_dot_tn(dsc_ref[h], q_ref[:, _head_lanes(h)]) for h in heads)
                    dv_var[kvh][e] = sum(_dot_tn(p_ref[h], do_ref[:, _head_lanes(h)]) for h in heads)
            dk_cat = _merge_kv_grads(dk_var)
            dv_cat = _merge_kv_grads(dv_var)

            @pl.when(n > 0)
            def _():
                dkv_ref[:, :KV_W] = (kcarry_ref[...] + dk_cat[:BLOCK]).astype(BF16)
                dkv_ref[:, KV_W:] = (vcarry_ref[...] + dv_cat[:BLOCK]).astype(BF16)

            kcarry_ref[...] = dk_cat[BLOCK:]
            vcarry_ref[...] = dv_cat[BLOCK:]

        @pl.when(n == nb)
        def _():
            dkv_ref[:, :KV_W] = kcarry_ref[...].astype(BF16)
            dkv_ref[:, KV_W:] = vcarry_ref[...].astype(BF16)
            bk = bucket_ref[...]
            row = lax.broadcasted_iota(jnp.int32, (N_HEADS, ATTN_SMALL_ROWS, LANES), 1)

            def add(b, acc):
                masked = jnp.where((bk == b)[None], ds_sum_ref[...], 0.0)
                val = jnp.sum(jnp.sum(masked, axis=1, keepdims=True), axis=2, keepdims=True)
                return acc + jnp.where(row == b, val, 0.0)

            small_ref[...] = lax.fori_loop(0, N_BUCKETS, add, jnp.where(row == N_BUCKETS, dsink_ref[...], 0.0))

    last = nb - 1
    cur = lambda w: pl.BlockSpec((BLOCK, w), lambda n: (jnp.minimum(n, last), 0))
    prev = lambda w: pl.BlockSpec((BLOCK, w), lambda n: (jnp.clip(n - 1, 0, last), 0))
    smem = pl.BlockSpec(memory_space=pltpu.SMEM)
    return _hosted_call(
        body, carry, _edge_1d(nb + 1), name="attn_bwd", grid=(nb + 1,),
        in_specs=[cur(ATTN_W), cur(ATTN_W), cur(KV_W), prev(KV_W), cur(KV_W), prev(KV_W),
                  _const_spec((BLOCK, 2 * BLOCK)), smem, smem],
        out_specs=[cur(ATTN_W), prev(2 * KV_W), pl.BlockSpec((N_HEADS, ATTN_SMALL_ROWS, LANES), lambda n: (0, 0, 0))],
        out_shape=[jax.ShapeDtypeStruct((T, ATTN_W), BF16), jax.ShapeDtypeStruct((T, 2 * KV_W), BF16),
                   jax.ShapeDtypeStruct((N_HEADS, ATTN_SMALL_ROWS, LANES), F32)],
        scratch_shapes=[pltpu.VMEM((N_HEADS, BLOCK, 2 * BLOCK), F32), pltpu.VMEM((N_HEADS, BLOCK, 2 * BLOCK), F32),
                        pltpu.VMEM((N_HEADS, 1, 1), F32), pltpu.VMEM((BLOCK, KV_W), F32), pltpu.VMEM((BLOCK, KV_W), F32),
                        pltpu.VMEM((N_HEADS, BLOCK, 2 * BLOCK), F32), pltpu.VMEM((N_HEADS, BLOCK, 2 * BLOCK), F32),
                        pltpu.VMEM((N_HEADS, BLOCK, 2 * BLOCK), BF16), pltpu.VMEM((N_HEADS, BLOCK, 2 * BLOCK), BF16)],
        compiler_params=_params(("arbitrary",)), inputs=(q, datt, k, k, v, v, bucket, rel_bias, sinks))


SCAN_UNROLL = 4


def _cmul(ar, ai, br, bi):
    return ar * br - ai * bi, ar * bi + ai * br


def _cmul_conj(ar, ai, br, bi):
    return ar * br + ai * bi, ar * bi - ai * br


def _ssm_discretize(lr, li, ldt):
    dt = jnp.exp(ldt)
    mag = jnp.exp(lr * dt)
    ab_re = mag * jnp.cos(li * dt)
    ab_im = mag * jnp.sin(li * dt)
    nr = ab_re - 1.0
    den = lr * lr + li * li
    f_re = (nr * lr + ab_im * li) / den
    f_im = (ab_im * lr - nr * li) / den
    return ab_re, ab_im, f_re, f_im


def _ssm_prep(lam_re, lam_im, ldt_rep, bd_re, bd_im):
    def body(lr_ref, li_ref, ldt_ref, bdr_ref, bdi_ref, ar_ref, ai_ref, br_ref, bi_ref):
        ab_re, ab_im, f_re, f_im = _ssm_discretize(lr_ref[...], li_ref[...], ldt_ref[...])
        ar_ref[...] = ab_re
        ai_ref[...] = ab_im
        bdr, bdi = bdr_ref[0], bdi_ref[0]
        br_ref[0] = (bdr * f_re - bdi * f_im).astype(BF16)
        bi_ref[0] = (bdi * f_re + bdr * f_im).astype(BF16)

    row = pl.BlockSpec((1, SSM_LANE_BLOCK), lambda j: (0, j))
    mat = pl.BlockSpec((1, LANES, SSM_LANE_BLOCK), lambda j: (j, 0, 0))
    return pl.pallas_call(
        body, name="ssm_prep", grid=(N_SSM_BLOCKS,),
        in_specs=[row, row, row, mat, mat], out_specs=[row, row, mat, mat],
        out_shape=[jax.ShapeDtypeStruct((1, STATES), F32)] * 2 + [jax.ShapeDtypeStruct((N_SSM_BLOCKS, LANES, SSM_LANE_BLOCK), BF16)] * 2,
        compiler_params=_params(("arbitrary",)),
    )(*_in_hbm(lam_re, lam_im, ldt_rep, bd_re, bd_im))


def _ssm_prep_bwd(lam_re, lam_im, ldt_rep, bd_re, bd_im, dbr, dbi, da_re, da_im):
    def body(lr_ref, li_ref, ldt_ref, bdr_ref, bdi_ref, dbr_ref, dbi_ref, dar_ref, dai_ref,
             dbdr_ref, dbdi_ref, dlr_ref, dli_ref, dldt_ref):
        lr, li, ldt = lr_ref[...], li_ref[...], ldt_ref[...]
        (_, _, f_re, f_im), vjp = jax.vjp(_ssm_discretize, lr, li, ldt)
        bdr, bdi, gbr, gbi = bdr_ref[0], bdi_ref[0], dbr_ref[0], dbi_ref[0]
        dbdr_ref[0] = gbr * f_re + gbi * f_im
        dbdi_ref[0] = gbi * f_re - gbr * f_im
        df_re = jnp.sum(gbr * bdr + gbi * bdi, axis=0, keepdims=True)
        df_im = jnp.sum(gbi * bdr - gbr * bdi, axis=0, keepdims=True)
        dlr, dli, dldt = vjp((dar_ref[...], dai_ref[...], df_re, df_im))
        dlr_ref[...] = dlr
        dli_ref[...] = dli
        dldt_ref[...] = dldt

    row = pl.BlockSpec((1, SSM_LANE_BLOCK), lambda j: (0, j))
    mat = pl.BlockSpec((1, LANES, SSM_LANE_BLOCK), lambda j: (j, 0, 0))
    mat_shape = jax.ShapeDtypeStruct((N_SSM_BLOCKS, LANES, SSM_LANE_BLOCK), F32)
    row_shape = jax.ShapeDtypeStruct((1, STATES), F32)
    return pl.pallas_call(
        body, name="ssm_prep_bwd", grid=(N_SSM_BLOCKS,),
        in_specs=[row, row, row, mat, mat, mat, mat, row, row], out_specs=[mat, mat, row, row, row],
        out_shape=[mat_shape, mat_shape, row_shape, row_shape, row_shape],
        compiler_params=_params(("arbitrary",)),
    )(*_in_hbm(lam_re, lam_im, ldt_rep, bd_re, bd_im, dbr, dbi, da_re, da_im))


def _group_sum(x):
    def body(x_ref, o_ref):
        o_ref[...] = jnp.sum(x_ref[...], axis=1, keepdims=True)
    return pl.pallas_call(body, name="ssm_group_sum", grid=(1,), in_specs=[_whole(x.shape)], out_specs=_whole((N_GROUPS, 1)),
                          out_shape=jax.ShapeDtypeStruct((N_GROUPS, 1), F32))(*_in_hbm(x))


def _power_table(ar, ai, p_re_ref, p_im_ref, steps):
    shape = (SUBLANES, SSM_LANE_BLOCK)
    p_re_ref[0:SUBLANES] = jnp.broadcast_to(ar, shape)
    p_im_ref[0:SUBLANES] = jnp.broadcast_to(ai, shape)
    m = 1
    while m < steps:
        rows = m * SUBLANES
        top_re = p_re_ref[rows - SUBLANES:rows]
        top_im = p_im_ref[rows - SUBLANES:rows]
        cur_re = p_re_ref[0:rows].reshape(m, SUBLANES, SSM_LANE_BLOCK)
        cur_im = p_im_ref[0:rows].reshape(m, SUBLANES, SSM_LANE_BLOCK)
        nxt_re, nxt_im = _cmul(cur_re, cur_im, top_re[None], top_im[None])
        p_re_ref[rows:2 * rows] = nxt_re.reshape(rows, SSM_LANE_BLOCK)
        p_im_ref[rows:2 * rows] = nxt_im.reshape(rows, SSM_LANE_BLOCK)
        m *= 2


def _to_segments(src_ref, dst_ref, steps):
    for s in range(SUBLANES):
        dst_ref[pl.ds(s, steps, stride=SUBLANES), :] = src_ref[s * steps:(s + 1) * steps, :]


def _from_segments(src_ref, dst_ref, steps):
    for s in range(SUBLANES):
        dst_ref[s * steps:(s + 1) * steps, :] = src_ref[pl.ds(s, steps, stride=SUBLANES), :]


def _segment_carries(e_re, e_im, an_re, an_im, c_re, c_im, reverse):
    order = range(SUBLANES - 1, -1, -1) if reverse else range(SUBLANES)
    ins_re, ins_im = [None] * SUBLANES, [None] * SUBLANES
    for s in order:
        ins_re[s], ins_im[s] = c_re, c_im
        pr, pi = _cmul(an_re, an_im, c_re, c_im)
        c_re = e_re[s:s + 1] + pr
        c_im = e_im[s:s + 1] + pi
    return jnp.concatenate(ins_re, axis=0), jnp.concatenate(ins_im, axis=0), c_re, c_im


def _ssm_fwd(u, a_re, a_im, b_re, b_im, c_re, c_im, d_skip, chunk, carry=None):
    T = u.shape[0]
    nc = T // chunk
    steps = chunk // SUBLANES
    blk = SSM_LANE_BLOCK

    def body(u_ref, ar_ref, ai_ref, br_ref, bi_ref, cr_ref, ci_ref, dk_ref,
             y_ref, hr_ref, hi_ref, inr_ref, ini_ref, useg_ref, yseg_ref, pr_ref, pi_ref, carry_ref):
        c = pl.program_id(1)
        ar, ai = ar_ref[...], ai_ref[...]

        @pl.when(c == 0)
        def _():
            _power_table(ar, ai, pr_ref, pi_ref, steps)
            carry_ref[...] = jnp.zeros_like(carry_ref)

        _to_segments(u_ref, useg_ref, steps)
        ub = useg_ref[...].astype(BF16)
        hr_ref[...] = _dot(ub, br_ref[0])
        hi_ref[...] = _dot(ub, bi_ref[0])
        first = slice(0, SUBLANES)

        def scan(t4, prev):
            for j in range(SCAN_UNROLL):
                rows = pl.ds(pl.multiple_of((t4 * SCAN_UNROLL + j) * SUBLANES, SUBLANES), SUBLANES)
                pr, pi = _cmul(pr_ref[first, :], pi_ref[first, :], prev[0], prev[1])
                prev = (pr + hr_ref[rows, :], pi + hi_ref[rows, :])
                hr_ref[rows, :] = prev[0]
                hi_ref[rows, :] = prev[1]
            return prev

        zero = jnp.zeros((SUBLANES, blk), F32)
        lax.fori_loop(0, steps // SCAN_UNROLL, scan, (zero, zero))

        top = slice(chunk - SUBLANES, chunk)
        in_re, in_im, out_re, out_im = _segment_carries(
            hr_ref[top, :], hi_ref[top, :], pr_ref[top, :][0:1], pi_ref[top, :][0:1],
            carry_ref[0:1, :], carry_ref[1:2, :], reverse=False)
        carry_ref[0:1, :] = out_re
        carry_ref[1:2, :] = out_im
        inr_ref[...] = in_re
        ini_ref[...] = in_im

        def fix(t4, _):
            for j in range(SCAN_UNROLL):
                rows = pl.ds(pl.multiple_of((t4 * SCAN_UNROLL + j) * SUBLANES, SUBLANES), SUBLANES)
                fr, fi = _cmul(pr_ref[rows, :], pi_ref[rows, :], in_re, in_im)
                hr_ref[rows, :] += fr
                hi_ref[rows, :] += fi
            return 0

        lax.fori_loop(0, steps // SCAN_UNROLL, fix, 0)

        yseg_ref[...] = _dot(hr_ref[...].astype(BF16), cr_ref[0]) - _dot(hi_ref[...].astype(BF16), ci_ref[0])
        _from_segments(yseg_ref, y_ref, steps)
        y_ref[...] += dk_ref[...] * u_ref[...]

    row = pl.BlockSpec((1, blk), lambda j, c: (0, j))
    b_mat = pl.BlockSpec((1, LANES, blk), lambda j, c: (j, 0, 0))
    c_mat = pl.BlockSpec((1, blk, LANES), lambda j, c: (j, 0, 0))
    tok = pl.BlockSpec((chunk, LANES), lambda j, c: (c, j))
    state = pl.BlockSpec((chunk, blk), lambda j, c: (c, j))
    enter = pl.BlockSpec((SUBLANES, blk), lambda j, c: (c, j))
    return _hosted_call(
        body, carry, _edge_2d(N_SSM_BLOCKS, nc), name="ssm_fwd", grid=(N_SSM_BLOCKS, nc),
        in_specs=[tok, row, row, b_mat, b_mat, c_mat, c_mat, pl.BlockSpec((1, LANES), lambda j, c: (0, j))],
        out_specs=[tok, state, state, enter, enter],
        out_shape=[jax.ShapeDtypeStruct((T, SSM_W), F32), jax.ShapeDtypeStruct((T, STATES), F32),
                   jax.ShapeDtypeStruct((T, STATES), F32), jax.ShapeDtypeStruct((nc * SUBLANES, STATES), F32),
                   jax.ShapeDtypeStruct((nc * SUBLANES, STATES), F32)],
        scratch_shapes=[pltpu.VMEM((chunk, LANES), F32), pltpu.VMEM((chunk, LANES), F32),
                        pltpu.VMEM((chunk, blk), F32), pltpu.VMEM((chunk, blk), F32), pltpu.VMEM((SUBLANES, blk), F32)],
        compiler_params=_params(("arbitrary", "arbitrary"), VMEM_MID),
        inputs=(u, a_re, a_im, b_re, b_im, c_re, c_im, d_skip))


def _ssm_bwd(dy, u, h_re, h_im, in_re, in_im, a_re, a_im, b_re, b_im, c_re, c_im, d_skip, chunk, carry=None):
    T = u.shape[0]
    nc = T // chunk
    steps = chunk // SUBLANES
    blk = SSM_LANE_BLOCK

    def body(dy_ref, u_ref, hr_ref, hi_ref, inr_ref, ini_ref, ar_ref, ai_ref, br_ref, bi_ref, cr_ref, ci_ref, dk_ref,
             du_ref, dbr_ref, dbi_ref, dcr_ref, dci_ref, dar_ref, dai_ref, ddk_ref,
             dyseg_ref, useg_ref, duseg_ref, gr_ref, gi_ref, pr_ref, pi_ref, carry_ref, accr_ref, acci_ref):
        c = pl.program_id(1)
        ar, ai = ar_ref[...], ai_ref[...]

        @pl.when(c == 0)
        def _():
            _power_table(ar, ai, pr_ref, pi_ref, steps)
            carry_ref[...] = jnp.zeros_like(carry_ref)
            accr_ref[...] = jnp.zeros_like(accr_ref)
            acci_ref[...] = jnp.zeros_like(acci_ref)

        _to_segments(dy_ref, dyseg_ref, steps)
        _to_segments(u_ref, useg_ref, steps)
        dyb = dyseg_ref[...].astype(BF16)
        ub = useg_ref[...].astype(BF16)
        gr_ref[...] = _dot_nt(dyb, cr_ref[0])
        gi_ref[...] = -_dot_nt(dyb, ci_ref[0])
        dcr = _dot_tn(hr_ref[...].astype(BF16), dyb)
        dci = -_dot_tn(hi_ref[...].astype(BF16), dyb)
        ddk = jnp.sum(dy_ref[...] * u_ref[...], axis=0, keepdims=True)

        first = slice(0, SUBLANES)

        def scan(k4, nxt):
            for j in range(SCAN_UNROLL):
                t = steps - 1 - (k4 * SCAN_UNROLL + j)
                rows = pl.ds(pl.multiple_of(t * SUBLANES, SUBLANES), SUBLANES)
                pr, pi = _cmul_conj(pr_ref[first, :], pi_ref[first, :], nxt[0], nxt[1])
                nxt = (pr + gr_ref[rows, :], pi + gi_ref[rows, :])
                gr_ref[rows, :] = nxt[0]
                gi_ref[rows, :] = nxt[1]
            return nxt

        top = slice(chunk - SUBLANES, chunk)
        zero = jnp.zeros((SUBLANES, blk), F32)
        lax.fori_loop(0, steps // SCAN_UNROLL, scan, (zero, zero))

        gin_re, gin_im, out_re, out_im = _segment_carries(
            gr_ref[0:SUBLANES, :], gi_ref[0:SUBLANES, :], pr_ref[top, :][0:1], -pi_ref[top, :][0:1],
            carry_ref[0:1, :], carry_ref[1:2, :], reverse=True)
        carry_ref[0:1, :] = out_re
        carry_ref[1:2, :] = out_im

        def fix_row(rows, prow, hp_re, hp_im, acc):
            fr, fi = _cmul_conj(pr_ref[prow, :], pi_ref[prow, :], gin_re, gin_im)
            g_re = gr_ref[rows, :] + fr
            g_im = gi_ref[rows, :] + fi
            gr_ref[rows, :] = g_re
            gi_ref[rows, :] = g_im
            return acc[0] + g_re * hp_re + g_im * hp_im, acc[1] + g_im * hp_re - g_re * hp_im

        def fix_at(t, acc):
            aligned = (lambda r: r * SUBLANES) if isinstance(t, int) else (lambda r: pl.multiple_of(r * SUBLANES, SUBLANES))
            rows, before, prow = (pl.ds(aligned(r), SUBLANES) for r in (t, t - 1, steps - 1 - t))
            return fix_row(rows, prow, hr_ref[before, :], hi_ref[before, :], acc)

        def fix(t4, acc):
            for j in range(SCAN_UNROLL):
                acc = fix_at(t4 * SCAN_UNROLL + j, acc)
            return acc

        acc = fix_row(first, top, inr_ref[...], ini_ref[...], (accr_ref[...], acci_ref[...]))
        for t in range(1, SCAN_UNROLL):
            acc = fix_at(t, acc)
        acc_re, acc_im = lax.fori_loop(1, steps // SCAN_UNROLL, fix, acc)
        accr_ref[...] = acc_re
        acci_ref[...] = acc_im

        gbr = gr_ref[...].astype(BF16)
        gbi = gi_ref[...].astype(BF16)
        duseg_ref[...] = _dot_nt(gbr, br_ref[0]) + _dot_nt(gbi, bi_ref[0])
        _from_segments(duseg_ref, dyseg_ref, steps)
        du_ref[...] = (dyseg_ref[...] + dk_ref[...] * dy_ref[...]).astype(BF16)
        dbr = _dot_tn(ub, gbr)
        dbi = _dot_tn(ub, gbi)

        @pl.when(c == 0)
        def _():
            dbr_ref[0] = dbr
            dbi_ref[0] = dbi
            dcr_ref[0] = dcr
            dci_ref[0] = dci
            ddk_ref[...] = ddk

        @pl.when(c > 0)
        def _():
            dbr_ref[0] += dbr
            dbi_ref[0] += dbi
            dcr_ref[0] += dcr
            dci_ref[0] += dci
            ddk_ref[...] += ddk

        @pl.when(c == nc - 1)
        def _():
            dar_ref[...] = jnp.sum(acc_re, axis=0, keepdims=True)
            dai_ref[...] = jnp.sum(acc_im, axis=0, keepdims=True)

    rev = lambda c: nc - 1 - c
    row = pl.BlockSpec((1, blk), lambda j, c: (0, j))
    b_mat = pl.BlockSpec((1, LANES, blk), lambda j, c: (j, 0, 0))
    c_mat = pl.BlockSpec((1, blk, LANES), lambda j, c: (j, 0, 0))
    tok = pl.BlockSpec((chunk, LANES), lambda j, c: (rev(c), j))
    state = pl.BlockSpec((chunk, blk), lambda j, c: (rev(c), j))
    enter = pl.BlockSpec((SUBLANES, blk), lambda j, c: (rev(c), j))
    chan = pl.BlockSpec((1, LANES), lambda j, c: (0, j))
    f32 = lambda *s: jax.ShapeDtypeStruct(s, F32)
    return _hosted_call(
        body, carry, _edge_2d(N_SSM_BLOCKS, nc), name="ssm_bwd", grid=(N_SSM_BLOCKS, nc),
        in_specs=[tok, tok, state, state, enter, enter, row, row, b_mat, b_mat, c_mat, c_mat, chan],
        out_specs=[tok, b_mat, b_mat, c_mat, c_mat, row, row, chan],
        out_shape=[jax.ShapeDtypeStruct((T, SSM_W), BF16), f32(N_SSM_BLOCKS, LANES, blk), f32(N_SSM_BLOCKS, LANES, blk),
                   f32(N_SSM_BLOCKS, blk, LANES), f32(N_SSM_BLOCKS, blk, LANES), f32(1, STATES), f32(1, STATES), f32(1, SSM_W)],
        scratch_shapes=[pltpu.VMEM((chunk, LANES), F32), pltpu.VMEM((chunk, LANES), F32), pltpu.VMEM((chunk, LANES), F32),
                        pltpu.VMEM((chunk, blk), F32), pltpu.VMEM((chunk, blk), F32),
                        pltpu.VMEM((chunk, blk), F32), pltpu.VMEM((chunk, blk), F32),
                        pltpu.VMEM((SUBLANES, blk), F32), pltpu.VMEM((SUBLANES, blk), F32), pltpu.VMEM((SUBLANES, blk), F32)],
        compiler_params=_params(("arbitrary", "arbitrary"), VMEM_BIG),
        inputs=(dy, u, h_re, h_im, in_re, in_im, a_re, a_im, b_re, b_im, c_re, c_im, d_skip))


def _merge_forward(y, att, ga, gs, w_glu, w_ssm, w_attn):
    z = jax.nn.gelu(y)
    zb = z.astype(BF16)
    gl = jax.nn.sigmoid(_dot(zb, w_glu))
    z2b = (z * gl).astype(BF16)
    y_ssm = _dot(z2b, w_ssm)
    y_attn = _dot(att, w_attn)
    sa = jax.nn.sigmoid(ga)
    ss = jax.nn.sigmoid(gs)
    merged = (sa * y_attn + ss * y_ssm).astype(BF16)
    return z, zb, gl, z2b, y_ssm, y_attn, sa, ss, merged


def _merge_fwd(x, y, att, ga, gs, g2, g3, w_glu, w_ssm, w_attn, w_out, tile):
    T = x.shape[0]

    def body(x_ref, y_ref, att_ref, ga_ref, gs_ref, g2_ref, g3_ref, wg_ref, ws_ref, wa_ref, wo_ref, x1_ref, o_ref, h2_ref):
        merged = _merge_forward(y_ref[...], att_ref[...], ga_ref[...], gs_ref[...], wg_ref[...], ws_ref[...], wa_ref[...])[-1]
        o = _dot(merged, wo_ref[...])
        x1 = x_ref[...] + o * _rms_scale(o) * g2_ref[...]
        o_ref[...] = o
        x1_ref[...] = x1
        h2_ref[...] = (x1 * _rms_scale(x1) * g3_ref[...]).astype(BF16)

    tok = lambda w: pl.BlockSpec((tile, w), lambda i: (i, 0))
    vec = _const_spec((1, D_MODEL))
    return pl.pallas_call(
        body, name="merge_fwd", grid=(T // tile,),
        in_specs=[tok(D_MODEL), tok(SSM_W), tok(ATTN_W), tok(D_MODEL), tok(D_MODEL), vec, vec,
                  _const_spec((SSM_W, SSM_W)), _const_spec((SSM_W, D_MODEL)), _const_spec((ATTN_W, D_MODEL)),
                  _const_spec((D_MODEL, D_MODEL))],
        out_specs=[tok(D_MODEL), tok(D_MODEL), tok(D_MODEL)],
        out_shape=_hbm_out([jax.ShapeDtypeStruct((T, D_MODEL), F32), jax.ShapeDtypeStruct((T, D_MODEL), F32),
                            jax.ShapeDtypeStruct((T, D_MODEL), BF16)]),
        compiler_params=_params(("arbitrary",), VMEM_MID),
    )(*_in_hbm(x, y, att, ga, gs, g2, g3, w_glu, w_ssm, w_attn, w_out))


def _merge_bwd(dh2, dx2, x1, o, y, att, ga, gs, g2, g3, w_glu, w_ssm, w_attn, w_out, tile, carry=None):
    T = x1.shape[0]
    n_steps = T // tile

    group = min(2, n_steps)
    staged_widths = (D_MODEL, D_MODEL, ATTN_W, D_MODEL, SSM_W, D_MODEL, SSM_W, SSM_W)

    def body(dh2_ref, dx2_ref, x1_ref, o_ref, y_ref, att_ref, ga_ref, gs_ref, g2_ref, g3_ref, wg_ref, ws_ref, wa_ref, wo_ref,
             dx1_ref, dgates_ref, datt_ref, dy_ref, dwg_hbm, dws_hbm, dwa_hbm, dwo_hbm, dg2_ref, dg3_ref,
             awg_ref, aws_ref, awa_ref, awo_ref, *staged):
        i = pl.program_id(0)
        x1v, ov = x1_ref[...], o_ref[...]
        dxn, dg3 = _rms_bwd(dh2_ref[...], x1v, _rms_scale(x1v), g3_ref[...])
        dx1 = dx2_ref[...] + dxn
        dx1_ref[...] = dx1
        do, dg2 = _rms_bwd(dx1, ov, _rms_scale(ov), g2_ref[...])
        dob = do.astype(BF16)

        yv = y_ref[...]
        att = att_ref[...]
        z, zb, gl, z2b, y_ssm, y_attn, sa, ss, merged = _merge_forward(
            yv, att, ga_ref[...], gs_ref[...], wg_ref[...], ws_ref[...], wa_ref[...])
        dmerged = _dot_nt(dob, wo_ref[...])
        dya = (dmerged * sa).astype(BF16)
        dys = (dmerged * ss).astype(BF16)
        dgates_ref[:, :D_MODEL] = (dmerged * y_attn * sa * (1.0 - sa)).astype(BF16)
        dgates_ref[:, D_MODEL:] = (dmerged * y_ssm * ss * (1.0 - ss)).astype(BF16)
        datt_ref[...] = _dot_nt(dya, wa_ref[...]).astype(BF16)
        dz2 = _dot_nt(dys, ws_ref[...])
        dpre = (dz2 * z * gl * (1.0 - gl)).astype(BF16)
        dz = dz2 * gl + _dot_nt(dpre, wg_ref[...])
        _, gelu_vjp = jax.vjp(jax.nn.gelu, yv)
        dy_ref[...] = gelu_vjp(dz)[0]

        part = pl.ds(pl.multiple_of((i % group) * tile, tile), tile)
        for ref, val in zip(staged, (merged, dob, att, dya, z2b, dys, zb, dpre)):
            ref[part, :] = val

        @pl.when(i == 0)
        def _():
            dg2_ref[...] = dg2
            dg3_ref[...] = dg3

        @pl.when(i > 0)
        def _():
            dg2_ref[...] += dg2
            dg3_ref[...] += dg3

        def weight_grads():
            s_merged, s_dob, s_att, s_dya, s_z2b, s_dys, s_zb, s_dpre = (ref[...] for ref in staged)
            return ((awo_ref, _dot_tn(s_merged, s_dob)), (awa_ref, _dot_tn(s_att, s_dya)),
                    (aws_ref, _dot_tn(s_z2b, s_dys)), (awg_ref, _dot_tn(s_zb, s_dpre)))

        @pl.when(i == group - 1)
        def _():
            for ref, val in weight_grads():
                ref[...] = val

        @pl.when((i % group == group - 1) & (i > group - 1))
        def _():
            for ref, val in weight_grads():
                ref[...] += val

        @pl.when(i == n_steps - 1)
        def _():
            pltpu.sync_copy(awg_ref, dwg_hbm)
            pltpu.sync_copy(aws_ref, dws_hbm)
            pltpu.sync_copy(awa_ref, dwa_hbm)
            pltpu.sync_copy(awo_ref, dwo_hbm)

    tok = lambda w: pl.BlockSpec((tile, w), lambda i: (i, 0))
    vec = _const_spec((1, D_MODEL))
    any_ = pl.BlockSpec(memory_space=pl.ANY)
    vec_out = pl.BlockSpec((1, D_MODEL), lambda i: (0, 0))
    f32 = lambda *s: jax.ShapeDtypeStruct(s, F32)
    bf = lambda *s: jax.ShapeDtypeStruct(s, BF16)
    return _hosted_call(
        body, carry, _edge_1d(n_steps), name="merge_bwd", grid=(n_steps,),
        in_specs=[tok(D_MODEL), tok(D_MODEL), tok(D_MODEL), tok(D_MODEL), tok(SSM_W), tok(ATTN_W), tok(D_MODEL), tok(D_MODEL),
                  vec, vec, _const_spec((SSM_W, SSM_W)), _const_spec((SSM_W, D_MODEL)), _const_spec((ATTN_W, D_MODEL)),
                  _const_spec((D_MODEL, D_MODEL))],
        out_specs=[tok(D_MODEL), tok(2 * D_MODEL), tok(ATTN_W), tok(SSM_W), any_, any_, any_, any_, vec_out, vec_out],
        out_shape=[f32(T, D_MODEL), bf(T, 2 * D_MODEL), bf(T, ATTN_W), f32(T, SSM_W),
                   f32(SSM_W, SSM_W), f32(SSM_W, D_MODEL), f32(ATTN_W, D_MODEL), f32(D_MODEL, D_MODEL),
                   f32(1, D_MODEL), f32(1, D_MODEL)],
        scratch_shapes=[pltpu.VMEM((SSM_W, SSM_W), F32), pltpu.VMEM((SSM_W, D_MODEL), F32),
                        pltpu.VMEM((ATTN_W, D_MODEL), F32), pltpu.VMEM((D_MODEL, D_MODEL), F32)]
        + [pltpu.VMEM((group * tile, wd), BF16) for wd in staged_widths],
        compiler_params=_params(("arbitrary",), VMEM_BIG),
        inputs=(dh2, dx2, x1, o, y, att, ga, gs, g2, g3, w_glu, w_ssm, w_attn, w_out))


FF_SHARD = D_FF // N_DEV


def _mlp_fwd(h2, x1, target, g4, w_ff_in, w_ff_out, tile):
    T = h2.shape[0]
    col_chunk = 2 * FF_SHARD

    def body(h2_ref, x1_ref, tg_ref, g4_ref, wi_ref, wo_ref, a_ref, dfo_ref, dx2_ref, loss_ref, dg4_ref, rr_ref):
        i = pl.program_id(0)
        h2v = h2_ref[...]
        for c in range(D_FF // col_chunk):
            cols = slice(c * col_chunk, (c + 1) * col_chunk)
            a = _dot_nt(h2v, wi_ref[cols, :])
            a_ref[:, cols] = a.astype(BF16)
            ra = jnp.maximum(a, 0.0)
            rr_ref[:, cols] = (ra * ra).astype(BF16)
        f = _dot(rr_ref[...], wo_ref[...])
        r = _rms_scale(f)
        g = g4_ref[...]
        err = x1_ref[...] + f * r * g - tg_ref[...]
        dx2 = err * (1.0 / D_MODEL)
        dx2_ref[...] = dx2
        dfo, dg = _rms_bwd(dx2, f, r, g)
        dfo_ref[...] = dfo.astype(BF16)
        row = lax.broadcasted_iota(jnp.int32, (SUBLANES, LANES), 0)
        col = lax.broadcasted_iota(jnp.int32, (SUBLANES, LANES), 1)
        loss = jnp.where((row == 0) & (col == 0), (0.5 / D_MODEL) * jnp.sum(err * err), 0.0)

        @pl.when(i == 0)
        def _():
            loss_ref[...] = loss
            dg4_ref[...] = dg

        @pl.when(i > 0)
        def _():
            loss_ref[...] += loss
            dg4_ref[...] += dg

    tok = pl.BlockSpec((tile, D_MODEL), lambda i: (i, 0))
    return pl.pallas_call(
        body, name="mlp_fwd", grid=(T // tile,),
        in_specs=[tok, tok, tok, _const_spec((1, D_MODEL)), _const_spec((D_FF, D_MODEL)), _const_spec((D_FF, D_MODEL))],
        out_specs=[pl.BlockSpec((tile, D_FF), lambda i: (i, 0)), tok, tok,
                   pl.BlockSpec((SUBLANES, LANES), lambda i: (0, 0)), pl.BlockSpec((1, D_MODEL), lambda i: (0, 0))],
        out_shape=_hbm_out([jax.ShapeDtypeStruct((T, D_FF), BF16), jax.ShapeDtypeStruct((T, D_MODEL), BF16),
                            jax.ShapeDtypeStruct((T, D_MODEL), F32), jax.ShapeDtypeStruct((SUBLANES, LANES), F32),
                            jax.ShapeDtypeStruct((1, D_MODEL), F32)]),
        scratch_shapes=[pltpu.VMEM((tile, D_FF), BF16)],
        compiler_params=_params(("arbitrary",), VMEM_MAX),
    )(*_in_hbm(h2, x1, target, g4, w_ff_in.reshape(D_FF, D_MODEL), w_ff_out.reshape(D_FF, D_MODEL)))


def _mlp_weight_grads(dfo, a, h2, w_ff_out, row_chunk):
    T = h2.shape[0]

    def body(dfo_ref, h2_ref, a_ref, wo_ref, dwi_ref, dwo_ref, da_ref, rr_ref):
        def rows(r, _):
            sl = pl.ds(pl.multiple_of(r * row_chunk, row_chunk), row_chunk)
            ra = jnp.maximum(a_ref[sl, :].astype(F32), 0.0)
            da_ref[sl, :] = (_dot_nt(dfo_ref[sl, :], wo_ref[0]) * (2.0 * ra)).astype(BF16)
            rr_ref[sl, :] = (ra * ra).astype(BF16)
            return 0

        lax.fori_loop(0, T // row_chunk, rows, 0)
        dwo_ref[0] = _dot_tn(rr_ref[...], dfo_ref[...])
        dwi_ref[0] = _dot_tn(h2_ref[...], da_ref[...])

    return pl.pallas_call(
        body, name="mlp_weight_grads", grid=(N_DEV,),
        in_specs=[_const_spec((T, D_MODEL)), _const_spec((T, D_MODEL)), pl.BlockSpec((T, FF_SHARD), lambda k: (0, k)),
                  pl.BlockSpec((1, FF_SHARD, D_MODEL), lambda k: (k, 0, 0))],
        out_specs=[pl.BlockSpec((1, D_MODEL, FF_SHARD), lambda k: (k, 0, 0)),
                   pl.BlockSpec((1, FF_SHARD, D_MODEL), lambda k: (k, 0, 0)), pl.BlockSpec((T, FF_SHARD), lambda k: (0, k))],
        out_shape=_hbm_out([jax.ShapeDtypeStruct((N_DEV, D_MODEL, FF_SHARD), F32),
                            jax.ShapeDtypeStruct((N_DEV, FF_SHARD, D_MODEL), F32), jax.ShapeDtypeStruct((T, D_FF), BF16)]),
        scratch_shapes=[pltpu.VMEM((T, FF_SHARD), BF16)],
        compiler_params=_params(("arbitrary",), VMEM_MAX),
    )(*_in_hbm(dfo, h2, a, w_ff_out))


def _mlp_input_grad(da, w_ff_in_t, tile):
    T = da.shape[0]

    def body(da_ref, w_ref, o_ref):
        o_ref[...] = _dot(da_ref[...], w_ref[...])

    return pl.pallas_call(
        body, name="mlp_input_grad", grid=(T // tile,),
        in_specs=[pl.BlockSpec((tile, D_FF), lambda i: (i, 0)), _const_spec((D_FF, D_MODEL))],
        out_specs=pl.BlockSpec((tile, D_MODEL), lambda i: (i, 0)),
        out_shape=_hbm_out(jax.ShapeDtypeStruct((T, D_MODEL), F32)),
        compiler_params=_params(("arbitrary",), VMEM_MID),
    )(*_in_hbm(da, w_ff_in_t))


def _block_diag_in(b):
    bt = b.reshape(N_SSM_BLOCKS, GROUPS_PER_BLOCK, GROUP_CH, N_STATE)
    eye = jnp.eye(GROUPS_PER_BLOCK, dtype=b.dtype)
    return jnp.einsum("jacp,ab->jacbp", bt, eye).reshape(N_SSM_BLOCKS, LANES, SSM_LANE_BLOCK)


def _block_diag_in_grad(g):
    g = g.reshape(N_SSM_BLOCKS, GROUPS_PER_BLOCK, GROUP_CH, GROUPS_PER_BLOCK, N_STATE)
    d = jnp.diagonal(g, axis1=1, axis2=3)
    return jnp.transpose(d, (0, 3, 1, 2)).reshape(N_GROUPS, GROUP_CH, N_STATE)


def _block_diag_out(c):
    ct = c.reshape(N_SSM_BLOCKS, GROUPS_PER_BLOCK, GROUP_CH, N_STATE)
    eye = jnp.eye(GROUPS_PER_BLOCK, dtype=c.dtype)
    return jnp.einsum("jacp,ab->japbc", ct, eye).reshape(N_SSM_BLOCKS, SSM_LANE_BLOCK, LANES)


def _block_diag_out_grad(g):
    g = g.reshape(N_SSM_BLOCKS, GROUPS_PER_BLOCK, N_STATE, GROUPS_PER_BLOCK, GROUP_CH)
    d = jnp.diagonal(g, axis1=1, axis2=3)
    return jnp.transpose(d, (0, 3, 2, 1)).reshape(N_GROUPS, GROUP_CH, N_STATE)


def _tiles(T):
    return dict(proj=min(512, T), proj_bwd=min(512, T // 2), merge=min(512, T), merge_bwd=min(256, T),
                mlp_fwd=min(512, T), mlp_bwd=min(512, T), ssm_chunk=min(1024, T))


def _mesh_position():
    x, y, c = lax.axis_index("x"), lax.axis_index("y"), lax.axis_index("c")
    other_chips = [(1 - x, y), (x, 1 - y), (1 - x, 1 - y)]
    return x, y, c, other_chips


def _gather_carry(arrays, pass_on=True):
    n = len(arrays)

    def copies(ins, outs, sems):
        send_sems, recv_sems, local_sems = sems
        x, y, c, chips = _mesh_position()
        me, sibling = (x, y, c), (x, y, 1 - c)

        def copy(a, k, block, to, src=None):
            px, py, pc = block
            dst = outs[a].at[4 * px + 2 * py + pc]
            return pltpu.make_async_remote_copy(
                src_ref=dst if src is None else src, dst_ref=dst, send_sem=send_sems.at[7 * a + k],
                recv_sem=recv_sems.at[7 * a + k], device_id=to, device_id_type=MESH_IDS)

        mine = [pltpu.make_async_copy(ins[a], outs[a].at[4 * x + 2 * y + c], local_sems.at[a]) for a in range(n)]
        first = []
        for a in range(n):
            first.append(copy(a, 0, me, sibling, src=ins[a]))
            first += [copy(a, 1 + j, me, (*chip, c), src=ins[a]) for j, chip in enumerate(chips)]
        return copy, mine, first, me, sibling, chips, c

    def start(ins, outs, sems):
        _, mine, first, *_ = copies(ins, outs, sems)
        for cp in mine + first:
            cp.start()

    def passed_on(copy, sibling, chips, c):
        return [copy(a, 4 + j, (*chip, c), sibling) for a in range(n) for j, chip in enumerate(chips)]

    def middle(ins, outs, sems):
        copy, _, _, me, sibling, chips, c = copies(ins, outs, sems)
        for a in range(n):
            for j, chip in enumerate(chips):
                copy(a, 1 + j, (*chip, c), me).wait_recv()
                copy(a, 4 + j, (*chip, c), sibling).start()

    def finish(ins, outs, sems):
        copy, mine, first, me, sibling, chips, c = copies(ins, outs, sems)
        for a in range(n):
            copy(a, 0, sibling, me).wait_recv()
            for j, chip in enumerate(chips):
                (copy(a, 4 + j, (*chip, 1 - c), me) if pass_on else copy(a, 1 + j, (*chip, c), me)).wait_recv()
        for cp in first + (passed_on(copy, sibling, chips, c) if pass_on else []):
            cp.wait_send()
        for cp in mine:
            cp.wait()

    return _Carry(arrays, [jax.ShapeDtypeStruct((N_DEV,) + a.shape, a.dtype) for a in arrays],
                  [pltpu.SemaphoreType.DMA((7 * n,)), pltpu.SemaphoreType.DMA((7 * n,)), pltpu.SemaphoreType.DMA((n,))],
                  start, finish, middle if pass_on else None)


def _gather_pass_on(gathered, name):
    n = len(gathered)

    def body(*refs):
        zones, send_sems, recv_sems = refs[:n], refs[2 * n], refs[2 * n + 1]
        x, y, c, chips = _mesh_position()
        copies = []
        for a in range(n):
            for j, (px, py) in enumerate(chips):
                block = zones[a].at[4 * px + 2 * py + c]
                copies.append(pltpu.make_async_remote_copy(
                    src_ref=block, dst_ref=block, send_sem=send_sems.at[3 * a + j], recv_sem=recv_sems.at[3 * a + j],
                    device_id=(x, y, 1 - c), device_id_type=MESH_IDS))
        for cp in copies:
            cp.start()
        for cp in copies:
            cp.wait()

    return pl.pallas_call(
        body, name=name, in_specs=[HBM_SPEC] * n, out_specs=[HBM_SPEC] * n,
        out_shape=_hbm_out([jax.ShapeDtypeStruct(g.shape, g.dtype) for g in gathered]),
        scratch_shapes=[pltpu.SemaphoreType.DMA((3 * n,)), pltpu.SemaphoreType.DMA((3 * n,))],
        input_output_aliases={a: a for a in range(n)})(*gathered)


def _pairwise_carry(arrays, n_slots, make_copies):
    n = len(arrays)

    def start(ins, outs, sems):
        for cp in make_copies(ins, outs, sems):
            cp.start()

    def finish(ins, outs, sems):
        for cp in make_copies(ins, outs, sems):
            cp.wait()

    return _Carry(arrays, [jax.ShapeDtypeStruct((n_slots,) + a.shape[1:], a.dtype) for a in arrays],
                  [pltpu.SemaphoreType.DMA((n_slots * n,)), pltpu.SemaphoreType.DMA((n_slots * n,))], start, finish)


def _sibling_carry(grads):
    def make_copies(ins, outs, sems):
        x, y, c, _ = _mesh_position()
        return [pltpu.make_async_remote_copy(
            src_ref=ins[a].at[2 * ch + (1 - c)], dst_ref=outs[a].at[ch], send_sem=sems[0].at[4 * a + ch],
            recv_sem=sems[1].at[4 * a + ch], device_id=(x, y, 1 - c), device_id_type=MESH_IDS)
            for a in range(len(grads)) for ch in range(4)]

    return _pairwise_carry(grads, 4, make_copies)


def _chips_carry(sums):
    def make_copies(ins, outs, sems):
        x, y, c, chips = _mesh_position()
        return [pltpu.make_async_remote_copy(
            src_ref=ins[a].at[2 * px + py], dst_ref=outs[a].at[j], send_sem=sems[0].at[3 * a + j],
            recv_sem=sems[1].at[3 * a + j], device_id=(px, py, c), device_id_type=MESH_IDS)
            for a in range(len(sums)) for j, (px, py) in enumerate(chips)]

    return _pairwise_carry(sums, 3, make_copies)


def _everyone_carry(arrays):
    def make_copies(ins, outs, sems):
        x, y, c, _ = _mesh_position()
        flip = lambda v, bit: 1 - v if bit else v
        return [pltpu.make_async_remote_copy(
            src_ref=ins[a], dst_ref=outs[a].at[r - 1], send_sem=sems[0].at[7 * a + r - 1], recv_sem=sems[1].at[7 * a + r - 1],
            device_id=(flip(x, r & 4), flip(y, r & 2), flip(c, r & 1)), device_id_type=MESH_IDS)
            for a in range(len(arrays)) for r in range(1, N_DEV)]

    carry = _pairwise_carry([jax.ShapeDtypeStruct((1,) + a.shape, a.dtype) for a in arrays], N_DEV - 1, make_copies)
    carry.inputs = list(arrays)
    return carry


def _sum_everyone(own, received, me, name, after=()):
    def body(me_ref, own_ref, r_ref, *refs):
        g = None
        for d in range(N_DEV):
            relation = jnp.bitwise_xor(d, me_ref[0])
            part = jnp.where(relation == 0, own_ref[...], r_ref[jnp.maximum(relation - 1, 0)])
            g = part if g is None else g + part
        refs[-1][...] = g

    whole = lambda shape: pl.BlockSpec(shape, lambda i, me_ref: (0,) * len(shape))
    return pl.pallas_call(
        body, name=name,
        grid_spec=pltpu.PrefetchScalarGridSpec(
            num_scalar_prefetch=1, grid=(1,), in_specs=[whole(own.shape), whole(received.shape)] + [HBM_SPEC] * len(after),
            out_specs=whole(own.shape)),
        out_shape=jax.ShapeDtypeStruct(own.shape, F32))(me, *_in_hbm(own, received), *after)


SEM_SPEC = pl.BlockSpec(memory_space=pltpu.SEMAPHORE)
DATAFLOW_EFFECT = pltpu.SideEffectType.DATAFLOW_SIDE_EFFECTING


def _exchange_start(carry, name, after=()):
    n, n_sems = len(carry.inputs), len(carry.sems)
    lands = [lax.empty(s.shape, s.dtype) for s in carry.out_shapes]

    def body(*refs):
        first_out = 2 * n + len(after)
        srcs, zones, sems, token = refs[:n], refs[n:2 * n], refs[first_out:first_out + n_sems], refs[-1]
        carry.start(srcs, zones, sems)
        token[...] = jnp.zeros_like(token)

    outs = pl.pallas_call(
        body, name=name, in_specs=[HBM_SPEC] * (2 * n + len(after)),
        out_specs=[SEM_SPEC] * n_sems + [HBM_SPEC] * (2 * n) + [pl.BlockSpec(memory_space=pltpu.VMEM)],
        out_shape=list(carry.sems) + _hbm_out([jax.ShapeDtypeStruct(a.shape, a.dtype) for a in carry.inputs])
        + _hbm_out(carry.out_shapes) + [jax.ShapeDtypeStruct((SUBLANES, LANES), F32)],
        input_output_aliases={j: n_sems + j for j in range(2 * n)},
        compiler_params=pltpu.CompilerParams(has_side_effects=DATAFLOW_EFFECT),
    )(*_in_hbm(*carry.inputs, *lands), *after)
    return outs[:-1], outs[-1]


def _exchange_wait(carry, in_flight, after, name):
    n, n_sems = len(carry.inputs), len(carry.sems)
    sems, srcs, zones = in_flight[:n_sems], in_flight[n_sems:n_sems + n], in_flight[n_sems + n:]

    def body(*refs):
        src_refs, zone_refs, sem_refs = refs[:n], refs[n:2 * n], refs[2 * n:2 * n + n_sems]
        carry.finish(src_refs, zone_refs, sem_refs)

    outs = pl.pallas_call(
        body, name=name, in_specs=[HBM_SPEC] * (2 * n) + [SEM_SPEC] * n_sems + [HBM_SPEC] * len(after),
        out_specs=[HBM_SPEC] * (2 * n),
        out_shape=_hbm_out([jax.ShapeDtypeStruct(a.shape, a.dtype) for a in carry.inputs]) + _hbm_out(carry.out_shapes),
        input_output_aliases={j: j for j in range(2 * n)},
        compiler_params=pltpu.CompilerParams(has_side_effects=DATAFLOW_EFFECT),
    )(*srcs, *zones, *sems, *after)
    return list(outs[:n]), list(outs[n:])


def _add_sibling(grads8, recvs, place, row_tiles, name):
    k = len(grads8)
    g4 = [g.reshape(4, 2, *g.shape[1:]) for g in grads8]

    def body(place_ref, *refs):
        g_refs, r_refs, o_refs, ob_refs = (refs[j * k:(j + 1) * k] for j in range(4))
        own = pl.program_id(1) == place_ref[1]
        for g_ref, r_ref, o_ref, ob_ref in zip(g_refs, r_refs, o_refs, ob_refs):
            s = g_ref[0] + r_ref[...]
            ob_ref[...] = s.astype(BF16)

            @pl.when(own)
            def _(o_ref=o_ref, s=s):
                o_ref[...] = s[0]

    def blocks(make):
        return [make(g.shape[1] // row_tiles, g.shape[2]) for g in grads8]

    slot = lambda tr, C: pl.BlockSpec((1, tr, C), lambda r, ch, place_ref: (ch, r, 0))
    outs = pl.pallas_call(
        body, name=name,
        grid_spec=pltpu.PrefetchScalarGridSpec(
            num_scalar_prefetch=1, grid=(row_tiles, 4),
            in_specs=blocks(lambda tr, C: pl.BlockSpec((1, 1, tr, C), lambda r, ch, place_ref: (ch, place_ref[0], r, 0)))
            + blocks(slot),
            out_specs=blocks(lambda tr, C: pl.BlockSpec((tr, C), lambda r, ch, place_ref: (r, 0))) + blocks(slot)),
        out_shape=_hbm_out([jax.ShapeDtypeStruct(g.shape[1:], F32) for g in grads8]
                           + [jax.ShapeDtypeStruct((4,) + g.shape[1:], BF16) for g in grads8]),
        compiler_params=_params(("arbitrary", "arbitrary")),
    )(place, *_in_hbm(*g4, *recvs))
    return list(outs[:k]), list(outs[k:])


def _adam_math(w, g, m, v):
    m = ADAM_B1 * m + (1.0 - ADAM_B1) * g
    v = ADAM_B2 * v + (1.0 - ADAM_B2) * jnp.square(g)
    m_hat = m / (1.0 - ADAM_B1 ** ADAM_STEP)
    v_hat = v / (1.0 - ADAM_B2 ** ADAM_STEP)
    delta = -ADAM_LR * (m_hat / (jnp.sqrt(v_hat) + ADAM_EPS) + ADAM_WD * w)
    return delta, m, v


def _adam_big(ws, ms, vs, chip_sums, recvs, row_tiles, name, after=()):
    k = len(ws)

    def body(*refs):
        refs = refs[:5 * k] + refs[5 * k + len(after):]
        w_refs, m_refs, v_refs, s_refs, r_refs, g_refs, d_refs, nm_refs, nv_refs = (refs[j * k:(j + 1) * k] for j in range(9))
        for a in range(k):
            r_ref = r_refs[a]
            g = s_refs[a][...] + r_ref[0].astype(F32) + r_ref[1].astype(F32) + r_ref[2].astype(F32)
            g_refs[a][...] = g
            d_refs[a][...], nm_refs[a][...], nv_refs[a][...] = _adam_math(w_refs[a][...], g, m_refs[a][...], v_refs[a][...])

    def blocks(make):
        return [make(w.shape[0] // row_tiles, w.shape[1]) for w in ws]

    blk = lambda tr, C: pl.BlockSpec((tr, C), lambda r: (r, 0))
    outs = pl.pallas_call(
        body, name=name, grid=(row_tiles,),
        in_specs=blocks(blk) * 4 + blocks(lambda tr, C: pl.BlockSpec((3, tr, C), lambda r: (0, r, 0))) + [HBM_SPEC] * len(after),
        out_specs=blocks(blk) * 4,
        out_shape=[jax.ShapeDtypeStruct(w.shape, F32) for w in ws] * 4,
        compiler_params=_params(("arbitrary",)),
    )(*_in_hbm(*ws, *ms, *vs, *chip_sums, *recvs), *after)
    return [list(outs[j * k:(j + 1) * k]) for j in range(4)]


def _sum_partials(partials, name, after=()):
    def body(p_ref, *refs):
        g = p_ref[0]
        for d in range(1, partials.shape[0]):
            g = g + p_ref[d]
        refs[-1][...] = g

    return pl.pallas_call(body, name=name, grid=(1,), in_specs=[_whole(partials.shape)] + [HBM_SPEC] * len(after),
                          out_specs=_whole(partials.shape[1:]),
                          out_shape=jax.ShapeDtypeStruct(partials.shape[1:], F32))(*_in_hbm(partials), *after)


def _adam_small(ws, ms, vs, gs):
    n = len(ws)

    def body(*refs):
        w_refs, m_refs, v_refs, g_refs = (refs[i * n:(i + 1) * n] for i in range(4))
        d_refs, nm_refs, nv_refs = (refs[(4 + i) * n:(5 + i) * n] for i in range(3))
        for j in range(n):
            d_refs[j][...], nm_refs[j][...], nv_refs[j][...] = _adam_math(
                w_refs[j][...], g_refs[j][...], m_refs[j][...], v_refs[j][...])

    specs = [_whole(w.shape) for w in ws]
    params = pltpu.CompilerParams(dimension_semantics=("arbitrary",), vmem_limit_bytes=VMEM_MID,
                                  allow_input_fusion=[False] * (3 * n) + [True] * n)
    outs = pl.pallas_call(body, name="adam_small", grid=(1,), in_specs=specs * 4, out_specs=specs * 3,
                          out_shape=[jax.ShapeDtypeStruct(w.shape, F32) for w in ws] * 3,
                          compiler_params=params)(*_in_hbm(*ws, *ms, *vs), *gs)
    return outs[:n], outs[n:2 * n], outs[2 * n:]


PACK_QUANTUM = SUBLANES * LANES


def _pack(named, names):
    parts = []
    for nme in names:
        flat = named[nme].reshape(-1)
        parts.append(jnp.pad(flat, (0, -flat.size % PACK_QUANTUM)))
    return jnp.concatenate(parts).reshape(-1, LANES)


def _unpack(packed, shapes, names):
    flat = packed.reshape(-1)
    out, pos = {}, 0
    for nme in names:
        size = math.prod(shapes[nme])
        out[nme] = flat[pos:pos + size].reshape(shapes[nme])
        pos += size + (-size % PACK_QUANTUM)
    return out


BIG = ("w_in", "w_glu", "w_attn_branch", "w_ssm_branch", "w_out", "w_ff_in", "w_ff_out")
COLUMN_SHARDED = ("w_in", "w_attn_branch", "w_ssm_branch", "w_ff_in")
SMALL = ("norm_mix_pre", "norm_mix_post", "norm_mlp_pre", "norm_mlp_post", "rel_bias", "sinks", "lam_re", "lam_im",
         "log_dt", "b_re", "b_im", "c_re", "c_im", "d_skip")
SWAPPED_SMALL = ("rel_bias", "b_re", "b_im")
SMALL_LATE = ("norm_mix_pre", "rel_bias", "sinks", "loss")
SMALL_BEFORE_ATTN_BWD = tuple(n for n in SMALL if n not in SMALL_LATE)
ALL_WEIGHTS = ("norm_mix_pre", "norm_mix_post", "norm_mlp_pre", "norm_mlp_post", "w_in", "rel_bias", "sinks", "lam_re",
               "lam_im", "log_dt", "b_re", "b_im", "c_re", "c_im", "d_skip", "w_glu", "w_attn_branch", "w_ssm_branch",
               "w_out", "w_ff_in", "w_ff_out")


def _full_from_gathered(name, gathered):
    _, r, c = gathered.shape
    if name in COLUMN_SHARDED:
        return jnp.transpose(gathered, (1, 0, 2)).reshape(r, N_DEV * c)
    return gathered.reshape(N_DEV * r, c)


def _blocks_from_full(name, full):
    r, c = full.shape
    if name in COLUMN_SHARDED:
        return jnp.transpose(full.reshape(r, N_DEV, c // N_DEV), (1, 0, 2))
    return full.reshape(N_DEV, r // N_DEV, c)


def kernel(x, norm_mix_pre, norm_mix_post, norm_mlp_pre, norm_mlp_post, w_in, rel_bias, sinks, lam_re, lam_im, log_dt, b_re, b_im, c_re, c_im, d_skip, w_glu, w_attn_branch, w_ssm_branch, w_out, w_ff_in, w_ff_out, loss_target, m_norm_mix_pre, m_norm_mix_post, m_norm_mlp_pre, m_norm_mlp_post, m_w_in, m_rel_bias, m_sinks, m_lam_re, m_lam_im, m_log_dt, m_b_re, m_b_im, m_c_re, m_c_im, m_d_skip, m_w_glu, m_w_attn_branch, m_w_ssm_branch, m_w_out, m_w_ff_in, m_w_ff_out, v_norm_mix_pre, v_norm_mix_post, v_norm_mlp_pre, v_norm_mlp_post, v_w_in, v_rel_bias, v_sinks, v_lam_re, v_lam_im, v_log_dt, v_b_re, v_b_im, v_c_re, v_c_im, v_d_skip, v_w_glu, v_w_attn_branch, v_w_ssm_branch, v_w_out, v_w_ff_in, v_w_ff_out):
    args = dict(locals())
    w = {n: args[n] for n in ALL_WEIGHTS}
    m = {n: args["m_" + n] for n in ALL_WEIGHTS}
    v = {n: args["v_" + n] for n in ALL_WEIGHTS}
    core = lax.axis_index("c").astype(jnp.int32).reshape(1)
    chip = (2 * lax.axis_index("x") + lax.axis_index("y")).astype(jnp.int32).reshape(1)
    xs, target = x[0], loss_target[0]
    t = _tiles(xs.shape[0])
    local = lambda d, n: d[n][0].T if n == "w_in" else d[n][0]
    gather_in = _gather_carry([local(w, "w_in").astype(BF16)], pass_on=False)
    gather_in_flight, token = _exchange_start(gather_in, "gather_w_in_start")
    one = token[0:1, 0:1] + 1.0
    shard = {n: (local(w, n) * one).astype(BF16) for n in BIG if n != "w_in"}
    shard["w_ff_in"] = shard["w_ff_in"].T
    view = lambda n, a: jnp.swapaxes(a, -1, -2) if n in SWAPPED_SMALL else a
    small = {n: (view(n, w[n]) if n == "rel_bias" else view(n, w[n])[0]) for n in SMALL}
    g1, g2, g3, g4 = (small[n].reshape(1, D_MODEL) for n in ("norm_mix_pre", "norm_mix_post", "norm_mlp_pre", "norm_mlp_post"))
    bucket = jnp.asarray(_bucket_table())
    rel_b, sink = small["rel_bias"], small["sinks"].reshape(1, N_HEADS)
    lam_r, lam_i = small["lam_re"].reshape(1, STATES), small["lam_im"].reshape(1, STATES)
    ldt_rep = jnp.repeat(small["log_dt"].reshape(N_GROUPS), N_STATE).reshape(1, STATES)
    bd_re, bd_im = _block_diag_in(small["b_re"] * one), _block_diag_in(small["b_im"] * one)
    cm_re, cm_im = _block_diag_out(small["c_re"] * one).astype(BF16), _block_diag_out(small["c_im"] * one).astype(BF16)
    dsk = small["d_skip"].reshape(1, SSM_W)
    a_re, a_im, bm_re, bm_im = _ssm_prep(lam_r, lam_i, ldt_rep, bd_re, bd_im)

    travelled_behind = list(shard.values()) + [a_re, a_im, bm_re, bm_im, cm_re, cm_im]
    _, landed = _exchange_wait(gather_in, gather_in_flight, travelled_behind, "gather_w_in_wait")
    (g_in,) = _gather_pass_on(landed, "gather_w_in_pass_on")
    wf_in = g_in.reshape(IN_W, D_MODEL)
    merge_names = ("w_glu", "w_attn_branch", "w_ssm_branch", "w_out")
    (q, k, vv, u, ga, gs, h), gathered = _in_proj_fwd(xs, g1, wf_in, t["proj"], _gather_carry([shard[n] for n in merge_names]))
    wf = {n: _full_from_gathered(n, g) for n, g in zip(merge_names, gathered)}
    (att,), (wf_ff_in,) = _attn_fwd(q, k, vv, bucket, rel_b, sink, _gather_carry([shard["w_ff_in"]]))
    (y, h_re, h_im, in_re, in_im), (wf_ff_out,) = _ssm_fwd(
        u, a_re, a_im, bm_re, bm_im, cm_re, cm_im, dsk, t["ssm_chunk"], _gather_carry([shard["w_ff_out"]]))
    x1, o, h2 = _merge_fwd(xs, y, att, ga, gs, g2, g3, wf["w_glu"], wf["w_ssm_branch"], wf["w_attn_branch"], wf["w_out"],
                           t["merge"])
    a, dfo, dx2, loss_blk, dg4 = _mlp_fwd(h2, x1, target, g4, wf_ff_in, wf_ff_out, t["mlp_fwd"])

    groups = {"ff": 4, "merge": 1, "w_in": 2}

    def add_sibling(group, blocks, received):
        return _add_sibling(blocks, received, jnp.concatenate([core, chip]), groups[group], "add_sibling_" + group)

    ff_names = ("w_ff_in", "w_ff_out")
    dw_ff_in, dw_ff_out, da = _mlp_weight_grads(dfo, a, h2, wf_ff_out, t["mlp_bwd"])
    dh2 = _mlp_input_grad(da, wf_ff_in.reshape(D_FF, D_MODEL), t["mlp_bwd"])
    ff_blocks = [dw_ff_in, dw_ff_out]
    (dx1, dgates, datt, dy, dw_glu, dw_ssm, dw_attn, dw_out, dg2, dg3), ff_recv = _merge_bwd(
        dh2, dx2, x1, o, y, att, ga, gs, g2, g3, wf["w_glu"], wf["w_ssm_branch"], wf["w_attn_branch"], wf["w_out"],
        t["merge_bwd"], _sibling_carry(ff_blocks))
    ff_sums, ff_sums_bf = add_sibling("ff", ff_blocks, ff_recv)
    merge_blocks = [_blocks_from_full(n, g) for n, g in zip(merge_names, (dw_glu, dw_attn, dw_ssm, dw_out))]
    (du, dbm_re, dbm_im, dcm_re, dcm_im, da_re, da_im, dd_skip), carried = _ssm_bwd(
        dy, u, h_re, h_im, in_re, in_im, a_re, a_im, bm_re, bm_im, cm_re, cm_im, dsk, t["ssm_chunk"],
        _join(_chips_carry(ff_sums_bf), _sibling_carry(merge_blocks)))
    ff_from_chips, merge_recv = carried[:2], carried[2:]
    merge_sums, merge_sums_bf = add_sibling("merge", merge_blocks, merge_recv)
    dbd_re, dbd_im, dlam_re, dlam_im, dldt_rep = _ssm_prep_bwd(lam_r, lam_i, ldt_rep, bd_re, bd_im, dbm_re, dbm_im, da_re, da_im)
    dlog_dt = _group_sum(dldt_rep.reshape(N_GROUPS, N_STATE))
    shapes = {n: view(n, w[n]).shape for n in SMALL}
    shapes["loss"] = (1,)
    small_grads = dict(
        norm_mix_post=dg2, norm_mlp_pre=dg3, norm_mlp_post=dg4, lam_re=dlam_re, lam_im=dlam_im, log_dt=dlog_dt,
        b_re=_block_diag_in_grad(dbd_re), b_im=_block_diag_in_grad(dbd_im),
        c_re=_block_diag_out_grad(dcm_re), c_im=_block_diag_out_grad(dcm_im), d_skip=dd_skip)
    packed_early = _pack({n: small_grads[n].reshape(shapes[n]) for n in SMALL_BEFORE_ATTN_BWD}, SMALL_BEFORE_ATTN_BWD)
    (dq, dkv, attn_small), carried = _attn_bwd(
        q, k, vv, datt, bucket, rel_b, sink, _join(_chips_carry(merge_sums_bf), _gather_carry([packed_early])))
    merge_from_chips, partials_early = carried[:-1], carried[-1]

    dparts = (dq, dkv, du, dgates)
    dw_in_t = _in_proj_weight_grad(h, dparts)
    in_blocks = [dw_in_t.reshape(N_DEV, IN_W // N_DEV, D_MODEL)]
    to_sibling = _sibling_carry(in_blocks)
    in_flight, token = _exchange_start(to_sibling, "w_in_sibling_start")
    n_tiles = xs.shape[0] // t["proj_bwd"]
    (grad_x, dg1), _ = _in_proj_input_grad(xs, g1 + token[0:1, 0:1], wf_in, dx1, dparts, t["proj_bwd"], 0, n_tiles, "in_proj_input_grad")
    late = dict(norm_mix_pre=dg1, rel_bias=attn_small[:, :N_BUCKETS, 0], sinks=attn_small[:, N_BUCKETS, 0], loss=loss_blk[0:1, 0])
    packed_late = _pack({n: late[n].reshape(shapes[n]) for n in SMALL_LATE}, SMALL_LATE)
    to_everyone = _everyone_carry([packed_late])
    in_blocks, in_recv = _exchange_wait(to_sibling, in_flight, [packed_late], "w_in_sibling_wait")
    in_sums, in_sums_bf = add_sibling("w_in", in_blocks, in_recv)
    to_chips = _chips_carry(in_sums_bf)
    started, chips_started = _exchange_start(_join(to_everyone, to_chips), "late_grads_and_w_in_chips_start")
    late_in_flight, in_flight = [[started[j] for j in js] for js in ((0, 1, 4, 6), (2, 3, 5, 7))]

    grads, deltas, new_m, new_v = {}, {}, {}, {}

    def adam_group(group, names, sums, received, after=()):
        outs = _adam_big(*[[local(d, n) for n in names] for d in (w, m, v)], sums, received, groups[group],
                         "adam_" + group, after)
        for store, vals in zip((grads, deltas, new_m, new_v), outs):
            store.update({n: (o.T if n == "w_in" else o)[None] for n, o in zip(names, vals)})

    adam_group("ff", ff_names, ff_sums, ff_from_chips, [chips_started])
    adam_group("merge", merge_names, merge_sums, merge_from_chips, [chips_started])

    grads.update(_unpack(_sum_partials(partials_early, "sum_small_grads", [chips_started]), shapes, SMALL_BEFORE_ATTN_BWD))
    (packed_late,), (late_received,) = _exchange_wait(
        to_everyone, late_in_flight, [new_v["w_ff_out"], new_v["w_out"]], "late_grads_wait")
    grads.update(_unpack(_sum_everyone(packed_late, late_received, 2 * chip + core, "sum_late_grads"), shapes, SMALL_LATE))
    loss = grads.pop("loss").reshape(())
    small_out = _adam_small(*[[view(n, d[n]) for n in SMALL] for d in (w, m, v)], [grads[n] for n in SMALL])
    for store, vals in zip((deltas, new_m, new_v), small_out):
        store.update(zip(SMALL, vals))
    for store in (grads, deltas, new_m, new_v):
        store.update({n: view(n, store[n]) for n in SWAPPED_SMALL})

    busy = [new_v["w_ff_out"], new_v["w_out"], deltas["norm_mix_pre"]]
    _, (in_from_chips,) = _exchange_wait(to_chips, in_flight, busy, "w_in_chips_wait")
    adam_group("w_in", ("w_in",), in_sums, [in_from_chips])

    return (loss, grad_x[None], *[grads[n] for n in ALL_WEIGHTS], *[deltas[n] for n in ALL_WEIGHTS],
            *[new_m[n] for n in ALL_WEIGHTS], *[new_v[n] for n in ALL_WEIGHTS])
```

```python
import math

import jax
import jax.numpy as jnp
import numpy as np
from jax import lax
from jax.experimental import pallas as pl
from jax.experimental.pallas import tpu as pltpu

F32 = jnp.float32
BF16 = jnp.bfloat16

D_MODEL = 1024
N_HEADS = 8
HEAD_DIM = 64
ATTN_W = 512
KV_W = 128
BLOCK = 128
N_BUCKETS = 32
SSM_W = 512
N_GROUPS = 32
N_STATE = 64
GROUP_CH = 16
STATES = N_GROUPS * N_STATE
D_FF = 4096
IN_W = 3328
SPLITS = (0, 512, 640, 768, 1280, 2304, 3328)
RMS_EPS = 1e-6
NEG_INF = -1e30
SUBLANES = 8
LANES = 128
SSM_LANE_BLOCK = 512
N_SSM_BLOCKS = STATES // SSM_LANE_BLOCK
GROUPS_PER_BLOCK = SSM_LANE_BLOCK // N_STATE
VMEM_BIG = 52 * 1024 * 1024
VMEM_MID = 40 * 1024 * 1024
VMEM_MAX = 60 * 1024 * 1024

ADAM_LR = 0.001
ADAM_B1 = 0.9
ADAM_B2 = 0.999
ADAM_EPS = 1e-08
ADAM_WD = 0.01
ADAM_STEP = 10

N_DEV = 8


def _dot(a, b):
    return jnp.dot(a, b, preferred_element_type=F32)


def _dot_nt(a, b):
    return lax.dot_general(a, b, (((1,), (1,)), ((), ())), preferred_element_type=F32)


def _dot_tn(a, b):
    return lax.dot_general(a, b, (((0,), (0,)), ((), ())), preferred_element_type=F32)


def _rms_scale(x):
    return lax.rsqrt(jnp.mean(x * x, axis=-1, keepdims=True) + RMS_EPS)


def _rms_bwd(dy, x, r, g):
    t = dy * g
    dx = r * t - x * (r * r * r) * jnp.mean(t * x, axis=-1, keepdims=True)
    dg = jnp.sum(dy * x * r, axis=0, keepdims=True)
    return dx, dg


def _const_spec(shape):
    nd = len(shape)
    return pl.BlockSpec(shape, lambda *_: (0,) * nd, pipeline_mode=pl.Buffered(1))


def _in_hbm(*arrays):
    return tuple(pltpu.with_memory_space_constraint(a, pltpu.HBM) for a in arrays)


def _hbm_out(shapes):
    if isinstance(shapes, (list, tuple)):
        return [_hbm_out(s) for s in shapes]
    return shapes if isinstance(shapes, pl.MemoryRef) else pltpu.HBM(shapes.shape, shapes.dtype)


def _whole(shape):
    nd = len(shape)
    return pl.BlockSpec(shape, lambda *_: (0,) * nd)


def _params(sem, vmem=None):
    return pltpu.CompilerParams(dimension_semantics=sem, vmem_limit_bytes=vmem)


MESH_IDS = pl.DeviceIdType.MESH
HBM_SPEC = pl.BlockSpec(memory_space=pl.ANY)


class _Carry:
    def __init__(self, inputs, out_shapes, sems, start, finish, middle=None):
        self.inputs, self.out_shapes, self.sems = list(inputs), list(out_shapes), list(sems)
        self.start, self.middle, self.finish = start, middle, finish


def _join(a, b):
    na_in, na_out, na_sem = len(a.inputs), len(a.out_shapes), len(a.sems)

    def both(phase):
        def run(ins, outs, sems):
            for carry, lo in ((a, True), (b, False)):
                part = (lambda seq, n: seq[:n] if lo else seq[n:])
                if getattr(carry, phase) is not None:
                    getattr(carry, phase)(part(ins, na_in), part(outs, na_out), part(sems, na_sem))
        return run

    middle = both("middle") if (a.middle or b.middle) else None
    return _Carry(a.inputs + b.inputs, a.out_shapes + b.out_shapes, a.sems + b.sems, both("start"), both("finish"), middle)


def _hosted_call(body, carry, edge, *, name, grid, in_specs, out_specs, out_shape, scratch_shapes, compiler_params, inputs):
    n_in, n_out = len(in_specs), len(out_specs)
    inputs = [a if s.memory_space == pltpu.SMEM else _in_hbm(a)[0] for a, s in zip(inputs, in_specs)]
    out_shape = _hbm_out(list(out_shape))
    if carry is None:
        outs = pl.pallas_call(body, name=name, grid=grid, in_specs=in_specs, out_specs=out_specs, out_shape=out_shape,
                              scratch_shapes=scratch_shapes, compiler_params=compiler_params)(*inputs)
        return list(outs), []
    c_in, c_out, c_sem = len(carry.inputs), len(carry.out_shapes), len(carry.sems)

    def wrapped(*refs):
        ins, refs = refs[:n_in], refs[n_in:]
        cins, refs = refs[:c_in], refs[c_in:]
        outs, refs = refs[:n_out], refs[n_out:]
        couts, refs = refs[:c_out], refs[c_out:]
        scratch, csems = refs[:len(refs) - c_sem], refs[len(refs) - c_sem:]
        first, middle, last = edge()

        @pl.when(first)
        def _():
            carry.start(cins, couts, csems)

        body(*ins, *outs, *scratch)

        if carry.middle is not None:
            @pl.when(middle)
            def _():
                carry.middle(cins, couts, csems)

        @pl.when(last)
        def _():
            carry.finish(cins, couts, csems)

    outs = pl.pallas_call(
        wrapped, name=name, grid=grid, in_specs=list(in_specs) + [HBM_SPEC] * c_in,
        out_specs=list(out_specs) + [HBM_SPEC] * c_out, out_shape=out_shape + _hbm_out(carry.out_shapes),
        scratch_shapes=list(scratch_shapes) + carry.sems, compiler_params=compiler_params)(*inputs, *_in_hbm(*carry.inputs))
    return list(outs[:n_out]), list(outs[n_out:])


def _pass_on_step(n_steps):
    return max(0, min((7 * n_steps) // 8, n_steps - 2))


def _edge_1d(n_steps, pass_on_last=False):
    middle = n_steps - 1 if pass_on_last else _pass_on_step(n_steps)
    return lambda: (pl.program_id(0) == 0, pl.program_id(0) == middle, pl.program_id(0) == n_steps - 1)


def _edge_2d(n0, n1):
    def edge():
        step = pl.program_id(0) * n1 + pl.program_id(1)
        return step == 0, step == _pass_on_step(n0 * n1), step == n0 * n1 - 1
    return edge


def _in_proj_fwd(x, g1, w_in_t, tile, carry=None):
    T = x.shape[0]

    def body(x_ref, g_ref, w_ref, q_ref, k_ref, v_ref, u_ref, ga_ref, gs_ref, h_ref):
        xv = x_ref[...]
        h = (xv * _rms_scale(xv) * g_ref[...]).astype(BF16)
        h_ref[...] = h
        outs = (q_ref, k_ref, v_ref, u_ref, ga_ref, gs_ref)
        for p, o_ref in enumerate(outs):
            o_ref[...] = _dot_nt(h, w_ref[SPLITS[p]:SPLITS[p + 1], :]).astype(o_ref.dtype)

    widths = [SPLITS[p + 1] - SPLITS[p] for p in range(6)] + [D_MODEL]
    dtypes = [BF16, BF16, BF16, F32, F32, F32, BF16]
    return _hosted_call(
        body, carry, _edge_1d(T // tile), name="in_proj_fwd", grid=(T // tile,),
        in_specs=[pl.BlockSpec((tile, D_MODEL), lambda i: (i, 0)), _const_spec((1, D_MODEL)), _const_spec((IN_W, D_MODEL))],
        out_specs=[pl.BlockSpec((tile, w), lambda i: (i, 0)) for w in widths],
        out_shape=[jax.ShapeDtypeStruct((T, w), dt) for w, dt in zip(widths, dtypes)],
        scratch_shapes=[], compiler_params=_params(("arbitrary",), VMEM_MID), inputs=(x, g1, w_in_t))


PROJ_PARTS = (512, 256, 512, 2048)
PROJ_GRAD_BLOCK = 256


def _in_proj_weight_grad(h, dparts):
    T = h.shape[0]
    blocks = [wd // PROJ_GRAD_BLOCK for wd in PROJ_PARTS]
    starts = [sum(blocks[:p]) for p in range(len(blocks))]

    def body(h_ref, *refs):
        part_refs, o_ref = refs[:-1], refs[-1]
        j = pl.program_id(0)
        for p_ref, start, count in zip(part_refs, starts, blocks):
            @pl.when((j >= start) & (j < start + count))
            def _(p_ref=p_ref):
                o_ref[...] = _dot_tn(p_ref[...], h_ref[...])

    def part_spec(start, count):
        return pl.BlockSpec((T, PROJ_GRAD_BLOCK), lambda j: (0, jnp.clip(j - start, 0, count - 1)))

    return pl.pallas_call(
        body, name="in_proj_weight_grad", grid=(sum(blocks),),
        in_specs=[_const_spec((T, D_MODEL))] + [part_spec(s, c) for s, c in zip(starts, blocks)],
        out_specs=pl.BlockSpec((PROJ_GRAD_BLOCK, D_MODEL), lambda j: (j, 0)),
        out_shape=_hbm_out(jax.ShapeDtypeStruct((IN_W, D_MODEL), F32)),
        compiler_params=_params(("arbitrary",), VMEM_MID),
    )(*_in_hbm(h, *dparts))


def _in_proj_input_grad(x, g1, w_in_t, dx1, dparts, tile, first_tile, n_tiles, name, carry=None):
    offsets = [sum(PROJ_PARTS[:p]) for p in range(len(PROJ_PARTS))]

    def body(x_ref, g_ref, w_ref, dx1_ref, *refs):
        part_refs, (gx_ref, dg_ref) = refs[:len(PROJ_PARTS)], refs[len(PROJ_PARTS):]
        i = pl.program_id(0)
        xv = x_ref[...]
        r = _rms_scale(xv)
        g = g_ref[...]
        dh = sum(_dot(p_ref[...], w_ref[off:off + wd, :]) for p_ref, off, wd in zip(part_refs, offsets, PROJ_PARTS))
        dxn, dg = _rms_bwd(dh, xv, r, g)
        gx_ref[...] = dx1_ref[...] + dxn

        @pl.when(i == 0)
        def _():
            dg_ref[...] = dg

        @pl.when(i > 0)
        def _():
            dg_ref[...] += dg

    tok = lambda wd: pl.BlockSpec((tile, wd), lambda i: (i + first_tile, 0))
    return _hosted_call(
        body, carry, _edge_1d(n_tiles), name=name, grid=(n_tiles,),
        in_specs=[tok(D_MODEL), _const_spec((1, D_MODEL)), _const_spec((IN_W, D_MODEL)), tok(D_MODEL)] + [tok(wd) for wd in PROJ_PARTS],
        out_specs=[pl.BlockSpec((tile, D_MODEL), lambda i: (i, 0)), pl.BlockSpec((1, D_MODEL), lambda i: (0, 0))],
        out_shape=[jax.ShapeDtypeStruct((n_tiles * tile, D_MODEL), F32), jax.ShapeDtypeStruct((1, D_MODEL), F32)],
        scratch_shapes=[], compiler_params=_params(("arbitrary",), VMEM_MID), inputs=(x, g1, w_in_t, dx1, *dparts))


def _bucket_table():
    qi = np.arange(BLOCK)[:, None]
    kj = np.arange(2 * BLOCK)[None, :]
    dist = qi + BLOCK - kj
    max_exact = N_BUCKETS // 2
    d = np.maximum(dist, 0)
    df = np.maximum(d, 1).astype(np.float32)
    large = max_exact + (np.log(df / np.float32(max_exact)) / np.float32(math.log(BLOCK / max_exact))
                         * np.float32(N_BUCKETS - max_exact)).astype(np.int32)
    large = np.minimum(large, N_BUCKETS - 1)
    bucket = np.where(d < max_exact, d, large)
    return np.where((dist >= 0) & (dist < BLOCK), bucket, -1).astype(np.int32)


def _build_bias(bucket_ref, rb_ref, bias_ref):
    bk = bucket_ref[...]
    for h in range(N_HEADS):
        def add(b, acc, h=h):
            return acc + jnp.where(bk == b, rb_ref[h, b], 0.0)
        bias_ref[h] = lax.fori_loop(0, N_BUCKETS, add, jnp.zeros((BLOCK, 2 * BLOCK), F32))


def _kv_variants(prev_ref, cur_ref):
    cat = jnp.concatenate([prev_ref[...], cur_ref[...]], axis=0)
    lo = lax.broadcasted_iota(jnp.int32, cat.shape, 1) < HEAD_DIM
    zero = jnp.zeros_like(cat)
    head0_lo = jnp.where(lo, cat, zero)
    head1_hi = jnp.where(lo, zero, cat)
    return ((head0_lo, pltpu.roll(head0_lo, HEAD_DIM, 1)), (pltpu.roll(head1_hi, HEAD_DIM, 1), head1_hi))


def _merge_kv_grads(g):
    lo = lax.broadcasted_iota(jnp.int32, g[0][0].shape, 1) < HEAD_DIM
    return jnp.where(lo, g[0][0] + pltpu.roll(g[0][1], HEAD_DIM, 1), g[1][1] + pltpu.roll(g[1][0], HEAD_DIM, 1))


def _head_lanes(h):
    return slice((h // 2) * LANES, (h // 2 + 1) * LANES)


def _attn_probs(q_ref, kvar, bias_ref, sk_ref, valid, s_ref):
    for h in range(N_HEADS):
        s_ref[h] = _dot_nt(q_ref[:, _head_lanes(h)], kvar[h // 4][h % 2])
    head = lax.broadcasted_iota(jnp.int32, (N_HEADS, 1, 1), 0)
    sink = jnp.zeros((N_HEADS, 1, 1), F32)
    for h in range(N_HEADS):
        sink = jnp.where(head == h, sk_ref[0, h], sink)
    s = jnp.where(valid[None], s_ref[...] * (HEAD_DIM ** -0.5) + bias_ref[...], NEG_INF)
    m = jnp.maximum(jnp.max(s, axis=-1, keepdims=True), sink)
    p = jnp.exp(s - m)
    e_sink = jnp.exp(sink - m)
    inv = 1.0 / (jnp.sum(p, axis=-1, keepdims=True) + e_sink)
    return p * inv, e_sink * inv


def _attn_valid(bucket_ref, n):
    col = lax.broadcasted_iota(jnp.int32, (BLOCK, 2 * BLOCK), 1)
    return (bucket_ref[...] >= 0) & ((n > 0) | (col >= BLOCK))


def _attn_fwd(q, k, v, bucket, rel_bias, sinks, carry=None):
    T = q.shape[0]
    nb = T // BLOCK

    def body(q_ref, kc_ref, kp_ref, vc_ref, vp_ref, bucket_ref, rb_ref, sk_ref, o_ref, bias_ref, s_ref, p_ref):
        n = pl.program_id(0)

        @pl.when(n == 0)
        def _():
            _build_bias(bucket_ref, rb_ref, bias_ref)

        kvar = _kv_variants(kp_ref, kc_ref)
        vvar = _kv_variants(vp_ref, vc_ref)
        pr, _ = _attn_probs(q_ref, kvar, bias_ref, sk_ref, _attn_valid(bucket_ref, n), s_ref)
        p_ref[...] = pr.astype(BF16)
        for m in range(N_HEADS // 2):
            acc = _dot(p_ref[2 * m], vvar[m // 2][0]) + _dot(p_ref[2 * m + 1], vvar[m // 2][1])
            o_ref[:, m * LANES:(m + 1) * LANES] = acc.astype(o_ref.dtype)

    cur = lambda w: pl.BlockSpec((BLOCK, w), lambda n: (n, 0))
    prev = lambda w: pl.BlockSpec((BLOCK, w), lambda n: (jnp.maximum(n - 1, 0), 0))
    smem = pl.BlockSpec(memory_space=pltpu.SMEM)
    return _hosted_call(
        body, carry, _edge_1d(nb, pass_on_last=True), name="attn_fwd", grid=(nb,),
        in_specs=[cur(ATTN_W), cur(KV_W), prev(KV_W), cur(KV_W), prev(KV_W), _const_spec((BLOCK, 2 * BLOCK)), smem, smem],
        out_specs=[cur(ATTN_W)],
        out_shape=[jax.ShapeDtypeStruct((T, ATTN_W), BF16)],
        scratch_shapes=[pltpu.VMEM((N_HEADS, BLOCK, 2 * BLOCK), F32), pltpu.VMEM((N_HEADS, BLOCK, 2 * BLOCK), F32),
                        pltpu.VMEM((N_HEADS, BLOCK, 2 * BLOCK), BF16)],
        compiler_params=_params(("arbitrary",)), inputs=(q, k, k, v, v, bucket, rel_bias, sinks))


ATTN_SMALL_ROWS = N_BUCKETS + SUBLANES


def _attn_bwd(q, k, v, datt, bucket, rel_bias, sinks, carry=None):
    T = q.shape[0]
    nb = T // BLOCK

    def body(q_ref, do_ref, kc_ref, kp_ref, vc_ref, vp_ref, bucket_ref, rb_ref, sk_ref,
             dq_ref, dkv_ref, small_ref, bias_ref, ds_sum_ref, dsink_ref, kcarry_ref, vcarry_ref,
             s_ref, dp_ref, p_ref, dsc_ref):
        n = pl.program_id(0)

        @pl.when(n == 0)
        def _():
            _build_bias(bucket_ref, rb_ref, bias_ref)
            ds_sum_ref[...] = jnp.zeros_like(ds_sum_ref)
            dsink_ref[...] = jnp.zeros_like(dsink_ref)
            kcarry_ref[...] = jnp.zeros_like(kcarry_ref)
            vcarry_ref[...] = jnp.zeros_like(vcarry_ref)

        @pl.when(n < nb)
        def _():
            kvar = _kv_variants(kp_ref, kc_ref)
            vvar = _kv_variants(vp_ref, vc_ref)
            pr, p_sink = _attn_probs(q_ref, kvar, bias_ref, sk_ref, _attn_valid(bucket_ref, n), s_ref)
            for h in range(N_HEADS):
                dp_ref[h] = _dot_nt(do_ref[:, _head_lanes(h)], vvar[h // 4][h % 2])
            dp = dp_ref[...]
            dsum = jnp.sum(pr * dp, axis=-1, keepdims=True)
            ds = pr * (dp - dsum)
            ds_sum_ref[...] += ds
            dsink_ref[...] -= jnp.sum(p_sink * dsum, axis=1, keepdims=True)
            dsc_ref[...] = (ds * (HEAD_DIM ** -0.5)).astype(BF16)
            p_ref[...] = pr.astype(BF16)
            for m in range(N_HEADS // 2):
                dqm = _dot(dsc_ref[2 * m], kvar[m // 2][0]) + _dot(dsc_ref[2 * m + 1], kvar[m // 2][1])
                dq_ref[:, m * LANES:(m + 1) * LANES] = dqm.astype(dq_ref.dtype)
            dk_var = [[None, None], [None, None]]
            dv_var = [[None, None], [None, None]]
            for kvh in range(2):
                for e in range(2):
                    heads = [h for h in range(N_HEADS) if h // 4 == kvh and h % 2 == e]
                    dk_var[kvh][e] = sum(_dot_tn(dsc_ref[h], q_ref[:, _head_lanes(h)]) for h in heads)
                    dv_var[kvh][e] = sum(_dot_tn(p_ref[h], do_ref[:, _head_lanes(h)]) for h in heads)
            dk_cat = _merge_kv_grads(dk_var)
            dv_cat = _merge_kv_grads(dv_var)

            @pl.when(n > 0)
            def _():
                dkv_ref[:, :KV_W] = (kcarry_ref[...] + dk_cat[:BLOCK]).astype(BF16)
                dkv_ref[:, KV_W:] = (vcarry_ref[...] + dv_cat[:BLOCK]).astype(BF16)

            kcarry_ref[...] = dk_cat[BLOCK:]
            vcarry_ref[...] = dv_cat[BLOCK:]

        @pl.when(n == nb)
        def _():
            dkv_ref[:, :KV_W] = kcarry_ref[...].astype(BF16)
            dkv_ref[:, KV_W:] = vcarry_ref[...].astype(BF16)
            bk = bucket_ref[...]
            row = lax.broadcasted_iota(jnp.int32, (N_HEADS, ATTN_SMALL_ROWS, LANES), 1)

            def add(b, acc):
                masked = jnp.where((bk == b)[None], ds_sum_ref[...], 0.0)
                val = jnp.sum(jnp.sum(masked, axis=1, keepdims=True), axis=2, keepdims=True)
                return acc + jnp.where(row == b, val, 0.0)

            small_ref[...] = lax.fori_loop(0, N_BUCKETS, add, jnp.where(row == N_BUCKETS, dsink_ref[...], 0.0))

    last = nb - 1
    cur = lambda w: pl.BlockSpec((BLOCK, w), lambda n: (jnp.minimum(n, last), 0))
    prev = lambda w: pl.BlockSpec((BLOCK, w), lambda n: (jnp.clip(n - 1, 0, last), 0))
    smem = pl.BlockSpec(memory_space=pltpu.SMEM)
    return _hosted_call(
        body, carry, _edge_1d(nb + 1), name="attn_bwd", grid=(nb + 1,),
        in_specs=[cur(ATTN_W), cur(ATTN_W), cur(KV_W), prev(KV_W), cur(KV_W), prev(KV_W),
                  _const_spec((BLOCK, 2 * BLOCK)), smem, smem],
        out_specs=[cur(ATTN_W), prev(2 * KV_W), pl.BlockSpec((N_HEADS, ATTN_SMALL_ROWS, LANES), lambda n: (0, 0, 0))],
        out_shape=[jax.ShapeDtypeStruct((T, ATTN_W), BF16), jax.ShapeDtypeStruct((T, 2 * KV_W), BF16),
                   jax.ShapeDtypeStruct((N_HEADS, ATTN_SMALL_ROWS, LANES), F32)],
        scratch_shapes=[pltpu.VMEM((N_HEADS, BLOCK, 2 * BLOCK), F32), pltpu.VMEM((N_HEADS, BLOCK, 2 * BLOCK), F32),
                        pltpu.VMEM((N_HEADS, 1, 1), F32), pltpu.VMEM((BLOCK, KV_W), F32), pltpu.VMEM((BLOCK, KV_W), F32),
                        pltpu.VMEM((N_HEADS, BLOCK, 2 * BLOCK), F32), pltpu.VMEM((N_HEADS, BLOCK, 2 * BLOCK), F32),
                        pltpu.VMEM((N_HEADS, BLOCK, 2 * BLOCK), BF16), pltpu.VMEM((N_HEADS, BLOCK, 2 * BLOCK), BF16)],
        compiler_params=_params(("arbitrary",)), inputs=(q, datt, k, k, v, v, bucket, rel_bias, sinks))


SCAN_UNROLL = 4


def _cmul(ar, ai, br, bi):
    return ar * br - ai * bi, ar * bi + ai * br


def _cmul_conj(ar, ai, br, bi):
    return ar * br + ai * bi, ar * bi - ai * br


def _ssm_discretize(lr, li, ldt):
    dt = jnp.exp(ldt)
    mag = jnp.exp(lr * dt)
    ab_re = mag * jnp.cos(li * dt)
    ab_im = mag * jnp.sin(li * dt)
    nr = ab_re - 1.0
    den = lr * lr + li * li
    f_re = (nr * lr + ab_im * li) / den
    f_im = (ab_im * lr - nr * li) / den
    return ab_re, ab_im, f_re, f_im


def _ssm_prep(lam_re, lam_im, ldt_rep, bd_re, bd_im):
    def body(lr_ref, li_ref, ldt_ref, bdr_ref, bdi_ref, ar_ref, ai_ref, br_ref, bi_ref):
        ab_re, ab_im, f_re, f_im = _ssm_discretize(lr_ref[...], li_ref[...], ldt_ref[...])
        ar_ref[...] = ab_re
        ai_ref[...] = ab_im
        bdr, bdi = bdr_ref[0], bdi_ref[0]
        br_ref[0] = (bdr * f_re - bdi * f_im).astype(BF16)
        bi_ref[0] = (bdi * f_re + bdr * f_im).astype(BF16)

    row = pl.BlockSpec((1, SSM_LANE_BLOCK), lambda j: (0, j))
    mat = pl.BlockSpec((1, LANES, SSM_LANE_BLOCK), lambda j: (j, 0, 0))
    return pl.pallas_call(
        body, name="ssm_prep", grid=(N_SSM_BLOCKS,),
        in_specs=[row, row, row, mat, mat], out_specs=[row, row, mat, mat],
        out_shape=[jax.ShapeDtypeStruct((1, STATES), F32)] * 2 + [jax.ShapeDtypeStruct((N_SSM_BLOCKS, LANES, SSM_LANE_BLOCK), BF16)] * 2,
        compiler_params=_params(("arbitrary",)),
    )(*_in_hbm(lam_re, lam_im, ldt_rep, bd_re, bd_im))


def _ssm_prep_bwd(lam_re, lam_im, ldt_rep, bd_re, bd_im, dbr, dbi, da_re, da_im):
    def body(lr_ref, li_ref, ldt_ref, bdr_ref, bdi_ref, dbr_ref, dbi_ref, dar_ref, dai_ref,
             dbdr_ref, dbdi_ref, dlr_ref, dli_ref, dldt_ref):
        lr, li, ldt = lr_ref[...], li_ref[...], ldt_ref[...]
        (_, _, f_re, f_im), vjp = jax.vjp(_ssm_discretize, lr, li, ldt)
        bdr, bdi, gbr, gbi = bdr_ref[0], bdi_ref[0], dbr_ref[0], dbi_ref[0]
        dbdr_ref[0] = gbr * f_re + gbi * f_im
        dbdi_ref[0] = gbi * f_re - gbr * f_im
        df_re = jnp.sum(gbr * bdr + gbi * bdi, axis=0, keepdims=True)
        df_im = jnp.sum(gbi * bdr - gbr * bdi, axis=0, keepdims=True)
        dlr, dli, dldt = vjp((dar_ref[...], dai_ref[...], df_re, df_im))
        dlr_ref[...] = dlr
        dli_ref[...] = dli
        dldt_ref[...] = dldt

    row = pl.BlockSpec((1, SSM_LANE_BLOCK), lambda j: (0, j))
    mat = pl.BlockSpec((1, LANES, SSM_LANE_BLOCK), lambda j: (j, 0, 0))
    mat_shape = jax.ShapeDtypeStruct((N_SSM_BLOCKS, LANES, SSM_LANE_BLOCK), F32)
    row_shape = jax.ShapeDtypeStruct((1, STATES), F32)
    return pl.pallas_call(
        body, name="ssm_prep_bwd", grid=(N_SSM_BLOCKS,),
        in_specs=[row, row, row, mat, mat, mat, mat, row, row], out_specs=[mat, mat, row, row, row],
        out_shape=[mat_shape, mat_shape, row_shape, row_shape, row_shape],
        compiler_params=_params(("arbitrary",)),
    )(*_in_hbm(lam_re, lam_im, ldt_rep, bd_re, bd_im, dbr, dbi, da_re, da_im))


def _group_sum(x):
    def body(x_ref, o_ref):
        o_ref[...] = jnp.sum(x_ref[...], axis=1, keepdims=True)
    return pl.pallas_call(body, name="ssm_group_sum", grid=(1,), in_specs=[_whole(x.shape)], out_specs=_whole((N_GROUPS, 1)),
                          out_shape=jax.ShapeDtypeStruct((N_GROUPS, 1), F32))(*_in_hbm(x))


def _power_table(ar, ai, p_re_ref, p_im_ref, steps):
    shape = (SUBLANES, SSM_LANE_BLOCK)
    p_re_ref[0:SUBLANES] = jnp.broadcast_to(ar, shape)
    p_im_ref[0:SUBLANES] = jnp.broadcast_to(ai, shape)
    m = 1
    while m < steps:
        rows = m * SUBLANES
        top_re = p_re_ref[rows - SUBLANES:rows]
        top_im = p_im_ref[rows - SUBLANES:rows]
        cur_re = p_re_ref[0:rows].reshape(m, SUBLANES, SSM_LANE_BLOCK)
        cur_im = p_im_ref[0:rows].reshape(m, SUBLANES, SSM_LANE_BLOCK)
        nxt_re, nxt_im = _cmul(cur_re, cur_im, top_re[None], top_im[None])
        p_re_ref[rows:2 * rows] = nxt_re.reshape(rows, SSM_LANE_BLOCK)
        p_im_ref[rows:2 * rows] = nxt_im.reshape(rows, SSM_LANE_BLOCK)
        m *= 2


def _to_segments(src_ref, dst_ref, steps):
    for s in range(SUBLANES):
        dst_ref[pl.ds(s, steps, stride=SUBLANES), :] = src_ref[s * steps:(s + 1) * steps, :]


def _from_segments(src_ref, dst_ref, steps):
    for s in range(SUBLANES):
        dst_ref[s * steps:(s + 1) * steps, :] = src_ref[pl.ds(s, steps, stride=SUBLANES), :]


def _segment_carries(e_re, e_im, an_re, an_im, c_re, c_im, reverse):
    order = range(SUBLANES - 1, -1, -1) if reverse else range(SUBLANES)
    ins_re, ins_im = [None] * SUBLANES, [None] * SUBLANES
    for s in order:
        ins_re[s], ins_im[s] = c_re, c_im
        pr, pi = _cmul(an_re, an_im, c_re, c_im)
        c_re = e_re[s:s + 1] + pr
        c_im = e_im[s:s + 1] + pi
    return jnp.concatenate(ins_re, axis=0), jnp.concatenate(ins_im, axis=0), c_re, c_im


def _ssm_fwd(u, a_re, a_im, b_re, b_im, c_re, c_im, d_skip, chunk, carry=None):
    T = u.shape[0]
    nc = T // chunk
    steps = chunk // SUBLANES
    blk = SSM_LANE_BLOCK

    def body(u_ref, ar_ref, ai_ref, br_ref, bi_ref, cr_ref, ci_ref, dk_ref,
             y_ref, hr_ref, hi_ref, inr_ref, ini_ref, useg_ref, yseg_ref, pr_ref, pi_ref, carry_ref):
        c = pl.program_id(1)
        ar, ai = ar_ref[...], ai_ref[...]

        @pl.when(c == 0)
        def _():
            _power_table(ar, ai, pr_ref, pi_ref, steps)
            carry_ref[...] = jnp.zeros_like(carry_ref)

        _to_segments(u_ref, useg_ref, steps)
        ub = useg_ref[...].astype(BF16)
        hr_ref[...] = _dot(ub, br_ref[0])
        hi_ref[...] = _dot(ub, bi_ref[0])
        first = slice(0, SUBLANES)

        def scan(t4, prev):
            for j in range(SCAN_UNROLL):
                rows = pl.ds(pl.multiple_of((t4 * SCAN_UNROLL + j) * SUBLANES, SUBLANES), SUBLANES)
                pr, pi = _cmul(pr_ref[first, :], pi_ref[first, :], prev[0], prev[1])
                prev = (pr + hr_ref[rows, :], pi + hi_ref[rows, :])
                hr_ref[rows, :] = prev[0]
                hi_ref[rows, :] = prev[1]
            return prev

        zero = jnp.zeros((SUBLANES, blk), F32)
        lax.fori_loop(0, steps // SCAN_UNROLL, scan, (zero, zero))

        top = slice(chunk - SUBLANES, chunk)
        in_re, in_im, out_re, out_im = _segment_carries(
            hr_ref[top, :], hi_ref[top, :], pr_ref[top, :][0:1], pi_ref[top, :][0:1],
            carry_ref[0:1, :], carry_ref[1:2, :], reverse=False)
        carry_ref[0:1, :] = out_re
        carry_ref[1:2, :] = out_im
        inr_ref[...] = in_re
        ini_ref[...] = in_im

        def fix(t4, _):
            for j in range(SCAN_UNROLL):
                rows = pl.ds(pl.multiple_of((t4 * SCAN_UNROLL + j) * SUBLANES, SUBLANES), SUBLANES)
                fr, fi = _cmul(pr_ref[rows, :], pi_ref[rows, :], in_re, in_im)
                hr_ref[rows, :] += fr
                hi_ref[rows, :] += fi
            return 0

        lax.fori_loop(0, steps // SCAN_UNROLL, fix, 0)

        yseg_ref[...] = _dot(hr_ref[...].astype(BF16), cr_ref[0]) - _dot(hi_ref[...].astype(BF16), ci_ref[0])
        _from_segments(yseg_ref, y_ref, steps)
        y_ref[...] += dk_ref[...] * u_ref[...]

    row = pl.BlockSpec((1, blk), lambda j, c: (0, j))
    b_mat = pl.BlockSpec((1, LANES, blk), lambda j, c: (j, 0, 0))
    c_mat = pl.BlockSpec((1, blk, LANES), lambda j, c: (j, 0, 0))
    tok = pl.BlockSpec((chunk, LANES), lambda j, c: (c, j))
    state = pl.BlockSpec((chunk, blk), lambda j, c: (c, j))
    enter = pl.BlockSpec((SUBLANES, blk), lambda j, c: (c, j))
    return _hosted_call(
        body, carry, _edge_2d(N_SSM_BLOCKS, nc), name="ssm_fwd", grid=(N_SSM_BLOCKS, nc),
        in_specs=[tok, row, row, b_mat, b_mat, c_mat, c_mat, pl.BlockSpec((1, LANES), lambda j, c: (0, j))],
        out_specs=[tok, state, state, enter, enter],
        out_shape=[jax.ShapeDtypeStruct((T, SSM_W), F32), jax.ShapeDtypeStruct((T, STATES), F32),
                   jax.ShapeDtypeStruct((T, STATES), F32), jax.ShapeDtypeStruct((nc * SUBLANES, STATES), F32),
                   jax.ShapeDtypeStruct((nc * SUBLANES, STATES), F32)],
        scratch_shapes=[pltpu.VMEM((chunk, LANES), F32), pltpu.VMEM((chunk, LANES), F32),
                        pltpu.VMEM((chunk, blk), F32), pltpu.VMEM((chunk, blk), F32), pltpu.VMEM((SUBLANES, blk), F32)],
        compiler_params=_params(("arbitrary", "arbitrary"), VMEM_MID),
        inputs=(u, a_re, a_im, b_re, b_im, c_re, c_im, d_skip))


def _ssm_bwd(dy, u, h_re, h_im, in_re, in_im, a_re, a_im, b_re, b_im, c_re, c_im, d_skip, chunk, carry=None):
    T = u.shape[0]
    nc = T // chunk
    steps = chunk // SUBLANES
    blk = SSM_LANE_BLOCK

    def body(dy_ref, u_ref, hr_ref, hi_ref, inr_ref, ini_ref, ar_ref, ai_ref, br_ref, bi_ref, cr_ref, ci_ref, dk_ref,
             du_ref, dbr_ref, dbi_ref, dcr_ref, dci_ref, dar_ref, dai_ref, ddk_ref,
             dyseg_ref, useg_ref, duseg_ref, gr_ref, gi_ref, pr_ref, pi_ref, carry_ref, accr_ref, acci_ref):
        c = pl.program_id(1)
        ar, ai = ar_ref[...], ai_ref[...]

        @pl.when(c == 0)
        def _():
            _power_table(ar, ai, pr_ref, pi_ref, steps)
            carry_ref[...] = jnp.zeros_like(carry_ref)
            accr_ref[...] = jnp.zeros_like(accr_ref)
            acci_ref[...] = jnp.zeros_like(acci_ref)

        _to_segments(dy_ref, dyseg_ref, steps)
        _to_segments(u_ref, useg_ref, steps)
        dyb = dyseg_ref[...].astype(BF16)
        ub = useg_ref[...].astype(BF16)
        gr_ref[...] = _dot_nt(dyb, cr_ref[0])
        gi_ref[...] = -_dot_nt(dyb, ci_ref[0])
        dcr = _dot_tn(hr_ref[...].astype(BF16), dyb)
        dci = -_dot_tn(hi_ref[...].astype(BF16), dyb)
        ddk = jnp.sum(dy_ref[...] * u_ref[...], axis=0, keepdims=True)

        first = slice(0, SUBLANES)

        def scan(k4, nxt):
            for j in range(SCAN_UNROLL):
                t = steps - 1 - (k4 * SCAN_UNROLL + j)
                rows = pl.ds(pl.multiple_of(t * SUBLANES, SUBLANES), SUBLANES)
                pr, pi = _cmul_conj(pr_ref[first, :], pi_ref[first, :], nxt[0], nxt[1])
                nxt = (pr + gr_ref[rows, :], pi + gi_ref[rows, :])
                gr_ref[rows, :] = nxt[0]
                gi_ref[rows, :] = nxt[1]
            return nxt

        top = slice(chunk - SUBLANES, chunk)
        zero = jnp.zeros((SUBLANES, blk), F32)
        lax.fori_loop(0, steps // SCAN_UNROLL, scan, (zero, zero))

        gin_re, gin_im, out_re, out_im = _segment_carries(
            gr_ref[0:SUBLANES, :], gi_ref[0:SUBLANES, :], pr_ref[top, :][0:1], -pi_ref[top, :][0:1],
            carry_ref[0:1, :], carry_ref[1:2, :], reverse=True)
        carry_ref[0:1, :] = out_re
        carry_ref[1:2, :] = out_im

        def fix_row(rows, prow, hp_re, hp_im, acc):
            fr, fi = _cmul_conj(pr_ref[prow, :], pi_ref[prow, :], gin_re, gin_im)
            g_re = gr_ref[rows, :] + fr
            g_im = gi_ref[rows, :] + fi
            gr_ref[rows, :] = g_re
            gi_ref[rows, :] = g_im
            return acc[0] + g_re * hp_re + g_im * hp_im, acc[1] + g_im * hp_re - g_re * hp_im

        def fix_at(t, acc):
            aligned = (lambda r: r * SUBLANES) if isinstance(t, int) else (lambda r: pl.multiple_of(r * SUBLANES, SUBLANES))
            rows, before, prow = (pl.ds(aligned(r), SUBLANES) for r in (t, t - 1, steps - 1 - t))
            return fix_row(rows, prow, hr_ref[before, :], hi_ref[before, :], acc)

        def fix(t4, acc):
            for j in range(SCAN_UNROLL):
                acc = fix_at(t4 * SCAN_UNROLL + j, acc)
            return acc

        acc = fix_row(first, top, inr_ref[...], ini_ref[...], (accr_ref[...], acci_ref[...]))
        for t in range(1, SCAN_UNROLL):
            acc = fix_at(t, acc)
        acc_re, acc_im = lax.fori_loop(1, steps // SCAN_UNROLL, fix, acc)
        accr_ref[...] = acc_re
        acci_ref[...] = acc_im

        gbr = gr_ref[...].astype(BF16)
        gbi = gi_ref[...].astype(BF16)
        duseg_ref[...] = _dot_nt(gbr, br_ref[0]) + _dot_nt(gbi, bi_ref[0])
        _from_segments(duseg_ref, dyseg_ref, steps)
        du_ref[...] = (dyseg_ref[...] + dk_ref[...] * dy_ref[...]).astype(BF16)
        dbr = _dot_tn(ub, gbr)
        dbi = _dot_tn(ub, gbi)

        @pl.when(c == 0)
        def _():
            dbr_ref[0] = dbr
            dbi_ref[0] = dbi
            dcr_ref[0] = dcr
            dci_ref[0] = dci
            ddk_ref[...] = ddk

        @pl.when(c > 0)
        def _():
            dbr_ref[0] += dbr
            dbi_ref[0] += dbi
            dcr_ref[0] += dcr
            dci_ref[0] += dci
            ddk_ref[...] += ddk

        @pl.when(c == nc - 1)
        def _():
            dar_ref[...] = jnp.sum(acc_re, axis=0, keepdims=True)
            dai_ref[...] = jnp.sum(acc_im, axis=0, keepdims=True)

    rev = lambda c: nc - 1 - c
    row = pl.BlockSpec((1, blk), lambda j, c: (0, j))
    b_mat = pl.BlockSpec((1, LANES, blk), lambda j, c: (j, 0, 0))
    c_mat = pl.BlockSpec((1, blk, LANES), lambda j, c: (j, 0, 0))
    tok = pl.BlockSpec((chunk, LANES), lambda j, c: (rev(c), j))
    state = pl.BlockSpec((chunk, blk), lambda j, c: (rev(c), j))
    enter = pl.BlockSpec((SUBLANES, blk), lambda j, c: (rev(c), j))
    chan = pl.BlockSpec((1, LANES), lambda j, c: (0, j))
    f32 = lambda *s: jax.ShapeDtypeStruct(s, F32)
    return _hosted_call(
        body, carry, _edge_2d(N_SSM_BLOCKS, nc), name="ssm_bwd", grid=(N_SSM_BLOCKS, nc),
        in_specs=[tok, tok, state, state, enter, enter, row, row, b_mat, b_mat, c_mat, c_mat, chan],
        out_specs=[tok, b_mat, b_mat, c_mat, c_mat, row, row, chan],
        out_shape=[jax.ShapeDtypeStruct((T, SSM_W), BF16), f32(N_SSM_BLOCKS, LANES, blk), f32(N_SSM_BLOCKS, LANES, blk),
                   f32(N_SSM_BLOCKS, blk, LANES), f32(N_SSM_BLOCKS, blk, LANES), f32(1, STATES), f32(1, STATES), f32(1, SSM_W)],
        scratch_shapes=[pltpu.VMEM((chunk, LANES), F32), pltpu.VMEM((chunk, LANES), F32), pltpu.VMEM((chunk, LANES), F32),
                        pltpu.VMEM((chunk, blk), F32), pltpu.VMEM((chunk, blk), F32),
                        pltpu.VMEM((chunk, blk), F32), pltpu.VMEM((chunk, blk), F32),
                        pltpu.VMEM((SUBLANES, blk), F32), pltpu.VMEM((SUBLANES, blk), F32), pltpu.VMEM((SUBLANES, blk), F32)],
        compiler_params=_params(("arbitrary", "arbitrary"), VMEM_BIG),
        inputs=(dy, u, h_re, h_im, in_re, in_im, a_re, a_im, b_re, b_im, c_re, c_im, d_skip))


def _merge_forward(y, att, ga, gs, w_glu, w_ssm, w_attn):
    z = jax.nn.gelu(y)
    zb = z.astype(BF16)
    gl = jax.nn.sigmoid(_dot(zb, w_glu))
    z2b = (z * gl).astype(BF16)
    y_ssm = _dot(z2b, w_ssm)
    y_attn = _dot(att, w_attn)
    sa = jax.nn.sigmoid(ga)
    ss = jax.nn.sigmoid(gs)
    merged = (sa * y_attn + ss * y_ssm).astype(BF16)
    return z, zb, gl, z2b, y_ssm, y_attn, sa, ss, merged


def _merge_fwd(x, y, att, ga, gs, g2, g3, w_glu, w_ssm, w_attn, w_out, tile):
    T = x.shape[0]

    def body(x_hbm, y_hbm, att_hbm, ga_hbm, gs_hbm, g2_ref, g3_ref, wg_ref, ws_ref, wa_ref, wo_ref, x1_hbm, o_hbm, h2_hbm):
        def step(x_ref, y_ref, att_ref, ga_ref, gs_ref, x1_ref, o_ref, h2_ref):
            merged = _merge_forward(y_ref[...], att_ref[...], ga_ref[...], gs_ref[...], wg_ref[...], ws_ref[...], wa_ref[...])[-1]
            o = _dot(merged, wo_ref[...])
            x1 = x_ref[...] + o * _rms_scale(o) * g2_ref[...]
            o_ref[...] = o
            x1_ref[...] = x1
            h2_ref[...] = (x1 * _rms_scale(x1) * g3_ref[...]).astype(BF16)

        tok = lambda w, **mode: pl.BlockSpec((tile, w), lambda i: (i, 0), **mode)
        tok_in = lambda w: tok(w, pipeline_mode=pl.Buffered(3))
        pltpu.emit_pipeline(
            step, grid=(T // tile,),
            in_specs=[tok_in(D_MODEL), tok_in(SSM_W), tok_in(ATTN_W), tok_in(D_MODEL), tok_in(D_MODEL)],
            out_specs=[tok(D_MODEL), tok(D_MODEL), tok(D_MODEL)],
        )(x_hbm, y_hbm, att_hbm, ga_hbm, gs_hbm, x1_hbm, o_hbm, h2_hbm)

    in_vmem = pl.BlockSpec(memory_space=pltpu.VMEM)
    return pl.pallas_call(
        body, name="merge_fwd", in_specs=[HBM_SPEC] * 5 + [in_vmem] * 6, out_specs=[HBM_SPEC] * 3,
        out_shape=_hbm_out([jax.ShapeDtypeStruct((T, D_MODEL), F32), jax.ShapeDtypeStruct((T, D_MODEL), F32),
                            jax.ShapeDtypeStruct((T, D_MODEL), BF16)]),
        compiler_params=pltpu.CompilerParams(vmem_limit_bytes=VMEM_BIG),
    )(*_in_hbm(x, y, att, ga, gs), g2, g3, w_glu, w_ssm, w_attn, w_out)


def _merge_bwd(dh2, dx2, x1, o, y, att, ga, gs, g2, g3, w_glu, w_ssm, w_attn, w_out, tile, carry=None):
    T = x1.shape[0]
    n_steps = T // tile

    group = min(2, n_steps)
    staged_widths = (D_MODEL, D_MODEL, ATTN_W, D_MODEL, SSM_W, D_MODEL, SSM_W, SSM_W)

    def body(dh2_ref, dx2_ref, x1_ref, o_ref, y_ref, att_ref, ga_ref, gs_ref, g2_ref, g3_ref, wg_ref, ws_ref, wa_ref, wo_ref,
             dx1_ref, dgates_ref, datt_ref, dy_ref, dwg_hbm, dws_hbm, dwa_hbm, dwo_hbm, dg2_ref, dg3_ref,
             awg_ref, aws_ref, awa_ref, awo_ref, *staged):
        i = pl.program_id(0)
        x1v, ov = x1_ref[...], o_ref[...]
        dxn, dg3 = _rms_bwd(dh2_ref[...], x1v, _rms_scale(x1v), g3_ref[...])
        dx1 = dx2_ref[...] + dxn
        dx1_ref[...] = dx1
        do, dg2 = _rms_bwd(dx1, ov, _rms_scale(ov), g2_ref[...])
        dob = do.astype(BF16)

        yv = y_ref[...]
        att = att_ref[...]
        z, zb, gl, z2b, y_ssm, y_attn, sa, ss, merged = _merge_forward(
            yv, att, ga_ref[...], gs_ref[...], wg_ref[...], ws_ref[...], wa_ref[...])
        dmerged = _dot_nt(dob, wo_ref[...])
        dya = (dmerged * sa).astype(BF16)
        dys = (dmerged * ss).astype(BF16)
        dgates_ref[:, :D_MODEL] = (dmerged * y_attn * sa * (1.0 - sa)).astype(BF16)
        dgates_ref[:, D_MODEL:] = (dmerged * y_ssm * ss * (1.0 - ss)).astype(BF16)
        datt_ref[...] = _dot_nt(dya, wa_ref[...]).astype(BF16)
        dz2 = _dot_nt(dys, ws_ref[...])
        dpre = (dz2 * z * gl * (1.0 - gl)).astype(BF16)
        dz = dz2 * gl + _dot_nt(dpre, wg_ref[...])
        _, gelu_vjp = jax.vjp(jax.nn.gelu, yv)
        dy_ref[...] = gelu_vjp(dz)[0]

        part = pl.ds(pl.multiple_of((i % group) * tile, tile), tile)
        for ref, val in zip(staged, (merged, dob, att, dya, z2b, dys, zb, dpre)):
            ref[part, :] = val

        @pl.when(i == 0)
        def _():
            dg2_ref[...] = dg2
            dg3_ref[...] = dg3

        @pl.when(i > 0)
        def _():
            dg2_ref[...] += dg2
            dg3_ref[...] += dg3

        def weight_grads():
            s_merged, s_dob, s_att, s_dya, s_z2b, s_dys, s_zb, s_dpre = (ref[...] for ref in staged)
            return ((awo_ref, _dot_tn(s_merged, s_dob)), (awa_ref, _dot_tn(s_att, s_dya)),
                    (aws_ref, _dot_tn(s_z2b, s_dys)), (awg_ref, _dot_tn(s_zb, s_dpre)))

        @pl.when(i == group - 1)
        def _():
            for ref, val in weight_grads():
                ref[...] = val

        @pl.when((i % group == group - 1) & (i > group - 1))
        def _():
            for ref, val in weight_grads():
                ref[...] += val

        @pl.when(i == n_steps - 1)
        def _():
            pltpu.sync_copy(awg_ref, dwg_hbm)
            pltpu.sync_copy(aws_ref, dws_hbm)
            pltpu.sync_copy(awa_ref, dwa_hbm)
            pltpu.sync_copy(awo_ref, dwo_hbm)

    tok = lambda w: pl.BlockSpec((tile, w), lambda i: (i, 0))
    vec = _const_spec((1, D_MODEL))
    any_ = pl.BlockSpec(memory_space=pl.ANY)
    vec_out = pl.BlockSpec((1, D_MODEL), lambda i: (0, 0))
    f32 = lambda *s: jax.ShapeDtypeStruct(s, F32)
    bf = lambda *s: jax.ShapeDtypeStruct(s, BF16)
    return _hosted_call(
        body, carry, _edge_1d(n_steps), name="merge_bwd", grid=(n_steps,),
        in_specs=[tok(D_MODEL), tok(D_MODEL), tok(D_MODEL), tok(D_MODEL), tok(SSM_W), tok(ATTN_W), tok(D_MODEL), tok(D_MODEL),
                  vec, vec, _const_spec((SSM_W, SSM_W)), _const_spec((SSM_W, D_MODEL)), _const_spec((ATTN_W, D_MODEL)),
                  _const_spec((D_MODEL, D_MODEL))],
        out_specs=[tok(D_MODEL), tok(2 * D_MODEL), tok(ATTN_W), tok(SSM_W), any_, any_, any_, any_, vec_out, vec_out],
        out_shape=[f32(T, D_MODEL), bf(T, 2 * D_MODEL), bf(T, ATTN_W), f32(T, SSM_W),
                   f32(SSM_W, SSM_W), f32(SSM_W, D_MODEL), f32(ATTN_W, D_MODEL), f32(D_MODEL, D_MODEL),
                   f32(1, D_MODEL), f32(1, D_MODEL)],
        scratch_shapes=[pltpu.VMEM((SSM_W, SSM_W), F32), pltpu.VMEM((SSM_W, D_MODEL), F32),
                        pltpu.VMEM((ATTN_W, D_MODEL), F32), pltpu.VMEM((D_MODEL, D_MODEL), F32)]
        + [pltpu.VMEM((group * tile, wd), BF16) for wd in staged_widths],
        compiler_params=_params(("arbitrary",), VMEM_BIG),
        inputs=(dh2, dx2, x1, o, y, att, ga, gs, g2, g3, w_glu, w_ssm, w_attn, w_out))


FF_SHARD = D_FF // N_DEV


def _mlp_fwd(h2, x1, target, g4, w_ff_in, w_ff_out, tile):
    T = h2.shape[0]
    col_chunk = 2 * FF_SHARD

    def body(h2_ref, x1_ref, tg_ref, g4_ref, wi_ref, wo_ref, a_ref, dfo_ref, dx2_ref, loss_ref, dg4_ref, rr_ref):
        i = pl.program_id(0)
        h2v = h2_ref[...]
        for c in range(D_FF // col_chunk):
            cols = slice(c * col_chunk, (c + 1) * col_chunk)
            a = _dot_nt(h2v, wi_ref[cols, :])
            a_ref[:, cols] = a.astype(BF16)
            ra = jnp.maximum(a, 0.0)
            rr_ref[:, cols] = (ra * ra).astype(BF16)
        f = _dot(rr_ref[...], wo_ref[...])
        r = _rms_scale(f)
        g = g4_ref[...]
        err = x1_ref[...] + f * r * g - tg_ref[...]
        dx2 = err * (1.0 / D_MODEL)
        dx2_ref[...] = dx2
        dfo, dg = _rms_bwd(dx2, f, r, g)
        dfo_ref[...] = dfo.astype(BF16)
        row = lax.broadcasted_iota(jnp.int32, (SUBLANES, LANES), 0)
        col = lax.broadcasted_iota(jnp.int32, (SUBLANES, LANES), 1)
        loss = jnp.where((row == 0) & (col == 0), (0.5 / D_MODEL) * jnp.sum(err * err), 0.0)

        @pl.when(i == 0)
        def _():
            loss_ref[...] = loss
            dg4_ref[...] = dg

        @pl.when(i > 0)
        def _():
            loss_ref[...] += loss
            dg4_ref[...] += dg

    tok = pl.BlockSpec((tile, D_MODEL), lambda i: (i, 0))
    return pl.pallas_call(
        body, name="mlp_fwd", grid=(T // tile,),
        in_specs=[tok, tok, tok, _const_spec((1, D_MODEL)), _const_spec((D_FF, D_MODEL)), _const_spec((D_FF, D_MODEL))],
        out_specs=[pl.BlockSpec((tile, D_FF), lambda i: (i, 0)), tok, tok,
                   pl.BlockSpec((SUBLANES, LANES), lambda i: (0, 0)), pl.BlockSpec((1, D_MODEL), lambda i: (0, 0))],
        out_shape=_hbm_out([jax.ShapeDtypeStruct((T, D_FF), BF16), jax.ShapeDtypeStruct((T, D_MODEL), BF16),
                            jax.ShapeDtypeStruct((T, D_MODEL), F32), jax.ShapeDtypeStruct((SUBLANES, LANES), F32),
                            jax.ShapeDtypeStruct((1, D_MODEL), F32)]),
        scratch_shapes=[pltpu.VMEM((tile, D_FF), BF16)],
        compiler_params=_params(("arbitrary",), VMEM_MAX),
    )(*_in_hbm(h2, x1, target, g4, w_ff_in.reshape(D_FF, D_MODEL), w_ff_out.reshape(D_FF, D_MODEL)))


def _mlp_weight_grads(dfo, a, h2, w_ff_out, row_chunk):
    T = h2.shape[0]

    def body(dfo_ref, h2_ref, a_ref, wo_ref, dwi_ref, dwo_ref, da_ref, rr_ref):
        def rows(r, _):
            sl = pl.ds(pl.multiple_of(r * row_chunk, row_chunk), row_chunk)
            ra = jnp.maximum(a_ref[sl, :].astype(F32), 0.0)
            da_ref[sl, :] = (_dot_nt(dfo_ref[sl, :], wo_ref[0]) * (2.0 * ra)).astype(BF16)
            rr_ref[sl, :] = (ra * ra).astype(BF16)
            return 0

        lax.fori_loop(0, T // row_chunk, rows, 0)
        dwo_ref[0] = _dot_tn(rr_ref[...], dfo_ref[...])
        dwi_ref[0] = _dot_tn(h2_ref[...], da_ref[...])

    return pl.pallas_call(
        body, name="mlp_weight_grads", grid=(N_DEV,),
        in_specs=[_const_spec((T, D_MODEL)), _const_spec((T, D_MODEL)), pl.BlockSpec((T, FF_SHARD), lambda k: (0, k)),
                  pl.BlockSpec((1, FF_SHARD, D_MODEL), lambda k: (k, 0, 0))],
        out_specs=[pl.BlockSpec((1, D_MODEL, FF_SHARD), lambda k: (k, 0, 0)),
                   pl.BlockSpec((1, FF_SHARD, D_MODEL), lambda k: (k, 0, 0)), pl.BlockSpec((T, FF_SHARD), lambda k: (0, k))],
        out_shape=_hbm_out([jax.ShapeDtypeStruct((N_DEV, D_MODEL, FF_SHARD), F32),
                            jax.ShapeDtypeStruct((N_DEV, FF_SHARD, D_MODEL), F32), jax.ShapeDtypeStruct((T, D_FF), BF16)]),
        scratch_shapes=[pltpu.VMEM((T, FF_SHARD), BF16)],
        compiler_params=_params(("arbitrary",), VMEM_MAX),
    )(*_in_hbm(dfo, h2, a, w_ff_out))


def _mlp_input_grad(da, w_ff_in_t, tile):
    T = da.shape[0]

    def body(da_ref, w_ref, o_ref):
        o_ref[...] = _dot(da_ref[...], w_ref[...])

    return pl.pallas_call(
        body, name="mlp_input_grad", grid=(T // tile,),
        in_specs=[pl.BlockSpec((tile, D_FF), lambda i: (i, 0)), _const_spec((D_FF, D_MODEL))],
        out_specs=pl.BlockSpec((tile, D_MODEL), lambda i: (i, 0)),
        out_shape=_hbm_out(jax.ShapeDtypeStruct((T, D_MODEL), F32)),
        compiler_params=_params(("arbitrary",), VMEM_MID),
    )(*_in_hbm(da, w_ff_in_t))


def _block_diag_in(b):
    bt = b.reshape(N_SSM_BLOCKS, GROUPS_PER_BLOCK, GROUP_CH, N_STATE)
    eye = jnp.eye(GROUPS_PER_BLOCK, dtype=b.dtype)
    return jnp.einsum("jacp,ab->jacbp", bt, eye).reshape(N_SSM_BLOCKS, LANES, SSM_LANE_BLOCK)


def _block_diag_in_grad(g):
    g = g.reshape(N_SSM_BLOCKS, GROUPS_PER_BLOCK, GROUP_CH, GROUPS_PER_BLOCK, N_STATE)
    d = jnp.diagonal(g, axis1=1, axis2=3)
    return jnp.transpose(d, (0, 3, 1, 2)).reshape(N_GROUPS, GROUP_CH, N_STATE)


def _block_diag_out(c):
    ct = c.reshape(N_SSM_BLOCKS, GROUPS_PER_BLOCK, GROUP_CH, N_STATE)
    eye = jnp.eye(GROUPS_PER_BLOCK, dtype=c.dtype)
    return jnp.einsum("jacp,ab->japbc", ct, eye).reshape(N_SSM_BLOCKS, SSM_LANE_BLOCK, LANES)


def _block_diag_out_grad(g):
    g = g.reshape(N_SSM_BLOCKS, GROUPS_PER_BLOCK, N_STATE, GROUPS_PER_BLOCK, GROUP_CH)
    d = jnp.diagonal(g, axis1=1, axis2=3)
    return jnp.transpose(d, (0, 3, 2, 1)).reshape(N_GROUPS, GROUP_CH, N_STATE)


def _tiles(T):
    return dict(proj=min(512, T), proj_bwd=min(512, T // 2), merge=min(512, T), merge_bwd=min(256, T),
                mlp_fwd=min(512, T), mlp_bwd=min(512, T), ssm_chunk=min(1024, T))


def _mesh_position():
    x, y, c = lax.axis_index("x"), lax.axis_index("y"), lax.axis_index("c")
    other_chips = [(1 - x, y), (x, 1 - y), (1 - x, 1 - y)]
    return x, y, c, other_chips


def _gather_carry(arrays, pass_on=True):
    n = len(arrays)

    def copies(ins, outs, sems):
        send_sems, recv_sems, local_sems = sems
        x, y, c, chips = _mesh_position()
        me, sibling = (x, y, c), (x, y, 1 - c)

        def copy(a, k, block, to, src=None):
            px, py, pc = block
            dst = outs[a].at[4 * px + 2 * py + pc]
            return pltpu.make_async_remote_copy(
                src_ref=dst if src is None else src, dst_ref=dst, send_sem=send_sems.at[7 * a + k],
                recv_sem=recv_sems.at[7 * a + k], device_id=to, device_id_type=MESH_IDS)

        mine = [pltpu.make_async_copy(ins[a], outs[a].at[4 * x + 2 * y + c], local_sems.at[a]) for a in range(n)]
        first = []
        for a in range(n):
            first.append(copy(a, 0, me, sibling, src=ins[a]))
            first += [copy(a, 1 + j, me, (*chip, c), src=ins[a]) for j, chip in enumerate(chips)]
        return copy, mine, first, me, sibling, chips, c

    def start(ins, outs, sems):
        _, mine, first, *_ = copies(ins, outs, sems)
        for cp in mine + first:
            cp.start()

    def passed_on(copy, sibling, chips, c):
        return [copy(a, 4 + j, (*chip, c), sibling) for a in range(n) for j, chip in enumerate(chips)]

    def middle(ins, outs, sems):
        copy, _, _, me, sibling, chips, c = copies(ins, outs, sems)
        for a in range(n):
            for j, chip in enumerate(chips):
                copy(a, 1 + j, (*chip, c), me).wait_recv()
                copy(a, 4 + j, (*chip, c), sibling).start()

    def finish(ins, outs, sems):
        copy, mine, first, me, sibling, chips, c = copies(ins, outs, sems)
        for a in range(n):
            copy(a, 0, sibling, me).wait_recv()
            for j, chip in enumerate(chips):
                (copy(a, 4 + j, (*chip, 1 - c), me) if pass_on else copy(a, 1 + j, (*chip, c), me)).wait_recv()
        for cp in first + (passed_on(copy, sibling, chips, c) if pass_on else []):
            cp.wait_send()
        for cp in mine:
            cp.wait()

    return _Carry(arrays, [jax.ShapeDtypeStruct((N_DEV,) + a.shape, a.dtype) for a in arrays],
                  [pltpu.SemaphoreType.DMA((7 * n,)), pltpu.SemaphoreType.DMA((7 * n,)), pltpu.SemaphoreType.DMA((n,))],
                  start, finish, middle if pass_on else None)


def _gather_pass_on(gathered, name):
    n = len(gathered)

    def body(*refs):
        zones, send_sems, recv_sems = refs[:n], refs[2 * n], refs[2 * n + 1]
        x, y, c, chips = _mesh_position()
        copies = []
        for a in range(n):
            for j, (px, py) in enumerate(chips):
                block = zones[a].at[4 * px + 2 * py + c]
                copies.append(pltpu.make_async_remote_copy(
                    src_ref=block, dst_ref=block, send_sem=send_sems.at[3 * a + j], recv_sem=recv_sems.at[3 * a + j],
                    device_id=(x, y, 1 - c), device_id_type=MESH_IDS))
        for cp in copies:
            cp.start()
        for cp in copies:
            cp.wait()

    return pl.pallas_call(
        body, name=name, in_specs=[HBM_SPEC] * n, out_specs=[HBM_SPEC] * n,
        out_shape=_hbm_out([jax.ShapeDtypeStruct(g.shape, g.dtype) for g in gathered]),
        scratch_shapes=[pltpu.SemaphoreType.DMA((3 * n,)), pltpu.SemaphoreType.DMA((3 * n,))],
        input_output_aliases={a: a for a in range(n)})(*gathered)


def _pairwise_carry(arrays, n_slots, make_copies):
    n = len(arrays)

    def start(ins, outs, sems):
        for cp in make_copies(ins, outs, sems):
            cp.start()

    def finish(ins, outs, sems):
        for cp in make_copies(ins, outs, sems):
            cp.wait()

    return _Carry(arrays, [jax.ShapeDtypeStruct((n_slots,) + a.shape[1:], a.dtype) for a in arrays],
                  [pltpu.SemaphoreType.DMA((n_slots * n,)), pltpu.SemaphoreType.DMA((n_slots * n,))], start, finish)


def _sibling_carry(grads):
    def make_copies(ins, outs, sems):
        x, y, c, _ = _mesh_position()
        return [pltpu.make_async_remote_copy(
            src_ref=ins[a].at[2 * ch + (1 - c)], dst_ref=outs[a].at[ch], send_sem=sems[0].at[4 * a + ch],
            recv_sem=sems[1].at[4 * a + ch], device_id=(x, y, 1 - c), device_id_type=MESH_IDS)
            for a in range(len(grads)) for ch in range(4)]

    return _pairwise_carry(grads, 4, make_copies)


def _chips_carry(sums):
    def make_copies(ins, outs, sems):
        x, y, c, chips = _mesh_position()
        return [pltpu.make_async_remote_copy(
            src_ref=ins[a].at[2 * px + py], dst_ref=outs[a].at[j], send_sem=sems[0].at[3 * a + j],
            recv_sem=sems[1].at[3 * a + j], device_id=(px, py, c), device_id_type=MESH_IDS)
            for a in range(len(sums)) for j, (px, py) in enumerate(chips)]

    return _pairwise_carry(sums, 3, make_copies)


def _everyone_carry(arrays):
    def make_copies(ins, outs, sems):
        x, y, c, _ = _mesh_position()
        flip = lambda v, bit: 1 - v if bit else v
        return [pltpu.make_async_remote_copy(
            src_ref=ins[a], dst_ref=outs[a].at[r - 1], send_sem=sems[0].at[7 * a + r - 1], recv_sem=sems[1].at[7 * a + r - 1],
            device_id=(flip(x, r & 4), flip(y, r & 2), flip(c, r & 1)), device_id_type=MESH_IDS)
            for a in range(len(arrays)) for r in range(1, N_DEV)]

    carry = _pairwise_carry([jax.ShapeDtypeStruct((1,) + a.shape, a.dtype) for a in arrays], N_DEV - 1, make_copies)
    carry.inputs = list(arrays)
    return carry


def _sum_everyone(own, received, me, name, after=()):
    def body(me_ref, own_ref, r_ref, *refs):
        g = None
        for d in range(N_DEV):
            relation = jnp.bitwise_xor(d, me_ref[0])
            part = jnp.where(relation == 0, own_ref[...], r_ref[jnp.maximum(relation - 1, 0)])
            g = part if g is None else g + part
        refs[-1][...] = g

    whole = lambda shape: pl.BlockSpec(shape, lambda i, me_ref: (0,) * len(shape))
    return pl.pallas_call(
        body, name=name,
        grid_spec=pltpu.PrefetchScalarGridSpec(
            num_scalar_prefetch=1, grid=(1,), in_specs=[whole(own.shape), whole(received.shape)] + [HBM_SPEC] * len(after),
            out_specs=whole(own.shape)),
        out_shape=jax.ShapeDtypeStruct(own.shape, F32))(me, *_in_hbm(own, received), *after)


SEM_SPEC = pl.BlockSpec(memory_space=pltpu.SEMAPHORE)
DATAFLOW_EFFECT = pltpu.SideEffectType.DATAFLOW_SIDE_EFFECTING


def _exchange_start(carry, name, after=()):
    n, n_sems = len(carry.inputs), len(carry.sems)
    lands = [lax.empty(s.shape, s.dtype) for s in carry.out_shapes]

    def body(*refs):
        first_out = 2 * n + len(after)
        srcs, zones, sems, token = refs[:n], refs[n:2 * n], refs[first_out:first_out + n_sems], refs[-1]
        carry.start(srcs, zones, sems)
        token[...] = jnp.zeros_like(token)

    outs = pl.pallas_call(
        body, name=name, in_specs=[HBM_SPEC] * (2 * n + len(after)),
        out_specs=[SEM_SPEC] * n_sems + [HBM_SPEC] * (2 * n) + [pl.BlockSpec(memory_space=pltpu.VMEM)],
        out_shape=list(carry.sems) + _hbm_out([jax.ShapeDtypeStruct(a.shape, a.dtype) for a in carry.inputs])
        + _hbm_out(carry.out_shapes) + [jax.ShapeDtypeStruct((SUBLANES, LANES), F32)],
        input_output_aliases={j: n_sems + j for j in range(2 * n)},
        compiler_params=pltpu.CompilerParams(has_side_effects=DATAFLOW_EFFECT),
    )(*_in_hbm(*carry.inputs, *lands), *after)
    return outs[:-1], outs[-1]


def _exchange_wait(carry, in_flight, after, name):
    n, n_sems = len(carry.inputs), len(carry.sems)
    sems, srcs, zones = in_flight[:n_sems], in_flight[n_sems:n_sems + n], in_flight[n_sems + n:]

    def body(*refs):
        src_refs, zone_refs, sem_refs = refs[:n], refs[n:2 * n], refs[2 * n:2 * n + n_sems]
        carry.finish(src_refs, zone_refs, sem_refs)

    outs = pl.pallas_call(
        body, name=name, in_specs=[HBM_SPEC] * (2 * n) + [SEM_SPEC] * n_sems + [HBM_SPEC] * len(after),
        out_specs=[HBM_SPEC] * (2 * n),
        out_shape=_hbm_out([jax.ShapeDtypeStruct(a.shape, a.dtype) for a in carry.inputs]) + _hbm_out(carry.out_shapes),
        input_output_aliases={j: j for j in range(2 * n)},
        compiler_params=pltpu.CompilerParams(has_side_effects=DATAFLOW_EFFECT),
    )(*srcs, *zones, *sems, *after)
    return list(outs[:n]), list(outs[n:])


def _add_sibling(grads8, recvs, place, row_tiles, name):
    k = len(grads8)
    g4 = [g.reshape(4, 2, *g.shape[1:]) for g in grads8]

    def body(place_ref, *refs):
        g_refs, r_refs, o_refs, ob_refs = (refs[j * k:(j + 1) * k] for j in range(4))
        own = pl.program_id(1) == place_ref[1]
        for g_ref, r_ref, o_ref, ob_ref in zip(g_refs, r_refs, o_refs, ob_refs):
            s = g_ref[0] + r_ref[...]
            ob_ref[...] = s.astype(BF16)

            @pl.when(own)
            def _(o_ref=o_ref, s=s):
                o_ref[...] = s[0]

    def blocks(make):
        return [make(g.shape[1] // row_tiles, g.shape[2]) for g in grads8]

    slot = lambda tr, C: pl.BlockSpec((1, tr, C), lambda r, ch, place_ref: (ch, r, 0))
    outs = pl.pallas_call(
        body, name=name,
        grid_spec=pltpu.PrefetchScalarGridSpec(
            num_scalar_prefetch=1, grid=(row_tiles, 4),
            in_specs=blocks(lambda tr, C: pl.BlockSpec((1, 1, tr, C), lambda r, ch, place_ref: (ch, place_ref[0], r, 0)))
            + blocks(slot),
            out_specs=blocks(lambda tr, C: pl.BlockSpec((tr, C), lambda r, ch, place_ref: (r, 0))) + blocks(slot)),
        out_shape=_hbm_out([jax.ShapeDtypeStruct(g.shape[1:], F32) for g in grads8]
                           + [jax.ShapeDtypeStruct((4,) + g.shape[1:], BF16) for g in grads8]),
        compiler_params=_params(("arbitrary", "arbitrary")),
    )(place, *_in_hbm(*g4, *recvs))
    return list(outs[:k]), list(outs[k:])


def _adam_math(w, g, m, v):
    m = ADAM_B1 * m + (1.0 - ADAM_B1) * g
    v = ADAM_B2 * v + (1.0 - ADAM_B2) * jnp.square(g)
    m_hat = m / (1.0 - ADAM_B1 ** ADAM_STEP)
    v_hat = v / (1.0 - ADAM_B2 ** ADAM_STEP)
    delta = -ADAM_LR * (m_hat / (jnp.sqrt(v_hat) + ADAM_EPS) + ADAM_WD * w)
    return delta, m, v


def _adam_big(ws, ms, vs, chip_sums, recvs, row_tiles, name, after=()):
    k = len(ws)

    def body(*refs):
        refs = refs[:5 * k] + refs[5 * k + len(after):]
        w_refs, m_refs, v_refs, s_refs, r_refs, g_refs, d_refs, nm_refs, nv_refs = (refs[j * k:(j + 1) * k] for j in range(9))
        for a in range(k):
            r_ref = r_refs[a]
            g = s_refs[a][...] + r_ref[0].astype(F32) + r_ref[1].astype(F32) + r_ref[2].astype(F32)
            g_refs[a][...] = g
            d_refs[a][...], nm_refs[a][...], nv_refs[a][...] = _adam_math(w_refs[a][...], g, m_refs[a][...], v_refs[a][...])

    def blocks(make):
        return [make(w.shape[0] // row_tiles, w.shape[1]) for w in ws]

    blk = lambda tr, C: pl.BlockSpec((tr, C), lambda r: (r, 0))
    outs = pl.pallas_call(
        body, name=name, grid=(row_tiles,),
        in_specs=blocks(blk) * 4 + blocks(lambda tr, C: pl.BlockSpec((3, tr, C), lambda r: (0, r, 0))) + [HBM_SPEC] * len(after),
        out_specs=blocks(blk) * 4,
        out_shape=[jax.ShapeDtypeStruct(w.shape, F32) for w in ws] * 4,
        compiler_params=_params(("arbitrary",)),
    )(*_in_hbm(*ws, *ms, *vs, *chip_sums, *recvs), *after)
    return [list(outs[j * k:(j + 1) * k]) for j in range(4)]


def _sum_partials(partials, name, after=()):
    def body(p_ref, *refs):
        g = p_ref[0]
        for d in range(1, partials.shape[0]):
            g = g + p_ref[d]
        refs[-1][...] = g

    return pl.pallas_call(body, name=name, grid=(1,), in_specs=[_whole(partials.shape)] + [HBM_SPEC] * len(after),
                          out_specs=_whole(partials.shape[1:]),
                          out_shape=jax.ShapeDtypeStruct(partials.shape[1:], F32))(*_in_hbm(partials), *after)


def _adam_small(ws, ms, vs, gs):
    n = len(ws)

    def body(*refs):
        w_refs, m_refs, v_refs, g_refs = (refs[i * n:(i + 1) * n] for i in range(4))
        d_refs, nm_refs, nv_refs = (refs[(4 + i) * n:(5 + i) * n] for i in range(3))
        for j in range(n):
            d_refs[j][...], nm_refs[j][...], nv_refs[j][...] = _adam_math(
                w_refs[j][...], g_refs[j][...], m_refs[j][...], v_refs[j][...])

    specs = [_whole(w.shape) for w in ws]
    params = pltpu.CompilerParams(dimension_semantics=("arbitrary",), vmem_limit_bytes=VMEM_MID,
                                  allow_input_fusion=[False] * (3 * n) + [True] * n)
    outs = pl.pallas_call(body, name="adam_small", grid=(1,), in_specs=specs * 4, out_specs=specs * 3,
                          out_shape=[jax.ShapeDtypeStruct(w.shape, F32) for w in ws] * 3,
                          compiler_params=params)(*_in_hbm(*ws, *ms, *vs), *gs)
    return outs[:n], outs[n:2 * n], outs[2 * n:]


PACK_QUANTUM = SUBLANES * LANES


def _pack(named, names):
    parts = []
    for nme in names:
        flat = named[nme].reshape(-1)
        parts.append(jnp.pad(flat, (0, -flat.size % PACK_QUANTUM)))
    return jnp.concatenate(parts).reshape(-1, LANES)


def _unpack(packed, shapes, names):
    flat = packed.reshape(-1)
    out, pos = {}, 0
    for nme in names:
        size = math.prod(shapes[nme])
        out[nme] = flat[pos:pos + size].reshape(shapes[nme])
        pos += size + (-size % PACK_QUANTUM)
    return out


BIG = ("w_in", "w_glu", "w_attn_branch", "w_ssm_branch", "w_out", "w_ff_in", "w_ff_out")
COLUMN_SHARDED = ("w_in", "w_attn_branch", "w_ssm_branch", "w_ff_in")
SMALL = ("norm_mix_pre", "norm_mix_post", "norm_mlp_pre", "norm_mlp_post", "rel_bias", "sinks", "lam_re", "lam_im",
         "log_dt", "b_re", "b_im", "c_re", "c_im", "d_skip")
SWAPPED_SMALL = ("rel_bias", "b_re", "b_im")
SMALL_LATE = ("norm_mix_pre", "rel_bias", "sinks", "loss")
SMALL_BEFORE_ATTN_BWD = tuple(n for n in SMALL if n not in SMALL_LATE)
ALL_WEIGHTS = ("norm_mix_pre", "norm_mix_post", "norm_mlp_pre", "norm_mlp_post", "w_in", "rel_bias", "sinks", "lam_re",
               "lam_im", "log_dt", "b_re", "b_im", "c_re", "c_im", "d_skip", "w_glu", "w_attn_branch", "w_ssm_branch",
               "w_out", "w_ff_in", "w_ff_out")


def _full_from_gathered(name, gathered):
    _, r, c = gathered.shape
    if name in COLUMN_SHARDED:
        return jnp.transpose(gathered, (1, 0, 2)).reshape(r, N_DEV * c)
    return gathered.reshape(N_DEV * r, c)


def _blocks_from_full(name, full):
    r, c = full.shape
    if name in COLUMN_SHARDED:
        return jnp.transpose(full.reshape(r, N_DEV, c // N_DEV), (1, 0, 2))
    return full.reshape(N_DEV, r // N_DEV, c)


def kernel(x, norm_mix_pre, norm_mix_post, norm_mlp_pre, norm_mlp_post, w_in, rel_bias, sinks, lam_re, lam_im, log_dt, b_re, b_im, c_re, c_im, d_skip, w_glu, w_attn_branch, w_ssm_branch, w_out, w_ff_in, w_ff_out, loss_target, m_norm_mix_pre, m_norm_mix_post, m_norm_mlp_pre, m_norm_mlp_post, m_w_in, m_rel_bias, m_sinks, m_lam_re, m_lam_im, m_log_dt, m_b_re, m_b_im, m_c_re, m_c_im, m_d_skip, m_w_glu, m_w_attn_branch, m_w_ssm_branch, m_w_out, m_w_ff_in, m_w_ff_out, v_norm_mix_pre, v_norm_mix_post, v_norm_mlp_pre, v_norm_mlp_post, v_w_in, v_rel_bias, v_sinks, v_lam_re, v_lam_im, v_log_dt, v_b_re, v_b_im, v_c_re, v_c_im, v_d_skip, v_w_glu, v_w_attn_branch, v_w_ssm_branch, v_w_out, v_w_ff_in, v_w_ff_out):
    args = dict(locals())
    w = {n: args[n] for n in ALL_WEIGHTS}
    m = {n: args["m_" + n] for n in ALL_WEIGHTS}
    v = {n: args["v_" + n] for n in ALL_WEIGHTS}
    core = lax.axis_index("c").astype(jnp.int32).reshape(1)
    chip = (2 * lax.axis_index("x") + lax.axis_index("y")).astype(jnp.int32).reshape(1)
    xs, target = x[0], loss_target[0]
    t = _tiles(xs.shape[0])
    local = lambda d, n: d[n][0].T if n == "w_in" else d[n][0]
    gather_in = _gather_carry([local(w, "w_in").astype(BF16)], pass_on=False)
    gather_in_flight, token = _exchange_start(gather_in, "gather_w_in_start")
    one = token[0:1, 0:1] + 1.0
    shard = {n: (local(w, n) * one).astype(BF16) for n in BIG if n != "w_in"}
    shard["w_ff_in"] = shard["w_ff_in"].T
    view = lambda n, a: jnp.swapaxes(a, -1, -2) if n in SWAPPED_SMALL else a
    small = {n: (view(n, w[n]) if n == "rel_bias" else view(n, w[n])[0]) for n in SMALL}
    g1, g2, g3, g4 = (small[n].reshape(1, D_MODEL) for n in ("norm_mix_pre", "norm_mix_post", "norm_mlp_pre", "norm_mlp_post"))
    bucket = jnp.asarray(_bucket_table())
    rel_b, sink = small["rel_bias"], small["sinks"].reshape(1, N_HEADS)
    lam_r, lam_i = small["lam_re"].reshape(1, STATES), small["lam_im"].reshape(1, STATES)
    ldt_rep = jnp.repeat(small["log_dt"].reshape(N_GROUPS), N_STATE).reshape(1, STATES)
    bd_re, bd_im = _block_diag_in(small["b_re"] * one), _block_diag_in(small["b_im"] * one)
    cm_re, cm_im = _block_diag_out(small["c_re"] * one).astype(BF16), _block_diag_out(small["c_im"] * one).astype(BF16)
    dsk = small["d_skip"].reshape(1, SSM_W)
    a_re, a_im, bm_re, bm_im = _ssm_prep(lam_r, lam_i, ldt_rep, bd_re, bd_im)

    travelled_behind = list(shard.values()) + [a_re, a_im, bm_re, bm_im, cm_re, cm_im]
    _, landed = _exchange_wait(gather_in, gather_in_flight, travelled_behind, "gather_w_in_wait")
    (g_in,) = _gather_pass_on(landed, "gather_w_in_pass_on")
    wf_in = g_in.reshape(IN_W, D_MODEL)
    merge_names = ("w_glu", "w_attn_branch", "w_ssm_branch", "w_out")
    (q, k, vv, u, ga, gs, h), gathered = _in_proj_fwd(xs, g1, wf_in, t["proj"], _gather_carry([shard[n] for n in merge_names]))
    wf = {n: _full_from_gathered(n, g) for n, g in zip(merge_names, gathered)}
    (att,), (wf_ff_in,) = _attn_fwd(q, k, vv, bucket, rel_b, sink, _gather_carry([shard["w_ff_in"]]))
    (y, h_re, h_im, in_re, in_im), (wf_ff_out,) = _ssm_fwd(
        u, a_re, a_im, bm_re, bm_im, cm_re, cm_im, dsk, t["ssm_chunk"], _gather_carry([shard["w_ff_out"]]))
    x1, o, h2 = _merge_fwd(xs, y, att, ga, gs, g2, g3, wf["w_glu"], wf["w_ssm_branch"], wf["w_attn_branch"], wf["w_out"],
                           t["merge"])
    a, dfo, dx2, loss_blk, dg4 = _mlp_fwd(h2, x1, target, g4, wf_ff_in, wf_ff_out, t["mlp_fwd"])

    groups = {"ff": 4, "merge": 1, "w_in": 2}

    def add_sibling(group, blocks, received):
        return _add_sibling(blocks, received, jnp.concatenate([core, chip]), groups[group], "add_sibling_" + group)

    ff_names = ("w_ff_in", "w_ff_out")
    dw_ff_in, dw_ff_out, da = _mlp_weight_grads(dfo, a, h2, wf_ff_out, t["mlp_bwd"])
    dh2 = _mlp_input_grad(da, wf_ff_in.reshape(D_FF, D_MODEL), t["mlp_bwd"])
    ff_blocks = [dw_ff_in, dw_ff_out]
    (dx1, dgates, datt, dy, dw_glu, dw_ssm, dw_attn, dw_out, dg2, dg3), ff_recv = _merge_bwd(
        dh2, dx2, x1, o, y, att, ga, gs, g2, g3, wf["w_glu"], wf["w_ssm_branch"], wf["w_attn_branch"], wf["w_out"],
        t["merge_bwd"], _sibling_carry(ff_blocks))
    ff_sums, ff_sums_bf = add_sibling("ff", ff_blocks, ff_recv)
    merge_blocks = [_blocks_from_full(n, g) for n, g in zip(merge_names, (dw_glu, dw_attn, dw_ssm, dw_out))]
    (du, dbm_re, dbm_im, dcm_re, dcm_im, da_re, da_im, dd_skip), carried = _ssm_bwd(
        dy, u, h_re, h_im, in_re, in_im, a_re, a_im, bm_re, bm_im, cm_re, cm_im, dsk, t["ssm_chunk"],
        _join(_chips_carry(ff_sums_bf), _sibling_carry(merge_blocks)))
    ff_from_chips, merge_recv = carried[:2], carried[2:]
    merge_sums, merge_sums_bf = add_sibling("merge", merge_blocks, merge_recv)
    dbd_re, dbd_im, dlam_re, dlam_im, dldt_rep = _ssm_prep_bwd(lam_r, lam_i, ldt_rep, bd_re, bd_im, dbm_re, dbm_im, da_re, da_im)
    dlog_dt = _group_sum(dldt_rep.reshape(N_GROUPS, N_STATE))
    shapes = {n: view(n, w[n]).shape for n in SMALL}
    shapes["loss"] = (1,)
    small_grads = dict(
        norm_mix_post=dg2, norm_mlp_pre=dg3, norm_mlp_post=dg4, lam_re=dlam_re, lam_im=dlam_im, log_dt=dlog_dt,
        b_re=_block_diag_in_grad(dbd_re), b_im=_block_diag_in_grad(dbd_im),
        c_re=_block_diag_out_grad(dcm_re), c_im=_block_diag_out_grad(dcm_im), d_skip=dd_skip)
    packed_early = _pack({n: small_grads[n].reshape(shapes[n]) for n in SMALL_BEFORE_ATTN_BWD}, SMALL_BEFORE_ATTN_BWD)
    (dq, dkv, attn_small), carried = _attn_bwd(
        q, k, vv, datt, bucket, rel_b, sink, _join(_chips_carry(merge_sums_bf), _gather_carry([packed_early])))
    merge_from_chips, partials_early = carried[:-1], carried[-1]

    dparts = (dq, dkv, du, dgates)
    dw_in_t = _in_proj_weight_grad(h, dparts)
    in_blocks = [dw_in_t.reshape(N_DEV, IN_W // N_DEV, D_MODEL)]
    grads, deltas, new_m, new_v = {}, {}, {}, {}

    def adam_group(group, names, sums, received, after=()):
        outs = _adam_big(*[[local(d, n) for n in names] for d in (w, m, v)], sums, received, groups[group],
                         "adam_" + group, after)
        for store, vals in zip((grads, deltas, new_m, new_v), outs):
            store.update({n: (o.T if n == "w_in" else o)[None] for n, o in zip(names, vals)})

    to_sibling = _sibling_carry(in_blocks)
    in_flight, token = _exchange_start(to_sibling, "w_in_sibling_start")
    adam_group("ff", ff_names, ff_sums, ff_from_chips, [token])
    in_blocks, in_recv = _exchange_wait(to_sibling, in_flight, [new_v["w_ff_out"]], "w_in_sibling_wait")
    in_sums, in_sums_bf = add_sibling("w_in", in_blocks, in_recv)
    to_chips = _chips_carry(in_sums_bf)
    in_flight, chips_started = _exchange_start(to_chips, "w_in_chips_start")
    n_tiles = xs.shape[0] // t["proj_bwd"]
    (grad_x, dg1), _ = _in_proj_input_grad(
        xs, g1 + chips_started[0:1, 0:1], wf_in, dx1, dparts, t["proj_bwd"], 0, n_tiles, "in_proj_input_grad")
    late = dict(norm_mix_pre=dg1, rel_bias=attn_small[:, :N_BUCKETS, 0], sinks=attn_small[:, N_BUCKETS, 0], loss=loss_blk[0:1, 0])
    packed_late = _pack({n: late[n].reshape(shapes[n]) for n in SMALL_LATE}, SMALL_LATE)
    to_everyone = _everyone_carry([packed_late])
    late_in_flight, late_started = _exchange_start(to_everyone, "late_grads_start")
    adam_group("merge", merge_names, merge_sums, merge_from_chips, [late_started])

    grads.update(_unpack(_sum_partials(partials_early, "sum_small_grads", [late_started]), shapes, SMALL_BEFORE_ATTN_BWD))
    (packed_late,), (late_received,) = _exchange_wait(to_everyone, late_in_flight, [new_v["w_out"]], "late_grads_wait")
    grads.update(_unpack(_sum_everyone(packed_late, late_received, 2 * chip + core, "sum_late_grads"), shapes, SMALL_LATE))
    loss = grads.pop("loss").reshape(())
    small_out = _adam_small(*[[view(n, d[n]) for n in SMALL] for d in (w, m, v)], [grads[n] for n in SMALL])
    for store, vals in zip((deltas, new_m, new_v), small_out):
        store.update(zip(SMALL, vals))
    for store in (grads, deltas, new_m, new_v):
        store.update({n: view(n, store[n]) for n in SWAPPED_SMALL})

    busy = [grad_x, new_v["w_out"], deltas["norm_mix_pre"]]
    _, (in_from_chips,) = _exchange_wait(to_chips, in_flight, busy, "w_in_chips_wait")
    adam_group("w_in", ("w_in",), in_sums, [in_from_chips])

    return (loss, grad_x[None], *[grads[n] for n in ALL_WEIGHTS], *[deltas[n] for n in ALL_WEIGHTS],
            *[new_m[n] for n in ALL_WEIGHTS], *[new_v[n] for n in ALL_WEIGHTS])
```
